```python
import math
import jax, jax.numpy as jnp
from jax import lax
import numpy as np


D_MODEL = 1024
BATCH = 8
SEQ = 16384
DEPTH = 2

N_META = 16
SB_HEAD_DIM = 64
SB_WIDTH = D_MODEL // 2
SB_HEADS = SB_WIDTH // SB_HEAD_DIM
SB_BLOCK = 128
S5_WIDTH = D_MODEL - SB_WIDTH
S5_GROUP = 16
S5_GROUPS = S5_WIDTH // S5_GROUP
S5_STATE = 64
DN_HEAD_DIM = 128
DN_WIDTH = D_MODEL
DN_HEADS = DN_WIDTH // DN_HEAD_DIM
DN_CONV = 4
DN_CHUNK = 64
D_FF = 4 * D_MODEL
N_EVEN = (DEPTH + 1) // 2
N_ODD = DEPTH // 2
EPS = 1e-6

kernel_name = 'hybrid_stickbreak_s5_gated_deltanet'


def rms_norm(x, g):
    xf = x.astype(jnp.float32)
    y = xf * lax.rsqrt(jnp.mean(xf * xf, axis=-1, keepdims=True) + EPS)
    return (y * g.astype(jnp.float32)).astype(x.dtype)


def l2_norm(x):
    return x * lax.rsqrt(jnp.sum(x * x, axis=-1, keepdims=True) + EPS)


def stick_breaking_attention(q, k, v):
    b, l, h, d = q.shape
    pad = (-N_META) % SB_BLOCK
    lp = l + pad
    nb = lp // SB_BLOCK
    to_heads = lambda t: jnp.pad(t.transpose(0, 2, 1, 3), ((0, 0), (0, 0), (pad, 0), (0, 0)))
    qh, kh, vh = to_heads(q), to_heads(k), to_heads(v)
    q_blocks = qh.reshape(b, h, nb, SB_BLOCK, d).transpose(2, 0, 1, 3, 4)
    key_pos = jnp.arange(lp)
    scale = d ** -0.5

    def block(args):
        qb, bi = args
        q_pos = bi * SB_BLOCK + jnp.arange(SB_BLOCK)
        valid = (key_pos[None, :] < q_pos[:, None]) & (key_pos[None, :] >= pad)
        z = jnp.einsum('bhqd,bhkd->bhqk', qb, kh) * scale
        log_keep = jnp.where(valid, jax.nn.log_sigmoid(-z), 0.0)
        log_surv = lax.cumsum(log_keep, axis=3, reverse=True) - log_keep
        w = jnp.exp(jnp.where(valid, jax.nn.log_sigmoid(z) + log_surv, -jnp.inf))
        return jnp.einsum('bhqk,bhkd->bhqd', w, vh)

    out = lax.map(block, (q_blocks, jnp.arange(nb)))
    out = out.transpose(1, 0, 3, 2, 4).reshape(b, lp, h, d)
    return out[:, pad:]


def _complex_affine_combine(e1, e2):
    a1r, a1i, b1r, b1i = e1
    a2r, a2i, b2r, b2i = e2
    ar = a1r * a2r - a1i * a2i
    ai = a1r * a2i + a1i * a2r
    br = a2r * b1r - a2i * b1i + b2r
    bi = a2r * b1i + a2i * b1r + b2i
    return (ar, ai, br, bi)


def s5_glu(u, lam_re, lam_im, log_dt, b_re, b_im, c_re, c_im, d_skip, w_glu, b_glu):
    f32 = jnp.float32
    b, l, _ = u.shape
    lr = jnp.minimum(lam_re.astype(f32), -1e-4)
    li = lam_im.astype(f32)
    dt = jnp.exp(log_dt.astype(f32))[:, None]
    mag = jnp.exp(lr * dt)
    ang = li * dt
    abar_re, abar_im = mag * jnp.cos(ang), mag * jnp.sin(ang)
    den = lr * lr + li * li
    nr, ni = abar_re - 1.0, abar_im
    coef_re = (nr * lr + ni * li) / den
    coef_im = (ni * lr - nr * li) / den
    br, bim = b_re.astype(f32), b_im.astype(f32)
    bbar_re = coef_re[..., None] * br - coef_im[..., None] * bim
    bbar_im = coef_re[..., None] * bim + coef_im[..., None] * br
    ug = u.reshape(b, l, S5_GROUPS, S5_GROUP)
    bu_re = jnp.einsum('blgp,gnp->blgn', ug, bbar_re)
    bu_im = jnp.einsum('blgp,gnp->blgn', ug, bbar_im)
    a_re = jnp.broadcast_to(abar_re, bu_re.shape)
    a_im = jnp.broadcast_to(abar_im, bu_im.shape)
    _, _, x_re, x_im = lax.associative_scan(_complex_affine_combine, (a_re, a_im, bu_re, bu_im), axis=1)
    y = (jnp.einsum('blgn,gpn->blgp', x_re, c_re.astype(f32))
         - jnp.einsum('blgn,gpn->blgp', x_im, c_im.astype(f32)))
    y = y.reshape(b, l, S5_WIDTH) + d_skip.astype(f32) * u
    hact = jax.nn.gelu(y)
    return hact * jax.nn.sigmoid(hact @ w_glu.astype(f32) + b_glu.astype(f32))


def sb_s5_mixer(h, w_in, w_out, sb_norm_g, lam_re, lam_im, log_dt, b_re, b_im, c_re, c_im,
                d_skip, w_glu, b_glu, s5_norm_g):
    b, l, _ = h.shape
    proj = h @ w_in
    qkv = proj[..., :3 * SB_WIDTH].astype(jnp.float32).reshape(b, l, 3, SB_HEADS, SB_HEAD_DIM)
    o_sb = stick_breaking_attention(qkv[:, :, 0], qkv[:, :, 1], qkv[:, :, 2]).reshape(b, l, SB_WIDTH)
    o_s5 = s5_glu(proj[..., 3 * SB_WIDTH:].astype(jnp.float32), lam_re, lam_im, log_dt,
                  b_re, b_im, c_re, c_im, d_skip, w_glu, b_glu)
    merged = jnp.concatenate([rms_norm(o_sb, sb_norm_g), rms_norm(o_s5, s5_norm_g)], axis=-1)
    return merged.astype(h.dtype) @ w_out


def causal_conv(x, w):
    kw = w.shape[0]
    return lax.conv_general_dilated(x, w[:, None, :].astype(x.dtype), window_strides=(1,),
                                    padding=[(kw - 1, 0)], dimension_numbers=('NWC', 'WIO', 'NWC'),
                                    feature_group_count=x.shape[-1])


def gated_delta_rule(q, k, v, g, beta):
    f32 = jnp.float32
    b, l, h, dk = q.shape
    dv = v.shape[-1]
    cs = DN_CHUNK
    pad = (-N_META) % cs
    lp = l + pad
    nc = lp // cs

    def chunks(t):
        t = jnp.moveaxis(t, 2, 1)
        t = jnp.pad(t, [(0, 0), (0, 0), (pad, 0)] + [(0, 0)] * (t.ndim - 3))
        return t.reshape(t.shape[:2] + (nc, cs) + t.shape[3:])

    q, k, v, g, beta = chunks(q * dk ** -0.5), chunks(k), chunks(v), chunks(g), chunks(beta)
    gcum = jnp.cumsum(g, axis=-1)
    idx = jnp.arange(cs)
    incl = idx[:, None] >= idx[None, :]
    strict = idx[:, None] > idx[None, :]
    decay = jnp.exp(jnp.where(incl, gcum[..., :, None] - gcum[..., None, :], -jnp.inf))
    kk = jnp.einsum('bhncd,bhnsd->bhncs', k, k)
    lower = jnp.where(strict, beta[..., :, None] * kk * decay, 0.0)
    rhs = jnp.concatenate([v * beta[..., None], k * (beta * jnp.exp(gcum))[..., None]], axis=-1)
    sol = lax.linalg.triangular_solve(jnp.eye(cs, dtype=f32) + lower, rhs, left_side=True, lower=True)
    value, k_cumdecay = sol[..., :dv], sol[..., dv:]
    attn_intra = jnp.einsum('bhncd,bhnsd->bhncs', q, k) * decay
    q_g = q * jnp.exp(gcum)[..., None]
    g_last = gcum[..., -1]
    k_tail = k * jnp.exp(g_last[..., None] - gcum)[..., None]

    def step(state, inp):
        qg_c, kcd_c, val_c, intra_c, kt_c, gl_c = inp
        v_new = val_c - jnp.einsum('bhcd,bhde->bhce', kcd_c, state)
        o_c = jnp.einsum('bhcd,bhde->bhce', qg_c, state) + jnp.einsum('bhcs,bhse->bhce', intra_c, v_new)
        state = state * jnp.exp(gl_c)[..., None, None] + jnp.einsum('bhcd,bhce->bhde', kt_c, v_new)
        return state, o_c

    xs = tuple(jnp.moveaxis(t, 2, 0) for t in (q_g, k_cumdecay, value, attn_intra, k_tail, g_last))
    s0 = jnp.zeros((b, h, dk, dv), f32)
    _, out = lax.scan(step, s0, xs)
    out = out.transpose(1, 0, 3, 2, 4).reshape(b, lp, h, dv)
    return out[:, pad:]


def gated_deltanet_mixer(h, w_in, conv_w, a_log, dt_bias, norm_g, w_out):
    f32 = jnp.float32
    b, l, _ = h.shape
    wd, nh, hd = DN_WIDTH, DN_HEADS, DN_HEAD_DIM
    proj = h @ w_in
    qkv = jax.nn.silu(causal_conv(proj[..., :3 * wd], conv_w)).astype(f32)
    z = proj[..., 3 * wd:4 * wd].astype(f32).reshape(b, l, nh, hd)
    a = proj[..., 4 * wd:4 * wd + nh].astype(f32)
    bb = proj[..., 4 * wd + nh:].astype(f32)
    q, k, v = jnp.split(qkv, 3, axis=-1)
    q = l2_norm(q.reshape(b, l, nh, hd))
    k = l2_norm(k.reshape(b, l, nh, hd))
    v = v.reshape(b, l, nh, hd)
    beta = jax.nn.sigmoid(bb)
    g = -jnp.exp(a_log.astype(f32)) * jax.nn.softplus(a + dt_bias.astype(f32))
    o = gated_delta_rule(q, k, v, g, beta)
    o = rms_norm(o, norm_g) * jax.nn.silu(z)
    return o.reshape(b, l, wd).astype(h.dtype) @ w_out


def sq_relu_mlp(h, w1, w2):
    return jnp.square(jax.nn.relu(h @ w1)) @ w2


def _fwd_setup_inputs(seed: int = 0) -> dict:
    key = jax.random.key(seed)
    ks = iter(jax.random.split(key, 40))
    f32 = jnp.float32
    nrm = lambda shape, s: jax.random.normal(next(ks), shape, f32) * s
    gain = lambda shape: 1.0 + nrm(shape, 0.02)
    log_uniform = lambda shape, lo, hi: jax.random.uniform(next(ks), shape, f32, math.log(lo), math.log(hi))
    dn_dt = jnp.exp(log_uniform((N_ODD, DN_HEADS), 1e-3, 1e-1))
    return {
        'x': nrm((BATCH, SEQ, D_MODEL), 1.0),
        'meta_tokens': nrm((N_META, D_MODEL), 1.0),
        'pre_mix_norm': gain((DEPTH, D_MODEL)),
        'post_mix_norm': gain((DEPTH, D_MODEL)),
        'pre_mlp_norm': gain((DEPTH, D_MODEL)),
        'post_mlp_norm': gain((DEPTH, D_MODEL)),
        'mlp_w1': nrm((DEPTH, D_MODEL, D_FF), D_MODEL ** -0.5),
        'mlp_w2': nrm((DEPTH, D_FF, D_MODEL), D_FF ** -0.5),
        'w_in_even': nrm((N_EVEN, D_MODEL, 3 * SB_WIDTH + S5_WIDTH), D_MODEL ** -0.5),
        'w_out_even': nrm((N_EVEN, SB_WIDTH + S5_WIDTH, D_MODEL), (SB_WIDTH + S5_WIDTH) ** -0.5),
        'sb_out_norm': gain((N_EVEN, SB_WIDTH)),
        's5_lambda_re': -0.5 + nrm((N_EVEN, S5_GROUPS, S5_STATE), 0.01),
        's5_lambda_im': jnp.pi * jnp.arange(S5_STATE, dtype=f32)[None, None, :] + nrm((N_EVEN, S5_GROUPS, S5_STATE), 0.01),
        's5_log_dt': log_uniform((N_EVEN, S5_GROUPS), 1e-3, 1e-1),
        's5_b_re': nrm((N_EVEN, S5_GROUPS, S5_STATE, S5_GROUP), (2 * S5_GROUP) ** -0.5),
        's5_b_im': nrm((N_EVEN, S5_GROUPS, S5_STATE, S5_GROUP), (2 * S5_GROUP) ** -0.5),
        's5_c_re': nrm((N_EVEN, S5_GROUPS, S5_GROUP, S5_STATE), S5_STATE ** -0.5),
        's5_c_im': nrm((N_EVEN, S5_GROUPS, S5_GROUP, S5_STATE), S5_STATE ** -0.5),
        's5_d': nrm((N_EVEN, S5_WIDTH), 0.5),
        's5_w_glu': nrm((N_EVEN, S5_WIDTH, S5_WIDTH), S5_WIDTH ** -0.5),
        's5_b_glu': nrm((N_EVEN, S5_WIDTH), 0.01),
        's5_out_norm': gain((N_EVEN, S5_WIDTH)),
        'w_in_odd': nrm((N_ODD, D_MODEL, 4 * DN_WIDTH + 2 * DN_HEADS), D_MODEL ** -0.5),
        'dn_conv_w': nrm((N_ODD, DN_CONV, 3 * DN_WIDTH), DN_CONV ** -0.5),
        'dn_a_log': jnp.log(jax.random.uniform(next(ks), (N_ODD, DN_HEADS), f32, 1.0, 16.0)),
        'dn_dt_bias': dn_dt + jnp.log(-jnp.expm1(-dn_dt)),
        'dn_out_norm': gain((N_ODD, DN_HEAD_DIM)),
        'w_out_odd': nrm((N_ODD, DN_WIDTH, D_MODEL), DN_WIDTH ** -0.5),
    }


def _fwd_reference(x, meta_tokens, pre_mix_norm, post_mix_norm, pre_mlp_norm, post_mlp_norm, mlp_w1, mlp_w2,
              w_in_even, w_out_even, sb_out_norm, s5_lambda_re, s5_lambda_im, s5_log_dt, s5_b_re, s5_b_im,
              s5_c_re, s5_c_im, s5_d, s5_w_glu, s5_b_glu, s5_out_norm,
              w_in_odd, dn_conv_w, dn_a_log, dn_dt_bias, dn_out_norm, w_out_odd):
    b = x.shape[0]
    meta = jnp.broadcast_to(meta_tokens.astype(x.dtype)[None], (b, N_META, D_MODEL))
    hs = jnp.concatenate([meta, x], axis=1)
    for i in range(DEPTH):
        j = i // 2
        hn = rms_norm(hs, pre_mix_norm[i])
        if i % 2 == 0:
            mix = sb_s5_mixer(hn, w_in_even[j], w_out_even[j], sb_out_norm[j], s5_lambda_re[j], s5_lambda_im[j],
                              s5_log_dt[j], s5_b_re[j], s5_b_im[j], s5_c_re[j], s5_c_im[j], s5_d[j],
                              s5_w_glu[j], s5_b_glu[j], s5_out_norm[j])
        else:
            mix = gated_deltanet_mixer(hn, w_in_odd[j], dn_conv_w[j], dn_a_log[j], dn_dt_bias[j],
                                       dn_out_norm[j], w_out_odd[j])
        hs = hs + rms_norm(mix, post_mix_norm[i])
        hn = rms_norm(hs, pre_mlp_norm[i])
        hs = hs + rms_norm(sq_relu_mlp(hn, mlp_w1[i], mlp_w2[i]), post_mlp_norm[i])
    return hs[:, N_META:]


import jax as _jax
import jax.numpy as _jnp

TWIN_FORMAT = 'train_step'
FWD_PARAMS = ['x', 'meta_tokens', 'pre_mix_norm', 'post_mix_norm', 'pre_mlp_norm', 'post_mlp_norm', 'mlp_w1', 'mlp_w2', 'w_in_even', 'w_out_even', 'sb_out_norm', 's5_lambda_re', 's5_lambda_im', 's5_log_dt', 's5_b_re', 's5_b_im', 's5_c_re', 's5_c_im', 's5_d', 's5_w_glu', 's5_b_glu', 's5_out_norm', 'w_in_odd', 'dn_conv_w', 'dn_a_log', 'dn_dt_bias', 'dn_out_norm', 'w_out_odd']
TWIN_WEIGHTS = ['meta_tokens', 'pre_mix_norm', 'post_mix_norm', 'pre_mlp_norm', 'post_mlp_norm', 'mlp_w1', 'mlp_w2', 'w_in_even', 'w_out_even', 'sb_out_norm', 's5_lambda_re', 's5_lambda_im', 's5_log_dt', 's5_b_re', 's5_b_im', 's5_c_re', 's5_c_im', 's5_d', 's5_w_glu', 's5_b_glu', 's5_out_norm', 'w_in_odd', 'dn_conv_w', 'dn_a_log', 'dn_dt_bias', 'dn_out_norm', 'w_out_odd']
TWIN_DIFF_INPUT = 'x'
TWIN_INPUTS = ['x', 'meta_tokens', 'pre_mix_norm', 'post_mix_norm', 'pre_mlp_norm', 'post_mlp_norm', 'mlp_w1', 'mlp_w2', 'w_in_even', 'w_out_even', 'sb_out_norm', 's5_lambda_re', 's5_lambda_im', 's5_log_dt', 's5_b_re', 's5_b_im', 's5_c_re', 's5_c_im', 's5_d', 's5_w_glu', 's5_b_glu', 's5_out_norm', 'w_in_odd', 'dn_conv_w', 'dn_a_log', 'dn_dt_bias', 'dn_out_norm', 'w_out_odd', 'loss_target', 'm_meta_tokens', 'm_pre_mix_norm', 'm_post_mix_norm', 'm_pre_mlp_norm', 'm_post_mlp_norm', 'm_mlp_w1', 'm_mlp_w2', 'm_w_in_even', 'm_w_out_even', 'm_sb_out_norm', 'm_s5_lambda_re', 'm_s5_lambda_im', 'm_s5_log_dt', 'm_s5_b_re', 'm_s5_b_im', 'm_s5_c_re', 'm_s5_c_im', 'm_s5_d', 'm_s5_w_glu', 'm_s5_b_glu', 'm_s5_out_norm', 'm_w_in_odd', 'm_dn_conv_w', 'm_dn_a_log', 'm_dn_dt_bias', 'm_dn_out_norm', 'm_w_out_odd', 'v_meta_tokens', 'v_pre_mix_norm', 'v_post_mix_norm', 'v_pre_mlp_norm', 'v_post_mlp_norm', 'v_mlp_w1', 'v_mlp_w2', 'v_w_in_even', 'v_w_out_even', 'v_sb_out_norm', 'v_s5_lambda_re', 'v_s5_lambda_im', 'v_s5_log_dt', 'v_s5_b_re', 'v_s5_b_im', 'v_s5_c_re', 'v_s5_c_im', 'v_s5_d', 'v_s5_w_glu', 'v_s5_b_glu', 'v_s5_out_norm', 'v_w_in_odd', 'v_dn_conv_w', 'v_dn_a_log', 'v_dn_dt_bias', 'v_dn_out_norm', 'v_w_out_odd']
TWIN_OUTPUTS = ['loss', 'grad_x', 'grad_meta_tokens', 'grad_pre_mix_norm', 'grad_post_mix_norm', 'grad_pre_mlp_norm', 'grad_post_mlp_norm', 'grad_mlp_w1', 'grad_mlp_w2', 'grad_w_in_even', 'grad_w_out_even', 'grad_sb_out_norm', 'grad_s5_lambda_re', 'grad_s5_lambda_im', 'grad_s5_log_dt', 'grad_s5_b_re', 'grad_s5_b_im', 'grad_s5_c_re', 'grad_s5_c_im', 'grad_s5_d', 'grad_s5_w_glu', 'grad_s5_b_glu', 'grad_s5_out_norm', 'grad_w_in_odd', 'grad_dn_conv_w', 'grad_dn_a_log', 'grad_dn_dt_bias', 'grad_dn_out_norm', 'grad_w_out_odd', 'delta_meta_tokens', 'delta_pre_mix_norm', 'delta_post_mix_norm', 'delta_pre_mlp_norm', 'delta_post_mlp_norm', 'delta_mlp_w1', 'delta_mlp_w2', 'delta_w_in_even', 'delta_w_out_even', 'delta_sb_out_norm', 'delta_s5_lambda_re', 'delta_s5_lambda_im', 'delta_s5_log_dt', 'delta_s5_b_re', 'delta_s5_b_im', 'delta_s5_c_re', 'delta_s5_c_im', 'delta_s5_d', 'delta_s5_w_glu', 'delta_s5_b_glu', 'delta_s5_out_norm', 'delta_w_in_odd', 'delta_dn_conv_w', 'delta_dn_a_log', 'delta_dn_dt_bias', 'delta_dn_out_norm', 'delta_w_out_odd', 'new_m_meta_tokens', 'new_m_pre_mix_norm', 'new_m_post_mix_norm', 'new_m_pre_mlp_norm', 'new_m_post_mlp_norm', 'new_m_mlp_w1', 'new_m_mlp_w2', 'new_m_w_in_even', 'new_m_w_out_even', 'new_m_sb_out_norm', 'new_m_s5_lambda_re', 'new_m_s5_lambda_im', 'new_m_s5_log_dt', 'new_m_s5_b_re', 'new_m_s5_b_im', 'new_m_s5_c_re', 'new_m_s5_c_im', 'new_m_s5_d', 'new_m_s5_w_glu', 'new_m_s5_b_glu', 'new_m_s5_out_norm', 'new_m_w_in_odd', 'new_m_dn_conv_w', 'new_m_dn_a_log', 'new_m_dn_dt_bias', 'new_m_dn_out_norm', 'new_m_w_out_odd', 'new_v_meta_tokens', 'new_v_pre_mix_norm', 'new_v_post_mix_norm', 'new_v_pre_mlp_norm', 'new_v_post_mlp_norm', 'new_v_mlp_w1', 'new_v_mlp_w2', 'new_v_w_in_even', 'new_v_w_out_even', 'new_v_sb_out_norm', 'new_v_s5_lambda_re', 'new_v_s5_lambda_im', 'new_v_s5_log_dt', 'new_v_s5_b_re', 'new_v_s5_b_im', 'new_v_s5_c_re', 'new_v_s5_c_im', 'new_v_s5_d', 'new_v_s5_w_glu', 'new_v_s5_b_glu', 'new_v_s5_out_norm', 'new_v_w_in_odd', 'new_v_dn_conv_w', 'new_v_dn_a_log', 'new_v_dn_dt_bias', 'new_v_dn_out_norm', 'new_v_w_out_odd']
TWIN_LEAF_KINDS = {'loss': 'loss', 'grad_x': 'grad_x', 'grad_meta_tokens': 'grad_w', 'grad_pre_mix_norm': 'grad_w', 'grad_post_mix_norm': 'grad_w', 'grad_pre_mlp_norm': 'grad_w', 'grad_post_mlp_norm': 'grad_w', 'grad_mlp_w1': 'grad_w', 'grad_mlp_w2': 'grad_w', 'grad_w_in_even': 'grad_w', 'grad_w_out_even': 'grad_w', 'grad_sb_out_norm': 'grad_w', 'grad_s5_lambda_re': 'grad_w', 'grad_s5_lambda_im': 'grad_w', 'grad_s5_log_dt': 'grad_w', 'grad_s5_b_re': 'grad_w', 'grad_s5_b_im': 'grad_w', 'grad_s5_c_re': 'grad_w', 'grad_s5_c_im': 'grad_w', 'grad_s5_d': 'grad_w', 'grad_s5_w_glu': 'grad_w', 'grad_s5_b_glu': 'grad_w', 'grad_s5_out_norm': 'grad_w', 'grad_w_in_odd': 'grad_w', 'grad_dn_conv_w': 'grad_w', 'grad_dn_a_log': 'grad_w', 'grad_dn_dt_bias': 'grad_w', 'grad_dn_out_norm': 'grad_w', 'grad_w_out_odd': 'grad_w', 'delta_meta_tokens': 'delta_w', 'delta_pre_mix_norm': 'delta_w', 'delta_post_mix_norm': 'delta_w', 'delta_pre_mlp_norm': 'delta_w', 'delta_post_mlp_norm': 'delta_w', 'delta_mlp_w1': 'delta_w', 'delta_mlp_w2': 'delta_w', 'delta_w_in_even': 'delta_w', 'delta_w_out_even': 'delta_w', 'delta_sb_out_norm': 'delta_w', 'delta_s5_lambda_re': 'delta_w', 'delta_s5_lambda_im': 'delta_w', 'delta_s5_log_dt': 'delta_w', 'delta_s5_b_re': 'delta_w', 'delta_s5_b_im': 'delta_w', 'delta_s5_c_re': 'delta_w', 'delta_s5_c_im': 'delta_w', 'delta_s5_d': 'delta_w', 'delta_s5_w_glu': 'delta_w', 'delta_s5_b_glu': 'delta_w', 'delta_s5_out_norm': 'delta_w', 'delta_w_in_odd': 'delta_w', 'delta_dn_conv_w': 'delta_w', 'delta_dn_a_log': 'delta_w', 'delta_dn_dt_bias': 'delta_w', 'delta_dn_out_norm': 'delta_w', 'delta_w_out_odd': 'delta_w', 'new_m_meta_tokens': 'new_m', 'new_m_pre_mix_norm': 'new_m', 'new_m_post_mix_norm': 'new_m', 'new_m_pre_mlp_norm': 'new_m', 'new_m_post_mlp_norm': 'new_m', 'new_m_mlp_w1': 'new_m', 'new_m_mlp_w2': 'new_m', 'new_m_w_in_even': 'new_m', 'new_m_w_out_even': 'new_m', 'new_m_sb_out_norm': 'new_m', 'new_m_s5_lambda_re': 'new_m', 'new_m_s5_lambda_im': 'new_m', 'new_m_s5_log_dt': 'new_m', 'new_m_s5_b_re': 'new_m', 'new_m_s5_b_im': 'new_m', 'new_m_s5_c_re': 'new_m', 'new_m_s5_c_im': 'new_m', 'new_m_s5_d': 'new_m', 'new_m_s5_w_glu': 'new_m', 'new_m_s5_b_glu': 'new_m', 'new_m_s5_out_norm': 'new_m', 'new_m_w_in_odd': 'new_m', 'new_m_dn_conv_w': 'new_m', 'new_m_dn_a_log': 'new_m', 'new_m_dn_dt_bias': 'new_m', 'new_m_dn_out_norm': 'new_m', 'new_m_w_out_odd': 'new_m', 'new_v_meta_tokens': 'new_v', 'new_v_pre_mix_norm': 'new_v', 'new_v_post_mix_norm': 'new_v', 'new_v_pre_mlp_norm': 'new_v', 'new_v_post_mlp_norm': 'new_v', 'new_v_mlp_w1': 'new_v', 'new_v_mlp_w2': 'new_v', 'new_v_w_in_even': 'new_v', 'new_v_w_out_even': 'new_v', 'new_v_sb_out_norm': 'new_v', 'new_v_s5_lambda_re': 'new_v', 'new_v_s5_lambda_im': 'new_v', 'new_v_s5_log_dt': 'new_v', 'new_v_s5_b_re': 'new_v', 'new_v_s5_b_im': 'new_v', 'new_v_s5_c_re': 'new_v', 'new_v_s5_c_im': 'new_v', 'new_v_s5_d': 'new_v', 'new_v_s5_w_glu': 'new_v', 'new_v_s5_b_glu': 'new_v', 'new_v_s5_out_norm': 'new_v', 'new_v_w_in_odd': 'new_v', 'new_v_dn_conv_w': 'new_v', 'new_v_dn_a_log': 'new_v', 'new_v_dn_dt_bias': 'new_v', 'new_v_dn_out_norm': 'new_v', 'new_v_w_out_odd': 'new_v'}


def _forward(args):
    return _fwd_reference(*[args[k] for k in FWD_PARAMS])


def _output_shape():
    def fwd():
        inp = _fwd_setup_inputs(0)
        return _fwd_reference(*[inp[k] for k in FWD_PARAMS])
    out = _jax.eval_shape(fwd)
    return out.shape, out.dtype

N_MICROBATCH = 1
ADAM_LR = 0.001
ADAM_B1 = 0.9
ADAM_B2 = 0.999
ADAM_EPS = 1e-08
ADAM_WD = 0.01
ADAM_STEP = 10
PER_EXAMPLE_BATCH_AXIS = {'x': 0, 'loss_target': 0}
SHARED_INPUTS = []
_WEIGHT_DTYPES = {'meta_tokens': _jnp.float32, 'pre_mix_norm': _jnp.float32, 'post_mix_norm': _jnp.float32, 'pre_mlp_norm': _jnp.float32, 'post_mlp_norm': _jnp.float32, 'mlp_w1': _jnp.float32, 'mlp_w2': _jnp.float32, 'w_in_even': _jnp.float32, 'w_out_even': _jnp.float32, 'sb_out_norm': _jnp.float32, 's5_lambda_re': _jnp.float32, 's5_lambda_im': _jnp.float32, 's5_log_dt': _jnp.float32, 's5_b_re': _jnp.float32, 's5_b_im': _jnp.float32, 's5_c_re': _jnp.float32, 's5_c_im': _jnp.float32, 's5_d': _jnp.float32, 's5_w_glu': _jnp.float32, 's5_b_glu': _jnp.float32, 's5_out_norm': _jnp.float32, 'w_in_odd': _jnp.float32, 'dn_conv_w': _jnp.float32, 'dn_a_log': _jnp.float32, 'dn_dt_bias': _jnp.float32, 'dn_out_norm': _jnp.float32, 'w_out_odd': _jnp.float32}
MOMENT_SCALE = {'meta_tokens': 9.731820e-02, 'pre_mix_norm': 1.051143e+01, 'post_mix_norm': 1.313765e+02, 'pre_mlp_norm': 1.314155e+01, 'post_mlp_norm': 1.374509e+02, 'mlp_w1': 6.084310e+00, 'mlp_w2': 3.521915e+01, 'w_in_even': 1.958883e+00, 'w_out_even': 2.654669e+01, 'sb_out_norm': 3.485424e+00, 's5_lambda_re': 3.490978e-01, 's5_lambda_im': 3.837726e-01, 's5_log_dt': 2.360794e+02, 's5_b_re': 2.131427e-01, 's5_b_im': 2.186796e-01, 's5_c_re': 3.114721e-01, 's5_c_im': 3.285692e-01, 's5_d': 8.921115e+01, 's5_w_glu': 2.233488e+00, 's5_b_glu': 2.021706e+01, 's5_out_norm': 4.198623e+01, 'w_in_odd': 7.847863e+00, 'dn_conv_w': 9.089833e+00, 'dn_a_log': 2.112886e+01, 'dn_dt_bias': 2.071641e+01, 'dn_out_norm': 5.619614e+01, 'w_out_odd': 2.427802e+01}


def _to_microbatches(a, axis):
    t = _jnp.moveaxis(a, axis, 0)
    t = t.reshape((N_MICROBATCH, t.shape[0] // N_MICROBATCH) + t.shape[1:])
    return _jnp.moveaxis(t, 1, axis + 1)


def setup_inputs(seed: int = 0) -> dict:
    inp = _fwd_setup_inputs(seed)
    key = _jax.random.fold_in(_jax.random.key(seed), 7919)
    shape, _ = _output_shape()
    out = dict(inp)
    out["loss_target"] = _jax.random.normal(_jax.random.fold_in(key, 0), shape, _jnp.float32)
    for i, name in enumerate(TWIN_WEIGHTS):
        w = inp[name].astype(_jnp.float32)
        if MOMENT_SCALE is None:
            s = _jnp.sqrt(_jnp.mean(_jnp.square(w)) + 1e-30)
        else:
            s = MOMENT_SCALE[name]
        km, kv = _jax.random.split(_jax.random.fold_in(key, i + 1))
        out[name] = w
        out["m_" + name] = s * _jax.random.normal(km, w.shape, _jnp.float32)
        out["v_" + name] = (s * s) * _jax.random.uniform(kv, w.shape, _jnp.float32, 0.5, 1.5)
    if N_MICROBATCH > 1:
        for name, axis in PER_EXAMPLE_BATCH_AXIS.items():
            out[name] = _to_microbatches(out[name], axis)
    return {'x': out['x'], 'meta_tokens': out['meta_tokens'], 'pre_mix_norm': out['pre_mix_norm'], 'post_mix_norm': out['post_mix_norm'], 'pre_mlp_norm': out['pre_mlp_norm'], 'post_mlp_norm': out['post_mlp_norm'], 'mlp_w1': out['mlp_w1'], 'mlp_w2': out['mlp_w2'], 'w_in_even': out['w_in_even'], 'w_out_even': out['w_out_even'], 'sb_out_norm': out['sb_out_norm'], 's5_lambda_re': out['s5_lambda_re'], 's5_lambda_im': out['s5_lambda_im'], 's5_log_dt': out['s5_log_dt'], 's5_b_re': out['s5_b_re'], 's5_b_im': out['s5_b_im'], 's5_c_re': out['s5_c_re'], 's5_c_im': out['s5_c_im'], 's5_d': out['s5_d'], 's5_w_glu': out['s5_w_glu'], 's5_b_glu': out['s5_b_glu'], 's5_out_norm': out['s5_out_norm'], 'w_in_odd': out['w_in_odd'], 'dn_conv_w': out['dn_conv_w'], 'dn_a_log': out['dn_a_log'], 'dn_dt_bias': out['dn_dt_bias'], 'dn_out_norm': out['dn_out_norm'], 'w_out_odd': out['w_out_odd'], 'loss_target': out['loss_target'], 'm_meta_tokens': out['m_meta_tokens'], 'm_pre_mix_norm': out['m_pre_mix_norm'], 'm_post_mix_norm': out['m_post_mix_norm'], 'm_pre_mlp_norm': out['m_pre_mlp_norm'], 'm_post_mlp_norm': out['m_post_mlp_norm'], 'm_mlp_w1': out['m_mlp_w1'], 'm_mlp_w2': out['m_mlp_w2'], 'm_w_in_even': out['m_w_in_even'], 'm_w_out_even': out['m_w_out_even'], 'm_sb_out_norm': out['m_sb_out_norm'], 'm_s5_lambda_re': out['m_s5_lambda_re'], 'm_s5_lambda_im': out['m_s5_lambda_im'], 'm_s5_log_dt': out['m_s5_log_dt'], 'm_s5_b_re': out['m_s5_b_re'], 'm_s5_b_im': out['m_s5_b_im'], 'm_s5_c_re': out['m_s5_c_re'], 'm_s5_c_im': out['m_s5_c_im'], 'm_s5_d': out['m_s5_d'], 'm_s5_w_glu': out['m_s5_w_glu'], 'm_s5_b_glu': out['m_s5_b_glu'], 'm_s5_out_norm': out['m_s5_out_norm'], 'm_w_in_odd': out['m_w_in_odd'], 'm_dn_conv_w': out['m_dn_conv_w'], 'm_dn_a_log': out['m_dn_a_log'], 'm_dn_dt_bias': out['m_dn_dt_bias'], 'm_dn_out_norm': out['m_dn_out_norm'], 'm_w_out_odd': out['m_w_out_odd'], 'v_meta_tokens': out['v_meta_tokens'], 'v_pre_mix_norm': out['v_pre_mix_norm'], 'v_post_mix_norm': out['v_post_mix_norm'], 'v_pre_mlp_norm': out['v_pre_mlp_norm'], 'v_post_mlp_norm': out['v_post_mlp_norm'], 'v_mlp_w1': out['v_mlp_w1'], 'v_mlp_w2': out['v_mlp_w2'], 'v_w_in_even': out['v_w_in_even'], 'v_w_out_even': out['v_w_out_even'], 'v_sb_out_norm': out['v_sb_out_norm'], 'v_s5_lambda_re': out['v_s5_lambda_re'], 'v_s5_lambda_im': out['v_s5_lambda_im'], 'v_s5_log_dt': out['v_s5_log_dt'], 'v_s5_b_re': out['v_s5_b_re'], 'v_s5_b_im': out['v_s5_b_im'], 'v_s5_c_re': out['v_s5_c_re'], 'v_s5_c_im': out['v_s5_c_im'], 'v_s5_d': out['v_s5_d'], 'v_s5_w_glu': out['v_s5_w_glu'], 'v_s5_b_glu': out['v_s5_b_glu'], 'v_s5_out_norm': out['v_s5_out_norm'], 'v_w_in_odd': out['v_w_in_odd'], 'v_dn_conv_w': out['v_dn_conv_w'], 'v_dn_a_log': out['v_dn_a_log'], 'v_dn_dt_bias': out['v_dn_dt_bias'], 'v_dn_out_norm': out['v_dn_out_norm'], 'v_w_out_odd': out['v_w_out_odd']}


def _loss(weights, diff, rest, loss_target):
    with _jax.named_scope("forward"):
        args = {**rest, TWIN_DIFF_INPUT: diff, **{k: w.astype(_WEIGHT_DTYPES[k]) for k, w in weights.items()}}
        y = _forward(args)
    with _jax.named_scope("loss_head"):
        err = _jnp.square(y.astype(_jnp.float32) - loss_target)
        return 0.5 * _jnp.sum(_jnp.mean(err, axis=-1)) if err.ndim else 0.5 * err


def _adamw(w, g, m, v):
    m = ADAM_B1 * m + (1.0 - ADAM_B1) * g
    v = ADAM_B2 * v + (1.0 - ADAM_B2) * _jnp.square(g)
    m_hat = m / (1.0 - ADAM_B1 ** ADAM_STEP)
    v_hat = v / (1.0 - ADAM_B2 ** ADAM_STEP)
    delta = -ADAM_LR * (m_hat / (_jnp.sqrt(v_hat) + ADAM_EPS) + ADAM_WD * w)
    return delta, m, v


def reference(x, meta_tokens, pre_mix_norm, post_mix_norm, pre_mlp_norm, post_mlp_norm, mlp_w1, mlp_w2, w_in_even, w_out_even, sb_out_norm, s5_lambda_re, s5_lambda_im, s5_log_dt, s5_b_re, s5_b_im, s5_c_re, s5_c_im, s5_d, s5_w_glu, s5_b_glu, s5_out_norm, w_in_odd, dn_conv_w, dn_a_log, dn_dt_bias, dn_out_norm, w_out_odd, loss_target, m_meta_tokens, m_pre_mix_norm, m_post_mix_norm, m_pre_mlp_norm, m_post_mlp_norm, m_mlp_w1, m_mlp_w2, m_w_in_even, m_w_out_even, m_sb_out_norm, m_s5_lambda_re, m_s5_lambda_im, m_s5_log_dt, m_s5_b_re, m_s5_b_im, m_s5_c_re, m_s5_c_im, m_s5_d, m_s5_w_glu, m_s5_b_glu, m_s5_out_norm, m_w_in_odd, m_dn_conv_w, m_dn_a_log, m_dn_dt_bias, m_dn_out_norm, m_w_out_odd, v_meta_tokens, v_pre_mix_norm, v_post_mix_norm, v_pre_mlp_norm, v_post_mlp_norm, v_mlp_w1, v_mlp_w2, v_w_in_even, v_w_out_even, v_sb_out_norm, v_s5_lambda_re, v_s5_lambda_im, v_s5_log_dt, v_s5_b_re, v_s5_b_im, v_s5_c_re, v_s5_c_im, v_s5_d, v_s5_w_glu, v_s5_b_glu, v_s5_out_norm, v_w_in_odd, v_dn_conv_w, v_dn_a_log, v_dn_dt_bias, v_dn_out_norm, v_w_out_odd):
    given = dict(x=x, meta_tokens=meta_tokens, pre_mix_norm=pre_mix_norm, post_mix_norm=post_mix_norm, pre_mlp_norm=pre_mlp_norm, post_mlp_norm=post_mlp_norm, mlp_w1=mlp_w1, mlp_w2=mlp_w2, w_in_even=w_in_even, w_out_even=w_out_even, sb_out_norm=sb_out_norm, s5_lambda_re=s5_lambda_re, s5_lambda_im=s5_lambda_im, s5_log_dt=s5_log_dt, s5_b_re=s5_b_re, s5_b_im=s5_b_im, s5_c_re=s5_c_re, s5_c_im=s5_c_im, s5_d=s5_d, s5_w_glu=s5_w_glu, s5_b_glu=s5_b_glu, s5_out_norm=s5_out_norm, w_in_odd=w_in_odd, dn_conv_w=dn_conv_w, dn_a_log=dn_a_log, dn_dt_bias=dn_dt_bias, dn_out_norm=dn_out_norm, w_out_odd=w_out_odd, loss_target=loss_target, m_meta_tokens=m_meta_tokens, m_pre_mix_norm=m_pre_mix_norm, m_post_mix_norm=m_post_mix_norm, m_pre_mlp_norm=m_pre_mlp_norm, m_post_mlp_norm=m_post_mlp_norm, m_mlp_w1=m_mlp_w1, m_mlp_w2=m_mlp_w2, m_w_in_even=m_w_in_even, m_w_out_even=m_w_out_even, m_sb_out_norm=m_sb_out_norm, m_s5_lambda_re=m_s5_lambda_re, m_s5_lambda_im=m_s5_lambda_im, m_s5_log_dt=m_s5_log_dt, m_s5_b_re=m_s5_b_re, m_s5_b_im=m_s5_b_im, m_s5_c_re=m_s5_c_re, m_s5_c_im=m_s5_c_im, m_s5_d=m_s5_d, m_s5_w_glu=m_s5_w_glu, m_s5_b_glu=m_s5_b_glu, m_s5_out_norm=m_s5_out_norm, m_w_in_odd=m_w_in_odd, m_dn_conv_w=m_dn_conv_w, m_dn_a_log=m_dn_a_log, m_dn_dt_bias=m_dn_dt_bias, m_dn_out_norm=m_dn_out_norm, m_w_out_odd=m_w_out_odd, v_meta_tokens=v_meta_tokens, v_pre_mix_norm=v_pre_mix_norm, v_post_mix_norm=v_post_mix_norm, v_pre_mlp_norm=v_pre_mlp_norm, v_post_mlp_norm=v_post_mlp_norm, v_mlp_w1=v_mlp_w1, v_mlp_w2=v_mlp_w2, v_w_in_even=v_w_in_even, v_w_out_even=v_w_out_even, v_sb_out_norm=v_sb_out_norm, v_s5_lambda_re=v_s5_lambda_re, v_s5_lambda_im=v_s5_lambda_im, v_s5_log_dt=v_s5_log_dt, v_s5_b_re=v_s5_b_re, v_s5_b_im=v_s5_b_im, v_s5_c_re=v_s5_c_re, v_s5_c_im=v_s5_c_im, v_s5_d=v_s5_d, v_s5_w_glu=v_s5_w_glu, v_s5_b_glu=v_s5_b_glu, v_s5_out_norm=v_s5_out_norm, v_w_in_odd=v_w_in_odd, v_dn_conv_w=v_dn_conv_w, v_dn_a_log=v_dn_a_log, v_dn_dt_bias=v_dn_dt_bias, v_dn_out_norm=v_dn_out_norm, v_w_out_odd=v_w_out_odd)
    weights = {n: given[n] for n in TWIN_WEIGHTS}
    shared = {n: given[n] for n in SHARED_INPUTS}
    per_example = {n: given[n] for n in ['x']}
    grad_fn = _jax.value_and_grad(_loss, argnums=(0, 1))

    def one_microbatch(ex, loss_target):
        ex = dict(ex)
        diff = ex.pop(TWIN_DIFF_INPUT)
        return grad_fn(weights, diff, {**shared, **ex}, loss_target)

    if N_MICROBATCH == 1:
        loss, (grad_w, grad_x) = one_microbatch(per_example, given["loss_target"])
    else:
        def body(carry, xs):
            loss_sum, grad_sum = carry
            l_k, (gw_k, gx_k) = one_microbatch(xs[0], xs[1])
            with _jax.named_scope("update"):
                return (loss_sum + l_k, _jax.tree.map(_jnp.add, grad_sum, gw_k)), gx_k

        init = (_jnp.zeros((), _jnp.float32), _jax.tree.map(_jnp.zeros_like, weights))
        (loss, grad_w), grad_x = _jax.lax.scan(body, init, (per_example, given["loss_target"]))
    with _jax.named_scope("update"):
        delta_w, new_m, new_v = {}, {}, {}
        for n in TWIN_WEIGHTS:
            delta_w[n], new_m[n], new_v[n] = _adamw(weights[n], grad_w[n], given["m_" + n], given["v_" + n])
    return (loss, grad_x, *[grad_w[n] for n in TWIN_WEIGHTS], *[delta_w[n] for n in TWIN_WEIGHTS],
            *[new_m[n] for n in TWIN_WEIGHTS], *[new_v[n] for n in TWIN_WEIGHTS])
```

```python
import functools
import math

import jax
import jax.numpy as jnp
from jax import lax
from jax.experimental import pallas as pl
from jax.experimental.pallas import tpu as pltpu

F32 = jnp.float32
BF16 = jnp.bfloat16

D_MODEL = 1024
N_META = 16
SB_HEAD_DIM = 64
SB_WIDTH = 512
S5_WIDTH = 512
S5_GROUP = 16
S5_GROUPS = 32
S5_STATE = 64
S5_NS = S5_GROUPS * S5_STATE
DN_HEAD_DIM = 128
DN_HEADS = 8
DN_WIDTH = 1024
DN_CONV = 4
D_FF = 4096
EPS = 1e-6
N_DEV = 8

ADAM_LR = 0.001
ADAM_B1 = 0.9
ADAM_B2 = 0.999
ADAM_EPS = 1e-08
ADAM_WD = 0.01
ADAM_STEP = 10

ROW_TILE = 512
ATT_BLK = 256
DN_CHUNK = 128
DN_SUB = 16
S5_TILE = 128
VMEM_LIMIT = 56 * 1024 * 1024

_HIGH = lax.Precision.HIGHEST


def _pallas(body, **kw):
    return pl.pallas_call(body, **kw)


def _cparams(sem):
    return pltpu.CompilerParams(dimension_semantics=sem, vmem_limit_bytes=VMEM_LIMIT)


def _dot(a, b, dims=((1,), (0,))):
    return lax.dot_general(a, b, (dims, ((), ())), preferred_element_type=F32)


def _dot_hi(a, b):
    return lax.dot_general(a, b, (((1,), (0,)), ((), ())), preferred_element_type=F32, precision=_HIGH)


def _split_dot(m_bf16, x):
    hi = x.astype(BF16)
    lo = (x - hi.astype(F32)).astype(BF16)
    return _dot(m_bf16, hi) + _dot(m_bf16, lo)


def _matmul(a, b, *, ta=False, tb=False, tm, tn, tk, name, out_dtypes=(F32,), extras=(), epilogue=None):
    m, k = (a.shape[1], a.shape[0]) if ta else a.shape
    n = b.shape[0] if tb else b.shape[1]
    assert (b.shape[1] if tb else b.shape[0]) == k
    assert m % tm == 0 and n % tn == 0 and k % tk == 0, (name, m, n, k, tm, tn, tk)
    nk = k // tk
    n_ex = len(extras)
    n_out = len(out_dtypes)
    dims = ((0 if ta else 1,), (1 if tb else 0,))

    def body(*refs):
        a_ref, b_ref = refs[0], refs[1]
        ex_refs = refs[2:2 + n_ex]
        o_refs = refs[2 + n_ex:2 + n_ex + n_out]
        acc_ref = refs[-1]
        kk = pl.program_id(2)

        @pl.when(kk == 0)
        def _():
            acc_ref[...] = jnp.zeros_like(acc_ref)

        acc_ref[...] += _dot(a_ref[...].astype(BF16), b_ref[...].astype(BF16), dims)

        @pl.when(kk == nk - 1)
        def _():
            acc = acc_ref[...]
            outs = (acc,) if epilogue is None else epilogue(acc, *[r[...] for r in ex_refs])
            for o_ref, o in zip(o_refs, outs):
                o_ref[...] = o.astype(o_ref.dtype)

    a_spec = pl.BlockSpec((tk, tm), lambda i, j, kk: (kk, i)) if ta else pl.BlockSpec((tm, tk), lambda i, j, kk: (i, kk))
    b_spec = pl.BlockSpec((tn, tk), lambda i, j, kk: (j, kk)) if tb else pl.BlockSpec((tk, tn), lambda i, j, kk: (kk, j))
    o_spec = pl.BlockSpec((tm, tn), lambda i, j, kk: (i, j))
    outs = _pallas(
        body, name=name,
        grid=(m // tm, n // tn, nk),
        in_specs=[a_spec, b_spec] + [o_spec] * n_ex,
        out_specs=[o_spec] * n_out,
        out_shape=[jax.ShapeDtypeStruct((m, n), dt) for dt in out_dtypes],
        scratch_shapes=[pltpu.VMEM((tm, tn), F32)],
        compiler_params=_cparams(("parallel", "parallel", "arbitrary")),
    )(a, b, *extras)
    return outs[0] if n_out == 1 else outs


def _tile(n, cap):
    best = 128
    for t in range(128, min(n, cap) + 1, 128):
        if n % t == 0:
            best = t
    assert n % best == 0, n
    return best


def _mm_fwd(x, w, name, **kw):
    k, n = w.shape
    return _matmul(x, w, tm=ROW_TILE, tn=_tile(n, 1024), tk=_tile(k, 1024), name=name, **kw)


def _mm_dgrad(dy, w, name, **kw):
    k, n = w.shape
    return _matmul(dy, w, tb=True, tm=ROW_TILE, tn=_tile(k, 1024), tk=_tile(n, 1024), name=name, **kw)


def _mm_wgrad(x, dy, name):
    k, n = x.shape[1], dy.shape[1]
    return _matmul(x, dy, ta=True, tm=_tile(k, 512), tn=_tile(n, 1024), tk=ROW_TILE, name=name)


def _rms(x, g):
    r = lax.rsqrt(jnp.mean(x * x, axis=-1, keepdims=True) + EPS)
    return x * r * g


def _rms_bwd(x, g, dy):
    r = lax.rsqrt(jnp.mean(x * x, axis=-1, keepdims=True) + EPS)
    xh = x * r
    dxh = dy * g
    dx = r * (dxh - xh * jnp.mean(dxh * xh, axis=-1, keepdims=True))
    dg = jnp.sum(dy * xh, axis=0, keepdims=True)
    return dx, dg


def _row_spec(width, tile=ROW_TILE):
    return pl.BlockSpec((tile, width), lambda i: (i, 0))


def _vec_spec(width):
    return pl.BlockSpec((1, width), lambda i: (0, 0))


def _norm_pre(hs, g, name):
    r, d = hs.shape

    def body(x_ref, g_ref, o_ref):
        o_ref[...] = _rms(x_ref[...], g_ref[...]).astype(BF16)

    return _pallas(body, name=name, grid=(r // ROW_TILE,), in_specs=[_row_spec(d), _vec_spec(d)],
                   out_specs=_row_spec(d), out_shape=jax.ShapeDtypeStruct((r, d), BF16),
                   compiler_params=_cparams(("parallel",)))(hs, g)


def _norm_post_pre(hs, m, g_post, g_pre, name):
    r, d = hs.shape

    def body(hs_ref, m_ref, gp_ref, gn_ref, o_ref, hn_ref):
        new = hs_ref[...] + _rms(m_ref[...], gp_ref[...])
        o_ref[...] = new
        hn_ref[...] = _rms(new, gn_ref[...]).astype(BF16)

    return _pallas(body, name=name, grid=(r // ROW_TILE,),
                   in_specs=[_row_spec(d), _row_spec(d), _vec_spec(d), _vec_spec(d)],
                   out_specs=[_row_spec(d), _row_spec(d)],
                   out_shape=[jax.ShapeDtypeStruct((r, d), F32), jax.ShapeDtypeStruct((r, d), BF16)],
                   compiler_params=_cparams(("parallel",)))(hs, m, g_post, g_pre)


def _norm_post_loss(hs, m, g_post, target, pad_tiles, name):
    r, d = hs.shape
    nt = r // ROW_TILE

    def body(hs_ref, m_ref, gp_ref, t_ref, dhs_ref, loss_ref):
        i = pl.program_id(0)
        new = hs_ref[...] + _rms(m_ref[...], gp_ref[...])
        live = (i >= pad_tiles).astype(F32)
        diff = (new - t_ref[...]) * live
        dhs_ref[...] = diff * (1.0 / d)
        loss_ref[...] = jnp.full((8, 128), 0.5 / d * jnp.sum(diff * diff), F32)

    dhs, parts = _pallas(
        body, name=name, grid=(nt,),
        in_specs=[_row_spec(d), _row_spec(d), _vec_spec(d),
                  pl.BlockSpec((ROW_TILE, d), lambda i: (jnp.maximum(i - pad_tiles, 0), 0))],
        out_specs=[_row_spec(d), pl.BlockSpec((8, 128), lambda i: (i, 0))],
        out_shape=[jax.ShapeDtypeStruct((r, d), F32), jax.ShapeDtypeStruct((nt * 8, 128), F32)],
        compiler_params=_cparams(("parallel",)))(hs, m, g_post, target)
    return dhs, jnp.sum(parts[::8, 0])


def _norm_bwd(dhs, *, pre=None, post=None, pad=0, name):
    r, d = dhs.shape
    has_pre, has_post = pre is not None, post is not None

    def body(*refs):
        it = iter(refs)
        dhs_ref = next(it)
        if has_pre:
            hs_ref, gn_ref, dhn_ref = next(it), next(it), next(it)
        if has_post:
            m_ref, gp_ref = next(it), next(it)
        if has_pre:
            o_dhs, o_dgn = next(it), next(it)
        if has_post:
            o_dm, o_dgp = next(it), next(it)
        i = pl.program_id(0)
        live = (i * ROW_TILE + lax.broadcasted_iota(jnp.int32, (ROW_TILE, 1), 0)) >= pad
        cur = jnp.where(live, dhs_ref[...], 0.0)
        if has_pre:
            dx, dg = _rms_bwd(hs_ref[...], gn_ref[...], jnp.where(live, dhn_ref[...].astype(F32), 0.0))
            cur = cur + dx
            o_dhs[...] = cur

            @pl.when(i == 0)
            def _():
                o_dgn[...] = jnp.zeros_like(o_dgn)
            o_dgn[...] += dg
        if has_post:
            dm, dg = _rms_bwd(m_ref[...], gp_ref[...], cur)
            o_dm[...] = dm

            @pl.when(i == 0)
            def _():
                o_dgp[...] = jnp.zeros_like(o_dgp)
            o_dgp[...] += dg

    ins, in_specs, out_specs, out_shape = [dhs], [_row_spec(d)], [], []
    if has_pre:
        ins += list(pre)
        in_specs += [_row_spec(d), _vec_spec(d), _row_spec(d)]
        out_specs += [_row_spec(d), _vec_spec(d)]
        out_shape += [jax.ShapeDtypeStruct((r, d), F32), jax.ShapeDtypeStruct((1, d), F32)]
    if has_post:
        ins += list(post)
        in_specs += [_row_spec(d), _vec_spec(d)]
        out_specs += [_row_spec(d), _vec_spec(d)]
        out_shape += [jax.ShapeDtypeStruct((r, d), F32), jax.ShapeDtypeStruct((1, d), F32)]
    outs = list(_pallas(body, name=name, grid=(r // ROW_TILE,), in_specs=in_specs, out_specs=out_specs,
                        out_shape=out_shape, compiler_params=_cparams(("arbitrary",)))(*ins))
    dhs_new, dgn = (outs.pop(0), outs.pop(0)) if has_pre else (dhs, None)
    dm, dgp = (outs.pop(0), outs.pop(0)) if has_post else (None, None)
    return dhs_new, dm, dgn, dgp


def _softplus(z):
    return jnp.maximum(z, 0.0) + jnp.log(1.0 + jnp.exp(-jnp.abs(z)))


def _sb_consts(t):
    row = lax.broadcasted_iota(jnp.int32, (t, t), 0)
    col = lax.broadcasted_iota(jnp.int32, (t, t), 1)
    m_up = (col >= row).astype(BF16)
    m_low = (col <= row).astype(BF16)
    return m_up, m_low


def _sb_fwd(q, k, vt3, pad, name):
    r = q.shape[0]
    t = ATT_BLK
    nb = r // t
    nbp = -(-nb // 8) * 8
    jmin = pad // t
    scale = SB_HEAD_DIM ** -0.5

    def body(q_ref, k_ref, vt_ref, o_ref, ss_ref, acc_ref):
        i = pl.program_id(1)
        qt = q_ref[...].astype(F32).T
        sub = lax.broadcasted_iota(jnp.int32, (128, 1), 0)
        m_up, _ = _sb_consts(t)
        kpos0 = lax.broadcasted_iota(jnp.int32, (t, 1), 0)
        qpos = i * t + lax.broadcasted_iota(jnp.int32, (1, t), 1)
        n_mid = jnp.maximum(i - 1 - jmin, 0)
        n_edge = jnp.where(i > jmin, 1, 0)
        for h in range(2):
            in_head = (sub >= 64 * h) & (sub < 64 * (h + 1))
            qth = jnp.where(in_head, qt, 0.0).astype(BF16)
            acc_ref[h] = jnp.zeros((128, t), F32)

            def step(j, s, masked):
                kb = k_ref[pl.ds(pl.multiple_of(j * t, t), t), :]
                zt = _dot(kb, qth) * scale
                sp = _softplus(zt)
                lk = -sp
                if masked:
                    kpos = j * t + kpos0
                    valid = (kpos < qpos) & (kpos >= pad)
                    lk = jnp.where(valid, lk, 0.0)
                inc = _split_dot(m_up, lk)
                ss_ref[h, 0, pl.ds(j, 1), :] = s
                w = jnp.exp(zt - sp + (inc - lk) + s)
                if masked:
                    w = jnp.where(valid, w, 0.0)
                acc_ref[h] += _dot(vt_ref[0, j], w.astype(BF16))
                return s + inc[0:1, :]

            s = step(i, jnp.zeros((1, t), F32), True)
            s = lax.fori_loop(0, n_mid, lambda it, c: step(i - 1 - it, c, False), s)
            lax.fori_loop(0, n_edge, lambda it, c: step(jmin + it * 0, c, True), s)
        acc = jnp.where(sub < 64, acc_ref[0], acc_ref[1])
        o_ref[...] = acc.T

    return _pallas(
        body, name=name, grid=(4, nb),
        in_specs=[pl.BlockSpec((t, 128), lambda hp, i: (i, hp)),
                  pl.BlockSpec((r, 128), lambda hp, i: (0, hp)),
                  pl.BlockSpec((1, nb, 128, t), lambda hp, i: (hp, 0, 0, 0))],
        out_specs=[pl.BlockSpec((t, 128), lambda hp, i: (i, hp)),
                   pl.BlockSpec((2, 1, nbp, t), lambda hp, i: (hp, i, 0, 0))],
        out_shape=[jax.ShapeDtypeStruct((r, SB_WIDTH), F32),
                   jax.ShapeDtypeStruct((8, nb, nbp, t), F32)],
        scratch_shapes=[pltpu.VMEM((2, 128, t), F32)],
        compiler_params=_cparams(("parallel", "arbitrary")),
    )(q, k, vt3)


def _sb_bwd(q, k, v, kt3, ssave, do, pad, name):
    r = q.shape[0]
    t = ATT_BLK
    nb = r // t
    nbp = ssave.shape[2]
    jmin = pad // t
    scale = SB_HEAD_DIM ** -0.5

    def body(q_ref, do_ref, k_ref, v_ref, kt_ref, ss_ref, dq_ref, dk_hbm, dv_hbm, dk_acc, dv_acc, dq_acc, sem):
        hp = pl.program_id(0)
        i = pl.program_id(1)

        @pl.when(i == 0)
        def _():
            dk_acc[...] = jnp.zeros_like(dk_acc)
            dv_acc[...] = jnp.zeros_like(dv_acc)

        qf = q_ref[...].astype(F32)
        dof = do_ref[...]
        qt = qf.T
        dot_ = dof.T
        sub = lax.broadcasted_iota(jnp.int32, (128, 1), 0)
        lane = lax.broadcasted_iota(jnp.int32, (1, 128), 1)
        m_up, m_low = _sb_consts(t)
        kpos0 = lax.broadcasted_iota(jnp.int32, (t, 1), 0)
        qpos = i * t + lax.broadcasted_iota(jnp.int32, (1, t), 1)
        n_mid = jnp.maximum(i - 1 - jmin, 0)
        n_edge = jnp.where(i > jmin, 1, 0)
        for h in range(2):
            in_t = (sub >= 64 * h) & (sub < 64 * (h + 1))
            in_l = (lane >= 64 * h) & (lane < 64 * (h + 1))
            qth = jnp.where(in_t, qt, 0.0).astype(BF16)
            doth = jnp.where(in_t, dot_, 0.0).astype(BF16)
            qh = jnp.where(in_l, qf, 0.0).astype(BF16)
            doh = jnp.where(in_l, dof, 0.0).astype(BF16)
            dq_acc[h] = jnp.zeros((128, t), F32)

            def step(j, e_carry, masked):
                rows = pl.ds(pl.multiple_of(j * t, t), t)
                kb = k_ref[rows, :]
                vb = v_ref[rows, :]
                zt = _dot(kb, qth) * scale
                sp = _softplus(zt)
                lk = -sp
                if masked:
                    kpos = j * t + kpos0
                    valid = (kpos < qpos) & (kpos >= pad)
                    lk = jnp.where(valid, lk, 0.0)
                inc = _split_dot(m_up, lk)
                la = zt - sp
                w = jnp.exp(la + (inc - lk) + ss_ref[h, 0, pl.ds(j, 1), :])
                if masked:
                    w = jnp.where(valid, w, 0.0)
                dv_acc[rows, :] += _dot(w.astype(BF16), doh)
                e = w * _dot(vb, doth)
                pinc = _split_dot(m_low, e)
                big_e = pinc - e + e_carry
                dz = (e * jnp.exp(-sp) - big_e * jnp.exp(la)) * scale
                if masked:
                    dz = jnp.where(valid, dz, 0.0)
                dzb = dz.astype(BF16)
                dk_acc[rows, :] += _dot(dzb, qh)
                dq_acc[h] += _dot(kt_ref[0, j], dzb)
                return e_carry + pinc[t - 1:t, :]

            e0 = jnp.zeros((1, t), F32)
            e0 = lax.fori_loop(0, n_edge, lambda it, c: step(jmin + it * 0, c, True), e0)
            e0 = lax.fori_loop(0, n_mid, lambda it, c: step(jmin + 1 + it, c, False), e0)
            step(i, e0, True)
        dq_ref[...] = jnp.where(sub < 64, dq_acc[0], dq_acc[1]).T

        @pl.when(i == nb - 1)
        def _():
            c1 = pltpu.make_async_copy(dk_acc, dk_hbm.at[hp], sem.at[0])
            c2 = pltpu.make_async_copy(dv_acc, dv_hbm.at[hp], sem.at[1])
            c1.start()
            c2.start()
            c1.wait()
            c2.wait()

    return _pallas(
        body, name=name, grid=(4, nb),
        in_specs=[pl.BlockSpec((t, 128), lambda hp, i: (i, hp)),
                  pl.BlockSpec((t, 128), lambda hp, i: (i, hp)),
                  pl.BlockSpec((r, 128), lambda hp, i: (0, hp)),
                  pl.BlockSpec((r, 128), lambda hp, i: (0, hp)),
                  pl.BlockSpec((1, nb, 128, t), lambda hp, i: (hp, 0, 0, 0)),
                  pl.BlockSpec((2, 1, nbp, t), lambda hp, i: (hp, i, 0, 0))],
        out_specs=[pl.BlockSpec((t, 128), lambda hp, i: (i, hp)),
                   pl.BlockSpec(memory_space=pl.ANY), pl.BlockSpec(memory_space=pl.ANY)],
        out_shape=[jax.ShapeDtypeStruct((r, SB_WIDTH), F32),
                   jax.ShapeDtypeStruct((4, r, 128), F32), jax.ShapeDtypeStruct((4, r, 128), F32)],
        scratch_shapes=[pltpu.VMEM((r, 128), F32), pltpu.VMEM((r, 128), F32), pltpu.VMEM((2, 128, t), F32),
                        pltpu.SemaphoreType.DMA((2,))],
        compiler_params=_cparams(("arbitrary", "arbitrary")),
    )(q, do, k, v, kt3, ssave)


def _s5_disc(lam_re, lam_im, logdt, btr, bti):
    lr = jnp.minimum(lam_re, -1e-4)
    li = lam_im
    dt = jnp.exp(logdt)
    mag = jnp.exp(lr * dt)
    ang = li * dt
    a_re, a_im = mag * jnp.cos(ang), mag * jnp.sin(ang)
    den = lr * lr + li * li
    nr, ni = a_re - 1.0, a_im
    c_re = (nr * lr + ni * li) / den
    c_im = (ni * lr - nr * li) / den
    return a_re, a_im, c_re * btr - c_im * bti, c_re * bti + c_im * btr


def _s5_prep(lam_re, lam_im, logdt, btr, bti, name):
    ns = lam_re.shape[1]

    def body(lr_ref, li_ref, dt_ref, br_ref, bi_ref, ar_ref, ai_ref, bbr_ref, bbi_ref):
        ar, ai, bbr, bbi = _s5_disc(lr_ref[...], li_ref[...], dt_ref[...], br_ref[...], bi_ref[...])
        ar_ref[...] = ar
        ai_ref[...] = ai
        bbr_ref[...] = bbr
        bbi_ref[...] = bbi

    return _pallas(body, name=name,
                   out_shape=[jax.ShapeDtypeStruct((1, ns), F32)] * 2 + [jax.ShapeDtypeStruct((S5_GROUP, ns), F32)] * 2,
                   )(lam_re, lam_im, logdt, btr, bti)


def _s5_prep_bwd(lam_re, lam_im, logdt, btr, bti, dar, dai, dbbr, dbbi, name):
    ns = lam_re.shape[1]

    def body(lr_ref, li_ref, dt_ref, br_ref, bi_ref, dar_ref, dai_ref, dbr_ref, dbi_ref, o_lr, o_li, o_dt, o_br, o_bi):
        _, vjp = jax.vjp(_s5_disc, lr_ref[...], li_ref[...], dt_ref[...], br_ref[...], bi_ref[...])
        g = vjp((dar_ref[...], dai_ref[...], dbr_ref[...], dbi_ref[...]))
        o_lr[...] = g[0]
        o_li[...] = g[1]
        row = lax.broadcasted_iota(jnp.int32, (ns, ns), 0) // S5_STATE
        col = lax.broadcasted_iota(jnp.int32, (ns, ns), 1) // S5_STATE
        same = (row == col).astype(F32)
        o_dt[...] = _dot_hi(jnp.broadcast_to(g[2], (8, ns)), same)[0:1]
        o_br[...] = g[3]
        o_bi[...] = g[4]

    return _pallas(body, name=name,
                   out_shape=[jax.ShapeDtypeStruct((1, ns), F32)] * 3 + [jax.ShapeDtypeStruct((S5_GROUP, ns), F32)] * 2,
                   compiler_params=pltpu.CompilerParams(vmem_limit_bytes=VMEM_LIMIT),
                   )(lam_re, lam_im, logdt, btr, bti, dar, dai, dbbr, dbbi)


def _s5_scan(br, bi, ar, ai, t, reverse=False):
    row = lax.broadcasted_iota(jnp.int32, (t, 1), 0)
    pr, pi_ = ar, ai
    k = 1
    while k < t:
        if reverse:
            sr, si, ok = pltpu.roll(br, t - k, 0), pltpu.roll(bi, t - k, 0), row < t - k
        else:
            sr, si, ok = pltpu.roll(br, k, 0), pltpu.roll(bi, k, 0), row >= k
        sr = jnp.where(ok, sr, 0.0)
        si = jnp.where(ok, si, 0.0)
        br, bi = br + pr * sr - pi_ * si, bi + pr * si + pi_ * sr
        pr, pi_ = pr * pr - pi_ * pi_, 2.0 * pr * pi_
        k *= 2
    return br, bi


def _s5_power_table(ar, ai, t, reverse=False):
    row = lax.broadcasted_iota(jnp.int32, (t, 1), 0)
    hot = row == (t - 1 if reverse else 0)
    return _s5_scan(jnp.where(hot, ar, 0.0), jnp.where(hot, ai, 0.0), ar, ai, t, reverse)


_GELU_C = math.sqrt(2.0 / math.pi)


def _gelu(y):
    th = jnp.tanh(_GELU_C * (y + 0.044715 * y * y * y))
    return 0.5 * y * (1.0 + th), th


def _sigmoid(x):
    return 1.0 / (1.0 + jnp.exp(-x))


def _s5_fwd(u, wb, a, wc, dskip, wglu, bglu, gnorm, name):
    r = u.shape[0]
    t = S5_TILE
    nt = r // t
    ns = wb.shape[2]
    w = S5_WIDTH

    def body(u_ref, wb_ref, a_ref, wc_ref, d_ref, wg_ref, bg_ref, gn_ref, y_ref, on_ref, xs_ref, pw_ref, carry_ref):
        i = pl.program_id(0)
        ar, ai = a_ref[0], a_ref[1]

        @pl.when(i == 0)
        def _():
            pr, pi_ = _s5_power_table(ar, ai, t)
            pw_ref[0] = pr
            pw_ref[1] = pi_
            carry_ref[...] = jnp.zeros_like(carry_ref)

        u_ = u_ref[...]
        ub = u_.astype(BF16)
        xr, xi = _s5_scan(_dot(ub, wb_ref[0]), _dot(ub, wb_ref[1]), ar, ai, t)
        cr, ci = carry_ref[0], carry_ref[1]
        xs_ref[0, 0:1, :] = cr
        xs_ref[0, 1:2, :] = ci
        pr, pi_ = pw_ref[0], pw_ref[1]
        xr = xr + pr * cr - pi_ * ci
        xi = xi + pr * ci + pi_ * cr
        carry_ref[0] = xr[t - 1:t, :]
        carry_ref[1] = xi[t - 1:t, :]
        y = _dot(xr.astype(BF16), wc_ref[0]) - _dot(xi.astype(BF16), wc_ref[1]) + d_ref[...] * u_
        h, _ = _gelu(y)
        gate = _sigmoid(_dot(h.astype(BF16), wg_ref[...]) + bg_ref[...])
        y_ref[...] = y
        on_ref[...] = _rms(h * gate, gn_ref[...]).astype(BF16)

    full = lambda shape: pl.BlockSpec(shape, lambda i: (0,) * len(shape))
    return _pallas(
        body, name=name, grid=(nt,),
        in_specs=[_row_spec(w, t), full((2, w, ns)), full((2, 1, ns)), full((2, ns, w)), full((1, w)),
                  full((w, w)), full((1, w)), full((1, w))],
        out_specs=[_row_spec(w, t), _row_spec(w, t), pl.BlockSpec((1, 2, ns), lambda i: (i, 0, 0))],
        out_shape=[jax.ShapeDtypeStruct((r, w), F32), jax.ShapeDtypeStruct((r, w), BF16),
                   jax.ShapeDtypeStruct((nt, 2, ns), F32)],
        scratch_shapes=[pltpu.VMEM((2, t, ns), F32), pltpu.VMEM((2, 1, ns), F32)],
        compiler_params=_cparams(("arbitrary",)),
    )(u, wb, a, wc, dskip, wglu, bglu, gnorm)


def _s5_bwd(u, y, don, xstart, wb, a, wc, dskip, wglu, bglu, gnorm, name):
    r = u.shape[0]
    t = S5_TILE
    nt = r // t
    ns = wb.shape[2]
    w = S5_WIDTH
    nt_dims = ((1,), (1,))
    tn_dims = ((0,), (0,))

    def body(u_ref, y_ref, don_ref, xs_ref, wb_hbm, a_ref, wc_hbm, d_ref, wg_ref, bg_ref, gn_ref,
             du_ref, da_ref, dd_ref, dbg_ref, dgn_ref, dwb_hbm, dwc_hbm, dwg_hbm,
             wb_ref, wc_ref, pw_ref, pwr_ref, lam_ref, acc_wb, acc_wc, acc_wg, sem):
        i = pl.program_id(0)
        ar, ai = a_ref[0], a_ref[1]

        @pl.when(i == 0)
        def _():
            c1 = pltpu.make_async_copy(wb_hbm, wb_ref, sem.at[0])
            c2 = pltpu.make_async_copy(wc_hbm, wc_ref, sem.at[1])
            c1.start()
            c2.start()
            pr, pi_ = _s5_power_table(ar, ai, t)
            pw_ref[0] = pr
            pw_ref[1] = pi_
            pr, pi_ = _s5_power_table(ar, -ai, t, reverse=True)
            pwr_ref[0] = pr
            pwr_ref[1] = pi_
            lam_ref[...] = jnp.zeros_like(lam_ref)
            acc_wb[...] = jnp.zeros_like(acc_wb)
            acc_wc[...] = jnp.zeros_like(acc_wc)
            acc_wg[...] = jnp.zeros_like(acc_wg)
            da_ref[...] = jnp.zeros_like(da_ref)
            dd_ref[...] = jnp.zeros_like(dd_ref)
            dbg_ref[...] = jnp.zeros_like(dbg_ref)
            dgn_ref[...] = jnp.zeros_like(dgn_ref)
            c1.wait()
            c2.wait()

        u_ = u_ref[...]
        y_ = y_ref[...]
        ub = u_.astype(BF16)
        h, th = _gelu(y_)
        hb = h.astype(BF16)
        wg = wg_ref[...]
        gate = _sigmoid(_dot(hb, wg) + bg_ref[...])
        d_out, dgn = _rms_bwd(h * gate, gn_ref[...], don_ref[...])
        dgn_ref[...] += dgn
        dhw = d_out * h * gate * (1.0 - gate)
        dhwb = dhw.astype(BF16)
        dh = d_out * gate + _dot(dhwb, wg, nt_dims)
        acc_wg[...] += _dot(hb, dhwb, tn_dims)
        dbg_ref[...] += jnp.sum(dhw, axis=0, keepdims=True)
        dgelu = 0.5 * (1.0 + th) + 0.5 * y_ * (1.0 - th * th) * _GELU_C * (1.0 + 3.0 * 0.044715 * y_ * y_)
        dy = dh * dgelu
        dd_ref[...] += jnp.sum(dy * u_, axis=0, keepdims=True)
        dyb = dy.astype(BF16)
        xr, xi = _s5_scan(_dot(ub, wb_ref[0]), _dot(ub, wb_ref[1]), ar, ai, t)
        cr, ci = xs_ref[0, 0:1, :], xs_ref[0, 1:2, :]
        pr, pi_ = pw_ref[0], pw_ref[1]
        xr = xr + pr * cr - pi_ * ci
        xi = xi + pr * ci + pi_ * cr
        acc_wc[0] += _dot(xr.astype(BF16), dyb, tn_dims)
        acc_wc[1] -= _dot(xi.astype(BF16), dyb, tn_dims)
        lr, li = _s5_scan(_dot(dyb, wc_ref[0], nt_dims), -_dot(dyb, wc_ref[1], nt_dims), ar, -ai, t, reverse=True)
        cr2, ci2 = lam_ref[0], lam_ref[1]
        pr, pi_ = pwr_ref[0], pwr_ref[1]
        lr = lr + pr * cr2 - pi_ * ci2
        li = li + pr * ci2 + pi_ * cr2
        lam_ref[0] = lr[0:1, :]
        lam_ref[1] = li[0:1, :]
        row = lax.broadcasted_iota(jnp.int32, (t, 1), 0)
        xpr = jnp.where(row == 0, cr, pltpu.roll(xr, 1, 0))
        xpi = jnp.where(row == 0, ci, pltpu.roll(xi, 1, 0))
        da_ref[0] += jnp.sum(lr * xpr + li * xpi, axis=0, keepdims=True)
        da_ref[1] += jnp.sum(li * xpr - lr * xpi, axis=0, keepdims=True)
        lrb, lib = lr.astype(BF16), li.astype(BF16)
        acc_wb[0] += _dot(ub, lrb, tn_dims)
        acc_wb[1] += _dot(ub, lib, tn_dims)
        du_ref[...] = d_ref[...] * dy + _dot(lrb, wb_ref[0], nt_dims) + _dot(lib, wb_ref[1], nt_dims)

        @pl.when(i == nt - 1)
        def _():
            cps = [pltpu.make_async_copy(acc_wb, dwb_hbm, sem.at[0]), pltpu.make_async_copy(acc_wc, dwc_hbm, sem.at[1]),
                   pltpu.make_async_copy(acc_wg, dwg_hbm, sem.at[2])]
            for c in cps:
                c.start()
            for c in cps:
                c.wait()

    rev = lambda i: (nt - 1 - i, 0)
    full = lambda shape: pl.BlockSpec(shape, lambda i: (0,) * len(shape))
    hbm = pl.BlockSpec(memory_space=pl.ANY)
    return _pallas(
        body, name=name, grid=(nt,),
        in_specs=[pl.BlockSpec((t, w), rev), pl.BlockSpec((t, w), rev), pl.BlockSpec((t, w), rev),
                  pl.BlockSpec((1, 2, ns), lambda i: (nt - 1 - i, 0, 0)), hbm, full((2, 1, ns)), hbm, full((1, w)),
                  full((w, w)), full((1, w)), full((1, w))],
        out_specs=[pl.BlockSpec((t, w), rev), full((2, 1, ns)), full((1, w)), full((1, w)), full((1, w)), hbm, hbm, hbm],
        out_shape=[jax.ShapeDtypeStruct((r, w), F32), jax.ShapeDtypeStruct((2, 1, ns), F32)]
        + [jax.ShapeDtypeStruct((1, w), F32)] * 3
        + [jax.ShapeDtypeStruct((2, w, ns), F32), jax.ShapeDtypeStruct((2, ns, w), F32), jax.ShapeDtypeStruct((w, w), F32)],
        scratch_shapes=[pltpu.VMEM((2, w, ns), BF16), pltpu.VMEM((2, ns, w), BF16),
                        pltpu.VMEM((2, t, ns), F32), pltpu.VMEM((2, t, ns), F32), pltpu.VMEM((2, 1, ns), F32),
                        pltpu.VMEM((2, w, ns), F32), pltpu.VMEM((2, ns, w), F32), pltpu.VMEM((w, w), F32),
                        pltpu.SemaphoreType.DMA((3,))],
        compiler_params=_cparams(("arbitrary",)),
    )(u, y, don, xstart, wb, a, wc, dskip, wglu, bglu, gnorm)


def _s5_expand(lam_re, lam_im, log_dt, b_re, b_im, c_re, c_im):
    g, n, p = S5_GROUPS, S5_STATE, S5_GROUP
    ns = g * n
    rows = lambda x: x.reshape(1, ns)
    logdt = jnp.repeat(log_dt.reshape(g), n).reshape(1, ns)
    btr = b_re.reshape(ns, p).T
    bti = b_im.reshape(ns, p).T
    ctr = c_re.transpose(0, 2, 1).reshape(ns, p)
    cti = c_im.transpose(0, 2, 1).reshape(ns, p)
    mask = (jnp.arange(g * p)[:, None] // p) == (jnp.arange(ns)[None, :] // n)
    return rows(lam_re), rows(lam_im), logdt, btr, bti, ctr, cti, mask


def _s5_block_diag_b(bb, mask):
    return jnp.where(mask, jnp.tile(bb, (S5_GROUPS, 1)), 0.0)


def _s5_block_diag_c(ct, mask):
    return jnp.where(mask.T, jnp.tile(ct, (1, S5_GROUPS)), 0.0)


def _s5_diag_of_b(dwb, mask):
    return jnp.where(mask, dwb, 0.0).reshape(S5_GROUPS, S5_GROUP, -1).sum(0)


def _s5_diag_of_c(dwc, mask):
    ns = dwc.shape[0]
    return jnp.where(mask.T, dwc, 0.0).reshape(ns, S5_GROUPS, S5_GROUP).sum(1)


DN_PRE_TILE = 256
_DN_QKV = 3 * DN_WIDTH


def _halo_specs(width, tile, nt, prev):
    per = tile // 8
    if prev:
        return pl.BlockSpec((8, width), lambda i: (jnp.maximum(i * per - 1, 0), 0))
    return pl.BlockSpec((8, width), lambda i: (jnp.minimum((i + 1) * per, nt * per - 1), 0))


def _shift_down(x, halo, s, t):
    xx = jnp.concatenate([halo, x], axis=0)
    return pltpu.roll(xx, s, 0)[8:]


def _shift_up(x, halo, s, t):
    xx = jnp.concatenate([x, halo], axis=0)
    return pltpu.roll(xx, t + 8 - s, 0)[:t]


def _silu(x):
    s = _sigmoid(x)
    return x * s, s


def _dn_gates(ab, alog, dtb, live):
    lane = lax.broadcasted_iota(jnp.int32, (1, 128), 1)
    g = -jnp.exp(alog) * _softplus(ab + dtb)
    beta = _sigmoid(ab)
    return jnp.where(live & (lane < DN_HEADS), g, jnp.where(live & (lane < 2 * DN_HEADS), beta, 0.0))


def _dn_pre_fwd(proj, ab, conv_w, alog, dtb, pad, name):
    r = proj.shape[0]
    t = DN_PRE_TILE
    nt = r // t
    scale = DN_HEAD_DIM ** -0.5

    def body(x_ref, halo_ref, ab_ref, w_ref, al_ref, dt_ref, co_ref, q_ref, k_ref, v_ref, gb_ref):
        i = pl.program_id(0)
        x = x_ref[...]
        halo = jnp.where(i > 0, halo_ref[...], 0.0)
        w = w_ref[...]
        co = w[3:4] * x
        for tap in range(DN_CONV - 1):
            co = co + w[tap:tap + 1] * _shift_down(x, halo, DN_CONV - 1 - tap, t)
        co_ref[...] = co
        act, _ = _silu(co)
        for hd in range(DN_HEADS):
            sl = slice(hd * 128, (hd + 1) * 128)
            for base, o_ref, sc in ((0, q_ref, scale), (DN_WIDTH, k_ref, 1.0)):
                xh = act[:, base + hd * 128: base + (hd + 1) * 128]
                o_ref[:, sl] = xh * (lax.rsqrt(jnp.sum(xh * xh, axis=-1, keepdims=True) + EPS) * sc)
        v_ref[...] = act[:, 2 * DN_WIDTH:]
        rows = i * t + lax.broadcasted_iota(jnp.int32, (t, 1), 0)
        gb_ref[...] = _dn_gates(ab_ref[...], al_ref[...], dt_ref[...], rows >= pad)

    return _pallas(
        body, name=name, grid=(nt,),
        in_specs=[pl.BlockSpec((t, _DN_QKV), lambda i: (i, 0)), _halo_specs(_DN_QKV, t, nt, True), _row_spec(128, t),
                  pl.BlockSpec((DN_CONV, _DN_QKV), lambda i: (0, 0)), _vec_spec(128), _vec_spec(128)],
        out_specs=[_row_spec(_DN_QKV, t), _row_spec(DN_WIDTH, t), _row_spec(DN_WIDTH, t), _row_spec(DN_WIDTH, t), _row_spec(128, t)],
        out_shape=[jax.ShapeDtypeStruct((r, _DN_QKV), F32)] + [jax.ShapeDtypeStruct((r, DN_WIDTH), F32)] * 3
        + [jax.ShapeDtypeStruct((r, 128), F32)],
        compiler_params=_cparams(("parallel",)),
    )(proj, proj, ab, conv_w, alog, dtb)


def _dn_pre_bwd(co, dq, dk, dv, dgb, ab, alog, dtb, pad, name):
    r = co.shape[0]
    t = DN_PRE_TILE
    nt = r // t
    scale = DN_HEAD_DIM ** -0.5

    def body(co_ref, dq_ref, dk_ref, dv_ref, dgb_ref, ab_ref, al_ref, dt_ref, dco_ref, dab_ref, dal_ref, ddt_ref):
        i = pl.program_id(0)

        @pl.when(i == 0)
        def _():
            dal_ref[...] = jnp.zeros_like(dal_ref)
            ddt_ref[...] = jnp.zeros_like(ddt_ref)

        co_ = co_ref[...]
        act, sg = _silu(co_)
        dsilu = sg * (1.0 + co_ * (1.0 - sg))
        for hd in range(DN_HEADS):
            sl = slice(hd * 128, (hd + 1) * 128)
            for base, d_ref, sc in ((0, dq_ref, scale), (DN_WIDTH, dk_ref, 1.0)):
                cs = slice(base + hd * 128, base + (hd + 1) * 128)
                xh = act[:, cs]
                rn = lax.rsqrt(jnp.sum(xh * xh, axis=-1, keepdims=True) + EPS)
                xhat = xh * rn
                dy = d_ref[:, sl]
                dx = (sc * rn) * (dy - xhat * jnp.sum(dy * xhat, axis=-1, keepdims=True))
                dco_ref[:, cs] = dx * dsilu[:, cs]
        dco_ref[:, 2 * DN_WIDTH:] = dv_ref[...] * dsilu[:, 2 * DN_WIDTH:]
        rows = i * t + lax.broadcasted_iota(jnp.int32, (t, 1), 0)
        live = rows >= pad
        lane = lax.broadcasted_iota(jnp.int32, (1, 128), 1)
        ab_ = ab_ref[...]
        dgb_ = dgb_ref[...]
        is_g = live & (lane < DN_HEADS)
        is_b = live & (lane >= DN_HEADS) & (lane < 2 * DN_HEADS)
        arg = ab_ + dt_ref[...]
        ea = jnp.exp(al_ref[...])
        da = jnp.where(is_g, -dgb_ * ea * _sigmoid(arg), 0.0)
        beta = _sigmoid(ab_)
        dab_ref[...] = da + jnp.where(is_b, dgb_ * beta * (1.0 - beta), 0.0)
        ddt_ref[...] += jnp.sum(da, axis=0, keepdims=True)
        dal_ref[...] += jnp.sum(jnp.where(is_g, -dgb_ * ea * _softplus(arg), 0.0), axis=0, keepdims=True)

    return _pallas(
        body, name=name, grid=(nt,),
        in_specs=[_row_spec(_DN_QKV, t), _row_spec(DN_WIDTH, t), _row_spec(DN_WIDTH, t), _row_spec(DN_WIDTH, t),
                  _row_spec(128, t), _row_spec(128, t), _vec_spec(128), _vec_spec(128)],
        out_specs=[_row_spec(_DN_QKV, t), _row_spec(128, t), _vec_spec(128), _vec_spec(128)],
        out_shape=[jax.ShapeDtypeStruct((r, _DN_QKV), F32), jax.ShapeDtypeStruct((r, 128), F32),
                   jax.ShapeDtypeStruct((1, 128), F32), jax.ShapeDtypeStruct((1, 128), F32)],
        compiler_params=_cparams(("arbitrary",)),
    )(co, dq, dk, dv, dgb, ab, alog, dtb)


def _dn_conv_bwd(dco, proj, conv_w, name):
    r = dco.shape[0]
    t = DN_PRE_TILE
    nt = r // t

    def body(d_ref, dh_ref, x_ref, xh_ref, w_ref, dx_ref, dw_ref):
        i = pl.program_id(0)

        @pl.when(i == 0)
        def _():
            dw_ref[...] = jnp.zeros_like(dw_ref)

        d = d_ref[...]
        dhalo = jnp.where(i < nt - 1, dh_ref[...], 0.0)
        x = x_ref[...]
        xhalo = jnp.where(i > 0, xh_ref[...], 0.0)
        w = w_ref[...]
        dx = w[3:4] * d
        dws = [None] * DN_CONV
        dws[3] = jnp.sum(d * x, axis=0, keepdims=True)
        for tap in range(DN_CONV - 1):
            s = DN_CONV - 1 - tap
            dx = dx + w[tap:tap + 1] * _shift_up(d, dhalo, s, t)
            dws[tap] = jnp.sum(d * _shift_down(x, xhalo, s, t), axis=0, keepdims=True)
        dx_ref[...] = dx
        dw_ref[...] += jnp.concatenate(dws + [jnp.zeros((8 - DN_CONV, _DN_QKV), F32)], axis=0)

    return _pallas(
        body, name=name, grid=(nt,),
        in_specs=[_row_spec(_DN_QKV, t), _halo_specs(_DN_QKV, t, nt, False),
                  pl.BlockSpec((t, _DN_QKV), lambda i: (i, 0)), _halo_specs(_DN_QKV, t, nt, True),
                  pl.BlockSpec((DN_CONV, _DN_QKV), lambda i: (0, 0))],
        out_specs=[_row_spec(_DN_QKV, t), pl.BlockSpec((8, _DN_QKV), lambda i: (0, 0))],
        out_shape=[jax.ShapeDtypeStruct((r, _DN_QKV), F32), jax.ShapeDtypeStruct((8, _DN_QKV), F32)],
        compiler_params=_cparams(("arbitrary",)),
    )(dco, dco, proj, proj, conv_w)


def _dn_inverse(n_mat):
    c = n_mat.shape[0]
    row = lax.broadcasted_iota(jnp.int32, (c, c), 0)
    col = lax.broadcasted_iota(jnp.int32, (c, c), 1)
    eye = (row == col).astype(F32)
    nd = jnp.where(row // DN_SUB == col // DN_SUB, n_mat, 0.0)
    no = n_mat - nd

    def geometric(b, order):
        x = eye + b
        p = b
        k = 2
        while k < order:
            p = _dot_hi(p, p)
            x = x + _dot_hi(x, p)
            k *= 2
        return x

    td = geometric(-nd, DN_SUB)
    x = geometric(-_dot_hi(td, no), c // DN_SUB)
    return _dot_hi(x, td)


def _dn_chunk_common(q_ref, k_ref, v_ref, gb_ref, gbt_ref, h):
    c = DN_CHUNK
    row = lax.broadcasted_iota(jnp.int32, (c, c), 0)
    col = lax.broadcasted_iota(jnp.int32, (c, c), 1)
    lane = lax.broadcasted_iota(jnp.int32, (1, 128), 1)
    q, k, v = q_ref[...], k_ref[...], v_ref[...]
    gbv = gb_ref[...]
    tri = (row >= col).astype(BF16)
    gam_all = _split_dot(tri, gbv)
    gam = jnp.sum(jnp.where(lane == h, gam_all, 0.0), axis=1, keepdims=True)
    beta = jnp.sum(jnp.where(lane == h + DN_HEADS, gbv, 0.0), axis=1, keepdims=True)
    g_row = jnp.broadcast_to(gbt_ref[pl.ds(h, 1), :], (8, c))
    hi = g_row.astype(BF16)
    lo = (g_row - hi.astype(F32)).astype(BF16)
    tri_t = (row <= col).astype(BF16)
    gam_row = (_dot(hi, tri_t) + _dot(lo, tri_t))[0:1]
    dec = jnp.where(row >= col, jnp.exp(jnp.minimum(gam - gam_row, 0.0)), 0.0)
    kb, qb = k.astype(BF16), q.astype(BF16)
    nt_dims = ((1,), (1,))
    kk = _dot(kb, kb, nt_dims)
    qk = _dot(qb, kb, nt_dims)
    eg = jnp.exp(gam)
    gam_l = gam[c - 1:c, :]
    return dict(q=q, k=k, v=v, qb=qb, kb=kb, gam=gam, beta=beta, dec=dec, kk=kk, qk=qk, eg=eg, gam_l=gam_l,
                row=row, col=col, lane=lane, att=qk * dec, qg=q * eg, kt=k * jnp.exp(gam_l - gam),
                rhs=jnp.concatenate([v * beta, k * (beta * eg)], axis=1))


def _dn_fwd(q, k, v, gb, gbt, name):
    r = q.shape[0]
    c = DN_CHUNK
    nc = r // c
    dh = DN_HEAD_DIM
    tn_dims = ((0,), (0,))

    def body(q_ref, k_ref, v_ref, gb_ref, gbt_ref, o_ref, ss_ref, ts_ref, s_ref):
        ci, h = pl.program_id(0), pl.program_id(1)

        @pl.when(ci == 0)
        def _():
            s_ref[h] = jnp.zeros((dh, dh), F32)

        z = _dn_chunk_common(q_ref, k_ref, v_ref, gb_ref, gbt_ref, h)
        n_mat = jnp.where(z["row"] > z["col"], z["beta"] * z["kk"] * z["dec"], 0.0)
        t_inv = _dn_inverse(n_mat)
        sol = _dot_hi(t_inv, z["rhs"])
        s = s_ref[h]
        sb = s.astype(BF16)
        v_new = sol[:, :dh] - _dot(sol[:, dh:].astype(BF16), sb)
        vnb = v_new.astype(BF16)
        o_ref[...] = _dot(z["qg"].astype(BF16), sb) + _dot(z["att"].astype(BF16), vnb)
        ss_ref[0, 0] = s
        ts_ref[0, 0] = t_inv
        s_ref[h] = s * jnp.exp(z["gam_l"]) + _dot(z["kt"].astype(BF16), vnb, tn_dims)

    blk = pl.BlockSpec((c, dh), lambda ci, h: (ci, h))
    sav = pl.BlockSpec((1, 1, dh, dh), lambda ci, h: (ci, h, 0, 0))
    return _pallas(
        body, name=name, grid=(nc, DN_HEADS),
        in_specs=[blk, blk, blk, pl.BlockSpec((c, 128), lambda ci, h: (ci, 0)), pl.BlockSpec((16, c), lambda ci, h: (0, ci))],
        out_specs=[blk, sav, sav],
        out_shape=[jax.ShapeDtypeStruct((r, DN_WIDTH), F32), jax.ShapeDtypeStruct((nc, DN_HEADS, dh, dh), F32),
                   jax.ShapeDtypeStruct((nc, DN_HEADS, dh, dh), F32)],
        scratch_shapes=[pltpu.VMEM((DN_HEADS, dh, dh), F32)],
        compiler_params=_cparams(("arbitrary", "arbitrary")),
    )(q, k, v, gb, gbt)


def _dn_bwd(q, k, v, gb, gbt, ssave, tsave, do, name):
    r = q.shape[0]
    c = DN_CHUNK
    nc = r // c
    dh = DN_HEAD_DIM
    nt_dims = ((1,), (1,))
    tn_dims = ((0,), (0,))

    def body(q_ref, k_ref, v_ref, gb_ref, gbt_ref, ss_ref, ts_ref, do_ref, dq_ref, dk_ref, dv_ref, dgb_ref, ds_ref):
        ci, h = pl.program_id(0), pl.program_id(1)

        @pl.when(ci == 0)
        def _():
            ds_ref[h] = jnp.zeros((dh, dh), F32)

        @pl.when(h == 0)
        def _():
            dgb_ref[...] = jnp.zeros_like(dgb_ref)

        z = _dn_chunk_common(q_ref, k_ref, v_ref, gb_ref, gbt_ref, h)
        row, col, lane = z["row"], z["col"], z["lane"]
        k_, v_, kb, qb = z["k"], z["v"], z["kb"], z["qb"]
        beta, eg, dec, kk, qk, gam, gam_l = z["beta"], z["eg"], z["dec"], z["kk"], z["qk"], z["gam"], z["gam_l"]
        qg, kt, att = z["qg"], z["kt"], z["att"]
        t_inv = ts_ref[0, 0]
        sol = _dot_hi(t_inv, z["rhs"])
        kcd = sol[:, dh:]
        s = ss_ref[0, 0]
        sb = s.astype(BF16)
        v_new = sol[:, :dh] - _dot(kcd.astype(BF16), sb)
        vnb = v_new.astype(BF16)
        ds_next = ds_ref[h]
        dsb = ds_next.astype(BF16)
        dob = do_ref[...].astype(BF16)
        rs = lambda x: jnp.sum(x, axis=1, keepdims=True)
        tot = lambda x: jnp.sum(rs(x), axis=0, keepdims=True)

        dqg = _dot(dob, sb, nt_dims)
        ds = _dot(qg.astype(BF16), dob, tn_dims)
        d_att = jnp.where(row >= col, _dot(dob, vnb, nt_dims), 0.0)
        dvn = _dot(att.astype(BF16), dob, tn_dims) + _dot(kt.astype(BF16), dsb)
        dkt = _dot(vnb, dsb, nt_dims)
        eg_l = jnp.exp(gam_l)
        ds = ds + ds_next * eg_l
        dgam_l = tot(ds_next * s) * eg_l
        dvnb = dvn.astype(BF16)
        dkcd = -_dot(dvnb, sb, nt_dims)
        ds = ds - _dot(kcd.astype(BF16), dvnb, tn_dims)
        dsol = jnp.concatenate([dvn, dkcd], axis=1)
        drhs = lax.dot_general(t_inv, dsol, (tn_dims, ((), ())), preferred_element_type=F32, precision=_HIGH)
        dn = jnp.where(row > col, -lax.dot_general(drhs, sol, (nt_dims, ((), ())), preferred_element_type=F32,
                                                   precision=_HIGH), 0.0)
        drv, drk = drhs[:, :dh], drhs[:, dh:]
        s_rkk = rs(drk * k_)
        dv_ref[...] = drv * beta
        dbeta = rs(drv * v_) + s_rkk * eg + rs(dn * kk * dec)
        dk = drk * (beta * eg)
        dgam = s_rkk * beta * eg
        dkk = (dn * beta * dec).astype(BF16)
        dd = dn * beta * kk + d_att * qk
        dqk = (d_att * dec).astype(BF16)
        dq_ref[...] = _dot(dqk, kb) + dqg * eg
        dk = dk + _dot(dqk, qb, tn_dims) + _dot(dkk, kb) + _dot(dkk, kb, tn_dims)
        w = dd * dec
        ones = jnp.ones((c, 128), F32)
        col_sum = lax.dot_general(w, ones, (tn_dims, ((), ())), preferred_element_type=F32, precision=_HIGH)[:, 0:1]
        dgam = dgam + rs(w) - col_sum + rs(dqg * qg) - rs(dkt * kt)
        dk_ref[...] = dk + dkt * jnp.exp(gam_l - gam)
        dgam_l = dgam_l + tot(dkt * kt)
        rowc = lax.broadcasted_iota(jnp.int32, (c, 1), 0)
        dgam = dgam + jnp.where(rowc == c - 1, dgam_l, 0.0)
        tri_u = (row <= col).astype(F32)
        dg = _dot_hi(tri_u, jnp.broadcast_to(dgam, (c, 128)))[:, 0:1]
        dgb_ref[...] += jnp.where(lane == h, dg, 0.0) + jnp.where(lane == h + DN_HEADS, dbeta, 0.0)
        ds_ref[h] = ds

    blk = pl.BlockSpec((c, dh), lambda ci, h: (nc - 1 - ci, h))
    sav = pl.BlockSpec((1, 1, dh, dh), lambda ci, h: (nc - 1 - ci, h, 0, 0))
    gspec = pl.BlockSpec((c, 128), lambda ci, h: (nc - 1 - ci, 0))
    return _pallas(
        body, name=name, grid=(nc, DN_HEADS),
        in_specs=[blk, blk, blk, gspec, pl.BlockSpec((16, c), lambda ci, h: (0, nc - 1 - ci)), sav, sav, blk],
        out_specs=[blk, blk, blk, gspec],
        out_shape=[jax.ShapeDtypeStruct((r, DN_WIDTH), F32)] * 3 + [jax.ShapeDtypeStruct((r, 128), F32)],
        scratch_shapes=[pltpu.VMEM((DN_HEADS, dh, dh), F32)],
        compiler_params=_cparams(("arbitrary", "arbitrary")),
    )(q, k, v, gb, gbt, ssave, tsave, do)


def _dn_post_fwd(o, proj, g, name):
    r = o.shape[0]

    def body(o_ref, z_ref, g_ref, y_ref):
        g_ = g_ref[...]
        for hd in range(DN_HEADS):
            sl = slice(hd * 128, (hd + 1) * 128)
            sz, _ = _silu(z_ref[:, sl])
            y_ref[:, sl] = (_rms(o_ref[:, sl], g_) * sz).astype(BF16)

    return _pallas(body, name=name, grid=(r // ROW_TILE,),
                   in_specs=[_row_spec(DN_WIDTH), pl.BlockSpec((ROW_TILE, DN_WIDTH), lambda i: (i, 3)), _vec_spec(128)],
                   out_specs=_row_spec(DN_WIDTH), out_shape=jax.ShapeDtypeStruct((r, DN_WIDTH), BF16),
                   compiler_params=_cparams(("parallel",)))(o, proj, g)


def _dn_post_bwd(o, proj, g, dy, name):
    r = o.shape[0]

    def body(o_ref, z_ref, g_ref, dy_ref, do_ref, dz_ref, dg_ref):
        @pl.when(pl.program_id(0) == 0)
        def _():
            dg_ref[...] = jnp.zeros_like(dg_ref)

        g_ = g_ref[...]
        for hd in range(DN_HEADS):
            sl = slice(hd * 128, (hd + 1) * 128)
            z_ = z_ref[:, sl]
            sz, sg = _silu(z_)
            dy_ = dy_ref[:, sl]
            o_ = o_ref[:, sl]
            dz_ref[:, sl] = dy_ * _rms(o_, g_) * (sg * (1.0 + z_ * (1.0 - sg)))
            dx, dg = _rms_bwd(o_, g_, dy_ * sz)
            do_ref[:, sl] = dx
            dg_ref[...] += dg

    return _pallas(body, name=name, grid=(r // ROW_TILE,),
                   in_specs=[_row_spec(DN_WIDTH), pl.BlockSpec((ROW_TILE, DN_WIDTH), lambda i: (i, 3)), _vec_spec(128),
                             _row_spec(DN_WIDTH)],
                   out_specs=[_row_spec(DN_WIDTH), _row_spec(DN_WIDTH), _vec_spec(128)],
                   out_shape=[jax.ShapeDtypeStruct((r, DN_WIDTH), F32)] * 2 + [jax.ShapeDtypeStruct((1, 128), F32)],
                   compiler_params=_cparams(("arbitrary",)))(o, proj, g, dy)


def _exchange(arrays, scatter, name):
    n = len(arrays)
    outs_shape = [jax.ShapeDtypeStruct((N_DEV,) + (a.shape[1:] if sc else a.shape), a.dtype) for a, sc in zip(arrays, scatter)]

    def body(*refs):
        in_refs, out_refs = refs[:n], refs[n:2 * n]
        send_sems, recv_sems, local_sems = refs[2 * n:]
        mx, my, mc = lax.axis_index("x"), lax.axis_index("y"), lax.axis_index("c")
        me = 4 * mx + 2 * my + mc
        started = []
        for a in range(n):
            src_own = in_refs[a].at[me] if scatter[a] else in_refs[a]
            loc = pltpu.make_async_copy(src_own, out_refs[a].at[me], local_sems.at[a])
            loc.start()
            started.append(loc)
        remote = []
        for a in range(n):
            for kbits in range(1, N_DEV):
                px = lax.rem(mx + ((kbits >> 2) & 1), 2)
                py = lax.rem(my + ((kbits >> 1) & 1), 2)
                pc = lax.rem(mc + (kbits & 1), 2)
                src = in_refs[a].at[4 * px + 2 * py + pc] if scatter[a] else in_refs[a]
                cp = pltpu.make_async_remote_copy(
                    src_ref=src, dst_ref=out_refs[a].at[me],
                    send_sem=send_sems.at[a * N_DEV + kbits], recv_sem=recv_sems.at[a * N_DEV + kbits],
                    device_id=(px, py, pc), device_id_type=pl.DeviceIdType.MESH)
                cp.start()
                remote.append(cp)
        for cp in remote:
            cp.wait()
        for loc in started:
            loc.wait()

    hbm = pl.BlockSpec(memory_space=pl.ANY)
    return _pallas(
        body, name=name, in_specs=[hbm] * n, out_specs=[hbm] * n, out_shape=outs_shape,
        scratch_shapes=[pltpu.SemaphoreType.DMA((n * N_DEV,)), pltpu.SemaphoreType.DMA((n * N_DEV,)),
                        pltpu.SemaphoreType.DMA((n,))],
    )(*arrays)


def _adamw(gstack, w, m, v, name):
    a, b = w.shape
    ta = a
    for t in (1024, 512, 256, 128, 64, 32, 16, 8):
        if a % t == 0 and N_DEV * t * b * 4 <= 4 * 1024 * 1024:
            ta = t
            break
    c1 = 1.0 / (1.0 - ADAM_B1 ** ADAM_STEP)
    c2 = 1.0 / (1.0 - ADAM_B2 ** ADAM_STEP)

    def body(g_ref, w_ref, m_ref, v_ref, og_ref, od_ref, om_ref, ov_ref):
        g = g_ref[0]
        for s in range(1, N_DEV):
            g = g + g_ref[s]
        m_new = ADAM_B1 * m_ref[...] + (1.0 - ADAM_B1) * g
        v_new = ADAM_B2 * v_ref[...] + (1.0 - ADAM_B2) * (g * g)
        og_ref[...] = g
        om_ref[...] = m_new
        ov_ref[...] = v_new
        od_ref[...] = -ADAM_LR * ((m_new * c1) / (jnp.sqrt(v_new * c2) + ADAM_EPS) + ADAM_WD * w_ref[...])

    spec = pl.BlockSpec((ta, b), lambda i: (i, 0))
    return _pallas(
        body, name=name, grid=(a // ta,),
        in_specs=[pl.BlockSpec((N_DEV, ta, b), lambda i: (0, i, 0)), spec, spec, spec],
        out_specs=[spec] * 4, out_shape=[jax.ShapeDtypeStruct((a, b), F32)] * 4,
        compiler_params=_cparams(("parallel",)),
    )(gstack, w, m, v)


_WEIGHTS = ['meta_tokens', 'pre_mix_norm', 'post_mix_norm', 'pre_mlp_norm', 'post_mlp_norm', 'mlp_w1', 'mlp_w2',
            'w_in_even', 'w_out_even', 'sb_out_norm', 's5_lambda_re', 's5_lambda_im', 's5_log_dt', 's5_b_re', 's5_b_im',
            's5_c_re', 's5_c_im', 's5_d', 's5_w_glu', 's5_b_glu', 's5_out_norm', 'w_in_odd', 'dn_conv_w', 'dn_a_log',
            'dn_dt_bias', 'dn_out_norm', 'w_out_odd']
_SHARDED = ['meta_tokens', 'mlp_w1', 'mlp_w2', 'w_in_even', 'w_out_even', 's5_w_glu', 'w_in_odd', 'dn_conv_w', 'w_out_odd']
_SMALL = [n for n in _WEIGHTS if n not in _SHARDED]


def _view2d(name, a):
    return a.reshape(-1, a.shape[-1])


def _unshard(name, g):
    if name == 'mlp_w1':
        return g.reshape(N_DEV, 2, D_MODEL, -1).transpose(1, 2, 0, 3).reshape(2, D_MODEL, D_FF)
    if name == 'mlp_w2':
        return g.reshape(N_DEV, 2, -1, D_MODEL).transpose(1, 0, 2, 3).reshape(2, D_FF, D_MODEL)
    if name in ('w_in_even', 'w_in_odd', 'dn_conv_w', 'meta_tokens'):
        return g.transpose(1, 0, 2).reshape(g.shape[1], -1)
    return g.reshape(-1, g.shape[-1])


def _to_blocks(name, full):
    if name == 'mlp_w1':
        return full.reshape(2, D_MODEL, N_DEV, -1).transpose(2, 0, 1, 3).reshape(N_DEV, 2 * D_MODEL, -1)
    if name == 'mlp_w2':
        return full.reshape(2, N_DEV, -1, D_MODEL).transpose(1, 0, 2, 3).reshape(N_DEV, -1, D_MODEL)
    if name in ('w_in_even', 'w_in_odd', 'dn_conv_w', 'meta_tokens'):
        return full.reshape(full.shape[0], N_DEV, -1).transpose(1, 0, 2)
    return full.reshape(N_DEV, -1, full.shape[-1])


def _pack(parts):
    rows = []
    for p in parts:
        flat = p.reshape(-1)
        rows.append(jnp.pad(flat, (0, (-flat.shape[0]) % 128)).reshape(-1, 128))
    return jnp.concatenate(rows, axis=0)


def _unpack(packed, like):
    out, at = [], 0
    for p in like:
        n = math.prod(p.shape)
        nrow = -(-n // 128)
        out.append(packed[at:at + nrow].reshape(-1)[:n].reshape(p.shape))
        at += nrow
    return out


def _lane_vec(x, width=128):
    flat = x.reshape(-1)
    return jnp.pad(flat, (0, width - flat.shape[0])).reshape(1, width)


def kernel(x, meta_tokens, pre_mix_norm, post_mix_norm, pre_mlp_norm, post_mlp_norm, mlp_w1, mlp_w2, w_in_even, w_out_even, sb_out_norm, s5_lambda_re, s5_lambda_im, s5_log_dt, s5_b_re, s5_b_im, s5_c_re, s5_c_im, s5_d, s5_w_glu, s5_b_glu, s5_out_norm, w_in_odd, dn_conv_w, dn_a_log, dn_dt_bias, dn_out_norm, w_out_odd, loss_target, m_meta_tokens, m_pre_mix_norm, m_post_mix_norm, m_pre_mlp_norm, m_post_mlp_norm, m_mlp_w1, m_mlp_w2, m_w_in_even, m_w_out_even, m_sb_out_norm, m_s5_lambda_re, m_s5_lambda_im, m_s5_log_dt, m_s5_b_re, m_s5_b_im, m_s5_c_re, m_s5_c_im, m_s5_d, m_s5_w_glu, m_s5_b_glu, m_s5_out_norm, m_w_in_odd, m_dn_conv_w, m_dn_a_log, m_dn_dt_bias, m_dn_out_norm, m_w_out_odd, v_meta_tokens, v_pre_mix_norm, v_post_mix_norm, v_pre_mlp_norm, v_post_mlp_norm, v_mlp_w1, v_mlp_w2, v_w_in_even, v_w_out_even, v_sb_out_norm, v_s5_lambda_re, v_s5_lambda_im, v_s5_log_dt, v_s5_b_re, v_s5_b_im, v_s5_c_re, v_s5_c_im, v_s5_d, v_s5_w_glu, v_s5_b_glu, v_s5_out_norm, v_w_in_odd, v_dn_conv_w, v_dn_a_log, v_dn_dt_bias, v_dn_out_norm, v_w_out_odd):
    given = dict(locals())
    w = {n: given[n] for n in _WEIGHTS}
    mom_m = {n: given["m_" + n] for n in _WEIGHTS}
    mom_v = {n: given["v_" + n] for n in _WEIGHTS}

    seq = x.shape[1]
    assert x.shape[0] == 1 and seq % ROW_TILE == 0
    r = seq + ROW_TILE
    pad = ROW_TILE - N_META
    pad_tiles = 1

    wire = {n: (F32 if n in ('dn_conv_w', 'meta_tokens') else BF16) for n in _SHARDED}
    gathered = _exchange([_view2d(n, w[n]).astype(wire[n]) for n in _SHARDED], [False] * len(_SHARDED), "gather_weights")
    full = {n: _unshard(n, g) for n, g in zip(_SHARDED, gathered)}
    w1, w2 = full['mlp_w1'], full['mlp_w2']
    w_ie, w_oe, w_glu, w_oo = full['w_in_even'], full['w_out_even'], full['s5_w_glu'], full['w_out_odd']
    w_io = full['w_in_odd'][:, :4 * DN_WIDTH]
    w_ab = jnp.pad(full['w_in_odd'][:, 4 * DN_WIDTH:], ((0, 0), (0, 128 - 2 * DN_HEADS)))
    conv_w = full['dn_conv_w']
    row = lambda v_: v_.reshape(1, -1)

    hs0 = jnp.concatenate([jnp.zeros((pad, D_MODEL), F32), full['meta_tokens'], x[0]], axis=0)
    hn0 = _norm_pre(hs0, row(pre_mix_norm[0]), "pre_mix_0")
    qkv = _mm_fwd(hn0, w_ie[:, :3 * SB_WIDTH], "in_even_qkv", out_dtypes=(BF16,))
    u = _mm_fwd(hn0, w_ie[:, 3 * SB_WIDTH:], "in_even_u")
    q, k, v = qkv[:, :SB_WIDTH], qkv[:, SB_WIDTH:2 * SB_WIDTH], qkv[:, 2 * SB_WIDTH:]
    nb = r // ATT_BLK
    blocks_t = lambda t_: t_.reshape(nb, ATT_BLK, 4, 128).transpose(2, 0, 3, 1)
    o_sb, ssave = _sb_fwd(q, k, blocks_t(v), pad, "sb_fwd")
    on_sb = _norm_pre(o_sb, row(sb_out_norm[0]), "sb_out_norm")

    lam_re, lam_im, logdt, btr, bti, ctr, cti, s5_mask = _s5_expand(
        s5_lambda_re[0], s5_lambda_im[0], s5_log_dt[0], s5_b_re[0], s5_b_im[0], s5_c_re[0], s5_c_im[0])
    a_re, a_im, bbr, bbi = _s5_prep(lam_re, lam_im, logdt, btr, bti, "s5_prep")
    s5_wb = jnp.stack([_s5_block_diag_b(bbr, s5_mask), _s5_block_diag_b(bbi, s5_mask)]).astype(BF16)
    s5_wc = jnp.stack([_s5_block_diag_c(ctr, s5_mask), _s5_block_diag_c(cti, s5_mask)]).astype(BF16)
    s5_a = jnp.stack([a_re, a_im])
    s5_args = (s5_wb, s5_a, s5_wc, row(s5_d[0]), w_glu, row(s5_b_glu[0]), row(s5_out_norm[0]))
    y_s5, on_s5, xstart = _s5_fwd(u, *s5_args, "s5_fwd")

    merged = jnp.concatenate([on_sb, on_s5], axis=1)
    mix0 = _mm_fwd(merged, w_oe, "out_even")
    hs1, hn1 = _norm_post_pre(hs0, mix0, row(post_mix_norm[0]), row(pre_mlp_norm[0]), "post_mix_0")
    relu2 = lambda acc: (jnp.square(jnp.maximum(acc, 0.0)), jnp.maximum(acc, 0.0))
    r0, ra0 = _mm_fwd(hn1, w1[0], "mlp_up_0", out_dtypes=(BF16, BF16), epilogue=relu2)
    m0 = _mm_fwd(r0, w2[0], "mlp_down_0")
    hs2, hn2 = _norm_post_pre(hs1, m0, row(post_mlp_norm[0]), row(pre_mix_norm[1]), "post_mlp_0")

    proj = _mm_fwd(hn2, w_io, "in_odd")
    ab = _mm_fwd(hn2, w_ab, "in_odd_gates")
    alog, dtb = _lane_vec(dn_a_log[0]), _lane_vec(dn_dt_bias[0])
    co, qd, kd, vd, gb = _dn_pre_fwd(proj, ab, conv_w, alog, dtb, pad, "dn_pre")
    gbt = gb[:, :2 * DN_HEADS].T
    o_dn, s_dn, t_dn = _dn_fwd(qd, kd, vd, gb, gbt, "dn_fwd")
    on_dn = _dn_post_fwd(o_dn, proj, row(dn_out_norm[0]), "dn_post")
    mix1 = _mm_fwd(on_dn, w_oo, "out_odd")
    hs3, hn3 = _norm_post_pre(hs2, mix1, row(post_mix_norm[1]), row(pre_mlp_norm[1]), "post_mix_1")
    r1, ra1 = _mm_fwd(hn3, w1[1], "mlp_up_1", out_dtypes=(BF16, BF16), epilogue=relu2)
    m1 = _mm_fwd(r1, w2[1], "mlp_down_1")
    dhs, loss_part = _norm_post_loss(hs3, m1, row(post_mlp_norm[1]), loss_target[0], pad_tiles, "post_mlp_1_loss")
    loss = lax.psum(loss_part, ("x", "y", "c"))

    g = {}
    drelu2 = lambda acc, ra: (acc * (2.0 * ra.astype(F32)),)

    def mlp_bwd(layer, hn, rr, ra, dm):
        dw2 = _mm_wgrad(rr, dm, f"mlp_down_{layer}_wgrad")
        da = _mm_dgrad(dm, w2[layer], f"mlp_down_{layer}_dgrad", out_dtypes=(BF16,), extras=(ra,), epilogue=drelu2)
        dw1 = _mm_wgrad(hn, da, f"mlp_up_{layer}_wgrad")
        return dw1, dw2, _mm_dgrad(da, w1[layer], f"mlp_up_{layer}_dgrad")

    _, dm1, _, dg_post_mlp1 = _norm_bwd(dhs, post=(m1, row(post_mlp_norm[1])), pad=pad, name="post_mlp_1_bwd")
    dw1_1, dw2_1, dhn3 = mlp_bwd(1, hn3, r1, ra1, dm1)
    dhs, dmix1, dg_pre_mlp1, dg_post_mix1 = _norm_bwd(
        dhs, pre=(hs3, row(pre_mlp_norm[1]), dhn3), post=(mix1, row(post_mix_norm[1])), pad=pad, name="post_mix_1_bwd")

    g['w_out_odd'] = _mm_wgrad(on_dn, dmix1, "out_odd_wgrad")
    d_on_dn = _mm_dgrad(dmix1, w_oo, "out_odd_dgrad")
    do_dn, dz, dg_dn = _dn_post_bwd(o_dn, proj, row(dn_out_norm[0]), d_on_dn, "dn_post_bwd")
    dqd, dkd, dvd, dgb = _dn_bwd(qd, kd, vd, gb, gbt, s_dn, t_dn, do_dn, "dn_bwd")
    dco, dab, d_alog, d_dtb = _dn_pre_bwd(co, dqd, dkd, dvd, dgb, ab, alog, dtb, pad, "dn_pre_bwd")
    dpre, d_conv = _dn_conv_bwd(dco, proj, conv_w, "dn_conv_bwd")
    dproj = jnp.concatenate([dpre, dz], axis=1)
    g['w_in_odd'] = jnp.concatenate([_mm_wgrad(hn2, dproj, "in_odd_wgrad"),
                                     _mm_wgrad(hn2, dab, "in_odd_gates_wgrad")[:, :2 * DN_HEADS]], axis=1)
    dhn2 = _mm_dgrad(dab, w_ab, "in_odd_gates_dgrad")
    dhn2 = _mm_dgrad(dproj, w_io, "in_odd_dgrad", extras=(dhn2,), epilogue=lambda acc, other: (acc + other,))
    g['dn_conv_w'] = d_conv[:DN_CONV]
    g['dn_a_log'], g['dn_dt_bias'], g['dn_out_norm'] = d_alog[0, :DN_HEADS], d_dtb[0, :DN_HEADS], dg_dn[0]

    dhs, dm0, dg_pre_mix1, dg_post_mlp0 = _norm_bwd(
        dhs, pre=(hs2, row(pre_mix_norm[1]), dhn2), post=(m0, row(post_mlp_norm[0])), pad=pad, name="post_mlp_0_bwd")
    dw1_0, dw2_0, dhn1 = mlp_bwd(0, hn1, r0, ra0, dm0)
    dhs, dmix0, dg_pre_mlp0, dg_post_mix0 = _norm_bwd(
        dhs, pre=(hs1, row(pre_mlp_norm[0]), dhn1), post=(mix0, row(post_mix_norm[0])), pad=pad, name="post_mix_0_bwd")

    g['w_out_even'] = _mm_wgrad(merged, dmix0, "out_even_wgrad")
    dmerged = _mm_dgrad(dmix0, w_oe, "out_even_dgrad")
    _, do_sb, _, dg_sb = _norm_bwd(dmerged[:, :SB_WIDTH], post=(o_sb, row(sb_out_norm[0])), pad=pad, name="sb_out_norm_bwd")
    dq, dk4, dv4 = _sb_bwd(q, k, v, blocks_t(k), ssave, do_sb, pad, "sb_bwd")
    unheads = lambda t_: t_.transpose(1, 0, 2).reshape(r, SB_WIDTH)
    du, d_a, d_d, d_bglu, dg_s5, d_wb, d_wc, g['s5_w_glu'] = _s5_bwd(u, y_s5, dmerged[:, SB_WIDTH:], xstart, *s5_args, "s5_bwd")
    g_lr, g_li, g_dt, g_btr, g_bti = _s5_prep_bwd(
        lam_re, lam_im, logdt, btr, bti, d_a[0], d_a[1],
        _s5_diag_of_b(d_wb[0], s5_mask), _s5_diag_of_b(d_wb[1], s5_mask), "s5_prep_bwd")
    gg, nn, pp = S5_GROUPS, S5_STATE, S5_GROUP
    g['s5_lambda_re'], g['s5_lambda_im'] = g_lr.reshape(gg, nn), g_li.reshape(gg, nn)
    g['s5_log_dt'] = g_dt.reshape(gg, nn)[:, 0]
    g['s5_b_re'], g['s5_b_im'] = g_btr.T.reshape(gg, nn, pp), g_bti.T.reshape(gg, nn, pp)
    g['s5_c_re'] = _s5_diag_of_c(d_wc[0], s5_mask).reshape(gg, nn, pp).transpose(0, 2, 1)
    g['s5_c_im'] = _s5_diag_of_c(d_wc[1], s5_mask).reshape(gg, nn, pp).transpose(0, 2, 1)
    g['s5_d'], g['s5_b_glu'], g['s5_out_norm'], g['sb_out_norm'] = d_d[0], d_bglu[0], dg_s5[0], dg_sb[0]
    dqkvu = jnp.concatenate([dq, unheads(dk4), unheads(dv4), du], axis=1)
    g['w_in_even'] = _mm_wgrad(hn0, dqkvu, "in_even_wgrad")
    dhn0 = _mm_dgrad(dqkvu, w_ie, "in_even_dgrad")
    dhs, _, dg_pre_mix0, _ = _norm_bwd(dhs, pre=(hs0, row(pre_mix_norm[0]), dhn0), pad=pad, name="pre_mix_0_bwd")

    g['mlp_w1'] = jnp.stack([dw1_0, dw1_1])
    g['mlp_w2'] = jnp.stack([dw2_0, dw2_1])
    g['meta_tokens'] = dhs[pad:pad + N_META]
    g['pre_mix_norm'] = jnp.concatenate([dg_pre_mix0, dg_pre_mix1], axis=0)
    g['post_mix_norm'] = jnp.concatenate([dg_post_mix0, dg_post_mix1], axis=0)
    g['pre_mlp_norm'] = jnp.concatenate([dg_pre_mlp0, dg_pre_mlp1], axis=0)
    g['post_mlp_norm'] = jnp.concatenate([dg_post_mlp0, dg_post_mlp1], axis=0)
    grad_x = dhs[pad + N_META:][None]

    small_like = [w[n] for n in _SMALL]
    partial = [_to_blocks(n, g[n].reshape(full[n].shape)) for n in _SHARDED]
    partial.append(_pack([g[n].reshape(w[n].shape) for n in _SMALL]))
    stacks = _exchange(partial, [True] * len(_SHARDED) + [False], "reduce_gradients")
    grads, deltas, new_m, new_v = {}, {}, {}, {}
    for n, st in zip(_SHARDED, stacks):
        outs = _adamw(st, _view2d(n, w[n]), _view2d(n, mom_m[n]), _view2d(n, mom_v[n]), f"adamw_{n}")
        grads[n], deltas[n], new_m[n], new_v[n] = (o.reshape(w[n].shape) for o in outs)
    outs = _adamw(stacks[-1], _pack(small_like), _pack([mom_m[n] for n in _SMALL]), _pack([mom_v[n] for n in _SMALL]),
                  "adamw_small")
    for dst, o in zip((grads, deltas, new_m, new_v), outs):
        for n, part in zip(_SMALL, _unpack(o, small_like)):
            dst[n] = part
    return (loss, grad_x, *[grads[n] for n in _WEIGHTS], *[deltas[n] for n in _WEIGHTS],
            *[new_m[n] for n in _WEIGHTS], *[new_v[n] for n in _WEIGHTS])
```

```python
import functools
import math

import jax
import jax.numpy as jnp
from jax import lax
from jax.experimental import pallas as pl
from jax.experimental.pallas import tpu as pltpu

F32 = jnp.float32
BF16 = jnp.bfloat16

D_MODEL = 1024
N_META = 16
SB_HEAD_DIM = 64
SB_WIDTH = 512
S5_WIDTH = 512
S5_GROUP = 16
S5_GROUPS = 32
S5_STATE = 64
S5_NS = S5_GROUPS * S5_STATE
DN_HEAD_DIM = 128
DN_HEADS = 8
DN_WIDTH = 1024
DN_CONV = 4
D_FF = 4096
EPS = 1e-6
N_DEV = 8

ADAM_LR = 0.001
ADAM_B1 = 0.9
ADAM_B2 = 0.999
ADAM_EPS = 1e-08
ADAM_WD = 0.01
ADAM_STEP = 10

ROW_TILE = 512
ATT_BLK = 256
SB_BLOCKS_PER_TRIP = 3
SB_FWD_SKEW = False
SB_BWD_SKEW = True
DN_CHUNK = 128
DN_SUB = 16
S5_TILE = 128
VMEM_LIMIT = 56 * 1024 * 1024

_HIGH = lax.Precision.HIGHEST


def _pallas(body, **kw):
    return pl.pallas_call(body, **kw)


def _cparams(sem):
    return pltpu.CompilerParams(dimension_semantics=sem, vmem_limit_bytes=VMEM_LIMIT)


def _dot(a, b, dims=((1,), (0,))):
    return lax.dot_general(a, b, (dims, ((), ())), preferred_element_type=F32)


def _dot_hi(a, b):
    return lax.dot_general(a, b, (((1,), (0,)), ((), ())), preferred_element_type=F32, precision=_HIGH)


def _split_dot(m_bf16, x):
    hi = x.astype(BF16)
    lo = (x - hi.astype(F32)).astype(BF16)
    return _dot(m_bf16, hi) + _dot(m_bf16, lo)


def _matmul(a, b, *, ta=False, tb=False, tm, tn, tk, name, out_dtypes=(F32,), extras=(), epilogue=None):
    m, k = (a.shape[1], a.shape[0]) if ta else a.shape
    n = b.shape[0] if tb else b.shape[1]
    assert (b.shape[1] if tb else b.shape[0]) == k
    assert m % tm == 0 and n % tn == 0 and k % tk == 0, (name, m, n, k, tm, tn, tk)
    nk = k // tk
    n_ex = len(extras)
    n_out = len(out_dtypes)
    dims = ((0 if ta else 1,), (1 if tb else 0,))

    def body(*refs):
        a_ref, b_ref = refs[0], refs[1]
        ex_refs = refs[2:2 + n_ex]
        o_refs = refs[2 + n_ex:2 + n_ex + n_out]
        acc_ref = refs[-1]
        kk = pl.program_id(2)

        @pl.when(kk == 0)
        def _():
            acc_ref[...] = jnp.zeros_like(acc_ref)

        acc_ref[...] += _dot(a_ref[...].astype(BF16), b_ref[...].astype(BF16), dims)

        @pl.when(kk == nk - 1)
        def _():
            acc = acc_ref[...]
            outs = (acc,) if epilogue is None else epilogue(acc, *[r[...] for r in ex_refs])
            for o_ref, o in zip(o_refs, outs):
                o_ref[...] = o.astype(o_ref.dtype)

    a_spec = pl.BlockSpec((tk, tm), lambda i, j, kk: (kk, i)) if ta else pl.BlockSpec((tm, tk), lambda i, j, kk: (i, kk))
    b_spec = pl.BlockSpec((tn, tk), lambda i, j, kk: (j, kk)) if tb else pl.BlockSpec((tk, tn), lambda i, j, kk: (kk, j))
    o_spec = pl.BlockSpec((tm, tn), lambda i, j, kk: (i, j))
    outs = _pallas(
        body, name=name,
        grid=(m // tm, n // tn, nk),
        in_specs=[a_spec, b_spec] + [o_spec] * n_ex,
        out_specs=[o_spec] * n_out,
        out_shape=[jax.ShapeDtypeStruct((m, n), dt) for dt in out_dtypes],
        scratch_shapes=[pltpu.VMEM((tm, tn), F32)],
        compiler_params=_cparams(("parallel", "parallel", "arbitrary")),
    )(a, b, *extras)
    return outs[0] if n_out == 1 else outs


def _tile(n, cap):
    best = 128
    for t in range(128, min(n, cap) + 1, 128):
        if n % t == 0:
            best = t
    assert n % best == 0, n
    return best


def _mm_fwd(x, w, name, **kw):
    k, n = w.shape
    return _matmul(x, w, tm=ROW_TILE, tn=_tile(n, 1024), tk=_tile(k, 1024), name=name, **kw)


def _mm_dgrad(dy, w, name, **kw):
    k, n = w.shape
    return _matmul(dy, w, tb=True, tm=ROW_TILE, tn=_tile(k, 1024), tk=_tile(n, 1024), name=name, **kw)


def _mm_wgrad(x, dy, name):
    k, n = x.shape[1], dy.shape[1]
    return _matmul(x, dy, ta=True, tm=_tile(k, 512), tn=_tile(n, 1024), tk=ROW_TILE, name=name)


def _rms(x, g):
    r = lax.rsqrt(jnp.mean(x * x, axis=-1, keepdims=True) + EPS)
    return x * r * g


def _rms_bwd(x, g, dy):
    r = lax.rsqrt(jnp.mean(x * x, axis=-1, keepdims=True) + EPS)
    xh = x * r
    dxh = dy * g
    dx = r * (dxh - xh * jnp.mean(dxh * xh, axis=-1, keepdims=True))
    dg = jnp.sum(dy * xh, axis=0, keepdims=True)
    return dx, dg


def _row_spec(width, tile=ROW_TILE):
    return pl.BlockSpec((tile, width), lambda i: (i, 0))


def _vec_spec(width):
    return pl.BlockSpec((1, width), lambda i: (0, 0))


def _norm_pre(hs, g, name):
    r, d = hs.shape

    def body(x_ref, g_ref, o_ref):
        o_ref[...] = _rms(x_ref[...], g_ref[...]).astype(BF16)

    return _pallas(body, name=name, grid=(r // ROW_TILE,), in_specs=[_row_spec(d), _vec_spec(d)],
                   out_specs=_row_spec(d), out_shape=jax.ShapeDtypeStruct((r, d), BF16),
                   compiler_params=_cparams(("parallel",)))(hs, g)


def _norm_post_pre(hs, m, g_post, g_pre, name):
    r, d = hs.shape

    def body(hs_ref, m_ref, gp_ref, gn_ref, o_ref, hn_ref):
        new = hs_ref[...] + _rms(m_ref[...], gp_ref[...])
        o_ref[...] = new
        hn_ref[...] = _rms(new, gn_ref[...]).astype(BF16)

    return _pallas(body, name=name, grid=(r // ROW_TILE,),
                   in_specs=[_row_spec(d), _row_spec(d), _vec_spec(d), _vec_spec(d)],
                   out_specs=[_row_spec(d), _row_spec(d)],
                   out_shape=[jax.ShapeDtypeStruct((r, d), F32), jax.ShapeDtypeStruct((r, d), BF16)],
                   compiler_params=_cparams(("parallel",)))(hs, m, g_post, g_pre)


def _norm_post_loss(hs, m, g_post, target, pad_tiles, name):
    r, d = hs.shape
    nt = r // ROW_TILE

    def body(hs_ref, m_ref, gp_ref, t_ref, dhs_ref, loss_ref):
        i = pl.program_id(0)
        new = hs_ref[...] + _rms(m_ref[...], gp_ref[...])
        live = (i >= pad_tiles).astype(F32)
        diff = (new - t_ref[...]) * live
        dhs_ref[...] = diff * (1.0 / d)
        loss_ref[...] = jnp.full((8, 128), 0.5 / d * jnp.sum(diff * diff), F32)

    dhs, parts = _pallas(
        body, name=name, grid=(nt,),
        in_specs=[_row_spec(d), _row_spec(d), _vec_spec(d),
                  pl.BlockSpec((ROW_TILE, d), lambda i: (jnp.maximum(i - pad_tiles, 0), 0))],
        out_specs=[_row_spec(d), pl.BlockSpec((8, 128), lambda i: (i, 0))],
        out_shape=[jax.ShapeDtypeStruct((r, d), F32), jax.ShapeDtypeStruct((nt * 8, 128), F32)],
        compiler_params=_cparams(("parallel",)))(hs, m, g_post, target)
    return dhs, jnp.sum(parts[::8, 0])


def _norm_bwd(dhs, *, pre=None, post=None, pad=0, name):
    r, d = dhs.shape
    has_pre, has_post = pre is not None, post is not None

    def body(*refs):
        it = iter(refs)
        dhs_ref = next(it)
        if has_pre:
            hs_ref, gn_ref, dhn_ref = next(it), next(it), next(it)
        if has_post:
            m_ref, gp_ref = next(it), next(it)
        if has_pre:
            o_dhs, o_dgn = next(it), next(it)
        if has_post:
            o_dm, o_dgp = next(it), next(it)
        i = pl.program_id(0)
        live = (i * ROW_TILE + lax.broadcasted_iota(jnp.int32, (ROW_TILE, 1), 0)) >= pad
        cur = jnp.where(live, dhs_ref[...], 0.0)
        if has_pre:
            dx, dg = _rms_bwd(hs_ref[...], gn_ref[...], jnp.where(live, dhn_ref[...].astype(F32), 0.0))
            cur = cur + dx
            o_dhs[...] = cur

            @pl.when(i == 0)
            def _():
                o_dgn[...] = jnp.zeros_like(o_dgn)
            o_dgn[...] += dg
        if has_post:
            dm, dg = _rms_bwd(m_ref[...], gp_ref[...], cur)
            o_dm[...] = dm

            @pl.when(i == 0)
            def _():
                o_dgp[...] = jnp.zeros_like(o_dgp)
            o_dgp[...] += dg

    ins, in_specs, out_specs, out_shape = [dhs], [_row_spec(d)], [], []
    if has_pre:
        ins += list(pre)
        in_specs += [_row_spec(d), _vec_spec(d), _row_spec(d)]
        out_specs += [_row_spec(d), _vec_spec(d)]
        out_shape += [jax.ShapeDtypeStruct((r, d), F32), jax.ShapeDtypeStruct((1, d), F32)]
    if has_post:
        ins += list(post)
        in_specs += [_row_spec(d), _vec_spec(d)]
        out_specs += [_row_spec(d), _vec_spec(d)]
        out_shape += [jax.ShapeDtypeStruct((r, d), F32), jax.ShapeDtypeStruct((1, d), F32)]
    outs = list(_pallas(body, name=name, grid=(r // ROW_TILE,), in_specs=in_specs, out_specs=out_specs,
                        out_shape=out_shape, compiler_params=_cparams(("arbitrary",)))(*ins))
    dhs_new, dgn = (outs.pop(0), outs.pop(0)) if has_pre else (dhs, None)
    dm, dgp = (outs.pop(0), outs.pop(0)) if has_post else (None, None)
    return dhs_new, dm, dgn, dgp


def _softplus(z):
    return jnp.maximum(z, 0.0) + jnp.log(1.0 + jnp.exp(-jnp.abs(z)))


def _sb_consts(t):
    row = lax.broadcasted_iota(jnp.int32, (t, t), 0)
    col = lax.broadcasted_iota(jnp.int32, (t, t), 1)
    m_up = (col >= row).astype(BF16)
    m_low = (col <= row).astype(BF16)
    return m_up, m_low


def _emit_chains(chains, stages, skew):
    if skew:
        for step in range(len(chains) + len(stages) - 1):
            for si, stage in enumerate(stages):
                if 0 <= step - si < len(chains):
                    stage(chains[step - si])
    else:
        for stage in stages:
            for c in chains:
                stage(c)


def _sb_fwd(q, k, vt3, pad, name):
    r = q.shape[0]
    t = ATT_BLK
    nb = r // t
    nbp = -(-nb // 8) * 8
    jmin = pad // t
    scale = SB_HEAD_DIM ** -0.5

    def body(q_ref, k_ref, vt_ref, o_ref, ss_ref, acc_ref):
        i = pl.program_id(1)
        qt = q_ref[...].astype(F32).T
        sub = lax.broadcasted_iota(jnp.int32, (128, 1), 0)
        m_up, _ = _sb_consts(t)
        kpos0 = lax.broadcasted_iota(jnp.int32, (t, 1), 0)
        qpos = i * t + lax.broadcasted_iota(jnp.int32, (1, t), 1)
        n_mid = jnp.maximum(i - 1 - jmin, 0)
        n_edge = jnp.where(i > jmin, 1, 0)
        qths = [jnp.where((sub >= 64 * h) & (sub < 64 * (h + 1)), qt * scale, 0.0).astype(BF16) for h in range(2)]
        acc_ref[...] = jnp.zeros_like(acc_ref)

        def sweep(js, carry, masked):
            kbs = [k_ref[pl.ds(pl.multiple_of(j * t, t), t), :] for j in js]
            vts = [vt_ref[0, j] for j in js]
            accs = [acc_ref[0], acc_ref[1]]
            s = list(carry)
            chains = [(n, h) for n in range(len(js)) for h in range(2)]
            valid = [(js[n] * t + kpos0 < qpos) & (js[n] * t + kpos0 >= pad) for n in range(len(js))] if masked else None
            zt, inc, saves = {}, {}, []

            def st_scores(c):
                zt[c] = _dot(kbs[c[0]], qths[c[1]])

            def st_cumsum(c):
                lk = -_softplus(zt[c])
                if masked:
                    lk = jnp.where(valid[c[0]], lk, 0.0)
                inc[c] = _split_dot(m_up, lk)

            def st_weights(c):
                n, h = c
                saves.append((h, js[n], s[h]))
                w = jnp.exp(zt[c] + inc[c] + s[h])
                if masked:
                    w = jnp.where(valid[n], w, 0.0)
                accs[h] = accs[h] + _dot(vts[n], w.astype(BF16))
                s[h] = s[h] + inc[c][0:1, :]

            _emit_chains(chains, [st_scores, st_cumsum, st_weights], SB_FWD_SKEW)
            for h, j, val in saves:
                ss_ref[h, 0, pl.ds(j, 1), :] = val
            acc_ref[0] = accs[0]
            acc_ref[1] = accs[1]
            return tuple(s)

        zero = jnp.zeros((1, t), F32)
        bpi = SB_BLOCKS_PER_TRIP
        carry = sweep([i], (zero, zero), True)
        carry = lax.fori_loop(0, n_mid // bpi, lambda it, c: sweep([i - 1 - bpi * it - b for b in range(bpi)], c, False), carry)
        n_rem = n_mid % bpi
        carry = lax.fori_loop(0, n_rem, lambda it, c: sweep([jmin + n_rem - it], c, False), carry)
        lax.fori_loop(0, n_edge, lambda it, c: sweep([jmin + it * 0], c, True), carry)
        acc = jnp.where(sub < 64, acc_ref[0], acc_ref[1])
        o_ref[...] = acc.T

    return _pallas(
        body, name=name, grid=(4, nb),
        in_specs=[pl.BlockSpec((t, 128), lambda hp, i: (i, hp)),
                  pl.BlockSpec((r, 128), lambda hp, i: (0, hp)),
                  pl.BlockSpec((1, nb, 128, t), lambda hp, i: (hp, 0, 0, 0))],
        out_specs=[pl.BlockSpec((t, 128), lambda hp, i: (i, hp)),
                   pl.BlockSpec((2, 1, nbp, t), lambda hp, i: (hp, i, 0, 0))],
        out_shape=[jax.ShapeDtypeStruct((r, SB_WIDTH), F32),
                   jax.ShapeDtypeStruct((8, nb, nbp, t), F32)],
        scratch_shapes=[pltpu.VMEM((2, 128, t), F32)],
        compiler_params=_cparams(("parallel", "arbitrary")),
    )(q, k, vt3)


def _sb_bwd(q, k, v, kt3, ssave, do, pad, name):
    r = q.shape[0]
    t = ATT_BLK
    nb = r // t
    nbp = ssave.shape[2]
    jmin = pad // t
    scale = SB_HEAD_DIM ** -0.5

    def body(q_ref, do_ref, k_ref, v_ref, kt_ref, ss_ref, dq_ref, dk_hbm, dv_hbm, dk_acc, dv_acc, dq_acc, sem):
        hp = pl.program_id(0)
        i = pl.program_id(1)

        @pl.when(i == 0)
        def _():
            dk_acc[...] = jnp.zeros_like(dk_acc)
            dv_acc[...] = jnp.zeros_like(dv_acc)

        qf = q_ref[...].astype(F32)
        dof = do_ref[...]
        qt = qf.T
        dot_ = dof.T
        sub = lax.broadcasted_iota(jnp.int32, (128, 1), 0)
        lane = lax.broadcasted_iota(jnp.int32, (1, 128), 1)
        m_up, m_low = _sb_consts(t)
        kpos0 = lax.broadcasted_iota(jnp.int32, (t, 1), 0)
        qpos = i * t + lax.broadcasted_iota(jnp.int32, (1, t), 1)
        n_mid = jnp.maximum(i - 1 - jmin, 0)
        n_edge = jnp.where(i > jmin, 1, 0)
        in_t = [(sub >= 64 * h) & (sub < 64 * (h + 1)) for h in range(2)]
        in_l = [(lane >= 64 * h) & (lane < 64 * (h + 1)) for h in range(2)]
        qths = [jnp.where(in_t[h], qt * scale, 0.0).astype(BF16) for h in range(2)]
        doths = [jnp.where(in_t[h], dot_, 0.0).astype(BF16) for h in range(2)]
        qhs = [jnp.where(in_l[h], qf * scale, 0.0).astype(BF16) for h in range(2)]
        dohs = [jnp.where(in_l[h], dof, 0.0).astype(BF16) for h in range(2)]
        dq_acc[...] = jnp.zeros_like(dq_acc)

        def sweep(js, carry, masked):
            rows = [pl.ds(pl.multiple_of(j * t, t), t) for j in js]
            kbs = [k_ref[rw, :] for rw in rows]
            vbs = [v_ref[rw, :] for rw in rows]
            kts = [kt_ref[0, j] for j in js]
            sss = [[ss_ref[h, 0, pl.ds(j, 1), :] for h in range(2)] for j in js]
            dv_old = [dv_acc[rw, :] for rw in rows]
            dk_old = [dk_acc[rw, :] for rw in rows]
            dqs = [dq_acc[0], dq_acc[1]]
            ec = list(carry)
            chains = [(n, h) for n in range(len(js)) for h in range(2)]
            nch = len(chains)
            valid = [(js[n] * t + kpos0 < qpos) & (js[n] * t + kpos0 >= pad) for n in range(len(js))] if masked else None
            zt, dvt, sp, inc, e, big_e = {}, {}, {}, {}, {}, {}

            def st_scores(c):
                zt[c] = _dot(kbs[c[0]], qths[c[1]])
                dvt[c] = _dot(vbs[c[0]], doths[c[1]])

            def st_cumsum(c):
                sp[c] = _softplus(zt[c])
                lk = -sp[c]
                if masked:
                    lk = jnp.where(valid[c[0]], lk, 0.0)
                inc[c] = _split_dot(m_up, lk)

            def st_weights(c):
                n, h = c
                w = jnp.exp(zt[c] + inc[c] + sss[n][h])
                if masked:
                    w = jnp.where(valid[n], w, 0.0)
                dv_old[n] = dv_old[n] + _dot(w.astype(BF16), dohs[h])
                e[c] = w * dvt[c]
                pinc = _split_dot(m_low, e[c])
                big_e[c] = pinc - e[c] + ec[h]
                ec[h] = ec[h] + pinc[t - 1:t, :]

            def st_dscores(c):
                n, h = c
                dz = e[c] - jnp.exp(zt[c] - sp[c]) * (e[c] + big_e[c])
                if masked:
                    dz = jnp.where(valid[n], dz, 0.0)
                dzb = dz.astype(BF16)
                dqs[h] = dqs[h] + _dot(kts[n], dzb)
                dk_old[n] = dk_old[n] + _dot(dzb, qhs[h])

            _emit_chains(chains, [st_scores, st_cumsum, st_weights, st_dscores], SB_BWD_SKEW)
            for n, rw in enumerate(rows):
                dv_acc[rw, :] = dv_old[n]
                dk_acc[rw, :] = dk_old[n]
            dq_acc[0] = dqs[0]
            dq_acc[1] = dqs[1]
            return tuple(ec)

        zero = jnp.zeros((1, t), F32)
        bpi = SB_BLOCKS_PER_TRIP
        carry = lax.fori_loop(0, n_edge, lambda it, c: sweep([jmin + it * 0], c, True), (zero, zero))
        carry = lax.fori_loop(0, n_mid // bpi, lambda it, c: sweep([jmin + 1 + bpi * it + b for b in range(bpi)], c, False), carry)
        n_rem = n_mid % bpi
        carry = lax.fori_loop(0, n_rem, lambda it, c: sweep([i - n_rem + it], c, False), carry)
        sweep([i], carry, True)
        dq_ref[...] = (jnp.where(sub < 64, dq_acc[0], dq_acc[1]) * scale).T

        @pl.when(i == nb - 1)
        def _():
            c1 = pltpu.make_async_copy(dk_acc, dk_hbm.at[hp], sem.at[0])
            c2 = pltpu.make_async_copy(dv_acc, dv_hbm.at[hp], sem.at[1])
            c1.start()
            c2.start()
            c1.wait()
            c2.wait()

    return _pallas(
        body, name=name, grid=(4, nb),
        in_specs=[pl.BlockSpec((t, 128), lambda hp, i: (i, hp)),
                  pl.BlockSpec((t, 128), lambda hp, i: (i, hp)),
                  pl.BlockSpec((r, 128), lambda hp, i: (0, hp)),
                  pl.BlockSpec((r, 128), lambda hp, i: (0, hp)),
                  pl.BlockSpec((1, nb, 128, t), lambda hp, i: (hp, 0, 0, 0)),
                  pl.BlockSpec((2, 1, nbp, t), lambda hp, i: (hp, i, 0, 0))],
        out_specs=[pl.BlockSpec((t, 128), lambda hp, i: (i, hp)),
                   pl.BlockSpec(memory_space=pl.ANY), pl.BlockSpec(memory_space=pl.ANY)],
        out_shape=[jax.ShapeDtypeStruct((r, SB_WIDTH), F32),
                   jax.ShapeDtypeStruct((4, r, 128), F32), jax.ShapeDtypeStruct((4, r, 128), F32)],
        scratch_shapes=[pltpu.VMEM((r, 128), F32), pltpu.VMEM((r, 128), F32), pltpu.VMEM((2, 128, t), F32),
                        pltpu.SemaphoreType.DMA((2,))],
        compiler_params=_cparams(("arbitrary", "arbitrary")),
    )(q, do, k, v, kt3, ssave)


def _s5_disc(lam_re, lam_im, logdt, btr, bti):
    lr = jnp.minimum(lam_re, -1e-4)
    li = lam_im
    dt = jnp.exp(logdt)
    mag = jnp.exp(lr * dt)
    ang = li * dt
    a_re, a_im = mag * jnp.cos(ang), mag * jnp.sin(ang)
    den = lr * lr + li * li
    nr, ni = a_re - 1.0, a_im
    c_re = (nr * lr + ni * li) / den
    c_im = (ni * lr - nr * li) / den
    return a_re, a_im, c_re * btr - c_im * bti, c_re * bti + c_im * btr


def _s5_prep(lam_re, lam_im, logdt, btr, bti, name):
    ns = lam_re.shape[1]

    def body(lr_ref, li_ref, dt_ref, br_ref, bi_ref, ar_ref, ai_ref, bbr_ref, bbi_ref):
        ar, ai, bbr, bbi = _s5_disc(lr_ref[...], li_ref[...], dt_ref[...], br_ref[...], bi_ref[...])
        ar_ref[...] = ar
        ai_ref[...] = ai
        bbr_ref[...] = bbr
        bbi_ref[...] = bbi

    return _pallas(body, name=name,
                   out_shape=[jax.ShapeDtypeStruct((1, ns), F32)] * 2 + [jax.ShapeDtypeStruct((S5_GROUP, ns), F32)] * 2,
                   )(lam_re, lam_im, logdt, btr, bti)


def _s5_prep_bwd(lam_re, lam_im, logdt, btr, bti, dar, dai, dbbr, dbbi, name):
    ns = lam_re.shape[1]

    def body(lr_ref, li_ref, dt_ref, br_ref, bi_ref, dar_ref, dai_ref, dbr_ref, dbi_ref, o_lr, o_li, o_dt, o_br, o_bi):
        _, vjp = jax.vjp(_s5_disc, lr_ref[...], li_ref[...], dt_ref[...], br_ref[...], bi_ref[...])
        g = vjp((dar_ref[...], dai_ref[...], dbr_ref[...], dbi_ref[...]))
        o_lr[...] = g[0]
        o_li[...] = g[1]
        row = lax.broadcasted_iota(jnp.int32, (ns, ns), 0) // S5_STATE
        col = lax.broadcasted_iota(jnp.int32, (ns, ns), 1) // S5_STATE
        same = (row == col).astype(F32)
        o_dt[...] = _dot_hi(jnp.broadcast_to(g[2], (8, ns)), same)[0:1]
        o_br[...] = g[3]
        o_bi[...] = g[4]

    return _pallas(body, name=name,
                   out_shape=[jax.ShapeDtypeStruct((1, ns), F32)] * 3 + [jax.ShapeDtypeStruct((S5_GROUP, ns), F32)] * 2,
                   compiler_params=pltpu.CompilerParams(vmem_limit_bytes=VMEM_LIMIT),
                   )(lam_re, lam_im, logdt, btr, bti, dar, dai, dbbr, dbbi)


def _s5_scan(br, bi, ar, ai, t, reverse=False):
    row = lax.broadcasted_iota(jnp.int32, (t, 1), 0)
    pr, pi_ = ar, ai
    k = 1
    while k < t:
        if reverse:
            sr, si, ok = pltpu.roll(br, t - k, 0), pltpu.roll(bi, t - k, 0), row < t - k
        else:
            sr, si, ok = pltpu.roll(br, k, 0), pltpu.roll(bi, k, 0), row >= k
        sr = jnp.where(ok, sr, 0.0)
        si = jnp.where(ok, si, 0.0)
        br, bi = br + pr * sr - pi_ * si, bi + pr * si + pi_ * sr
        pr, pi_ = pr * pr - pi_ * pi_, 2.0 * pr * pi_
        k *= 2
    return br, bi


def _s5_power_table(ar, ai, t, reverse=False):
    row = lax.broadcasted_iota(jnp.int32, (t, 1), 0)
    hot = row == (t - 1 if reverse else 0)
    return _s5_scan(jnp.where(hot, ar, 0.0), jnp.where(hot, ai, 0.0), ar, ai, t, reverse)


_GELU_C = math.sqrt(2.0 / math.pi)


def _gelu(y):
    th = jnp.tanh(_GELU_C * (y + 0.044715 * y * y * y))
    return 0.5 * y * (1.0 + th), th


def _sigmoid(x):
    return 1.0 / (1.0 + jnp.exp(-x))


def _s5_fwd(u, wb, a, wc, dskip, wglu, bglu, gnorm, name):
    r = u.shape[0]
    t = S5_TILE
    nt = r // t
    ns = wb.shape[2]
    w = S5_WIDTH

    def body(u_ref, wb_ref, a_ref, wc_ref, d_ref, wg_ref, bg_ref, gn_ref, y_ref, on_ref, xs_ref, pw_ref, carry_ref):
        i = pl.program_id(0)
        ar, ai = a_ref[0], a_ref[1]

        @pl.when(i == 0)
        def _():
            pr, pi_ = _s5_power_table(ar, ai, t)
            pw_ref[0] = pr
            pw_ref[1] = pi_
            carry_ref[...] = jnp.zeros_like(carry_ref)

        u_ = u_ref[...]
        ub = u_.astype(BF16)
        xr, xi = _s5_scan(_dot(ub, wb_ref[0]), _dot(ub, wb_ref[1]), ar, ai, t)
        cr, ci = carry_ref[0], carry_ref[1]
        xs_ref[0, 0:1, :] = cr
        xs_ref[0, 1:2, :] = ci
        pr, pi_ = pw_ref[0], pw_ref[1]
        xr = xr + pr * cr - pi_ * ci
        xi = xi + pr * ci + pi_ * cr
        carry_ref[0] = xr[t - 1:t, :]
        carry_ref[1] = xi[t - 1:t, :]
        y = _dot(xr.astype(BF16), wc_ref[0]) - _dot(xi.astype(BF16), wc_ref[1]) + d_ref[...] * u_
        h, _ = _gelu(y)
        gate = _sigmoid(_dot(h.astype(BF16), wg_ref[...]) + bg_ref[...])
        y_ref[...] = y
        on_ref[...] = _rms(h * gate, gn_ref[...]).astype(BF16)

    full = lambda shape: pl.BlockSpec(shape, lambda i: (0,) * len(shape))
    return _pallas(
        body, name=name, grid=(nt,),
        in_specs=[_row_spec(w, t), full((2, w, ns)), full((2, 1, ns)), full((2, ns, w)), full((1, w)),
                  full((w, w)), full((1, w)), full((1, w))],
        out_specs=[_row_spec(w, t), _row_spec(w, t), pl.BlockSpec((1, 2, ns), lambda i: (i, 0, 0))],
        out_shape=[jax.ShapeDtypeStruct((r, w), F32), jax.ShapeDtypeStruct((r, w), BF16),
                   jax.ShapeDtypeStruct((nt, 2, ns), F32)],
        scratch_shapes=[pltpu.VMEM((2, t, ns), F32), pltpu.VMEM((2, 1, ns), F32)],
        compiler_params=_cparams(("arbitrary",)),
    )(u, wb, a, wc, dskip, wglu, bglu, gnorm)


def _s5_bwd(u, y, don, xstart, wb, a, wc, dskip, wglu, bglu, gnorm, name):
    r = u.shape[0]
    t = S5_TILE
    nt = r // t
    ns = wb.shape[2]
    w = S5_WIDTH
    nt_dims = ((1,), (1,))
    tn_dims = ((0,), (0,))

    def body(u_ref, y_ref, don_ref, xs_ref, wb_hbm, a_ref, wc_hbm, d_ref, wg_ref, bg_ref, gn_ref,
             du_ref, da_ref, dd_ref, dbg_ref, dgn_ref, dwb_hbm, dwc_hbm, dwg_hbm,
             wb_ref, wc_ref, pw_ref, pwr_ref, lam_ref, acc_wb, acc_wc, acc_wg, sem):
        i = pl.program_id(0)
        ar, ai = a_ref[0], a_ref[1]

        @pl.when(i == 0)
        def _():
            c1 = pltpu.make_async_copy(wb_hbm, wb_ref, sem.at[0])
            c2 = pltpu.make_async_copy(wc_hbm, wc_ref, sem.at[1])
            c1.start()
            c2.start()
            pr, pi_ = _s5_power_table(ar, ai, t)
            pw_ref[0] = pr
            pw_ref[1] = pi_
            pr, pi_ = _s5_power_table(ar, -ai, t, reverse=True)
            pwr_ref[0] = pr
            pwr_ref[1] = pi_
            lam_ref[...] = jnp.zeros_like(lam_ref)
            acc_wb[...] = jnp.zeros_like(acc_wb)
            acc_wc[...] = jnp.zeros_like(acc_wc)
            acc_wg[...] = jnp.zeros_like(acc_wg)
            da_ref[...] = jnp.zeros_like(da_ref)
            dd_ref[...] = jnp.zeros_like(dd_ref)
            dbg_ref[...] = jnp.zeros_like(dbg_ref)
            dgn_ref[...] = jnp.zeros_like(dgn_ref)
            c1.wait()
            c2.wait()

        u_ = u_ref[...]
        y_ = y_ref[...]
        ub = u_.astype(BF16)
        h, th = _gelu(y_)
        hb = h.astype(BF16)
        wg = wg_ref[...]
        gate = _sigmoid(_dot(hb, wg) + bg_ref[...])
        d_out, dgn = _rms_bwd(h * gate, gn_ref[...], don_ref[...])
        dgn_ref[...] += dgn
        dhw = d_out * h * gate * (1.0 - gate)
        dhwb = dhw.astype(BF16)
        dh = d_out * gate + _dot(dhwb, wg, nt_dims)
        acc_wg[...] += _dot(hb, dhwb, tn_dims)
        dbg_ref[...] += jnp.sum(dhw, axis=0, keepdims=True)
        dgelu = 0.5 * (1.0 + th) + 0.5 * y_ * (1.0 - th * th) * _GELU_C * (1.0 + 3.0 * 0.044715 * y_ * y_)
        dy = dh * dgelu
        dd_ref[...] += jnp.sum(dy * u_, axis=0, keepdims=True)
        dyb = dy.astype(BF16)
        xr, xi = _s5_scan(_dot(ub, wb_ref[0]), _dot(ub, wb_ref[1]), ar, ai, t)
        cr, ci = xs_ref[0, 0:1, :], xs_ref[0, 1:2, :]
        pr, pi_ = pw_ref[0], pw_ref[1]
        xr = xr + pr * cr - pi_ * ci
        xi = xi + pr * ci + pi_ * cr
        acc_wc[0] += _dot(xr.astype(BF16), dyb, tn_dims)
        acc_wc[1] -= _dot(xi.astype(BF16), dyb, tn_dims)
        lr, li = _s5_scan(_dot(dyb, wc_ref[0], nt_dims), -_dot(dyb, wc_ref[1], nt_dims), ar, -ai, t, reverse=True)
        cr2, ci2 = lam_ref[0], lam_ref[1]
        pr, pi_ = pwr_ref[0], pwr_ref[1]
        lr = lr + pr * cr2 - pi_ * ci2
        li = li + pr * ci2 + pi_ * cr2
        lam_ref[0] = lr[0:1, :]
        lam_ref[1] = li[0:1, :]
        row = lax.broadcasted_iota(jnp.int32, (t, 1), 0)
        xpr = jnp.where(row == 0, cr, pltpu.roll(xr, 1, 0))
        xpi = jnp.where(row == 0, ci, pltpu.roll(xi, 1, 0))
        da_ref[0] += jnp.sum(lr * xpr + li * xpi, axis=0, keepdims=True)
        da_ref[1] += jnp.sum(li * xpr - lr * xpi, axis=0, keepdims=True)
        lrb, lib = lr.astype(BF16), li.astype(BF16)
        acc_wb[0] += _dot(ub, lrb, tn_dims)
        acc_wb[1] += _dot(ub, lib, tn_dims)
        du_ref[...] = d_ref[...] * dy + _dot(lrb, wb_ref[0], nt_dims) + _dot(lib, wb_ref[1], nt_dims)

        @pl.when(i == nt - 1)
        def _():
            cps = [pltpu.make_async_copy(acc_wb, dwb_hbm, sem.at[0]), pltpu.make_async_copy(acc_wc, dwc_hbm, sem.at[1]),
                   pltpu.make_async_copy(acc_wg, dwg_hbm, sem.at[2])]
            for c in cps:
                c.start()
            for c in cps:
                c.wait()

    rev = lambda i: (nt - 1 - i, 0)
    full = lambda shape: pl.BlockSpec(shape, lambda i: (0,) * len(shape))
    hbm = pl.BlockSpec(memory_space=pl.ANY)
    return _pallas(
        body, name=name, grid=(nt,),
        in_specs=[pl.BlockSpec((t, w), rev), pl.BlockSpec((t, w), rev), pl.BlockSpec((t, w), rev),
                  pl.BlockSpec((1, 2, ns), lambda i: (nt - 1 - i, 0, 0)), hbm, full((2, 1, ns)), hbm, full((1, w)),
                  full((w, w)), full((1, w)), full((1, w))],
        out_specs=[pl.BlockSpec((t, w), rev), full((2, 1, ns)), full((1, w)), full((1, w)), full((1, w)), hbm, hbm, hbm],
        out_shape=[jax.ShapeDtypeStruct((r, w), F32), jax.ShapeDtypeStruct((2, 1, ns), F32)]
        + [jax.ShapeDtypeStruct((1, w), F32)] * 3
        + [jax.ShapeDtypeStruct((2, w, ns), F32), jax.ShapeDtypeStruct((2, ns, w), F32), jax.ShapeDtypeStruct((w, w), F32)],
        scratch_shapes=[pltpu.VMEM((2, w, ns), BF16), pltpu.VMEM((2, ns, w), BF16),
                        pltpu.VMEM((2, t, ns), F32), pltpu.VMEM((2, t, ns), F32), pltpu.VMEM((2, 1, ns), F32),
                        pltpu.VMEM((2, w, ns), F32), pltpu.VMEM((2, ns, w), F32), pltpu.VMEM((w, w), F32),
                        pltpu.SemaphoreType.DMA((3,))],
        compiler_params=_cparams(("arbitrary",)),
    )(u, y, don, xstart, wb, a, wc, dskip, wglu, bglu, gnorm)


def _s5_expand(lam_re, lam_im, log_dt, b_re, b_im, c_re, c_im):
    g, n, p = S5_GROUPS, S5_STATE, S5_GROUP
    ns = g * n
    rows = lambda x: x.reshape(1, ns)
    logdt = jnp.repeat(log_dt.reshape(g), n).reshape(1, ns)
    btr = b_re.reshape(ns, p).T
    bti = b_im.reshape(ns, p).T
    ctr = c_re.transpose(0, 2, 1).reshape(ns, p)
    cti = c_im.transpose(0, 2, 1).reshape(ns, p)
    mask = (jnp.arange(g * p)[:, None] // p) == (jnp.arange(ns)[None, :] // n)
    return rows(lam_re), rows(lam_im), logdt, btr, bti, ctr, cti, mask


def _s5_block_diag_b(bb, mask):
    return jnp.where(mask, jnp.tile(bb, (S5_GROUPS, 1)), 0.0)


def _s5_block_diag_c(ct, mask):
    return jnp.where(mask.T, jnp.tile(ct, (1, S5_GROUPS)), 0.0)


def _s5_diag_of_b(dwb, mask):
    return jnp.where(mask, dwb, 0.0).reshape(S5_GROUPS, S5_GROUP, -1).sum(0)


def _s5_diag_of_c(dwc, mask):
    ns = dwc.shape[0]
    return jnp.where(mask.T, dwc, 0.0).reshape(ns, S5_GROUPS, S5_GROUP).sum(1)


DN_PRE_TILE = 256
_DN_QKV = 3 * DN_WIDTH


def _halo_specs(width, tile, nt, prev):
    per = tile // 8
    if prev:
        return pl.BlockSpec((8, width), lambda i: (jnp.maximum(i * per - 1, 0), 0))
    return pl.BlockSpec((8, width), lambda i: (jnp.minimum((i + 1) * per, nt * per - 1), 0))


def _shift_down(x, halo, s, t):
    xx = jnp.concatenate([halo, x], axis=0)
    return pltpu.roll(xx, s, 0)[8:]


def _shift_up(x, halo, s, t):
    xx = jnp.concatenate([x, halo], axis=0)
    return pltpu.roll(xx, t + 8 - s, 0)[:t]


def _silu(x):
    s = _sigmoid(x)
    return x * s, s


def _dn_gates(ab, alog, dtb, live):
    lane = lax.broadcasted_iota(jnp.int32, (1, 128), 1)
    g = -jnp.exp(alog) * _softplus(ab + dtb)
    beta = _sigmoid(ab)
    return jnp.where(live & (lane < DN_HEADS), g, jnp.where(live & (lane < 2 * DN_HEADS), beta, 0.0))


def _dn_pre_fwd(proj, ab, conv_w, alog, dtb, pad, name):
    r = proj.shape[0]
    t = DN_PRE_TILE
    nt = r // t
    scale = DN_HEAD_DIM ** -0.5

    def body(x_ref, halo_ref, ab_ref, w_ref, al_ref, dt_ref, co_ref, q_ref, k_ref, v_ref, gb_ref):
        i = pl.program_id(0)
        x = x_ref[...]
        halo = jnp.where(i > 0, halo_ref[...], 0.0)
        w = w_ref[...]
        co = w[3:4] * x
        for tap in range(DN_CONV - 1):
            co = co + w[tap:tap + 1] * _shift_down(x, halo, DN_CONV - 1 - tap, t)
        co_ref[...] = co
        act, _ = _silu(co)
        for hd in range(DN_HEADS):
            sl = slice(hd * 128, (hd + 1) * 128)
            for base, o_ref, sc in ((0, q_ref, scale), (DN_WIDTH, k_ref, 1.0)):
                xh = act[:, base + hd * 128: base + (hd + 1) * 128]
                o_ref[:, sl] = xh * (lax.rsqrt(jnp.sum(xh * xh, axis=-1, keepdims=True) + EPS) * sc)
        v_ref[...] = act[:, 2 * DN_WIDTH:]
        rows = i * t + lax.broadcasted_iota(jnp.int32, (t, 1), 0)
        gb_ref[...] = _dn_gates(ab_ref[...], al_ref[...], dt_ref[...], rows >= pad)

    return _pallas(
        body, name=name, grid=(nt,),
        in_specs=[pl.BlockSpec((t, _DN_QKV), lambda i: (i, 0)), _halo_specs(_DN_QKV, t, nt, True), _row_spec(128, t),
                  pl.BlockSpec((DN_CONV, _DN_QKV), lambda i: (0, 0)), _vec_spec(128), _vec_spec(128)],
        out_specs=[_row_spec(_DN_QKV, t), _row_spec(DN_WIDTH, t), _row_spec(DN_WIDTH, t), _row_spec(DN_WIDTH, t), _row_spec(128, t)],
        out_shape=[jax.ShapeDtypeStruct((r, _DN_QKV), F32)] + [jax.ShapeDtypeStruct((r, DN_WIDTH), F32)] * 3
        + [jax.ShapeDtypeStruct((r, 128), F32)],
        compiler_params=_cparams(("parallel",)),
    )(proj, proj, ab, conv_w, alog, dtb)


def _dn_pre_bwd(co, dq, dk, dv, dgb, ab, alog, dtb, pad, name):
    r = co.shape[0]
    t = DN_PRE_TILE
    nt = r // t
    scale = DN_HEAD_DIM ** -0.5

    def body(co_ref, dq_ref, dk_ref, dv_ref, dgb_ref, ab_ref, al_ref, dt_ref, dco_ref, dab_ref, dal_ref, ddt_ref):
        i = pl.program_id(0)

        @pl.when(i == 0)
        def _():
            dal_ref[...] = jnp.zeros_like(dal_ref)
            ddt_ref[...] = jnp.zeros_like(ddt_ref)

        co_ = co_ref[...]
        act, sg = _silu(co_)
        dsilu = sg * (1.0 + co_ * (1.0 - sg))
        for hd in range(DN_HEADS):
            sl = slice(hd * 128, (hd + 1) * 128)
            for base, d_ref, sc in ((0, dq_ref, scale), (DN_WIDTH, dk_ref, 1.0)):
                cs = slice(base + hd * 128, base + (hd + 1) * 128)
                xh = act[:, cs]
                rn = lax.rsqrt(jnp.sum(xh * xh, axis=-1, keepdims=True) + EPS)
                xhat = xh * rn
                dy = d_ref[:, sl]
                dx = (sc * rn) * (dy - xhat * jnp.sum(dy * xhat, axis=-1, keepdims=True))
                dco_ref[:, cs] = dx * dsilu[:, cs]
        dco_ref[:, 2 * DN_WIDTH:] = dv_ref[...] * dsilu[:, 2 * DN_WIDTH:]
        rows = i * t + lax.broadcasted_iota(jnp.int32, (t, 1), 0)
        live = rows >= pad
        lane = lax.broadcasted_iota(jnp.int32, (1, 128), 1)
        ab_ = ab_ref[...]
        dgb_ = dgb_ref[...]
        is_g = live & (lane < DN_HEADS)
        is_b = live & (lane >= DN_HEADS) & (lane < 2 * DN_HEADS)
        arg = ab_ + dt_ref[...]
        ea = jnp.exp(al_ref[...])
        da = jnp.where(is_g, -dgb_ * ea * _sigmoid(arg), 0.0)
        beta = _sigmoid(ab_)
        dab_ref[...] = da + jnp.where(is_b, dgb_ * beta * (1.0 - beta), 0.0)
        ddt_ref[...] += jnp.sum(da, axis=0, keepdims=True)
        dal_ref[...] += jnp.sum(jnp.where(is_g, -dgb_ * ea * _softplus(arg), 0.0), axis=0, keepdims=True)

    return _pallas(
        body, name=name, grid=(nt,),
        in_specs=[_row_spec(_DN_QKV, t), _row_spec(DN_WIDTH, t), _row_spec(DN_WIDTH, t), _row_spec(DN_WIDTH, t),
                  _row_spec(128, t), _row_spec(128, t), _vec_spec(128), _vec_spec(128)],
        out_specs=[_row_spec(_DN_QKV, t), _row_spec(128, t), _vec_spec(128), _vec_spec(128)],
        out_shape=[jax.ShapeDtypeStruct((r, _DN_QKV), F32), jax.ShapeDtypeStruct((r, 128), F32),
                   jax.ShapeDtypeStruct((1, 128), F32), jax.ShapeDtypeStruct((1, 128), F32)],
        compiler_params=_cparams(("arbitrary",)),
    )(co, dq, dk, dv, dgb, ab, alog, dtb)


def _dn_conv_bwd(dco, proj, conv_w, name):
    r = dco.shape[0]
    t = DN_PRE_TILE
    nt = r // t

    def body(d_ref, dh_ref, x_ref, xh_ref, w_ref, dx_ref, dw_ref):
        i = pl.program_id(0)

        @pl.when(i == 0)
        def _():
            dw_ref[...] = jnp.zeros_like(dw_ref)

        d = d_ref[...]
        dhalo = jnp.where(i < nt - 1, dh_ref[...], 0.0)
        x = x_ref[...]
        xhalo = jnp.where(i > 0, xh_ref[...], 0.0)
        w = w_ref[...]
        dx = w[3:4] * d
        dws = [None] * DN_CONV
        dws[3] = jnp.sum(d * x, axis=0, keepdims=True)
        for tap in range(DN_CONV - 1):
            s = DN_CONV - 1 - tap
            dx = dx + w[tap:tap + 1] * _shift_up(d, dhalo, s, t)
            dws[tap] = jnp.sum(d * _shift_down(x, xhalo, s, t), axis=0, keepdims=True)
        dx_ref[...] = dx
        dw_ref[...] += jnp.concatenate(dws + [jnp.zeros((8 - DN_CONV, _DN_QKV), F32)], axis=0)

    return _pallas(
        body, name=name, grid=(nt,),
        in_specs=[_row_spec(_DN_QKV, t), _halo_specs(_DN_QKV, t, nt, False),
                  pl.BlockSpec((t, _DN_QKV), lambda i: (i, 0)), _halo_specs(_DN_QKV, t, nt, True),
                  pl.BlockSpec((DN_CONV, _DN_QKV), lambda i: (0, 0))],
        out_specs=[_row_spec(_DN_QKV, t), pl.BlockSpec((8, _DN_QKV), lambda i: (0, 0))],
        out_shape=[jax.ShapeDtypeStruct((r, _DN_QKV), F32), jax.ShapeDtypeStruct((8, _DN_QKV), F32)],
        compiler_params=_cparams(("arbitrary",)),
    )(dco, dco, proj, proj, conv_w)


def _dn_inverse(n_mat):
    c = n_mat.shape[0]
    row = lax.broadcasted_iota(jnp.int32, (c, c), 0)
    col = lax.broadcasted_iota(jnp.int32, (c, c), 1)
    eye = (row == col).astype(F32)
    nd = jnp.where(row // DN_SUB == col // DN_SUB, n_mat, 0.0)
    no = n_mat - nd

    def geometric(b, order):
        x = eye + b
        p = b
        k = 2
        while k < order:
            p = _dot_hi(p, p)
            x = x + _dot_hi(x, p)
            k *= 2
        return x

    td = geometric(-nd, DN_SUB)
    x = geometric(-_dot_hi(td, no), c // DN_SUB)
    return _dot_hi(x, td)


def _dn_chunk_common(q_ref, k_ref, v_ref, gb_ref, gbt_ref, h):
    c = DN_CHUNK
    row = lax.broadcasted_iota(jnp.int32, (c, c), 0)
    col = lax.broadcasted_iota(jnp.int32, (c, c), 1)
    lane = lax.broadcasted_iota(jnp.int32, (1, 128), 1)
    q, k, v = q_ref[...], k_ref[...], v_ref[...]
    gbv = gb_ref[...]
    tri = (row >= col).astype(BF16)
    gam_all = _split_dot(tri, gbv)
    gam = jnp.sum(jnp.where(lane == h, gam_all, 0.0), axis=1, keepdims=True)
    beta = jnp.sum(jnp.where(lane == h + DN_HEADS, gbv, 0.0), axis=1, keepdims=True)
    g_row = jnp.broadcast_to(gbt_ref[pl.ds(h, 1), :], (8, c))
    hi = g_row.astype(BF16)
    lo = (g_row - hi.astype(F32)).astype(BF16)
    tri_t = (row <= col).astype(BF16)
    gam_row = (_dot(hi, tri_t) + _dot(lo, tri_t))[0:1]
    dec = jnp.where(row >= col, jnp.exp(jnp.minimum(gam - gam_row, 0.0)), 0.0)
    kb, qb = k.astype(BF16), q.astype(BF16)
    nt_dims = ((1,), (1,))
    kk = _dot(kb, kb, nt_dims)
    qk = _dot(qb, kb, nt_dims)
    eg = jnp.exp(gam)
    gam_l = gam[c - 1:c, :]
    return dict(q=q, k=k, v=v, qb=qb, kb=kb, gam=gam, beta=beta, dec=dec, kk=kk, qk=qk, eg=eg, gam_l=gam_l,
                row=row, col=col, lane=lane, att=qk * dec, qg=q * eg, kt=k * jnp.exp(gam_l - gam),
                rhs=jnp.concatenate([v * beta, k * (beta * eg)], axis=1))


def _dn_fwd(q, k, v, gb, gbt, name):
    r = q.shape[0]
    c = DN_CHUNK
    nc = r // c
    dh = DN_HEAD_DIM
    tn_dims = ((0,), (0,))

    def body(q_ref, k_ref, v_ref, gb_ref, gbt_ref, o_ref, ss_ref, ts_ref, s_ref):
        ci, h = pl.program_id(0), pl.program_id(1)

        @pl.when(ci == 0)
        def _():
            s_ref[h] = jnp.zeros((dh, dh), F32)

        z = _dn_chunk_common(q_ref, k_ref, v_ref, gb_ref, gbt_ref, h)
        n_mat = jnp.where(z["row"] > z["col"], z["beta"] * z["kk"] * z["dec"], 0.0)
        t_inv = _dn_inverse(n_mat)
        sol = _dot_hi(t_inv, z["rhs"])
        s = s_ref[h]
        sb = s.astype(BF16)
        v_new = sol[:, :dh] - _dot(sol[:, dh:].astype(BF16), sb)
        vnb = v_new.astype(BF16)
        o_ref[...] = _dot(z["qg"].astype(BF16), sb) + _dot(z["att"].astype(BF16), vnb)
        ss_ref[0, 0] = s
        ts_ref[0, 0] = t_inv
        s_ref[h] = s * jnp.exp(z["gam_l"]) + _dot(z["kt"].astype(BF16), vnb, tn_dims)

    blk = pl.BlockSpec((c, dh), lambda ci, h: (ci, h))
    sav = pl.BlockSpec((1, 1, dh, dh), lambda ci, h: (ci, h, 0, 0))
    return _pallas(
        body, name=name, grid=(nc, DN_HEADS),
        in_specs=[blk, blk, blk, pl.BlockSpec((c, 128), lambda ci, h: (ci, 0)), pl.BlockSpec((16, c), lambda ci, h: (0, ci))],
        out_specs=[blk, sav, sav],
        out_shape=[jax.ShapeDtypeStruct((r, DN_WIDTH), F32), jax.ShapeDtypeStruct((nc, DN_HEADS, dh, dh), F32),
                   jax.ShapeDtypeStruct((nc, DN_HEADS, dh, dh), F32)],
        scratch_shapes=[pltpu.VMEM((DN_HEADS, dh, dh), F32)],
        compiler_params=_cparams(("arbitrary", "arbitrary")),
    )(q, k, v, gb, gbt)


def _dn_bwd(q, k, v, gb, gbt, ssave, tsave, do, name):
    r = q.shape[0]
    c = DN_CHUNK
    nc = r // c
    dh = DN_HEAD_DIM
    nt_dims = ((1,), (1,))
    tn_dims = ((0,), (0,))

    def body(q_ref, k_ref, v_ref, gb_ref, gbt_ref, ss_ref, ts_ref, do_ref, dq_ref, dk_ref, dv_ref, dgb_ref, ds_ref):
        ci, h = pl.program_id(0), pl.program_id(1)

        @pl.when(ci == 0)
        def _():
            ds_ref[h] = jnp.zeros((dh, dh), F32)

        @pl.when(h == 0)
        def _():
            dgb_ref[...] = jnp.zeros_like(dgb_ref)

        z = _dn_chunk_common(q_ref, k_ref, v_ref, gb_ref, gbt_ref, h)
        row, col, lane = z["row"], z["col"], z["lane"]
        k_, v_, kb, qb = z["k"], z["v"], z["kb"], z["qb"]
        beta, eg, dec, kk, qk, gam, gam_l = z["beta"], z["eg"], z["dec"], z["kk"], z["qk"], z["gam"], z["gam_l"]
        qg, kt, att = z["qg"], z["kt"], z["att"]
        t_inv = ts_ref[0, 0]
        sol = _dot_hi(t_inv, z["rhs"])
        kcd = sol[:, dh:]
        s = ss_ref[0, 0]
        sb = s.astype(BF16)
        v_new = sol[:, :dh] - _dot(kcd.astype(BF16), sb)
        vnb = v_new.astype(BF16)
        ds_next = ds_ref[h]
        dsb = ds_next.astype(BF16)
        dob = do_ref[...].astype(BF16)
        rs = lambda x: jnp.sum(x, axis=1, keepdims=True)
        tot = lambda x: jnp.sum(rs(x), axis=0, keepdims=True)

        dqg = _dot(dob, sb, nt_dims)
        ds = _dot(qg.astype(BF16), dob, tn_dims)
        d_att = jnp.where(row >= col, _dot(dob, vnb, nt_dims), 0.0)
        dvn = _dot(att.astype(BF16), dob, tn_dims) + _dot(kt.astype(BF16), dsb)
        dkt = _dot(vnb, dsb, nt_dims)
        eg_l = jnp.exp(gam_l)
        ds = ds + ds_next * eg_l
        dgam_l = tot(ds_next * s) * eg_l
        dvnb = dvn.astype(BF16)
        dkcd = -_dot(dvnb, sb, nt_dims)
        ds = ds - _dot(kcd.astype(BF16), dvnb, tn_dims)
        dsol = jnp.concatenate([dvn, dkcd], axis=1)
        drhs = lax.dot_general(t_inv, dsol, (tn_dims, ((), ())), preferred_element_type=F32, precision=_HIGH)
        dn = jnp.where(row > col, -lax.dot_general(drhs, sol, (nt_dims, ((), ())), preferred_element_type=F32,
                                                   precision=_HIGH), 0.0)
        drv, drk = drhs[:, :dh], drhs[:, dh:]
        s_rkk = rs(drk * k_)
        dv_ref[...] = drv * beta
        dbeta = rs(drv * v_) + s_rkk * eg + rs(dn * kk * dec)
        dk = drk * (beta * eg)
        dgam = s_rkk * beta * eg
        dkk = (dn * beta * dec).astype(BF16)
        dd = dn * beta * kk + d_att * qk
        dqk = (d_att * dec).astype(BF16)
        dq_ref[...] = _dot(dqk, kb) + dqg * eg
        dk = dk + _dot(dqk, qb, tn_dims) + _dot(dkk, kb) + _dot(dkk, kb, tn_dims)
        w = dd * dec
        ones = jnp.ones((c, 128), F32)
        col_sum = lax.dot_general(w, ones, (tn_dims, ((), ())), preferred_element_type=F32, precision=_HIGH)[:, 0:1]
        dgam = dgam + rs(w) - col_sum + rs(dqg * qg) - rs(dkt * kt)
        dk_ref[...] = dk + dkt * jnp.exp(gam_l - gam)
        dgam_l = dgam_l + tot(dkt * kt)
        rowc = lax.broadcasted_iota(jnp.int32, (c, 1), 0)
        dgam = dgam + jnp.where(rowc == c - 1, dgam_l, 0.0)
        tri_u = (row <= col).astype(F32)
        dg = _dot_hi(tri_u, jnp.broadcast_to(dgam, (c, 128)))[:, 0:1]
        dgb_ref[...] += jnp.where(lane == h, dg, 0.0) + jnp.where(lane == h + DN_HEADS, dbeta, 0.0)
        ds_ref[h] = ds

    blk = pl.BlockSpec((c, dh), lambda ci, h: (nc - 1 - ci, h))
    sav = pl.BlockSpec((1, 1, dh, dh), lambda ci, h: (nc - 1 - ci, h, 0, 0))
    gspec = pl.BlockSpec((c, 128), lambda ci, h: (nc - 1 - ci, 0))
    return _pallas(
        body, name=name, grid=(nc, DN_HEADS),
        in_specs=[blk, blk, blk, gspec, pl.BlockSpec((16, c), lambda ci, h: (0, nc - 1 - ci)), sav, sav, blk],
        out_specs=[blk, blk, blk, gspec],
        out_shape=[jax.ShapeDtypeStruct((r, DN_WIDTH), F32)] * 3 + [jax.ShapeDtypeStruct((r, 128), F32)],
        scratch_shapes=[pltpu.VMEM((DN_HEADS, dh, dh), F32)],
        compiler_params=_cparams(("arbitrary", "arbitrary")),
    )(q, k, v, gb, gbt, ssave, tsave, do)


def _dn_post_fwd(o, proj, g, name):
    r = o.shape[0]

    def body(o_ref, z_ref, g_ref, y_ref):
        g_ = g_ref[...]
        for hd in range(DN_HEADS):
            sl = slice(hd * 128, (hd + 1) * 128)
            sz, _ = _silu(z_ref[:, sl])
            y_ref[:, sl] = (_rms(o_ref[:, sl], g_) * sz).astype(BF16)

    return _pallas(body, name=name, grid=(r // ROW_TILE,),
                   in_specs=[_row_spec(DN_WIDTH), pl.BlockSpec((ROW_TILE, DN_WIDTH), lambda i: (i, 3)), _vec_spec(128)],
                   out_specs=_row_spec(DN_WIDTH), out_shape=jax.ShapeDtypeStruct((r, DN_WIDTH), BF16),
                   compiler_params=_cparams(("parallel",)))(o, proj, g)


def _dn_post_bwd(o, proj, g, dy, name):
    r = o.shape[0]

    def body(o_ref, z_ref, g_ref, dy_ref, do_ref, dz_ref, dg_ref):
        @pl.when(pl.program_id(0) == 0)
        def _():
            dg_ref[...] = jnp.zeros_like(dg_ref)

        g_ = g_ref[...]
        for hd in range(DN_HEADS):
            sl = slice(hd * 128, (hd + 1) * 128)
            z_ = z_ref[:, sl]
            sz, sg = _silu(z_)
            dy_ = dy_ref[:, sl]
            o_ = o_ref[:, sl]
            dz_ref[:, sl] = dy_ * _rms(o_, g_) * (sg * (1.0 + z_ * (1.0 - sg)))
            dx, dg = _rms_bwd(o_, g_, dy_ * sz)
            do_ref[:, sl] = dx
            dg_ref[...] += dg

    return _pallas(body, name=name, grid=(r // ROW_TILE,),
                   in_specs=[_row_spec(DN_WIDTH), pl.BlockSpec((ROW_TILE, DN_WIDTH), lambda i: (i, 3)), _vec_spec(128),
                             _row_spec(DN_WIDTH)],
                   out_specs=[_row_spec(DN_WIDTH), _row_spec(DN_WIDTH), _vec_spec(128)],
                   out_shape=[jax.ShapeDtypeStruct((r, DN_WIDTH), F32)] * 2 + [jax.ShapeDtypeStruct((1, 128), F32)],
                   compiler_params=_cparams(("arbitrary",)))(o, proj, g, dy)


def _exchange(arrays, scatter, name):
    n = len(arrays)
    outs_shape = [jax.ShapeDtypeStruct((N_DEV,) + (a.shape[1:] if sc else a.shape), a.dtype) for a, sc in zip(arrays, scatter)]

    def body(*refs):
        in_refs, out_refs = refs[:n], refs[n:2 * n]
        send_sems, recv_sems, local_sems = refs[2 * n:]
        mx, my, mc = lax.axis_index("x"), lax.axis_index("y"), lax.axis_index("c")
        me = 4 * mx + 2 * my + mc
        started = []
        for a in range(n):
            src_own = in_refs[a].at[me] if scatter[a] else in_refs[a]
            loc = pltpu.make_async_copy(src_own, out_refs[a].at[me], local_sems.at[a])
            loc.start()
            started.append(loc)
        remote = []
        for a in range(n):
            for kbits in range(1, N_DEV):
                px = lax.rem(mx + ((kbits >> 2) & 1), 2)
                py = lax.rem(my + ((kbits >> 1) & 1), 2)
                pc = lax.rem(mc + (kbits & 1), 2)
                src = in_refs[a].at[4 * px + 2 * py + pc] if scatter[a] else in_refs[a]
                cp = pltpu.make_async_remote_copy(
                    src_ref=src, dst_ref=out_refs[a].at[me],
                    send_sem=send_sems.at[a * N_DEV + kbits], recv_sem=recv_sems.at[a * N_DEV + kbits],
                    device_id=(px, py, pc), device_id_type=pl.DeviceIdType.MESH)
                cp.start()
                remote.append(cp)
        for cp in remote:
            cp.wait()
        for loc in started:
            loc.wait()

    hbm = pl.BlockSpec(memory_space=pl.ANY)
    return _pallas(
        body, name=name, in_specs=[hbm] * n, out_specs=[hbm] * n, out_shape=outs_shape,
        scratch_shapes=[pltpu.SemaphoreType.DMA((n * N_DEV,)), pltpu.SemaphoreType.DMA((n * N_DEV,)),
                        pltpu.SemaphoreType.DMA((n,))],
    )(*arrays)


def _adamw(gstack, w, m, v, name):
    a, b = w.shape
    ta = a
    for t in (1024, 512, 256, 128, 64, 32, 16, 8):
        if a % t == 0 and N_DEV * t * b * 4 <= 4 * 1024 * 1024:
            ta = t
            break
    c1 = 1.0 / (1.0 - ADAM_B1 ** ADAM_STEP)
    c2 = 1.0 / (1.0 - ADAM_B2 ** ADAM_STEP)

    def body(g_ref, w_ref, m_ref, v_ref, og_ref, od_ref, om_ref, ov_ref):
        g = g_ref[0]
        for s in range(1, N_DEV):
            g = g + g_ref[s]
        m_new = ADAM_B1 * m_ref[...] + (1.0 - ADAM_B1) * g
        v_new = ADAM_B2 * v_ref[...] + (1.0 - ADAM_B2) * (g * g)
        og_ref[...] = g
        om_ref[...] = m_new
        ov_ref[...] = v_new
        od_ref[...] = -ADAM_LR * ((m_new * c1) / (jnp.sqrt(v_new * c2) + ADAM_EPS) + ADAM_WD * w_ref[...])

    spec = pl.BlockSpec((ta, b), lambda i: (i, 0))
    return _pallas(
        body, name=name, grid=(a // ta,),
        in_specs=[pl.BlockSpec((N_DEV, ta, b), lambda i: (0, i, 0)), spec, spec, spec],
        out_specs=[spec] * 4, out_shape=[jax.ShapeDtypeStruct((a, b), F32)] * 4,
        compiler_params=_cparams(("parallel",)),
    )(gstack, w, m, v)


_WEIGHTS = ['meta_tokens', 'pre_mix_norm', 'post_mix_norm', 'pre_mlp_norm', 'post_mlp_norm', 'mlp_w1', 'mlp_w2',
            'w_in_even', 'w_out_even', 'sb_out_norm', 's5_lambda_re', 's5_lambda_im', 's5_log_dt', 's5_b_re', 's5_b_im',
            's5_c_re', 's5_c_im', 's5_d', 's5_w_glu', 's5_b_glu', 's5_out_norm', 'w_in_odd', 'dn_conv_w', 'dn_a_log',
            'dn_dt_bias', 'dn_out_norm', 'w_out_odd']
_SHARDED = ['meta_tokens', 'mlp_w1', 'mlp_w2', 'w_in_even', 'w_out_even', 's5_w_glu', 'w_in_odd', 'dn_conv_w', 'w_out_odd']
_SMALL = [n for n in _WEIGHTS if n not in _SHARDED]


def _view2d(name, a):
    return a.reshape(-1, a.shape[-1])


def _unshard(name, g):
    if name == 'mlp_w1':
        return g.reshape(N_DEV, 2, D_MODEL, -1).transpose(1, 2, 0, 3).reshape(2, D_MODEL, D_FF)
    if name == 'mlp_w2':
        return g.reshape(N_DEV, 2, -1, D_MODEL).transpose(1, 0, 2, 3).reshape(2, D_FF, D_MODEL)
    if name in ('w_in_even', 'w_in_odd', 'dn_conv_w', 'meta_tokens'):
        return g.transpose(1, 0, 2).reshape(g.shape[1], -1)
    return g.reshape(-1, g.shape[-1])


def _to_blocks(name, full):
    if name == 'mlp_w1':
        return full.reshape(2, D_MODEL, N_DEV, -1).transpose(2, 0, 1, 3).reshape(N_DEV, 2 * D_MODEL, -1)
    if name == 'mlp_w2':
        return full.reshape(2, N_DEV, -1, D_MODEL).transpose(1, 0, 2, 3).reshape(N_DEV, -1, D_MODEL)
    if name in ('w_in_even', 'w_in_odd', 'dn_conv_w', 'meta_tokens'):
        return full.reshape(full.shape[0], N_DEV, -1).transpose(1, 0, 2)
    return full.reshape(N_DEV, -1, full.shape[-1])


def _pack(parts):
    rows = []
    for p in parts:
        flat = p.reshape(-1)
        rows.append(jnp.pad(flat, (0, (-flat.shape[0]) % 128)).reshape(-1, 128))
    return jnp.concatenate(rows, axis=0)


def _unpack(packed, like):
    out, at = [], 0
    for p in like:
        n = math.prod(p.shape)
        nrow = -(-n // 128)
        out.append(packed[at:at + nrow].reshape(-1)[:n].reshape(p.shape))
        at += nrow
    return out


def _lane_vec(x, width=128):
    flat = x.reshape(-1)
    return jnp.pad(flat, (0, width - flat.shape[0])).reshape(1, width)


def kernel(x, meta_tokens, pre_mix_norm, post_mix_norm, pre_mlp_norm, post_mlp_norm, mlp_w1, mlp_w2, w_in_even, w_out_even, sb_out_norm, s5_lambda_re, s5_lambda_im, s5_log_dt, s5_b_re, s5_b_im, s5_c_re, s5_c_im, s5_d, s5_w_glu, s5_b_glu, s5_out_norm, w_in_odd, dn_conv_w, dn_a_log, dn_dt_bias, dn_out_norm, w_out_odd, loss_target, m_meta_tokens, m_pre_mix_norm, m_post_mix_norm, m_pre_mlp_norm, m_post_mlp_norm, m_mlp_w1, m_mlp_w2, m_w_in_even, m_w_out_even, m_sb_out_norm, m_s5_lambda_re, m_s5_lambda_im, m_s5_log_dt, m_s5_b_re, m_s5_b_im, m_s5_c_re, m_s5_c_im, m_s5_d, m_s5_w_glu, m_s5_b_glu, m_s5_out_norm, m_w_in_odd, m_dn_conv_w, m_dn_a_log, m_dn_dt_bias, m_dn_out_norm, m_w_out_odd, v_meta_tokens, v_pre_mix_norm, v_post_mix_norm, v_pre_mlp_norm, v_post_mlp_norm, v_mlp_w1, v_mlp_w2, v_w_in_even, v_w_out_even, v_sb_out_norm, v_s5_lambda_re, v_s5_lambda_im, v_s5_log_dt, v_s5_b_re, v_s5_b_im, v_s5_c_re, v_s5_c_im, v_s5_d, v_s5_w_glu, v_s5_b_glu, v_s5_out_norm, v_w_in_odd, v_dn_conv_w, v_dn_a_log, v_dn_dt_bias, v_dn_out_norm, v_w_out_odd):
    given = dict(locals())
    w = {n: given[n] for n in _WEIGHTS}
    mom_m = {n: given["m_" + n] for n in _WEIGHTS}
    mom_v = {n: given["v_" + n] for n in _WEIGHTS}

    seq = x.shape[1]
    assert x.shape[0] == 1 and seq % ROW_TILE == 0
    r = seq + ROW_TILE
    pad = ROW_TILE - N_META
    pad_tiles = 1

    wire = {n: (F32 if n in ('dn_conv_w', 'meta_tokens') else BF16) for n in _SHARDED}
    gathered = _exchange([_view2d(n, w[n]).astype(wire[n]) for n in _SHARDED], [False] * len(_SHARDED), "gather_weights")
    full = {n: _unshard(n, g) for n, g in zip(_SHARDED, gathered)}
    w1, w2 = full['mlp_w1'], full['mlp_w2']
    w_ie, w_oe, w_glu, w_oo = full['w_in_even'], full['w_out_even'], full['s5_w_glu'], full['w_out_odd']
    w_io = full['w_in_odd'][:, :4 * DN_WIDTH]
    w_ab = jnp.pad(full['w_in_odd'][:, 4 * DN_WIDTH:], ((0, 0), (0, 128 - 2 * DN_HEADS)))
    conv_w = full['dn_conv_w']
    row = lambda v_: v_.reshape(1, -1)

    hs0 = jnp.concatenate([jnp.zeros((pad, D_MODEL), F32), full['meta_tokens'], x[0]], axis=0)
    hn0 = _norm_pre(hs0, row(pre_mix_norm[0]), "pre_mix_0")
    qkv = _mm_fwd(hn0, w_ie[:, :3 * SB_WIDTH], "in_even_qkv", out_dtypes=(BF16,))
    u = _mm_fwd(hn0, w_ie[:, 3 * SB_WIDTH:], "in_even_u")
    q, k, v = qkv[:, :SB_WIDTH], qkv[:, SB_WIDTH:2 * SB_WIDTH], qkv[:, 2 * SB_WIDTH:]
    nb = r // ATT_BLK
    blocks_t = lambda t_: t_.reshape(nb, ATT_BLK, 4, 128).transpose(2, 0, 3, 1)
    o_sb, ssave = _sb_fwd(q, k, blocks_t(v), pad, "sb_fwd")
    on_sb = _norm_pre(o_sb, row(sb_out_norm[0]), "sb_out_norm")

    lam_re, lam_im, logdt, btr, bti, ctr, cti, s5_mask = _s5_expand(
        s5_lambda_re[0], s5_lambda_im[0], s5_log_dt[0], s5_b_re[0], s5_b_im[0], s5_c_re[0], s5_c_im[0])
    a_re, a_im, bbr, bbi = _s5_prep(lam_re, lam_im, logdt, btr, bti, "s5_prep")
    s5_wb = jnp.stack([_s5_block_diag_b(bbr, s5_mask), _s5_block_diag_b(bbi, s5_mask)]).astype(BF16)
    s5_wc = jnp.stack([_s5_block_diag_c(ctr, s5_mask), _s5_block_diag_c(cti, s5_mask)]).astype(BF16)
    s5_a = jnp.stack([a_re, a_im])
    s5_args = (s5_wb, s5_a, s5_wc, row(s5_d[0]), w_glu, row(s5_b_glu[0]), row(s5_out_norm[0]))
    y_s5, on_s5, xstart = _s5_fwd(u, *s5_args, "s5_fwd")

    merged = jnp.concatenate([on_sb, on_s5], axis=1)
    mix0 = _mm_fwd(merged, w_oe, "out_even")
    hs1, hn1 = _norm_post_pre(hs0, mix0, row(post_mix_norm[0]), row(pre_mlp_norm[0]), "post_mix_0")
    relu2 = lambda acc: (jnp.square(jnp.maximum(acc, 0.0)), jnp.maximum(acc, 0.0))
    r0, ra0 = _mm_fwd(hn1, w1[0], "mlp_up_0", out_dtypes=(BF16, BF16), epilogue=relu2)
    m0 = _mm_fwd(r0, w2[0], "mlp_down_0")
    hs2, hn2 = _norm_post_pre(hs1, m0, row(post_mlp_norm[0]), row(pre_mix_norm[1]), "post_mlp_0")

    proj = _mm_fwd(hn2, w_io, "in_odd")
    ab = _mm_fwd(hn2, w_ab, "in_odd_gates")
    alog, dtb = _lane_vec(dn_a_log[0]), _lane_vec(dn_dt_bias[0])
    co, qd, kd, vd, gb = _dn_pre_fwd(proj, ab, conv_w, alog, dtb, pad, "dn_pre")
    gbt = gb[:, :2 * DN_HEADS].T
    o_dn, s_dn, t_dn = _dn_fwd(qd, kd, vd, gb, gbt, "dn_fwd")
    on_dn = _dn_post_fwd(o_dn, proj, row(dn_out_norm[0]), "dn_post")
    mix1 = _mm_fwd(on_dn, w_oo, "out_odd")
    hs3, hn3 = _norm_post_pre(hs2, mix1, row(post_mix_norm[1]), row(pre_mlp_norm[1]), "post_mix_1")
    r1, ra1 = _mm_fwd(hn3, w1[1], "mlp_up_1", out_dtypes=(BF16, BF16), epilogue=relu2)
    m1 = _mm_fwd(r1, w2[1], "mlp_down_1")
    dhs, loss_part = _norm_post_loss(hs3, m1, row(post_mlp_norm[1]), loss_target[0], pad_tiles, "post_mlp_1_loss")
    loss = lax.psum(loss_part, ("x", "y", "c"))

    g = {}
    drelu2 = lambda acc, ra: (acc * (2.0 * ra.astype(F32)),)

    def mlp_bwd(layer, hn, rr, ra, dm):
        dw2 = _mm_wgrad(rr, dm, f"mlp_down_{layer}_wgrad")
        da = _mm_dgrad(dm, w2[layer], f"mlp_down_{layer}_dgrad", out_dtypes=(BF16,), extras=(ra,), epilogue=drelu2)
        dw1 = _mm_wgrad(hn, da, f"mlp_up_{layer}_wgrad")
        return dw1, dw2, _mm_dgrad(da, w1[layer], f"mlp_up_{layer}_dgrad")

    _, dm1, _, dg_post_mlp1 = _norm_bwd(dhs, post=(m1, row(post_mlp_norm[1])), pad=pad, name="post_mlp_1_bwd")
    dw1_1, dw2_1, dhn3 = mlp_bwd(1, hn3, r1, ra1, dm1)
    dhs, dmix1, dg_pre_mlp1, dg_post_mix1 = _norm_bwd(
        dhs, pre=(hs3, row(pre_mlp_norm[1]), dhn3), post=(mix1, row(post_mix_norm[1])), pad=pad, name="post_mix_1_bwd")

    g['w_out_odd'] = _mm_wgrad(on_dn, dmix1, "out_odd_wgrad")
    d_on_dn = _mm_dgrad(dmix1, w_oo, "out_odd_dgrad")
    do_dn, dz, dg_dn = _dn_post_bwd(o_dn, proj, row(dn_out_norm[0]), d_on_dn, "dn_post_bwd")
    dqd, dkd, dvd, dgb = _dn_bwd(qd, kd, vd, gb, gbt, s_dn, t_dn, do_dn, "dn_bwd")
    dco, dab, d_alog, d_dtb = _dn_pre_bwd(co, dqd, dkd, dvd, dgb, ab, alog, dtb, pad, "dn_pre_bwd")
    dpre, d_conv = _dn_conv_bwd(dco, proj, conv_w, "dn_conv_bwd")
    dproj = jnp.concatenate([dpre, dz], axis=1)
    g['w_in_odd'] = jnp.concatenate([_mm_wgrad(hn2, dproj, "in_odd_wgrad"),
                                     _mm_wgrad(hn2, dab, "in_odd_gates_wgrad")[:, :2 * DN_HEADS]], axis=1)
    dhn2 = _mm_dgrad(dab, w_ab, "in_odd_gates_dgrad")
    dhn2 = _mm_dgrad(dproj, w_io, "in_odd_dgrad", extras=(dhn2,), epilogue=lambda acc, other: (acc + other,))
    g['dn_conv_w'] = d_conv[:DN_CONV]
    g['dn_a_log'], g['dn_dt_bias'], g['dn_out_norm'] = d_alog[0, :DN_HEADS], d_dtb[0, :DN_HEADS], dg_dn[0]

    dhs, dm0, dg_pre_mix1, dg_post_mlp0 = _norm_bwd(
        dhs, pre=(hs2, row(pre_mix_norm[1]), dhn2), post=(m0, row(post_mlp_norm[0])), pad=pad, name="post_mlp_0_bwd")
    dw1_0, dw2_0, dhn1 = mlp_bwd(0, hn1, r0, ra0, dm0)
    dhs, dmix0, dg_pre_mlp0, dg_post_mix0 = _norm_bwd(
        dhs, pre=(hs1, row(pre_mlp_norm[0]), dhn1), post=(mix0, row(post_mix_norm[0])), pad=pad, name="post_mix_0_bwd")

    g['w_out_even'] = _mm_wgrad(merged, dmix0, "out_even_wgrad")
    dmerged = _mm_dgrad(dmix0, w_oe, "out_even_dgrad")
    _, do_sb, _, dg_sb = _norm_bwd(dmerged[:, :SB_WIDTH], post=(o_sb, row(sb_out_norm[0])), pad=pad, name="sb_out_norm_bwd")
    dq, dk4, dv4 = _sb_bwd(q, k, v, blocks_t(k), ssave, do_sb, pad, "sb_bwd")
    unheads = lambda t_: t_.transpose(1, 0, 2).reshape(r, SB_WIDTH)
    du, d_a, d_d, d_bglu, dg_s5, d_wb, d_wc, g['s5_w_glu'] = _s5_bwd(u, y_s5, dmerged[:, SB_WIDTH:], xstart, *s5_args, "s5_bwd")
    g_lr, g_li, g_dt, g_btr, g_bti = _s5_prep_bwd(
        lam_re, lam_im, logdt, btr, bti, d_a[0], d_a[1],
        _s5_diag_of_b(d_wb[0], s5_mask), _s5_diag_of_b(d_wb[1], s5_mask), "s5_prep_bwd")
    gg, nn, pp = S5_GROUPS, S5_STATE, S5_GROUP
    g['s5_lambda_re'], g['s5_lambda_im'] = g_lr.reshape(gg, nn), g_li.reshape(gg, nn)
    g['s5_log_dt'] = g_dt.reshape(gg, nn)[:, 0]
    g['s5_b_re'], g['s5_b_im'] = g_btr.T.reshape(gg, nn, pp), g_bti.T.reshape(gg, nn, pp)
    g['s5_c_re'] = _s5_diag_of_c(d_wc[0], s5_mask).reshape(gg, nn, pp).transpose(0, 2, 1)
    g['s5_c_im'] = _s5_diag_of_c(d_wc[1], s5_mask).reshape(gg, nn, pp).transpose(0, 2, 1)
    g['s5_d'], g['s5_b_glu'], g['s5_out_norm'], g['sb_out_norm'] = d_d[0], d_bglu[0], dg_s5[0], dg_sb[0]
    dqkvu = jnp.concatenate([dq, unheads(dk4), unheads(dv4), du], axis=1)
    g['w_in_even'] = _mm_wgrad(hn0, dqkvu, "in_even_wgrad")
    dhn0 = _mm_dgrad(dqkvu, w_ie, "in_even_dgrad")
    dhs, _, dg_pre_mix0, _ = _norm_bwd(dhs, pre=(hs0, row(pre_mix_norm[0]), dhn0), pad=pad, name="pre_mix_0_bwd")

    g['mlp_w1'] = jnp.stack([dw1_0, dw1_1])
    g['mlp_w2'] = jnp.stack([dw2_0, dw2_1])
    g['meta_tokens'] = dhs[pad:pad + N_META]
    g['pre_mix_norm'] = jnp.concatenate([dg_pre_mix0, dg_pre_mix1], axis=0)
    g['post_mix_norm'] = jnp.concatenate([dg_post_mix0, dg_post_mix1], axis=0)
    g['pre_mlp_norm'] = jnp.concatenate([dg_pre_mlp0, dg_pre_mlp1], axis=0)
    g['post_mlp_norm'] = jnp.concatenate([dg_post_mlp0, dg_post_mlp1], axis=0)
    grad_x = dhs[pad + N_META:][None]

    small_like = [w[n] for n in _SMALL]
    partial = [_to_blocks(n, g[n].reshape(full[n].shape)) for n in _SHARDED]
    partial.append(_pack([g[n].reshape(w[n].shape) for n in _SMALL]))
    stacks = _exchange(partial, [True] * len(_SHARDED) + [False], "reduce_gradients")
    grads, deltas, new_m, new_v = {}, {}, {}, {}
    for n, st in zip(_SHARDED, stacks):
        outs = _adamw(st, _view2d(n, w[n]), _view2d(n, mom_m[n]), _view2d(n, mom_v[n]), f"adamw_{n}")
        grads[n], deltas[n], new_m[n], new_v[n] = (o.reshape(w[n].shape) for o in outs)
    outs = _adamw(stacks[-1], _pack(small_like), _pack([mom_m[n] for n in _SMALL]), _pack([mom_v[n] for n in _SMALL]),
                  "adamw_small")
    for dst, o in zip((grads, deltas, new_m, new_v), outs):
        for n, part in zip(_SMALL, _unpack(o, small_like)):
            dst[n] = part
    return (loss, grad_x, *[grads[n] for n in _WEIGHTS], *[deltas[n] for n in _WEIGHTS],
            *[new_m[n] for n in _WEIGHTS], *[new_v[n] for n in _WEIGHTS])
```

```python
import functools
import math

import jax
import jax.numpy as jnp
from jax import lax
from jax.experimental import pallas as pl
from jax.experimental.pallas import tpu as pltpu

F32 = jnp.float32
BF16 = jnp.bfloat16

D_MODEL = 1024
N_META = 16
SB_HEAD_DIM = 64
SB_WIDTH = 512
S5_WIDTH = 512
S5_GROUP = 16
S5_GROUPS = 32
S5_STATE = 64
S5_NS = S5_GROUPS * S5_STATE
DN_HEAD_DIM = 128
DN_HEADS = 8
DN_WIDTH = 1024
DN_CONV = 4
D_FF = 4096
EPS = 1e-6
N_DEV = 8

ADAM_LR = 0.001
ADAM_B1 = 0.9
ADAM_B2 = 0.999
ADAM_EPS = 1e-08
ADAM_WD = 0.01
ADAM_STEP = 10

ROW_TILE = 512
ATT_BLK = 256
SB_BLOCKS_PER_TRIP = 3
SB_FWD_SKEW = False
SB_BWD_SKEW = True
DN_CHUNK = 128
DN_SUB = 16
S5_TILE = 128
VMEM_LIMIT = 56 * 1024 * 1024

_HIGH = lax.Precision.HIGHEST


def _pallas(body, **kw):
    return pl.pallas_call(body, **kw)


def _cparams(sem):
    return pltpu.CompilerParams(dimension_semantics=sem, vmem_limit_bytes=VMEM_LIMIT)


def _dot(a, b, dims=((1,), (0,))):
    return lax.dot_general(a, b, (dims, ((), ())), preferred_element_type=F32)


def _dot_hi(a, b):
    return lax.dot_general(a, b, (((1,), (0,)), ((), ())), preferred_element_type=F32, precision=_HIGH)


def _split_dot(m_bf16, x):
    hi = x.astype(BF16)
    lo = (x - hi.astype(F32)).astype(BF16)
    return _dot(m_bf16, hi) + _dot(m_bf16, lo)


def _matmul(a, b, *, ta=False, tb=False, tm, tn, tk, name, out_dtypes=(F32,), extras=(), epilogue=None):
    m, k = (a.shape[1], a.shape[0]) if ta else a.shape
    n = b.shape[0] if tb else b.shape[1]
    assert (b.shape[1] if tb else b.shape[0]) == k
    assert m % tm == 0 and n % tn == 0 and k % tk == 0, (name, m, n, k, tm, tn, tk)
    nk = k // tk
    n_ex = len(extras)
    n_out = len(out_dtypes)
    dims = ((0 if ta else 1,), (1 if tb else 0,))

    def finish(acc, ex_refs, o_refs):
        outs = (acc,) if epilogue is None else epilogue(acc, *[r[...] for r in ex_refs])
        for o_ref, o in zip(o_refs, outs):
            o_ref[...] = o.astype(o_ref.dtype)

    def body(*refs):
        a_ref, b_ref = refs[0], refs[1]
        ex_refs = refs[2:2 + n_ex]
        o_refs = refs[2 + n_ex:2 + n_ex + n_out]
        prod = _dot(a_ref[...].astype(BF16), b_ref[...].astype(BF16), dims)
        if nk == 1:
            finish(prod, ex_refs, o_refs)
            return
        acc_ref = refs[-1]
        kk = pl.program_id(2)

        @pl.when(kk == 0)
        def _():
            acc_ref[...] = prod

        @pl.when(kk > 0)
        def _():
            acc_ref[...] += prod

        @pl.when(kk == nk - 1)
        def _():
            finish(acc_ref[...], ex_refs, o_refs)

    a_spec = pl.BlockSpec((tk, tm), lambda j, i, kk: (kk, i)) if ta else pl.BlockSpec((tm, tk), lambda j, i, kk: (i, kk))
    b_spec = pl.BlockSpec((tn, tk), lambda j, i, kk: (j, kk)) if tb else pl.BlockSpec((tk, tn), lambda j, i, kk: (kk, j))
    o_spec = pl.BlockSpec((tm, tn), lambda j, i, kk: (i, j))
    outs = _pallas(
        body, name=name,
        grid=(n // tn, m // tm, nk),
        in_specs=[a_spec, b_spec] + [o_spec] * n_ex,
        out_specs=[o_spec] * n_out,
        out_shape=[jax.ShapeDtypeStruct((m, n), dt) for dt in out_dtypes],
        scratch_shapes=[] if nk == 1 else [pltpu.VMEM((tm, tn), F32)],
        compiler_params=_cparams(("parallel", "parallel", "arbitrary")),
    )(a, b, *extras)
    return outs[0] if n_out == 1 else outs


def _tile(n, cap):
    best = 128
    for t in range(128, min(n, cap) + 1, 128):
        if n % t == 0:
            best = t
    assert n % best == 0, n
    return best


MM_K_CAP = 4096
WGRAD_ROWS = 1536


def _mm_fwd(x, w, name, **kw):
    k, n = w.shape
    return _matmul(x, w, tm=ROW_TILE, tn=_tile(n, 1024), tk=_tile(k, MM_K_CAP), name=name, **kw)


def _mm_dgrad(dy, w, name, **kw):
    k, n = w.shape
    return _matmul(dy, w, tb=True, tm=ROW_TILE, tn=_tile(k, 1024), tk=_tile(n, MM_K_CAP), name=name, **kw)


def _mm_wgrad(x, dy, name):
    k, n = x.shape[1], dy.shape[1]
    rows = x.shape[0]
    return _matmul(x, dy, ta=True, tm=_tile(k, 512), tn=_tile(n, 1024),
                   tk=WGRAD_ROWS if rows % WGRAD_ROWS == 0 else ROW_TILE, name=name)


def _rms(x, g):
    r = lax.rsqrt(jnp.mean(x * x, axis=-1, keepdims=True) + EPS)
    return x * r * g


def _rms_bwd(x, g, dy):
    r = lax.rsqrt(jnp.mean(x * x, axis=-1, keepdims=True) + EPS)
    xh = x * r
    dxh = dy * g
    dx = r * (dxh - xh * jnp.mean(dxh * xh, axis=-1, keepdims=True))
    dg = jnp.sum(dy * xh, axis=0, keepdims=True)
    return dx, dg


def _row_spec(width, tile=ROW_TILE):
    return pl.BlockSpec((tile, width), lambda i: (i, 0))


def _vec_spec(width):
    return pl.BlockSpec((1, width), lambda i: (0, 0))


def _norm_pre(hs, g, name):
    r, d = hs.shape

    def body(x_ref, g_ref, o_ref):
        o_ref[...] = _rms(x_ref[...], g_ref[...]).astype(BF16)

    return _pallas(body, name=name, grid=(r // ROW_TILE,), in_specs=[_row_spec(d), _vec_spec(d)],
                   out_specs=_row_spec(d), out_shape=jax.ShapeDtypeStruct((r, d), BF16),
                   compiler_params=_cparams(("parallel",)))(hs, g)


def _norm_post_pre(hs, m, g_post, g_pre, name):
    r, d = hs.shape

    def body(hs_ref, m_ref, gp_ref, gn_ref, o_ref, hn_ref):
        new = hs_ref[...] + _rms(m_ref[...], gp_ref[...])
        o_ref[...] = new
        hn_ref[...] = _rms(new, gn_ref[...]).astype(BF16)

    return _pallas(body, name=name, grid=(r // ROW_TILE,),
                   in_specs=[_row_spec(d), _row_spec(d), _vec_spec(d), _vec_spec(d)],
                   out_specs=[_row_spec(d), _row_spec(d)],
                   out_shape=[jax.ShapeDtypeStruct((r, d), F32), jax.ShapeDtypeStruct((r, d), BF16)],
                   compiler_params=_cparams(("parallel",)))(hs, m, g_post, g_pre)


def _norm_post_loss(hs, m, g_post, target, pad_tiles, name):
    r, d = hs.shape
    nt = r // ROW_TILE

    def body(hs_ref, m_ref, gp_ref, t_ref, dhs_ref, loss_ref):
        i = pl.program_id(0)
        new = hs_ref[...] + _rms(m_ref[...], gp_ref[...])
        live = (i >= pad_tiles).astype(F32)
        diff = (new - t_ref[...]) * live
        dhs_ref[...] = diff * (1.0 / d)
        loss_ref[...] = jnp.full((8, 128), 0.5 / d * jnp.sum(diff * diff), F32)

    dhs, parts = _pallas(
        body, name=name, grid=(nt,),
        in_specs=[_row_spec(d), _row_spec(d), _vec_spec(d),
                  pl.BlockSpec((ROW_TILE, d), lambda i: (jnp.maximum(i - pad_tiles, 0), 0))],
        out_specs=[_row_spec(d), pl.BlockSpec((8, 128), lambda i: (i, 0))],
        out_shape=[jax.ShapeDtypeStruct((r, d), F32), jax.ShapeDtypeStruct((nt * 8, 128), F32)],
        compiler_params=_cparams(("parallel",)))(hs, m, g_post, target)
    return dhs, jnp.sum(parts[::8, 0])


def _norm_bwd(dhs, *, pre=None, post=None, pad=0, name):
    r, d = dhs.shape
    has_pre, has_post = pre is not None, post is not None

    def body(*refs):
        it = iter(refs)
        dhs_ref = next(it)
        if has_pre:
            hs_ref, gn_ref, dhn_ref = next(it), next(it), next(it)
        if has_post:
            m_ref, gp_ref = next(it), next(it)
        if has_pre:
            o_dhs, o_dgn = next(it), next(it)
        if has_post:
            o_dm, o_dgp = next(it), next(it)
        i = pl.program_id(0)
        live = (i * ROW_TILE + lax.broadcasted_iota(jnp.int32, (ROW_TILE, 1), 0)) >= pad
        cur = jnp.where(live, dhs_ref[...], 0.0)
        if has_pre:
            dx, dg = _rms_bwd(hs_ref[...], gn_ref[...], jnp.where(live, dhn_ref[...].astype(F32), 0.0))
            cur = cur + dx
            o_dhs[...] = cur

            @pl.when(i == 0)
            def _():
                o_dgn[...] = jnp.zeros_like(o_dgn)
            o_dgn[...] += dg
        if has_post:
            dm, dg = _rms_bwd(m_ref[...], gp_ref[...], cur)
            o_dm[...] = dm

            @pl.when(i == 0)
            def _():
                o_dgp[...] = jnp.zeros_like(o_dgp)
            o_dgp[...] += dg

    ins, in_specs, out_specs, out_shape = [dhs], [_row_spec(d)], [], []
    if has_pre:
        ins += list(pre)
        in_specs += [_row_spec(d), _vec_spec(d), _row_spec(d)]
        out_specs += [_row_spec(d), _vec_spec(d)]
        out_shape += [jax.ShapeDtypeStruct((r, d), F32), jax.ShapeDtypeStruct((1, d), F32)]
    if has_post:
        ins += list(post)
        in_specs += [_row_spec(d), _vec_spec(d)]
        out_specs += [_row_spec(d), _vec_spec(d)]
        out_shape += [jax.ShapeDtypeStruct((r, d), F32), jax.ShapeDtypeStruct((1, d), F32)]
    outs = list(_pallas(body, name=name, grid=(r // ROW_TILE,), in_specs=in_specs, out_specs=out_specs,
                        out_shape=out_shape, compiler_params=_cparams(("arbitrary",)))(*ins))
    dhs_new, dgn = (outs.pop(0), outs.pop(0)) if has_pre else (dhs, None)
    dm, dgp = (outs.pop(0), outs.pop(0)) if has_post else (None, None)
    return dhs_new, dm, dgn, dgp


def _softplus(z):
    return jnp.maximum(z, 0.0) + jnp.log(1.0 + jnp.exp(-jnp.abs(z)))


def _sb_consts(t):
    row = lax.broadcasted_iota(jnp.int32, (t, t), 0)
    col = lax.broadcasted_iota(jnp.int32, (t, t), 1)
    m_up = (col >= row).astype(BF16)
    m_low = (col <= row).astype(BF16)
    return m_up, m_low


def _emit_chains(chains, stages, skew):
    if skew:
        for step in range(len(chains) + len(stages) - 1):
            for si, stage in enumerate(stages):
                if 0 <= step - si < len(chains):
                    stage(chains[step - si])
    else:
        for stage in stages:
            for c in chains:
                stage(c)


def _sb_fwd(q, k, vt3, pad, name):
    r = q.shape[0]
    t = ATT_BLK
    nb = r // t
    nbp = -(-nb // 8) * 8
    jmin = pad // t
    scale = SB_HEAD_DIM ** -0.5

    def body(q_ref, k_ref, vt_ref, o_ref, ss_ref, acc_ref):
        i = pl.program_id(1)
        qt = q_ref[...].astype(F32).T
        sub = lax.broadcasted_iota(jnp.int32, (128, 1), 0)
        m_up, _ = _sb_consts(t)
        kpos0 = lax.broadcasted_iota(jnp.int32, (t, 1), 0)
        qpos = i * t + lax.broadcasted_iota(jnp.int32, (1, t), 1)
        n_mid = jnp.maximum(i - 1 - jmin, 0)
        n_edge = jnp.where(i > jmin, 1, 0)
        qths = [jnp.where((sub >= 64 * h) & (sub < 64 * (h + 1)), qt * scale, 0.0).astype(BF16) for h in range(2)]
        acc_ref[...] = jnp.zeros_like(acc_ref)

        def sweep(js, carry, masked):
            kbs = [k_ref[pl.ds(pl.multiple_of(j * t, t), t), :] for j in js]
            vts = [vt_ref[0, j] for j in js]
            accs = [acc_ref[0], acc_ref[1]]
            s = list(carry)
            chains = [(n, h) for n in range(len(js)) for h in range(2)]
            valid = [(js[n] * t + kpos0 < qpos) & (js[n] * t + kpos0 >= pad) for n in range(len(js))] if masked else None
            zt, inc, saves = {}, {}, []

            def st_scores(c):
                zt[c] = _dot(kbs[c[0]], qths[c[1]])

            def st_cumsum(c):
                lk = -_softplus(zt[c])
                if masked:
                    lk = jnp.where(valid[c[0]], lk, 0.0)
                inc[c] = _split_dot(m_up, lk)

            def st_weights(c):
                n, h = c
                saves.append((h, js[n], s[h]))
                w = jnp.exp(zt[c] + inc[c] + s[h])
                if masked:
                    w = jnp.where(valid[n], w, 0.0)
                accs[h] = accs[h] + _dot(vts[n], w.astype(BF16))
                s[h] = s[h] + inc[c][0:1, :]

            _emit_chains(chains, [st_scores, st_cumsum, st_weights], SB_FWD_SKEW)
            for h, j, val in saves:
                ss_ref[h, 0, pl.ds(j, 1), :] = val
            acc_ref[0] = accs[0]
            acc_ref[1] = accs[1]
            return tuple(s)

        zero = jnp.zeros((1, t), F32)
        bpi = SB_BLOCKS_PER_TRIP
        carry = sweep([i], (zero, zero), True)
        carry = lax.fori_loop(0, n_mid // bpi, lambda it, c: sweep([i - 1 - bpi * it - b for b in range(bpi)], c, False), carry)
        n_rem = n_mid % bpi
        carry = lax.fori_loop(0, n_rem, lambda it, c: sweep([jmin + n_rem - it], c, False), carry)
        lax.fori_loop(0, n_edge, lambda it, c: sweep([jmin + it * 0], c, True), carry)
        acc = jnp.where(sub < 64, acc_ref[0], acc_ref[1])
        o_ref[...] = acc.T

    return _pallas(
        body, name=name, grid=(4, nb),
        in_specs=[pl.BlockSpec((t, 128), lambda hp, i: (i, hp)),
                  pl.BlockSpec((r, 128), lambda hp, i: (0, hp)),
                  pl.BlockSpec((1, nb, 128, t), lambda hp, i: (hp, 0, 0, 0))],
        out_specs=[pl.BlockSpec((t, 128), lambda hp, i: (i, hp)),
                   pl.BlockSpec((2, 1, nbp, t), lambda hp, i: (hp, i, 0, 0))],
        out_shape=[jax.ShapeDtypeStruct((r, SB_WIDTH), F32),
                   jax.ShapeDtypeStruct((8, nb, nbp, t), F32)],
        scratch_shapes=[pltpu.VMEM((2, 128, t), F32)],
        compiler_params=_cparams(("parallel", "arbitrary")),
    )(q, k, vt3)


def _sb_bwd(q, k, v, kt3, ssave, do, pad, name):
    r = q.shape[0]
    t = ATT_BLK
    nb = r // t
    nbp = ssave.shape[2]
    jmin = pad // t
    scale = SB_HEAD_DIM ** -0.5

    def body(q_ref, do_ref, k_ref, v_ref, kt_ref, ss_ref, dq_ref, dk_hbm, dv_hbm, dk_acc, dv_acc, dq_acc, sem):
        hp = pl.program_id(0)
        i = pl.program_id(1)

        @pl.when(i == 0)
        def _():
            dk_acc[...] = jnp.zeros_like(dk_acc)
            dv_acc[...] = jnp.zeros_like(dv_acc)

        qf = q_ref[...].astype(F32)
        dof = do_ref[...]
        qt = qf.T
        dot_ = dof.T
        sub = lax.broadcasted_iota(jnp.int32, (128, 1), 0)
        lane = lax.broadcasted_iota(jnp.int32, (1, 128), 1)
        m_up, m_low = _sb_consts(t)
        kpos0 = lax.broadcasted_iota(jnp.int32, (t, 1), 0)
        qpos = i * t + lax.broadcasted_iota(jnp.int32, (1, t), 1)
        n_mid = jnp.maximum(i - 1 - jmin, 0)
        n_edge = jnp.where(i > jmin, 1, 0)
        in_t = [(sub >= 64 * h) & (sub < 64 * (h + 1)) for h in range(2)]
        in_l = [(lane >= 64 * h) & (lane < 64 * (h + 1)) for h in range(2)]
        qths = [jnp.where(in_t[h], qt * scale, 0.0).astype(BF16) for h in range(2)]
        doths = [jnp.where(in_t[h], dot_, 0.0).astype(BF16) for h in range(2)]
        qhs = [jnp.where(in_l[h], qf * scale, 0.0).astype(BF16) for h in range(2)]
        dohs = [jnp.where(in_l[h], dof, 0.0).astype(BF16) for h in range(2)]
        dq_acc[...] = jnp.zeros_like(dq_acc)

        def sweep(js, carry, masked):
            rows = [pl.ds(pl.multiple_of(j * t, t), t) for j in js]
            kbs = [k_ref[rw, :] for rw in rows]
            vbs = [v_ref[rw, :] for rw in rows]
            kts = [kt_ref[0, j] for j in js]
            sss = [[ss_ref[h, 0, pl.ds(j, 1), :] for h in range(2)] for j in js]
            dv_old = [dv_acc[rw, :] for rw in rows]
            dk_old = [dk_acc[rw, :] for rw in rows]
            dqs = [dq_acc[0], dq_acc[1]]
            ec = list(carry)
            chains = [(n, h) for n in range(len(js)) for h in range(2)]
            nch = len(chains)
            valid = [(js[n] * t + kpos0 < qpos) & (js[n] * t + kpos0 >= pad) for n in range(len(js))] if masked else None
            zt, dvt, sp, inc, e, big_e = {}, {}, {}, {}, {}, {}

            def st_scores(c):
                zt[c] = _dot(kbs[c[0]], qths[c[1]])
                dvt[c] = _dot(vbs[c[0]], doths[c[1]])

            def st_cumsum(c):
                sp[c] = _softplus(zt[c])
                lk = -sp[c]
                if masked:
                    lk = jnp.where(valid[c[0]], lk, 0.0)
                inc[c] = _split_dot(m_up, lk)

            def st_weights(c):
                n, h = c
                w = jnp.exp(zt[c] + inc[c] + sss[n][h])
                if masked:
                    w = jnp.where(valid[n], w, 0.0)
                dv_old[n] = dv_old[n] + _dot(w.astype(BF16), dohs[h])
                e[c] = w * dvt[c]
                pinc = _split_dot(m_low, e[c])
                big_e[c] = pinc - e[c] + ec[h]
                ec[h] = ec[h] + pinc[t - 1:t, :]

            def st_dscores(c):
                n, h = c
                dz = e[c] - jnp.exp(zt[c] - sp[c]) * (e[c] + big_e[c])
                if masked:
                    dz = jnp.where(valid[n], dz, 0.0)
                dzb = dz.astype(BF16)
                dqs[h] = dqs[h] + _dot(kts[n], dzb)
                dk_old[n] = dk_old[n] + _dot(dzb, qhs[h])

            _emit_chains(chains, [st_scores, st_cumsum, st_weights, st_dscores], SB_BWD_SKEW)
            for n, rw in enumerate(rows):
                dv_acc[rw, :] = dv_old[n]
                dk_acc[rw, :] = dk_old[n]
            dq_acc[0] = dqs[0]
            dq_acc[1] = dqs[1]
            return tuple(ec)

        zero = jnp.zeros((1, t), F32)
        bpi = SB_BLOCKS_PER_TRIP
        carry = lax.fori_loop(0, n_edge, lambda it, c: sweep([jmin + it * 0], c, True), (zero, zero))
        carry = lax.fori_loop(0, n_mid // bpi, lambda it, c: sweep([jmin + 1 + bpi * it + b for b in range(bpi)], c, False), carry)
        n_rem = n_mid % bpi
        carry = lax.fori_loop(0, n_rem, lambda it, c: sweep([i - n_rem + it], c, False), carry)
        sweep([i], carry, True)
        dq_ref[...] = (jnp.where(sub < 64, dq_acc[0], dq_acc[1]) * scale).T

        @pl.when(i == nb - 1)
        def _():
            c1 = pltpu.make_async_copy(dk_acc, dk_hbm.at[hp], sem.at[0])
            c2 = pltpu.make_async_copy(dv_acc, dv_hbm.at[hp], sem.at[1])
            c1.start()
            c2.start()
            c1.wait()
            c2.wait()

    return _pallas(
        body, name=name, grid=(4, nb),
        in_specs=[pl.BlockSpec((t, 128), lambda hp, i: (i, hp)),
                  pl.BlockSpec((t, 128), lambda hp, i: (i, hp)),
                  pl.BlockSpec((r, 128), lambda hp, i: (0, hp)),
                  pl.BlockSpec((r, 128), lambda hp, i: (0, hp)),
                  pl.BlockSpec((1, nb, 128, t), lambda hp, i: (hp, 0, 0, 0)),
                  pl.BlockSpec((2, 1, nbp, t), lambda hp, i: (hp, i, 0, 0))],
        out_specs=[pl.BlockSpec((t, 128), lambda hp, i: (i, hp)),
                   pl.BlockSpec(memory_space=pl.ANY), pl.BlockSpec(memory_space=pl.ANY)],
        out_shape=[jax.ShapeDtypeStruct((r, SB_WIDTH), F32),
                   jax.ShapeDtypeStruct((4, r, 128), F32), jax.ShapeDtypeStruct((4, r, 128), F32)],
        scratch_shapes=[pltpu.VMEM((r, 128), F32), pltpu.VMEM((r, 128), F32), pltpu.VMEM((2, 128, t), F32),
                        pltpu.SemaphoreType.DMA((2,))],
        compiler_params=_cparams(("arbitrary", "arbitrary")),
    )(q, do, k, v, kt3, ssave)


def _s5_disc(lam_re, lam_im, logdt, btr, bti):
    lr = jnp.minimum(lam_re, -1e-4)
    li = lam_im
    dt = jnp.exp(logdt)
    mag = jnp.exp(lr * dt)
    ang = li * dt
    a_re, a_im = mag * jnp.cos(ang), mag * jnp.sin(ang)
    den = lr * lr + li * li
    nr, ni = a_re - 1.0, a_im
    c_re = (nr * lr + ni * li) / den
    c_im = (ni * lr - nr * li) / den
    return a_re, a_im, c_re * btr - c_im * bti, c_re * bti + c_im * btr


def _s5_prep(lam_re, lam_im, logdt, btr, bti, name):
    ns = lam_re.shape[1]

    def body(lr_ref, li_ref, dt_ref, br_ref, bi_ref, ar_ref, ai_ref, bbr_ref, bbi_ref):
        ar, ai, bbr, bbi = _s5_disc(lr_ref[...], li_ref[...], dt_ref[...], br_ref[...], bi_ref[...])
        ar_ref[...] = ar
        ai_ref[...] = ai
        bbr_ref[...] = bbr
        bbi_ref[...] = bbi

    return _pallas(body, name=name,
                   out_shape=[jax.ShapeDtypeStruct((1, ns), F32)] * 2 + [jax.ShapeDtypeStruct((S5_GROUP, ns), F32)] * 2,
                   )(lam_re, lam_im, logdt, btr, bti)


def _s5_prep_bwd(lam_re, lam_im, logdt, btr, bti, dar, dai, dbbr, dbbi, name):
    ns = lam_re.shape[1]

    def body(lr_ref, li_ref, dt_ref, br_ref, bi_ref, dar_ref, dai_ref, dbr_ref, dbi_ref, o_lr, o_li, o_dt, o_br, o_bi):
        _, vjp = jax.vjp(_s5_disc, lr_ref[...], li_ref[...], dt_ref[...], br_ref[...], bi_ref[...])
        g = vjp((dar_ref[...], dai_ref[...], dbr_ref[...], dbi_ref[...]))
        o_lr[...] = g[0]
        o_li[...] = g[1]
        row = lax.broadcasted_iota(jnp.int32, (ns, ns), 0) // S5_STATE
        col = lax.broadcasted_iota(jnp.int32, (ns, ns), 1) // S5_STATE
        same = (row == col).astype(F32)
        o_dt[...] = _dot_hi(jnp.broadcast_to(g[2], (8, ns)), same)[0:1]
        o_br[...] = g[3]
        o_bi[...] = g[4]

    return _pallas(body, name=name,
                   out_shape=[jax.ShapeDtypeStruct((1, ns), F32)] * 3 + [jax.ShapeDtypeStruct((S5_GROUP, ns), F32)] * 2,
                   compiler_params=pltpu.CompilerParams(vmem_limit_bytes=VMEM_LIMIT),
                   )(lam_re, lam_im, logdt, btr, bti, dar, dai, dbbr, dbbi)


def _s5_scan(br, bi, ar, ai, t, reverse=False):
    row = lax.broadcasted_iota(jnp.int32, (t, 1), 0)
    pr, pi_ = ar, ai
    k = 1
    while k < t:
        if reverse:
            sr, si, ok = pltpu.roll(br, t - k, 0), pltpu.roll(bi, t - k, 0), row < t - k
        else:
            sr, si, ok = pltpu.roll(br, k, 0), pltpu.roll(bi, k, 0), row >= k
        sr = jnp.where(ok, sr, 0.0)
        si = jnp.where(ok, si, 0.0)
        br, bi = br + pr * sr - pi_ * si, bi + pr * si + pi_ * sr
        pr, pi_ = pr * pr - pi_ * pi_, 2.0 * pr * pi_
        k *= 2
    return br, bi


def _s5_power_table(ar, ai, t, reverse=False):
    row = lax.broadcasted_iota(jnp.int32, (t, 1), 0)
    hot = row == (t - 1 if reverse else 0)
    return _s5_scan(jnp.where(hot, ar, 0.0), jnp.where(hot, ai, 0.0), ar, ai, t, reverse)


_GELU_C = math.sqrt(2.0 / math.pi)


def _gelu(y):
    th = jnp.tanh(_GELU_C * (y + 0.044715 * y * y * y))
    return 0.5 * y * (1.0 + th), th


def _sigmoid(x):
    return 1.0 / (1.0 + jnp.exp(-x))


def _s5_fwd(u, wb, a, wc, dskip, wglu, bglu, gnorm, name):
    r = u.shape[0]
    t = S5_TILE
    nt = r // t
    ns = wb.shape[2]
    w = S5_WIDTH

    def body(u_ref, wb_ref, a_ref, wc_ref, d_ref, wg_ref, bg_ref, gn_ref, y_ref, on_ref, xs_ref, pw_ref, carry_ref):
        i = pl.program_id(0)
        ar, ai = a_ref[0], a_ref[1]

        @pl.when(i == 0)
        def _():
            pr, pi_ = _s5_power_table(ar, ai, t)
            pw_ref[0] = pr
            pw_ref[1] = pi_
            carry_ref[...] = jnp.zeros_like(carry_ref)

        u_ = u_ref[...]
        ub = u_.astype(BF16)
        xr, xi = _s5_scan(_dot(ub, wb_ref[0]), _dot(ub, wb_ref[1]), ar, ai, t)
        cr, ci = carry_ref[0], carry_ref[1]
        xs_ref[0, 0:1, :] = cr
        xs_ref[0, 1:2, :] = ci
        pr, pi_ = pw_ref[0], pw_ref[1]
        xr = xr + pr * cr - pi_ * ci
        xi = xi + pr * ci + pi_ * cr
        carry_ref[0] = xr[t - 1:t, :]
        carry_ref[1] = xi[t - 1:t, :]
        y = _dot(xr.astype(BF16), wc_ref[0]) - _dot(xi.astype(BF16), wc_ref[1]) + d_ref[...] * u_
        h, _ = _gelu(y)
        gate = _sigmoid(_dot(h.astype(BF16), wg_ref[...]) + bg_ref[...])
        y_ref[...] = y
        on_ref[...] = _rms(h * gate, gn_ref[...]).astype(BF16)

    full = lambda shape: pl.BlockSpec(shape, lambda i: (0,) * len(shape))
    return _pallas(
        body, name=name, grid=(nt,),
        in_specs=[_row_spec(w, t), full((2, w, ns)), full((2, 1, ns)), full((2, ns, w)), full((1, w)),
                  full((w, w)), full((1, w)), full((1, w))],
        out_specs=[_row_spec(w, t), _row_spec(w, t), pl.BlockSpec((1, 2, ns), lambda i: (i, 0, 0))],
        out_shape=[jax.ShapeDtypeStruct((r, w), F32), jax.ShapeDtypeStruct((r, w), BF16),
                   jax.ShapeDtypeStruct((nt, 2, ns), F32)],
        scratch_shapes=[pltpu.VMEM((2, t, ns), F32), pltpu.VMEM((2, 1, ns), F32)],
        compiler_params=_cparams(("arbitrary",)),
    )(u, wb, a, wc, dskip, wglu, bglu, gnorm)


def _s5_bwd(u, y, don, xstart, wb, a, wc, dskip, wglu, bglu, gnorm, name):
    r = u.shape[0]
    t = S5_TILE
    nt = r // t
    ns = wb.shape[2]
    w = S5_WIDTH
    nt_dims = ((1,), (1,))
    tn_dims = ((0,), (0,))

    def body(u_ref, y_ref, don_ref, xs_ref, wb_hbm, a_ref, wc_hbm, d_ref, wg_ref, bg_ref, gn_ref,
             du_ref, da_ref, dd_ref, dbg_ref, dgn_ref, dwb_hbm, dwc_hbm, dwg_hbm,
             wb_ref, wc_ref, pw_ref, pwr_ref, lam_ref, acc_wb, acc_wc, acc_wg, sem):
        i = pl.program_id(0)
        ar, ai = a_ref[0], a_ref[1]

        @pl.when(i == 0)
        def _():
            c1 = pltpu.make_async_copy(wb_hbm, wb_ref, sem.at[0])
            c2 = pltpu.make_async_copy(wc_hbm, wc_ref, sem.at[1])
            c1.start()
            c2.start()
            pr, pi_ = _s5_power_table(ar, ai, t)
            pw_ref[0] = pr
            pw_ref[1] = pi_
            pr, pi_ = _s5_power_table(ar, -ai, t, reverse=True)
            pwr_ref[0] = pr
            pwr_ref[1] = pi_
            lam_ref[...] = jnp.zeros_like(lam_ref)
            acc_wb[...] = jnp.zeros_like(acc_wb)
            acc_wc[...] = jnp.zeros_like(acc_wc)
            acc_wg[...] = jnp.zeros_like(acc_wg)
            da_ref[...] = jnp.zeros_like(da_ref)
            dd_ref[...] = jnp.zeros_like(dd_ref)
            dbg_ref[...] = jnp.zeros_like(dbg_ref)
            dgn_ref[...] = jnp.zeros_like(dgn_ref)
            c1.wait()
            c2.wait()

        u_ = u_ref[...]
        y_ = y_ref[...]
        ub = u_.astype(BF16)
        h, th = _gelu(y_)
        hb = h.astype(BF16)
        wg = wg_ref[...]
        gate = _sigmoid(_dot(hb, wg) + bg_ref[...])
        d_out, dgn = _rms_bwd(h * gate, gn_ref[...], don_ref[...])
        dgn_ref[...] += dgn
        dhw = d_out * h * gate * (1.0 - gate)
        dhwb = dhw.astype(BF16)
        dh = d_out * gate + _dot(dhwb, wg, nt_dims)
        acc_wg[...] += _dot(hb, dhwb, tn_dims)
        dbg_ref[...] += jnp.sum(dhw, axis=0, keepdims=True)
        dgelu = 0.5 * (1.0 + th) + 0.5 * y_ * (1.0 - th * th) * _GELU_C * (1.0 + 3.0 * 0.044715 * y_ * y_)
        dy = dh * dgelu
        dd_ref[...] += jnp.sum(dy * u_, axis=0, keepdims=True)
        dyb = dy.astype(BF16)
        xr, xi = _s5_scan(_dot(ub, wb_ref[0]), _dot(ub, wb_ref[1]), ar, ai, t)
        cr, ci = xs_ref[0, 0:1, :], xs_ref[0, 1:2, :]
        pr, pi_ = pw_ref[0], pw_ref[1]
        xr = xr + pr * cr - pi_ * ci
        xi = xi + pr * ci + pi_ * cr
        acc_wc[0] += _dot(xr.astype(BF16), dyb, tn_dims)
        acc_wc[1] -= _dot(xi.astype(BF16), dyb, tn_dims)
        lr, li = _s5_scan(_dot(dyb, wc_ref[0], nt_dims), -_dot(dyb, wc_ref[1], nt_dims), ar, -ai, t, reverse=True)
        cr2, ci2 = lam_ref[0], lam_ref[1]
        pr, pi_ = pwr_ref[0], pwr_ref[1]
        lr = lr + pr * cr2 - pi_ * ci2
        li = li + pr * ci2 + pi_ * cr2
        lam_ref[0] = lr[0:1, :]
        lam_ref[1] = li[0:1, :]
        row = lax.broadcasted_iota(jnp.int32, (t, 1), 0)
        xpr = jnp.where(row == 0, cr, pltpu.roll(xr, 1, 0))
        xpi = jnp.where(row == 0, ci, pltpu.roll(xi, 1, 0))
        da_ref[0] += jnp.sum(lr * xpr + li * xpi, axis=0, keepdims=True)
        da_ref[1] += jnp.sum(li * xpr - lr * xpi, axis=0, keepdims=True)
        lrb, lib = lr.astype(BF16), li.astype(BF16)
        acc_wb[0] += _dot(ub, lrb, tn_dims)
        acc_wb[1] += _dot(ub, lib, tn_dims)
        du_ref[...] = d_ref[...] * dy + _dot(lrb, wb_ref[0], nt_dims) + _dot(lib, wb_ref[1], nt_dims)

        @pl.when(i == nt - 1)
        def _():
            cps = [pltpu.make_async_copy(acc_wb, dwb_hbm, sem.at[0]), pltpu.make_async_copy(acc_wc, dwc_hbm, sem.at[1]),
                   pltpu.make_async_copy(acc_wg, dwg_hbm, sem.at[2])]
            for c in cps:
                c.start()
            for c in cps:
                c.wait()

    rev = lambda i: (nt - 1 - i, 0)
    full = lambda shape: pl.BlockSpec(shape, lambda i: (0,) * len(shape))
    hbm = pl.BlockSpec(memory_space=pl.ANY)
    return _pallas(
        body, name=name, grid=(nt,),
        in_specs=[pl.BlockSpec((t, w), rev), pl.BlockSpec((t, w), rev), pl.BlockSpec((t, w), rev),
                  pl.BlockSpec((1, 2, ns), lambda i: (nt - 1 - i, 0, 0)), hbm, full((2, 1, ns)), hbm, full((1, w)),
                  full((w, w)), full((1, w)), full((1, w))],
        out_specs=[pl.BlockSpec((t, w), rev), full((2, 1, ns)), full((1, w)), full((1, w)), full((1, w)), hbm, hbm, hbm],
        out_shape=[jax.ShapeDtypeStruct((r, w), F32), jax.ShapeDtypeStruct((2, 1, ns), F32)]
        + [jax.ShapeDtypeStruct((1, w), F32)] * 3
        + [jax.ShapeDtypeStruct((2, w, ns), F32), jax.ShapeDtypeStruct((2, ns, w), F32), jax.ShapeDtypeStruct((w, w), F32)],
        scratch_shapes=[pltpu.VMEM((2, w, ns), BF16), pltpu.VMEM((2, ns, w), BF16),
                        pltpu.VMEM((2, t, ns), F32), pltpu.VMEM((2, t, ns), F32), pltpu.VMEM((2, 1, ns), F32),
                        pltpu.VMEM((2, w, ns), F32), pltpu.VMEM((2, ns, w), F32), pltpu.VMEM((w, w), F32),
                        pltpu.SemaphoreType.DMA((3,))],
        compiler_params=_cparams(("arbitrary",)),
    )(u, y, don, xstart, wb, a, wc, dskip, wglu, bglu, gnorm)


def _s5_expand(lam_re, lam_im, log_dt, b_re, b_im, c_re, c_im):
    g, n, p = S5_GROUPS, S5_STATE, S5_GROUP
    ns = g * n
    rows = lambda x: x.reshape(1, ns)
    logdt = jnp.repeat(log_dt.reshape(g), n).reshape(1, ns)
    btr = b_re.reshape(ns, p).T
    bti = b_im.reshape(ns, p).T
    ctr = c_re.transpose(0, 2, 1).reshape(ns, p)
    cti = c_im.transpose(0, 2, 1).reshape(ns, p)
    mask = (jnp.arange(g * p)[:, None] // p) == (jnp.arange(ns)[None, :] // n)
    return rows(lam_re), rows(lam_im), logdt, btr, bti, ctr, cti, mask


def _s5_block_diag_b(bb, mask):
    return jnp.where(mask, jnp.tile(bb, (S5_GROUPS, 1)), 0.0)


def _s5_block_diag_c(ct, mask):
    return jnp.where(mask.T, jnp.tile(ct, (1, S5_GROUPS)), 0.0)


def _s5_diag_of_b(dwb, mask):
    return jnp.where(mask, dwb, 0.0).reshape(S5_GROUPS, S5_GROUP, -1).sum(0)


def _s5_diag_of_c(dwc, mask):
    ns = dwc.shape[0]
    return jnp.where(mask.T, dwc, 0.0).reshape(ns, S5_GROUPS, S5_GROUP).sum(1)


DN_PRE_TILE = 256
_DN_QKV = 3 * DN_WIDTH


def _halo_specs(width, tile, nt, prev):
    per = tile // 8
    if prev:
        return pl.BlockSpec((8, width), lambda i: (jnp.maximum(i * per - 1, 0), 0))
    return pl.BlockSpec((8, width), lambda i: (jnp.minimum((i + 1) * per, nt * per - 1), 0))


def _shift_down(x, halo, s, t):
    xx = jnp.concatenate([halo, x], axis=0)
    return pltpu.roll(xx, s, 0)[8:]


def _shift_up(x, halo, s, t):
    xx = jnp.concatenate([x, halo], axis=0)
    return pltpu.roll(xx, t + 8 - s, 0)[:t]


def _silu(x):
    s = _sigmoid(x)
    return x * s, s


def _dn_gates(ab, alog, dtb, live):
    lane = lax.broadcasted_iota(jnp.int32, (1, 128), 1)
    g = -jnp.exp(alog) * _softplus(ab + dtb)
    beta = _sigmoid(ab)
    return jnp.where(live & (lane < DN_HEADS), g, jnp.where(live & (lane < 2 * DN_HEADS), beta, 0.0))


def _dn_pre_fwd(proj, ab, conv_w, alog, dtb, pad, name):
    r = proj.shape[0]
    t = DN_PRE_TILE
    nt = r // t
    scale = DN_HEAD_DIM ** -0.5

    def body(x_ref, halo_ref, ab_ref, w_ref, al_ref, dt_ref, co_ref, q_ref, k_ref, v_ref, gb_ref):
        i = pl.program_id(0)
        x = x_ref[...]
        halo = jnp.where(i > 0, halo_ref[...], 0.0)
        w = w_ref[...]
        co = w[3:4] * x
        for tap in range(DN_CONV - 1):
            co = co + w[tap:tap + 1] * _shift_down(x, halo, DN_CONV - 1 - tap, t)
        co_ref[...] = co
        act, _ = _silu(co)
        for hd in range(DN_HEADS):
            sl = slice(hd * 128, (hd + 1) * 128)
            for base, o_ref, sc in ((0, q_ref, scale), (DN_WIDTH, k_ref, 1.0)):
                xh = act[:, base + hd * 128: base + (hd + 1) * 128]
                o_ref[:, sl] = xh * (lax.rsqrt(jnp.sum(xh * xh, axis=-1, keepdims=True) + EPS) * sc)
        v_ref[...] = act[:, 2 * DN_WIDTH:]
        rows = i * t + lax.broadcasted_iota(jnp.int32, (t, 1), 0)
        gb_ref[...] = _dn_gates(ab_ref[...], al_ref[...], dt_ref[...], rows >= pad)

    return _pallas(
        body, name=name, grid=(nt,),
        in_specs=[pl.BlockSpec((t, _DN_QKV), lambda i: (i, 0)), _halo_specs(_DN_QKV, t, nt, True), _row_spec(128, t),
                  pl.BlockSpec((DN_CONV, _DN_QKV), lambda i: (0, 0)), _vec_spec(128), _vec_spec(128)],
        out_specs=[_row_spec(_DN_QKV, t), _row_spec(DN_WIDTH, t), _row_spec(DN_WIDTH, t), _row_spec(DN_WIDTH, t), _row_spec(128, t)],
        out_shape=[jax.ShapeDtypeStruct((r, _DN_QKV), F32)] + [jax.ShapeDtypeStruct((r, DN_WIDTH), F32)] * 3
        + [jax.ShapeDtypeStruct((r, 128), F32)],
        compiler_params=_cparams(("parallel",)),
    )(proj, proj, ab, conv_w, alog, dtb)


def _dn_pre_bwd(co, dq, dk, dv, dgb, ab, alog, dtb, pad, name):
    r = co.shape[0]
    t = DN_PRE_TILE
    nt = r // t
    scale = DN_HEAD_DIM ** -0.5

    def body(co_ref, dq_ref, dk_ref, dv_ref, dgb_ref, ab_ref, al_ref, dt_ref, dco_ref, dab_ref, dal_ref, ddt_ref):
        i = pl.program_id(0)

        @pl.when(i == 0)
        def _():
            dal_ref[...] = jnp.zeros_like(dal_ref)
            ddt_ref[...] = jnp.zeros_like(ddt_ref)

        co_ = co_ref[...]
        act, sg = _silu(co_)
        dsilu = sg * (1.0 + co_ * (1.0 - sg))
        for hd in range(DN_HEADS):
            sl = slice(hd * 128, (hd + 1) * 128)
            for base, d_ref, sc in ((0, dq_ref, scale), (DN_WIDTH, dk_ref, 1.0)):
                cs = slice(base + hd * 128, base + (hd + 1) * 128)
                xh = act[:, cs]
                rn = lax.rsqrt(jnp.sum(xh * xh, axis=-1, keepdims=True) + EPS)
                xhat = xh * rn
                dy = d_ref[:, sl]
                dx = (sc * rn) * (dy - xhat * jnp.sum(dy * xhat, axis=-1, keepdims=True))
                dco_ref[:, cs] = dx * dsilu[:, cs]
        dco_ref[:, 2 * DN_WIDTH:] = dv_ref[...] * dsilu[:, 2 * DN_WIDTH:]
        rows = i * t + lax.broadcasted_iota(jnp.int32, (t, 1), 0)
        live = rows >= pad
        lane = lax.broadcasted_iota(jnp.int32, (1, 128), 1)
        ab_ = ab_ref[...]
        dgb_ = dgb_ref[...]
        is_g = live & (lane < DN_HEADS)
        is_b = live & (lane >= DN_HEADS) & (lane < 2 * DN_HEADS)
        arg = ab_ + dt_ref[...]
        ea = jnp.exp(al_ref[...])
        da = jnp.where(is_g, -dgb_ * ea * _sigmoid(arg), 0.0)
        beta = _sigmoid(ab_)
        dab_ref[...] = da + jnp.where(is_b, dgb_ * beta * (1.0 - beta), 0.0)
        ddt_ref[...] += jnp.sum(da, axis=0, keepdims=True)
        dal_ref[...] += jnp.sum(jnp.where(is_g, -dgb_ * ea * _softplus(arg), 0.0), axis=0, keepdims=True)

    return _pallas(
        body, name=name, grid=(nt,),
        in_specs=[_row_spec(_DN_QKV, t), _row_spec(DN_WIDTH, t), _row_spec(DN_WIDTH, t), _row_spec(DN_WIDTH, t),
                  _row_spec(128, t), _row_spec(128, t), _vec_spec(128), _vec_spec(128)],
        out_specs=[_row_spec(_DN_QKV, t), _row_spec(128, t), _vec_spec(128), _vec_spec(128)],
        out_shape=[jax.ShapeDtypeStruct((r, _DN_QKV), F32), jax.ShapeDtypeStruct((r, 128), F32),
                   jax.ShapeDtypeStruct((1, 128), F32), jax.ShapeDtypeStruct((1, 128), F32)],
        compiler_params=_cparams(("arbitrary",)),
    )(co, dq, dk, dv, dgb, ab, alog, dtb)


def _dn_conv_bwd(dco, proj, conv_w, name):
    r = dco.shape[0]
    t = DN_PRE_TILE
    nt = r // t

    def body(d_ref, dh_ref, x_ref, xh_ref, w_ref, dx_ref, dw_ref):
        i = pl.program_id(0)

        @pl.when(i == 0)
        def _():
            dw_ref[...] = jnp.zeros_like(dw_ref)

        d = d_ref[...]
        dhalo = jnp.where(i < nt - 1, dh_ref[...], 0.0)
        x = x_ref[...]
        xhalo = jnp.where(i > 0, xh_ref[...], 0.0)
        w = w_ref[...]
        dx = w[3:4] * d
        dws = [None] * DN_CONV
        dws[3] = jnp.sum(d * x, axis=0, keepdims=True)
        for tap in range(DN_CONV - 1):
            s = DN_CONV - 1 - tap
            dx = dx + w[tap:tap + 1] * _shift_up(d, dhalo, s, t)
            dws[tap] = jnp.sum(d * _shift_down(x, xhalo, s, t), axis=0, keepdims=True)
        dx_ref[...] = dx
        dw_ref[...] += jnp.concatenate(dws + [jnp.zeros((8 - DN_CONV, _DN_QKV), F32)], axis=0)

    return _pallas(
        body, name=name, grid=(nt,),
        in_specs=[_row_spec(_DN_QKV, t), _halo_specs(_DN_QKV, t, nt, False),
                  pl.BlockSpec((t, _DN_QKV), lambda i: (i, 0)), _halo_specs(_DN_QKV, t, nt, True),
                  pl.BlockSpec((DN_CONV, _DN_QKV), lambda i: (0, 0))],
        out_specs=[_row_spec(_DN_QKV, t), pl.BlockSpec((8, _DN_QKV), lambda i: (0, 0))],
        out_shape=[jax.ShapeDtypeStruct((r, _DN_QKV), F32), jax.ShapeDtypeStruct((8, _DN_QKV), F32)],
        compiler_params=_cparams(("arbitrary",)),
    )(dco, dco, proj, proj, conv_w)


def _split3(x):
    hi = x.astype(BF16)
    return hi, (x - hi.astype(F32)).astype(BF16)


def _dot3s(a, b, dims=((1,), (0,))):
    return _dot(a[0], b[0], dims) + (_dot(a[0], b[1], dims) + _dot(a[1], b[0], dims))


def _dot3(a, b, dims=((1,), (0,))):
    return _dot3s(_split3(a), _split3(b), dims)


def _dn_inverse_many(n_mats):
    c = n_mats[0].shape[0]
    row = lax.broadcasted_iota(jnp.int32, (c, c), 0)
    col = lax.broadcasted_iota(jnp.int32, (c, c), 1)
    eye = (row == col).astype(F32)
    same = row // DN_SUB == col // DN_SUB
    nds = [jnp.where(same, n, 0.0) for n in n_mats]
    nos = [n - nd for n, nd in zip(n_mats, nds)]

    def geometric(bs, order):
        xs = [eye + b for b in bs]
        sp = [_split3(b) for b in bs]
        k = 2
        while k < order:
            sp = [_split3(_dot3s(s_, s_)) for s_ in sp]
            xs = [x + _dot3s(_split3(x), s_) for x, s_ in zip(xs, sp)]
            k *= 2
        return xs

    tds = [_split3(td) for td in geometric([-nd for nd in nds], DN_SUB)]
    ms = [_dot3s(td, _split3(no)) for td, no in zip(tds, nos)]
    xs = geometric([-m for m in ms], c // DN_SUB)
    return [_dot3s(_split3(x), td) for x, td in zip(xs, tds)]


def _dn_chunk_shared(gb_ref, gbt_ref):
    c = DN_CHUNK
    row = lax.broadcasted_iota(jnp.int32, (c, c), 0)
    col = lax.broadcasted_iota(jnp.int32, (c, c), 1)
    gbv = gb_ref[...]
    gam_all = _split_dot((row >= col).astype(BF16), gbv)
    hi, lo = _split3(gbt_ref[...])
    tri_t = (row <= col).astype(BF16)
    return dict(row=row, col=col, gbv=gbv, gam_all=gam_all, gam_rows=_dot(hi, tri_t) + _dot(lo, tri_t),
                lane=lax.broadcasted_iota(jnp.int32, (1, 128), 1))


def _dn_chunk_common(q, k, v, sh, h):
    c = DN_CHUNK
    row, col, lane = sh["row"], sh["col"], sh["lane"]
    gam = jnp.sum(jnp.where(lane == h, sh["gam_all"], 0.0), axis=1, keepdims=True)
    beta = jnp.sum(jnp.where(lane == h + DN_HEADS, sh["gbv"], 0.0), axis=1, keepdims=True)
    gam_row = sh["gam_rows"][h:h + 1]
    dec = jnp.where(row >= col, jnp.exp(jnp.minimum(gam - gam_row, 0.0)), 0.0)
    kb, qb = k.astype(BF16), q.astype(BF16)
    nt_dims = ((1,), (1,))
    kk = _dot(kb, kb, nt_dims)
    qk = _dot(qb, kb, nt_dims)
    eg = jnp.exp(gam)
    gam_l = gam[c - 1:c, :]
    return dict(q=q, k=k, v=v, qb=qb, kb=kb, gam=gam, beta=beta, dec=dec, kk=kk, qk=qk, eg=eg, gam_l=gam_l,
                row=row, col=col, lane=lane, att=qk * dec, qg=q * eg, kt=k * jnp.exp(gam_l - gam),
                rhs=jnp.concatenate([v * beta, k * (beta * eg)], axis=1))


def _dn_fwd(q, k, v, gb, gbt, name):
    r = q.shape[0]
    c = DN_CHUNK
    nc = r // c
    dh = DN_HEAD_DIM
    tn_dims = ((0,), (0,))

    def body(q_ref, k_ref, v_ref, gb_ref, gbt_ref, o_ref, ss_ref, ts_ref, s_ref):
        @pl.when(pl.program_id(0) == 0)
        def _():
            s_ref[...] = jnp.zeros_like(s_ref)

        heads = list(range(DN_HEADS))
        sl = [slice(h * dh, (h + 1) * dh) for h in heads]
        sh = _dn_chunk_shared(gb_ref, gbt_ref)
        zs = [_dn_chunk_common(q_ref[:, sl[h]], k_ref[:, sl[h]], v_ref[:, sl[h]], sh, h) for h in heads]
        t_invs = _dn_inverse_many([jnp.where(sh["row"] > sh["col"], z["beta"] * z["kk"] * z["dec"], 0.0) for z in zs])
        sols = [_dot3(t_inv, z["rhs"]) for t_inv, z in zip(t_invs, zs)]
        ss = [s_ref[h] for h in heads]
        sbs = [s.astype(BF16) for s in ss]
        vnbs = [(sol[:, :dh] - _dot(sol[:, dh:].astype(BF16), sb)).astype(BF16) for sol, sb in zip(sols, sbs)]
        for h in heads:
            o_ref[:, sl[h]] = _dot(zs[h]["qg"].astype(BF16), sbs[h]) + _dot(zs[h]["att"].astype(BF16), vnbs[h])
        for h in heads:
            ss_ref[0, h] = ss[h]
            ts_ref[0, h] = t_invs[h]
            s_ref[h] = ss[h] * jnp.exp(zs[h]["gam_l"]) + _dot(zs[h]["kt"].astype(BF16), vnbs[h], tn_dims)

    blk = pl.BlockSpec((c, DN_WIDTH), lambda ci: (ci, 0))
    sav = pl.BlockSpec((1, DN_HEADS, dh, dh), lambda ci: (ci, 0, 0, 0))
    return _pallas(
        body, name=name, grid=(nc,),
        in_specs=[blk, blk, blk, pl.BlockSpec((c, 128), lambda ci: (ci, 0)), pl.BlockSpec((16, c), lambda ci: (0, ci))],
        out_specs=[blk, sav, sav],
        out_shape=[jax.ShapeDtypeStruct((r, DN_WIDTH), F32), jax.ShapeDtypeStruct((nc, DN_HEADS, dh, dh), F32),
                   jax.ShapeDtypeStruct((nc, DN_HEADS, dh, dh), F32)],
        scratch_shapes=[pltpu.VMEM((DN_HEADS, dh, dh), F32)],
        compiler_params=_cparams(("arbitrary",)),
    )(q, k, v, gb, gbt)


def _dn_bwd(q, k, v, gb, gbt, ssave, tsave, do, name):
    r = q.shape[0]
    c = DN_CHUNK
    nc = r // c
    dh = DN_HEAD_DIM
    nt_dims = ((1,), (1,))
    tn_dims = ((0,), (0,))

    def body(q_ref, k_ref, v_ref, gb_ref, gbt_ref, ss_ref, ts_ref, do_ref, dq_ref, dk_ref, dv_ref, dgb_ref, ds_ref):
        @pl.when(pl.program_id(0) == 0)
        def _():
            ds_ref[...] = jnp.zeros_like(ds_ref)

        heads = list(range(DN_HEADS))
        sl = [slice(h * dh, (h + 1) * dh) for h in heads]
        sh = _dn_chunk_shared(gb_ref, gbt_ref)
        row, col, lane = sh["row"], sh["col"], sh["lane"]
        rs = lambda x: jnp.sum(x, axis=1, keepdims=True)
        tot = lambda x: jnp.sum(rs(x), axis=0, keepdims=True)
        st = [dict() for _ in heads]
        dgb_parts = []

        def s_common(h):
            st[h].update(_dn_chunk_common(q_ref[:, sl[h]], k_ref[:, sl[h]], v_ref[:, sl[h]], sh, h))
            st[h]["t"] = _split3(ts_ref[0, h])

        def s_sol(h):
            st[h]["sol"] = _dot3s(st[h]["t"], _split3(st[h]["rhs"]))

        def s_state(h):
            z = st[h]
            sol = z["sol"]
            kcd = sol[:, dh:]
            s = ss_ref[0, h]
            sb = s.astype(BF16)
            vnb = (sol[:, :dh] - _dot(kcd.astype(BF16), sb)).astype(BF16)
            ds_next = ds_ref[h]
            dsb = ds_next.astype(BF16)
            dob = do_ref[:, sl[h]].astype(BF16)
            z["dqg"] = _dot(dob, sb, nt_dims)
            ds = _dot(z["qg"].astype(BF16), dob, tn_dims)
            z["d_att"] = jnp.where(row >= col, _dot(dob, vnb, nt_dims), 0.0)
            dvn = _dot(z["att"].astype(BF16), dob, tn_dims) + _dot(z["kt"].astype(BF16), dsb)
            z["dkt"] = _dot(vnb, dsb, nt_dims)
            eg_l = jnp.exp(z["gam_l"])
            ds = ds + ds_next * eg_l
            z["dgam_l"] = tot(ds_next * s) * eg_l
            dvnb = dvn.astype(BF16)
            dkcd = -_dot(dvnb, sb, nt_dims)
            ds_ref[h] = ds - _dot(kcd.astype(BF16), dvnb, tn_dims)
            z["dsol"] = jnp.concatenate([dvn, dkcd], axis=1)

        def s_drhs(h):
            st[h]["drhs"] = _dot3s(st[h]["t"], _split3(st[h]["dsol"]), tn_dims)

        def s_dn(h):
            z = st[h]
            z["dn"] = jnp.where(row > col, -_dot3(z["drhs"], z["sol"], nt_dims), 0.0)

        def s_rest(h):
            z = st[h]
            k_, v_, kb, qb = z["k"], z["v"], z["kb"], z["qb"]
            beta, eg, dec, kk, qk, gam, gam_l = z["beta"], z["eg"], z["dec"], z["kk"], z["qk"], z["gam"], z["gam_l"]
            dn, d_att, dqg, dkt = z["dn"], z["d_att"], z["dqg"], z["dkt"]
            drv, drk = z["drhs"][:, :dh], z["drhs"][:, dh:]
            s_rkk = rs(drk * k_)
            dv_ref[:, sl[h]] = drv * beta
            dbeta = rs(drv * v_) + s_rkk * eg + rs(dn * kk * dec)
            dk = drk * (beta * eg)
            dgam = s_rkk * beta * eg
            dkk = (dn * beta * dec).astype(BF16)
            dd = dn * beta * kk + d_att * qk
            dqk = (d_att * dec).astype(BF16)
            dq_ref[:, sl[h]] = _dot(dqk, kb) + dqg * eg
            dk = dk + _dot(dqk, qb, tn_dims) + _dot(dkk, kb) + _dot(dkk, kb, tn_dims)
            w = dd * dec
            wh, wl = _split3(w)
            ones = jnp.ones((c, 128), BF16)
            col_sum = (_dot(wh, ones, tn_dims) + _dot(wl, ones, tn_dims))[:, 0:1]
            dgam = dgam + rs(w) - col_sum + rs(dqg * z["qg"]) - rs(dkt * z["kt"])
            dk_ref[:, sl[h]] = dk + dkt * jnp.exp(gam_l - gam)
            dgam_l = z["dgam_l"] + tot(dkt * z["kt"])
            rowc = lax.broadcasted_iota(jnp.int32, (c, 1), 0)
            dgam = dgam + jnp.where(rowc == c - 1, dgam_l, 0.0)
            dg = _split_dot((row <= col).astype(BF16), jnp.broadcast_to(dgam, (c, 128)))[:, 0:1]
            dgb_parts.append(jnp.where(lane == h, dg, 0.0) + jnp.where(lane == h + DN_HEADS, dbeta, 0.0))

        _emit_chains(heads, [s_common, s_sol, s_state, s_drhs, s_dn, s_rest], False)
        dgb = dgb_parts[0]
        for part in dgb_parts[1:]:
            dgb = dgb + part
        dgb_ref[...] = dgb

    blk = pl.BlockSpec((c, DN_WIDTH), lambda ci: (nc - 1 - ci, 0))
    sav = pl.BlockSpec((1, DN_HEADS, dh, dh), lambda ci: (nc - 1 - ci, 0, 0, 0))
    gspec = pl.BlockSpec((c, 128), lambda ci: (nc - 1 - ci, 0))
    return _pallas(
        body, name=name, grid=(nc,),
        in_specs=[blk, blk, blk, gspec, pl.BlockSpec((16, c), lambda ci: (0, nc - 1 - ci)), sav, sav, blk],
        out_specs=[blk, blk, blk, gspec],
        out_shape=[jax.ShapeDtypeStruct((r, DN_WIDTH), F32)] * 3 + [jax.ShapeDtypeStruct((r, 128), F32)],
        scratch_shapes=[pltpu.VMEM((DN_HEADS, dh, dh), F32)],
        compiler_params=_cparams(("arbitrary",)),
    )(q, k, v, gb, gbt, ssave, tsave, do)


def _dn_post_fwd(o, proj, g, name):
    r = o.shape[0]

    def body(o_ref, z_ref, g_ref, y_ref):
        g_ = g_ref[...]
        for hd in range(DN_HEADS):
            sl = slice(hd * 128, (hd + 1) * 128)
            sz, _ = _silu(z_ref[:, sl])
            y_ref[:, sl] = (_rms(o_ref[:, sl], g_) * sz).astype(BF16)

    return _pallas(body, name=name, grid=(r // ROW_TILE,),
                   in_specs=[_row_spec(DN_WIDTH), pl.BlockSpec((ROW_TILE, DN_WIDTH), lambda i: (i, 3)), _vec_spec(128)],
                   out_specs=_row_spec(DN_WIDTH), out_shape=jax.ShapeDtypeStruct((r, DN_WIDTH), BF16),
                   compiler_params=_cparams(("parallel",)))(o, proj, g)


def _dn_post_bwd(o, proj, g, dy, name):
    r = o.shape[0]

    def body(o_ref, z_ref, g_ref, dy_ref, do_ref, dz_ref, dg_ref):
        @pl.when(pl.program_id(0) == 0)
        def _():
            dg_ref[...] = jnp.zeros_like(dg_ref)

        g_ = g_ref[...]
        for hd in range(DN_HEADS):
            sl = slice(hd * 128, (hd + 1) * 128)
            z_ = z_ref[:, sl]
            sz, sg = _silu(z_)
            dy_ = dy_ref[:, sl]
            o_ = o_ref[:, sl]
            dz_ref[:, sl] = dy_ * _rms(o_, g_) * (sg * (1.0 + z_ * (1.0 - sg)))
            dx, dg = _rms_bwd(o_, g_, dy_ * sz)
            do_ref[:, sl] = dx
            dg_ref[...] += dg

    return _pallas(body, name=name, grid=(r // ROW_TILE,),
                   in_specs=[_row_spec(DN_WIDTH), pl.BlockSpec((ROW_TILE, DN_WIDTH), lambda i: (i, 3)), _vec_spec(128),
                             _row_spec(DN_WIDTH)],
                   out_specs=[_row_spec(DN_WIDTH), _row_spec(DN_WIDTH), _vec_spec(128)],
                   out_shape=[jax.ShapeDtypeStruct((r, DN_WIDTH), F32)] * 2 + [jax.ShapeDtypeStruct((1, 128), F32)],
                   compiler_params=_cparams(("arbitrary",)))(o, proj, g, dy)


def _exchange(arrays, scatter, name):
    n = len(arrays)
    outs_shape = [jax.ShapeDtypeStruct((N_DEV,) + (a.shape[1:] if sc else a.shape), a.dtype) for a, sc in zip(arrays, scatter)]

    def body(*refs):
        in_refs, out_refs = refs[:n], refs[n:2 * n]
        send_sems, recv_sems, local_sems = refs[2 * n:]
        mx, my, mc = lax.axis_index("x"), lax.axis_index("y"), lax.axis_index("c")
        me = 4 * mx + 2 * my + mc
        started = []
        for a in range(n):
            src_own = in_refs[a].at[me] if scatter[a] else in_refs[a]
            loc = pltpu.make_async_copy(src_own, out_refs[a].at[me], local_sems.at[a])
            loc.start()
            started.append(loc)
        remote = []
        for a in range(n):
            for kbits in range(1, N_DEV):
                px = lax.rem(mx + ((kbits >> 2) & 1), 2)
                py = lax.rem(my + ((kbits >> 1) & 1), 2)
                pc = lax.rem(mc + (kbits & 1), 2)
                src = in_refs[a].at[4 * px + 2 * py + pc] if scatter[a] else in_refs[a]
                cp = pltpu.make_async_remote_copy(
                    src_ref=src, dst_ref=out_refs[a].at[me],
                    send_sem=send_sems.at[a * N_DEV + kbits], recv_sem=recv_sems.at[a * N_DEV + kbits],
                    device_id=(px, py, pc), device_id_type=pl.DeviceIdType.MESH)
                cp.start()
                remote.append(cp)
        for cp in remote:
            cp.wait()
        for loc in started:
            loc.wait()

    hbm = pl.BlockSpec(memory_space=pl.ANY)
    return _pallas(
        body, name=name, in_specs=[hbm] * n, out_specs=[hbm] * n, out_shape=outs_shape,
        scratch_shapes=[pltpu.SemaphoreType.DMA((n * N_DEV,)), pltpu.SemaphoreType.DMA((n * N_DEV,)),
                        pltpu.SemaphoreType.DMA((n,))],
    )(*arrays)


def _adamw(gstack, w, m, v, name):
    a, b = w.shape
    ta = a
    for t in (1024, 512, 256, 128, 64, 32, 16, 8):
        if a % t == 0 and N_DEV * t * b * 4 <= 4 * 1024 * 1024:
            ta = t
            break
    c1 = 1.0 / (1.0 - ADAM_B1 ** ADAM_STEP)
    c2 = 1.0 / (1.0 - ADAM_B2 ** ADAM_STEP)

    def body(g_ref, w_ref, m_ref, v_ref, og_ref, od_ref, om_ref, ov_ref):
        g = g_ref[0].astype(F32)
        for s in range(1, N_DEV):
            g = g + g_ref[s].astype(F32)
        m_new = ADAM_B1 * m_ref[...] + (1.0 - ADAM_B1) * g
        v_new = ADAM_B2 * v_ref[...] + (1.0 - ADAM_B2) * (g * g)
        og_ref[...] = g
        om_ref[...] = m_new
        ov_ref[...] = v_new
        od_ref[...] = -ADAM_LR * ((m_new * c1) / (jnp.sqrt(v_new * c2) + ADAM_EPS) + ADAM_WD * w_ref[...])

    spec = pl.BlockSpec((ta, b), lambda i: (i, 0))
    return _pallas(
        body, name=name, grid=(a // ta,),
        in_specs=[pl.BlockSpec((N_DEV, ta, b), lambda i: (0, i, 0)), spec, spec, spec],
        out_specs=[spec] * 4, out_shape=[jax.ShapeDtypeStruct((a, b), F32)] * 4,
        compiler_params=_cparams(("parallel",)),
    )(gstack, w, m, v)


_WEIGHTS = ['meta_tokens', 'pre_mix_norm', 'post_mix_norm', 'pre_mlp_norm', 'post_mlp_norm', 'mlp_w1', 'mlp_w2',
            'w_in_even', 'w_out_even', 'sb_out_norm', 's5_lambda_re', 's5_lambda_im', 's5_log_dt', 's5_b_re', 's5_b_im',
            's5_c_re', 's5_c_im', 's5_d', 's5_w_glu', 's5_b_glu', 's5_out_norm', 'w_in_odd', 'dn_conv_w', 'dn_a_log',
            'dn_dt_bias', 'dn_out_norm', 'w_out_odd']
_SHARDED = ['meta_tokens', 'mlp_w1', 'mlp_w2', 'w_in_even', 'w_out_even', 's5_w_glu', 'w_in_odd', 'dn_conv_w', 'w_out_odd']
_SMALL = [n for n in _WEIGHTS if n not in _SHARDED]


def _view2d(name, a):
    return a.reshape(-1, a.shape[-1])


def _unshard(name, g):
    if name == 'mlp_w1':
        return g.reshape(N_DEV, 2, D_MODEL, -1).transpose(1, 2, 0, 3).reshape(2, D_MODEL, D_FF)
    if name == 'mlp_w2':
        return g.reshape(N_DEV, 2, -1, D_MODEL).transpose(1, 0, 2, 3).reshape(2, D_FF, D_MODEL)
    if name in ('w_in_even', 'w_in_odd', 'dn_conv_w', 'meta_tokens'):
        return g.transpose(1, 0, 2).reshape(g.shape[1], -1)
    return g.reshape(-1, g.shape[-1])


def _to_blocks(name, full):
    if name == 'mlp_w1':
        return full.reshape(2, D_MODEL, N_DEV, -1).transpose(2, 0, 1, 3).reshape(N_DEV, 2 * D_MODEL, -1)
    if name == 'mlp_w2':
        return full.reshape(2, N_DEV, -1, D_MODEL).transpose(1, 0, 2, 3).reshape(N_DEV, -1, D_MODEL)
    if name in ('w_in_even', 'w_in_odd', 'dn_conv_w', 'meta_tokens'):
        return full.reshape(full.shape[0], N_DEV, -1).transpose(1, 0, 2)
    return full.reshape(N_DEV, -1, full.shape[-1])


def _pack(parts):
    rows = []
    for p in parts:
        flat = p.reshape(-1)
        rows.append(jnp.pad(flat, (0, (-flat.shape[0]) % 128)).reshape(-1, 128))
    return jnp.concatenate(rows, axis=0)


def _unpack(packed, like):
    out, at = [], 0
    for p in like:
        n = math.prod(p.shape)
        nrow = -(-n // 128)
        out.append(packed[at:at + nrow].reshape(-1)[:n].reshape(p.shape))
        at += nrow
    return out


def _lane_vec(x, width=128):
    flat = x.reshape(-1)
    return jnp.pad(flat, (0, width - flat.shape[0])).reshape(1, width)


def kernel(x, meta_tokens, pre_mix_norm, post_mix_norm, pre_mlp_norm, post_mlp_norm, mlp_w1, mlp_w2, w_in_even, w_out_even, sb_out_norm, s5_lambda_re, s5_lambda_im, s5_log_dt, s5_b_re, s5_b_im, s5_c_re, s5_c_im, s5_d, s5_w_glu, s5_b_glu, s5_out_norm, w_in_odd, dn_conv_w, dn_a_log, dn_dt_bias, dn_out_norm, w_out_odd, loss_target, m_meta_tokens, m_pre_mix_norm, m_post_mix_norm, m_pre_mlp_norm, m_post_mlp_norm, m_mlp_w1, m_mlp_w2, m_w_in_even, m_w_out_even, m_sb_out_norm, m_s5_lambda_re, m_s5_lambda_im, m_s5_log_dt, m_s5_b_re, m_s5_b_im, m_s5_c_re, m_s5_c_im, m_s5_d, m_s5_w_glu, m_s5_b_glu, m_s5_out_norm, m_w_in_odd, m_dn_conv_w, m_dn_a_log, m_dn_dt_bias, m_dn_out_norm, m_w_out_odd, v_meta_tokens, v_pre_mix_norm, v_post_mix_norm, v_pre_mlp_norm, v_post_mlp_norm, v_mlp_w1, v_mlp_w2, v_w_in_even, v_w_out_even, v_sb_out_norm, v_s5_lambda_re, v_s5_lambda_im, v_s5_log_dt, v_s5_b_re, v_s5_b_im, v_s5_c_re, v_s5_c_im, v_s5_d, v_s5_w_glu, v_s5_b_glu, v_s5_out_norm, v_w_in_odd, v_dn_conv_w, v_dn_a_log, v_dn_dt_bias, v_dn_out_norm, v_w_out_odd):
    given = dict(locals())
    w = {n: given[n] for n in _WEIGHTS}
    mom_m = {n: given["m_" + n] for n in _WEIGHTS}
    mom_v = {n: given["v_" + n] for n in _WEIGHTS}

    seq = x.shape[1]
    assert x.shape[0] == 1 and seq % ROW_TILE == 0
    r = seq + ROW_TILE
    pad = ROW_TILE - N_META
    pad_tiles = 1

    wire = {n: (F32 if n in ('dn_conv_w', 'meta_tokens') else BF16) for n in _SHARDED}
    gathered = _exchange([_view2d(n, w[n]).astype(wire[n]) for n in _SHARDED], [False] * len(_SHARDED), "gather_weights")
    full = {n: _unshard(n, g) for n, g in zip(_SHARDED, gathered)}
    w1, w2 = full['mlp_w1'], full['mlp_w2']
    w_ie, w_oe, w_glu, w_oo = full['w_in_even'], full['w_out_even'], full['s5_w_glu'], full['w_out_odd']
    w_io = full['w_in_odd'][:, :4 * DN_WIDTH]
    w_ab = jnp.pad(full['w_in_odd'][:, 4 * DN_WIDTH:], ((0, 0), (0, 128 - 2 * DN_HEADS)))
    conv_w = full['dn_conv_w']
    row = lambda v_: v_.reshape(1, -1)

    hs0 = jnp.concatenate([jnp.zeros((pad, D_MODEL), F32), full['meta_tokens'], x[0]], axis=0)
    hn0 = _norm_pre(hs0, row(pre_mix_norm[0]), "pre_mix_0")
    qkv = _mm_fwd(hn0, w_ie[:, :3 * SB_WIDTH], "in_even_qkv", out_dtypes=(BF16,))
    u = _mm_fwd(hn0, w_ie[:, 3 * SB_WIDTH:], "in_even_u")
    q, k, v = qkv[:, :SB_WIDTH], qkv[:, SB_WIDTH:2 * SB_WIDTH], qkv[:, 2 * SB_WIDTH:]
    nb = r // ATT_BLK
    blocks_t = lambda t_: t_.reshape(nb, ATT_BLK, 4, 128).transpose(2, 0, 3, 1)
    o_sb, ssave = _sb_fwd(q, k, blocks_t(v), pad, "sb_fwd")
    on_sb = _norm_pre(o_sb, row(sb_out_norm[0]), "sb_out_norm")

    lam_re, lam_im, logdt, btr, bti, ctr, cti, s5_mask = _s5_expand(
        s5_lambda_re[0], s5_lambda_im[0], s5_log_dt[0], s5_b_re[0], s5_b_im[0], s5_c_re[0], s5_c_im[0])
    a_re, a_im, bbr, bbi = _s5_prep(lam_re, lam_im, logdt, btr, bti, "s5_prep")
    s5_wb = jnp.stack([_s5_block_diag_b(bbr, s5_mask), _s5_block_diag_b(bbi, s5_mask)]).astype(BF16)
    s5_wc = jnp.stack([_s5_block_diag_c(ctr, s5_mask), _s5_block_diag_c(cti, s5_mask)]).astype(BF16)
    s5_a = jnp.stack([a_re, a_im])
    s5_args = (s5_wb, s5_a, s5_wc, row(s5_d[0]), w_glu, row(s5_b_glu[0]), row(s5_out_norm[0]))
    y_s5, on_s5, xstart = _s5_fwd(u, *s5_args, "s5_fwd")

    merged = jnp.concatenate([on_sb, on_s5], axis=1)
    mix0 = _mm_fwd(merged, w_oe, "out_even")
    hs1, hn1 = _norm_post_pre(hs0, mix0, row(post_mix_norm[0]), row(pre_mlp_norm[0]), "post_mix_0")
    relu2 = lambda acc: (jnp.square(jnp.maximum(acc, 0.0)), jnp.maximum(acc, 0.0))
    r0, ra0 = _mm_fwd(hn1, w1[0], "mlp_up_0", out_dtypes=(BF16, BF16), epilogue=relu2)
    m0 = _mm_fwd(r0, w2[0], "mlp_down_0")
    hs2, hn2 = _norm_post_pre(hs1, m0, row(post_mlp_norm[0]), row(pre_mix_norm[1]), "post_mlp_0")

    proj = _mm_fwd(hn2, w_io, "in_odd")
    ab = _mm_fwd(hn2, w_ab, "in_odd_gates")
    alog, dtb = _lane_vec(dn_a_log[0]), _lane_vec(dn_dt_bias[0])
    co, qd, kd, vd, gb = _dn_pre_fwd(proj, ab, conv_w, alog, dtb, pad, "dn_pre")
    gbt = gb[:, :2 * DN_HEADS].T
    o_dn, s_dn, t_dn = _dn_fwd(qd, kd, vd, gb, gbt, "dn_fwd")
    on_dn = _dn_post_fwd(o_dn, proj, row(dn_out_norm[0]), "dn_post")
    mix1 = _mm_fwd(on_dn, w_oo, "out_odd")
    hs3, hn3 = _norm_post_pre(hs2, mix1, row(post_mix_norm[1]), row(pre_mlp_norm[1]), "post_mix_1")
    r1, ra1 = _mm_fwd(hn3, w1[1], "mlp_up_1", out_dtypes=(BF16, BF16), epilogue=relu2)
    m1 = _mm_fwd(r1, w2[1], "mlp_down_1")
    dhs, loss_part = _norm_post_loss(hs3, m1, row(post_mlp_norm[1]), loss_target[0], pad_tiles, "post_mlp_1_loss")
    loss = lax.psum(loss_part, ("x", "y", "c"))

    g = {}
    drelu2 = lambda acc, ra: (acc * (2.0 * ra.astype(F32)),)

    def mlp_bwd(layer, hn, rr, ra, dm):
        dw2 = _mm_wgrad(rr, dm, f"mlp_down_{layer}_wgrad")
        da = _mm_dgrad(dm, w2[layer], f"mlp_down_{layer}_dgrad", out_dtypes=(BF16,), extras=(ra,), epilogue=drelu2)
        dw1 = _mm_wgrad(hn, da, f"mlp_up_{layer}_wgrad")
        return dw1, dw2, _mm_dgrad(da, w1[layer], f"mlp_up_{layer}_dgrad")

    _, dm1, _, dg_post_mlp1 = _norm_bwd(dhs, post=(m1, row(post_mlp_norm[1])), pad=pad, name="post_mlp_1_bwd")
    dw1_1, dw2_1, dhn3 = mlp_bwd(1, hn3, r1, ra1, dm1)
    dhs, dmix1, dg_pre_mlp1, dg_post_mix1 = _norm_bwd(
        dhs, pre=(hs3, row(pre_mlp_norm[1]), dhn3), post=(mix1, row(post_mix_norm[1])), pad=pad, name="post_mix_1_bwd")

    g['w_out_odd'] = _mm_wgrad(on_dn, dmix1, "out_odd_wgrad")
    d_on_dn = _mm_dgrad(dmix1, w_oo, "out_odd_dgrad")
    do_dn, dz, dg_dn = _dn_post_bwd(o_dn, proj, row(dn_out_norm[0]), d_on_dn, "dn_post_bwd")
    dqd, dkd, dvd, dgb = _dn_bwd(qd, kd, vd, gb, gbt, s_dn, t_dn, do_dn, "dn_bwd")
    dco, dab, d_alog, d_dtb = _dn_pre_bwd(co, dqd, dkd, dvd, dgb, ab, alog, dtb, pad, "dn_pre_bwd")
    dpre, d_conv = _dn_conv_bwd(dco, proj, conv_w, "dn_conv_bwd")
    dproj = jnp.concatenate([dpre, dz], axis=1)
    g['w_in_odd'] = jnp.concatenate([_mm_wgrad(hn2, dproj, "in_odd_wgrad"),
                                     _mm_wgrad(hn2, dab, "in_odd_gates_wgrad")[:, :2 * DN_HEADS]], axis=1)
    dhn2 = _mm_dgrad(dab, w_ab, "in_odd_gates_dgrad")
    dhn2 = _mm_dgrad(dproj, w_io, "in_odd_dgrad", extras=(dhn2,), epilogue=lambda acc, other: (acc + other,))
    g['dn_conv_w'] = d_conv[:DN_CONV]
    g['dn_a_log'], g['dn_dt_bias'], g['dn_out_norm'] = d_alog[0, :DN_HEADS], d_dtb[0, :DN_HEADS], dg_dn[0]

    dhs, dm0, dg_pre_mix1, dg_post_mlp0 = _norm_bwd(
        dhs, pre=(hs2, row(pre_mix_norm[1]), dhn2), post=(m0, row(post_mlp_norm[0])), pad=pad, name="post_mlp_0_bwd")
    dw1_0, dw2_0, dhn1 = mlp_bwd(0, hn1, r0, ra0, dm0)
    dhs, dmix0, dg_pre_mlp0, dg_post_mix0 = _norm_bwd(
        dhs, pre=(hs1, row(pre_mlp_norm[0]), dhn1), post=(mix0, row(post_mix_norm[0])), pad=pad, name="post_mix_0_bwd")

    g['w_out_even'] = _mm_wgrad(merged, dmix0, "out_even_wgrad")
    dmerged = _mm_dgrad(dmix0, w_oe, "out_even_dgrad")
    _, do_sb, _, dg_sb = _norm_bwd(dmerged[:, :SB_WIDTH], post=(o_sb, row(sb_out_norm[0])), pad=pad, name="sb_out_norm_bwd")
    dq, dk4, dv4 = _sb_bwd(q, k, v, blocks_t(k), ssave, do_sb, pad, "sb_bwd")
    unheads = lambda t_: t_.transpose(1, 0, 2).reshape(r, SB_WIDTH)
    du, d_a, d_d, d_bglu, dg_s5, d_wb, d_wc, g['s5_w_glu'] = _s5_bwd(u, y_s5, dmerged[:, SB_WIDTH:], xstart, *s5_args, "s5_bwd")
    g_lr, g_li, g_dt, g_btr, g_bti = _s5_prep_bwd(
        lam_re, lam_im, logdt, btr, bti, d_a[0], d_a[1],
        _s5_diag_of_b(d_wb[0], s5_mask), _s5_diag_of_b(d_wb[1], s5_mask), "s5_prep_bwd")
    gg, nn, pp = S5_GROUPS, S5_STATE, S5_GROUP
    g['s5_lambda_re'], g['s5_lambda_im'] = g_lr.reshape(gg, nn), g_li.reshape(gg, nn)
    g['s5_log_dt'] = g_dt.reshape(gg, nn)[:, 0]
    g['s5_b_re'], g['s5_b_im'] = g_btr.T.reshape(gg, nn, pp), g_bti.T.reshape(gg, nn, pp)
    g['s5_c_re'] = _s5_diag_of_c(d_wc[0], s5_mask).reshape(gg, nn, pp).transpose(0, 2, 1)
    g['s5_c_im'] = _s5_diag_of_c(d_wc[1], s5_mask).reshape(gg, nn, pp).transpose(0, 2, 1)
    g['s5_d'], g['s5_b_glu'], g['s5_out_norm'], g['sb_out_norm'] = d_d[0], d_bglu[0], dg_s5[0], dg_sb[0]
    dqkvu = jnp.concatenate([dq, unheads(dk4), unheads(dv4), du], axis=1)
    g['w_in_even'] = _mm_wgrad(hn0, dqkvu, "in_even_wgrad")
    dhn0 = _mm_dgrad(dqkvu, w_ie, "in_even_dgrad")
    dhs, _, dg_pre_mix0, _ = _norm_bwd(dhs, pre=(hs0, row(pre_mix_norm[0]), dhn0), pad=pad, name="pre_mix_0_bwd")

    g['mlp_w1'] = jnp.stack([dw1_0, dw1_1])
    g['mlp_w2'] = jnp.stack([dw2_0, dw2_1])
    g['meta_tokens'] = dhs[pad:pad + N_META]
    g['pre_mix_norm'] = jnp.concatenate([dg_pre_mix0, dg_pre_mix1], axis=0)
    g['post_mix_norm'] = jnp.concatenate([dg_post_mix0, dg_post_mix1], axis=0)
    g['pre_mlp_norm'] = jnp.concatenate([dg_pre_mlp0, dg_pre_mlp1], axis=0)
    g['post_mlp_norm'] = jnp.concatenate([dg_post_mlp0, dg_post_mlp1], axis=0)
    grad_x = dhs[pad + N_META:][None]

    small_like = [w[n] for n in _SMALL]
    partial = [_to_blocks(n, g[n].reshape(full[n].shape)).astype(wire[n]) for n in _SHARDED]
    partial.append(_pack([g[n].reshape(w[n].shape) for n in _SMALL]))
    stacks = _exchange(partial, [True] * len(_SHARDED) + [False], "reduce_gradients")
    grads, deltas, new_m, new_v = {}, {}, {}, {}
    for n, st in zip(_SHARDED, stacks):
        outs = _adamw(st, _view2d(n, w[n]), _view2d(n, mom_m[n]), _view2d(n, mom_v[n]), f"adamw_{n}")
        grads[n], deltas[n], new_m[n], new_v[n] = (o.reshape(w[n].shape) for o in outs)
    outs = _adamw(stacks[-1], _pack(small_like), _pack([mom_m[n] for n in _SMALL]), _pack([mom_v[n] for n in _SMALL]),
                  "adamw_small")
    for dst, o in zip((grads, deltas, new_m, new_v), outs):
        for n, part in zip(_SMALL, _unpack(o, small_like)):
            dst[n] = part
    return (loss, grad_x, *[grads[n] for n in _WEIGHTS], *[deltas[n] for n in _WEIGHTS],
            *[new_m[n] for n in _WEIGHTS], *[new_v[n] for n in _WEIGHTS])
```

```python
import functools
import math

import jax
import jax.numpy as jnp
from jax import lax
from jax.experimental import pallas as pl
from jax.experimental.pallas import tpu as pltpu

F32 = jnp.float32
BF16 = jnp.bfloat16

D_MODEL = 1024
N_META = 16
SB_HEAD_DIM = 64
SB_WIDTH = 512
S5_WIDTH = 512
S5_GROUP = 16
S5_GROUPS = 32
S5_STATE = 64
S5_NS = S5_GROUPS * S5_STATE
DN_HEAD_DIM = 128
DN_HEADS = 8
DN_WIDTH = 1024
DN_CONV = 4
D_FF = 4096
EPS = 1e-6
N_DEV = 8

ADAM_LR = 0.001
ADAM_B1 = 0.9
ADAM_B2 = 0.999
ADAM_EPS = 1e-08
ADAM_WD = 0.01
ADAM_STEP = 10

ROW_TILE = 512
ATT_BLK = 256
SB_BLOCKS_PER_TRIP = 3
SB_LOG_ZERO = -106.0
SB_FWD_SKEW = False
SB_BWD_SKEW = True
DN_CHUNK = 128
DN_SUB = 16
S5_TILE = 128
VMEM_LIMIT = 56 * 1024 * 1024

_HIGH = lax.Precision.HIGHEST


def _pallas(body, **kw):
    return pl.pallas_call(body, **kw)


def _cparams(sem):
    return pltpu.CompilerParams(dimension_semantics=sem, vmem_limit_bytes=VMEM_LIMIT)


def _dot(a, b, dims=((1,), (0,))):
    return lax.dot_general(a, b, (dims, ((), ())), preferred_element_type=F32)


def _dot_hi(a, b):
    return lax.dot_general(a, b, (((1,), (0,)), ((), ())), preferred_element_type=F32, precision=_HIGH)


def _split_dot(m_bf16, x):
    hi = x.astype(BF16)
    lo = (x - hi.astype(F32)).astype(BF16)
    return _dot(m_bf16, hi) + _dot(m_bf16, lo)


def _matmul(a, b, *, ta=False, tb=False, tm, tn, tk, name, out_dtypes=(F32,), extras=(), epilogue=None):
    m, k = (a.shape[1], a.shape[0]) if ta else a.shape
    n = b.shape[0] if tb else b.shape[1]
    assert (b.shape[1] if tb else b.shape[0]) == k
    assert m % tm == 0 and n % tn == 0 and k % tk == 0, (name, m, n, k, tm, tn, tk)
    nk = k // tk
    n_ex = len(extras)
    n_out = len(out_dtypes)
    dims = ((0 if ta else 1,), (1 if tb else 0,))

    def finish(acc, ex_refs, o_refs):
        outs = (acc,) if epilogue is None else epilogue(acc, *[r[...] for r in ex_refs])
        for o_ref, o in zip(o_refs, outs):
            o_ref[...] = o.astype(o_ref.dtype)

    def body(*refs):
        a_ref, b_ref = refs[0], refs[1]
        ex_refs = refs[2:2 + n_ex]
        o_refs = refs[2 + n_ex:2 + n_ex + n_out]
        prod = _dot(a_ref[...].astype(BF16), b_ref[...].astype(BF16), dims)
        if nk == 1:
            finish(prod, ex_refs, o_refs)
            return
        acc_ref = refs[-1]
        kk = pl.program_id(2)

        @pl.when(kk == 0)
        def _():
            acc_ref[...] = prod

        @pl.when(kk > 0)
        def _():
            acc_ref[...] += prod

        @pl.when(kk == nk - 1)
        def _():
            finish(acc_ref[...], ex_refs, o_refs)

    a_spec = pl.BlockSpec((tk, tm), lambda j, i, kk: (kk, i)) if ta else pl.BlockSpec((tm, tk), lambda j, i, kk: (i, kk))
    b_spec = pl.BlockSpec((tn, tk), lambda j, i, kk: (j, kk)) if tb else pl.BlockSpec((tk, tn), lambda j, i, kk: (kk, j))
    o_spec = pl.BlockSpec((tm, tn), lambda j, i, kk: (i, j))
    outs = _pallas(
        body, name=name,
        grid=(n // tn, m // tm, nk),
        in_specs=[a_spec, b_spec] + [o_spec] * n_ex,
        out_specs=[o_spec] * n_out,
        out_shape=[jax.ShapeDtypeStruct((m, n), dt) for dt in out_dtypes],
        scratch_shapes=[] if nk == 1 else [pltpu.VMEM((tm, tn), F32)],
        compiler_params=_cparams(("parallel", "parallel", "arbitrary")),
    )(a, b, *extras)
    return outs[0] if n_out == 1 else outs


def _tile(n, cap):
    best = 128
    for t in range(128, min(n, cap) + 1, 128):
        if n % t == 0:
            best = t
    assert n % best == 0, n
    return best


MM_K_CAP = 4096
WGRAD_ROWS = 1536


def _mm_fwd(x, w, name, **kw):
    k, n = w.shape
    return _matmul(x, w, tm=ROW_TILE, tn=_tile(n, 1024), tk=_tile(k, MM_K_CAP), name=name, **kw)


def _mm_dgrad(dy, w, name, **kw):
    k, n = w.shape
    return _matmul(dy, w, tb=True, tm=ROW_TILE, tn=_tile(k, 1024), tk=_tile(n, MM_K_CAP), name=name, **kw)


def _mm_wgrad(x, dy, name):
    k, n = x.shape[1], dy.shape[1]
    rows = x.shape[0]
    return _matmul(x, dy, ta=True, tm=_tile(k, 512), tn=_tile(n, 1024),
                   tk=WGRAD_ROWS if rows % WGRAD_ROWS == 0 else ROW_TILE, name=name)


def _rms(x, g):
    r = lax.rsqrt(jnp.mean(x * x, axis=-1, keepdims=True) + EPS)
    return x * r * g


def _rms_bwd(x, g, dy):
    r = lax.rsqrt(jnp.mean(x * x, axis=-1, keepdims=True) + EPS)
    xh = x * r
    dxh = dy * g
    dx = r * (dxh - xh * jnp.mean(dxh * xh, axis=-1, keepdims=True))
    dg = jnp.sum(dy * xh, axis=0, keepdims=True)
    return dx, dg


def _row_spec(width, tile=ROW_TILE):
    return pl.BlockSpec((tile, width), lambda i: (i, 0))


def _vec_spec(width):
    return pl.BlockSpec((1, width), lambda i: (0, 0))


def _norm_pre(hs, g, name):
    r, d = hs.shape

    def body(x_ref, g_ref, o_ref):
        o_ref[...] = _rms(x_ref[...], g_ref[...]).astype(BF16)

    return _pallas(body, name=name, grid=(r // ROW_TILE,), in_specs=[_row_spec(d), _vec_spec(d)],
                   out_specs=_row_spec(d), out_shape=jax.ShapeDtypeStruct((r, d), BF16),
                   compiler_params=_cparams(("parallel",)))(hs, g)


def _norm_post_pre(hs, m, g_post, g_pre, name):
    r, d = hs.shape

    def body(hs_ref, m_ref, gp_ref, gn_ref, o_ref, hn_ref):
        new = hs_ref[...] + _rms(m_ref[...], gp_ref[...])
        o_ref[...] = new
        hn_ref[...] = _rms(new, gn_ref[...]).astype(BF16)

    return _pallas(body, name=name, grid=(r // ROW_TILE,),
                   in_specs=[_row_spec(d), _row_spec(d), _vec_spec(d), _vec_spec(d)],
                   out_specs=[_row_spec(d), _row_spec(d)],
                   out_shape=[jax.ShapeDtypeStruct((r, d), F32), jax.ShapeDtypeStruct((r, d), BF16)],
                   compiler_params=_cparams(("parallel",)))(hs, m, g_post, g_pre)


def _norm_post_loss(hs, m, g_post, target, pad_tiles, name):
    r, d = hs.shape
    nt = r // ROW_TILE

    def body(hs_ref, m_ref, gp_ref, t_ref, dhs_ref, loss_ref):
        i = pl.program_id(0)
        new = hs_ref[...] + _rms(m_ref[...], gp_ref[...])
        live = (i >= pad_tiles).astype(F32)
        diff = (new - t_ref[...]) * live
        dhs_ref[...] = diff * (1.0 / d)
        loss_ref[...] = jnp.full((8, 128), 0.5 / d * jnp.sum(diff * diff), F32)

    dhs, parts = _pallas(
        body, name=name, grid=(nt,),
        in_specs=[_row_spec(d), _row_spec(d), _vec_spec(d),
                  pl.BlockSpec((ROW_TILE, d), lambda i: (jnp.maximum(i - pad_tiles, 0), 0))],
        out_specs=[_row_spec(d), pl.BlockSpec((8, 128), lambda i: (i, 0))],
        out_shape=[jax.ShapeDtypeStruct((r, d), F32), jax.ShapeDtypeStruct((nt * 8, 128), F32)],
        compiler_params=_cparams(("parallel",)))(hs, m, g_post, target)
    return dhs, jnp.sum(parts[::8, 0])


def _norm_bwd(dhs, *, pre=None, post=None, pad=0, name):
    r, d = dhs.shape
    has_pre, has_post = pre is not None, post is not None

    def body(*refs):
        it = iter(refs)
        dhs_ref = next(it)
        if has_pre:
            hs_ref, gn_ref, dhn_ref = next(it), next(it), next(it)
        if has_post:
            m_ref, gp_ref = next(it), next(it)
        if has_pre:
            o_dhs, o_dgn = next(it), next(it)
        if has_post:
            o_dm, o_dgp = next(it), next(it)
        i = pl.program_id(0)
        live = (i * ROW_TILE + lax.broadcasted_iota(jnp.int32, (ROW_TILE, 1), 0)) >= pad
        cur = jnp.where(live, dhs_ref[...], 0.0)
        if has_pre:
            dx, dg = _rms_bwd(hs_ref[...], gn_ref[...], jnp.where(live, dhn_ref[...].astype(F32), 0.0))
            cur = cur + dx
            o_dhs[...] = cur

            @pl.when(i == 0)
            def _():
                o_dgn[...] = jnp.zeros_like(o_dgn)
            o_dgn[...] += dg
        if has_post:
            dm, dg = _rms_bwd(m_ref[...], gp_ref[...], cur)
            o_dm[...] = dm

            @pl.when(i == 0)
            def _():
                o_dgp[...] = jnp.zeros_like(o_dgp)
            o_dgp[...] += dg

    ins, in_specs, out_specs, out_shape = [dhs], [_row_spec(d)], [], []
    if has_pre:
        ins += list(pre)
        in_specs += [_row_spec(d), _vec_spec(d), _row_spec(d)]
        out_specs += [_row_spec(d), _vec_spec(d)]
        out_shape += [jax.ShapeDtypeStruct((r, d), F32), jax.ShapeDtypeStruct((1, d), F32)]
    if has_post:
        ins += list(post)
        in_specs += [_row_spec(d), _vec_spec(d)]
        out_specs += [_row_spec(d), _vec_spec(d)]
        out_shape += [jax.ShapeDtypeStruct((r, d), F32), jax.ShapeDtypeStruct((1, d), F32)]
    outs = list(_pallas(body, name=name, grid=(r // ROW_TILE,), in_specs=in_specs, out_specs=out_specs,
                        out_shape=out_shape, compiler_params=_cparams(("arbitrary",)))(*ins))
    dhs_new, dgn = (outs.pop(0), outs.pop(0)) if has_pre else (dhs, None)
    dm, dgp = (outs.pop(0), outs.pop(0)) if has_post else (None, None)
    return dhs_new, dm, dgn, dgp


def _softplus(z):
    return jnp.maximum(z, 0.0) + jnp.log(1.0 + jnp.exp(-jnp.abs(z)))


def _sb_consts(t):
    row = lax.broadcasted_iota(jnp.int32, (t, t), 0)
    col = lax.broadcasted_iota(jnp.int32, (t, t), 1)
    m_up = (col >= row).astype(BF16)
    m_low = (col <= row).astype(BF16)
    return m_up, m_low


def _emit_chains(chains, stages, skew):
    if skew:
        for step in range(len(chains) + len(stages) - 1):
            for si, stage in enumerate(stages):
                if 0 <= step - si < len(chains):
                    stage(chains[step - si])
    else:
        for stage in stages:
            for c in chains:
                stage(c)


def _sb_fwd(q, k, vt3, pad, name):
    r = q.shape[0]
    t = ATT_BLK
    nb = r // t
    nbp = -(-(nb + 1) // 8) * 8
    jmin = pad // t
    scale = SB_HEAD_DIM ** -0.5

    def body(q_ref, k_ref, vt_ref, o_ref, ss_ref, acc_ref, kn_ref):
        i = pl.program_id(1)

        @pl.when(i == 0)
        def _():
            def blk(b, m):
                kb = k_ref[pl.ds(pl.multiple_of(b * t, t), t), :].astype(F32)
                return jnp.maximum(m, jnp.max(jnp.sum(kb * kb, axis=1, keepdims=True), axis=0, keepdims=True))
            kn_ref[...] = jnp.broadcast_to(lax.fori_loop(0, nb, blk, jnp.zeros((1, 1), F32)), (8, 128))

        qf = q_ref[...].astype(F32)
        z_bound = scale * jnp.sqrt(jnp.max(jnp.sum(qf * qf, axis=1, keepdims=True)) * jnp.max(kn_ref[...]))

        def need(carry):
            return jnp.maximum(jnp.max(carry[0]), jnp.max(carry[1])) + z_bound >= SB_LOG_ZERO

        qt = qf.T
        sub = lax.broadcasted_iota(jnp.int32, (128, 1), 0)
        m_up, _ = _sb_consts(t)
        kpos0 = lax.broadcasted_iota(jnp.int32, (t, 1), 0)
        qpos = i * t + lax.broadcasted_iota(jnp.int32, (1, t), 1)
        n_mid = jnp.maximum(i - 1 - jmin, 0)
        n_edge = jnp.where(i > jmin, 1, 0)
        qths = [jnp.where((sub >= 64 * h) & (sub < 64 * (h + 1)), qt * scale, 0.0).astype(BF16) for h in range(2)]
        acc_ref[...] = jnp.zeros_like(acc_ref)

        def sweep(js, carry, masked):
            kbs = [k_ref[pl.ds(pl.multiple_of(j * t, t), t), :] for j in js]
            vts = [vt_ref[0, j] for j in js]
            accs = [acc_ref[0], acc_ref[1]]
            s = list(carry)
            chains = [(n, h) for n in range(len(js)) for h in range(2)]
            valid = [(js[n] * t + kpos0 < qpos) & (js[n] * t + kpos0 >= pad) for n in range(len(js))] if masked else None
            zt, inc, saves = {}, {}, []

            def st_scores(c):
                zt[c] = _dot(kbs[c[0]], qths[c[1]])

            def st_cumsum(c):
                lk = -_softplus(zt[c])
                if masked:
                    lk = jnp.where(valid[c[0]], lk, 0.0)
                inc[c] = _split_dot(m_up, lk)

            def st_weights(c):
                n, h = c
                saves.append((h, js[n], s[h]))
                w = jnp.exp(zt[c] + inc[c] + s[h])
                if masked:
                    w = jnp.where(valid[n], w, 0.0)
                accs[h] = accs[h] + _dot(vts[n], w.astype(BF16))
                s[h] = s[h] + inc[c][0:1, :]

            _emit_chains(chains, [st_scores, st_cumsum, st_weights], SB_FWD_SKEW)
            for h, j, val in saves:
                ss_ref[h, 0, pl.ds(j, 1), :] = val
            acc_ref[0] = accs[0]
            acc_ref[1] = accs[1]
            return tuple(s)

        zero = jnp.zeros((1, t), F32)
        bpi = SB_BLOCKS_PER_TRIP
        carry = sweep([i], (zero, zero), True)
        j, carry = lax.while_loop(
            lambda st: (st[0] == i - 1) & (st[0] > jmin) & need(st[1]),
            lambda st: (st[0] - 1, sweep([st[0]], st[1], False)), (i - 1, carry))
        j, carry = lax.while_loop(
            lambda st: (st[0] - bpi >= jmin) & need(st[1]),
            lambda st: (st[0] - bpi, sweep([st[0] - b for b in range(bpi)], st[1], False)), (j, carry))
        j, carry = lax.while_loop(
            lambda st: (st[0] > jmin) & need(st[1]),
            lambda st: (st[0] - 1, sweep([st[0]], st[1], False)), (j, carry))
        j, carry = lax.while_loop(
            lambda st: (st[0] == jmin) & (i > jmin) & need(st[1]),
            lambda st: (st[0] - 1, sweep([st[0]], st[1], True)), (j, carry))
        first = jnp.full((1, t), j + 1, jnp.int32).astype(F32)
        ss_ref[0, 0, nbp - 1:nbp, :] = first
        ss_ref[1, 0, nbp - 1:nbp, :] = first
        acc = jnp.where(sub < 64, acc_ref[0], acc_ref[1])
        o_ref[...] = acc.T

    return _pallas(
        body, name=name, grid=(4, nb),
        in_specs=[pl.BlockSpec((t, 128), lambda hp, i: (i, hp)),
                  pl.BlockSpec((r, 128), lambda hp, i: (0, hp)),
                  pl.BlockSpec((1, nb, 128, t), lambda hp, i: (hp, 0, 0, 0))],
        out_specs=[pl.BlockSpec((t, 128), lambda hp, i: (i, hp)),
                   pl.BlockSpec((2, 1, nbp, t), lambda hp, i: (hp, i, 0, 0))],
        out_shape=[jax.ShapeDtypeStruct((r, SB_WIDTH), F32),
                   jax.ShapeDtypeStruct((8, nb, nbp, t), F32)],
        scratch_shapes=[pltpu.VMEM((2, 128, t), F32), pltpu.VMEM((8, 128), F32)],
        compiler_params=_cparams(("arbitrary", "arbitrary")),
    )(q, k, vt3)


def _sb_bwd(q, k, v, kt3, ssave, do, pad, name):
    r = q.shape[0]
    t = ATT_BLK
    nb = r // t
    nbp = ssave.shape[2]
    jmin = pad // t
    scale = SB_HEAD_DIM ** -0.5

    def body(q_ref, do_ref, k_ref, v_ref, kt_ref, ss_ref, dq_ref, dk_hbm, dv_hbm, dk_acc, dv_acc, dq_acc, sem):
        hp = pl.program_id(0)
        i = pl.program_id(1)

        @pl.when(i == 0)
        def _():
            dk_acc[...] = jnp.zeros_like(dk_acc)
            dv_acc[...] = jnp.zeros_like(dv_acc)

        qf = q_ref[...].astype(F32)
        dof = do_ref[...]
        qt = qf.T
        dot_ = dof.T
        sub = lax.broadcasted_iota(jnp.int32, (128, 1), 0)
        lane = lax.broadcasted_iota(jnp.int32, (1, 128), 1)
        m_up, m_low = _sb_consts(t)
        kpos0 = lax.broadcasted_iota(jnp.int32, (t, 1), 0)
        qpos = i * t + lax.broadcasted_iota(jnp.int32, (1, t), 1)
        first = jnp.clip(jnp.max(ss_ref[0, 0, nbp - 1:nbp, :]).astype(jnp.int32), jmin, i)
        mid0 = jnp.maximum(first, jmin + 1)
        n_mid = jnp.maximum(i - mid0, 0)
        n_edge = jnp.where((i > jmin) & (first == jmin), 1, 0)
        in_t = [(sub >= 64 * h) & (sub < 64 * (h + 1)) for h in range(2)]
        in_l = [(lane >= 64 * h) & (lane < 64 * (h + 1)) for h in range(2)]
        qths = [jnp.where(in_t[h], qt * scale, 0.0).astype(BF16) for h in range(2)]
        doths = [jnp.where(in_t[h], dot_, 0.0).astype(BF16) for h in range(2)]
        qhs = [jnp.where(in_l[h], qf * scale, 0.0).astype(BF16) for h in range(2)]
        dohs = [jnp.where(in_l[h], dof, 0.0).astype(BF16) for h in range(2)]
        dq_acc[...] = jnp.zeros_like(dq_acc)

        def sweep(js, carry, masked):
            rows = [pl.ds(pl.multiple_of(j * t, t), t) for j in js]
            kbs = [k_ref[rw, :] for rw in rows]
            vbs = [v_ref[rw, :] for rw in rows]
            kts = [kt_ref[0, j] for j in js]
            sss = [[ss_ref[h, 0, pl.ds(j, 1), :] for h in range(2)] for j in js]
            dv_old = [dv_acc[rw, :] for rw in rows]
            dk_old = [dk_acc[rw, :] for rw in rows]
            dqs = [dq_acc[0], dq_acc[1]]
            ec = list(carry)
            chains = [(n, h) for n in range(len(js)) for h in range(2)]
            nch = len(chains)
            valid = [(js[n] * t + kpos0 < qpos) & (js[n] * t + kpos0 >= pad) for n in range(len(js))] if masked else None
            zt, dvt, sp, inc, e, big_e = {}, {}, {}, {}, {}, {}

            def st_scores(c):
                zt[c] = _dot(kbs[c[0]], qths[c[1]])
                dvt[c] = _dot(vbs[c[0]], doths[c[1]])

            def st_cumsum(c):
                sp[c] = _softplus(zt[c])
                lk = -sp[c]
                if masked:
                    lk = jnp.where(valid[c[0]], lk, 0.0)
                inc[c] = _split_dot(m_up, lk)

            def st_weights(c):
                n, h = c
                w = jnp.exp(zt[c] + inc[c] + sss[n][h])
                if masked:
                    w = jnp.where(valid[n], w, 0.0)
                dv_old[n] = dv_old[n] + _dot(w.astype(BF16), dohs[h])
                e[c] = w * dvt[c]
                pinc = _split_dot(m_low, e[c])
                big_e[c] = pinc - e[c] + ec[h]
                ec[h] = ec[h] + pinc[t - 1:t, :]

            def st_dscores(c):
                n, h = c
                dz = e[c] - jnp.exp(zt[c] - sp[c]) * (e[c] + big_e[c])
                if masked:
                    dz = jnp.where(valid[n], dz, 0.0)
                dzb = dz.astype(BF16)
                dqs[h] = dqs[h] + _dot(kts[n], dzb)
                dk_old[n] = dk_old[n] + _dot(dzb, qhs[h])

            _emit_chains(chains, [st_scores, st_cumsum, st_weights, st_dscores], SB_BWD_SKEW)
            for n, rw in enumerate(rows):
                dv_acc[rw, :] = dv_old[n]
                dk_acc[rw, :] = dk_old[n]
            dq_acc[0] = dqs[0]
            dq_acc[1] = dqs[1]
            return tuple(ec)

        zero = jnp.zeros((1, t), F32)
        bpi = SB_BLOCKS_PER_TRIP
        carry = lax.fori_loop(0, n_edge, lambda it, c: sweep([jmin + it * 0], c, True), (zero, zero))
        carry = lax.fori_loop(0, n_mid // bpi, lambda it, c: sweep([mid0 + bpi * it + b for b in range(bpi)], c, False), carry)
        n_rem = n_mid % bpi
        carry = lax.fori_loop(0, n_rem, lambda it, c: sweep([i - n_rem + it], c, False), carry)
        sweep([i], carry, True)
        dq_ref[...] = (jnp.where(sub < 64, dq_acc[0], dq_acc[1]) * scale).T

        @pl.when(i == nb - 1)
        def _():
            c1 = pltpu.make_async_copy(dk_acc, dk_hbm.at[hp], sem.at[0])
            c2 = pltpu.make_async_copy(dv_acc, dv_hbm.at[hp], sem.at[1])
            c1.start()
            c2.start()
            c1.wait()
            c2.wait()

    return _pallas(
        body, name=name, grid=(4, nb),
        in_specs=[pl.BlockSpec((t, 128), lambda hp, i: (i, hp)),
                  pl.BlockSpec((t, 128), lambda hp, i: (i, hp)),
                  pl.BlockSpec((r, 128), lambda hp, i: (0, hp)),
                  pl.BlockSpec((r, 128), lambda hp, i: (0, hp)),
                  pl.BlockSpec((1, nb, 128, t), lambda hp, i: (hp, 0, 0, 0)),
                  pl.BlockSpec((2, 1, nbp, t), lambda hp, i: (hp, i, 0, 0))],
        out_specs=[pl.BlockSpec((t, 128), lambda hp, i: (i, hp)),
                   pl.BlockSpec(memory_space=pl.ANY), pl.BlockSpec(memory_space=pl.ANY)],
        out_shape=[jax.ShapeDtypeStruct((r, SB_WIDTH), F32),
                   jax.ShapeDtypeStruct((4, r, 128), F32), jax.ShapeDtypeStruct((4, r, 128), F32)],
        scratch_shapes=[pltpu.VMEM((r, 128), F32), pltpu.VMEM((r, 128), F32), pltpu.VMEM((2, 128, t), F32),
                        pltpu.SemaphoreType.DMA((2,))],
        compiler_params=_cparams(("arbitrary", "arbitrary")),
    )(q, do, k, v, kt3, ssave)


def _s5_disc(lam_re, lam_im, logdt, btr, bti):
    lr = jnp.minimum(lam_re, -1e-4)
    li = lam_im
    dt = jnp.exp(logdt)
    mag = jnp.exp(lr * dt)
    ang = li * dt
    a_re, a_im = mag * jnp.cos(ang), mag * jnp.sin(ang)
    den = lr * lr + li * li
    nr, ni = a_re - 1.0, a_im
    c_re = (nr * lr + ni * li) / den
    c_im = (ni * lr - nr * li) / den
    return a_re, a_im, c_re * btr - c_im * bti, c_re * bti + c_im * btr


def _s5_prep(lam_re, lam_im, logdt, btr, bti, name):
    ns = lam_re.shape[1]

    def body(lr_ref, li_ref, dt_ref, br_ref, bi_ref, ar_ref, ai_ref, bbr_ref, bbi_ref):
        ar, ai, bbr, bbi = _s5_disc(lr_ref[...], li_ref[...], dt_ref[...], br_ref[...], bi_ref[...])
        ar_ref[...] = ar
        ai_ref[...] = ai
        bbr_ref[...] = bbr
        bbi_ref[...] = bbi

    return _pallas(body, name=name,
                   out_shape=[jax.ShapeDtypeStruct((1, ns), F32)] * 2 + [jax.ShapeDtypeStruct((S5_GROUP, ns), F32)] * 2,
                   )(lam_re, lam_im, logdt, btr, bti)


def _s5_prep_bwd(lam_re, lam_im, logdt, btr, bti, dar, dai, dbbr, dbbi, name):
    ns = lam_re.shape[1]

    def body(lr_ref, li_ref, dt_ref, br_ref, bi_ref, dar_ref, dai_ref, dbr_ref, dbi_ref, o_lr, o_li, o_dt, o_br, o_bi):
        _, vjp = jax.vjp(_s5_disc, lr_ref[...], li_ref[...], dt_ref[...], br_ref[...], bi_ref[...])
        g = vjp((dar_ref[...], dai_ref[...], dbr_ref[...], dbi_ref[...]))
        o_lr[...] = g[0]
        o_li[...] = g[1]
        row = lax.broadcasted_iota(jnp.int32, (ns, ns), 0) // S5_STATE
        col = lax.broadcasted_iota(jnp.int32, (ns, ns), 1) // S5_STATE
        same = (row == col).astype(F32)
        o_dt[...] = _dot_hi(jnp.broadcast_to(g[2], (8, ns)), same)[0:1]
        o_br[...] = g[3]
        o_bi[...] = g[4]

    return _pallas(body, name=name,
                   out_shape=[jax.ShapeDtypeStruct((1, ns), F32)] * 3 + [jax.ShapeDtypeStruct((S5_GROUP, ns), F32)] * 2,
                   compiler_params=pltpu.CompilerParams(vmem_limit_bytes=VMEM_LIMIT),
                   )(lam_re, lam_im, logdt, btr, bti, dar, dai, dbbr, dbbi)


def _s5_scan(br, bi, ar, ai, t, reverse=False):
    row = lax.broadcasted_iota(jnp.int32, (t, 1), 0)
    pr, pi_ = ar, ai
    k = 1
    while k < t:
        if reverse:
            sr, si, ok = pltpu.roll(br, t - k, 0), pltpu.roll(bi, t - k, 0), row < t - k
        else:
            sr, si, ok = pltpu.roll(br, k, 0), pltpu.roll(bi, k, 0), row >= k
        sr = jnp.where(ok, sr, 0.0)
        si = jnp.where(ok, si, 0.0)
        br, bi = br + pr * sr - pi_ * si, bi + pr * si + pi_ * sr
        pr, pi_ = pr * pr - pi_ * pi_, 2.0 * pr * pi_
        k *= 2
    return br, bi


def _s5_power_table(ar, ai, t, reverse=False):
    row = lax.broadcasted_iota(jnp.int32, (t, 1), 0)
    hot = row == (t - 1 if reverse else 0)
    return _s5_scan(jnp.where(hot, ar, 0.0), jnp.where(hot, ai, 0.0), ar, ai, t, reverse)


_GELU_C = math.sqrt(2.0 / math.pi)


def _gelu(y):
    th = jnp.tanh(_GELU_C * (y + 0.044715 * y * y * y))
    return 0.5 * y * (1.0 + th), th


def _sigmoid(x):
    return 1.0 / (1.0 + jnp.exp(-x))


def _s5_fwd(u, wb, a, wc, dskip, wglu, bglu, gnorm, name):
    r = u.shape[0]
    t = S5_TILE
    nt = r // t
    ns = wb.shape[2]
    w = S5_WIDTH

    def body(u_ref, wb_ref, a_ref, wc_ref, d_ref, wg_ref, bg_ref, gn_ref, y_ref, on_ref, xs_ref, pw_ref, carry_ref):
        i = pl.program_id(0)
        ar, ai = a_ref[0], a_ref[1]

        @pl.when(i == 0)
        def _():
            pr, pi_ = _s5_power_table(ar, ai, t)
            pw_ref[0] = pr
            pw_ref[1] = pi_
            carry_ref[...] = jnp.zeros_like(carry_ref)

        u_ = u_ref[...]
        ub = u_.astype(BF16)
        xr, xi = _s5_scan(_dot(ub, wb_ref[0]), _dot(ub, wb_ref[1]), ar, ai, t)
        cr, ci = carry_ref[0], carry_ref[1]
        xs_ref[0, 0:1, :] = cr
        xs_ref[0, 1:2, :] = ci
        pr, pi_ = pw_ref[0], pw_ref[1]
        xr = xr + pr * cr - pi_ * ci
        xi = xi + pr * ci + pi_ * cr
        carry_ref[0] = xr[t - 1:t, :]
        carry_ref[1] = xi[t - 1:t, :]
        y = _dot(xr.astype(BF16), wc_ref[0]) - _dot(xi.astype(BF16), wc_ref[1]) + d_ref[...] * u_
        h, _ = _gelu(y)
        gate = _sigmoid(_dot(h.astype(BF16), wg_ref[...]) + bg_ref[...])
        y_ref[...] = y
        on_ref[...] = _rms(h * gate, gn_ref[...]).astype(BF16)

    full = lambda shape: pl.BlockSpec(shape, lambda i: (0,) * len(shape))
    return _pallas(
        body, name=name, grid=(nt,),
        in_specs=[_row_spec(w, t), full((2, w, ns)), full((2, 1, ns)), full((2, ns, w)), full((1, w)),
                  full((w, w)), full((1, w)), full((1, w))],
        out_specs=[_row_spec(w, t), _row_spec(w, t), pl.BlockSpec((1, 2, ns), lambda i: (i, 0, 0))],
        out_shape=[jax.ShapeDtypeStruct((r, w), F32), jax.ShapeDtypeStruct((r, w), BF16),
                   jax.ShapeDtypeStruct((nt, 2, ns), F32)],
        scratch_shapes=[pltpu.VMEM((2, t, ns), F32), pltpu.VMEM((2, 1, ns), F32)],
        compiler_params=_cparams(("arbitrary",)),
    )(u, wb, a, wc, dskip, wglu, bglu, gnorm)


def _s5_bwd(u, y, don, xstart, wb, a, wc, dskip, wglu, bglu, gnorm, name):
    r = u.shape[0]
    t = S5_TILE
    nt = r // t
    ns = wb.shape[2]
    w = S5_WIDTH
    nt_dims = ((1,), (1,))
    tn_dims = ((0,), (0,))

    def body(u_ref, y_ref, don_ref, xs_ref, wb_hbm, a_ref, wc_hbm, d_ref, wg_ref, bg_ref, gn_ref,
             du_ref, da_ref, dd_ref, dbg_ref, dgn_ref, dwb_hbm, dwc_hbm, dwg_hbm,
             wb_ref, wc_ref, pw_ref, pwr_ref, lam_ref, acc_wb, acc_wc, acc_wg, sem):
        i = pl.program_id(0)
        ar, ai = a_ref[0], a_ref[1]

        @pl.when(i == 0)
        def _():
            c1 = pltpu.make_async_copy(wb_hbm, wb_ref, sem.at[0])
            c2 = pltpu.make_async_copy(wc_hbm, wc_ref, sem.at[1])
            c1.start()
            c2.start()
            pr, pi_ = _s5_power_table(ar, ai, t)
            pw_ref[0] = pr
            pw_ref[1] = pi_
            pr, pi_ = _s5_power_table(ar, -ai, t, reverse=True)
            pwr_ref[0] = pr
            pwr_ref[1] = pi_
            lam_ref[...] = jnp.zeros_like(lam_ref)
            acc_wb[...] = jnp.zeros_like(acc_wb)
            acc_wc[...] = jnp.zeros_like(acc_wc)
            acc_wg[...] = jnp.zeros_like(acc_wg)
            da_ref[...] = jnp.zeros_like(da_ref)
            dd_ref[...] = jnp.zeros_like(dd_ref)
            dbg_ref[...] = jnp.zeros_like(dbg_ref)
            dgn_ref[...] = jnp.zeros_like(dgn_ref)
            c1.wait()
            c2.wait()

        u_ = u_ref[...]
        y_ = y_ref[...]
        ub = u_.astype(BF16)
        h, th = _gelu(y_)
        hb = h.astype(BF16)
        wg = wg_ref[...]
        gate = _sigmoid(_dot(hb, wg) + bg_ref[...])
        d_out, dgn = _rms_bwd(h * gate, gn_ref[...], don_ref[...])
        dgn_ref[...] += dgn
        dhw = d_out * h * gate * (1.0 - gate)
        dhwb = dhw.astype(BF16)
        dh = d_out * gate + _dot(dhwb, wg, nt_dims)
        acc_wg[...] += _dot(hb, dhwb, tn_dims)
        dbg_ref[...] += jnp.sum(dhw, axis=0, keepdims=True)
        dgelu = 0.5 * (1.0 + th) + 0.5 * y_ * (1.0 - th * th) * _GELU_C * (1.0 + 3.0 * 0.044715 * y_ * y_)
        dy = dh * dgelu
        dd_ref[...] += jnp.sum(dy * u_, axis=0, keepdims=True)
        dyb = dy.astype(BF16)
        xr, xi = _s5_scan(_dot(ub, wb_ref[0]), _dot(ub, wb_ref[1]), ar, ai, t)
        cr, ci = xs_ref[0, 0:1, :], xs_ref[0, 1:2, :]
        pr, pi_ = pw_ref[0], pw_ref[1]
        xr = xr + pr * cr - pi_ * ci
        xi = xi + pr * ci + pi_ * cr
        acc_wc[0] += _dot(xr.astype(BF16), dyb, tn_dims)
        acc_wc[1] -= _dot(xi.astype(BF16), dyb, tn_dims)
        lr, li = _s5_scan(_dot(dyb, wc_ref[0], nt_dims), -_dot(dyb, wc_ref[1], nt_dims), ar, -ai, t, reverse=True)
        cr2, ci2 = lam_ref[0], lam_ref[1]
        pr, pi_ = pwr_ref[0], pwr_ref[1]
        lr = lr + pr * cr2 - pi_ * ci2
        li = li + pr * ci2 + pi_ * cr2
        lam_ref[0] = lr[0:1, :]
        lam_ref[1] = li[0:1, :]
        row = lax.broadcasted_iota(jnp.int32, (t, 1), 0)
        xpr = jnp.where(row == 0, cr, pltpu.roll(xr, 1, 0))
        xpi = jnp.where(row == 0, ci, pltpu.roll(xi, 1, 0))
        da_ref[0] += jnp.sum(lr * xpr + li * xpi, axis=0, keepdims=True)
        da_ref[1] += jnp.sum(li * xpr - lr * xpi, axis=0, keepdims=True)
        lrb, lib = lr.astype(BF16), li.astype(BF16)
        acc_wb[0] += _dot(ub, lrb, tn_dims)
        acc_wb[1] += _dot(ub, lib, tn_dims)
        du_ref[...] = d_ref[...] * dy + _dot(lrb, wb_ref[0], nt_dims) + _dot(lib, wb_ref[1], nt_dims)

        @pl.when(i == nt - 1)
        def _():
            cps = [pltpu.make_async_copy(acc_wb, dwb_hbm, sem.at[0]), pltpu.make_async_copy(acc_wc, dwc_hbm, sem.at[1]),
                   pltpu.make_async_copy(acc_wg, dwg_hbm, sem.at[2])]
            for c in cps:
                c.start()
            for c in cps:
                c.wait()

    rev = lambda i: (nt - 1 - i, 0)
    full = lambda shape: pl.BlockSpec(shape, lambda i: (0,) * len(shape))
    hbm = pl.BlockSpec(memory_space=pl.ANY)
    return _pallas(
        body, name=name, grid=(nt,),
        in_specs=[pl.BlockSpec((t, w), rev), pl.BlockSpec((t, w), rev), pl.BlockSpec((t, w), rev),
                  pl.BlockSpec((1, 2, ns), lambda i: (nt - 1 - i, 0, 0)), hbm, full((2, 1, ns)), hbm, full((1, w)),
                  full((w, w)), full((1, w)), full((1, w))],
        out_specs=[pl.BlockSpec((t, w), rev), full((2, 1, ns)), full((1, w)), full((1, w)), full((1, w)), hbm, hbm, hbm],
        out_shape=[jax.ShapeDtypeStruct((r, w), F32), jax.ShapeDtypeStruct((2, 1, ns), F32)]
        + [jax.ShapeDtypeStruct((1, w), F32)] * 3
        + [jax.ShapeDtypeStruct((2, w, ns), F32), jax.ShapeDtypeStruct((2, ns, w), F32), jax.ShapeDtypeStruct((w, w), F32)],
        scratch_shapes=[pltpu.VMEM((2, w, ns), BF16), pltpu.VMEM((2, ns, w), BF16),
                        pltpu.VMEM((2, t, ns), F32), pltpu.VMEM((2, t, ns), F32), pltpu.VMEM((2, 1, ns), F32),
                        pltpu.VMEM((2, w, ns), F32), pltpu.VMEM((2, ns, w), F32), pltpu.VMEM((w, w), F32),
                        pltpu.SemaphoreType.DMA((3,))],
        compiler_params=_cparams(("arbitrary",)),
    )(u, y, don, xstart, wb, a, wc, dskip, wglu, bglu, gnorm)


def _s5_expand(lam_re, lam_im, log_dt, b_re, b_im, c_re, c_im):
    g, n, p = S5_GROUPS, S5_STATE, S5_GROUP
    ns = g * n
    rows = lambda x: x.reshape(1, ns)
    logdt = jnp.repeat(log_dt.reshape(g), n).reshape(1, ns)
    btr = b_re.reshape(ns, p).T
    bti = b_im.reshape(ns, p).T
    ctr = c_re.transpose(0, 2, 1).reshape(ns, p)
    cti = c_im.transpose(0, 2, 1).reshape(ns, p)
    mask = (jnp.arange(g * p)[:, None] // p) == (jnp.arange(ns)[None, :] // n)
    return rows(lam_re), rows(lam_im), logdt, btr, bti, ctr, cti, mask


def _s5_block_diag_b(bb, mask):
    return jnp.where(mask, jnp.tile(bb, (S5_GROUPS, 1)), 0.0)


def _s5_block_diag_c(ct, mask):
    return jnp.where(mask.T, jnp.tile(ct, (1, S5_GROUPS)), 0.0)


def _s5_diag_of_b(dwb, mask):
    return jnp.where(mask, dwb, 0.0).reshape(S5_GROUPS, S5_GROUP, -1).sum(0)


def _s5_diag_of_c(dwc, mask):
    ns = dwc.shape[0]
    return jnp.where(mask.T, dwc, 0.0).reshape(ns, S5_GROUPS, S5_GROUP).sum(1)


DN_PRE_TILE = 256
_DN_QKV = 3 * DN_WIDTH


def _halo_specs(width, tile, nt, prev):
    per = tile // 8
    if prev:
        return pl.BlockSpec((8, width), lambda i: (jnp.maximum(i * per - 1, 0), 0))
    return pl.BlockSpec((8, width), lambda i: (jnp.minimum((i + 1) * per, nt * per - 1), 0))


def _shift_down(x, halo, s, t):
    xx = jnp.concatenate([halo, x], axis=0)
    return pltpu.roll(xx, s, 0)[8:]


def _shift_up(x, halo, s, t):
    xx = jnp.concatenate([x, halo], axis=0)
    return pltpu.roll(xx, t + 8 - s, 0)[:t]


def _silu(x):
    s = _sigmoid(x)
    return x * s, s


def _dn_gates(ab, alog, dtb, live):
    lane = lax.broadcasted_iota(jnp.int32, (1, 128), 1)
    g = -jnp.exp(alog) * _softplus(ab + dtb)
    beta = _sigmoid(ab)
    return jnp.where(live & (lane < DN_HEADS), g, jnp.where(live & (lane < 2 * DN_HEADS), beta, 0.0))


def _dn_pre_fwd(proj, ab, conv_w, alog, dtb, pad, name):
    r = proj.shape[0]
    t = DN_PRE_TILE
    nt = r // t
    scale = DN_HEAD_DIM ** -0.5

    def body(x_ref, halo_ref, ab_ref, w_ref, al_ref, dt_ref, co_ref, q_ref, k_ref, v_ref, gb_ref):
        i = pl.program_id(0)
        x = x_ref[...]
        halo = jnp.where(i > 0, halo_ref[...], 0.0)
        w = w_ref[...]
        co = w[3:4] * x
        for tap in range(DN_CONV - 1):
            co = co + w[tap:tap + 1] * _shift_down(x, halo, DN_CONV - 1 - tap, t)
        co_ref[...] = co
        act, _ = _silu(co)
        for hd in range(DN_HEADS):
            sl = slice(hd * 128, (hd + 1) * 128)
            for base, o_ref, sc in ((0, q_ref, scale), (DN_WIDTH, k_ref, 1.0)):
                xh = act[:, base + hd * 128: base + (hd + 1) * 128]
                o_ref[:, sl] = xh * (lax.rsqrt(jnp.sum(xh * xh, axis=-1, keepdims=True) + EPS) * sc)
        v_ref[...] = act[:, 2 * DN_WIDTH:]
        rows = i * t + lax.broadcasted_iota(jnp.int32, (t, 1), 0)
        gb_ref[...] = _dn_gates(ab_ref[...], al_ref[...], dt_ref[...], rows >= pad)

    return _pallas(
        body, name=name, grid=(nt,),
        in_specs=[pl.BlockSpec((t, _DN_QKV), lambda i: (i, 0)), _halo_specs(_DN_QKV, t, nt, True), _row_spec(128, t),
                  pl.BlockSpec((DN_CONV, _DN_QKV), lambda i: (0, 0)), _vec_spec(128), _vec_spec(128)],
        out_specs=[_row_spec(_DN_QKV, t), _row_spec(DN_WIDTH, t), _row_spec(DN_WIDTH, t), _row_spec(DN_WIDTH, t), _row_spec(128, t)],
        out_shape=[jax.ShapeDtypeStruct((r, _DN_QKV), F32)] + [jax.ShapeDtypeStruct((r, DN_WIDTH), F32)] * 3
        + [jax.ShapeDtypeStruct((r, 128), F32)],
        compiler_params=_cparams(("parallel",)),
    )(proj, proj, ab, conv_w, alog, dtb)


def _dn_pre_bwd(co, dq, dk, dv, dgb, ab, alog, dtb, pad, name):
    r = co.shape[0]
    t = DN_PRE_TILE
    nt = r // t
    scale = DN_HEAD_DIM ** -0.5

    def body(co_ref, dq_ref, dk_ref, dv_ref, dgb_ref, ab_ref, al_ref, dt_ref, dco_ref, dab_ref, dal_ref, ddt_ref):
        i = pl.program_id(0)

        @pl.when(i == 0)
        def _():
            dal_ref[...] = jnp.zeros_like(dal_ref)
            ddt_ref[...] = jnp.zeros_like(ddt_ref)

        co_ = co_ref[...]
        act, sg = _silu(co_)
        dsilu = sg * (1.0 + co_ * (1.0 - sg))
        for hd in range(DN_HEADS):
            sl = slice(hd * 128, (hd + 1) * 128)
            for base, d_ref, sc in ((0, dq_ref, scale), (DN_WIDTH, dk_ref, 1.0)):
                cs = slice(base + hd * 128, base + (hd + 1) * 128)
                xh = act[:, cs]
                rn = lax.rsqrt(jnp.sum(xh * xh, axis=-1, keepdims=True) + EPS)
                xhat = xh * rn
                dy = d_ref[:, sl]
                dx = (sc * rn) * (dy - xhat * jnp.sum(dy * xhat, axis=-1, keepdims=True))
                dco_ref[:, cs] = dx * dsilu[:, cs]
        dco_ref[:, 2 * DN_WIDTH:] = dv_ref[...] * dsilu[:, 2 * DN_WIDTH:]
        rows = i * t + lax.broadcasted_iota(jnp.int32, (t, 1), 0)
        live = rows >= pad
        lane = lax.broadcasted_iota(jnp.int32, (1, 128), 1)
        ab_ = ab_ref[...]
        dgb_ = dgb_ref[...]
        is_g = live & (lane < DN_HEADS)
        is_b = live & (lane >= DN_HEADS) & (lane < 2 * DN_HEADS)
        arg = ab_ + dt_ref[...]
        ea = jnp.exp(al_ref[...])
        da = jnp.where(is_g, -dgb_ * ea * _sigmoid(arg), 0.0)
        beta = _sigmoid(ab_)
        dab_ref[...] = da + jnp.where(is_b, dgb_ * beta * (1.0 - beta), 0.0)
        ddt_ref[...] += jnp.sum(da, axis=0, keepdims=True)
        dal_ref[...] += jnp.sum(jnp.where(is_g, -dgb_ * ea * _softplus(arg), 0.0), axis=0, keepdims=True)

    return _pallas(
        body, name=name, grid=(nt,),
        in_specs=[_row_spec(_DN_QKV, t), _row_spec(DN_WIDTH, t), _row_spec(DN_WIDTH, t), _row_spec(DN_WIDTH, t),
                  _row_spec(128, t), _row_spec(128, t), _vec_spec(128), _vec_spec(128)],
        out_specs=[_row_spec(_DN_QKV, t), _row_spec(128, t), _vec_spec(128), _vec_spec(128)],
        out_shape=[jax.ShapeDtypeStruct((r, _DN_QKV), F32), jax.ShapeDtypeStruct((r, 128), F32),
                   jax.ShapeDtypeStruct((1, 128), F32), jax.ShapeDtypeStruct((1, 128), F32)],
        compiler_params=_cparams(("arbitrary",)),
    )(co, dq, dk, dv, dgb, ab, alog, dtb)


def _dn_conv_bwd(dco, proj, conv_w, name):
    r = dco.shape[0]
    t = DN_PRE_TILE
    nt = r // t

    def body(d_ref, dh_ref, x_ref, xh_ref, w_ref, dx_ref, dw_ref):
        i = pl.program_id(0)

        @pl.when(i == 0)
        def _():
            dw_ref[...] = jnp.zeros_like(dw_ref)

        d = d_ref[...]
        dhalo = jnp.where(i < nt - 1, dh_ref[...], 0.0)
        x = x_ref[...]
        xhalo = jnp.where(i > 0, xh_ref[...], 0.0)
        w = w_ref[...]
        dx = w[3:4] * d
        dws = [None] * DN_CONV
        dws[3] = jnp.sum(d * x, axis=0, keepdims=True)
        for tap in range(DN_CONV - 1):
            s = DN_CONV - 1 - tap
            dx = dx + w[tap:tap + 1] * _shift_up(d, dhalo, s, t)
            dws[tap] = jnp.sum(d * _shift_down(x, xhalo, s, t), axis=0, keepdims=True)
        dx_ref[...] = dx
        dw_ref[...] += jnp.concatenate(dws + [jnp.zeros((8 - DN_CONV, _DN_QKV), F32)], axis=0)

    return _pallas(
        body, name=name, grid=(nt,),
        in_specs=[_row_spec(_DN_QKV, t), _halo_specs(_DN_QKV, t, nt, False),
                  pl.BlockSpec((t, _DN_QKV), lambda i: (i, 0)), _halo_specs(_DN_QKV, t, nt, True),
                  pl.BlockSpec((DN_CONV, _DN_QKV), lambda i: (0, 0))],
        out_specs=[_row_spec(_DN_QKV, t), pl.BlockSpec((8, _DN_QKV), lambda i: (0, 0))],
        out_shape=[jax.ShapeDtypeStruct((r, _DN_QKV), F32), jax.ShapeDtypeStruct((8, _DN_QKV), F32)],
        compiler_params=_cparams(("arbitrary",)),
    )(dco, dco, proj, proj, conv_w)


def _split3(x):
    hi = x.astype(BF16)
    return hi, (x - hi.astype(F32)).astype(BF16)


def _dot3s(a, b, dims=((1,), (0,))):
    return _dot(a[0], b[0], dims) + (_dot(a[0], b[1], dims) + _dot(a[1], b[0], dims))


def _dot3(a, b, dims=((1,), (0,))):
    return _dot3s(_split3(a), _split3(b), dims)


def _dn_inverse_many(n_mats):
    c = n_mats[0].shape[0]
    row = lax.broadcasted_iota(jnp.int32, (c, c), 0)
    col = lax.broadcasted_iota(jnp.int32, (c, c), 1)
    eye = (row == col).astype(F32)
    same = row // DN_SUB == col // DN_SUB
    nds = [jnp.where(same, n, 0.0) for n in n_mats]
    nos = [n - nd for n, nd in zip(n_mats, nds)]

    def geometric(bs, order):
        xs = [eye + b for b in bs]
        sp = [_split3(b) for b in bs]
        k = 2
        while k < order:
            sp = [_split3(_dot3s(s_, s_)) for s_ in sp]
            xs = [x + _dot3s(_split3(x), s_) for x, s_ in zip(xs, sp)]
            k *= 2
        return xs

    tds = [_split3(td) for td in geometric([-nd for nd in nds], DN_SUB)]
    ms = [_dot3s(td, _split3(no)) for td, no in zip(tds, nos)]
    xs = geometric([-m for m in ms], c // DN_SUB)
    return [_dot3s(_split3(x), td) for x, td in zip(xs, tds)]


def _dn_chunk_shared(gb_ref, gbt_ref):
    c = DN_CHUNK
    row = lax.broadcasted_iota(jnp.int32, (c, c), 0)
    col = lax.broadcasted_iota(jnp.int32, (c, c), 1)
    gbv = gb_ref[...]
    gam_all = _split_dot((row >= col).astype(BF16), gbv)
    hi, lo = _split3(gbt_ref[...])
    tri_t = (row <= col).astype(BF16)
    return dict(row=row, col=col, gbv=gbv, gam_all=gam_all, gam_rows=_dot(hi, tri_t) + _dot(lo, tri_t),
                lane=lax.broadcasted_iota(jnp.int32, (1, 128), 1))


def _dn_chunk_common(q, k, v, sh, h):
    c = DN_CHUNK
    row, col, lane = sh["row"], sh["col"], sh["lane"]
    gam = jnp.sum(jnp.where(lane == h, sh["gam_all"], 0.0), axis=1, keepdims=True)
    beta = jnp.sum(jnp.where(lane == h + DN_HEADS, sh["gbv"], 0.0), axis=1, keepdims=True)
    gam_row = sh["gam_rows"][h:h + 1]
    dec = jnp.where(row >= col, jnp.exp(jnp.minimum(gam - gam_row, 0.0)), 0.0)
    kb, qb = k.astype(BF16), q.astype(BF16)
    nt_dims = ((1,), (1,))
    kk = _dot(kb, kb, nt_dims)
    qk = _dot(qb, kb, nt_dims)
    eg = jnp.exp(gam)
    gam_l = gam[c - 1:c, :]
    return dict(q=q, k=k, v=v, qb=qb, kb=kb, gam=gam, beta=beta, dec=dec, kk=kk, qk=qk, eg=eg, gam_l=gam_l,
                row=row, col=col, lane=lane, att=qk * dec, qg=q * eg, kt=k * jnp.exp(gam_l - gam),
                rhs=jnp.concatenate([v * beta, k * (beta * eg)], axis=1))


def _dn_fwd(q, k, v, gb, gbt, name):
    r = q.shape[0]
    c = DN_CHUNK
    nc = r // c
    dh = DN_HEAD_DIM
    tn_dims = ((0,), (0,))

    def body(q_ref, k_ref, v_ref, gb_ref, gbt_ref, o_ref, ss_ref, ts_ref, s_ref):
        @pl.when(pl.program_id(0) == 0)
        def _():
            s_ref[...] = jnp.zeros_like(s_ref)

        heads = list(range(DN_HEADS))
        sl = [slice(h * dh, (h + 1) * dh) for h in heads]
        sh = _dn_chunk_shared(gb_ref, gbt_ref)
        zs = [_dn_chunk_common(q_ref[:, sl[h]], k_ref[:, sl[h]], v_ref[:, sl[h]], sh, h) for h in heads]
        t_invs = _dn_inverse_many([jnp.where(sh["row"] > sh["col"], z["beta"] * z["kk"] * z["dec"], 0.0) for z in zs])
        sols = [_dot3(t_inv, z["rhs"]) for t_inv, z in zip(t_invs, zs)]
        ss = [s_ref[h] for h in heads]
        sbs = [s.astype(BF16) for s in ss]
        vnbs = [(sol[:, :dh] - _dot(sol[:, dh:].astype(BF16), sb)).astype(BF16) for sol, sb in zip(sols, sbs)]
        for h in heads:
            o_ref[:, sl[h]] = _dot(zs[h]["qg"].astype(BF16), sbs[h]) + _dot(zs[h]["att"].astype(BF16), vnbs[h])
        for h in heads:
            ss_ref[0, h] = ss[h]
            ts_ref[0, h] = t_invs[h]
            s_ref[h] = ss[h] * jnp.exp(zs[h]["gam_l"]) + _dot(zs[h]["kt"].astype(BF16), vnbs[h], tn_dims)

    blk = pl.BlockSpec((c, DN_WIDTH), lambda ci: (ci, 0))
    sav = pl.BlockSpec((1, DN_HEADS, dh, dh), lambda ci: (ci, 0, 0, 0))
    return _pallas(
        body, name=name, grid=(nc,),
        in_specs=[blk, blk, blk, pl.BlockSpec((c, 128), lambda ci: (ci, 0)), pl.BlockSpec((16, c), lambda ci: (0, ci))],
        out_specs=[blk, sav, sav],
        out_shape=[jax.ShapeDtypeStruct((r, DN_WIDTH), F32), jax.ShapeDtypeStruct((nc, DN_HEADS, dh, dh), F32),
                   jax.ShapeDtypeStruct((nc, DN_HEADS, dh, dh), F32)],
        scratch_shapes=[pltpu.VMEM((DN_HEADS, dh, dh), F32)],
        compiler_params=_cparams(("arbitrary",)),
    )(q, k, v, gb, gbt)


def _dn_bwd(q, k, v, gb, gbt, ssave, tsave, do, name):
    r = q.shape[0]
    c = DN_CHUNK
    nc = r // c
    dh = DN_HEAD_DIM
    nt_dims = ((1,), (1,))
    tn_dims = ((0,), (0,))

    def body(q_ref, k_ref, v_ref, gb_ref, gbt_ref, ss_ref, ts_ref, do_ref, dq_ref, dk_ref, dv_ref, dgb_ref, ds_ref):
        @pl.when(pl.program_id(0) == 0)
        def _():
            ds_ref[...] = jnp.zeros_like(ds_ref)

        heads = list(range(DN_HEADS))
        sl = [slice(h * dh, (h + 1) * dh) for h in heads]
        sh = _dn_chunk_shared(gb_ref, gbt_ref)
        row, col, lane = sh["row"], sh["col"], sh["lane"]
        rs = lambda x: jnp.sum(x, axis=1, keepdims=True)
        tot = lambda x: jnp.sum(rs(x), axis=0, keepdims=True)
        st = [dict() for _ in heads]
        dgb_parts = []

        def s_common(h):
            st[h].update(_dn_chunk_common(q_ref[:, sl[h]], k_ref[:, sl[h]], v_ref[:, sl[h]], sh, h))
            st[h]["t"] = _split3(ts_ref[0, h])

        def s_sol(h):
            st[h]["sol"] = _dot3s(st[h]["t"], _split3(st[h]["rhs"]))

        def s_state(h):
            z = st[h]
            sol = z["sol"]
            kcd = sol[:, dh:]
            s = ss_ref[0, h]
            sb = s.astype(BF16)
            vnb = (sol[:, :dh] - _dot(kcd.astype(BF16), sb)).astype(BF16)
            ds_next = ds_ref[h]
            dsb = ds_next.astype(BF16)
            dob = do_ref[:, sl[h]].astype(BF16)
            z["dqg"] = _dot(dob, sb, nt_dims)
            ds = _dot(z["qg"].astype(BF16), dob, tn_dims)
            z["d_att"] = jnp.where(row >= col, _dot(dob, vnb, nt_dims), 0.0)
            dvn = _dot(z["att"].astype(BF16), dob, tn_dims) + _dot(z["kt"].astype(BF16), dsb)
            z["dkt"] = _dot(vnb, dsb, nt_dims)
            eg_l = jnp.exp(z["gam_l"])
            ds = ds + ds_next * eg_l
            z["dgam_l"] = tot(ds_next * s) * eg_l
            dvnb = dvn.astype(BF16)
            dkcd = -_dot(dvnb, sb, nt_dims)
            ds_ref[h] = ds - _dot(kcd.astype(BF16), dvnb, tn_dims)
            z["dsol"] = jnp.concatenate([dvn, dkcd], axis=1)

        def s_drhs(h):
            st[h]["drhs"] = _dot3s(st[h]["t"], _split3(st[h]["dsol"]), tn_dims)

        def s_dn(h):
            z = st[h]
            z["dn"] = jnp.where(row > col, -_dot3(z["drhs"], z["sol"], nt_dims), 0.0)

        def s_rest(h):
            z = st[h]
            k_, v_, kb, qb = z["k"], z["v"], z["kb"], z["qb"]
            beta, eg, dec, kk, qk, gam, gam_l = z["beta"], z["eg"], z["dec"], z["kk"], z["qk"], z["gam"], z["gam_l"]
            dn, d_att, dqg, dkt = z["dn"], z["d_att"], z["dqg"], z["dkt"]
            drv, drk = z["drhs"][:, :dh], z["drhs"][:, dh:]
            s_rkk = rs(drk * k_)
            dv_ref[:, sl[h]] = drv * beta
            dbeta = rs(drv * v_) + s_rkk * eg + rs(dn * kk * dec)
            dk = drk * (beta * eg)
            dgam = s_rkk * beta * eg
            dkk = (dn * beta * dec).astype(BF16)
            dd = dn * beta * kk + d_att * qk
            dqk = (d_att * dec).astype(BF16)
            dq_ref[:, sl[h]] = _dot(dqk, kb) + dqg * eg
            dk = dk + _dot(dqk, qb, tn_dims) + _dot(dkk, kb) + _dot(dkk, kb, tn_dims)
            w = dd * dec
            wh, wl = _split3(w)
            ones = jnp.ones((c, 128), BF16)
            col_sum = (_dot(wh, ones, tn_dims) + _dot(wl, ones, tn_dims))[:, 0:1]
            dgam = dgam + rs(w) - col_sum + rs(dqg * z["qg"]) - rs(dkt * z["kt"])
            dk_ref[:, sl[h]] = dk + dkt * jnp.exp(gam_l - gam)
            dgam_l = z["dgam_l"] + tot(dkt * z["kt"])
            rowc = lax.broadcasted_iota(jnp.int32, (c, 1), 0)
            dgam = dgam + jnp.where(rowc == c - 1, dgam_l, 0.0)
            dg = _split_dot((row <= col).astype(BF16), jnp.broadcast_to(dgam, (c, 128)))[:, 0:1]
            dgb_parts.append(jnp.where(lane == h, dg, 0.0) + jnp.where(lane == h + DN_HEADS, dbeta, 0.0))

        _emit_chains(heads, [s_common, s_sol, s_state, s_drhs, s_dn, s_rest], False)
        dgb = dgb_parts[0]
        for part in dgb_parts[1:]:
            dgb = dgb + part
        dgb_ref[...] = dgb

    blk = pl.BlockSpec((c, DN_WIDTH), lambda ci: (nc - 1 - ci, 0))
    sav = pl.BlockSpec((1, DN_HEADS, dh, dh), lambda ci: (nc - 1 - ci, 0, 0, 0))
    gspec = pl.BlockSpec((c, 128), lambda ci: (nc - 1 - ci, 0))
    return _pallas(
        body, name=name, grid=(nc,),
        in_specs=[blk, blk, blk, gspec, pl.BlockSpec((16, c), lambda ci: (0, nc - 1 - ci)), sav, sav, blk],
        out_specs=[blk, blk, blk, gspec],
        out_shape=[jax.ShapeDtypeStruct((r, DN_WIDTH), F32)] * 3 + [jax.ShapeDtypeStruct((r, 128), F32)],
        scratch_shapes=[pltpu.VMEM((DN_HEADS, dh, dh), F32)],
        compiler_params=_cparams(("arbitrary",)),
    )(q, k, v, gb, gbt, ssave, tsave, do)


def _dn_post_fwd(o, proj, g, name):
    r = o.shape[0]

    def body(o_ref, z_ref, g_ref, y_ref):
        g_ = g_ref[...]
        for hd in range(DN_HEADS):
            sl = slice(hd * 128, (hd + 1) * 128)
            sz, _ = _silu(z_ref[:, sl])
            y_ref[:, sl] = (_rms(o_ref[:, sl], g_) * sz).astype(BF16)

    return _pallas(body, name=name, grid=(r // ROW_TILE,),
                   in_specs=[_row_spec(DN_WIDTH), pl.BlockSpec((ROW_TILE, DN_WIDTH), lambda i: (i, 3)), _vec_spec(128)],
                   out_specs=_row_spec(DN_WIDTH), out_shape=jax.ShapeDtypeStruct((r, DN_WIDTH), BF16),
                   compiler_params=_cparams(("parallel",)))(o, proj, g)


def _dn_post_bwd(o, proj, g, dy, name):
    r = o.shape[0]

    def body(o_ref, z_ref, g_ref, dy_ref, do_ref, dz_ref, dg_ref):
        @pl.when(pl.program_id(0) == 0)
        def _():
            dg_ref[...] = jnp.zeros_like(dg_ref)

        g_ = g_ref[...]
        for hd in range(DN_HEADS):
            sl = slice(hd * 128, (hd + 1) * 128)
            z_ = z_ref[:, sl]
            sz, sg = _silu(z_)
            dy_ = dy_ref[:, sl]
            o_ = o_ref[:, sl]
            dz_ref[:, sl] = dy_ * _rms(o_, g_) * (sg * (1.0 + z_ * (1.0 - sg)))
            dx, dg = _rms_bwd(o_, g_, dy_ * sz)
            do_ref[:, sl] = dx
            dg_ref[...] += dg

    return _pallas(body, name=name, grid=(r // ROW_TILE,),
                   in_specs=[_row_spec(DN_WIDTH), pl.BlockSpec((ROW_TILE, DN_WIDTH), lambda i: (i, 3)), _vec_spec(128),
                             _row_spec(DN_WIDTH)],
                   out_specs=[_row_spec(DN_WIDTH), _row_spec(DN_WIDTH), _vec_spec(128)],
                   out_shape=[jax.ShapeDtypeStruct((r, DN_WIDTH), F32)] * 2 + [jax.ShapeDtypeStruct((1, 128), F32)],
                   compiler_params=_cparams(("arbitrary",)))(o, proj, g, dy)


def _exchange(arrays, scatter, name):
    n = len(arrays)
    outs_shape = [jax.ShapeDtypeStruct((N_DEV,) + (a.shape[1:] if sc else a.shape), a.dtype) for a, sc in zip(arrays, scatter)]

    def body(*refs):
        in_refs, out_refs = refs[:n], refs[n:2 * n]
        send_sems, recv_sems, local_sems = refs[2 * n:]
        mx, my, mc = lax.axis_index("x"), lax.axis_index("y"), lax.axis_index("c")
        me = 4 * mx + 2 * my + mc
        started = []
        for a in range(n):
            src_own = in_refs[a].at[me] if scatter[a] else in_refs[a]
            loc = pltpu.make_async_copy(src_own, out_refs[a].at[me], local_sems.at[a])
            loc.start()
            started.append(loc)
        remote = []
        for a in range(n):
            for kbits in range(1, N_DEV):
                px = lax.rem(mx + ((kbits >> 2) & 1), 2)
                py = lax.rem(my + ((kbits >> 1) & 1), 2)
                pc = lax.rem(mc + (kbits & 1), 2)
                src = in_refs[a].at[4 * px + 2 * py + pc] if scatter[a] else in_refs[a]
                cp = pltpu.make_async_remote_copy(
                    src_ref=src, dst_ref=out_refs[a].at[me],
                    send_sem=send_sems.at[a * N_DEV + kbits], recv_sem=recv_sems.at[a * N_DEV + kbits],
                    device_id=(px, py, pc), device_id_type=pl.DeviceIdType.MESH)
                cp.start()
                remote.append(cp)
        for cp in remote:
            cp.wait()
        for loc in started:
            loc.wait()

    hbm = pl.BlockSpec(memory_space=pl.ANY)
    return _pallas(
        body, name=name, in_specs=[hbm] * n, out_specs=[hbm] * n, out_shape=outs_shape,
        scratch_shapes=[pltpu.SemaphoreType.DMA((n * N_DEV,)), pltpu.SemaphoreType.DMA((n * N_DEV,)),
                        pltpu.SemaphoreType.DMA((n,))],
    )(*arrays)


def _adamw(gstack, w, m, v, name):
    a, b = w.shape
    ta = a
    for t in (1024, 512, 256, 128, 64, 32, 16, 8):
        if a % t == 0 and N_DEV * t * b * 4 <= 4 * 1024 * 1024:
            ta = t
            break
    c1 = 1.0 / (1.0 - ADAM_B1 ** ADAM_STEP)
    c2 = 1.0 / (1.0 - ADAM_B2 ** ADAM_STEP)

    def body(g_ref, w_ref, m_ref, v_ref, og_ref, od_ref, om_ref, ov_ref):
        g = g_ref[0].astype(F32)
        for s in range(1, N_DEV):
            g = g + g_ref[s].astype(F32)
        m_new = ADAM_B1 * m_ref[...] + (1.0 - ADAM_B1) * g
        v_new = ADAM_B2 * v_ref[...] + (1.0 - ADAM_B2) * (g * g)
        og_ref[...] = g
        om_ref[...] = m_new
        ov_ref[...] = v_new
        od_ref[...] = -ADAM_LR * ((m_new * c1) / (jnp.sqrt(v_new * c2) + ADAM_EPS) + ADAM_WD * w_ref[...])

    spec = pl.BlockSpec((ta, b), lambda i: (i, 0))
    return _pallas(
        body, name=name, grid=(a // ta,),
        in_specs=[pl.BlockSpec((N_DEV, ta, b), lambda i: (0, i, 0)), spec, spec, spec],
        out_specs=[spec] * 4, out_shape=[jax.ShapeDtypeStruct((a, b), F32)] * 4,
        compiler_params=_cparams(("parallel",)),
    )(gstack, w, m, v)


_WEIGHTS = ['meta_tokens', 'pre_mix_norm', 'post_mix_norm', 'pre_mlp_norm', 'post_mlp_norm', 'mlp_w1', 'mlp_w2',
            'w_in_even', 'w_out_even', 'sb_out_norm', 's5_lambda_re', 's5_lambda_im', 's5_log_dt', 's5_b_re', 's5_b_im',
            's5_c_re', 's5_c_im', 's5_d', 's5_w_glu', 's5_b_glu', 's5_out_norm', 'w_in_odd', 'dn_conv_w', 'dn_a_log',
            'dn_dt_bias', 'dn_out_norm', 'w_out_odd']
_SHARDED = ['meta_tokens', 'mlp_w1', 'mlp_w2', 'w_in_even', 'w_out_even', 's5_w_glu', 'w_in_odd', 'dn_conv_w', 'w_out_odd']
_SMALL = [n for n in _WEIGHTS if n not in _SHARDED]


def _view2d(name, a):
    return a.reshape(-1, a.shape[-1])


def _unshard(name, g):
    if name == 'mlp_w1':
        return g.reshape(N_DEV, 2, D_MODEL, -1).transpose(1, 2, 0, 3).reshape(2, D_MODEL, D_FF)
    if name == 'mlp_w2':
        return g.reshape(N_DEV, 2, -1, D_MODEL).transpose(1, 0, 2, 3).reshape(2, D_FF, D_MODEL)
    if name in ('w_in_even', 'w_in_odd', 'dn_conv_w', 'meta_tokens'):
        return g.transpose(1, 0, 2).reshape(g.shape[1], -1)
    return g.reshape(-1, g.shape[-1])


def _to_blocks(name, full):
    if name == 'mlp_w1':
        return full.reshape(2, D_MODEL, N_DEV, -1).transpose(2, 0, 1, 3).reshape(N_DEV, 2 * D_MODEL, -1)
    if name == 'mlp_w2':
        return full.reshape(2, N_DEV, -1, D_MODEL).transpose(1, 0, 2, 3).reshape(N_DEV, -1, D_MODEL)
    if name in ('w_in_even', 'w_in_odd', 'dn_conv_w', 'meta_tokens'):
        return full.reshape(full.shape[0], N_DEV, -1).transpose(1, 0, 2)
    return full.reshape(N_DEV, -1, full.shape[-1])


def _pack(parts):
    rows = []
    for p in parts:
        flat = p.reshape(-1)
        rows.append(jnp.pad(flat, (0, (-flat.shape[0]) % 128)).reshape(-1, 128))
    return jnp.concatenate(rows, axis=0)


def _unpack(packed, like):
    out, at = [], 0
    for p in like:
        n = math.prod(p.shape)
        nrow = -(-n // 128)
        out.append(packed[at:at + nrow].reshape(-1)[:n].reshape(p.shape))
        at += nrow
    return out


def _lane_vec(x, width=128):
    flat = x.reshape(-1)
    return jnp.pad(flat, (0, width - flat.shape[0])).reshape(1, width)


def kernel(x, meta_tokens, pre_mix_norm, post_mix_norm, pre_mlp_norm, post_mlp_norm, mlp_w1, mlp_w2, w_in_even, w_out_even, sb_out_norm, s5_lambda_re, s5_lambda_im, s5_log_dt, s5_b_re, s5_b_im, s5_c_re, s5_c_im, s5_d, s5_w_glu, s5_b_glu, s5_out_norm, w_in_odd, dn_conv_w, dn_a_log, dn_dt_bias, dn_out_norm, w_out_odd, loss_target, m_meta_tokens, m_pre_mix_norm, m_post_mix_norm, m_pre_mlp_norm, m_post_mlp_norm, m_mlp_w1, m_mlp_w2, m_w_in_even, m_w_out_even, m_sb_out_norm, m_s5_lambda_re, m_s5_lambda_im, m_s5_log_dt, m_s5_b_re, m_s5_b_im, m_s5_c_re, m_s5_c_im, m_s5_d, m_s5_w_glu, m_s5_b_glu, m_s5_out_norm, m_w_in_odd, m_dn_conv_w, m_dn_a_log, m_dn_dt_bias, m_dn_out_norm, m_w_out_odd, v_meta_tokens, v_pre_mix_norm, v_post_mix_norm, v_pre_mlp_norm, v_post_mlp_norm, v_mlp_w1, v_mlp_w2, v_w_in_even, v_w_out_even, v_sb_out_norm, v_s5_lambda_re, v_s5_lambda_im, v_s5_log_dt, v_s5_b_re, v_s5_b_im, v_s5_c_re, v_s5_c_im, v_s5_d, v_s5_w_glu, v_s5_b_glu, v_s5_out_norm, v_w_in_odd, v_dn_conv_w, v_dn_a_log, v_dn_dt_bias, v_dn_out_norm, v_w_out_odd):
    given = dict(locals())
    w = {n: given[n] for n in _WEIGHTS}
    mom_m = {n: given["m_" + n] for n in _WEIGHTS}
    mom_v = {n: given["v_" + n] for n in _WEIGHTS}

    seq = x.shape[1]
    assert x.shape[0] == 1 and seq % ROW_TILE == 0
    r = seq + ROW_TILE
    pad = ROW_TILE - N_META
    pad_tiles = 1

    wire = {n: (F32 if n in ('dn_conv_w', 'meta_tokens') else BF16) for n in _SHARDED}
    gathered = _exchange([_view2d(n, w[n]).astype(wire[n]) for n in _SHARDED], [False] * len(_SHARDED), "gather_weights")
    full = {n: _unshard(n, g) for n, g in zip(_SHARDED, gathered)}
    w1, w2 = full['mlp_w1'], full['mlp_w2']
    w_ie, w_oe, w_glu, w_oo = full['w_in_even'], full['w_out_even'], full['s5_w_glu'], full['w_out_odd']
    w_io = full['w_in_odd'][:, :4 * DN_WIDTH]
    w_ab = jnp.pad(full['w_in_odd'][:, 4 * DN_WIDTH:], ((0, 0), (0, 128 - 2 * DN_HEADS)))
    conv_w = full['dn_conv_w']
    row = lambda v_: v_.reshape(1, -1)

    hs0 = jnp.concatenate([jnp.zeros((pad, D_MODEL), F32), full['meta_tokens'], x[0]], axis=0)
    hn0 = _norm_pre(hs0, row(pre_mix_norm[0]), "pre_mix_0")
    qkv = _mm_fwd(hn0, w_ie[:, :3 * SB_WIDTH], "in_even_qkv", out_dtypes=(BF16,))
    u = _mm_fwd(hn0, w_ie[:, 3 * SB_WIDTH:], "in_even_u")
    q, k, v = qkv[:, :SB_WIDTH], qkv[:, SB_WIDTH:2 * SB_WIDTH], qkv[:, 2 * SB_WIDTH:]
    nb = r // ATT_BLK
    blocks_t = lambda t_: t_.reshape(nb, ATT_BLK, 4, 128).transpose(2, 0, 3, 1)
    o_sb, ssave = _sb_fwd(q, k, blocks_t(v), pad, "sb_fwd")
    on_sb = _norm_pre(o_sb, row(sb_out_norm[0]), "sb_out_norm")

    lam_re, lam_im, logdt, btr, bti, ctr, cti, s5_mask = _s5_expand(
        s5_lambda_re[0], s5_lambda_im[0], s5_log_dt[0], s5_b_re[0], s5_b_im[0], s5_c_re[0], s5_c_im[0])
    a_re, a_im, bbr, bbi = _s5_prep(lam_re, lam_im, logdt, btr, bti, "s5_prep")
    s5_wb = jnp.stack([_s5_block_diag_b(bbr, s5_mask), _s5_block_diag_b(bbi, s5_mask)]).astype(BF16)
    s5_wc = jnp.stack([_s5_block_diag_c(ctr, s5_mask), _s5_block_diag_c(cti, s5_mask)]).astype(BF16)
    s5_a = jnp.stack([a_re, a_im])
    s5_args = (s5_wb, s5_a, s5_wc, row(s5_d[0]), w_glu, row(s5_b_glu[0]), row(s5_out_norm[0]))
    y_s5, on_s5, xstart = _s5_fwd(u, *s5_args, "s5_fwd")

    merged = jnp.concatenate([on_sb, on_s5], axis=1)
    mix0 = _mm_fwd(merged, w_oe, "out_even")
    hs1, hn1 = _norm_post_pre(hs0, mix0, row(post_mix_norm[0]), row(pre_mlp_norm[0]), "post_mix_0")
    relu2 = lambda acc: (jnp.square(jnp.maximum(acc, 0.0)), jnp.maximum(acc, 0.0))
    r0, ra0 = _mm_fwd(hn1, w1[0], "mlp_up_0", out_dtypes=(BF16, BF16), epilogue=relu2)
    m0 = _mm_fwd(r0, w2[0], "mlp_down_0")
    hs2, hn2 = _norm_post_pre(hs1, m0, row(post_mlp_norm[0]), row(pre_mix_norm[1]), "post_mlp_0")

    proj = _mm_fwd(hn2, w_io, "in_odd")
    ab = _mm_fwd(hn2, w_ab, "in_odd_gates")
    alog, dtb = _lane_vec(dn_a_log[0]), _lane_vec(dn_dt_bias[0])
    co, qd, kd, vd, gb = _dn_pre_fwd(proj, ab, conv_w, alog, dtb, pad, "dn_pre")
    gbt = gb[:, :2 * DN_HEADS].T
    o_dn, s_dn, t_dn = _dn_fwd(qd, kd, vd, gb, gbt, "dn_fwd")
    on_dn = _dn_post_fwd(o_dn, proj, row(dn_out_norm[0]), "dn_post")
    mix1 = _mm_fwd(on_dn, w_oo, "out_odd")
    hs3, hn3 = _norm_post_pre(hs2, mix1, row(post_mix_norm[1]), row(pre_mlp_norm[1]), "post_mix_1")
    r1, ra1 = _mm_fwd(hn3, w1[1], "mlp_up_1", out_dtypes=(BF16, BF16), epilogue=relu2)
    m1 = _mm_fwd(r1, w2[1], "mlp_down_1")
    dhs, loss_part = _norm_post_loss(hs3, m1, row(post_mlp_norm[1]), loss_target[0], pad_tiles, "post_mlp_1_loss")
    loss = lax.psum(loss_part, ("x", "y", "c"))

    g = {}
    drelu2 = lambda acc, ra: (acc * (2.0 * ra.astype(F32)),)

    def mlp_bwd(layer, hn, rr, ra, dm):
        dw2 = _mm_wgrad(rr, dm, f"mlp_down_{layer}_wgrad")
        da = _mm_dgrad(dm, w2[layer], f"mlp_down_{layer}_dgrad", out_dtypes=(BF16,), extras=(ra,), epilogue=drelu2)
        dw1 = _mm_wgrad(hn, da, f"mlp_up_{layer}_wgrad")
        return dw1, dw2, _mm_dgrad(da, w1[layer], f"mlp_up_{layer}_dgrad")

    _, dm1, _, dg_post_mlp1 = _norm_bwd(dhs, post=(m1, row(post_mlp_norm[1])), pad=pad, name="post_mlp_1_bwd")
    dw1_1, dw2_1, dhn3 = mlp_bwd(1, hn3, r1, ra1, dm1)
    dhs, dmix1, dg_pre_mlp1, dg_post_mix1 = _norm_bwd(
        dhs, pre=(hs3, row(pre_mlp_norm[1]), dhn3), post=(mix1, row(post_mix_norm[1])), pad=pad, name="post_mix_1_bwd")

    g['w_out_odd'] = _mm_wgrad(on_dn, dmix1, "out_odd_wgrad")
    d_on_dn = _mm_dgrad(dmix1, w_oo, "out_odd_dgrad")
    do_dn, dz, dg_dn = _dn_post_bwd(o_dn, proj, row(dn_out_norm[0]), d_on_dn, "dn_post_bwd")
    dqd, dkd, dvd, dgb = _dn_bwd(qd, kd, vd, gb, gbt, s_dn, t_dn, do_dn, "dn_bwd")
    dco, dab, d_alog, d_dtb = _dn_pre_bwd(co, dqd, dkd, dvd, dgb, ab, alog, dtb, pad, "dn_pre_bwd")
    dpre, d_conv = _dn_conv_bwd(dco, proj, conv_w, "dn_conv_bwd")
    dproj = jnp.concatenate([dpre, dz], axis=1)
    g['w_in_odd'] = jnp.concatenate([_mm_wgrad(hn2, dproj, "in_odd_wgrad"),
                                     _mm_wgrad(hn2, dab, "in_odd_gates_wgrad")[:, :2 * DN_HEADS]], axis=1)
    dhn2 = _mm_dgrad(dab, w_ab, "in_odd_gates_dgrad")
    dhn2 = _mm_dgrad(dproj, w_io, "in_odd_dgrad", extras=(dhn2,), epilogue=lambda acc, other: (acc + other,))
    g['dn_conv_w'] = d_conv[:DN_CONV]
    g['dn_a_log'], g['dn_dt_bias'], g['dn_out_norm'] = d_alog[0, :DN_HEADS], d_dtb[0, :DN_HEADS], dg_dn[0]

    dhs, dm0, dg_pre_mix1, dg_post_mlp0 = _norm_bwd(
        dhs, pre=(hs2, row(pre_mix_norm[1]), dhn2), post=(m0, row(post_mlp_norm[0])), pad=pad, name="post_mlp_0_bwd")
    dw1_0, dw2_0, dhn1 = mlp_bwd(0, hn1, r0, ra0, dm0)
    dhs, dmix0, dg_pre_mlp0, dg_post_mix0 = _norm_bwd(
        dhs, pre=(hs1, row(pre_mlp_norm[0]), dhn1), post=(mix0, row(post_mix_norm[0])), pad=pad, name="post_mix_0_bwd")

    g['w_out_even'] = _mm_wgrad(merged, dmix0, "out_even_wgrad")
    dmerged = _mm_dgrad(dmix0, w_oe, "out_even_dgrad")
    _, do_sb, _, dg_sb = _norm_bwd(dmerged[:, :SB_WIDTH], post=(o_sb, row(sb_out_norm[0])), pad=pad, name="sb_out_norm_bwd")
    dq, dk4, dv4 = _sb_bwd(q, k, v, blocks_t(k), ssave, do_sb, pad, "sb_bwd")
    unheads = lambda t_: t_.transpose(1, 0, 2).reshape(r, SB_WIDTH)
    du, d_a, d_d, d_bglu, dg_s5, d_wb, d_wc, g['s5_w_glu'] = _s5_bwd(u, y_s5, dmerged[:, SB_WIDTH:], xstart, *s5_args, "s5_bwd")
    g_lr, g_li, g_dt, g_btr, g_bti = _s5_prep_bwd(
        lam_re, lam_im, logdt, btr, bti, d_a[0], d_a[1],
        _s5_diag_of_b(d_wb[0], s5_mask), _s5_diag_of_b(d_wb[1], s5_mask), "s5_prep_bwd")
    gg, nn, pp = S5_GROUPS, S5_STATE, S5_GROUP
    g['s5_lambda_re'], g['s5_lambda_im'] = g_lr.reshape(gg, nn), g_li.reshape(gg, nn)
    g['s5_log_dt'] = g_dt.reshape(gg, nn)[:, 0]
    g['s5_b_re'], g['s5_b_im'] = g_btr.T.reshape(gg, nn, pp), g_bti.T.reshape(gg, nn, pp)
    g['s5_c_re'] = _s5_diag_of_c(d_wc[0], s5_mask).reshape(gg, nn, pp).transpose(0, 2, 1)
    g['s5_c_im'] = _s5_diag_of_c(d_wc[1], s5_mask).reshape(gg, nn, pp).transpose(0, 2, 1)
    g['s5_d'], g['s5_b_glu'], g['s5_out_norm'], g['sb_out_norm'] = d_d[0], d_bglu[0], dg_s5[0], dg_sb[0]
    dqkvu = jnp.concatenate([dq, unheads(dk4), unheads(dv4), du], axis=1)
    g['w_in_even'] = _mm_wgrad(hn0, dqkvu, "in_even_wgrad")
    dhn0 = _mm_dgrad(dqkvu, w_ie, "in_even_dgrad")
    dhs, _, dg_pre_mix0, _ = _norm_bwd(dhs, pre=(hs0, row(pre_mix_norm[0]), dhn0), pad=pad, name="pre_mix_0_bwd")

    g['mlp_w1'] = jnp.stack([dw1_0, dw1_1])
    g['mlp_w2'] = jnp.stack([dw2_0, dw2_1])
    g['meta_tokens'] = dhs[pad:pad + N_META]
    g['pre_mix_norm'] = jnp.concatenate([dg_pre_mix0, dg_pre_mix1], axis=0)
    g['post_mix_norm'] = jnp.concatenate([dg_post_mix0, dg_post_mix1], axis=0)
    g['pre_mlp_norm'] = jnp.concatenate([dg_pre_mlp0, dg_pre_mlp1], axis=0)
    g['post_mlp_norm'] = jnp.concatenate([dg_post_mlp0, dg_post_mlp1], axis=0)
    grad_x = dhs[pad + N_META:][None]

    small_like = [w[n] for n in _SMALL]
    partial = [_to_blocks(n, g[n].reshape(full[n].shape)).astype(wire[n]) for n in _SHARDED]
    partial.append(_pack([g[n].reshape(w[n].shape) for n in _SMALL]))
    stacks = _exchange(partial, [True] * len(_SHARDED) + [False], "reduce_gradients")
    grads, deltas, new_m, new_v = {}, {}, {}, {}
    for n, st in zip(_SHARDED, stacks):
        outs = _adamw(st, _view2d(n, w[n]), _view2d(n, mom_m[n]), _view2d(n, mom_v[n]), f"adamw_{n}")
        grads[n], deltas[n], new_m[n], new_v[n] = (o.reshape(w[n].shape) for o in outs)
    outs = _adamw(stacks[-1], _pack(small_like), _pack([mom_m[n] for n in _SMALL]), _pack([mom_v[n] for n in _SMALL]),
                  "adamw_small")
    for dst, o in zip((grads, deltas, new_m, new_v), outs):
        for n, part in zip(_SMALL, _unpack(o, small_like)):
            dst[n] = part
    return (loss, grad_x, *[grads[n] for n in _WEIGHTS], *[deltas[n] for n in _WEIGHTS],
            *[new_m[n] for n in _WEIGHTS], *[new_v[n] for n in _WEIGHTS])
```

```python
import functools
import math

import jax
import jax.numpy as jnp
from jax import lax
from jax.experimental import pallas as pl
from jax.experimental.pallas import tpu as pltpu

F32 = jnp.float32
BF16 = jnp.bfloat16

D_MODEL = 1024
N_META = 16
SB_HEAD_DIM = 64
SB_WIDTH = 512
S5_WIDTH = 512
S5_GROUP = 16
S5_GROUPS = 32
S5_STATE = 64
S5_NS = S5_GROUPS * S5_STATE
DN_HEAD_DIM = 128
DN_HEADS = 8
DN_WIDTH = 1024
DN_CONV = 4
D_FF = 4096
EPS = 1e-6
N_DEV = 8

ADAM_LR = 0.001
ADAM_B1 = 0.9
ADAM_B2 = 0.999
ADAM_EPS = 1e-08
ADAM_WD = 0.01
ADAM_STEP = 10

ROW_TILE = 512
ATT_BLK = 256
SB_BLOCKS_PER_TRIP = 3
SB_LOG_ZERO = -106.0
SB_FWD_SKEW = False
SB_BWD_SKEW = True
DN_CHUNK = 128
DN_SUB = 16
S5_TILE = 128
VMEM_LIMIT = 56 * 1024 * 1024

_HIGH = lax.Precision.HIGHEST


def _pallas(body, **kw):
    return pl.pallas_call(body, **kw)


def _cparams(sem):
    return pltpu.CompilerParams(dimension_semantics=sem, vmem_limit_bytes=VMEM_LIMIT)


def _dot(a, b, dims=((1,), (0,))):
    return lax.dot_general(a, b, (dims, ((), ())), preferred_element_type=F32)


def _dot_hi(a, b):
    return lax.dot_general(a, b, (((1,), (0,)), ((), ())), preferred_element_type=F32, precision=_HIGH)


def _split_dot(m_bf16, x):
    hi = x.astype(BF16)
    lo = (x - hi.astype(F32)).astype(BF16)
    return _dot(m_bf16, hi) + _dot(m_bf16, lo)


def _matmul(a, b, *, ta=False, tb=False, tm, tn, tk, name, out_dtypes=(F32,), extras=(), epilogue=None):
    m, k = (a.shape[1], a.shape[0]) if ta else a.shape
    n = b.shape[0] if tb else b.shape[1]
    assert (b.shape[1] if tb else b.shape[0]) == k
    assert m % tm == 0 and n % tn == 0 and k % tk == 0, (name, m, n, k, tm, tn, tk)
    nk = k // tk
    n_ex = len(extras)
    n_out = len(out_dtypes)
    dims = ((0 if ta else 1,), (1 if tb else 0,))

    def finish(acc, ex_refs, o_refs):
        outs = (acc,) if epilogue is None else epilogue(acc, *[r[...] for r in ex_refs])
        for o_ref, o in zip(o_refs, outs):
            o_ref[...] = o.astype(o_ref.dtype)

    def body(*refs):
        a_ref, b_ref = refs[0], refs[1]
        ex_refs = refs[2:2 + n_ex]
        o_refs = refs[2 + n_ex:2 + n_ex + n_out]
        prod = _dot(a_ref[...].astype(BF16), b_ref[...].astype(BF16), dims)
        if nk == 1:
            finish(prod, ex_refs, o_refs)
            return
        acc_ref = refs[-1]
        kk = pl.program_id(2)

        @pl.when(kk == 0)
        def _():
            acc_ref[...] = prod

        @pl.when(kk > 0)
        def _():
            acc_ref[...] += prod

        @pl.when(kk == nk - 1)
        def _():
            finish(acc_ref[...], ex_refs, o_refs)

    a_spec = pl.BlockSpec((tk, tm), lambda j, i, kk: (kk, i)) if ta else pl.BlockSpec((tm, tk), lambda j, i, kk: (i, kk))
    b_spec = pl.BlockSpec((tn, tk), lambda j, i, kk: (j, kk)) if tb else pl.BlockSpec((tk, tn), lambda j, i, kk: (kk, j))
    o_spec = pl.BlockSpec((tm, tn), lambda j, i, kk: (i, j))
    outs = _pallas(
        body, name=name,
        grid=(n // tn, m // tm, nk),
        in_specs=[a_spec, b_spec] + [o_spec] * n_ex,
        out_specs=[o_spec] * n_out,
        out_shape=[jax.ShapeDtypeStruct((m, n), dt) for dt in out_dtypes],
        scratch_shapes=[] if nk == 1 else [pltpu.VMEM((tm, tn), F32)],
        compiler_params=_cparams(("parallel", "parallel", "arbitrary")),
    )(a, b, *extras)
    return outs[0] if n_out == 1 else outs


def _tile(n, cap):
    best = 128
    for t in range(128, min(n, cap) + 1, 128):
        if n % t == 0:
            best = t
    assert n % best == 0, n
    return best


MM_K_CAP = 4096
WGRAD_ROWS = 1536


MM_LHS_TILE_BYTES = 6 * 1024 * 1024


def _row_tile(x, depth):
    tall = 3 * ROW_TILE
    fits = tall * depth * x.dtype.itemsize <= MM_LHS_TILE_BYTES
    return tall if (x.shape[0] % tall == 0 and fits) else ROW_TILE


def _mm_fwd(x, w, name, **kw):
    k, n = w.shape
    tk = _tile(k, MM_K_CAP)
    return _matmul(x, w, tm=_row_tile(x, tk), tn=_tile(n, 1024), tk=tk, name=name, **kw)


def _mm_dgrad(dy, w, name, **kw):
    k, n = w.shape
    tk = _tile(n, MM_K_CAP)
    return _matmul(dy, w, tb=True, tm=_row_tile(dy, tk), tn=_tile(k, 1024), tk=tk, name=name, **kw)


def _mm_wgrad(x, dy, name):
    k, n = x.shape[1], dy.shape[1]
    rows = x.shape[0]
    return _matmul(x, dy, ta=True, tm=_tile(k, 512), tn=_tile(n, 1024),
                   tk=WGRAD_ROWS if rows % WGRAD_ROWS == 0 else ROW_TILE, name=name)


def _rms(x, g):
    r = lax.rsqrt(jnp.mean(x * x, axis=-1, keepdims=True) + EPS)
    return x * r * g


def _rms_bwd(x, g, dy):
    r = lax.rsqrt(jnp.mean(x * x, axis=-1, keepdims=True) + EPS)
    xh = x * r
    dxh = dy * g
    dx = r * (dxh - xh * jnp.mean(dxh * xh, axis=-1, keepdims=True))
    dg = jnp.sum(dy * xh, axis=0, keepdims=True)
    return dx, dg


def _row_spec(width, tile=ROW_TILE):
    return pl.BlockSpec((tile, width), lambda i: (i, 0))


def _vec_spec(width):
    return pl.BlockSpec((1, width), lambda i: (0, 0))


def _norm_pre(hs, g, name):
    r, d = hs.shape

    def body(x_ref, g_ref, o_ref):
        o_ref[...] = _rms(x_ref[...], g_ref[...]).astype(BF16)

    return _pallas(body, name=name, grid=(r // ROW_TILE,), in_specs=[_row_spec(d), _vec_spec(d)],
                   out_specs=_row_spec(d), out_shape=jax.ShapeDtypeStruct((r, d), BF16),
                   compiler_params=_cparams(("parallel",)))(hs, g)


def _norm_post_pre(hs, m, g_post, g_pre, name):
    r, d = hs.shape

    def body(hs_ref, m_ref, gp_ref, gn_ref, o_ref, hn_ref):
        new = hs_ref[...] + _rms(m_ref[...], gp_ref[...])
        o_ref[...] = new
        hn_ref[...] = _rms(new, gn_ref[...]).astype(BF16)

    return _pallas(body, name=name, grid=(r // ROW_TILE,),
                   in_specs=[_row_spec(d), _row_spec(d), _vec_spec(d), _vec_spec(d)],
                   out_specs=[_row_spec(d), _row_spec(d)],
                   out_shape=[jax.ShapeDtypeStruct((r, d), F32), jax.ShapeDtypeStruct((r, d), BF16)],
                   compiler_params=_cparams(("parallel",)))(hs, m, g_post, g_pre)


def _norm_post_loss(hs, m, g_post, target, pad_tiles, name):
    r, d = hs.shape
    nt = r // ROW_TILE

    def body(hs_ref, m_ref, gp_ref, t_ref, dhs_ref, loss_ref):
        i = pl.program_id(0)
        new = hs_ref[...] + _rms(m_ref[...], gp_ref[...])
        live = (i >= pad_tiles).astype(F32)
        diff = (new - t_ref[...]) * live
        dhs_ref[...] = diff * (1.0 / d)
        loss_ref[...] = jnp.full((8, 128), 0.5 / d * jnp.sum(diff * diff), F32)

    dhs, parts = _pallas(
        body, name=name, grid=(nt,),
        in_specs=[_row_spec(d), _row_spec(d), _vec_spec(d),
                  pl.BlockSpec((ROW_TILE, d), lambda i: (jnp.maximum(i - pad_tiles, 0), 0))],
        out_specs=[_row_spec(d), pl.BlockSpec((8, 128), lambda i: (i, 0))],
        out_shape=[jax.ShapeDtypeStruct((r, d), F32), jax.ShapeDtypeStruct((nt * 8, 128), F32)],
        compiler_params=_cparams(("parallel",)))(hs, m, g_post, target)
    return dhs, jnp.sum(parts[::8, 0])


def _norm_bwd(dhs, *, pre=None, post=None, pad=0, name):
    r, d = dhs.shape
    has_pre, has_post = pre is not None, post is not None

    def body(*refs):
        it = iter(refs)
        dhs_ref = next(it)
        if has_pre:
            hs_ref, gn_ref, dhn_ref = next(it), next(it), next(it)
        if has_post:
            m_ref, gp_ref = next(it), next(it)
        if has_pre:
            o_dhs, o_dgn = next(it), next(it)
        if has_post:
            o_dm, o_dgp = next(it), next(it)
        i = pl.program_id(0)
        live = (i * ROW_TILE + lax.broadcasted_iota(jnp.int32, (ROW_TILE, 1), 0)) >= pad
        cur = jnp.where(live, dhs_ref[...], 0.0)
        if has_pre:
            dx, dg = _rms_bwd(hs_ref[...], gn_ref[...], jnp.where(live, dhn_ref[...].astype(F32), 0.0))
            cur = cur + dx
            o_dhs[...] = cur

            @pl.when(i == 0)
            def _():
                o_dgn[...] = jnp.zeros_like(o_dgn)
            o_dgn[...] += dg
        if has_post:
            dm, dg = _rms_bwd(m_ref[...], gp_ref[...], cur)
            o_dm[...] = dm

            @pl.when(i == 0)
            def _():
                o_dgp[...] = jnp.zeros_like(o_dgp)
            o_dgp[...] += dg

    ins, in_specs, out_specs, out_shape = [dhs], [_row_spec(d)], [], []
    if has_pre:
        ins += list(pre)
        in_specs += [_row_spec(d), _vec_spec(d), _row_spec(d)]
        out_specs += [_row_spec(d), _vec_spec(d)]
        out_shape += [jax.ShapeDtypeStruct((r, d), F32), jax.ShapeDtypeStruct((1, d), F32)]
    if has_post:
        ins += list(post)
        in_specs += [_row_spec(d), _vec_spec(d)]
        out_specs += [_row_spec(d), _vec_spec(d)]
        out_shape += [jax.ShapeDtypeStruct((r, d), F32), jax.ShapeDtypeStruct((1, d), F32)]
    outs = list(_pallas(body, name=name, grid=(r // ROW_TILE,), in_specs=in_specs, out_specs=out_specs,
                        out_shape=out_shape, compiler_params=_cparams(("arbitrary",)))(*ins))
    dhs_new, dgn = (outs.pop(0), outs.pop(0)) if has_pre else (dhs, None)
    dm, dgp = (outs.pop(0), outs.pop(0)) if has_post else (None, None)
    return dhs_new, dm, dgn, dgp


def _softplus(z):
    return jnp.maximum(z, 0.0) + jnp.log(1.0 + jnp.exp(-jnp.abs(z)))


def _sb_consts(t):
    row = lax.broadcasted_iota(jnp.int32, (t, t), 0)
    col = lax.broadcasted_iota(jnp.int32, (t, t), 1)
    m_up = (col >= row).astype(BF16)
    m_low = (col <= row).astype(BF16)
    return m_up, m_low


def _emit_chains(chains, stages, skew):
    if skew:
        for step in range(len(chains) + len(stages) - 1):
            for si, stage in enumerate(stages):
                if 0 <= step - si < len(chains):
                    stage(chains[step - si])
    else:
        for stage in stages:
            for c in chains:
                stage(c)


def _sb_fwd(q, k, vt3, pad, name):
    r = q.shape[0]
    t = ATT_BLK
    nb = r // t
    nbp = -(-(nb + 1) // 8) * 8
    jmin = pad // t
    scale = SB_HEAD_DIM ** -0.5

    def body(q_ref, k_ref, vt_ref, o_ref, ss_ref, acc_ref, kn_ref):
        i = pl.program_id(1)

        @pl.when(i == 0)
        def _():
            def blk(b, m):
                kb = k_ref[pl.ds(pl.multiple_of(b * t, t), t), :].astype(F32)
                return jnp.maximum(m, jnp.max(jnp.sum(kb * kb, axis=1, keepdims=True), axis=0, keepdims=True))
            kn_ref[...] = jnp.broadcast_to(lax.fori_loop(0, nb, blk, jnp.zeros((1, 1), F32)), (8, 128))

        qf = q_ref[...].astype(F32)
        z_bound = scale * jnp.sqrt(jnp.max(jnp.sum(qf * qf, axis=1, keepdims=True)) * jnp.max(kn_ref[...]))

        def need(carry):
            return jnp.maximum(jnp.max(carry[0]), jnp.max(carry[1])) + z_bound >= SB_LOG_ZERO

        qt = qf.T
        sub = lax.broadcasted_iota(jnp.int32, (128, 1), 0)
        m_up, _ = _sb_consts(t)
        kpos0 = lax.broadcasted_iota(jnp.int32, (t, 1), 0)
        qpos = i * t + lax.broadcasted_iota(jnp.int32, (1, t), 1)
        n_mid = jnp.maximum(i - 1 - jmin, 0)
        n_edge = jnp.where(i > jmin, 1, 0)
        qths = [jnp.where((sub >= 64 * h) & (sub < 64 * (h + 1)), qt * scale, 0.0).astype(BF16) for h in range(2)]
        acc_ref[...] = jnp.zeros_like(acc_ref)

        def sweep(js, carry, masked):
            kbs = [k_ref[pl.ds(pl.multiple_of(j * t, t), t), :] for j in js]
            vts = [vt_ref[0, j] for j in js]
            accs = [acc_ref[0], acc_ref[1]]
            s = list(carry)
            chains = [(n, h) for n in range(len(js)) for h in range(2)]
            masked = [masked] * len(js) if isinstance(masked, bool) else masked
            valid = [(js[n] * t + kpos0 < qpos) & (js[n] * t + kpos0 >= pad) if masked[n] else None for n in range(len(js))]
            zt, inc, saves = {}, {}, []

            def st_scores(c):
                zt[c] = _dot(kbs[c[0]], qths[c[1]])

            def st_cumsum(c):
                lk = -_softplus(zt[c])
                if masked[c[0]]:
                    lk = jnp.where(valid[c[0]], lk, 0.0)
                inc[c] = _split_dot(m_up, lk)

            def st_weights(c):
                n, h = c
                saves.append((h, js[n], s[h]))
                w = jnp.exp(zt[c] + inc[c] + s[h])
                if masked[n]:
                    w = jnp.where(valid[n], w, 0.0)
                accs[h] = accs[h] + _dot(vts[n], w.astype(BF16))
                s[h] = s[h] + inc[c][0:1, :]

            _emit_chains(chains, [st_scores, st_cumsum, st_weights], SB_FWD_SKEW)
            for h, j, val in saves:
                ss_ref[h, 0, pl.ds(j, 1), :] = val
            acc_ref[0] = accs[0]
            acc_ref[1] = accs[1]
            return tuple(s)

        zero = jnp.zeros((1, t), F32)
        bpi = SB_BLOCKS_PER_TRIP
        j, carry = lax.cond(
            i - 1 > jmin,
            lambda: (i - 2, sweep([i, i - 1], (zero, zero), [True, False])),
            lambda: (i - 1, sweep([i], (zero, zero), True)))
        j, carry = lax.while_loop(
            lambda st: (st[0] - bpi >= jmin) & need(st[1]),
            lambda st: (st[0] - bpi, sweep([st[0] - b for b in range(bpi)], st[1], False)), (j, carry))
        j, carry = lax.while_loop(
            lambda st: (st[0] > jmin) & need(st[1]),
            lambda st: (st[0] - 1, sweep([st[0]], st[1], False)), (j, carry))
        j, carry = lax.while_loop(
            lambda st: (st[0] == jmin) & (i > jmin) & need(st[1]),
            lambda st: (st[0] - 1, sweep([st[0]], st[1], True)), (j, carry))
        first = jnp.full((1, t), j + 1, jnp.int32).astype(F32)
        ss_ref[0, 0, nbp - 1:nbp, :] = first
        ss_ref[1, 0, nbp - 1:nbp, :] = first
        acc = jnp.where(sub < 64, acc_ref[0], acc_ref[1])
        o_ref[...] = acc.T

    return _pallas(
        body, name=name, grid=(4, nb),
        in_specs=[pl.BlockSpec((t, 128), lambda hp, i: (i, hp)),
                  pl.BlockSpec((r, 128), lambda hp, i: (0, hp)),
                  pl.BlockSpec((1, nb, 128, t), lambda hp, i: (hp, 0, 0, 0))],
        out_specs=[pl.BlockSpec((t, 128), lambda hp, i: (i, hp)),
                   pl.BlockSpec((2, 1, nbp, t), lambda hp, i: (hp, i, 0, 0))],
        out_shape=[jax.ShapeDtypeStruct((r, SB_WIDTH), F32),
                   jax.ShapeDtypeStruct((8, nb, nbp, t), F32)],
        scratch_shapes=[pltpu.VMEM((2, 128, t), F32), pltpu.VMEM((8, 128), F32)],
        compiler_params=_cparams(("arbitrary", "arbitrary")),
    )(q, k, vt3)


def _sb_bwd(q, k, v, kt3, ssave, do, pad, name):
    r = q.shape[0]
    t = ATT_BLK
    nb = r // t
    nbp = ssave.shape[2]
    jmin = pad // t
    scale = SB_HEAD_DIM ** -0.5

    def body(q_ref, do_ref, k_ref, v_ref, kt_ref, ss_ref, dq_ref, dk_hbm, dv_hbm, dk_acc, dv_acc, dq_acc, sem):
        hp = pl.program_id(0)
        i = pl.program_id(1)

        @pl.when(i == 0)
        def _():
            dk_acc[...] = jnp.zeros_like(dk_acc)
            dv_acc[...] = jnp.zeros_like(dv_acc)

        qf = q_ref[...].astype(F32)
        dof = do_ref[...]
        qt = qf.T
        dot_ = dof.T
        sub = lax.broadcasted_iota(jnp.int32, (128, 1), 0)
        lane = lax.broadcasted_iota(jnp.int32, (1, 128), 1)
        m_up, m_low = _sb_consts(t)
        kpos0 = lax.broadcasted_iota(jnp.int32, (t, 1), 0)
        qpos = i * t + lax.broadcasted_iota(jnp.int32, (1, t), 1)
        first = jnp.clip(jnp.max(ss_ref[0, 0, nbp - 1:nbp, :]).astype(jnp.int32), jmin, i)
        mid0 = jnp.maximum(first, jmin + 1)
        pair = i - mid0 >= 1
        n_mid = jnp.maximum(i - mid0 - 1, 0)
        n_edge = jnp.where((i > jmin) & (first == jmin), 1, 0)
        in_t = [(sub >= 64 * h) & (sub < 64 * (h + 1)) for h in range(2)]
        in_l = [(lane >= 64 * h) & (lane < 64 * (h + 1)) for h in range(2)]
        qths = [jnp.where(in_t[h], qt * scale, 0.0).astype(BF16) for h in range(2)]
        doths = [jnp.where(in_t[h], dot_, 0.0).astype(BF16) for h in range(2)]
        qhs = [jnp.where(in_l[h], qf * scale, 0.0).astype(BF16) for h in range(2)]
        dohs = [jnp.where(in_l[h], dof, 0.0).astype(BF16) for h in range(2)]
        dq_acc[...] = jnp.zeros_like(dq_acc)

        def sweep(js, carry, masked):
            rows = [pl.ds(pl.multiple_of(j * t, t), t) for j in js]
            kbs = [k_ref[rw, :] for rw in rows]
            vbs = [v_ref[rw, :] for rw in rows]
            kts = [kt_ref[0, j] for j in js]
            sss = [[ss_ref[h, 0, pl.ds(j, 1), :] for h in range(2)] for j in js]
            dv_old = [dv_acc[rw, :] for rw in rows]
            dk_old = [dk_acc[rw, :] for rw in rows]
            dqs = [dq_acc[0], dq_acc[1]]
            ec = list(carry)
            chains = [(n, h) for n in range(len(js)) for h in range(2)]
            masked = [masked] * len(js) if isinstance(masked, bool) else masked
            valid = [(js[n] * t + kpos0 < qpos) & (js[n] * t + kpos0 >= pad) if masked[n] else None for n in range(len(js))]
            zt, dvt, sp, inc, e, big_e = {}, {}, {}, {}, {}, {}

            def st_scores(c):
                zt[c] = _dot(kbs[c[0]], qths[c[1]])
                dvt[c] = _dot(vbs[c[0]], doths[c[1]])

            def st_cumsum(c):
                sp[c] = _softplus(zt[c])
                lk = -sp[c]
                if masked[c[0]]:
                    lk = jnp.where(valid[c[0]], lk, 0.0)
                inc[c] = _split_dot(m_up, lk)

            def st_weights(c):
                n, h = c
                w = jnp.exp(zt[c] + inc[c] + sss[n][h])
                if masked[n]:
                    w = jnp.where(valid[n], w, 0.0)
                dv_old[n] = dv_old[n] + _dot(w.astype(BF16), dohs[h])
                e[c] = w * dvt[c]
                pinc = _split_dot(m_low, e[c])
                big_e[c] = pinc - e[c] + ec[h]
                ec[h] = ec[h] + pinc[t - 1:t, :]

            def st_dscores(c):
                n, h = c
                dz = e[c] - jnp.exp(zt[c] - sp[c]) * (e[c] + big_e[c])
                if masked[n]:
                    dz = jnp.where(valid[n], dz, 0.0)
                dzb = dz.astype(BF16)
                dqs[h] = dqs[h] + _dot(kts[n], dzb)
                dk_old[n] = dk_old[n] + _dot(dzb, qhs[h])

            _emit_chains(chains, [st_scores, st_cumsum, st_weights, st_dscores], SB_BWD_SKEW)
            for n, rw in enumerate(rows):
                dv_acc[rw, :] = dv_old[n]
                dk_acc[rw, :] = dk_old[n]
            dq_acc[0] = dqs[0]
            dq_acc[1] = dqs[1]
            return tuple(ec)

        zero = jnp.zeros((1, t), F32)
        bpi = SB_BLOCKS_PER_TRIP
        carry = lax.fori_loop(0, n_edge, lambda it, c: sweep([jmin + it * 0], c, True), (zero, zero))
        carry = lax.fori_loop(0, n_mid // bpi, lambda it, c: sweep([mid0 + bpi * it + b for b in range(bpi)], c, False), carry)
        n_rem = n_mid % bpi
        carry = lax.fori_loop(0, n_rem, lambda it, c: sweep([i - 1 - n_rem + it], c, False), carry)
        lax.cond(pair, lambda: sweep([i - 1, i], carry, [False, True]), lambda: sweep([i], carry, True))
        dq_ref[...] = (jnp.where(sub < 64, dq_acc[0], dq_acc[1]) * scale).T

        @pl.when(i == nb - 1)
        def _():
            c1 = pltpu.make_async_copy(dk_acc, dk_hbm.at[hp], sem.at[0])
            c2 = pltpu.make_async_copy(dv_acc, dv_hbm.at[hp], sem.at[1])
            c1.start()
            c2.start()
            c1.wait()
            c2.wait()

    return _pallas(
        body, name=name, grid=(4, nb),
        in_specs=[pl.BlockSpec((t, 128), lambda hp, i: (i, hp)),
                  pl.BlockSpec((t, 128), lambda hp, i: (i, hp)),
                  pl.BlockSpec((r, 128), lambda hp, i: (0, hp)),
                  pl.BlockSpec((r, 128), lambda hp, i: (0, hp)),
                  pl.BlockSpec((1, nb, 128, t), lambda hp, i: (hp, 0, 0, 0)),
                  pl.BlockSpec((2, 1, nbp, t), lambda hp, i: (hp, i, 0, 0))],
        out_specs=[pl.BlockSpec((t, 128), lambda hp, i: (i, hp)),
                   pl.BlockSpec(memory_space=pl.ANY), pl.BlockSpec(memory_space=pl.ANY)],
        out_shape=[jax.ShapeDtypeStruct((r, SB_WIDTH), F32),
                   jax.ShapeDtypeStruct((4, r, 128), F32), jax.ShapeDtypeStruct((4, r, 128), F32)],
        scratch_shapes=[pltpu.VMEM((r, 128), F32), pltpu.VMEM((r, 128), F32), pltpu.VMEM((2, 128, t), F32),
                        pltpu.SemaphoreType.DMA((2,))],
        compiler_params=_cparams(("arbitrary", "arbitrary")),
    )(q, do, k, v, kt3, ssave)


def _s5_disc(lam_re, lam_im, logdt, btr, bti):
    lr = jnp.minimum(lam_re, -1e-4)
    li = lam_im
    dt = jnp.exp(logdt)
    mag = jnp.exp(lr * dt)
    ang = li * dt
    a_re, a_im = mag * jnp.cos(ang), mag * jnp.sin(ang)
    den = lr * lr + li * li
    nr, ni = a_re - 1.0, a_im
    c_re = (nr * lr + ni * li) / den
    c_im = (ni * lr - nr * li) / den
    return a_re, a_im, c_re * btr - c_im * bti, c_re * bti + c_im * btr


def _s5_prep(lam_re, lam_im, logdt, btr, bti, name):
    ns = lam_re.shape[1]

    def body(lr_ref, li_ref, dt_ref, br_ref, bi_ref, ar_ref, ai_ref, bbr_ref, bbi_ref):
        ar, ai, bbr, bbi = _s5_disc(lr_ref[...], li_ref[...], dt_ref[...], br_ref[...], bi_ref[...])
        ar_ref[...] = ar
        ai_ref[...] = ai
        bbr_ref[...] = bbr
        bbi_ref[...] = bbi

    return _pallas(body, name=name,
                   out_shape=[jax.ShapeDtypeStruct((1, ns), F32)] * 2 + [jax.ShapeDtypeStruct((S5_GROUP, ns), F32)] * 2,
                   )(lam_re, lam_im, logdt, btr, bti)


def _s5_prep_bwd(lam_re, lam_im, logdt, btr, bti, dar, dai, dbbr, dbbi, name):
    ns = lam_re.shape[1]

    def body(lr_ref, li_ref, dt_ref, br_ref, bi_ref, dar_ref, dai_ref, dbr_ref, dbi_ref, o_lr, o_li, o_dt, o_br, o_bi):
        _, vjp = jax.vjp(_s5_disc, lr_ref[...], li_ref[...], dt_ref[...], br_ref[...], bi_ref[...])
        g = vjp((dar_ref[...], dai_ref[...], dbr_ref[...], dbi_ref[...]))
        o_lr[...] = g[0]
        o_li[...] = g[1]
        row = lax.broadcasted_iota(jnp.int32, (ns, ns), 0) // S5_STATE
        col = lax.broadcasted_iota(jnp.int32, (ns, ns), 1) // S5_STATE
        same = (row == col).astype(F32)
        o_dt[...] = _dot_hi(jnp.broadcast_to(g[2], (8, ns)), same)[0:1]
        o_br[...] = g[3]
        o_bi[...] = g[4]

    return _pallas(body, name=name,
                   out_shape=[jax.ShapeDtypeStruct((1, ns), F32)] * 3 + [jax.ShapeDtypeStruct((S5_GROUP, ns), F32)] * 2,
                   compiler_params=pltpu.CompilerParams(vmem_limit_bytes=VMEM_LIMIT),
                   )(lam_re, lam_im, logdt, btr, bti, dar, dai, dbbr, dbbi)


def _s5_scan(br, bi, ar, ai, t, reverse=False):
    row8 = lax.broadcasted_iota(jnp.int32, (t, 1), 0) % 8
    pr, pi_ = ar, ai
    for k in (1, 2, 4):
        if reverse:
            sr, si, ok = pltpu.roll(br, t - k, 0), pltpu.roll(bi, t - k, 0), row8 < 8 - k
        else:
            sr, si, ok = pltpu.roll(br, k, 0), pltpu.roll(bi, k, 0), row8 >= k
        sr = jnp.where(ok, sr, 0.0)
        si = jnp.where(ok, si, 0.0)
        br, bi = br + pr * sr - pi_ * si, bi + pr * si + pi_ * sr
        pr, pi_ = pr * pr - pi_ * pi_, 2.0 * pr * pi_
    pw_r, pw_i = [ar], [ai]
    for _ in range(7):
        pw_r.append(pw_r[-1] * ar - pw_i[-1] * ai)
        pw_i.append(pw_r[-2] * ai + pw_i[-1] * ar)
    if reverse:
        pw_r.reverse()
        pw_i.reverse()
    p8r, p8i = jnp.concatenate(pw_r, axis=0), jnp.concatenate(pw_i, axis=0)
    ng = t // 8
    out_r, out_i = [None] * ng, [None] * ng
    order = range(ng - 1, -1, -1) if reverse else range(ng)
    prev = None
    for g in order:
        gr, gi = br[8 * g:8 * g + 8], bi[8 * g:8 * g + 8]
        if prev is not None:
            edge = 0 if reverse else 7
            cr, ci = out_r[prev][edge:edge + 1], out_i[prev][edge:edge + 1]
            gr, gi = gr + p8r * cr - p8i * ci, gi + p8r * ci + p8i * cr
        out_r[g], out_i[g] = gr, gi
        prev = g
    return jnp.concatenate(out_r, axis=0), jnp.concatenate(out_i, axis=0)


def _s5_power_table(ar, ai, t, reverse=False):
    row = lax.broadcasted_iota(jnp.int32, (t, 1), 0)
    hot = row == (t - 1 if reverse else 0)
    return _s5_scan(jnp.where(hot, ar, 0.0), jnp.where(hot, ai, 0.0), ar, ai, t, reverse)


_GELU_C = math.sqrt(2.0 / math.pi)


def _gelu(y):
    th = jnp.tanh(_GELU_C * (y + 0.044715 * y * y * y))
    return 0.5 * y * (1.0 + th), th


def _sigmoid(x):
    return 1.0 / (1.0 + jnp.exp(-x))


def _s5_fwd(u, wb, a, wc, dskip, wglu, bglu, gnorm, name):
    r = u.shape[0]
    t = S5_TILE
    nt = r // t
    ns = wb.shape[2]
    w = S5_WIDTH

    def body(u_ref, wb_ref, a_ref, wc_ref, d_ref, wg_ref, bg_ref, gn_ref, y_ref, on_ref, xs_ref, pw_ref, carry_ref):
        i = pl.program_id(0)
        ar, ai = a_ref[0], a_ref[1]

        @pl.when(i == 0)
        def _():
            pr, pi_ = _s5_power_table(ar, ai, t)
            pw_ref[0] = pr
            pw_ref[1] = pi_
            carry_ref[...] = jnp.zeros_like(carry_ref)

        u_ = u_ref[...]
        ub = u_.astype(BF16)
        xr, xi = _s5_scan(_dot(ub, wb_ref[0]), _dot(ub, wb_ref[1]), ar, ai, t)
        cr, ci = carry_ref[0], carry_ref[1]
        xs_ref[0, 0:1, :] = cr
        xs_ref[0, 1:2, :] = ci
        pr, pi_ = pw_ref[0], pw_ref[1]
        xr = xr + pr * cr - pi_ * ci
        xi = xi + pr * ci + pi_ * cr
        carry_ref[0] = xr[t - 1:t, :]
        carry_ref[1] = xi[t - 1:t, :]
        y = _dot(xr.astype(BF16), wc_ref[0]) - _dot(xi.astype(BF16), wc_ref[1]) + d_ref[...] * u_
        h, _ = _gelu(y)
        gate = _sigmoid(_dot(h.astype(BF16), wg_ref[...]) + bg_ref[...])
        y_ref[...] = y
        on_ref[...] = _rms(h * gate, gn_ref[...]).astype(BF16)

    full = lambda shape: pl.BlockSpec(shape, lambda i: (0,) * len(shape))
    return _pallas(
        body, name=name, grid=(nt,),
        in_specs=[_row_spec(w, t), full((2, w, ns)), full((2, 1, ns)), full((2, ns, w)), full((1, w)),
                  full((w, w)), full((1, w)), full((1, w))],
        out_specs=[_row_spec(w, t), _row_spec(w, t), pl.BlockSpec((1, 2, ns), lambda i: (i, 0, 0))],
        out_shape=[jax.ShapeDtypeStruct((r, w), F32), jax.ShapeDtypeStruct((r, w), BF16),
                   jax.ShapeDtypeStruct((nt, 2, ns), F32)],
        scratch_shapes=[pltpu.VMEM((2, t, ns), F32), pltpu.VMEM((2, 1, ns), F32)],
        compiler_params=_cparams(("arbitrary",)),
    )(u, wb, a, wc, dskip, wglu, bglu, gnorm)


def _s5_bwd(u, y, don, xstart, wb, a, wc, dskip, wglu, bglu, gnorm, name):
    r = u.shape[0]
    t = S5_TILE
    nt = r // t
    ns = wb.shape[2]
    w = S5_WIDTH
    nt_dims = ((1,), (1,))
    tn_dims = ((0,), (0,))

    def body(u_ref, y_ref, don_ref, xs_ref, wb_hbm, a_ref, wc_hbm, d_ref, wg_ref, bg_ref, gn_ref,
             du_ref, da_ref, dd_ref, dbg_ref, dgn_ref, dwb_hbm, dwc_hbm, dwg_hbm,
             wb_ref, wc_ref, pw_ref, pwr_ref, lam_ref, acc_wb, acc_wc, acc_wg, sem):
        i = pl.program_id(0)
        ar, ai = a_ref[0], a_ref[1]

        @pl.when(i == 0)
        def _():
            c1 = pltpu.make_async_copy(wb_hbm, wb_ref, sem.at[0])
            c2 = pltpu.make_async_copy(wc_hbm, wc_ref, sem.at[1])
            c1.start()
            c2.start()
            pr, pi_ = _s5_power_table(ar, ai, t)
            pw_ref[0] = pr
            pw_ref[1] = pi_
            pr, pi_ = _s5_power_table(ar, -ai, t, reverse=True)
            pwr_ref[0] = pr
            pwr_ref[1] = pi_
            lam_ref[...] = jnp.zeros_like(lam_ref)
            acc_wb[...] = jnp.zeros_like(acc_wb)
            acc_wc[...] = jnp.zeros_like(acc_wc)
            acc_wg[...] = jnp.zeros_like(acc_wg)
            da_ref[...] = jnp.zeros_like(da_ref)
            dd_ref[...] = jnp.zeros_like(dd_ref)
            dbg_ref[...] = jnp.zeros_like(dbg_ref)
            dgn_ref[...] = jnp.zeros_like(dgn_ref)
            c1.wait()
            c2.wait()

        u_ = u_ref[...]
        y_ = y_ref[...]
        ub = u_.astype(BF16)
        h, th = _gelu(y_)
        hb = h.astype(BF16)
        wg = wg_ref[...]
        gate = _sigmoid(_dot(hb, wg) + bg_ref[...])
        d_out, dgn = _rms_bwd(h * gate, gn_ref[...], don_ref[...])
        dgn_ref[...] += dgn
        dhw = d_out * h * gate * (1.0 - gate)
        dhwb = dhw.astype(BF16)
        dh = d_out * gate + _dot(dhwb, wg, nt_dims)
        acc_wg[...] += _dot(hb, dhwb, tn_dims)
        dbg_ref[...] += jnp.sum(dhw, axis=0, keepdims=True)
        dgelu = 0.5 * (1.0 + th) + 0.5 * y_ * (1.0 - th * th) * _GELU_C * (1.0 + 3.0 * 0.044715 * y_ * y_)
        dy = dh * dgelu
        dd_ref[...] += jnp.sum(dy * u_, axis=0, keepdims=True)
        dyb = dy.astype(BF16)
        xr, xi = _s5_scan(_dot(ub, wb_ref[0]), _dot(ub, wb_ref[1]), ar, ai, t)
        cr, ci = xs_ref[0, 0:1, :], xs_ref[0, 1:2, :]
        pr, pi_ = pw_ref[0], pw_ref[1]
        xr = xr + pr * cr - pi_ * ci
        xi = xi + pr * ci + pi_ * cr
        acc_wc[0] += _dot(xr.astype(BF16), dyb, tn_dims)
        acc_wc[1] -= _dot(xi.astype(BF16), dyb, tn_dims)
        lr, li = _s5_scan(_dot(dyb, wc_ref[0], nt_dims), -_dot(dyb, wc_ref[1], nt_dims), ar, -ai, t, reverse=True)
        cr2, ci2 = lam_ref[0], lam_ref[1]
        pr, pi_ = pwr_ref[0], pwr_ref[1]
        lr = lr + pr * cr2 - pi_ * ci2
        li = li + pr * ci2 + pi_ * cr2
        lam_ref[0] = lr[0:1, :]
        lam_ref[1] = li[0:1, :]
        row = lax.broadcasted_iota(jnp.int32, (t, 1), 0)
        xpr = jnp.where(row == 0, cr, pltpu.roll(xr, 1, 0))
        xpi = jnp.where(row == 0, ci, pltpu.roll(xi, 1, 0))
        da_ref[0] += jnp.sum(lr * xpr + li * xpi, axis=0, keepdims=True)
        da_ref[1] += jnp.sum(li * xpr - lr * xpi, axis=0, keepdims=True)
        lrb, lib = lr.astype(BF16), li.astype(BF16)
        acc_wb[0] += _dot(ub, lrb, tn_dims)
        acc_wb[1] += _dot(ub, lib, tn_dims)
        du_ref[...] = d_ref[...] * dy + _dot(lrb, wb_ref[0], nt_dims) + _dot(lib, wb_ref[1], nt_dims)

        @pl.when(i == nt - 1)
        def _():
            cps = [pltpu.make_async_copy(acc_wb, dwb_hbm, sem.at[0]), pltpu.make_async_copy(acc_wc, dwc_hbm, sem.at[1]),
                   pltpu.make_async_copy(acc_wg, dwg_hbm, sem.at[2])]
            for c in cps:
                c.start()
            for c in cps:
                c.wait()

    rev = lambda i: (nt - 1 - i, 0)
    full = lambda shape: pl.BlockSpec(shape, lambda i: (0,) * len(shape))
    hbm = pl.BlockSpec(memory_space=pl.ANY)
    return _pallas(
        body, name=name, grid=(nt,),
        in_specs=[pl.BlockSpec((t, w), rev), pl.BlockSpec((t, w), rev), pl.BlockSpec((t, w), rev),
                  pl.BlockSpec((1, 2, ns), lambda i: (nt - 1 - i, 0, 0)), hbm, full((2, 1, ns)), hbm, full((1, w)),
                  full((w, w)), full((1, w)), full((1, w))],
        out_specs=[pl.BlockSpec((t, w), rev), full((2, 1, ns)), full((1, w)), full((1, w)), full((1, w)), hbm, hbm, hbm],
        out_shape=[jax.ShapeDtypeStruct((r, w), F32), jax.ShapeDtypeStruct((2, 1, ns), F32)]
        + [jax.ShapeDtypeStruct((1, w), F32)] * 3
        + [jax.ShapeDtypeStruct((2, w, ns), F32), jax.ShapeDtypeStruct((2, ns, w), F32), jax.ShapeDtypeStruct((w, w), F32)],
        scratch_shapes=[pltpu.VMEM((2, w, ns), BF16), pltpu.VMEM((2, ns, w), BF16),
                        pltpu.VMEM((2, t, ns), F32), pltpu.VMEM((2, t, ns), F32), pltpu.VMEM((2, 1, ns), F32),
                        pltpu.VMEM((2, w, ns), F32), pltpu.VMEM((2, ns, w), F32), pltpu.VMEM((w, w), F32),
                        pltpu.SemaphoreType.DMA((3,))],
        compiler_params=_cparams(("arbitrary",)),
    )(u, y, don, xstart, wb, a, wc, dskip, wglu, bglu, gnorm)


def _s5_expand(lam_re, lam_im, log_dt, b_re, b_im, c_re, c_im):
    g, n, p = S5_GROUPS, S5_STATE, S5_GROUP
    ns = g * n
    rows = lambda x: x.reshape(1, ns)
    logdt = jnp.repeat(log_dt.reshape(g), n).reshape(1, ns)
    btr = b_re.reshape(ns, p).T
    bti = b_im.reshape(ns, p).T
    ctr = c_re.transpose(0, 2, 1).reshape(ns, p)
    cti = c_im.transpose(0, 2, 1).reshape(ns, p)
    mask = (jnp.arange(g * p)[:, None] // p) == (jnp.arange(ns)[None, :] // n)
    return rows(lam_re), rows(lam_im), logdt, btr, bti, ctr, cti, mask


def _s5_block_diag_b(bb, mask):
    return jnp.where(mask, jnp.tile(bb, (S5_GROUPS, 1)), 0.0)


def _s5_block_diag_c(ct, mask):
    return jnp.where(mask.T, jnp.tile(ct, (1, S5_GROUPS)), 0.0)


def _s5_diag_of_b(dwb, mask):
    return jnp.where(mask, dwb, 0.0).reshape(S5_GROUPS, S5_GROUP, -1).sum(0)


def _s5_diag_of_c(dwc, mask):
    ns = dwc.shape[0]
    return jnp.where(mask.T, dwc, 0.0).reshape(ns, S5_GROUPS, S5_GROUP).sum(1)


DN_PRE_TILE = 256
_DN_QKV = 3 * DN_WIDTH


def _halo_specs(width, tile, nt, prev):
    per = tile // 8
    if prev:
        return pl.BlockSpec((8, width), lambda i: (jnp.maximum(i * per - 1, 0), 0))
    return pl.BlockSpec((8, width), lambda i: (jnp.minimum((i + 1) * per, nt * per - 1), 0))


def _shift_down(x, halo, s, t):
    xx = jnp.concatenate([halo, x], axis=0)
    return pltpu.roll(xx, s, 0)[8:]


def _shift_up(x, halo, s, t):
    xx = jnp.concatenate([x, halo], axis=0)
    return pltpu.roll(xx, t + 8 - s, 0)[:t]


def _silu(x):
    s = _sigmoid(x)
    return x * s, s


def _dn_gates(ab, alog, dtb, live):
    lane = lax.broadcasted_iota(jnp.int32, (1, 128), 1)
    g = -jnp.exp(alog) * _softplus(ab + dtb)
    beta = _sigmoid(ab)
    return jnp.where(live & (lane < DN_HEADS), g, jnp.where(live & (lane < 2 * DN_HEADS), beta, 0.0))


def _dn_pre_fwd(proj, ab, conv_w, alog, dtb, pad, name):
    r = proj.shape[0]
    t = DN_PRE_TILE
    nt = r // t
    scale = DN_HEAD_DIM ** -0.5

    def body(x_ref, halo_ref, ab_ref, w_ref, al_ref, dt_ref, co_ref, q_ref, k_ref, v_ref, gb_ref):
        i = pl.program_id(0)
        x = x_ref[...]
        halo = jnp.where(i > 0, halo_ref[...], 0.0)
        w = w_ref[...]
        co = w[3:4] * x
        for tap in range(DN_CONV - 1):
            co = co + w[tap:tap + 1] * _shift_down(x, halo, DN_CONV - 1 - tap, t)
        co_ref[...] = co
        act, _ = _silu(co)
        for hd in range(DN_HEADS):
            sl = slice(hd * 128, (hd + 1) * 128)
            for base, o_ref, sc in ((0, q_ref, scale), (DN_WIDTH, k_ref, 1.0)):
                xh = act[:, base + hd * 128: base + (hd + 1) * 128]
                o_ref[:, sl] = xh * (lax.rsqrt(jnp.sum(xh * xh, axis=-1, keepdims=True) + EPS) * sc)
        v_ref[...] = act[:, 2 * DN_WIDTH:]
        rows = i * t + lax.broadcasted_iota(jnp.int32, (t, 1), 0)
        gb_ref[...] = _dn_gates(ab_ref[...], al_ref[...], dt_ref[...], rows >= pad)

    return _pallas(
        body, name=name, grid=(nt,),
        in_specs=[pl.BlockSpec((t, _DN_QKV), lambda i: (i, 0)), _halo_specs(_DN_QKV, t, nt, True), _row_spec(128, t),
                  pl.BlockSpec((DN_CONV, _DN_QKV), lambda i: (0, 0)), _vec_spec(128), _vec_spec(128)],
        out_specs=[_row_spec(_DN_QKV, t), _row_spec(DN_WIDTH, t), _row_spec(DN_WIDTH, t), _row_spec(DN_WIDTH, t), _row_spec(128, t)],
        out_shape=[jax.ShapeDtypeStruct((r, _DN_QKV), F32)] + [jax.ShapeDtypeStruct((r, DN_WIDTH), F32)] * 3
        + [jax.ShapeDtypeStruct((r, 128), F32)],
        compiler_params=_cparams(("parallel",)),
    )(proj, proj, ab, conv_w, alog, dtb)


def _dn_pre_bwd(co, dq, dk, dv, dgb, ab, alog, dtb, pad, name):
    r = co.shape[0]
    t = DN_PRE_TILE
    nt = r // t
    scale = DN_HEAD_DIM ** -0.5

    def body(co_ref, dq_ref, dk_ref, dv_ref, dgb_ref, ab_ref, al_ref, dt_ref, dco_ref, dab_ref, dal_ref, ddt_ref):
        i = pl.program_id(0)

        @pl.when(i == 0)
        def _():
            dal_ref[...] = jnp.zeros_like(dal_ref)
            ddt_ref[...] = jnp.zeros_like(ddt_ref)

        co_ = co_ref[...]
        act, sg = _silu(co_)
        dsilu = sg * (1.0 + co_ * (1.0 - sg))
        for hd in range(DN_HEADS):
            sl = slice(hd * 128, (hd + 1) * 128)
            for base, d_ref, sc in ((0, dq_ref, scale), (DN_WIDTH, dk_ref, 1.0)):
                cs = slice(base + hd * 128, base + (hd + 1) * 128)
                xh = act[:, cs]
                rn = lax.rsqrt(jnp.sum(xh * xh, axis=-1, keepdims=True) + EPS)
                xhat = xh * rn
                dy = d_ref[:, sl]
                dx = (sc * rn) * (dy - xhat * jnp.sum(dy * xhat, axis=-1, keepdims=True))
                dco_ref[:, cs] = dx * dsilu[:, cs]
        dco_ref[:, 2 * DN_WIDTH:] = dv_ref[...] * dsilu[:, 2 * DN_WIDTH:]
        rows = i * t + lax.broadcasted_iota(jnp.int32, (t, 1), 0)
        live = rows >= pad
        lane = lax.broadcasted_iota(jnp.int32, (1, 128), 1)
        ab_ = ab_ref[...]
        dgb_ = dgb_ref[...]
        is_g = live & (lane < DN_HEADS)
        is_b = live & (lane >= DN_HEADS) & (lane < 2 * DN_HEADS)
        arg = ab_ + dt_ref[...]
        ea = jnp.exp(al_ref[...])
        da = jnp.where(is_g, -dgb_ * ea * _sigmoid(arg), 0.0)
        beta = _sigmoid(ab_)
        dab_ref[...] = da + jnp.where(is_b, dgb_ * beta * (1.0 - beta), 0.0)
        ddt_ref[...] += jnp.sum(da, axis=0, keepdims=True)
        dal_ref[...] += jnp.sum(jnp.where(is_g, -dgb_ * ea * _softplus(arg), 0.0), axis=0, keepdims=True)

    return _pallas(
        body, name=name, grid=(nt,),
        in_specs=[_row_spec(_DN_QKV, t), _row_spec(DN_WIDTH, t), _row_spec(DN_WIDTH, t), _row_spec(DN_WIDTH, t),
                  _row_spec(128, t), _row_spec(128, t), _vec_spec(128), _vec_spec(128)],
        out_specs=[_row_spec(_DN_QKV, t), _row_spec(128, t), _vec_spec(128), _vec_spec(128)],
        out_shape=[jax.ShapeDtypeStruct((r, _DN_QKV), F32), jax.ShapeDtypeStruct((r, 128), F32),
                   jax.ShapeDtypeStruct((1, 128), F32), jax.ShapeDtypeStruct((1, 128), F32)],
        compiler_params=_cparams(("arbitrary",)),
    )(co, dq, dk, dv, dgb, ab, alog, dtb)


def _dn_conv_bwd(dco, proj, conv_w, name):
    r = dco.shape[0]
    t = DN_PRE_TILE
    nt = r // t

    def body(d_ref, dh_ref, x_ref, xh_ref, w_ref, dx_ref, dw_ref):
        i = pl.program_id(0)

        @pl.when(i == 0)
        def _():
            dw_ref[...] = jnp.zeros_like(dw_ref)

        d = d_ref[...]
        dhalo = jnp.where(i < nt - 1, dh_ref[...], 0.0)
        x = x_ref[...]
        xhalo = jnp.where(i > 0, xh_ref[...], 0.0)
        w = w_ref[...]
        dx = w[3:4] * d
        dws = [None] * DN_CONV
        dws[3] = jnp.sum(d * x, axis=0, keepdims=True)
        for tap in range(DN_CONV - 1):
            s = DN_CONV - 1 - tap
            dx = dx + w[tap:tap + 1] * _shift_up(d, dhalo, s, t)
            dws[tap] = jnp.sum(d * _shift_down(x, xhalo, s, t), axis=0, keepdims=True)
        dx_ref[...] = dx
        dw_ref[...] += jnp.concatenate(dws + [jnp.zeros((8 - DN_CONV, _DN_QKV), F32)], axis=0)

    return _pallas(
        body, name=name, grid=(nt,),
        in_specs=[_row_spec(_DN_QKV, t), _halo_specs(_DN_QKV, t, nt, False),
                  pl.BlockSpec((t, _DN_QKV), lambda i: (i, 0)), _halo_specs(_DN_QKV, t, nt, True),
                  pl.BlockSpec((DN_CONV, _DN_QKV), lambda i: (0, 0))],
        out_specs=[_row_spec(_DN_QKV, t), pl.BlockSpec((8, _DN_QKV), lambda i: (0, 0))],
        out_shape=[jax.ShapeDtypeStruct((r, _DN_QKV), F32), jax.ShapeDtypeStruct((8, _DN_QKV), F32)],
        compiler_params=_cparams(("arbitrary",)),
    )(dco, dco, proj, proj, conv_w)


def _split3(x):
    hi = x.astype(BF16)
    return hi, (x - hi.astype(F32)).astype(BF16)


def _dot3s(a, b, dims=((1,), (0,))):
    return _dot(a[0], b[0], dims) + (_dot(a[0], b[1], dims) + _dot(a[1], b[0], dims))


def _dot3(a, b, dims=((1,), (0,))):
    return _dot3s(_split3(a), _split3(b), dims)


def _dn_inverse_many(n_mats):
    c = n_mats[0].shape[0]
    row = lax.broadcasted_iota(jnp.int32, (c, c), 0)
    col = lax.broadcasted_iota(jnp.int32, (c, c), 1)
    eye = (row == col).astype(F32)
    same = row // DN_SUB == col // DN_SUB
    nds = [jnp.where(same, n, 0.0) for n in n_mats]
    nos = [n - nd for n, nd in zip(n_mats, nds)]

    def geometric(bs, order):
        xs = [eye + b for b in bs]
        sp = [_split3(b) for b in bs]
        k = 2
        while k < order:
            sp = [_split3(_dot3s(s_, s_)) for s_ in sp]
            xs = [x + _dot3s(_split3(x), s_) for x, s_ in zip(xs, sp)]
            k *= 2
        return xs

    tds = [_split3(td) for td in geometric([-nd for nd in nds], DN_SUB)]
    ms = [_dot3s(td, _split3(no)) for td, no in zip(tds, nos)]
    xs = geometric([-m for m in ms], c // DN_SUB)
    return [_dot3s(_split3(x), td) for x, td in zip(xs, tds)]


def _dn_chunk_shared(gb_ref, gbt_ref):
    c = DN_CHUNK
    row = lax.broadcasted_iota(jnp.int32, (c, c), 0)
    col = lax.broadcasted_iota(jnp.int32, (c, c), 1)
    gbv = gb_ref[...]
    gam_all = _split_dot((row >= col).astype(BF16), gbv)
    hi, lo = _split3(gbt_ref[...])
    tri_t = (row <= col).astype(BF16)
    return dict(row=row, col=col, gbv=gbv, gam_all=gam_all, gam_rows=_dot(hi, tri_t) + _dot(lo, tri_t),
                lane=lax.broadcasted_iota(jnp.int32, (1, 128), 1))


def _dn_chunk_common(q, k, v, sh, h):
    c = DN_CHUNK
    row, col, lane = sh["row"], sh["col"], sh["lane"]
    gam = jnp.sum(jnp.where(lane == h, sh["gam_all"], 0.0), axis=1, keepdims=True)
    beta = jnp.sum(jnp.where(lane == h + DN_HEADS, sh["gbv"], 0.0), axis=1, keepdims=True)
    gam_row = sh["gam_rows"][h:h + 1]
    dec = jnp.where(row >= col, jnp.exp(jnp.minimum(gam - gam_row, 0.0)), 0.0)
    kb, qb = k.astype(BF16), q.astype(BF16)
    nt_dims = ((1,), (1,))
    kk = _dot(kb, kb, nt_dims)
    qk = _dot(qb, kb, nt_dims)
    eg = jnp.exp(gam)
    gam_l = gam[c - 1:c, :]
    return dict(q=q, k=k, v=v, qb=qb, kb=kb, gam=gam, beta=beta, dec=dec, kk=kk, qk=qk, eg=eg, gam_l=gam_l,
                row=row, col=col, lane=lane, att=qk * dec, qg=q * eg, kt=k * jnp.exp(gam_l - gam),
                rhs=jnp.concatenate([v * beta, k * (beta * eg)], axis=1))


def _dn_fwd(q, k, v, gb, gbt, name):
    r = q.shape[0]
    c = DN_CHUNK
    nc = r // c
    dh = DN_HEAD_DIM
    tn_dims = ((0,), (0,))

    def body(q_ref, k_ref, v_ref, gb_ref, gbt_ref, o_ref, ss_ref, ts_ref, s_ref):
        @pl.when(pl.program_id(0) == 0)
        def _():
            s_ref[...] = jnp.zeros_like(s_ref)

        heads = list(range(DN_HEADS))
        sl = [slice(h * dh, (h + 1) * dh) for h in heads]
        sh = _dn_chunk_shared(gb_ref, gbt_ref)
        zs = [_dn_chunk_common(q_ref[:, sl[h]], k_ref[:, sl[h]], v_ref[:, sl[h]], sh, h) for h in heads]
        t_invs = _dn_inverse_many([jnp.where(sh["row"] > sh["col"], z["beta"] * z["kk"] * z["dec"], 0.0) for z in zs])
        sols = [_dot3(t_inv, z["rhs"]) for t_inv, z in zip(t_invs, zs)]
        ss = [s_ref[h] for h in heads]
        sbs = [s.astype(BF16) for s in ss]
        vnbs = [(sol[:, :dh] - _dot(sol[:, dh:].astype(BF16), sb)).astype(BF16) for sol, sb in zip(sols, sbs)]
        for h in heads:
            o_ref[:, sl[h]] = _dot(zs[h]["qg"].astype(BF16), sbs[h]) + _dot(zs[h]["att"].astype(BF16), vnbs[h])
        for h in heads:
            ss_ref[0, h] = ss[h]
            ts_ref[0, h] = t_invs[h]
            s_ref[h] = ss[h] * jnp.exp(zs[h]["gam_l"]) + _dot(zs[h]["kt"].astype(BF16), vnbs[h], tn_dims)

    blk = pl.BlockSpec((c, DN_WIDTH), lambda ci: (ci, 0))
    sav = pl.BlockSpec((1, DN_HEADS, dh, dh), lambda ci: (ci, 0, 0, 0))
    return _pallas(
        body, name=name, grid=(nc,),
        in_specs=[blk, blk, blk, pl.BlockSpec((c, 128), lambda ci: (ci, 0)), pl.BlockSpec((16, c), lambda ci: (0, ci))],
        out_specs=[blk, sav, sav],
        out_shape=[jax.ShapeDtypeStruct((r, DN_WIDTH), F32), jax.ShapeDtypeStruct((nc, DN_HEADS, dh, dh), F32),
                   jax.ShapeDtypeStruct((nc, DN_HEADS, dh, dh), F32)],
        scratch_shapes=[pltpu.VMEM((DN_HEADS, dh, dh), F32)],
        compiler_params=_cparams(("arbitrary",)),
    )(q, k, v, gb, gbt)


def _dn_bwd(q, k, v, gb, gbt, ssave, tsave, do, name):
    r = q.shape[0]
    c = DN_CHUNK
    nc = r // c
    dh = DN_HEAD_DIM
    nt_dims = ((1,), (1,))
    tn_dims = ((0,), (0,))

    def body(q_ref, k_ref, v_ref, gb_ref, gbt_ref, ss_ref, ts_ref, do_ref, dq_ref, dk_ref, dv_ref, dgb_ref, ds_ref):
        @pl.when(pl.program_id(0) == 0)
        def _():
            ds_ref[...] = jnp.zeros_like(ds_ref)

        heads = list(range(DN_HEADS))
        sl = [slice(h * dh, (h + 1) * dh) for h in heads]
        sh = _dn_chunk_shared(gb_ref, gbt_ref)
        row, col, lane = sh["row"], sh["col"], sh["lane"]
        rs = lambda x: jnp.sum(x, axis=1, keepdims=True)
        tot = lambda x: jnp.sum(rs(x), axis=0, keepdims=True)
        st = [dict() for _ in heads]
        dgb_parts = []

        def s_common(h):
            st[h].update(_dn_chunk_common(q_ref[:, sl[h]], k_ref[:, sl[h]], v_ref[:, sl[h]], sh, h))
            st[h]["t"] = _split3(ts_ref[0, h])

        def s_sol(h):
            st[h]["sol"] = _dot3s(st[h]["t"], _split3(st[h]["rhs"]))

        def s_state(h):
            z = st[h]
            sol = z["sol"]
            kcd = sol[:, dh:]
            s = ss_ref[0, h]
            sb = s.astype(BF16)
            vnb = (sol[:, :dh] - _dot(kcd.astype(BF16), sb)).astype(BF16)
            ds_next = ds_ref[h]
            dsb = ds_next.astype(BF16)
            dob = do_ref[:, sl[h]].astype(BF16)
            z["dqg"] = _dot(dob, sb, nt_dims)
            ds = _dot(z["qg"].astype(BF16), dob, tn_dims)
            z["d_att"] = jnp.where(row >= col, _dot(dob, vnb, nt_dims), 0.0)
            dvn = _dot(z["att"].astype(BF16), dob, tn_dims) + _dot(z["kt"].astype(BF16), dsb)
            z["dkt"] = _dot(vnb, dsb, nt_dims)
            eg_l = jnp.exp(z["gam_l"])
            ds = ds + ds_next * eg_l
            z["dgam_l"] = tot(ds_next * s) * eg_l
            dvnb = dvn.astype(BF16)
            dkcd = -_dot(dvnb, sb, nt_dims)
            ds_ref[h] = ds - _dot(kcd.astype(BF16), dvnb, tn_dims)
            z["dsol"] = jnp.concatenate([dvn, dkcd], axis=1)

        def s_drhs(h):
            st[h]["drhs"] = _dot3s(st[h]["t"], _split3(st[h]["dsol"]), tn_dims)

        def s_dn(h):
            z = st[h]
            z["dn"] = jnp.where(row > col, -_dot3(z["drhs"], z["sol"], nt_dims), 0.0)

        def s_rest(h):
            z = st[h]
            k_, v_, kb, qb = z["k"], z["v"], z["kb"], z["qb"]
            beta, eg, dec, kk, qk, gam, gam_l = z["beta"], z["eg"], z["dec"], z["kk"], z["qk"], z["gam"], z["gam_l"]
            dn, d_att, dqg, dkt = z["dn"], z["d_att"], z["dqg"], z["dkt"]
            drv, drk = z["drhs"][:, :dh], z["drhs"][:, dh:]
            s_rkk = rs(drk * k_)
            dv_ref[:, sl[h]] = drv * beta
            dbeta = rs(drv * v_) + s_rkk * eg + rs(dn * kk * dec)
            dk = drk * (beta * eg)
            dgam = s_rkk * beta * eg
            dkk = (dn * beta * dec).astype(BF16)
            dd = dn * beta * kk + d_att * qk
            dqk = (d_att * dec).astype(BF16)
            dq_ref[:, sl[h]] = _dot(dqk, kb) + dqg * eg
            dk = dk + _dot(dqk, qb, tn_dims) + _dot(dkk, kb) + _dot(dkk, kb, tn_dims)
            w = dd * dec
            wh, wl = _split3(w)
            ones = jnp.ones((c, 128), BF16)
            col_sum = (_dot(wh, ones, tn_dims) + _dot(wl, ones, tn_dims))[:, 0:1]
            dgam = dgam + rs(w) - col_sum + rs(dqg * z["qg"]) - rs(dkt * z["kt"])
            dk_ref[:, sl[h]] = dk + dkt * jnp.exp(gam_l - gam)
            dgam_l = z["dgam_l"] + tot(dkt * z["kt"])
            rowc = lax.broadcasted_iota(jnp.int32, (c, 1), 0)
            dgam = dgam + jnp.where(rowc == c - 1, dgam_l, 0.0)
            dg = _split_dot((row <= col).astype(BF16), jnp.broadcast_to(dgam, (c, 128)))[:, 0:1]
            dgb_parts.append(jnp.where(lane == h, dg, 0.0) + jnp.where(lane == h + DN_HEADS, dbeta, 0.0))

        _emit_chains(heads, [s_common, s_sol, s_state, s_drhs, s_dn, s_rest], False)
        dgb = dgb_parts[0]
        for part in dgb_parts[1:]:
            dgb = dgb + part
        dgb_ref[...] = dgb

    blk = pl.BlockSpec((c, DN_WIDTH), lambda ci: (nc - 1 - ci, 0))
    sav = pl.BlockSpec((1, DN_HEADS, dh, dh), lambda ci: (nc - 1 - ci, 0, 0, 0))
    gspec = pl.BlockSpec((c, 128), lambda ci: (nc - 1 - ci, 0))
    return _pallas(
        body, name=name, grid=(nc,),
        in_specs=[blk, blk, blk, gspec, pl.BlockSpec((16, c), lambda ci: (0, nc - 1 - ci)), sav, sav, blk],
        out_specs=[blk, blk, blk, gspec],
        out_shape=[jax.ShapeDtypeStruct((r, DN_WIDTH), F32)] * 3 + [jax.ShapeDtypeStruct((r, 128), F32)],
        scratch_shapes=[pltpu.VMEM((DN_HEADS, dh, dh), F32)],
        compiler_params=_cparams(("arbitrary",)),
    )(q, k, v, gb, gbt, ssave, tsave, do)


def _dn_post_fwd(o, proj, g, name):
    r = o.shape[0]

    def body(o_ref, z_ref, g_ref, y_ref):
        g_ = g_ref[...]
        for hd in range(DN_HEADS):
            sl = slice(hd * 128, (hd + 1) * 128)
            sz, _ = _silu(z_ref[:, sl])
            y_ref[:, sl] = (_rms(o_ref[:, sl], g_) * sz).astype(BF16)

    return _pallas(body, name=name, grid=(r // ROW_TILE,),
                   in_specs=[_row_spec(DN_WIDTH), pl.BlockSpec((ROW_TILE, DN_WIDTH), lambda i: (i, 3)), _vec_spec(128)],
                   out_specs=_row_spec(DN_WIDTH), out_shape=jax.ShapeDtypeStruct((r, DN_WIDTH), BF16),
                   compiler_params=_cparams(("parallel",)))(o, proj, g)


def _dn_post_bwd(o, proj, g, dy, name):
    r = o.shape[0]

    def body(o_ref, z_ref, g_ref, dy_ref, do_ref, dz_ref, dg_ref):
        @pl.when(pl.program_id(0) == 0)
        def _():
            dg_ref[...] = jnp.zeros_like(dg_ref)

        g_ = g_ref[...]
        for hd in range(DN_HEADS):
            sl = slice(hd * 128, (hd + 1) * 128)
            z_ = z_ref[:, sl]
            sz, sg = _silu(z_)
            dy_ = dy_ref[:, sl]
            o_ = o_ref[:, sl]
            dz_ref[:, sl] = dy_ * _rms(o_, g_) * (sg * (1.0 + z_ * (1.0 - sg)))
            dx, dg = _rms_bwd(o_, g_, dy_ * sz)
            do_ref[:, sl] = dx
            dg_ref[...] += dg

    return _pallas(body, name=name, grid=(r // ROW_TILE,),
                   in_specs=[_row_spec(DN_WIDTH), pl.BlockSpec((ROW_TILE, DN_WIDTH), lambda i: (i, 3)), _vec_spec(128),
                             _row_spec(DN_WIDTH)],
                   out_specs=[_row_spec(DN_WIDTH), _row_spec(DN_WIDTH), _vec_spec(128)],
                   out_shape=[jax.ShapeDtypeStruct((r, DN_WIDTH), F32)] * 2 + [jax.ShapeDtypeStruct((1, 128), F32)],
                   compiler_params=_cparams(("arbitrary",)))(o, proj, g, dy)


def _exchange(arrays, scatter, name):
    n = len(arrays)
    outs_shape = [jax.ShapeDtypeStruct((N_DEV,) + (a.shape[1:] if sc else a.shape), a.dtype) for a, sc in zip(arrays, scatter)]

    def body(*refs):
        in_refs, out_refs = refs[:n], refs[n:2 * n]
        send_sems, recv_sems, local_sems = refs[2 * n:]
        mx, my, mc = lax.axis_index("x"), lax.axis_index("y"), lax.axis_index("c")
        me = 4 * mx + 2 * my + mc
        started = []
        for a in range(n):
            src_own = in_refs[a].at[me] if scatter[a] else in_refs[a]
            loc = pltpu.make_async_copy(src_own, out_refs[a].at[me], local_sems.at[a])
            loc.start()
            started.append(loc)
        remote = []
        for a in range(n):
            for kbits in range(1, N_DEV):
                px = lax.rem(mx + ((kbits >> 2) & 1), 2)
                py = lax.rem(my + ((kbits >> 1) & 1), 2)
                pc = lax.rem(mc + (kbits & 1), 2)
                src = in_refs[a].at[4 * px + 2 * py + pc] if scatter[a] else in_refs[a]
                cp = pltpu.make_async_remote_copy(
                    src_ref=src, dst_ref=out_refs[a].at[me],
                    send_sem=send_sems.at[a * N_DEV + kbits], recv_sem=recv_sems.at[a * N_DEV + kbits],
                    device_id=(px, py, pc), device_id_type=pl.DeviceIdType.MESH)
                cp.start()
                remote.append(cp)
        for cp in remote:
            cp.wait()
        for loc in started:
            loc.wait()

    hbm = pl.BlockSpec(memory_space=pl.ANY)
    return _pallas(
        body, name=name, in_specs=[hbm] * n, out_specs=[hbm] * n, out_shape=outs_shape,
        scratch_shapes=[pltpu.SemaphoreType.DMA((n * N_DEV,)), pltpu.SemaphoreType.DMA((n * N_DEV,)),
                        pltpu.SemaphoreType.DMA((n,))],
    )(*arrays)


def _adamw(gstack, w, m, v, name):
    a, b = w.shape
    ta = a
    for t in (1024, 512, 256, 128, 64, 32, 16, 8):
        if a % t == 0 and N_DEV * t * b * 4 <= 4 * 1024 * 1024:
            ta = t
            break
    c1 = 1.0 / (1.0 - ADAM_B1 ** ADAM_STEP)
    c2 = 1.0 / (1.0 - ADAM_B2 ** ADAM_STEP)

    def body(g_ref, w_ref, m_ref, v_ref, og_ref, od_ref, om_ref, ov_ref):
        g = g_ref[0].astype(F32)
        for s in range(1, N_DEV):
            g = g + g_ref[s].astype(F32)
        m_new = ADAM_B1 * m_ref[...] + (1.0 - ADAM_B1) * g
        v_new = ADAM_B2 * v_ref[...] + (1.0 - ADAM_B2) * (g * g)
        og_ref[...] = g
        om_ref[...] = m_new
        ov_ref[...] = v_new
        od_ref[...] = -ADAM_LR * ((m_new * c1) / (jnp.sqrt(v_new * c2) + ADAM_EPS) + ADAM_WD * w_ref[...])

    spec = pl.BlockSpec((ta, b), lambda i: (i, 0))
    return _pallas(
        body, name=name, grid=(a // ta,),
        in_specs=[pl.BlockSpec((N_DEV, ta, b), lambda i: (0, i, 0)), spec, spec, spec],
        out_specs=[spec] * 4, out_shape=[jax.ShapeDtypeStruct((a, b), F32)] * 4,
        compiler_params=_cparams(("parallel",)),
    )(gstack, w, m, v)


_WEIGHTS = ['meta_tokens', 'pre_mix_norm', 'post_mix_norm', 'pre_mlp_norm', 'post_mlp_norm', 'mlp_w1', 'mlp_w2',
            'w_in_even', 'w_out_even', 'sb_out_norm', 's5_lambda_re', 's5_lambda_im', 's5_log_dt', 's5_b_re', 's5_b_im',
            's5_c_re', 's5_c_im', 's5_d', 's5_w_glu', 's5_b_glu', 's5_out_norm', 'w_in_odd', 'dn_conv_w', 'dn_a_log',
            'dn_dt_bias', 'dn_out_norm', 'w_out_odd']
_SHARDED = ['meta_tokens', 'mlp_w1', 'mlp_w2', 'w_in_even', 'w_out_even', 's5_w_glu', 'w_in_odd', 'dn_conv_w', 'w_out_odd']
_SMALL = [n for n in _WEIGHTS if n not in _SHARDED]


def _view2d(name, a):
    return a.reshape(-1, a.shape[-1])


def _unshard(name, g):
    if name == 'mlp_w1':
        return g.reshape(N_DEV, 2, D_MODEL, -1).transpose(1, 2, 0, 3).reshape(2, D_MODEL, D_FF)
    if name == 'mlp_w2':
        return g.reshape(N_DEV, 2, -1, D_MODEL).transpose(1, 0, 2, 3).reshape(2, D_FF, D_MODEL)
    if name in ('w_in_even', 'w_in_odd', 'dn_conv_w', 'meta_tokens'):
        return g.transpose(1, 0, 2).reshape(g.shape[1], -1)
    return g.reshape(-1, g.shape[-1])


def _to_blocks(name, full):
    if name == 'mlp_w1':
        return full.reshape(2, D_MODEL, N_DEV, -1).transpose(2, 0, 1, 3).reshape(N_DEV, 2 * D_MODEL, -1)
    if name == 'mlp_w2':
        return full.reshape(2, N_DEV, -1, D_MODEL).transpose(1, 0, 2, 3).reshape(N_DEV, -1, D_MODEL)
    if name in ('w_in_even', 'w_in_odd', 'dn_conv_w', 'meta_tokens'):
        return full.reshape(full.shape[0], N_DEV, -1).transpose(1, 0, 2)
    return full.reshape(N_DEV, -1, full.shape[-1])


def _pack(parts):
    rows = []
    for p in parts:
        flat = p.reshape(-1)
        rows.append(jnp.pad(flat, (0, (-flat.shape[0]) % 128)).reshape(-1, 128))
    return jnp.concatenate(rows, axis=0)


def _unpack(packed, like):
    out, at = [], 0
    for p in like:
        n = math.prod(p.shape)
        nrow = -(-n // 128)
        out.append(packed[at:at + nrow].reshape(-1)[:n].reshape(p.shape))
        at += nrow
    return out


def _lane_vec(x, width=128):
    flat = x.reshape(-1)
    return jnp.pad(flat, (0, width - flat.shape[0])).reshape(1, width)


def kernel(x, meta_tokens, pre_mix_norm, post_mix_norm, pre_mlp_norm, post_mlp_norm, mlp_w1, mlp_w2, w_in_even, w_out_even, sb_out_norm, s5_lambda_re, s5_lambda_im, s5_log_dt, s5_b_re, s5_b_im, s5_c_re, s5_c_im, s5_d, s5_w_glu, s5_b_glu, s5_out_norm, w_in_odd, dn_conv_w, dn_a_log, dn_dt_bias, dn_out_norm, w_out_odd, loss_target, m_meta_tokens, m_pre_mix_norm, m_post_mix_norm, m_pre_mlp_norm, m_post_mlp_norm, m_mlp_w1, m_mlp_w2, m_w_in_even, m_w_out_even, m_sb_out_norm, m_s5_lambda_re, m_s5_lambda_im, m_s5_log_dt, m_s5_b_re, m_s5_b_im, m_s5_c_re, m_s5_c_im, m_s5_d, m_s5_w_glu, m_s5_b_glu, m_s5_out_norm, m_w_in_odd, m_dn_conv_w, m_dn_a_log, m_dn_dt_bias, m_dn_out_norm, m_w_out_odd, v_meta_tokens, v_pre_mix_norm, v_post_mix_norm, v_pre_mlp_norm, v_post_mlp_norm, v_mlp_w1, v_mlp_w2, v_w_in_even, v_w_out_even, v_sb_out_norm, v_s5_lambda_re, v_s5_lambda_im, v_s5_log_dt, v_s5_b_re, v_s5_b_im, v_s5_c_re, v_s5_c_im, v_s5_d, v_s5_w_glu, v_s5_b_glu, v_s5_out_norm, v_w_in_odd, v_dn_conv_w, v_dn_a_log, v_dn_dt_bias, v_dn_out_norm, v_w_out_odd):
    given = dict(locals())
    w = {n: given[n] for n in _WEIGHTS}
    mom_m = {n: given["m_" + n] for n in _WEIGHTS}
    mom_v = {n: given["v_" + n] for n in _WEIGHTS}

    seq = x.shape[1]
    assert x.shape[0] == 1 and seq % ROW_TILE == 0
    r = seq + ROW_TILE
    pad = ROW_TILE - N_META
    pad_tiles = 1

    wire = {n: (F32 if n in ('dn_conv_w', 'meta_tokens') else BF16) for n in _SHARDED}
    gathered = _exchange([_view2d(n, w[n]).astype(wire[n]) for n in _SHARDED], [False] * len(_SHARDED), "gather_weights")
    full = {n: _unshard(n, g) for n, g in zip(_SHARDED, gathered)}
    w1, w2 = full['mlp_w1'], full['mlp_w2']
    w_ie, w_oe, w_glu, w_oo = full['w_in_even'], full['w_out_even'], full['s5_w_glu'], full['w_out_odd']
    w_io = full['w_in_odd'][:, :4 * DN_WIDTH]
    w_ab = jnp.pad(full['w_in_odd'][:, 4 * DN_WIDTH:], ((0, 0), (0, 128 - 2 * DN_HEADS)))
    conv_w = full['dn_conv_w']
    row = lambda v_: v_.reshape(1, -1)

    hs0 = jnp.concatenate([jnp.zeros((pad, D_MODEL), F32), full['meta_tokens'], x[0]], axis=0)
    hn0 = _norm_pre(hs0, row(pre_mix_norm[0]), "pre_mix_0")
    qkv = _mm_fwd(hn0, w_ie[:, :3 * SB_WIDTH], "in_even_qkv", out_dtypes=(BF16,))
    u = _mm_fwd(hn0, w_ie[:, 3 * SB_WIDTH:], "in_even_u")
    q, k, v = qkv[:, :SB_WIDTH], qkv[:, SB_WIDTH:2 * SB_WIDTH], qkv[:, 2 * SB_WIDTH:]
    nb = r // ATT_BLK
    blocks_t = lambda t_: t_.reshape(nb, ATT_BLK, 4, 128).transpose(2, 0, 3, 1)
    o_sb, ssave = _sb_fwd(q, k, blocks_t(v), pad, "sb_fwd")
    on_sb = _norm_pre(o_sb, row(sb_out_norm[0]), "sb_out_norm")

    lam_re, lam_im, logdt, btr, bti, ctr, cti, s5_mask = _s5_expand(
        s5_lambda_re[0], s5_lambda_im[0], s5_log_dt[0], s5_b_re[0], s5_b_im[0], s5_c_re[0], s5_c_im[0])
    a_re, a_im, bbr, bbi = _s5_prep(lam_re, lam_im, logdt, btr, bti, "s5_prep")
    s5_wb = jnp.stack([_s5_block_diag_b(bbr, s5_mask), _s5_block_diag_b(bbi, s5_mask)]).astype(BF16)
    s5_wc = jnp.stack([_s5_block_diag_c(ctr, s5_mask), _s5_block_diag_c(cti, s5_mask)]).astype(BF16)
    s5_a = jnp.stack([a_re, a_im])
    s5_args = (s5_wb, s5_a, s5_wc, row(s5_d[0]), w_glu, row(s5_b_glu[0]), row(s5_out_norm[0]))
    y_s5, on_s5, xstart = _s5_fwd(u, *s5_args, "s5_fwd")

    merged = jnp.concatenate([on_sb, on_s5], axis=1)
    mix0 = _mm_fwd(merged, w_oe, "out_even")
    hs1, hn1 = _norm_post_pre(hs0, mix0, row(post_mix_norm[0]), row(pre_mlp_norm[0]), "post_mix_0")
    relu2 = lambda acc: (jnp.square(jnp.maximum(acc, 0.0)), jnp.maximum(acc, 0.0))
    r0, ra0 = _mm_fwd(hn1, w1[0], "mlp_up_0", out_dtypes=(BF16, BF16), epilogue=relu2)
    m0 = _mm_fwd(r0, w2[0], "mlp_down_0")
    hs2, hn2 = _norm_post_pre(hs1, m0, row(post_mlp_norm[0]), row(pre_mix_norm[1]), "post_mlp_0")

    proj = _mm_fwd(hn2, w_io, "in_odd")
    ab = _mm_fwd(hn2, w_ab, "in_odd_gates")
    alog, dtb = _lane_vec(dn_a_log[0]), _lane_vec(dn_dt_bias[0])
    co, qd, kd, vd, gb = _dn_pre_fwd(proj, ab, conv_w, alog, dtb, pad, "dn_pre")
    gbt = gb[:, :2 * DN_HEADS].T
    o_dn, s_dn, t_dn = _dn_fwd(qd, kd, vd, gb, gbt, "dn_fwd")
    on_dn = _dn_post_fwd(o_dn, proj, row(dn_out_norm[0]), "dn_post")
    mix1 = _mm_fwd(on_dn, w_oo, "out_odd")
    hs3, hn3 = _norm_post_pre(hs2, mix1, row(post_mix_norm[1]), row(pre_mlp_norm[1]), "post_mix_1")
    r1, ra1 = _mm_fwd(hn3, w1[1], "mlp_up_1", out_dtypes=(BF16, BF16), epilogue=relu2)
    m1 = _mm_fwd(r1, w2[1], "mlp_down_1")
    dhs, loss_part = _norm_post_loss(hs3, m1, row(post_mlp_norm[1]), loss_target[0], pad_tiles, "post_mlp_1_loss")
    loss = lax.psum(loss_part, ("x", "y", "c"))

    g = {}
    drelu2 = lambda acc, ra: (acc * (2.0 * ra.astype(F32)),)

    def mlp_bwd(layer, hn, rr, ra, dm):
        dw2 = _mm_wgrad(rr, dm, f"mlp_down_{layer}_wgrad")
        da = _mm_dgrad(dm, w2[layer], f"mlp_down_{layer}_dgrad", out_dtypes=(BF16,), extras=(ra,), epilogue=drelu2)
        dw1 = _mm_wgrad(hn, da, f"mlp_up_{layer}_wgrad")
        return dw1, dw2, _mm_dgrad(da, w1[layer], f"mlp_up_{layer}_dgrad")

    _, dm1, _, dg_post_mlp1 = _norm_bwd(dhs, post=(m1, row(post_mlp_norm[1])), pad=pad, name="post_mlp_1_bwd")
    dw1_1, dw2_1, dhn3 = mlp_bwd(1, hn3, r1, ra1, dm1)
    dhs, dmix1, dg_pre_mlp1, dg_post_mix1 = _norm_bwd(
        dhs, pre=(hs3, row(pre_mlp_norm[1]), dhn3), post=(mix1, row(post_mix_norm[1])), pad=pad, name="post_mix_1_bwd")

    g['w_out_odd'] = _mm_wgrad(on_dn, dmix1, "out_odd_wgrad")
    d_on_dn = _mm_dgrad(dmix1, w_oo, "out_odd_dgrad")
    do_dn, dz, dg_dn = _dn_post_bwd(o_dn, proj, row(dn_out_norm[0]), d_on_dn, "dn_post_bwd")
    dqd, dkd, dvd, dgb = _dn_bwd(qd, kd, vd, gb, gbt, s_dn, t_dn, do_dn, "dn_bwd")
    dco, dab, d_alog, d_dtb = _dn_pre_bwd(co, dqd, dkd, dvd, dgb, ab, alog, dtb, pad, "dn_pre_bwd")
    dpre, d_conv = _dn_conv_bwd(dco, proj, conv_w, "dn_conv_bwd")
    dproj = jnp.concatenate([dpre, dz], axis=1)
    g['w_in_odd'] = jnp.concatenate([_mm_wgrad(hn2, dproj, "in_odd_wgrad"),
                                     _mm_wgrad(hn2, dab, "in_odd_gates_wgrad")[:, :2 * DN_HEADS]], axis=1)
    dhn2 = _mm_dgrad(dab, w_ab, "in_odd_gates_dgrad")
    dhn2 = _mm_dgrad(dproj, w_io, "in_odd_dgrad", extras=(dhn2,), epilogue=lambda acc, other: (acc + other,))
    g['dn_conv_w'] = d_conv[:DN_CONV]
    g['dn_a_log'], g['dn_dt_bias'], g['dn_out_norm'] = d_alog[0, :DN_HEADS], d_dtb[0, :DN_HEADS], dg_dn[0]

    dhs, dm0, dg_pre_mix1, dg_post_mlp0 = _norm_bwd(
        dhs, pre=(hs2, row(pre_mix_norm[1]), dhn2), post=(m0, row(post_mlp_norm[0])), pad=pad, name="post_mlp_0_bwd")
    dw1_0, dw2_0, dhn1 = mlp_bwd(0, hn1, r0, ra0, dm0)
    dhs, dmix0, dg_pre_mlp0, dg_post_mix0 = _norm_bwd(
        dhs, pre=(hs1, row(pre_mlp_norm[0]), dhn1), post=(mix0, row(post_mix_norm[0])), pad=pad, name="post_mix_0_bwd")

    g['w_out_even'] = _mm_wgrad(merged, dmix0, "out_even_wgrad")
    dmerged = _mm_dgrad(dmix0, w_oe, "out_even_dgrad")
    _, do_sb, _, dg_sb = _norm_bwd(dmerged[:, :SB_WIDTH], post=(o_sb, row(sb_out_norm[0])), pad=pad, name="sb_out_norm_bwd")
    dq, dk4, dv4 = _sb_bwd(q, k, v, blocks_t(k), ssave, do_sb, pad, "sb_bwd")
    unheads = lambda t_: t_.transpose(1, 0, 2).reshape(r, SB_WIDTH)
    du, d_a, d_d, d_bglu, dg_s5, d_wb, d_wc, g['s5_w_glu'] = _s5_bwd(u, y_s5, dmerged[:, SB_WIDTH:], xstart, *s5_args, "s5_bwd")
    g_lr, g_li, g_dt, g_btr, g_bti = _s5_prep_bwd(
        lam_re, lam_im, logdt, btr, bti, d_a[0], d_a[1],
        _s5_diag_of_b(d_wb[0], s5_mask), _s5_diag_of_b(d_wb[1], s5_mask), "s5_prep_bwd")
    gg, nn, pp = S5_GROUPS, S5_STATE, S5_GROUP
    g['s5_lambda_re'], g['s5_lambda_im'] = g_lr.reshape(gg, nn), g_li.reshape(gg, nn)
    g['s5_log_dt'] = g_dt.reshape(gg, nn)[:, 0]
    g['s5_b_re'], g['s5_b_im'] = g_btr.T.reshape(gg, nn, pp), g_bti.T.reshape(gg, nn, pp)
    g['s5_c_re'] = _s5_diag_of_c(d_wc[0], s5_mask).reshape(gg, nn, pp).transpose(0, 2, 1)
    g['s5_c_im'] = _s5_diag_of_c(d_wc[1], s5_mask).reshape(gg, nn, pp).transpose(0, 2, 1)
    g['s5_d'], g['s5_b_glu'], g['s5_out_norm'], g['sb_out_norm'] = d_d[0], d_bglu[0], dg_s5[0], dg_sb[0]
    dqkvu = jnp.concatenate([dq, unheads(dk4), unheads(dv4), du], axis=1)
    g['w_in_even'] = _mm_wgrad(hn0, dqkvu, "in_even_wgrad")
    dhn0 = _mm_dgrad(dqkvu, w_ie, "in_even_dgrad")
    dhs, _, dg_pre_mix0, _ = _norm_bwd(dhs, pre=(hs0, row(pre_mix_norm[0]), dhn0), pad=pad, name="pre_mix_0_bwd")

    g['mlp_w1'] = jnp.stack([dw1_0, dw1_1])
    g['mlp_w2'] = jnp.stack([dw2_0, dw2_1])
    g['meta_tokens'] = dhs[pad:pad + N_META]
    g['pre_mix_norm'] = jnp.concatenate([dg_pre_mix0, dg_pre_mix1], axis=0)
    g['post_mix_norm'] = jnp.concatenate([dg_post_mix0, dg_post_mix1], axis=0)
    g['pre_mlp_norm'] = jnp.concatenate([dg_pre_mlp0, dg_pre_mlp1], axis=0)
    g['post_mlp_norm'] = jnp.concatenate([dg_post_mlp0, dg_post_mlp1], axis=0)
    grad_x = dhs[pad + N_META:][None]

    small_like = [w[n] for n in _SMALL]
    partial = [_to_blocks(n, g[n].reshape(full[n].shape)).astype(wire[n]) for n in _SHARDED]
    partial.append(_pack([g[n].reshape(w[n].shape) for n in _SMALL]))
    stacks = _exchange(partial, [True] * len(_SHARDED) + [False], "reduce_gradients")
    grads, deltas, new_m, new_v = {}, {}, {}, {}
    for n, st in zip(_SHARDED, stacks):
        outs = _adamw(st, _view2d(n, w[n]), _view2d(n, mom_m[n]), _view2d(n, mom_v[n]), f"adamw_{n}")
        grads[n], deltas[n], new_m[n], new_v[n] = (o.reshape(w[n].shape) for o in outs)
    outs = _adamw(stacks[-1], _pack(small_like), _pack([mom_m[n] for n in _SMALL]), _pack([mom_v[n] for n in _SMALL]),
                  "adamw_small")
    for dst, o in zip((grads, deltas, new_m, new_v), outs):
        for n, part in zip(_SMALL, _unpack(o, small_like)):
            dst[n] = part
    return (loss, grad_x, *[grads[n] for n in _WEIGHTS], *[deltas[n] for n in _WEIGHTS],
            *[new_m[n] for n in _WEIGHTS], *[new_v[n] for n in _WEIGHTS])
```

```python
import functools
import math

import jax
import jax.numpy as jnp
from jax import lax
from jax.experimental import pallas as pl
from jax.experimental.pallas import tpu as pltpu

F32 = jnp.float32
BF16 = jnp.bfloat16

D_MODEL = 1024
N_META = 16
SB_HEAD_DIM = 64
SB_WIDTH = 512
S5_WIDTH = 512
S5_GROUP = 16
S5_GROUPS = 32
S5_STATE = 64
S5_NS = S5_GROUPS * S5_STATE
DN_HEAD_DIM = 128
DN_HEADS = 8
DN_WIDTH = 1024
DN_CONV = 4
D_FF = 4096
EPS = 1e-6
N_DEV = 8

ADAM_LR = 0.001
ADAM_B1 = 0.9
ADAM_B2 = 0.999
ADAM_EPS = 1e-08
ADAM_WD = 0.01
ADAM_STEP = 10

ROW_TILE = 512
ATT_BLK = 256
SB_BLOCKS_PER_TRIP = 3
SB_LOG_ZERO = -106.0
SB_FWD_SKEW = False
SB_BWD_SKEW = True
DN_CHUNK = 128
DN_SUB = 16
S5_TILE = 128
VMEM_LIMIT = 56 * 1024 * 1024

_HIGH = lax.Precision.HIGHEST


def _pallas(body, **kw):
    return pl.pallas_call(body, **kw)


def _cparams(sem):
    return pltpu.CompilerParams(dimension_semantics=sem, vmem_limit_bytes=VMEM_LIMIT)


def _dot(a, b, dims=((1,), (0,))):
    return lax.dot_general(a, b, (dims, ((), ())), preferred_element_type=F32)


def _dot_hi(a, b):
    return lax.dot_general(a, b, (((1,), (0,)), ((), ())), preferred_element_type=F32, precision=_HIGH)


def _split_dot(m_bf16, x):
    hi = x.astype(BF16)
    lo = (x - hi.astype(F32)).astype(BF16)
    return _dot(m_bf16, hi) + _dot(m_bf16, lo)


def _matmul(a, b, *, ta=False, tb=False, tm, tn, tk, name, out_dtypes=(F32,), extras=(), epilogue=None):
    m, k = (a.shape[1], a.shape[0]) if ta else a.shape
    n = b.shape[0] if tb else b.shape[1]
    assert (b.shape[1] if tb else b.shape[0]) == k
    assert m % tm == 0 and n % tn == 0 and k % tk == 0, (name, m, n, k, tm, tn, tk)
    nk = k // tk
    n_ex = len(extras)
    n_out = len(out_dtypes)
    dims = ((0 if ta else 1,), (1 if tb else 0,))

    def finish(acc, ex_refs, o_refs):
        outs = (acc,) if epilogue is None else epilogue(acc, *[r[...] for r in ex_refs])
        for o_ref, o in zip(o_refs, outs):
            o_ref[...] = o.astype(o_ref.dtype)

    def body(*refs):
        a_ref, b_ref = refs[0], refs[1]
        ex_refs = refs[2:2 + n_ex]
        o_refs = refs[2 + n_ex:2 + n_ex + n_out]
        prod = _dot(a_ref[...].astype(BF16), b_ref[...].astype(BF16), dims)
        if nk == 1:
            finish(prod, ex_refs, o_refs)
            return
        acc_ref = refs[-1]
        kk = pl.program_id(2)

        @pl.when(kk == 0)
        def _():
            acc_ref[...] = prod

        @pl.when(kk > 0)
        def _():
            acc_ref[...] += prod

        @pl.when(kk == nk - 1)
        def _():
            finish(acc_ref[...], ex_refs, o_refs)

    a_spec = pl.BlockSpec((tk, tm), lambda j, i, kk: (kk, i)) if ta else pl.BlockSpec((tm, tk), lambda j, i, kk: (i, kk))
    b_spec = pl.BlockSpec((tn, tk), lambda j, i, kk: (j, kk)) if tb else pl.BlockSpec((tk, tn), lambda j, i, kk: (kk, j))
    o_spec = pl.BlockSpec((tm, tn), lambda j, i, kk: (i, j))
    outs = _pallas(
        body, name=name,
        grid=(n // tn, m // tm, nk),
        in_specs=[a_spec, b_spec] + [o_spec] * n_ex,
        out_specs=[o_spec] * n_out,
        out_shape=[jax.ShapeDtypeStruct((m, n), dt) for dt in out_dtypes],
        scratch_shapes=[] if nk == 1 else [pltpu.VMEM((tm, tn), F32)],
        compiler_params=_cparams(("parallel", "parallel", "arbitrary")),
    )(a, b, *extras)
    return outs[0] if n_out == 1 else outs


def _tile(n, cap):
    best = 128
    for t in range(128, min(n, cap) + 1, 128):
        if n % t == 0:
            best = t
    assert n % best == 0, n
    return best


MM_K_CAP = 4096
WGRAD_ROWS = 1536


MM_LHS_TILE_BYTES = 6 * 1024 * 1024


def _row_tile(x, depth):
    tall = 3 * ROW_TILE
    fits = tall * depth * x.dtype.itemsize <= MM_LHS_TILE_BYTES
    return tall if (x.shape[0] % tall == 0 and fits) else ROW_TILE


def _mm_fwd(x, w, name, **kw):
    k, n = w.shape
    tk = _tile(k, MM_K_CAP)
    return _matmul(x, w, tm=_row_tile(x, tk), tn=_tile(n, 1024), tk=tk, name=name, **kw)


def _mm_dgrad(dy, w, name, **kw):
    k, n = w.shape
    tk = _tile(n, MM_K_CAP)
    return _matmul(dy, w, tb=True, tm=_row_tile(dy, tk), tn=_tile(k, 1024), tk=tk, name=name, **kw)


def _mm_wgrad(x, dy, name):
    k, n = x.shape[1], dy.shape[1]
    rows = x.shape[0]
    return _matmul(x, dy, ta=True, tm=_tile(k, 512), tn=_tile(n, 1024),
                   tk=WGRAD_ROWS if rows % WGRAD_ROWS == 0 else ROW_TILE, name=name)


def _rms(x, g):
    r = lax.rsqrt(jnp.mean(x * x, axis=-1, keepdims=True) + EPS)
    return x * r * g


def _rms_bwd(x, g, dy):
    r = lax.rsqrt(jnp.mean(x * x, axis=-1, keepdims=True) + EPS)
    xh = x * r
    dxh = dy * g
    dx = r * (dxh - xh * jnp.mean(dxh * xh, axis=-1, keepdims=True))
    dg = jnp.sum(dy * xh, axis=0, keepdims=True)
    return dx, dg


def _row_spec(width, tile=ROW_TILE):
    return pl.BlockSpec((tile, width), lambda i: (i, 0))


def _vec_spec(width):
    return pl.BlockSpec((1, width), lambda i: (0, 0))


def _norm_pre(hs, g, name):
    r, d = hs.shape

    def body(x_ref, g_ref, o_ref):
        o_ref[...] = _rms(x_ref[...], g_ref[...]).astype(BF16)

    return _pallas(body, name=name, grid=(r // ROW_TILE,), in_specs=[_row_spec(d), _vec_spec(d)],
                   out_specs=_row_spec(d), out_shape=jax.ShapeDtypeStruct((r, d), BF16),
                   compiler_params=_cparams(("parallel",)))(hs, g)


def _norm_post_pre(hs, m, g_post, g_pre, name):
    r, d = hs.shape

    def body(hs_ref, m_ref, gp_ref, gn_ref, o_ref, hn_ref):
        new = hs_ref[...] + _rms(m_ref[...], gp_ref[...])
        o_ref[...] = new
        hn_ref[...] = _rms(new, gn_ref[...]).astype(BF16)

    return _pallas(body, name=name, grid=(r // ROW_TILE,),
                   in_specs=[_row_spec(d), _row_spec(d), _vec_spec(d), _vec_spec(d)],
                   out_specs=[_row_spec(d), _row_spec(d)],
                   out_shape=[jax.ShapeDtypeStruct((r, d), F32), jax.ShapeDtypeStruct((r, d), BF16)],
                   compiler_params=_cparams(("parallel",)))(hs, m, g_post, g_pre)


def _norm_post_loss(hs, m, g_post, target, pad_tiles, name):
    r, d = hs.shape
    nt = r // ROW_TILE

    def body(hs_ref, m_ref, gp_ref, t_ref, dhs_ref, loss_ref):
        i = pl.program_id(0)
        new = hs_ref[...] + _rms(m_ref[...], gp_ref[...])
        live = (i >= pad_tiles).astype(F32)
        diff = (new - t_ref[...]) * live
        dhs_ref[...] = diff * (1.0 / d)
        loss_ref[...] = jnp.full((8, 128), 0.5 / d * jnp.sum(diff * diff), F32)

    dhs, parts = _pallas(
        body, name=name, grid=(nt,),
        in_specs=[_row_spec(d), _row_spec(d), _vec_spec(d),
                  pl.BlockSpec((ROW_TILE, d), lambda i: (jnp.maximum(i - pad_tiles, 0), 0))],
        out_specs=[_row_spec(d), pl.BlockSpec((8, 128), lambda i: (i, 0))],
        out_shape=[jax.ShapeDtypeStruct((r, d), F32), jax.ShapeDtypeStruct((nt * 8, 128), F32)],
        compiler_params=_cparams(("parallel",)))(hs, m, g_post, target)
    return dhs, jnp.sum(parts[::8, 0])


def _norm_bwd(dhs, *, pre=None, post=None, pad=0, name):
    r, d = dhs.shape
    has_pre, has_post = pre is not None, post is not None

    def body(*refs):
        it = iter(refs)
        dhs_ref = next(it)
        if has_pre:
            hs_ref, gn_ref, dhn_ref = next(it), next(it), next(it)
        if has_post:
            m_ref, gp_ref = next(it), next(it)
        if has_pre:
            o_dhs, o_dgn = next(it), next(it)
        if has_post:
            o_dm, o_dgp = next(it), next(it)
        i = pl.program_id(0)
        live = (i * ROW_TILE + lax.broadcasted_iota(jnp.int32, (ROW_TILE, 1), 0)) >= pad
        cur = jnp.where(live, dhs_ref[...], 0.0)
        if has_pre:
            dx, dg = _rms_bwd(hs_ref[...], gn_ref[...], jnp.where(live, dhn_ref[...].astype(F32), 0.0))
            cur = cur + dx
            o_dhs[...] = cur

            @pl.when(i == 0)
            def _():
                o_dgn[...] = jnp.zeros_like(o_dgn)
            o_dgn[...] += dg
        if has_post:
            dm, dg = _rms_bwd(m_ref[...], gp_ref[...], cur)
            o_dm[...] = dm

            @pl.when(i == 0)
            def _():
                o_dgp[...] = jnp.zeros_like(o_dgp)
            o_dgp[...] += dg

    ins, in_specs, out_specs, out_shape = [dhs], [_row_spec(d)], [], []
    if has_pre:
        ins += list(pre)
        in_specs += [_row_spec(d), _vec_spec(d), _row_spec(d)]
        out_specs += [_row_spec(d), _vec_spec(d)]
        out_shape += [jax.ShapeDtypeStruct((r, d), F32), jax.ShapeDtypeStruct((1, d), F32)]
    if has_post:
        ins += list(post)
        in_specs += [_row_spec(d), _vec_spec(d)]
        out_specs += [_row_spec(d), _vec_spec(d)]
        out_shape += [jax.ShapeDtypeStruct((r, d), F32), jax.ShapeDtypeStruct((1, d), F32)]
    outs = list(_pallas(body, name=name, grid=(r // ROW_TILE,), in_specs=in_specs, out_specs=out_specs,
                        out_shape=out_shape, compiler_params=_cparams(("arbitrary",)))(*ins))
    dhs_new, dgn = (outs.pop(0), outs.pop(0)) if has_pre else (dhs, None)
    dm, dgp = (outs.pop(0), outs.pop(0)) if has_post else (None, None)
    return dhs_new, dm, dgn, dgp


def _softplus(z):
    return jnp.maximum(z, 0.0) + jnp.log(1.0 + jnp.exp(-jnp.abs(z)))


def _sb_consts(t):
    row = lax.broadcasted_iota(jnp.int32, (t, t), 0)
    col = lax.broadcasted_iota(jnp.int32, (t, t), 1)
    m_up = (col >= row).astype(BF16)
    m_low = (col <= row).astype(BF16)
    return m_up, m_low


def _emit_chains(chains, stages, skew):
    if skew:
        for step in range(len(chains) + len(stages) - 1):
            for si, stage in enumerate(stages):
                if 0 <= step - si < len(chains):
                    stage(chains[step - si])
    else:
        for stage in stages:
            for c in chains:
                stage(c)


def _sb_fwd(q, k, vt3, pad, name, ride=((), ())):
    r = q.shape[0]
    t = ATT_BLK
    nb = r // t
    nbp = -(-(nb + 1) // 8) * 8
    jmin = pad // t
    scale = SB_HEAD_DIM ** -0.5
    n_ride = len(ride[0])

    def body(q_ref, k_ref, vt_ref, *rest):
        ride_in, (o_ref, ss_ref), ride_out = rest[:n_ride], rest[n_ride:n_ride + 2], rest[n_ride + 2:2 * n_ride + 2]
        acc_ref, kn_ref = rest[2 * n_ride + 2:2 * n_ride + 4]
        ride_sems = rest[2 * n_ride + 4:]
        i = pl.program_id(1)
        if n_ride:
            @pl.when((pl.program_id(0) == 0) & (i == 0))
            def _():
                for cp in _exchange_copies(ride_in, ride_out, ride[1], *ride_sems):
                    cp.start()

        @pl.when(i == 0)
        def _():
            def blk(b, m):
                kb = k_ref[pl.ds(pl.multiple_of(b * t, t), t), :].astype(F32)
                return jnp.maximum(m, jnp.max(jnp.sum(kb * kb, axis=1, keepdims=True), axis=0, keepdims=True))
            kn_ref[...] = jnp.broadcast_to(lax.fori_loop(0, nb, blk, jnp.zeros((1, 1), F32)), (8, 128))

        qf = q_ref[...].astype(F32)
        z_bound = scale * jnp.sqrt(jnp.max(jnp.sum(qf * qf, axis=1, keepdims=True)) * jnp.max(kn_ref[...]))

        def need(carry):
            return jnp.maximum(jnp.max(carry[0]), jnp.max(carry[1])) + z_bound >= SB_LOG_ZERO

        qt = qf.T
        sub = lax.broadcasted_iota(jnp.int32, (128, 1), 0)
        m_up, _ = _sb_consts(t)
        kpos0 = lax.broadcasted_iota(jnp.int32, (t, 1), 0)
        qpos = i * t + lax.broadcasted_iota(jnp.int32, (1, t), 1)
        n_mid = jnp.maximum(i - 1 - jmin, 0)
        n_edge = jnp.where(i > jmin, 1, 0)
        qths = [jnp.where((sub >= 64 * h) & (sub < 64 * (h + 1)), qt * scale, 0.0).astype(BF16) for h in range(2)]
        acc_ref[...] = jnp.zeros_like(acc_ref)

        def sweep(js, carry, masked):
            kbs = [k_ref[pl.ds(pl.multiple_of(j * t, t), t), :] for j in js]
            vts = [vt_ref[0, j] for j in js]
            accs = [acc_ref[0], acc_ref[1]]
            s = list(carry)
            chains = [(n, h) for n in range(len(js)) for h in range(2)]
            masked = [masked] * len(js) if isinstance(masked, bool) else masked
            valid = [(js[n] * t + kpos0 < qpos) & (js[n] * t + kpos0 >= pad) if masked[n] else None for n in range(len(js))]
            zt, inc, saves = {}, {}, []

            def st_scores(c):
                zt[c] = _dot(kbs[c[0]], qths[c[1]])

            def st_cumsum(c):
                lk = -_softplus(zt[c])
                if masked[c[0]]:
                    lk = jnp.where(valid[c[0]], lk, 0.0)
                inc[c] = _split_dot(m_up, lk)

            def st_weights(c):
                n, h = c
                saves.append((h, js[n], s[h]))
                w = jnp.exp(zt[c] + inc[c] + s[h])
                if masked[n]:
                    w = jnp.where(valid[n], w, 0.0)
                accs[h] = accs[h] + _dot(vts[n], w.astype(BF16))
                s[h] = s[h] + inc[c][0:1, :]

            _emit_chains(chains, [st_scores, st_cumsum, st_weights], SB_FWD_SKEW)
            for h, j, val in saves:
                ss_ref[h, 0, pl.ds(j, 1), :] = val
            acc_ref[0] = accs[0]
            acc_ref[1] = accs[1]
            return tuple(s)

        zero = jnp.zeros((1, t), F32)
        bpi = SB_BLOCKS_PER_TRIP
        j, carry = lax.cond(
            i - 1 > jmin,
            lambda: (i - 2, sweep([i, i - 1], (zero, zero), [True, False])),
            lambda: (i - 1, sweep([i], (zero, zero), True)))
        j, carry = lax.while_loop(
            lambda st: (st[0] - bpi >= jmin) & need(st[1]),
            lambda st: (st[0] - bpi, sweep([st[0] - b for b in range(bpi)], st[1], False)), (j, carry))
        j, carry = lax.while_loop(
            lambda st: (st[0] > jmin) & need(st[1]),
            lambda st: (st[0] - 1, sweep([st[0]], st[1], False)), (j, carry))
        j, carry = lax.while_loop(
            lambda st: (st[0] == jmin) & (i > jmin) & need(st[1]),
            lambda st: (st[0] - 1, sweep([st[0]], st[1], True)), (j, carry))
        first = jnp.full((1, t), j + 1, jnp.int32).astype(F32)
        ss_ref[0, 0, nbp - 1:nbp, :] = first
        ss_ref[1, 0, nbp - 1:nbp, :] = first
        acc = jnp.where(sub < 64, acc_ref[0], acc_ref[1])
        o_ref[...] = acc.T
        if n_ride:
            @pl.when((pl.program_id(0) == 3) & (i == nb - 1))
            def _():
                for cp in _exchange_copies(ride_in, ride_out, ride[1], *ride_sems):
                    cp.wait()

    hbm = pl.BlockSpec(memory_space=pl.ANY)
    outs = _pallas(
        body, name=name, grid=(4, nb),
        in_specs=[pl.BlockSpec((t, 128), lambda hp, i: (i, hp)),
                  pl.BlockSpec((r, 128), lambda hp, i: (0, hp)),
                  pl.BlockSpec((1, nb, 128, t), lambda hp, i: (hp, 0, 0, 0))] + [hbm] * n_ride,
        out_specs=[pl.BlockSpec((t, 128), lambda hp, i: (i, hp)),
                   pl.BlockSpec((2, 1, nbp, t), lambda hp, i: (hp, i, 0, 0))] + [hbm] * n_ride,
        out_shape=[jax.ShapeDtypeStruct((r, SB_WIDTH), F32),
                   jax.ShapeDtypeStruct((8, nb, nbp, t), F32)] + _exchange_shapes(*ride),
        scratch_shapes=[pltpu.VMEM((2, 128, t), F32), pltpu.VMEM((8, 128), F32)] + (_exchange_sems(n_ride) if n_ride else []),
        compiler_params=_cparams(("arbitrary", "arbitrary")),
    )(q, k, vt3, *ride[0])
    return outs[0], outs[1], list(outs[2:])


def _sb_bwd(q, k, v, kt3, ssave, do, pad, name):
    r = q.shape[0]
    t = ATT_BLK
    nb = r // t
    nbp = ssave.shape[2]
    jmin = pad // t
    scale = SB_HEAD_DIM ** -0.5

    def body(q_ref, do_ref, k_ref, v_ref, kt_ref, ss_ref, dq_ref, dk_hbm, dv_hbm, dk_acc, dv_acc, dq_acc, sem):
        hp = pl.program_id(0)
        i = pl.program_id(1)

        @pl.when(i == 0)
        def _():
            dk_acc[...] = jnp.zeros_like(dk_acc)
            dv_acc[...] = jnp.zeros_like(dv_acc)

        qf = q_ref[...].astype(F32)
        dof = do_ref[...]
        qt = qf.T
        dot_ = dof.T
        sub = lax.broadcasted_iota(jnp.int32, (128, 1), 0)
        lane = lax.broadcasted_iota(jnp.int32, (1, 128), 1)
        m_up, m_low = _sb_consts(t)
        kpos0 = lax.broadcasted_iota(jnp.int32, (t, 1), 0)
        qpos = i * t + lax.broadcasted_iota(jnp.int32, (1, t), 1)
        first = jnp.clip(jnp.max(ss_ref[0, 0, nbp - 1:nbp, :]).astype(jnp.int32), jmin, i)
        mid0 = jnp.maximum(first, jmin + 1)
        pair = i - mid0 >= 1
        n_mid = jnp.maximum(i - mid0 - 1, 0)
        n_edge = jnp.where((i > jmin) & (first == jmin), 1, 0)
        in_t = [(sub >= 64 * h) & (sub < 64 * (h + 1)) for h in range(2)]
        in_l = [(lane >= 64 * h) & (lane < 64 * (h + 1)) for h in range(2)]
        qths = [jnp.where(in_t[h], qt * scale, 0.0).astype(BF16) for h in range(2)]
        doths = [jnp.where(in_t[h], dot_, 0.0).astype(BF16) for h in range(2)]
        qhs = [jnp.where(in_l[h], qf * scale, 0.0).astype(BF16) for h in range(2)]
        dohs = [jnp.where(in_l[h], dof, 0.0).astype(BF16) for h in range(2)]
        dq_acc[...] = jnp.zeros_like(dq_acc)

        def sweep(js, carry, masked):
            rows = [pl.ds(pl.multiple_of(j * t, t), t) for j in js]
            kbs = [k_ref[rw, :] for rw in rows]
            vbs = [v_ref[rw, :] for rw in rows]
            kts = [kt_ref[0, j] for j in js]
            sss = [[ss_ref[h, 0, pl.ds(j, 1), :] for h in range(2)] for j in js]
            dv_old = [dv_acc[rw, :] for rw in rows]
            dk_old = [dk_acc[rw, :] for rw in rows]
            dqs = [dq_acc[0], dq_acc[1]]
            ec = list(carry)
            chains = [(n, h) for n in range(len(js)) for h in range(2)]
            masked = [masked] * len(js) if isinstance(masked, bool) else masked
            valid = [(js[n] * t + kpos0 < qpos) & (js[n] * t + kpos0 >= pad) if masked[n] else None for n in range(len(js))]
            zt, dvt, sp, inc, e, big_e = {}, {}, {}, {}, {}, {}

            def st_scores(c):
                zt[c] = _dot(kbs[c[0]], qths[c[1]])
                dvt[c] = _dot(vbs[c[0]], doths[c[1]])

            def st_cumsum(c):
                sp[c] = _softplus(zt[c])
                lk = -sp[c]
                if masked[c[0]]:
                    lk = jnp.where(valid[c[0]], lk, 0.0)
                inc[c] = _split_dot(m_up, lk)

            def st_weights(c):
                n, h = c
                w = jnp.exp(zt[c] + inc[c] + sss[n][h])
                if masked[n]:
                    w = jnp.where(valid[n], w, 0.0)
                dv_old[n] = dv_old[n] + _dot(w.astype(BF16), dohs[h])
                e[c] = w * dvt[c]
                pinc = _split_dot(m_low, e[c])
                big_e[c] = pinc - e[c] + ec[h]
                ec[h] = ec[h] + pinc[t - 1:t, :]

            def st_dscores(c):
                n, h = c
                dz = e[c] - jnp.exp(zt[c] - sp[c]) * (e[c] + big_e[c])
                if masked[n]:
                    dz = jnp.where(valid[n], dz, 0.0)
                dzb = dz.astype(BF16)
                dqs[h] = dqs[h] + _dot(kts[n], dzb)
                dk_old[n] = dk_old[n] + _dot(dzb, qhs[h])

            _emit_chains(chains, [st_scores, st_cumsum, st_weights, st_dscores], SB_BWD_SKEW)
            for n, rw in enumerate(rows):
                dv_acc[rw, :] = dv_old[n]
                dk_acc[rw, :] = dk_old[n]
            dq_acc[0] = dqs[0]
            dq_acc[1] = dqs[1]
            return tuple(ec)

        zero = jnp.zeros((1, t), F32)
        bpi = SB_BLOCKS_PER_TRIP
        carry = lax.fori_loop(0, n_edge, lambda it, c: sweep([jmin + it * 0], c, True), (zero, zero))
        carry = lax.fori_loop(0, n_mid // bpi, lambda it, c: sweep([mid0 + bpi * it + b for b in range(bpi)], c, False), carry)
        n_rem = n_mid % bpi
        carry = lax.fori_loop(0, n_rem, lambda it, c: sweep([i - 1 - n_rem + it], c, False), carry)
        lax.cond(pair, lambda: sweep([i - 1, i], carry, [False, True]), lambda: sweep([i], carry, True))
        dq_ref[...] = (jnp.where(sub < 64, dq_acc[0], dq_acc[1]) * scale).T

        @pl.when(i == nb - 1)
        def _():
            c1 = pltpu.make_async_copy(dk_acc, dk_hbm.at[hp], sem.at[0])
            c2 = pltpu.make_async_copy(dv_acc, dv_hbm.at[hp], sem.at[1])
            c1.start()
            c2.start()
            c1.wait()
            c2.wait()

    return _pallas(
        body, name=name, grid=(4, nb),
        in_specs=[pl.BlockSpec((t, 128), lambda hp, i: (i, hp)),
                  pl.BlockSpec((t, 128), lambda hp, i: (i, hp)),
                  pl.BlockSpec((r, 128), lambda hp, i: (0, hp)),
                  pl.BlockSpec((r, 128), lambda hp, i: (0, hp)),
                  pl.BlockSpec((1, nb, 128, t), lambda hp, i: (hp, 0, 0, 0)),
                  pl.BlockSpec((2, 1, nbp, t), lambda hp, i: (hp, i, 0, 0))],
        out_specs=[pl.BlockSpec((t, 128), lambda hp, i: (i, hp)),
                   pl.BlockSpec(memory_space=pl.ANY), pl.BlockSpec(memory_space=pl.ANY)],
        out_shape=[jax.ShapeDtypeStruct((r, SB_WIDTH), F32),
                   jax.ShapeDtypeStruct((4, r, 128), F32), jax.ShapeDtypeStruct((4, r, 128), F32)],
        scratch_shapes=[pltpu.VMEM((r, 128), F32), pltpu.VMEM((r, 128), F32), pltpu.VMEM((2, 128, t), F32),
                        pltpu.SemaphoreType.DMA((2,))],
        compiler_params=_cparams(("arbitrary", "arbitrary")),
    )(q, do, k, v, kt3, ssave)


def _s5_disc(lam_re, lam_im, logdt, btr, bti):
    lr = jnp.minimum(lam_re, -1e-4)
    li = lam_im
    dt = jnp.exp(logdt)
    mag = jnp.exp(lr * dt)
    ang = li * dt
    a_re, a_im = mag * jnp.cos(ang), mag * jnp.sin(ang)
    den = lr * lr + li * li
    nr, ni = a_re - 1.0, a_im
    c_re = (nr * lr + ni * li) / den
    c_im = (ni * lr - nr * li) / den
    return a_re, a_im, c_re * btr - c_im * bti, c_re * bti + c_im * btr


def _s5_prep(lam_re, lam_im, logdt, btr, bti, name):
    ns = lam_re.shape[1]

    def body(lr_ref, li_ref, dt_ref, br_ref, bi_ref, ar_ref, ai_ref, bbr_ref, bbi_ref):
        ar, ai, bbr, bbi = _s5_disc(lr_ref[...], li_ref[...], dt_ref[...], br_ref[...], bi_ref[...])
        ar_ref[...] = ar
        ai_ref[...] = ai
        bbr_ref[...] = bbr
        bbi_ref[...] = bbi

    return _pallas(body, name=name,
                   out_shape=[jax.ShapeDtypeStruct((1, ns), F32)] * 2 + [jax.ShapeDtypeStruct((S5_GROUP, ns), F32)] * 2,
                   )(lam_re, lam_im, logdt, btr, bti)


def _s5_prep_bwd(lam_re, lam_im, logdt, btr, bti, dar, dai, dbbr, dbbi, name):
    ns = lam_re.shape[1]

    def body(lr_ref, li_ref, dt_ref, br_ref, bi_ref, dar_ref, dai_ref, dbr_ref, dbi_ref, o_lr, o_li, o_dt, o_br, o_bi):
        _, vjp = jax.vjp(_s5_disc, lr_ref[...], li_ref[...], dt_ref[...], br_ref[...], bi_ref[...])
        g = vjp((dar_ref[...], dai_ref[...], dbr_ref[...], dbi_ref[...]))
        o_lr[...] = g[0]
        o_li[...] = g[1]
        row = lax.broadcasted_iota(jnp.int32, (ns, ns), 0) // S5_STATE
        col = lax.broadcasted_iota(jnp.int32, (ns, ns), 1) // S5_STATE
        same = (row == col).astype(F32)
        o_dt[...] = _dot_hi(jnp.broadcast_to(g[2], (8, ns)), same)[0:1]
        o_br[...] = g[3]
        o_bi[...] = g[4]

    return _pallas(body, name=name,
                   out_shape=[jax.ShapeDtypeStruct((1, ns), F32)] * 3 + [jax.ShapeDtypeStruct((S5_GROUP, ns), F32)] * 2,
                   compiler_params=pltpu.CompilerParams(vmem_limit_bytes=VMEM_LIMIT),
                   )(lam_re, lam_im, logdt, btr, bti, dar, dai, dbbr, dbbi)


def _s5_scan(br, bi, ar, ai, t, reverse=False):
    ng = t // 8
    ns = br.shape[1]
    br, bi = br.reshape(ng, 8, ns), bi.reshape(ng, 8, ns)
    row8 = lax.broadcasted_iota(jnp.int32, (1, 8, 1), 1)
    pr, pi_ = ar, ai
    for k in (1, 2, 4):
        if reverse:
            sr, si, ok = pltpu.roll(br, 8 - k, 1), pltpu.roll(bi, 8 - k, 1), row8 < 8 - k
        else:
            sr, si, ok = pltpu.roll(br, k, 1), pltpu.roll(bi, k, 1), row8 >= k
        sr = jnp.where(ok, sr, 0.0)
        si = jnp.where(ok, si, 0.0)
        br, bi = br + pr * sr - pi_ * si, bi + pr * si + pi_ * sr
        pr, pi_ = pr * pr - pi_ * pi_, 2.0 * pr * pi_
    pw_r, pw_i = [ar], [ai]
    for _ in range(7):
        pw_r.append(pw_r[-1] * ar - pw_i[-1] * ai)
        pw_i.append(pw_r[-2] * ai + pw_i[-1] * ar)
    if reverse:
        pw_r.reverse()
        pw_i.reverse()
    p8r, p8i = jnp.concatenate(pw_r, axis=0), jnp.concatenate(pw_i, axis=0)
    out_r, out_i = [None] * ng, [None] * ng
    order = range(ng - 1, -1, -1) if reverse else range(ng)
    prev = None
    for g in order:
        gr, gi = br[g], bi[g]
        if prev is not None:
            edge = 0 if reverse else 7
            cr, ci = out_r[prev][edge:edge + 1], out_i[prev][edge:edge + 1]
            gr, gi = gr + p8r * cr - p8i * ci, gi + p8r * ci + p8i * cr
        out_r[g], out_i[g] = gr, gi
        prev = g
    return jnp.concatenate(out_r, axis=0), jnp.concatenate(out_i, axis=0)


def _s5_power_table(ar, ai, t, reverse=False):
    row = lax.broadcasted_iota(jnp.int32, (t, 1), 0)
    hot = row == (t - 1 if reverse else 0)
    return _s5_scan(jnp.where(hot, ar, 0.0), jnp.where(hot, ai, 0.0), ar, ai, t, reverse)


_GELU_C = math.sqrt(2.0 / math.pi)


def _gelu(y):
    th = jnp.tanh(_GELU_C * (y + 0.044715 * y * y * y))
    return 0.5 * y * (1.0 + th), th


def _sigmoid(x):
    return 1.0 / (1.0 + jnp.exp(-x))


def _s5_fwd(u, wb, a, wc, dskip, wglu, bglu, gnorm, name):
    r = u.shape[0]
    t = S5_TILE
    nt = r // t
    ns = wb.shape[2]
    w = S5_WIDTH

    def body(u_ref, wb_ref, a_ref, wc_ref, d_ref, wg_ref, bg_ref, gn_ref, y_ref, on_ref, xs_ref, pw_ref, carry_ref):
        i = pl.program_id(0)
        ar, ai = a_ref[0], a_ref[1]

        @pl.when(i == 0)
        def _():
            pr, pi_ = _s5_power_table(ar, ai, t)
            pw_ref[0] = pr
            pw_ref[1] = pi_
            carry_ref[...] = jnp.zeros_like(carry_ref)

        u_ = u_ref[...]
        ub = u_.astype(BF16)
        xr, xi = _s5_scan(_dot(ub, wb_ref[0]), _dot(ub, wb_ref[1]), ar, ai, t)
        cr, ci = carry_ref[0], carry_ref[1]
        xs_ref[0, 0:1, :] = cr
        xs_ref[0, 1:2, :] = ci
        pr, pi_ = pw_ref[0], pw_ref[1]
        xr = xr + pr * cr - pi_ * ci
        xi = xi + pr * ci + pi_ * cr
        carry_ref[0] = xr[t - 1:t, :]
        carry_ref[1] = xi[t - 1:t, :]
        y = _dot(xr.astype(BF16), wc_ref[0]) - _dot(xi.astype(BF16), wc_ref[1]) + d_ref[...] * u_
        h, _ = _gelu(y)
        gate = _sigmoid(_dot(h.astype(BF16), wg_ref[...]) + bg_ref[...])
        y_ref[...] = y
        on_ref[...] = _rms(h * gate, gn_ref[...]).astype(BF16)

    full = lambda shape: pl.BlockSpec(shape, lambda i: (0,) * len(shape))
    return _pallas(
        body, name=name, grid=(nt,),
        in_specs=[_row_spec(w, t), full((2, w, ns)), full((2, 1, ns)), full((2, ns, w)), full((1, w)),
                  full((w, w)), full((1, w)), full((1, w))],
        out_specs=[_row_spec(w, t), _row_spec(w, t), pl.BlockSpec((1, 2, ns), lambda i: (i, 0, 0))],
        out_shape=[jax.ShapeDtypeStruct((r, w), F32), jax.ShapeDtypeStruct((r, w), BF16),
                   jax.ShapeDtypeStruct((nt, 2, ns), F32)],
        scratch_shapes=[pltpu.VMEM((2, t, ns), F32), pltpu.VMEM((2, 1, ns), F32)],
        compiler_params=_cparams(("arbitrary",)),
    )(u, wb, a, wc, dskip, wglu, bglu, gnorm)


def _s5_bwd(u, y, don, xstart, wb, a, wc, dskip, wglu, bglu, gnorm, name, ride=((), ())):
    r = u.shape[0]
    t = S5_TILE
    nt = r // t
    ns = wb.shape[2]
    w = S5_WIDTH
    nt_dims = ((1,), (1,))
    tn_dims = ((0,), (0,))

    def body(u_ref, y_ref, don_ref, xs_ref, wb_hbm, a_ref, wc_hbm, d_ref, wg_ref, bg_ref, gn_ref,
             du_ref, da_ref, dd_ref, dbg_ref, dgn_ref, dwb_hbm, dwc_hbm, dwg_hbm,
             wb_ref, wc_ref, pw_ref, pwr_ref, lam_ref, acc_wb, acc_wc, acc_wg, sem):
        i = pl.program_id(0)
        ar, ai = a_ref[0], a_ref[1]

        @pl.when(i == 0)
        def _():
            c1 = pltpu.make_async_copy(wb_hbm, wb_ref, sem.at[0])
            c2 = pltpu.make_async_copy(wc_hbm, wc_ref, sem.at[1])
            c1.start()
            c2.start()
            pr, pi_ = _s5_power_table(ar, ai, t)
            pw_ref[0] = pr
            pw_ref[1] = pi_
            pr, pi_ = _s5_power_table(ar, -ai, t, reverse=True)
            pwr_ref[0] = pr
            pwr_ref[1] = pi_
            lam_ref[...] = jnp.zeros_like(lam_ref)
            acc_wb[...] = jnp.zeros_like(acc_wb)
            acc_wc[...] = jnp.zeros_like(acc_wc)
            acc_wg[...] = jnp.zeros_like(acc_wg)
            da_ref[...] = jnp.zeros_like(da_ref)
            dd_ref[...] = jnp.zeros_like(dd_ref)
            dbg_ref[...] = jnp.zeros_like(dbg_ref)
            dgn_ref[...] = jnp.zeros_like(dgn_ref)
            c1.wait()
            c2.wait()

        u_ = u_ref[...]
        y_ = y_ref[...]
        ub = u_.astype(BF16)
        h, th = _gelu(y_)
        hb = h.astype(BF16)
        wg = wg_ref[...]
        gate = _sigmoid(_dot(hb, wg) + bg_ref[...])
        d_out, dgn = _rms_bwd(h * gate, gn_ref[...], don_ref[...])
        dgn_ref[...] += dgn
        dhw = d_out * h * gate * (1.0 - gate)
        dhwb = dhw.astype(BF16)
        dh = d_out * gate + _dot(dhwb, wg, nt_dims)
        acc_wg[...] += _dot(hb, dhwb, tn_dims)
        dbg_ref[...] += jnp.sum(dhw, axis=0, keepdims=True)
        dgelu = 0.5 * (1.0 + th) + 0.5 * y_ * (1.0 - th * th) * _GELU_C * (1.0 + 3.0 * 0.044715 * y_ * y_)
        dy = dh * dgelu
        dd_ref[...] += jnp.sum(dy * u_, axis=0, keepdims=True)
        dyb = dy.astype(BF16)
        xr, xi = _s5_scan(_dot(ub, wb_ref[0]), _dot(ub, wb_ref[1]), ar, ai, t)
        cr, ci = xs_ref[0, 0:1, :], xs_ref[0, 1:2, :]
        pr, pi_ = pw_ref[0], pw_ref[1]
        xr = xr + pr * cr - pi_ * ci
        xi = xi + pr * ci + pi_ * cr
        acc_wc[0] += _dot(xr.astype(BF16), dyb, tn_dims)
        acc_wc[1] -= _dot(xi.astype(BF16), dyb, tn_dims)
        lr, li = _s5_scan(_dot(dyb, wc_ref[0], nt_dims), -_dot(dyb, wc_ref[1], nt_dims), ar, -ai, t, reverse=True)
        cr2, ci2 = lam_ref[0], lam_ref[1]
        pr, pi_ = pwr_ref[0], pwr_ref[1]
        lr = lr + pr * cr2 - pi_ * ci2
        li = li + pr * ci2 + pi_ * cr2
        lam_ref[0] = lr[0:1, :]
        lam_ref[1] = li[0:1, :]
        row = lax.broadcasted_iota(jnp.int32, (t, 1), 0)
        xpr = jnp.where(row == 0, cr, pltpu.roll(xr, 1, 0))
        xpi = jnp.where(row == 0, ci, pltpu.roll(xi, 1, 0))
        da_ref[0] += jnp.sum(lr * xpr + li * xpi, axis=0, keepdims=True)
        da_ref[1] += jnp.sum(li * xpr - lr * xpi, axis=0, keepdims=True)
        lrb, lib = lr.astype(BF16), li.astype(BF16)
        acc_wb[0] += _dot(ub, lrb, tn_dims)
        acc_wb[1] += _dot(ub, lib, tn_dims)
        du_ref[...] = d_ref[...] * dy + _dot(lrb, wb_ref[0], nt_dims) + _dot(lib, wb_ref[1], nt_dims)

        @pl.when(i == nt - 1)
        def _():
            cps = [pltpu.make_async_copy(acc_wb, dwb_hbm, sem.at[0]), pltpu.make_async_copy(acc_wc, dwc_hbm, sem.at[1]),
                   pltpu.make_async_copy(acc_wg, dwg_hbm, sem.at[2])]
            for c in cps:
                c.start()
            for c in cps:
                c.wait()

    n_ride = len(ride[0])
    n_in, n_out, n_scratch = 11, 8, 9

    def body_with_ride(*refs):
        ins, rest = refs[:n_in], refs[n_in:]
        ride_in, rest = rest[:n_ride], rest[n_ride:]
        outs, rest = rest[:n_out], rest[n_out:]
        ride_out, rest = rest[:n_ride], rest[n_ride:]
        scratch, ride_sems = rest[:n_scratch], rest[n_scratch:]
        if n_ride:
            @pl.when(pl.program_id(0) == 0)
            def _():
                for cp in _exchange_copies(ride_in, ride_out, ride[1], *ride_sems):
                    cp.start()
        body(*ins, *outs, *scratch)
        if n_ride:
            @pl.when(pl.program_id(0) == nt - 1)
            def _():
                for cp in _exchange_copies(ride_in, ride_out, ride[1], *ride_sems):
                    cp.wait()

    rev = lambda i: (nt - 1 - i, 0)
    full = lambda shape: pl.BlockSpec(shape, lambda i: (0,) * len(shape))
    hbm = pl.BlockSpec(memory_space=pl.ANY)
    outs = _pallas(
        body_with_ride, name=name, grid=(nt,),
        in_specs=[pl.BlockSpec((t, w), rev), pl.BlockSpec((t, w), rev), pl.BlockSpec((t, w), rev),
                  pl.BlockSpec((1, 2, ns), lambda i: (nt - 1 - i, 0, 0)), hbm, full((2, 1, ns)), hbm, full((1, w)),
                  full((w, w)), full((1, w)), full((1, w))] + [hbm] * n_ride,
        out_specs=[pl.BlockSpec((t, w), rev), full((2, 1, ns)), full((1, w)), full((1, w)), full((1, w)), hbm, hbm, hbm]
        + [hbm] * n_ride,
        out_shape=[jax.ShapeDtypeStruct((r, w), F32), jax.ShapeDtypeStruct((2, 1, ns), F32)]
        + [jax.ShapeDtypeStruct((1, w), F32)] * 3
        + [jax.ShapeDtypeStruct((2, w, ns), F32), jax.ShapeDtypeStruct((2, ns, w), F32), jax.ShapeDtypeStruct((w, w), F32)]
        + _exchange_shapes(*ride),
        scratch_shapes=[pltpu.VMEM((2, w, ns), BF16), pltpu.VMEM((2, ns, w), BF16),
                        pltpu.VMEM((2, t, ns), F32), pltpu.VMEM((2, t, ns), F32), pltpu.VMEM((2, 1, ns), F32),
                        pltpu.VMEM((2, w, ns), F32), pltpu.VMEM((2, ns, w), F32), pltpu.VMEM((w, w), F32),
                        pltpu.SemaphoreType.DMA((3,))] + (_exchange_sems(n_ride) if n_ride else []),
        compiler_params=_cparams(("arbitrary",)),
    )(u, y, don, xstart, wb, a, wc, dskip, wglu, bglu, gnorm, *ride[0])
    return tuple(outs[:n_out]) + (list(outs[n_out:]),)


def _s5_expand(lam_re, lam_im, log_dt, b_re, b_im, c_re, c_im):
    g, n, p = S5_GROUPS, S5_STATE, S5_GROUP
    ns = g * n
    rows = lambda x: x.reshape(1, ns)
    logdt = jnp.repeat(log_dt.reshape(g), n).reshape(1, ns)
    btr = b_re.reshape(ns, p).T
    bti = b_im.reshape(ns, p).T
    ctr = c_re.transpose(0, 2, 1).reshape(ns, p)
    cti = c_im.transpose(0, 2, 1).reshape(ns, p)
    mask = (jnp.arange(g * p)[:, None] // p) == (jnp.arange(ns)[None, :] // n)
    return rows(lam_re), rows(lam_im), logdt, btr, bti, ctr, cti, mask


def _s5_block_diag_b(bb, mask):
    return jnp.where(mask, jnp.tile(bb, (S5_GROUPS, 1)), 0.0)


def _s5_block_diag_c(ct, mask):
    return jnp.where(mask.T, jnp.tile(ct, (1, S5_GROUPS)), 0.0)


def _s5_diag_of_b(dwb, mask):
    return jnp.where(mask, dwb, 0.0).reshape(S5_GROUPS, S5_GROUP, -1).sum(0)


def _s5_diag_of_c(dwc, mask):
    ns = dwc.shape[0]
    return jnp.where(mask.T, dwc, 0.0).reshape(ns, S5_GROUPS, S5_GROUP).sum(1)


DN_PRE_TILE = 256
_DN_QKV = 3 * DN_WIDTH


def _halo_specs(width, tile, nt, prev):
    per = tile // 8
    if prev:
        return pl.BlockSpec((8, width), lambda i: (jnp.maximum(i * per - 1, 0), 0))
    return pl.BlockSpec((8, width), lambda i: (jnp.minimum((i + 1) * per, nt * per - 1), 0))


def _shift_down(x, halo, s, t):
    xx = jnp.concatenate([halo, x], axis=0)
    return pltpu.roll(xx, s, 0)[8:]


def _shift_up(x, halo, s, t):
    xx = jnp.concatenate([x, halo], axis=0)
    return pltpu.roll(xx, t + 8 - s, 0)[:t]


def _silu(x):
    s = _sigmoid(x)
    return x * s, s


def _dn_gates(ab, alog, dtb, live):
    lane = lax.broadcasted_iota(jnp.int32, (1, 128), 1)
    g = -jnp.exp(alog) * _softplus(ab + dtb)
    beta = _sigmoid(ab)
    return jnp.where(live & (lane < DN_HEADS), g, jnp.where(live & (lane < 2 * DN_HEADS), beta, 0.0))


def _dn_pre_fwd(proj, ab, conv_w, alog, dtb, pad, name):
    r = proj.shape[0]
    t = DN_PRE_TILE
    nt = r // t
    scale = DN_HEAD_DIM ** -0.5

    def body(x_ref, halo_ref, ab_ref, w_ref, al_ref, dt_ref, co_ref, q_ref, k_ref, v_ref, gb_ref):
        i = pl.program_id(0)
        x = x_ref[...]
        halo = jnp.where(i > 0, halo_ref[...], 0.0)
        w = w_ref[...]
        co = w[3:4] * x
        for tap in range(DN_CONV - 1):
            co = co + w[tap:tap + 1] * _shift_down(x, halo, DN_CONV - 1 - tap, t)
        co_ref[...] = co
        act, _ = _silu(co)
        for hd in range(DN_HEADS):
            sl = slice(hd * 128, (hd + 1) * 128)
            for base, o_ref, sc in ((0, q_ref, scale), (DN_WIDTH, k_ref, 1.0)):
                xh = act[:, base + hd * 128: base + (hd + 1) * 128]
                o_ref[:, sl] = xh * (lax.rsqrt(jnp.sum(xh * xh, axis=-1, keepdims=True) + EPS) * sc)
        v_ref[...] = act[:, 2 * DN_WIDTH:]
        rows = i * t + lax.broadcasted_iota(jnp.int32, (t, 1), 0)
        gb_ref[...] = _dn_gates(ab_ref[...], al_ref[...], dt_ref[...], rows >= pad)

    return _pallas(
        body, name=name, grid=(nt,),
        in_specs=[pl.BlockSpec((t, _DN_QKV), lambda i: (i, 0)), _halo_specs(_DN_QKV, t, nt, True), _row_spec(128, t),
                  pl.BlockSpec((DN_CONV, _DN_QKV), lambda i: (0, 0)), _vec_spec(128), _vec_spec(128)],
        out_specs=[_row_spec(_DN_QKV, t), _row_spec(DN_WIDTH, t), _row_spec(DN_WIDTH, t), _row_spec(DN_WIDTH, t), _row_spec(128, t)],
        out_shape=[jax.ShapeDtypeStruct((r, _DN_QKV), F32)] + [jax.ShapeDtypeStruct((r, DN_WIDTH), F32)] * 3
        + [jax.ShapeDtypeStruct((r, 128), F32)],
        compiler_params=_cparams(("parallel",)),
    )(proj, proj, ab, conv_w, alog, dtb)


def _dn_pre_bwd(co, dq, dk, dv, dgb, ab, alog, dtb, pad, name):
    r = co.shape[0]
    t = DN_PRE_TILE
    nt = r // t
    scale = DN_HEAD_DIM ** -0.5

    def body(co_ref, dq_ref, dk_ref, dv_ref, dgb_ref, ab_ref, al_ref, dt_ref, dco_ref, dab_ref, dal_ref, ddt_ref):
        i = pl.program_id(0)

        @pl.when(i == 0)
        def _():
            dal_ref[...] = jnp.zeros_like(dal_ref)
            ddt_ref[...] = jnp.zeros_like(ddt_ref)

        co_ = co_ref[...]
        act, sg = _silu(co_)
        dsilu = sg * (1.0 + co_ * (1.0 - sg))
        for hd in range(DN_HEADS):
            sl = slice(hd * 128, (hd + 1) * 128)
            for base, d_ref, sc in ((0, dq_ref, scale), (DN_WIDTH, dk_ref, 1.0)):
                cs = slice(base + hd * 128, base + (hd + 1) * 128)
                xh = act[:, cs]
                rn = lax.rsqrt(jnp.sum(xh * xh, axis=-1, keepdims=True) + EPS)
                xhat = xh * rn
                dy = d_ref[:, sl]
                dx = (sc * rn) * (dy - xhat * jnp.sum(dy * xhat, axis=-1, keepdims=True))
                dco_ref[:, cs] = dx * dsilu[:, cs]
        dco_ref[:, 2 * DN_WIDTH:] = dv_ref[...] * dsilu[:, 2 * DN_WIDTH:]
        rows = i * t + lax.broadcasted_iota(jnp.int32, (t, 1), 0)
        live = rows >= pad
        lane = lax.broadcasted_iota(jnp.int32, (1, 128), 1)
        ab_ = ab_ref[...]
        dgb_ = dgb_ref[...]
        is_g = live & (lane < DN_HEADS)
        is_b = live & (lane >= DN_HEADS) & (lane < 2 * DN_HEADS)
        arg = ab_ + dt_ref[...]
        ea = jnp.exp(al_ref[...])
        da = jnp.where(is_g, -dgb_ * ea * _sigmoid(arg), 0.0)
        beta = _sigmoid(ab_)
        dab_ref[...] = da + jnp.where(is_b, dgb_ * beta * (1.0 - beta), 0.0)
        ddt_ref[...] += jnp.sum(da, axis=0, keepdims=True)
        dal_ref[...] += jnp.sum(jnp.where(is_g, -dgb_ * ea * _softplus(arg), 0.0), axis=0, keepdims=True)

    return _pallas(
        body, name=name, grid=(nt,),
        in_specs=[_row_spec(_DN_QKV, t), _row_spec(DN_WIDTH, t), _row_spec(DN_WIDTH, t), _row_spec(DN_WIDTH, t),
                  _row_spec(128, t), _row_spec(128, t), _vec_spec(128), _vec_spec(128)],
        out_specs=[_row_spec(_DN_QKV, t), _row_spec(128, t), _vec_spec(128), _vec_spec(128)],
        out_shape=[jax.ShapeDtypeStruct((r, _DN_QKV), F32), jax.ShapeDtypeStruct((r, 128), F32),
                   jax.ShapeDtypeStruct((1, 128), F32), jax.ShapeDtypeStruct((1, 128), F32)],
        compiler_params=_cparams(("arbitrary",)),
    )(co, dq, dk, dv, dgb, ab, alog, dtb)


def _dn_conv_bwd(dco, proj, conv_w, name):
    r = dco.shape[0]
    t = DN_PRE_TILE
    nt = r // t

    def body(d_ref, dh_ref, x_ref, xh_ref, w_ref, dx_ref, dw_ref):
        i = pl.program_id(0)

        @pl.when(i == 0)
        def _():
            dw_ref[...] = jnp.zeros_like(dw_ref)

        d = d_ref[...]
        dhalo = jnp.where(i < nt - 1, dh_ref[...], 0.0)
        x = x_ref[...]
        xhalo = jnp.where(i > 0, xh_ref[...], 0.0)
        w = w_ref[...]
        dx = w[3:4] * d
        dws = [None] * DN_CONV
        dws[3] = jnp.sum(d * x, axis=0, keepdims=True)
        for tap in range(DN_CONV - 1):
            s = DN_CONV - 1 - tap
            dx = dx + w[tap:tap + 1] * _shift_up(d, dhalo, s, t)
            dws[tap] = jnp.sum(d * _shift_down(x, xhalo, s, t), axis=0, keepdims=True)
        dx_ref[...] = dx
        dw_ref[...] += jnp.concatenate(dws + [jnp.zeros((8 - DN_CONV, _DN_QKV), F32)], axis=0)

    return _pallas(
        body, name=name, grid=(nt,),
        in_specs=[_row_spec(_DN_QKV, t), _halo_specs(_DN_QKV, t, nt, False),
                  pl.BlockSpec((t, _DN_QKV), lambda i: (i, 0)), _halo_specs(_DN_QKV, t, nt, True),
                  pl.BlockSpec((DN_CONV, _DN_QKV), lambda i: (0, 0))],
        out_specs=[_row_spec(_DN_QKV, t), pl.BlockSpec((8, _DN_QKV), lambda i: (0, 0))],
        out_shape=[jax.ShapeDtypeStruct((r, _DN_QKV), F32), jax.ShapeDtypeStruct((8, _DN_QKV), F32)],
        compiler_params=_cparams(("arbitrary",)),
    )(dco, dco, proj, proj, conv_w)


def _split3(x):
    hi = x.astype(BF16)
    return hi, (x - hi.astype(F32)).astype(BF16)


def _dot3s(a, b, dims=((1,), (0,))):
    return _dot(a[0], b[0], dims) + (_dot(a[0], b[1], dims) + _dot(a[1], b[0], dims))


def _dot3(a, b, dims=((1,), (0,))):
    return _dot3s(_split3(a), _split3(b), dims)


def _dn_inverse_many(n_mats):
    c = n_mats[0].shape[0]
    row = lax.broadcasted_iota(jnp.int32, (c, c), 0)
    col = lax.broadcasted_iota(jnp.int32, (c, c), 1)
    eye = (row == col).astype(F32)
    same = row // DN_SUB == col // DN_SUB
    nds = [jnp.where(same, n, 0.0) for n in n_mats]
    nos = [n - nd for n, nd in zip(n_mats, nds)]

    def geometric(bs, order):
        xs = [eye + b for b in bs]
        sp = [_split3(b) for b in bs]
        k = 2
        while k < order:
            sp = [_split3(_dot3s(s_, s_)) for s_ in sp]
            xs = [x + _dot3s(_split3(x), s_) for x, s_ in zip(xs, sp)]
            k *= 2
        return xs

    tds = [_split3(td) for td in geometric([-nd for nd in nds], DN_SUB)]
    ms = [_dot3s(td, _split3(no)) for td, no in zip(tds, nos)]
    xs = geometric([-m for m in ms], c // DN_SUB)
    return [_dot3s(_split3(x), td) for x, td in zip(xs, tds)]


def _dn_chunk_shared(gb_ref, gbt_ref):
    c = DN_CHUNK
    row = lax.broadcasted_iota(jnp.int32, (c, c), 0)
    col = lax.broadcasted_iota(jnp.int32, (c, c), 1)
    gbv = gb_ref[...]
    gam_all = _split_dot((row >= col).astype(BF16), gbv)
    hi, lo = _split3(gbt_ref[...])
    tri_t = (row <= col).astype(BF16)
    return dict(row=row, col=col, gbv=gbv, gam_all=gam_all, gam_rows=_dot(hi, tri_t) + _dot(lo, tri_t),
                lane=lax.broadcasted_iota(jnp.int32, (1, 128), 1))


def _dn_chunk_common(q, k, v, sh, h):
    c = DN_CHUNK
    row, col, lane = sh["row"], sh["col"], sh["lane"]
    gam = jnp.sum(jnp.where(lane == h, sh["gam_all"], 0.0), axis=1, keepdims=True)
    beta = jnp.sum(jnp.where(lane == h + DN_HEADS, sh["gbv"], 0.0), axis=1, keepdims=True)
    gam_row = sh["gam_rows"][h:h + 1]
    dec = jnp.where(row >= col, jnp.exp(jnp.minimum(gam - gam_row, 0.0)), 0.0)
    kb, qb = k.astype(BF16), q.astype(BF16)
    nt_dims = ((1,), (1,))
    kk = _dot(kb, kb, nt_dims)
    qk = _dot(qb, kb, nt_dims)
    eg = jnp.exp(gam)
    gam_l = gam[c - 1:c, :]
    return dict(q=q, k=k, v=v, qb=qb, kb=kb, gam=gam, beta=beta, dec=dec, kk=kk, qk=qk, eg=eg, gam_l=gam_l,
                row=row, col=col, lane=lane, att=qk * dec, qg=q * eg, kt=k * jnp.exp(gam_l - gam),
                rhs=jnp.concatenate([v * beta, k * (beta * eg)], axis=1))


def _dn_fwd(q, k, v, gb, gbt, name):
    r = q.shape[0]
    c = DN_CHUNK
    nc = r // c
    dh = DN_HEAD_DIM
    tn_dims = ((0,), (0,))

    def body(q_ref, k_ref, v_ref, gb_ref, gbt_ref, o_ref, ss_ref, ts_ref, s_ref):
        @pl.when(pl.program_id(0) == 0)
        def _():
            s_ref[...] = jnp.zeros_like(s_ref)

        heads = list(range(DN_HEADS))
        sl = [slice(h * dh, (h + 1) * dh) for h in heads]
        sh = _dn_chunk_shared(gb_ref, gbt_ref)
        zs = [_dn_chunk_common(q_ref[:, sl[h]], k_ref[:, sl[h]], v_ref[:, sl[h]], sh, h) for h in heads]
        t_invs = _dn_inverse_many([jnp.where(sh["row"] > sh["col"], z["beta"] * z["kk"] * z["dec"], 0.0) for z in zs])
        sols = [_dot3(t_inv, z["rhs"]) for t_inv, z in zip(t_invs, zs)]
        ss = [s_ref[h] for h in heads]
        sbs = [s.astype(BF16) for s in ss]
        vnbs = [(sol[:, :dh] - _dot(sol[:, dh:].astype(BF16), sb)).astype(BF16) for sol, sb in zip(sols, sbs)]
        for h in heads:
            o_ref[:, sl[h]] = _dot(zs[h]["qg"].astype(BF16), sbs[h]) + _dot(zs[h]["att"].astype(BF16), vnbs[h])
        for h in heads:
            ss_ref[0, h] = ss[h]
            ts_ref[0, h] = t_invs[h]
            s_ref[h] = ss[h] * jnp.exp(zs[h]["gam_l"]) + _dot(zs[h]["kt"].astype(BF16), vnbs[h], tn_dims)

    blk = pl.BlockSpec((c, DN_WIDTH), lambda ci: (ci, 0))
    sav = pl.BlockSpec((1, DN_HEADS, dh, dh), lambda ci: (ci, 0, 0, 0))
    return _pallas(
        body, name=name, grid=(nc,),
        in_specs=[blk, blk, blk, pl.BlockSpec((c, 128), lambda ci: (ci, 0)), pl.BlockSpec((16, c), lambda ci: (0, ci))],
        out_specs=[blk, sav, sav],
        out_shape=[jax.ShapeDtypeStruct((r, DN_WIDTH), F32), jax.ShapeDtypeStruct((nc, DN_HEADS, dh, dh), F32),
                   jax.ShapeDtypeStruct((nc, DN_HEADS, dh, dh), F32)],
        scratch_shapes=[pltpu.VMEM((DN_HEADS, dh, dh), F32)],
        compiler_params=_cparams(("arbitrary",)),
    )(q, k, v, gb, gbt)


def _dn_bwd(q, k, v, gb, gbt, ssave, tsave, do, name):
    r = q.shape[0]
    c = DN_CHUNK
    nc = r // c
    dh = DN_HEAD_DIM
    nt_dims = ((1,), (1,))
    tn_dims = ((0,), (0,))

    def body(q_ref, k_ref, v_ref, gb_ref, gbt_ref, ss_ref, ts_ref, do_ref, dq_ref, dk_ref, dv_ref, dgb_ref, ds_ref):
        @pl.when(pl.program_id(0) == 0)
        def _():
            ds_ref[...] = jnp.zeros_like(ds_ref)

        heads = list(range(DN_HEADS))
        sl = [slice(h * dh, (h + 1) * dh) for h in heads]
        sh = _dn_chunk_shared(gb_ref, gbt_ref)
        row, col, lane = sh["row"], sh["col"], sh["lane"]
        rs = lambda x: jnp.sum(x, axis=1, keepdims=True)
        tot = lambda x: jnp.sum(rs(x), axis=0, keepdims=True)
        st = [dict() for _ in heads]
        dgb_parts = []

        def s_common(h):
            st[h].update(_dn_chunk_common(q_ref[:, sl[h]], k_ref[:, sl[h]], v_ref[:, sl[h]], sh, h))
            st[h]["t"] = _split3(ts_ref[0, h])

        def s_sol(h):
            st[h]["sol"] = _dot3s(st[h]["t"], _split3(st[h]["rhs"]))

        def s_state(h):
            z = st[h]
            sol = z["sol"]
            kcd = sol[:, dh:]
            s = ss_ref[0, h]
            sb = s.astype(BF16)
            vnb = (sol[:, :dh] - _dot(kcd.astype(BF16), sb)).astype(BF16)
            ds_next = ds_ref[h]
            dsb = ds_next.astype(BF16)
            dob = do_ref[:, sl[h]].astype(BF16)
            z["dqg"] = _dot(dob, sb, nt_dims)
            ds = _dot(z["qg"].astype(BF16), dob, tn_dims)
            z["d_att"] = jnp.where(row >= col, _dot(dob, vnb, nt_dims), 0.0)
            dvn = _dot(z["att"].astype(BF16), dob, tn_dims) + _dot(z["kt"].astype(BF16), dsb)
            z["dkt"] = _dot(vnb, dsb, nt_dims)
            eg_l = jnp.exp(z["gam_l"])
            ds = ds + ds_next * eg_l
            z["dgam_l"] = tot(ds_next * s) * eg_l
            dvnb = dvn.astype(BF16)
            dkcd = -_dot(dvnb, sb, nt_dims)
            ds_ref[h] = ds - _dot(kcd.astype(BF16), dvnb, tn_dims)
            z["dsol"] = jnp.concatenate([dvn, dkcd], axis=1)

        def s_drhs(h):
            st[h]["drhs"] = _dot3s(st[h]["t"], _split3(st[h]["dsol"]), tn_dims)

        def s_dn(h):
            z = st[h]
            z["dn"] = jnp.where(row > col, -_dot3(z["drhs"], z["sol"], nt_dims), 0.0)

        def s_rest(h):
            z = st[h]
            k_, v_, kb, qb = z["k"], z["v"], z["kb"], z["qb"]
            beta, eg, dec, kk, qk, gam, gam_l = z["beta"], z["eg"], z["dec"], z["kk"], z["qk"], z["gam"], z["gam_l"]
            dn, d_att, dqg, dkt = z["dn"], z["d_att"], z["dqg"], z["dkt"]
            drv, drk = z["drhs"][:, :dh], z["drhs"][:, dh:]
            s_rkk = rs(drk * k_)
            dv_ref[:, sl[h]] = drv * beta
            dbeta = rs(drv * v_) + s_rkk * eg + rs(dn * kk * dec)
            dk = drk * (beta * eg)
            dgam = s_rkk * beta * eg
            dkk = (dn * beta * dec).astype(BF16)
            dd = dn * beta * kk + d_att * qk
            dqk = (d_att * dec).astype(BF16)
            dq_ref[:, sl[h]] = _dot(dqk, kb) + dqg * eg
            dk = dk + _dot(dqk, qb, tn_dims) + _dot(dkk, kb) + _dot(dkk, kb, tn_dims)
            w = dd * dec
            wh, wl = _split3(w)
            ones = jnp.ones((c, 128), BF16)
            col_sum = (_dot(wh, ones, tn_dims) + _dot(wl, ones, tn_dims))[:, 0:1]
            dgam = dgam + rs(w) - col_sum + rs(dqg * z["qg"]) - rs(dkt * z["kt"])
            dk_ref[:, sl[h]] = dk + dkt * jnp.exp(gam_l - gam)
            dgam_l = z["dgam_l"] + tot(dkt * z["kt"])
            rowc = lax.broadcasted_iota(jnp.int32, (c, 1), 0)
            dgam = dgam + jnp.where(rowc == c - 1, dgam_l, 0.0)
            dg = _split_dot((row <= col).astype(BF16), jnp.broadcast_to(dgam, (c, 128)))[:, 0:1]
            dgb_parts.append(jnp.where(lane == h, dg, 0.0) + jnp.where(lane == h + DN_HEADS, dbeta, 0.0))

        _emit_chains(heads, [s_common, s_sol, s_state, s_drhs, s_dn, s_rest], False)
        dgb = dgb_parts[0]
        for part in dgb_parts[1:]:
            dgb = dgb + part
        dgb_ref[...] = dgb

    blk = pl.BlockSpec((c, DN_WIDTH), lambda ci: (nc - 1 - ci, 0))
    sav = pl.BlockSpec((1, DN_HEADS, dh, dh), lambda ci: (nc - 1 - ci, 0, 0, 0))
    gspec = pl.BlockSpec((c, 128), lambda ci: (nc - 1 - ci, 0))
    return _pallas(
        body, name=name, grid=(nc,),
        in_specs=[blk, blk, blk, gspec, pl.BlockSpec((16, c), lambda ci: (0, nc - 1 - ci)), sav, sav, blk],
        out_specs=[blk, blk, blk, gspec],
        out_shape=[jax.ShapeDtypeStruct((r, DN_WIDTH), F32)] * 3 + [jax.ShapeDtypeStruct((r, 128), F32)],
        scratch_shapes=[pltpu.VMEM((DN_HEADS, dh, dh), F32)],
        compiler_params=_cparams(("arbitrary",)),
    )(q, k, v, gb, gbt, ssave, tsave, do)


def _dn_post_fwd(o, proj, g, name):
    r = o.shape[0]

    def body(o_ref, z_ref, g_ref, y_ref):
        g_ = g_ref[...]
        for hd in range(DN_HEADS):
            sl = slice(hd * 128, (hd + 1) * 128)
            sz, _ = _silu(z_ref[:, sl])
            y_ref[:, sl] = (_rms(o_ref[:, sl], g_) * sz).astype(BF16)

    return _pallas(body, name=name, grid=(r // ROW_TILE,),
                   in_specs=[_row_spec(DN_WIDTH), pl.BlockSpec((ROW_TILE, DN_WIDTH), lambda i: (i, 3)), _vec_spec(128)],
                   out_specs=_row_spec(DN_WIDTH), out_shape=jax.ShapeDtypeStruct((r, DN_WIDTH), BF16),
                   compiler_params=_cparams(("parallel",)))(o, proj, g)


def _dn_post_bwd(o, proj, g, dy, name):
    r = o.shape[0]

    def body(o_ref, z_ref, g_ref, dy_ref, do_ref, dz_ref, dg_ref):
        @pl.when(pl.program_id(0) == 0)
        def _():
            dg_ref[...] = jnp.zeros_like(dg_ref)

        g_ = g_ref[...]
        for hd in range(DN_HEADS):
            sl = slice(hd * 128, (hd + 1) * 128)
            z_ = z_ref[:, sl]
            sz, sg = _silu(z_)
            dy_ = dy_ref[:, sl]
            o_ = o_ref[:, sl]
            dz_ref[:, sl] = dy_ * _rms(o_, g_) * (sg * (1.0 + z_ * (1.0 - sg)))
            dx, dg = _rms_bwd(o_, g_, dy_ * sz)
            do_ref[:, sl] = dx
            dg_ref[...] += dg

    return _pallas(body, name=name, grid=(r // ROW_TILE,),
                   in_specs=[_row_spec(DN_WIDTH), pl.BlockSpec((ROW_TILE, DN_WIDTH), lambda i: (i, 3)), _vec_spec(128),
                             _row_spec(DN_WIDTH)],
                   out_specs=[_row_spec(DN_WIDTH), _row_spec(DN_WIDTH), _vec_spec(128)],
                   out_shape=[jax.ShapeDtypeStruct((r, DN_WIDTH), F32)] * 2 + [jax.ShapeDtypeStruct((1, 128), F32)],
                   compiler_params=_cparams(("arbitrary",)))(o, proj, g, dy)


def _exchange(arrays, scatter, name):
    n = len(arrays)

    def body(*refs):
        copies = _exchange_copies(refs[:n], refs[n:2 * n], scatter, *refs[2 * n:])
        for cp in copies:
            cp.start()
        for cp in copies:
            cp.wait()

    hbm = pl.BlockSpec(memory_space=pl.ANY)
    return _pallas(
        body, name=name, in_specs=[hbm] * n, out_specs=[hbm] * n, out_shape=_exchange_shapes(arrays, scatter),
        scratch_shapes=_exchange_sems(n),
    )(*arrays)


def _exchange_shapes(arrays, scatter):
    return [jax.ShapeDtypeStruct((N_DEV,) + (a.shape[1:] if sc else a.shape), a.dtype) for a, sc in zip(arrays, scatter)]


def _exchange_sems(n):
    return [pltpu.SemaphoreType.DMA((n * N_DEV,)), pltpu.SemaphoreType.DMA((n * N_DEV,)), pltpu.SemaphoreType.DMA((n,))]


def _exchange_copies(in_refs, out_refs, scatter, send_sems, recv_sems, local_sems):
    mx, my, mc = lax.axis_index("x"), lax.axis_index("y"), lax.axis_index("c")
    me = 4 * mx + 2 * my + mc
    copies = []
    for a in range(len(in_refs)):
        src_own = in_refs[a].at[me] if scatter[a] else in_refs[a]
        copies.append(pltpu.make_async_copy(src_own, out_refs[a].at[me], local_sems.at[a]))
        for kbits in range(1, N_DEV):
            px = lax.rem(mx + ((kbits >> 2) & 1), 2)
            py = lax.rem(my + ((kbits >> 1) & 1), 2)
            pc = lax.rem(mc + (kbits & 1), 2)
            src = in_refs[a].at[4 * px + 2 * py + pc] if scatter[a] else in_refs[a]
            copies.append(pltpu.make_async_remote_copy(
                src_ref=src, dst_ref=out_refs[a].at[me],
                send_sem=send_sems.at[a * N_DEV + kbits], recv_sem=recv_sems.at[a * N_DEV + kbits],
                device_id=(px, py, pc), device_id_type=pl.DeviceIdType.MESH))
    return copies


def _adamw(gstack, w, m, v, name):
    a, b = w.shape
    ta = a
    for t in (1024, 512, 256, 128, 64, 32, 16, 8):
        if a % t == 0 and N_DEV * t * b * 4 <= 4 * 1024 * 1024:
            ta = t
            break
    c1 = 1.0 / (1.0 - ADAM_B1 ** ADAM_STEP)
    c2 = 1.0 / (1.0 - ADAM_B2 ** ADAM_STEP)

    def body(g_ref, w_ref, m_ref, v_ref, og_ref, od_ref, om_ref, ov_ref):
        g = g_ref[0].astype(F32)
        for s in range(1, N_DEV):
            g = g + g_ref[s].astype(F32)
        m_new = ADAM_B1 * m_ref[...] + (1.0 - ADAM_B1) * g
        v_new = ADAM_B2 * v_ref[...] + (1.0 - ADAM_B2) * (g * g)
        og_ref[...] = g
        om_ref[...] = m_new
        ov_ref[...] = v_new
        od_ref[...] = -ADAM_LR * ((m_new * c1) / (jnp.sqrt(v_new * c2) + ADAM_EPS) + ADAM_WD * w_ref[...])

    spec = pl.BlockSpec((ta, b), lambda i: (i, 0))
    return _pallas(
        body, name=name, grid=(a // ta,),
        in_specs=[pl.BlockSpec((N_DEV, ta, b), lambda i: (0, i, 0)), spec, spec, spec],
        out_specs=[spec] * 4, out_shape=[jax.ShapeDtypeStruct((a, b), F32)] * 4,
        compiler_params=_cparams(("parallel",)),
    )(gstack, w, m, v)


_WEIGHTS = ['meta_tokens', 'pre_mix_norm', 'post_mix_norm', 'pre_mlp_norm', 'post_mlp_norm', 'mlp_w1', 'mlp_w2',
            'w_in_even', 'w_out_even', 'sb_out_norm', 's5_lambda_re', 's5_lambda_im', 's5_log_dt', 's5_b_re', 's5_b_im',
            's5_c_re', 's5_c_im', 's5_d', 's5_w_glu', 's5_b_glu', 's5_out_norm', 'w_in_odd', 'dn_conv_w', 'dn_a_log',
            'dn_dt_bias', 'dn_out_norm', 'w_out_odd']
_SHARDED = ['meta_tokens', 'mlp_w1', 'mlp_w2', 'w_in_even', 'w_out_even', 's5_w_glu', 'w_in_odd', 'dn_conv_w', 'w_out_odd']
_SMALL = [n for n in _WEIGHTS if n not in _SHARDED]
_GATHER_FIRST = ['meta_tokens', 'w_in_even', 's5_w_glu', 'w_out_even']
_GATHER_LATE = [n for n in _SHARDED if n not in _GATHER_FIRST]
_REDUCE_EARLY = ['mlp_w1', 'mlp_w2', 'w_in_odd', 'dn_conv_w', 'w_out_odd', 'w_out_even']


def _view2d(name, a):
    return a.reshape(-1, a.shape[-1])


def _unshard(name, g):
    if name == 'mlp_w1':
        return g.reshape(N_DEV, 2, D_MODEL, -1).transpose(1, 2, 0, 3).reshape(2, D_MODEL, D_FF)
    if name == 'mlp_w2':
        return g.reshape(N_DEV, 2, -1, D_MODEL).transpose(1, 0, 2, 3).reshape(2, D_FF, D_MODEL)
    if name in ('w_in_even', 'w_in_odd', 'dn_conv_w', 'meta_tokens'):
        return g.transpose(1, 0, 2).reshape(g.shape[1], -1)
    return g.reshape(-1, g.shape[-1])


def _to_blocks(name, full):
    if name == 'mlp_w1':
        return full.reshape(2, D_MODEL, N_DEV, -1).transpose(2, 0, 1, 3).reshape(N_DEV, 2 * D_MODEL, -1)
    if name == 'mlp_w2':
        return full.reshape(2, N_DEV, -1, D_MODEL).transpose(1, 0, 2, 3).reshape(N_DEV, -1, D_MODEL)
    if name in ('w_in_even', 'w_in_odd', 'dn_conv_w', 'meta_tokens'):
        return full.reshape(full.shape[0], N_DEV, -1).transpose(1, 0, 2)
    return full.reshape(N_DEV, -1, full.shape[-1])


def _pack(parts):
    rows = []
    for p in parts:
        flat = p.reshape(-1)
        rows.append(jnp.pad(flat, (0, (-flat.shape[0]) % 128)).reshape(-1, 128))
    return jnp.concatenate(rows, axis=0)


def _unpack(packed, like):
    out, at = [], 0
    for p in like:
        n = math.prod(p.shape)
        nrow = -(-n // 128)
        out.append(packed[at:at + nrow].reshape(-1)[:n].reshape(p.shape))
        at += nrow
    return out


def _lane_vec(x, width=128):
    flat = x.reshape(-1)
    return jnp.pad(flat, (0, width - flat.shape[0])).reshape(1, width)


def kernel(x, meta_tokens, pre_mix_norm, post_mix_norm, pre_mlp_norm, post_mlp_norm, mlp_w1, mlp_w2, w_in_even, w_out_even, sb_out_norm, s5_lambda_re, s5_lambda_im, s5_log_dt, s5_b_re, s5_b_im, s5_c_re, s5_c_im, s5_d, s5_w_glu, s5_b_glu, s5_out_norm, w_in_odd, dn_conv_w, dn_a_log, dn_dt_bias, dn_out_norm, w_out_odd, loss_target, m_meta_tokens, m_pre_mix_norm, m_post_mix_norm, m_pre_mlp_norm, m_post_mlp_norm, m_mlp_w1, m_mlp_w2, m_w_in_even, m_w_out_even, m_sb_out_norm, m_s5_lambda_re, m_s5_lambda_im, m_s5_log_dt, m_s5_b_re, m_s5_b_im, m_s5_c_re, m_s5_c_im, m_s5_d, m_s5_w_glu, m_s5_b_glu, m_s5_out_norm, m_w_in_odd, m_dn_conv_w, m_dn_a_log, m_dn_dt_bias, m_dn_out_norm, m_w_out_odd, v_meta_tokens, v_pre_mix_norm, v_post_mix_norm, v_pre_mlp_norm, v_post_mlp_norm, v_mlp_w1, v_mlp_w2, v_w_in_even, v_w_out_even, v_sb_out_norm, v_s5_lambda_re, v_s5_lambda_im, v_s5_log_dt, v_s5_b_re, v_s5_b_im, v_s5_c_re, v_s5_c_im, v_s5_d, v_s5_w_glu, v_s5_b_glu, v_s5_out_norm, v_w_in_odd, v_dn_conv_w, v_dn_a_log, v_dn_dt_bias, v_dn_out_norm, v_w_out_odd):
    given = dict(locals())
    w = {n: given[n] for n in _WEIGHTS}
    mom_m = {n: given["m_" + n] for n in _WEIGHTS}
    mom_v = {n: given["v_" + n] for n in _WEIGHTS}

    seq = x.shape[1]
    assert x.shape[0] == 1 and seq % ROW_TILE == 0
    r = seq + ROW_TILE
    pad = ROW_TILE - N_META
    pad_tiles = 1

    wire = {n: (F32 if n in ('dn_conv_w', 'meta_tokens') else BF16) for n in _SHARDED}
    shard_wire = lambda n: _view2d(n, w[n]).astype(wire[n])
    gathered = _exchange([shard_wire(n) for n in _GATHER_FIRST], [False] * len(_GATHER_FIRST), "gather_first")
    full = {n: _unshard(n, g_) for n, g_ in zip(_GATHER_FIRST, gathered)}
    w_ie, w_oe, w_glu = full['w_in_even'], full['w_out_even'], full['s5_w_glu']
    row = lambda v_: v_.reshape(1, -1)

    hs0 = jnp.concatenate([jnp.zeros((pad, D_MODEL), F32), full['meta_tokens'], x[0]], axis=0)
    hn0 = _norm_pre(hs0, row(pre_mix_norm[0]), "pre_mix_0")
    qkv = _mm_fwd(hn0, w_ie[:, :3 * SB_WIDTH], "in_even_qkv", out_dtypes=(BF16,))
    u = _mm_fwd(hn0, w_ie[:, 3 * SB_WIDTH:], "in_even_u")
    q, k, v = qkv[:, :SB_WIDTH], qkv[:, SB_WIDTH:2 * SB_WIDTH], qkv[:, 2 * SB_WIDTH:]
    nb = r // ATT_BLK
    blocks_t = lambda t_: t_.reshape(nb, ATT_BLK, 4, 128).transpose(2, 0, 3, 1)
    o_sb, ssave, gathered = _sb_fwd(q, k, blocks_t(v), pad, "sb_fwd",
                                    ride=([shard_wire(n) for n in _GATHER_LATE], [False] * len(_GATHER_LATE)))
    full.update({n: _unshard(n, g_) for n, g_ in zip(_GATHER_LATE, gathered)})
    w1, w2, w_oo, conv_w = full['mlp_w1'], full['mlp_w2'], full['w_out_odd'], full['dn_conv_w']
    w_io = full['w_in_odd'][:, :4 * DN_WIDTH]
    w_ab = jnp.pad(full['w_in_odd'][:, 4 * DN_WIDTH:], ((0, 0), (0, 128 - 2 * DN_HEADS)))
    on_sb = _norm_pre(o_sb, row(sb_out_norm[0]), "sb_out_norm")

    lam_re, lam_im, logdt, btr, bti, ctr, cti, s5_mask = _s5_expand(
        s5_lambda_re[0], s5_lambda_im[0], s5_log_dt[0], s5_b_re[0], s5_b_im[0], s5_c_re[0], s5_c_im[0])
    a_re, a_im, bbr, bbi = _s5_prep(lam_re, lam_im, logdt, btr, bti, "s5_prep")
    s5_wb = jnp.stack([_s5_block_diag_b(bbr, s5_mask), _s5_block_diag_b(bbi, s5_mask)]).astype(BF16)
    s5_wc = jnp.stack([_s5_block_diag_c(ctr, s5_mask), _s5_block_diag_c(cti, s5_mask)]).astype(BF16)
    s5_a = jnp.stack([a_re, a_im])
    s5_args = (s5_wb, s5_a, s5_wc, row(s5_d[0]), w_glu, row(s5_b_glu[0]), row(s5_out_norm[0]))
    y_s5, on_s5, xstart = _s5_fwd(u, *s5_args, "s5_fwd")

    merged = jnp.concatenate([on_sb, on_s5], axis=1)
    mix0 = _mm_fwd(merged, w_oe, "out_even")
    hs1, hn1 = _norm_post_pre(hs0, mix0, row(post_mix_norm[0]), row(pre_mlp_norm[0]), "post_mix_0")
    relu2 = lambda acc: (jnp.square(jnp.maximum(acc, 0.0)), jnp.maximum(acc, 0.0))
    r0, ra0 = _mm_fwd(hn1, w1[0], "mlp_up_0", out_dtypes=(BF16, BF16), epilogue=relu2)
    m0 = _mm_fwd(r0, w2[0], "mlp_down_0")
    hs2, hn2 = _norm_post_pre(hs1, m0, row(post_mlp_norm[0]), row(pre_mix_norm[1]), "post_mlp_0")

    proj = _mm_fwd(hn2, w_io, "in_odd")
    ab = _mm_fwd(hn2, w_ab, "in_odd_gates")
    alog, dtb = _lane_vec(dn_a_log[0]), _lane_vec(dn_dt_bias[0])
    co, qd, kd, vd, gb = _dn_pre_fwd(proj, ab, conv_w, alog, dtb, pad, "dn_pre")
    gbt = gb[:, :2 * DN_HEADS].T
    o_dn, s_dn, t_dn = _dn_fwd(qd, kd, vd, gb, gbt, "dn_fwd")
    on_dn = _dn_post_fwd(o_dn, proj, row(dn_out_norm[0]), "dn_post")
    mix1 = _mm_fwd(on_dn, w_oo, "out_odd")
    hs3, hn3 = _norm_post_pre(hs2, mix1, row(post_mix_norm[1]), row(pre_mlp_norm[1]), "post_mix_1")
    r1, ra1 = _mm_fwd(hn3, w1[1], "mlp_up_1", out_dtypes=(BF16, BF16), epilogue=relu2)
    m1 = _mm_fwd(r1, w2[1], "mlp_down_1")
    dhs, loss_part = _norm_post_loss(hs3, m1, row(post_mlp_norm[1]), loss_target[0], pad_tiles, "post_mlp_1_loss")
    loss = lax.psum(loss_part, ("x", "y", "c"))

    g = {}
    drelu2 = lambda acc, ra: (acc * (2.0 * ra.astype(F32)),)

    def mlp_bwd(layer, hn, rr, ra, dm):
        dw2 = _mm_wgrad(rr, dm, f"mlp_down_{layer}_wgrad")
        da = _mm_dgrad(dm, w2[layer], f"mlp_down_{layer}_dgrad", out_dtypes=(BF16,), extras=(ra,), epilogue=drelu2)
        dw1 = _mm_wgrad(hn, da, f"mlp_up_{layer}_wgrad")
        return dw1, dw2, _mm_dgrad(da, w1[layer], f"mlp_up_{layer}_dgrad")

    _, dm1, _, dg_post_mlp1 = _norm_bwd(dhs, post=(m1, row(post_mlp_norm[1])), pad=pad, name="post_mlp_1_bwd")
    dw1_1, dw2_1, dhn3 = mlp_bwd(1, hn3, r1, ra1, dm1)
    dhs, dmix1, dg_pre_mlp1, dg_post_mix1 = _norm_bwd(
        dhs, pre=(hs3, row(pre_mlp_norm[1]), dhn3), post=(mix1, row(post_mix_norm[1])), pad=pad, name="post_mix_1_bwd")

    g['w_out_odd'] = _mm_wgrad(on_dn, dmix1, "out_odd_wgrad")
    d_on_dn = _mm_dgrad(dmix1, w_oo, "out_odd_dgrad")
    do_dn, dz, dg_dn = _dn_post_bwd(o_dn, proj, row(dn_out_norm[0]), d_on_dn, "dn_post_bwd")
    dqd, dkd, dvd, dgb = _dn_bwd(qd, kd, vd, gb, gbt, s_dn, t_dn, do_dn, "dn_bwd")
    dco, dab, d_alog, d_dtb = _dn_pre_bwd(co, dqd, dkd, dvd, dgb, ab, alog, dtb, pad, "dn_pre_bwd")
    dpre, d_conv = _dn_conv_bwd(dco, proj, conv_w, "dn_conv_bwd")
    dproj = jnp.concatenate([dpre, dz], axis=1)
    g['w_in_odd'] = jnp.concatenate([_mm_wgrad(hn2, dproj, "in_odd_wgrad"),
                                     _mm_wgrad(hn2, dab, "in_odd_gates_wgrad")[:, :2 * DN_HEADS]], axis=1)
    dhn2 = _mm_dgrad(dab, w_ab, "in_odd_gates_dgrad")
    dhn2 = _mm_dgrad(dproj, w_io, "in_odd_dgrad", extras=(dhn2,), epilogue=lambda acc, other: (acc + other,))
    g['dn_conv_w'] = d_conv[:DN_CONV]
    g['dn_a_log'], g['dn_dt_bias'], g['dn_out_norm'] = d_alog[0, :DN_HEADS], d_dtb[0, :DN_HEADS], dg_dn[0]

    dhs, dm0, dg_pre_mix1, dg_post_mlp0 = _norm_bwd(
        dhs, pre=(hs2, row(pre_mix_norm[1]), dhn2), post=(m0, row(post_mlp_norm[0])), pad=pad, name="post_mlp_0_bwd")
    dw1_0, dw2_0, dhn1 = mlp_bwd(0, hn1, r0, ra0, dm0)
    dhs, dmix0, dg_pre_mlp0, dg_post_mix0 = _norm_bwd(
        dhs, pre=(hs1, row(pre_mlp_norm[0]), dhn1), post=(mix0, row(post_mix_norm[0])), pad=pad, name="post_mix_0_bwd")

    g['w_out_even'] = _mm_wgrad(merged, dmix0, "out_even_wgrad")
    dmerged = _mm_dgrad(dmix0, w_oe, "out_even_dgrad")
    _, do_sb, _, dg_sb = _norm_bwd(dmerged[:, :SB_WIDTH], post=(o_sb, row(sb_out_norm[0])), pad=pad, name="sb_out_norm_bwd")
    dq, dk4, dv4 = _sb_bwd(q, k, v, blocks_t(k), ssave, do_sb, pad, "sb_bwd")
    unheads = lambda t_: t_.transpose(1, 0, 2).reshape(r, SB_WIDTH)
    g['mlp_w1'] = jnp.stack([dw1_0, dw1_1])
    g['mlp_w2'] = jnp.stack([dw2_0, dw2_1])
    grad_wire = lambda n: _to_blocks(n, g[n].reshape(full[n].shape)).astype(wire[n])
    du, d_a, d_d, d_bglu, dg_s5, d_wb, d_wc, g['s5_w_glu'], reduced = _s5_bwd(
        u, y_s5, dmerged[:, SB_WIDTH:], xstart, *s5_args, "s5_bwd",
        ride=([grad_wire(n) for n in _REDUCE_EARLY], [True] * len(_REDUCE_EARLY)))
    stacks = dict(zip(_REDUCE_EARLY, reduced))
    g_lr, g_li, g_dt, g_btr, g_bti = _s5_prep_bwd(
        lam_re, lam_im, logdt, btr, bti, d_a[0], d_a[1],
        _s5_diag_of_b(d_wb[0], s5_mask), _s5_diag_of_b(d_wb[1], s5_mask), "s5_prep_bwd")
    gg, nn, pp = S5_GROUPS, S5_STATE, S5_GROUP
    g['s5_lambda_re'], g['s5_lambda_im'] = g_lr.reshape(gg, nn), g_li.reshape(gg, nn)
    g['s5_log_dt'] = g_dt.reshape(gg, nn)[:, 0]
    g['s5_b_re'], g['s5_b_im'] = g_btr.T.reshape(gg, nn, pp), g_bti.T.reshape(gg, nn, pp)
    g['s5_c_re'] = _s5_diag_of_c(d_wc[0], s5_mask).reshape(gg, nn, pp).transpose(0, 2, 1)
    g['s5_c_im'] = _s5_diag_of_c(d_wc[1], s5_mask).reshape(gg, nn, pp).transpose(0, 2, 1)
    g['s5_d'], g['s5_b_glu'], g['s5_out_norm'], g['sb_out_norm'] = d_d[0], d_bglu[0], dg_s5[0], dg_sb[0]
    dqkvu = jnp.concatenate([dq, unheads(dk4), unheads(dv4), du], axis=1)
    g['w_in_even'] = _mm_wgrad(hn0, dqkvu, "in_even_wgrad")
    dhn0 = _mm_dgrad(dqkvu, w_ie, "in_even_dgrad")
    dhs, _, dg_pre_mix0, _ = _norm_bwd(dhs, pre=(hs0, row(pre_mix_norm[0]), dhn0), pad=pad, name="pre_mix_0_bwd")

    g['meta_tokens'] = dhs[pad:pad + N_META]
    g['pre_mix_norm'] = jnp.concatenate([dg_pre_mix0, dg_pre_mix1], axis=0)
    g['post_mix_norm'] = jnp.concatenate([dg_post_mix0, dg_post_mix1], axis=0)
    g['pre_mlp_norm'] = jnp.concatenate([dg_pre_mlp0, dg_pre_mlp1], axis=0)
    g['post_mlp_norm'] = jnp.concatenate([dg_post_mlp0, dg_post_mlp1], axis=0)
    grad_x = dhs[pad + N_META:][None]

    small_like = [w[n] for n in _SMALL]
    last = [n for n in _SHARDED if n not in _REDUCE_EARLY]
    partial = [grad_wire(n) for n in last] + [_pack([g[n].reshape(w[n].shape) for n in _SMALL])]
    reduced = _exchange(partial, [True] * len(last) + [False], "reduce_last")
    stacks.update(zip(last, reduced[:-1]))
    grads, deltas, new_m, new_v = {}, {}, {}, {}
    for n in _SHARDED:
        outs = _adamw(stacks[n], _view2d(n, w[n]), _view2d(n, mom_m[n]), _view2d(n, mom_v[n]), f"adamw_{n}")
        grads[n], deltas[n], new_m[n], new_v[n] = (o.reshape(w[n].shape) for o in outs)
    outs = _adamw(reduced[-1], _pack(small_like), _pack([mom_m[n] for n in _SMALL]), _pack([mom_v[n] for n in _SMALL]),
                  "adamw_small")
    for dst, o in zip((grads, deltas, new_m, new_v), outs):
        for n, part in zip(_SMALL, _unpack(o, small_like)):
            dst[n] = part
    return (loss, grad_x, *[grads[n] for n in _WEIGHTS], *[deltas[n] for n in _WEIGHTS],
            *[new_m[n] for n in _WEIGHTS], *[new_v[n] for n in _WEIGHTS])
```

```python
import functools
import math

import jax
import jax.numpy as jnp
from jax import lax
from jax.experimental import pallas as pl
from jax.experimental.pallas import tpu as pltpu

F32 = jnp.float32
BF16 = jnp.bfloat16

D_MODEL = 1024
N_META = 16
SB_HEAD_DIM = 64
SB_WIDTH = 512
S5_WIDTH = 512
S5_GROUP = 16
S5_GROUPS = 32
S5_STATE = 64
S5_NS = S5_GROUPS * S5_STATE
DN_HEAD_DIM = 128
DN_HEADS = 8
DN_WIDTH = 1024
DN_CONV = 4
D_FF = 4096
EPS = 1e-6
N_DEV = 8

ADAM_LR = 0.001
ADAM_B1 = 0.9
ADAM_B2 = 0.999
ADAM_EPS = 1e-08
ADAM_WD = 0.01
ADAM_STEP = 10

ROW_TILE = 512
ATT_BLK = 256
SB_BLOCKS_PER_TRIP = 3
SB_LOG_ZERO = -106.0
SB_FWD_SKEW = False
SB_BWD_SKEW = True
DN_CHUNK = 128
DN_SUB = 16
S5_TILE = 128
S5_CHUNKS = 4
VMEM_LIMIT = 56 * 1024 * 1024

_HIGH = lax.Precision.HIGHEST


def _pallas(body, **kw):
    return pl.pallas_call(body, **kw)


def _cparams(sem):
    return pltpu.CompilerParams(dimension_semantics=sem, vmem_limit_bytes=VMEM_LIMIT)


def _dot(a, b, dims=((1,), (0,))):
    return lax.dot_general(a, b, (dims, ((), ())), preferred_element_type=F32)


def _dot_hi(a, b):
    return lax.dot_general(a, b, (((1,), (0,)), ((), ())), preferred_element_type=F32, precision=_HIGH)


def _split_dot(m_bf16, x):
    hi = x.astype(BF16)
    lo = (x - hi.astype(F32)).astype(BF16)
    return _dot(m_bf16, hi) + _dot(m_bf16, lo)


def _matmul(a, b, *, ta=False, tb=False, tm, tn, tk, name, out_dtypes=(F32,), extras=(), epilogue=None):
    m, k = (a.shape[1], a.shape[0]) if ta else a.shape
    n = b.shape[0] if tb else b.shape[1]
    assert (b.shape[1] if tb else b.shape[0]) == k
    assert m % tm == 0 and n % tn == 0 and k % tk == 0, (name, m, n, k, tm, tn, tk)
    nk = k // tk
    n_ex = len(extras)
    n_out = len(out_dtypes)
    dims = ((0 if ta else 1,), (1 if tb else 0,))

    def finish(acc, ex_refs, o_refs):
        outs = (acc,) if epilogue is None else epilogue(acc, *[r[...] for r in ex_refs])
        for o_ref, o in zip(o_refs, outs):
            o_ref[...] = o.astype(o_ref.dtype)

    def body(*refs):
        a_ref, b_ref = refs[0], refs[1]
        ex_refs = refs[2:2 + n_ex]
        o_refs = refs[2 + n_ex:2 + n_ex + n_out]
        prod = _dot(a_ref[...].astype(BF16), b_ref[...].astype(BF16), dims)
        if nk == 1:
            finish(prod, ex_refs, o_refs)
            return
        acc_ref = refs[-1]
        kk = pl.program_id(2)

        @pl.when(kk == 0)
        def _():
            acc_ref[...] = prod

        @pl.when(kk > 0)
        def _():
            acc_ref[...] += prod

        @pl.when(kk == nk - 1)
        def _():
            finish(acc_ref[...], ex_refs, o_refs)

    a_spec = pl.BlockSpec((tk, tm), lambda j, i, kk: (kk, i)) if ta else pl.BlockSpec((tm, tk), lambda j, i, kk: (i, kk))
    b_spec = pl.BlockSpec((tn, tk), lambda j, i, kk: (j, kk)) if tb else pl.BlockSpec((tk, tn), lambda j, i, kk: (kk, j))
    o_spec = pl.BlockSpec((tm, tn), lambda j, i, kk: (i, j))
    outs = _pallas(
        body, name=name,
        grid=(n // tn, m // tm, nk),
        in_specs=[a_spec, b_spec] + [o_spec] * n_ex,
        out_specs=[o_spec] * n_out,
        out_shape=[jax.ShapeDtypeStruct((m, n), dt) for dt in out_dtypes],
        scratch_shapes=[] if nk == 1 else [pltpu.VMEM((tm, tn), F32)],
        compiler_params=_cparams(("parallel", "parallel", "arbitrary")),
    )(a, b, *extras)
    return outs[0] if n_out == 1 else outs


def _tile(n, cap):
    best = 128
    for t in range(128, min(n, cap) + 1, 128):
        if n % t == 0:
            best = t
    assert n % best == 0, n
    return best


MM_K_CAP = 4096
WGRAD_ROWS = 1536


MM_LHS_TILE_BYTES = 6 * 1024 * 1024


def _row_tile(x, depth):
    tall = 3 * ROW_TILE
    fits = tall * depth * x.dtype.itemsize <= MM_LHS_TILE_BYTES
    return tall if (x.shape[0] % tall == 0 and fits) else ROW_TILE


def _mm_fwd(x, w, name, **kw):
    k, n = w.shape
    tk = _tile(k, MM_K_CAP)
    return _matmul(x, w, tm=_row_tile(x, tk), tn=_tile(n, 1024), tk=tk, name=name, **kw)


def _mm_dgrad(dy, w, name, **kw):
    k, n = w.shape
    tk = _tile(n, MM_K_CAP)
    return _matmul(dy, w, tb=True, tm=_row_tile(dy, tk), tn=_tile(k, 1024), tk=tk, name=name, **kw)


def _mm_wgrad(x, dy, name):
    k, n = x.shape[1], dy.shape[1]
    rows = x.shape[0]
    return _matmul(x, dy, ta=True, tm=_tile(k, 512), tn=_tile(n, 1024),
                   tk=WGRAD_ROWS if rows % WGRAD_ROWS == 0 else ROW_TILE, name=name)


def _rms(x, g):
    r = lax.rsqrt(jnp.mean(x * x, axis=-1, keepdims=True) + EPS)
    return x * r * g


def _rms_bwd(x, g, dy):
    r = lax.rsqrt(jnp.mean(x * x, axis=-1, keepdims=True) + EPS)
    xh = x * r
    dxh = dy * g
    dx = r * (dxh - xh * jnp.mean(dxh * xh, axis=-1, keepdims=True))
    dg = jnp.sum(dy * xh, axis=0, keepdims=True)
    return dx, dg


def _row_spec(width, tile=ROW_TILE):
    return pl.BlockSpec((tile, width), lambda i: (i, 0))


def _vec_spec(width):
    return pl.BlockSpec((1, width), lambda i: (0, 0))


def _norm_pre(hs, g, name):
    r, d = hs.shape

    def body(x_ref, g_ref, o_ref):
        o_ref[...] = _rms(x_ref[...], g_ref[...]).astype(BF16)

    return _pallas(body, name=name, grid=(r // ROW_TILE,), in_specs=[_row_spec(d), _vec_spec(d)],
                   out_specs=_row_spec(d), out_shape=jax.ShapeDtypeStruct((r, d), BF16),
                   compiler_params=_cparams(("parallel",)))(hs, g)


def _norm_post_pre(hs, m, g_post, g_pre, name):
    r, d = hs.shape

    def body(hs_ref, m_ref, gp_ref, gn_ref, o_ref, hn_ref):
        new = hs_ref[...] + _rms(m_ref[...], gp_ref[...])
        o_ref[...] = new
        hn_ref[...] = _rms(new, gn_ref[...]).astype(BF16)

    return _pallas(body, name=name, grid=(r // ROW_TILE,),
                   in_specs=[_row_spec(d), _row_spec(d), _vec_spec(d), _vec_spec(d)],
                   out_specs=[_row_spec(d), _row_spec(d)],
                   out_shape=[jax.ShapeDtypeStruct((r, d), F32), jax.ShapeDtypeStruct((r, d), BF16)],
                   compiler_params=_cparams(("parallel",)))(hs, m, g_post, g_pre)


def _norm_post_loss(hs, m, g_post, target, pad_tiles, name):
    r, d = hs.shape
    nt = r // ROW_TILE

    def body(hs_ref, m_ref, gp_ref, t_ref, dhs_ref, loss_ref):
        i = pl.program_id(0)
        new = hs_ref[...] + _rms(m_ref[...], gp_ref[...])
        live = (i >= pad_tiles).astype(F32)
        diff = (new - t_ref[...]) * live
        dhs_ref[...] = diff * (1.0 / d)
        loss_ref[...] = jnp.full((8, 128), 0.5 / d * jnp.sum(diff * diff), F32)

    dhs, parts = _pallas(
        body, name=name, grid=(nt,),
        in_specs=[_row_spec(d), _row_spec(d), _vec_spec(d),
                  pl.BlockSpec((ROW_TILE, d), lambda i: (jnp.maximum(i - pad_tiles, 0), 0))],
        out_specs=[_row_spec(d), pl.BlockSpec((8, 128), lambda i: (i, 0))],
        out_shape=[jax.ShapeDtypeStruct((r, d), F32), jax.ShapeDtypeStruct((nt * 8, 128), F32)],
        compiler_params=_cparams(("parallel",)))(hs, m, g_post, target)
    return dhs, jnp.sum(parts[::8, 0])


def _norm_bwd(dhs, *, pre=None, post=None, pad=0, dm_dtype=BF16, name):
    r, d = dhs.shape
    has_pre, has_post = pre is not None, post is not None

    def body(*refs):
        it = iter(refs)
        dhs_ref = next(it)
        if has_pre:
            hs_ref, gn_ref, dhn_ref = next(it), next(it), next(it)
        if has_post:
            m_ref, gp_ref = next(it), next(it)
        if has_pre:
            o_dhs, o_dgn = next(it), next(it)
        if has_post:
            o_dm, o_dgp = next(it), next(it)
        i = pl.program_id(0)
        live = (i * ROW_TILE + lax.broadcasted_iota(jnp.int32, (ROW_TILE, 1), 0)) >= pad
        cur = jnp.where(live, dhs_ref[...], 0.0)
        if has_pre:
            dx, dg = _rms_bwd(hs_ref[...], gn_ref[...], jnp.where(live, dhn_ref[...].astype(F32), 0.0))
            cur = cur + dx
            o_dhs[...] = cur

            @pl.when(i == 0)
            def _():
                o_dgn[...] = jnp.zeros_like(o_dgn)
            o_dgn[...] += dg
        if has_post:
            dm, dg = _rms_bwd(m_ref[...], gp_ref[...], cur)
            o_dm[...] = dm.astype(o_dm.dtype)

            @pl.when(i == 0)
            def _():
                o_dgp[...] = jnp.zeros_like(o_dgp)
            o_dgp[...] += dg

    ins, in_specs, out_specs, out_shape = [dhs], [_row_spec(d)], [], []
    if has_pre:
        ins += list(pre)
        in_specs += [_row_spec(d), _vec_spec(d), _row_spec(d)]
        out_specs += [_row_spec(d), _vec_spec(d)]
        out_shape += [jax.ShapeDtypeStruct((r, d), F32), jax.ShapeDtypeStruct((1, d), F32)]
    if has_post:
        ins += list(post)
        in_specs += [_row_spec(d), _vec_spec(d)]
        out_specs += [_row_spec(d), _vec_spec(d)]
        out_shape += [jax.ShapeDtypeStruct((r, d), dm_dtype), jax.ShapeDtypeStruct((1, d), F32)]
    outs = list(_pallas(body, name=name, grid=(r // ROW_TILE,), in_specs=in_specs, out_specs=out_specs,
                        out_shape=out_shape, compiler_params=_cparams(("arbitrary",)))(*ins))
    dhs_new, dgn = (outs.pop(0), outs.pop(0)) if has_pre else (dhs, None)
    dm, dgp = (outs.pop(0), outs.pop(0)) if has_post else (None, None)
    return dhs_new, dm, dgn, dgp


def _softplus(z):
    return jnp.maximum(z, 0.0) + jnp.log(1.0 + jnp.exp(-jnp.abs(z)))


def _sb_consts(t):
    row = lax.broadcasted_iota(jnp.int32, (t, t), 0)
    col = lax.broadcasted_iota(jnp.int32, (t, t), 1)
    m_up = (col >= row).astype(BF16)
    m_low = (col <= row).astype(BF16)
    return m_up, m_low


def _emit_chains(chains, stages, skew):
    if skew:
        for step in range(len(chains) + len(stages) - 1):
            for si, stage in enumerate(stages):
                if 0 <= step - si < len(chains):
                    stage(chains[step - si])
    else:
        for stage in stages:
            for c in chains:
                stage(c)


def _sb_fwd(q, k, vt3, pad, name, ride=((), ())):
    r = q.shape[0]
    t = ATT_BLK
    nb = r // t
    nbp = -(-(nb + 1) // 8) * 8
    jmin = pad // t
    scale = SB_HEAD_DIM ** -0.5
    n_ride = len(ride[0])

    def body(q_ref, k_ref, vt_ref, *rest):
        ride_in, (o_ref, ss_ref), ride_out = rest[:n_ride], rest[n_ride:n_ride + 2], rest[n_ride + 2:2 * n_ride + 2]
        acc_ref, kn_ref = rest[2 * n_ride + 2:2 * n_ride + 4]
        ride_sems = rest[2 * n_ride + 4:]
        i = pl.program_id(1)
        if n_ride:
            @pl.when((pl.program_id(0) == 0) & (i == 0))
            def _():
                for cp in _exchange_copies(ride_in, ride_out, ride[1], *ride_sems):
                    cp.start()

        @pl.when(i == 0)
        def _():
            def blk(b, m):
                kb = k_ref[pl.ds(pl.multiple_of(b * t, t), t), :].astype(F32)
                return jnp.maximum(m, jnp.max(jnp.sum(kb * kb, axis=1, keepdims=True), axis=0, keepdims=True))
            kn_ref[...] = jnp.broadcast_to(lax.fori_loop(0, nb, blk, jnp.zeros((1, 1), F32)), (8, 128))

        qf = q_ref[...].astype(F32)
        z_bound = scale * jnp.sqrt(jnp.max(jnp.sum(qf * qf, axis=1, keepdims=True)) * jnp.max(kn_ref[...]))

        def need(carry):
            return jnp.maximum(jnp.max(carry[0]), jnp.max(carry[1])) + z_bound >= SB_LOG_ZERO

        qt = qf.T
        sub = lax.broadcasted_iota(jnp.int32, (128, 1), 0)
        m_up, _ = _sb_consts(t)
        kpos0 = lax.broadcasted_iota(jnp.int32, (t, 1), 0)
        qpos = i * t + lax.broadcasted_iota(jnp.int32, (1, t), 1)
        n_mid = jnp.maximum(i - 1 - jmin, 0)
        n_edge = jnp.where(i > jmin, 1, 0)
        qths = [jnp.where((sub >= 64 * h) & (sub < 64 * (h + 1)), qt * scale, 0.0).astype(BF16) for h in range(2)]
        acc_ref[...] = jnp.zeros_like(acc_ref)

        def sweep(js, carry, masked):
            kbs = [k_ref[pl.ds(pl.multiple_of(j * t, t), t), :] for j in js]
            vts = [vt_ref[0, j] for j in js]
            accs = [acc_ref[0], acc_ref[1]]
            s = list(carry)
            chains = [(n, h) for n in range(len(js)) for h in range(2)]
            masked = [masked] * len(js) if isinstance(masked, bool) else masked
            valid = [(js[n] * t + kpos0 < qpos) & (js[n] * t + kpos0 >= pad) if masked[n] else None for n in range(len(js))]
            zt, inc, saves = {}, {}, []

            def st_scores(c):
                zt[c] = _dot(kbs[c[0]], qths[c[1]])

            def st_cumsum(c):
                lk = -_softplus(zt[c])
                if masked[c[0]]:
                    lk = jnp.where(valid[c[0]], lk, 0.0)
                inc[c] = _split_dot(m_up, lk)

            def st_weights(c):
                n, h = c
                saves.append((h, js[n], s[h]))
                w = jnp.exp(zt[c] + inc[c] + s[h])
                if masked[n]:
                    w = jnp.where(valid[n], w, 0.0)
                accs[h] = accs[h] + _dot(vts[n], w.astype(BF16))
                s[h] = s[h] + inc[c][0:1, :]

            _emit_chains(chains, [st_scores, st_cumsum, st_weights], SB_FWD_SKEW)
            for h, j, val in saves:
                ss_ref[h, 0, pl.ds(j, 1), :] = val
            acc_ref[0] = accs[0]
            acc_ref[1] = accs[1]
            return tuple(s)

        zero = jnp.zeros((1, t), F32)
        bpi = SB_BLOCKS_PER_TRIP
        j, carry = lax.cond(
            i - 1 > jmin,
            lambda: (i - 2, sweep([i, i - 1], (zero, zero), [True, False])),
            lambda: (i - 1, sweep([i], (zero, zero), True)))
        j, carry = lax.while_loop(
            lambda st: (st[0] - bpi >= jmin) & need(st[1]),
            lambda st: (st[0] - bpi, sweep([st[0] - b for b in range(bpi)], st[1], False)), (j, carry))
        j, carry = lax.while_loop(
            lambda st: (st[0] > jmin) & need(st[1]),
            lambda st: (st[0] - 1, sweep([st[0]], st[1], False)), (j, carry))
        j, carry = lax.while_loop(
            lambda st: (st[0] == jmin) & (i > jmin) & need(st[1]),
            lambda st: (st[0] - 1, sweep([st[0]], st[1], True)), (j, carry))
        first = jnp.full((1, t), j + 1, jnp.int32).astype(F32)
        ss_ref[0, 0, nbp - 1:nbp, :] = first
        ss_ref[1, 0, nbp - 1:nbp, :] = first
        acc = jnp.where(sub < 64, acc_ref[0], acc_ref[1])
        o_ref[...] = acc.T
        if n_ride:
            @pl.when((pl.program_id(0) == 3) & (i == nb - 1))
            def _():
                for cp in _exchange_copies(ride_in, ride_out, ride[1], *ride_sems):
                    cp.wait()

    hbm = pl.BlockSpec(memory_space=pl.ANY)
    outs = _pallas(
        body, name=name, grid=(4, nb),
        in_specs=[pl.BlockSpec((t, 128), lambda hp, i: (i, hp)),
                  pl.BlockSpec((r, 128), lambda hp, i: (0, hp)),
                  pl.BlockSpec((1, nb, 128, t), lambda hp, i: (hp, 0, 0, 0))] + [hbm] * n_ride,
        out_specs=[pl.BlockSpec((t, 128), lambda hp, i: (i, hp)),
                   pl.BlockSpec((2, 1, nbp, t), lambda hp, i: (hp, i, 0, 0))] + [hbm] * n_ride,
        out_shape=[jax.ShapeDtypeStruct((r, SB_WIDTH), F32),
                   jax.ShapeDtypeStruct((8, nb, nbp, t), F32)] + _exchange_shapes(*ride),
        scratch_shapes=[pltpu.VMEM((2, 128, t), F32), pltpu.VMEM((8, 128), F32)] + (_exchange_sems(n_ride) if n_ride else []),
        compiler_params=_cparams(("arbitrary", "arbitrary")),
    )(q, k, vt3, *ride[0])
    return outs[0], outs[1], list(outs[2:])


def _sb_bwd(q, k, v, kt3, ssave, do, pad, name):
    r = q.shape[0]
    t = ATT_BLK
    nb = r // t
    nbp = ssave.shape[2]
    jmin = pad // t
    scale = SB_HEAD_DIM ** -0.5

    def body(q_ref, do_ref, k_ref, v_ref, kt_ref, ss_ref, dq_ref, dk_hbm, dv_hbm, dk_acc, dv_acc, dq_acc, sem):
        hp = pl.program_id(0)
        i = pl.program_id(1)

        @pl.when(i == 0)
        def _():
            dk_acc[...] = jnp.zeros_like(dk_acc)
            dv_acc[...] = jnp.zeros_like(dv_acc)

        qf = q_ref[...].astype(F32)
        dof = do_ref[...]
        qt = qf.T
        dot_ = dof.T
        sub = lax.broadcasted_iota(jnp.int32, (128, 1), 0)
        lane = lax.broadcasted_iota(jnp.int32, (1, 128), 1)
        m_up, m_low = _sb_consts(t)
        kpos0 = lax.broadcasted_iota(jnp.int32, (t, 1), 0)
        qpos = i * t + lax.broadcasted_iota(jnp.int32, (1, t), 1)
        first = jnp.clip(jnp.max(ss_ref[0, 0, nbp - 1:nbp, :]).astype(jnp.int32), jmin, i)
        mid0 = jnp.maximum(first, jmin + 1)
        pair = i - mid0 >= 1
        n_mid = jnp.maximum(i - mid0 - 1, 0)
        n_edge = jnp.where((i > jmin) & (first == jmin), 1, 0)
        in_t = [(sub >= 64 * h) & (sub < 64 * (h + 1)) for h in range(2)]
        in_l = [(lane >= 64 * h) & (lane < 64 * (h + 1)) for h in range(2)]
        qths = [jnp.where(in_t[h], qt * scale, 0.0).astype(BF16) for h in range(2)]
        doths = [jnp.where(in_t[h], dot_, 0.0).astype(BF16) for h in range(2)]
        qhs = [jnp.where(in_l[h], qf * scale, 0.0).astype(BF16) for h in range(2)]
        dohs = [jnp.where(in_l[h], dof, 0.0).astype(BF16) for h in range(2)]
        dq_acc[...] = jnp.zeros_like(dq_acc)

        def sweep(js, carry, masked):
            rows = [pl.ds(pl.multiple_of(j * t, t), t) for j in js]
            kbs = [k_ref[rw, :] for rw in rows]
            vbs = [v_ref[rw, :] for rw in rows]
            kts = [kt_ref[0, j] for j in js]
            sss = [[ss_ref[h, 0, pl.ds(j, 1), :] for h in range(2)] for j in js]
            dv_old = [dv_acc[rw, :] for rw in rows]
            dk_old = [dk_acc[rw, :] for rw in rows]
            dqs = [dq_acc[0], dq_acc[1]]
            ec = list(carry)
            chains = [(n, h) for n in range(len(js)) for h in range(2)]
            masked = [masked] * len(js) if isinstance(masked, bool) else masked
            valid = [(js[n] * t + kpos0 < qpos) & (js[n] * t + kpos0 >= pad) if masked[n] else None for n in range(len(js))]
            zt, dvt, sp, inc, e, big_e = {}, {}, {}, {}, {}, {}

            def st_scores(c):
                zt[c] = _dot(kbs[c[0]], qths[c[1]])
                dvt[c] = _dot(vbs[c[0]], doths[c[1]])

            def st_cumsum(c):
                sp[c] = _softplus(zt[c])
                lk = -sp[c]
                if masked[c[0]]:
                    lk = jnp.where(valid[c[0]], lk, 0.0)
                inc[c] = _split_dot(m_up, lk)

            def st_weights(c):
                n, h = c
                w = jnp.exp(zt[c] + inc[c] + sss[n][h])
                if masked[n]:
                    w = jnp.where(valid[n], w, 0.0)
                dv_old[n] = dv_old[n] + _dot(w.astype(BF16), dohs[h])
                e[c] = w * dvt[c]
                pinc = _split_dot(m_low, e[c])
                big_e[c] = pinc - e[c] + ec[h]
                ec[h] = ec[h] + pinc[t - 1:t, :]

            def st_dscores(c):
                n, h = c
                dz = e[c] - jnp.exp(zt[c] - sp[c]) * (e[c] + big_e[c])
                if masked[n]:
                    dz = jnp.where(valid[n], dz, 0.0)
                dzb = dz.astype(BF16)
                dqs[h] = dqs[h] + _dot(kts[n], dzb)
                dk_old[n] = dk_old[n] + _dot(dzb, qhs[h])

            _emit_chains(chains, [st_scores, st_cumsum, st_weights, st_dscores], SB_BWD_SKEW)
            for n, rw in enumerate(rows):
                dv_acc[rw, :] = dv_old[n]
                dk_acc[rw, :] = dk_old[n]
            dq_acc[0] = dqs[0]
            dq_acc[1] = dqs[1]
            return tuple(ec)

        zero = jnp.zeros((1, t), F32)
        bpi = SB_BLOCKS_PER_TRIP
        carry = lax.fori_loop(0, n_edge, lambda it, c: sweep([jmin + it * 0], c, True), (zero, zero))
        carry = lax.fori_loop(0, n_mid // bpi, lambda it, c: sweep([mid0 + bpi * it + b for b in range(bpi)], c, False), carry)
        n_rem = n_mid % bpi
        carry = lax.fori_loop(0, n_rem, lambda it, c: sweep([i - 1 - n_rem + it], c, False), carry)
        lax.cond(pair, lambda: sweep([i - 1, i], carry, [False, True]), lambda: sweep([i], carry, True))
        dq_ref[...] = (jnp.where(sub < 64, dq_acc[0], dq_acc[1]) * scale).T

        @pl.when(i == nb - 1)
        def _():
            c1 = pltpu.make_async_copy(dk_acc, dk_hbm.at[hp], sem.at[0])
            c2 = pltpu.make_async_copy(dv_acc, dv_hbm.at[hp], sem.at[1])
            c1.start()
            c2.start()
            c1.wait()
            c2.wait()

    return _pallas(
        body, name=name, grid=(4, nb),
        in_specs=[pl.BlockSpec((t, 128), lambda hp, i: (i, hp)),
                  pl.BlockSpec((t, 128), lambda hp, i: (i, hp)),
                  pl.BlockSpec((r, 128), lambda hp, i: (0, hp)),
                  pl.BlockSpec((r, 128), lambda hp, i: (0, hp)),
                  pl.BlockSpec((1, nb, 128, t), lambda hp, i: (hp, 0, 0, 0)),
                  pl.BlockSpec((2, 1, nbp, t), lambda hp, i: (hp, i, 0, 0))],
        out_specs=[pl.BlockSpec((t, 128), lambda hp, i: (i, hp)),
                   pl.BlockSpec(memory_space=pl.ANY), pl.BlockSpec(memory_space=pl.ANY)],
        out_shape=[jax.ShapeDtypeStruct((r, SB_WIDTH), F32),
                   jax.ShapeDtypeStruct((4, r, 128), F32), jax.ShapeDtypeStruct((4, r, 128), F32)],
        scratch_shapes=[pltpu.VMEM((r, 128), F32), pltpu.VMEM((r, 128), F32), pltpu.VMEM((2, 128, t), F32),
                        pltpu.SemaphoreType.DMA((2,))],
        compiler_params=_cparams(("arbitrary", "arbitrary")),
    )(q, do, k, v, kt3, ssave)


def _s5_disc(lam_re, lam_im, logdt, btr, bti):
    lr = jnp.minimum(lam_re, -1e-4)
    li = lam_im
    dt = jnp.exp(logdt)
    mag = jnp.exp(lr * dt)
    ang = li * dt
    a_re, a_im = mag * jnp.cos(ang), mag * jnp.sin(ang)
    den = lr * lr + li * li
    nr, ni = a_re - 1.0, a_im
    c_re = (nr * lr + ni * li) / den
    c_im = (ni * lr - nr * li) / den
    return a_re, a_im, c_re * btr - c_im * bti, c_re * bti + c_im * btr


def _s5_prep(lam_re, lam_im, logdt, btr, bti, name):
    ns = lam_re.shape[1]

    def body(lr_ref, li_ref, dt_ref, br_ref, bi_ref, ar_ref, ai_ref, bbr_ref, bbi_ref):
        ar, ai, bbr, bbi = _s5_disc(lr_ref[...], li_ref[...], dt_ref[...], br_ref[...], bi_ref[...])
        ar_ref[...] = ar
        ai_ref[...] = ai
        bbr_ref[...] = bbr
        bbi_ref[...] = bbi

    return _pallas(body, name=name,
                   out_shape=[jax.ShapeDtypeStruct((1, ns), F32)] * 2 + [jax.ShapeDtypeStruct((S5_GROUP, ns), F32)] * 2,
                   )(lam_re, lam_im, logdt, btr, bti)


def _s5_prep_bwd(lam_re, lam_im, logdt, btr, bti, dar, dai, dbbr, dbbi, name):
    ns = lam_re.shape[1]

    def body(lr_ref, li_ref, dt_ref, br_ref, bi_ref, dar_ref, dai_ref, dbr_ref, dbi_ref, o_lr, o_li, o_dt, o_br, o_bi):
        _, vjp = jax.vjp(_s5_disc, lr_ref[...], li_ref[...], dt_ref[...], br_ref[...], bi_ref[...])
        g = vjp((dar_ref[...], dai_ref[...], dbr_ref[...], dbi_ref[...]))
        o_lr[...] = g[0]
        o_li[...] = g[1]
        row = lax.broadcasted_iota(jnp.int32, (ns, ns), 0) // S5_STATE
        col = lax.broadcasted_iota(jnp.int32, (ns, ns), 1) // S5_STATE
        same = (row == col).astype(F32)
        o_dt[...] = _dot_hi(jnp.broadcast_to(g[2], (8, ns)), same)[0:1]
        o_br[...] = g[3]
        o_bi[...] = g[4]

    return _pallas(body, name=name,
                   out_shape=[jax.ShapeDtypeStruct((1, ns), F32)] * 3 + [jax.ShapeDtypeStruct((S5_GROUP, ns), F32)] * 2,
                   compiler_params=pltpu.CompilerParams(vmem_limit_bytes=VMEM_LIMIT),
                   )(lam_re, lam_im, logdt, btr, bti, dar, dai, dbbr, dbbi)


def _s5_scan(br, bi, ar, ai, t, reverse=False, carry=None):
    ng = t // 8
    ns = br.shape[1]
    br, bi = br.reshape(ng, 8, ns), bi.reshape(ng, 8, ns)
    row8 = lax.broadcasted_iota(jnp.int32, (1, 8, 1), 1)
    pr, pi_ = ar, ai
    for k in (1, 2, 4):
        if reverse:
            sr, si, ok = pltpu.roll(br, 8 - k, 1), pltpu.roll(bi, 8 - k, 1), row8 < 8 - k
        else:
            sr, si, ok = pltpu.roll(br, k, 1), pltpu.roll(bi, k, 1), row8 >= k
        sr = jnp.where(ok, sr, 0.0)
        si = jnp.where(ok, si, 0.0)
        br, bi = br + pr * sr - pi_ * si, bi + pr * si + pi_ * sr
        pr, pi_ = pr * pr - pi_ * pi_, 2.0 * pr * pi_
    pw_r, pw_i = [ar], [ai]
    for _ in range(7):
        pw_r.append(pw_r[-1] * ar - pw_i[-1] * ai)
        pw_i.append(pw_r[-2] * ai + pw_i[-1] * ar)
    if reverse:
        pw_r.reverse()
        pw_i.reverse()
    p8r, p8i = jnp.concatenate(pw_r, axis=0), jnp.concatenate(pw_i, axis=0)
    out_r, out_i = [None] * ng, [None] * ng
    order = range(ng - 1, -1, -1) if reverse else range(ng)
    edge = 0 if reverse else 7
    for g in order:
        gr, gi = br[g], bi[g]
        if carry is not None:
            cr, ci = carry
            gr, gi = gr + p8r * cr - p8i * ci, gi + p8r * ci + p8i * cr
        out_r[g], out_i[g] = gr, gi
        carry = (gr[edge:edge + 1], gi[edge:edge + 1])
    return jnp.concatenate(out_r, axis=0), jnp.concatenate(out_i, axis=0)


def _s5_prev_rows(x, first, t):
    ng = t // 8
    ns = x.shape[1]
    x3 = x.reshape(ng, 8, ns)
    last = x3[:, 7:8, :]
    before = jnp.concatenate([first.reshape(1, 1, ns), last[:ng - 1]], axis=0)
    row8 = lax.broadcasted_iota(jnp.int32, (1, 8, 1), 1)
    return jnp.where(row8 == 0, before, pltpu.roll(x3, 1, 1)).reshape(t, ns)


_GELU_C = math.sqrt(2.0 / math.pi)


def _gelu(y):
    th = jnp.tanh(_GELU_C * (y + 0.044715 * y * y * y))
    return 0.5 * y * (1.0 + th), th


def _sigmoid(x):
    return 1.0 / (1.0 + jnp.exp(-x))


def _s5_fwd(u, wb, a, wc, dskip, wglu, bglu, gnorm, name):
    r = u.shape[0]
    t = S5_TILE
    nt = r // t
    ns = wb.shape[2]
    w = S5_WIDTH

    def body(u_ref, wb_ref, a_ref, wc_ref, d_ref, wg_ref, bg_ref, gn_ref, y_ref, on_ref, xs_ref, carry_ref):
        i = pl.program_id(0)
        ar, ai = a_ref[0], a_ref[1]

        @pl.when(i == 0)
        def _():
            carry_ref[...] = jnp.zeros_like(carry_ref)

        u_ = u_ref[...]
        ub = u_.astype(BF16)
        xs_ref[0] = carry_ref[:, 0, :]
        chunks = list(range(S5_CHUNKS))
        sl_s = [slice(c * (ns // S5_CHUNKS), (c + 1) * (ns // S5_CHUNKS)) for c in chunks]
        sl_u = [slice(c * (w // S5_CHUNKS), (c + 1) * (w // S5_CHUNKS)) for c in chunks]
        bu, xs, ys = {}, {}, {}

        def st_inputs(c):
            bu[c] = (_dot(ub[:, sl_u[c]], wb_ref[0, sl_u[c], sl_s[c]]), _dot(ub[:, sl_u[c]], wb_ref[1, sl_u[c], sl_s[c]]))

        def st_scan(c):
            xr, xi = _s5_scan(*bu[c], ar[:, sl_s[c]], ai[:, sl_s[c]], t, carry=(carry_ref[0, :, sl_s[c]], carry_ref[1, :, sl_s[c]]))
            carry_ref[0, :, sl_s[c]] = xr[t - 1:t, :]
            carry_ref[1, :, sl_s[c]] = xi[t - 1:t, :]
            xs[c] = (xr.astype(BF16), xi.astype(BF16))

        def st_outputs(c):
            ys[c] = _dot(xs[c][0], wc_ref[0, sl_s[c], sl_u[c]]) - _dot(xs[c][1], wc_ref[1, sl_s[c], sl_u[c]])

        _emit_chains(chunks, [st_inputs, st_scan, st_outputs], False)
        y = jnp.concatenate([ys[c] for c in chunks], axis=1) + d_ref[...] * u_
        h, _ = _gelu(y)
        gate = _sigmoid(_dot(h.astype(BF16), wg_ref[...]) + bg_ref[...])
        y_ref[...] = y
        on_ref[...] = _rms(h * gate, gn_ref[...]).astype(BF16)

    full = lambda shape: pl.BlockSpec(shape, lambda i: (0,) * len(shape))
    return _pallas(
        body, name=name, grid=(nt,),
        in_specs=[_row_spec(w, t), full((2, w, ns)), full((2, 1, ns)), full((2, ns, w)), full((1, w)),
                  full((w, w)), full((1, w)), full((1, w))],
        out_specs=[_row_spec(w, t), _row_spec(w, t), pl.BlockSpec((1, 2, ns), lambda i: (i, 0, 0))],
        out_shape=[jax.ShapeDtypeStruct((r, w), F32), jax.ShapeDtypeStruct((r, w), BF16),
                   jax.ShapeDtypeStruct((nt, 2, ns), F32)],
        scratch_shapes=[pltpu.VMEM((2, 1, ns), F32)],
        compiler_params=_cparams(("arbitrary",)),
    )(u, wb, a, wc, dskip, wglu, bglu, gnorm)


def _s5_bwd(u, y, don, xstart, wb, a, wc, dskip, wglu, bglu, gnorm, name, ride=((), ())):
    r = u.shape[0]
    t = S5_TILE
    nt = r // t
    ns = wb.shape[2]
    w = S5_WIDTH
    nt_dims = ((1,), (1,))
    tn_dims = ((0,), (0,))

    def body(u_ref, y_ref, don_ref, xs_ref, wb_hbm, a_ref, wc_hbm, d_ref, wg_ref, bg_ref, gn_ref,
             du_ref, da_ref, dd_ref, dbg_ref, dgn_ref, dwb_hbm, dwc_hbm, dwg_hbm,
             wb_ref, wc_ref, lam_ref, acc_wb, acc_wc, acc_wg, sem):
        i = pl.program_id(0)
        ar, ai = a_ref[0], a_ref[1]

        @pl.when(i == 0)
        def _():
            c1 = pltpu.make_async_copy(wb_hbm, wb_ref, sem.at[0])
            c2 = pltpu.make_async_copy(wc_hbm, wc_ref, sem.at[1])
            c1.start()
            c2.start()
            lam_ref[...] = jnp.zeros_like(lam_ref)
            acc_wb[...] = jnp.zeros_like(acc_wb)
            acc_wc[...] = jnp.zeros_like(acc_wc)
            acc_wg[...] = jnp.zeros_like(acc_wg)
            da_ref[...] = jnp.zeros_like(da_ref)
            dd_ref[...] = jnp.zeros_like(dd_ref)
            dbg_ref[...] = jnp.zeros_like(dbg_ref)
            dgn_ref[...] = jnp.zeros_like(dgn_ref)
            c1.wait()
            c2.wait()

        u_ = u_ref[...]
        y_ = y_ref[...]
        ub = u_.astype(BF16)
        h, th = _gelu(y_)
        hb = h.astype(BF16)
        wg = wg_ref[...]
        gate = _sigmoid(_dot(hb, wg) + bg_ref[...])
        d_out, dgn = _rms_bwd(h * gate, gn_ref[...], don_ref[...])
        dgn_ref[...] += dgn
        dhw = d_out * h * gate * (1.0 - gate)
        dhwb = dhw.astype(BF16)
        dh = d_out * gate + _dot(dhwb, wg, nt_dims)
        acc_wg[...] += _dot(hb, dhwb, tn_dims)
        dbg_ref[...] += jnp.sum(dhw, axis=0, keepdims=True)
        dgelu = 0.5 * (1.0 + th) + 0.5 * y_ * (1.0 - th * th) * _GELU_C * (1.0 + 3.0 * 0.044715 * y_ * y_)
        dy = dh * dgelu
        dd_ref[...] += jnp.sum(dy * u_, axis=0, keepdims=True)
        dyb = dy.astype(BF16)
        chunks = list(range(S5_CHUNKS))
        sl_s = [slice(c * (ns // S5_CHUNKS), (c + 1) * (ns // S5_CHUNKS)) for c in chunks]
        sl_u = [slice(c * (w // S5_CHUNKS), (c + 1) * (w // S5_CHUNKS)) for c in chunks]
        bu, gx, x_, lam, dus = {}, {}, {}, {}, {}

        def st_inputs(c):
            su, ss = sl_u[c], sl_s[c]
            bu[c] = (_dot(ub[:, su], wb_ref[0, su, ss]), _dot(ub[:, su], wb_ref[1, su, ss]))
            gx[c] = (_dot(dyb[:, su], wc_ref[0, ss, su], nt_dims), -_dot(dyb[:, su], wc_ref[1, ss, su], nt_dims))

        def st_states(c):
            su, ss = sl_u[c], sl_s[c]
            first = (xs_ref[0, 0:1, ss], xs_ref[0, 1:2, ss])
            xr, xi = _s5_scan(*bu[c], ar[:, ss], ai[:, ss], t, carry=first)
            acc_wc[0, ss, su] += _dot(xr.astype(BF16), dyb[:, su], tn_dims)
            acc_wc[1, ss, su] -= _dot(xi.astype(BF16), dyb[:, su], tn_dims)
            x_[c] = (_s5_prev_rows(xr, first[0], t), _s5_prev_rows(xi, first[1], t))

        def st_adjoint(c):
            su, ss = sl_u[c], sl_s[c]
            lr, li = _s5_scan(*gx[c], ar[:, ss], -ai[:, ss], t, reverse=True, carry=(lam_ref[0, :, ss], lam_ref[1, :, ss]))
            lam_ref[0, :, ss] = lr[0:1, :]
            lam_ref[1, :, ss] = li[0:1, :]
            lrb, lib = lr.astype(BF16), li.astype(BF16)
            acc_wb[0, su, ss] += _dot(ub[:, su], lrb, tn_dims)
            acc_wb[1, su, ss] += _dot(ub[:, su], lib, tn_dims)
            dus[c] = _dot(lrb, wb_ref[0, su, ss], nt_dims) + _dot(lib, wb_ref[1, su, ss], nt_dims)
            lam[c] = (lr, li)

        def st_decay(c):
            ss = sl_s[c]
            (lr, li), (xpr, xpi) = lam[c], x_[c]
            da_ref[0, :, ss] += jnp.sum(lr * xpr + li * xpi, axis=0, keepdims=True)
            da_ref[1, :, ss] += jnp.sum(li * xpr - lr * xpi, axis=0, keepdims=True)

        _emit_chains(chunks, [st_inputs, st_states, st_adjoint, st_decay], False)
        du_ref[...] = d_ref[...] * dy + jnp.concatenate([dus[c] for c in chunks], axis=1)

        @pl.when(i == nt - 1)
        def _():
            cps = [pltpu.make_async_copy(acc_wb, dwb_hbm, sem.at[0]), pltpu.make_async_copy(acc_wc, dwc_hbm, sem.at[1]),
                   pltpu.make_async_copy(acc_wg, dwg_hbm, sem.at[2])]
            for c in cps:
                c.start()
            for c in cps:
                c.wait()

    n_ride = len(ride[0])
    n_in, n_out, n_scratch = 11, 8, 7

    def body_with_ride(*refs):
        ins, rest = refs[:n_in], refs[n_in:]
        ride_in, rest = rest[:n_ride], rest[n_ride:]
        outs, rest = rest[:n_out], rest[n_out:]
        ride_out, rest = rest[:n_ride], rest[n_ride:]
        scratch, ride_sems = rest[:n_scratch], rest[n_scratch:]
        if n_ride:
            @pl.when(pl.program_id(0) == 0)
            def _():
                for cp in _exchange_copies(ride_in, ride_out, ride[1], *ride_sems):
                    cp.start()
        body(*ins, *outs, *scratch)
        if n_ride:
            @pl.when(pl.program_id(0) == nt - 1)
            def _():
                for cp in _exchange_copies(ride_in, ride_out, ride[1], *ride_sems):
                    cp.wait()

    rev = lambda i: (nt - 1 - i, 0)
    full = lambda shape: pl.BlockSpec(shape, lambda i: (0,) * len(shape))
    hbm = pl.BlockSpec(memory_space=pl.ANY)
    outs = _pallas(
        body_with_ride, name=name, grid=(nt,),
        in_specs=[pl.BlockSpec((t, w), rev), pl.BlockSpec((t, w), rev), pl.BlockSpec((t, w), rev),
                  pl.BlockSpec((1, 2, ns), lambda i: (nt - 1 - i, 0, 0)), hbm, full((2, 1, ns)), hbm, full((1, w)),
                  full((w, w)), full((1, w)), full((1, w))] + [hbm] * n_ride,
        out_specs=[pl.BlockSpec((t, w), rev), full((2, 1, ns)), full((1, w)), full((1, w)), full((1, w)), hbm, hbm, hbm]
        + [hbm] * n_ride,
        out_shape=[jax.ShapeDtypeStruct((r, w), F32), jax.ShapeDtypeStruct((2, 1, ns), F32)]
        + [jax.ShapeDtypeStruct((1, w), F32)] * 3
        + [jax.ShapeDtypeStruct((2, w, ns), F32), jax.ShapeDtypeStruct((2, ns, w), F32), jax.ShapeDtypeStruct((w, w), F32)]
        + _exchange_shapes(*ride),
        scratch_shapes=[pltpu.VMEM((2, w, ns), BF16), pltpu.VMEM((2, ns, w), BF16), pltpu.VMEM((2, 1, ns), F32),
                        pltpu.VMEM((2, w, ns), F32), pltpu.VMEM((2, ns, w), F32), pltpu.VMEM((w, w), F32),
                        pltpu.SemaphoreType.DMA((3,))] + (_exchange_sems(n_ride) if n_ride else []),
        compiler_params=_cparams(("arbitrary",)),
    )(u, y, don, xstart, wb, a, wc, dskip, wglu, bglu, gnorm, *ride[0])
    return tuple(outs[:n_out]) + (list(outs[n_out:]),)


def _s5_expand(lam_re, lam_im, log_dt, b_re, b_im, c_re, c_im):
    g, n, p = S5_GROUPS, S5_STATE, S5_GROUP
    ns = g * n
    rows = lambda x: x.reshape(1, ns)
    logdt = jnp.repeat(log_dt.reshape(g), n).reshape(1, ns)
    btr = b_re.reshape(ns, p).T
    bti = b_im.reshape(ns, p).T
    ctr = c_re.transpose(0, 2, 1).reshape(ns, p)
    cti = c_im.transpose(0, 2, 1).reshape(ns, p)
    mask = (jnp.arange(g * p)[:, None] // p) == (jnp.arange(ns)[None, :] // n)
    return rows(lam_re), rows(lam_im), logdt, btr, bti, ctr, cti, mask


def _s5_block_diag_b(bb, mask):
    return jnp.where(mask, jnp.tile(bb, (S5_GROUPS, 1)), 0.0)


def _s5_block_diag_c(ct, mask):
    return jnp.where(mask.T, jnp.tile(ct, (1, S5_GROUPS)), 0.0)


def _s5_diag_of_b(dwb, mask):
    return jnp.where(mask, dwb, 0.0).reshape(S5_GROUPS, S5_GROUP, -1).sum(0)


def _s5_diag_of_c(dwc, mask):
    ns = dwc.shape[0]
    return jnp.where(mask.T, dwc, 0.0).reshape(ns, S5_GROUPS, S5_GROUP).sum(1)


DN_PRE_TILE = 256
_DN_QKV = 3 * DN_WIDTH


def _halo_specs(width, tile, nt, prev):
    per = tile // 8
    if prev:
        return pl.BlockSpec((8, width), lambda i: (jnp.maximum(i * per - 1, 0), 0))
    return pl.BlockSpec((8, width), lambda i: (jnp.minimum((i + 1) * per, nt * per - 1), 0))


def _shift_down(x, halo, s, t):
    xx = jnp.concatenate([halo, x], axis=0)
    return pltpu.roll(xx, s, 0)[8:]


def _shift_up(x, halo, s, t):
    xx = jnp.concatenate([x, halo], axis=0)
    return pltpu.roll(xx, t + 8 - s, 0)[:t]


def _silu(x):
    s = _sigmoid(x)
    return x * s, s


def _dn_gates(ab, alog, dtb, live):
    lane = lax.broadcasted_iota(jnp.int32, (1, 128), 1)
    g = -jnp.exp(alog) * _softplus(ab + dtb)
    beta = _sigmoid(ab)
    return jnp.where(live & (lane < DN_HEADS), g, jnp.where(live & (lane < 2 * DN_HEADS), beta, 0.0))


def _dn_pre_fwd(proj, ab, conv_w, alog, dtb, pad, name):
    r = proj.shape[0]
    t = DN_PRE_TILE
    nt = r // t
    scale = DN_HEAD_DIM ** -0.5

    def body(x_ref, halo_ref, ab_ref, w_ref, al_ref, dt_ref, co_ref, q_ref, k_ref, v_ref, gb_ref):
        i = pl.program_id(0)
        x = x_ref[...]
        halo = jnp.where(i > 0, halo_ref[...], 0.0)
        w = w_ref[...]
        co = w[3:4] * x
        for tap in range(DN_CONV - 1):
            co = co + w[tap:tap + 1] * _shift_down(x, halo, DN_CONV - 1 - tap, t)
        co_ref[...] = co
        act, _ = _silu(co)
        for hd in range(DN_HEADS):
            sl = slice(hd * 128, (hd + 1) * 128)
            for base, o_ref, sc in ((0, q_ref, scale), (DN_WIDTH, k_ref, 1.0)):
                xh = act[:, base + hd * 128: base + (hd + 1) * 128]
                o_ref[:, sl] = xh * (lax.rsqrt(jnp.sum(xh * xh, axis=-1, keepdims=True) + EPS) * sc)
        v_ref[...] = act[:, 2 * DN_WIDTH:]
        rows = i * t + lax.broadcasted_iota(jnp.int32, (t, 1), 0)
        gb_ref[...] = _dn_gates(ab_ref[...], al_ref[...], dt_ref[...], rows >= pad)

    return _pallas(
        body, name=name, grid=(nt,),
        in_specs=[pl.BlockSpec((t, _DN_QKV), lambda i: (i, 0)), _halo_specs(_DN_QKV, t, nt, True), _row_spec(128, t),
                  pl.BlockSpec((DN_CONV, _DN_QKV), lambda i: (0, 0)), _vec_spec(128), _vec_spec(128)],
        out_specs=[_row_spec(_DN_QKV, t), _row_spec(DN_WIDTH, t), _row_spec(DN_WIDTH, t), _row_spec(DN_WIDTH, t), _row_spec(128, t)],
        out_shape=[jax.ShapeDtypeStruct((r, _DN_QKV), F32)] + [jax.ShapeDtypeStruct((r, DN_WIDTH), F32)] * 3
        + [jax.ShapeDtypeStruct((r, 128), F32)],
        compiler_params=_cparams(("parallel",)),
    )(proj, proj, ab, conv_w, alog, dtb)


def _dn_pre_bwd(co, dq, dk, dv, dgb, ab, alog, dtb, pad, name):
    r = co.shape[0]
    t = DN_PRE_TILE
    nt = r // t
    scale = DN_HEAD_DIM ** -0.5

    def body(co_ref, dq_ref, dk_ref, dv_ref, dgb_ref, ab_ref, al_ref, dt_ref, dco_ref, dab_ref, dal_ref, ddt_ref):
        i = pl.program_id(0)

        @pl.when(i == 0)
        def _():
            dal_ref[...] = jnp.zeros_like(dal_ref)
            ddt_ref[...] = jnp.zeros_like(ddt_ref)

        co_ = co_ref[...]
        act, sg = _silu(co_)
        dsilu = sg * (1.0 + co_ * (1.0 - sg))
        for hd in range(DN_HEADS):
            sl = slice(hd * 128, (hd + 1) * 128)
            for base, d_ref, sc in ((0, dq_ref, scale), (DN_WIDTH, dk_ref, 1.0)):
                cs = slice(base + hd * 128, base + (hd + 1) * 128)
                xh = act[:, cs]
                rn = lax.rsqrt(jnp.sum(xh * xh, axis=-1, keepdims=True) + EPS)
                xhat = xh * rn
                dy = d_ref[:, sl]
                dx = (sc * rn) * (dy - xhat * jnp.sum(dy * xhat, axis=-1, keepdims=True))
                dco_ref[:, cs] = dx * dsilu[:, cs]
        dco_ref[:, 2 * DN_WIDTH:] = dv_ref[...] * dsilu[:, 2 * DN_WIDTH:]
        rows = i * t + lax.broadcasted_iota(jnp.int32, (t, 1), 0)
        live = rows >= pad
        lane = lax.broadcasted_iota(jnp.int32, (1, 128), 1)
        ab_ = ab_ref[...]
        dgb_ = dgb_ref[...]
        is_g = live & (lane < DN_HEADS)
        is_b = live & (lane >= DN_HEADS) & (lane < 2 * DN_HEADS)
        arg = ab_ + dt_ref[...]
        ea = jnp.exp(al_ref[...])
        da = jnp.where(is_g, -dgb_ * ea * _sigmoid(arg), 0.0)
        beta = _sigmoid(ab_)
        dab_ref[...] = (da + jnp.where(is_b, dgb_ * beta * (1.0 - beta), 0.0)).astype(BF16)
        ddt_ref[...] += jnp.sum(da, axis=0, keepdims=True)
        dal_ref[...] += jnp.sum(jnp.where(is_g, -dgb_ * ea * _softplus(arg), 0.0), axis=0, keepdims=True)

    return _pallas(
        body, name=name, grid=(nt,),
        in_specs=[_row_spec(_DN_QKV, t), _row_spec(DN_WIDTH, t), _row_spec(DN_WIDTH, t), _row_spec(DN_WIDTH, t),
                  _row_spec(128, t), _row_spec(128, t), _vec_spec(128), _vec_spec(128)],
        out_specs=[_row_spec(_DN_QKV, t), _row_spec(128, t), _vec_spec(128), _vec_spec(128)],
        out_shape=[jax.ShapeDtypeStruct((r, _DN_QKV), F32), jax.ShapeDtypeStruct((r, 128), BF16),
                   jax.ShapeDtypeStruct((1, 128), F32), jax.ShapeDtypeStruct((1, 128), F32)],
        compiler_params=_cparams(("arbitrary",)),
    )(co, dq, dk, dv, dgb, ab, alog, dtb)


def _dn_conv_bwd(dco, proj, conv_w, name):
    r = dco.shape[0]
    t = DN_PRE_TILE
    nt = r // t

    def body(d_ref, dh_ref, x_ref, xh_ref, w_ref, dx_ref, dw_ref):
        i = pl.program_id(0)

        @pl.when(i == 0)
        def _():
            dw_ref[...] = jnp.zeros_like(dw_ref)

        d = d_ref[...]
        dhalo = jnp.where(i < nt - 1, dh_ref[...], 0.0)
        x = x_ref[...]
        xhalo = jnp.where(i > 0, xh_ref[...], 0.0)
        w = w_ref[...]
        dx = w[3:4] * d
        dws = [None] * DN_CONV
        dws[3] = jnp.sum(d * x, axis=0, keepdims=True)
        for tap in range(DN_CONV - 1):
            s = DN_CONV - 1 - tap
            dx = dx + w[tap:tap + 1] * _shift_up(d, dhalo, s, t)
            dws[tap] = jnp.sum(d * _shift_down(x, xhalo, s, t), axis=0, keepdims=True)
        dx_ref[...] = dx.astype(BF16)
        dw_ref[...] += jnp.concatenate(dws + [jnp.zeros((8 - DN_CONV, _DN_QKV), F32)], axis=0)

    return _pallas(
        body, name=name, grid=(nt,),
        in_specs=[_row_spec(_DN_QKV, t), _halo_specs(_DN_QKV, t, nt, False),
                  pl.BlockSpec((t, _DN_QKV), lambda i: (i, 0)), _halo_specs(_DN_QKV, t, nt, True),
                  pl.BlockSpec((DN_CONV, _DN_QKV), lambda i: (0, 0))],
        out_specs=[_row_spec(_DN_QKV, t), pl.BlockSpec((8, _DN_QKV), lambda i: (0, 0))],
        out_shape=[jax.ShapeDtypeStruct((r, _DN_QKV), BF16), jax.ShapeDtypeStruct((8, _DN_QKV), F32)],
        compiler_params=_cparams(("arbitrary",)),
    )(dco, dco, proj, proj, conv_w)


def _split3(x):
    hi = x.astype(BF16)
    return hi, (x - hi.astype(F32)).astype(BF16)


def _dot3s(a, b, dims=((1,), (0,))):
    return _dot(a[0], b[0], dims) + (_dot(a[0], b[1], dims) + _dot(a[1], b[0], dims))


def _dot3(a, b, dims=((1,), (0,))):
    return _dot3s(_split3(a), _split3(b), dims)


def _dn_inverse_many(n_mats):
    c = n_mats[0].shape[0]
    row = lax.broadcasted_iota(jnp.int32, (c, c), 0)
    col = lax.broadcasted_iota(jnp.int32, (c, c), 1)
    eye = (row == col).astype(F32)
    same = row // DN_SUB == col // DN_SUB
    nds = [jnp.where(same, n, 0.0) for n in n_mats]
    nos = [n - nd for n, nd in zip(n_mats, nds)]

    def geometric(bs, order):
        xs = [eye + b for b in bs]
        sp = [_split3(b) for b in bs]
        k = 2
        while k < order:
            sp = [_split3(_dot3s(s_, s_)) for s_ in sp]
            xs = [x + _dot3s(_split3(x), s_) for x, s_ in zip(xs, sp)]
            k *= 2
        return xs

    tds = [_split3(td) for td in geometric([-nd for nd in nds], DN_SUB)]
    ms = [_dot3s(td, _split3(no)) for td, no in zip(tds, nos)]
    xs = geometric([-m for m in ms], c // DN_SUB)
    return [_dot3s(_split3(x), td) for x, td in zip(xs, tds)]


def _dn_chunk_shared(gb_ref, gbt_ref):
    c = DN_CHUNK
    row = lax.broadcasted_iota(jnp.int32, (c, c), 0)
    col = lax.broadcasted_iota(jnp.int32, (c, c), 1)
    gbv = gb_ref[...]
    gam_all = _split_dot((row >= col).astype(BF16), gbv)
    hi, lo = _split3(gbt_ref[...])
    tri_t = (row <= col).astype(BF16)
    return dict(row=row, col=col, gbv=gbv, gam_all=gam_all, gam_rows=_dot(hi, tri_t) + _dot(lo, tri_t),
                lane=lax.broadcasted_iota(jnp.int32, (1, 128), 1))


def _dn_chunk_common(q, k, v, sh, h):
    c = DN_CHUNK
    row, col, lane = sh["row"], sh["col"], sh["lane"]
    gam = jnp.sum(jnp.where(lane == h, sh["gam_all"], 0.0), axis=1, keepdims=True)
    beta = jnp.sum(jnp.where(lane == h + DN_HEADS, sh["gbv"], 0.0), axis=1, keepdims=True)
    gam_row = sh["gam_rows"][h:h + 1]
    dec = jnp.where(row >= col, jnp.exp(jnp.minimum(gam - gam_row, 0.0)), 0.0)
    kb, qb = k.astype(BF16), q.astype(BF16)
    nt_dims = ((1,), (1,))
    kk = _dot(kb, kb, nt_dims)
    qk = _dot(qb, kb, nt_dims)
    eg = jnp.exp(gam)
    gam_l = gam[c - 1:c, :]
    return dict(q=q, k=k, v=v, qb=qb, kb=kb, gam=gam, beta=beta, dec=dec, kk=kk, qk=qk, eg=eg, gam_l=gam_l,
                row=row, col=col, lane=lane, att=qk * dec, qg=q * eg, kt=k * jnp.exp(gam_l - gam),
                rhs=jnp.concatenate([v * beta, k * (beta * eg)], axis=1))


def _dn_fwd(q, k, v, gb, gbt, name):
    r = q.shape[0]
    c = DN_CHUNK
    nc = r // c
    dh = DN_HEAD_DIM
    tn_dims = ((0,), (0,))

    def body(q_ref, k_ref, v_ref, gb_ref, gbt_ref, o_ref, ss_ref, ts_ref, s_ref):
        @pl.when(pl.program_id(0) == 0)
        def _():
            s_ref[...] = jnp.zeros_like(s_ref)

        heads = list(range(DN_HEADS))
        sl = [slice(h * dh, (h + 1) * dh) for h in heads]
        sh = _dn_chunk_shared(gb_ref, gbt_ref)
        zs = [_dn_chunk_common(q_ref[:, sl[h]], k_ref[:, sl[h]], v_ref[:, sl[h]], sh, h) for h in heads]
        t_invs = _dn_inverse_many([jnp.where(sh["row"] > sh["col"], z["beta"] * z["kk"] * z["dec"], 0.0) for z in zs])
        sols = [_dot3(t_inv, z["rhs"]) for t_inv, z in zip(t_invs, zs)]
        ss = [s_ref[h] for h in heads]
        sbs = [s.astype(BF16) for s in ss]
        vnbs = [(sol[:, :dh] - _dot(sol[:, dh:].astype(BF16), sb)).astype(BF16) for sol, sb in zip(sols, sbs)]
        for h in heads:
            o_ref[:, sl[h]] = _dot(zs[h]["qg"].astype(BF16), sbs[h]) + _dot(zs[h]["att"].astype(BF16), vnbs[h])
        for h in heads:
            ss_ref[0, h] = ss[h]
            ts_ref[0, h] = t_invs[h]
            s_ref[h] = ss[h] * jnp.exp(zs[h]["gam_l"]) + _dot(zs[h]["kt"].astype(BF16), vnbs[h], tn_dims)

    blk = pl.BlockSpec((c, DN_WIDTH), lambda ci: (ci, 0))
    sav = pl.BlockSpec((1, DN_HEADS, dh, dh), lambda ci: (ci, 0, 0, 0))
    return _pallas(
        body, name=name, grid=(nc,),
        in_specs=[blk, blk, blk, pl.BlockSpec((c, 128), lambda ci: (ci, 0)), pl.BlockSpec((16, c), lambda ci: (0, ci))],
        out_specs=[blk, sav, sav],
        out_shape=[jax.ShapeDtypeStruct((r, DN_WIDTH), F32), jax.ShapeDtypeStruct((nc, DN_HEADS, dh, dh), F32),
                   jax.ShapeDtypeStruct((nc, DN_HEADS, dh, dh), F32)],
        scratch_shapes=[pltpu.VMEM((DN_HEADS, dh, dh), F32)],
        compiler_params=_cparams(("arbitrary",)),
    )(q, k, v, gb, gbt)


def _dn_bwd(q, k, v, gb, gbt, ssave, tsave, do, name):
    r = q.shape[0]
    c = DN_CHUNK
    nc = r // c
    dh = DN_HEAD_DIM
    nt_dims = ((1,), (1,))
    tn_dims = ((0,), (0,))

    def body(q_ref, k_ref, v_ref, gb_ref, gbt_ref, ss_ref, ts_ref, do_ref, dq_ref, dk_ref, dv_ref, dgb_ref, ds_ref):
        @pl.when(pl.program_id(0) == 0)
        def _():
            ds_ref[...] = jnp.zeros_like(ds_ref)

        heads = list(range(DN_HEADS))
        sl = [slice(h * dh, (h + 1) * dh) for h in heads]
        sh = _dn_chunk_shared(gb_ref, gbt_ref)
        row, col, lane = sh["row"], sh["col"], sh["lane"]
        rs = lambda x: jnp.sum(x, axis=1, keepdims=True)
        tot = lambda x: jnp.sum(rs(x), axis=0, keepdims=True)
        st = [dict() for _ in heads]
        dgb_parts = []

        def s_common(h):
            st[h].update(_dn_chunk_common(q_ref[:, sl[h]], k_ref[:, sl[h]], v_ref[:, sl[h]], sh, h))
            st[h]["t"] = _split3(ts_ref[0, h])

        def s_sol(h):
            st[h]["sol"] = _dot3s(st[h]["t"], _split3(st[h]["rhs"]))

        def s_state(h):
            z = st[h]
            sol = z["sol"]
            kcd = sol[:, dh:]
            s = ss_ref[0, h]
            sb = s.astype(BF16)
            vnb = (sol[:, :dh] - _dot(kcd.astype(BF16), sb)).astype(BF16)
            ds_next = ds_ref[h]
            dsb = ds_next.astype(BF16)
            dob = do_ref[:, sl[h]].astype(BF16)
            z["dqg"] = _dot(dob, sb, nt_dims)
            ds = _dot(z["qg"].astype(BF16), dob, tn_dims)
            z["d_att"] = jnp.where(row >= col, _dot(dob, vnb, nt_dims), 0.0)
            dvn = _dot(z["att"].astype(BF16), dob, tn_dims) + _dot(z["kt"].astype(BF16), dsb)
            z["dkt"] = _dot(vnb, dsb, nt_dims)
            eg_l = jnp.exp(z["gam_l"])
            ds = ds + ds_next * eg_l
            z["dgam_l"] = tot(ds_next * s) * eg_l
            dvnb = dvn.astype(BF16)
            dkcd = -_dot(dvnb, sb, nt_dims)
            ds_ref[h] = ds - _dot(kcd.astype(BF16), dvnb, tn_dims)
            z["dsol"] = jnp.concatenate([dvn, dkcd], axis=1)

        def s_drhs(h):
            st[h]["drhs"] = _dot3s(st[h]["t"], _split3(st[h]["dsol"]), tn_dims)

        def s_dn(h):
            z = st[h]
            z["dn"] = jnp.where(row > col, -_dot3(z["drhs"], z["sol"], nt_dims), 0.0)

        def s_rest(h):
            z = st[h]
            k_, v_, kb, qb = z["k"], z["v"], z["kb"], z["qb"]
            beta, eg, dec, kk, qk, gam, gam_l = z["beta"], z["eg"], z["dec"], z["kk"], z["qk"], z["gam"], z["gam_l"]
            dn, d_att, dqg, dkt = z["dn"], z["d_att"], z["dqg"], z["dkt"]
            drv, drk = z["drhs"][:, :dh], z["drhs"][:, dh:]
            s_rkk = rs(drk * k_)
            dv_ref[:, sl[h]] = drv * beta
            dbeta = rs(drv * v_) + s_rkk * eg + rs(dn * kk * dec)
            dk = drk * (beta * eg)
            dgam = s_rkk * beta * eg
            dkk = (dn * beta * dec).astype(BF16)
            dd = dn * beta * kk + d_att * qk
            dqk = (d_att * dec).astype(BF16)
            dq_ref[:, sl[h]] = _dot(dqk, kb) + dqg * eg
            dk = dk + _dot(dqk, qb, tn_dims) + _dot(dkk, kb) + _dot(dkk, kb, tn_dims)
            w = dd * dec
            wh, wl = _split3(w)
            ones = jnp.ones((c, 128), BF16)
            col_sum = (_dot(wh, ones, tn_dims) + _dot(wl, ones, tn_dims))[:, 0:1]
            dgam = dgam + rs(w) - col_sum + rs(dqg * z["qg"]) - rs(dkt * z["kt"])
            dk_ref[:, sl[h]] = dk + dkt * jnp.exp(gam_l - gam)
            dgam_l = z["dgam_l"] + tot(dkt * z["kt"])
            rowc = lax.broadcasted_iota(jnp.int32, (c, 1), 0)
            dgam = dgam + jnp.where(rowc == c - 1, dgam_l, 0.0)
            dg = _split_dot((row <= col).astype(BF16), jnp.broadcast_to(dgam, (c, 128)))[:, 0:1]
            dgb_parts.append(jnp.where(lane == h, dg, 0.0) + jnp.where(lane == h + DN_HEADS, dbeta, 0.0))

        _emit_chains(heads, [s_common, s_sol, s_state, s_drhs, s_dn, s_rest], False)
        dgb = dgb_parts[0]
        for part in dgb_parts[1:]:
            dgb = dgb + part
        dgb_ref[...] = dgb

    blk = pl.BlockSpec((c, DN_WIDTH), lambda ci: (nc - 1 - ci, 0))
    sav = pl.BlockSpec((1, DN_HEADS, dh, dh), lambda ci: (nc - 1 - ci, 0, 0, 0))
    gspec = pl.BlockSpec((c, 128), lambda ci: (nc - 1 - ci, 0))
    return _pallas(
        body, name=name, grid=(nc,),
        in_specs=[blk, blk, blk, gspec, pl.BlockSpec((16, c), lambda ci: (0, nc - 1 - ci)), sav, sav, blk],
        out_specs=[blk, blk, blk, gspec],
        out_shape=[jax.ShapeDtypeStruct((r, DN_WIDTH), F32)] * 3 + [jax.ShapeDtypeStruct((r, 128), F32)],
        scratch_shapes=[pltpu.VMEM((DN_HEADS, dh, dh), F32)],
        compiler_params=_cparams(("arbitrary",)),
    )(q, k, v, gb, gbt, ssave, tsave, do)


def _dn_post_fwd(o, proj, g, name):
    r = o.shape[0]

    def body(o_ref, z_ref, g_ref, y_ref):
        g_ = g_ref[...]
        for hd in range(DN_HEADS):
            sl = slice(hd * 128, (hd + 1) * 128)
            sz, _ = _silu(z_ref[:, sl])
            y_ref[:, sl] = (_rms(o_ref[:, sl], g_) * sz).astype(BF16)

    return _pallas(body, name=name, grid=(r // ROW_TILE,),
                   in_specs=[_row_spec(DN_WIDTH), pl.BlockSpec((ROW_TILE, DN_WIDTH), lambda i: (i, 3)), _vec_spec(128)],
                   out_specs=_row_spec(DN_WIDTH), out_shape=jax.ShapeDtypeStruct((r, DN_WIDTH), BF16),
                   compiler_params=_cparams(("parallel",)))(o, proj, g)


def _dn_post_bwd(o, proj, g, dy, name):
    r = o.shape[0]

    def body(o_ref, z_ref, g_ref, dy_ref, do_ref, dz_ref, dg_ref):
        @pl.when(pl.program_id(0) == 0)
        def _():
            dg_ref[...] = jnp.zeros_like(dg_ref)

        g_ = g_ref[...]
        for hd in range(DN_HEADS):
            sl = slice(hd * 128, (hd + 1) * 128)
            z_ = z_ref[:, sl]
            sz, sg = _silu(z_)
            dy_ = dy_ref[:, sl]
            o_ = o_ref[:, sl]
            dz_ref[:, sl] = (dy_ * _rms(o_, g_) * (sg * (1.0 + z_ * (1.0 - sg)))).astype(BF16)
            dx, dg = _rms_bwd(o_, g_, dy_ * sz)
            do_ref[:, sl] = dx
            dg_ref[...] += dg

    return _pallas(body, name=name, grid=(r // ROW_TILE,),
                   in_specs=[_row_spec(DN_WIDTH), pl.BlockSpec((ROW_TILE, DN_WIDTH), lambda i: (i, 3)), _vec_spec(128),
                             _row_spec(DN_WIDTH)],
                   out_specs=[_row_spec(DN_WIDTH), _row_spec(DN_WIDTH), _vec_spec(128)],
                   out_shape=[jax.ShapeDtypeStruct((r, DN_WIDTH), F32), jax.ShapeDtypeStruct((r, DN_WIDTH), BF16),
                              jax.ShapeDtypeStruct((1, 128), F32)],
                   compiler_params=_cparams(("arbitrary",)))(o, proj, g, dy)


def _exchange(arrays, scatter, name):
    n = len(arrays)

    def body(*refs):
        copies = _exchange_copies(refs[:n], refs[n:2 * n], scatter, *refs[2 * n:])
        for cp in copies:
            cp.start()
        for cp in copies:
            cp.wait()

    hbm = pl.BlockSpec(memory_space=pl.ANY)
    return _pallas(
        body, name=name, in_specs=[hbm] * n, out_specs=[hbm] * n, out_shape=_exchange_shapes(arrays, scatter),
        scratch_shapes=_exchange_sems(n),
    )(*arrays)


def _exchange_shapes(arrays, scatter):
    return [jax.ShapeDtypeStruct((N_DEV,) + (a.shape[1:] if sc else a.shape), a.dtype) for a, sc in zip(arrays, scatter)]


def _exchange_sems(n):
    return [pltpu.SemaphoreType.DMA((n * N_DEV,)), pltpu.SemaphoreType.DMA((n * N_DEV,)), pltpu.SemaphoreType.DMA((n,))]


def _exchange_copies(in_refs, out_refs, scatter, send_sems, recv_sems, local_sems):
    mx, my, mc = lax.axis_index("x"), lax.axis_index("y"), lax.axis_index("c")
    me = 4 * mx + 2 * my + mc
    copies = []
    for a in range(len(in_refs)):
        src_own = in_refs[a].at[me] if scatter[a] else in_refs[a]
        copies.append(pltpu.make_async_copy(src_own, out_refs[a].at[me], local_sems.at[a]))
        for kbits in range(1, N_DEV):
            px = lax.rem(mx + ((kbits >> 2) & 1), 2)
            py = lax.rem(my + ((kbits >> 1) & 1), 2)
            pc = lax.rem(mc + (kbits & 1), 2)
            src = in_refs[a].at[4 * px + 2 * py + pc] if scatter[a] else in_refs[a]
            copies.append(pltpu.make_async_remote_copy(
                src_ref=src, dst_ref=out_refs[a].at[me],
                send_sem=send_sems.at[a * N_DEV + kbits], recv_sem=recv_sems.at[a * N_DEV + kbits],
                device_id=(px, py, pc), device_id_type=pl.DeviceIdType.MESH))
    return copies


def _adamw(gstack, w, m, v, name):
    a, b = w.shape
    ta = a
    for t in (1024, 512, 256, 128, 64, 32, 16, 8):
        if a % t == 0 and N_DEV * t * b * 4 <= 4 * 1024 * 1024:
            ta = t
            break
    c1 = 1.0 / (1.0 - ADAM_B1 ** ADAM_STEP)
    c2 = 1.0 / (1.0 - ADAM_B2 ** ADAM_STEP)

    def body(g_ref, w_ref, m_ref, v_ref, og_ref, od_ref, om_ref, ov_ref):
        g = g_ref[0].astype(F32)
        for s in range(1, N_DEV):
            g = g + g_ref[s].astype(F32)
        m_new = ADAM_B1 * m_ref[...] + (1.0 - ADAM_B1) * g
        v_new = ADAM_B2 * v_ref[...] + (1.0 - ADAM_B2) * (g * g)
        og_ref[...] = g
        om_ref[...] = m_new
        ov_ref[...] = v_new
        od_ref[...] = -ADAM_LR * ((m_new * c1) / (jnp.sqrt(v_new * c2) + ADAM_EPS) + ADAM_WD * w_ref[...])

    spec = pl.BlockSpec((ta, b), lambda i: (i, 0))
    return _pallas(
        body, name=name, grid=(a // ta,),
        in_specs=[pl.BlockSpec((N_DEV, ta, b), lambda i: (0, i, 0)), spec, spec, spec],
        out_specs=[spec] * 4, out_shape=[jax.ShapeDtypeStruct((a, b), F32)] * 4,
        compiler_params=_cparams(("parallel",)),
    )(gstack, w, m, v)


_WEIGHTS = ['meta_tokens', 'pre_mix_norm', 'post_mix_norm', 'pre_mlp_norm', 'post_mlp_norm', 'mlp_w1', 'mlp_w2',
            'w_in_even', 'w_out_even', 'sb_out_norm', 's5_lambda_re', 's5_lambda_im', 's5_log_dt', 's5_b_re', 's5_b_im',
            's5_c_re', 's5_c_im', 's5_d', 's5_w_glu', 's5_b_glu', 's5_out_norm', 'w_in_odd', 'dn_conv_w', 'dn_a_log',
            'dn_dt_bias', 'dn_out_norm', 'w_out_odd']
_SHARDED = ['meta_tokens', 'mlp_w1', 'mlp_w2', 'w_in_even', 'w_out_even', 's5_w_glu', 'w_in_odd', 'dn_conv_w', 'w_out_odd']
_SMALL = [n for n in _WEIGHTS if n not in _SHARDED]
_GATHER_FIRST = ['meta_tokens', 'w_in_even', 's5_w_glu', 'w_out_even']
_GATHER_LATE = [n for n in _SHARDED if n not in _GATHER_FIRST]
_REDUCE_EARLY = ['mlp_w1', 'mlp_w2', 'w_in_odd', 'dn_conv_w', 'w_out_odd', 'w_out_even']


def _view2d(name, a):
    return a.reshape(-1, a.shape[-1])


def _unshard(name, g):
    if name == 'mlp_w1':
        return g.reshape(N_DEV, 2, D_MODEL, -1).transpose(1, 2, 0, 3).reshape(2, D_MODEL, D_FF)
    if name == 'mlp_w2':
        return g.reshape(N_DEV, 2, -1, D_MODEL).transpose(1, 0, 2, 3).reshape(2, D_FF, D_MODEL)
    if name in ('w_in_even', 'w_in_odd', 'dn_conv_w', 'meta_tokens'):
        return g.transpose(1, 0, 2).reshape(g.shape[1], -1)
    return g.reshape(-1, g.shape[-1])


def _to_blocks(name, full):
    if name == 'mlp_w1':
        return full.reshape(2, D_MODEL, N_DEV, -1).transpose(2, 0, 1, 3).reshape(N_DEV, 2 * D_MODEL, -1)
    if name == 'mlp_w2':
        return full.reshape(2, N_DEV, -1, D_MODEL).transpose(1, 0, 2, 3).reshape(N_DEV, -1, D_MODEL)
    if name in ('w_in_even', 'w_in_odd', 'dn_conv_w', 'meta_tokens'):
        return full.reshape(full.shape[0], N_DEV, -1).transpose(1, 0, 2)
    return full.reshape(N_DEV, -1, full.shape[-1])


def _pack(parts):
    rows = []
    for p in parts:
        flat = p.reshape(-1)
        rows.append(jnp.pad(flat, (0, (-flat.shape[0]) % 128)).reshape(-1, 128))
    return jnp.concatenate(rows, axis=0)


def _unpack(packed, like):
    out, at = [], 0
    for p in like:
        n = math.prod(p.shape)
        nrow = -(-n // 128)
        out.append(packed[at:at + nrow].reshape(-1)[:n].reshape(p.shape))
        at += nrow
    return out


def _lane_vec(x, width=128):
    flat = x.reshape(-1)
    return jnp.pad(flat, (0, width - flat.shape[0])).reshape(1, width)


def kernel(x, meta_tokens, pre_mix_norm, post_mix_norm, pre_mlp_norm, post_mlp_norm, mlp_w1, mlp_w2, w_in_even, w_out_even, sb_out_norm, s5_lambda_re, s5_lambda_im, s5_log_dt, s5_b_re, s5_b_im, s5_c_re, s5_c_im, s5_d, s5_w_glu, s5_b_glu, s5_out_norm, w_in_odd, dn_conv_w, dn_a_log, dn_dt_bias, dn_out_norm, w_out_odd, loss_target, m_meta_tokens, m_pre_mix_norm, m_post_mix_norm, m_pre_mlp_norm, m_post_mlp_norm, m_mlp_w1, m_mlp_w2, m_w_in_even, m_w_out_even, m_sb_out_norm, m_s5_lambda_re, m_s5_lambda_im, m_s5_log_dt, m_s5_b_re, m_s5_b_im, m_s5_c_re, m_s5_c_im, m_s5_d, m_s5_w_glu, m_s5_b_glu, m_s5_out_norm, m_w_in_odd, m_dn_conv_w, m_dn_a_log, m_dn_dt_bias, m_dn_out_norm, m_w_out_odd, v_meta_tokens, v_pre_mix_norm, v_post_mix_norm, v_pre_mlp_norm, v_post_mlp_norm, v_mlp_w1, v_mlp_w2, v_w_in_even, v_w_out_even, v_sb_out_norm, v_s5_lambda_re, v_s5_lambda_im, v_s5_log_dt, v_s5_b_re, v_s5_b_im, v_s5_c_re, v_s5_c_im, v_s5_d, v_s5_w_glu, v_s5_b_glu, v_s5_out_norm, v_w_in_odd, v_dn_conv_w, v_dn_a_log, v_dn_dt_bias, v_dn_out_norm, v_w_out_odd):
    given = dict(locals())
    w = {n: given[n] for n in _WEIGHTS}
    mom_m = {n: given["m_" + n] for n in _WEIGHTS}
    mom_v = {n: given["v_" + n] for n in _WEIGHTS}

    seq = x.shape[1]
    assert x.shape[0] == 1 and seq % ROW_TILE == 0
    r = seq + ROW_TILE
    pad = ROW_TILE - N_META
    pad_tiles = 1

    wire = {n: (F32 if n in ('dn_conv_w', 'meta_tokens') else BF16) for n in _SHARDED}
    shard_wire = lambda n: _view2d(n, w[n]).astype(wire[n])
    gathered = _exchange([shard_wire(n) for n in _GATHER_FIRST], [False] * len(_GATHER_FIRST), "gather_first")
    full = {n: _unshard(n, g_) for n, g_ in zip(_GATHER_FIRST, gathered)}
    w_ie, w_oe, w_glu = full['w_in_even'], full['w_out_even'], full['s5_w_glu']
    row = lambda v_: v_.reshape(1, -1)

    hs0 = jnp.concatenate([jnp.zeros((pad, D_MODEL), F32), full['meta_tokens'], x[0]], axis=0)
    hn0 = _norm_pre(hs0, row(pre_mix_norm[0]), "pre_mix_0")
    qkv = _mm_fwd(hn0, w_ie[:, :3 * SB_WIDTH], "in_even_qkv", out_dtypes=(BF16,))
    u = _mm_fwd(hn0, w_ie[:, 3 * SB_WIDTH:], "in_even_u")
    q, k, v = qkv[:, :SB_WIDTH], qkv[:, SB_WIDTH:2 * SB_WIDTH], qkv[:, 2 * SB_WIDTH:]
    nb = r // ATT_BLK
    blocks_t = lambda t_: t_.reshape(nb, ATT_BLK, 4, 128).transpose(2, 0, 3, 1)
    o_sb, ssave, gathered = _sb_fwd(q, k, blocks_t(v), pad, "sb_fwd",
                                    ride=([shard_wire(n) for n in _GATHER_LATE], [False] * len(_GATHER_LATE)))
    full.update({n: _unshard(n, g_) for n, g_ in zip(_GATHER_LATE, gathered)})
    w1, w2, w_oo, conv_w = full['mlp_w1'], full['mlp_w2'], full['w_out_odd'], full['dn_conv_w']
    w_io = full['w_in_odd'][:, :4 * DN_WIDTH]
    w_ab = jnp.pad(full['w_in_odd'][:, 4 * DN_WIDTH:], ((0, 0), (0, 128 - 2 * DN_HEADS)))
    on_sb = _norm_pre(o_sb, row(sb_out_norm[0]), "sb_out_norm")

    lam_re, lam_im, logdt, btr, bti, ctr, cti, s5_mask = _s5_expand(
        s5_lambda_re[0], s5_lambda_im[0], s5_log_dt[0], s5_b_re[0], s5_b_im[0], s5_c_re[0], s5_c_im[0])
    a_re, a_im, bbr, bbi = _s5_prep(lam_re, lam_im, logdt, btr, bti, "s5_prep")
    s5_wb = jnp.stack([_s5_block_diag_b(bbr, s5_mask), _s5_block_diag_b(bbi, s5_mask)]).astype(BF16)
    s5_wc = jnp.stack([_s5_block_diag_c(ctr, s5_mask), _s5_block_diag_c(cti, s5_mask)]).astype(BF16)
    s5_a = jnp.stack([a_re, a_im])
    s5_args = (s5_wb, s5_a, s5_wc, row(s5_d[0]), w_glu, row(s5_b_glu[0]), row(s5_out_norm[0]))
    y_s5, on_s5, xstart = _s5_fwd(u, *s5_args, "s5_fwd")

    merged = jnp.concatenate([on_sb, on_s5], axis=1)
    mix0 = _mm_fwd(merged, w_oe, "out_even")
    hs1, hn1 = _norm_post_pre(hs0, mix0, row(post_mix_norm[0]), row(pre_mlp_norm[0]), "post_mix_0")
    relu2 = lambda acc: (jnp.square(jnp.maximum(acc, 0.0)), jnp.maximum(acc, 0.0))
    r0, ra0 = _mm_fwd(hn1, w1[0], "mlp_up_0", out_dtypes=(BF16, BF16), epilogue=relu2)
    m0 = _mm_fwd(r0, w2[0], "mlp_down_0")
    hs2, hn2 = _norm_post_pre(hs1, m0, row(post_mlp_norm[0]), row(pre_mix_norm[1]), "post_mlp_0")

    proj = _mm_fwd(hn2, w_io, "in_odd")
    ab = _mm_fwd(hn2, w_ab, "in_odd_gates")
    alog, dtb = _lane_vec(dn_a_log[0]), _lane_vec(dn_dt_bias[0])
    co, qd, kd, vd, gb = _dn_pre_fwd(proj, ab, conv_w, alog, dtb, pad, "dn_pre")
    gbt = gb[:, :2 * DN_HEADS].T
    o_dn, s_dn, t_dn = _dn_fwd(qd, kd, vd, gb, gbt, "dn_fwd")
    on_dn = _dn_post_fwd(o_dn, proj, row(dn_out_norm[0]), "dn_post")
    mix1 = _mm_fwd(on_dn, w_oo, "out_odd")
    hs3, hn3 = _norm_post_pre(hs2, mix1, row(post_mix_norm[1]), row(pre_mlp_norm[1]), "post_mix_1")
    r1, ra1 = _mm_fwd(hn3, w1[1], "mlp_up_1", out_dtypes=(BF16, BF16), epilogue=relu2)
    m1 = _mm_fwd(r1, w2[1], "mlp_down_1")
    dhs, loss_part = _norm_post_loss(hs3, m1, row(post_mlp_norm[1]), loss_target[0], pad_tiles, "post_mlp_1_loss")
    loss = lax.psum(loss_part, ("x", "y", "c"))

    g = {}
    drelu2 = lambda acc, ra: (acc * (2.0 * ra.astype(F32)),)

    def mlp_bwd(layer, hn, rr, ra, dm):
        dw2 = _mm_wgrad(rr, dm, f"mlp_down_{layer}_wgrad")
        da = _mm_dgrad(dm, w2[layer], f"mlp_down_{layer}_dgrad", out_dtypes=(BF16,), extras=(ra,), epilogue=drelu2)
        dw1 = _mm_wgrad(hn, da, f"mlp_up_{layer}_wgrad")
        return dw1, dw2, _mm_dgrad(da, w1[layer], f"mlp_up_{layer}_dgrad")

    _, dm1, _, dg_post_mlp1 = _norm_bwd(dhs, post=(m1, row(post_mlp_norm[1])), pad=pad, name="post_mlp_1_bwd")
    dw1_1, dw2_1, dhn3 = mlp_bwd(1, hn3, r1, ra1, dm1)
    dhs, dmix1, dg_pre_mlp1, dg_post_mix1 = _norm_bwd(
        dhs, pre=(hs3, row(pre_mlp_norm[1]), dhn3), post=(mix1, row(post_mix_norm[1])), pad=pad, name="post_mix_1_bwd")

    g['w_out_odd'] = _mm_wgrad(on_dn, dmix1, "out_odd_wgrad")
    d_on_dn = _mm_dgrad(dmix1, w_oo, "out_odd_dgrad")
    do_dn, dz, dg_dn = _dn_post_bwd(o_dn, proj, row(dn_out_norm[0]), d_on_dn, "dn_post_bwd")
    dqd, dkd, dvd, dgb = _dn_bwd(qd, kd, vd, gb, gbt, s_dn, t_dn, do_dn, "dn_bwd")
    dco, dab, d_alog, d_dtb = _dn_pre_bwd(co, dqd, dkd, dvd, dgb, ab, alog, dtb, pad, "dn_pre_bwd")
    dpre, d_conv = _dn_conv_bwd(dco, proj, conv_w, "dn_conv_bwd")
    dproj = jnp.concatenate([dpre, dz], axis=1)
    g['w_in_odd'] = jnp.concatenate([_mm_wgrad(hn2, dproj, "in_odd_wgrad"),
                                     _mm_wgrad(hn2, dab, "in_odd_gates_wgrad")[:, :2 * DN_HEADS]], axis=1)
    dhn2 = _mm_dgrad(dab, w_ab, "in_odd_gates_dgrad")
    dhn2 = _mm_dgrad(dproj, w_io, "in_odd_dgrad", extras=(dhn2,), epilogue=lambda acc, other: (acc + other,))
    g['dn_conv_w'] = d_conv[:DN_CONV]
    g['dn_a_log'], g['dn_dt_bias'], g['dn_out_norm'] = d_alog[0, :DN_HEADS], d_dtb[0, :DN_HEADS], dg_dn[0]

    dhs, dm0, dg_pre_mix1, dg_post_mlp0 = _norm_bwd(
        dhs, pre=(hs2, row(pre_mix_norm[1]), dhn2), post=(m0, row(post_mlp_norm[0])), pad=pad, name="post_mlp_0_bwd")
    dw1_0, dw2_0, dhn1 = mlp_bwd(0, hn1, r0, ra0, dm0)
    dhs, dmix0, dg_pre_mlp0, dg_post_mix0 = _norm_bwd(
        dhs, pre=(hs1, row(pre_mlp_norm[0]), dhn1), post=(mix0, row(post_mix_norm[0])), pad=pad, name="post_mix_0_bwd")

    g['w_out_even'] = _mm_wgrad(merged, dmix0, "out_even_wgrad")
    dmerged = _mm_dgrad(dmix0, w_oe, "out_even_dgrad")
    _, do_sb, _, dg_sb = _norm_bwd(dmerged[:, :SB_WIDTH], post=(o_sb, row(sb_out_norm[0])), pad=pad, dm_dtype=F32,
                                   name="sb_out_norm_bwd")
    dq, dk4, dv4 = _sb_bwd(q, k, v, blocks_t(k), ssave, do_sb, pad, "sb_bwd")
    unheads = lambda t_: t_.transpose(1, 0, 2).reshape(r, SB_WIDTH)
    g['mlp_w1'] = jnp.stack([dw1_0, dw1_1])
    g['mlp_w2'] = jnp.stack([dw2_0, dw2_1])
    grad_wire = lambda n: _to_blocks(n, g[n].reshape(full[n].shape)).astype(wire[n])
    du, d_a, d_d, d_bglu, dg_s5, d_wb, d_wc, g['s5_w_glu'], reduced = _s5_bwd(
        u, y_s5, dmerged[:, SB_WIDTH:], xstart, *s5_args, "s5_bwd",
        ride=([grad_wire(n) for n in _REDUCE_EARLY], [True] * len(_REDUCE_EARLY)))
    stacks = dict(zip(_REDUCE_EARLY, reduced))
    g_lr, g_li, g_dt, g_btr, g_bti = _s5_prep_bwd(
        lam_re, lam_im, logdt, btr, bti, d_a[0], d_a[1],
        _s5_diag_of_b(d_wb[0], s5_mask), _s5_diag_of_b(d_wb[1], s5_mask), "s5_prep_bwd")
    gg, nn, pp = S5_GROUPS, S5_STATE, S5_GROUP
    g['s5_lambda_re'], g['s5_lambda_im'] = g_lr.reshape(gg, nn), g_li.reshape(gg, nn)
    g['s5_log_dt'] = g_dt.reshape(gg, nn)[:, 0]
    g['s5_b_re'], g['s5_b_im'] = g_btr.T.reshape(gg, nn, pp), g_bti.T.reshape(gg, nn, pp)
    g['s5_c_re'] = _s5_diag_of_c(d_wc[0], s5_mask).reshape(gg, nn, pp).transpose(0, 2, 1)
    g['s5_c_im'] = _s5_diag_of_c(d_wc[1], s5_mask).reshape(gg, nn, pp).transpose(0, 2, 1)
    g['s5_d'], g['s5_b_glu'], g['s5_out_norm'], g['sb_out_norm'] = d_d[0], d_bglu[0], dg_s5[0], dg_sb[0]
    dqkvu = jnp.concatenate([dq, unheads(dk4), unheads(dv4), du], axis=1).astype(BF16)
    g['w_in_even'] = _mm_wgrad(hn0, dqkvu, "in_even_wgrad")
    dhn0 = _mm_dgrad(dqkvu, w_ie, "in_even_dgrad")
    dhs, _, dg_pre_mix0, _ = _norm_bwd(dhs, pre=(hs0, row(pre_mix_norm[0]), dhn0), pad=pad, name="pre_mix_0_bwd")

    g['meta_tokens'] = dhs[pad:pad + N_META]
    g['pre_mix_norm'] = jnp.concatenate([dg_pre_mix0, dg_pre_mix1], axis=0)
    g['post_mix_norm'] = jnp.concatenate([dg_post_mix0, dg_post_mix1], axis=0)
    g['pre_mlp_norm'] = jnp.concatenate([dg_pre_mlp0, dg_pre_mlp1], axis=0)
    g['post_mlp_norm'] = jnp.concatenate([dg_post_mlp0, dg_post_mlp1], axis=0)
    grad_x = dhs[pad + N_META:][None]

    small_like = [w[n] for n in _SMALL]
    last = [n for n in _SHARDED if n not in _REDUCE_EARLY]
    partial = [grad_wire(n) for n in last] + [_pack([g[n].reshape(w[n].shape) for n in _SMALL])]
    reduced = _exchange(partial, [True] * len(last) + [False], "reduce_last")
    stacks.update(zip(last, reduced[:-1]))
    grads, deltas, new_m, new_v = {}, {}, {}, {}
    for n in _SHARDED:
        outs = _adamw(stacks[n], _view2d(n, w[n]), _view2d(n, mom_m[n]), _view2d(n, mom_v[n]), f"adamw_{n}")
        grads[n], deltas[n], new_m[n], new_v[n] = (o.reshape(w[n].shape) for o in outs)
    outs = _adamw(reduced[-1], _pack(small_like), _pack([mom_m[n] for n in _SMALL]), _pack([mom_v[n] for n in _SMALL]),
                  "adamw_small")
    for dst, o in zip((grads, deltas, new_m, new_v), outs):
        for n, part in zip(_SMALL, _unpack(o, small_like)):
            dst[n] = part
    return (loss, grad_x, *[grads[n] for n in _WEIGHTS], *[deltas[n] for n in _WEIGHTS],
            *[new_m[n] for n in _WEIGHTS], *[new_v[n] for n in _WEIGHTS])
```

```python
import functools
import math

import jax
import jax.numpy as jnp
from jax import lax
from jax.experimental import pallas as pl
from jax.experimental.pallas import tpu as pltpu

F32 = jnp.float32
BF16 = jnp.bfloat16

D_MODEL = 1024
N_META = 16
SB_HEAD_DIM = 64
SB_WIDTH = 512
S5_WIDTH = 512
S5_GROUP = 16
S5_GROUPS = 32
S5_STATE = 64
S5_NS = S5_GROUPS * S5_STATE
DN_HEAD_DIM = 128
DN_HEADS = 8
DN_WIDTH = 1024
DN_CONV = 4
D_FF = 4096
EPS = 1e-6
N_DEV = 8

ADAM_LR = 0.001
ADAM_B1 = 0.9
ADAM_B2 = 0.999
ADAM_EPS = 1e-08
ADAM_WD = 0.01
ADAM_STEP = 10

ROW_TILE = 512
ATT_BLK = 256
SB_BLOCKS_PER_TRIP = 3
SB_LOG_ZERO = -106.0
SB_FWD_SKEW = False
SB_BWD_SKEW = True
DN_CHUNK = 128
DN_SUB = 16
S5_TILE = 128
S5_CHUNKS = 4
VMEM_LIMIT = 56 * 1024 * 1024

_HIGH = lax.Precision.HIGHEST


def _pallas(body, **kw):
    return pl.pallas_call(body, **kw)


def _cparams(sem):
    return pltpu.CompilerParams(dimension_semantics=sem, vmem_limit_bytes=VMEM_LIMIT)


def _dot(a, b, dims=((1,), (0,))):
    return lax.dot_general(a, b, (dims, ((), ())), preferred_element_type=F32)


def _dot_hi(a, b):
    return lax.dot_general(a, b, (((1,), (0,)), ((), ())), preferred_element_type=F32, precision=_HIGH)


def _split_dot(m_bf16, x):
    hi = x.astype(BF16)
    lo = (x - hi.astype(F32)).astype(BF16)
    return _dot(m_bf16, hi) + _dot(m_bf16, lo)


def _matmul(a, b, *, ta=False, tb=False, tm, tn, tk, name, out_dtypes=(F32,), extras=(), epilogue=None):
    m, k = (a.shape[1], a.shape[0]) if ta else a.shape
    n = b.shape[0] if tb else b.shape[1]
    assert (b.shape[1] if tb else b.shape[0]) == k
    assert m % tm == 0 and n % tn == 0 and k % tk == 0, (name, m, n, k, tm, tn, tk)
    nk = k // tk
    n_ex = len(extras)
    n_out = len(out_dtypes)
    dims = ((0 if ta else 1,), (1 if tb else 0,))

    def finish(acc, ex_refs, o_refs):
        outs = (acc,) if epilogue is None else epilogue(acc, *[r[...] for r in ex_refs])
        for o_ref, o in zip(o_refs, outs):
            o_ref[...] = o.astype(o_ref.dtype)

    def body(*refs):
        a_ref, b_ref = refs[0], refs[1]
        ex_refs = refs[2:2 + n_ex]
        o_refs = refs[2 + n_ex:2 + n_ex + n_out]
        prod = _dot(a_ref[...].astype(BF16), b_ref[...].astype(BF16), dims)
        if nk == 1:
            finish(prod, ex_refs, o_refs)
            return
        acc_ref = refs[-1]
        kk = pl.program_id(2)

        @pl.when(kk == 0)
        def _():
            acc_ref[...] = prod

        @pl.when(kk > 0)
        def _():
            acc_ref[...] += prod

        @pl.when(kk == nk - 1)
        def _():
            finish(acc_ref[...], ex_refs, o_refs)

    a_spec = pl.BlockSpec((tk, tm), lambda j, i, kk: (kk, i)) if ta else pl.BlockSpec((tm, tk), lambda j, i, kk: (i, kk))
    b_spec = pl.BlockSpec((tn, tk), lambda j, i, kk: (j, kk)) if tb else pl.BlockSpec((tk, tn), lambda j, i, kk: (kk, j))
    o_spec = pl.BlockSpec((tm, tn), lambda j, i, kk: (i, j))
    outs = _pallas(
        body, name=name,
        grid=(n // tn, m // tm, nk),
        in_specs=[a_spec, b_spec] + [o_spec] * n_ex,
        out_specs=[o_spec] * n_out,
        out_shape=[jax.ShapeDtypeStruct((m, n), dt) for dt in out_dtypes],
        scratch_shapes=[] if nk == 1 else [pltpu.VMEM((tm, tn), F32)],
        compiler_params=_cparams(("parallel", "parallel", "arbitrary")),
    )(a, b, *extras)
    return outs[0] if n_out == 1 else outs


def _tile(n, cap):
    best = 128
    for t in range(128, min(n, cap) + 1, 128):
        if n % t == 0:
            best = t
    assert n % best == 0, n
    return best


MM_K_CAP = 4096
WGRAD_ROWS = 1536


MM_LHS_TILE_BYTES = 6 * 1024 * 1024


def _row_tile(x, depth):
    tall = 3 * ROW_TILE
    fits = tall * depth * x.dtype.itemsize <= MM_LHS_TILE_BYTES
    return tall if (x.shape[0] % tall == 0 and fits) else ROW_TILE


def _mm_fwd(x, w, name, **kw):
    k, n = w.shape
    tk = _tile(k, MM_K_CAP)
    return _matmul(x, w, tm=_row_tile(x, tk), tn=_tile(n, 1024), tk=tk, name=name, **kw)


def _mm_dgrad(dy, w, name, **kw):
    k, n = w.shape
    tk = _tile(n, MM_K_CAP)
    return _matmul(dy, w, tb=True, tm=_row_tile(dy, tk), tn=_tile(k, 1024), tk=tk, name=name, **kw)


def _mm_wgrad(x, dy, name):
    k, n = x.shape[1], dy.shape[1]
    rows = x.shape[0]
    return _matmul(x, dy, ta=True, tm=_tile(k, 512), tn=_tile(n, 1024),
                   tk=WGRAD_ROWS if rows % WGRAD_ROWS == 0 else ROW_TILE, name=name)


def _rms(x, g):
    r = lax.rsqrt(jnp.mean(x * x, axis=-1, keepdims=True) + EPS)
    return x * r * g


def _rms_bwd(x, g, dy):
    r = lax.rsqrt(jnp.mean(x * x, axis=-1, keepdims=True) + EPS)
    xh = x * r
    dxh = dy * g
    dx = r * (dxh - xh * jnp.mean(dxh * xh, axis=-1, keepdims=True))
    dg = jnp.sum(dy * xh, axis=0, keepdims=True)
    return dx, dg


def _row_spec(width, tile=ROW_TILE):
    return pl.BlockSpec((tile, width), lambda i: (i, 0))


def _vec_spec(width):
    return pl.BlockSpec((1, width), lambda i: (0, 0))


def _norm_pre(hs, g, name):
    r, d = hs.shape

    def body(x_ref, g_ref, o_ref):
        o_ref[...] = _rms(x_ref[...], g_ref[...]).astype(BF16)

    return _pallas(body, name=name, grid=(r // ROW_TILE,), in_specs=[_row_spec(d), _vec_spec(d)],
                   out_specs=_row_spec(d), out_shape=jax.ShapeDtypeStruct((r, d), BF16),
                   compiler_params=_cparams(("parallel",)))(hs, g)


def _norm_post_pre(hs, m, g_post, g_pre, name):
    r, d = hs.shape

    def body(hs_ref, m_ref, gp_ref, gn_ref, o_ref, hn_ref):
        new = hs_ref[...] + _rms(m_ref[...], gp_ref[...])
        o_ref[...] = new
        hn_ref[...] = _rms(new, gn_ref[...]).astype(BF16)

    return _pallas(body, name=name, grid=(r // ROW_TILE,),
                   in_specs=[_row_spec(d), _row_spec(d), _vec_spec(d), _vec_spec(d)],
                   out_specs=[_row_spec(d), _row_spec(d)],
                   out_shape=[jax.ShapeDtypeStruct((r, d), F32), jax.ShapeDtypeStruct((r, d), BF16)],
                   compiler_params=_cparams(("parallel",)))(hs, m, g_post, g_pre)


def _norm_post_loss(hs, m, g_post, target, pad_tiles, name):
    r, d = hs.shape
    nt = r // ROW_TILE

    def body(hs_ref, m_ref, gp_ref, t_ref, dhs_ref, loss_ref):
        i = pl.program_id(0)
        new = hs_ref[...] + _rms(m_ref[...], gp_ref[...])
        live = (i >= pad_tiles).astype(F32)
        diff = (new - t_ref[...]) * live
        dhs_ref[...] = diff * (1.0 / d)
        loss_ref[...] = jnp.full((8, 128), 0.5 / d * jnp.sum(diff * diff), F32)

    dhs, parts = _pallas(
        body, name=name, grid=(nt,),
        in_specs=[_row_spec(d), _row_spec(d), _vec_spec(d),
                  pl.BlockSpec((ROW_TILE, d), lambda i: (jnp.maximum(i - pad_tiles, 0), 0))],
        out_specs=[_row_spec(d), pl.BlockSpec((8, 128), lambda i: (i, 0))],
        out_shape=[jax.ShapeDtypeStruct((r, d), F32), jax.ShapeDtypeStruct((nt * 8, 128), F32)],
        compiler_params=_cparams(("parallel",)))(hs, m, g_post, target)
    return dhs, jnp.sum(parts[::8, 0])


def _norm_bwd(dhs, *, pre=None, post=None, pad=0, dm_dtype=BF16, dhs_cols=None, name):
    r = dhs.shape[0]
    d = dhs.shape[1] if dhs_cols is None else dhs_cols[0]
    has_pre, has_post = pre is not None, post is not None

    def body(*refs):
        it = iter(refs)
        dhs_ref = next(it)
        if has_pre:
            hs_ref, gn_ref, dhn_ref = next(it), next(it), next(it)
        if has_post:
            m_ref, gp_ref = next(it), next(it)
        if has_pre:
            o_dhs, o_dgn = next(it), next(it)
        if has_post:
            o_dm, o_dgp = next(it), next(it)
        i = pl.program_id(0)
        live = (i * ROW_TILE + lax.broadcasted_iota(jnp.int32, (ROW_TILE, 1), 0)) >= pad
        cur = jnp.where(live, dhs_ref[...], 0.0)
        if has_pre:
            dx, dg = _rms_bwd(hs_ref[...], gn_ref[...], jnp.where(live, dhn_ref[...].astype(F32), 0.0))
            cur = cur + dx
            o_dhs[...] = cur

            @pl.when(i == 0)
            def _():
                o_dgn[...] = jnp.zeros_like(o_dgn)
            o_dgn[...] += dg
        if has_post:
            dm, dg = _rms_bwd(m_ref[...], gp_ref[...], cur)
            o_dm[...] = dm.astype(o_dm.dtype)

            @pl.when(i == 0)
            def _():
                o_dgp[...] = jnp.zeros_like(o_dgp)
            o_dgp[...] += dg

    dhs_spec = _row_spec(d) if dhs_cols is None else pl.BlockSpec((ROW_TILE, d), lambda i: (i, dhs_cols[1]))
    ins, in_specs, out_specs, out_shape = [dhs], [dhs_spec], [], []
    if has_pre:
        ins += list(pre)
        in_specs += [_row_spec(d), _vec_spec(d), _row_spec(d)]
        out_specs += [_row_spec(d), _vec_spec(d)]
        out_shape += [jax.ShapeDtypeStruct((r, d), F32), jax.ShapeDtypeStruct((1, d), F32)]
    if has_post:
        ins += list(post)
        in_specs += [_row_spec(d), _vec_spec(d)]
        out_specs += [_row_spec(d), _vec_spec(d)]
        out_shape += [jax.ShapeDtypeStruct((r, d), dm_dtype), jax.ShapeDtypeStruct((1, d), F32)]
    outs = list(_pallas(body, name=name, grid=(r // ROW_TILE,), in_specs=in_specs, out_specs=out_specs,
                        out_shape=out_shape, compiler_params=_cparams(("arbitrary",)))(*ins))
    dhs_new, dgn = (outs.pop(0), outs.pop(0)) if has_pre else (dhs, None)
    dm, dgp = (outs.pop(0), outs.pop(0)) if has_post else (None, None)
    return dhs_new, dm, dgn, dgp


def _softplus(z):
    return jnp.maximum(z, 0.0) + jnp.log(1.0 + jnp.exp(-jnp.abs(z)))


def _sb_consts(t):
    row = lax.broadcasted_iota(jnp.int32, (t, t), 0)
    col = lax.broadcasted_iota(jnp.int32, (t, t), 1)
    m_up = (col >= row).astype(BF16)
    m_low = (col <= row).astype(BF16)
    return m_up, m_low


def _emit_chains(chains, stages, skew):
    if skew:
        for step in range(len(chains) + len(stages) - 1):
            for si, stage in enumerate(stages):
                if 0 <= step - si < len(chains):
                    stage(chains[step - si])
    else:
        for stage in stages:
            for c in chains:
                stage(c)


def _sb_fwd(q, k, vt3, pad, name, ride=((), ())):
    r = q.shape[0]
    t = ATT_BLK
    nb = r // t
    nbp = -(-(nb + 1) // 8) * 8
    jmin = pad // t
    scale = SB_HEAD_DIM ** -0.5
    n_ride = len(ride[0])

    def body(q_ref, k_ref, vt_ref, *rest):
        ride_in, (o_ref, ss_ref), ride_out = rest[:n_ride], rest[n_ride:n_ride + 2], rest[n_ride + 2:2 * n_ride + 2]
        acc_ref, kn_ref = rest[2 * n_ride + 2:2 * n_ride + 4]
        ride_sems = rest[2 * n_ride + 4:]
        i = pl.program_id(1)
        if n_ride:
            @pl.when((pl.program_id(0) == 0) & (i == 0))
            def _():
                for cp in _exchange_copies(ride_in, ride_out, ride[1], *ride_sems):
                    cp.start()

        @pl.when(i == 0)
        def _():
            def blk(b, m):
                kb = k_ref[pl.ds(pl.multiple_of(b * t, t), t), :].astype(F32)
                return jnp.maximum(m, jnp.max(jnp.sum(kb * kb, axis=1, keepdims=True), axis=0, keepdims=True))
            kn_ref[...] = jnp.broadcast_to(lax.fori_loop(0, nb, blk, jnp.zeros((1, 1), F32)), (8, 128))

        qf = q_ref[...].astype(F32)
        z_bound = scale * jnp.sqrt(jnp.max(jnp.sum(qf * qf, axis=1, keepdims=True)) * jnp.max(kn_ref[...]))

        def need(carry):
            return jnp.maximum(jnp.max(carry[0]), jnp.max(carry[1])) + z_bound >= SB_LOG_ZERO

        qt = qf.T
        sub = lax.broadcasted_iota(jnp.int32, (128, 1), 0)
        m_up, _ = _sb_consts(t)
        kpos0 = lax.broadcasted_iota(jnp.int32, (t, 1), 0)
        qpos = i * t + lax.broadcasted_iota(jnp.int32, (1, t), 1)
        n_mid = jnp.maximum(i - 1 - jmin, 0)
        n_edge = jnp.where(i > jmin, 1, 0)
        qths = [jnp.where((sub >= 64 * h) & (sub < 64 * (h + 1)), qt * scale, 0.0).astype(BF16) for h in range(2)]
        acc_ref[...] = jnp.zeros_like(acc_ref)

        def sweep(js, carry, masked):
            kbs = [k_ref[pl.ds(pl.multiple_of(j * t, t), t), :] for j in js]
            vts = [vt_ref[0, j] for j in js]
            accs = [acc_ref[0], acc_ref[1]]
            s = list(carry)
            chains = [(n, h) for n in range(len(js)) for h in range(2)]
            masked = [masked] * len(js) if isinstance(masked, bool) else masked
            valid = [(js[n] * t + kpos0 < qpos) & (js[n] * t + kpos0 >= pad) if masked[n] else None for n in range(len(js))]
            zt, inc, saves = {}, {}, []

            def st_scores(c):
                zt[c] = _dot(kbs[c[0]], qths[c[1]])

            def st_cumsum(c):
                lk = -_softplus(zt[c])
                if masked[c[0]]:
                    lk = jnp.where(valid[c[0]], lk, 0.0)
                inc[c] = _split_dot(m_up, lk)

            def st_weights(c):
                n, h = c
                saves.append((h, js[n], s[h]))
                w = jnp.exp(zt[c] + inc[c] + s[h])
                if masked[n]:
                    w = jnp.where(valid[n], w, 0.0)
                accs[h] = accs[h] + _dot(vts[n], w.astype(BF16))
                s[h] = s[h] + inc[c][0:1, :]

            _emit_chains(chains, [st_scores, st_cumsum, st_weights], SB_FWD_SKEW)
            for h, j, val in saves:
                ss_ref[h, 0, pl.ds(j, 1), :] = val
            acc_ref[0] = accs[0]
            acc_ref[1] = accs[1]
            return tuple(s)

        zero = jnp.zeros((1, t), F32)
        bpi = SB_BLOCKS_PER_TRIP
        j, carry = lax.cond(
            i - 1 > jmin,
            lambda: (i - 2, sweep([i, i - 1], (zero, zero), [True, False])),
            lambda: (i - 1, sweep([i], (zero, zero), True)))
        j, carry = lax.while_loop(
            lambda st: (st[0] - bpi >= jmin) & need(st[1]),
            lambda st: (st[0] - bpi, sweep([st[0] - b for b in range(bpi)], st[1], False)), (j, carry))
        j, carry = lax.while_loop(
            lambda st: (st[0] > jmin) & need(st[1]),
            lambda st: (st[0] - 1, sweep([st[0]], st[1], False)), (j, carry))
        j, carry = lax.while_loop(
            lambda st: (st[0] == jmin) & (i > jmin) & need(st[1]),
            lambda st: (st[0] - 1, sweep([st[0]], st[1], True)), (j, carry))
        first = jnp.full((1, t), j + 1, jnp.int32).astype(F32)
        ss_ref[0, 0, nbp - 1:nbp, :] = first
        ss_ref[1, 0, nbp - 1:nbp, :] = first
        acc = jnp.where(sub < 64, acc_ref[0], acc_ref[1])
        o_ref[...] = acc.T
        if n_ride:
            @pl.when((pl.program_id(0) == 3) & (i == nb - 1))
            def _():
                for cp in _exchange_copies(ride_in, ride_out, ride[1], *ride_sems):
                    cp.wait()

    hbm = pl.BlockSpec(memory_space=pl.ANY)
    outs = _pallas(
        body, name=name, grid=(4, nb),
        in_specs=[pl.BlockSpec((t, 128), lambda hp, i: (i, hp)),
                  pl.BlockSpec((r, 128), lambda hp, i: (0, hp)),
                  pl.BlockSpec((1, nb, 128, t), lambda hp, i: (hp, 0, 0, 0))] + [hbm] * n_ride,
        out_specs=[pl.BlockSpec((t, 128), lambda hp, i: (i, hp)),
                   pl.BlockSpec((2, 1, nbp, t), lambda hp, i: (hp, i, 0, 0))] + [hbm] * n_ride,
        out_shape=[jax.ShapeDtypeStruct((r, SB_WIDTH), F32),
                   jax.ShapeDtypeStruct((8, nb, nbp, t), F32)] + _exchange_shapes(*ride),
        scratch_shapes=[pltpu.VMEM((2, 128, t), F32), pltpu.VMEM((8, 128), F32)] + (_exchange_sems(n_ride) if n_ride else []),
        compiler_params=_cparams(("arbitrary", "arbitrary")),
    )(q, k, vt3, *ride[0])
    return outs[0], outs[1], list(outs[2:])


def _sb_bwd(q, k, v, kt3, ssave, do, pad, name):
    r = q.shape[0]
    t = ATT_BLK
    nb = r // t
    nbp = ssave.shape[2]
    jmin = pad // t
    scale = SB_HEAD_DIM ** -0.5

    def body(q_ref, do_ref, k_ref, v_ref, kt_ref, ss_ref, dq_ref, dk_hbm, dv_hbm, dk_acc, dv_acc, dq_acc, sem):
        hp = pl.program_id(0)
        i = pl.program_id(1)

        @pl.when(i == 0)
        def _():
            dk_acc[...] = jnp.zeros_like(dk_acc)
            dv_acc[...] = jnp.zeros_like(dv_acc)

        qf = q_ref[...].astype(F32)
        dof = do_ref[...]
        qt = qf.T
        dot_ = dof.T
        sub = lax.broadcasted_iota(jnp.int32, (128, 1), 0)
        lane = lax.broadcasted_iota(jnp.int32, (1, 128), 1)
        m_up, m_low = _sb_consts(t)
        kpos0 = lax.broadcasted_iota(jnp.int32, (t, 1), 0)
        qpos = i * t + lax.broadcasted_iota(jnp.int32, (1, t), 1)
        first = jnp.clip(jnp.max(ss_ref[0, 0, nbp - 1:nbp, :]).astype(jnp.int32), jmin, i)
        mid0 = jnp.maximum(first, jmin + 1)
        pair = i - mid0 >= 1
        n_mid = jnp.maximum(i - mid0 - 1, 0)
        n_edge = jnp.where((i > jmin) & (first == jmin), 1, 0)
        in_t = [(sub >= 64 * h) & (sub < 64 * (h + 1)) for h in range(2)]
        in_l = [(lane >= 64 * h) & (lane < 64 * (h + 1)) for h in range(2)]
        qths = [jnp.where(in_t[h], qt * scale, 0.0).astype(BF16) for h in range(2)]
        doths = [jnp.where(in_t[h], dot_, 0.0).astype(BF16) for h in range(2)]
        qhs = [jnp.where(in_l[h], qf * scale, 0.0).astype(BF16) for h in range(2)]
        dohs = [jnp.where(in_l[h], dof, 0.0).astype(BF16) for h in range(2)]
        dq_acc[...] = jnp.zeros_like(dq_acc)

        def sweep(js, carry, masked):
            rows = [pl.ds(pl.multiple_of(j * t, t), t) for j in js]
            kbs = [k_ref[rw, :] for rw in rows]
            vbs = [v_ref[rw, :] for rw in rows]
            kts = [kt_ref[0, j] for j in js]
            sss = [[ss_ref[h, 0, pl.ds(j, 1), :] for h in range(2)] for j in js]
            dv_old = [dv_acc[rw, :] for rw in rows]
            dk_old = [dk_acc[rw, :] for rw in rows]
            dqs = [dq_acc[0], dq_acc[1]]
            ec = list(carry)
            chains = [(n, h) for n in range(len(js)) for h in range(2)]
            masked = [masked] * len(js) if isinstance(masked, bool) else masked
            valid = [(js[n] * t + kpos0 < qpos) & (js[n] * t + kpos0 >= pad) if masked[n] else None for n in range(len(js))]
            zt, dvt, sp, inc, e, big_e = {}, {}, {}, {}, {}, {}

            def st_scores(c):
                zt[c] = _dot(kbs[c[0]], qths[c[1]])
                dvt[c] = _dot(vbs[c[0]], doths[c[1]])

            def st_cumsum(c):
                sp[c] = _softplus(zt[c])
                lk = -sp[c]
                if masked[c[0]]:
                    lk = jnp.where(valid[c[0]], lk, 0.0)
                inc[c] = _split_dot(m_up, lk)

            def st_weights(c):
                n, h = c
                w = jnp.exp(zt[c] + inc[c] + sss[n][h])
                if masked[n]:
                    w = jnp.where(valid[n], w, 0.0)
                dv_old[n] = dv_old[n] + _dot(w.astype(BF16), dohs[h])
                e[c] = w * dvt[c]
                pinc = _split_dot(m_low, e[c])
                big_e[c] = pinc - e[c] + ec[h]
                ec[h] = ec[h] + pinc[t - 1:t, :]

            def st_dscores(c):
                n, h = c
                dz = e[c] - jnp.exp(zt[c] - sp[c]) * (e[c] + big_e[c])
                if masked[n]:
                    dz = jnp.where(valid[n], dz, 0.0)
                dzb = dz.astype(BF16)
                dqs[h] = dqs[h] + _dot(kts[n], dzb)
                dk_old[n] = dk_old[n] + _dot(dzb, qhs[h])

            _emit_chains(chains, [st_scores, st_cumsum, st_weights, st_dscores], SB_BWD_SKEW)
            for n, rw in enumerate(rows):
                dv_acc[rw, :] = dv_old[n]
                dk_acc[rw, :] = dk_old[n]
            dq_acc[0] = dqs[0]
            dq_acc[1] = dqs[1]
            return tuple(ec)

        zero = jnp.zeros((1, t), F32)
        bpi = SB_BLOCKS_PER_TRIP
        carry = lax.fori_loop(0, n_edge, lambda it, c: sweep([jmin + it * 0], c, True), (zero, zero))
        carry = lax.fori_loop(0, n_mid // bpi, lambda it, c: sweep([mid0 + bpi * it + b for b in range(bpi)], c, False), carry)
        n_rem = n_mid % bpi
        carry = lax.fori_loop(0, n_rem, lambda it, c: sweep([i - 1 - n_rem + it], c, False), carry)
        lax.cond(pair, lambda: sweep([i - 1, i], carry, [False, True]), lambda: sweep([i], carry, True))
        dq_ref[...] = (jnp.where(sub < 64, dq_acc[0], dq_acc[1]) * scale).T

        @pl.when(i == nb - 1)
        def _():
            c1 = pltpu.make_async_copy(dk_acc, dk_hbm.at[hp], sem.at[0])
            c2 = pltpu.make_async_copy(dv_acc, dv_hbm.at[hp], sem.at[1])
            c1.start()
            c2.start()
            c1.wait()
            c2.wait()

    return _pallas(
        body, name=name, grid=(4, nb),
        in_specs=[pl.BlockSpec((t, 128), lambda hp, i: (i, hp)),
                  pl.BlockSpec((t, 128), lambda hp, i: (i, hp)),
                  pl.BlockSpec((r, 128), lambda hp, i: (0, hp)),
                  pl.BlockSpec((r, 128), lambda hp, i: (0, hp)),
                  pl.BlockSpec((1, nb, 128, t), lambda hp, i: (hp, 0, 0, 0)),
                  pl.BlockSpec((2, 1, nbp, t), lambda hp, i: (hp, i, 0, 0))],
        out_specs=[pl.BlockSpec((t, 128), lambda hp, i: (i, hp)),
                   pl.BlockSpec(memory_space=pl.ANY), pl.BlockSpec(memory_space=pl.ANY)],
        out_shape=[jax.ShapeDtypeStruct((r, SB_WIDTH), F32),
                   jax.ShapeDtypeStruct((4, r, 128), F32), jax.ShapeDtypeStruct((4, r, 128), F32)],
        scratch_shapes=[pltpu.VMEM((r, 128), F32), pltpu.VMEM((r, 128), F32), pltpu.VMEM((2, 128, t), F32),
                        pltpu.SemaphoreType.DMA((2,))],
        compiler_params=_cparams(("arbitrary", "arbitrary")),
    )(q, do, k, v, kt3, ssave)


def _s5_disc(lam_re, lam_im, logdt, btr, bti):
    lr = jnp.minimum(lam_re, -1e-4)
    li = lam_im
    dt = jnp.exp(logdt)
    mag = jnp.exp(lr * dt)
    ang = li * dt
    a_re, a_im = mag * jnp.cos(ang), mag * jnp.sin(ang)
    den = lr * lr + li * li
    nr, ni = a_re - 1.0, a_im
    c_re = (nr * lr + ni * li) / den
    c_im = (ni * lr - nr * li) / den
    return a_re, a_im, c_re * btr - c_im * bti, c_re * bti + c_im * btr


def _s5_prep(lam_re, lam_im, logdt, btr, bti, name):
    ns = lam_re.shape[1]

    def body(lr_ref, li_ref, dt_ref, br_ref, bi_ref, ar_ref, ai_ref, bbr_ref, bbi_ref):
        ar, ai, bbr, bbi = _s5_disc(lr_ref[...], li_ref[...], dt_ref[...], br_ref[...], bi_ref[...])
        ar_ref[...] = ar
        ai_ref[...] = ai
        bbr_ref[...] = bbr
        bbi_ref[...] = bbi

    return _pallas(body, name=name,
                   out_shape=[jax.ShapeDtypeStruct((1, ns), F32)] * 2 + [jax.ShapeDtypeStruct((S5_GROUP, ns), F32)] * 2,
                   )(lam_re, lam_im, logdt, btr, bti)


def _s5_prep_bwd(lam_re, lam_im, logdt, btr, bti, dar, dai, dbbr, dbbi, name):
    ns = lam_re.shape[1]

    def body(lr_ref, li_ref, dt_ref, br_ref, bi_ref, dar_ref, dai_ref, dbr_ref, dbi_ref, o_lr, o_li, o_dt, o_br, o_bi):
        _, vjp = jax.vjp(_s5_disc, lr_ref[...], li_ref[...], dt_ref[...], br_ref[...], bi_ref[...])
        g = vjp((dar_ref[...], dai_ref[...], dbr_ref[...], dbi_ref[...]))
        o_lr[...] = g[0]
        o_li[...] = g[1]
        row = lax.broadcasted_iota(jnp.int32, (ns, ns), 0) // S5_STATE
        col = lax.broadcasted_iota(jnp.int32, (ns, ns), 1) // S5_STATE
        same = (row == col).astype(F32)
        o_dt[...] = _dot_hi(jnp.broadcast_to(g[2], (8, ns)), same)[0:1]
        o_br[...] = g[3]
        o_bi[...] = g[4]

    return _pallas(body, name=name,
                   out_shape=[jax.ShapeDtypeStruct((1, ns), F32)] * 3 + [jax.ShapeDtypeStruct((S5_GROUP, ns), F32)] * 2,
                   compiler_params=pltpu.CompilerParams(vmem_limit_bytes=VMEM_LIMIT),
                   )(lam_re, lam_im, logdt, btr, bti, dar, dai, dbbr, dbbi)


def _s5_scan(br, bi, ar, ai, t, reverse=False, carry=None):
    ng = t // 8
    ns = br.shape[1]
    br, bi = br.reshape(ng, 8, ns), bi.reshape(ng, 8, ns)
    row8 = lax.broadcasted_iota(jnp.int32, (1, 8, 1), 1)
    pr, pi_ = ar, ai
    for k in (1, 2, 4):
        if reverse:
            sr, si, ok = pltpu.roll(br, 8 - k, 1), pltpu.roll(bi, 8 - k, 1), row8 < 8 - k
        else:
            sr, si, ok = pltpu.roll(br, k, 1), pltpu.roll(bi, k, 1), row8 >= k
        sr = jnp.where(ok, sr, 0.0)
        si = jnp.where(ok, si, 0.0)
        br, bi = br + pr * sr - pi_ * si, bi + pr * si + pi_ * sr
        pr, pi_ = pr * pr - pi_ * pi_, 2.0 * pr * pi_
    pw_r, pw_i = [ar], [ai]
    for _ in range(7):
        pw_r.append(pw_r[-1] * ar - pw_i[-1] * ai)
        pw_i.append(pw_r[-2] * ai + pw_i[-1] * ar)
    if reverse:
        pw_r.reverse()
        pw_i.reverse()
    p8r, p8i = jnp.concatenate(pw_r, axis=0), jnp.concatenate(pw_i, axis=0)
    out_r, out_i = [None] * ng, [None] * ng
    order = range(ng - 1, -1, -1) if reverse else range(ng)
    edge = 0 if reverse else 7
    for g in order:
        gr, gi = br[g], bi[g]
        if carry is not None:
            cr, ci = carry
            gr, gi = gr + p8r * cr - p8i * ci, gi + p8r * ci + p8i * cr
        out_r[g], out_i[g] = gr, gi
        carry = (gr[edge:edge + 1], gi[edge:edge + 1])
    return jnp.concatenate(out_r, axis=0), jnp.concatenate(out_i, axis=0)


def _s5_prev_rows(x, first, t):
    ng = t // 8
    ns = x.shape[1]
    x3 = x.reshape(ng, 8, ns)
    last = x3[:, 7:8, :]
    before = jnp.concatenate([first.reshape(1, 1, ns), last[:ng - 1]], axis=0)
    row8 = lax.broadcasted_iota(jnp.int32, (1, 8, 1), 1)
    return jnp.where(row8 == 0, before, pltpu.roll(x3, 1, 1)).reshape(t, ns)


_GELU_C = math.sqrt(2.0 / math.pi)


def _gelu(y):
    th = jnp.tanh(_GELU_C * (y + 0.044715 * y * y * y))
    return 0.5 * y * (1.0 + th), th


def _sigmoid(x):
    return 1.0 / (1.0 + jnp.exp(-x))


def _s5_fwd(u, wb, a, wc, dskip, wglu, bglu, gnorm, name):
    r = u.shape[0]
    t = S5_TILE
    nt = r // t
    ns = wb.shape[2]
    w = S5_WIDTH

    def body(u_ref, wb_ref, a_ref, wc_ref, d_ref, wg_ref, bg_ref, gn_ref, y_ref, on_ref, xs_ref, carry_ref):
        i = pl.program_id(0)
        ar, ai = a_ref[0], a_ref[1]

        @pl.when(i == 0)
        def _():
            carry_ref[...] = jnp.zeros_like(carry_ref)

        u_ = u_ref[...]
        ub = u_.astype(BF16)
        xs_ref[0] = carry_ref[:, 0, :]
        chunks = list(range(S5_CHUNKS))
        sl_s = [slice(c * (ns // S5_CHUNKS), (c + 1) * (ns // S5_CHUNKS)) for c in chunks]
        sl_u = [slice(c * (w // S5_CHUNKS), (c + 1) * (w // S5_CHUNKS)) for c in chunks]
        bu, xs, ys = {}, {}, {}

        def st_inputs(c):
            bu[c] = (_dot(ub[:, sl_u[c]], wb_ref[0, sl_u[c], sl_s[c]]), _dot(ub[:, sl_u[c]], wb_ref[1, sl_u[c], sl_s[c]]))

        def st_scan(c):
            xr, xi = _s5_scan(*bu[c], ar[:, sl_s[c]], ai[:, sl_s[c]], t, carry=(carry_ref[0, :, sl_s[c]], carry_ref[1, :, sl_s[c]]))
            carry_ref[0, :, sl_s[c]] = xr[t - 1:t, :]
            carry_ref[1, :, sl_s[c]] = xi[t - 1:t, :]
            xs[c] = (xr.astype(BF16), xi.astype(BF16))

        def st_outputs(c):
            ys[c] = _dot(xs[c][0], wc_ref[0, sl_s[c], sl_u[c]]) - _dot(xs[c][1], wc_ref[1, sl_s[c], sl_u[c]])

        _emit_chains(chunks, [st_inputs, st_scan, st_outputs], False)
        y = jnp.concatenate([ys[c] for c in chunks], axis=1) + d_ref[...] * u_
        h, _ = _gelu(y)
        gate = _sigmoid(_dot(h.astype(BF16), wg_ref[...]) + bg_ref[...])
        y_ref[...] = y
        on_ref[...] = _rms(h * gate, gn_ref[...]).astype(BF16)

    full = lambda shape: pl.BlockSpec(shape, lambda i: (0,) * len(shape))
    return _pallas(
        body, name=name, grid=(nt,),
        in_specs=[_row_spec(w, t), full((2, w, ns)), full((2, 1, ns)), full((2, ns, w)), full((1, w)),
                  full((w, w)), full((1, w)), full((1, w))],
        out_specs=[_row_spec(w, t), _row_spec(w, t), pl.BlockSpec((1, 2, ns), lambda i: (i, 0, 0))],
        out_shape=[jax.ShapeDtypeStruct((r, w), F32), jax.ShapeDtypeStruct((r, w), BF16),
                   jax.ShapeDtypeStruct((nt, 2, ns), F32)],
        scratch_shapes=[pltpu.VMEM((2, 1, ns), F32)],
        compiler_params=_cparams(("arbitrary",)),
    )(u, wb, a, wc, dskip, wglu, bglu, gnorm)


def _s5_bwd(u, y, don, xstart, wb, a, wc, dskip, wglu, bglu, gnorm, name, ride=((), ()), don_block=0):
    r = u.shape[0]
    t = S5_TILE
    nt = r // t
    ns = wb.shape[2]
    w = S5_WIDTH
    nt_dims = ((1,), (1,))
    tn_dims = ((0,), (0,))

    def body(u_ref, y_ref, don_ref, xs_ref, wb_hbm, a_ref, wc_hbm, d_ref, wg_ref, bg_ref, gn_ref,
             du_ref, da_ref, dd_ref, dbg_ref, dgn_ref, dwb_hbm, dwc_hbm, dwg_hbm,
             wb_ref, wc_ref, lam_ref, acc_wb, acc_wc, acc_wg, sem):
        i = pl.program_id(0)
        ar, ai = a_ref[0], a_ref[1]

        @pl.when(i == 0)
        def _():
            c1 = pltpu.make_async_copy(wb_hbm, wb_ref, sem.at[0])
            c2 = pltpu.make_async_copy(wc_hbm, wc_ref, sem.at[1])
            c1.start()
            c2.start()
            lam_ref[...] = jnp.zeros_like(lam_ref)
            acc_wb[...] = jnp.zeros_like(acc_wb)
            acc_wc[...] = jnp.zeros_like(acc_wc)
            acc_wg[...] = jnp.zeros_like(acc_wg)
            da_ref[...] = jnp.zeros_like(da_ref)
            dd_ref[...] = jnp.zeros_like(dd_ref)
            dbg_ref[...] = jnp.zeros_like(dbg_ref)
            dgn_ref[...] = jnp.zeros_like(dgn_ref)
            c1.wait()
            c2.wait()

        u_ = u_ref[...]
        y_ = y_ref[...]
        ub = u_.astype(BF16)
        h, th = _gelu(y_)
        hb = h.astype(BF16)
        wg = wg_ref[...]
        gate = _sigmoid(_dot(hb, wg) + bg_ref[...])
        d_out, dgn = _rms_bwd(h * gate, gn_ref[...], don_ref[...])
        dgn_ref[...] += dgn
        dhw = d_out * h * gate * (1.0 - gate)
        dhwb = dhw.astype(BF16)
        dh = d_out * gate + _dot(dhwb, wg, nt_dims)
        acc_wg[...] += _dot(hb, dhwb, tn_dims)
        dbg_ref[...] += jnp.sum(dhw, axis=0, keepdims=True)
        dgelu = 0.5 * (1.0 + th) + 0.5 * y_ * (1.0 - th * th) * _GELU_C * (1.0 + 3.0 * 0.044715 * y_ * y_)
        dy = dh * dgelu
        dd_ref[...] += jnp.sum(dy * u_, axis=0, keepdims=True)
        dyb = dy.astype(BF16)
        chunks = list(range(S5_CHUNKS))
        sl_s = [slice(c * (ns // S5_CHUNKS), (c + 1) * (ns // S5_CHUNKS)) for c in chunks]
        sl_u = [slice(c * (w // S5_CHUNKS), (c + 1) * (w // S5_CHUNKS)) for c in chunks]
        bu, gx, x_, lam, dus = {}, {}, {}, {}, {}

        def st_inputs(c):
            su, ss = sl_u[c], sl_s[c]
            bu[c] = (_dot(ub[:, su], wb_ref[0, su, ss]), _dot(ub[:, su], wb_ref[1, su, ss]))
            gx[c] = (_dot(dyb[:, su], wc_ref[0, ss, su], nt_dims), -_dot(dyb[:, su], wc_ref[1, ss, su], nt_dims))

        def st_states(c):
            su, ss = sl_u[c], sl_s[c]
            first = (xs_ref[0, 0:1, ss], xs_ref[0, 1:2, ss])
            xr, xi = _s5_scan(*bu[c], ar[:, ss], ai[:, ss], t, carry=first)
            acc_wc[0, ss, su] += _dot(xr.astype(BF16), dyb[:, su], tn_dims)
            acc_wc[1, ss, su] -= _dot(xi.astype(BF16), dyb[:, su], tn_dims)
            x_[c] = (_s5_prev_rows(xr, first[0], t), _s5_prev_rows(xi, first[1], t))

        def st_adjoint(c):
            su, ss = sl_u[c], sl_s[c]
            lr, li = _s5_scan(*gx[c], ar[:, ss], -ai[:, ss], t, reverse=True, carry=(lam_ref[0, :, ss], lam_ref[1, :, ss]))
            lam_ref[0, :, ss] = lr[0:1, :]
            lam_ref[1, :, ss] = li[0:1, :]
            lrb, lib = lr.astype(BF16), li.astype(BF16)
            acc_wb[0, su, ss] += _dot(ub[:, su], lrb, tn_dims)
            acc_wb[1, su, ss] += _dot(ub[:, su], lib, tn_dims)
            dus[c] = _dot(lrb, wb_ref[0, su, ss], nt_dims) + _dot(lib, wb_ref[1, su, ss], nt_dims)
            lam[c] = (lr, li)

        def st_decay(c):
            ss = sl_s[c]
            (lr, li), (xpr, xpi) = lam[c], x_[c]
            da_ref[0, :, ss] += jnp.sum(lr * xpr + li * xpi, axis=0, keepdims=True)
            da_ref[1, :, ss] += jnp.sum(li * xpr - lr * xpi, axis=0, keepdims=True)

        _emit_chains(chunks, [st_inputs, st_states, st_adjoint, st_decay], False)
        du_ref[...] = d_ref[...] * dy + jnp.concatenate([dus[c] for c in chunks], axis=1)

        @pl.when(i == nt - 1)
        def _():
            cps = [pltpu.make_async_copy(acc_wb, dwb_hbm, sem.at[0]), pltpu.make_async_copy(acc_wc, dwc_hbm, sem.at[1]),
                   pltpu.make_async_copy(acc_wg, dwg_hbm, sem.at[2])]
            for c in cps:
                c.start()
            for c in cps:
                c.wait()

    n_ride = len(ride[0])
    n_in, n_out, n_scratch = 11, 8, 7

    def body_with_ride(*refs):
        ins, rest = refs[:n_in], refs[n_in:]
        ride_in, rest = rest[:n_ride], rest[n_ride:]
        outs, rest = rest[:n_out], rest[n_out:]
        ride_out, rest = rest[:n_ride], rest[n_ride:]
        scratch, ride_sems = rest[:n_scratch], rest[n_scratch:]
        if n_ride:
            @pl.when(pl.program_id(0) == 0)
            def _():
                for cp in _exchange_copies(ride_in, ride_out, ride[1], *ride_sems):
                    cp.start()
        body(*ins, *outs, *scratch)
        if n_ride:
            @pl.when(pl.program_id(0) == nt - 1)
            def _():
                for cp in _exchange_copies(ride_in, ride_out, ride[1], *ride_sems):
                    cp.wait()

    rev = lambda i: (nt - 1 - i, 0)
    full = lambda shape: pl.BlockSpec(shape, lambda i: (0,) * len(shape))
    hbm = pl.BlockSpec(memory_space=pl.ANY)
    outs = _pallas(
        body_with_ride, name=name, grid=(nt,),
        in_specs=[pl.BlockSpec((t, w), rev), pl.BlockSpec((t, w), rev), pl.BlockSpec((t, w), lambda i: (nt - 1 - i, don_block)),
                  pl.BlockSpec((1, 2, ns), lambda i: (nt - 1 - i, 0, 0)), hbm, full((2, 1, ns)), hbm, full((1, w)),
                  full((w, w)), full((1, w)), full((1, w))] + [hbm] * n_ride,
        out_specs=[pl.BlockSpec((t, w), rev), full((2, 1, ns)), full((1, w)), full((1, w)), full((1, w)), hbm, hbm, hbm]
        + [hbm] * n_ride,
        out_shape=[jax.ShapeDtypeStruct((r, w), F32), jax.ShapeDtypeStruct((2, 1, ns), F32)]
        + [jax.ShapeDtypeStruct((1, w), F32)] * 3
        + [jax.ShapeDtypeStruct((2, w, ns), F32), jax.ShapeDtypeStruct((2, ns, w), F32), jax.ShapeDtypeStruct((w, w), F32)]
        + _exchange_shapes(*ride),
        scratch_shapes=[pltpu.VMEM((2, w, ns), BF16), pltpu.VMEM((2, ns, w), BF16), pltpu.VMEM((2, 1, ns), F32),
                        pltpu.VMEM((2, w, ns), F32), pltpu.VMEM((2, ns, w), F32), pltpu.VMEM((w, w), F32),
                        pltpu.SemaphoreType.DMA((3,))] + (_exchange_sems(n_ride) if n_ride else []),
        compiler_params=_cparams(("arbitrary",)),
    )(u, y, don, xstart, wb, a, wc, dskip, wglu, bglu, gnorm, *ride[0])
    return tuple(outs[:n_out]) + (list(outs[n_out:]),)


def _s5_expand(lam_re, lam_im, log_dt, b_re, b_im, c_re, c_im):
    g, n, p = S5_GROUPS, S5_STATE, S5_GROUP
    ns = g * n
    rows = lambda x: x.reshape(1, ns)
    logdt = jnp.repeat(log_dt.reshape(g), n).reshape(1, ns)
    btr = b_re.reshape(ns, p).T
    bti = b_im.reshape(ns, p).T
    ctr = c_re.transpose(0, 2, 1).reshape(ns, p)
    cti = c_im.transpose(0, 2, 1).reshape(ns, p)
    mask = (jnp.arange(g * p)[:, None] // p) == (jnp.arange(ns)[None, :] // n)
    return rows(lam_re), rows(lam_im), logdt, btr, bti, ctr, cti, mask


def _s5_block_diag_b(bb, mask):
    return jnp.where(mask, jnp.tile(bb, (S5_GROUPS, 1)), 0.0)


def _s5_block_diag_c(ct, mask):
    return jnp.where(mask.T, jnp.tile(ct, (1, S5_GROUPS)), 0.0)


def _s5_diag_of_b(dwb, mask):
    return jnp.where(mask, dwb, 0.0).reshape(S5_GROUPS, S5_GROUP, -1).sum(0)


def _s5_diag_of_c(dwc, mask):
    ns = dwc.shape[0]
    return jnp.where(mask.T, dwc, 0.0).reshape(ns, S5_GROUPS, S5_GROUP).sum(1)


DN_PRE_TILE = 256
_DN_QKV = 3 * DN_WIDTH


def _halo_specs(width, tile, nt, prev):
    per = tile // 8
    if prev:
        return pl.BlockSpec((8, width), lambda i: (jnp.maximum(i * per - 1, 0), 0))
    return pl.BlockSpec((8, width), lambda i: (jnp.minimum((i + 1) * per, nt * per - 1), 0))


def _shift_down(x, halo, s, t):
    xx = jnp.concatenate([halo, x], axis=0)
    return pltpu.roll(xx, s, 0)[8:]


def _shift_up(x, halo, s, t):
    xx = jnp.concatenate([x, halo], axis=0)
    return pltpu.roll(xx, t + 8 - s, 0)[:t]


def _silu(x):
    s = _sigmoid(x)
    return x * s, s


def _dn_gates(ab, alog, dtb, live):
    lane = lax.broadcasted_iota(jnp.int32, (1, 128), 1)
    g = -jnp.exp(alog) * _softplus(ab + dtb)
    beta = _sigmoid(ab)
    return jnp.where(live & (lane < DN_HEADS), g, jnp.where(live & (lane < 2 * DN_HEADS), beta, 0.0))


def _dn_pre_fwd(proj, ab, conv_w, alog, dtb, pad, name):
    r = proj.shape[0]
    t = DN_PRE_TILE
    nt = r // t
    scale = DN_HEAD_DIM ** -0.5

    def body(x_ref, halo_ref, ab_ref, w_ref, al_ref, dt_ref, q_ref, k_ref, v_ref, gb_ref):
        i = pl.program_id(0)
        act, _ = _silu(_dn_conv(x_ref[...], jnp.where(i > 0, halo_ref[...], 0.0), w_ref[...], t))
        for hd in range(DN_HEADS):
            sl = slice(hd * 128, (hd + 1) * 128)
            for base, o_ref, sc in ((0, q_ref, scale), (DN_WIDTH, k_ref, 1.0)):
                xh = act[:, base + hd * 128: base + (hd + 1) * 128]
                o_ref[:, sl] = (xh * (lax.rsqrt(jnp.sum(xh * xh, axis=-1, keepdims=True) + EPS) * sc)).astype(BF16)
        v_ref[...] = act[:, 2 * DN_WIDTH:].astype(BF16)
        rows = i * t + lax.broadcasted_iota(jnp.int32, (t, 1), 0)
        gb_ref[...] = _dn_gates(ab_ref[...], al_ref[...], dt_ref[...], rows >= pad)

    return _pallas(
        body, name=name, grid=(nt,),
        in_specs=[pl.BlockSpec((t, _DN_QKV), lambda i: (i, 0)), _halo_specs(_DN_QKV, t, nt, True), _row_spec(128, t),
                  pl.BlockSpec((DN_CONV, _DN_QKV), lambda i: (0, 0)), _vec_spec(128), _vec_spec(128)],
        out_specs=[_row_spec(DN_WIDTH, t), _row_spec(DN_WIDTH, t), _row_spec(DN_WIDTH, t), _row_spec(128, t)],
        out_shape=[jax.ShapeDtypeStruct((r, DN_WIDTH), BF16)] * 3 + [jax.ShapeDtypeStruct((r, 128), F32)],
        compiler_params=_cparams(("parallel",)),
    )(proj, proj, ab, conv_w, alog, dtb)


def _dn_conv(x, halo, w, t):
    co = w[DN_CONV - 1:DN_CONV] * x
    for tap in range(DN_CONV - 1):
        co = co + w[tap:tap + 1] * _shift_down(x, halo, DN_CONV - 1 - tap, t)
    return co


def _dn_pre_bwd(proj, conv_w, dq, dk, dv, dgb, ab, alog, dtb, pad, name):
    r = proj.shape[0]
    t = DN_PRE_TILE
    nt = r // t
    scale = DN_HEAD_DIM ** -0.5

    def body(x_ref, halo_ref, w_ref, dq_ref, dk_ref, dv_ref, dgb_ref, ab_ref, al_ref, dt_ref, dco_ref, dab_ref, dal_ref,
             ddt_ref):
        i = pl.program_id(0)

        @pl.when(i == 0)
        def _():
            dal_ref[...] = jnp.zeros_like(dal_ref)
            ddt_ref[...] = jnp.zeros_like(ddt_ref)

        co_ = _dn_conv(x_ref[...], jnp.where(i > 0, halo_ref[...], 0.0), w_ref[...], t)
        act, sg = _silu(co_)
        dsilu = sg * (1.0 + co_ * (1.0 - sg))
        for hd in range(DN_HEADS):
            sl = slice(hd * 128, (hd + 1) * 128)
            for base, d_ref, sc in ((0, dq_ref, scale), (DN_WIDTH, dk_ref, 1.0)):
                cs = slice(base + hd * 128, base + (hd + 1) * 128)
                xh = act[:, cs]
                rn = lax.rsqrt(jnp.sum(xh * xh, axis=-1, keepdims=True) + EPS)
                xhat = xh * rn
                dy = d_ref[:, sl]
                dx = (sc * rn) * (dy - xhat * jnp.sum(dy * xhat, axis=-1, keepdims=True))
                dco_ref[:, cs] = dx * dsilu[:, cs]
        dco_ref[:, 2 * DN_WIDTH:] = dv_ref[...] * dsilu[:, 2 * DN_WIDTH:]
        rows = i * t + lax.broadcasted_iota(jnp.int32, (t, 1), 0)
        live = rows >= pad
        lane = lax.broadcasted_iota(jnp.int32, (1, 128), 1)
        ab_ = ab_ref[...]
        dgb_ = dgb_ref[...]
        is_g = live & (lane < DN_HEADS)
        is_b = live & (lane >= DN_HEADS) & (lane < 2 * DN_HEADS)
        arg = ab_ + dt_ref[...]
        ea = jnp.exp(al_ref[...])
        da = jnp.where(is_g, -dgb_ * ea * _sigmoid(arg), 0.0)
        beta = _sigmoid(ab_)
        dab_ref[...] = (da + jnp.where(is_b, dgb_ * beta * (1.0 - beta), 0.0)).astype(BF16)
        ddt_ref[...] += jnp.sum(da, axis=0, keepdims=True)
        dal_ref[...] += jnp.sum(jnp.where(is_g, -dgb_ * ea * _softplus(arg), 0.0), axis=0, keepdims=True)

    return _pallas(
        body, name=name, grid=(nt,),
        in_specs=[pl.BlockSpec((t, _DN_QKV), lambda i: (i, 0)), _halo_specs(_DN_QKV, t, nt, True),
                  pl.BlockSpec((DN_CONV, _DN_QKV), lambda i: (0, 0)),
                  _row_spec(DN_WIDTH, t), _row_spec(DN_WIDTH, t), _row_spec(DN_WIDTH, t),
                  _row_spec(128, t), _row_spec(128, t), _vec_spec(128), _vec_spec(128)],
        out_specs=[_row_spec(_DN_QKV, t), _row_spec(128, t), _vec_spec(128), _vec_spec(128)],
        out_shape=[jax.ShapeDtypeStruct((r, _DN_QKV), F32), jax.ShapeDtypeStruct((r, 128), BF16),
                   jax.ShapeDtypeStruct((1, 128), F32), jax.ShapeDtypeStruct((1, 128), F32)],
        compiler_params=_cparams(("arbitrary",)),
    )(proj, proj, conv_w, dq, dk, dv, dgb, ab, alog, dtb)


def _dn_conv_bwd(dco, proj, conv_w, name):
    r = dco.shape[0]
    t = DN_PRE_TILE
    nt = r // t

    def body(d_ref, dh_ref, x_ref, xh_ref, w_ref, dx_ref, dw_ref):
        i = pl.program_id(0)

        @pl.when(i == 0)
        def _():
            dw_ref[...] = jnp.zeros_like(dw_ref)

        d = d_ref[...]
        dhalo = jnp.where(i < nt - 1, dh_ref[...], 0.0)
        x = x_ref[...]
        xhalo = jnp.where(i > 0, xh_ref[...], 0.0)
        w = w_ref[...]
        dx = w[3:4] * d
        dws = [None] * DN_CONV
        dws[3] = jnp.sum(d * x, axis=0, keepdims=True)
        for tap in range(DN_CONV - 1):
            s = DN_CONV - 1 - tap
            dx = dx + w[tap:tap + 1] * _shift_up(d, dhalo, s, t)
            dws[tap] = jnp.sum(d * _shift_down(x, xhalo, s, t), axis=0, keepdims=True)
        dx_ref[...] = dx.astype(BF16)
        dw_ref[...] += jnp.concatenate(dws + [jnp.zeros((8 - DN_CONV, _DN_QKV), F32)], axis=0)

    return _pallas(
        body, name=name, grid=(nt,),
        in_specs=[_row_spec(_DN_QKV, t), _halo_specs(_DN_QKV, t, nt, False),
                  pl.BlockSpec((t, _DN_QKV), lambda i: (i, 0)), _halo_specs(_DN_QKV, t, nt, True),
                  pl.BlockSpec((DN_CONV, _DN_QKV), lambda i: (0, 0))],
        out_specs=[_row_spec(_DN_QKV, t), pl.BlockSpec((8, _DN_QKV), lambda i: (0, 0))],
        out_shape=[jax.ShapeDtypeStruct((r, _DN_QKV), BF16), jax.ShapeDtypeStruct((8, _DN_QKV), F32)],
        compiler_params=_cparams(("arbitrary",)),
    )(dco, dco, proj, proj, conv_w)


def _split3(x):
    hi = x.astype(BF16)
    return hi, (x - hi.astype(F32)).astype(BF16)


def _dot3s(a, b, dims=((1,), (0,))):
    return _dot(a[0], b[0], dims) + (_dot(a[0], b[1], dims) + _dot(a[1], b[0], dims))


def _dot3(a, b, dims=((1,), (0,))):
    return _dot3s(_split3(a), _split3(b), dims)


def _dn_inverse_many(n_mats):
    c = n_mats[0].shape[0]
    row = lax.broadcasted_iota(jnp.int32, (c, c), 0)
    col = lax.broadcasted_iota(jnp.int32, (c, c), 1)
    eye = (row == col).astype(F32)
    same = row // DN_SUB == col // DN_SUB
    nds = [jnp.where(same, n, 0.0) for n in n_mats]
    nos = [n - nd for n, nd in zip(n_mats, nds)]

    def geometric(bs, order):
        xs = [eye + b for b in bs]
        sp = [_split3(b) for b in bs]
        k = 2
        while k < order:
            sp = [_split3(_dot3s(s_, s_)) for s_ in sp]
            xs = [x + _dot3s(_split3(x), s_) for x, s_ in zip(xs, sp)]
            k *= 2
        return xs

    tds = [_split3(td) for td in geometric([-nd for nd in nds], DN_SUB)]
    ms = [_dot3s(td, _split3(no)) for td, no in zip(tds, nos)]
    xs = geometric([-m for m in ms], c // DN_SUB)
    return [_dot3s(_split3(x), td) for x, td in zip(xs, tds)]


def _dn_chunk_shared(gb_ref, gbt_ref):
    c = DN_CHUNK
    row = lax.broadcasted_iota(jnp.int32, (c, c), 0)
    col = lax.broadcasted_iota(jnp.int32, (c, c), 1)
    gbv = gb_ref[...]
    gam_all = _split_dot((row >= col).astype(BF16), gbv)
    hi, lo = _split3(gbt_ref[...])
    tri_t = (row <= col).astype(BF16)
    return dict(row=row, col=col, gbv=gbv, gam_all=gam_all, gam_rows=_dot(hi, tri_t) + _dot(lo, tri_t),
                lane=lax.broadcasted_iota(jnp.int32, (1, 128), 1))


def _dn_chunk_common(q, k, v, sh, h):
    c = DN_CHUNK
    row, col, lane = sh["row"], sh["col"], sh["lane"]
    q, k, v = q.astype(F32), k.astype(F32), v.astype(F32)
    gam = jnp.sum(jnp.where(lane == h, sh["gam_all"], 0.0), axis=1, keepdims=True)
    beta = jnp.sum(jnp.where(lane == h + DN_HEADS, sh["gbv"], 0.0), axis=1, keepdims=True)
    gam_row = sh["gam_rows"][h:h + 1]
    dec = jnp.where(row >= col, jnp.exp(jnp.minimum(gam - gam_row, 0.0)), 0.0)
    kb, qb = k.astype(BF16), q.astype(BF16)
    nt_dims = ((1,), (1,))
    kk = _dot(kb, kb, nt_dims)
    qk = _dot(qb, kb, nt_dims)
    eg = jnp.exp(gam)
    gam_l = gam[c - 1:c, :]
    return dict(q=q, k=k, v=v, qb=qb, kb=kb, gam=gam, beta=beta, dec=dec, kk=kk, qk=qk, eg=eg, gam_l=gam_l,
                row=row, col=col, lane=lane, att=qk * dec, qg=q * eg, kt=k * jnp.exp(gam_l - gam),
                rhs=jnp.concatenate([v * beta, k * (beta * eg)], axis=1))


def _dn_fwd(q, k, v, gb, gbt, name):
    r = q.shape[0]
    c = DN_CHUNK
    nc = r // c
    dh = DN_HEAD_DIM
    tn_dims = ((0,), (0,))

    def body(q_ref, k_ref, v_ref, gb_ref, gbt_ref, o_ref, ss_ref, ts_ref, s_ref):
        @pl.when(pl.program_id(0) == 0)
        def _():
            s_ref[...] = jnp.zeros_like(s_ref)

        heads = list(range(DN_HEADS))
        sl = [slice(h * dh, (h + 1) * dh) for h in heads]
        sh = _dn_chunk_shared(gb_ref, gbt_ref)
        zs = [_dn_chunk_common(q_ref[:, sl[h]], k_ref[:, sl[h]], v_ref[:, sl[h]], sh, h) for h in heads]
        t_invs = _dn_inverse_many([jnp.where(sh["row"] > sh["col"], z["beta"] * z["kk"] * z["dec"], 0.0) for z in zs])
        sols = [_dot3(t_inv, z["rhs"]) for t_inv, z in zip(t_invs, zs)]
        ss = [s_ref[h] for h in heads]
        sbs = [s.astype(BF16) for s in ss]
        vnbs = [(sol[:, :dh] - _dot(sol[:, dh:].astype(BF16), sb)).astype(BF16) for sol, sb in zip(sols, sbs)]
        for h in heads:
            o_ref[:, sl[h]] = _dot(zs[h]["qg"].astype(BF16), sbs[h]) + _dot(zs[h]["att"].astype(BF16), vnbs[h])
        for h in heads:
            ss_ref[0, h] = ss[h]
            ts_ref[0, h] = t_invs[h]
            s_ref[h] = ss[h] * jnp.exp(zs[h]["gam_l"]) + _dot(zs[h]["kt"].astype(BF16), vnbs[h], tn_dims)

    blk = pl.BlockSpec((c, DN_WIDTH), lambda ci: (ci, 0))
    sav = pl.BlockSpec((1, DN_HEADS, dh, dh), lambda ci: (ci, 0, 0, 0))
    return _pallas(
        body, name=name, grid=(nc,),
        in_specs=[blk, blk, blk, pl.BlockSpec((c, 128), lambda ci: (ci, 0)), pl.BlockSpec((16, c), lambda ci: (0, ci))],
        out_specs=[blk, sav, sav],
        out_shape=[jax.ShapeDtypeStruct((r, DN_WIDTH), F32), jax.ShapeDtypeStruct((nc, DN_HEADS, dh, dh), F32),
                   jax.ShapeDtypeStruct((nc, DN_HEADS, dh, dh), F32)],
        scratch_shapes=[pltpu.VMEM((DN_HEADS, dh, dh), F32)],
        compiler_params=_cparams(("arbitrary",)),
    )(q, k, v, gb, gbt)


def _dn_bwd(q, k, v, gb, gbt, ssave, tsave, do, name):
    r = q.shape[0]
    c = DN_CHUNK
    nc = r // c
    dh = DN_HEAD_DIM
    nt_dims = ((1,), (1,))
    tn_dims = ((0,), (0,))

    def body(q_ref, k_ref, v_ref, gb_ref, gbt_ref, ss_ref, ts_ref, do_ref, dq_ref, dk_ref, dv_ref, dgb_ref, ds_ref):
        @pl.when(pl.program_id(0) == 0)
        def _():
            ds_ref[...] = jnp.zeros_like(ds_ref)

        heads = list(range(DN_HEADS))
        sl = [slice(h * dh, (h + 1) * dh) for h in heads]
        sh = _dn_chunk_shared(gb_ref, gbt_ref)
        row, col, lane = sh["row"], sh["col"], sh["lane"]
        rs = lambda x: jnp.sum(x, axis=1, keepdims=True)
        tot = lambda x: jnp.sum(rs(x), axis=0, keepdims=True)
        st = [dict() for _ in heads]
        dgb_parts = []

        def s_common(h):
            st[h].update(_dn_chunk_common(q_ref[:, sl[h]], k_ref[:, sl[h]], v_ref[:, sl[h]], sh, h))
            st[h]["t"] = _split3(ts_ref[0, h])

        def s_sol(h):
            st[h]["sol"] = _dot3s(st[h]["t"], _split3(st[h]["rhs"]))

        def s_state(h):
            z = st[h]
            sol = z["sol"]
            kcd = sol[:, dh:]
            s = ss_ref[0, h]
            sb = s.astype(BF16)
            vnb = (sol[:, :dh] - _dot(kcd.astype(BF16), sb)).astype(BF16)
            ds_next = ds_ref[h]
            dsb = ds_next.astype(BF16)
            dob = do_ref[:, sl[h]].astype(BF16)
            z["dqg"] = _dot(dob, sb, nt_dims)
            ds = _dot(z["qg"].astype(BF16), dob, tn_dims)
            z["d_att"] = jnp.where(row >= col, _dot(dob, vnb, nt_dims), 0.0)
            dvn = _dot(z["att"].astype(BF16), dob, tn_dims) + _dot(z["kt"].astype(BF16), dsb)
            z["dkt"] = _dot(vnb, dsb, nt_dims)
            eg_l = jnp.exp(z["gam_l"])
            ds = ds + ds_next * eg_l
            z["dgam_l"] = tot(ds_next * s) * eg_l
            dvnb = dvn.astype(BF16)
            dkcd = -_dot(dvnb, sb, nt_dims)
            ds_ref[h] = ds - _dot(kcd.astype(BF16), dvnb, tn_dims)
            z["dsol"] = jnp.concatenate([dvn, dkcd], axis=1)

        def s_drhs(h):
            st[h]["drhs"] = _dot3s(st[h]["t"], _split3(st[h]["dsol"]), tn_dims)

        def s_dn(h):
            z = st[h]
            z["dn"] = jnp.where(row > col, -_dot3(z["drhs"], z["sol"], nt_dims), 0.0)

        def s_rest(h):
            z = st[h]
            k_, v_, kb, qb = z["k"], z["v"], z["kb"], z["qb"]
            beta, eg, dec, kk, qk, gam, gam_l = z["beta"], z["eg"], z["dec"], z["kk"], z["qk"], z["gam"], z["gam_l"]
            dn, d_att, dqg, dkt = z["dn"], z["d_att"], z["dqg"], z["dkt"]
            drv, drk = z["drhs"][:, :dh], z["drhs"][:, dh:]
            s_rkk = rs(drk * k_)
            dv_ref[:, sl[h]] = drv * beta
            dbeta = rs(drv * v_) + s_rkk * eg + rs(dn * kk * dec)
            dk = drk * (beta * eg)
            dgam = s_rkk * beta * eg
            dkk = (dn * beta * dec).astype(BF16)
            dd = dn * beta * kk + d_att * qk
            dqk = (d_att * dec).astype(BF16)
            dq_ref[:, sl[h]] = _dot(dqk, kb) + dqg * eg
            dk = dk + _dot(dqk, qb, tn_dims) + _dot(dkk, kb) + _dot(dkk, kb, tn_dims)
            w = dd * dec
            wh, wl = _split3(w)
            ones = jnp.ones((c, 128), BF16)
            col_sum = (_dot(wh, ones, tn_dims) + _dot(wl, ones, tn_dims))[:, 0:1]
            dgam = dgam + rs(w) - col_sum + rs(dqg * z["qg"]) - rs(dkt * z["kt"])
            dk_ref[:, sl[h]] = dk + dkt * jnp.exp(gam_l - gam)
            dgam_l = z["dgam_l"] + tot(dkt * z["kt"])
            rowc = lax.broadcasted_iota(jnp.int32, (c, 1), 0)
            dgam = dgam + jnp.where(rowc == c - 1, dgam_l, 0.0)
            dg = _split_dot((row <= col).astype(BF16), jnp.broadcast_to(dgam, (c, 128)))[:, 0:1]
            dgb_parts.append(jnp.where(lane == h, dg, 0.0) + jnp.where(lane == h + DN_HEADS, dbeta, 0.0))

        _emit_chains(heads, [s_common, s_sol, s_state, s_drhs, s_dn, s_rest], False)
        dgb = dgb_parts[0]
        for part in dgb_parts[1:]:
            dgb = dgb + part
        dgb_ref[...] = dgb

    blk = pl.BlockSpec((c, DN_WIDTH), lambda ci: (nc - 1 - ci, 0))
    sav = pl.BlockSpec((1, DN_HEADS, dh, dh), lambda ci: (nc - 1 - ci, 0, 0, 0))
    gspec = pl.BlockSpec((c, 128), lambda ci: (nc - 1 - ci, 0))
    return _pallas(
        body, name=name, grid=(nc,),
        in_specs=[blk, blk, blk, gspec, pl.BlockSpec((16, c), lambda ci: (0, nc - 1 - ci)), sav, sav, blk],
        out_specs=[blk, blk, blk, gspec],
        out_shape=[jax.ShapeDtypeStruct((r, DN_WIDTH), F32)] * 3 + [jax.ShapeDtypeStruct((r, 128), F32)],
        scratch_shapes=[pltpu.VMEM((DN_HEADS, dh, dh), F32)],
        compiler_params=_cparams(("arbitrary",)),
    )(q, k, v, gb, gbt, ssave, tsave, do)


def _dn_post_fwd(o, proj, g, name):
    r = o.shape[0]

    def body(o_ref, z_ref, g_ref, y_ref):
        g_ = g_ref[...]
        for hd in range(DN_HEADS):
            sl = slice(hd * 128, (hd + 1) * 128)
            sz, _ = _silu(z_ref[:, sl])
            y_ref[:, sl] = (_rms(o_ref[:, sl], g_) * sz).astype(BF16)

    return _pallas(body, name=name, grid=(r // ROW_TILE,),
                   in_specs=[_row_spec(DN_WIDTH), pl.BlockSpec((ROW_TILE, DN_WIDTH), lambda i: (i, 3)), _vec_spec(128)],
                   out_specs=_row_spec(DN_WIDTH), out_shape=jax.ShapeDtypeStruct((r, DN_WIDTH), BF16),
                   compiler_params=_cparams(("parallel",)))(o, proj, g)


def _dn_post_bwd(o, proj, g, dy, name):
    r = o.shape[0]

    def body(o_ref, z_ref, g_ref, dy_ref, do_ref, dz_ref, dg_ref):
        @pl.when(pl.program_id(0) == 0)
        def _():
            dg_ref[...] = jnp.zeros_like(dg_ref)

        g_ = g_ref[...]
        for hd in range(DN_HEADS):
            sl = slice(hd * 128, (hd + 1) * 128)
            z_ = z_ref[:, sl]
            sz, sg = _silu(z_)
            dy_ = dy_ref[:, sl]
            o_ = o_ref[:, sl]
            dz_ref[:, sl] = (dy_ * _rms(o_, g_) * (sg * (1.0 + z_ * (1.0 - sg)))).astype(BF16)
            dx, dg = _rms_bwd(o_, g_, dy_ * sz)
            do_ref[:, sl] = dx
            dg_ref[...] += dg

    return _pallas(body, name=name, grid=(r // ROW_TILE,),
                   in_specs=[_row_spec(DN_WIDTH), pl.BlockSpec((ROW_TILE, DN_WIDTH), lambda i: (i, 3)), _vec_spec(128),
                             _row_spec(DN_WIDTH)],
                   out_specs=[_row_spec(DN_WIDTH), _row_spec(DN_WIDTH), _vec_spec(128)],
                   out_shape=[jax.ShapeDtypeStruct((r, DN_WIDTH), F32), jax.ShapeDtypeStruct((r, DN_WIDTH), BF16),
                              jax.ShapeDtypeStruct((1, 128), F32)],
                   compiler_params=_cparams(("arbitrary",)))(o, proj, g, dy)


def _exchange(arrays, scatter, name):
    n = len(arrays)

    def body(*refs):
        copies = _exchange_copies(refs[:n], refs[n:2 * n], scatter, *refs[2 * n:])
        for cp in copies:
            cp.start()
        for cp in copies:
            cp.wait()

    hbm = pl.BlockSpec(memory_space=pl.ANY)
    return _pallas(
        body, name=name, in_specs=[hbm] * n, out_specs=[hbm] * n, out_shape=_exchange_shapes(arrays, scatter),
        scratch_shapes=_exchange_sems(n),
    )(*arrays)


def _exchange_shapes(arrays, scatter):
    return [jax.ShapeDtypeStruct((N_DEV,) + (a.shape[1:] if sc else a.shape), a.dtype) for a, sc in zip(arrays, scatter)]


def _exchange_sems(n):
    return [pltpu.SemaphoreType.DMA((n * N_DEV,)), pltpu.SemaphoreType.DMA((n * N_DEV,)), pltpu.SemaphoreType.DMA((n,))]


def _exchange_copies(in_refs, out_refs, scatter, send_sems, recv_sems, local_sems):
    mx, my, mc = lax.axis_index("x"), lax.axis_index("y"), lax.axis_index("c")
    me = 4 * mx + 2 * my + mc
    copies = []
    for a in range(len(in_refs)):
        src_own = in_refs[a].at[me] if scatter[a] else in_refs[a]
        copies.append(pltpu.make_async_copy(src_own, out_refs[a].at[me], local_sems.at[a]))
        for kbits in range(1, N_DEV):
            px = lax.rem(mx + ((kbits >> 2) & 1), 2)
            py = lax.rem(my + ((kbits >> 1) & 1), 2)
            pc = lax.rem(mc + (kbits & 1), 2)
            src = in_refs[a].at[4 * px + 2 * py + pc] if scatter[a] else in_refs[a]
            copies.append(pltpu.make_async_remote_copy(
                src_ref=src, dst_ref=out_refs[a].at[me],
                send_sem=send_sems.at[a * N_DEV + kbits], recv_sem=recv_sems.at[a * N_DEV + kbits],
                device_id=(px, py, pc), device_id_type=pl.DeviceIdType.MESH))
    return copies


def _adamw(gstack, w, m, v, name):
    a, b = w.shape
    ta = a
    for t in (1024, 512, 256, 128, 64, 32, 16, 8):
        if a % t == 0 and N_DEV * t * b * 4 <= 4 * 1024 * 1024:
            ta = t
            break
    c1 = 1.0 / (1.0 - ADAM_B1 ** ADAM_STEP)
    c2 = 1.0 / (1.0 - ADAM_B2 ** ADAM_STEP)

    def body(g_ref, w_ref, m_ref, v_ref, og_ref, od_ref, om_ref, ov_ref):
        g = g_ref[0].astype(F32)
        for s in range(1, N_DEV):
            g = g + g_ref[s].astype(F32)
        m_new = ADAM_B1 * m_ref[...] + (1.0 - ADAM_B1) * g
        v_new = ADAM_B2 * v_ref[...] + (1.0 - ADAM_B2) * (g * g)
        og_ref[...] = g
        om_ref[...] = m_new
        ov_ref[...] = v_new
        od_ref[...] = -ADAM_LR * ((m_new * c1) / (jnp.sqrt(v_new * c2) + ADAM_EPS) + ADAM_WD * w_ref[...])

    spec = pl.BlockSpec((ta, b), lambda i: (i, 0))
    return _pallas(
        body, name=name, grid=(a // ta,),
        in_specs=[pl.BlockSpec((N_DEV, ta, b), lambda i: (0, i, 0)), spec, spec, spec],
        out_specs=[spec] * 4, out_shape=[jax.ShapeDtypeStruct((a, b), F32)] * 4,
        compiler_params=_cparams(("parallel",)),
    )(gstack, w, m, v)


_WEIGHTS = ['meta_tokens', 'pre_mix_norm', 'post_mix_norm', 'pre_mlp_norm', 'post_mlp_norm', 'mlp_w1', 'mlp_w2',
            'w_in_even', 'w_out_even', 'sb_out_norm', 's5_lambda_re', 's5_lambda_im', 's5_log_dt', 's5_b_re', 's5_b_im',
            's5_c_re', 's5_c_im', 's5_d', 's5_w_glu', 's5_b_glu', 's5_out_norm', 'w_in_odd', 'dn_conv_w', 'dn_a_log',
            'dn_dt_bias', 'dn_out_norm', 'w_out_odd']
_SHARDED = ['meta_tokens', 'mlp_w1', 'mlp_w2', 'w_in_even', 'w_out_even', 's5_w_glu', 'w_in_odd', 'dn_conv_w', 'w_out_odd']
_SMALL = [n for n in _WEIGHTS if n not in _SHARDED]
_GATHER_FIRST = ['meta_tokens', 'w_in_even', 's5_w_glu', 'w_out_even']
_GATHER_LATE = [n for n in _SHARDED if n not in _GATHER_FIRST]
_REDUCE_EARLY = ['mlp_w1', 'mlp_w2', 'w_in_odd', 'dn_conv_w', 'w_out_odd', 'w_out_even']


def _view2d(name, a):
    return a.reshape(-1, a.shape[-1])


def _unshard(name, g):
    if name == 'mlp_w1':
        return g.reshape(N_DEV, 2, D_MODEL, -1).transpose(1, 2, 0, 3).reshape(2, D_MODEL, D_FF)
    if name == 'mlp_w2':
        return g.reshape(N_DEV, 2, -1, D_MODEL).transpose(1, 0, 2, 3).reshape(2, D_FF, D_MODEL)
    if name in ('w_in_even', 'w_in_odd', 'dn_conv_w', 'meta_tokens'):
        return g.transpose(1, 0, 2).reshape(g.shape[1], -1)
    return g.reshape(-1, g.shape[-1])


def _to_blocks(name, full):
    if name == 'mlp_w1':
        return full.reshape(2, D_MODEL, N_DEV, -1).transpose(2, 0, 1, 3).reshape(N_DEV, 2 * D_MODEL, -1)
    if name == 'mlp_w2':
        return full.reshape(2, N_DEV, -1, D_MODEL).transpose(1, 0, 2, 3).reshape(N_DEV, -1, D_MODEL)
    if name in ('w_in_even', 'w_in_odd', 'dn_conv_w', 'meta_tokens'):
        return full.reshape(full.shape[0], N_DEV, -1).transpose(1, 0, 2)
    return full.reshape(N_DEV, -1, full.shape[-1])


def _pack(parts):
    rows = []
    for p in parts:
        flat = p.reshape(-1)
        rows.append(jnp.pad(flat, (0, (-flat.shape[0]) % 128)).reshape(-1, 128))
    return jnp.concatenate(rows, axis=0)


def _unpack(packed, like):
    out, at = [], 0
    for p in like:
        n = math.prod(p.shape)
        nrow = -(-n // 128)
        out.append(packed[at:at + nrow].reshape(-1)[:n].reshape(p.shape))
        at += nrow
    return out


def _lane_vec(x, width=128):
    flat = x.reshape(-1)
    return jnp.pad(flat, (0, width - flat.shape[0])).reshape(1, width)


def kernel(x, meta_tokens, pre_mix_norm, post_mix_norm, pre_mlp_norm, post_mlp_norm, mlp_w1, mlp_w2, w_in_even, w_out_even, sb_out_norm, s5_lambda_re, s5_lambda_im, s5_log_dt, s5_b_re, s5_b_im, s5_c_re, s5_c_im, s5_d, s5_w_glu, s5_b_glu, s5_out_norm, w_in_odd, dn_conv_w, dn_a_log, dn_dt_bias, dn_out_norm, w_out_odd, loss_target, m_meta_tokens, m_pre_mix_norm, m_post_mix_norm, m_pre_mlp_norm, m_post_mlp_norm, m_mlp_w1, m_mlp_w2, m_w_in_even, m_w_out_even, m_sb_out_norm, m_s5_lambda_re, m_s5_lambda_im, m_s5_log_dt, m_s5_b_re, m_s5_b_im, m_s5_c_re, m_s5_c_im, m_s5_d, m_s5_w_glu, m_s5_b_glu, m_s5_out_norm, m_w_in_odd, m_dn_conv_w, m_dn_a_log, m_dn_dt_bias, m_dn_out_norm, m_w_out_odd, v_meta_tokens, v_pre_mix_norm, v_post_mix_norm, v_pre_mlp_norm, v_post_mlp_norm, v_mlp_w1, v_mlp_w2, v_w_in_even, v_w_out_even, v_sb_out_norm, v_s5_lambda_re, v_s5_lambda_im, v_s5_log_dt, v_s5_b_re, v_s5_b_im, v_s5_c_re, v_s5_c_im, v_s5_d, v_s5_w_glu, v_s5_b_glu, v_s5_out_norm, v_w_in_odd, v_dn_conv_w, v_dn_a_log, v_dn_dt_bias, v_dn_out_norm, v_w_out_odd):
    given = dict(locals())
    w = {n: given[n] for n in _WEIGHTS}
    mom_m = {n: given["m_" + n] for n in _WEIGHTS}
    mom_v = {n: given["v_" + n] for n in _WEIGHTS}

    seq = x.shape[1]
    assert x.shape[0] == 1 and seq % ROW_TILE == 0
    r = seq + ROW_TILE
    pad = ROW_TILE - N_META
    pad_tiles = 1

    wire = {n: (F32 if n in ('dn_conv_w', 'meta_tokens') else BF16) for n in _SHARDED}
    shard_wire = lambda n: _view2d(n, w[n]).astype(wire[n])
    gathered = _exchange([shard_wire(n) for n in _GATHER_FIRST], [False] * len(_GATHER_FIRST), "gather_first")
    full = {n: _unshard(n, g_) for n, g_ in zip(_GATHER_FIRST, gathered)}
    w_ie, w_oe, w_glu = full['w_in_even'], full['w_out_even'], full['s5_w_glu']
    row = lambda v_: v_.reshape(1, -1)

    hs0 = jnp.concatenate([jnp.zeros((pad, D_MODEL), F32), full['meta_tokens'], x[0]], axis=0)
    hn0 = _norm_pre(hs0, row(pre_mix_norm[0]), "pre_mix_0")
    qkv = _mm_fwd(hn0, w_ie[:, :3 * SB_WIDTH], "in_even_qkv", out_dtypes=(BF16,))
    u = _mm_fwd(hn0, w_ie[:, 3 * SB_WIDTH:], "in_even_u")
    q, k, v = qkv[:, :SB_WIDTH], qkv[:, SB_WIDTH:2 * SB_WIDTH], qkv[:, 2 * SB_WIDTH:]
    nb = r // ATT_BLK
    blocks_t = lambda t_: t_.reshape(nb, ATT_BLK, 4, 128).transpose(2, 0, 3, 1)
    o_sb, ssave, gathered = _sb_fwd(q, k, blocks_t(v), pad, "sb_fwd",
                                    ride=([shard_wire(n) for n in _GATHER_LATE], [False] * len(_GATHER_LATE)))
    full.update({n: _unshard(n, g_) for n, g_ in zip(_GATHER_LATE, gathered)})
    w1, w2, w_oo, conv_w = full['mlp_w1'], full['mlp_w2'], full['w_out_odd'], full['dn_conv_w']
    w_io = full['w_in_odd'][:, :4 * DN_WIDTH]
    w_ab = jnp.pad(full['w_in_odd'][:, 4 * DN_WIDTH:], ((0, 0), (0, 128 - 2 * DN_HEADS)))
    on_sb = _norm_pre(o_sb, row(sb_out_norm[0]), "sb_out_norm")

    lam_re, lam_im, logdt, btr, bti, ctr, cti, s5_mask = _s5_expand(
        s5_lambda_re[0], s5_lambda_im[0], s5_log_dt[0], s5_b_re[0], s5_b_im[0], s5_c_re[0], s5_c_im[0])
    a_re, a_im, bbr, bbi = _s5_prep(lam_re, lam_im, logdt, btr, bti, "s5_prep")
    s5_wb = jnp.stack([_s5_block_diag_b(bbr, s5_mask), _s5_block_diag_b(bbi, s5_mask)]).astype(BF16)
    s5_wc = jnp.stack([_s5_block_diag_c(ctr, s5_mask), _s5_block_diag_c(cti, s5_mask)]).astype(BF16)
    s5_a = jnp.stack([a_re, a_im])
    s5_args = (s5_wb, s5_a, s5_wc, row(s5_d[0]), w_glu, row(s5_b_glu[0]), row(s5_out_norm[0]))
    y_s5, on_s5, xstart = _s5_fwd(u, *s5_args, "s5_fwd")

    merged = jnp.concatenate([on_sb, on_s5], axis=1)
    mix0 = _mm_fwd(merged, w_oe, "out_even")
    hs1, hn1 = _norm_post_pre(hs0, mix0, row(post_mix_norm[0]), row(pre_mlp_norm[0]), "post_mix_0")
    relu2 = lambda acc: (jnp.square(jnp.maximum(acc, 0.0)), jnp.maximum(acc, 0.0))
    r0, ra0 = _mm_fwd(hn1, w1[0], "mlp_up_0", out_dtypes=(BF16, BF16), epilogue=relu2)
    m0 = _mm_fwd(r0, w2[0], "mlp_down_0")
    hs2, hn2 = _norm_post_pre(hs1, m0, row(post_mlp_norm[0]), row(pre_mix_norm[1]), "post_mlp_0")

    proj = _mm_fwd(hn2, w_io, "in_odd")
    ab = _mm_fwd(hn2, w_ab, "in_odd_gates")
    alog, dtb = _lane_vec(dn_a_log[0]), _lane_vec(dn_dt_bias[0])
    qd, kd, vd, gb = _dn_pre_fwd(proj, ab, conv_w, alog, dtb, pad, "dn_pre")
    gbt = gb[:, :2 * DN_HEADS].T
    o_dn, s_dn, t_dn = _dn_fwd(qd, kd, vd, gb, gbt, "dn_fwd")
    on_dn = _dn_post_fwd(o_dn, proj, row(dn_out_norm[0]), "dn_post")
    mix1 = _mm_fwd(on_dn, w_oo, "out_odd")
    hs3, hn3 = _norm_post_pre(hs2, mix1, row(post_mix_norm[1]), row(pre_mlp_norm[1]), "post_mix_1")
    r1, ra1 = _mm_fwd(hn3, w1[1], "mlp_up_1", out_dtypes=(BF16, BF16), epilogue=relu2)
    m1 = _mm_fwd(r1, w2[1], "mlp_down_1")
    dhs, loss_part = _norm_post_loss(hs3, m1, row(post_mlp_norm[1]), loss_target[0], pad_tiles, "post_mlp_1_loss")
    loss = lax.psum(loss_part, ("x", "y", "c"))

    g = {}
    drelu2 = lambda acc, ra: (acc * (2.0 * ra.astype(F32)),)

    def mlp_bwd(layer, hn, rr, ra, dm):
        dw2 = _mm_wgrad(rr, dm, f"mlp_down_{layer}_wgrad")
        da = _mm_dgrad(dm, w2[layer], f"mlp_down_{layer}_dgrad", out_dtypes=(BF16,), extras=(ra,), epilogue=drelu2)
        dw1 = _mm_wgrad(hn, da, f"mlp_up_{layer}_wgrad")
        return dw1, dw2, _mm_dgrad(da, w1[layer], f"mlp_up_{layer}_dgrad")

    _, dm1, _, dg_post_mlp1 = _norm_bwd(dhs, post=(m1, row(post_mlp_norm[1])), pad=pad, name="post_mlp_1_bwd")
    dw1_1, dw2_1, dhn3 = mlp_bwd(1, hn3, r1, ra1, dm1)
    dhs, dmix1, dg_pre_mlp1, dg_post_mix1 = _norm_bwd(
        dhs, pre=(hs3, row(pre_mlp_norm[1]), dhn3), post=(mix1, row(post_mix_norm[1])), pad=pad, name="post_mix_1_bwd")

    g['w_out_odd'] = _mm_wgrad(on_dn, dmix1, "out_odd_wgrad")
    d_on_dn = _mm_dgrad(dmix1, w_oo, "out_odd_dgrad")
    do_dn, dz, dg_dn = _dn_post_bwd(o_dn, proj, row(dn_out_norm[0]), d_on_dn, "dn_post_bwd")
    dqd, dkd, dvd, dgb = _dn_bwd(qd, kd, vd, gb, gbt, s_dn, t_dn, do_dn, "dn_bwd")
    dco, dab, d_alog, d_dtb = _dn_pre_bwd(proj, conv_w, dqd, dkd, dvd, dgb, ab, alog, dtb, pad, "dn_pre_bwd")
    dpre, d_conv = _dn_conv_bwd(dco, proj, conv_w, "dn_conv_bwd")
    dproj = jnp.concatenate([dpre, dz], axis=1)
    g['w_in_odd'] = jnp.concatenate([_mm_wgrad(hn2, dproj, "in_odd_wgrad"),
                                     _mm_wgrad(hn2, dab, "in_odd_gates_wgrad")[:, :2 * DN_HEADS]], axis=1)
    dhn2 = _mm_dgrad(dab, w_ab, "in_odd_gates_dgrad")
    dhn2 = _mm_dgrad(dproj, w_io, "in_odd_dgrad", extras=(dhn2,), epilogue=lambda acc, other: (acc + other,))
    g['dn_conv_w'] = d_conv[:DN_CONV]
    g['dn_a_log'], g['dn_dt_bias'], g['dn_out_norm'] = d_alog[0, :DN_HEADS], d_dtb[0, :DN_HEADS], dg_dn[0]

    dhs, dm0, dg_pre_mix1, dg_post_mlp0 = _norm_bwd(
        dhs, pre=(hs2, row(pre_mix_norm[1]), dhn2), post=(m0, row(post_mlp_norm[0])), pad=pad, name="post_mlp_0_bwd")
    dw1_0, dw2_0, dhn1 = mlp_bwd(0, hn1, r0, ra0, dm0)
    dhs, dmix0, dg_pre_mlp0, dg_post_mix0 = _norm_bwd(
        dhs, pre=(hs1, row(pre_mlp_norm[0]), dhn1), post=(mix0, row(post_mix_norm[0])), pad=pad, name="post_mix_0_bwd")

    g['w_out_even'] = _mm_wgrad(merged, dmix0, "out_even_wgrad")
    dmerged = _mm_dgrad(dmix0, w_oe, "out_even_dgrad")
    _, do_sb, _, dg_sb = _norm_bwd(dmerged, post=(o_sb, row(sb_out_norm[0])), pad=pad, dm_dtype=F32,
                                   dhs_cols=(SB_WIDTH, 0), name="sb_out_norm_bwd")
    dq, dk4, dv4 = _sb_bwd(q, k, v, blocks_t(k), ssave, do_sb, pad, "sb_bwd")
    unheads = lambda t_: t_.transpose(1, 0, 2).reshape(r, SB_WIDTH)
    g['mlp_w1'] = jnp.stack([dw1_0, dw1_1])
    g['mlp_w2'] = jnp.stack([dw2_0, dw2_1])
    grad_wire = lambda n: _to_blocks(n, g[n].reshape(full[n].shape)).astype(wire[n])
    du, d_a, d_d, d_bglu, dg_s5, d_wb, d_wc, g['s5_w_glu'], reduced = _s5_bwd(
        u, y_s5, dmerged, xstart, *s5_args, "s5_bwd", don_block=1,
        ride=([grad_wire(n) for n in _REDUCE_EARLY], [True] * len(_REDUCE_EARLY)))
    stacks = dict(zip(_REDUCE_EARLY, reduced))
    g_lr, g_li, g_dt, g_btr, g_bti = _s5_prep_bwd(
        lam_re, lam_im, logdt, btr, bti, d_a[0], d_a[1],
        _s5_diag_of_b(d_wb[0], s5_mask), _s5_diag_of_b(d_wb[1], s5_mask), "s5_prep_bwd")
    gg, nn, pp = S5_GROUPS, S5_STATE, S5_GROUP
    g['s5_lambda_re'], g['s5_lambda_im'] = g_lr.reshape(gg, nn), g_li.reshape(gg, nn)
    g['s5_log_dt'] = g_dt.reshape(gg, nn)[:, 0]
    g['s5_b_re'], g['s5_b_im'] = g_btr.T.reshape(gg, nn, pp), g_bti.T.reshape(gg, nn, pp)
    g['s5_c_re'] = _s5_diag_of_c(d_wc[0], s5_mask).reshape(gg, nn, pp).transpose(0, 2, 1)
    g['s5_c_im'] = _s5_diag_of_c(d_wc[1], s5_mask).reshape(gg, nn, pp).transpose(0, 2, 1)
    g['s5_d'], g['s5_b_glu'], g['s5_out_norm'], g['sb_out_norm'] = d_d[0], d_bglu[0], dg_s5[0], dg_sb[0]
    dqkvu = jnp.concatenate([dq, unheads(dk4), unheads(dv4), du], axis=1).astype(BF16)
    g['w_in_even'] = _mm_wgrad(hn0, dqkvu, "in_even_wgrad")
    dhn0 = _mm_dgrad(dqkvu, w_ie, "in_even_dgrad")
    dhs, _, dg_pre_mix0, _ = _norm_bwd(dhs, pre=(hs0, row(pre_mix_norm[0]), dhn0), pad=pad, name="pre_mix_0_bwd")

    g['meta_tokens'] = dhs[pad:pad + N_META]
    g['pre_mix_norm'] = jnp.concatenate([dg_pre_mix0, dg_pre_mix1], axis=0)
    g['post_mix_norm'] = jnp.concatenate([dg_post_mix0, dg_post_mix1], axis=0)
    g['pre_mlp_norm'] = jnp.concatenate([dg_pre_mlp0, dg_pre_mlp1], axis=0)
    g['post_mlp_norm'] = jnp.concatenate([dg_post_mlp0, dg_post_mlp1], axis=0)
    grad_x = dhs[pad + N_META:][None]

    small_like = [w[n] for n in _SMALL]
    last = [n for n in _SHARDED if n not in _REDUCE_EARLY]
    partial = [grad_wire(n) for n in last] + [_pack([g[n].reshape(w[n].shape) for n in _SMALL])]
    reduced = _exchange(partial, [True] * len(last) + [False], "reduce_last")
    stacks.update(zip(last, reduced[:-1]))
    grads, deltas, new_m, new_v = {}, {}, {}, {}
    for n in _SHARDED:
        outs = _adamw(stacks[n], _view2d(n, w[n]), _view2d(n, mom_m[n]), _view2d(n, mom_v[n]), f"adamw_{n}")
        grads[n], deltas[n], new_m[n], new_v[n] = (o.reshape(w[n].shape) for o in outs)
    outs = _adamw(reduced[-1], _pack(small_like), _pack([mom_m[n] for n in _SMALL]), _pack([mom_v[n] for n in _SMALL]),
                  "adamw_small")
    for dst, o in zip((grads, deltas, new_m, new_v), outs):
        for n, part in zip(_SMALL, _unpack(o, small_like)):
            dst[n] = part
    return (loss, grad_x, *[grads[n] for n in _WEIGHTS], *[deltas[n] for n in _WEIGHTS],
            *[new_m[n] for n in _WEIGHTS], *[new_v[n] for n in _WEIGHTS])
```

```python
import functools
import math

import jax
import jax.numpy as jnp
from jax import lax
from jax.experimental import pallas as pl
from jax.experimental.pallas import tpu as pltpu

F32 = jnp.float32
BF16 = jnp.bfloat16

D_MODEL = 1024
N_META = 16
SB_HEAD_DIM = 64
SB_WIDTH = 512
S5_WIDTH = 512
S5_GROUP = 16
S5_GROUPS = 32
S5_STATE = 64
S5_NS = S5_GROUPS * S5_STATE
DN_HEAD_DIM = 128
DN_HEADS = 8
DN_WIDTH = 1024
DN_CONV = 4
D_FF = 4096
EPS = 1e-6
N_DEV = 8

ADAM_LR = 0.001
ADAM_B1 = 0.9
ADAM_B2 = 0.999
ADAM_EPS = 1e-08
ADAM_WD = 0.01
ADAM_STEP = 10

ROW_TILE = 512
ATT_BLK = 256
SB_BLOCKS_PER_TRIP = 3
SB_LOG_ZERO = -106.0
SB_FWD_SKEW = False
SB_BWD_SKEW = True
DN_CHUNK = 128
DN_SUB = 16
DN_BWD_GROUP = 8
DN_BWD_SKEW = False
S5_TILE = 128
S5_CHUNKS = 4
VMEM_LIMIT = 56 * 1024 * 1024

_HIGH = lax.Precision.HIGHEST


def _pallas(body, **kw):
    return pl.pallas_call(body, **kw)


def _cparams(sem):
    return pltpu.CompilerParams(dimension_semantics=sem, vmem_limit_bytes=VMEM_LIMIT)


def _dot(a, b, dims=((1,), (0,))):
    return lax.dot_general(a, b, (dims, ((), ())), preferred_element_type=F32)


def _dot_hi(a, b):
    return lax.dot_general(a, b, (((1,), (0,)), ((), ())), preferred_element_type=F32, precision=_HIGH)


def _split_dot(m_bf16, x):
    hi = x.astype(BF16)
    lo = (x - hi.astype(F32)).astype(BF16)
    return _dot(m_bf16, hi) + _dot(m_bf16, lo)


def _matmul(a, b, *, ta=False, tb=False, tm, tn, tk, name, out_dtypes=(F32,), extras=(), epilogue=None):
    m, k = (a.shape[1], a.shape[0]) if ta else a.shape
    n = b.shape[0] if tb else b.shape[1]
    assert (b.shape[1] if tb else b.shape[0]) == k
    assert m % tm == 0 and n % tn == 0 and k % tk == 0, (name, m, n, k, tm, tn, tk)
    nk = k // tk
    n_ex = len(extras)
    n_out = len(out_dtypes)
    dims = ((0 if ta else 1,), (1 if tb else 0,))

    def finish(acc, ex_refs, o_refs):
        outs = (acc,) if epilogue is None else epilogue(acc, *[r[...] for r in ex_refs])
        for o_ref, o in zip(o_refs, outs):
            o_ref[...] = o.astype(o_ref.dtype)

    def body(*refs):
        a_ref, b_ref = refs[0], refs[1]
        ex_refs = refs[2:2 + n_ex]
        o_refs = refs[2 + n_ex:2 + n_ex + n_out]
        prod = _dot(a_ref[...].astype(BF16), b_ref[...].astype(BF16), dims)
        if nk == 1:
            finish(prod, ex_refs, o_refs)
            return
        acc_ref = refs[-1]
        kk = pl.program_id(2)

        @pl.when(kk == 0)
        def _():
            acc_ref[...] = prod

        @pl.when(kk > 0)
        def _():
            acc_ref[...] += prod

        @pl.when(kk == nk - 1)
        def _():
            finish(acc_ref[...], ex_refs, o_refs)

    a_spec = pl.BlockSpec((tk, tm), lambda j, i, kk: (kk, i)) if ta else pl.BlockSpec((tm, tk), lambda j, i, kk: (i, kk))
    b_spec = pl.BlockSpec((tn, tk), lambda j, i, kk: (j, kk)) if tb else pl.BlockSpec((tk, tn), lambda j, i, kk: (kk, j))
    o_spec = pl.BlockSpec((tm, tn), lambda j, i, kk: (i, j))
    outs = _pallas(
        body, name=name,
        grid=(n // tn, m // tm, nk),
        in_specs=[a_spec, b_spec] + [o_spec] * n_ex,
        out_specs=[o_spec] * n_out,
        out_shape=[jax.ShapeDtypeStruct((m, n), dt) for dt in out_dtypes],
        scratch_shapes=[] if nk == 1 else [pltpu.VMEM((tm, tn), F32)],
        compiler_params=_cparams(("parallel", "parallel", "arbitrary")),
    )(a, b, *extras)
    return outs[0] if n_out == 1 else outs


def _tile(n, cap):
    best = 128
    for t in range(128, min(n, cap) + 1, 128):
        if n % t == 0:
            best = t
    assert n % best == 0, n
    return best


MM_K_CAP = 4096
WGRAD_ROWS = 1536


MM_LHS_TILE_BYTES = 6 * 1024 * 1024


def _row_tile(x, depth):
    tall = 3 * ROW_TILE
    fits = tall * depth * x.dtype.itemsize <= MM_LHS_TILE_BYTES
    return tall if (x.shape[0] % tall == 0 and fits) else ROW_TILE


def _mm_fwd(x, w, name, **kw):
    k, n = w.shape
    tk = _tile(k, MM_K_CAP)
    return _matmul(x, w, tm=_row_tile(x, tk), tn=_tile(n, 1024), tk=tk, name=name, **kw)


def _mm_dgrad(dy, w, name, **kw):
    k, n = w.shape
    tk = _tile(n, MM_K_CAP)
    return _matmul(dy, w, tb=True, tm=_row_tile(dy, tk), tn=_tile(k, 1024), tk=tk, name=name, **kw)


def _mm_wgrad(x, dy, name):
    k, n = x.shape[1], dy.shape[1]
    rows = x.shape[0]
    return _matmul(x, dy, ta=True, tm=_tile(k, 512), tn=_tile(n, 1024),
                   tk=WGRAD_ROWS if rows % WGRAD_ROWS == 0 else ROW_TILE, name=name)


def _rms(x, g):
    r = lax.rsqrt(jnp.mean(x * x, axis=-1, keepdims=True) + EPS)
    return x * r * g


def _rms_bwd(x, g, dy):
    r = lax.rsqrt(jnp.mean(x * x, axis=-1, keepdims=True) + EPS)
    xh = x * r
    dxh = dy * g
    dx = r * (dxh - xh * jnp.mean(dxh * xh, axis=-1, keepdims=True))
    dg = jnp.sum(dy * xh, axis=0, keepdims=True)
    return dx, dg


def _row_spec(width, tile=ROW_TILE):
    return pl.BlockSpec((tile, width), lambda i: (i, 0))


def _vec_spec(width):
    return pl.BlockSpec((1, width), lambda i: (0, 0))


def _norm_pre(hs, g, name):
    r, d = hs.shape

    def body(x_ref, g_ref, o_ref):
        o_ref[...] = _rms(x_ref[...], g_ref[...]).astype(BF16)

    return _pallas(body, name=name, grid=(r // ROW_TILE,), in_specs=[_row_spec(d), _vec_spec(d)],
                   out_specs=_row_spec(d), out_shape=jax.ShapeDtypeStruct((r, d), BF16),
                   compiler_params=_cparams(("parallel",)))(hs, g)


def _norm_post_pre(hs, m, g_post, g_pre, name):
    r, d = hs.shape

    def body(hs_ref, m_ref, gp_ref, gn_ref, o_ref, hn_ref):
        new = hs_ref[...] + _rms(m_ref[...], gp_ref[...])
        o_ref[...] = new
        hn_ref[...] = _rms(new, gn_ref[...]).astype(BF16)

    return _pallas(body, name=name, grid=(r // ROW_TILE,),
                   in_specs=[_row_spec(d), _row_spec(d), _vec_spec(d), _vec_spec(d)],
                   out_specs=[_row_spec(d), _row_spec(d)],
                   out_shape=[jax.ShapeDtypeStruct((r, d), F32), jax.ShapeDtypeStruct((r, d), BF16)],
                   compiler_params=_cparams(("parallel",)))(hs, m, g_post, g_pre)


def _norm_post_loss(hs, m, g_post, target, pad_tiles, name):
    r, d = hs.shape
    nt = r // ROW_TILE

    def body(hs_ref, m_ref, gp_ref, t_ref, dhs_ref, loss_ref):
        i = pl.program_id(0)
        new = hs_ref[...] + _rms(m_ref[...], gp_ref[...])
        live = (i >= pad_tiles).astype(F32)
        diff = (new - t_ref[...]) * live
        dhs_ref[...] = diff * (1.0 / d)
        loss_ref[...] = jnp.full((8, 128), 0.5 / d * jnp.sum(diff * diff), F32)

    dhs, parts = _pallas(
        body, name=name, grid=(nt,),
        in_specs=[_row_spec(d), _row_spec(d), _vec_spec(d),
                  pl.BlockSpec((ROW_TILE, d), lambda i: (jnp.maximum(i - pad_tiles, 0), 0))],
        out_specs=[_row_spec(d), pl.BlockSpec((8, 128), lambda i: (i, 0))],
        out_shape=[jax.ShapeDtypeStruct((r, d), F32), jax.ShapeDtypeStruct((nt * 8, 128), F32)],
        compiler_params=_cparams(("parallel",)))(hs, m, g_post, target)
    return dhs, jnp.sum(parts[::8, 0])


def _norm_bwd(dhs, *, pre=None, post=None, pad=0, dm_dtype=BF16, dhs_cols=None, name):
    r = dhs.shape[0]
    d = dhs.shape[1] if dhs_cols is None else dhs_cols[0]
    has_pre, has_post = pre is not None, post is not None

    def body(*refs):
        it = iter(refs)
        dhs_ref = next(it)
        if has_pre:
            hs_ref, gn_ref, dhn_ref = next(it), next(it), next(it)
        if has_post:
            m_ref, gp_ref = next(it), next(it)
        if has_pre:
            o_dhs, o_dgn = next(it), next(it)
        if has_post:
            o_dm, o_dgp = next(it), next(it)
        i = pl.program_id(0)
        live = (i * ROW_TILE + lax.broadcasted_iota(jnp.int32, (ROW_TILE, 1), 0)) >= pad
        cur = jnp.where(live, dhs_ref[...], 0.0)
        if has_pre:
            dx, dg = _rms_bwd(hs_ref[...], gn_ref[...], jnp.where(live, dhn_ref[...].astype(F32), 0.0))
            cur = cur + dx
            o_dhs[...] = cur

            @pl.when(i == 0)
            def _():
                o_dgn[...] = jnp.zeros_like(o_dgn)
            o_dgn[...] += dg
        if has_post:
            dm, dg = _rms_bwd(m_ref[...], gp_ref[...], cur)
            o_dm[...] = dm.astype(o_dm.dtype)

            @pl.when(i == 0)
            def _():
                o_dgp[...] = jnp.zeros_like(o_dgp)
            o_dgp[...] += dg

    dhs_spec = _row_spec(d) if dhs_cols is None else pl.BlockSpec((ROW_TILE, d), lambda i: (i, dhs_cols[1]))
    ins, in_specs, out_specs, out_shape = [dhs], [dhs_spec], [], []
    if has_pre:
        ins += list(pre)
        in_specs += [_row_spec(d), _vec_spec(d), _row_spec(d)]
        out_specs += [_row_spec(d), _vec_spec(d)]
        out_shape += [jax.ShapeDtypeStruct((r, d), F32), jax.ShapeDtypeStruct((1, d), F32)]
    if has_post:
        ins += list(post)
        in_specs += [_row_spec(d), _vec_spec(d)]
        out_specs += [_row_spec(d), _vec_spec(d)]
        out_shape += [jax.ShapeDtypeStruct((r, d), dm_dtype), jax.ShapeDtypeStruct((1, d), F32)]
    outs = list(_pallas(body, name=name, grid=(r // ROW_TILE,), in_specs=in_specs, out_specs=out_specs,
                        out_shape=out_shape, compiler_params=_cparams(("arbitrary",)))(*ins))
    dhs_new, dgn = (outs.pop(0), outs.pop(0)) if has_pre else (dhs, None)
    dm, dgp = (outs.pop(0), outs.pop(0)) if has_post else (None, None)
    return dhs_new, dm, dgn, dgp


def _softplus(z):
    return jnp.maximum(z, 0.0) + jnp.log(1.0 + jnp.exp(-jnp.abs(z)))


def _sb_consts(t):
    row = lax.broadcasted_iota(jnp.int32, (t, t), 0)
    col = lax.broadcasted_iota(jnp.int32, (t, t), 1)
    m_up = (col >= row).astype(BF16)
    m_low = (col <= row).astype(BF16)
    return m_up, m_low


def _emit_chains(chains, stages, skew):
    if skew:
        for step in range(len(chains) + len(stages) - 1):
            for si, stage in enumerate(stages):
                if 0 <= step - si < len(chains):
                    stage(chains[step - si])
    else:
        for stage in stages:
            for c in chains:
                stage(c)


def _sb_fwd(q, k, vt3, pad, name, ride=((), ())):
    r = q.shape[0]
    t = ATT_BLK
    nb = r // t
    nbp = -(-(nb + 1) // 8) * 8
    jmin = pad // t
    scale = SB_HEAD_DIM ** -0.5
    n_ride = len(ride[0])

    def body(q_ref, k_ref, vt_ref, *rest):
        ride_in, (o_ref, ss_ref), ride_out = rest[:n_ride], rest[n_ride:n_ride + 2], rest[n_ride + 2:2 * n_ride + 2]
        acc_ref, kn_ref = rest[2 * n_ride + 2:2 * n_ride + 4]
        ride_sems = rest[2 * n_ride + 4:]
        i = pl.program_id(1)
        if n_ride:
            @pl.when((pl.program_id(0) == 0) & (i == 0))
            def _():
                for cp in _exchange_copies(ride_in, ride_out, ride[1], *ride_sems):
                    cp.start()

        @pl.when(i == 0)
        def _():
            def blk(b, m):
                kb = k_ref[pl.ds(pl.multiple_of(b * t, t), t), :].astype(F32)
                return jnp.maximum(m, jnp.max(jnp.sum(kb * kb, axis=1, keepdims=True), axis=0, keepdims=True))
            kn_ref[...] = jnp.broadcast_to(lax.fori_loop(0, nb, blk, jnp.zeros((1, 1), F32)), (8, 128))

        qf = q_ref[...].astype(F32)
        z_bound = scale * jnp.sqrt(jnp.max(jnp.sum(qf * qf, axis=1, keepdims=True)) * jnp.max(kn_ref[...]))

        def need(carry):
            return jnp.maximum(jnp.max(carry[0]), jnp.max(carry[1])) + z_bound >= SB_LOG_ZERO

        qt = qf.T
        sub = lax.broadcasted_iota(jnp.int32, (128, 1), 0)
        m_up, _ = _sb_consts(t)
        kpos0 = lax.broadcasted_iota(jnp.int32, (t, 1), 0)
        qpos = i * t + lax.broadcasted_iota(jnp.int32, (1, t), 1)
        n_mid = jnp.maximum(i - 1 - jmin, 0)
        n_edge = jnp.where(i > jmin, 1, 0)
        qths = [jnp.where((sub >= 64 * h) & (sub < 64 * (h + 1)), qt * scale, 0.0).astype(BF16) for h in range(2)]
        acc_ref[...] = jnp.zeros_like(acc_ref)

        def sweep(js, carry, masked):
            kbs = [k_ref[pl.ds(pl.multiple_of(j * t, t), t), :] for j in js]
            vts = [vt_ref[0, j] for j in js]
            accs = [acc_ref[0], acc_ref[1]]
            s = list(carry)
            chains = [(n, h) for n in range(len(js)) for h in range(2)]
            masked = [masked] * len(js) if isinstance(masked, bool) else masked
            valid = [(js[n] * t + kpos0 < qpos) & (js[n] * t + kpos0 >= pad) if masked[n] else None for n in range(len(js))]
            zt, inc, saves = {}, {}, []

            def st_scores(c):
                zt[c] = _dot(kbs[c[0]], qths[c[1]])

            def st_cumsum(c):
                lk = -_softplus(zt[c])
                if masked[c[0]]:
                    lk = jnp.where(valid[c[0]], lk, 0.0)
                inc[c] = _split_dot(m_up, lk)

            def st_weights(c):
                n, h = c
                saves.append((h, js[n], s[h]))
                w = jnp.exp(zt[c] + inc[c] + s[h])
                if masked[n]:
                    w = jnp.where(valid[n], w, 0.0)
                accs[h] = accs[h] + _dot(vts[n], w.astype(BF16))
                s[h] = s[h] + inc[c][0:1, :]

            _emit_chains(chains, [st_scores, st_cumsum, st_weights], SB_FWD_SKEW)
            for h, j, val in saves:
                ss_ref[h, 0, pl.ds(j, 1), :] = val
            acc_ref[0] = accs[0]
            acc_ref[1] = accs[1]
            return tuple(s)

        zero = jnp.zeros((1, t), F32)
        bpi = SB_BLOCKS_PER_TRIP
        j, carry = lax.cond(
            i - 1 > jmin,
            lambda: (i - 2, sweep([i, i - 1], (zero, zero), [True, False])),
            lambda: (i - 1, sweep([i], (zero, zero), True)))
        def further(j, carry):
            j, carry = lax.while_loop(
                lambda st: (st[0] - bpi >= jmin) & need(st[1]),
                lambda st: (st[0] - bpi, sweep([st[0] - b for b in range(bpi)], st[1], False)), (j, carry))
            j, carry = lax.while_loop(
                lambda st: (st[0] > jmin) & need(st[1]),
                lambda st: (st[0] - 1, sweep([st[0]], st[1], False)), (j, carry))
            return lax.while_loop(
                lambda st: (st[0] == jmin) & (i > jmin) & need(st[1]),
                lambda st: (st[0] - 1, sweep([st[0]], st[1], True)), (j, carry))[0]

        j = lax.cond((j >= jmin) & need(carry), lambda: further(j, carry), lambda: j)
        first = jnp.full((1, t), j + 1, jnp.int32).astype(F32)
        ss_ref[0, 0, nbp - 1:nbp, :] = first
        ss_ref[1, 0, nbp - 1:nbp, :] = first
        acc = jnp.where(sub < 64, acc_ref[0], acc_ref[1])
        o_ref[...] = acc.T
        if n_ride:
            @pl.when((pl.program_id(0) == 3) & (i == nb - 1))
            def _():
                for cp in _exchange_copies(ride_in, ride_out, ride[1], *ride_sems):
                    cp.wait()

    hbm = pl.BlockSpec(memory_space=pl.ANY)
    outs = _pallas(
        body, name=name, grid=(4, nb),
        in_specs=[pl.BlockSpec((t, 128), lambda hp, i: (i, hp)),
                  pl.BlockSpec((r, 128), lambda hp, i: (0, hp)),
                  pl.BlockSpec((1, nb, 128, t), lambda hp, i: (hp, 0, 0, 0))] + [hbm] * n_ride,
        out_specs=[pl.BlockSpec((t, 128), lambda hp, i: (i, hp)),
                   pl.BlockSpec((2, 1, nbp, t), lambda hp, i: (hp, i, 0, 0))] + [hbm] * n_ride,
        out_shape=[jax.ShapeDtypeStruct((r, SB_WIDTH), F32),
                   jax.ShapeDtypeStruct((8, nb, nbp, t), F32)] + _exchange_shapes(*ride),
        scratch_shapes=[pltpu.VMEM((2, 128, t), F32), pltpu.VMEM((8, 128), F32)] + (_exchange_sems(n_ride) if n_ride else []),
        compiler_params=_cparams(("arbitrary", "arbitrary")),
    )(q, k, vt3, *ride[0])
    return outs[0], outs[1], list(outs[2:])


def _sb_bwd(q, k, v, kt3, ssave, do, pad, name):
    r = q.shape[0]
    t = ATT_BLK
    nb = r // t
    nbp = ssave.shape[2]
    jmin = pad // t
    scale = SB_HEAD_DIM ** -0.5

    def body(q_ref, do_ref, k_ref, v_ref, kt_ref, ss_ref, dq_ref, dk_hbm, dv_hbm, dk_acc, dv_acc, dq_acc, sem):
        hp = pl.program_id(0)
        i = pl.program_id(1)

        @pl.when(i == 0)
        def _():
            dk_acc[...] = jnp.zeros_like(dk_acc)
            dv_acc[...] = jnp.zeros_like(dv_acc)

        qf = q_ref[...].astype(F32)
        dof = do_ref[...]
        qt = qf.T
        dot_ = dof.T
        sub = lax.broadcasted_iota(jnp.int32, (128, 1), 0)
        lane = lax.broadcasted_iota(jnp.int32, (1, 128), 1)
        m_up, m_low = _sb_consts(t)
        kpos0 = lax.broadcasted_iota(jnp.int32, (t, 1), 0)
        qpos = i * t + lax.broadcasted_iota(jnp.int32, (1, t), 1)
        first = jnp.clip(jnp.max(ss_ref[0, 0, nbp - 1:nbp, :]).astype(jnp.int32), jmin, i)
        mid0 = jnp.maximum(first, jmin + 1)
        pair = i - mid0 >= 1
        n_mid = jnp.maximum(i - mid0 - 1, 0)
        n_edge = jnp.where((i > jmin) & (first == jmin), 1, 0)
        in_t = [(sub >= 64 * h) & (sub < 64 * (h + 1)) for h in range(2)]
        in_l = [(lane >= 64 * h) & (lane < 64 * (h + 1)) for h in range(2)]
        qths = [jnp.where(in_t[h], qt * scale, 0.0).astype(BF16) for h in range(2)]
        doths = [jnp.where(in_t[h], dot_, 0.0).astype(BF16) for h in range(2)]
        qhs = [jnp.where(in_l[h], qf * scale, 0.0).astype(BF16) for h in range(2)]
        dohs = [jnp.where(in_l[h], dof, 0.0).astype(BF16) for h in range(2)]
        dq_acc[...] = jnp.zeros_like(dq_acc)

        def sweep(js, carry, masked):
            rows = [pl.ds(pl.multiple_of(j * t, t), t) for j in js]
            kbs = [k_ref[rw, :] for rw in rows]
            vbs = [v_ref[rw, :] for rw in rows]
            kts = [kt_ref[0, j] for j in js]
            sss = [[ss_ref[h, 0, pl.ds(j, 1), :] for h in range(2)] for j in js]
            dv_old = [dv_acc[rw, :] for rw in rows]
            dk_old = [dk_acc[rw, :] for rw in rows]
            dqs = [dq_acc[0], dq_acc[1]]
            ec = list(carry)
            chains = [(n, h) for n in range(len(js)) for h in range(2)]
            masked = [masked] * len(js) if isinstance(masked, bool) else masked
            valid = [(js[n] * t + kpos0 < qpos) & (js[n] * t + kpos0 >= pad) if masked[n] else None for n in range(len(js))]
            zt, dvt, sp, inc, e, big_e = {}, {}, {}, {}, {}, {}

            def st_scores(c):
                zt[c] = _dot(kbs[c[0]], qths[c[1]])
                dvt[c] = _dot(vbs[c[0]], doths[c[1]])

            def st_cumsum(c):
                sp[c] = _softplus(zt[c])
                lk = -sp[c]
                if masked[c[0]]:
                    lk = jnp.where(valid[c[0]], lk, 0.0)
                inc[c] = _split_dot(m_up, lk)

            def st_weights(c):
                n, h = c
                w = jnp.exp(zt[c] + inc[c] + sss[n][h])
                if masked[n]:
                    w = jnp.where(valid[n], w, 0.0)
                dv_old[n] = dv_old[n] + _dot(w.astype(BF16), dohs[h])
                e[c] = w * dvt[c]
                pinc = _split_dot(m_low, e[c])
                big_e[c] = pinc - e[c] + ec[h]
                ec[h] = ec[h] + pinc[t - 1:t, :]

            def st_dscores(c):
                n, h = c
                dz = e[c] - jnp.exp(zt[c] - sp[c]) * (e[c] + big_e[c])
                if masked[n]:
                    dz = jnp.where(valid[n], dz, 0.0)
                dzb = dz.astype(BF16)
                dqs[h] = dqs[h] + _dot(kts[n], dzb)
                dk_old[n] = dk_old[n] + _dot(dzb, qhs[h])

            _emit_chains(chains, [st_scores, st_cumsum, st_weights, st_dscores], SB_BWD_SKEW)
            for n, rw in enumerate(rows):
                dv_acc[rw, :] = dv_old[n]
                dk_acc[rw, :] = dk_old[n]
            dq_acc[0] = dqs[0]
            dq_acc[1] = dqs[1]
            return tuple(ec)

        zero = jnp.zeros((1, t), F32)
        bpi = SB_BLOCKS_PER_TRIP
        carry = lax.fori_loop(0, n_edge, lambda it, c: sweep([jmin + it * 0], c, True), (zero, zero))
        carry = lax.fori_loop(0, n_mid // bpi, lambda it, c: sweep([mid0 + bpi * it + b for b in range(bpi)], c, False), carry)
        n_rem = n_mid % bpi
        carry = lax.fori_loop(0, n_rem, lambda it, c: sweep([i - 1 - n_rem + it], c, False), carry)
        lax.cond(pair, lambda: sweep([i - 1, i], carry, [False, True]), lambda: sweep([i], carry, True))
        dq_ref[...] = (jnp.where(sub < 64, dq_acc[0], dq_acc[1]) * scale).T

        @pl.when(i == nb - 1)
        def _():
            c1 = pltpu.make_async_copy(dk_acc, dk_hbm.at[hp], sem.at[0])
            c2 = pltpu.make_async_copy(dv_acc, dv_hbm.at[hp], sem.at[1])
            c1.start()
            c2.start()
            c1.wait()
            c2.wait()

    return _pallas(
        body, name=name, grid=(4, nb),
        in_specs=[pl.BlockSpec((t, 128), lambda hp, i: (i, hp)),
                  pl.BlockSpec((t, 128), lambda hp, i: (i, hp)),
                  pl.BlockSpec((r, 128), lambda hp, i: (0, hp)),
                  pl.BlockSpec((r, 128), lambda hp, i: (0, hp)),
                  pl.BlockSpec((1, nb, 128, t), lambda hp, i: (hp, 0, 0, 0)),
                  pl.BlockSpec((2, 1, nbp, t), lambda hp, i: (hp, i, 0, 0))],
        out_specs=[pl.BlockSpec((t, 128), lambda hp, i: (i, hp)),
                   pl.BlockSpec(memory_space=pl.ANY), pl.BlockSpec(memory_space=pl.ANY)],
        out_shape=[jax.ShapeDtypeStruct((r, SB_WIDTH), F32),
                   jax.ShapeDtypeStruct((4, r, 128), F32), jax.ShapeDtypeStruct((4, r, 128), F32)],
        scratch_shapes=[pltpu.VMEM((r, 128), F32), pltpu.VMEM((r, 128), F32), pltpu.VMEM((2, 128, t), F32),
                        pltpu.SemaphoreType.DMA((2,))],
        compiler_params=_cparams(("arbitrary", "arbitrary")),
    )(q, do, k, v, kt3, ssave)


def _s5_disc(lam_re, lam_im, logdt, btr, bti):
    lr = jnp.minimum(lam_re, -1e-4)
    li = lam_im
    dt = jnp.exp(logdt)
    mag = jnp.exp(lr * dt)
    ang = li * dt
    a_re, a_im = mag * jnp.cos(ang), mag * jnp.sin(ang)
    den = lr * lr + li * li
    nr, ni = a_re - 1.0, a_im
    c_re = (nr * lr + ni * li) / den
    c_im = (ni * lr - nr * li) / den
    return a_re, a_im, c_re * btr - c_im * bti, c_re * bti + c_im * btr


def _s5_prep(lam_re, lam_im, logdt, btr, bti, name):
    ns = lam_re.shape[1]

    def body(lr_ref, li_ref, dt_ref, br_ref, bi_ref, ar_ref, ai_ref, bbr_ref, bbi_ref):
        ar, ai, bbr, bbi = _s5_disc(lr_ref[...], li_ref[...], dt_ref[...], br_ref[...], bi_ref[...])
        ar_ref[...] = ar
        ai_ref[...] = ai
        bbr_ref[...] = bbr
        bbi_ref[...] = bbi

    return _pallas(body, name=name,
                   out_shape=[jax.ShapeDtypeStruct((1, ns), F32)] * 2 + [jax.ShapeDtypeStruct((S5_GROUP, ns), F32)] * 2,
                   )(lam_re, lam_im, logdt, btr, bti)


def _s5_prep_bwd(lam_re, lam_im, logdt, btr, bti, dar, dai, dbbr, dbbi, name):
    ns = lam_re.shape[1]

    def body(lr_ref, li_ref, dt_ref, br_ref, bi_ref, dar_ref, dai_ref, dbr_ref, dbi_ref, o_lr, o_li, o_dt, o_br, o_bi):
        _, vjp = jax.vjp(_s5_disc, lr_ref[...], li_ref[...], dt_ref[...], br_ref[...], bi_ref[...])
        g = vjp((dar_ref[...], dai_ref[...], dbr_ref[...], dbi_ref[...]))
        o_lr[...] = g[0]
        o_li[...] = g[1]
        row = lax.broadcasted_iota(jnp.int32, (ns, ns), 0) // S5_STATE
        col = lax.broadcasted_iota(jnp.int32, (ns, ns), 1) // S5_STATE
        same = (row == col).astype(F32)
        o_dt[...] = _dot_hi(jnp.broadcast_to(g[2], (8, ns)), same)[0:1]
        o_br[...] = g[3]
        o_bi[...] = g[4]

    return _pallas(body, name=name,
                   out_shape=[jax.ShapeDtypeStruct((1, ns), F32)] * 3 + [jax.ShapeDtypeStruct((S5_GROUP, ns), F32)] * 2,
                   compiler_params=pltpu.CompilerParams(vmem_limit_bytes=VMEM_LIMIT),
                   )(lam_re, lam_im, logdt, btr, bti, dar, dai, dbbr, dbbi)


def _s5_scan(br, bi, ar, ai, t, reverse=False, carry=None):
    ng = t // 8
    ns = br.shape[1]
    br, bi = br.reshape(ng, 8, ns), bi.reshape(ng, 8, ns)
    row8 = lax.broadcasted_iota(jnp.int32, (1, 8, 1), 1)
    pr, pi_ = ar, ai
    for k in (1, 2, 4):
        if reverse:
            sr, si, ok = pltpu.roll(br, 8 - k, 1), pltpu.roll(bi, 8 - k, 1), row8 < 8 - k
        else:
            sr, si, ok = pltpu.roll(br, k, 1), pltpu.roll(bi, k, 1), row8 >= k
        sr = jnp.where(ok, sr, 0.0)
        si = jnp.where(ok, si, 0.0)
        br, bi = br + pr * sr - pi_ * si, bi + pr * si + pi_ * sr
        pr, pi_ = pr * pr - pi_ * pi_, 2.0 * pr * pi_
    pw_r, pw_i = [ar], [ai]
    for _ in range(7):
        pw_r.append(pw_r[-1] * ar - pw_i[-1] * ai)
        pw_i.append(pw_r[-2] * ai + pw_i[-1] * ar)
    if reverse:
        pw_r.reverse()
        pw_i.reverse()
    p8r, p8i = jnp.concatenate(pw_r, axis=0), jnp.concatenate(pw_i, axis=0)
    out_r, out_i = [None] * ng, [None] * ng
    order = range(ng - 1, -1, -1) if reverse else range(ng)
    edge = 0 if reverse else 7
    for g in order:
        gr, gi = br[g], bi[g]
        if carry is not None:
            cr, ci = carry
            gr, gi = gr + p8r * cr - p8i * ci, gi + p8r * ci + p8i * cr
        out_r[g], out_i[g] = gr, gi
        carry = (gr[edge:edge + 1], gi[edge:edge + 1])
    return jnp.concatenate(out_r, axis=0), jnp.concatenate(out_i, axis=0)


def _s5_prev_rows(x, first, t):
    ng = t // 8
    ns = x.shape[1]
    x3 = x.reshape(ng, 8, ns)
    last = x3[:, 7:8, :]
    before = jnp.concatenate([first.reshape(1, 1, ns), last[:ng - 1]], axis=0)
    row8 = lax.broadcasted_iota(jnp.int32, (1, 8, 1), 1)
    return jnp.where(row8 == 0, before, pltpu.roll(x3, 1, 1)).reshape(t, ns)


_GELU_C = math.sqrt(2.0 / math.pi)


def _gelu(y):
    th = jnp.tanh(_GELU_C * (y + 0.044715 * y * y * y))
    return 0.5 * y * (1.0 + th), th


def _sigmoid(x):
    return 1.0 / (1.0 + jnp.exp(-x))


def _s5_fwd(u, wb, a, wc, dskip, wglu, bglu, gnorm, name):
    r = u.shape[0]
    t = S5_TILE
    nt = r // t
    ns = wb.shape[2]
    w = S5_WIDTH

    def body(u_ref, wb_ref, a_ref, wc_ref, d_ref, wg_ref, bg_ref, gn_ref, y_ref, on_ref, xs_ref, carry_ref):
        i = pl.program_id(0)
        ar, ai = a_ref[0], a_ref[1]

        @pl.when(i == 0)
        def _():
            carry_ref[...] = jnp.zeros_like(carry_ref)

        u_ = u_ref[...]
        ub = u_.astype(BF16)
        xs_ref[0] = carry_ref[:, 0, :]
        chunks = list(range(S5_CHUNKS))
        sl_s = [slice(c * (ns // S5_CHUNKS), (c + 1) * (ns // S5_CHUNKS)) for c in chunks]
        sl_u = [slice(c * (w // S5_CHUNKS), (c + 1) * (w // S5_CHUNKS)) for c in chunks]
        bu, xs, ys = {}, {}, {}

        def st_inputs(c):
            bu[c] = (_dot(ub[:, sl_u[c]], wb_ref[0, sl_u[c], sl_s[c]]), _dot(ub[:, sl_u[c]], wb_ref[1, sl_u[c], sl_s[c]]))

        def st_scan(c):
            xr, xi = _s5_scan(*bu[c], ar[:, sl_s[c]], ai[:, sl_s[c]], t, carry=(carry_ref[0, :, sl_s[c]], carry_ref[1, :, sl_s[c]]))
            carry_ref[0, :, sl_s[c]] = xr[t - 1:t, :]
            carry_ref[1, :, sl_s[c]] = xi[t - 1:t, :]
            xs[c] = (xr.astype(BF16), xi.astype(BF16))

        def st_outputs(c):
            ys[c] = _dot(xs[c][0], wc_ref[0, sl_s[c], sl_u[c]]) - _dot(xs[c][1], wc_ref[1, sl_s[c], sl_u[c]])

        _emit_chains(chunks, [st_inputs, st_scan, st_outputs], False)
        y = jnp.concatenate([ys[c] for c in chunks], axis=1) + d_ref[...] * u_
        h, _ = _gelu(y)
        gate = _sigmoid(_dot(h.astype(BF16), wg_ref[...]) + bg_ref[...])
        y_ref[...] = y
        on_ref[...] = _rms(h * gate, gn_ref[...]).astype(BF16)

    full = lambda shape: pl.BlockSpec(shape, lambda i: (0,) * len(shape))
    return _pallas(
        body, name=name, grid=(nt,),
        in_specs=[_row_spec(w, t), full((2, w, ns)), full((2, 1, ns)), full((2, ns, w)), full((1, w)),
                  full((w, w)), full((1, w)), full((1, w))],
        out_specs=[_row_spec(w, t), _row_spec(w, t), pl.BlockSpec((1, 2, ns), lambda i: (i, 0, 0))],
        out_shape=[jax.ShapeDtypeStruct((r, w), F32), jax.ShapeDtypeStruct((r, w), BF16),
                   jax.ShapeDtypeStruct((nt, 2, ns), F32)],
        scratch_shapes=[pltpu.VMEM((2, 1, ns), F32)],
        compiler_params=_cparams(("arbitrary",)),
    )(u, wb, a, wc, dskip, wglu, bglu, gnorm)


def _s5_bwd(u, y, don, xstart, wb, a, wc, dskip, wglu, bglu, gnorm, name, ride=((), ()), don_block=0):
    r = u.shape[0]
    t = S5_TILE
    nt = r // t
    ns = wb.shape[2]
    w = S5_WIDTH
    nt_dims = ((1,), (1,))
    tn_dims = ((0,), (0,))

    def body(u_ref, y_ref, don_ref, xs_ref, wb_hbm, a_ref, wc_hbm, d_ref, wg_ref, bg_ref, gn_ref,
             du_ref, da_ref, dd_ref, dbg_ref, dgn_ref, dwb_hbm, dwc_hbm, dwg_hbm,
             wb_ref, wc_ref, lam_ref, acc_wb, acc_wc, acc_wg, sem):
        i = pl.program_id(0)
        ar, ai = a_ref[0], a_ref[1]

        @pl.when(i == 0)
        def _():
            c1 = pltpu.make_async_copy(wb_hbm, wb_ref, sem.at[0])
            c2 = pltpu.make_async_copy(wc_hbm, wc_ref, sem.at[1])
            c1.start()
            c2.start()
            lam_ref[...] = jnp.zeros_like(lam_ref)
            acc_wb[...] = jnp.zeros_like(acc_wb)
            acc_wc[...] = jnp.zeros_like(acc_wc)
            acc_wg[...] = jnp.zeros_like(acc_wg)
            da_ref[...] = jnp.zeros_like(da_ref)
            dd_ref[...] = jnp.zeros_like(dd_ref)
            dbg_ref[...] = jnp.zeros_like(dbg_ref)
            dgn_ref[...] = jnp.zeros_like(dgn_ref)
            c1.wait()
            c2.wait()

        u_ = u_ref[...]
        y_ = y_ref[...]
        ub = u_.astype(BF16)
        h, th = _gelu(y_)
        hb = h.astype(BF16)
        wg = wg_ref[...]
        gate = _sigmoid(_dot(hb, wg) + bg_ref[...])
        d_out, dgn = _rms_bwd(h * gate, gn_ref[...], don_ref[...])
        dgn_ref[...] += dgn
        dhw = d_out * h * gate * (1.0 - gate)
        dhwb = dhw.astype(BF16)
        dh = d_out * gate + _dot(dhwb, wg, nt_dims)
        acc_wg[...] += _dot(hb, dhwb, tn_dims)
        dbg_ref[...] += jnp.sum(dhw, axis=0, keepdims=True)
        dgelu = 0.5 * (1.0 + th) + 0.5 * y_ * (1.0 - th * th) * _GELU_C * (1.0 + 3.0 * 0.044715 * y_ * y_)
        dy = dh * dgelu
        dd_ref[...] += jnp.sum(dy * u_, axis=0, keepdims=True)
        dyb = dy.astype(BF16)
        chunks = list(range(S5_CHUNKS))
        sl_s = [slice(c * (ns // S5_CHUNKS), (c + 1) * (ns // S5_CHUNKS)) for c in chunks]
        sl_u = [slice(c * (w // S5_CHUNKS), (c + 1) * (w // S5_CHUNKS)) for c in chunks]
        bu, gx, x_, lam, dus = {}, {}, {}, {}, {}

        def st_inputs(c):
            su, ss = sl_u[c], sl_s[c]
            bu[c] = (_dot(ub[:, su], wb_ref[0, su, ss]), _dot(ub[:, su], wb_ref[1, su, ss]))
            gx[c] = (_dot(dyb[:, su], wc_ref[0, ss, su], nt_dims), -_dot(dyb[:, su], wc_ref[1, ss, su], nt_dims))

        def st_states(c):
            su, ss = sl_u[c], sl_s[c]
            first = (xs_ref[0, 0:1, ss], xs_ref[0, 1:2, ss])
            xr, xi = _s5_scan(*bu[c], ar[:, ss], ai[:, ss], t, carry=first)
            acc_wc[0, ss, su] += _dot(xr.astype(BF16), dyb[:, su], tn_dims)
            acc_wc[1, ss, su] -= _dot(xi.astype(BF16), dyb[:, su], tn_dims)
            x_[c] = (_s5_prev_rows(xr, first[0], t), _s5_prev_rows(xi, first[1], t))

        def st_adjoint(c):
            su, ss = sl_u[c], sl_s[c]
            lr, li = _s5_scan(*gx[c], ar[:, ss], -ai[:, ss], t, reverse=True, carry=(lam_ref[0, :, ss], lam_ref[1, :, ss]))
            lam_ref[0, :, ss] = lr[0:1, :]
            lam_ref[1, :, ss] = li[0:1, :]
            lrb, lib = lr.astype(BF16), li.astype(BF16)
            acc_wb[0, su, ss] += _dot(ub[:, su], lrb, tn_dims)
            acc_wb[1, su, ss] += _dot(ub[:, su], lib, tn_dims)
            dus[c] = _dot(lrb, wb_ref[0, su, ss], nt_dims) + _dot(lib, wb_ref[1, su, ss], nt_dims)
            lam[c] = (lr, li)

        def st_decay(c):
            ss = sl_s[c]
            (lr, li), (xpr, xpi) = lam[c], x_[c]
            da_ref[0, :, ss] += jnp.sum(lr * xpr + li * xpi, axis=0, keepdims=True)
            da_ref[1, :, ss] += jnp.sum(li * xpr - lr * xpi, axis=0, keepdims=True)

        _emit_chains(chunks, [st_inputs, st_states, st_adjoint, st_decay], False)
        du_ref[...] = d_ref[...] * dy + jnp.concatenate([dus[c] for c in chunks], axis=1)

        @pl.when(i == nt - 1)
        def _():
            cps = [pltpu.make_async_copy(acc_wb, dwb_hbm, sem.at[0]), pltpu.make_async_copy(acc_wc, dwc_hbm, sem.at[1]),
                   pltpu.make_async_copy(acc_wg, dwg_hbm, sem.at[2])]
            for c in cps:
                c.start()
            for c in cps:
                c.wait()

    n_ride = len(ride[0])
    n_in, n_out, n_scratch = 11, 8, 7

    def body_with_ride(*refs):
        ins, rest = refs[:n_in], refs[n_in:]
        ride_in, rest = rest[:n_ride], rest[n_ride:]
        outs, rest = rest[:n_out], rest[n_out:]
        ride_out, rest = rest[:n_ride], rest[n_ride:]
        scratch, ride_sems = rest[:n_scratch], rest[n_scratch:]
        if n_ride:
            @pl.when(pl.program_id(0) == 0)
            def _():
                for cp in _exchange_copies(ride_in, ride_out, ride[1], *ride_sems):
                    cp.start()
        body(*ins, *outs, *scratch)
        if n_ride:
            @pl.when(pl.program_id(0) == nt - 1)
            def _():
                for cp in _exchange_copies(ride_in, ride_out, ride[1], *ride_sems):
                    cp.wait()

    rev = lambda i: (nt - 1 - i, 0)
    full = lambda shape: pl.BlockSpec(shape, lambda i: (0,) * len(shape))
    hbm = pl.BlockSpec(memory_space=pl.ANY)
    outs = _pallas(
        body_with_ride, name=name, grid=(nt,),
        in_specs=[pl.BlockSpec((t, w), rev), pl.BlockSpec((t, w), rev), pl.BlockSpec((t, w), lambda i: (nt - 1 - i, don_block)),
                  pl.BlockSpec((1, 2, ns), lambda i: (nt - 1 - i, 0, 0)), hbm, full((2, 1, ns)), hbm, full((1, w)),
                  full((w, w)), full((1, w)), full((1, w))] + [hbm] * n_ride,
        out_specs=[pl.BlockSpec((t, w), rev), full((2, 1, ns)), full((1, w)), full((1, w)), full((1, w)), hbm, hbm, hbm]
        + [hbm] * n_ride,
        out_shape=[jax.ShapeDtypeStruct((r, w), F32), jax.ShapeDtypeStruct((2, 1, ns), F32)]
        + [jax.ShapeDtypeStruct((1, w), F32)] * 3
        + [jax.ShapeDtypeStruct((2, w, ns), F32), jax.ShapeDtypeStruct((2, ns, w), F32), jax.ShapeDtypeStruct((w, w), F32)]
        + _exchange_shapes(*ride),
        scratch_shapes=[pltpu.VMEM((2, w, ns), BF16), pltpu.VMEM((2, ns, w), BF16), pltpu.VMEM((2, 1, ns), F32),
                        pltpu.VMEM((2, w, ns), F32), pltpu.VMEM((2, ns, w), F32), pltpu.VMEM((w, w), F32),
                        pltpu.SemaphoreType.DMA((3,))] + (_exchange_sems(n_ride) if n_ride else []),
        compiler_params=_cparams(("arbitrary",)),
    )(u, y, don, xstart, wb, a, wc, dskip, wglu, bglu, gnorm, *ride[0])
    return tuple(outs[:n_out]) + (list(outs[n_out:]),)


def _s5_expand(lam_re, lam_im, log_dt, b_re, b_im, c_re, c_im):
    g, n, p = S5_GROUPS, S5_STATE, S5_GROUP
    ns = g * n
    rows = lambda x: x.reshape(1, ns)
    logdt = jnp.repeat(log_dt.reshape(g), n).reshape(1, ns)
    btr = b_re.reshape(ns, p).T
    bti = b_im.reshape(ns, p).T
    ctr = c_re.transpose(0, 2, 1).reshape(ns, p)
    cti = c_im.transpose(0, 2, 1).reshape(ns, p)
    mask = (jnp.arange(g * p)[:, None] // p) == (jnp.arange(ns)[None, :] // n)
    return rows(lam_re), rows(lam_im), logdt, btr, bti, ctr, cti, mask


def _s5_block_diag_b(bb, mask):
    return jnp.where(mask, jnp.tile(bb, (S5_GROUPS, 1)), 0.0)


def _s5_block_diag_c(ct, mask):
    return jnp.where(mask.T, jnp.tile(ct, (1, S5_GROUPS)), 0.0)


def _s5_diag_of_b(dwb, mask):
    return jnp.where(mask, dwb, 0.0).reshape(S5_GROUPS, S5_GROUP, -1).sum(0)


def _s5_diag_of_c(dwc, mask):
    ns = dwc.shape[0]
    return jnp.where(mask.T, dwc, 0.0).reshape(ns, S5_GROUPS, S5_GROUP).sum(1)


DN_PRE_TILE = 256
_DN_QKV = 3 * DN_WIDTH


def _halo_specs(width, tile, nt, prev):
    per = tile // 8
    if prev:
        return pl.BlockSpec((8, width), lambda i: (jnp.maximum(i * per - 1, 0), 0))
    return pl.BlockSpec((8, width), lambda i: (jnp.minimum((i + 1) * per, nt * per - 1), 0))


def _shift_down(x, halo, s, t):
    xx = jnp.concatenate([halo, x], axis=0)
    return pltpu.roll(xx, s, 0)[8:]


def _shift_up(x, halo, s, t):
    xx = jnp.concatenate([x, halo], axis=0)
    return pltpu.roll(xx, t + 8 - s, 0)[:t]


def _silu(x):
    s = _sigmoid(x)
    return x * s, s


def _dn_gates(ab, alog, dtb, live):
    lane = lax.broadcasted_iota(jnp.int32, (1, 128), 1)
    g = -jnp.exp(alog) * _softplus(ab + dtb)
    beta = _sigmoid(ab)
    return jnp.where(live & (lane < DN_HEADS), g, jnp.where(live & (lane < 2 * DN_HEADS), beta, 0.0))


def _dn_pre_fwd(proj, ab, conv_w, alog, dtb, pad, name):
    r = proj.shape[0]
    t = DN_PRE_TILE
    nt = r // t
    scale = DN_HEAD_DIM ** -0.5

    def body(x_ref, halo_ref, ab_ref, w_ref, al_ref, dt_ref, q_ref, k_ref, v_ref, gb_ref):
        i = pl.program_id(0)
        act, _ = _silu(_dn_conv(x_ref[...], jnp.where(i > 0, halo_ref[...], 0.0), w_ref[...], t))
        for hd in range(DN_HEADS):
            sl = slice(hd * 128, (hd + 1) * 128)
            for base, o_ref, sc in ((0, q_ref, scale), (DN_WIDTH, k_ref, 1.0)):
                xh = act[:, base + hd * 128: base + (hd + 1) * 128]
                o_ref[:, sl] = (xh * (lax.rsqrt(jnp.sum(xh * xh, axis=-1, keepdims=True) + EPS) * sc)).astype(BF16)
        v_ref[...] = act[:, 2 * DN_WIDTH:].astype(BF16)
        rows = i * t + lax.broadcasted_iota(jnp.int32, (t, 1), 0)
        gb_ref[...] = _dn_gates(ab_ref[...], al_ref[...], dt_ref[...], rows >= pad)

    return _pallas(
        body, name=name, grid=(nt,),
        in_specs=[pl.BlockSpec((t, _DN_QKV), lambda i: (i, 0)), _halo_specs(_DN_QKV, t, nt, True), _row_spec(128, t),
                  pl.BlockSpec((DN_CONV, _DN_QKV), lambda i: (0, 0)), _vec_spec(128), _vec_spec(128)],
        out_specs=[_row_spec(DN_WIDTH, t), _row_spec(DN_WIDTH, t), _row_spec(DN_WIDTH, t), _row_spec(128, t)],
        out_shape=[jax.ShapeDtypeStruct((r, DN_WIDTH), BF16)] * 3 + [jax.ShapeDtypeStruct((r, 128), F32)],
        compiler_params=_cparams(("parallel",)),
    )(proj, proj, ab, conv_w, alog, dtb)


def _dn_conv(x, halo, w, t):
    co = w[DN_CONV - 1:DN_CONV] * x
    for tap in range(DN_CONV - 1):
        co = co + w[tap:tap + 1] * _shift_down(x, halo, DN_CONV - 1 - tap, t)
    return co


def _dn_pre_bwd(proj, conv_w, dq, dk, dv, dgb, ab, alog, dtb, pad, name):
    r = proj.shape[0]
    t = DN_PRE_TILE
    nt = r // t
    scale = DN_HEAD_DIM ** -0.5

    def body(x_ref, halo_ref, w_ref, dq_ref, dk_ref, dv_ref, dgb_ref, ab_ref, al_ref, dt_ref, dco_ref, dab_ref, dal_ref,
             ddt_ref):
        i = pl.program_id(0)

        @pl.when(i == 0)
        def _():
            dal_ref[...] = jnp.zeros_like(dal_ref)
            ddt_ref[...] = jnp.zeros_like(ddt_ref)

        co_ = _dn_conv(x_ref[...], jnp.where(i > 0, halo_ref[...], 0.0), w_ref[...], t)
        act, sg = _silu(co_)
        dsilu = sg * (1.0 + co_ * (1.0 - sg))
        for hd in range(DN_HEADS):
            sl = slice(hd * 128, (hd + 1) * 128)
            for base, d_ref, sc in ((0, dq_ref, scale), (DN_WIDTH, dk_ref, 1.0)):
                cs = slice(base + hd * 128, base + (hd + 1) * 128)
                xh = act[:, cs]
                rn = lax.rsqrt(jnp.sum(xh * xh, axis=-1, keepdims=True) + EPS)
                xhat = xh * rn
                dy = d_ref[:, sl]
                dx = (sc * rn) * (dy - xhat * jnp.sum(dy * xhat, axis=-1, keepdims=True))
                dco_ref[:, cs] = dx * dsilu[:, cs]
        dco_ref[:, 2 * DN_WIDTH:] = dv_ref[...] * dsilu[:, 2 * DN_WIDTH:]
        rows = i * t + lax.broadcasted_iota(jnp.int32, (t, 1), 0)
        live = rows >= pad
        lane = lax.broadcasted_iota(jnp.int32, (1, 128), 1)
        ab_ = ab_ref[...]
        dgb_ = dgb_ref[...]
        is_g = live & (lane < DN_HEADS)
        is_b = live & (lane >= DN_HEADS) & (lane < 2 * DN_HEADS)
        arg = ab_ + dt_ref[...]
        ea = jnp.exp(al_ref[...])
        da = jnp.where(is_g, -dgb_ * ea * _sigmoid(arg), 0.0)
        beta = _sigmoid(ab_)
        dab_ref[...] = (da + jnp.where(is_b, dgb_ * beta * (1.0 - beta), 0.0)).astype(BF16)
        ddt_ref[...] += jnp.sum(da, axis=0, keepdims=True)
        dal_ref[...] += jnp.sum(jnp.where(is_g, -dgb_ * ea * _softplus(arg), 0.0), axis=0, keepdims=True)

    return _pallas(
        body, name=name, grid=(nt,),
        in_specs=[pl.BlockSpec((t, _DN_QKV), lambda i: (i, 0)), _halo_specs(_DN_QKV, t, nt, True),
                  pl.BlockSpec((DN_CONV, _DN_QKV), lambda i: (0, 0)),
                  _row_spec(DN_WIDTH, t), _row_spec(DN_WIDTH, t), _row_spec(DN_WIDTH, t),
                  _row_spec(128, t), _row_spec(128, t), _vec_spec(128), _vec_spec(128)],
        out_specs=[_row_spec(_DN_QKV, t), _row_spec(128, t), _vec_spec(128), _vec_spec(128)],
        out_shape=[jax.ShapeDtypeStruct((r, _DN_QKV), F32), jax.ShapeDtypeStruct((r, 128), BF16),
                   jax.ShapeDtypeStruct((1, 128), F32), jax.ShapeDtypeStruct((1, 128), F32)],
        compiler_params=_cparams(("arbitrary",)),
    )(proj, proj, conv_w, dq, dk, dv, dgb, ab, alog, dtb)


def _dn_conv_bwd(dco, proj, conv_w, dproj, name):
    r = dco.shape[0]
    t = DN_PRE_TILE
    nt = r // t

    def body(d_ref, dh_ref, x_ref, xh_ref, w_ref, dproj_in, dx_ref, dw_ref):
        i = pl.program_id(0)

        @pl.when(i == 0)
        def _():
            dw_ref[...] = jnp.zeros_like(dw_ref)

        d = d_ref[...]
        dhalo = jnp.where(i < nt - 1, dh_ref[...], 0.0)
        x = x_ref[...]
        xhalo = jnp.where(i > 0, xh_ref[...], 0.0)
        w = w_ref[...]
        dx = w[3:4] * d
        dws = [None] * DN_CONV
        dws[3] = jnp.sum(d * x, axis=0, keepdims=True)
        for tap in range(DN_CONV - 1):
            s = DN_CONV - 1 - tap
            dx = dx + w[tap:tap + 1] * _shift_up(d, dhalo, s, t)
            dws[tap] = jnp.sum(d * _shift_down(x, xhalo, s, t), axis=0, keepdims=True)
        dx_ref[...] = dx.astype(BF16)
        dw_ref[...] += jnp.concatenate(dws + [jnp.zeros((8 - DN_CONV, _DN_QKV), F32)], axis=0)

    return _pallas(
        body, name=name, grid=(nt,),
        in_specs=[_row_spec(_DN_QKV, t), _halo_specs(_DN_QKV, t, nt, False),
                  pl.BlockSpec((t, _DN_QKV), lambda i: (i, 0)), _halo_specs(_DN_QKV, t, nt, True),
                  pl.BlockSpec((DN_CONV, _DN_QKV), lambda i: (0, 0)), pl.BlockSpec(memory_space=pl.ANY)],
        out_specs=[_row_spec(_DN_QKV, t), pl.BlockSpec((8, _DN_QKV), lambda i: (0, 0))],
        out_shape=[jax.ShapeDtypeStruct(dproj.shape, BF16), jax.ShapeDtypeStruct((8, _DN_QKV), F32)],
        input_output_aliases={5: 0},
        compiler_params=_cparams(("arbitrary",)),
    )(dco, dco, proj, proj, conv_w, dproj)


def _split3(x):
    hi = x.astype(BF16)
    return hi, (x - hi.astype(F32)).astype(BF16)


def _dot3s(a, b, dims=((1,), (0,))):
    return _dot(a[0], b[0], dims) + (_dot(a[0], b[1], dims) + _dot(a[1], b[0], dims))


def _dot3(a, b, dims=((1,), (0,))):
    return _dot3s(_split3(a), _split3(b), dims)


def _dn_inverse_many(n_mats):
    c = n_mats[0].shape[0]
    row = lax.broadcasted_iota(jnp.int32, (c, c), 0)
    col = lax.broadcasted_iota(jnp.int32, (c, c), 1)
    eye = (row == col).astype(F32)
    same = row // DN_SUB == col // DN_SUB
    nds = [jnp.where(same, n, 0.0) for n in n_mats]
    nos = [n - nd for n, nd in zip(n_mats, nds)]

    def geometric(bs, order):
        xs = [eye + b for b in bs]
        sp = [_split3(b) for b in bs]
        k = 2
        while k < order:
            sp = [_split3(_dot3s(s_, s_)) for s_ in sp]
            xs = [x + _dot3s(_split3(x), s_) for x, s_ in zip(xs, sp)]
            k *= 2
        return xs

    tds = [_split3(td) for td in geometric([-nd for nd in nds], DN_SUB)]
    ms = [_dot3s(td, _split3(no)) for td, no in zip(tds, nos)]
    xs = geometric([-m for m in ms], c // DN_SUB)
    return [_dot3s(_split3(x), td) for x, td in zip(xs, tds)]


def _dn_chunk_shared(gb_ref, gbt_ref):
    c = DN_CHUNK
    row = lax.broadcasted_iota(jnp.int32, (c, c), 0)
    col = lax.broadcasted_iota(jnp.int32, (c, c), 1)
    gbv = gb_ref[...]
    gam_all = _split_dot((row >= col).astype(BF16), gbv)
    hi, lo = _split3(gbt_ref[...])
    tri_t = (row <= col).astype(BF16)
    return dict(row=row, col=col, gbv=gbv, gam_all=gam_all, gam_rows=_dot(hi, tri_t) + _dot(lo, tri_t),
                lane=lax.broadcasted_iota(jnp.int32, (1, 128), 1))


def _dn_chunk_common(q, k, v, sh, h):
    c = DN_CHUNK
    row, col, lane = sh["row"], sh["col"], sh["lane"]
    q, k, v = q.astype(F32), k.astype(F32), v.astype(F32)
    gam = jnp.sum(jnp.where(lane == h, sh["gam_all"], 0.0), axis=1, keepdims=True)
    beta = jnp.sum(jnp.where(lane == h + DN_HEADS, sh["gbv"], 0.0), axis=1, keepdims=True)
    gam_row = sh["gam_rows"][h:h + 1]
    dec = jnp.where(row >= col, jnp.exp(jnp.minimum(gam - gam_row, 0.0)), 0.0)
    kb, qb = k.astype(BF16), q.astype(BF16)
    nt_dims = ((1,), (1,))
    kk = _dot(kb, kb, nt_dims)
    qk = _dot(qb, kb, nt_dims)
    eg = jnp.exp(gam)
    gam_l = gam[c - 1:c, :]
    return dict(q=q, k=k, v=v, qb=qb, kb=kb, gam=gam, beta=beta, dec=dec, kk=kk, qk=qk, eg=eg, gam_l=gam_l,
                row=row, col=col, lane=lane, att=qk * dec, qg=q * eg, kt=k * jnp.exp(gam_l - gam),
                rhs=jnp.concatenate([v * beta, k * (beta * eg)], axis=1))


def _dn_fwd(q, k, v, gb, gbt, name):
    r = q.shape[0]
    c = DN_CHUNK
    nc = r // c
    dh = DN_HEAD_DIM
    tn_dims = ((0,), (0,))

    def body(q_ref, k_ref, v_ref, gb_ref, gbt_ref, o_ref, ss_ref, ts_ref, s_ref):
        @pl.when(pl.program_id(0) == 0)
        def _():
            s_ref[...] = jnp.zeros_like(s_ref)

        heads = list(range(DN_HEADS))
        sl = [slice(h * dh, (h + 1) * dh) for h in heads]
        sh = _dn_chunk_shared(gb_ref, gbt_ref)
        zs = [_dn_chunk_common(q_ref[:, sl[h]], k_ref[:, sl[h]], v_ref[:, sl[h]], sh, h) for h in heads]
        t_invs = _dn_inverse_many([jnp.where(sh["row"] > sh["col"], z["beta"] * z["kk"] * z["dec"], 0.0) for z in zs])
        sols = [_dot3(t_inv, z["rhs"]) for t_inv, z in zip(t_invs, zs)]
        ss = [s_ref[h] for h in heads]
        sbs = [s.astype(BF16) for s in ss]
        vnbs = [(sol[:, :dh] - _dot(sol[:, dh:].astype(BF16), sb)).astype(BF16) for sol, sb in zip(sols, sbs)]
        for h in heads:
            o_ref[:, sl[h]] = _dot(zs[h]["qg"].astype(BF16), sbs[h]) + _dot(zs[h]["att"].astype(BF16), vnbs[h])
        for h in heads:
            ss_ref[0, h] = ss[h]
            ts_ref[0, h] = t_invs[h]
            s_ref[h] = ss[h] * jnp.exp(zs[h]["gam_l"]) + _dot(zs[h]["kt"].astype(BF16), vnbs[h], tn_dims)

    blk = pl.BlockSpec((c, DN_WIDTH), lambda ci: (ci, 0))
    sav = pl.BlockSpec((1, DN_HEADS, dh, dh), lambda ci: (ci, 0, 0, 0))
    return _pallas(
        body, name=name, grid=(nc,),
        in_specs=[blk, blk, blk, pl.BlockSpec((c, 128), lambda ci: (ci, 0)), pl.BlockSpec((16, c), lambda ci: (0, ci))],
        out_specs=[blk, sav, sav],
        out_shape=[jax.ShapeDtypeStruct((r, DN_WIDTH), F32), jax.ShapeDtypeStruct((nc, DN_HEADS, dh, dh), F32),
                   jax.ShapeDtypeStruct((nc, DN_HEADS, dh, dh), F32)],
        scratch_shapes=[pltpu.VMEM((DN_HEADS, dh, dh), F32)],
        compiler_params=_cparams(("arbitrary",)),
    )(q, k, v, gb, gbt)


def _dn_bwd(q, k, v, gb, gbt, ssave, tsave, do, name):
    r = q.shape[0]
    c = DN_CHUNK
    nc = r // c
    dh = DN_HEAD_DIM
    nt_dims = ((1,), (1,))
    tn_dims = ((0,), (0,))

    def body(q_ref, k_ref, v_ref, gb_ref, gbt_ref, ss_ref, ts_ref, do_ref, dq_ref, dk_ref, dv_ref, dgb_ref, ds_ref):
        @pl.when(pl.program_id(0) == 0)
        def _():
            ds_ref[...] = jnp.zeros_like(ds_ref)

        heads = list(range(DN_HEADS))
        sl = [slice(h * dh, (h + 1) * dh) for h in heads]
        sh = _dn_chunk_shared(gb_ref, gbt_ref)
        row, col, lane = sh["row"], sh["col"], sh["lane"]
        rs = lambda x: jnp.sum(x, axis=1, keepdims=True)
        tot = lambda x: jnp.sum(rs(x), axis=0, keepdims=True)
        st = [dict() for _ in heads]
        dgb_parts = []

        def s_common(h):
            st[h].update(_dn_chunk_common(q_ref[:, sl[h]], k_ref[:, sl[h]], v_ref[:, sl[h]], sh, h))
            st[h]["t"] = _split3(ts_ref[0, h])

        def s_sol(h):
            st[h]["sol"] = _dot3s(st[h]["t"], _split3(st[h]["rhs"]))

        def s_state(h):
            z = st[h]
            sol = z["sol"]
            kcd = sol[:, dh:]
            s = ss_ref[0, h]
            sb = s.astype(BF16)
            vnb = (sol[:, :dh] - _dot(kcd.astype(BF16), sb)).astype(BF16)
            ds_next = ds_ref[h]
            dsb = ds_next.astype(BF16)
            dob = do_ref[:, sl[h]].astype(BF16)
            z["dqg"] = _dot(dob, sb, nt_dims)
            ds = _dot(z["qg"].astype(BF16), dob, tn_dims)
            z["d_att"] = jnp.where(row >= col, _dot(dob, vnb, nt_dims), 0.0)
            dvn = _dot(z["att"].astype(BF16), dob, tn_dims) + _dot(z["kt"].astype(BF16), dsb)
            z["dkt"] = _dot(vnb, dsb, nt_dims)
            eg_l = jnp.exp(z["gam_l"])
            ds = ds + ds_next * eg_l
            z["dgam_l"] = tot(ds_next * s) * eg_l
            dvnb = dvn.astype(BF16)
            dkcd = -_dot(dvnb, sb, nt_dims)
            ds_ref[h] = ds - _dot(kcd.astype(BF16), dvnb, tn_dims)
            z["dsol"] = jnp.concatenate([dvn, dkcd], axis=1)

        def s_drhs(h):
            st[h]["drhs"] = _dot3s(st[h]["t"], _split3(st[h]["dsol"]), tn_dims)

        def s_dn(h):
            z = st[h]
            z["dn"] = jnp.where(row > col, -_dot3(z["drhs"], z["sol"], nt_dims), 0.0)

        def s_rest(h):
            z = st[h]
            k_, v_, kb, qb = z["k"], z["v"], z["kb"], z["qb"]
            beta, eg, dec, kk, qk, gam, gam_l = z["beta"], z["eg"], z["dec"], z["kk"], z["qk"], z["gam"], z["gam_l"]
            dn, d_att, dqg, dkt = z["dn"], z["d_att"], z["dqg"], z["dkt"]
            drv, drk = z["drhs"][:, :dh], z["drhs"][:, dh:]
            s_rkk = rs(drk * k_)
            dv_ref[:, sl[h]] = drv * beta
            dbeta = rs(drv * v_) + s_rkk * eg + rs(dn * kk * dec)
            dk = drk * (beta * eg)
            dgam = s_rkk * beta * eg
            dkk = (dn * beta * dec).astype(BF16)
            dd = dn * beta * kk + d_att * qk
            dqk = (d_att * dec).astype(BF16)
            dq_ref[:, sl[h]] = _dot(dqk, kb) + dqg * eg
            dk = dk + _dot(dqk, qb, tn_dims) + _dot(dkk, kb) + _dot(dkk, kb, tn_dims)
            w = dd * dec
            wh, wl = _split3(w)
            ones = jnp.ones((c, 128), BF16)
            col_sum = (_dot(wh, ones, tn_dims) + _dot(wl, ones, tn_dims))[:, 0:1]
            dgam = dgam + rs(w) - col_sum + rs(dqg * z["qg"]) - rs(dkt * z["kt"])
            dk_ref[:, sl[h]] = dk + dkt * jnp.exp(gam_l - gam)
            dgam_l = z["dgam_l"] + tot(dkt * z["kt"])
            rowc = lax.broadcasted_iota(jnp.int32, (c, 1), 0)
            dgam = dgam + jnp.where(rowc == c - 1, dgam_l, 0.0)
            dg = _split_dot((row <= col).astype(BF16), jnp.broadcast_to(dgam, (c, 128)))[:, 0:1]
            dgb_parts.append(jnp.where(lane == h, dg, 0.0) + jnp.where(lane == h + DN_HEADS, dbeta, 0.0))

        for g0 in range(0, DN_HEADS, DN_BWD_GROUP):
            _emit_chains(heads[g0:g0 + DN_BWD_GROUP], [s_common, s_sol, s_state, s_drhs, s_dn, s_rest], DN_BWD_SKEW)
        dgb = dgb_parts[0]
        for part in dgb_parts[1:]:
            dgb = dgb + part
        dgb_ref[...] = dgb

    blk = pl.BlockSpec((c, DN_WIDTH), lambda ci: (nc - 1 - ci, 0))
    sav = pl.BlockSpec((1, DN_HEADS, dh, dh), lambda ci: (nc - 1 - ci, 0, 0, 0))
    gspec = pl.BlockSpec((c, 128), lambda ci: (nc - 1 - ci, 0))
    return _pallas(
        body, name=name, grid=(nc,),
        in_specs=[blk, blk, blk, gspec, pl.BlockSpec((16, c), lambda ci: (0, nc - 1 - ci)), sav, sav, blk],
        out_specs=[blk, blk, blk, gspec],
        out_shape=[jax.ShapeDtypeStruct((r, DN_WIDTH), F32)] * 3 + [jax.ShapeDtypeStruct((r, 128), F32)],
        scratch_shapes=[pltpu.VMEM((DN_HEADS, dh, dh), F32)],
        compiler_params=_cparams(("arbitrary",)),
    )(q, k, v, gb, gbt, ssave, tsave, do)


def _dn_post_fwd(o, proj, g, name):
    r = o.shape[0]

    def body(o_ref, z_ref, g_ref, y_ref):
        g_ = g_ref[...]
        for hd in range(DN_HEADS):
            sl = slice(hd * 128, (hd + 1) * 128)
            sz, _ = _silu(z_ref[:, sl])
            y_ref[:, sl] = (_rms(o_ref[:, sl], g_) * sz).astype(BF16)

    return _pallas(body, name=name, grid=(r // ROW_TILE,),
                   in_specs=[_row_spec(DN_WIDTH), pl.BlockSpec((ROW_TILE, DN_WIDTH), lambda i: (i, 3)), _vec_spec(128)],
                   out_specs=_row_spec(DN_WIDTH), out_shape=jax.ShapeDtypeStruct((r, DN_WIDTH), BF16),
                   compiler_params=_cparams(("parallel",)))(o, proj, g)


def _dn_post_bwd(o, proj, g, dy, name):
    r = o.shape[0]

    def body(o_ref, z_ref, g_ref, dy_ref, do_ref, dz_ref, dg_ref):
        @pl.when(pl.program_id(0) == 0)
        def _():
            dg_ref[...] = jnp.zeros_like(dg_ref)

        g_ = g_ref[...]
        for hd in range(DN_HEADS):
            sl = slice(hd * 128, (hd + 1) * 128)
            z_ = z_ref[:, sl]
            sz, sg = _silu(z_)
            dy_ = dy_ref[:, sl]
            o_ = o_ref[:, sl]
            dz_ref[:, sl] = (dy_ * _rms(o_, g_) * (sg * (1.0 + z_ * (1.0 - sg)))).astype(BF16)
            dx, dg = _rms_bwd(o_, g_, dy_ * sz)
            do_ref[:, sl] = dx
            dg_ref[...] += dg

    return _pallas(body, name=name, grid=(r // ROW_TILE,),
                   in_specs=[_row_spec(DN_WIDTH), pl.BlockSpec((ROW_TILE, DN_WIDTH), lambda i: (i, 3)), _vec_spec(128),
                             _row_spec(DN_WIDTH)],
                   out_specs=[_row_spec(DN_WIDTH), pl.BlockSpec((ROW_TILE, DN_WIDTH), lambda i: (i, 3)), _vec_spec(128)],
                   out_shape=[jax.ShapeDtypeStruct((r, DN_WIDTH), F32), jax.ShapeDtypeStruct((r, 4 * DN_WIDTH), BF16),
                              jax.ShapeDtypeStruct((1, 128), F32)],
                   compiler_params=_cparams(("arbitrary",)))(o, proj, g, dy)


def _exchange(arrays, scatter, name):
    n = len(arrays)

    def body(*refs):
        copies = _exchange_copies(refs[:n], refs[n:2 * n], scatter, *refs[2 * n:])
        for cp in copies:
            cp.start()
        for cp in copies:
            cp.wait()

    hbm = pl.BlockSpec(memory_space=pl.ANY)
    return _pallas(
        body, name=name, in_specs=[hbm] * n, out_specs=[hbm] * n, out_shape=_exchange_shapes(arrays, scatter),
        scratch_shapes=_exchange_sems(n),
    )(*arrays)


def _exchange_shapes(arrays, scatter):
    return [jax.ShapeDtypeStruct((N_DEV,) + (a.shape[1:] if sc else a.shape), a.dtype) for a, sc in zip(arrays, scatter)]


def _exchange_sems(n):
    return [pltpu.SemaphoreType.DMA((n * N_DEV,)), pltpu.SemaphoreType.DMA((n * N_DEV,)), pltpu.SemaphoreType.DMA((n,))]


def _exchange_copies(in_refs, out_refs, scatter, send_sems, recv_sems, local_sems):
    mx, my, mc = lax.axis_index("x"), lax.axis_index("y"), lax.axis_index("c")
    me = 4 * mx + 2 * my + mc
    copies = []
    for a in range(len(in_refs)):
        src_own = in_refs[a].at[me] if scatter[a] else in_refs[a]
        copies.append(pltpu.make_async_copy(src_own, out_refs[a].at[me], local_sems.at[a]))
        for kbits in range(1, N_DEV):
            px = lax.rem(mx + ((kbits >> 2) & 1), 2)
            py = lax.rem(my + ((kbits >> 1) & 1), 2)
            pc = lax.rem(mc + (kbits & 1), 2)
            src = in_refs[a].at[4 * px + 2 * py + pc] if scatter[a] else in_refs[a]
            copies.append(pltpu.make_async_remote_copy(
                src_ref=src, dst_ref=out_refs[a].at[me],
                send_sem=send_sems.at[a * N_DEV + kbits], recv_sem=recv_sems.at[a * N_DEV + kbits],
                device_id=(px, py, pc), device_id_type=pl.DeviceIdType.MESH))
    return copies


def _adamw(gstack, w, m, v, name):
    a, b = w.shape
    ta = a
    for t in (1024, 512, 256, 128, 64, 32, 16, 8):
        if a % t == 0 and N_DEV * t * b * 4 <= 4 * 1024 * 1024:
            ta = t
            break
    c1 = 1.0 / (1.0 - ADAM_B1 ** ADAM_STEP)
    c2 = 1.0 / (1.0 - ADAM_B2 ** ADAM_STEP)

    def body(g_ref, w_ref, m_ref, v_ref, og_ref, od_ref, om_ref, ov_ref):
        g = g_ref[0].astype(F32)
        for s in range(1, N_DEV):
            g = g + g_ref[s].astype(F32)
        m_new = ADAM_B1 * m_ref[...] + (1.0 - ADAM_B1) * g
        v_new = ADAM_B2 * v_ref[...] + (1.0 - ADAM_B2) * (g * g)
        og_ref[...] = g
        om_ref[...] = m_new
        ov_ref[...] = v_new
        od_ref[...] = -ADAM_LR * ((m_new * c1) / (jnp.sqrt(v_new * c2) + ADAM_EPS) + ADAM_WD * w_ref[...])

    spec = pl.BlockSpec((ta, b), lambda i: (i, 0))
    return _pallas(
        body, name=name, grid=(a // ta,),
        in_specs=[pl.BlockSpec((N_DEV, ta, b), lambda i: (0, i, 0)), spec, spec, spec],
        out_specs=[spec] * 4, out_shape=[jax.ShapeDtypeStruct((a, b), F32)] * 4,
        compiler_params=_cparams(("parallel",)),
    )(gstack, w, m, v)


_WEIGHTS = ['meta_tokens', 'pre_mix_norm', 'post_mix_norm', 'pre_mlp_norm', 'post_mlp_norm', 'mlp_w1', 'mlp_w2',
            'w_in_even', 'w_out_even', 'sb_out_norm', 's5_lambda_re', 's5_lambda_im', 's5_log_dt', 's5_b_re', 's5_b_im',
            's5_c_re', 's5_c_im', 's5_d', 's5_w_glu', 's5_b_glu', 's5_out_norm', 'w_in_odd', 'dn_conv_w', 'dn_a_log',
            'dn_dt_bias', 'dn_out_norm', 'w_out_odd']
_SHARDED = ['meta_tokens', 'mlp_w1', 'mlp_w2', 'w_in_even', 'w_out_even', 's5_w_glu', 'w_in_odd', 'dn_conv_w', 'w_out_odd']
_SMALL = [n for n in _WEIGHTS if n not in _SHARDED]
_GATHER_FIRST = ['meta_tokens', 'w_in_even', 's5_w_glu', 'w_out_even']
_GATHER_LATE = [n for n in _SHARDED if n not in _GATHER_FIRST]
_REDUCE_EARLY = ['mlp_w1', 'mlp_w2', 'w_in_odd', 'dn_conv_w', 'w_out_odd', 'w_out_even']


def _view2d(name, a):
    return a.reshape(-1, a.shape[-1])


def _unshard(name, g):
    if name == 'mlp_w1':
        return g.reshape(N_DEV, 2, D_MODEL, -1).transpose(1, 2, 0, 3).reshape(2, D_MODEL, D_FF)
    if name == 'mlp_w2':
        return g.reshape(N_DEV, 2, -1, D_MODEL).transpose(1, 0, 2, 3).reshape(2, D_FF, D_MODEL)
    if name in ('w_in_even', 'w_in_odd', 'dn_conv_w', 'meta_tokens'):
        return g.transpose(1, 0, 2).reshape(g.shape[1], -1)
    return g.reshape(-1, g.shape[-1])


def _to_blocks(name, full):
    if name == 'mlp_w1':
        return full.reshape(2, D_MODEL, N_DEV, -1).transpose(2, 0, 1, 3).reshape(N_DEV, 2 * D_MODEL, -1)
    if name == 'mlp_w2':
        return full.reshape(2, N_DEV, -1, D_MODEL).transpose(1, 0, 2, 3).reshape(N_DEV, -1, D_MODEL)
    if name in ('w_in_even', 'w_in_odd', 'dn_conv_w', 'meta_tokens'):
        return full.reshape(full.shape[0], N_DEV, -1).transpose(1, 0, 2)
    return full.reshape(N_DEV, -1, full.shape[-1])


def _pack(parts):
    rows = []
    for p in parts:
        flat = p.reshape(-1)
        rows.append(jnp.pad(flat, (0, (-flat.shape[0]) % 128)).reshape(-1, 128))
    return jnp.concatenate(rows, axis=0)


def _unpack(packed, like):
    out, at = [], 0
    for p in like:
        n = math.prod(p.shape)
        nrow = -(-n // 128)
        out.append(packed[at:at + nrow].reshape(-1)[:n].reshape(p.shape))
        at += nrow
    return out


def _lane_vec(x, width=128):
    flat = x.reshape(-1)
    return jnp.pad(flat, (0, width - flat.shape[0])).reshape(1, width)


def kernel(x, meta_tokens, pre_mix_norm, post_mix_norm, pre_mlp_norm, post_mlp_norm, mlp_w1, mlp_w2, w_in_even, w_out_even, sb_out_norm, s5_lambda_re, s5_lambda_im, s5_log_dt, s5_b_re, s5_b_im, s5_c_re, s5_c_im, s5_d, s5_w_glu, s5_b_glu, s5_out_norm, w_in_odd, dn_conv_w, dn_a_log, dn_dt_bias, dn_out_norm, w_out_odd, loss_target, m_meta_tokens, m_pre_mix_norm, m_post_mix_norm, m_pre_mlp_norm, m_post_mlp_norm, m_mlp_w1, m_mlp_w2, m_w_in_even, m_w_out_even, m_sb_out_norm, m_s5_lambda_re, m_s5_lambda_im, m_s5_log_dt, m_s5_b_re, m_s5_b_im, m_s5_c_re, m_s5_c_im, m_s5_d, m_s5_w_glu, m_s5_b_glu, m_s5_out_norm, m_w_in_odd, m_dn_conv_w, m_dn_a_log, m_dn_dt_bias, m_dn_out_norm, m_w_out_odd, v_meta_tokens, v_pre_mix_norm, v_post_mix_norm, v_pre_mlp_norm, v_post_mlp_norm, v_mlp_w1, v_mlp_w2, v_w_in_even, v_w_out_even, v_sb_out_norm, v_s5_lambda_re, v_s5_lambda_im, v_s5_log_dt, v_s5_b_re, v_s5_b_im, v_s5_c_re, v_s5_c_im, v_s5_d, v_s5_w_glu, v_s5_b_glu, v_s5_out_norm, v_w_in_odd, v_dn_conv_w, v_dn_a_log, v_dn_dt_bias, v_dn_out_norm, v_w_out_odd):
    given = dict(locals())
    w = {n: given[n] for n in _WEIGHTS}
    mom_m = {n: given["m_" + n] for n in _WEIGHTS}
    mom_v = {n: given["v_" + n] for n in _WEIGHTS}

    seq = x.shape[1]
    assert x.shape[0] == 1 and seq % ROW_TILE == 0
    r = seq + ROW_TILE
    pad = ROW_TILE - N_META
    pad_tiles = 1

    wire = {n: (F32 if n in ('dn_conv_w', 'meta_tokens') else BF16) for n in _SHARDED}
    shard_wire = lambda n: _view2d(n, w[n]).astype(wire[n])
    gathered = _exchange([shard_wire(n) for n in _GATHER_FIRST], [False] * len(_GATHER_FIRST), "gather_first")
    full = {n: _unshard(n, g_) for n, g_ in zip(_GATHER_FIRST, gathered)}
    w_ie, w_oe, w_glu = full['w_in_even'], full['w_out_even'], full['s5_w_glu']
    row = lambda v_: v_.reshape(1, -1)

    hs0 = jnp.concatenate([jnp.zeros((pad, D_MODEL), F32), full['meta_tokens'], x[0]], axis=0)
    hn0 = _norm_pre(hs0, row(pre_mix_norm[0]), "pre_mix_0")
    qkv = _mm_fwd(hn0, w_ie[:, :3 * SB_WIDTH], "in_even_qkv", out_dtypes=(BF16,))
    u = _mm_fwd(hn0, w_ie[:, 3 * SB_WIDTH:], "in_even_u")
    q, k, v = qkv[:, :SB_WIDTH], qkv[:, SB_WIDTH:2 * SB_WIDTH], qkv[:, 2 * SB_WIDTH:]
    nb = r // ATT_BLK
    blocks_t = lambda t_: t_.reshape(nb, ATT_BLK, 4, 128).transpose(2, 0, 3, 1)
    o_sb, ssave, gathered = _sb_fwd(q, k, blocks_t(v), pad, "sb_fwd",
                                    ride=([shard_wire(n) for n in _GATHER_LATE], [False] * len(_GATHER_LATE)))
    full.update({n: _unshard(n, g_) for n, g_ in zip(_GATHER_LATE, gathered)})
    w1, w2, w_oo, conv_w = full['mlp_w1'], full['mlp_w2'], full['w_out_odd'], full['dn_conv_w']
    w_io = full['w_in_odd'][:, :4 * DN_WIDTH]
    w_ab = jnp.pad(full['w_in_odd'][:, 4 * DN_WIDTH:], ((0, 0), (0, 128 - 2 * DN_HEADS)))
    on_sb = _norm_pre(o_sb, row(sb_out_norm[0]), "sb_out_norm")

    lam_re, lam_im, logdt, btr, bti, ctr, cti, s5_mask = _s5_expand(
        s5_lambda_re[0], s5_lambda_im[0], s5_log_dt[0], s5_b_re[0], s5_b_im[0], s5_c_re[0], s5_c_im[0])
    a_re, a_im, bbr, bbi = _s5_prep(lam_re, lam_im, logdt, btr, bti, "s5_prep")
    s5_wb = jnp.stack([_s5_block_diag_b(bbr, s5_mask), _s5_block_diag_b(bbi, s5_mask)]).astype(BF16)
    s5_wc = jnp.stack([_s5_block_diag_c(ctr, s5_mask), _s5_block_diag_c(cti, s5_mask)]).astype(BF16)
    s5_a = jnp.stack([a_re, a_im])
    s5_args = (s5_wb, s5_a, s5_wc, row(s5_d[0]), w_glu, row(s5_b_glu[0]), row(s5_out_norm[0]))
    y_s5, on_s5, xstart = _s5_fwd(u, *s5_args, "s5_fwd")

    merged = jnp.concatenate([on_sb, on_s5], axis=1)
    mix0 = _mm_fwd(merged, w_oe, "out_even")
    hs1, hn1 = _norm_post_pre(hs0, mix0, row(post_mix_norm[0]), row(pre_mlp_norm[0]), "post_mix_0")
    relu2 = lambda acc: (jnp.square(jnp.maximum(acc, 0.0)), jnp.maximum(acc, 0.0))
    r0, ra0 = _mm_fwd(hn1, w1[0], "mlp_up_0", out_dtypes=(BF16, BF16), epilogue=relu2)
    m0 = _mm_fwd(r0, w2[0], "mlp_down_0")
    hs2, hn2 = _norm_post_pre(hs1, m0, row(post_mlp_norm[0]), row(pre_mix_norm[1]), "post_mlp_0")

    proj = _mm_fwd(hn2, w_io, "in_odd")
    ab = _mm_fwd(hn2, w_ab, "in_odd_gates")
    alog, dtb = _lane_vec(dn_a_log[0]), _lane_vec(dn_dt_bias[0])
    qd, kd, vd, gb = _dn_pre_fwd(proj, ab, conv_w, alog, dtb, pad, "dn_pre")
    gbt = gb[:, :2 * DN_HEADS].T
    o_dn, s_dn, t_dn = _dn_fwd(qd, kd, vd, gb, gbt, "dn_fwd")
    on_dn = _dn_post_fwd(o_dn, proj, row(dn_out_norm[0]), "dn_post")
    mix1 = _mm_fwd(on_dn, w_oo, "out_odd")
    hs3, hn3 = _norm_post_pre(hs2, mix1, row(post_mix_norm[1]), row(pre_mlp_norm[1]), "post_mix_1")
    r1, ra1 = _mm_fwd(hn3, w1[1], "mlp_up_1", out_dtypes=(BF16, BF16), epilogue=relu2)
    m1 = _mm_fwd(r1, w2[1], "mlp_down_1")
    dhs, loss_part = _norm_post_loss(hs3, m1, row(post_mlp_norm[1]), loss_target[0], pad_tiles, "post_mlp_1_loss")
    loss = lax.psum(loss_part, ("x", "y", "c"))

    g = {}
    drelu2 = lambda acc, ra: (acc * (2.0 * ra.astype(F32)),)

    def mlp_bwd(layer, hn, rr, ra, dm):
        dw2 = _mm_wgrad(rr, dm, f"mlp_down_{layer}_wgrad")
        da = _mm_dgrad(dm, w2[layer], f"mlp_down_{layer}_dgrad", out_dtypes=(BF16,), extras=(ra,), epilogue=drelu2)
        dw1 = _mm_wgrad(hn, da, f"mlp_up_{layer}_wgrad")
        return dw1, dw2, _mm_dgrad(da, w1[layer], f"mlp_up_{layer}_dgrad")

    _, dm1, _, dg_post_mlp1 = _norm_bwd(dhs, post=(m1, row(post_mlp_norm[1])), pad=pad, name="post_mlp_1_bwd")
    dw1_1, dw2_1, dhn3 = mlp_bwd(1, hn3, r1, ra1, dm1)
    dhs, dmix1, dg_pre_mlp1, dg_post_mix1 = _norm_bwd(
        dhs, pre=(hs3, row(pre_mlp_norm[1]), dhn3), post=(mix1, row(post_mix_norm[1])), pad=pad, name="post_mix_1_bwd")

    g['w_out_odd'] = _mm_wgrad(on_dn, dmix1, "out_odd_wgrad")
    d_on_dn = _mm_dgrad(dmix1, w_oo, "out_odd_dgrad")
    do_dn, dproj, dg_dn = _dn_post_bwd(o_dn, proj, row(dn_out_norm[0]), d_on_dn, "dn_post_bwd")
    dqd, dkd, dvd, dgb = _dn_bwd(qd, kd, vd, gb, gbt, s_dn, t_dn, do_dn, "dn_bwd")
    dco, dab, d_alog, d_dtb = _dn_pre_bwd(proj, conv_w, dqd, dkd, dvd, dgb, ab, alog, dtb, pad, "dn_pre_bwd")
    dproj, d_conv = _dn_conv_bwd(dco, proj, conv_w, dproj, "dn_conv_bwd")
    g['w_in_odd'] = jnp.concatenate([_mm_wgrad(hn2, dproj, "in_odd_wgrad"),
                                     _mm_wgrad(hn2, dab, "in_odd_gates_wgrad")[:, :2 * DN_HEADS]], axis=1)
    dhn2 = _mm_dgrad(dab, w_ab, "in_odd_gates_dgrad")
    dhn2 = _mm_dgrad(dproj, w_io, "in_odd_dgrad", extras=(dhn2,), epilogue=lambda acc, other: (acc + other,))
    g['dn_conv_w'] = d_conv[:DN_CONV]
    g['dn_a_log'], g['dn_dt_bias'], g['dn_out_norm'] = d_alog[0, :DN_HEADS], d_dtb[0, :DN_HEADS], dg_dn[0]

    dhs, dm0, dg_pre_mix1, dg_post_mlp0 = _norm_bwd(
        dhs, pre=(hs2, row(pre_mix_norm[1]), dhn2), post=(m0, row(post_mlp_norm[0])), pad=pad, name="post_mlp_0_bwd")
    dw1_0, dw2_0, dhn1 = mlp_bwd(0, hn1, r0, ra0, dm0)
    dhs, dmix0, dg_pre_mlp0, dg_post_mix0 = _norm_bwd(
        dhs, pre=(hs1, row(pre_mlp_norm[0]), dhn1), post=(mix0, row(post_mix_norm[0])), pad=pad, name="post_mix_0_bwd")

    g['w_out_even'] = _mm_wgrad(merged, dmix0, "out_even_wgrad")
    dmerged = _mm_dgrad(dmix0, w_oe, "out_even_dgrad")
    _, do_sb, _, dg_sb = _norm_bwd(dmerged, post=(o_sb, row(sb_out_norm[0])), pad=pad, dm_dtype=F32,
                                   dhs_cols=(SB_WIDTH, 0), name="sb_out_norm_bwd")
    dq, dk4, dv4 = _sb_bwd(q, k, v, blocks_t(k), ssave, do_sb, pad, "sb_bwd")
    unheads = lambda t_: t_.transpose(1, 0, 2).reshape(r, SB_WIDTH)
    g['mlp_w1'] = jnp.stack([dw1_0, dw1_1])
    g['mlp_w2'] = jnp.stack([dw2_0, dw2_1])
    grad_wire = lambda n: _to_blocks(n, g[n].reshape(full[n].shape)).astype(wire[n])
    du, d_a, d_d, d_bglu, dg_s5, d_wb, d_wc, g['s5_w_glu'], reduced = _s5_bwd(
        u, y_s5, dmerged, xstart, *s5_args, "s5_bwd", don_block=1,
        ride=([grad_wire(n) for n in _REDUCE_EARLY], [True] * len(_REDUCE_EARLY)))
    stacks = dict(zip(_REDUCE_EARLY, reduced))
    g_lr, g_li, g_dt, g_btr, g_bti = _s5_prep_bwd(
        lam_re, lam_im, logdt, btr, bti, d_a[0], d_a[1],
        _s5_diag_of_b(d_wb[0], s5_mask), _s5_diag_of_b(d_wb[1], s5_mask), "s5_prep_bwd")
    gg, nn, pp = S5_GROUPS, S5_STATE, S5_GROUP
    g['s5_lambda_re'], g['s5_lambda_im'] = g_lr.reshape(gg, nn), g_li.reshape(gg, nn)
    g['s5_log_dt'] = g_dt.reshape(gg, nn)[:, 0]
    g['s5_b_re'], g['s5_b_im'] = g_btr.T.reshape(gg, nn, pp), g_bti.T.reshape(gg, nn, pp)
    g['s5_c_re'] = _s5_diag_of_c(d_wc[0], s5_mask).reshape(gg, nn, pp).transpose(0, 2, 1)
    g['s5_c_im'] = _s5_diag_of_c(d_wc[1], s5_mask).reshape(gg, nn, pp).transpose(0, 2, 1)
    g['s5_d'], g['s5_b_glu'], g['s5_out_norm'], g['sb_out_norm'] = d_d[0], d_bglu[0], dg_s5[0], dg_sb[0]
    dqkvu = jnp.concatenate([dq, unheads(dk4), unheads(dv4), du], axis=1).astype(BF16)
    g['w_in_even'] = _mm_wgrad(hn0, dqkvu, "in_even_wgrad")
    dhn0 = _mm_dgrad(dqkvu, w_ie, "in_even_dgrad")
    dhs, _, dg_pre_mix0, _ = _norm_bwd(dhs, pre=(hs0, row(pre_mix_norm[0]), dhn0), pad=pad, name="pre_mix_0_bwd")

    g['meta_tokens'] = dhs[pad:pad + N_META]
    g['pre_mix_norm'] = jnp.concatenate([dg_pre_mix0, dg_pre_mix1], axis=0)
    g['post_mix_norm'] = jnp.concatenate([dg_post_mix0, dg_post_mix1], axis=0)
    g['pre_mlp_norm'] = jnp.concatenate([dg_pre_mlp0, dg_pre_mlp1], axis=0)
    g['post_mlp_norm'] = jnp.concatenate([dg_post_mlp0, dg_post_mlp1], axis=0)
    grad_x = dhs[pad + N_META:][None]

    small_like = [w[n] for n in _SMALL]
    last = [n for n in _SHARDED if n not in _REDUCE_EARLY]
    partial = [grad_wire(n) for n in last] + [_pack([g[n].reshape(w[n].shape) for n in _SMALL])]
    reduced = _exchange(partial, [True] * len(last) + [False], "reduce_last")
    stacks.update(zip(last, reduced[:-1]))
    grads, deltas, new_m, new_v = {}, {}, {}, {}
    for n in _SHARDED:
        outs = _adamw(stacks[n], _view2d(n, w[n]), _view2d(n, mom_m[n]), _view2d(n, mom_v[n]), f"adamw_{n}")
        grads[n], deltas[n], new_m[n], new_v[n] = (o.reshape(w[n].shape) for o in outs)
    outs = _adamw(reduced[-1], _pack(small_like), _pack([mom_m[n] for n in _SMALL]), _pack([mom_v[n] for n in _SMALL]),
                  "adamw_small")
    for dst, o in zip((grads, deltas, new_m, new_v), outs):
        for n, part in zip(_SMALL, _unpack(o, small_like)):
            dst[n] = part
    return (loss, grad_x, *[grads[n] for n in _WEIGHTS], *[deltas[n] for n in _WEIGHTS],
            *[new_m[n] for n in _WEIGHTS], *[new_v[n] for n in _WEIGHTS])
```

```python
import math

import jax
import jax.numpy as jnp
from jax import lax
from jax.experimental import pallas as pl
from jax.experimental.pallas import tpu as pltpu

F32 = jnp.float32
BF16 = jnp.bfloat16

D_MODEL = 1024
N_META = 16
SB_HEAD_DIM = 64
SB_WIDTH = 512
S5_WIDTH = 512
S5_GROUP = 16
S5_GROUPS = 32
S5_STATE = 64
S5_NS = S5_GROUPS * S5_STATE
DN_HEAD_DIM = 128
DN_HEADS = 8
DN_WIDTH = 1024
DN_CONV = 4
D_FF = 4096
EPS = 1e-6
N_DEV = 8

ADAM_LR = 0.001
ADAM_B1 = 0.9
ADAM_B2 = 0.999
ADAM_EPS = 1e-08
ADAM_WD = 0.01
ADAM_STEP = 10

ROW_TILE = 512
ATT_BLK = 256
SB_BLOCKS_PER_TRIP = 3
SB_LOG_ZERO = -106.0
SB_FWD_SKEW = False
SB_BWD_SKEW = True
DN_CHUNK = 128
DN_SUB = 16
S5_TILE = 128
S5_CHUNKS = 4
VMEM_LIMIT = 56 * 1024 * 1024

_HIGH = lax.Precision.HIGHEST


def _pallas(body, **kw):
    return pl.pallas_call(body, **kw)


def _cparams(sem):
    return pltpu.CompilerParams(dimension_semantics=sem, vmem_limit_bytes=VMEM_LIMIT)


def _dot(a, b, dims=((1,), (0,))):
    return lax.dot_general(a, b, (dims, ((), ())), preferred_element_type=F32)


def _dot_hi(a, b):
    return lax.dot_general(a, b, (((1,), (0,)), ((), ())), preferred_element_type=F32, precision=_HIGH)


def _split_dot(m_bf16, x):
    hi = x.astype(BF16)
    lo = (x - hi.astype(F32)).astype(BF16)
    return _dot(m_bf16, hi) + _dot(m_bf16, lo)


def _matmul(a, b, *, ta=False, tb=False, tm, tn, tk, name, out_dtypes=(F32,), extras=(), epilogue=None):
    m, k = (a.shape[1], a.shape[0]) if ta else a.shape
    n = b.shape[0] if tb else b.shape[1]
    assert (b.shape[1] if tb else b.shape[0]) == k
    assert m % tm == 0 and n % tn == 0 and k % tk == 0, (name, m, n, k, tm, tn, tk)
    nk = k // tk
    n_ex = len(extras)
    n_out = len(out_dtypes)
    dims = ((0 if ta else 1,), (1 if tb else 0,))

    def finish(acc, ex_refs, o_refs):
        outs = (acc,) if epilogue is None else epilogue(acc, *[r[...] for r in ex_refs])
        for o_ref, o in zip(o_refs, outs):
            o_ref[...] = o.astype(o_ref.dtype)

    def body(*refs):
        a_ref, b_ref = refs[0], refs[1]
        ex_refs = refs[2:2 + n_ex]
        o_refs = refs[2 + n_ex:2 + n_ex + n_out]
        prod = _dot(a_ref[...].astype(BF16), b_ref[...].astype(BF16), dims)
        if nk == 1:
            finish(prod, ex_refs, o_refs)
            return
        acc_ref = refs[-1]
        kk = pl.program_id(2)

        @pl.when(kk == 0)
        def _():
            acc_ref[...] = prod

        @pl.when(kk > 0)
        def _():
            acc_ref[...] += prod

        @pl.when(kk == nk - 1)
        def _():
            finish(acc_ref[...], ex_refs, o_refs)

    a_spec = pl.BlockSpec((tk, tm), lambda j, i, kk: (kk, i)) if ta else pl.BlockSpec((tm, tk), lambda j, i, kk: (i, kk))
    b_spec = pl.BlockSpec((tn, tk), lambda j, i, kk: (j, kk)) if tb else pl.BlockSpec((tk, tn), lambda j, i, kk: (kk, j))
    o_spec = pl.BlockSpec((tm, tn), lambda j, i, kk: (i, j))
    outs = _pallas(
        body, name=name,
        grid=(n // tn, m // tm, nk),
        in_specs=[a_spec, b_spec] + [o_spec] * n_ex,
        out_specs=[o_spec] * n_out,
        out_shape=[jax.ShapeDtypeStruct((m, n), dt) for dt in out_dtypes],
        scratch_shapes=[] if nk == 1 else [pltpu.VMEM((tm, tn), F32)],
        compiler_params=_cparams(("parallel", "parallel", "arbitrary")),
    )(a, b, *extras)
    return outs[0] if n_out == 1 else outs


def _tile(n, cap):
    best = 128
    for t in range(128, min(n, cap) + 1, 128):
        if n % t == 0:
            best = t
    assert n % best == 0, n
    return best


MM_K_CAP = 4096
WGRAD_ROWS = 1536


MM_LHS_TILE_BYTES = 6 * 1024 * 1024


def _row_tile(x, depth):
    tall = 3 * ROW_TILE
    fits = tall * depth * x.dtype.itemsize <= MM_LHS_TILE_BYTES
    return tall if (x.shape[0] % tall == 0 and fits) else ROW_TILE


def _mm_fwd(x, w, name, **kw):
    k, n = w.shape
    tk = _tile(k, MM_K_CAP)
    return _matmul(x, w, tm=_row_tile(x, tk), tn=_tile(n, 1024), tk=tk, name=name, **kw)


def _mm_dgrad(dy, w, name, **kw):
    k, n = w.shape
    tk = _tile(n, MM_K_CAP)
    return _matmul(dy, w, tb=True, tm=_row_tile(dy, tk), tn=_tile(k, 1024), tk=tk, name=name, **kw)


def _mm_wgrad(x, dy, name):
    k, n = x.shape[1], dy.shape[1]
    rows = x.shape[0]
    return _matmul(x, dy, ta=True, tm=_tile(k, 512), tn=_tile(n, 1024),
                   tk=WGRAD_ROWS if rows % WGRAD_ROWS == 0 else ROW_TILE, name=name)


def _rms(x, g):
    r = lax.rsqrt(jnp.mean(x * x, axis=-1, keepdims=True) + EPS)
    return x * r * g


def _rms_bwd(x, g, dy):
    r = lax.rsqrt(jnp.mean(x * x, axis=-1, keepdims=True) + EPS)
    xh = x * r
    dxh = dy * g
    dx = r * (dxh - xh * jnp.mean(dxh * xh, axis=-1, keepdims=True))
    dg = jnp.sum(dy * xh, axis=0, keepdims=True)
    return dx, dg


def _row_spec(width, tile=ROW_TILE):
    return pl.BlockSpec((tile, width), lambda i: (i, 0))


def _vec_spec(width):
    return pl.BlockSpec((1, width), lambda i: (0, 0))


def _norm_pre(hs, g, name):
    r, d = hs.shape

    def body(x_ref, g_ref, o_ref):
        o_ref[...] = _rms(x_ref[...], g_ref[...]).astype(BF16)

    return _pallas(body, name=name, grid=(r // ROW_TILE,), in_specs=[_row_spec(d), _vec_spec(d)],
                   out_specs=_row_spec(d), out_shape=jax.ShapeDtypeStruct((r, d), BF16),
                   compiler_params=_cparams(("parallel",)))(hs, g)


def _norm_post_pre(hs, m, g_post, g_pre, name):
    r, d = hs.shape

    def body(hs_ref, m_ref, gp_ref, gn_ref, o_ref, hn_ref):
        new = hs_ref[...] + _rms(m_ref[...], gp_ref[...])
        o_ref[...] = new
        hn_ref[...] = _rms(new, gn_ref[...]).astype(BF16)

    return _pallas(body, name=name, grid=(r // ROW_TILE,),
                   in_specs=[_row_spec(d), _row_spec(d), _vec_spec(d), _vec_spec(d)],
                   out_specs=[_row_spec(d), _row_spec(d)],
                   out_shape=[jax.ShapeDtypeStruct((r, d), F32), jax.ShapeDtypeStruct((r, d), BF16)],
                   compiler_params=_cparams(("parallel",)))(hs, m, g_post, g_pre)


def _norm_post_loss(hs, m, g_post, target, pad_tiles, name):
    r, d = hs.shape
    nt = r // ROW_TILE

    def body(hs_ref, m_ref, gp_ref, t_ref, dhs_ref, loss_ref):
        i = pl.program_id(0)
        new = hs_ref[...] + _rms(m_ref[...], gp_ref[...])
        live = (i >= pad_tiles).astype(F32)
        diff = (new - t_ref[...]) * live
        dhs_ref[...] = diff * (1.0 / d)
        loss_ref[...] = jnp.full((8, 128), 0.5 / d * jnp.sum(diff * diff), F32)

    dhs, parts = _pallas(
        body, name=name, grid=(nt,),
        in_specs=[_row_spec(d), _row_spec(d), _vec_spec(d),
                  pl.BlockSpec((ROW_TILE, d), lambda i: (jnp.maximum(i - pad_tiles, 0), 0))],
        out_specs=[_row_spec(d), pl.BlockSpec((8, 128), lambda i: (i, 0))],
        out_shape=[jax.ShapeDtypeStruct((r, d), F32), jax.ShapeDtypeStruct((nt * 8, 128), F32)],
        compiler_params=_cparams(("parallel",)))(hs, m, g_post, target)
    return dhs, jnp.sum(parts[::8, 0])


def _norm_bwd(dhs, *, pre=None, post=None, pad=0, dm_dtype=BF16, dhs_cols=None, name):
    r = dhs.shape[0]
    d = dhs.shape[1] if dhs_cols is None else dhs_cols[0]
    has_pre, has_post = pre is not None, post is not None

    def body(*refs):
        it = iter(refs)
        dhs_ref = next(it)
        if has_pre:
            hs_ref, gn_ref, dhn_ref = next(it), next(it), next(it)
        if has_post:
            m_ref, gp_ref = next(it), next(it)
        if has_pre:
            o_dhs, o_dgn = next(it), next(it)
        if has_post:
            o_dm, o_dgp = next(it), next(it)
        i = pl.program_id(0)
        live = (i * ROW_TILE + lax.broadcasted_iota(jnp.int32, (ROW_TILE, 1), 0)) >= pad
        cur = jnp.where(live, dhs_ref[...], 0.0)
        if has_pre:
            dx, dg = _rms_bwd(hs_ref[...], gn_ref[...], jnp.where(live, dhn_ref[...].astype(F32), 0.0))
            cur = cur + dx
            o_dhs[...] = cur

            @pl.when(i == 0)
            def _():
                o_dgn[...] = jnp.zeros_like(o_dgn)
            o_dgn[...] += dg
        if has_post:
            dm, dg = _rms_bwd(m_ref[...], gp_ref[...], cur)
            o_dm[...] = dm.astype(o_dm.dtype)

            @pl.when(i == 0)
            def _():
                o_dgp[...] = jnp.zeros_like(o_dgp)
            o_dgp[...] += dg

    dhs_spec = _row_spec(d) if dhs_cols is None else pl.BlockSpec((ROW_TILE, d), lambda i: (i, dhs_cols[1]))
    ins, in_specs, out_specs, out_shape = [dhs], [dhs_spec], [], []
    if has_pre:
        ins += list(pre)
        in_specs += [_row_spec(d), _vec_spec(d), _row_spec(d)]
        out_specs += [_row_spec(d), _vec_spec(d)]
        out_shape += [jax.ShapeDtypeStruct((r, d), F32), jax.ShapeDtypeStruct((1, d), F32)]
    if has_post:
        ins += list(post)
        in_specs += [_row_spec(d), _vec_spec(d)]
        out_specs += [_row_spec(d), _vec_spec(d)]
        out_shape += [jax.ShapeDtypeStruct((r, d), dm_dtype), jax.ShapeDtypeStruct((1, d), F32)]
    outs = list(_pallas(body, name=name, grid=(r // ROW_TILE,), in_specs=in_specs, out_specs=out_specs,
                        out_shape=out_shape, compiler_params=_cparams(("arbitrary",)))(*ins))
    dhs_new, dgn = (outs.pop(0), outs.pop(0)) if has_pre else (dhs, None)
    dm, dgp = (outs.pop(0), outs.pop(0)) if has_post else (None, None)
    return dhs_new, dm, dgn, dgp


def _dgrad_norm_bwd(dy, w, dhs, hs, g_pre, *, post=None, add=None, pad=0, name):
    d, n = w.shape
    r = dy.shape[0]
    assert n <= MM_K_CAP and d == dhs.shape[1]
    t = ROW_TILE // 2
    has_post, has_add = post is not None, add is not None
    dims = ((1,), (1,))

    def body(*refs):
        it = iter(refs)
        dy_ref, w_ref = next(it), next(it)
        add_ref = next(it) if has_add else None
        dhs_ref, hs_ref, gn_ref = next(it), next(it), next(it)
        if has_post:
            m_ref, gp_ref = next(it), next(it)
        o_dhs, o_dgn = next(it), next(it)
        if has_post:
            o_dm, o_dgp = next(it), next(it)
        i = pl.program_id(0)
        dhn = _dot(dy_ref[...].astype(BF16), w_ref[...].astype(BF16), dims)
        if has_add:
            dhn = dhn + add_ref[...]
        live = (i * t + lax.broadcasted_iota(jnp.int32, (t, 1), 0)) >= pad
        dx, dg = _rms_bwd(hs_ref[...], gn_ref[...], jnp.where(live, dhn, 0.0))
        cur = jnp.where(live, dhs_ref[...], 0.0) + dx
        o_dhs[...] = cur

        @pl.when(i == 0)
        def _():
            o_dgn[...] = jnp.zeros_like(o_dgn)
        o_dgn[...] += dg
        if has_post:
            dm, dg = _rms_bwd(m_ref[...], gp_ref[...], cur)
            o_dm[...] = dm.astype(BF16)

            @pl.when(i == 0)
            def _():
                o_dgp[...] = jnp.zeros_like(o_dgp)
            o_dgp[...] += dg

    ins = [dy, w] + ([add] if has_add else []) + [dhs, hs, g_pre] + (list(post) if has_post else [])
    in_specs = ([_row_spec(n, t), pl.BlockSpec((d, n), lambda i: (0, 0))] + ([_row_spec(d, t)] if has_add else [])
                + [_row_spec(d, t), _row_spec(d, t), _vec_spec(d)] + ([_row_spec(d, t), _vec_spec(d)] if has_post else []))
    out_specs = [_row_spec(d, t), _vec_spec(d)] + ([_row_spec(d, t), _vec_spec(d)] if has_post else [])
    out_shape = [jax.ShapeDtypeStruct((r, d), F32), jax.ShapeDtypeStruct((1, d), F32)]
    if has_post:
        out_shape += [jax.ShapeDtypeStruct((r, d), BF16), jax.ShapeDtypeStruct((1, d), F32)]
    outs = list(_pallas(body, name=name, grid=(r // t,), in_specs=in_specs, out_specs=out_specs,
                        out_shape=out_shape, compiler_params=_cparams(("arbitrary",)))(*ins))
    return (outs[0], outs[2], outs[1], outs[3]) if has_post else (outs[0], None, outs[1], None)


def _softplus(z):
    return jnp.maximum(z, 0.0) + jnp.log(1.0 + jnp.exp(-jnp.abs(z)))


def _sb_consts(t):
    row = lax.broadcasted_iota(jnp.int32, (t, t), 0)
    col = lax.broadcasted_iota(jnp.int32, (t, t), 1)
    m_up = (col >= row).astype(BF16)
    m_low = (col <= row).astype(BF16)
    return m_up, m_low


def _emit_chains(chains, stages, skew):
    if skew:
        for step in range(len(chains) + len(stages) - 1):
            for si, stage in enumerate(stages):
                if 0 <= step - si < len(chains):
                    stage(chains[step - si])
    else:
        for stage in stages:
            for c in chains:
                stage(c)


def _sb_fwd(q, k, vt3, pad, name, ride=((), ())):
    r = q.shape[0]
    t = ATT_BLK
    nb = r // t
    nbp = -(-(nb + 1) // 8) * 8
    jmin = pad // t
    scale = SB_HEAD_DIM ** -0.5
    n_ride = len(ride[0])

    def body(q_ref, k_ref, vt_ref, *rest):
        ride_in, (o_ref, ss_ref), ride_out = rest[:n_ride], rest[n_ride:n_ride + 2], rest[n_ride + 2:2 * n_ride + 2]
        acc_ref, kn_ref = rest[2 * n_ride + 2:2 * n_ride + 4]
        ride_sems = rest[2 * n_ride + 4:]
        i = pl.program_id(1)
        if n_ride:
            @pl.when((pl.program_id(0) == 0) & (i == 0))
            def _():
                for cp in _exchange_copies(ride_in, ride_out, ride[1], *ride_sems):
                    cp.start()

        @pl.when(i == 0)
        def _():
            def blk(b, m):
                kb = k_ref[pl.ds(pl.multiple_of(b * t, t), t), :].astype(F32)
                return jnp.maximum(m, jnp.max(jnp.sum(kb * kb, axis=1, keepdims=True), axis=0, keepdims=True))
            kn_ref[...] = jnp.broadcast_to(lax.fori_loop(0, nb, blk, jnp.zeros((1, 1), F32)), (8, 128))

        qf = q_ref[...].astype(F32)
        z_bound = scale * jnp.sqrt(jnp.max(jnp.sum(qf * qf, axis=1, keepdims=True)) * jnp.max(kn_ref[...]))

        def need(carry):
            return jnp.maximum(jnp.max(carry[0]), jnp.max(carry[1])) + z_bound >= SB_LOG_ZERO

        qt = qf.T
        sub = lax.broadcasted_iota(jnp.int32, (128, 1), 0)
        m_up, _ = _sb_consts(t)
        kpos0 = lax.broadcasted_iota(jnp.int32, (t, 1), 0)
        qpos = i * t + lax.broadcasted_iota(jnp.int32, (1, t), 1)
        qths = [jnp.where((sub >= 64 * h) & (sub < 64 * (h + 1)), qt * scale, 0.0).astype(BF16) for h in range(2)]
        acc_ref[...] = jnp.zeros_like(acc_ref)

        def sweep(js, carry, masked):
            kbs = [k_ref[pl.ds(pl.multiple_of(j * t, t), t), :] for j in js]
            vts = [vt_ref[0, j] for j in js]
            accs = [acc_ref[0], acc_ref[1]]
            s = list(carry)
            chains = [(n, h) for n in range(len(js)) for h in range(2)]
            masked = [masked] * len(js) if isinstance(masked, bool) else masked
            valid = [(js[n] * t + kpos0 < qpos) & (js[n] * t + kpos0 >= pad) if masked[n] else None for n in range(len(js))]
            zt, inc, saves = {}, {}, []

            def st_scores(c):
                zt[c] = _dot(kbs[c[0]], qths[c[1]])

            def st_cumsum(c):
                lk = -_softplus(zt[c])
                if masked[c[0]]:
                    lk = jnp.where(valid[c[0]], lk, 0.0)
                inc[c] = _split_dot(m_up, lk)

            def st_weights(c):
                n, h = c
                saves.append((h, js[n], s[h]))
                w = jnp.exp(zt[c] + inc[c] + s[h])
                if masked[n]:
                    w = jnp.where(valid[n], w, 0.0)
                accs[h] = accs[h] + _dot(vts[n], w.astype(BF16))
                s[h] = s[h] + inc[c][0:1, :]

            _emit_chains(chains, [st_scores, st_cumsum, st_weights], SB_FWD_SKEW)
            for h, j, val in saves:
                ss_ref[h, 0, pl.ds(j, 1), :] = val
            acc_ref[0] = accs[0]
            acc_ref[1] = accs[1]
            return tuple(s)

        zero = jnp.zeros((1, t), F32)
        bpi = SB_BLOCKS_PER_TRIP
        j, carry = lax.cond(
            i - 1 > jmin,
            lambda: (i - 2, sweep([i, i - 1], (zero, zero), [True, False])),
            lambda: (i - 1, sweep([i], (zero, zero), True)))
        def further(j, carry):
            j, carry = lax.while_loop(
                lambda st: (st[0] - bpi >= jmin) & need(st[1]),
                lambda st: (st[0] - bpi, sweep([st[0] - b for b in range(bpi)], st[1], False)), (j, carry))
            j, carry = lax.while_loop(
                lambda st: (st[0] > jmin) & need(st[1]),
                lambda st: (st[0] - 1, sweep([st[0]], st[1], False)), (j, carry))
            return lax.while_loop(
                lambda st: (st[0] == jmin) & (i > jmin) & need(st[1]),
                lambda st: (st[0] - 1, sweep([st[0]], st[1], True)), (j, carry))[0]

        j = lax.cond((j >= jmin) & need(carry), lambda: further(j, carry), lambda: j)
        first = jnp.full((1, t), j + 1, jnp.int32).astype(F32)
        ss_ref[0, 0, nbp - 1:nbp, :] = first
        ss_ref[1, 0, nbp - 1:nbp, :] = first
        acc = jnp.where(sub < 64, acc_ref[0], acc_ref[1])
        o_ref[...] = acc.T
        if n_ride:
            @pl.when((pl.program_id(0) == 3) & (i == nb - 1))
            def _():
                for cp in _exchange_copies(ride_in, ride_out, ride[1], *ride_sems):
                    cp.wait()

    hbm = pl.BlockSpec(memory_space=pl.ANY)
    outs = _pallas(
        body, name=name, grid=(4, nb),
        in_specs=[pl.BlockSpec((t, 128), lambda hp, i: (i, hp)),
                  pl.BlockSpec((r, 128), lambda hp, i: (0, hp)),
                  pl.BlockSpec((1, nb, 128, t), lambda hp, i: (hp, 0, 0, 0))] + [hbm] * n_ride,
        out_specs=[pl.BlockSpec((t, 128), lambda hp, i: (i, hp)),
                   pl.BlockSpec((2, 1, nbp, t), lambda hp, i: (hp, i, 0, 0))] + [hbm] * n_ride,
        out_shape=[jax.ShapeDtypeStruct((r, SB_WIDTH), F32),
                   jax.ShapeDtypeStruct((8, nb, nbp, t), F32)] + _exchange_shapes(*ride),
        scratch_shapes=[pltpu.VMEM((2, 128, t), F32), pltpu.VMEM((8, 128), F32)] + (_exchange_sems(n_ride) if n_ride else []),
        compiler_params=_cparams(("arbitrary", "arbitrary")),
    )(q, k, vt3, *ride[0])
    return outs[0], outs[1], list(outs[2:])


def _sb_bwd(q, k, v, kt3, ssave, do, pad, name):
    r = q.shape[0]
    t = ATT_BLK
    nb = r // t
    nbp = ssave.shape[2]
    jmin = pad // t
    scale = SB_HEAD_DIM ** -0.5

    def body(q_ref, do_ref, k_ref, v_ref, kt_ref, ss_ref, dq_ref, dk_hbm, dv_hbm, dk_acc, dv_acc, dq_acc, sem):
        hp = pl.program_id(0)
        i = pl.program_id(1)

        @pl.when(i == 0)
        def _():
            dk_acc[...] = jnp.zeros_like(dk_acc)
            dv_acc[...] = jnp.zeros_like(dv_acc)

        qf = q_ref[...].astype(F32)
        dof = do_ref[...]
        qt = qf.T
        dot_ = dof.T
        sub = lax.broadcasted_iota(jnp.int32, (128, 1), 0)
        lane = lax.broadcasted_iota(jnp.int32, (1, 128), 1)
        m_up, m_low = _sb_consts(t)
        kpos0 = lax.broadcasted_iota(jnp.int32, (t, 1), 0)
        qpos = i * t + lax.broadcasted_iota(jnp.int32, (1, t), 1)
        first = jnp.clip(jnp.max(ss_ref[0, 0, nbp - 1:nbp, :]).astype(jnp.int32), jmin, i)
        mid0 = jnp.maximum(first, jmin + 1)
        pair = i - mid0 >= 1
        n_mid = jnp.maximum(i - mid0 - 1, 0)
        n_edge = jnp.where((i > jmin) & (first == jmin), 1, 0)
        in_t = [(sub >= 64 * h) & (sub < 64 * (h + 1)) for h in range(2)]
        in_l = [(lane >= 64 * h) & (lane < 64 * (h + 1)) for h in range(2)]
        qths = [jnp.where(in_t[h], qt * scale, 0.0).astype(BF16) for h in range(2)]
        doths = [jnp.where(in_t[h], dot_, 0.0).astype(BF16) for h in range(2)]
        qhs = [jnp.where(in_l[h], qf * scale, 0.0).astype(BF16) for h in range(2)]
        dohs = [jnp.where(in_l[h], dof, 0.0).astype(BF16) for h in range(2)]
        dq_acc[...] = jnp.zeros_like(dq_acc)

        def sweep(js, carry, masked):
            rows = [pl.ds(pl.multiple_of(j * t, t), t) for j in js]
            kbs = [k_ref[rw, :] for rw in rows]
            vbs = [v_ref[rw, :] for rw in rows]
            kts = [kt_ref[0, j] for j in js]
            sss = [[ss_ref[h, 0, pl.ds(j, 1), :] for h in range(2)] for j in js]
            dv_old = [dv_acc[rw, :] for rw in rows]
            dk_old = [dk_acc[rw, :] for rw in rows]
            dqs = [dq_acc[0], dq_acc[1]]
            ec = list(carry)
            chains = [(n, h) for n in range(len(js)) for h in range(2)]
            masked = [masked] * len(js) if isinstance(masked, bool) else masked
            valid = [(js[n] * t + kpos0 < qpos) & (js[n] * t + kpos0 >= pad) if masked[n] else None for n in range(len(js))]
            zt, dvt, sp, inc, e, big_e = {}, {}, {}, {}, {}, {}

            def st_scores(c):
                zt[c] = _dot(kbs[c[0]], qths[c[1]])
                dvt[c] = _dot(vbs[c[0]], doths[c[1]])

            def st_cumsum(c):
                sp[c] = _softplus(zt[c])
                lk = -sp[c]
                if masked[c[0]]:
                    lk = jnp.where(valid[c[0]], lk, 0.0)
                inc[c] = _split_dot(m_up, lk)

            def st_weights(c):
                n, h = c
                w = jnp.exp(zt[c] + inc[c] + sss[n][h])
                if masked[n]:
                    w = jnp.where(valid[n], w, 0.0)
                dv_old[n] = dv_old[n] + _dot(w.astype(BF16), dohs[h])
                e[c] = w * dvt[c]
                pinc = _split_dot(m_low, e[c])
                big_e[c] = pinc - e[c] + ec[h]
                ec[h] = ec[h] + pinc[t - 1:t, :]

            def st_dscores(c):
                n, h = c
                dz = e[c] - jnp.exp(zt[c] - sp[c]) * (e[c] + big_e[c])
                if masked[n]:
                    dz = jnp.where(valid[n], dz, 0.0)
                dzb = dz.astype(BF16)
                dqs[h] = dqs[h] + _dot(kts[n], dzb)
                dk_old[n] = dk_old[n] + _dot(dzb, qhs[h])

            _emit_chains(chains, [st_scores, st_cumsum, st_weights, st_dscores], SB_BWD_SKEW)
            for n, rw in enumerate(rows):
                dv_acc[rw, :] = dv_old[n]
                dk_acc[rw, :] = dk_old[n]
            dq_acc[0] = dqs[0]
            dq_acc[1] = dqs[1]
            return tuple(ec)

        zero = jnp.zeros((1, t), F32)
        bpi = SB_BLOCKS_PER_TRIP
        carry = lax.fori_loop(0, n_edge, lambda it, c: sweep([jmin + it * 0], c, True), (zero, zero))
        carry = lax.fori_loop(0, n_mid // bpi, lambda it, c: sweep([mid0 + bpi * it + b for b in range(bpi)], c, False), carry)
        n_rem = n_mid % bpi
        carry = lax.fori_loop(0, n_rem, lambda it, c: sweep([i - 1 - n_rem + it], c, False), carry)
        lax.cond(pair, lambda: sweep([i - 1, i], carry, [False, True]), lambda: sweep([i], carry, True))
        dq_ref[...] = (jnp.where(sub < 64, dq_acc[0], dq_acc[1]) * scale).T

        @pl.when(i == nb - 1)
        def _():
            c1 = pltpu.make_async_copy(dk_acc, dk_hbm.at[hp], sem.at[0])
            c2 = pltpu.make_async_copy(dv_acc, dv_hbm.at[hp], sem.at[1])
            c1.start()
            c2.start()
            c1.wait()
            c2.wait()

    return _pallas(
        body, name=name, grid=(4, nb),
        in_specs=[pl.BlockSpec((t, 128), lambda hp, i: (i, hp)),
                  pl.BlockSpec((t, 128), lambda hp, i: (i, hp)),
                  pl.BlockSpec((r, 128), lambda hp, i: (0, hp)),
                  pl.BlockSpec((r, 128), lambda hp, i: (0, hp)),
                  pl.BlockSpec((1, nb, 128, t), lambda hp, i: (hp, 0, 0, 0)),
                  pl.BlockSpec((2, 1, nbp, t), lambda hp, i: (hp, i, 0, 0))],
        out_specs=[pl.BlockSpec((t, 128), lambda hp, i: (i, hp)),
                   pl.BlockSpec(memory_space=pl.ANY), pl.BlockSpec(memory_space=pl.ANY)],
        out_shape=[jax.ShapeDtypeStruct((r, SB_WIDTH), F32),
                   jax.ShapeDtypeStruct((4, r, 128), F32), jax.ShapeDtypeStruct((4, r, 128), F32)],
        scratch_shapes=[pltpu.VMEM((r, 128), F32), pltpu.VMEM((r, 128), F32), pltpu.VMEM((2, 128, t), F32),
                        pltpu.SemaphoreType.DMA((2,))],
        compiler_params=_cparams(("arbitrary", "arbitrary")),
    )(q, do, k, v, kt3, ssave)


def _s5_disc(lam_re, lam_im, logdt, btr, bti):
    lr = jnp.minimum(lam_re, -1e-4)
    li = lam_im
    dt = jnp.exp(logdt)
    mag = jnp.exp(lr * dt)
    ang = li * dt
    a_re, a_im = mag * jnp.cos(ang), mag * jnp.sin(ang)
    den = lr * lr + li * li
    nr, ni = a_re - 1.0, a_im
    c_re = (nr * lr + ni * li) / den
    c_im = (ni * lr - nr * li) / den
    return a_re, a_im, c_re * btr - c_im * bti, c_re * bti + c_im * btr


def _s5_prep(lam_re, lam_im, logdt, btr, bti, name):
    ns = lam_re.shape[1]

    def body(lr_ref, li_ref, dt_ref, br_ref, bi_ref, ar_ref, ai_ref, bbr_ref, bbi_ref):
        ar, ai, bbr, bbi = _s5_disc(lr_ref[...], li_ref[...], dt_ref[...], br_ref[...], bi_ref[...])
        ar_ref[...] = ar
        ai_ref[...] = ai
        bbr_ref[...] = bbr
        bbi_ref[...] = bbi

    return _pallas(body, name=name,
                   out_shape=[jax.ShapeDtypeStruct((1, ns), F32)] * 2 + [jax.ShapeDtypeStruct((S5_GROUP, ns), F32)] * 2,
                   )(lam_re, lam_im, logdt, btr, bti)


def _s5_prep_bwd(lam_re, lam_im, logdt, btr, bti, dar, dai, dbbr, dbbi, name):
    ns = lam_re.shape[1]

    def body(lr_ref, li_ref, dt_ref, br_ref, bi_ref, dar_ref, dai_ref, dbr_ref, dbi_ref, o_lr, o_li, o_dt, o_br, o_bi):
        _, vjp = jax.vjp(_s5_disc, lr_ref[...], li_ref[...], dt_ref[...], br_ref[...], bi_ref[...])
        g = vjp((dar_ref[...], dai_ref[...], dbr_ref[...], dbi_ref[...]))
        o_lr[...] = g[0]
        o_li[...] = g[1]
        row = lax.broadcasted_iota(jnp.int32, (ns, ns), 0) // S5_STATE
        col = lax.broadcasted_iota(jnp.int32, (ns, ns), 1) // S5_STATE
        same = (row == col).astype(F32)
        o_dt[...] = _dot_hi(jnp.broadcast_to(g[2], (8, ns)), same)[0:1]
        o_br[...] = g[3]
        o_bi[...] = g[4]

    return _pallas(body, name=name,
                   out_shape=[jax.ShapeDtypeStruct((1, ns), F32)] * 3 + [jax.ShapeDtypeStruct((S5_GROUP, ns), F32)] * 2,
                   compiler_params=pltpu.CompilerParams(vmem_limit_bytes=VMEM_LIMIT),
                   )(lam_re, lam_im, logdt, btr, bti, dar, dai, dbbr, dbbi)


def _s5_scan(br, bi, ar, ai, t, reverse=False, carry=None):
    ng = t // 8
    ns = br.shape[1]
    br, bi = br.reshape(ng, 8, ns), bi.reshape(ng, 8, ns)
    row8 = lax.broadcasted_iota(jnp.int32, (1, 8, 1), 1)
    pr, pi_ = ar, ai
    for k in (1, 2, 4):
        if reverse:
            sr, si, ok = pltpu.roll(br, 8 - k, 1), pltpu.roll(bi, 8 - k, 1), row8 < 8 - k
        else:
            sr, si, ok = pltpu.roll(br, k, 1), pltpu.roll(bi, k, 1), row8 >= k
        sr = jnp.where(ok, sr, 0.0)
        si = jnp.where(ok, si, 0.0)
        br, bi = br + pr * sr - pi_ * si, bi + pr * si + pi_ * sr
        pr, pi_ = pr * pr - pi_ * pi_, 2.0 * pr * pi_
    pw_r, pw_i = [ar], [ai]
    for _ in range(7):
        pw_r.append(pw_r[-1] * ar - pw_i[-1] * ai)
        pw_i.append(pw_r[-2] * ai + pw_i[-1] * ar)
    if reverse:
        pw_r.reverse()
        pw_i.reverse()
    p8r, p8i = jnp.concatenate(pw_r, axis=0), jnp.concatenate(pw_i, axis=0)
    out_r, out_i = [None] * ng, [None] * ng
    order = range(ng - 1, -1, -1) if reverse else range(ng)
    edge = 0 if reverse else 7
    for g in order:
        gr, gi = br[g], bi[g]
        if carry is not None:
            cr, ci = carry
            gr, gi = gr + p8r * cr - p8i * ci, gi + p8r * ci + p8i * cr
        out_r[g], out_i[g] = gr, gi
        carry = (gr[edge:edge + 1], gi[edge:edge + 1])
    return jnp.concatenate(out_r, axis=0), jnp.concatenate(out_i, axis=0)


def _s5_prev_rows(x, first, t):
    ng = t // 8
    ns = x.shape[1]
    x3 = x.reshape(ng, 8, ns)
    last = x3[:, 7:8, :]
    before = jnp.concatenate([first.reshape(1, 1, ns), last[:ng - 1]], axis=0)
    row8 = lax.broadcasted_iota(jnp.int32, (1, 8, 1), 1)
    return jnp.where(row8 == 0, before, pltpu.roll(x3, 1, 1)).reshape(t, ns)


_GELU_C = math.sqrt(2.0 / math.pi)


def _gelu(y):
    th = jnp.tanh(_GELU_C * (y + 0.044715 * y * y * y))
    return 0.5 * y * (1.0 + th), th


def _sigmoid(x):
    return 1.0 / (1.0 + jnp.exp(-x))


def _s5_fwd(u, wb, a, wc, dskip, wglu, bglu, gnorm, name):
    r = u.shape[0]
    t = S5_TILE
    nt = r // t
    ns = wb.shape[2]
    w = S5_WIDTH

    def body(u_ref, wb_ref, a_ref, wc_ref, d_ref, wg_ref, bg_ref, gn_ref, y_ref, on_ref, xs_ref, carry_ref):
        i = pl.program_id(0)
        ar, ai = a_ref[0], a_ref[1]

        @pl.when(i == 0)
        def _():
            carry_ref[...] = jnp.zeros_like(carry_ref)

        u_ = u_ref[...]
        ub = u_.astype(BF16)
        xs_ref[0] = carry_ref[:, 0, :]
        chunks = list(range(S5_CHUNKS))
        sl_s = [slice(c * (ns // S5_CHUNKS), (c + 1) * (ns // S5_CHUNKS)) for c in chunks]
        sl_u = [slice(c * (w // S5_CHUNKS), (c + 1) * (w // S5_CHUNKS)) for c in chunks]
        bu, xs, ys = {}, {}, {}

        def st_inputs(c):
            bu[c] = (_dot(ub[:, sl_u[c]], wb_ref[0, sl_u[c], sl_s[c]]), _dot(ub[:, sl_u[c]], wb_ref[1, sl_u[c], sl_s[c]]))

        def st_scan(c):
            xr, xi = _s5_scan(*bu[c], ar[:, sl_s[c]], ai[:, sl_s[c]], t, carry=(carry_ref[0, :, sl_s[c]], carry_ref[1, :, sl_s[c]]))
            carry_ref[0, :, sl_s[c]] = xr[t - 1:t, :]
            carry_ref[1, :, sl_s[c]] = xi[t - 1:t, :]
            xs[c] = (xr.astype(BF16), xi.astype(BF16))

        def st_outputs(c):
            ys[c] = _dot(xs[c][0], wc_ref[0, sl_s[c], sl_u[c]]) - _dot(xs[c][1], wc_ref[1, sl_s[c], sl_u[c]])

        _emit_chains(chunks, [st_inputs, st_scan, st_outputs], False)
        y = jnp.concatenate([ys[c] for c in chunks], axis=1) + d_ref[...] * u_
        h, _ = _gelu(y)
        gate = _sigmoid(_dot(h.astype(BF16), wg_ref[...]) + bg_ref[...])
        y_ref[...] = y
        on_ref[...] = _rms(h * gate, gn_ref[...]).astype(BF16)

    full = lambda shape: pl.BlockSpec(shape, lambda i: (0,) * len(shape))
    return _pallas(
        body, name=name, grid=(nt,),
        in_specs=[_row_spec(w, t), full((2, w, ns)), full((2, 1, ns)), full((2, ns, w)), full((1, w)),
                  full((w, w)), full((1, w)), full((1, w))],
        out_specs=[_row_spec(w, t), _row_spec(w, t), pl.BlockSpec((1, 2, ns), lambda i: (i, 0, 0))],
        out_shape=[jax.ShapeDtypeStruct((r, w), F32), jax.ShapeDtypeStruct((r, w), BF16),
                   jax.ShapeDtypeStruct((nt, 2, ns), F32)],
        scratch_shapes=[pltpu.VMEM((2, 1, ns), F32)],
        compiler_params=_cparams(("arbitrary",)),
    )(u, wb, a, wc, dskip, wglu, bglu, gnorm)


def _s5_bwd(u, y, don, xstart, wb, a, wc, dskip, wglu, bglu, gnorm, name, ride=((), ()), don_block=0):
    r = u.shape[0]
    t = S5_TILE
    nt = r // t
    ns = wb.shape[2]
    w = S5_WIDTH
    nt_dims = ((1,), (1,))
    tn_dims = ((0,), (0,))

    def body(u_ref, y_ref, don_ref, xs_ref, wb_hbm, a_ref, wc_hbm, d_ref, wg_ref, bg_ref, gn_ref,
             du_ref, da_ref, dd_ref, dbg_ref, dgn_ref, dwb_hbm, dwc_hbm, dwg_hbm,
             wb_ref, wc_ref, lam_ref, acc_wb, acc_wc, acc_wg, sem):
        i = pl.program_id(0)
        ar, ai = a_ref[0], a_ref[1]

        @pl.when(i == 0)
        def _():
            c1 = pltpu.make_async_copy(wb_hbm, wb_ref, sem.at[0])
            c2 = pltpu.make_async_copy(wc_hbm, wc_ref, sem.at[1])
            c1.start()
            c2.start()
            lam_ref[...] = jnp.zeros_like(lam_ref)
            acc_wb[...] = jnp.zeros_like(acc_wb)
            acc_wc[...] = jnp.zeros_like(acc_wc)
            acc_wg[...] = jnp.zeros_like(acc_wg)
            da_ref[...] = jnp.zeros_like(da_ref)
            dd_ref[...] = jnp.zeros_like(dd_ref)
            dbg_ref[...] = jnp.zeros_like(dbg_ref)
            dgn_ref[...] = jnp.zeros_like(dgn_ref)
            c1.wait()
            c2.wait()

        u_ = u_ref[...]
        y_ = y_ref[...]
        ub = u_.astype(BF16)
        h, th = _gelu(y_)
        hb = h.astype(BF16)
        wg = wg_ref[...]
        gate = _sigmoid(_dot(hb, wg) + bg_ref[...])
        d_out, dgn = _rms_bwd(h * gate, gn_ref[...], don_ref[...])
        dgn_ref[...] += dgn
        dhw = d_out * h * gate * (1.0 - gate)
        dhwb = dhw.astype(BF16)
        dh = d_out * gate + _dot(dhwb, wg, nt_dims)
        acc_wg[...] += _dot(hb, dhwb, tn_dims)
        dbg_ref[...] += jnp.sum(dhw, axis=0, keepdims=True)
        dgelu = 0.5 * (1.0 + th) + 0.5 * y_ * (1.0 - th * th) * _GELU_C * (1.0 + 3.0 * 0.044715 * y_ * y_)
        dy = dh * dgelu
        dd_ref[...] += jnp.sum(dy * u_, axis=0, keepdims=True)
        dyb = dy.astype(BF16)
        chunks = list(range(S5_CHUNKS))
        sl_s = [slice(c * (ns // S5_CHUNKS), (c + 1) * (ns // S5_CHUNKS)) for c in chunks]
        sl_u = [slice(c * (w // S5_CHUNKS), (c + 1) * (w // S5_CHUNKS)) for c in chunks]
        bu, gx, x_, lam, dus = {}, {}, {}, {}, {}

        def st_inputs(c):
            su, ss = sl_u[c], sl_s[c]
            bu[c] = (_dot(ub[:, su], wb_ref[0, su, ss]), _dot(ub[:, su], wb_ref[1, su, ss]))
            gx[c] = (_dot(dyb[:, su], wc_ref[0, ss, su], nt_dims), -_dot(dyb[:, su], wc_ref[1, ss, su], nt_dims))

        def st_states(c):
            su, ss = sl_u[c], sl_s[c]
            first = (xs_ref[0, 0:1, ss], xs_ref[0, 1:2, ss])
            xr, xi = _s5_scan(*bu[c], ar[:, ss], ai[:, ss], t, carry=first)
            acc_wc[0, ss, su] += _dot(xr.astype(BF16), dyb[:, su], tn_dims)
            acc_wc[1, ss, su] -= _dot(xi.astype(BF16), dyb[:, su], tn_dims)
            x_[c] = (_s5_prev_rows(xr, first[0], t), _s5_prev_rows(xi, first[1], t))

        def st_adjoint(c):
            su, ss = sl_u[c], sl_s[c]
            lr, li = _s5_scan(*gx[c], ar[:, ss], -ai[:, ss], t, reverse=True, carry=(lam_ref[0, :, ss], lam_ref[1, :, ss]))
            lam_ref[0, :, ss] = lr[0:1, :]
            lam_ref[1, :, ss] = li[0:1, :]
            lrb, lib = lr.astype(BF16), li.astype(BF16)
            acc_wb[0, su, ss] += _dot(ub[:, su], lrb, tn_dims)
            acc_wb[1, su, ss] += _dot(ub[:, su], lib, tn_dims)
            dus[c] = _dot(lrb, wb_ref[0, su, ss], nt_dims) + _dot(lib, wb_ref[1, su, ss], nt_dims)
            lam[c] = (lr, li)

        def st_decay(c):
            ss = sl_s[c]
            (lr, li), (xpr, xpi) = lam[c], x_[c]
            da_ref[0, :, ss] += jnp.sum(lr * xpr + li * xpi, axis=0, keepdims=True)
            da_ref[1, :, ss] += jnp.sum(li * xpr - lr * xpi, axis=0, keepdims=True)

        _emit_chains(chunks, [st_inputs, st_states, st_adjoint, st_decay], False)
        du_ref[...] = d_ref[...] * dy + jnp.concatenate([dus[c] for c in chunks], axis=1)

        @pl.when(i == nt - 1)
        def _():
            cps = [pltpu.make_async_copy(acc_wb, dwb_hbm, sem.at[0]), pltpu.make_async_copy(acc_wc, dwc_hbm, sem.at[1]),
                   pltpu.make_async_copy(acc_wg, dwg_hbm, sem.at[2])]
            for c in cps:
                c.start()
            for c in cps:
                c.wait()

    n_ride = len(ride[0])
    n_in, n_out, n_scratch = 11, 8, 7

    def body_with_ride(*refs):
        ins, rest = refs[:n_in], refs[n_in:]
        ride_in, rest = rest[:n_ride], rest[n_ride:]
        outs, rest = rest[:n_out], rest[n_out:]
        ride_out, rest = rest[:n_ride], rest[n_ride:]
        scratch, ride_sems = rest[:n_scratch], rest[n_scratch:]
        if n_ride:
            @pl.when(pl.program_id(0) == 0)
            def _():
                for cp in _exchange_copies(ride_in, ride_out, ride[1], *ride_sems):
                    cp.start()
        body(*ins, *outs, *scratch)
        if n_ride:
            @pl.when(pl.program_id(0) == nt - 1)
            def _():
                for cp in _exchange_copies(ride_in, ride_out, ride[1], *ride_sems):
                    cp.wait()

    rev = lambda i: (nt - 1 - i, 0)
    full = lambda shape: pl.BlockSpec(shape, lambda i: (0,) * len(shape))
    hbm = pl.BlockSpec(memory_space=pl.ANY)
    outs = _pallas(
        body_with_ride, name=name, grid=(nt,),
        in_specs=[pl.BlockSpec((t, w), rev), pl.BlockSpec((t, w), rev), pl.BlockSpec((t, w), lambda i: (nt - 1 - i, don_block)),
                  pl.BlockSpec((1, 2, ns), lambda i: (nt - 1 - i, 0, 0)), hbm, full((2, 1, ns)), hbm, full((1, w)),
                  full((w, w)), full((1, w)), full((1, w))] + [hbm] * n_ride,
        out_specs=[pl.BlockSpec((t, w), rev), full((2, 1, ns)), full((1, w)), full((1, w)), full((1, w)), hbm, hbm, hbm]
        + [hbm] * n_ride,
        out_shape=[jax.ShapeDtypeStruct((r, w), F32), jax.ShapeDtypeStruct((2, 1, ns), F32)]
        + [jax.ShapeDtypeStruct((1, w), F32)] * 3
        + [jax.ShapeDtypeStruct((2, w, ns), F32), jax.ShapeDtypeStruct((2, ns, w), F32), jax.ShapeDtypeStruct((w, w), F32)]
        + _exchange_shapes(*ride),
        scratch_shapes=[pltpu.VMEM((2, w, ns), BF16), pltpu.VMEM((2, ns, w), BF16), pltpu.VMEM((2, 1, ns), F32),
                        pltpu.VMEM((2, w, ns), F32), pltpu.VMEM((2, ns, w), F32), pltpu.VMEM((w, w), F32),
                        pltpu.SemaphoreType.DMA((3,))] + (_exchange_sems(n_ride) if n_ride else []),
        compiler_params=_cparams(("arbitrary",)),
    )(u, y, don, xstart, wb, a, wc, dskip, wglu, bglu, gnorm, *ride[0])
    return tuple(outs[:n_out]) + (list(outs[n_out:]),)


def _s5_expand(lam_re, lam_im, log_dt, b_re, b_im, c_re, c_im):
    g, n, p = S5_GROUPS, S5_STATE, S5_GROUP
    ns = g * n
    rows = lambda x: x.reshape(1, ns)
    logdt = jnp.repeat(log_dt.reshape(g), n).reshape(1, ns)
    btr = b_re.reshape(ns, p).T
    bti = b_im.reshape(ns, p).T
    ctr = c_re.transpose(0, 2, 1).reshape(ns, p)
    cti = c_im.transpose(0, 2, 1).reshape(ns, p)
    mask = (jnp.arange(g * p)[:, None] // p) == (jnp.arange(ns)[None, :] // n)
    return rows(lam_re), rows(lam_im), logdt, btr, bti, ctr, cti, mask


def _s5_block_diag_b(bb, mask):
    return jnp.where(mask, jnp.tile(bb, (S5_GROUPS, 1)), 0.0)


def _s5_block_diag_c(ct, mask):
    return jnp.where(mask.T, jnp.tile(ct, (1, S5_GROUPS)), 0.0)


def _s5_diag_of_b(dwb, mask):
    return jnp.where(mask, dwb, 0.0).reshape(S5_GROUPS, S5_GROUP, -1).sum(0)


def _s5_diag_of_c(dwc, mask):
    ns = dwc.shape[0]
    return jnp.where(mask.T, dwc, 0.0).reshape(ns, S5_GROUPS, S5_GROUP).sum(1)


DN_PRE_TILE = 256
_DN_QKV = 3 * DN_WIDTH


def _halo_specs(width, tile, nt, prev):
    per = tile // 8
    if prev:
        return pl.BlockSpec((8, width), lambda i: (jnp.maximum(i * per - 1, 0), 0))
    return pl.BlockSpec((8, width), lambda i: (jnp.minimum((i + 1) * per, nt * per - 1), 0))


def _shift_down(x, halo, s, t):
    xx = jnp.concatenate([halo, x], axis=0)
    return pltpu.roll(xx, s, 0)[8:]


def _shift_up(x, halo, s, t):
    xx = jnp.concatenate([x, halo], axis=0)
    return pltpu.roll(xx, t + 8 - s, 0)[:t]


def _silu(x):
    s = _sigmoid(x)
    return x * s, s


def _dn_gates(ab, alog, dtb, live):
    lane = lax.broadcasted_iota(jnp.int32, (1, 128), 1)
    g = -jnp.exp(alog) * _softplus(ab + dtb)
    beta = _sigmoid(ab)
    return jnp.where(live & (lane < DN_HEADS), g, jnp.where(live & (lane < 2 * DN_HEADS), beta, 0.0))


def _dn_pre_fwd(proj, ab, conv_w, alog, dtb, pad, name):
    r = proj.shape[0]
    t = DN_PRE_TILE
    nt = r // t
    scale = DN_HEAD_DIM ** -0.5

    def body(x_ref, halo_ref, ab_ref, w_ref, al_ref, dt_ref, q_ref, k_ref, v_ref, gb_ref):
        i = pl.program_id(0)
        act, _ = _silu(_dn_conv(x_ref[...], jnp.where(i > 0, halo_ref[...], 0.0), w_ref[...], t))
        for hd in range(DN_HEADS):
            sl = slice(hd * 128, (hd + 1) * 128)
            for base, o_ref, sc in ((0, q_ref, scale), (DN_WIDTH, k_ref, 1.0)):
                xh = act[:, base + hd * 128: base + (hd + 1) * 128]
                o_ref[:, sl] = (xh * (lax.rsqrt(jnp.sum(xh * xh, axis=-1, keepdims=True) + EPS) * sc)).astype(BF16)
        v_ref[...] = act[:, 2 * DN_WIDTH:].astype(BF16)
        rows = i * t + lax.broadcasted_iota(jnp.int32, (t, 1), 0)
        gb_ref[...] = _dn_gates(ab_ref[...], al_ref[...], dt_ref[...], rows >= pad)

    return _pallas(
        body, name=name, grid=(nt,),
        in_specs=[pl.BlockSpec((t, _DN_QKV), lambda i: (i, 0)), _halo_specs(_DN_QKV, t, nt, True), _row_spec(128, t),
                  pl.BlockSpec((DN_CONV, _DN_QKV), lambda i: (0, 0)), _vec_spec(128), _vec_spec(128)],
        out_specs=[_row_spec(DN_WIDTH, t), _row_spec(DN_WIDTH, t), _row_spec(DN_WIDTH, t), _row_spec(128, t)],
        out_shape=[jax.ShapeDtypeStruct((r, DN_WIDTH), BF16)] * 3 + [jax.ShapeDtypeStruct((r, 128), F32)],
        compiler_params=_cparams(("parallel",)),
    )(proj, proj, ab, conv_w, alog, dtb)


def _dn_conv(x, halo, w, t):
    co = w[DN_CONV - 1:DN_CONV] * x
    for tap in range(DN_CONV - 1):
        co = co + w[tap:tap + 1] * _shift_down(x, halo, DN_CONV - 1 - tap, t)
    return co


def _dn_pre_bwd(proj, conv_w, dq, dk, dv, dgb, ab, alog, dtb, pad, name):
    r = proj.shape[0]
    t = DN_PRE_TILE
    nt = r // t
    scale = DN_HEAD_DIM ** -0.5

    def body(x_ref, halo_ref, w_ref, dq_ref, dk_ref, dv_ref, dgb_ref, ab_ref, al_ref, dt_ref, dco_ref, dab_ref, dal_ref,
             ddt_ref):
        i = pl.program_id(0)

        @pl.when(i == 0)
        def _():
            dal_ref[...] = jnp.zeros_like(dal_ref)
            ddt_ref[...] = jnp.zeros_like(ddt_ref)

        co_ = _dn_conv(x_ref[...], jnp.where(i > 0, halo_ref[...], 0.0), w_ref[...], t)
        act, sg = _silu(co_)
        dsilu = sg * (1.0 + co_ * (1.0 - sg))
        for hd in range(DN_HEADS):
            sl = slice(hd * 128, (hd + 1) * 128)
            for base, d_ref, sc in ((0, dq_ref, scale), (DN_WIDTH, dk_ref, 1.0)):
                cs = slice(base + hd * 128, base + (hd + 1) * 128)
                xh = act[:, cs]
                rn = lax.rsqrt(jnp.sum(xh * xh, axis=-1, keepdims=True) + EPS)
                xhat = xh * rn
                dy = d_ref[:, sl]
                dx = (sc * rn) * (dy - xhat * jnp.sum(dy * xhat, axis=-1, keepdims=True))
                dco_ref[:, cs] = dx * dsilu[:, cs]
        dco_ref[:, 2 * DN_WIDTH:] = dv_ref[...] * dsilu[:, 2 * DN_WIDTH:]
        rows = i * t + lax.broadcasted_iota(jnp.int32, (t, 1), 0)
        live = rows >= pad
        lane = lax.broadcasted_iota(jnp.int32, (1, 128), 1)
        ab_ = ab_ref[...]
        dgb_ = dgb_ref[...]
        is_g = live & (lane < DN_HEADS)
        is_b = live & (lane >= DN_HEADS) & (lane < 2 * DN_HEADS)
        arg = ab_ + dt_ref[...]
        ea = jnp.exp(al_ref[...])
        da = jnp.where(is_g, -dgb_ * ea * _sigmoid(arg), 0.0)
        beta = _sigmoid(ab_)
        dab_ref[...] = (da + jnp.where(is_b, dgb_ * beta * (1.0 - beta), 0.0)).astype(BF16)
        ddt_ref[...] += jnp.sum(da, axis=0, keepdims=True)
        dal_ref[...] += jnp.sum(jnp.where(is_g, -dgb_ * ea * _softplus(arg), 0.0), axis=0, keepdims=True)

    return _pallas(
        body, name=name, grid=(nt,),
        in_specs=[pl.BlockSpec((t, _DN_QKV), lambda i: (i, 0)), _halo_specs(_DN_QKV, t, nt, True),
                  pl.BlockSpec((DN_CONV, _DN_QKV), lambda i: (0, 0)),
                  _row_spec(DN_WIDTH, t), _row_spec(DN_WIDTH, t), _row_spec(DN_WIDTH, t),
                  _row_spec(128, t), _row_spec(128, t), _vec_spec(128), _vec_spec(128)],
        out_specs=[_row_spec(_DN_QKV, t), _row_spec(128, t), _vec_spec(128), _vec_spec(128)],
        out_shape=[jax.ShapeDtypeStruct((r, _DN_QKV), F32), jax.ShapeDtypeStruct((r, 128), BF16),
                   jax.ShapeDtypeStruct((1, 128), F32), jax.ShapeDtypeStruct((1, 128), F32)],
        compiler_params=_cparams(("arbitrary",)),
    )(proj, proj, conv_w, dq, dk, dv, dgb, ab, alog, dtb)


def _dn_conv_bwd(dco, proj, conv_w, dproj, name):
    r = dco.shape[0]
    t = DN_PRE_TILE
    nt = r // t

    def body(d_ref, dh_ref, x_ref, xh_ref, w_ref, dproj_in, dx_ref, dw_ref):
        i = pl.program_id(0)

        @pl.when(i == 0)
        def _():
            dw_ref[...] = jnp.zeros_like(dw_ref)

        d = d_ref[...]
        dhalo = jnp.where(i < nt - 1, dh_ref[...], 0.0)
        x = x_ref[...]
        xhalo = jnp.where(i > 0, xh_ref[...], 0.0)
        w = w_ref[...]
        dx = w[3:4] * d
        dws = [None] * DN_CONV
        dws[3] = jnp.sum(d * x, axis=0, keepdims=True)
        for tap in range(DN_CONV - 1):
            s = DN_CONV - 1 - tap
            dx = dx + w[tap:tap + 1] * _shift_up(d, dhalo, s, t)
            dws[tap] = jnp.sum(d * _shift_down(x, xhalo, s, t), axis=0, keepdims=True)
        dx_ref[...] = dx.astype(BF16)
        dw_ref[...] += jnp.concatenate(dws + [jnp.zeros((8 - DN_CONV, _DN_QKV), F32)], axis=0)

    return _pallas(
        body, name=name, grid=(nt,),
        in_specs=[_row_spec(_DN_QKV, t), _halo_specs(_DN_QKV, t, nt, False),
                  pl.BlockSpec((t, _DN_QKV), lambda i: (i, 0)), _halo_specs(_DN_QKV, t, nt, True),
                  pl.BlockSpec((DN_CONV, _DN_QKV), lambda i: (0, 0)), pl.BlockSpec(memory_space=pl.ANY)],
        out_specs=[_row_spec(_DN_QKV, t), pl.BlockSpec((8, _DN_QKV), lambda i: (0, 0))],
        out_shape=[jax.ShapeDtypeStruct(dproj.shape, BF16), jax.ShapeDtypeStruct((8, _DN_QKV), F32)],
        input_output_aliases={5: 0},
        compiler_params=_cparams(("arbitrary",)),
    )(dco, dco, proj, proj, conv_w, dproj)


def _split3(x):
    hi = x.astype(BF16)
    return hi, (x - hi.astype(F32)).astype(BF16)


def _dot3s(a, b, dims=((1,), (0,))):
    return _dot(a[0], b[0], dims) + (_dot(a[0], b[1], dims) + _dot(a[1], b[0], dims))


def _dot3(a, b, dims=((1,), (0,))):
    return _dot3s(_split3(a), _split3(b), dims)


def _dn_inverse_many(n_mats):
    c = n_mats[0].shape[0]
    row = lax.broadcasted_iota(jnp.int32, (c, c), 0)
    col = lax.broadcasted_iota(jnp.int32, (c, c), 1)
    eye = (row == col).astype(F32)
    same = row // DN_SUB == col // DN_SUB
    nds = [jnp.where(same, n, 0.0) for n in n_mats]
    nos = [n - nd for n, nd in zip(n_mats, nds)]

    def geometric(bs, order):
        xs = [eye + b for b in bs]
        sp = [_split3(b) for b in bs]
        k = 2
        while k < order:
            sp = [_split3(_dot3s(s_, s_)) for s_ in sp]
            xs = [x + _dot3s(_split3(x), s_) for x, s_ in zip(xs, sp)]
            k *= 2
        return xs

    tds = [_split3(td) for td in geometric([-nd for nd in nds], DN_SUB)]
    ms = [_dot3s(td, _split3(no)) for td, no in zip(tds, nos)]
    xs = geometric([-m for m in ms], c // DN_SUB)
    return [_dot3s(_split3(x), td) for x, td in zip(xs, tds)]


def _dn_chunk_shared(gb_ref, gbt_ref):
    c = DN_CHUNK
    row = lax.broadcasted_iota(jnp.int32, (c, c), 0)
    col = lax.broadcasted_iota(jnp.int32, (c, c), 1)
    gbv = gb_ref[...]
    gam_all = _split_dot((row >= col).astype(BF16), gbv)
    hi, lo = _split3(gbt_ref[...])
    tri_t = (row <= col).astype(BF16)
    return dict(row=row, col=col, gbv=gbv, gam_all=gam_all, gam_rows=_dot(hi, tri_t) + _dot(lo, tri_t),
                lane=lax.broadcasted_iota(jnp.int32, (1, 128), 1))


def _dn_chunk_common(q, k, v, sh, h):
    c = DN_CHUNK
    row, col, lane = sh["row"], sh["col"], sh["lane"]
    q, k, v = q.astype(F32), k.astype(F32), v.astype(F32)
    gam = jnp.sum(jnp.where(lane == h, sh["gam_all"], 0.0), axis=1, keepdims=True)
    beta = jnp.sum(jnp.where(lane == h + DN_HEADS, sh["gbv"], 0.0), axis=1, keepdims=True)
    gam_row = sh["gam_rows"][h:h + 1]
    dec = jnp.where(row >= col, jnp.exp(jnp.minimum(gam - gam_row, 0.0)), 0.0)
    kb, qb = k.astype(BF16), q.astype(BF16)
    nt_dims = ((1,), (1,))
    kk = _dot(kb, kb, nt_dims)
    qk = _dot(qb, kb, nt_dims)
    eg = jnp.exp(gam)
    gam_l = gam[c - 1:c, :]
    return dict(q=q, k=k, v=v, qb=qb, kb=kb, gam=gam, beta=beta, dec=dec, kk=kk, qk=qk, eg=eg, gam_l=gam_l,
                row=row, col=col, lane=lane, att=qk * dec, qg=q * eg, kt=k * jnp.exp(gam_l - gam),
                rhs=jnp.concatenate([v * beta, k * (beta * eg)], axis=1))


def _dn_fwd(q, k, v, gb, gbt, name):
    r = q.shape[0]
    c = DN_CHUNK
    nc = r // c
    dh = DN_HEAD_DIM
    tn_dims = ((0,), (0,))

    def body(q_ref, k_ref, v_ref, gb_ref, gbt_ref, o_ref, ss_ref, ts_ref, s_ref):
        @pl.when(pl.program_id(0) == 0)
        def _():
            s_ref[...] = jnp.zeros_like(s_ref)

        heads = list(range(DN_HEADS))
        sl = [slice(h * dh, (h + 1) * dh) for h in heads]
        sh = _dn_chunk_shared(gb_ref, gbt_ref)
        zs = [_dn_chunk_common(q_ref[:, sl[h]], k_ref[:, sl[h]], v_ref[:, sl[h]], sh, h) for h in heads]
        t_invs = _dn_inverse_many([jnp.where(sh["row"] > sh["col"], z["beta"] * z["kk"] * z["dec"], 0.0) for z in zs])
        sols = [_dot3(t_inv, z["rhs"]) for t_inv, z in zip(t_invs, zs)]
        ss = [s_ref[h] for h in heads]
        sbs = [s.astype(BF16) for s in ss]
        vnbs = [(sol[:, :dh] - _dot(sol[:, dh:].astype(BF16), sb)).astype(BF16) for sol, sb in zip(sols, sbs)]
        for h in heads:
            o_ref[:, sl[h]] = _dot(zs[h]["qg"].astype(BF16), sbs[h]) + _dot(zs[h]["att"].astype(BF16), vnbs[h])
        for h in heads:
            ss_ref[0, h] = ss[h]
            ts_ref[0, h] = t_invs[h]
            s_ref[h] = ss[h] * jnp.exp(zs[h]["gam_l"]) + _dot(zs[h]["kt"].astype(BF16), vnbs[h], tn_dims)

    blk = pl.BlockSpec((c, DN_WIDTH), lambda ci: (ci, 0))
    sav = pl.BlockSpec((1, DN_HEADS, dh, dh), lambda ci: (ci, 0, 0, 0))
    return _pallas(
        body, name=name, grid=(nc,),
        in_specs=[blk, blk, blk, pl.BlockSpec((c, 128), lambda ci: (ci, 0)), pl.BlockSpec((16, c), lambda ci: (0, ci))],
        out_specs=[blk, sav, sav],
        out_shape=[jax.ShapeDtypeStruct((r, DN_WIDTH), F32), jax.ShapeDtypeStruct((nc, DN_HEADS, dh, dh), F32),
                   jax.ShapeDtypeStruct((nc, DN_HEADS, dh, dh), F32)],
        scratch_shapes=[pltpu.VMEM((DN_HEADS, dh, dh), F32)],
        compiler_params=_cparams(("arbitrary",)),
    )(q, k, v, gb, gbt)


def _dn_bwd(q, k, v, gb, gbt, ssave, tsave, do, name):
    r = q.shape[0]
    c = DN_CHUNK
    nc = r // c
    dh = DN_HEAD_DIM
    nt_dims = ((1,), (1,))
    tn_dims = ((0,), (0,))

    def body(q_ref, k_ref, v_ref, gb_ref, gbt_ref, ss_ref, ts_ref, do_ref, dq_ref, dk_ref, dv_ref, dgb_ref, ds_ref):
        @pl.when(pl.program_id(0) == 0)
        def _():
            ds_ref[...] = jnp.zeros_like(ds_ref)

        heads = list(range(DN_HEADS))
        sl = [slice(h * dh, (h + 1) * dh) for h in heads]
        sh = _dn_chunk_shared(gb_ref, gbt_ref)
        row, col, lane = sh["row"], sh["col"], sh["lane"]
        rs = lambda x: jnp.sum(x, axis=1, keepdims=True)
        tot = lambda x: jnp.sum(rs(x), axis=0, keepdims=True)
        st = [dict() for _ in heads]
        dgb_parts = []

        def s_common(h):
            st[h].update(_dn_chunk_common(q_ref[:, sl[h]], k_ref[:, sl[h]], v_ref[:, sl[h]], sh, h))
            st[h]["t"] = _split3(ts_ref[0, h])

        def s_sol(h):
            st[h]["sol"] = _dot3s(st[h]["t"], _split3(st[h]["rhs"]))

        def s_state(h):
            z = st[h]
            sol = z["sol"]
            kcd = sol[:, dh:]
            s = ss_ref[0, h]
            sb = s.astype(BF16)
            vnb = (sol[:, :dh] - _dot(kcd.astype(BF16), sb)).astype(BF16)
            ds_next = ds_ref[h]
            dsb = ds_next.astype(BF16)
            dob = do_ref[:, sl[h]].astype(BF16)
            z["dqg"] = _dot(dob, sb, nt_dims)
            ds = _dot(z["qg"].astype(BF16), dob, tn_dims)
            z["d_att"] = jnp.where(row >= col, _dot(dob, vnb, nt_dims), 0.0)
            dvn = _dot(z["att"].astype(BF16), dob, tn_dims) + _dot(z["kt"].astype(BF16), dsb)
            z["dkt"] = _dot(vnb, dsb, nt_dims)
            eg_l = jnp.exp(z["gam_l"])
            ds = ds + ds_next * eg_l
            z["dgam_l"] = tot(ds_next * s) * eg_l
            dvnb = dvn.astype(BF16)
            dkcd = -_dot(dvnb, sb, nt_dims)
            ds_ref[h] = ds - _dot(kcd.astype(BF16), dvnb, tn_dims)
            z["dsol"] = jnp.concatenate([dvn, dkcd], axis=1)

        def s_drhs(h):
            st[h]["drhs"] = _dot3s(st[h]["t"], _split3(st[h]["dsol"]), tn_dims)

        def s_dn(h):
            z = st[h]
            z["dn"] = jnp.where(row > col, -_dot3(z["drhs"], z["sol"], nt_dims), 0.0)

        def s_rest(h):
            z = st[h]
            k_, v_, kb, qb = z["k"], z["v"], z["kb"], z["qb"]
            beta, eg, dec, kk, qk, gam, gam_l = z["beta"], z["eg"], z["dec"], z["kk"], z["qk"], z["gam"], z["gam_l"]
            dn, d_att, dqg, dkt = z["dn"], z["d_att"], z["dqg"], z["dkt"]
            drv, drk = z["drhs"][:, :dh], z["drhs"][:, dh:]
            s_rkk = rs(drk * k_)
            dv_ref[:, sl[h]] = drv * beta
            dbeta = rs(drv * v_) + s_rkk * eg + rs(dn * kk * dec)
            dk = drk * (beta * eg)
            dgam = s_rkk * beta * eg
            dkk = (dn * beta * dec).astype(BF16)
            dd = dn * beta * kk + d_att * qk
            dqk = (d_att * dec).astype(BF16)
            dq_ref[:, sl[h]] = _dot(dqk, kb) + dqg * eg
            dk = dk + _dot(dqk, qb, tn_dims) + _dot(dkk, kb) + _dot(dkk, kb, tn_dims)
            w = dd * dec
            wh, wl = _split3(w)
            ones = jnp.ones((c, 128), BF16)
            col_sum = (_dot(wh, ones, tn_dims) + _dot(wl, ones, tn_dims))[:, 0:1]
            dgam = dgam + rs(w) - col_sum + rs(dqg * z["qg"]) - rs(dkt * z["kt"])
            dk_ref[:, sl[h]] = dk + dkt * jnp.exp(gam_l - gam)
            dgam_l = z["dgam_l"] + tot(dkt * z["kt"])
            rowc = lax.broadcasted_iota(jnp.int32, (c, 1), 0)
            dgam = dgam + jnp.where(rowc == c - 1, dgam_l, 0.0)
            dg = _split_dot((row <= col).astype(BF16), jnp.broadcast_to(dgam, (c, 128)))[:, 0:1]
            dgb_parts.append(jnp.where(lane == h, dg, 0.0) + jnp.where(lane == h + DN_HEADS, dbeta, 0.0))

        _emit_chains(heads, [s_common, s_sol, s_state, s_drhs, s_dn, s_rest], False)
        dgb = dgb_parts[0]
        for part in dgb_parts[1:]:
            dgb = dgb + part
        dgb_ref[...] = dgb

    blk = pl.BlockSpec((c, DN_WIDTH), lambda ci: (nc - 1 - ci, 0))
    sav = pl.BlockSpec((1, DN_HEADS, dh, dh), lambda ci: (nc - 1 - ci, 0, 0, 0))
    gspec = pl.BlockSpec((c, 128), lambda ci: (nc - 1 - ci, 0))
    return _pallas(
        body, name=name, grid=(nc,),
        in_specs=[blk, blk, blk, gspec, pl.BlockSpec((16, c), lambda ci: (0, nc - 1 - ci)), sav, sav, blk],
        out_specs=[blk, blk, blk, gspec],
        out_shape=[jax.ShapeDtypeStruct((r, DN_WIDTH), F32)] * 3 + [jax.ShapeDtypeStruct((r, 128), F32)],
        scratch_shapes=[pltpu.VMEM((DN_HEADS, dh, dh), F32)],
        compiler_params=_cparams(("arbitrary",)),
    )(q, k, v, gb, gbt, ssave, tsave, do)


def _dn_post_fwd(o, proj, g, name):
    r = o.shape[0]

    def body(o_ref, z_ref, g_ref, y_ref):
        g_ = g_ref[...]
        for hd in range(DN_HEADS):
            sl = slice(hd * 128, (hd + 1) * 128)
            sz, _ = _silu(z_ref[:, sl])
            y_ref[:, sl] = (_rms(o_ref[:, sl], g_) * sz).astype(BF16)

    return _pallas(body, name=name, grid=(r // ROW_TILE,),
                   in_specs=[_row_spec(DN_WIDTH), pl.BlockSpec((ROW_TILE, DN_WIDTH), lambda i: (i, 3)), _vec_spec(128)],
                   out_specs=_row_spec(DN_WIDTH), out_shape=jax.ShapeDtypeStruct((r, DN_WIDTH), BF16),
                   compiler_params=_cparams(("parallel",)))(o, proj, g)


def _dn_post_bwd(o, proj, g, dy, name):
    r = o.shape[0]

    def body(o_ref, z_ref, g_ref, dy_ref, do_ref, dz_ref, dg_ref):
        @pl.when(pl.program_id(0) == 0)
        def _():
            dg_ref[...] = jnp.zeros_like(dg_ref)

        g_ = g_ref[...]
        for hd in range(DN_HEADS):
            sl = slice(hd * 128, (hd + 1) * 128)
            z_ = z_ref[:, sl]
            sz, sg = _silu(z_)
            dy_ = dy_ref[:, sl]
            o_ = o_ref[:, sl]
            dz_ref[:, sl] = (dy_ * _rms(o_, g_) * (sg * (1.0 + z_ * (1.0 - sg)))).astype(BF16)
            dx, dg = _rms_bwd(o_, g_, dy_ * sz)
            do_ref[:, sl] = dx
            dg_ref[...] += dg

    return _pallas(body, name=name, grid=(r // ROW_TILE,),
                   in_specs=[_row_spec(DN_WIDTH), pl.BlockSpec((ROW_TILE, DN_WIDTH), lambda i: (i, 3)), _vec_spec(128),
                             _row_spec(DN_WIDTH)],
                   out_specs=[_row_spec(DN_WIDTH), pl.BlockSpec((ROW_TILE, DN_WIDTH), lambda i: (i, 3)), _vec_spec(128)],
                   out_shape=[jax.ShapeDtypeStruct((r, DN_WIDTH), F32), jax.ShapeDtypeStruct((r, 4 * DN_WIDTH), BF16),
                              jax.ShapeDtypeStruct((1, 128), F32)],
                   compiler_params=_cparams(("arbitrary",)))(o, proj, g, dy)


def _exchange(arrays, scatter, name):
    n = len(arrays)

    def body(*refs):
        copies = _exchange_copies(refs[:n], refs[n:2 * n], scatter, *refs[2 * n:])
        for cp in copies:
            cp.start()
        for cp in copies:
            cp.wait()

    hbm = pl.BlockSpec(memory_space=pl.ANY)
    return _pallas(
        body, name=name, in_specs=[hbm] * n, out_specs=[hbm] * n, out_shape=_exchange_shapes(arrays, scatter),
        scratch_shapes=_exchange_sems(n),
    )(*arrays)


def _exchange_shapes(arrays, scatter):
    return [jax.ShapeDtypeStruct((N_DEV,) + (a.shape[1:] if sc else a.shape), a.dtype) for a, sc in zip(arrays, scatter)]


def _exchange_sems(n):
    return [pltpu.SemaphoreType.DMA((n * N_DEV,)), pltpu.SemaphoreType.DMA((n * N_DEV,)), pltpu.SemaphoreType.DMA((n,))]


def _exchange_copies(in_refs, out_refs, scatter, send_sems, recv_sems, local_sems):
    mx, my, mc = lax.axis_index("x"), lax.axis_index("y"), lax.axis_index("c")
    me = 4 * mx + 2 * my + mc
    copies = []
    for a in range(len(in_refs)):
        src_own = in_refs[a].at[me] if scatter[a] else in_refs[a]
        copies.append(pltpu.make_async_copy(src_own, out_refs[a].at[me], local_sems.at[a]))
        for kbits in range(1, N_DEV):
            px = lax.rem(mx + ((kbits >> 2) & 1), 2)
            py = lax.rem(my + ((kbits >> 1) & 1), 2)
            pc = lax.rem(mc + (kbits & 1), 2)
            src = in_refs[a].at[4 * px + 2 * py + pc] if scatter[a] else in_refs[a]
            copies.append(pltpu.make_async_remote_copy(
                src_ref=src, dst_ref=out_refs[a].at[me],
                send_sem=send_sems.at[a * N_DEV + kbits], recv_sem=recv_sems.at[a * N_DEV + kbits],
                device_id=(px, py, pc), device_id_type=pl.DeviceIdType.MESH))
    return copies


def _adamw(gstack, w, m, v, name):
    a, b = w.shape
    ta = a
    for t in (1024, 512, 256, 128, 64, 32, 16, 8):
        if a % t == 0 and N_DEV * t * b * 4 <= 4 * 1024 * 1024:
            ta = t
            break
    c1 = 1.0 / (1.0 - ADAM_B1 ** ADAM_STEP)
    c2 = 1.0 / (1.0 - ADAM_B2 ** ADAM_STEP)

    def body(g_ref, w_ref, m_ref, v_ref, og_ref, od_ref, om_ref, ov_ref):
        g = g_ref[0].astype(F32)
        for s in range(1, N_DEV):
            g = g + g_ref[s].astype(F32)
        m_new = ADAM_B1 * m_ref[...] + (1.0 - ADAM_B1) * g
        v_new = ADAM_B2 * v_ref[...] + (1.0 - ADAM_B2) * (g * g)
        og_ref[...] = g
        om_ref[...] = m_new
        ov_ref[...] = v_new
        od_ref[...] = -ADAM_LR * ((m_new * c1) / (jnp.sqrt(v_new * c2) + ADAM_EPS) + ADAM_WD * w_ref[...])

    spec = pl.BlockSpec((ta, b), lambda i: (i, 0))
    return _pallas(
        body, name=name, grid=(a // ta,),
        in_specs=[pl.BlockSpec((N_DEV, ta, b), lambda i: (0, i, 0)), spec, spec, spec],
        out_specs=[spec] * 4, out_shape=[jax.ShapeDtypeStruct((a, b), F32)] * 4,
        compiler_params=_cparams(("parallel",)),
    )(gstack, w, m, v)


_WEIGHTS = ['meta_tokens', 'pre_mix_norm', 'post_mix_norm', 'pre_mlp_norm', 'post_mlp_norm', 'mlp_w1', 'mlp_w2',
            'w_in_even', 'w_out_even', 'sb_out_norm', 's5_lambda_re', 's5_lambda_im', 's5_log_dt', 's5_b_re', 's5_b_im',
            's5_c_re', 's5_c_im', 's5_d', 's5_w_glu', 's5_b_glu', 's5_out_norm', 'w_in_odd', 'dn_conv_w', 'dn_a_log',
            'dn_dt_bias', 'dn_out_norm', 'w_out_odd']
_SHARDED = ['meta_tokens', 'mlp_w1', 'mlp_w2', 'w_in_even', 'w_out_even', 's5_w_glu', 'w_in_odd', 'dn_conv_w', 'w_out_odd']
_SMALL = [n for n in _WEIGHTS if n not in _SHARDED]
_GATHER_FIRST = ['meta_tokens', 'w_in_even', 's5_w_glu', 'w_out_even']
_GATHER_LATE = [n for n in _SHARDED if n not in _GATHER_FIRST]
_REDUCE_EARLY = ['mlp_w1', 'mlp_w2', 'w_in_odd', 'dn_conv_w', 'w_out_odd', 'w_out_even']


def _view2d(name, a):
    return a.reshape(-1, a.shape[-1])


def _unshard(name, g):
    if name == 'mlp_w1':
        return g.reshape(N_DEV, 2, D_MODEL, -1).transpose(1, 2, 0, 3).reshape(2, D_MODEL, D_FF)
    if name == 'mlp_w2':
        return g.reshape(N_DEV, 2, -1, D_MODEL).transpose(1, 0, 2, 3).reshape(2, D_FF, D_MODEL)
    if name in ('w_in_even', 'w_in_odd', 'dn_conv_w', 'meta_tokens'):
        return g.transpose(1, 0, 2).reshape(g.shape[1], -1)
    return g.reshape(-1, g.shape[-1])


def _to_blocks(name, full):
    if name == 'mlp_w1':
        return full.reshape(2, D_MODEL, N_DEV, -1).transpose(2, 0, 1, 3).reshape(N_DEV, 2 * D_MODEL, -1)
    if name == 'mlp_w2':
        return full.reshape(2, N_DEV, -1, D_MODEL).transpose(1, 0, 2, 3).reshape(N_DEV, -1, D_MODEL)
    if name in ('w_in_even', 'w_in_odd', 'dn_conv_w', 'meta_tokens'):
        return full.reshape(full.shape[0], N_DEV, -1).transpose(1, 0, 2)
    return full.reshape(N_DEV, -1, full.shape[-1])


def _pack(parts):
    rows = []
    for p in parts:
        flat = p.reshape(-1)
        rows.append(jnp.pad(flat, (0, (-flat.shape[0]) % 128)).reshape(-1, 128))
    return jnp.concatenate(rows, axis=0)


def _unpack(packed, like):
    out, at = [], 0
    for p in like:
        n = math.prod(p.shape)
        nrow = -(-n // 128)
        out.append(packed[at:at + nrow].reshape(-1)[:n].reshape(p.shape))
        at += nrow
    return out


def _lane_vec(x, width=128):
    flat = x.reshape(-1)
    return jnp.pad(flat, (0, width - flat.shape[0])).reshape(1, width)


def kernel(x, meta_tokens, pre_mix_norm, post_mix_norm, pre_mlp_norm, post_mlp_norm, mlp_w1, mlp_w2, w_in_even, w_out_even, sb_out_norm, s5_lambda_re, s5_lambda_im, s5_log_dt, s5_b_re, s5_b_im, s5_c_re, s5_c_im, s5_d, s5_w_glu, s5_b_glu, s5_out_norm, w_in_odd, dn_conv_w, dn_a_log, dn_dt_bias, dn_out_norm, w_out_odd, loss_target, m_meta_tokens, m_pre_mix_norm, m_post_mix_norm, m_pre_mlp_norm, m_post_mlp_norm, m_mlp_w1, m_mlp_w2, m_w_in_even, m_w_out_even, m_sb_out_norm, m_s5_lambda_re, m_s5_lambda_im, m_s5_log_dt, m_s5_b_re, m_s5_b_im, m_s5_c_re, m_s5_c_im, m_s5_d, m_s5_w_glu, m_s5_b_glu, m_s5_out_norm, m_w_in_odd, m_dn_conv_w, m_dn_a_log, m_dn_dt_bias, m_dn_out_norm, m_w_out_odd, v_meta_tokens, v_pre_mix_norm, v_post_mix_norm, v_pre_mlp_norm, v_post_mlp_norm, v_mlp_w1, v_mlp_w2, v_w_in_even, v_w_out_even, v_sb_out_norm, v_s5_lambda_re, v_s5_lambda_im, v_s5_log_dt, v_s5_b_re, v_s5_b_im, v_s5_c_re, v_s5_c_im, v_s5_d, v_s5_w_glu, v_s5_b_glu, v_s5_out_norm, v_w_in_odd, v_dn_conv_w, v_dn_a_log, v_dn_dt_bias, v_dn_out_norm, v_w_out_odd):
    given = dict(locals())
    w = {n: given[n] for n in _WEIGHTS}
    mom_m = {n: given["m_" + n] for n in _WEIGHTS}
    mom_v = {n: given["v_" + n] for n in _WEIGHTS}

    seq = x.shape[1]
    assert x.shape[0] == 1 and seq % ROW_TILE == 0
    r = seq + ROW_TILE
    pad = ROW_TILE - N_META
    pad_tiles = 1

    wire = {n: (F32 if n in ('dn_conv_w', 'meta_tokens') else BF16) for n in _SHARDED}
    shard_wire = lambda n: _view2d(n, w[n]).astype(wire[n])
    gathered = _exchange([shard_wire(n) for n in _GATHER_FIRST], [False] * len(_GATHER_FIRST), "gather_first")
    full = {n: _unshard(n, g_) for n, g_ in zip(_GATHER_FIRST, gathered)}
    w_ie, w_oe, w_glu = full['w_in_even'], full['w_out_even'], full['s5_w_glu']
    row = lambda v_: v_.reshape(1, -1)

    hs0 = jnp.concatenate([jnp.zeros((pad, D_MODEL), F32), full['meta_tokens'], x[0]], axis=0)
    hn0 = _norm_pre(hs0, row(pre_mix_norm[0]), "pre_mix_0")
    qkv = _mm_fwd(hn0, w_ie[:, :3 * SB_WIDTH], "in_even_qkv", out_dtypes=(BF16,))
    u = _mm_fwd(hn0, w_ie[:, 3 * SB_WIDTH:], "in_even_u")
    q, k, v = qkv[:, :SB_WIDTH], qkv[:, SB_WIDTH:2 * SB_WIDTH], qkv[:, 2 * SB_WIDTH:]
    nb = r // ATT_BLK
    blocks_t = lambda t_: t_.reshape(nb, ATT_BLK, 4, 128).transpose(2, 0, 3, 1)
    o_sb, ssave, gathered = _sb_fwd(q, k, blocks_t(v), pad, "sb_fwd",
                                    ride=([shard_wire(n) for n in _GATHER_LATE], [False] * len(_GATHER_LATE)))
    full.update({n: _unshard(n, g_) for n, g_ in zip(_GATHER_LATE, gathered)})
    w1, w2, w_oo, conv_w = full['mlp_w1'], full['mlp_w2'], full['w_out_odd'], full['dn_conv_w']
    w_io = full['w_in_odd'][:, :4 * DN_WIDTH]
    w_ab = jnp.pad(full['w_in_odd'][:, 4 * DN_WIDTH:], ((0, 0), (0, 128 - 2 * DN_HEADS)))
    on_sb = _norm_pre(o_sb, row(sb_out_norm[0]), "sb_out_norm")

    lam_re, lam_im, logdt, btr, bti, ctr, cti, s5_mask = _s5_expand(
        s5_lambda_re[0], s5_lambda_im[0], s5_log_dt[0], s5_b_re[0], s5_b_im[0], s5_c_re[0], s5_c_im[0])
    a_re, a_im, bbr, bbi = _s5_prep(lam_re, lam_im, logdt, btr, bti, "s5_prep")
    s5_wb = jnp.stack([_s5_block_diag_b(bbr, s5_mask), _s5_block_diag_b(bbi, s5_mask)]).astype(BF16)
    s5_wc = jnp.stack([_s5_block_diag_c(ctr, s5_mask), _s5_block_diag_c(cti, s5_mask)]).astype(BF16)
    s5_a = jnp.stack([a_re, a_im])
    s5_args = (s5_wb, s5_a, s5_wc, row(s5_d[0]), w_glu, row(s5_b_glu[0]), row(s5_out_norm[0]))
    y_s5, on_s5, xstart = _s5_fwd(u, *s5_args, "s5_fwd")

    merged = jnp.concatenate([on_sb, on_s5], axis=1)
    mix0 = _mm_fwd(merged, w_oe, "out_even")
    hs1, hn1 = _norm_post_pre(hs0, mix0, row(post_mix_norm[0]), row(pre_mlp_norm[0]), "post_mix_0")
    relu2 = lambda acc: (jnp.square(jnp.maximum(acc, 0.0)), jnp.maximum(acc, 0.0))
    r0, ra0 = _mm_fwd(hn1, w1[0], "mlp_up_0", out_dtypes=(BF16, BF16), epilogue=relu2)
    m0 = _mm_fwd(r0, w2[0], "mlp_down_0")
    hs2, hn2 = _norm_post_pre(hs1, m0, row(post_mlp_norm[0]), row(pre_mix_norm[1]), "post_mlp_0")

    proj = _mm_fwd(hn2, w_io, "in_odd")
    ab = _mm_fwd(hn2, w_ab, "in_odd_gates")
    alog, dtb = _lane_vec(dn_a_log[0]), _lane_vec(dn_dt_bias[0])
    qd, kd, vd, gb = _dn_pre_fwd(proj, ab, conv_w, alog, dtb, pad, "dn_pre")
    gbt = gb[:, :2 * DN_HEADS].T
    o_dn, s_dn, t_dn = _dn_fwd(qd, kd, vd, gb, gbt, "dn_fwd")
    on_dn = _dn_post_fwd(o_dn, proj, row(dn_out_norm[0]), "dn_post")
    mix1 = _mm_fwd(on_dn, w_oo, "out_odd")
    hs3, hn3 = _norm_post_pre(hs2, mix1, row(post_mix_norm[1]), row(pre_mlp_norm[1]), "post_mix_1")
    r1, ra1 = _mm_fwd(hn3, w1[1], "mlp_up_1", out_dtypes=(BF16, BF16), epilogue=relu2)
    m1 = _mm_fwd(r1, w2[1], "mlp_down_1")
    dhs, loss_part = _norm_post_loss(hs3, m1, row(post_mlp_norm[1]), loss_target[0], pad_tiles, "post_mlp_1_loss")
    loss = lax.psum(loss_part, ("x", "y", "c"))

    g = {}
    drelu2 = lambda acc, ra: (acc * (2.0 * ra.astype(F32)),)

    def mlp_bwd(layer, hn, rr, ra, dm):
        dw2 = _mm_wgrad(rr, dm, f"mlp_down_{layer}_wgrad")
        da = _mm_dgrad(dm, w2[layer], f"mlp_down_{layer}_dgrad", out_dtypes=(BF16,), extras=(ra,), epilogue=drelu2)
        dw1 = _mm_wgrad(hn, da, f"mlp_up_{layer}_wgrad")
        return dw1, dw2, da

    _, dm1, _, dg_post_mlp1 = _norm_bwd(dhs, post=(m1, row(post_mlp_norm[1])), pad=pad, name="post_mlp_1_bwd")
    dw1_1, dw2_1, da1 = mlp_bwd(1, hn3, r1, ra1, dm1)
    dhs, dmix1, dg_pre_mlp1, dg_post_mix1 = _dgrad_norm_bwd(
        da1, w1[1], dhs, hs3, row(pre_mlp_norm[1]), post=(mix1, row(post_mix_norm[1])), pad=pad, name="post_mix_1_bwd")

    g['w_out_odd'] = _mm_wgrad(on_dn, dmix1, "out_odd_wgrad")
    d_on_dn = _mm_dgrad(dmix1, w_oo, "out_odd_dgrad")
    do_dn, dproj, dg_dn = _dn_post_bwd(o_dn, proj, row(dn_out_norm[0]), d_on_dn, "dn_post_bwd")
    dqd, dkd, dvd, dgb = _dn_bwd(qd, kd, vd, gb, gbt, s_dn, t_dn, do_dn, "dn_bwd")
    dco, dab, d_alog, d_dtb = _dn_pre_bwd(proj, conv_w, dqd, dkd, dvd, dgb, ab, alog, dtb, pad, "dn_pre_bwd")
    dproj, d_conv = _dn_conv_bwd(dco, proj, conv_w, dproj, "dn_conv_bwd")
    g['w_in_odd'] = jnp.concatenate([_mm_wgrad(hn2, dproj, "in_odd_wgrad"),
                                     _mm_wgrad(hn2, dab, "in_odd_gates_wgrad")[:, :2 * DN_HEADS]], axis=1)
    dhn2_gates = _mm_dgrad(dab, w_ab, "in_odd_gates_dgrad")
    g['dn_conv_w'] = d_conv[:DN_CONV]
    g['dn_a_log'], g['dn_dt_bias'], g['dn_out_norm'] = d_alog[0, :DN_HEADS], d_dtb[0, :DN_HEADS], dg_dn[0]

    dhs, dm0, dg_pre_mix1, dg_post_mlp0 = _dgrad_norm_bwd(
        dproj, w_io, dhs, hs2, row(pre_mix_norm[1]), post=(m0, row(post_mlp_norm[0])), add=dhn2_gates, pad=pad,
        name="post_mlp_0_bwd")
    dw1_0, dw2_0, da0 = mlp_bwd(0, hn1, r0, ra0, dm0)
    dhs, dmix0, dg_pre_mlp0, dg_post_mix0 = _dgrad_norm_bwd(
        da0, w1[0], dhs, hs1, row(pre_mlp_norm[0]), post=(mix0, row(post_mix_norm[0])), pad=pad, name="post_mix_0_bwd")

    g['w_out_even'] = _mm_wgrad(merged, dmix0, "out_even_wgrad")
    dmerged = _mm_dgrad(dmix0, w_oe, "out_even_dgrad")
    _, do_sb, _, dg_sb = _norm_bwd(dmerged, post=(o_sb, row(sb_out_norm[0])), pad=pad, dm_dtype=F32,
                                   dhs_cols=(SB_WIDTH, 0), name="sb_out_norm_bwd")
    dq, dk4, dv4 = _sb_bwd(q, k, v, blocks_t(k), ssave, do_sb, pad, "sb_bwd")
    unheads = lambda t_: t_.transpose(1, 0, 2).reshape(r, SB_WIDTH)
    g['mlp_w1'] = jnp.stack([dw1_0, dw1_1])
    g['mlp_w2'] = jnp.stack([dw2_0, dw2_1])
    grad_wire = lambda n: _to_blocks(n, g[n].reshape(full[n].shape)).astype(wire[n])
    du, d_a, d_d, d_bglu, dg_s5, d_wb, d_wc, g['s5_w_glu'], reduced = _s5_bwd(
        u, y_s5, dmerged, xstart, *s5_args, "s5_bwd", don_block=1,
        ride=([grad_wire(n) for n in _REDUCE_EARLY], [True] * len(_REDUCE_EARLY)))
    stacks = dict(zip(_REDUCE_EARLY, reduced))
    g_lr, g_li, g_dt, g_btr, g_bti = _s5_prep_bwd(
        lam_re, lam_im, logdt, btr, bti, d_a[0], d_a[1],
        _s5_diag_of_b(d_wb[0], s5_mask), _s5_diag_of_b(d_wb[1], s5_mask), "s5_prep_bwd")
    gg, nn, pp = S5_GROUPS, S5_STATE, S5_GROUP
    g['s5_lambda_re'], g['s5_lambda_im'] = g_lr.reshape(gg, nn), g_li.reshape(gg, nn)
    g['s5_log_dt'] = g_dt.reshape(gg, nn)[:, 0]
    g['s5_b_re'], g['s5_b_im'] = g_btr.T.reshape(gg, nn, pp), g_bti.T.reshape(gg, nn, pp)
    g['s5_c_re'] = _s5_diag_of_c(d_wc[0], s5_mask).reshape(gg, nn, pp).transpose(0, 2, 1)
    g['s5_c_im'] = _s5_diag_of_c(d_wc[1], s5_mask).reshape(gg, nn, pp).transpose(0, 2, 1)
    g['s5_d'], g['s5_b_glu'], g['s5_out_norm'], g['sb_out_norm'] = d_d[0], d_bglu[0], dg_s5[0], dg_sb[0]
    dqkvu = jnp.concatenate([dq, unheads(dk4), unheads(dv4), du], axis=1).astype(BF16)
    g['w_in_even'] = _mm_wgrad(hn0, dqkvu, "in_even_wgrad")
    dhs, _, dg_pre_mix0, _ = _dgrad_norm_bwd(dqkvu, w_ie, dhs, hs0, row(pre_mix_norm[0]), pad=pad, name="pre_mix_0_bwd")

    g['meta_tokens'] = dhs[pad:pad + N_META]
    g['pre_mix_norm'] = jnp.concatenate([dg_pre_mix0, dg_pre_mix1], axis=0)
    g['post_mix_norm'] = jnp.concatenate([dg_post_mix0, dg_post_mix1], axis=0)
    g['pre_mlp_norm'] = jnp.concatenate([dg_pre_mlp0, dg_pre_mlp1], axis=0)
    g['post_mlp_norm'] = jnp.concatenate([dg_post_mlp0, dg_post_mlp1], axis=0)
    grad_x = dhs[pad + N_META:][None]

    small_like = [w[n] for n in _SMALL]
    last = [n for n in _SHARDED if n not in _REDUCE_EARLY]
    partial = [grad_wire(n) for n in last] + [_pack([g[n].reshape(w[n].shape) for n in _SMALL])]
    reduced = _exchange(partial, [True] * len(last) + [False], "reduce_last")
    stacks.update(zip(last, reduced[:-1]))
    grads, deltas, new_m, new_v = {}, {}, {}, {}
    for n in _SHARDED:
        outs = _adamw(stacks[n], _view2d(n, w[n]), _view2d(n, mom_m[n]), _view2d(n, mom_v[n]), f"adamw_{n}")
        grads[n], deltas[n], new_m[n], new_v[n] = (o.reshape(w[n].shape) for o in outs)
    outs = _adamw(reduced[-1], _pack(small_like), _pack([mom_m[n] for n in _SMALL]), _pack([mom_v[n] for n in _SMALL]),
                  "adamw_small")
    for dst, o in zip((grads, deltas, new_m, new_v), outs):
        for n, part in zip(_SMALL, _unpack(o, small_like)):
            dst[n] = part
    return (loss, grad_x, *[grads[n] for n in _WEIGHTS], *[deltas[n] for n in _WEIGHTS],
            *[new_m[n] for n in _WEIGHTS], *[new_v[n] for n in _WEIGHTS])
```

```python
import math

import jax
import jax.numpy as jnp
from jax import lax
from jax.experimental import pallas as pl
from jax.experimental.pallas import tpu as pltpu

F32 = jnp.float32
BF16 = jnp.bfloat16

D_MODEL = 1024
N_META = 16
SB_HEAD_DIM = 64
SB_WIDTH = 512
S5_WIDTH = 512
S5_GROUP = 16
S5_GROUPS = 32
S5_STATE = 64
S5_NS = S5_GROUPS * S5_STATE
DN_HEAD_DIM = 128
DN_HEADS = 8
DN_WIDTH = 1024
DN_CONV = 4
D_FF = 4096
EPS = 1e-6
N_DEV = 8

ADAM_LR = 0.001
ADAM_B1 = 0.9
ADAM_B2 = 0.999
ADAM_EPS = 1e-08
ADAM_WD = 0.01
ADAM_STEP = 10

ROW_TILE = 512
ATT_BLK = 256
SB_BLOCKS_PER_TRIP = 3
SB_LOG_ZERO = -106.0
SB_FWD_SKEW = False
SB_BWD_SKEW = True
DN_CHUNK = 128
DN_SUB = 16
S5_TILE = 128
S5_CHUNKS = 4
VMEM_LIMIT = 56 * 1024 * 1024

_HIGH = lax.Precision.HIGHEST


def _pallas(body, **kw):
    return pl.pallas_call(body, **kw)


def _cparams(sem):
    return pltpu.CompilerParams(dimension_semantics=sem, vmem_limit_bytes=VMEM_LIMIT)


def _dot(a, b, dims=((1,), (0,))):
    return lax.dot_general(a, b, (dims, ((), ())), preferred_element_type=F32)


def _dot_hi(a, b):
    return lax.dot_general(a, b, (((1,), (0,)), ((), ())), preferred_element_type=F32, precision=_HIGH)


def _split_dot(m_bf16, x):
    hi = x.astype(BF16)
    lo = (x - hi.astype(F32)).astype(BF16)
    return _dot(m_bf16, hi) + _dot(m_bf16, lo)


def _matmul(a, b, *, ta=False, tb=False, tm, tn, tk, name, out_dtypes=(F32,), extras=(), epilogue=None):
    m, k = (a.shape[1], a.shape[0]) if ta else a.shape
    n = b.shape[0] if tb else b.shape[1]
    assert (b.shape[1] if tb else b.shape[0]) == k
    assert m % tm == 0 and n % tn == 0 and k % tk == 0, (name, m, n, k, tm, tn, tk)
    nk = k // tk
    n_ex = len(extras)
    n_out = len(out_dtypes)
    dims = ((0 if ta else 1,), (1 if tb else 0,))

    def finish(acc, ex_refs, o_refs):
        outs = (acc,) if epilogue is None else epilogue(acc, *[r[...] for r in ex_refs])
        for o_ref, o in zip(o_refs, outs):
            o_ref[...] = o.astype(o_ref.dtype)

    def body(*refs):
        a_ref, b_ref = refs[0], refs[1]
        ex_refs = refs[2:2 + n_ex]
        o_refs = refs[2 + n_ex:2 + n_ex + n_out]
        prod = _dot(a_ref[...].astype(BF16), b_ref[...].astype(BF16), dims)
        if nk == 1:
            finish(prod, ex_refs, o_refs)
            return
        acc_ref = refs[-1]
        kk = pl.program_id(2)

        @pl.when(kk == 0)
        def _():
            acc_ref[...] = prod

        @pl.when(kk > 0)
        def _():
            acc_ref[...] += prod

        @pl.when(kk == nk - 1)
        def _():
            finish(acc_ref[...], ex_refs, o_refs)

    a_spec = pl.BlockSpec((tk, tm), lambda j, i, kk: (kk, i)) if ta else pl.BlockSpec((tm, tk), lambda j, i, kk: (i, kk))
    b_spec = pl.BlockSpec((tn, tk), lambda j, i, kk: (j, kk)) if tb else pl.BlockSpec((tk, tn), lambda j, i, kk: (kk, j))
    o_spec = pl.BlockSpec((tm, tn), lambda j, i, kk: (i, j))
    outs = _pallas(
        body, name=name,
        grid=(n // tn, m // tm, nk),
        in_specs=[a_spec, b_spec] + [o_spec] * n_ex,
        out_specs=[o_spec] * n_out,
        out_shape=[jax.ShapeDtypeStruct((m, n), dt) for dt in out_dtypes],
        scratch_shapes=[] if nk == 1 else [pltpu.VMEM((tm, tn), F32)],
        compiler_params=_cparams(("parallel", "parallel", "arbitrary")),
    )(a, b, *extras)
    return outs[0] if n_out == 1 else outs


def _tile(n, cap):
    best = 128
    for t in range(128, min(n, cap) + 1, 128):
        if n % t == 0:
            best = t
    assert n % best == 0, n
    return best


MM_K_CAP = 4096
WGRAD_ROWS = 1536


MM_LHS_TILE_BYTES = 6 * 1024 * 1024


def _row_tile(x, depth):
    tall = 3 * ROW_TILE
    fits = tall * depth * x.dtype.itemsize <= MM_LHS_TILE_BYTES
    return tall if (x.shape[0] % tall == 0 and fits) else ROW_TILE


def _mm_fwd(x, w, name, **kw):
    k, n = w.shape
    tk = _tile(k, MM_K_CAP)
    return _matmul(x, w, tm=_row_tile(x, tk), tn=_tile(n, 1024), tk=tk, name=name, **kw)


def _mm_dgrad(dy, w, name, **kw):
    k, n = w.shape
    tk = _tile(n, MM_K_CAP)
    return _matmul(dy, w, tb=True, tm=_row_tile(dy, tk), tn=_tile(k, 1024), tk=tk, name=name, **kw)


def _mm_wgrad(x, dy, name):
    k, n = x.shape[1], dy.shape[1]
    rows = x.shape[0]
    return _matmul(x, dy, ta=True, tm=_tile(k, 512), tn=_tile(n, 1024),
                   tk=WGRAD_ROWS if rows % WGRAD_ROWS == 0 else ROW_TILE, name=name)


def _rms(x, g):
    r = lax.rsqrt(jnp.mean(x * x, axis=-1, keepdims=True) + EPS)
    return x * r * g


def _rms_bwd(x, g, dy):
    r = lax.rsqrt(jnp.mean(x * x, axis=-1, keepdims=True) + EPS)
    xh = x * r
    dxh = dy * g
    dx = r * (dxh - xh * jnp.mean(dxh * xh, axis=-1, keepdims=True))
    dg = jnp.sum(dy * xh, axis=0, keepdims=True)
    return dx, dg


def _row_spec(width, tile=ROW_TILE):
    return pl.BlockSpec((tile, width), lambda i: (i, 0))


def _vec_spec(width):
    return pl.BlockSpec((1, width), lambda i: (0, 0))


def _norm_pre(hs, g, name):
    r, d = hs.shape

    def body(x_ref, g_ref, o_ref):
        o_ref[...] = _rms(x_ref[...], g_ref[...]).astype(BF16)

    return _pallas(body, name=name, grid=(r // ROW_TILE,), in_specs=[_row_spec(d), _vec_spec(d)],
                   out_specs=_row_spec(d), out_shape=jax.ShapeDtypeStruct((r, d), BF16),
                   compiler_params=_cparams(("parallel",)))(hs, g)


def _mm_norm_fwd(a, w, hs, g_post, *, g_pre=None, loss=None, name):
    k, d = w.shape
    r = a.shape[0]
    assert k <= MM_K_CAP and d == hs.shape[1]
    t = ROW_TILE // 2
    nt = r // t

    def body(*refs):
        a_ref, w_ref, hs_ref, gp_ref = refs[:4]
        i = pl.program_id(0)
        m = _dot(a_ref[...].astype(BF16), w_ref[...].astype(BF16))
        gp = gp_ref[...]
        new = hs_ref[...] + _rms(m, gp)
        if loss is None:
            gn_ref, m_ref, o_ref, hn_ref = refs[4:]
            m_ref[...] = m
            o_ref[...] = new
            hn_ref[...] = _rms(new, gn_ref[...]).astype(BF16)
        else:
            t_ref, dhs_ref, dm_ref, dgp_ref, loss_ref = refs[4:]
            live = (i * t + lax.broadcasted_iota(jnp.int32, (t, 1), 0)) >= loss[1]
            diff = jnp.where(live, new - t_ref[...], 0.0)
            dhs = diff * (1.0 / d)
            dhs_ref[...] = dhs
            loss_ref[...] = jnp.full((8, 128), 0.5 / d * jnp.sum(diff * diff), F32)
            dm, dg = _rms_bwd(m, gp, dhs)
            dm_ref[...] = dm.astype(BF16)

            @pl.when(i == 0)
            def _():
                dgp_ref[...] = jnp.zeros_like(dgp_ref)
            dgp_ref[...] += dg

    common_in = [_row_spec(k, t), pl.BlockSpec((k, d), lambda i: (0, 0)), _row_spec(d, t), _vec_spec(d)]
    if loss is None:
        return _pallas(
            body, name=name, grid=(nt,), in_specs=common_in + [_vec_spec(d)],
            out_specs=[_row_spec(d, t)] * 3,
            out_shape=[jax.ShapeDtypeStruct((r, d), F32), jax.ShapeDtypeStruct((r, d), F32), jax.ShapeDtypeStruct((r, d), BF16)],
            compiler_params=_cparams(("parallel",)))(a, w, hs, g_post, g_pre)
    target, first_row = loss
    assert first_row % t == 0
    dhs, dm, dgp, parts = _pallas(
        body, name=name, grid=(nt,),
        in_specs=common_in + [pl.BlockSpec((t, d), lambda i: (jnp.maximum(i - first_row // t, 0), 0))],
        out_specs=[_row_spec(d, t), _row_spec(d, t), _vec_spec(d), pl.BlockSpec((8, 128), lambda i: (i, 0))],
        out_shape=[jax.ShapeDtypeStruct((r, d), F32), jax.ShapeDtypeStruct((r, d), BF16), jax.ShapeDtypeStruct((1, d), F32),
                   jax.ShapeDtypeStruct((nt * 8, 128), F32)],
        compiler_params=_cparams(("arbitrary",)))(a, w, hs, g_post, target)
    return dhs, dm, dgp, jnp.sum(parts[::8, 0])


def _norm_bwd(dhs, *, pre=None, post=None, pad=0, dm_dtype=BF16, dhs_cols=None, name):
    r = dhs.shape[0]
    d = dhs.shape[1] if dhs_cols is None else dhs_cols[0]
    has_pre, has_post = pre is not None, post is not None

    def body(*refs):
        it = iter(refs)
        dhs_ref = next(it)
        if has_pre:
            hs_ref, gn_ref, dhn_ref = next(it), next(it), next(it)
        if has_post:
            m_ref, gp_ref = next(it), next(it)
        if has_pre:
            o_dhs, o_dgn = next(it), next(it)
        if has_post:
            o_dm, o_dgp = next(it), next(it)
        i = pl.program_id(0)
        live = (i * ROW_TILE + lax.broadcasted_iota(jnp.int32, (ROW_TILE, 1), 0)) >= pad
        cur = jnp.where(live, dhs_ref[...], 0.0)
        if has_pre:
            dx, dg = _rms_bwd(hs_ref[...], gn_ref[...], jnp.where(live, dhn_ref[...].astype(F32), 0.0))
            cur = cur + dx
            o_dhs[...] = cur

            @pl.when(i == 0)
            def _():
                o_dgn[...] = jnp.zeros_like(o_dgn)
            o_dgn[...] += dg
        if has_post:
            dm, dg = _rms_bwd(m_ref[...], gp_ref[...], cur)
            o_dm[...] = dm.astype(o_dm.dtype)

            @pl.when(i == 0)
            def _():
                o_dgp[...] = jnp.zeros_like(o_dgp)
            o_dgp[...] += dg

    dhs_spec = _row_spec(d) if dhs_cols is None else pl.BlockSpec((ROW_TILE, d), lambda i: (i, dhs_cols[1]))
    ins, in_specs, out_specs, out_shape = [dhs], [dhs_spec], [], []
    if has_pre:
        ins += list(pre)
        in_specs += [_row_spec(d), _vec_spec(d), _row_spec(d)]
        out_specs += [_row_spec(d), _vec_spec(d)]
        out_shape += [jax.ShapeDtypeStruct((r, d), F32), jax.ShapeDtypeStruct((1, d), F32)]
    if has_post:
        ins += list(post)
        in_specs += [_row_spec(d), _vec_spec(d)]
        out_specs += [_row_spec(d), _vec_spec(d)]
        out_shape += [jax.ShapeDtypeStruct((r, d), dm_dtype), jax.ShapeDtypeStruct((1, d), F32)]
    outs = list(_pallas(body, name=name, grid=(r // ROW_TILE,), in_specs=in_specs, out_specs=out_specs,
                        out_shape=out_shape, compiler_params=_cparams(("arbitrary",)))(*ins))
    dhs_new, dgn = (outs.pop(0), outs.pop(0)) if has_pre else (dhs, None)
    dm, dgp = (outs.pop(0), outs.pop(0)) if has_post else (None, None)
    return dhs_new, dm, dgn, dgp


def _dgrad_norm_bwd(dy, w, dhs, hs, g_pre, *, post=None, add=None, pad=0, name):
    d, n = w.shape
    r = dy.shape[0]
    assert n <= MM_K_CAP and d == dhs.shape[1]
    t = ROW_TILE // 2
    has_post, has_add = post is not None, add is not None
    dims = ((1,), (1,))

    def body(*refs):
        it = iter(refs)
        dy_ref, w_ref = next(it), next(it)
        add_ref = next(it) if has_add else None
        dhs_ref, hs_ref, gn_ref = next(it), next(it), next(it)
        if has_post:
            m_ref, gp_ref = next(it), next(it)
        o_dhs, o_dgn = next(it), next(it)
        if has_post:
            o_dm, o_dgp = next(it), next(it)
        i = pl.program_id(0)
        dhn = _dot(dy_ref[...].astype(BF16), w_ref[...].astype(BF16), dims)
        if has_add:
            dhn = dhn + add_ref[...]
        live = (i * t + lax.broadcasted_iota(jnp.int32, (t, 1), 0)) >= pad
        dx, dg = _rms_bwd(hs_ref[...], gn_ref[...], jnp.where(live, dhn, 0.0))
        cur = jnp.where(live, dhs_ref[...], 0.0) + dx
        o_dhs[...] = cur

        @pl.when(i == 0)
        def _():
            o_dgn[...] = jnp.zeros_like(o_dgn)
        o_dgn[...] += dg
        if has_post:
            dm, dg = _rms_bwd(m_ref[...], gp_ref[...], cur)
            o_dm[...] = dm.astype(BF16)

            @pl.when(i == 0)
            def _():
                o_dgp[...] = jnp.zeros_like(o_dgp)
            o_dgp[...] += dg

    ins = [dy, w] + ([add] if has_add else []) + [dhs, hs, g_pre] + (list(post) if has_post else [])
    in_specs = ([_row_spec(n, t), pl.BlockSpec((d, n), lambda i: (0, 0))] + ([_row_spec(d, t)] if has_add else [])
                + [_row_spec(d, t), _row_spec(d, t), _vec_spec(d)] + ([_row_spec(d, t), _vec_spec(d)] if has_post else []))
    out_specs = [_row_spec(d, t), _vec_spec(d)] + ([_row_spec(d, t), _vec_spec(d)] if has_post else [])
    out_shape = [jax.ShapeDtypeStruct((r, d), F32), jax.ShapeDtypeStruct((1, d), F32)]
    if has_post:
        out_shape += [jax.ShapeDtypeStruct((r, d), BF16), jax.ShapeDtypeStruct((1, d), F32)]
    outs = list(_pallas(body, name=name, grid=(r // t,), in_specs=in_specs, out_specs=out_specs,
                        out_shape=out_shape, compiler_params=_cparams(("arbitrary",)))(*ins))
    return (outs[0], outs[2], outs[1], outs[3]) if has_post else (outs[0], None, outs[1], None)


def _softplus(z):
    return jnp.maximum(z, 0.0) + jnp.log(1.0 + jnp.exp(-jnp.abs(z)))


def _sb_consts(t):
    row = lax.broadcasted_iota(jnp.int32, (t, t), 0)
    col = lax.broadcasted_iota(jnp.int32, (t, t), 1)
    m_up = (col >= row).astype(BF16)
    m_low = (col <= row).astype(BF16)
    return m_up, m_low


def _emit_chains(chains, stages, skew):
    if skew:
        for step in range(len(chains) + len(stages) - 1):
            for si, stage in enumerate(stages):
                if 0 <= step - si < len(chains):
                    stage(chains[step - si])
    else:
        for stage in stages:
            for c in chains:
                stage(c)


def _sb_fwd(q, k, vt3, pad, name, ride=((), ())):
    r = q.shape[0]
    t = ATT_BLK
    nb = r // t
    nbp = -(-(nb + 1) // 8) * 8
    jmin = pad // t
    scale = SB_HEAD_DIM ** -0.5
    n_ride = len(ride[0])

    def body(q_ref, k_ref, vt_ref, *rest):
        ride_in, (o_ref, ss_ref), ride_out = rest[:n_ride], rest[n_ride:n_ride + 2], rest[n_ride + 2:2 * n_ride + 2]
        acc_ref, kn_ref = rest[2 * n_ride + 2:2 * n_ride + 4]
        ride_sems = rest[2 * n_ride + 4:]
        i = pl.program_id(1)
        if n_ride:
            @pl.when((pl.program_id(0) == 0) & (i == 0))
            def _():
                for cp in _exchange_copies(ride_in, ride_out, ride[1], *ride_sems):
                    cp.start()

        @pl.when(i == 0)
        def _():
            def blk(b, m):
                kb = k_ref[pl.ds(pl.multiple_of(b * t, t), t), :].astype(F32)
                return jnp.maximum(m, jnp.max(jnp.sum(kb * kb, axis=1, keepdims=True), axis=0, keepdims=True))
            kn_ref[...] = jnp.broadcast_to(lax.fori_loop(0, nb, blk, jnp.zeros((1, 1), F32)), (8, 128))

        qf = q_ref[...].astype(F32)
        z_bound = scale * jnp.sqrt(jnp.max(jnp.sum(qf * qf, axis=1, keepdims=True)) * jnp.max(kn_ref[...]))

        def need(carry):
            return jnp.maximum(jnp.max(carry[0]), jnp.max(carry[1])) + z_bound >= SB_LOG_ZERO

        qt = qf.T
        sub = lax.broadcasted_iota(jnp.int32, (128, 1), 0)
        m_up, _ = _sb_consts(t)
        kpos0 = lax.broadcasted_iota(jnp.int32, (t, 1), 0)
        qpos = i * t + lax.broadcasted_iota(jnp.int32, (1, t), 1)
        qths = [jnp.where((sub >= 64 * h) & (sub < 64 * (h + 1)), qt * scale, 0.0).astype(BF16) for h in range(2)]
        acc_ref[...] = jnp.zeros_like(acc_ref)

        def sweep(js, carry, masked):
            kbs = [k_ref[pl.ds(pl.multiple_of(j * t, t), t), :] for j in js]
            vts = [vt_ref[0, j] for j in js]
            accs = [acc_ref[0], acc_ref[1]]
            s = list(carry)
            chains = [(n, h) for n in range(len(js)) for h in range(2)]
            masked = [masked] * len(js) if isinstance(masked, bool) else masked
            valid = [(js[n] * t + kpos0 < qpos) & (js[n] * t + kpos0 >= pad) if masked[n] else None for n in range(len(js))]
            zt, inc, saves = {}, {}, []

            def st_scores(c):
                zt[c] = _dot(kbs[c[0]], qths[c[1]])

            def st_cumsum(c):
                lk = -_softplus(zt[c])
                if masked[c[0]]:
                    lk = jnp.where(valid[c[0]], lk, 0.0)
                inc[c] = _split_dot(m_up, lk)

            def st_weights(c):
                n, h = c
                saves.append((h, js[n], s[h]))
                w = jnp.exp(zt[c] + inc[c] + s[h])
                if masked[n]:
                    w = jnp.where(valid[n], w, 0.0)
                accs[h] = accs[h] + _dot(vts[n], w.astype(BF16))
                s[h] = s[h] + inc[c][0:1, :]

            _emit_chains(chains, [st_scores, st_cumsum, st_weights], SB_FWD_SKEW)
            for h, j, val in saves:
                ss_ref[h, 0, pl.ds(j, 1), :] = val
            acc_ref[0] = accs[0]
            acc_ref[1] = accs[1]
            return tuple(s)

        zero = jnp.zeros((1, t), F32)
        bpi = SB_BLOCKS_PER_TRIP
        j, carry = lax.cond(
            i - 1 > jmin,
            lambda: (i - 2, sweep([i, i - 1], (zero, zero), [True, False])),
            lambda: (i - 1, sweep([i], (zero, zero), True)))
        def further(j, carry):
            j, carry = lax.while_loop(
                lambda st: (st[0] - bpi >= jmin) & need(st[1]),
                lambda st: (st[0] - bpi, sweep([st[0] - b for b in range(bpi)], st[1], False)), (j, carry))
            j, carry = lax.while_loop(
                lambda st: (st[0] > jmin) & need(st[1]),
                lambda st: (st[0] - 1, sweep([st[0]], st[1], False)), (j, carry))
            return lax.while_loop(
                lambda st: (st[0] == jmin) & (i > jmin) & need(st[1]),
                lambda st: (st[0] - 1, sweep([st[0]], st[1], True)), (j, carry))[0]

        j = lax.cond((j >= jmin) & need(carry), lambda: further(j, carry), lambda: j)
        first = jnp.full((1, t), j + 1, jnp.int32).astype(F32)
        ss_ref[0, 0, nbp - 1:nbp, :] = first
        ss_ref[1, 0, nbp - 1:nbp, :] = first
        acc = jnp.where(sub < 64, acc_ref[0], acc_ref[1])
        o_ref[...] = acc.T
        if n_ride:
            @pl.when((pl.program_id(0) == 3) & (i == nb - 1))
            def _():
                for cp in _exchange_copies(ride_in, ride_out, ride[1], *ride_sems):
                    cp.wait()

    hbm = pl.BlockSpec(memory_space=pl.ANY)
    outs = _pallas(
        body, name=name, grid=(4, nb),
        in_specs=[pl.BlockSpec((t, 128), lambda hp, i: (i, hp)),
                  pl.BlockSpec((r, 128), lambda hp, i: (0, hp)),
                  pl.BlockSpec((1, nb, 128, t), lambda hp, i: (hp, 0, 0, 0))] + [hbm] * n_ride,
        out_specs=[pl.BlockSpec((t, 128), lambda hp, i: (i, hp)),
                   pl.BlockSpec((2, 1, nbp, t), lambda hp, i: (hp, i, 0, 0))] + [hbm] * n_ride,
        out_shape=[jax.ShapeDtypeStruct((r, SB_WIDTH), F32),
                   jax.ShapeDtypeStruct((8, nb, nbp, t), F32)] + _exchange_shapes(*ride),
        scratch_shapes=[pltpu.VMEM((2, 128, t), F32), pltpu.VMEM((8, 128), F32)] + (_exchange_sems(n_ride) if n_ride else []),
        compiler_params=_cparams(("arbitrary", "arbitrary")),
    )(q, k, vt3, *ride[0])
    return outs[0], outs[1], list(outs[2:])


def _sb_bwd(q, k, v, kt3, ssave, do, pad, name):
    r = q.shape[0]
    t = ATT_BLK
    nb = r // t
    nbp = ssave.shape[2]
    jmin = pad // t
    scale = SB_HEAD_DIM ** -0.5

    def body(q_ref, do_ref, k_ref, v_ref, kt_ref, ss_ref, dq_ref, dk_hbm, dv_hbm, dk_acc, dv_acc, dq_acc, sem):
        hp = pl.program_id(0)
        i = pl.program_id(1)

        @pl.when(i == 0)
        def _():
            dk_acc[...] = jnp.zeros_like(dk_acc)
            dv_acc[...] = jnp.zeros_like(dv_acc)

        qf = q_ref[...].astype(F32)
        dof = do_ref[...]
        qt = qf.T
        dot_ = dof.T
        sub = lax.broadcasted_iota(jnp.int32, (128, 1), 0)
        lane = lax.broadcasted_iota(jnp.int32, (1, 128), 1)
        m_up, m_low = _sb_consts(t)
        kpos0 = lax.broadcasted_iota(jnp.int32, (t, 1), 0)
        qpos = i * t + lax.broadcasted_iota(jnp.int32, (1, t), 1)
        first = jnp.clip(jnp.max(ss_ref[0, 0, nbp - 1:nbp, :]).astype(jnp.int32), jmin, i)
        mid0 = jnp.maximum(first, jmin + 1)
        pair = i - mid0 >= 1
        n_mid = jnp.maximum(i - mid0 - 1, 0)
        n_edge = jnp.where((i > jmin) & (first == jmin), 1, 0)
        in_t = [(sub >= 64 * h) & (sub < 64 * (h + 1)) for h in range(2)]
        in_l = [(lane >= 64 * h) & (lane < 64 * (h + 1)) for h in range(2)]
        qths = [jnp.where(in_t[h], qt * scale, 0.0).astype(BF16) for h in range(2)]
        doths = [jnp.where(in_t[h], dot_, 0.0).astype(BF16) for h in range(2)]
        qhs = [jnp.where(in_l[h], qf * scale, 0.0).astype(BF16) for h in range(2)]
        dohs = [jnp.where(in_l[h], dof, 0.0).astype(BF16) for h in range(2)]
        dq_acc[...] = jnp.zeros_like(dq_acc)

        def sweep(js, carry, masked):
            rows = [pl.ds(pl.multiple_of(j * t, t), t) for j in js]
            kbs = [k_ref[rw, :] for rw in rows]
            vbs = [v_ref[rw, :] for rw in rows]
            kts = [kt_ref[0, j] for j in js]
            sss = [[ss_ref[h, 0, pl.ds(j, 1), :] for h in range(2)] for j in js]
            dv_old = [dv_acc[rw, :] for rw in rows]
            dk_old = [dk_acc[rw, :] for rw in rows]
            dqs = [dq_acc[0], dq_acc[1]]
            ec = list(carry)
            chains = [(n, h) for n in range(len(js)) for h in range(2)]
            masked = [masked] * len(js) if isinstance(masked, bool) else masked
            valid = [(js[n] * t + kpos0 < qpos) & (js[n] * t + kpos0 >= pad) if masked[n] else None for n in range(len(js))]
            zt, dvt, sp, inc, e, big_e = {}, {}, {}, {}, {}, {}

            def st_scores(c):
                zt[c] = _dot(kbs[c[0]], qths[c[1]])
                dvt[c] = _dot(vbs[c[0]], doths[c[1]])

            def st_cumsum(c):
                sp[c] = _softplus(zt[c])
                lk = -sp[c]
                if masked[c[0]]:
                    lk = jnp.where(valid[c[0]], lk, 0.0)
                inc[c] = _split_dot(m_up, lk)

            def st_weights(c):
                n, h = c
                w = jnp.exp(zt[c] + inc[c] + sss[n][h])
                if masked[n]:
                    w = jnp.where(valid[n], w, 0.0)
                dv_old[n] = dv_old[n] + _dot(w.astype(BF16), dohs[h])
                e[c] = w * dvt[c]
                pinc = _split_dot(m_low, e[c])
                big_e[c] = pinc - e[c] + ec[h]
                ec[h] = ec[h] + pinc[t - 1:t, :]

            def st_dscores(c):
                n, h = c
                dz = e[c] - jnp.exp(zt[c] - sp[c]) * (e[c] + big_e[c])
                if masked[n]:
                    dz = jnp.where(valid[n], dz, 0.0)
                dzb = dz.astype(BF16)
                dqs[h] = dqs[h] + _dot(kts[n], dzb)
                dk_old[n] = dk_old[n] + _dot(dzb, qhs[h])

            _emit_chains(chains, [st_scores, st_cumsum, st_weights, st_dscores], SB_BWD_SKEW)
            for n, rw in enumerate(rows):
                dv_acc[rw, :] = dv_old[n]
                dk_acc[rw, :] = dk_old[n]
            dq_acc[0] = dqs[0]
            dq_acc[1] = dqs[1]
            return tuple(ec)

        zero = jnp.zeros((1, t), F32)
        bpi = SB_BLOCKS_PER_TRIP
        carry = lax.fori_loop(0, n_edge, lambda it, c: sweep([jmin + it * 0], c, True), (zero, zero))
        carry = lax.fori_loop(0, n_mid // bpi, lambda it, c: sweep([mid0 + bpi * it + b for b in range(bpi)], c, False), carry)
        n_rem = n_mid % bpi
        carry = lax.fori_loop(0, n_rem, lambda it, c: sweep([i - 1 - n_rem + it], c, False), carry)
        lax.cond(pair, lambda: sweep([i - 1, i], carry, [False, True]), lambda: sweep([i], carry, True))
        dq_ref[...] = (jnp.where(sub < 64, dq_acc[0], dq_acc[1]) * scale).T

        @pl.when(i == nb - 1)
        def _():
            c1 = pltpu.make_async_copy(dk_acc, dk_hbm.at[hp], sem.at[0])
            c2 = pltpu.make_async_copy(dv_acc, dv_hbm.at[hp], sem.at[1])
            c1.start()
            c2.start()
            c1.wait()
            c2.wait()

    return _pallas(
        body, name=name, grid=(4, nb),
        in_specs=[pl.BlockSpec((t, 128), lambda hp, i: (i, hp)),
                  pl.BlockSpec((t, 128), lambda hp, i: (i, hp)),
                  pl.BlockSpec((r, 128), lambda hp, i: (0, hp)),
                  pl.BlockSpec((r, 128), lambda hp, i: (0, hp)),
                  pl.BlockSpec((1, nb, 128, t), lambda hp, i: (hp, 0, 0, 0)),
                  pl.BlockSpec((2, 1, nbp, t), lambda hp, i: (hp, i, 0, 0))],
        out_specs=[pl.BlockSpec((t, 128), lambda hp, i: (i, hp)),
                   pl.BlockSpec(memory_space=pl.ANY), pl.BlockSpec(memory_space=pl.ANY)],
        out_shape=[jax.ShapeDtypeStruct((r, SB_WIDTH), F32),
                   jax.ShapeDtypeStruct((4, r, 128), F32), jax.ShapeDtypeStruct((4, r, 128), F32)],
        scratch_shapes=[pltpu.VMEM((r, 128), F32), pltpu.VMEM((r, 128), F32), pltpu.VMEM((2, 128, t), F32),
                        pltpu.SemaphoreType.DMA((2,))],
        compiler_params=_cparams(("arbitrary", "arbitrary")),
    )(q, do, k, v, kt3, ssave)


def _s5_disc(lam_re, lam_im, logdt, btr, bti):
    lr = jnp.minimum(lam_re, -1e-4)
    li = lam_im
    dt = jnp.exp(logdt)
    mag = jnp.exp(lr * dt)
    ang = li * dt
    a_re, a_im = mag * jnp.cos(ang), mag * jnp.sin(ang)
    den = lr * lr + li * li
    nr, ni = a_re - 1.0, a_im
    c_re = (nr * lr + ni * li) / den
    c_im = (ni * lr - nr * li) / den
    return a_re, a_im, c_re * btr - c_im * bti, c_re * bti + c_im * btr


def _s5_prep(lam_re, lam_im, logdt, btr, bti, name):
    ns = lam_re.shape[1]

    def body(lr_ref, li_ref, dt_ref, br_ref, bi_ref, ar_ref, ai_ref, bbr_ref, bbi_ref):
        ar, ai, bbr, bbi = _s5_disc(lr_ref[...], li_ref[...], dt_ref[...], br_ref[...], bi_ref[...])
        ar_ref[...] = ar
        ai_ref[...] = ai
        bbr_ref[...] = bbr
        bbi_ref[...] = bbi

    return _pallas(body, name=name,
                   out_shape=[jax.ShapeDtypeStruct((1, ns), F32)] * 2 + [jax.ShapeDtypeStruct((S5_GROUP, ns), F32)] * 2,
                   )(lam_re, lam_im, logdt, btr, bti)


def _s5_prep_bwd(lam_re, lam_im, logdt, btr, bti, dar, dai, dbbr, dbbi, name):
    ns = lam_re.shape[1]

    def body(lr_ref, li_ref, dt_ref, br_ref, bi_ref, dar_ref, dai_ref, dbr_ref, dbi_ref, o_lr, o_li, o_dt, o_br, o_bi):
        _, vjp = jax.vjp(_s5_disc, lr_ref[...], li_ref[...], dt_ref[...], br_ref[...], bi_ref[...])
        g = vjp((dar_ref[...], dai_ref[...], dbr_ref[...], dbi_ref[...]))
        o_lr[...] = g[0]
        o_li[...] = g[1]
        row = lax.broadcasted_iota(jnp.int32, (ns, ns), 0) // S5_STATE
        col = lax.broadcasted_iota(jnp.int32, (ns, ns), 1) // S5_STATE
        same = (row == col).astype(F32)
        o_dt[...] = _dot_hi(jnp.broadcast_to(g[2], (8, ns)), same)[0:1]
        o_br[...] = g[3]
        o_bi[...] = g[4]

    return _pallas(body, name=name,
                   out_shape=[jax.ShapeDtypeStruct((1, ns), F32)] * 3 + [jax.ShapeDtypeStruct((S5_GROUP, ns), F32)] * 2,
                   compiler_params=pltpu.CompilerParams(vmem_limit_bytes=VMEM_LIMIT),
                   )(lam_re, lam_im, logdt, btr, bti, dar, dai, dbbr, dbbi)


def _s5_scan(br, bi, ar, ai, t, reverse=False, carry=None):
    ng = t // 8
    ns = br.shape[1]
    br, bi = br.reshape(ng, 8, ns), bi.reshape(ng, 8, ns)
    row8 = lax.broadcasted_iota(jnp.int32, (1, 8, 1), 1)
    pr, pi_ = ar, ai
    for k in (1, 2, 4):
        if reverse:
            sr, si, ok = pltpu.roll(br, 8 - k, 1), pltpu.roll(bi, 8 - k, 1), row8 < 8 - k
        else:
            sr, si, ok = pltpu.roll(br, k, 1), pltpu.roll(bi, k, 1), row8 >= k
        sr = jnp.where(ok, sr, 0.0)
        si = jnp.where(ok, si, 0.0)
        br, bi = br + pr * sr - pi_ * si, bi + pr * si + pi_ * sr
        pr, pi_ = pr * pr - pi_ * pi_, 2.0 * pr * pi_
    pw_r, pw_i = [ar], [ai]
    for _ in range(7):
        pw_r.append(pw_r[-1] * ar - pw_i[-1] * ai)
        pw_i.append(pw_r[-2] * ai + pw_i[-1] * ar)
    if reverse:
        pw_r.reverse()
        pw_i.reverse()
    p8r, p8i = jnp.concatenate(pw_r, axis=0), jnp.concatenate(pw_i, axis=0)
    out_r, out_i = [None] * ng, [None] * ng
    order = range(ng - 1, -1, -1) if reverse else range(ng)
    edge = 0 if reverse else 7
    for g in order:
        gr, gi = br[g], bi[g]
        if carry is not None:
            cr, ci = carry
            gr, gi = gr + p8r * cr - p8i * ci, gi + p8r * ci + p8i * cr
        out_r[g], out_i[g] = gr, gi
        carry = (gr[edge:edge + 1], gi[edge:edge + 1])
    return jnp.concatenate(out_r, axis=0), jnp.concatenate(out_i, axis=0)


def _s5_prev_rows(x, first, t):
    ng = t // 8
    ns = x.shape[1]
    x3 = x.reshape(ng, 8, ns)
    last = x3[:, 7:8, :]
    before = jnp.concatenate([first.reshape(1, 1, ns), last[:ng - 1]], axis=0)
    row8 = lax.broadcasted_iota(jnp.int32, (1, 8, 1), 1)
    return jnp.where(row8 == 0, before, pltpu.roll(x3, 1, 1)).reshape(t, ns)


_GELU_C = math.sqrt(2.0 / math.pi)


def _gelu(y):
    th = jnp.tanh(_GELU_C * (y + 0.044715 * y * y * y))
    return 0.5 * y * (1.0 + th), th


def _sigmoid(x):
    return 1.0 / (1.0 + jnp.exp(-x))


def _s5_fwd(u, wb, a, wc, dskip, wglu, bglu, gnorm, name):
    r = u.shape[0]
    t = S5_TILE
    nt = r // t
    ns = wb.shape[2]
    w = S5_WIDTH

    def body(u_ref, wb_ref, a_ref, wc_ref, d_ref, wg_ref, bg_ref, gn_ref, y_ref, on_ref, xs_ref, carry_ref):
        i = pl.program_id(0)
        ar, ai = a_ref[0], a_ref[1]

        @pl.when(i == 0)
        def _():
            carry_ref[...] = jnp.zeros_like(carry_ref)

        u_ = u_ref[...]
        ub = u_.astype(BF16)
        xs_ref[0] = carry_ref[:, 0, :]
        chunks = list(range(S5_CHUNKS))
        sl_s = [slice(c * (ns // S5_CHUNKS), (c + 1) * (ns // S5_CHUNKS)) for c in chunks]
        sl_u = [slice(c * (w // S5_CHUNKS), (c + 1) * (w // S5_CHUNKS)) for c in chunks]
        bu, xs, ys = {}, {}, {}

        def st_inputs(c):
            bu[c] = (_dot(ub[:, sl_u[c]], wb_ref[0, sl_u[c], sl_s[c]]), _dot(ub[:, sl_u[c]], wb_ref[1, sl_u[c], sl_s[c]]))

        def st_scan(c):
            xr, xi = _s5_scan(*bu[c], ar[:, sl_s[c]], ai[:, sl_s[c]], t, carry=(carry_ref[0, :, sl_s[c]], carry_ref[1, :, sl_s[c]]))
            carry_ref[0, :, sl_s[c]] = xr[t - 1:t, :]
            carry_ref[1, :, sl_s[c]] = xi[t - 1:t, :]
            xs[c] = (xr.astype(BF16), xi.astype(BF16))

        def st_outputs(c):
            ys[c] = _dot(xs[c][0], wc_ref[0, sl_s[c], sl_u[c]]) - _dot(xs[c][1], wc_ref[1, sl_s[c], sl_u[c]])

        _emit_chains(chunks, [st_inputs, st_scan, st_outputs], False)
        y = jnp.concatenate([ys[c] for c in chunks], axis=1) + d_ref[...] * u_
        h, _ = _gelu(y)
        gate = _sigmoid(_dot(h.astype(BF16), wg_ref[...]) + bg_ref[...])
        y_ref[...] = y
        on_ref[...] = _rms(h * gate, gn_ref[...]).astype(BF16)

    full = lambda shape: pl.BlockSpec(shape, lambda i: (0,) * len(shape))
    return _pallas(
        body, name=name, grid=(nt,),
        in_specs=[_row_spec(w, t), full((2, w, ns)), full((2, 1, ns)), full((2, ns, w)), full((1, w)),
                  full((w, w)), full((1, w)), full((1, w))],
        out_specs=[_row_spec(w, t), _row_spec(w, t), pl.BlockSpec((1, 2, ns), lambda i: (i, 0, 0))],
        out_shape=[jax.ShapeDtypeStruct((r, w), F32), jax.ShapeDtypeStruct((r, w), BF16),
                   jax.ShapeDtypeStruct((nt, 2, ns), F32)],
        scratch_shapes=[pltpu.VMEM((2, 1, ns), F32)],
        compiler_params=_cparams(("arbitrary",)),
    )(u, wb, a, wc, dskip, wglu, bglu, gnorm)


def _s5_bwd(u, y, don, xstart, wb, a, wc, dskip, wglu, bglu, gnorm, name, ride=((), ()), don_block=0):
    r = u.shape[0]
    t = S5_TILE
    nt = r // t
    ns = wb.shape[2]
    w = S5_WIDTH
    nt_dims = ((1,), (1,))
    tn_dims = ((0,), (0,))

    def body(u_ref, y_ref, don_ref, xs_ref, wb_hbm, a_ref, wc_hbm, d_ref, wg_ref, bg_ref, gn_ref,
             du_ref, da_ref, dd_ref, dbg_ref, dgn_ref, dwb_hbm, dwc_hbm, dwg_hbm,
             wb_ref, wc_ref, lam_ref, acc_wb, acc_wc, acc_wg, sem):
        i = pl.program_id(0)
        ar, ai = a_ref[0], a_ref[1]

        @pl.when(i == 0)
        def _():
            c1 = pltpu.make_async_copy(wb_hbm, wb_ref, sem.at[0])
            c2 = pltpu.make_async_copy(wc_hbm, wc_ref, sem.at[1])
            c1.start()
            c2.start()
            lam_ref[...] = jnp.zeros_like(lam_ref)
            acc_wb[...] = jnp.zeros_like(acc_wb)
            acc_wc[...] = jnp.zeros_like(acc_wc)
            acc_wg[...] = jnp.zeros_like(acc_wg)
            da_ref[...] = jnp.zeros_like(da_ref)
            dd_ref[...] = jnp.zeros_like(dd_ref)
            dbg_ref[...] = jnp.zeros_like(dbg_ref)
            dgn_ref[...] = jnp.zeros_like(dgn_ref)
            c1.wait()
            c2.wait()

        u_ = u_ref[...]
        y_ = y_ref[...]
        ub = u_.astype(BF16)
        h, th = _gelu(y_)
        hb = h.astype(BF16)
        wg = wg_ref[...]
        gate = _sigmoid(_dot(hb, wg) + bg_ref[...])
        d_out, dgn = _rms_bwd(h * gate, gn_ref[...], don_ref[...])
        dgn_ref[...] += dgn
        dhw = d_out * h * gate * (1.0 - gate)
        dhwb = dhw.astype(BF16)
        dh = d_out * gate + _dot(dhwb, wg, nt_dims)
        acc_wg[...] += _dot(hb, dhwb, tn_dims)
        dbg_ref[...] += jnp.sum(dhw, axis=0, keepdims=True)
        dgelu = 0.5 * (1.0 + th) + 0.5 * y_ * (1.0 - th * th) * _GELU_C * (1.0 + 3.0 * 0.044715 * y_ * y_)
        dy = dh * dgelu
        dd_ref[...] += jnp.sum(dy * u_, axis=0, keepdims=True)
        dyb = dy.astype(BF16)
        chunks = list(range(S5_CHUNKS))
        sl_s = [slice(c * (ns // S5_CHUNKS), (c + 1) * (ns // S5_CHUNKS)) for c in chunks]
        sl_u = [slice(c * (w // S5_CHUNKS), (c + 1) * (w // S5_CHUNKS)) for c in chunks]
        bu, gx, x_, lam, dus = {}, {}, {}, {}, {}

        def st_inputs(c):
            su, ss = sl_u[c], sl_s[c]
            bu[c] = (_dot(ub[:, su], wb_ref[0, su, ss]), _dot(ub[:, su], wb_ref[1, su, ss]))
            gx[c] = (_dot(dyb[:, su], wc_ref[0, ss, su], nt_dims), -_dot(dyb[:, su], wc_ref[1, ss, su], nt_dims))

        def st_states(c):
            su, ss = sl_u[c], sl_s[c]
            first = (xs_ref[0, 0:1, ss], xs_ref[0, 1:2, ss])
            xr, xi = _s5_scan(*bu[c], ar[:, ss], ai[:, ss], t, carry=first)
            acc_wc[0, ss, su] += _dot(xr.astype(BF16), dyb[:, su], tn_dims)
            acc_wc[1, ss, su] -= _dot(xi.astype(BF16), dyb[:, su], tn_dims)
            x_[c] = (_s5_prev_rows(xr, first[0], t), _s5_prev_rows(xi, first[1], t))

        def st_adjoint(c):
            su, ss = sl_u[c], sl_s[c]
            lr, li = _s5_scan(*gx[c], ar[:, ss], -ai[:, ss], t, reverse=True, carry=(lam_ref[0, :, ss], lam_ref[1, :, ss]))
            lam_ref[0, :, ss] = lr[0:1, :]
            lam_ref[1, :, ss] = li[0:1, :]
            lrb, lib = lr.astype(BF16), li.astype(BF16)
            acc_wb[0, su, ss] += _dot(ub[:, su], lrb, tn_dims)
            acc_wb[1, su, ss] += _dot(ub[:, su], lib, tn_dims)
            dus[c] = _dot(lrb, wb_ref[0, su, ss], nt_dims) + _dot(lib, wb_ref[1, su, ss], nt_dims)
            lam[c] = (lr, li)

        def st_decay(c):
            ss = sl_s[c]
            (lr, li), (xpr, xpi) = lam[c], x_[c]
            da_ref[0, :, ss] += jnp.sum(lr * xpr + li * xpi, axis=0, keepdims=True)
            da_ref[1, :, ss] += jnp.sum(li * xpr - lr * xpi, axis=0, keepdims=True)

        _emit_chains(chunks, [st_inputs, st_states, st_adjoint, st_decay], False)
        du_ref[...] = d_ref[...] * dy + jnp.concatenate([dus[c] for c in chunks], axis=1)

        @pl.when(i == nt - 1)
        def _():
            cps = [pltpu.make_async_copy(acc_wb, dwb_hbm, sem.at[0]), pltpu.make_async_copy(acc_wc, dwc_hbm, sem.at[1]),
                   pltpu.make_async_copy(acc_wg, dwg_hbm, sem.at[2])]
            for c in cps:
                c.start()
            for c in cps:
                c.wait()

    n_ride = len(ride[0])
    n_in, n_out, n_scratch = 11, 8, 7

    def body_with_ride(*refs):
        ins, rest = refs[:n_in], refs[n_in:]
        ride_in, rest = rest[:n_ride], rest[n_ride:]
        outs, rest = rest[:n_out], rest[n_out:]
        ride_out, rest = rest[:n_ride], rest[n_ride:]
        scratch, ride_sems = rest[:n_scratch], rest[n_scratch:]
        if n_ride:
            @pl.when(pl.program_id(0) == 0)
            def _():
                for cp in _exchange_copies(ride_in, ride_out, ride[1], *ride_sems):
                    cp.start()
        body(*ins, *outs, *scratch)
        if n_ride:
            @pl.when(pl.program_id(0) == nt - 1)
            def _():
                for cp in _exchange_copies(ride_in, ride_out, ride[1], *ride_sems):
                    cp.wait()

    rev = lambda i: (nt - 1 - i, 0)
    full = lambda shape: pl.BlockSpec(shape, lambda i: (0,) * len(shape))
    hbm = pl.BlockSpec(memory_space=pl.ANY)
    outs = _pallas(
        body_with_ride, name=name, grid=(nt,),
        in_specs=[pl.BlockSpec((t, w), rev), pl.BlockSpec((t, w), rev), pl.BlockSpec((t, w), lambda i: (nt - 1 - i, don_block)),
                  pl.BlockSpec((1, 2, ns), lambda i: (nt - 1 - i, 0, 0)), hbm, full((2, 1, ns)), hbm, full((1, w)),
                  full((w, w)), full((1, w)), full((1, w))] + [hbm] * n_ride,
        out_specs=[pl.BlockSpec((t, w), rev), full((2, 1, ns)), full((1, w)), full((1, w)), full((1, w)), hbm, hbm, hbm]
        + [hbm] * n_ride,
        out_shape=[jax.ShapeDtypeStruct((r, w), F32), jax.ShapeDtypeStruct((2, 1, ns), F32)]
        + [jax.ShapeDtypeStruct((1, w), F32)] * 3
        + [jax.ShapeDtypeStruct((2, w, ns), F32), jax.ShapeDtypeStruct((2, ns, w), F32), jax.ShapeDtypeStruct((w, w), F32)]
        + _exchange_shapes(*ride),
        scratch_shapes=[pltpu.VMEM((2, w, ns), BF16), pltpu.VMEM((2, ns, w), BF16), pltpu.VMEM((2, 1, ns), F32),
                        pltpu.VMEM((2, w, ns), F32), pltpu.VMEM((2, ns, w), F32), pltpu.VMEM((w, w), F32),
                        pltpu.SemaphoreType.DMA((3,))] + (_exchange_sems(n_ride) if n_ride else []),
        compiler_params=_cparams(("arbitrary",)),
    )(u, y, don, xstart, wb, a, wc, dskip, wglu, bglu, gnorm, *ride[0])
    return tuple(outs[:n_out]) + (list(outs[n_out:]),)


def _s5_expand(lam_re, lam_im, log_dt, b_re, b_im, c_re, c_im):
    g, n, p = S5_GROUPS, S5_STATE, S5_GROUP
    ns = g * n
    rows = lambda x: x.reshape(1, ns)
    logdt = jnp.repeat(log_dt.reshape(g), n).reshape(1, ns)
    btr = b_re.reshape(ns, p).T
    bti = b_im.reshape(ns, p).T
    ctr = c_re.transpose(0, 2, 1).reshape(ns, p)
    cti = c_im.transpose(0, 2, 1).reshape(ns, p)
    mask = (jnp.arange(g * p)[:, None] // p) == (jnp.arange(ns)[None, :] // n)
    return rows(lam_re), rows(lam_im), logdt, btr, bti, ctr, cti, mask


def _s5_block_diag_b(bb, mask):
    return jnp.where(mask, jnp.tile(bb, (S5_GROUPS, 1)), 0.0)


def _s5_block_diag_c(ct, mask):
    return jnp.where(mask.T, jnp.tile(ct, (1, S5_GROUPS)), 0.0)


def _s5_diag_of_b(dwb, mask):
    return jnp.where(mask, dwb, 0.0).reshape(S5_GROUPS, S5_GROUP, -1).sum(0)


def _s5_diag_of_c(dwc, mask):
    ns = dwc.shape[0]
    return jnp.where(mask.T, dwc, 0.0).reshape(ns, S5_GROUPS, S5_GROUP).sum(1)


DN_PRE_TILE = 256
_DN_QKV = 3 * DN_WIDTH


def _halo_specs(width, tile, nt, prev):
    per = tile // 8
    if prev:
        return pl.BlockSpec((8, width), lambda i: (jnp.maximum(i * per - 1, 0), 0))
    return pl.BlockSpec((8, width), lambda i: (jnp.minimum((i + 1) * per, nt * per - 1), 0))


def _shift_down(x, halo, s, t):
    xx = jnp.concatenate([halo, x], axis=0)
    return pltpu.roll(xx, s, 0)[8:]


def _shift_up(x, halo, s, t):
    xx = jnp.concatenate([x, halo], axis=0)
    return pltpu.roll(xx, t + 8 - s, 0)[:t]


def _silu(x):
    s = _sigmoid(x)
    return x * s, s


def _dn_gates(ab, alog, dtb, live):
    lane = lax.broadcasted_iota(jnp.int32, (1, 128), 1)
    g = -jnp.exp(alog) * _softplus(ab + dtb)
    beta = _sigmoid(ab)
    return jnp.where(live & (lane < DN_HEADS), g, jnp.where(live & (lane < 2 * DN_HEADS), beta, 0.0))


def _dn_pre_fwd(proj, ab, conv_w, alog, dtb, pad, name):
    r = proj.shape[0]
    t = DN_PRE_TILE
    nt = r // t
    scale = DN_HEAD_DIM ** -0.5

    def body(x_ref, halo_ref, ab_ref, w_ref, al_ref, dt_ref, q_ref, k_ref, v_ref, gb_ref):
        i = pl.program_id(0)
        act, _ = _silu(_dn_conv(x_ref[...], jnp.where(i > 0, halo_ref[...], 0.0), w_ref[...], t))
        for hd in range(DN_HEADS):
            sl = slice(hd * 128, (hd + 1) * 128)
            for base, o_ref, sc in ((0, q_ref, scale), (DN_WIDTH, k_ref, 1.0)):
                xh = act[:, base + hd * 128: base + (hd + 1) * 128]
                o_ref[:, sl] = (xh * (lax.rsqrt(jnp.sum(xh * xh, axis=-1, keepdims=True) + EPS) * sc)).astype(BF16)
        v_ref[...] = act[:, 2 * DN_WIDTH:].astype(BF16)
        rows = i * t + lax.broadcasted_iota(jnp.int32, (t, 1), 0)
        gb_ref[...] = _dn_gates(ab_ref[...], al_ref[...], dt_ref[...], rows >= pad)

    return _pallas(
        body, name=name, grid=(nt,),
        in_specs=[pl.BlockSpec((t, _DN_QKV), lambda i: (i, 0)), _halo_specs(_DN_QKV, t, nt, True), _row_spec(128, t),
                  pl.BlockSpec((DN_CONV, _DN_QKV), lambda i: (0, 0)), _vec_spec(128), _vec_spec(128)],
        out_specs=[_row_spec(DN_WIDTH, t), _row_spec(DN_WIDTH, t), _row_spec(DN_WIDTH, t), _row_spec(128, t)],
        out_shape=[jax.ShapeDtypeStruct((r, DN_WIDTH), BF16)] * 3 + [jax.ShapeDtypeStruct((r, 128), F32)],
        compiler_params=_cparams(("parallel",)),
    )(proj, proj, ab, conv_w, alog, dtb)


def _dn_conv(x, halo, w, t):
    co = w[DN_CONV - 1:DN_CONV] * x
    for tap in range(DN_CONV - 1):
        co = co + w[tap:tap + 1] * _shift_down(x, halo, DN_CONV - 1 - tap, t)
    return co


def _dn_pre_bwd(proj, conv_w, dq, dk, dv, dgb, ab, alog, dtb, pad, name):
    r = proj.shape[0]
    t = DN_PRE_TILE
    nt = r // t
    scale = DN_HEAD_DIM ** -0.5

    def body(x_ref, halo_ref, w_ref, dq_ref, dk_ref, dv_ref, dgb_ref, ab_ref, al_ref, dt_ref, dco_ref, dab_ref, dal_ref,
             ddt_ref):
        i = pl.program_id(0)

        @pl.when(i == 0)
        def _():
            dal_ref[...] = jnp.zeros_like(dal_ref)
            ddt_ref[...] = jnp.zeros_like(ddt_ref)

        co_ = _dn_conv(x_ref[...], jnp.where(i > 0, halo_ref[...], 0.0), w_ref[...], t)
        act, sg = _silu(co_)
        dsilu = sg * (1.0 + co_ * (1.0 - sg))
        for hd in range(DN_HEADS):
            sl = slice(hd * 128, (hd + 1) * 128)
            for base, d_ref, sc in ((0, dq_ref, scale), (DN_WIDTH, dk_ref, 1.0)):
                cs = slice(base + hd * 128, base + (hd + 1) * 128)
                xh = act[:, cs]
                rn = lax.rsqrt(jnp.sum(xh * xh, axis=-1, keepdims=True) + EPS)
                xhat = xh * rn
                dy = d_ref[:, sl]
                dx = (sc * rn) * (dy - xhat * jnp.sum(dy * xhat, axis=-1, keepdims=True))
                dco_ref[:, cs] = dx * dsilu[:, cs]
        dco_ref[:, 2 * DN_WIDTH:] = dv_ref[...] * dsilu[:, 2 * DN_WIDTH:]
        rows = i * t + lax.broadcasted_iota(jnp.int32, (t, 1), 0)
        live = rows >= pad
        lane = lax.broadcasted_iota(jnp.int32, (1, 128), 1)
        ab_ = ab_ref[...]
        dgb_ = dgb_ref[...]
        is_g = live & (lane < DN_HEADS)
        is_b = live & (lane >= DN_HEADS) & (lane < 2 * DN_HEADS)
        arg = ab_ + dt_ref[...]
        ea = jnp.exp(al_ref[...])
        da = jnp.where(is_g, -dgb_ * ea * _sigmoid(arg), 0.0)
        beta = _sigmoid(ab_)
        dab_ref[...] = (da + jnp.where(is_b, dgb_ * beta * (1.0 - beta), 0.0)).astype(BF16)
        ddt_ref[...] += jnp.sum(da, axis=0, keepdims=True)
        dal_ref[...] += jnp.sum(jnp.where(is_g, -dgb_ * ea * _softplus(arg), 0.0), axis=0, keepdims=True)

    return _pallas(
        body, name=name, grid=(nt,),
        in_specs=[pl.BlockSpec((t, _DN_QKV), lambda i: (i, 0)), _halo_specs(_DN_QKV, t, nt, True),
                  pl.BlockSpec((DN_CONV, _DN_QKV), lambda i: (0, 0)),
                  _row_spec(DN_WIDTH, t), _row_spec(DN_WIDTH, t), _row_spec(DN_WIDTH, t),
                  _row_spec(128, t), _row_spec(128, t), _vec_spec(128), _vec_spec(128)],
        out_specs=[_row_spec(_DN_QKV, t), _row_spec(128, t), _vec_spec(128), _vec_spec(128)],
        out_shape=[jax.ShapeDtypeStruct((r, _DN_QKV), F32), jax.ShapeDtypeStruct((r, 128), BF16),
                   jax.ShapeDtypeStruct((1, 128), F32), jax.ShapeDtypeStruct((1, 128), F32)],
        compiler_params=_cparams(("arbitrary",)),
    )(proj, proj, conv_w, dq, dk, dv, dgb, ab, alog, dtb)


def _dn_conv_bwd(dco, proj, conv_w, dproj, name):
    r = dco.shape[0]
    t = DN_PRE_TILE
    nt = r // t

    def body(d_ref, dh_ref, x_ref, xh_ref, w_ref, dproj_in, dx_ref, dw_ref):
        i = pl.program_id(0)

        @pl.when(i == 0)
        def _():
            dw_ref[...] = jnp.zeros_like(dw_ref)

        d = d_ref[...]
        dhalo = jnp.where(i < nt - 1, dh_ref[...], 0.0)
        x = x_ref[...]
        xhalo = jnp.where(i > 0, xh_ref[...], 0.0)
        w = w_ref[...]
        dx = w[3:4] * d
        dws = [None] * DN_CONV
        dws[3] = jnp.sum(d * x, axis=0, keepdims=True)
        for tap in range(DN_CONV - 1):
            s = DN_CONV - 1 - tap
            dx = dx + w[tap:tap + 1] * _shift_up(d, dhalo, s, t)
            dws[tap] = jnp.sum(d * _shift_down(x, xhalo, s, t), axis=0, keepdims=True)
        dx_ref[...] = dx.astype(BF16)
        dw_ref[...] += jnp.concatenate(dws + [jnp.zeros((8 - DN_CONV, _DN_QKV), F32)], axis=0)

    return _pallas(
        body, name=name, grid=(nt,),
        in_specs=[_row_spec(_DN_QKV, t), _halo_specs(_DN_QKV, t, nt, False),
                  pl.BlockSpec((t, _DN_QKV), lambda i: (i, 0)), _halo_specs(_DN_QKV, t, nt, True),
                  pl.BlockSpec((DN_CONV, _DN_QKV), lambda i: (0, 0)), pl.BlockSpec(memory_space=pl.ANY)],
        out_specs=[_row_spec(_DN_QKV, t), pl.BlockSpec((8, _DN_QKV), lambda i: (0, 0))],
        out_shape=[jax.ShapeDtypeStruct(dproj.shape, BF16), jax.ShapeDtypeStruct((8, _DN_QKV), F32)],
        input_output_aliases={5: 0},
        compiler_params=_cparams(("arbitrary",)),
    )(dco, dco, proj, proj, conv_w, dproj)


def _split3(x):
    hi = x.astype(BF16)
    return hi, (x - hi.astype(F32)).astype(BF16)


def _dot3s(a, b, dims=((1,), (0,))):
    return _dot(a[0], b[0], dims) + (_dot(a[0], b[1], dims) + _dot(a[1], b[0], dims))


def _dot3(a, b, dims=((1,), (0,))):
    return _dot3s(_split3(a), _split3(b), dims)


def _dn_inverse_many(n_mats):
    c = n_mats[0].shape[0]
    row = lax.broadcasted_iota(jnp.int32, (c, c), 0)
    col = lax.broadcasted_iota(jnp.int32, (c, c), 1)
    eye = (row == col).astype(F32)
    same = row // DN_SUB == col // DN_SUB
    nds = [jnp.where(same, n, 0.0) for n in n_mats]
    nos = [n - nd for n, nd in zip(n_mats, nds)]

    def geometric(bs, order):
        xs = [eye + b for b in bs]
        sp = [_split3(b) for b in bs]
        k = 2
        while k < order:
            sp = [_split3(_dot3s(s_, s_)) for s_ in sp]
            xs = [x + _dot3s(_split3(x), s_) for x, s_ in zip(xs, sp)]
            k *= 2
        return xs

    tds = [_split3(td) for td in geometric([-nd for nd in nds], DN_SUB)]
    ms = [_dot3s(td, _split3(no)) for td, no in zip(tds, nos)]
    xs = geometric([-m for m in ms], c // DN_SUB)
    return [_dot3s(_split3(x), td) for x, td in zip(xs, tds)]


def _dn_chunk_shared(gb_ref, gbt_ref):
    c = DN_CHUNK
    row = lax.broadcasted_iota(jnp.int32, (c, c), 0)
    col = lax.broadcasted_iota(jnp.int32, (c, c), 1)
    gbv = gb_ref[...]
    gam_all = _split_dot((row >= col).astype(BF16), gbv)
    hi, lo = _split3(gbt_ref[...])
    tri_t = (row <= col).astype(BF16)
    return dict(row=row, col=col, gbv=gbv, gam_all=gam_all, gam_rows=_dot(hi, tri_t) + _dot(lo, tri_t),
                lane=lax.broadcasted_iota(jnp.int32, (1, 128), 1))


def _dn_chunk_common(q, k, v, sh, h):
    c = DN_CHUNK
    row, col, lane = sh["row"], sh["col"], sh["lane"]
    q, k, v = q.astype(F32), k.astype(F32), v.astype(F32)
    gam = jnp.sum(jnp.where(lane == h, sh["gam_all"], 0.0), axis=1, keepdims=True)
    beta = jnp.sum(jnp.where(lane == h + DN_HEADS, sh["gbv"], 0.0), axis=1, keepdims=True)
    gam_row = sh["gam_rows"][h:h + 1]
    dec = jnp.where(row >= col, jnp.exp(jnp.minimum(gam - gam_row, 0.0)), 0.0)
    kb, qb = k.astype(BF16), q.astype(BF16)
    nt_dims = ((1,), (1,))
    kk = _dot(kb, kb, nt_dims)
    qk = _dot(qb, kb, nt_dims)
    eg = jnp.exp(gam)
    gam_l = gam[c - 1:c, :]
    return dict(q=q, k=k, v=v, qb=qb, kb=kb, gam=gam, beta=beta, dec=dec, kk=kk, qk=qk, eg=eg, gam_l=gam_l,
                row=row, col=col, lane=lane, att=qk * dec, qg=q * eg, kt=k * jnp.exp(gam_l - gam),
                rhs=jnp.concatenate([v * beta, k * (beta * eg)], axis=1))


def _dn_fwd(q, k, v, gb, gbt, name):
    r = q.shape[0]
    c = DN_CHUNK
    nc = r // c
    dh = DN_HEAD_DIM
    tn_dims = ((0,), (0,))

    def body(q_ref, k_ref, v_ref, gb_ref, gbt_ref, o_ref, ss_ref, ts_ref, s_ref):
        @pl.when(pl.program_id(0) == 0)
        def _():
            s_ref[...] = jnp.zeros_like(s_ref)

        heads = list(range(DN_HEADS))
        sl = [slice(h * dh, (h + 1) * dh) for h in heads]
        sh = _dn_chunk_shared(gb_ref, gbt_ref)
        zs = [_dn_chunk_common(q_ref[:, sl[h]], k_ref[:, sl[h]], v_ref[:, sl[h]], sh, h) for h in heads]
        t_invs = _dn_inverse_many([jnp.where(sh["row"] > sh["col"], z["beta"] * z["kk"] * z["dec"], 0.0) for z in zs])
        sols = [_dot3(t_inv, z["rhs"]) for t_inv, z in zip(t_invs, zs)]
        ss = [s_ref[h] for h in heads]
        sbs = [s.astype(BF16) for s in ss]
        vnbs = [(sol[:, :dh] - _dot(sol[:, dh:].astype(BF16), sb)).astype(BF16) for sol, sb in zip(sols, sbs)]
        for h in heads:
            o_ref[:, sl[h]] = _dot(zs[h]["qg"].astype(BF16), sbs[h]) + _dot(zs[h]["att"].astype(BF16), vnbs[h])
        for h in heads:
            ss_ref[0, h] = ss[h]
            ts_ref[0, h] = t_invs[h]
            s_ref[h] = ss[h] * jnp.exp(zs[h]["gam_l"]) + _dot(zs[h]["kt"].astype(BF16), vnbs[h], tn_dims)

    blk = pl.BlockSpec((c, DN_WIDTH), lambda ci: (ci, 0))
    sav = pl.BlockSpec((1, DN_HEADS, dh, dh), lambda ci: (ci, 0, 0, 0))
    return _pallas(
        body, name=name, grid=(nc,),
        in_specs=[blk, blk, blk, pl.BlockSpec((c, 128), lambda ci: (ci, 0)), pl.BlockSpec((16, c), lambda ci: (0, ci))],
        out_specs=[blk, sav, sav],
        out_shape=[jax.ShapeDtypeStruct((r, DN_WIDTH), F32), jax.ShapeDtypeStruct((nc, DN_HEADS, dh, dh), F32),
                   jax.ShapeDtypeStruct((nc, DN_HEADS, dh, dh), F32)],
        scratch_shapes=[pltpu.VMEM((DN_HEADS, dh, dh), F32)],
        compiler_params=_cparams(("arbitrary",)),
    )(q, k, v, gb, gbt)


def _dn_bwd(q, k, v, gb, gbt, ssave, tsave, do, name):
    r = q.shape[0]
    c = DN_CHUNK
    nc = r // c
    dh = DN_HEAD_DIM
    nt_dims = ((1,), (1,))
    tn_dims = ((0,), (0,))

    def body(q_ref, k_ref, v_ref, gb_ref, gbt_ref, ss_ref, ts_ref, do_ref, dq_ref, dk_ref, dv_ref, dgb_ref, ds_ref):
        @pl.when(pl.program_id(0) == 0)
        def _():
            ds_ref[...] = jnp.zeros_like(ds_ref)

        heads = list(range(DN_HEADS))
        sl = [slice(h * dh, (h + 1) * dh) for h in heads]
        sh = _dn_chunk_shared(gb_ref, gbt_ref)
        row, col, lane = sh["row"], sh["col"], sh["lane"]
        rs = lambda x: jnp.sum(x, axis=1, keepdims=True)
        tot = lambda x: jnp.sum(rs(x), axis=0, keepdims=True)
        st = [dict() for _ in heads]
        dgb_parts = []

        def s_common(h):
            st[h].update(_dn_chunk_common(q_ref[:, sl[h]], k_ref[:, sl[h]], v_ref[:, sl[h]], sh, h))
            st[h]["t"] = _split3(ts_ref[0, h])

        def s_sol(h):
            st[h]["sol"] = _dot3s(st[h]["t"], _split3(st[h]["rhs"]))

        def s_state(h):
            z = st[h]
            sol = z["sol"]
            kcd = sol[:, dh:]
            s = ss_ref[0, h]
            sb = s.astype(BF16)
            vnb = (sol[:, :dh] - _dot(kcd.astype(BF16), sb)).astype(BF16)
            ds_next = ds_ref[h]
            dsb = ds_next.astype(BF16)
            dob = do_ref[:, sl[h]].astype(BF16)
            z["dqg"] = _dot(dob, sb, nt_dims)
            ds = _dot(z["qg"].astype(BF16), dob, tn_dims)
            z["d_att"] = jnp.where(row >= col, _dot(dob, vnb, nt_dims), 0.0)
            dvn = _dot(z["att"].astype(BF16), dob, tn_dims) + _dot(z["kt"].astype(BF16), dsb)
            z["dkt"] = _dot(vnb, dsb, nt_dims)
            eg_l = jnp.exp(z["gam_l"])
            ds = ds + ds_next * eg_l
            z["dgam_l"] = tot(ds_next * s) * eg_l
            dvnb = dvn.astype(BF16)
            dkcd = -_dot(dvnb, sb, nt_dims)
            ds_ref[h] = ds - _dot(kcd.astype(BF16), dvnb, tn_dims)
            z["dsol"] = jnp.concatenate([dvn, dkcd], axis=1)

        def s_drhs(h):
            st[h]["drhs"] = _dot3s(st[h]["t"], _split3(st[h]["dsol"]), tn_dims)

        def s_dn(h):
            z = st[h]
            z["dn"] = jnp.where(row > col, -_dot3(z["drhs"], z["sol"], nt_dims), 0.0)

        def s_rest(h):
            z = st[h]
            k_, v_, kb, qb = z["k"], z["v"], z["kb"], z["qb"]
            beta, eg, dec, kk, qk, gam, gam_l = z["beta"], z["eg"], z["dec"], z["kk"], z["qk"], z["gam"], z["gam_l"]
            dn, d_att, dqg, dkt = z["dn"], z["d_att"], z["dqg"], z["dkt"]
            drv, drk = z["drhs"][:, :dh], z["drhs"][:, dh:]
            s_rkk = rs(drk * k_)
            dv_ref[:, sl[h]] = drv * beta
            dbeta = rs(drv * v_) + s_rkk * eg + rs(dn * kk * dec)
            dk = drk * (beta * eg)
            dgam = s_rkk * beta * eg
            dkk = (dn * beta * dec).astype(BF16)
            dd = dn * beta * kk + d_att * qk
            dqk = (d_att * dec).astype(BF16)
            dq_ref[:, sl[h]] = _dot(dqk, kb) + dqg * eg
            dk = dk + _dot(dqk, qb, tn_dims) + _dot(dkk, kb) + _dot(dkk, kb, tn_dims)
            w = dd * dec
            wh, wl = _split3(w)
            ones = jnp.ones((c, 128), BF16)
            col_sum = (_dot(wh, ones, tn_dims) + _dot(wl, ones, tn_dims))[:, 0:1]
            dgam = dgam + rs(w) - col_sum + rs(dqg * z["qg"]) - rs(dkt * z["kt"])
            dk_ref[:, sl[h]] = dk + dkt * jnp.exp(gam_l - gam)
            dgam_l = z["dgam_l"] + tot(dkt * z["kt"])
            rowc = lax.broadcasted_iota(jnp.int32, (c, 1), 0)
            dgam = dgam + jnp.where(rowc == c - 1, dgam_l, 0.0)
            dg = _split_dot((row <= col).astype(BF16), jnp.broadcast_to(dgam, (c, 128)))[:, 0:1]
            dgb_parts.append(jnp.where(lane == h, dg, 0.0) + jnp.where(lane == h + DN_HEADS, dbeta, 0.0))

        _emit_chains(heads, [s_common, s_sol, s_state, s_drhs, s_dn, s_rest], False)
        dgb = dgb_parts[0]
        for part in dgb_parts[1:]:
            dgb = dgb + part
        dgb_ref[...] = dgb

    blk = pl.BlockSpec((c, DN_WIDTH), lambda ci: (nc - 1 - ci, 0))
    sav = pl.BlockSpec((1, DN_HEADS, dh, dh), lambda ci: (nc - 1 - ci, 0, 0, 0))
    gspec = pl.BlockSpec((c, 128), lambda ci: (nc - 1 - ci, 0))
    return _pallas(
        body, name=name, grid=(nc,),
        in_specs=[blk, blk, blk, gspec, pl.BlockSpec((16, c), lambda ci: (0, nc - 1 - ci)), sav, sav, blk],
        out_specs=[blk, blk, blk, gspec],
        out_shape=[jax.ShapeDtypeStruct((r, DN_WIDTH), F32)] * 3 + [jax.ShapeDtypeStruct((r, 128), F32)],
        scratch_shapes=[pltpu.VMEM((DN_HEADS, dh, dh), F32)],
        compiler_params=_cparams(("arbitrary",)),
    )(q, k, v, gb, gbt, ssave, tsave, do)


def _dn_post_fwd(o, proj, g, name):
    r = o.shape[0]

    def body(o_ref, z_ref, g_ref, y_ref):
        g_ = g_ref[...]
        for hd in range(DN_HEADS):
            sl = slice(hd * 128, (hd + 1) * 128)
            sz, _ = _silu(z_ref[:, sl])
            y_ref[:, sl] = (_rms(o_ref[:, sl], g_) * sz).astype(BF16)

    return _pallas(body, name=name, grid=(r // ROW_TILE,),
                   in_specs=[_row_spec(DN_WIDTH), pl.BlockSpec((ROW_TILE, DN_WIDTH), lambda i: (i, 3)), _vec_spec(128)],
                   out_specs=_row_spec(DN_WIDTH), out_shape=jax.ShapeDtypeStruct((r, DN_WIDTH), BF16),
                   compiler_params=_cparams(("parallel",)))(o, proj, g)


def _dn_post_bwd(o, proj, g, dy, name):
    r = o.shape[0]

    def body(o_ref, z_ref, g_ref, dy_ref, do_ref, dz_ref, dg_ref):
        @pl.when(pl.program_id(0) == 0)
        def _():
            dg_ref[...] = jnp.zeros_like(dg_ref)

        g_ = g_ref[...]
        for hd in range(DN_HEADS):
            sl = slice(hd * 128, (hd + 1) * 128)
            z_ = z_ref[:, sl]
            sz, sg = _silu(z_)
            dy_ = dy_ref[:, sl]
            o_ = o_ref[:, sl]
            dz_ref[:, sl] = (dy_ * _rms(o_, g_) * (sg * (1.0 + z_ * (1.0 - sg)))).astype(BF16)
            dx, dg = _rms_bwd(o_, g_, dy_ * sz)
            do_ref[:, sl] = dx
            dg_ref[...] += dg

    return _pallas(body, name=name, grid=(r // ROW_TILE,),
                   in_specs=[_row_spec(DN_WIDTH), pl.BlockSpec((ROW_TILE, DN_WIDTH), lambda i: (i, 3)), _vec_spec(128),
                             _row_spec(DN_WIDTH)],
                   out_specs=[_row_spec(DN_WIDTH), pl.BlockSpec((ROW_TILE, DN_WIDTH), lambda i: (i, 3)), _vec_spec(128)],
                   out_shape=[jax.ShapeDtypeStruct((r, DN_WIDTH), F32), jax.ShapeDtypeStruct((r, 4 * DN_WIDTH), BF16),
                              jax.ShapeDtypeStruct((1, 128), F32)],
                   compiler_params=_cparams(("arbitrary",)))(o, proj, g, dy)


def _exchange(arrays, scatter, name):
    n = len(arrays)

    def body(*refs):
        copies = _exchange_copies(refs[:n], refs[n:2 * n], scatter, *refs[2 * n:])
        for cp in copies:
            cp.start()
        for cp in copies:
            cp.wait()

    hbm = pl.BlockSpec(memory_space=pl.ANY)
    return _pallas(
        body, name=name, in_specs=[hbm] * n, out_specs=[hbm] * n, out_shape=_exchange_shapes(arrays, scatter),
        scratch_shapes=_exchange_sems(n),
    )(*arrays)


def _exchange_shapes(arrays, scatter):
    return [jax.ShapeDtypeStruct((N_DEV,) + (a.shape[1:] if sc else a.shape), a.dtype) for a, sc in zip(arrays, scatter)]


def _exchange_sems(n):
    return [pltpu.SemaphoreType.DMA((n * N_DEV,)), pltpu.SemaphoreType.DMA((n * N_DEV,)), pltpu.SemaphoreType.DMA((n,))]


def _exchange_copies(in_refs, out_refs, scatter, send_sems, recv_sems, local_sems):
    mx, my, mc = lax.axis_index("x"), lax.axis_index("y"), lax.axis_index("c")
    me = 4 * mx + 2 * my + mc
    copies = []
    for a in range(len(in_refs)):
        src_own = in_refs[a].at[me] if scatter[a] else in_refs[a]
        copies.append(pltpu.make_async_copy(src_own, out_refs[a].at[me], local_sems.at[a]))
        for kbits in range(1, N_DEV):
            px = lax.rem(mx + ((kbits >> 2) & 1), 2)
            py = lax.rem(my + ((kbits >> 1) & 1), 2)
            pc = lax.rem(mc + (kbits & 1), 2)
            src = in_refs[a].at[4 * px + 2 * py + pc] if scatter[a] else in_refs[a]
            copies.append(pltpu.make_async_remote_copy(
                src_ref=src, dst_ref=out_refs[a].at[me],
                send_sem=send_sems.at[a * N_DEV + kbits], recv_sem=recv_sems.at[a * N_DEV + kbits],
                device_id=(px, py, pc), device_id_type=pl.DeviceIdType.MESH))
    return copies


def _adamw(gstack, w, m, v, name):
    a, b = w.shape
    ta = a
    for t in (1024, 512, 256, 128, 64, 32, 16, 8):
        if a % t == 0 and N_DEV * t * b * 4 <= 4 * 1024 * 1024:
            ta = t
            break
    c1 = 1.0 / (1.0 - ADAM_B1 ** ADAM_STEP)
    c2 = 1.0 / (1.0 - ADAM_B2 ** ADAM_STEP)

    def body(g_ref, w_ref, m_ref, v_ref, og_ref, od_ref, om_ref, ov_ref):
        g = g_ref[0].astype(F32)
        for s in range(1, N_DEV):
            g = g + g_ref[s].astype(F32)
        m_new = ADAM_B1 * m_ref[...] + (1.0 - ADAM_B1) * g
        v_new = ADAM_B2 * v_ref[...] + (1.0 - ADAM_B2) * (g * g)
        og_ref[...] = g
        om_ref[...] = m_new
        ov_ref[...] = v_new
        od_ref[...] = -ADAM_LR * ((m_new * c1) / (jnp.sqrt(v_new * c2) + ADAM_EPS) + ADAM_WD * w_ref[...])

    spec = pl.BlockSpec((ta, b), lambda i: (i, 0))
    return _pallas(
        body, name=name, grid=(a // ta,),
        in_specs=[pl.BlockSpec((N_DEV, ta, b), lambda i: (0, i, 0)), spec, spec, spec],
        out_specs=[spec] * 4, out_shape=[jax.ShapeDtypeStruct((a, b), F32)] * 4,
        compiler_params=_cparams(("parallel",)),
    )(gstack, w, m, v)


_WEIGHTS = ['meta_tokens', 'pre_mix_norm', 'post_mix_norm', 'pre_mlp_norm', 'post_mlp_norm', 'mlp_w1', 'mlp_w2',
            'w_in_even', 'w_out_even', 'sb_out_norm', 's5_lambda_re', 's5_lambda_im', 's5_log_dt', 's5_b_re', 's5_b_im',
            's5_c_re', 's5_c_im', 's5_d', 's5_w_glu', 's5_b_glu', 's5_out_norm', 'w_in_odd', 'dn_conv_w', 'dn_a_log',
            'dn_dt_bias', 'dn_out_norm', 'w_out_odd']
_SHARDED = ['meta_tokens', 'mlp_w1', 'mlp_w2', 'w_in_even', 'w_out_even', 's5_w_glu', 'w_in_odd', 'dn_conv_w', 'w_out_odd']
_SMALL = [n for n in _WEIGHTS if n not in _SHARDED]
_GATHER_FIRST = ['meta_tokens', 'w_in_even', 's5_w_glu', 'w_out_even']
_GATHER_LATE = [n for n in _SHARDED if n not in _GATHER_FIRST]
_REDUCE_EARLY = ['mlp_w1', 'mlp_w2', 'w_in_odd', 'dn_conv_w', 'w_out_odd', 'w_out_even']


def _view2d(name, a):
    return a.reshape(-1, a.shape[-1])


def _unshard(name, g):
    if name == 'mlp_w1':
        return g.reshape(N_DEV, 2, D_MODEL, -1).transpose(1, 2, 0, 3).reshape(2, D_MODEL, D_FF)
    if name == 'mlp_w2':
        return g.reshape(N_DEV, 2, -1, D_MODEL).transpose(1, 0, 2, 3).reshape(2, D_FF, D_MODEL)
    if name in ('w_in_even', 'w_in_odd', 'dn_conv_w', 'meta_tokens'):
        return g.transpose(1, 0, 2).reshape(g.shape[1], -1)
    return g.reshape(-1, g.shape[-1])


def _to_blocks(name, full):
    if name == 'mlp_w1':
        return full.reshape(2, D_MODEL, N_DEV, -1).transpose(2, 0, 1, 3).reshape(N_DEV, 2 * D_MODEL, -1)
    if name == 'mlp_w2':
        return full.reshape(2, N_DEV, -1, D_MODEL).transpose(1, 0, 2, 3).reshape(N_DEV, -1, D_MODEL)
    if name in ('w_in_even', 'w_in_odd', 'dn_conv_w', 'meta_tokens'):
        return full.reshape(full.shape[0], N_DEV, -1).transpose(1, 0, 2)
    return full.reshape(N_DEV, -1, full.shape[-1])


def _pack(parts):
    rows = []
    for p in parts:
        flat = p.reshape(-1)
        rows.append(jnp.pad(flat, (0, (-flat.shape[0]) % 128)).reshape(-1, 128))
    return jnp.concatenate(rows, axis=0)


def _unpack(packed, like):
    out, at = [], 0
    for p in like:
        n = math.prod(p.shape)
        nrow = -(-n // 128)
        out.append(packed[at:at + nrow].reshape(-1)[:n].reshape(p.shape))
        at += nrow
    return out


def _lane_vec(x, width=128):
    flat = x.reshape(-1)
    return jnp.pad(flat, (0, width - flat.shape[0])).reshape(1, width)


def kernel(x, meta_tokens, pre_mix_norm, post_mix_norm, pre_mlp_norm, post_mlp_norm, mlp_w1, mlp_w2, w_in_even, w_out_even, sb_out_norm, s5_lambda_re, s5_lambda_im, s5_log_dt, s5_b_re, s5_b_im, s5_c_re, s5_c_im, s5_d, s5_w_glu, s5_b_glu, s5_out_norm, w_in_odd, dn_conv_w, dn_a_log, dn_dt_bias, dn_out_norm, w_out_odd, loss_target, m_meta_tokens, m_pre_mix_norm, m_post_mix_norm, m_pre_mlp_norm, m_post_mlp_norm, m_mlp_w1, m_mlp_w2, m_w_in_even, m_w_out_even, m_sb_out_norm, m_s5_lambda_re, m_s5_lambda_im, m_s5_log_dt, m_s5_b_re, m_s5_b_im, m_s5_c_re, m_s5_c_im, m_s5_d, m_s5_w_glu, m_s5_b_glu, m_s5_out_norm, m_w_in_odd, m_dn_conv_w, m_dn_a_log, m_dn_dt_bias, m_dn_out_norm, m_w_out_odd, v_meta_tokens, v_pre_mix_norm, v_post_mix_norm, v_pre_mlp_norm, v_post_mlp_norm, v_mlp_w1, v_mlp_w2, v_w_in_even, v_w_out_even, v_sb_out_norm, v_s5_lambda_re, v_s5_lambda_im, v_s5_log_dt, v_s5_b_re, v_s5_b_im, v_s5_c_re, v_s5_c_im, v_s5_d, v_s5_w_glu, v_s5_b_glu, v_s5_out_norm, v_w_in_odd, v_dn_conv_w, v_dn_a_log, v_dn_dt_bias, v_dn_out_norm, v_w_out_odd):
    given = dict(locals())
    w = {n: given[n] for n in _WEIGHTS}
    mom_m = {n: given["m_" + n] for n in _WEIGHTS}
    mom_v = {n: given["v_" + n] for n in _WEIGHTS}

    seq = x.shape[1]
    assert x.shape[0] == 1 and seq % ROW_TILE == 0
    r = seq + ROW_TILE
    pad = ROW_TILE - N_META

    wire = {n: (F32 if n in ('dn_conv_w', 'meta_tokens') else BF16) for n in _SHARDED}
    shard_wire = lambda n: _view2d(n, w[n]).astype(wire[n])
    gathered = _exchange([shard_wire(n) for n in _GATHER_FIRST], [False] * len(_GATHER_FIRST), "gather_first")
    full = {n: _unshard(n, g_) for n, g_ in zip(_GATHER_FIRST, gathered)}
    w_ie, w_oe, w_glu = full['w_in_even'], full['w_out_even'], full['s5_w_glu']
    row = lambda v_: v_.reshape(1, -1)

    hs0 = jnp.concatenate([jnp.zeros((pad, D_MODEL), F32), full['meta_tokens'], x[0]], axis=0)
    hn0 = _norm_pre(hs0, row(pre_mix_norm[0]), "pre_mix_0")
    qkv = _mm_fwd(hn0, w_ie[:, :3 * SB_WIDTH], "in_even_qkv", out_dtypes=(BF16,))
    u = _mm_fwd(hn0, w_ie[:, 3 * SB_WIDTH:], "in_even_u")
    q, k, v = qkv[:, :SB_WIDTH], qkv[:, SB_WIDTH:2 * SB_WIDTH], qkv[:, 2 * SB_WIDTH:]
    nb = r // ATT_BLK
    blocks_t = lambda t_: t_.reshape(nb, ATT_BLK, 4, 128).transpose(2, 0, 3, 1)
    o_sb, ssave, gathered = _sb_fwd(q, k, blocks_t(v), pad, "sb_fwd",
                                    ride=([shard_wire(n) for n in _GATHER_LATE], [False] * len(_GATHER_LATE)))
    full.update({n: _unshard(n, g_) for n, g_ in zip(_GATHER_LATE, gathered)})
    w1, w2, w_oo, conv_w = full['mlp_w1'], full['mlp_w2'], full['w_out_odd'], full['dn_conv_w']
    w_io = full['w_in_odd'][:, :4 * DN_WIDTH]
    w_ab = jnp.pad(full['w_in_odd'][:, 4 * DN_WIDTH:], ((0, 0), (0, 128 - 2 * DN_HEADS)))
    on_sb = _norm_pre(o_sb, row(sb_out_norm[0]), "sb_out_norm")

    lam_re, lam_im, logdt, btr, bti, ctr, cti, s5_mask = _s5_expand(
        s5_lambda_re[0], s5_lambda_im[0], s5_log_dt[0], s5_b_re[0], s5_b_im[0], s5_c_re[0], s5_c_im[0])
    a_re, a_im, bbr, bbi = _s5_prep(lam_re, lam_im, logdt, btr, bti, "s5_prep")
    s5_wb = jnp.stack([_s5_block_diag_b(bbr, s5_mask), _s5_block_diag_b(bbi, s5_mask)]).astype(BF16)
    s5_wc = jnp.stack([_s5_block_diag_c(ctr, s5_mask), _s5_block_diag_c(cti, s5_mask)]).astype(BF16)
    s5_a = jnp.stack([a_re, a_im])
    s5_args = (s5_wb, s5_a, s5_wc, row(s5_d[0]), w_glu, row(s5_b_glu[0]), row(s5_out_norm[0]))
    y_s5, on_s5, xstart = _s5_fwd(u, *s5_args, "s5_fwd")

    merged = jnp.concatenate([on_sb, on_s5], axis=1)
    mix0, hs1, hn1 = _mm_norm_fwd(merged, w_oe, hs0, row(post_mix_norm[0]), g_pre=row(pre_mlp_norm[0]), name="out_even")
    relu2 = lambda acc: (jnp.square(jnp.maximum(acc, 0.0)), jnp.maximum(acc, 0.0))
    r0, ra0 = _mm_fwd(hn1, w1[0], "mlp_up_0", out_dtypes=(BF16, BF16), epilogue=relu2)
    m0, hs2, hn2 = _mm_norm_fwd(r0, w2[0], hs1, row(post_mlp_norm[0]), g_pre=row(pre_mix_norm[1]), name="mlp_down_0")

    proj = _mm_fwd(hn2, w_io, "in_odd")
    ab = _mm_fwd(hn2, w_ab, "in_odd_gates")
    alog, dtb = _lane_vec(dn_a_log[0]), _lane_vec(dn_dt_bias[0])
    qd, kd, vd, gb = _dn_pre_fwd(proj, ab, conv_w, alog, dtb, pad, "dn_pre")
    gbt = gb[:, :2 * DN_HEADS].T
    o_dn, s_dn, t_dn = _dn_fwd(qd, kd, vd, gb, gbt, "dn_fwd")
    on_dn = _dn_post_fwd(o_dn, proj, row(dn_out_norm[0]), "dn_post")
    mix1, hs3, hn3 = _mm_norm_fwd(on_dn, w_oo, hs2, row(post_mix_norm[1]), g_pre=row(pre_mlp_norm[1]), name="out_odd")
    r1, ra1 = _mm_fwd(hn3, w1[1], "mlp_up_1", out_dtypes=(BF16, BF16), epilogue=relu2)
    dhs, dm1, dg_post_mlp1, loss_part = _mm_norm_fwd(r1, w2[1], hs3, row(post_mlp_norm[1]),
                                                     loss=(loss_target[0], pad + N_META), name="mlp_down_1_loss")
    loss = lax.psum(loss_part, ("x", "y", "c"))

    g = {}
    drelu2 = lambda acc, ra: (acc * (2.0 * ra.astype(F32)),)

    def mlp_bwd(layer, hn, rr, ra, dm):
        dw2 = _mm_wgrad(rr, dm, f"mlp_down_{layer}_wgrad")
        da = _mm_dgrad(dm, w2[layer], f"mlp_down_{layer}_dgrad", out_dtypes=(BF16,), extras=(ra,), epilogue=drelu2)
        dw1 = _mm_wgrad(hn, da, f"mlp_up_{layer}_wgrad")
        return dw1, dw2, da

    dw1_1, dw2_1, da1 = mlp_bwd(1, hn3, r1, ra1, dm1)
    dhs, dmix1, dg_pre_mlp1, dg_post_mix1 = _dgrad_norm_bwd(
        da1, w1[1], dhs, hs3, row(pre_mlp_norm[1]), post=(mix1, row(post_mix_norm[1])), pad=pad, name="post_mix_1_bwd")

    g['w_out_odd'] = _mm_wgrad(on_dn, dmix1, "out_odd_wgrad")
    d_on_dn = _mm_dgrad(dmix1, w_oo, "out_odd_dgrad")
    do_dn, dproj, dg_dn = _dn_post_bwd(o_dn, proj, row(dn_out_norm[0]), d_on_dn, "dn_post_bwd")
    dqd, dkd, dvd, dgb = _dn_bwd(qd, kd, vd, gb, gbt, s_dn, t_dn, do_dn, "dn_bwd")
    dco, dab, d_alog, d_dtb = _dn_pre_bwd(proj, conv_w, dqd, dkd, dvd, dgb, ab, alog, dtb, pad, "dn_pre_bwd")
    dproj, d_conv = _dn_conv_bwd(dco, proj, conv_w, dproj, "dn_conv_bwd")
    g['w_in_odd'] = jnp.concatenate([_mm_wgrad(hn2, dproj, "in_odd_wgrad"),
                                     _mm_wgrad(hn2, dab, "in_odd_gates_wgrad")[:, :2 * DN_HEADS]], axis=1)
    dhn2_gates = _mm_dgrad(dab, w_ab, "in_odd_gates_dgrad")
    g['dn_conv_w'] = d_conv[:DN_CONV]
    g['dn_a_log'], g['dn_dt_bias'], g['dn_out_norm'] = d_alog[0, :DN_HEADS], d_dtb[0, :DN_HEADS], dg_dn[0]

    dhs, dm0, dg_pre_mix1, dg_post_mlp0 = _dgrad_norm_bwd(
        dproj, w_io, dhs, hs2, row(pre_mix_norm[1]), post=(m0, row(post_mlp_norm[0])), add=dhn2_gates, pad=pad,
        name="post_mlp_0_bwd")
    dw1_0, dw2_0, da0 = mlp_bwd(0, hn1, r0, ra0, dm0)
    dhs, dmix0, dg_pre_mlp0, dg_post_mix0 = _dgrad_norm_bwd(
        da0, w1[0], dhs, hs1, row(pre_mlp_norm[0]), post=(mix0, row(post_mix_norm[0])), pad=pad, name="post_mix_0_bwd")

    g['w_out_even'] = _mm_wgrad(merged, dmix0, "out_even_wgrad")
    dmerged = _mm_dgrad(dmix0, w_oe, "out_even_dgrad")
    _, do_sb, _, dg_sb = _norm_bwd(dmerged, post=(o_sb, row(sb_out_norm[0])), pad=pad, dm_dtype=F32,
                                   dhs_cols=(SB_WIDTH, 0), name="sb_out_norm_bwd")
    dq, dk4, dv4 = _sb_bwd(q, k, v, blocks_t(k), ssave, do_sb, pad, "sb_bwd")
    unheads = lambda t_: t_.transpose(1, 0, 2).reshape(r, SB_WIDTH)
    g['mlp_w1'] = jnp.stack([dw1_0, dw1_1])
    g['mlp_w2'] = jnp.stack([dw2_0, dw2_1])
    grad_wire = lambda n: _to_blocks(n, g[n].reshape(full[n].shape)).astype(wire[n])
    du, d_a, d_d, d_bglu, dg_s5, d_wb, d_wc, g['s5_w_glu'], reduced = _s5_bwd(
        u, y_s5, dmerged, xstart, *s5_args, "s5_bwd", don_block=1,
        ride=([grad_wire(n) for n in _REDUCE_EARLY], [True] * len(_REDUCE_EARLY)))
    stacks = dict(zip(_REDUCE_EARLY, reduced))
    g_lr, g_li, g_dt, g_btr, g_bti = _s5_prep_bwd(
        lam_re, lam_im, logdt, btr, bti, d_a[0], d_a[1],
        _s5_diag_of_b(d_wb[0], s5_mask), _s5_diag_of_b(d_wb[1], s5_mask), "s5_prep_bwd")
    gg, nn, pp = S5_GROUPS, S5_STATE, S5_GROUP
    g['s5_lambda_re'], g['s5_lambda_im'] = g_lr.reshape(gg, nn), g_li.reshape(gg, nn)
    g['s5_log_dt'] = g_dt.reshape(gg, nn)[:, 0]
    g['s5_b_re'], g['s5_b_im'] = g_btr.T.reshape(gg, nn, pp), g_bti.T.reshape(gg, nn, pp)
    g['s5_c_re'] = _s5_diag_of_c(d_wc[0], s5_mask).reshape(gg, nn, pp).transpose(0, 2, 1)
    g['s5_c_im'] = _s5_diag_of_c(d_wc[1], s5_mask).reshape(gg, nn, pp).transpose(0, 2, 1)
    g['s5_d'], g['s5_b_glu'], g['s5_out_norm'], g['sb_out_norm'] = d_d[0], d_bglu[0], dg_s5[0], dg_sb[0]
    dqkvu = jnp.concatenate([dq, unheads(dk4), unheads(dv4), du], axis=1).astype(BF16)
    g['w_in_even'] = _mm_wgrad(hn0, dqkvu, "in_even_wgrad")
    dhs, _, dg_pre_mix0, _ = _dgrad_norm_bwd(dqkvu, w_ie, dhs, hs0, row(pre_mix_norm[0]), pad=pad, name="pre_mix_0_bwd")

    g['meta_tokens'] = dhs[pad:pad + N_META]
    g['pre_mix_norm'] = jnp.concatenate([dg_pre_mix0, dg_pre_mix1], axis=0)
    g['post_mix_norm'] = jnp.concatenate([dg_post_mix0, dg_post_mix1], axis=0)
    g['pre_mlp_norm'] = jnp.concatenate([dg_pre_mlp0, dg_pre_mlp1], axis=0)
    g['post_mlp_norm'] = jnp.concatenate([dg_post_mlp0, dg_post_mlp1], axis=0)
    grad_x = dhs[pad + N_META:][None]

    small_like = [w[n] for n in _SMALL]
    last = [n for n in _SHARDED if n not in _REDUCE_EARLY]
    partial = [grad_wire(n) for n in last] + [_pack([g[n].reshape(w[n].shape) for n in _SMALL])]
    reduced = _exchange(partial, [True] * len(last) + [False], "reduce_last")
    stacks.update(zip(last, reduced[:-1]))
    grads, deltas, new_m, new_v = {}, {}, {}, {}
    for n in _SHARDED:
        outs = _adamw(stacks[n], _view2d(n, w[n]), _view2d(n, mom_m[n]), _view2d(n, mom_v[n]), f"adamw_{n}")
        grads[n], deltas[n], new_m[n], new_v[n] = (o.reshape(w[n].shape) for o in outs)
    outs = _adamw(reduced[-1], _pack(small_like), _pack([mom_m[n] for n in _SMALL]), _pack([mom_v[n] for n in _SMALL]),
                  "adamw_small")
    for dst, o in zip((grads, deltas, new_m, new_v), outs):
        for n, part in zip(_SMALL, _unpack(o, small_like)):
            dst[n] = part
    return (loss, grad_x, *[grads[n] for n in _WEIGHTS], *[deltas[n] for n in _WEIGHTS],
            *[new_m[n] for n in _WEIGHTS], *[new_v[n] for n in _WEIGHTS])
```

```python
import math

import jax
import jax.numpy as jnp
from jax import lax
from jax.experimental import pallas as pl
from jax.experimental.pallas import tpu as pltpu

F32 = jnp.float32
BF16 = jnp.bfloat16

D_MODEL = 1024
N_META = 16
SB_HEAD_DIM = 64
SB_WIDTH = 512
S5_WIDTH = 512
S5_GROUP = 16
S5_GROUPS = 32
S5_STATE = 64
S5_NS = S5_GROUPS * S5_STATE
DN_HEAD_DIM = 128
DN_HEADS = 8
DN_WIDTH = 1024
DN_CONV = 4
D_FF = 4096
EPS = 1e-6
N_DEV = 8

ADAM_LR = 0.001
ADAM_B1 = 0.9
ADAM_B2 = 0.999
ADAM_EPS = 1e-08
ADAM_WD = 0.01
ADAM_STEP = 10

ROW_TILE = 512
ATT_BLK = 256
SB_BLOCKS_PER_TRIP = 3
SB_LOG_ZERO = -106.0
SB_FWD_SKEW = False
SB_BWD_SKEW = True
DN_CHUNK = 128
DN_SUB = 16
S5_TILE = 128
S5_CHUNKS = 4
VMEM_LIMIT = 56 * 1024 * 1024

_HIGH = lax.Precision.HIGHEST


def _pallas(body, **kw):
    return pl.pallas_call(body, **kw)


def _cparams(sem):
    return pltpu.CompilerParams(dimension_semantics=sem, vmem_limit_bytes=VMEM_LIMIT)


def _dot(a, b, dims=((1,), (0,))):
    return lax.dot_general(a, b, (dims, ((), ())), preferred_element_type=F32)


def _dot_hi(a, b):
    return lax.dot_general(a, b, (((1,), (0,)), ((), ())), preferred_element_type=F32, precision=_HIGH)


def _split_dot(m_bf16, x):
    hi = x.astype(BF16)
    lo = (x - hi.astype(F32)).astype(BF16)
    return _dot(m_bf16, hi) + _dot(m_bf16, lo)


def _matmul(a, b, *, ta=False, tb=False, tm, tn, tk, name, out_dtypes=(F32,), extras=(), epilogue=None):
    m, k = (a.shape[1], a.shape[0]) if ta else a.shape
    n = b.shape[0] if tb else b.shape[1]
    assert (b.shape[1] if tb else b.shape[0]) == k
    assert m % tm == 0 and n % tn == 0 and k % tk == 0, (name, m, n, k, tm, tn, tk)
    nk = k // tk
    n_ex = len(extras)
    n_out = len(out_dtypes)
    dims = ((0 if ta else 1,), (1 if tb else 0,))

    def finish(acc, ex_refs, o_refs):
        outs = (acc,) if epilogue is None else epilogue(acc, *[r[...] for r in ex_refs])
        for o_ref, o in zip(o_refs, outs):
            o_ref[...] = o.astype(o_ref.dtype)

    def body(*refs):
        a_ref, b_ref = refs[0], refs[1]
        ex_refs = refs[2:2 + n_ex]
        o_refs = refs[2 + n_ex:2 + n_ex + n_out]
        prod = _dot(a_ref[...].astype(BF16), b_ref[...].astype(BF16), dims)
        if nk == 1:
            finish(prod, ex_refs, o_refs)
            return
        acc_ref = refs[-1]
        kk = pl.program_id(2)

        @pl.when(kk == 0)
        def _():
            acc_ref[...] = prod

        @pl.when(kk > 0)
        def _():
            acc_ref[...] += prod

        @pl.when(kk == nk - 1)
        def _():
            finish(acc_ref[...], ex_refs, o_refs)

    a_spec = pl.BlockSpec((tk, tm), lambda j, i, kk: (kk, i)) if ta else pl.BlockSpec((tm, tk), lambda j, i, kk: (i, kk))
    b_spec = pl.BlockSpec((tn, tk), lambda j, i, kk: (j, kk)) if tb else pl.BlockSpec((tk, tn), lambda j, i, kk: (kk, j))
    o_spec = pl.BlockSpec((tm, tn), lambda j, i, kk: (i, j))
    outs = _pallas(
        body, name=name,
        grid=(n // tn, m // tm, nk),
        in_specs=[a_spec, b_spec] + [o_spec] * n_ex,
        out_specs=[o_spec] * n_out,
        out_shape=[jax.ShapeDtypeStruct((m, n), dt) for dt in out_dtypes],
        scratch_shapes=[] if nk == 1 else [pltpu.VMEM((tm, tn), F32)],
        compiler_params=_cparams(("parallel", "parallel", "arbitrary")),
    )(a, b, *extras)
    return outs[0] if n_out == 1 else outs


def _tile(n, cap):
    best = 128
    for t in range(128, min(n, cap) + 1, 128):
        if n % t == 0:
            best = t
    assert n % best == 0, n
    return best


MM_K_CAP = 4096
WGRAD_ROWS = 1536


MM_LHS_TILE_BYTES = 6 * 1024 * 1024


def _row_tile(x, depth):
    tall = 3 * ROW_TILE
    fits = tall * depth * x.dtype.itemsize <= MM_LHS_TILE_BYTES
    return tall if (x.shape[0] % tall == 0 and fits) else ROW_TILE


def _mm_fwd(x, w, name, **kw):
    k, n = w.shape
    tk = _tile(k, MM_K_CAP)
    return _matmul(x, w, tm=_row_tile(x, tk), tn=_tile(n, 1024), tk=tk, name=name, **kw)


def _mm_dgrad(dy, w, name, **kw):
    k, n = w.shape
    tk = _tile(n, MM_K_CAP)
    return _matmul(dy, w, tb=True, tm=_row_tile(dy, tk), tn=_tile(k, 1024), tk=tk, name=name, **kw)


def _mm_wgrad(x, dy, name):
    k, n = x.shape[1], dy.shape[1]
    rows = x.shape[0]
    return _matmul(x, dy, ta=True, tm=_tile(k, 512), tn=_tile(n, 1024),
                   tk=WGRAD_ROWS if rows % WGRAD_ROWS == 0 else ROW_TILE, name=name)


def _rms(x, g):
    r = lax.rsqrt(jnp.mean(x * x, axis=-1, keepdims=True) + EPS)
    return x * r * g


def _rms_bwd(x, g, dy):
    r = lax.rsqrt(jnp.mean(x * x, axis=-1, keepdims=True) + EPS)
    xh = x * r
    dxh = dy * g
    dx = r * (dxh - xh * jnp.mean(dxh * xh, axis=-1, keepdims=True))
    dg = jnp.sum(dy * xh, axis=0, keepdims=True)
    return dx, dg


def _row_spec(width, tile=ROW_TILE):
    return pl.BlockSpec((tile, width), lambda i: (i, 0))


def _vec_spec(width):
    return pl.BlockSpec((1, width), lambda i: (0, 0))


def _norm_pre(hs, g, name):
    r, d = hs.shape

    def body(x_ref, g_ref, o_ref):
        o_ref[...] = _rms(x_ref[...], g_ref[...]).astype(BF16)

    return _pallas(body, name=name, grid=(r // ROW_TILE,), in_specs=[_row_spec(d), _vec_spec(d)],
                   out_specs=_row_spec(d), out_shape=jax.ShapeDtypeStruct((r, d), BF16),
                   compiler_params=_cparams(("parallel",)))(hs, g)


def _mm_norm_fwd(a, w, hs, g_post, *, g_pre=None, loss=None, name):
    k, d = w.shape
    r = a.shape[0]
    assert k <= MM_K_CAP and d == hs.shape[1]
    t = ROW_TILE // 2
    nt = r // t

    def body(*refs):
        a_ref, w_ref, hs_ref, gp_ref = refs[:4]
        i = pl.program_id(0)
        m = _dot(a_ref[...].astype(BF16), w_ref[...].astype(BF16))
        gp = gp_ref[...]
        new = hs_ref[...] + _rms(m, gp)
        if loss is None:
            gn_ref, m_ref, o_ref, hn_ref = refs[4:]
            m_ref[...] = m
            o_ref[...] = new
            hn_ref[...] = _rms(new, gn_ref[...]).astype(BF16)
        else:
            t_ref, dhs_ref, dm_ref, dgp_ref, loss_ref = refs[4:]
            live = (i * t + lax.broadcasted_iota(jnp.int32, (t, 1), 0)) >= loss[1]
            diff = jnp.where(live, new - t_ref[...], 0.0)
            dhs = diff * (1.0 / d)
            dhs_ref[...] = dhs
            loss_ref[...] = jnp.full((8, 128), 0.5 / d * jnp.sum(diff * diff), F32)
            dm, dg = _rms_bwd(m, gp, dhs)
            dm_ref[...] = dm.astype(BF16)

            @pl.when(i == 0)
            def _():
                dgp_ref[...] = jnp.zeros_like(dgp_ref)
            dgp_ref[...] += dg

    common_in = [_row_spec(k, t), pl.BlockSpec((k, d), lambda i: (0, 0)), _row_spec(d, t), _vec_spec(d)]
    if loss is None:
        return _pallas(
            body, name=name, grid=(nt,), in_specs=common_in + [_vec_spec(d)],
            out_specs=[_row_spec(d, t)] * 3,
            out_shape=[jax.ShapeDtypeStruct((r, d), F32), jax.ShapeDtypeStruct((r, d), F32), jax.ShapeDtypeStruct((r, d), BF16)],
            compiler_params=_cparams(("parallel",)))(a, w, hs, g_post, g_pre)
    target, first_row = loss
    assert first_row % t == 0
    dhs, dm, dgp, parts = _pallas(
        body, name=name, grid=(nt,),
        in_specs=common_in + [pl.BlockSpec((t, d), lambda i: (jnp.maximum(i - first_row // t, 0), 0))],
        out_specs=[_row_spec(d, t), _row_spec(d, t), _vec_spec(d), pl.BlockSpec((8, 128), lambda i: (i, 0))],
        out_shape=[jax.ShapeDtypeStruct((r, d), F32), jax.ShapeDtypeStruct((r, d), BF16), jax.ShapeDtypeStruct((1, d), F32),
                   jax.ShapeDtypeStruct((nt * 8, 128), F32)],
        compiler_params=_cparams(("arbitrary",)))(a, w, hs, g_post, target)
    return dhs, dm, dgp, jnp.sum(parts[::8, 0])


def _norm_bwd(dhs, *, pre=None, post=None, pad=0, dm_dtype=BF16, dhs_cols=None, name):
    r = dhs.shape[0]
    d = dhs.shape[1] if dhs_cols is None else dhs_cols[0]
    has_pre, has_post = pre is not None, post is not None

    def body(*refs):
        it = iter(refs)
        dhs_ref = next(it)
        if has_pre:
            hs_ref, gn_ref, dhn_ref = next(it), next(it), next(it)
        if has_post:
            m_ref, gp_ref = next(it), next(it)
        if has_pre:
            o_dhs, o_dgn = next(it), next(it)
        if has_post:
            o_dm, o_dgp = next(it), next(it)
        i = pl.program_id(0)
        live = (i * ROW_TILE + lax.broadcasted_iota(jnp.int32, (ROW_TILE, 1), 0)) >= pad
        cur = jnp.where(live, dhs_ref[...], 0.0)
        if has_pre:
            dx, dg = _rms_bwd(hs_ref[...], gn_ref[...], jnp.where(live, dhn_ref[...].astype(F32), 0.0))
            cur = cur + dx
            o_dhs[...] = cur

            @pl.when(i == 0)
            def _():
                o_dgn[...] = jnp.zeros_like(o_dgn)
            o_dgn[...] += dg
        if has_post:
            dm, dg = _rms_bwd(m_ref[...], gp_ref[...], cur)
            o_dm[...] = dm.astype(o_dm.dtype)

            @pl.when(i == 0)
            def _():
                o_dgp[...] = jnp.zeros_like(o_dgp)
            o_dgp[...] += dg

    dhs_spec = _row_spec(d) if dhs_cols is None else pl.BlockSpec((ROW_TILE, d), lambda i: (i, dhs_cols[1]))
    ins, in_specs, out_specs, out_shape = [dhs], [dhs_spec], [], []
    if has_pre:
        ins += list(pre)
        in_specs += [_row_spec(d), _vec_spec(d), _row_spec(d)]
        out_specs += [_row_spec(d), _vec_spec(d)]
        out_shape += [jax.ShapeDtypeStruct((r, d), F32), jax.ShapeDtypeStruct((1, d), F32)]
    if has_post:
        ins += list(post)
        in_specs += [_row_spec(d), _vec_spec(d)]
        out_specs += [_row_spec(d), _vec_spec(d)]
        out_shape += [jax.ShapeDtypeStruct((r, d), dm_dtype), jax.ShapeDtypeStruct((1, d), F32)]
    outs = list(_pallas(body, name=name, grid=(r // ROW_TILE,), in_specs=in_specs, out_specs=out_specs,
                        out_shape=out_shape, compiler_params=_cparams(("arbitrary",)))(*ins))
    dhs_new, dgn = (outs.pop(0), outs.pop(0)) if has_pre else (dhs, None)
    dm, dgp = (outs.pop(0), outs.pop(0)) if has_post else (None, None)
    return dhs_new, dm, dgn, dgp


def _dgrad_norm_bwd(dy, w, dhs, hs, g_pre, *, post=None, add=None, pad=0, name):
    d, n = w.shape
    r = dy.shape[0]
    assert n <= MM_K_CAP and d == dhs.shape[1]
    t = ROW_TILE // 2
    has_post, has_add = post is not None, add is not None
    dims = ((1,), (1,))

    def body(*refs):
        it = iter(refs)
        dy_ref, w_ref = next(it), next(it)
        add_ref = next(it) if has_add else None
        dhs_ref, hs_ref, gn_ref = next(it), next(it), next(it)
        if has_post:
            m_ref, gp_ref = next(it), next(it)
        o_dhs, o_dgn = next(it), next(it)
        if has_post:
            o_dm, o_dgp = next(it), next(it)
        i = pl.program_id(0)
        dhn = _dot(dy_ref[...].astype(BF16), w_ref[...].astype(BF16), dims)
        if has_add:
            dhn = dhn + add_ref[...]
        live = (i * t + lax.broadcasted_iota(jnp.int32, (t, 1), 0)) >= pad
        dx, dg = _rms_bwd(hs_ref[...], gn_ref[...], jnp.where(live, dhn, 0.0))
        cur = jnp.where(live, dhs_ref[...], 0.0) + dx
        o_dhs[...] = cur

        @pl.when(i == 0)
        def _():
            o_dgn[...] = jnp.zeros_like(o_dgn)
        o_dgn[...] += dg
        if has_post:
            dm, dg = _rms_bwd(m_ref[...], gp_ref[...], cur)
            o_dm[...] = dm.astype(BF16)

            @pl.when(i == 0)
            def _():
                o_dgp[...] = jnp.zeros_like(o_dgp)
            o_dgp[...] += dg

    ins = [dy, w] + ([add] if has_add else []) + [dhs, hs, g_pre] + (list(post) if has_post else [])
    in_specs = ([_row_spec(n, t), pl.BlockSpec((d, n), lambda i: (0, 0))] + ([_row_spec(d, t)] if has_add else [])
                + [_row_spec(d, t), _row_spec(d, t), _vec_spec(d)] + ([_row_spec(d, t), _vec_spec(d)] if has_post else []))
    out_specs = [_row_spec(d, t), _vec_spec(d)] + ([_row_spec(d, t), _vec_spec(d)] if has_post else [])
    out_shape = [jax.ShapeDtypeStruct((r, d), F32), jax.ShapeDtypeStruct((1, d), F32)]
    if has_post:
        out_shape += [jax.ShapeDtypeStruct((r, d), BF16), jax.ShapeDtypeStruct((1, d), F32)]
    outs = list(_pallas(body, name=name, grid=(r // t,), in_specs=in_specs, out_specs=out_specs,
                        out_shape=out_shape, compiler_params=_cparams(("arbitrary",)))(*ins))
    return (outs[0], outs[2], outs[1], outs[3]) if has_post else (outs[0], None, outs[1], None)


def _softplus(z):
    return jnp.maximum(z, 0.0) + jnp.log(1.0 + jnp.exp(-jnp.abs(z)))


def _sb_consts(t):
    row = lax.broadcasted_iota(jnp.int32, (t, t), 0)
    col = lax.broadcasted_iota(jnp.int32, (t, t), 1)
    m_up = (col >= row).astype(BF16)
    m_low = (col <= row).astype(BF16)
    return m_up, m_low


def _emit_chains(chains, stages, skew):
    if skew:
        for step in range(len(chains) + len(stages) - 1):
            for si, stage in enumerate(stages):
                if 0 <= step - si < len(chains):
                    stage(chains[step - si])
    else:
        for stage in stages:
            for c in chains:
                stage(c)


def _sb_fwd(q, k, vt3, pad, name, ride=((), ())):
    r = q.shape[0]
    t = ATT_BLK
    nb = r // t
    nbp = -(-(nb + 1) // 8) * 8
    jmin = pad // t
    scale = SB_HEAD_DIM ** -0.5
    n_ride = len(ride[0])

    def body(q_ref, k_ref, vt_ref, *rest):
        ride_in, (o_ref, ss_ref), ride_out = rest[:n_ride], rest[n_ride:n_ride + 2], rest[n_ride + 2:2 * n_ride + 2]
        acc_ref, kn_ref = rest[2 * n_ride + 2:2 * n_ride + 4]
        ride_sems = rest[2 * n_ride + 4:]
        i = pl.program_id(1)
        if n_ride:
            @pl.when((pl.program_id(0) == 0) & (i == 0))
            def _():
                for cp in _exchange_copies(ride_in, ride_out, ride[1], *ride_sems):
                    cp.start()

        @pl.when(i == 0)
        def _():
            def blk(b, m):
                kb = k_ref[pl.ds(pl.multiple_of(b * t, t), t), :].astype(F32)
                return jnp.maximum(m, jnp.max(jnp.sum(kb * kb, axis=1, keepdims=True), axis=0, keepdims=True))
            kn_ref[...] = jnp.broadcast_to(lax.fori_loop(0, nb, blk, jnp.zeros((1, 1), F32)), (8, 128))

        qf = q_ref[...].astype(F32)
        z_bound = scale * jnp.sqrt(jnp.max(jnp.sum(qf * qf, axis=1, keepdims=True)) * jnp.max(kn_ref[...]))

        def need(carry):
            return jnp.maximum(jnp.max(carry[0]), jnp.max(carry[1])) + z_bound >= SB_LOG_ZERO

        qt = qf.T
        sub = lax.broadcasted_iota(jnp.int32, (128, 1), 0)
        m_up, _ = _sb_consts(t)
        kpos0 = lax.broadcasted_iota(jnp.int32, (t, 1), 0)
        qpos = i * t + lax.broadcasted_iota(jnp.int32, (1, t), 1)
        qths = [jnp.where((sub >= 64 * h) & (sub < 64 * (h + 1)), qt * scale, 0.0).astype(BF16) for h in range(2)]
        acc_ref[...] = jnp.zeros_like(acc_ref)

        def sweep(js, carry, masked):
            kbs = [k_ref[pl.ds(pl.multiple_of(j * t, t), t), :] for j in js]
            vts = [vt_ref[0, j] for j in js]
            accs = [acc_ref[0], acc_ref[1]]
            s = list(carry)
            chains = [(n, h) for n in range(len(js)) for h in range(2)]
            masked = [masked] * len(js) if isinstance(masked, bool) else masked
            valid = [(js[n] * t + kpos0 < qpos) & (js[n] * t + kpos0 >= pad) if masked[n] else None for n in range(len(js))]
            zt, inc, saves = {}, {}, []

            def st_scores(c):
                zt[c] = _dot(kbs[c[0]], qths[c[1]])

            def st_cumsum(c):
                lk = -_softplus(zt[c])
                if masked[c[0]]:
                    lk = jnp.where(valid[c[0]], lk, 0.0)
                inc[c] = _split_dot(m_up, lk)

            def st_weights(c):
                n, h = c
                saves.append((h, js[n], s[h]))
                w = jnp.exp(zt[c] + inc[c] + s[h])
                if masked[n]:
                    w = jnp.where(valid[n], w, 0.0)
                accs[h] = accs[h] + _dot(vts[n], w.astype(BF16))
                s[h] = s[h] + inc[c][0:1, :]

            _emit_chains(chains, [st_scores, st_cumsum, st_weights], SB_FWD_SKEW)
            for h, j, val in saves:
                ss_ref[h, 0, pl.ds(j, 1), :] = val
            acc_ref[0] = accs[0]
            acc_ref[1] = accs[1]
            return tuple(s)

        zero = jnp.zeros((1, t), F32)
        bpi = SB_BLOCKS_PER_TRIP
        j, carry = lax.cond(
            i - 1 > jmin,
            lambda: (i - 2, sweep([i, i - 1], (zero, zero), [True, False])),
            lambda: (i - 1, sweep([i], (zero, zero), True)))
        def further(j, carry):
            j, carry = lax.while_loop(
                lambda st: (st[0] - bpi >= jmin) & need(st[1]),
                lambda st: (st[0] - bpi, sweep([st[0] - b for b in range(bpi)], st[1], False)), (j, carry))
            j, carry = lax.while_loop(
                lambda st: (st[0] > jmin) & need(st[1]),
                lambda st: (st[0] - 1, sweep([st[0]], st[1], False)), (j, carry))
            return lax.while_loop(
                lambda st: (st[0] == jmin) & (i > jmin) & need(st[1]),
                lambda st: (st[0] - 1, sweep([st[0]], st[1], True)), (j, carry))[0]

        j = lax.cond((j >= jmin) & need(carry), lambda: further(j, carry), lambda: j)
        first = jnp.full((1, t), j + 1, jnp.int32).astype(F32)
        ss_ref[0, 0, nbp - 1:nbp, :] = first
        ss_ref[1, 0, nbp - 1:nbp, :] = first
        acc = jnp.where(sub < 64, acc_ref[0], acc_ref[1])
        o_ref[...] = acc.T
        if n_ride:
            @pl.when((pl.program_id(0) == 3) & (i == nb - 1))
            def _():
                for cp in _exchange_copies(ride_in, ride_out, ride[1], *ride_sems):
                    cp.wait()

    hbm = pl.BlockSpec(memory_space=pl.ANY)
    outs = _pallas(
        body, name=name, grid=(4, nb),
        in_specs=[pl.BlockSpec((t, 128), lambda hp, i: (i, hp)),
                  pl.BlockSpec((r, 128), lambda hp, i: (0, hp)),
                  pl.BlockSpec((1, nb, 128, t), lambda hp, i: (hp, 0, 0, 0))] + [hbm] * n_ride,
        out_specs=[pl.BlockSpec((t, 128), lambda hp, i: (i, hp)),
                   pl.BlockSpec((2, 1, nbp, t), lambda hp, i: (hp, i, 0, 0))] + [hbm] * n_ride,
        out_shape=[jax.ShapeDtypeStruct((r, SB_WIDTH), F32),
                   jax.ShapeDtypeStruct((8, nb, nbp, t), F32)] + _exchange_shapes(*ride),
        scratch_shapes=[pltpu.VMEM((2, 128, t), F32), pltpu.VMEM((8, 128), F32)] + (_exchange_sems(n_ride) if n_ride else []),
        compiler_params=_cparams(("arbitrary", "arbitrary")),
    )(q, k, vt3, *ride[0])
    return outs[0], outs[1], list(outs[2:])


def _sb_bwd(q, k, v, kt3, ssave, do, pad, name):
    r = q.shape[0]
    t = ATT_BLK
    nb = r // t
    nbp = ssave.shape[2]
    jmin = pad // t
    scale = SB_HEAD_DIM ** -0.5

    def body(q_ref, do_ref, k_ref, v_ref, kt_ref, ss_ref, dq_ref, dk_hbm, dv_hbm, dk_acc, dv_acc, dq_acc, sem):
        hp = pl.program_id(0)
        i = pl.program_id(1)

        @pl.when(i == 0)
        def _():
            dk_acc[...] = jnp.zeros_like(dk_acc)
            dv_acc[...] = jnp.zeros_like(dv_acc)

        qf = q_ref[...].astype(F32)
        dof = do_ref[...]
        qt = qf.T
        dot_ = dof.T
        sub = lax.broadcasted_iota(jnp.int32, (128, 1), 0)
        lane = lax.broadcasted_iota(jnp.int32, (1, 128), 1)
        m_up, m_low = _sb_consts(t)
        kpos0 = lax.broadcasted_iota(jnp.int32, (t, 1), 0)
        qpos = i * t + lax.broadcasted_iota(jnp.int32, (1, t), 1)
        first = jnp.clip(jnp.max(ss_ref[0, 0, nbp - 1:nbp, :]).astype(jnp.int32), jmin, i)
        mid0 = jnp.maximum(first, jmin + 1)
        pair = i - mid0 >= 1
        n_mid = jnp.maximum(i - mid0 - 1, 0)
        n_edge = jnp.where((i > jmin) & (first == jmin), 1, 0)
        in_t = [(sub >= 64 * h) & (sub < 64 * (h + 1)) for h in range(2)]
        in_l = [(lane >= 64 * h) & (lane < 64 * (h + 1)) for h in range(2)]
        qths = [jnp.where(in_t[h], qt * scale, 0.0).astype(BF16) for h in range(2)]
        doths = [jnp.where(in_t[h], dot_, 0.0).astype(BF16) for h in range(2)]
        qhs = [jnp.where(in_l[h], qf * scale, 0.0).astype(BF16) for h in range(2)]
        dohs = [jnp.where(in_l[h], dof, 0.0).astype(BF16) for h in range(2)]
        dq_acc[...] = jnp.zeros_like(dq_acc)

        def sweep(js, carry, masked):
            rows = [pl.ds(pl.multiple_of(j * t, t), t) for j in js]
            kbs = [k_ref[rw, :] for rw in rows]
            vbs = [v_ref[rw, :] for rw in rows]
            kts = [kt_ref[0, j] for j in js]
            sss = [[ss_ref[h, 0, pl.ds(j, 1), :] for h in range(2)] for j in js]
            dv_old = [dv_acc[rw, :] for rw in rows]
            dk_old = [dk_acc[rw, :] for rw in rows]
            dqs = [dq_acc[0], dq_acc[1]]
            ec = list(carry)
            chains = [(n, h) for n in range(len(js)) for h in range(2)]
            masked = [masked] * len(js) if isinstance(masked, bool) else masked
            valid = [(js[n] * t + kpos0 < qpos) & (js[n] * t + kpos0 >= pad) if masked[n] else None for n in range(len(js))]
            zt, dvt, sp, inc, e, big_e = {}, {}, {}, {}, {}, {}

            def st_scores(c):
                zt[c] = _dot(kbs[c[0]], qths[c[1]])
                dvt[c] = _dot(vbs[c[0]], doths[c[1]])

            def st_cumsum(c):
                sp[c] = _softplus(zt[c])
                lk = -sp[c]
                if masked[c[0]]:
                    lk = jnp.where(valid[c[0]], lk, 0.0)
                inc[c] = _split_dot(m_up, lk)

            def st_weights(c):
                n, h = c
                w = jnp.exp(zt[c] + inc[c] + sss[n][h])
                if masked[n]:
                    w = jnp.where(valid[n], w, 0.0)
                dv_old[n] = dv_old[n] + _dot(w.astype(BF16), dohs[h])
                e[c] = w * dvt[c]
                pinc = _split_dot(m_low, e[c])
                big_e[c] = pinc - e[c] + ec[h]
                ec[h] = ec[h] + pinc[t - 1:t, :]

            def st_dscores(c):
                n, h = c
                dz = e[c] - jnp.exp(zt[c] - sp[c]) * (e[c] + big_e[c])
                if masked[n]:
                    dz = jnp.where(valid[n], dz, 0.0)
                dzb = dz.astype(BF16)
                dqs[h] = dqs[h] + _dot(kts[n], dzb)
                dk_old[n] = dk_old[n] + _dot(dzb, qhs[h])

            _emit_chains(chains, [st_scores, st_cumsum, st_weights, st_dscores], SB_BWD_SKEW)
            for n, rw in enumerate(rows):
                dv_acc[rw, :] = dv_old[n]
                dk_acc[rw, :] = dk_old[n]
            dq_acc[0] = dqs[0]
            dq_acc[1] = dqs[1]
            return tuple(ec)

        zero = jnp.zeros((1, t), F32)
        bpi = SB_BLOCKS_PER_TRIP
        carry = lax.fori_loop(0, n_edge, lambda it, c: sweep([jmin + it * 0], c, True), (zero, zero))
        carry = lax.fori_loop(0, n_mid // bpi, lambda it, c: sweep([mid0 + bpi * it + b for b in range(bpi)], c, False), carry)
        n_rem = n_mid % bpi
        carry = lax.fori_loop(0, n_rem, lambda it, c: sweep([i - 1 - n_rem + it], c, False), carry)
        lax.cond(pair, lambda: sweep([i - 1, i], carry, [False, True]), lambda: sweep([i], carry, True))
        dq_ref[...] = (jnp.where(sub < 64, dq_acc[0], dq_acc[1]) * scale).T

        @pl.when(i == nb - 1)
        def _():
            c1 = pltpu.make_async_copy(dk_acc, dk_hbm.at[hp], sem.at[0])
            c2 = pltpu.make_async_copy(dv_acc, dv_hbm.at[hp], sem.at[1])
            c1.start()
            c2.start()
            c1.wait()
            c2.wait()

    return _pallas(
        body, name=name, grid=(4, nb),
        in_specs=[pl.BlockSpec((t, 128), lambda hp, i: (i, hp)),
                  pl.BlockSpec((t, 128), lambda hp, i: (i, hp)),
                  pl.BlockSpec((r, 128), lambda hp, i: (0, hp)),
                  pl.BlockSpec((r, 128), lambda hp, i: (0, hp)),
                  pl.BlockSpec((1, nb, 128, t), lambda hp, i: (hp, 0, 0, 0)),
                  pl.BlockSpec((2, 1, nbp, t), lambda hp, i: (hp, i, 0, 0))],
        out_specs=[pl.BlockSpec((t, 128), lambda hp, i: (i, hp)),
                   pl.BlockSpec(memory_space=pl.ANY), pl.BlockSpec(memory_space=pl.ANY)],
        out_shape=[jax.ShapeDtypeStruct((r, SB_WIDTH), F32),
                   jax.ShapeDtypeStruct((4, r, 128), F32), jax.ShapeDtypeStruct((4, r, 128), F32)],
        scratch_shapes=[pltpu.VMEM((r, 128), F32), pltpu.VMEM((r, 128), F32), pltpu.VMEM((2, 128, t), F32),
                        pltpu.SemaphoreType.DMA((2,))],
        compiler_params=_cparams(("arbitrary", "arbitrary")),
    )(q, do, k, v, kt3, ssave)


def _s5_disc(lam_re, lam_im, logdt, btr, bti):
    lr = jnp.minimum(lam_re, -1e-4)
    li = lam_im
    dt = jnp.exp(logdt)
    mag = jnp.exp(lr * dt)
    ang = li * dt
    a_re, a_im = mag * jnp.cos(ang), mag * jnp.sin(ang)
    den = lr * lr + li * li
    nr, ni = a_re - 1.0, a_im
    c_re = (nr * lr + ni * li) / den
    c_im = (ni * lr - nr * li) / den
    return a_re, a_im, c_re * btr - c_im * bti, c_re * bti + c_im * btr


def _s5_prep(lam_re, lam_im, logdt, btr, bti, name):
    ns = lam_re.shape[1]

    def body(lr_ref, li_ref, dt_ref, br_ref, bi_ref, ar_ref, ai_ref, bbr_ref, bbi_ref):
        ar, ai, bbr, bbi = _s5_disc(lr_ref[...], li_ref[...], dt_ref[...], br_ref[...], bi_ref[...])
        ar_ref[...] = ar
        ai_ref[...] = ai
        bbr_ref[...] = bbr
        bbi_ref[...] = bbi

    return _pallas(body, name=name,
                   out_shape=[jax.ShapeDtypeStruct((1, ns), F32)] * 2 + [jax.ShapeDtypeStruct((S5_GROUP, ns), F32)] * 2,
                   )(lam_re, lam_im, logdt, btr, bti)


def _s5_prep_bwd(lam_re, lam_im, logdt, btr, bti, dar, dai, dbbr, dbbi, name):
    ns = lam_re.shape[1]

    def body(lr_ref, li_ref, dt_ref, br_ref, bi_ref, dar_ref, dai_ref, dbr_ref, dbi_ref, o_lr, o_li, o_dt, o_br, o_bi):
        _, vjp = jax.vjp(_s5_disc, lr_ref[...], li_ref[...], dt_ref[...], br_ref[...], bi_ref[...])
        g = vjp((dar_ref[...], dai_ref[...], dbr_ref[...], dbi_ref[...]))
        o_lr[...] = g[0]
        o_li[...] = g[1]
        row = lax.broadcasted_iota(jnp.int32, (ns, ns), 0) // S5_STATE
        col = lax.broadcasted_iota(jnp.int32, (ns, ns), 1) // S5_STATE
        same = (row == col).astype(F32)
        o_dt[...] = _dot_hi(jnp.broadcast_to(g[2], (8, ns)), same)[0:1]
        o_br[...] = g[3]
        o_bi[...] = g[4]

    return _pallas(body, name=name,
                   out_shape=[jax.ShapeDtypeStruct((1, ns), F32)] * 3 + [jax.ShapeDtypeStruct((S5_GROUP, ns), F32)] * 2,
                   compiler_params=pltpu.CompilerParams(vmem_limit_bytes=VMEM_LIMIT),
                   )(lam_re, lam_im, logdt, btr, bti, dar, dai, dbbr, dbbi)


def _s5_scan(br, bi, ar, ai, t, reverse=False, carry=None):
    ng = t // 8
    ns = br.shape[1]
    br, bi = br.reshape(ng, 8, ns), bi.reshape(ng, 8, ns)
    row8 = lax.broadcasted_iota(jnp.int32, (1, 8, 1), 1)
    pr, pi_ = ar, ai
    for k in (1, 2, 4):
        if reverse:
            sr, si, ok = pltpu.roll(br, 8 - k, 1), pltpu.roll(bi, 8 - k, 1), row8 < 8 - k
        else:
            sr, si, ok = pltpu.roll(br, k, 1), pltpu.roll(bi, k, 1), row8 >= k
        sr = jnp.where(ok, sr, 0.0)
        si = jnp.where(ok, si, 0.0)
        br, bi = br + pr * sr - pi_ * si, bi + pr * si + pi_ * sr
        pr, pi_ = pr * pr - pi_ * pi_, 2.0 * pr * pi_
    pw_r, pw_i = [ar], [ai]
    for _ in range(7):
        pw_r.append(pw_r[-1] * ar - pw_i[-1] * ai)
        pw_i.append(pw_r[-2] * ai + pw_i[-1] * ar)
    if reverse:
        pw_r.reverse()
        pw_i.reverse()
    p8r, p8i = jnp.concatenate(pw_r, axis=0), jnp.concatenate(pw_i, axis=0)
    out_r, out_i = [None] * ng, [None] * ng
    order = range(ng - 1, -1, -1) if reverse else range(ng)
    edge = 0 if reverse else 7
    for g in order:
        gr, gi = br[g], bi[g]
        if carry is not None:
            cr, ci = carry
            gr, gi = gr + p8r * cr - p8i * ci, gi + p8r * ci + p8i * cr
        out_r[g], out_i[g] = gr, gi
        carry = (gr[edge:edge + 1], gi[edge:edge + 1])
    return jnp.concatenate(out_r, axis=0), jnp.concatenate(out_i, axis=0)


def _s5_prev_rows(x, first, t):
    ng = t // 8
    ns = x.shape[1]
    x3 = x.reshape(ng, 8, ns)
    last = x3[:, 7:8, :]
    before = jnp.concatenate([first.reshape(1, 1, ns), last[:ng - 1]], axis=0)
    row8 = lax.broadcasted_iota(jnp.int32, (1, 8, 1), 1)
    return jnp.where(row8 == 0, before, pltpu.roll(x3, 1, 1)).reshape(t, ns)


_GELU_C = math.sqrt(2.0 / math.pi)


def _gelu(y):
    th = jnp.tanh(_GELU_C * (y + 0.044715 * y * y * y))
    return 0.5 * y * (1.0 + th), th


def _sigmoid(x):
    return 1.0 / (1.0 + jnp.exp(-x))


def _s5_fwd(u, wb, a, wc, dskip, wglu, bglu, gnorm, name):
    r = u.shape[0]
    t = S5_TILE
    nt = r // t
    ns = wb.shape[2]
    w = S5_WIDTH

    def body(u_ref, wb_ref, a_ref, wc_ref, d_ref, wg_ref, bg_ref, gn_ref, y_ref, on_ref, xs_ref, carry_ref):
        i = pl.program_id(0)
        ar, ai = a_ref[0], a_ref[1]

        @pl.when(i == 0)
        def _():
            carry_ref[...] = jnp.zeros_like(carry_ref)

        u_ = u_ref[...]
        ub = u_.astype(BF16)
        xs_ref[0] = carry_ref[:, 0, :]
        chunks = list(range(S5_CHUNKS))
        sl_s = [slice(c * (ns // S5_CHUNKS), (c + 1) * (ns // S5_CHUNKS)) for c in chunks]
        sl_u = [slice(c * (w // S5_CHUNKS), (c + 1) * (w // S5_CHUNKS)) for c in chunks]
        bu, xs, ys = {}, {}, {}

        def st_inputs(c):
            bu[c] = (_dot(ub[:, sl_u[c]], wb_ref[0, sl_u[c], sl_s[c]]), _dot(ub[:, sl_u[c]], wb_ref[1, sl_u[c], sl_s[c]]))

        def st_scan(c):
            xr, xi = _s5_scan(*bu[c], ar[:, sl_s[c]], ai[:, sl_s[c]], t, carry=(carry_ref[0, :, sl_s[c]], carry_ref[1, :, sl_s[c]]))
            carry_ref[0, :, sl_s[c]] = xr[t - 1:t, :]
            carry_ref[1, :, sl_s[c]] = xi[t - 1:t, :]
            xs[c] = (xr.astype(BF16), xi.astype(BF16))

        def st_outputs(c):
            ys[c] = _dot(xs[c][0], wc_ref[0, sl_s[c], sl_u[c]]) - _dot(xs[c][1], wc_ref[1, sl_s[c], sl_u[c]])

        _emit_chains(chunks, [st_inputs, st_scan, st_outputs], False)
        y = jnp.concatenate([ys[c] for c in chunks], axis=1) + d_ref[...] * u_
        h, _ = _gelu(y)
        gate = _sigmoid(_dot(h.astype(BF16), wg_ref[...]) + bg_ref[...])
        y_ref[...] = y
        on_ref[...] = _rms(h * gate, gn_ref[...]).astype(BF16)

    full = lambda shape: pl.BlockSpec(shape, lambda i: (0,) * len(shape))
    return _pallas(
        body, name=name, grid=(nt,),
        in_specs=[_row_spec(w, t), full((2, w, ns)), full((2, 1, ns)), full((2, ns, w)), full((1, w)),
                  full((w, w)), full((1, w)), full((1, w))],
        out_specs=[_row_spec(w, t), _row_spec(w, t), pl.BlockSpec((1, 2, ns), lambda i: (i, 0, 0))],
        out_shape=[jax.ShapeDtypeStruct((r, w), F32), jax.ShapeDtypeStruct((r, w), BF16),
                   jax.ShapeDtypeStruct((nt, 2, ns), F32)],
        scratch_shapes=[pltpu.VMEM((2, 1, ns), F32)],
        compiler_params=_cparams(("arbitrary",)),
    )(u, wb, a, wc, dskip, wglu, bglu, gnorm)


def _s5_bwd(u, y, don, xstart, wb, a, wc, dskip, wglu, bglu, gnorm, name, ride=((), ()), don_block=0):
    r = u.shape[0]
    t = S5_TILE
    nt = r // t
    ns = wb.shape[2]
    w = S5_WIDTH
    nt_dims = ((1,), (1,))
    tn_dims = ((0,), (0,))

    def body(u_ref, y_ref, don_ref, xs_ref, wb_hbm, a_ref, wc_hbm, d_ref, wg_ref, bg_ref, gn_ref,
             du_ref, da_ref, dd_ref, dbg_ref, dgn_ref, dwb_hbm, dwc_hbm, dwg_hbm,
             wb_ref, wc_ref, lam_ref, acc_wb, acc_wc, acc_wg, sem):
        i = pl.program_id(0)
        ar, ai = a_ref[0], a_ref[1]

        @pl.when(i == 0)
        def _():
            c1 = pltpu.make_async_copy(wb_hbm, wb_ref, sem.at[0])
            c2 = pltpu.make_async_copy(wc_hbm, wc_ref, sem.at[1])
            c1.start()
            c2.start()
            lam_ref[...] = jnp.zeros_like(lam_ref)
            acc_wb[...] = jnp.zeros_like(acc_wb)
            acc_wc[...] = jnp.zeros_like(acc_wc)
            acc_wg[...] = jnp.zeros_like(acc_wg)
            da_ref[...] = jnp.zeros_like(da_ref)
            dd_ref[...] = jnp.zeros_like(dd_ref)
            dbg_ref[...] = jnp.zeros_like(dbg_ref)
            dgn_ref[...] = jnp.zeros_like(dgn_ref)
            c1.wait()
            c2.wait()

        u_ = u_ref[...]
        y_ = y_ref[...]
        ub = u_.astype(BF16)
        h, th = _gelu(y_)
        hb = h.astype(BF16)
        wg = wg_ref[...]
        gate = _sigmoid(_dot(hb, wg) + bg_ref[...])
        d_out, dgn = _rms_bwd(h * gate, gn_ref[...], don_ref[...])
        dgn_ref[...] += dgn
        dhw = d_out * h * gate * (1.0 - gate)
        dhwb = dhw.astype(BF16)
        dh = d_out * gate + _dot(dhwb, wg, nt_dims)
        acc_wg[...] += _dot(hb, dhwb, tn_dims)
        dbg_ref[...] += jnp.sum(dhw, axis=0, keepdims=True)
        dgelu = 0.5 * (1.0 + th) + 0.5 * y_ * (1.0 - th * th) * _GELU_C * (1.0 + 3.0 * 0.044715 * y_ * y_)
        dy = dh * dgelu
        dd_ref[...] += jnp.sum(dy * u_, axis=0, keepdims=True)
        dyb = dy.astype(BF16)
        chunks = list(range(S5_CHUNKS))
        sl_s = [slice(c * (ns // S5_CHUNKS), (c + 1) * (ns // S5_CHUNKS)) for c in chunks]
        sl_u = [slice(c * (w // S5_CHUNKS), (c + 1) * (w // S5_CHUNKS)) for c in chunks]
        bu, gx, x_, lam, dus = {}, {}, {}, {}, {}

        def st_inputs(c):
            su, ss = sl_u[c], sl_s[c]
            bu[c] = (_dot(ub[:, su], wb_ref[0, su, ss]), _dot(ub[:, su], wb_ref[1, su, ss]))
            gx[c] = (_dot(dyb[:, su], wc_ref[0, ss, su], nt_dims), -_dot(dyb[:, su], wc_ref[1, ss, su], nt_dims))

        def st_states(c):
            su, ss = sl_u[c], sl_s[c]
            first = (xs_ref[0, 0:1, ss], xs_ref[0, 1:2, ss])
            xr, xi = _s5_scan(*bu[c], ar[:, ss], ai[:, ss], t, carry=first)
            acc_wc[0, ss, su] += _dot(xr.astype(BF16), dyb[:, su], tn_dims)
            acc_wc[1, ss, su] -= _dot(xi.astype(BF16), dyb[:, su], tn_dims)
            x_[c] = (_s5_prev_rows(xr, first[0], t), _s5_prev_rows(xi, first[1], t))

        def st_adjoint(c):
            su, ss = sl_u[c], sl_s[c]
            lr, li = _s5_scan(*gx[c], ar[:, ss], -ai[:, ss], t, reverse=True, carry=(lam_ref[0, :, ss], lam_ref[1, :, ss]))
            lam_ref[0, :, ss] = lr[0:1, :]
            lam_ref[1, :, ss] = li[0:1, :]
            lrb, lib = lr.astype(BF16), li.astype(BF16)
            acc_wb[0, su, ss] += _dot(ub[:, su], lrb, tn_dims)
            acc_wb[1, su, ss] += _dot(ub[:, su], lib, tn_dims)
            dus[c] = _dot(lrb, wb_ref[0, su, ss], nt_dims) + _dot(lib, wb_ref[1, su, ss], nt_dims)
            lam[c] = (lr, li)

        def st_decay(c):
            ss = sl_s[c]
            (lr, li), (xpr, xpi) = lam[c], x_[c]
            da_ref[0, :, ss] += jnp.sum(lr * xpr + li * xpi, axis=0, keepdims=True)
            da_ref[1, :, ss] += jnp.sum(li * xpr - lr * xpi, axis=0, keepdims=True)

        _emit_chains(chunks, [st_inputs, st_states, st_adjoint, st_decay], False)
        du_ref[...] = d_ref[...] * dy + jnp.concatenate([dus[c] for c in chunks], axis=1)

        @pl.when(i == nt - 1)
        def _():
            cps = [pltpu.make_async_copy(acc_wb, dwb_hbm, sem.at[0]), pltpu.make_async_copy(acc_wc, dwc_hbm, sem.at[1]),
                   pltpu.make_async_copy(acc_wg, dwg_hbm, sem.at[2])]
            for c in cps:
                c.start()
            for c in cps:
                c.wait()

    n_ride = len(ride[0])
    n_in, n_out, n_scratch = 11, 8, 7

    def body_with_ride(*refs):
        ins, rest = refs[:n_in], refs[n_in:]
        ride_in, rest = rest[:n_ride], rest[n_ride:]
        outs, rest = rest[:n_out], rest[n_out:]
        ride_out, rest = rest[:n_ride], rest[n_ride:]
        scratch, ride_sems = rest[:n_scratch], rest[n_scratch:]
        if n_ride:
            @pl.when(pl.program_id(0) == 0)
            def _():
                for cp in _exchange_copies(ride_in, ride_out, ride[1], *ride_sems):
                    cp.start()
        body(*ins, *outs, *scratch)
        if n_ride:
            @pl.when(pl.program_id(0) == nt - 1)
            def _():
                for cp in _exchange_copies(ride_in, ride_out, ride[1], *ride_sems):
                    cp.wait()

    rev = lambda i: (nt - 1 - i, 0)
    full = lambda shape: pl.BlockSpec(shape, lambda i: (0,) * len(shape))
    hbm = pl.BlockSpec(memory_space=pl.ANY)
    outs = _pallas(
        body_with_ride, name=name, grid=(nt,),
        in_specs=[pl.BlockSpec((t, w), rev), pl.BlockSpec((t, w), rev), pl.BlockSpec((t, w), lambda i: (nt - 1 - i, don_block)),
                  pl.BlockSpec((1, 2, ns), lambda i: (nt - 1 - i, 0, 0)), hbm, full((2, 1, ns)), hbm, full((1, w)),
                  full((w, w)), full((1, w)), full((1, w))] + [hbm] * n_ride,
        out_specs=[pl.BlockSpec((t, w), rev), full((2, 1, ns)), full((1, w)), full((1, w)), full((1, w)), hbm, hbm, hbm]
        + [hbm] * n_ride,
        out_shape=[jax.ShapeDtypeStruct((r, w), F32), jax.ShapeDtypeStruct((2, 1, ns), F32)]
        + [jax.ShapeDtypeStruct((1, w), F32)] * 3
        + [jax.ShapeDtypeStruct((2, w, ns), F32), jax.ShapeDtypeStruct((2, ns, w), F32), jax.ShapeDtypeStruct((w, w), F32)]
        + _exchange_shapes(*ride),
        scratch_shapes=[pltpu.VMEM((2, w, ns), BF16), pltpu.VMEM((2, ns, w), BF16), pltpu.VMEM((2, 1, ns), F32),
                        pltpu.VMEM((2, w, ns), F32), pltpu.VMEM((2, ns, w), F32), pltpu.VMEM((w, w), F32),
                        pltpu.SemaphoreType.DMA((3,))] + (_exchange_sems(n_ride) if n_ride else []),
        compiler_params=_cparams(("arbitrary",)),
    )(u, y, don, xstart, wb, a, wc, dskip, wglu, bglu, gnorm, *ride[0])
    return tuple(outs[:n_out]) + (list(outs[n_out:]),)


def _s5_expand(lam_re, lam_im, log_dt, b_re, b_im, c_re, c_im):
    g, n, p = S5_GROUPS, S5_STATE, S5_GROUP
    ns = g * n
    rows = lambda x: x.reshape(1, ns)
    logdt = jnp.repeat(log_dt.reshape(g), n).reshape(1, ns)
    btr = b_re.reshape(ns, p).T
    bti = b_im.reshape(ns, p).T
    ctr = c_re.transpose(0, 2, 1).reshape(ns, p)
    cti = c_im.transpose(0, 2, 1).reshape(ns, p)
    mask = (jnp.arange(g * p)[:, None] // p) == (jnp.arange(ns)[None, :] // n)
    return rows(lam_re), rows(lam_im), logdt, btr, bti, ctr, cti, mask


def _s5_block_diag_b(bb, mask):
    return jnp.where(mask, jnp.tile(bb, (S5_GROUPS, 1)), 0.0)


def _s5_block_diag_c(ct, mask):
    return jnp.where(mask.T, jnp.tile(ct, (1, S5_GROUPS)), 0.0)


def _s5_diag_of_b(dwb, mask):
    return jnp.where(mask, dwb, 0.0).reshape(S5_GROUPS, S5_GROUP, -1).sum(0)


def _s5_diag_of_c(dwc, mask):
    ns = dwc.shape[0]
    return jnp.where(mask.T, dwc, 0.0).reshape(ns, S5_GROUPS, S5_GROUP).sum(1)


DN_PRE_TILE = 256
_DN_QKV = 3 * DN_WIDTH


def _halo_specs(width, tile, nt, prev):
    per = tile // 8
    if prev:
        return pl.BlockSpec((8, width), lambda i: (jnp.maximum(i * per - 1, 0), 0))
    return pl.BlockSpec((8, width), lambda i: (jnp.minimum((i + 1) * per, nt * per - 1), 0))


def _shift_down(x, halo, s, t):
    xx = jnp.concatenate([halo, x], axis=0)
    return pltpu.roll(xx, s, 0)[8:]


def _shift_up(x, halo, s, t):
    xx = jnp.concatenate([x, halo], axis=0)
    return pltpu.roll(xx, t + 8 - s, 0)[:t]


def _silu(x):
    s = _sigmoid(x)
    return x * s, s


def _dn_gates(ab, alog, dtb, live):
    lane = lax.broadcasted_iota(jnp.int32, (1, 128), 1)
    g = -jnp.exp(alog) * _softplus(ab + dtb)
    beta = _sigmoid(ab)
    return jnp.where(live & (lane < DN_HEADS), g, jnp.where(live & (lane < 2 * DN_HEADS), beta, 0.0))


def _dn_pre_fwd(proj, ab, conv_w, alog, dtb, pad, name):
    r = proj.shape[0]
    t = DN_PRE_TILE
    nt = r // t
    scale = DN_HEAD_DIM ** -0.5

    def body(x_ref, halo_ref, ab_ref, w_ref, al_ref, dt_ref, q_ref, k_ref, v_ref, gb_ref):
        i = pl.program_id(0)
        act, _ = _silu(_dn_conv(x_ref[...], jnp.where(i > 0, halo_ref[...], 0.0), w_ref[...], t))
        for hd in range(DN_HEADS):
            sl = slice(hd * 128, (hd + 1) * 128)
            for base, o_ref, sc in ((0, q_ref, scale), (DN_WIDTH, k_ref, 1.0)):
                xh = act[:, base + hd * 128: base + (hd + 1) * 128]
                o_ref[:, sl] = (xh * (lax.rsqrt(jnp.sum(xh * xh, axis=-1, keepdims=True) + EPS) * sc)).astype(BF16)
        v_ref[...] = act[:, 2 * DN_WIDTH:].astype(BF16)
        rows = i * t + lax.broadcasted_iota(jnp.int32, (t, 1), 0)
        gb_ref[...] = _dn_gates(ab_ref[...], al_ref[...], dt_ref[...], rows >= pad)

    return _pallas(
        body, name=name, grid=(nt,),
        in_specs=[pl.BlockSpec((t, _DN_QKV), lambda i: (i, 0)), _halo_specs(_DN_QKV, t, nt, True), _row_spec(128, t),
                  pl.BlockSpec((DN_CONV, _DN_QKV), lambda i: (0, 0)), _vec_spec(128), _vec_spec(128)],
        out_specs=[_row_spec(DN_WIDTH, t), _row_spec(DN_WIDTH, t), _row_spec(DN_WIDTH, t), _row_spec(128, t)],
        out_shape=[jax.ShapeDtypeStruct((r, DN_WIDTH), BF16)] * 3 + [jax.ShapeDtypeStruct((r, 128), F32)],
        compiler_params=_cparams(("parallel",)),
    )(proj, proj, ab, conv_w, alog, dtb)


def _dn_conv(x, halo, w, t):
    co = w[DN_CONV - 1:DN_CONV] * x
    for tap in range(DN_CONV - 1):
        co = co + w[tap:tap + 1] * _shift_down(x, halo, DN_CONV - 1 - tap, t)
    return co


def _dn_pre_bwd(proj, conv_w, dq, dk, dv, dgb, ab, alog, dtb, pad, name):
    r = proj.shape[0]
    t = DN_PRE_TILE
    nt = r // t
    scale = DN_HEAD_DIM ** -0.5

    def body(x_ref, halo_ref, w_ref, dq_ref, dk_ref, dv_ref, dgb_ref, ab_ref, al_ref, dt_ref, dco_ref, dab_ref, dal_ref,
             ddt_ref):
        i = pl.program_id(0)

        @pl.when(i == 0)
        def _():
            dal_ref[...] = jnp.zeros_like(dal_ref)
            ddt_ref[...] = jnp.zeros_like(ddt_ref)

        co_ = _dn_conv(x_ref[...], jnp.where(i > 0, halo_ref[...], 0.0), w_ref[...], t)
        act, sg = _silu(co_)
        dsilu = sg * (1.0 + co_ * (1.0 - sg))
        for hd in range(DN_HEADS):
            sl = slice(hd * 128, (hd + 1) * 128)
            for base, d_ref, sc in ((0, dq_ref, scale), (DN_WIDTH, dk_ref, 1.0)):
                cs = slice(base + hd * 128, base + (hd + 1) * 128)
                xh = act[:, cs]
                rn = lax.rsqrt(jnp.sum(xh * xh, axis=-1, keepdims=True) + EPS)
                xhat = xh * rn
                dy = d_ref[:, sl]
                dx = (sc * rn) * (dy - xhat * jnp.sum(dy * xhat, axis=-1, keepdims=True))
                dco_ref[:, cs] = dx * dsilu[:, cs]
        dco_ref[:, 2 * DN_WIDTH:] = dv_ref[...] * dsilu[:, 2 * DN_WIDTH:]
        rows = i * t + lax.broadcasted_iota(jnp.int32, (t, 1), 0)
        live = rows >= pad
        lane = lax.broadcasted_iota(jnp.int32, (1, 128), 1)
        ab_ = ab_ref[...]
        dgb_ = dgb_ref[...]
        is_g = live & (lane < DN_HEADS)
        is_b = live & (lane >= DN_HEADS) & (lane < 2 * DN_HEADS)
        arg = ab_ + dt_ref[...]
        ea = jnp.exp(al_ref[...])
        da = jnp.where(is_g, -dgb_ * ea * _sigmoid(arg), 0.0)
        beta = _sigmoid(ab_)
        dab_ref[...] = (da + jnp.where(is_b, dgb_ * beta * (1.0 - beta), 0.0)).astype(BF16)
        ddt_ref[...] += jnp.sum(da, axis=0, keepdims=True)
        dal_ref[...] += jnp.sum(jnp.where(is_g, -dgb_ * ea * _softplus(arg), 0.0), axis=0, keepdims=True)

    return _pallas(
        body, name=name, grid=(nt,),
        in_specs=[pl.BlockSpec((t, _DN_QKV), lambda i: (i, 0)), _halo_specs(_DN_QKV, t, nt, True),
                  pl.BlockSpec((DN_CONV, _DN_QKV), lambda i: (0, 0)),
                  _row_spec(DN_WIDTH, t), _row_spec(DN_WIDTH, t), _row_spec(DN_WIDTH, t),
                  _row_spec(128, t), _row_spec(128, t), _vec_spec(128), _vec_spec(128)],
        out_specs=[_row_spec(_DN_QKV, t), _row_spec(128, t), _vec_spec(128), _vec_spec(128)],
        out_shape=[jax.ShapeDtypeStruct((r, _DN_QKV), F32), jax.ShapeDtypeStruct((r, 128), BF16),
                   jax.ShapeDtypeStruct((1, 128), F32), jax.ShapeDtypeStruct((1, 128), F32)],
        compiler_params=_cparams(("arbitrary",)),
    )(proj, proj, conv_w, dq, dk, dv, dgb, ab, alog, dtb)


def _dn_conv_bwd(dco, proj, conv_w, dproj, name):
    r = dco.shape[0]
    t = DN_PRE_TILE
    nt = r // t

    def body(d_ref, dh_ref, x_ref, xh_ref, w_ref, dproj_in, dx_ref, dw_ref):
        i = pl.program_id(0)

        @pl.when(i == 0)
        def _():
            dw_ref[...] = jnp.zeros_like(dw_ref)

        d = d_ref[...]
        dhalo = jnp.where(i < nt - 1, dh_ref[...], 0.0)
        x = x_ref[...]
        xhalo = jnp.where(i > 0, xh_ref[...], 0.0)
        w = w_ref[...]
        dx = w[3:4] * d
        dws = [None] * DN_CONV
        dws[3] = jnp.sum(d * x, axis=0, keepdims=True)
        for tap in range(DN_CONV - 1):
            s = DN_CONV - 1 - tap
            dx = dx + w[tap:tap + 1] * _shift_up(d, dhalo, s, t)
            dws[tap] = jnp.sum(d * _shift_down(x, xhalo, s, t), axis=0, keepdims=True)
        dx_ref[...] = dx.astype(BF16)
        dw_ref[...] += jnp.concatenate(dws + [jnp.zeros((8 - DN_CONV, _DN_QKV), F32)], axis=0)

    return _pallas(
        body, name=name, grid=(nt,),
        in_specs=[_row_spec(_DN_QKV, t), _halo_specs(_DN_QKV, t, nt, False),
                  pl.BlockSpec((t, _DN_QKV), lambda i: (i, 0)), _halo_specs(_DN_QKV, t, nt, True),
                  pl.BlockSpec((DN_CONV, _DN_QKV), lambda i: (0, 0)), pl.BlockSpec(memory_space=pl.ANY)],
        out_specs=[_row_spec(_DN_QKV, t), pl.BlockSpec((8, _DN_QKV), lambda i: (0, 0))],
        out_shape=[jax.ShapeDtypeStruct(dproj.shape, BF16), jax.ShapeDtypeStruct((8, _DN_QKV), F32)],
        input_output_aliases={5: 0},
        compiler_params=_cparams(("arbitrary",)),
    )(dco, dco, proj, proj, conv_w, dproj)


def _split3(x):
    hi = x.astype(BF16)
    return hi, (x - hi.astype(F32)).astype(BF16)


def _dot3s(a, b, dims=((1,), (0,))):
    return _dot(a[0], b[0], dims) + (_dot(a[0], b[1], dims) + _dot(a[1], b[0], dims))


def _dot2s(a, b, dims=((1,), (0,))):
    return _dot(a[0], b[0], dims) + _dot(a[1], b[0], dims)


def _dot3(a, b, dims=((1,), (0,))):
    return _dot3s(_split3(a), _split3(b), dims)


def _dn_inverse_many(n_mats):
    c = n_mats[0].shape[0]
    row = lax.broadcasted_iota(jnp.int32, (c, c), 0)
    col = lax.broadcasted_iota(jnp.int32, (c, c), 1)
    eye = (row == col).astype(F32)
    same = row // DN_SUB == col // DN_SUB
    nds = [jnp.where(same, n, 0.0) for n in n_mats]
    nos = [n - nd for n, nd in zip(n_mats, nds)]

    def geometric(bs, order):
        xs = [eye + b for b in bs]
        sp = [_split3(b) for b in bs]
        k = 2
        while k < order:
            sp = [_split3(_dot2s(s_, s_)) for s_ in sp]
            xs = [x + _dot2s(_split3(x), s_) for x, s_ in zip(xs, sp)]
            k *= 2
        return xs

    tds = [_split3(td) for td in geometric([-nd for nd in nds], DN_SUB)]
    ms = [_dot3s(td, _split3(no)) for td, no in zip(tds, nos)]
    xs = geometric([-m for m in ms], c // DN_SUB)
    return [_dot3s(_split3(x), td) for x, td in zip(xs, tds)]


def _dn_chunk_shared(gb_ref, gbt_ref):
    c = DN_CHUNK
    row = lax.broadcasted_iota(jnp.int32, (c, c), 0)
    col = lax.broadcasted_iota(jnp.int32, (c, c), 1)
    gbv = gb_ref[...]
    gam_all = _split_dot((row >= col).astype(BF16), gbv)
    hi, lo = _split3(gbt_ref[...])
    tri_t = (row <= col).astype(BF16)
    return dict(row=row, col=col, gbv=gbv, gam_all=gam_all, gam_rows=_dot(hi, tri_t) + _dot(lo, tri_t),
                lane=lax.broadcasted_iota(jnp.int32, (1, 128), 1))


def _dn_chunk_common(q, k, v, sh, h):
    c = DN_CHUNK
    row, col, lane = sh["row"], sh["col"], sh["lane"]
    q, k, v = q.astype(F32), k.astype(F32), v.astype(F32)
    gam = jnp.sum(jnp.where(lane == h, sh["gam_all"], 0.0), axis=1, keepdims=True)
    beta = jnp.sum(jnp.where(lane == h + DN_HEADS, sh["gbv"], 0.0), axis=1, keepdims=True)
    gam_row = sh["gam_rows"][h:h + 1]
    dec = jnp.where(row >= col, jnp.exp(jnp.minimum(gam - gam_row, 0.0)), 0.0)
    kb, qb = k.astype(BF16), q.astype(BF16)
    nt_dims = ((1,), (1,))
    kk = _dot(kb, kb, nt_dims)
    qk = _dot(qb, kb, nt_dims)
    eg = jnp.exp(gam)
    gam_l = gam[c - 1:c, :]
    return dict(q=q, k=k, v=v, qb=qb, kb=kb, gam=gam, beta=beta, dec=dec, kk=kk, qk=qk, eg=eg, gam_l=gam_l,
                row=row, col=col, lane=lane, att=qk * dec, qg=q * eg, kt=k * jnp.exp(gam_l - gam),
                rhs=jnp.concatenate([v * beta, k * (beta * eg)], axis=1))


def _dn_fwd(q, k, v, gb, gbt, name):
    r = q.shape[0]
    c = DN_CHUNK
    nc = r // c
    dh = DN_HEAD_DIM
    tn_dims = ((0,), (0,))

    def body(q_ref, k_ref, v_ref, gb_ref, gbt_ref, o_ref, ss_ref, ts_ref, s_ref):
        @pl.when(pl.program_id(0) == 0)
        def _():
            s_ref[...] = jnp.zeros_like(s_ref)

        heads = list(range(DN_HEADS))
        sl = [slice(h * dh, (h + 1) * dh) for h in heads]
        sh = _dn_chunk_shared(gb_ref, gbt_ref)
        zs = [_dn_chunk_common(q_ref[:, sl[h]], k_ref[:, sl[h]], v_ref[:, sl[h]], sh, h) for h in heads]
        t_invs = _dn_inverse_many([jnp.where(sh["row"] > sh["col"], z["beta"] * z["kk"] * z["dec"], 0.0) for z in zs])
        sols = [_dot3(t_inv, z["rhs"]) for t_inv, z in zip(t_invs, zs)]
        ss = [s_ref[h] for h in heads]
        sbs = [s.astype(BF16) for s in ss]
        vnbs = [(sol[:, :dh] - _dot(sol[:, dh:].astype(BF16), sb)).astype(BF16) for sol, sb in zip(sols, sbs)]
        for h in heads:
            o_ref[:, sl[h]] = _dot(zs[h]["qg"].astype(BF16), sbs[h]) + _dot(zs[h]["att"].astype(BF16), vnbs[h])
        for h in heads:
            ss_ref[0, h] = ss[h]
            ts_ref[0, h] = t_invs[h]
            s_ref[h] = ss[h] * jnp.exp(zs[h]["gam_l"]) + _dot(zs[h]["kt"].astype(BF16), vnbs[h], tn_dims)

    blk = pl.BlockSpec((c, DN_WIDTH), lambda ci: (ci, 0))
    sav = pl.BlockSpec((1, DN_HEADS, dh, dh), lambda ci: (ci, 0, 0, 0))
    return _pallas(
        body, name=name, grid=(nc,),
        in_specs=[blk, blk, blk, pl.BlockSpec((c, 128), lambda ci: (ci, 0)), pl.BlockSpec((16, c), lambda ci: (0, ci))],
        out_specs=[blk, sav, sav],
        out_shape=[jax.ShapeDtypeStruct((r, DN_WIDTH), F32), jax.ShapeDtypeStruct((nc, DN_HEADS, dh, dh), F32),
                   jax.ShapeDtypeStruct((nc, DN_HEADS, dh, dh), F32)],
        scratch_shapes=[pltpu.VMEM((DN_HEADS, dh, dh), F32)],
        compiler_params=_cparams(("arbitrary",)),
    )(q, k, v, gb, gbt)


def _dn_bwd(q, k, v, gb, gbt, ssave, tsave, do, name):
    r = q.shape[0]
    c = DN_CHUNK
    nc = r // c
    dh = DN_HEAD_DIM
    nt_dims = ((1,), (1,))
    tn_dims = ((0,), (0,))

    def body(q_ref, k_ref, v_ref, gb_ref, gbt_ref, ss_ref, ts_ref, do_ref, dq_ref, dk_ref, dv_ref, dgb_ref, ds_ref):
        @pl.when(pl.program_id(0) == 0)
        def _():
            ds_ref[...] = jnp.zeros_like(ds_ref)

        heads = list(range(DN_HEADS))
        sl = [slice(h * dh, (h + 1) * dh) for h in heads]
        sh = _dn_chunk_shared(gb_ref, gbt_ref)
        row, col, lane = sh["row"], sh["col"], sh["lane"]
        rs = lambda x: jnp.sum(x, axis=1, keepdims=True)
        tot = lambda x: jnp.sum(rs(x), axis=0, keepdims=True)
        st = [dict() for _ in heads]
        dgb_parts = []

        def s_common(h):
            st[h].update(_dn_chunk_common(q_ref[:, sl[h]], k_ref[:, sl[h]], v_ref[:, sl[h]], sh, h))
            st[h]["t"] = _split3(ts_ref[0, h])

        def s_sol(h):
            st[h]["sol"] = _dot3s(st[h]["t"], _split3(st[h]["rhs"]))

        def s_state(h):
            z = st[h]
            sol = z["sol"]
            kcd = sol[:, dh:]
            s = ss_ref[0, h]
            sb = s.astype(BF16)
            vnb = (sol[:, :dh] - _dot(kcd.astype(BF16), sb)).astype(BF16)
            ds_next = ds_ref[h]
            dsb = ds_next.astype(BF16)
            dob = do_ref[:, sl[h]].astype(BF16)
            z["dqg"] = _dot(dob, sb, nt_dims)
            ds = _dot(z["qg"].astype(BF16), dob, tn_dims)
            z["d_att"] = jnp.where(row >= col, _dot(dob, vnb, nt_dims), 0.0)
            dvn = _dot(z["att"].astype(BF16), dob, tn_dims) + _dot(z["kt"].astype(BF16), dsb)
            z["dkt"] = _dot(vnb, dsb, nt_dims)
            eg_l = jnp.exp(z["gam_l"])
            ds = ds + ds_next * eg_l
            z["dgam_l"] = tot(ds_next * s) * eg_l
            dvnb = dvn.astype(BF16)
            dkcd = -_dot(dvnb, sb, nt_dims)
            ds_ref[h] = ds - _dot(kcd.astype(BF16), dvnb, tn_dims)
            z["dsol"] = jnp.concatenate([dvn, dkcd], axis=1)

        def s_drhs(h):
            st[h]["drhs"] = _dot3s(st[h]["t"], _split3(st[h]["dsol"]), tn_dims)

        def s_dn(h):
            z = st[h]
            z["dn"] = jnp.where(row > col, -_dot3(z["drhs"], z["sol"], nt_dims), 0.0)

        def s_rest(h):
            z = st[h]
            k_, v_, kb, qb = z["k"], z["v"], z["kb"], z["qb"]
            beta, eg, dec, kk, qk, gam, gam_l = z["beta"], z["eg"], z["dec"], z["kk"], z["qk"], z["gam"], z["gam_l"]
            dn, d_att, dqg, dkt = z["dn"], z["d_att"], z["dqg"], z["dkt"]
            drv, drk = z["drhs"][:, :dh], z["drhs"][:, dh:]
            s_rkk = rs(drk * k_)
            dv_ref[:, sl[h]] = drv * beta
            dbeta = rs(drv * v_) + s_rkk * eg + rs(dn * kk * dec)
            dk = drk * (beta * eg)
            dgam = s_rkk * beta * eg
            dkk = (dn * beta * dec).astype(BF16)
            dd = dn * beta * kk + d_att * qk
            dqk = (d_att * dec).astype(BF16)
            dq_ref[:, sl[h]] = _dot(dqk, kb) + dqg * eg
            dk = dk + _dot(dqk, qb, tn_dims) + _dot(dkk, kb) + _dot(dkk, kb, tn_dims)
            w = dd * dec
            wh, wl = _split3(w)
            ones = jnp.ones((c, 128), BF16)
            col_sum = (_dot(wh, ones, tn_dims) + _dot(wl, ones, tn_dims))[:, 0:1]
            dgam = dgam + rs(w) - col_sum + rs(dqg * z["qg"]) - rs(dkt * z["kt"])
            dk_ref[:, sl[h]] = dk + dkt * jnp.exp(gam_l - gam)
            dgam_l = z["dgam_l"] + tot(dkt * z["kt"])
            rowc = lax.broadcasted_iota(jnp.int32, (c, 1), 0)
            dgam = dgam + jnp.where(rowc == c - 1, dgam_l, 0.0)
            dg = _split_dot((row <= col).astype(BF16), jnp.broadcast_to(dgam, (c, 128)))[:, 0:1]
            dgb_parts.append(jnp.where(lane == h, dg, 0.0) + jnp.where(lane == h + DN_HEADS, dbeta, 0.0))

        _emit_chains(heads, [s_common, s_sol, s_state, s_drhs, s_dn, s_rest], False)
        dgb = dgb_parts[0]
        for part in dgb_parts[1:]:
            dgb = dgb + part
        dgb_ref[...] = dgb

    blk = pl.BlockSpec((c, DN_WIDTH), lambda ci: (nc - 1 - ci, 0))
    sav = pl.BlockSpec((1, DN_HEADS, dh, dh), lambda ci: (nc - 1 - ci, 0, 0, 0))
    gspec = pl.BlockSpec((c, 128), lambda ci: (nc - 1 - ci, 0))
    return _pallas(
        body, name=name, grid=(nc,),
        in_specs=[blk, blk, blk, gspec, pl.BlockSpec((16, c), lambda ci: (0, nc - 1 - ci)), sav, sav, blk],
        out_specs=[blk, blk, blk, gspec],
        out_shape=[jax.ShapeDtypeStruct((r, DN_WIDTH), F32)] * 3 + [jax.ShapeDtypeStruct((r, 128), F32)],
        scratch_shapes=[pltpu.VMEM((DN_HEADS, dh, dh), F32)],
        compiler_params=_cparams(("arbitrary",)),
    )(q, k, v, gb, gbt, ssave, tsave, do)


def _dn_post_fwd(o, proj, g, name):
    r = o.shape[0]

    def body(o_ref, z_ref, g_ref, y_ref):
        g_ = g_ref[...]
        for hd in range(DN_HEADS):
            sl = slice(hd * 128, (hd + 1) * 128)
            sz, _ = _silu(z_ref[:, sl])
            y_ref[:, sl] = (_rms(o_ref[:, sl], g_) * sz).astype(BF16)

    return _pallas(body, name=name, grid=(r // ROW_TILE,),
                   in_specs=[_row_spec(DN_WIDTH), pl.BlockSpec((ROW_TILE, DN_WIDTH), lambda i: (i, 3)), _vec_spec(128)],
                   out_specs=_row_spec(DN_WIDTH), out_shape=jax.ShapeDtypeStruct((r, DN_WIDTH), BF16),
                   compiler_params=_cparams(("parallel",)))(o, proj, g)


def _dn_post_bwd(o, proj, g, dy, name):
    r = o.shape[0]

    def body(o_ref, z_ref, g_ref, dy_ref, do_ref, dz_ref, dg_ref):
        @pl.when(pl.program_id(0) == 0)
        def _():
            dg_ref[...] = jnp.zeros_like(dg_ref)

        g_ = g_ref[...]
        for hd in range(DN_HEADS):
            sl = slice(hd * 128, (hd + 1) * 128)
            z_ = z_ref[:, sl]
            sz, sg = _silu(z_)
            dy_ = dy_ref[:, sl]
            o_ = o_ref[:, sl]
            dz_ref[:, sl] = (dy_ * _rms(o_, g_) * (sg * (1.0 + z_ * (1.0 - sg)))).astype(BF16)
            dx, dg = _rms_bwd(o_, g_, dy_ * sz)
            do_ref[:, sl] = dx
            dg_ref[...] += dg

    return _pallas(body, name=name, grid=(r // ROW_TILE,),
                   in_specs=[_row_spec(DN_WIDTH), pl.BlockSpec((ROW_TILE, DN_WIDTH), lambda i: (i, 3)), _vec_spec(128),
                             _row_spec(DN_WIDTH)],
                   out_specs=[_row_spec(DN_WIDTH), pl.BlockSpec((ROW_TILE, DN_WIDTH), lambda i: (i, 3)), _vec_spec(128)],
                   out_shape=[jax.ShapeDtypeStruct((r, DN_WIDTH), F32), jax.ShapeDtypeStruct((r, 4 * DN_WIDTH), BF16),
                              jax.ShapeDtypeStruct((1, 128), F32)],
                   compiler_params=_cparams(("arbitrary",)))(o, proj, g, dy)


def _exchange(arrays, scatter, name):
    n = len(arrays)

    def body(*refs):
        copies = _exchange_copies(refs[:n], refs[n:2 * n], scatter, *refs[2 * n:])
        for cp in copies:
            cp.start()
        for cp in copies:
            cp.wait()

    hbm = pl.BlockSpec(memory_space=pl.ANY)
    return _pallas(
        body, name=name, in_specs=[hbm] * n, out_specs=[hbm] * n, out_shape=_exchange_shapes(arrays, scatter),
        scratch_shapes=_exchange_sems(n),
    )(*arrays)


def _exchange_shapes(arrays, scatter):
    return [jax.ShapeDtypeStruct((N_DEV,) + (a.shape[1:] if sc else a.shape), a.dtype) for a, sc in zip(arrays, scatter)]


def _exchange_sems(n):
    return [pltpu.SemaphoreType.DMA((n * N_DEV,)), pltpu.SemaphoreType.DMA((n * N_DEV,)), pltpu.SemaphoreType.DMA((n,))]


def _exchange_copies(in_refs, out_refs, scatter, send_sems, recv_sems, local_sems):
    mx, my, mc = lax.axis_index("x"), lax.axis_index("y"), lax.axis_index("c")
    me = 4 * mx + 2 * my + mc
    copies = []
    for a in range(len(in_refs)):
        src_own = in_refs[a].at[me] if scatter[a] else in_refs[a]
        copies.append(pltpu.make_async_copy(src_own, out_refs[a].at[me], local_sems.at[a]))
        for kbits in range(1, N_DEV):
            px = lax.rem(mx + ((kbits >> 2) & 1), 2)
            py = lax.rem(my + ((kbits >> 1) & 1), 2)
            pc = lax.rem(mc + (kbits & 1), 2)
            src = in_refs[a].at[4 * px + 2 * py + pc] if scatter[a] else in_refs[a]
            copies.append(pltpu.make_async_remote_copy(
                src_ref=src, dst_ref=out_refs[a].at[me],
                send_sem=send_sems.at[a * N_DEV + kbits], recv_sem=recv_sems.at[a * N_DEV + kbits],
                device_id=(px, py, pc), device_id_type=pl.DeviceIdType.MESH))
    return copies


def _adamw(gstack, w, m, v, name):
    a, b = w.shape
    ta = a
    for t in (1024, 512, 256, 128, 64, 32, 16, 8):
        if a % t == 0 and N_DEV * t * b * 4 <= 4 * 1024 * 1024:
            ta = t
            break
    c1 = 1.0 / (1.0 - ADAM_B1 ** ADAM_STEP)
    c2 = 1.0 / (1.0 - ADAM_B2 ** ADAM_STEP)

    def body(g_ref, w_ref, m_ref, v_ref, og_ref, od_ref, om_ref, ov_ref):
        g = g_ref[0].astype(F32)
        for s in range(1, N_DEV):
            g = g + g_ref[s].astype(F32)
        m_new = ADAM_B1 * m_ref[...] + (1.0 - ADAM_B1) * g
        v_new = ADAM_B2 * v_ref[...] + (1.0 - ADAM_B2) * (g * g)
        og_ref[...] = g
        om_ref[...] = m_new
        ov_ref[...] = v_new
        od_ref[...] = -ADAM_LR * ((m_new * c1) / (jnp.sqrt(v_new * c2) + ADAM_EPS) + ADAM_WD * w_ref[...])

    spec = pl.BlockSpec((ta, b), lambda i: (i, 0))
    return _pallas(
        body, name=name, grid=(a // ta,),
        in_specs=[pl.BlockSpec((N_DEV, ta, b), lambda i: (0, i, 0)), spec, spec, spec],
        out_specs=[spec] * 4, out_shape=[jax.ShapeDtypeStruct((a, b), F32)] * 4,
        compiler_params=_cparams(("parallel",)),
    )(gstack, w, m, v)


_WEIGHTS = ['meta_tokens', 'pre_mix_norm', 'post_mix_norm', 'pre_mlp_norm', 'post_mlp_norm', 'mlp_w1', 'mlp_w2',
            'w_in_even', 'w_out_even', 'sb_out_norm', 's5_lambda_re', 's5_lambda_im', 's5_log_dt', 's5_b_re', 's5_b_im',
            's5_c_re', 's5_c_im', 's5_d', 's5_w_glu', 's5_b_glu', 's5_out_norm', 'w_in_odd', 'dn_conv_w', 'dn_a_log',
            'dn_dt_bias', 'dn_out_norm', 'w_out_odd']
_SHARDED = ['meta_tokens', 'mlp_w1', 'mlp_w2', 'w_in_even', 'w_out_even', 's5_w_glu', 'w_in_odd', 'dn_conv_w', 'w_out_odd']
_SMALL = [n for n in _WEIGHTS if n not in _SHARDED]
_GATHER_FIRST = ['meta_tokens', 'w_in_even', 's5_w_glu', 'w_out_even']
_GATHER_LATE = [n for n in _SHARDED if n not in _GATHER_FIRST]
_REDUCE_EARLY = ['mlp_w1', 'mlp_w2', 'w_in_odd', 'dn_conv_w', 'w_out_odd', 'w_out_even']


def _view2d(name, a):
    return a.reshape(-1, a.shape[-1])


def _unshard(name, g):
    if name == 'mlp_w1':
        return g.reshape(N_DEV, 2, D_MODEL, -1).transpose(1, 2, 0, 3).reshape(2, D_MODEL, D_FF)
    if name == 'mlp_w2':
        return g.reshape(N_DEV, 2, -1, D_MODEL).transpose(1, 0, 2, 3).reshape(2, D_FF, D_MODEL)
    if name in ('w_in_even', 'w_in_odd', 'dn_conv_w', 'meta_tokens'):
        return g.transpose(1, 0, 2).reshape(g.shape[1], -1)
    return g.reshape(-1, g.shape[-1])


def _to_blocks(name, full):
    if name == 'mlp_w1':
        return full.reshape(2, D_MODEL, N_DEV, -1).transpose(2, 0, 1, 3).reshape(N_DEV, 2 * D_MODEL, -1)
    if name == 'mlp_w2':
        return full.reshape(2, N_DEV, -1, D_MODEL).transpose(1, 0, 2, 3).reshape(N_DEV, -1, D_MODEL)
    if name in ('w_in_even', 'w_in_odd', 'dn_conv_w', 'meta_tokens'):
        return full.reshape(full.shape[0], N_DEV, -1).transpose(1, 0, 2)
    return full.reshape(N_DEV, -1, full.shape[-1])


def _pack(parts):
    rows = []
    for p in parts:
        flat = p.reshape(-1)
        rows.append(jnp.pad(flat, (0, (-flat.shape[0]) % 128)).reshape(-1, 128))
    return jnp.concatenate(rows, axis=0)


def _unpack(packed, like):
    out, at = [], 0
    for p in like:
        n = math.prod(p.shape)
        nrow = -(-n // 128)
        out.append(packed[at:at + nrow].reshape(-1)[:n].reshape(p.shape))
        at += nrow
    return out


def _lane_vec(x, width=128):
    flat = x.reshape(-1)
    return jnp.pad(flat, (0, width - flat.shape[0])).reshape(1, width)


def kernel(x, meta_tokens, pre_mix_norm, post_mix_norm, pre_mlp_norm, post_mlp_norm, mlp_w1, mlp_w2, w_in_even, w_out_even, sb_out_norm, s5_lambda_re, s5_lambda_im, s5_log_dt, s5_b_re, s5_b_im, s5_c_re, s5_c_im, s5_d, s5_w_glu, s5_b_glu, s5_out_norm, w_in_odd, dn_conv_w, dn_a_log, dn_dt_bias, dn_out_norm, w_out_odd, loss_target, m_meta_tokens, m_pre_mix_norm, m_post_mix_norm, m_pre_mlp_norm, m_post_mlp_norm, m_mlp_w1, m_mlp_w2, m_w_in_even, m_w_out_even, m_sb_out_norm, m_s5_lambda_re, m_s5_lambda_im, m_s5_log_dt, m_s5_b_re, m_s5_b_im, m_s5_c_re, m_s5_c_im, m_s5_d, m_s5_w_glu, m_s5_b_glu, m_s5_out_norm, m_w_in_odd, m_dn_conv_w, m_dn_a_log, m_dn_dt_bias, m_dn_out_norm, m_w_out_odd, v_meta_tokens, v_pre_mix_norm, v_post_mix_norm, v_pre_mlp_norm, v_post_mlp_norm, v_mlp_w1, v_mlp_w2, v_w_in_even, v_w_out_even, v_sb_out_norm, v_s5_lambda_re, v_s5_lambda_im, v_s5_log_dt, v_s5_b_re, v_s5_b_im, v_s5_c_re, v_s5_c_im, v_s5_d, v_s5_w_glu, v_s5_b_glu, v_s5_out_norm, v_w_in_odd, v_dn_conv_w, v_dn_a_log, v_dn_dt_bias, v_dn_out_norm, v_w_out_odd):
    given = dict(locals())
    w = {n: given[n] for n in _WEIGHTS}
    mom_m = {n: given["m_" + n] for n in _WEIGHTS}
    mom_v = {n: given["v_" + n] for n in _WEIGHTS}

    seq = x.shape[1]
    assert x.shape[0] == 1 and seq % ROW_TILE == 0
    r = seq + ROW_TILE
    pad = ROW_TILE - N_META

    wire = {n: (F32 if n in ('dn_conv_w', 'meta_tokens') else BF16) for n in _SHARDED}
    shard_wire = lambda n: _view2d(n, w[n]).astype(wire[n])
    gathered = _exchange([shard_wire(n) for n in _GATHER_FIRST], [False] * len(_GATHER_FIRST), "gather_first")
    full = {n: _unshard(n, g_) for n, g_ in zip(_GATHER_FIRST, gathered)}
    w_ie, w_oe, w_glu = full['w_in_even'], full['w_out_even'], full['s5_w_glu']
    row = lambda v_: v_.reshape(1, -1)

    hs0 = jnp.concatenate([jnp.zeros((pad, D_MODEL), F32), full['meta_tokens'], x[0]], axis=0)
    hn0 = _norm_pre(hs0, row(pre_mix_norm[0]), "pre_mix_0")
    qkv = _mm_fwd(hn0, w_ie[:, :3 * SB_WIDTH], "in_even_qkv", out_dtypes=(BF16,))
    u = _mm_fwd(hn0, w_ie[:, 3 * SB_WIDTH:], "in_even_u")
    q, k, v = qkv[:, :SB_WIDTH], qkv[:, SB_WIDTH:2 * SB_WIDTH], qkv[:, 2 * SB_WIDTH:]
    nb = r // ATT_BLK
    blocks_t = lambda t_: t_.reshape(nb, ATT_BLK, 4, 128).transpose(2, 0, 3, 1)
    o_sb, ssave, gathered = _sb_fwd(q, k, blocks_t(v), pad, "sb_fwd",
                                    ride=([shard_wire(n) for n in _GATHER_LATE], [False] * len(_GATHER_LATE)))
    full.update({n: _unshard(n, g_) for n, g_ in zip(_GATHER_LATE, gathered)})
    w1, w2, w_oo, conv_w = full['mlp_w1'], full['mlp_w2'], full['w_out_odd'], full['dn_conv_w']
    w_io = full['w_in_odd'][:, :4 * DN_WIDTH]
    w_ab = jnp.pad(full['w_in_odd'][:, 4 * DN_WIDTH:], ((0, 0), (0, 128 - 2 * DN_HEADS)))
    on_sb = _norm_pre(o_sb, row(sb_out_norm[0]), "sb_out_norm")

    lam_re, lam_im, logdt, btr, bti, ctr, cti, s5_mask = _s5_expand(
        s5_lambda_re[0], s5_lambda_im[0], s5_log_dt[0], s5_b_re[0], s5_b_im[0], s5_c_re[0], s5_c_im[0])
    a_re, a_im, bbr, bbi = _s5_prep(lam_re, lam_im, logdt, btr, bti, "s5_prep")
    s5_wb = jnp.stack([_s5_block_diag_b(bbr, s5_mask), _s5_block_diag_b(bbi, s5_mask)]).astype(BF16)
    s5_wc = jnp.stack([_s5_block_diag_c(ctr, s5_mask), _s5_block_diag_c(cti, s5_mask)]).astype(BF16)
    s5_a = jnp.stack([a_re, a_im])
    s5_args = (s5_wb, s5_a, s5_wc, row(s5_d[0]), w_glu, row(s5_b_glu[0]), row(s5_out_norm[0]))
    y_s5, on_s5, xstart = _s5_fwd(u, *s5_args, "s5_fwd")

    merged = jnp.concatenate([on_sb, on_s5], axis=1)
    mix0, hs1, hn1 = _mm_norm_fwd(merged, w_oe, hs0, row(post_mix_norm[0]), g_pre=row(pre_mlp_norm[0]), name="out_even")
    relu2 = lambda acc: (jnp.square(jnp.maximum(acc, 0.0)), jnp.maximum(acc, 0.0))
    r0, ra0 = _mm_fwd(hn1, w1[0], "mlp_up_0", out_dtypes=(BF16, BF16), epilogue=relu2)
    m0, hs2, hn2 = _mm_norm_fwd(r0, w2[0], hs1, row(post_mlp_norm[0]), g_pre=row(pre_mix_norm[1]), name="mlp_down_0")

    proj = _mm_fwd(hn2, w_io, "in_odd")
    ab = _mm_fwd(hn2, w_ab, "in_odd_gates")
    alog, dtb = _lane_vec(dn_a_log[0]), _lane_vec(dn_dt_bias[0])
    qd, kd, vd, gb = _dn_pre_fwd(proj, ab, conv_w, alog, dtb, pad, "dn_pre")
    gbt = gb[:, :2 * DN_HEADS].T
    o_dn, s_dn, t_dn = _dn_fwd(qd, kd, vd, gb, gbt, "dn_fwd")
    on_dn = _dn_post_fwd(o_dn, proj, row(dn_out_norm[0]), "dn_post")
    mix1, hs3, hn3 = _mm_norm_fwd(on_dn, w_oo, hs2, row(post_mix_norm[1]), g_pre=row(pre_mlp_norm[1]), name="out_odd")
    r1, ra1 = _mm_fwd(hn3, w1[1], "mlp_up_1", out_dtypes=(BF16, BF16), epilogue=relu2)
    dhs, dm1, dg_post_mlp1, loss_part = _mm_norm_fwd(r1, w2[1], hs3, row(post_mlp_norm[1]),
                                                     loss=(loss_target[0], pad + N_META), name="mlp_down_1_loss")
    loss = lax.psum(loss_part, ("x", "y", "c"))

    g = {}
    drelu2 = lambda acc, ra: (acc * (2.0 * ra.astype(F32)),)

    def mlp_bwd(layer, hn, rr, ra, dm):
        dw2 = _mm_wgrad(rr, dm, f"mlp_down_{layer}_wgrad")
        da = _mm_dgrad(dm, w2[layer], f"mlp_down_{layer}_dgrad", out_dtypes=(BF16,), extras=(ra,), epilogue=drelu2)
        dw1 = _mm_wgrad(hn, da, f"mlp_up_{layer}_wgrad")
        return dw1, dw2, da

    dw1_1, dw2_1, da1 = mlp_bwd(1, hn3, r1, ra1, dm1)
    dhs, dmix1, dg_pre_mlp1, dg_post_mix1 = _dgrad_norm_bwd(
        da1, w1[1], dhs, hs3, row(pre_mlp_norm[1]), post=(mix1, row(post_mix_norm[1])), pad=pad, name="post_mix_1_bwd")

    g['w_out_odd'] = _mm_wgrad(on_dn, dmix1, "out_odd_wgrad")
    d_on_dn = _mm_dgrad(dmix1, w_oo, "out_odd_dgrad")
    do_dn, dproj, dg_dn = _dn_post_bwd(o_dn, proj, row(dn_out_norm[0]), d_on_dn, "dn_post_bwd")
    dqd, dkd, dvd, dgb = _dn_bwd(qd, kd, vd, gb, gbt, s_dn, t_dn, do_dn, "dn_bwd")
    dco, dab, d_alog, d_dtb = _dn_pre_bwd(proj, conv_w, dqd, dkd, dvd, dgb, ab, alog, dtb, pad, "dn_pre_bwd")
    dproj, d_conv = _dn_conv_bwd(dco, proj, conv_w, dproj, "dn_conv_bwd")
    g['w_in_odd'] = jnp.concatenate([_mm_wgrad(hn2, dproj, "in_odd_wgrad"),
                                     _mm_wgrad(hn2, dab, "in_odd_gates_wgrad")[:, :2 * DN_HEADS]], axis=1)
    dhn2_gates = _mm_dgrad(dab, w_ab, "in_odd_gates_dgrad")
    g['dn_conv_w'] = d_conv[:DN_CONV]
    g['dn_a_log'], g['dn_dt_bias'], g['dn_out_norm'] = d_alog[0, :DN_HEADS], d_dtb[0, :DN_HEADS], dg_dn[0]

    dhs, dm0, dg_pre_mix1, dg_post_mlp0 = _dgrad_norm_bwd(
        dproj, w_io, dhs, hs2, row(pre_mix_norm[1]), post=(m0, row(post_mlp_norm[0])), add=dhn2_gates, pad=pad,
        name="post_mlp_0_bwd")
    dw1_0, dw2_0, da0 = mlp_bwd(0, hn1, r0, ra0, dm0)
    dhs, dmix0, dg_pre_mlp0, dg_post_mix0 = _dgrad_norm_bwd(
        da0, w1[0], dhs, hs1, row(pre_mlp_norm[0]), post=(mix0, row(post_mix_norm[0])), pad=pad, name="post_mix_0_bwd")

    g['w_out_even'] = _mm_wgrad(merged, dmix0, "out_even_wgrad")
    dmerged = _mm_dgrad(dmix0, w_oe, "out_even_dgrad")
    _, do_sb, _, dg_sb = _norm_bwd(dmerged, post=(o_sb, row(sb_out_norm[0])), pad=pad, dm_dtype=F32,
                                   dhs_cols=(SB_WIDTH, 0), name="sb_out_norm_bwd")
    dq, dk4, dv4 = _sb_bwd(q, k, v, blocks_t(k), ssave, do_sb, pad, "sb_bwd")
    unheads = lambda t_: t_.transpose(1, 0, 2).reshape(r, SB_WIDTH)
    g['mlp_w1'] = jnp.stack([dw1_0, dw1_1])
    g['mlp_w2'] = jnp.stack([dw2_0, dw2_1])
    grad_wire = lambda n: _to_blocks(n, g[n].reshape(full[n].shape)).astype(wire[n])
    du, d_a, d_d, d_bglu, dg_s5, d_wb, d_wc, g['s5_w_glu'], reduced = _s5_bwd(
        u, y_s5, dmerged, xstart, *s5_args, "s5_bwd", don_block=1,
        ride=([grad_wire(n) for n in _REDUCE_EARLY], [True] * len(_REDUCE_EARLY)))
    stacks = dict(zip(_REDUCE_EARLY, reduced))
    g_lr, g_li, g_dt, g_btr, g_bti = _s5_prep_bwd(
        lam_re, lam_im, logdt, btr, bti, d_a[0], d_a[1],
        _s5_diag_of_b(d_wb[0], s5_mask), _s5_diag_of_b(d_wb[1], s5_mask), "s5_prep_bwd")
    gg, nn, pp = S5_GROUPS, S5_STATE, S5_GROUP
    g['s5_lambda_re'], g['s5_lambda_im'] = g_lr.reshape(gg, nn), g_li.reshape(gg, nn)
    g['s5_log_dt'] = g_dt.reshape(gg, nn)[:, 0]
    g['s5_b_re'], g['s5_b_im'] = g_btr.T.reshape(gg, nn, pp), g_bti.T.reshape(gg, nn, pp)
    g['s5_c_re'] = _s5_diag_of_c(d_wc[0], s5_mask).reshape(gg, nn, pp).transpose(0, 2, 1)
    g['s5_c_im'] = _s5_diag_of_c(d_wc[1], s5_mask).reshape(gg, nn, pp).transpose(0, 2, 1)
    g['s5_d'], g['s5_b_glu'], g['s5_out_norm'], g['sb_out_norm'] = d_d[0], d_bglu[0], dg_s5[0], dg_sb[0]
    dqkvu = jnp.concatenate([dq, unheads(dk4), unheads(dv4), du], axis=1).astype(BF16)
    g['w_in_even'] = _mm_wgrad(hn0, dqkvu, "in_even_wgrad")
    dhs, _, dg_pre_mix0, _ = _dgrad_norm_bwd(dqkvu, w_ie, dhs, hs0, row(pre_mix_norm[0]), pad=pad, name="pre_mix_0_bwd")

    g['meta_tokens'] = dhs[pad:pad + N_META]
    g['pre_mix_norm'] = jnp.concatenate([dg_pre_mix0, dg_pre_mix1], axis=0)
    g['post_mix_norm'] = jnp.concatenate([dg_post_mix0, dg_post_mix1], axis=0)
    g['pre_mlp_norm'] = jnp.concatenate([dg_pre_mlp0, dg_pre_mlp1], axis=0)
    g['post_mlp_norm'] = jnp.concatenate([dg_post_mlp0, dg_post_mlp1], axis=0)
    grad_x = dhs[pad + N_META:][None]

    small_like = [w[n] for n in _SMALL]
    last = [n for n in _SHARDED if n not in _REDUCE_EARLY]
    partial = [grad_wire(n) for n in last] + [_pack([g[n].reshape(w[n].shape) for n in _SMALL])]
    reduced = _exchange(partial, [True] * len(last) + [False], "reduce_last")
    stacks.update(zip(last, reduced[:-1]))
    grads, deltas, new_m, new_v = {}, {}, {}, {}
    for n in _SHARDED:
        outs = _adamw(stacks[n], _view2d(n, w[n]), _view2d(n, mom_m[n]), _view2d(n, mom_v[n]), f"adamw_{n}")
        grads[n], deltas[n], new_m[n], new_v[n] = (o.reshape(w[n].shape) for o in outs)
    outs = _adamw(reduced[-1], _pack(small_like), _pack([mom_m[n] for n in _SMALL]), _pack([mom_v[n] for n in _SMALL]),
                  "adamw_small")
    for dst, o in zip((grads, deltas, new_m, new_v), outs):
        for n, part in zip(_SMALL, _unpack(o, small_like)):
            dst[n] = part
    return (loss, grad_x, *[grads[n] for n in _WEIGHTS], *[deltas[n] for n in _WEIGHTS],
            *[new_m[n] for n in _WEIGHTS], *[new_v[n] for n in _WEIGHTS])
```

```python
import math

import jax
import jax.numpy as jnp
from jax import lax
from jax.experimental import pallas as pl
from jax.experimental.pallas import tpu as pltpu

F32 = jnp.float32
BF16 = jnp.bfloat16

D_MODEL = 1024
N_META = 16
SB_HEAD_DIM = 64
SB_WIDTH = 512
S5_WIDTH = 512
S5_GROUP = 16
S5_GROUPS = 32
S5_STATE = 64
S5_NS = S5_GROUPS * S5_STATE
DN_HEAD_DIM = 128
DN_HEADS = 8
DN_WIDTH = 1024
DN_CONV = 4
D_FF = 4096
EPS = 1e-6
N_DEV = 8

ADAM_LR = 0.001
ADAM_B1 = 0.9
ADAM_B2 = 0.999
ADAM_EPS = 1e-08
ADAM_WD = 0.01
ADAM_STEP = 10

ROW_TILE = 512
ATT_BLK = 256
SB_BLOCKS_PER_TRIP = 3
SB_LOG_ZERO = -106.0
SB_FWD_SKEW = False
SB_BWD_SKEW = True
DN_CHUNK = 128
DN_SUB = 16
S5_TILE = 128
S5_CHUNKS = 4
VMEM_LIMIT = 56 * 1024 * 1024

_HIGH = lax.Precision.HIGHEST


def _pallas(body, **kw):
    return pl.pallas_call(body, **kw)


def _cparams(sem):
    return pltpu.CompilerParams(dimension_semantics=sem, vmem_limit_bytes=VMEM_LIMIT)


def _dot(a, b, dims=((1,), (0,))):
    return lax.dot_general(a, b, (dims, ((), ())), preferred_element_type=F32)


def _dot_hi(a, b):
    return lax.dot_general(a, b, (((1,), (0,)), ((), ())), preferred_element_type=F32, precision=_HIGH)


def _split_dot(m_bf16, x):
    hi = x.astype(BF16)
    lo = (x - hi.astype(F32)).astype(BF16)
    return _dot(m_bf16, hi) + _dot(m_bf16, lo)


def _matmul(a, b, *, ta=False, tb=False, tm, tn, tk, name, out_dtypes=(F32,), extras=(), epilogue=None):
    m, k = (a.shape[1], a.shape[0]) if ta else a.shape
    n = b.shape[0] if tb else b.shape[1]
    assert (b.shape[1] if tb else b.shape[0]) == k
    assert m % tm == 0 and n % tn == 0 and k % tk == 0, (name, m, n, k, tm, tn, tk)
    nk = k // tk
    n_ex = len(extras)
    n_out = len(out_dtypes)
    dims = ((0 if ta else 1,), (1 if tb else 0,))

    def finish(acc, ex_refs, o_refs):
        outs = (acc,) if epilogue is None else epilogue(acc, *[r[...] for r in ex_refs])
        for o_ref, o in zip(o_refs, outs):
            o_ref[...] = o.astype(o_ref.dtype)

    def body(*refs):
        a_ref, b_ref = refs[0], refs[1]
        ex_refs = refs[2:2 + n_ex]
        o_refs = refs[2 + n_ex:2 + n_ex + n_out]
        prod = _dot(a_ref[...].astype(BF16), b_ref[...].astype(BF16), dims)
        if nk == 1:
            finish(prod, ex_refs, o_refs)
            return
        acc_ref = refs[-1]
        kk = pl.program_id(2)

        @pl.when(kk == 0)
        def _():
            acc_ref[...] = prod

        @pl.when(kk > 0)
        def _():
            acc_ref[...] += prod

        @pl.when(kk == nk - 1)
        def _():
            finish(acc_ref[...], ex_refs, o_refs)

    a_spec = pl.BlockSpec((tk, tm), lambda j, i, kk: (kk, i)) if ta else pl.BlockSpec((tm, tk), lambda j, i, kk: (i, kk))
    b_spec = pl.BlockSpec((tn, tk), lambda j, i, kk: (j, kk)) if tb else pl.BlockSpec((tk, tn), lambda j, i, kk: (kk, j))
    o_spec = pl.BlockSpec((tm, tn), lambda j, i, kk: (i, j))
    outs = _pallas(
        body, name=name,
        grid=(n // tn, m // tm, nk),
        in_specs=[a_spec, b_spec] + [o_spec] * n_ex,
        out_specs=[o_spec] * n_out,
        out_shape=[jax.ShapeDtypeStruct((m, n), dt) for dt in out_dtypes],
        scratch_shapes=[] if nk == 1 else [pltpu.VMEM((tm, tn), F32)],
        compiler_params=_cparams(("parallel", "parallel", "arbitrary")),
    )(a, b, *extras)
    return outs[0] if n_out == 1 else outs


def _tile(n, cap):
    best = 128
    for t in range(128, min(n, cap) + 1, 128):
        if n % t == 0:
            best = t
    assert n % best == 0, n
    return best


MM_K_CAP = 4096
WGRAD_ROWS = 1536


MM_LHS_TILE_BYTES = 6 * 1024 * 1024


def _row_tile(x, depth):
    tall = 3 * ROW_TILE
    fits = tall * depth * x.dtype.itemsize <= MM_LHS_TILE_BYTES
    return tall if (x.shape[0] % tall == 0 and fits) else ROW_TILE


def _mm_fwd(x, w, name, **kw):
    k, n = w.shape
    tk = _tile(k, MM_K_CAP)
    return _matmul(x, w, tm=_row_tile(x, tk), tn=_tile(n, 1024), tk=tk, name=name, **kw)


def _mm_dgrad(dy, w, name, **kw):
    k, n = w.shape
    tk = _tile(n, MM_K_CAP)
    return _matmul(dy, w, tb=True, tm=_row_tile(dy, tk), tn=_tile(k, 1024), tk=tk, name=name, **kw)


def _mm_wgrad(x, dy, name):
    k, n = x.shape[1], dy.shape[1]
    rows = x.shape[0]
    return _matmul(x, dy, ta=True, tm=_tile(k, 512), tn=_tile(n, 1024),
                   tk=WGRAD_ROWS if rows % WGRAD_ROWS == 0 else ROW_TILE, name=name)


def _rms(x, g):
    r = lax.rsqrt(jnp.mean(x * x, axis=-1, keepdims=True) + EPS)
    return x * r * g


def _rms_bwd(x, g, dy):
    r = lax.rsqrt(jnp.mean(x * x, axis=-1, keepdims=True) + EPS)
    xh = x * r
    dxh = dy * g
    dx = r * (dxh - xh * jnp.mean(dxh * xh, axis=-1, keepdims=True))
    dg = jnp.sum(dy * xh, axis=0, keepdims=True)
    return dx, dg


def _row_spec(width, tile=ROW_TILE):
    return pl.BlockSpec((tile, width), lambda i: (i, 0))


def _vec_spec(width):
    return pl.BlockSpec((1, width), lambda i: (0, 0))


def _norm_pre(hs, g, name, into=None):
    r, d = hs.shape

    def body(x_ref, g_ref, *rest):
        rest[-1][...] = _rms(x_ref[...], g_ref[...]).astype(BF16)

    if into is None:
        return _pallas(body, name=name, grid=(r // ROW_TILE,), in_specs=[_row_spec(d), _vec_spec(d)],
                       out_specs=_row_spec(d), out_shape=jax.ShapeDtypeStruct((r, d), BF16),
                       compiler_params=_cparams(("parallel",)))(hs, g)
    return _pallas(body, name=name, grid=(r // ROW_TILE,),
                   in_specs=[_row_spec(d), _vec_spec(d), pl.BlockSpec(memory_space=pl.ANY)],
                   out_specs=_row_spec(d), out_shape=jax.ShapeDtypeStruct(into.shape, BF16), input_output_aliases={2: 0},
                   compiler_params=_cparams(("parallel",)))(hs, g, into)


def _mm_norm_fwd(a, w, hs, g_post, *, g_pre=None, loss=None, name):
    k, d = w.shape
    r = a.shape[0]
    assert k <= MM_K_CAP and d == hs.shape[1]
    t = ROW_TILE // 2
    nt = r // t

    def body(*refs):
        a_ref, w_ref, hs_ref, gp_ref = refs[:4]
        i = pl.program_id(0)
        m = _dot(a_ref[...].astype(BF16), w_ref[...].astype(BF16))
        gp = gp_ref[...]
        new = hs_ref[...] + _rms(m, gp)
        if loss is None:
            gn_ref, m_ref, o_ref, hn_ref = refs[4:]
            m_ref[...] = m
            o_ref[...] = new
            hn_ref[...] = _rms(new, gn_ref[...]).astype(BF16)
        else:
            t_ref, dhs_ref, dm_ref, dgp_ref, loss_ref = refs[4:]
            live = (i * t + lax.broadcasted_iota(jnp.int32, (t, 1), 0)) >= loss[1]
            diff = jnp.where(live, new - t_ref[...], 0.0)
            dhs = diff * (1.0 / d)
            dhs_ref[...] = dhs
            loss_ref[...] = jnp.full((8, 128), 0.5 / d * jnp.sum(diff * diff), F32)
            dm, dg = _rms_bwd(m, gp, dhs)
            dm_ref[...] = dm.astype(BF16)

            @pl.when(i == 0)
            def _():
                dgp_ref[...] = jnp.zeros_like(dgp_ref)
            dgp_ref[...] += dg

    common_in = [_row_spec(k, t), pl.BlockSpec((k, d), lambda i: (0, 0)), _row_spec(d, t), _vec_spec(d)]
    if loss is None:
        return _pallas(
            body, name=name, grid=(nt,), in_specs=common_in + [_vec_spec(d)],
            out_specs=[_row_spec(d, t)] * 3,
            out_shape=[jax.ShapeDtypeStruct((r, d), F32), jax.ShapeDtypeStruct((r, d), F32), jax.ShapeDtypeStruct((r, d), BF16)],
            compiler_params=_cparams(("parallel",)))(a, w, hs, g_post, g_pre)
    target, first_row = loss
    assert first_row % t == 0
    dhs, dm, dgp, parts = _pallas(
        body, name=name, grid=(nt,),
        in_specs=common_in + [pl.BlockSpec((t, d), lambda i: (jnp.maximum(i - first_row // t, 0), 0))],
        out_specs=[_row_spec(d, t), _row_spec(d, t), _vec_spec(d), pl.BlockSpec((8, 128), lambda i: (i, 0))],
        out_shape=[jax.ShapeDtypeStruct((r, d), F32), jax.ShapeDtypeStruct((r, d), BF16), jax.ShapeDtypeStruct((1, d), F32),
                   jax.ShapeDtypeStruct((nt * 8, 128), F32)],
        compiler_params=_cparams(("arbitrary",)))(a, w, hs, g_post, target)
    return dhs, dm, dgp, jnp.sum(parts[::8, 0])


def _norm_bwd(dhs, *, pre=None, post=None, pad=0, dm_dtype=BF16, dhs_cols=None, name):
    r = dhs.shape[0]
    d = dhs.shape[1] if dhs_cols is None else dhs_cols[0]
    has_pre, has_post = pre is not None, post is not None

    def body(*refs):
        it = iter(refs)
        dhs_ref = next(it)
        if has_pre:
            hs_ref, gn_ref, dhn_ref = next(it), next(it), next(it)
        if has_post:
            m_ref, gp_ref = next(it), next(it)
        if has_pre:
            o_dhs, o_dgn = next(it), next(it)
        if has_post:
            o_dm, o_dgp = next(it), next(it)
        i = pl.program_id(0)
        live = (i * ROW_TILE + lax.broadcasted_iota(jnp.int32, (ROW_TILE, 1), 0)) >= pad
        cur = jnp.where(live, dhs_ref[...], 0.0)
        if has_pre:
            dx, dg = _rms_bwd(hs_ref[...], gn_ref[...], jnp.where(live, dhn_ref[...].astype(F32), 0.0))
            cur = cur + dx
            o_dhs[...] = cur

            @pl.when(i == 0)
            def _():
                o_dgn[...] = jnp.zeros_like(o_dgn)
            o_dgn[...] += dg
        if has_post:
            dm, dg = _rms_bwd(m_ref[...], gp_ref[...], cur)
            o_dm[...] = dm.astype(o_dm.dtype)

            @pl.when(i == 0)
            def _():
                o_dgp[...] = jnp.zeros_like(o_dgp)
            o_dgp[...] += dg

    dhs_spec = _row_spec(d) if dhs_cols is None else pl.BlockSpec((ROW_TILE, d), lambda i: (i, dhs_cols[1]))
    ins, in_specs, out_specs, out_shape = [dhs], [dhs_spec], [], []
    if has_pre:
        ins += list(pre)
        in_specs += [_row_spec(d), _vec_spec(d), _row_spec(d)]
        out_specs += [_row_spec(d), _vec_spec(d)]
        out_shape += [jax.ShapeDtypeStruct((r, d), F32), jax.ShapeDtypeStruct((1, d), F32)]
    if has_post:
        ins += list(post)
        in_specs += [_row_spec(d), _vec_spec(d)]
        out_specs += [_row_spec(d), _vec_spec(d)]
        out_shape += [jax.ShapeDtypeStruct((r, d), dm_dtype), jax.ShapeDtypeStruct((1, d), F32)]
    outs = list(_pallas(body, name=name, grid=(r // ROW_TILE,), in_specs=in_specs, out_specs=out_specs,
                        out_shape=out_shape, compiler_params=_cparams(("arbitrary",)))(*ins))
    dhs_new, dgn = (outs.pop(0), outs.pop(0)) if has_pre else (dhs, None)
    dm, dgp = (outs.pop(0), outs.pop(0)) if has_post else (None, None)
    return dhs_new, dm, dgn, dgp


def _dgrad_norm_bwd(dy, w, dhs, hs, g_pre, *, post=None, add=None, pad=0, name):
    d, n = w.shape
    r = dy.shape[0]
    assert n <= MM_K_CAP and d == dhs.shape[1]
    t = ROW_TILE // 2
    has_post, has_add = post is not None, add is not None
    dims = ((1,), (1,))

    def body(*refs):
        it = iter(refs)
        dy_ref, w_ref = next(it), next(it)
        add_ref = next(it) if has_add else None
        dhs_ref, hs_ref, gn_ref = next(it), next(it), next(it)
        if has_post:
            m_ref, gp_ref = next(it), next(it)
        o_dhs, o_dgn = next(it), next(it)
        if has_post:
            o_dm, o_dgp = next(it), next(it)
        i = pl.program_id(0)
        dhn = _dot(dy_ref[...].astype(BF16), w_ref[...].astype(BF16), dims)
        if has_add:
            dhn = dhn + add_ref[...]
        live = (i * t + lax.broadcasted_iota(jnp.int32, (t, 1), 0)) >= pad
        dx, dg = _rms_bwd(hs_ref[...], gn_ref[...], jnp.where(live, dhn, 0.0))
        cur = jnp.where(live, dhs_ref[...], 0.0) + dx
        o_dhs[...] = cur

        @pl.when(i == 0)
        def _():
            o_dgn[...] = jnp.zeros_like(o_dgn)
        o_dgn[...] += dg
        if has_post:
            dm, dg = _rms_bwd(m_ref[...], gp_ref[...], cur)
            o_dm[...] = dm.astype(BF16)

            @pl.when(i == 0)
            def _():
                o_dgp[...] = jnp.zeros_like(o_dgp)
            o_dgp[...] += dg

    ins = [dy, w] + ([add] if has_add else []) + [dhs, hs, g_pre] + (list(post) if has_post else [])
    in_specs = ([_row_spec(n, t), pl.BlockSpec((d, n), lambda i: (0, 0))] + ([_row_spec(d, t)] if has_add else [])
                + [_row_spec(d, t), _row_spec(d, t), _vec_spec(d)] + ([_row_spec(d, t), _vec_spec(d)] if has_post else []))
    out_specs = [_row_spec(d, t), _vec_spec(d)] + ([_row_spec(d, t), _vec_spec(d)] if has_post else [])
    out_shape = [jax.ShapeDtypeStruct((r, d), F32), jax.ShapeDtypeStruct((1, d), F32)]
    if has_post:
        out_shape += [jax.ShapeDtypeStruct((r, d), BF16), jax.ShapeDtypeStruct((1, d), F32)]
    outs = list(_pallas(body, name=name, grid=(r // t,), in_specs=in_specs, out_specs=out_specs,
                        out_shape=out_shape, compiler_params=_cparams(("arbitrary",)))(*ins))
    return (outs[0], outs[2], outs[1], outs[3]) if has_post else (outs[0], None, outs[1], None)


def _softplus(z):
    return jnp.maximum(z, 0.0) + jnp.log(1.0 + jnp.exp(-jnp.abs(z)))


def _sb_consts(t):
    row = lax.broadcasted_iota(jnp.int32, (t, t), 0)
    col = lax.broadcasted_iota(jnp.int32, (t, t), 1)
    m_up = (col >= row).astype(BF16)
    m_low = (col <= row).astype(BF16)
    return m_up, m_low


def _emit_chains(chains, stages, skew):
    if skew:
        for step in range(len(chains) + len(stages) - 1):
            for si, stage in enumerate(stages):
                if 0 <= step - si < len(chains):
                    stage(chains[step - si])
    else:
        for stage in stages:
            for c in chains:
                stage(c)


def _sb_fwd(q, k, vt3, pad, name, ride=((), ())):
    r = q.shape[0]
    t = ATT_BLK
    nb = r // t
    nbp = -(-(nb + 1) // 8) * 8
    jmin = pad // t
    scale = SB_HEAD_DIM ** -0.5
    n_ride = len(ride[0])

    def body(q_ref, k_ref, vt_ref, *rest):
        ride_in, (o_ref, ss_ref), ride_out = rest[:n_ride], rest[n_ride:n_ride + 2], rest[n_ride + 2:2 * n_ride + 2]
        acc_ref, kn_ref = rest[2 * n_ride + 2:2 * n_ride + 4]
        ride_sems = rest[2 * n_ride + 4:]
        i = pl.program_id(1)
        if n_ride:
            @pl.when((pl.program_id(0) == 0) & (i == 0))
            def _():
                for cp in _exchange_copies(ride_in, ride_out, ride[1], *ride_sems):
                    cp.start()

        @pl.when(i == 0)
        def _():
            def blk(b, m):
                kb = k_ref[pl.ds(pl.multiple_of(b * t, t), t), :].astype(F32)
                return jnp.maximum(m, jnp.max(jnp.sum(kb * kb, axis=1, keepdims=True), axis=0, keepdims=True))
            kn_ref[...] = jnp.broadcast_to(lax.fori_loop(0, nb, blk, jnp.zeros((1, 1), F32)), (8, 128))

        qf = q_ref[...].astype(F32)
        z_bound = scale * jnp.sqrt(jnp.max(jnp.sum(qf * qf, axis=1, keepdims=True)) * jnp.max(kn_ref[...]))

        def need(carry):
            return jnp.maximum(jnp.max(carry[0]), jnp.max(carry[1])) + z_bound >= SB_LOG_ZERO

        qt = qf.T
        sub = lax.broadcasted_iota(jnp.int32, (128, 1), 0)
        m_up, _ = _sb_consts(t)
        kpos0 = lax.broadcasted_iota(jnp.int32, (t, 1), 0)
        qpos = i * t + lax.broadcasted_iota(jnp.int32, (1, t), 1)
        qths = [jnp.where((sub >= 64 * h) & (sub < 64 * (h + 1)), qt * scale, 0.0).astype(BF16) for h in range(2)]
        acc_ref[...] = jnp.zeros_like(acc_ref)

        def sweep(js, carry, masked):
            kbs = [k_ref[pl.ds(pl.multiple_of(j * t, t), t), :] for j in js]
            vts = [vt_ref[0, j] for j in js]
            accs = [acc_ref[0], acc_ref[1]]
            s = list(carry)
            chains = [(n, h) for n in range(len(js)) for h in range(2)]
            masked = [masked] * len(js) if isinstance(masked, bool) else masked
            valid = [(js[n] * t + kpos0 < qpos) & (js[n] * t + kpos0 >= pad) if masked[n] else None for n in range(len(js))]
            zt, inc, saves = {}, {}, []

            def st_scores(c):
                zt[c] = _dot(kbs[c[0]], qths[c[1]])

            def st_cumsum(c):
                lk = -_softplus(zt[c])
                if masked[c[0]]:
                    lk = jnp.where(valid[c[0]], lk, 0.0)
                inc[c] = _split_dot(m_up, lk)

            def st_weights(c):
                n, h = c
                saves.append((h, js[n], s[h]))
                w = jnp.exp(zt[c] + inc[c] + s[h])
                if masked[n]:
                    w = jnp.where(valid[n], w, 0.0)
                accs[h] = accs[h] + _dot(vts[n], w.astype(BF16))
                s[h] = s[h] + inc[c][0:1, :]

            _emit_chains(chains, [st_scores, st_cumsum, st_weights], SB_FWD_SKEW)
            for h, j, val in saves:
                ss_ref[h, 0, pl.ds(j, 1), :] = val
            acc_ref[0] = accs[0]
            acc_ref[1] = accs[1]
            return tuple(s)

        zero = jnp.zeros((1, t), F32)
        bpi = SB_BLOCKS_PER_TRIP
        j, carry = lax.cond(
            i - 1 > jmin,
            lambda: (i - 2, sweep([i, i - 1], (zero, zero), [True, False])),
            lambda: (i - 1, sweep([i], (zero, zero), True)))
        def further(j, carry):
            j, carry = lax.while_loop(
                lambda st: (st[0] - bpi >= jmin) & need(st[1]),
                lambda st: (st[0] - bpi, sweep([st[0] - b for b in range(bpi)], st[1], False)), (j, carry))
            j, carry = lax.while_loop(
                lambda st: (st[0] > jmin) & need(st[1]),
                lambda st: (st[0] - 1, sweep([st[0]], st[1], False)), (j, carry))
            return lax.while_loop(
                lambda st: (st[0] == jmin) & (i > jmin) & need(st[1]),
                lambda st: (st[0] - 1, sweep([st[0]], st[1], True)), (j, carry))[0]

        j = lax.cond((j >= jmin) & need(carry), lambda: further(j, carry), lambda: j)
        first = jnp.full((1, t), j + 1, jnp.int32).astype(F32)
        ss_ref[0, 0, nbp - 1:nbp, :] = first
        ss_ref[1, 0, nbp - 1:nbp, :] = first
        acc = jnp.where(sub < 64, acc_ref[0], acc_ref[1])
        o_ref[...] = acc.T
        if n_ride:
            @pl.when((pl.program_id(0) == 3) & (i == nb - 1))
            def _():
                for cp in _exchange_copies(ride_in, ride_out, ride[1], *ride_sems):
                    cp.wait()

    hbm = pl.BlockSpec(memory_space=pl.ANY)
    outs = _pallas(
        body, name=name, grid=(4, nb),
        in_specs=[pl.BlockSpec((t, 128), lambda hp, i: (i, hp)),
                  pl.BlockSpec((r, 128), lambda hp, i: (0, hp)),
                  pl.BlockSpec((1, nb, 128, t), lambda hp, i: (hp, 0, 0, 0))] + [hbm] * n_ride,
        out_specs=[pl.BlockSpec((t, 128), lambda hp, i: (i, hp)),
                   pl.BlockSpec((2, 1, nbp, t), lambda hp, i: (hp, i, 0, 0))] + [hbm] * n_ride,
        out_shape=[jax.ShapeDtypeStruct((r, SB_WIDTH), F32),
                   jax.ShapeDtypeStruct((8, nb, nbp, t), F32)] + _exchange_shapes(*ride),
        scratch_shapes=[pltpu.VMEM((2, 128, t), F32), pltpu.VMEM((8, 128), F32)] + (_exchange_sems(n_ride) if n_ride else []),
        compiler_params=_cparams(("arbitrary", "arbitrary")),
    )(q, k, vt3, *ride[0])
    return outs[0], outs[1], list(outs[2:])


def _sb_bwd(q, k, v, kt3, ssave, do, pad, name):
    r = q.shape[0]
    t = ATT_BLK
    nb = r // t
    nbp = ssave.shape[2]
    jmin = pad // t
    scale = SB_HEAD_DIM ** -0.5

    def body(q_ref, do_ref, k_ref, v_ref, kt_ref, ss_ref, dq_ref, dk_hbm, dv_hbm, dk_acc, dv_acc, dq_acc, sem):
        hp = pl.program_id(0)
        i = pl.program_id(1)

        @pl.when(i == 0)
        def _():
            dk_acc[...] = jnp.zeros_like(dk_acc)
            dv_acc[...] = jnp.zeros_like(dv_acc)

        qf = q_ref[...].astype(F32)
        dof = do_ref[...]
        qt = qf.T
        dot_ = dof.T
        sub = lax.broadcasted_iota(jnp.int32, (128, 1), 0)
        lane = lax.broadcasted_iota(jnp.int32, (1, 128), 1)
        m_up, m_low = _sb_consts(t)
        kpos0 = lax.broadcasted_iota(jnp.int32, (t, 1), 0)
        qpos = i * t + lax.broadcasted_iota(jnp.int32, (1, t), 1)
        first = jnp.clip(jnp.max(ss_ref[0, 0, nbp - 1:nbp, :]).astype(jnp.int32), jmin, i)
        mid0 = jnp.maximum(first, jmin + 1)
        pair = i - mid0 >= 1
        n_mid = jnp.maximum(i - mid0 - 1, 0)
        n_edge = jnp.where((i > jmin) & (first == jmin), 1, 0)
        in_t = [(sub >= 64 * h) & (sub < 64 * (h + 1)) for h in range(2)]
        in_l = [(lane >= 64 * h) & (lane < 64 * (h + 1)) for h in range(2)]
        qths = [jnp.where(in_t[h], qt * scale, 0.0).astype(BF16) for h in range(2)]
        doths = [jnp.where(in_t[h], dot_, 0.0).astype(BF16) for h in range(2)]
        qhs = [jnp.where(in_l[h], qf * scale, 0.0).astype(BF16) for h in range(2)]
        dohs = [jnp.where(in_l[h], dof, 0.0).astype(BF16) for h in range(2)]
        dq_acc[...] = jnp.zeros_like(dq_acc)

        def sweep(js, carry, masked):
            rows = [pl.ds(pl.multiple_of(j * t, t), t) for j in js]
            kbs = [k_ref[rw, :] for rw in rows]
            vbs = [v_ref[rw, :] for rw in rows]
            kts = [kt_ref[0, j] for j in js]
            sss = [[ss_ref[h, 0, pl.ds(j, 1), :] for h in range(2)] for j in js]
            dv_old = [dv_acc[rw, :] for rw in rows]
            dk_old = [dk_acc[rw, :] for rw in rows]
            dqs = [dq_acc[0], dq_acc[1]]
            ec = list(carry)
            chains = [(n, h) for n in range(len(js)) for h in range(2)]
            masked = [masked] * len(js) if isinstance(masked, bool) else masked
            valid = [(js[n] * t + kpos0 < qpos) & (js[n] * t + kpos0 >= pad) if masked[n] else None for n in range(len(js))]
            zt, dvt, sp, inc, e, big_e = {}, {}, {}, {}, {}, {}

            def st_scores(c):
                zt[c] = _dot(kbs[c[0]], qths[c[1]])
                dvt[c] = _dot(vbs[c[0]], doths[c[1]])

            def st_cumsum(c):
                sp[c] = _softplus(zt[c])
                lk = -sp[c]
                if masked[c[0]]:
                    lk = jnp.where(valid[c[0]], lk, 0.0)
                inc[c] = _split_dot(m_up, lk)

            def st_weights(c):
                n, h = c
                w = jnp.exp(zt[c] + inc[c] + sss[n][h])
                if masked[n]:
                    w = jnp.where(valid[n], w, 0.0)
                dv_old[n] = dv_old[n] + _dot(w.astype(BF16), dohs[h])
                e[c] = w * dvt[c]
                pinc = _split_dot(m_low, e[c])
                big_e[c] = pinc - e[c] + ec[h]
                ec[h] = ec[h] + pinc[t - 1:t, :]

            def st_dscores(c):
                n, h = c
                dz = e[c] - jnp.exp(zt[c] - sp[c]) * (e[c] + big_e[c])
                if masked[n]:
                    dz = jnp.where(valid[n], dz, 0.0)
                dzb = dz.astype(BF16)
                dqs[h] = dqs[h] + _dot(kts[n], dzb)
                dk_old[n] = dk_old[n] + _dot(dzb, qhs[h])

            _emit_chains(chains, [st_scores, st_cumsum, st_weights, st_dscores], SB_BWD_SKEW)
            for n, rw in enumerate(rows):
                dv_acc[rw, :] = dv_old[n]
                dk_acc[rw, :] = dk_old[n]
            dq_acc[0] = dqs[0]
            dq_acc[1] = dqs[1]
            return tuple(ec)

        zero = jnp.zeros((1, t), F32)
        bpi = SB_BLOCKS_PER_TRIP
        carry = lax.fori_loop(0, n_edge, lambda it, c: sweep([jmin + it * 0], c, True), (zero, zero))
        carry = lax.fori_loop(0, n_mid // bpi, lambda it, c: sweep([mid0 + bpi * it + b for b in range(bpi)], c, False), carry)
        n_rem = n_mid % bpi
        carry = lax.fori_loop(0, n_rem, lambda it, c: sweep([i - 1 - n_rem + it], c, False), carry)
        lax.cond(pair, lambda: sweep([i - 1, i], carry, [False, True]), lambda: sweep([i], carry, True))
        dq_ref[...] = (jnp.where(sub < 64, dq_acc[0], dq_acc[1]) * scale).T

        @pl.when(i == nb - 1)
        def _():
            lanes = pl.ds(pl.multiple_of(hp * 128, 128), 128)
            c1 = pltpu.make_async_copy(dk_acc, dk_hbm.at[:, lanes], sem.at[0])
            c2 = pltpu.make_async_copy(dv_acc, dv_hbm.at[:, lanes], sem.at[1])
            c1.start()
            c2.start()
            c1.wait()
            c2.wait()

    return _pallas(
        body, name=name, grid=(4, nb),
        in_specs=[pl.BlockSpec((t, 128), lambda hp, i: (i, hp)),
                  pl.BlockSpec((t, 128), lambda hp, i: (i, hp)),
                  pl.BlockSpec((r, 128), lambda hp, i: (0, hp)),
                  pl.BlockSpec((r, 128), lambda hp, i: (0, hp)),
                  pl.BlockSpec((1, nb, 128, t), lambda hp, i: (hp, 0, 0, 0)),
                  pl.BlockSpec((2, 1, nbp, t), lambda hp, i: (hp, i, 0, 0))],
        out_specs=[pl.BlockSpec((t, 128), lambda hp, i: (i, hp)),
                   pl.BlockSpec(memory_space=pl.ANY), pl.BlockSpec(memory_space=pl.ANY)],
        out_shape=[jax.ShapeDtypeStruct((r, SB_WIDTH), F32),
                   jax.ShapeDtypeStruct((r, SB_WIDTH), F32), jax.ShapeDtypeStruct((r, SB_WIDTH), F32)],
        scratch_shapes=[pltpu.VMEM((r, 128), F32), pltpu.VMEM((r, 128), F32), pltpu.VMEM((2, 128, t), F32),
                        pltpu.SemaphoreType.DMA((2,))],
        compiler_params=_cparams(("arbitrary", "arbitrary")),
    )(q, do, k, v, kt3, ssave)


def _s5_disc(lam_re, lam_im, logdt, btr, bti):
    lr = jnp.minimum(lam_re, -1e-4)
    li = lam_im
    dt = jnp.exp(logdt)
    mag = jnp.exp(lr * dt)
    ang = li * dt
    a_re, a_im = mag * jnp.cos(ang), mag * jnp.sin(ang)
    den = lr * lr + li * li
    nr, ni = a_re - 1.0, a_im
    c_re = (nr * lr + ni * li) / den
    c_im = (ni * lr - nr * li) / den
    return a_re, a_im, c_re * btr - c_im * bti, c_re * bti + c_im * btr


def _s5_prep(lam_re, lam_im, logdt, btr, bti, name):
    ns = lam_re.shape[1]

    def body(lr_ref, li_ref, dt_ref, br_ref, bi_ref, ar_ref, ai_ref, bbr_ref, bbi_ref):
        ar, ai, bbr, bbi = _s5_disc(lr_ref[...], li_ref[...], dt_ref[...], br_ref[...], bi_ref[...])
        ar_ref[...] = ar
        ai_ref[...] = ai
        bbr_ref[...] = bbr
        bbi_ref[...] = bbi

    return _pallas(body, name=name,
                   out_shape=[jax.ShapeDtypeStruct((1, ns), F32)] * 2 + [jax.ShapeDtypeStruct((S5_GROUP, ns), F32)] * 2,
                   )(lam_re, lam_im, logdt, btr, bti)


def _s5_prep_bwd(lam_re, lam_im, logdt, btr, bti, dar, dai, dbbr, dbbi, name):
    ns = lam_re.shape[1]

    def body(lr_ref, li_ref, dt_ref, br_ref, bi_ref, dar_ref, dai_ref, dbr_ref, dbi_ref, o_lr, o_li, o_dt, o_br, o_bi):
        _, vjp = jax.vjp(_s5_disc, lr_ref[...], li_ref[...], dt_ref[...], br_ref[...], bi_ref[...])
        g = vjp((dar_ref[...], dai_ref[...], dbr_ref[...], dbi_ref[...]))
        o_lr[...] = g[0]
        o_li[...] = g[1]
        row = lax.broadcasted_iota(jnp.int32, (ns, ns), 0) // S5_STATE
        col = lax.broadcasted_iota(jnp.int32, (ns, ns), 1) // S5_STATE
        same = (row == col).astype(F32)
        o_dt[...] = _dot_hi(jnp.broadcast_to(g[2], (8, ns)), same)[0:1]
        o_br[...] = g[3]
        o_bi[...] = g[4]

    return _pallas(body, name=name,
                   out_shape=[jax.ShapeDtypeStruct((1, ns), F32)] * 3 + [jax.ShapeDtypeStruct((S5_GROUP, ns), F32)] * 2,
                   compiler_params=pltpu.CompilerParams(vmem_limit_bytes=VMEM_LIMIT),
                   )(lam_re, lam_im, logdt, btr, bti, dar, dai, dbbr, dbbi)


def _s5_scan(br, bi, ar, ai, t, reverse=False, carry=None):
    ng = t // 8
    ns = br.shape[1]
    br, bi = br.reshape(ng, 8, ns), bi.reshape(ng, 8, ns)
    row8 = lax.broadcasted_iota(jnp.int32, (1, 8, 1), 1)
    pr, pi_ = ar, ai
    for k in (1, 2, 4):
        if reverse:
            sr, si, ok = pltpu.roll(br, 8 - k, 1), pltpu.roll(bi, 8 - k, 1), row8 < 8 - k
        else:
            sr, si, ok = pltpu.roll(br, k, 1), pltpu.roll(bi, k, 1), row8 >= k
        sr = jnp.where(ok, sr, 0.0)
        si = jnp.where(ok, si, 0.0)
        br, bi = br + pr * sr - pi_ * si, bi + pr * si + pi_ * sr
        pr, pi_ = pr * pr - pi_ * pi_, 2.0 * pr * pi_
    pw_r, pw_i = [ar], [ai]
    for _ in range(7):
        pw_r.append(pw_r[-1] * ar - pw_i[-1] * ai)
        pw_i.append(pw_r[-2] * ai + pw_i[-1] * ar)
    if reverse:
        pw_r.reverse()
        pw_i.reverse()
    p8r, p8i = jnp.concatenate(pw_r, axis=0), jnp.concatenate(pw_i, axis=0)
    out_r, out_i = [None] * ng, [None] * ng
    order = range(ng - 1, -1, -1) if reverse else range(ng)
    edge = 0 if reverse else 7
    for g in order:
        gr, gi = br[g], bi[g]
        if carry is not None:
            cr, ci = carry
            gr, gi = gr + p8r * cr - p8i * ci, gi + p8r * ci + p8i * cr
        out_r[g], out_i[g] = gr, gi
        carry = (gr[edge:edge + 1], gi[edge:edge + 1])
    return jnp.concatenate(out_r, axis=0), jnp.concatenate(out_i, axis=0)


def _s5_prev_rows(x, first, t):
    ng = t // 8
    ns = x.shape[1]
    x3 = x.reshape(ng, 8, ns)
    last = x3[:, 7:8, :]
    before = jnp.concatenate([first.reshape(1, 1, ns), last[:ng - 1]], axis=0)
    row8 = lax.broadcasted_iota(jnp.int32, (1, 8, 1), 1)
    return jnp.where(row8 == 0, before, pltpu.roll(x3, 1, 1)).reshape(t, ns)


_GELU_C = math.sqrt(2.0 / math.pi)


def _gelu(y):
    th = jnp.tanh(_GELU_C * (y + 0.044715 * y * y * y))
    return 0.5 * y * (1.0 + th), th


def _sigmoid(x):
    return 1.0 / (1.0 + jnp.exp(-x))


def _s5_fwd(u, wb, a, wc, dskip, wglu, bglu, gnorm, name):
    r = u.shape[0]
    t = S5_TILE
    nt = r // t
    ns = wb.shape[2]
    w = S5_WIDTH

    def body(u_ref, wb_ref, a_ref, wc_ref, d_ref, wg_ref, bg_ref, gn_ref, y_ref, on_ref, xs_ref, carry_ref):
        i = pl.program_id(0)
        ar, ai = a_ref[0], a_ref[1]

        @pl.when(i == 0)
        def _():
            carry_ref[...] = jnp.zeros_like(carry_ref)

        u_ = u_ref[...]
        ub = u_.astype(BF16)
        xs_ref[0] = carry_ref[:, 0, :]
        chunks = list(range(S5_CHUNKS))
        sl_s = [slice(c * (ns // S5_CHUNKS), (c + 1) * (ns // S5_CHUNKS)) for c in chunks]
        sl_u = [slice(c * (w // S5_CHUNKS), (c + 1) * (w // S5_CHUNKS)) for c in chunks]
        bu, xs, ys = {}, {}, {}

        def st_inputs(c):
            bu[c] = (_dot(ub[:, sl_u[c]], wb_ref[0, sl_u[c], sl_s[c]]), _dot(ub[:, sl_u[c]], wb_ref[1, sl_u[c], sl_s[c]]))

        def st_scan(c):
            xr, xi = _s5_scan(*bu[c], ar[:, sl_s[c]], ai[:, sl_s[c]], t, carry=(carry_ref[0, :, sl_s[c]], carry_ref[1, :, sl_s[c]]))
            carry_ref[0, :, sl_s[c]] = xr[t - 1:t, :]
            carry_ref[1, :, sl_s[c]] = xi[t - 1:t, :]
            xs[c] = (xr.astype(BF16), xi.astype(BF16))

        def st_outputs(c):
            ys[c] = _dot(xs[c][0], wc_ref[0, sl_s[c], sl_u[c]]) - _dot(xs[c][1], wc_ref[1, sl_s[c], sl_u[c]])

        _emit_chains(chunks, [st_inputs, st_scan, st_outputs], False)
        y = jnp.concatenate([ys[c] for c in chunks], axis=1) + d_ref[...] * u_
        h, _ = _gelu(y)
        gate = _sigmoid(_dot(h.astype(BF16), wg_ref[...]) + bg_ref[...])
        y_ref[...] = y
        on_ref[...] = _rms(h * gate, gn_ref[...]).astype(BF16)

    full = lambda shape: pl.BlockSpec(shape, lambda i: (0,) * len(shape))
    return _pallas(
        body, name=name, grid=(nt,),
        in_specs=[_row_spec(w, t), full((2, w, ns)), full((2, 1, ns)), full((2, ns, w)), full((1, w)),
                  full((w, w)), full((1, w)), full((1, w))],
        out_specs=[_row_spec(w, t), pl.BlockSpec((t, w), lambda i: (i, 1)), pl.BlockSpec((1, 2, ns), lambda i: (i, 0, 0))],
        out_shape=[jax.ShapeDtypeStruct((r, w), F32), jax.ShapeDtypeStruct((r, 2 * w), BF16),
                   jax.ShapeDtypeStruct((nt, 2, ns), F32)],
        scratch_shapes=[pltpu.VMEM((2, 1, ns), F32)],
        compiler_params=_cparams(("arbitrary",)),
    )(u, wb, a, wc, dskip, wglu, bglu, gnorm)


def _s5_bwd(u, y, don, xstart, wb, a, wc, dskip, wglu, bglu, gnorm, name, ride=((), ()), don_block=0):
    r = u.shape[0]
    t = S5_TILE
    nt = r // t
    ns = wb.shape[2]
    w = S5_WIDTH
    nt_dims = ((1,), (1,))
    tn_dims = ((0,), (0,))

    def body(u_ref, y_ref, don_ref, xs_ref, wb_hbm, a_ref, wc_hbm, d_ref, wg_ref, bg_ref, gn_ref,
             du_ref, da_ref, dd_ref, dbg_ref, dgn_ref, dwb_hbm, dwc_hbm, dwg_hbm,
             wb_ref, wc_ref, lam_ref, acc_wb, acc_wc, acc_wg, sem):
        i = pl.program_id(0)
        ar, ai = a_ref[0], a_ref[1]

        @pl.when(i == 0)
        def _():
            c1 = pltpu.make_async_copy(wb_hbm, wb_ref, sem.at[0])
            c2 = pltpu.make_async_copy(wc_hbm, wc_ref, sem.at[1])
            c1.start()
            c2.start()
            lam_ref[...] = jnp.zeros_like(lam_ref)
            acc_wb[...] = jnp.zeros_like(acc_wb)
            acc_wc[...] = jnp.zeros_like(acc_wc)
            acc_wg[...] = jnp.zeros_like(acc_wg)
            da_ref[...] = jnp.zeros_like(da_ref)
            dd_ref[...] = jnp.zeros_like(dd_ref)
            dbg_ref[...] = jnp.zeros_like(dbg_ref)
            dgn_ref[...] = jnp.zeros_like(dgn_ref)
            c1.wait()
            c2.wait()

        u_ = u_ref[...]
        y_ = y_ref[...]
        ub = u_.astype(BF16)
        h, th = _gelu(y_)
        hb = h.astype(BF16)
        wg = wg_ref[...]
        gate = _sigmoid(_dot(hb, wg) + bg_ref[...])
        d_out, dgn = _rms_bwd(h * gate, gn_ref[...], don_ref[...])
        dgn_ref[...] += dgn
        dhw = d_out * h * gate * (1.0 - gate)
        dhwb = dhw.astype(BF16)
        dh = d_out * gate + _dot(dhwb, wg, nt_dims)
        acc_wg[...] += _dot(hb, dhwb, tn_dims)
        dbg_ref[...] += jnp.sum(dhw, axis=0, keepdims=True)
        dgelu = 0.5 * (1.0 + th) + 0.5 * y_ * (1.0 - th * th) * _GELU_C * (1.0 + 3.0 * 0.044715 * y_ * y_)
        dy = dh * dgelu
        dd_ref[...] += jnp.sum(dy * u_, axis=0, keepdims=True)
        dyb = dy.astype(BF16)
        chunks = list(range(S5_CHUNKS))
        sl_s = [slice(c * (ns // S5_CHUNKS), (c + 1) * (ns // S5_CHUNKS)) for c in chunks]
        sl_u = [slice(c * (w // S5_CHUNKS), (c + 1) * (w // S5_CHUNKS)) for c in chunks]
        bu, gx, x_, lam, dus = {}, {}, {}, {}, {}

        def st_inputs(c):
            su, ss = sl_u[c], sl_s[c]
            bu[c] = (_dot(ub[:, su], wb_ref[0, su, ss]), _dot(ub[:, su], wb_ref[1, su, ss]))
            gx[c] = (_dot(dyb[:, su], wc_ref[0, ss, su], nt_dims), -_dot(dyb[:, su], wc_ref[1, ss, su], nt_dims))

        def st_states(c):
            su, ss = sl_u[c], sl_s[c]
            first = (xs_ref[0, 0:1, ss], xs_ref[0, 1:2, ss])
            xr, xi = _s5_scan(*bu[c], ar[:, ss], ai[:, ss], t, carry=first)
            acc_wc[0, ss, su] += _dot(xr.astype(BF16), dyb[:, su], tn_dims)
            acc_wc[1, ss, su] -= _dot(xi.astype(BF16), dyb[:, su], tn_dims)
            x_[c] = (_s5_prev_rows(xr, first[0], t), _s5_prev_rows(xi, first[1], t))

        def st_adjoint(c):
            su, ss = sl_u[c], sl_s[c]
            lr, li = _s5_scan(*gx[c], ar[:, ss], -ai[:, ss], t, reverse=True, carry=(lam_ref[0, :, ss], lam_ref[1, :, ss]))
            lam_ref[0, :, ss] = lr[0:1, :]
            lam_ref[1, :, ss] = li[0:1, :]
            lrb, lib = lr.astype(BF16), li.astype(BF16)
            acc_wb[0, su, ss] += _dot(ub[:, su], lrb, tn_dims)
            acc_wb[1, su, ss] += _dot(ub[:, su], lib, tn_dims)
            dus[c] = _dot(lrb, wb_ref[0, su, ss], nt_dims) + _dot(lib, wb_ref[1, su, ss], nt_dims)
            lam[c] = (lr, li)

        def st_decay(c):
            ss = sl_s[c]
            (lr, li), (xpr, xpi) = lam[c], x_[c]
            da_ref[0, :, ss] += jnp.sum(lr * xpr + li * xpi, axis=0, keepdims=True)
            da_ref[1, :, ss] += jnp.sum(li * xpr - lr * xpi, axis=0, keepdims=True)

        _emit_chains(chunks, [st_inputs, st_states, st_adjoint, st_decay], False)
        du_ref[...] = d_ref[...] * dy + jnp.concatenate([dus[c] for c in chunks], axis=1)

        @pl.when(i == nt - 1)
        def _():
            cps = [pltpu.make_async_copy(acc_wb, dwb_hbm, sem.at[0]), pltpu.make_async_copy(acc_wc, dwc_hbm, sem.at[1]),
                   pltpu.make_async_copy(acc_wg, dwg_hbm, sem.at[2])]
            for c in cps:
                c.start()
            for c in cps:
                c.wait()

    n_ride = len(ride[0])
    n_in, n_out, n_scratch = 11, 8, 7

    def body_with_ride(*refs):
        ins, rest = refs[:n_in], refs[n_in:]
        ride_in, rest = rest[:n_ride], rest[n_ride:]
        outs, rest = rest[:n_out], rest[n_out:]
        ride_out, rest = rest[:n_ride], rest[n_ride:]
        scratch, ride_sems = rest[:n_scratch], rest[n_scratch:]
        if n_ride:
            @pl.when(pl.program_id(0) == 0)
            def _():
                for cp in _exchange_copies(ride_in, ride_out, ride[1], *ride_sems):
                    cp.start()
        body(*ins, *outs, *scratch)
        if n_ride:
            @pl.when(pl.program_id(0) == nt - 1)
            def _():
                for cp in _exchange_copies(ride_in, ride_out, ride[1], *ride_sems):
                    cp.wait()

    rev = lambda i: (nt - 1 - i, 0)
    full = lambda shape: pl.BlockSpec(shape, lambda i: (0,) * len(shape))
    hbm = pl.BlockSpec(memory_space=pl.ANY)
    outs = _pallas(
        body_with_ride, name=name, grid=(nt,),
        in_specs=[pl.BlockSpec((t, w), rev), pl.BlockSpec((t, w), rev), pl.BlockSpec((t, w), lambda i: (nt - 1 - i, don_block)),
                  pl.BlockSpec((1, 2, ns), lambda i: (nt - 1 - i, 0, 0)), hbm, full((2, 1, ns)), hbm, full((1, w)),
                  full((w, w)), full((1, w)), full((1, w))] + [hbm] * n_ride,
        out_specs=[pl.BlockSpec((t, w), rev), full((2, 1, ns)), full((1, w)), full((1, w)), full((1, w)), hbm, hbm, hbm]
        + [hbm] * n_ride,
        out_shape=[jax.ShapeDtypeStruct((r, w), F32), jax.ShapeDtypeStruct((2, 1, ns), F32)]
        + [jax.ShapeDtypeStruct((1, w), F32)] * 3
        + [jax.ShapeDtypeStruct((2, w, ns), F32), jax.ShapeDtypeStruct((2, ns, w), F32), jax.ShapeDtypeStruct((w, w), F32)]
        + _exchange_shapes(*ride),
        scratch_shapes=[pltpu.VMEM((2, w, ns), BF16), pltpu.VMEM((2, ns, w), BF16), pltpu.VMEM((2, 1, ns), F32),
                        pltpu.VMEM((2, w, ns), F32), pltpu.VMEM((2, ns, w), F32), pltpu.VMEM((w, w), F32),
                        pltpu.SemaphoreType.DMA((3,))] + (_exchange_sems(n_ride) if n_ride else []),
        compiler_params=_cparams(("arbitrary",)),
    )(u, y, don, xstart, wb, a, wc, dskip, wglu, bglu, gnorm, *ride[0])
    return tuple(outs[:n_out]) + (list(outs[n_out:]),)


def _s5_expand(lam_re, lam_im, log_dt, b_re, b_im, c_re, c_im):
    g, n, p = S5_GROUPS, S5_STATE, S5_GROUP
    ns = g * n
    rows = lambda x: x.reshape(1, ns)
    logdt = jnp.repeat(log_dt.reshape(g), n).reshape(1, ns)
    btr = b_re.reshape(ns, p).T
    bti = b_im.reshape(ns, p).T
    ctr = c_re.transpose(0, 2, 1).reshape(ns, p)
    cti = c_im.transpose(0, 2, 1).reshape(ns, p)
    mask = (jnp.arange(g * p)[:, None] // p) == (jnp.arange(ns)[None, :] // n)
    return rows(lam_re), rows(lam_im), logdt, btr, bti, ctr, cti, mask


def _s5_block_diag_b(bb, mask):
    return jnp.where(mask, jnp.tile(bb, (S5_GROUPS, 1)), 0.0)


def _s5_block_diag_c(ct, mask):
    return jnp.where(mask.T, jnp.tile(ct, (1, S5_GROUPS)), 0.0)


def _s5_diag_of_b(dwb, mask):
    return jnp.where(mask, dwb, 0.0).reshape(S5_GROUPS, S5_GROUP, -1).sum(0)


def _s5_diag_of_c(dwc, mask):
    ns = dwc.shape[0]
    return jnp.where(mask.T, dwc, 0.0).reshape(ns, S5_GROUPS, S5_GROUP).sum(1)


DN_PRE_TILE = 256
_DN_QKV = 3 * DN_WIDTH


def _halo_specs(width, tile, nt, prev):
    per = tile // 8
    if prev:
        return pl.BlockSpec((8, width), lambda i: (jnp.maximum(i * per - 1, 0), 0))
    return pl.BlockSpec((8, width), lambda i: (jnp.minimum((i + 1) * per, nt * per - 1), 0))


def _shift_down(x, halo, s, t):
    xx = jnp.concatenate([halo, x], axis=0)
    return pltpu.roll(xx, s, 0)[8:]


def _shift_up(x, halo, s, t):
    xx = jnp.concatenate([x, halo], axis=0)
    return pltpu.roll(xx, t + 8 - s, 0)[:t]


def _silu(x):
    s = _sigmoid(x)
    return x * s, s


def _dn_gates(ab, alog, dtb, live):
    lane = lax.broadcasted_iota(jnp.int32, (1, 128), 1)
    g = -jnp.exp(alog) * _softplus(ab + dtb)
    beta = _sigmoid(ab)
    return jnp.where(live & (lane < DN_HEADS), g, jnp.where(live & (lane < 2 * DN_HEADS), beta, 0.0))


def _dn_pre_fwd(proj, ab, conv_w, alog, dtb, pad, name):
    r = proj.shape[0]
    t = DN_PRE_TILE
    nt = r // t
    scale = DN_HEAD_DIM ** -0.5

    def body(x_ref, halo_ref, ab_ref, w_ref, al_ref, dt_ref, q_ref, k_ref, v_ref, gb_ref):
        i = pl.program_id(0)
        act, _ = _silu(_dn_conv(x_ref[...], jnp.where(i > 0, halo_ref[...], 0.0), w_ref[...], t))
        for hd in range(DN_HEADS):
            sl = slice(hd * 128, (hd + 1) * 128)
            for base, o_ref, sc in ((0, q_ref, scale), (DN_WIDTH, k_ref, 1.0)):
                xh = act[:, base + hd * 128: base + (hd + 1) * 128]
                o_ref[:, sl] = (xh * (lax.rsqrt(jnp.sum(xh * xh, axis=-1, keepdims=True) + EPS) * sc)).astype(BF16)
        v_ref[...] = act[:, 2 * DN_WIDTH:].astype(BF16)
        rows = i * t + lax.broadcasted_iota(jnp.int32, (t, 1), 0)
        gb_ref[...] = _dn_gates(ab_ref[...], al_ref[...], dt_ref[...], rows >= pad)

    return _pallas(
        body, name=name, grid=(nt,),
        in_specs=[pl.BlockSpec((t, _DN_QKV), lambda i: (i, 0)), _halo_specs(_DN_QKV, t, nt, True), _row_spec(128, t),
                  pl.BlockSpec((DN_CONV, _DN_QKV), lambda i: (0, 0)), _vec_spec(128), _vec_spec(128)],
        out_specs=[_row_spec(DN_WIDTH, t), _row_spec(DN_WIDTH, t), _row_spec(DN_WIDTH, t), _row_spec(128, t)],
        out_shape=[jax.ShapeDtypeStruct((r, DN_WIDTH), BF16)] * 3 + [jax.ShapeDtypeStruct((r, 128), F32)],
        compiler_params=_cparams(("parallel",)),
    )(proj, proj, ab, conv_w, alog, dtb)


def _dn_conv(x, halo, w, t):
    co = w[DN_CONV - 1:DN_CONV] * x
    for tap in range(DN_CONV - 1):
        co = co + w[tap:tap + 1] * _shift_down(x, halo, DN_CONV - 1 - tap, t)
    return co


def _dn_pre_bwd(proj, conv_w, dq, dk, dv, dgb, ab, alog, dtb, pad, name):
    r = proj.shape[0]
    t = DN_PRE_TILE
    nt = r // t
    scale = DN_HEAD_DIM ** -0.5

    def body(x_ref, halo_ref, w_ref, dq_ref, dk_ref, dv_ref, dgb_ref, ab_ref, al_ref, dt_ref, dco_ref, dab_ref, dal_ref,
             ddt_ref):
        i = pl.program_id(0)

        @pl.when(i == 0)
        def _():
            dal_ref[...] = jnp.zeros_like(dal_ref)
            ddt_ref[...] = jnp.zeros_like(ddt_ref)

        co_ = _dn_conv(x_ref[...], jnp.where(i > 0, halo_ref[...], 0.0), w_ref[...], t)
        act, sg = _silu(co_)
        dsilu = sg * (1.0 + co_ * (1.0 - sg))
        for hd in range(DN_HEADS):
            sl = slice(hd * 128, (hd + 1) * 128)
            for base, d_ref, sc in ((0, dq_ref, scale), (DN_WIDTH, dk_ref, 1.0)):
                cs = slice(base + hd * 128, base + (hd + 1) * 128)
                xh = act[:, cs]
                rn = lax.rsqrt(jnp.sum(xh * xh, axis=-1, keepdims=True) + EPS)
                xhat = xh * rn
                dy = d_ref[:, sl]
                dx = (sc * rn) * (dy - xhat * jnp.sum(dy * xhat, axis=-1, keepdims=True))
                dco_ref[:, cs] = dx * dsilu[:, cs]
        dco_ref[:, 2 * DN_WIDTH:] = dv_ref[...] * dsilu[:, 2 * DN_WIDTH:]
        rows = i * t + lax.broadcasted_iota(jnp.int32, (t, 1), 0)
        live = rows >= pad
        lane = lax.broadcasted_iota(jnp.int32, (1, 128), 1)
        ab_ = ab_ref[...]
        dgb_ = dgb_ref[...]
        is_g = live & (lane < DN_HEADS)
        is_b = live & (lane >= DN_HEADS) & (lane < 2 * DN_HEADS)
        arg = ab_ + dt_ref[...]
        ea = jnp.exp(al_ref[...])
        da = jnp.where(is_g, -dgb_ * ea * _sigmoid(arg), 0.0)
        beta = _sigmoid(ab_)
        dab_ref[...] = (da + jnp.where(is_b, dgb_ * beta * (1.0 - beta), 0.0)).astype(BF16)
        ddt_ref[...] += jnp.sum(da, axis=0, keepdims=True)
        dal_ref[...] += jnp.sum(jnp.where(is_g, -dgb_ * ea * _softplus(arg), 0.0), axis=0, keepdims=True)

    return _pallas(
        body, name=name, grid=(nt,),
        in_specs=[pl.BlockSpec((t, _DN_QKV), lambda i: (i, 0)), _halo_specs(_DN_QKV, t, nt, True),
                  pl.BlockSpec((DN_CONV, _DN_QKV), lambda i: (0, 0)),
                  _row_spec(DN_WIDTH, t), _row_spec(DN_WIDTH, t), _row_spec(DN_WIDTH, t),
                  _row_spec(128, t), _row_spec(128, t), _vec_spec(128), _vec_spec(128)],
        out_specs=[_row_spec(_DN_QKV, t), _row_spec(128, t), _vec_spec(128), _vec_spec(128)],
        out_shape=[jax.ShapeDtypeStruct((r, _DN_QKV), F32), jax.ShapeDtypeStruct((r, 128), BF16),
                   jax.ShapeDtypeStruct((1, 128), F32), jax.ShapeDtypeStruct((1, 128), F32)],
        compiler_params=_cparams(("arbitrary",)),
    )(proj, proj, conv_w, dq, dk, dv, dgb, ab, alog, dtb)


def _dn_conv_bwd(dco, proj, conv_w, dproj, name):
    r = dco.shape[0]
    t = DN_PRE_TILE
    nt = r // t

    def body(d_ref, dh_ref, x_ref, xh_ref, w_ref, dproj_in, dx_ref, dw_ref):
        i = pl.program_id(0)

        @pl.when(i == 0)
        def _():
            dw_ref[...] = jnp.zeros_like(dw_ref)

        d = d_ref[...]
        dhalo = jnp.where(i < nt - 1, dh_ref[...], 0.0)
        x = x_ref[...]
        xhalo = jnp.where(i > 0, xh_ref[...], 0.0)
        w = w_ref[...]
        dx = w[3:4] * d
        dws = [None] * DN_CONV
        dws[3] = jnp.sum(d * x, axis=0, keepdims=True)
        for tap in range(DN_CONV - 1):
            s = DN_CONV - 1 - tap
            dx = dx + w[tap:tap + 1] * _shift_up(d, dhalo, s, t)
            dws[tap] = jnp.sum(d * _shift_down(x, xhalo, s, t), axis=0, keepdims=True)
        dx_ref[...] = dx.astype(BF16)
        dw_ref[...] += jnp.concatenate(dws + [jnp.zeros((8 - DN_CONV, _DN_QKV), F32)], axis=0)

    return _pallas(
        body, name=name, grid=(nt,),
        in_specs=[_row_spec(_DN_QKV, t), _halo_specs(_DN_QKV, t, nt, False),
                  pl.BlockSpec((t, _DN_QKV), lambda i: (i, 0)), _halo_specs(_DN_QKV, t, nt, True),
                  pl.BlockSpec((DN_CONV, _DN_QKV), lambda i: (0, 0)), pl.BlockSpec(memory_space=pl.ANY)],
        out_specs=[_row_spec(_DN_QKV, t), pl.BlockSpec((8, _DN_QKV), lambda i: (0, 0))],
        out_shape=[jax.ShapeDtypeStruct(dproj.shape, BF16), jax.ShapeDtypeStruct((8, _DN_QKV), F32)],
        input_output_aliases={5: 0},
        compiler_params=_cparams(("arbitrary",)),
    )(dco, dco, proj, proj, conv_w, dproj)


def _split3(x):
    hi = x.astype(BF16)
    return hi, (x - hi.astype(F32)).astype(BF16)


def _dot3s(a, b, dims=((1,), (0,))):
    return _dot(a[0], b[0], dims) + (_dot(a[0], b[1], dims) + _dot(a[1], b[0], dims))


def _dot2s(a, b, dims=((1,), (0,))):
    return _dot(a[0], b[0], dims) + _dot(a[1], b[0], dims)


def _dot3(a, b, dims=((1,), (0,))):
    return _dot3s(_split3(a), _split3(b), dims)


def _dn_inverse_many(n_mats):
    c = n_mats[0].shape[0]
    row = lax.broadcasted_iota(jnp.int32, (c, c), 0)
    col = lax.broadcasted_iota(jnp.int32, (c, c), 1)
    eye = (row == col).astype(F32)
    same = row // DN_SUB == col // DN_SUB
    nds = [jnp.where(same, n, 0.0) for n in n_mats]
    nos = [n - nd for n, nd in zip(n_mats, nds)]

    def geometric(bs, order):
        xs = [eye + b for b in bs]
        sp = [_split3(b) for b in bs]
        k = 2
        while k < order:
            sp = [_split3(_dot2s(s_, s_)) for s_ in sp]
            xs = [x + _dot2s(_split3(x), s_) for x, s_ in zip(xs, sp)]
            k *= 2
        return xs

    tds = [_split3(td) for td in geometric([-nd for nd in nds], DN_SUB)]
    ms = [_dot3s(td, _split3(no)) for td, no in zip(tds, nos)]
    xs = geometric([-m for m in ms], c // DN_SUB)
    return [_dot3s(_split3(x), td) for x, td in zip(xs, tds)]


def _dn_chunk_shared(gb_ref, gbt_ref):
    c = DN_CHUNK
    row = lax.broadcasted_iota(jnp.int32, (c, c), 0)
    col = lax.broadcasted_iota(jnp.int32, (c, c), 1)
    gbv = gb_ref[...]
    gam_all = _split_dot((row >= col).astype(BF16), gbv)
    hi, lo = _split3(gbt_ref[...])
    tri_t = (row <= col).astype(BF16)
    return dict(row=row, col=col, gbv=gbv, gam_all=gam_all, gam_rows=_dot(hi, tri_t) + _dot(lo, tri_t),
                lane=lax.broadcasted_iota(jnp.int32, (1, 128), 1))


def _dn_chunk_common(q, k, v, sh, h):
    c = DN_CHUNK
    row, col, lane = sh["row"], sh["col"], sh["lane"]
    q, k, v = q.astype(F32), k.astype(F32), v.astype(F32)
    gam = jnp.sum(jnp.where(lane == h, sh["gam_all"], 0.0), axis=1, keepdims=True)
    beta = jnp.sum(jnp.where(lane == h + DN_HEADS, sh["gbv"], 0.0), axis=1, keepdims=True)
    gam_row = sh["gam_rows"][h:h + 1]
    dec = jnp.where(row >= col, jnp.exp(jnp.minimum(gam - gam_row, 0.0)), 0.0)
    kb, qb = k.astype(BF16), q.astype(BF16)
    nt_dims = ((1,), (1,))
    kk = _dot(kb, kb, nt_dims)
    qk = _dot(qb, kb, nt_dims)
    eg = jnp.exp(gam)
    gam_l = gam[c - 1:c, :]
    return dict(q=q, k=k, v=v, qb=qb, kb=kb, gam=gam, beta=beta, dec=dec, kk=kk, qk=qk, eg=eg, gam_l=gam_l,
                row=row, col=col, lane=lane, att=qk * dec, qg=q * eg, kt=k * jnp.exp(gam_l - gam),
                rhs=jnp.concatenate([v * beta, k * (beta * eg)], axis=1))


def _dn_fwd(q, k, v, gb, gbt, name):
    r = q.shape[0]
    c = DN_CHUNK
    nc = r // c
    dh = DN_HEAD_DIM
    tn_dims = ((0,), (0,))

    def body(q_ref, k_ref, v_ref, gb_ref, gbt_ref, o_ref, ss_ref, ts_ref, s_ref):
        @pl.when(pl.program_id(0) == 0)
        def _():
            s_ref[...] = jnp.zeros_like(s_ref)

        heads = list(range(DN_HEADS))
        sl = [slice(h * dh, (h + 1) * dh) for h in heads]
        sh = _dn_chunk_shared(gb_ref, gbt_ref)
        zs = [_dn_chunk_common(q_ref[:, sl[h]], k_ref[:, sl[h]], v_ref[:, sl[h]], sh, h) for h in heads]
        t_invs = _dn_inverse_many([jnp.where(sh["row"] > sh["col"], z["beta"] * z["kk"] * z["dec"], 0.0) for z in zs])
        sols = [_dot3(t_inv, z["rhs"]) for t_inv, z in zip(t_invs, zs)]
        ss = [s_ref[h] for h in heads]
        sbs = [s.astype(BF16) for s in ss]
        vnbs = [(sol[:, :dh] - _dot(sol[:, dh:].astype(BF16), sb)).astype(BF16) for sol, sb in zip(sols, sbs)]
        for h in heads:
            o_ref[:, sl[h]] = _dot(zs[h]["qg"].astype(BF16), sbs[h]) + _dot(zs[h]["att"].astype(BF16), vnbs[h])
        for h in heads:
            ss_ref[0, h] = ss[h]
            ts_ref[0, h] = t_invs[h]
            s_ref[h] = ss[h] * jnp.exp(zs[h]["gam_l"]) + _dot(zs[h]["kt"].astype(BF16), vnbs[h], tn_dims)

    blk = pl.BlockSpec((c, DN_WIDTH), lambda ci: (ci, 0))
    sav = pl.BlockSpec((1, DN_HEADS, dh, dh), lambda ci: (ci, 0, 0, 0))
    return _pallas(
        body, name=name, grid=(nc,),
        in_specs=[blk, blk, blk, pl.BlockSpec((c, 128), lambda ci: (ci, 0)), pl.BlockSpec((16, c), lambda ci: (0, ci))],
        out_specs=[blk, sav, sav],
        out_shape=[jax.ShapeDtypeStruct((r, DN_WIDTH), F32), jax.ShapeDtypeStruct((nc, DN_HEADS, dh, dh), F32),
                   jax.ShapeDtypeStruct((nc, DN_HEADS, dh, dh), F32)],
        scratch_shapes=[pltpu.VMEM((DN_HEADS, dh, dh), F32)],
        compiler_params=_cparams(("arbitrary",)),
    )(q, k, v, gb, gbt)


def _dn_bwd(q, k, v, gb, gbt, ssave, tsave, do, name):
    r = q.shape[0]
    c = DN_CHUNK
    nc = r // c
    dh = DN_HEAD_DIM
    nt_dims = ((1,), (1,))
    tn_dims = ((0,), (0,))

    def body(q_ref, k_ref, v_ref, gb_ref, gbt_ref, ss_ref, ts_ref, do_ref, dq_ref, dk_ref, dv_ref, dgb_ref, ds_ref):
        @pl.when(pl.program_id(0) == 0)
        def _():
            ds_ref[...] = jnp.zeros_like(ds_ref)

        heads = list(range(DN_HEADS))
        sl = [slice(h * dh, (h + 1) * dh) for h in heads]
        sh = _dn_chunk_shared(gb_ref, gbt_ref)
        row, col, lane = sh["row"], sh["col"], sh["lane"]
        rs = lambda x: jnp.sum(x, axis=1, keepdims=True)
        tot = lambda x: jnp.sum(rs(x), axis=0, keepdims=True)
        st = [dict() for _ in heads]
        dgb_parts = []

        def s_common(h):
            st[h].update(_dn_chunk_common(q_ref[:, sl[h]], k_ref[:, sl[h]], v_ref[:, sl[h]], sh, h))
            st[h]["t"] = _split3(ts_ref[0, h])

        def s_sol(h):
            st[h]["sol"] = _dot3s(st[h]["t"], _split3(st[h]["rhs"]))

        def s_state(h):
            z = st[h]
            sol = z["sol"]
            kcd = sol[:, dh:]
            s = ss_ref[0, h]
            sb = s.astype(BF16)
            vnb = (sol[:, :dh] - _dot(kcd.astype(BF16), sb)).astype(BF16)
            ds_next = ds_ref[h]
            dsb = ds_next.astype(BF16)
            dob = do_ref[:, sl[h]].astype(BF16)
            z["dqg"] = _dot(dob, sb, nt_dims)
            ds = _dot(z["qg"].astype(BF16), dob, tn_dims)
            z["d_att"] = jnp.where(row >= col, _dot(dob, vnb, nt_dims), 0.0)
            dvn = _dot(z["att"].astype(BF16), dob, tn_dims) + _dot(z["kt"].astype(BF16), dsb)
            z["dkt"] = _dot(vnb, dsb, nt_dims)
            eg_l = jnp.exp(z["gam_l"])
            ds = ds + ds_next * eg_l
            z["dgam_l"] = tot(ds_next * s) * eg_l
            dvnb = dvn.astype(BF16)
            dkcd = -_dot(dvnb, sb, nt_dims)
            ds_ref[h] = ds - _dot(kcd.astype(BF16), dvnb, tn_dims)
            z["dsol"] = jnp.concatenate([dvn, dkcd], axis=1)

        def s_drhs(h):
            st[h]["drhs"] = _dot3s(st[h]["t"], _split3(st[h]["dsol"]), tn_dims)

        def s_dn(h):
            z = st[h]
            z["dn"] = jnp.where(row > col, -_dot3(z["drhs"], z["sol"], nt_dims), 0.0)

        def s_rest(h):
            z = st[h]
            k_, v_, kb, qb = z["k"], z["v"], z["kb"], z["qb"]
            beta, eg, dec, kk, qk, gam, gam_l = z["beta"], z["eg"], z["dec"], z["kk"], z["qk"], z["gam"], z["gam_l"]
            dn, d_att, dqg, dkt = z["dn"], z["d_att"], z["dqg"], z["dkt"]
            drv, drk = z["drhs"][:, :dh], z["drhs"][:, dh:]
            s_rkk = rs(drk * k_)
            dv_ref[:, sl[h]] = drv * beta
            dbeta = rs(drv * v_) + s_rkk * eg + rs(dn * kk * dec)
            dk = drk * (beta * eg)
            dgam = s_rkk * beta * eg
            dkk = (dn * beta * dec).astype(BF16)
            dd = dn * beta * kk + d_att * qk
            dqk = (d_att * dec).astype(BF16)
            dq_ref[:, sl[h]] = _dot(dqk, kb) + dqg * eg
            dk = dk + _dot(dqk, qb, tn_dims) + _dot(dkk, kb) + _dot(dkk, kb, tn_dims)
            w = dd * dec
            wh, wl = _split3(w)
            ones = jnp.ones((c, 128), BF16)
            col_sum = (_dot(wh, ones, tn_dims) + _dot(wl, ones, tn_dims))[:, 0:1]
            dgam = dgam + rs(w) - col_sum + rs(dqg * z["qg"]) - rs(dkt * z["kt"])
            dk_ref[:, sl[h]] = dk + dkt * jnp.exp(gam_l - gam)
            dgam_l = z["dgam_l"] + tot(dkt * z["kt"])
            rowc = lax.broadcasted_iota(jnp.int32, (c, 1), 0)
            dgam = dgam + jnp.where(rowc == c - 1, dgam_l, 0.0)
            dg = _split_dot((row <= col).astype(BF16), jnp.broadcast_to(dgam, (c, 128)))[:, 0:1]
            dgb_parts.append(jnp.where(lane == h, dg, 0.0) + jnp.where(lane == h + DN_HEADS, dbeta, 0.0))

        _emit_chains(heads, [s_common, s_sol, s_state, s_drhs, s_dn, s_rest], False)
        dgb = dgb_parts[0]
        for part in dgb_parts[1:]:
            dgb = dgb + part
        dgb_ref[...] = dgb

    blk = pl.BlockSpec((c, DN_WIDTH), lambda ci: (nc - 1 - ci, 0))
    sav = pl.BlockSpec((1, DN_HEADS, dh, dh), lambda ci: (nc - 1 - ci, 0, 0, 0))
    gspec = pl.BlockSpec((c, 128), lambda ci: (nc - 1 - ci, 0))
    return _pallas(
        body, name=name, grid=(nc,),
        in_specs=[blk, blk, blk, gspec, pl.BlockSpec((16, c), lambda ci: (0, nc - 1 - ci)), sav, sav, blk],
        out_specs=[blk, blk, blk, gspec],
        out_shape=[jax.ShapeDtypeStruct((r, DN_WIDTH), F32)] * 3 + [jax.ShapeDtypeStruct((r, 128), F32)],
        scratch_shapes=[pltpu.VMEM((DN_HEADS, dh, dh), F32)],
        compiler_params=_cparams(("arbitrary",)),
    )(q, k, v, gb, gbt, ssave, tsave, do)


def _dn_post_fwd(o, proj, g, name):
    r = o.shape[0]

    def body(o_ref, z_ref, g_ref, y_ref):
        g_ = g_ref[...]
        for hd in range(DN_HEADS):
            sl = slice(hd * 128, (hd + 1) * 128)
            sz, _ = _silu(z_ref[:, sl])
            y_ref[:, sl] = (_rms(o_ref[:, sl], g_) * sz).astype(BF16)

    return _pallas(body, name=name, grid=(r // ROW_TILE,),
                   in_specs=[_row_spec(DN_WIDTH), pl.BlockSpec((ROW_TILE, DN_WIDTH), lambda i: (i, 3)), _vec_spec(128)],
                   out_specs=_row_spec(DN_WIDTH), out_shape=jax.ShapeDtypeStruct((r, DN_WIDTH), BF16),
                   compiler_params=_cparams(("parallel",)))(o, proj, g)


def _dn_post_bwd(o, proj, g, dy, name):
    r = o.shape[0]

    def body(o_ref, z_ref, g_ref, dy_ref, do_ref, dz_ref, dg_ref):
        @pl.when(pl.program_id(0) == 0)
        def _():
            dg_ref[...] = jnp.zeros_like(dg_ref)

        g_ = g_ref[...]
        for hd in range(DN_HEADS):
            sl = slice(hd * 128, (hd + 1) * 128)
            z_ = z_ref[:, sl]
            sz, sg = _silu(z_)
            dy_ = dy_ref[:, sl]
            o_ = o_ref[:, sl]
            dz_ref[:, sl] = (dy_ * _rms(o_, g_) * (sg * (1.0 + z_ * (1.0 - sg)))).astype(BF16)
            dx, dg = _rms_bwd(o_, g_, dy_ * sz)
            do_ref[:, sl] = dx
            dg_ref[...] += dg

    return _pallas(body, name=name, grid=(r // ROW_TILE,),
                   in_specs=[_row_spec(DN_WIDTH), pl.BlockSpec((ROW_TILE, DN_WIDTH), lambda i: (i, 3)), _vec_spec(128),
                             _row_spec(DN_WIDTH)],
                   out_specs=[_row_spec(DN_WIDTH), pl.BlockSpec((ROW_TILE, DN_WIDTH), lambda i: (i, 3)), _vec_spec(128)],
                   out_shape=[jax.ShapeDtypeStruct((r, DN_WIDTH), F32), jax.ShapeDtypeStruct((r, 4 * DN_WIDTH), BF16),
                              jax.ShapeDtypeStruct((1, 128), F32)],
                   compiler_params=_cparams(("arbitrary",)))(o, proj, g, dy)


def _exchange(arrays, scatter, name):
    n = len(arrays)

    def body(*refs):
        copies = _exchange_copies(refs[:n], refs[n:2 * n], scatter, *refs[2 * n:])
        for cp in copies:
            cp.start()
        for cp in copies:
            cp.wait()

    hbm = pl.BlockSpec(memory_space=pl.ANY)
    return _pallas(
        body, name=name, in_specs=[hbm] * n, out_specs=[hbm] * n, out_shape=_exchange_shapes(arrays, scatter),
        scratch_shapes=_exchange_sems(n),
    )(*arrays)


def _exchange_shapes(arrays, scatter):
    return [jax.ShapeDtypeStruct((N_DEV,) + (a.shape[1:] if sc else a.shape), a.dtype) for a, sc in zip(arrays, scatter)]


def _exchange_sems(n):
    return [pltpu.SemaphoreType.DMA((n * N_DEV,)), pltpu.SemaphoreType.DMA((n * N_DEV,)), pltpu.SemaphoreType.DMA((n,))]


def _exchange_copies(in_refs, out_refs, scatter, send_sems, recv_sems, local_sems):
    mx, my, mc = lax.axis_index("x"), lax.axis_index("y"), lax.axis_index("c")
    me = 4 * mx + 2 * my + mc
    copies = []
    for a in range(len(in_refs)):
        src_own = in_refs[a].at[me] if scatter[a] else in_refs[a]
        copies.append(pltpu.make_async_copy(src_own, out_refs[a].at[me], local_sems.at[a]))
        for kbits in range(1, N_DEV):
            px = lax.rem(mx + ((kbits >> 2) & 1), 2)
            py = lax.rem(my + ((kbits >> 1) & 1), 2)
            pc = lax.rem(mc + (kbits & 1), 2)
            src = in_refs[a].at[4 * px + 2 * py + pc] if scatter[a] else in_refs[a]
            copies.append(pltpu.make_async_remote_copy(
                src_ref=src, dst_ref=out_refs[a].at[me],
                send_sem=send_sems.at[a * N_DEV + kbits], recv_sem=recv_sems.at[a * N_DEV + kbits],
                device_id=(px, py, pc), device_id_type=pl.DeviceIdType.MESH))
    return copies


def _adamw(gstack, w, m, v, name):
    a, b = w.shape
    ta = a
    for t in (1024, 512, 256, 128, 64, 32, 16, 8):
        if a % t == 0 and N_DEV * t * b * 4 <= 4 * 1024 * 1024:
            ta = t
            break
    c1 = 1.0 / (1.0 - ADAM_B1 ** ADAM_STEP)
    c2 = 1.0 / (1.0 - ADAM_B2 ** ADAM_STEP)

    def body(g_ref, w_ref, m_ref, v_ref, og_ref, od_ref, om_ref, ov_ref):
        g = g_ref[0].astype(F32)
        for s in range(1, N_DEV):
            g = g + g_ref[s].astype(F32)
        m_new = ADAM_B1 * m_ref[...] + (1.0 - ADAM_B1) * g
        v_new = ADAM_B2 * v_ref[...] + (1.0 - ADAM_B2) * (g * g)
        og_ref[...] = g
        om_ref[...] = m_new
        ov_ref[...] = v_new
        od_ref[...] = -ADAM_LR * ((m_new * c1) / (jnp.sqrt(v_new * c2) + ADAM_EPS) + ADAM_WD * w_ref[...])

    spec = pl.BlockSpec((ta, b), lambda i: (i, 0))
    return _pallas(
        body, name=name, grid=(a // ta,),
        in_specs=[pl.BlockSpec((N_DEV, ta, b), lambda i: (0, i, 0)), spec, spec, spec],
        out_specs=[spec] * 4, out_shape=[jax.ShapeDtypeStruct((a, b), F32)] * 4,
        compiler_params=_cparams(("parallel",)),
    )(gstack, w, m, v)


_WEIGHTS = ['meta_tokens', 'pre_mix_norm', 'post_mix_norm', 'pre_mlp_norm', 'post_mlp_norm', 'mlp_w1', 'mlp_w2',
            'w_in_even', 'w_out_even', 'sb_out_norm', 's5_lambda_re', 's5_lambda_im', 's5_log_dt', 's5_b_re', 's5_b_im',
            's5_c_re', 's5_c_im', 's5_d', 's5_w_glu', 's5_b_glu', 's5_out_norm', 'w_in_odd', 'dn_conv_w', 'dn_a_log',
            'dn_dt_bias', 'dn_out_norm', 'w_out_odd']
_SHARDED = ['meta_tokens', 'mlp_w1', 'mlp_w2', 'w_in_even', 'w_out_even', 's5_w_glu', 'w_in_odd', 'dn_conv_w', 'w_out_odd']
_SMALL = [n for n in _WEIGHTS if n not in _SHARDED]
_GATHER_FIRST = ['meta_tokens', 'w_in_even', 's5_w_glu', 'w_out_even']
_GATHER_LATE = [n for n in _SHARDED if n not in _GATHER_FIRST]
_REDUCE_EARLY = ['mlp_w1', 'mlp_w2', 'w_in_odd', 'dn_conv_w', 'w_out_odd', 'w_out_even']


def _view2d(name, a):
    return a.reshape(-1, a.shape[-1])


def _unshard(name, g):
    if name == 'mlp_w1':
        return g.reshape(N_DEV, 2, D_MODEL, -1).transpose(1, 2, 0, 3).reshape(2, D_MODEL, D_FF)
    if name == 'mlp_w2':
        return g.reshape(N_DEV, 2, -1, D_MODEL).transpose(1, 0, 2, 3).reshape(2, D_FF, D_MODEL)
    if name in ('w_in_even', 'w_in_odd', 'dn_conv_w', 'meta_tokens'):
        return g.transpose(1, 0, 2).reshape(g.shape[1], -1)
    return g.reshape(-1, g.shape[-1])


def _to_blocks(name, full):
    if name == 'mlp_w1':
        return full.reshape(2, D_MODEL, N_DEV, -1).transpose(2, 0, 1, 3).reshape(N_DEV, 2 * D_MODEL, -1)
    if name == 'mlp_w2':
        return full.reshape(2, N_DEV, -1, D_MODEL).transpose(1, 0, 2, 3).reshape(N_DEV, -1, D_MODEL)
    if name in ('w_in_even', 'w_in_odd', 'dn_conv_w', 'meta_tokens'):
        return full.reshape(full.shape[0], N_DEV, -1).transpose(1, 0, 2)
    return full.reshape(N_DEV, -1, full.shape[-1])


def _pack(parts):
    rows = []
    for p in parts:
        flat = p.reshape(-1)
        rows.append(jnp.pad(flat, (0, (-flat.shape[0]) % 128)).reshape(-1, 128))
    return jnp.concatenate(rows, axis=0)


def _unpack(packed, like):
    out, at = [], 0
    for p in like:
        n = math.prod(p.shape)
        nrow = -(-n // 128)
        out.append(packed[at:at + nrow].reshape(-1)[:n].reshape(p.shape))
        at += nrow
    return out


def _lane_vec(x, width=128):
    flat = x.reshape(-1)
    return jnp.pad(flat, (0, width - flat.shape[0])).reshape(1, width)


def kernel(x, meta_tokens, pre_mix_norm, post_mix_norm, pre_mlp_norm, post_mlp_norm, mlp_w1, mlp_w2, w_in_even, w_out_even, sb_out_norm, s5_lambda_re, s5_lambda_im, s5_log_dt, s5_b_re, s5_b_im, s5_c_re, s5_c_im, s5_d, s5_w_glu, s5_b_glu, s5_out_norm, w_in_odd, dn_conv_w, dn_a_log, dn_dt_bias, dn_out_norm, w_out_odd, loss_target, m_meta_tokens, m_pre_mix_norm, m_post_mix_norm, m_pre_mlp_norm, m_post_mlp_norm, m_mlp_w1, m_mlp_w2, m_w_in_even, m_w_out_even, m_sb_out_norm, m_s5_lambda_re, m_s5_lambda_im, m_s5_log_dt, m_s5_b_re, m_s5_b_im, m_s5_c_re, m_s5_c_im, m_s5_d, m_s5_w_glu, m_s5_b_glu, m_s5_out_norm, m_w_in_odd, m_dn_conv_w, m_dn_a_log, m_dn_dt_bias, m_dn_out_norm, m_w_out_odd, v_meta_tokens, v_pre_mix_norm, v_post_mix_norm, v_pre_mlp_norm, v_post_mlp_norm, v_mlp_w1, v_mlp_w2, v_w_in_even, v_w_out_even, v_sb_out_norm, v_s5_lambda_re, v_s5_lambda_im, v_s5_log_dt, v_s5_b_re, v_s5_b_im, v_s5_c_re, v_s5_c_im, v_s5_d, v_s5_w_glu, v_s5_b_glu, v_s5_out_norm, v_w_in_odd, v_dn_conv_w, v_dn_a_log, v_dn_dt_bias, v_dn_out_norm, v_w_out_odd):
    given = dict(locals())
    w = {n: given[n] for n in _WEIGHTS}
    mom_m = {n: given["m_" + n] for n in _WEIGHTS}
    mom_v = {n: given["v_" + n] for n in _WEIGHTS}

    seq = x.shape[1]
    assert x.shape[0] == 1 and seq % ROW_TILE == 0
    r = seq + ROW_TILE
    pad = ROW_TILE - N_META

    wire = {n: (F32 if n in ('dn_conv_w', 'meta_tokens') else BF16) for n in _SHARDED}
    shard_wire = lambda n: _view2d(n, w[n]).astype(wire[n])
    gathered = _exchange([shard_wire(n) for n in _GATHER_FIRST], [False] * len(_GATHER_FIRST), "gather_first")
    full = {n: _unshard(n, g_) for n, g_ in zip(_GATHER_FIRST, gathered)}
    w_ie, w_oe, w_glu = full['w_in_even'], full['w_out_even'], full['s5_w_glu']
    row = lambda v_: v_.reshape(1, -1)

    hs0 = jnp.concatenate([jnp.zeros((pad, D_MODEL), F32), full['meta_tokens'], x[0]], axis=0)
    hn0 = _norm_pre(hs0, row(pre_mix_norm[0]), "pre_mix_0")
    qkv = _mm_fwd(hn0, w_ie[:, :3 * SB_WIDTH], "in_even_qkv", out_dtypes=(BF16,))
    u = _mm_fwd(hn0, w_ie[:, 3 * SB_WIDTH:], "in_even_u")
    q, k, v = qkv[:, :SB_WIDTH], qkv[:, SB_WIDTH:2 * SB_WIDTH], qkv[:, 2 * SB_WIDTH:]
    nb = r // ATT_BLK
    blocks_t = lambda t_: t_.reshape(nb, ATT_BLK, 4, 128).transpose(2, 0, 3, 1)
    o_sb, ssave, gathered = _sb_fwd(q, k, blocks_t(v), pad, "sb_fwd",
                                    ride=([shard_wire(n) for n in _GATHER_LATE], [False] * len(_GATHER_LATE)))
    full.update({n: _unshard(n, g_) for n, g_ in zip(_GATHER_LATE, gathered)})
    w1, w2, w_oo, conv_w = full['mlp_w1'], full['mlp_w2'], full['w_out_odd'], full['dn_conv_w']
    w_io = full['w_in_odd'][:, :4 * DN_WIDTH]
    w_ab = jnp.pad(full['w_in_odd'][:, 4 * DN_WIDTH:], ((0, 0), (0, 128 - 2 * DN_HEADS)))

    lam_re, lam_im, logdt, btr, bti, ctr, cti, s5_mask = _s5_expand(
        s5_lambda_re[0], s5_lambda_im[0], s5_log_dt[0], s5_b_re[0], s5_b_im[0], s5_c_re[0], s5_c_im[0])
    a_re, a_im, bbr, bbi = _s5_prep(lam_re, lam_im, logdt, btr, bti, "s5_prep")
    s5_wb = jnp.stack([_s5_block_diag_b(bbr, s5_mask), _s5_block_diag_b(bbi, s5_mask)]).astype(BF16)
    s5_wc = jnp.stack([_s5_block_diag_c(ctr, s5_mask), _s5_block_diag_c(cti, s5_mask)]).astype(BF16)
    s5_a = jnp.stack([a_re, a_im])
    s5_args = (s5_wb, s5_a, s5_wc, row(s5_d[0]), w_glu, row(s5_b_glu[0]), row(s5_out_norm[0]))
    y_s5, merged, xstart = _s5_fwd(u, *s5_args, "s5_fwd")
    merged = _norm_pre(o_sb, row(sb_out_norm[0]), "sb_out_norm", into=merged)

    mix0, hs1, hn1 = _mm_norm_fwd(merged, w_oe, hs0, row(post_mix_norm[0]), g_pre=row(pre_mlp_norm[0]), name="out_even")
    relu2 = lambda acc: (jnp.square(jnp.maximum(acc, 0.0)), jnp.maximum(acc, 0.0))
    r0, ra0 = _mm_fwd(hn1, w1[0], "mlp_up_0", out_dtypes=(BF16, BF16), epilogue=relu2)
    m0, hs2, hn2 = _mm_norm_fwd(r0, w2[0], hs1, row(post_mlp_norm[0]), g_pre=row(pre_mix_norm[1]), name="mlp_down_0")

    proj = _mm_fwd(hn2, w_io, "in_odd")
    ab = _mm_fwd(hn2, w_ab, "in_odd_gates")
    alog, dtb = _lane_vec(dn_a_log[0]), _lane_vec(dn_dt_bias[0])
    qd, kd, vd, gb = _dn_pre_fwd(proj, ab, conv_w, alog, dtb, pad, "dn_pre")
    gbt = gb[:, :2 * DN_HEADS].T
    o_dn, s_dn, t_dn = _dn_fwd(qd, kd, vd, gb, gbt, "dn_fwd")
    on_dn = _dn_post_fwd(o_dn, proj, row(dn_out_norm[0]), "dn_post")
    mix1, hs3, hn3 = _mm_norm_fwd(on_dn, w_oo, hs2, row(post_mix_norm[1]), g_pre=row(pre_mlp_norm[1]), name="out_odd")
    r1, ra1 = _mm_fwd(hn3, w1[1], "mlp_up_1", out_dtypes=(BF16, BF16), epilogue=relu2)
    dhs, dm1, dg_post_mlp1, loss_part = _mm_norm_fwd(r1, w2[1], hs3, row(post_mlp_norm[1]),
                                                     loss=(loss_target[0], pad + N_META), name="mlp_down_1_loss")
    loss = lax.psum(loss_part, ("x", "y", "c"))

    g = {}
    drelu2 = lambda acc, ra: (acc * (2.0 * ra.astype(F32)),)

    def mlp_bwd(layer, hn, rr, ra, dm):
        dw2 = _mm_wgrad(rr, dm, f"mlp_down_{layer}_wgrad")
        da = _mm_dgrad(dm, w2[layer], f"mlp_down_{layer}_dgrad", out_dtypes=(BF16,), extras=(ra,), epilogue=drelu2)
        dw1 = _mm_wgrad(hn, da, f"mlp_up_{layer}_wgrad")
        return dw1, dw2, da

    dw1_1, dw2_1, da1 = mlp_bwd(1, hn3, r1, ra1, dm1)
    dhs, dmix1, dg_pre_mlp1, dg_post_mix1 = _dgrad_norm_bwd(
        da1, w1[1], dhs, hs3, row(pre_mlp_norm[1]), post=(mix1, row(post_mix_norm[1])), pad=pad, name="post_mix_1_bwd")

    g['w_out_odd'] = _mm_wgrad(on_dn, dmix1, "out_odd_wgrad")
    d_on_dn = _mm_dgrad(dmix1, w_oo, "out_odd_dgrad")
    do_dn, dproj, dg_dn = _dn_post_bwd(o_dn, proj, row(dn_out_norm[0]), d_on_dn, "dn_post_bwd")
    dqd, dkd, dvd, dgb = _dn_bwd(qd, kd, vd, gb, gbt, s_dn, t_dn, do_dn, "dn_bwd")
    dco, dab, d_alog, d_dtb = _dn_pre_bwd(proj, conv_w, dqd, dkd, dvd, dgb, ab, alog, dtb, pad, "dn_pre_bwd")
    dproj, d_conv = _dn_conv_bwd(dco, proj, conv_w, dproj, "dn_conv_bwd")
    g['w_in_odd'] = jnp.concatenate([_mm_wgrad(hn2, dproj, "in_odd_wgrad"),
                                     _mm_wgrad(hn2, dab, "in_odd_gates_wgrad")[:, :2 * DN_HEADS]], axis=1)
    dhn2_gates = _mm_dgrad(dab, w_ab, "in_odd_gates_dgrad")
    g['dn_conv_w'] = d_conv[:DN_CONV]
    g['dn_a_log'], g['dn_dt_bias'], g['dn_out_norm'] = d_alog[0, :DN_HEADS], d_dtb[0, :DN_HEADS], dg_dn[0]

    dhs, dm0, dg_pre_mix1, dg_post_mlp0 = _dgrad_norm_bwd(
        dproj, w_io, dhs, hs2, row(pre_mix_norm[1]), post=(m0, row(post_mlp_norm[0])), add=dhn2_gates, pad=pad,
        name="post_mlp_0_bwd")
    dw1_0, dw2_0, da0 = mlp_bwd(0, hn1, r0, ra0, dm0)
    dhs, dmix0, dg_pre_mlp0, dg_post_mix0 = _dgrad_norm_bwd(
        da0, w1[0], dhs, hs1, row(pre_mlp_norm[0]), post=(mix0, row(post_mix_norm[0])), pad=pad, name="post_mix_0_bwd")

    g['w_out_even'] = _mm_wgrad(merged, dmix0, "out_even_wgrad")
    dmerged = _mm_dgrad(dmix0, w_oe, "out_even_dgrad")
    _, do_sb, _, dg_sb = _norm_bwd(dmerged, post=(o_sb, row(sb_out_norm[0])), pad=pad, dm_dtype=F32,
                                   dhs_cols=(SB_WIDTH, 0), name="sb_out_norm_bwd")
    dq, dk, dv = _sb_bwd(q, k, v, blocks_t(k), ssave, do_sb, pad, "sb_bwd")
    g['mlp_w1'] = jnp.stack([dw1_0, dw1_1])
    g['mlp_w2'] = jnp.stack([dw2_0, dw2_1])
    grad_wire = lambda n: _to_blocks(n, g[n].reshape(full[n].shape)).astype(wire[n])
    du, d_a, d_d, d_bglu, dg_s5, d_wb, d_wc, g['s5_w_glu'], reduced = _s5_bwd(
        u, y_s5, dmerged, xstart, *s5_args, "s5_bwd", don_block=1,
        ride=([grad_wire(n) for n in _REDUCE_EARLY], [True] * len(_REDUCE_EARLY)))
    stacks = dict(zip(_REDUCE_EARLY, reduced))
    g_lr, g_li, g_dt, g_btr, g_bti = _s5_prep_bwd(
        lam_re, lam_im, logdt, btr, bti, d_a[0], d_a[1],
        _s5_diag_of_b(d_wb[0], s5_mask), _s5_diag_of_b(d_wb[1], s5_mask), "s5_prep_bwd")
    gg, nn, pp = S5_GROUPS, S5_STATE, S5_GROUP
    g['s5_lambda_re'], g['s5_lambda_im'] = g_lr.reshape(gg, nn), g_li.reshape(gg, nn)
    g['s5_log_dt'] = g_dt.reshape(gg, nn)[:, 0]
    g['s5_b_re'], g['s5_b_im'] = g_btr.T.reshape(gg, nn, pp), g_bti.T.reshape(gg, nn, pp)
    g['s5_c_re'] = _s5_diag_of_c(d_wc[0], s5_mask).reshape(gg, nn, pp).transpose(0, 2, 1)
    g['s5_c_im'] = _s5_diag_of_c(d_wc[1], s5_mask).reshape(gg, nn, pp).transpose(0, 2, 1)
    g['s5_d'], g['s5_b_glu'], g['s5_out_norm'], g['sb_out_norm'] = d_d[0], d_bglu[0], dg_s5[0], dg_sb[0]
    dqkvu = jnp.concatenate([dq, dk, dv, du], axis=1).astype(BF16)
    g['w_in_even'] = _mm_wgrad(hn0, dqkvu, "in_even_wgrad")
    dhs, _, dg_pre_mix0, _ = _dgrad_norm_bwd(dqkvu, w_ie, dhs, hs0, row(pre_mix_norm[0]), pad=pad, name="pre_mix_0_bwd")

    g['meta_tokens'] = dhs[pad:pad + N_META]
    g['pre_mix_norm'] = jnp.concatenate([dg_pre_mix0, dg_pre_mix1], axis=0)
    g['post_mix_norm'] = jnp.concatenate([dg_post_mix0, dg_post_mix1], axis=0)
    g['pre_mlp_norm'] = jnp.concatenate([dg_pre_mlp0, dg_pre_mlp1], axis=0)
    g['post_mlp_norm'] = jnp.concatenate([dg_post_mlp0, dg_post_mlp1], axis=0)
    grad_x = dhs[pad + N_META:][None]

    small_like = [w[n] for n in _SMALL]
    last = [n for n in _SHARDED if n not in _REDUCE_EARLY]
    partial = [grad_wire(n) for n in last] + [_pack([g[n].reshape(w[n].shape) for n in _SMALL])]
    reduced = _exchange(partial, [True] * len(last) + [False], "reduce_last")
    stacks.update(zip(last, reduced[:-1]))
    grads, deltas, new_m, new_v = {}, {}, {}, {}
    for n in _SHARDED:
        outs = _adamw(stacks[n], _view2d(n, w[n]), _view2d(n, mom_m[n]), _view2d(n, mom_v[n]), f"adamw_{n}")
        grads[n], deltas[n], new_m[n], new_v[n] = (o.reshape(w[n].shape) for o in outs)
    outs = _adamw(reduced[-1], _pack(small_like), _pack([mom_m[n] for n in _SMALL]), _pack([mom_v[n] for n in _SMALL]),
                  "adamw_small")
    for dst, o in zip((grads, deltas, new_m, new_v), outs):
        for n, part in zip(_SMALL, _unpack(o, small_like)):
            dst[n] = part
    return (loss, grad_x, *[grads[n] for n in _WEIGHTS], *[deltas[n] for n in _WEIGHTS],
            *[new_m[n] for n in _WEIGHTS], *[new_v[n] for n in _WEIGHTS])
```

```python
import math

import jax
import jax.numpy as jnp
from jax import lax
from jax.experimental import pallas as pl
from jax.experimental.pallas import tpu as pltpu

F32 = jnp.float32
BF16 = jnp.bfloat16

D_MODEL = 1024
N_META = 16
SB_HEAD_DIM = 64
SB_WIDTH = 512
S5_WIDTH = 512
S5_GROUP = 16
S5_GROUPS = 32
S5_STATE = 64
S5_NS = S5_GROUPS * S5_STATE
DN_HEAD_DIM = 128
DN_HEADS = 8
DN_WIDTH = 1024
DN_CONV = 4
D_FF = 4096
EPS = 1e-6
N_DEV = 8

ADAM_LR = 0.001
ADAM_B1 = 0.9
ADAM_B2 = 0.999
ADAM_EPS = 1e-08
ADAM_WD = 0.01
ADAM_STEP = 10

ROW_TILE = 512
ATT_BLK = 256
SB_BLOCKS_PER_TRIP = 3
SB_LOG_ZERO = -106.0
SB_FWD_SKEW = False
SB_BWD_SKEW = True
DN_CHUNK = 128
DN_SUB = 16
S5_TILE = 256
S5_CHUNKS = 4
VMEM_LIMIT = 56 * 1024 * 1024

_HIGH = lax.Precision.HIGHEST


def _pallas(body, **kw):
    return pl.pallas_call(body, **kw)


def _cparams(sem):
    return pltpu.CompilerParams(dimension_semantics=sem, vmem_limit_bytes=VMEM_LIMIT)


def _dot(a, b, dims=((1,), (0,))):
    return lax.dot_general(a, b, (dims, ((), ())), preferred_element_type=F32)


def _dot_hi(a, b):
    return lax.dot_general(a, b, (((1,), (0,)), ((), ())), preferred_element_type=F32, precision=_HIGH)


def _split_dot(m_bf16, x):
    hi = x.astype(BF16)
    lo = (x - hi.astype(F32)).astype(BF16)
    return _dot(m_bf16, hi) + _dot(m_bf16, lo)


def _matmul(a, b, *, ta=False, tb=False, tm, tn, tk, name, out_dtypes=(F32,), extras=(), epilogue=None):
    m, k = (a.shape[1], a.shape[0]) if ta else a.shape
    n = b.shape[0] if tb else b.shape[1]
    assert (b.shape[1] if tb else b.shape[0]) == k
    assert m % tm == 0 and n % tn == 0 and k % tk == 0, (name, m, n, k, tm, tn, tk)
    nk = k // tk
    n_ex = len(extras)
    n_out = len(out_dtypes)
    dims = ((0 if ta else 1,), (1 if tb else 0,))

    def finish(acc, ex_refs, o_refs):
        outs = (acc,) if epilogue is None else epilogue(acc, *[r[...] for r in ex_refs])
        for o_ref, o in zip(o_refs, outs):
            o_ref[...] = o.astype(o_ref.dtype)

    def body(*refs):
        a_ref, b_ref = refs[0], refs[1]
        ex_refs = refs[2:2 + n_ex]
        o_refs = refs[2 + n_ex:2 + n_ex + n_out]
        prod = _dot(a_ref[...].astype(BF16), b_ref[...].astype(BF16), dims)
        if nk == 1:
            finish(prod, ex_refs, o_refs)
            return
        acc_ref = refs[-1]
        kk = pl.program_id(2)

        @pl.when(kk == 0)
        def _():
            acc_ref[...] = prod

        @pl.when(kk > 0)
        def _():
            acc_ref[...] += prod

        @pl.when(kk == nk - 1)
        def _():
            finish(acc_ref[...], ex_refs, o_refs)

    a_spec = pl.BlockSpec((tk, tm), lambda j, i, kk: (kk, i)) if ta else pl.BlockSpec((tm, tk), lambda j, i, kk: (i, kk))
    b_spec = pl.BlockSpec((tn, tk), lambda j, i, kk: (j, kk)) if tb else pl.BlockSpec((tk, tn), lambda j, i, kk: (kk, j))
    o_spec = pl.BlockSpec((tm, tn), lambda j, i, kk: (i, j))
    outs = _pallas(
        body, name=name,
        grid=(n // tn, m // tm, nk),
        in_specs=[a_spec, b_spec] + [o_spec] * n_ex,
        out_specs=[o_spec] * n_out,
        out_shape=[jax.ShapeDtypeStruct((m, n), dt) for dt in out_dtypes],
        scratch_shapes=[] if nk == 1 else [pltpu.VMEM((tm, tn), F32)],
        compiler_params=_cparams(("parallel", "parallel", "arbitrary")),
    )(a, b, *extras)
    return outs[0] if n_out == 1 else outs


def _tile(n, cap):
    best = 128
    for t in range(128, min(n, cap) + 1, 128):
        if n % t == 0:
            best = t
    assert n % best == 0, n
    return best


MM_K_CAP = 4096
WGRAD_ROWS = 1536


MM_LHS_TILE_BYTES = 6 * 1024 * 1024


def _row_tile(x, depth):
    tall = 3 * ROW_TILE
    fits = tall * depth * x.dtype.itemsize <= MM_LHS_TILE_BYTES
    return tall if (x.shape[0] % tall == 0 and fits) else ROW_TILE


def _mm_fwd(x, w, name, **kw):
    k, n = w.shape
    tk = _tile(k, MM_K_CAP)
    return _matmul(x, w, tm=_row_tile(x, tk), tn=_tile(n, 1024), tk=tk, name=name, **kw)


def _mm_dgrad(dy, w, name, **kw):
    k, n = w.shape
    tk = _tile(n, MM_K_CAP)
    return _matmul(dy, w, tb=True, tm=_row_tile(dy, tk), tn=_tile(k, 1024), tk=tk, name=name, **kw)


def _mm_wgrad(x, dy, name):
    k, n = x.shape[1], dy.shape[1]
    rows = x.shape[0]
    return _matmul(x, dy, ta=True, tm=_tile(k, 512), tn=_tile(n, 1024),
                   tk=WGRAD_ROWS if rows % WGRAD_ROWS == 0 else ROW_TILE, name=name)


def _rms(x, g):
    r = lax.rsqrt(jnp.mean(x * x, axis=-1, keepdims=True) + EPS)
    return x * r * g


def _rms_bwd(x, g, dy):
    r = lax.rsqrt(jnp.mean(x * x, axis=-1, keepdims=True) + EPS)
    xh = x * r
    dxh = dy * g
    dx = r * (dxh - xh * jnp.mean(dxh * xh, axis=-1, keepdims=True))
    dg = jnp.sum(dy * xh, axis=0, keepdims=True)
    return dx, dg


def _row_spec(width, tile=ROW_TILE):
    return pl.BlockSpec((tile, width), lambda i: (i, 0))


def _vec_spec(width):
    return pl.BlockSpec((1, width), lambda i: (0, 0))


def _norm_pre(hs, g, name, into=None):
    r, d = hs.shape

    def body(x_ref, g_ref, *rest):
        rest[-1][...] = _rms(x_ref[...], g_ref[...]).astype(BF16)

    if into is None:
        return _pallas(body, name=name, grid=(r // ROW_TILE,), in_specs=[_row_spec(d), _vec_spec(d)],
                       out_specs=_row_spec(d), out_shape=jax.ShapeDtypeStruct((r, d), BF16),
                       compiler_params=_cparams(("parallel",)))(hs, g)
    return _pallas(body, name=name, grid=(r // ROW_TILE,),
                   in_specs=[_row_spec(d), _vec_spec(d), pl.BlockSpec(memory_space=pl.ANY)],
                   out_specs=_row_spec(d), out_shape=jax.ShapeDtypeStruct(into.shape, BF16), input_output_aliases={2: 0},
                   compiler_params=_cparams(("parallel",)))(hs, g, into)


def _mm_norm_fwd(a, w, hs, g_post, *, g_pre=None, loss=None, name):
    k, d = w.shape
    r = a.shape[0]
    assert k <= MM_K_CAP and d == hs.shape[1]
    t = ROW_TILE // 2
    nt = r // t

    def body(*refs):
        a_ref, w_ref, hs_ref, gp_ref = refs[:4]
        i = pl.program_id(0)
        m = _dot(a_ref[...].astype(BF16), w_ref[...].astype(BF16))
        gp = gp_ref[...]
        new = hs_ref[...] + _rms(m, gp)
        if loss is None:
            gn_ref, m_ref, o_ref, hn_ref = refs[4:]
            m_ref[...] = m
            o_ref[...] = new
            hn_ref[...] = _rms(new, gn_ref[...]).astype(BF16)
        else:
            t_ref, dhs_ref, dm_ref, dgp_ref, loss_ref = refs[4:]
            live = (i * t + lax.broadcasted_iota(jnp.int32, (t, 1), 0)) >= loss[1]
            diff = jnp.where(live, new - t_ref[...], 0.0)
            dhs = diff * (1.0 / d)
            dhs_ref[...] = dhs
            loss_ref[...] = jnp.full((8, 128), 0.5 / d * jnp.sum(diff * diff), F32)
            dm, dg = _rms_bwd(m, gp, dhs)
            dm_ref[...] = dm.astype(BF16)

            @pl.when(i == 0)
            def _():
                dgp_ref[...] = jnp.zeros_like(dgp_ref)
            dgp_ref[...] += dg

    common_in = [_row_spec(k, t), pl.BlockSpec((k, d), lambda i: (0, 0)), _row_spec(d, t), _vec_spec(d)]
    if loss is None:
        return _pallas(
            body, name=name, grid=(nt,), in_specs=common_in + [_vec_spec(d)],
            out_specs=[_row_spec(d, t)] * 3,
            out_shape=[jax.ShapeDtypeStruct((r, d), F32), jax.ShapeDtypeStruct((r, d), F32), jax.ShapeDtypeStruct((r, d), BF16)],
            compiler_params=_cparams(("parallel",)))(a, w, hs, g_post, g_pre)
    target, first_row = loss
    assert first_row % t == 0
    dhs, dm, dgp, parts = _pallas(
        body, name=name, grid=(nt,),
        in_specs=common_in + [pl.BlockSpec((t, d), lambda i: (jnp.maximum(i - first_row // t, 0), 0))],
        out_specs=[_row_spec(d, t), _row_spec(d, t), _vec_spec(d), pl.BlockSpec((8, 128), lambda i: (i, 0))],
        out_shape=[jax.ShapeDtypeStruct((r, d), F32), jax.ShapeDtypeStruct((r, d), BF16), jax.ShapeDtypeStruct((1, d), F32),
                   jax.ShapeDtypeStruct((nt * 8, 128), F32)],
        compiler_params=_cparams(("arbitrary",)))(a, w, hs, g_post, target)
    return dhs, dm, dgp, jnp.sum(parts[::8, 0])


def _norm_bwd(dhs, *, pre=None, post=None, pad=0, dm_dtype=BF16, dhs_cols=None, name):
    r = dhs.shape[0]
    d = dhs.shape[1] if dhs_cols is None else dhs_cols[0]
    has_pre, has_post = pre is not None, post is not None

    def body(*refs):
        it = iter(refs)
        dhs_ref = next(it)
        if has_pre:
            hs_ref, gn_ref, dhn_ref = next(it), next(it), next(it)
        if has_post:
            m_ref, gp_ref = next(it), next(it)
        if has_pre:
            o_dhs, o_dgn = next(it), next(it)
        if has_post:
            o_dm, o_dgp = next(it), next(it)
        i = pl.program_id(0)
        live = (i * ROW_TILE + lax.broadcasted_iota(jnp.int32, (ROW_TILE, 1), 0)) >= pad
        cur = jnp.where(live, dhs_ref[...], 0.0)
        if has_pre:
            dx, dg = _rms_bwd(hs_ref[...], gn_ref[...], jnp.where(live, dhn_ref[...].astype(F32), 0.0))
            cur = cur + dx
            o_dhs[...] = cur

            @pl.when(i == 0)
            def _():
                o_dgn[...] = jnp.zeros_like(o_dgn)
            o_dgn[...] += dg
        if has_post:
            dm, dg = _rms_bwd(m_ref[...], gp_ref[...], cur)
            o_dm[...] = dm.astype(o_dm.dtype)

            @pl.when(i == 0)
            def _():
                o_dgp[...] = jnp.zeros_like(o_dgp)
            o_dgp[...] += dg

    dhs_spec = _row_spec(d) if dhs_cols is None else pl.BlockSpec((ROW_TILE, d), lambda i: (i, dhs_cols[1]))
    ins, in_specs, out_specs, out_shape = [dhs], [dhs_spec], [], []
    if has_pre:
        ins += list(pre)
        in_specs += [_row_spec(d), _vec_spec(d), _row_spec(d)]
        out_specs += [_row_spec(d), _vec_spec(d)]
        out_shape += [jax.ShapeDtypeStruct((r, d), F32), jax.ShapeDtypeStruct((1, d), F32)]
    if has_post:
        ins += list(post)
        in_specs += [_row_spec(d), _vec_spec(d)]
        out_specs += [_row_spec(d), _vec_spec(d)]
        out_shape += [jax.ShapeDtypeStruct((r, d), dm_dtype), jax.ShapeDtypeStruct((1, d), F32)]
    outs = list(_pallas(body, name=name, grid=(r // ROW_TILE,), in_specs=in_specs, out_specs=out_specs,
                        out_shape=out_shape, compiler_params=_cparams(("arbitrary",)))(*ins))
    dhs_new, dgn = (outs.pop(0), outs.pop(0)) if has_pre else (dhs, None)
    dm, dgp = (outs.pop(0), outs.pop(0)) if has_post else (None, None)
    return dhs_new, dm, dgn, dgp


def _dgrad_norm_bwd(dy, w, dhs, hs, g_pre, *, post=None, add=None, pad=0, name):
    d, n = w.shape
    r = dy.shape[0]
    assert n <= MM_K_CAP and d == dhs.shape[1]
    t = ROW_TILE // 2
    has_post, has_add = post is not None, add is not None
    dims = ((1,), (1,))

    def body(*refs):
        it = iter(refs)
        dy_ref, w_ref = next(it), next(it)
        add_ref = next(it) if has_add else None
        dhs_ref, hs_ref, gn_ref = next(it), next(it), next(it)
        if has_post:
            m_ref, gp_ref = next(it), next(it)
        o_dhs, o_dgn = next(it), next(it)
        if has_post:
            o_dm, o_dgp = next(it), next(it)
        i = pl.program_id(0)
        dhn = _dot(dy_ref[...].astype(BF16), w_ref[...].astype(BF16), dims)
        if has_add:
            dhn = dhn + add_ref[...]
        live = (i * t + lax.broadcasted_iota(jnp.int32, (t, 1), 0)) >= pad
        dx, dg = _rms_bwd(hs_ref[...], gn_ref[...], jnp.where(live, dhn, 0.0))
        cur = jnp.where(live, dhs_ref[...], 0.0) + dx
        o_dhs[...] = cur

        @pl.when(i == 0)
        def _():
            o_dgn[...] = jnp.zeros_like(o_dgn)
        o_dgn[...] += dg
        if has_post:
            dm, dg = _rms_bwd(m_ref[...], gp_ref[...], cur)
            o_dm[...] = dm.astype(BF16)

            @pl.when(i == 0)
            def _():
                o_dgp[...] = jnp.zeros_like(o_dgp)
            o_dgp[...] += dg

    ins = [dy, w] + ([add] if has_add else []) + [dhs, hs, g_pre] + (list(post) if has_post else [])
    in_specs = ([_row_spec(n, t), pl.BlockSpec((d, n), lambda i: (0, 0))] + ([_row_spec(d, t)] if has_add else [])
                + [_row_spec(d, t), _row_spec(d, t), _vec_spec(d)] + ([_row_spec(d, t), _vec_spec(d)] if has_post else []))
    out_specs = [_row_spec(d, t), _vec_spec(d)] + ([_row_spec(d, t), _vec_spec(d)] if has_post else [])
    out_shape = [jax.ShapeDtypeStruct((r, d), F32), jax.ShapeDtypeStruct((1, d), F32)]
    if has_post:
        out_shape += [jax.ShapeDtypeStruct((r, d), BF16), jax.ShapeDtypeStruct((1, d), F32)]
    outs = list(_pallas(body, name=name, grid=(r // t,), in_specs=in_specs, out_specs=out_specs,
                        out_shape=out_shape, compiler_params=_cparams(("arbitrary",)))(*ins))
    return (outs[0], outs[2], outs[1], outs[3]) if has_post else (outs[0], None, outs[1], None)


def _softplus(z):
    return jnp.maximum(z, 0.0) + jnp.log(1.0 + jnp.exp(-jnp.abs(z)))


def _sb_consts(t):
    row = lax.broadcasted_iota(jnp.int32, (t, t), 0)
    col = lax.broadcasted_iota(jnp.int32, (t, t), 1)
    m_up = (col >= row).astype(BF16)
    m_low = (col <= row).astype(BF16)
    return m_up, m_low


def _emit_chains(chains, stages, skew):
    if skew:
        for step in range(len(chains) + len(stages) - 1):
            for si, stage in enumerate(stages):
                if 0 <= step - si < len(chains):
                    stage(chains[step - si])
    else:
        for stage in stages:
            for c in chains:
                stage(c)


def _sb_fwd(q, k, vt3, pad, name, ride=((), ())):
    r = q.shape[0]
    t = ATT_BLK
    nb = r // t
    nbp = -(-(nb + 1) // 8) * 8
    jmin = pad // t
    scale = SB_HEAD_DIM ** -0.5
    n_ride = len(ride[0])

    def body(q_ref, k_ref, vt_ref, *rest):
        ride_in, (o_ref, ss_ref), ride_out = rest[:n_ride], rest[n_ride:n_ride + 2], rest[n_ride + 2:2 * n_ride + 2]
        acc_ref, kn_ref = rest[2 * n_ride + 2:2 * n_ride + 4]
        ride_sems = rest[2 * n_ride + 4:]
        i = pl.program_id(1)
        if n_ride:
            @pl.when((pl.program_id(0) == 0) & (i == 0))
            def _():
                for cp in _exchange_copies(ride_in, ride_out, ride[1], *ride_sems):
                    cp.start()

        @pl.when(i == 0)
        def _():
            def blk(b, m):
                kb = k_ref[pl.ds(pl.multiple_of(b * t, t), t), :].astype(F32)
                return jnp.maximum(m, jnp.max(jnp.sum(kb * kb, axis=1, keepdims=True), axis=0, keepdims=True))
            kn_ref[...] = jnp.broadcast_to(lax.fori_loop(0, nb, blk, jnp.zeros((1, 1), F32)), (8, 128))

        qf = q_ref[...].astype(F32)
        z_bound = scale * jnp.sqrt(jnp.max(jnp.sum(qf * qf, axis=1, keepdims=True)) * jnp.max(kn_ref[...]))

        def need(carry):
            return jnp.maximum(jnp.max(carry[0]), jnp.max(carry[1])) + z_bound >= SB_LOG_ZERO

        qt = qf.T
        sub = lax.broadcasted_iota(jnp.int32, (128, 1), 0)
        m_up, _ = _sb_consts(t)
        kpos0 = lax.broadcasted_iota(jnp.int32, (t, 1), 0)
        qpos = i * t + lax.broadcasted_iota(jnp.int32, (1, t), 1)
        qths = [jnp.where((sub >= 64 * h) & (sub < 64 * (h + 1)), qt * scale, 0.0).astype(BF16) for h in range(2)]
        acc_ref[...] = jnp.zeros_like(acc_ref)

        def sweep(js, carry, masked):
            kbs = [k_ref[pl.ds(pl.multiple_of(j * t, t), t), :] for j in js]
            vts = [vt_ref[0, j] for j in js]
            accs = [acc_ref[0], acc_ref[1]]
            s = list(carry)
            chains = [(n, h) for n in range(len(js)) for h in range(2)]
            masked = [masked] * len(js) if isinstance(masked, bool) else masked
            valid = [(js[n] * t + kpos0 < qpos) & (js[n] * t + kpos0 >= pad) if masked[n] else None for n in range(len(js))]
            zt, inc, saves = {}, {}, []

            def st_scores(c):
                zt[c] = _dot(kbs[c[0]], qths[c[1]])

            def st_cumsum(c):
                lk = -_softplus(zt[c])
                if masked[c[0]]:
                    lk = jnp.where(valid[c[0]], lk, 0.0)
                inc[c] = _split_dot(m_up, lk)

            def st_weights(c):
                n, h = c
                saves.append((h, js[n], s[h]))
                w = jnp.exp(zt[c] + inc[c] + s[h])
                if masked[n]:
                    w = jnp.where(valid[n], w, 0.0)
                accs[h] = accs[h] + _dot(vts[n], w.astype(BF16))
                s[h] = s[h] + inc[c][0:1, :]

            _emit_chains(chains, [st_scores, st_cumsum, st_weights], SB_FWD_SKEW)
            for h, j, val in saves:
                ss_ref[h, 0, pl.ds(j, 1), :] = val
            acc_ref[0] = accs[0]
            acc_ref[1] = accs[1]
            return tuple(s)

        zero = jnp.zeros((1, t), F32)
        bpi = SB_BLOCKS_PER_TRIP
        j, carry = lax.cond(
            i - 1 > jmin,
            lambda: (i - 2, sweep([i, i - 1], (zero, zero), [True, False])),
            lambda: (i - 1, sweep([i], (zero, zero), True)))
        def further(j, carry):
            j, carry = lax.while_loop(
                lambda st: (st[0] - bpi >= jmin) & need(st[1]),
                lambda st: (st[0] - bpi, sweep([st[0] - b for b in range(bpi)], st[1], False)), (j, carry))
            j, carry = lax.while_loop(
                lambda st: (st[0] > jmin) & need(st[1]),
                lambda st: (st[0] - 1, sweep([st[0]], st[1], False)), (j, carry))
            return lax.while_loop(
                lambda st: (st[0] == jmin) & (i > jmin) & need(st[1]),
                lambda st: (st[0] - 1, sweep([st[0]], st[1], True)), (j, carry))[0]

        j = lax.cond((j >= jmin) & need(carry), lambda: further(j, carry), lambda: j)
        first = jnp.full((1, t), j + 1, jnp.int32).astype(F32)
        ss_ref[0, 0, nbp - 1:nbp, :] = first
        ss_ref[1, 0, nbp - 1:nbp, :] = first
        acc = jnp.where(sub < 64, acc_ref[0], acc_ref[1])
        o_ref[...] = acc.T
        if n_ride:
            @pl.when((pl.program_id(0) == 3) & (i == nb - 1))
            def _():
                for cp in _exchange_copies(ride_in, ride_out, ride[1], *ride_sems):
                    cp.wait()

    hbm = pl.BlockSpec(memory_space=pl.ANY)
    outs = _pallas(
        body, name=name, grid=(4, nb),
        in_specs=[pl.BlockSpec((t, 128), lambda hp, i: (i, hp)),
                  pl.BlockSpec((r, 128), lambda hp, i: (0, hp)),
                  pl.BlockSpec((1, nb, 128, t), lambda hp, i: (hp, 0, 0, 0))] + [hbm] * n_ride,
        out_specs=[pl.BlockSpec((t, 128), lambda hp, i: (i, hp)),
                   pl.BlockSpec((2, 1, nbp, t), lambda hp, i: (hp, i, 0, 0))] + [hbm] * n_ride,
        out_shape=[jax.ShapeDtypeStruct((r, SB_WIDTH), F32),
                   jax.ShapeDtypeStruct((8, nb, nbp, t), F32)] + _exchange_shapes(*ride),
        scratch_shapes=[pltpu.VMEM((2, 128, t), F32), pltpu.VMEM((8, 128), F32)] + (_exchange_sems(n_ride) if n_ride else []),
        compiler_params=_cparams(("arbitrary", "arbitrary")),
    )(q, k, vt3, *ride[0])
    return outs[0], outs[1], list(outs[2:])


def _sb_bwd(q, k, v, kt3, ssave, do, pad, name):
    r = q.shape[0]
    t = ATT_BLK
    nb = r // t
    nbp = ssave.shape[2]
    jmin = pad // t
    scale = SB_HEAD_DIM ** -0.5

    def body(q_ref, do_ref, k_ref, v_ref, kt_ref, ss_ref, dq_ref, dk_hbm, dv_hbm, dk_acc, dv_acc, dq_acc, sem):
        hp = pl.program_id(0)
        i = pl.program_id(1)

        @pl.when(i == 0)
        def _():
            dk_acc[...] = jnp.zeros_like(dk_acc)
            dv_acc[...] = jnp.zeros_like(dv_acc)

        qf = q_ref[...].astype(F32)
        dof = do_ref[...]
        qt = qf.T
        dot_ = dof.T
        sub = lax.broadcasted_iota(jnp.int32, (128, 1), 0)
        lane = lax.broadcasted_iota(jnp.int32, (1, 128), 1)
        m_up, m_low = _sb_consts(t)
        kpos0 = lax.broadcasted_iota(jnp.int32, (t, 1), 0)
        qpos = i * t + lax.broadcasted_iota(jnp.int32, (1, t), 1)
        first = jnp.clip(jnp.max(ss_ref[0, 0, nbp - 1:nbp, :]).astype(jnp.int32), jmin, i)
        mid0 = jnp.maximum(first, jmin + 1)
        pair = i - mid0 >= 1
        n_mid = jnp.maximum(i - mid0 - 1, 0)
        n_edge = jnp.where((i > jmin) & (first == jmin), 1, 0)
        in_t = [(sub >= 64 * h) & (sub < 64 * (h + 1)) for h in range(2)]
        in_l = [(lane >= 64 * h) & (lane < 64 * (h + 1)) for h in range(2)]
        qths = [jnp.where(in_t[h], qt * scale, 0.0).astype(BF16) for h in range(2)]
        doths = [jnp.where(in_t[h], dot_, 0.0).astype(BF16) for h in range(2)]
        qhs = [jnp.where(in_l[h], qf * scale, 0.0).astype(BF16) for h in range(2)]
        dohs = [jnp.where(in_l[h], dof, 0.0).astype(BF16) for h in range(2)]
        dq_acc[...] = jnp.zeros_like(dq_acc)

        def sweep(js, carry, masked):
            rows = [pl.ds(pl.multiple_of(j * t, t), t) for j in js]
            kbs = [k_ref[rw, :] for rw in rows]
            vbs = [v_ref[rw, :] for rw in rows]
            kts = [kt_ref[0, j] for j in js]
            sss = [[ss_ref[h, 0, pl.ds(j, 1), :] for h in range(2)] for j in js]
            dv_old = [dv_acc[rw, :] for rw in rows]
            dk_old = [dk_acc[rw, :] for rw in rows]
            dqs = [dq_acc[0], dq_acc[1]]
            ec = list(carry)
            chains = [(n, h) for n in range(len(js)) for h in range(2)]
            masked = [masked] * len(js) if isinstance(masked, bool) else masked
            valid = [(js[n] * t + kpos0 < qpos) & (js[n] * t + kpos0 >= pad) if masked[n] else None for n in range(len(js))]
            zt, dvt, sp, inc, e, big_e = {}, {}, {}, {}, {}, {}

            def st_scores(c):
                zt[c] = _dot(kbs[c[0]], qths[c[1]])
                dvt[c] = _dot(vbs[c[0]], doths[c[1]])

            def st_cumsum(c):
                sp[c] = _softplus(zt[c])
                lk = -sp[c]
                if masked[c[0]]:
                    lk = jnp.where(valid[c[0]], lk, 0.0)
                inc[c] = _split_dot(m_up, lk)

            def st_weights(c):
                n, h = c
                w = jnp.exp(zt[c] + inc[c] + sss[n][h])
                if masked[n]:
                    w = jnp.where(valid[n], w, 0.0)
                dv_old[n] = dv_old[n] + _dot(w.astype(BF16), dohs[h])
                e[c] = w * dvt[c]
                pinc = _split_dot(m_low, e[c])
                big_e[c] = pinc - e[c] + ec[h]
                ec[h] = ec[h] + pinc[t - 1:t, :]

            def st_dscores(c):
                n, h = c
                dz = e[c] - jnp.exp(zt[c] - sp[c]) * (e[c] + big_e[c])
                if masked[n]:
                    dz = jnp.where(valid[n], dz, 0.0)
                dzb = dz.astype(BF16)
                dqs[h] = dqs[h] + _dot(kts[n], dzb)
                dk_old[n] = dk_old[n] + _dot(dzb, qhs[h])

            _emit_chains(chains, [st_scores, st_cumsum, st_weights, st_dscores], SB_BWD_SKEW)
            for n, rw in enumerate(rows):
                dv_acc[rw, :] = dv_old[n]
                dk_acc[rw, :] = dk_old[n]
            dq_acc[0] = dqs[0]
            dq_acc[1] = dqs[1]
            return tuple(ec)

        zero = jnp.zeros((1, t), F32)
        bpi = SB_BLOCKS_PER_TRIP
        carry = lax.fori_loop(0, n_edge, lambda it, c: sweep([jmin + it * 0], c, True), (zero, zero))
        carry = lax.fori_loop(0, n_mid // bpi, lambda it, c: sweep([mid0 + bpi * it + b for b in range(bpi)], c, False), carry)
        n_rem = n_mid % bpi
        carry = lax.fori_loop(0, n_rem, lambda it, c: sweep([i - 1 - n_rem + it], c, False), carry)
        lax.cond(pair, lambda: sweep([i - 1, i], carry, [False, True]), lambda: sweep([i], carry, True))
        dq_ref[...] = (jnp.where(sub < 64, dq_acc[0], dq_acc[1]) * scale).T

        @pl.when(i == nb - 1)
        def _():
            lanes = pl.ds(pl.multiple_of(hp * 128, 128), 128)
            c1 = pltpu.make_async_copy(dk_acc, dk_hbm.at[:, lanes], sem.at[0])
            c2 = pltpu.make_async_copy(dv_acc, dv_hbm.at[:, lanes], sem.at[1])
            c1.start()
            c2.start()
            c1.wait()
            c2.wait()

    return _pallas(
        body, name=name, grid=(4, nb),
        in_specs=[pl.BlockSpec((t, 128), lambda hp, i: (i, hp)),
                  pl.BlockSpec((t, 128), lambda hp, i: (i, hp)),
                  pl.BlockSpec((r, 128), lambda hp, i: (0, hp)),
                  pl.BlockSpec((r, 128), lambda hp, i: (0, hp)),
                  pl.BlockSpec((1, nb, 128, t), lambda hp, i: (hp, 0, 0, 0)),
                  pl.BlockSpec((2, 1, nbp, t), lambda hp, i: (hp, i, 0, 0))],
        out_specs=[pl.BlockSpec((t, 128), lambda hp, i: (i, hp)),
                   pl.BlockSpec(memory_space=pl.ANY), pl.BlockSpec(memory_space=pl.ANY)],
        out_shape=[jax.ShapeDtypeStruct((r, SB_WIDTH), F32),
                   jax.ShapeDtypeStruct((r, SB_WIDTH), F32), jax.ShapeDtypeStruct((r, SB_WIDTH), F32)],
        scratch_shapes=[pltpu.VMEM((r, 128), F32), pltpu.VMEM((r, 128), F32), pltpu.VMEM((2, 128, t), F32),
                        pltpu.SemaphoreType.DMA((2,))],
        compiler_params=_cparams(("arbitrary", "arbitrary")),
    )(q, do, k, v, kt3, ssave)


def _s5_disc(lam_re, lam_im, logdt, btr, bti):
    lr = jnp.minimum(lam_re, -1e-4)
    li = lam_im
    dt = jnp.exp(logdt)
    mag = jnp.exp(lr * dt)
    ang = li * dt
    a_re, a_im = mag * jnp.cos(ang), mag * jnp.sin(ang)
    den = lr * lr + li * li
    nr, ni = a_re - 1.0, a_im
    c_re = (nr * lr + ni * li) / den
    c_im = (ni * lr - nr * li) / den
    return a_re, a_im, c_re * btr - c_im * bti, c_re * bti + c_im * btr


def _s5_prep(lam_re, lam_im, logdt, btr, bti, name):
    ns = lam_re.shape[1]

    def body(lr_ref, li_ref, dt_ref, br_ref, bi_ref, ar_ref, ai_ref, bbr_ref, bbi_ref):
        ar, ai, bbr, bbi = _s5_disc(lr_ref[...], li_ref[...], dt_ref[...], br_ref[...], bi_ref[...])
        ar_ref[...] = ar
        ai_ref[...] = ai
        bbr_ref[...] = bbr
        bbi_ref[...] = bbi

    return _pallas(body, name=name,
                   out_shape=[jax.ShapeDtypeStruct((1, ns), F32)] * 2 + [jax.ShapeDtypeStruct((S5_GROUP, ns), F32)] * 2,
                   )(lam_re, lam_im, logdt, btr, bti)


def _s5_prep_bwd(lam_re, lam_im, logdt, btr, bti, dar, dai, dbbr, dbbi, name):
    ns = lam_re.shape[1]

    def body(lr_ref, li_ref, dt_ref, br_ref, bi_ref, dar_ref, dai_ref, dbr_ref, dbi_ref, o_lr, o_li, o_dt, o_br, o_bi):
        _, vjp = jax.vjp(_s5_disc, lr_ref[...], li_ref[...], dt_ref[...], br_ref[...], bi_ref[...])
        g = vjp((dar_ref[...], dai_ref[...], dbr_ref[...], dbi_ref[...]))
        o_lr[...] = g[0]
        o_li[...] = g[1]
        row = lax.broadcasted_iota(jnp.int32, (ns, ns), 0) // S5_STATE
        col = lax.broadcasted_iota(jnp.int32, (ns, ns), 1) // S5_STATE
        same = (row == col).astype(F32)
        o_dt[...] = _dot_hi(jnp.broadcast_to(g[2], (8, ns)), same)[0:1]
        o_br[...] = g[3]
        o_bi[...] = g[4]

    return _pallas(body, name=name,
                   out_shape=[jax.ShapeDtypeStruct((1, ns), F32)] * 3 + [jax.ShapeDtypeStruct((S5_GROUP, ns), F32)] * 2,
                   compiler_params=pltpu.CompilerParams(vmem_limit_bytes=VMEM_LIMIT),
                   )(lam_re, lam_im, logdt, btr, bti, dar, dai, dbbr, dbbi)


def _s5_scan(br, bi, ar, ai, t, reverse=False, carry=None):
    ng = t // 8
    ns = br.shape[1]
    br, bi = br.reshape(ng, 8, ns), bi.reshape(ng, 8, ns)
    row8 = lax.broadcasted_iota(jnp.int32, (1, 8, 1), 1)
    pr, pi_ = ar, ai
    for k in (1, 2, 4):
        if reverse:
            sr, si, ok = pltpu.roll(br, 8 - k, 1), pltpu.roll(bi, 8 - k, 1), row8 < 8 - k
        else:
            sr, si, ok = pltpu.roll(br, k, 1), pltpu.roll(bi, k, 1), row8 >= k
        sr = jnp.where(ok, sr, 0.0)
        si = jnp.where(ok, si, 0.0)
        br, bi = br + pr * sr - pi_ * si, bi + pr * si + pi_ * sr
        pr, pi_ = pr * pr - pi_ * pi_, 2.0 * pr * pi_
    pw_r, pw_i = [ar], [ai]
    for _ in range(7):
        pw_r.append(pw_r[-1] * ar - pw_i[-1] * ai)
        pw_i.append(pw_r[-2] * ai + pw_i[-1] * ar)
    if reverse:
        pw_r.reverse()
        pw_i.reverse()
    p8r, p8i = jnp.concatenate(pw_r, axis=0), jnp.concatenate(pw_i, axis=0)
    out_r, out_i = [None] * ng, [None] * ng
    order = range(ng - 1, -1, -1) if reverse else range(ng)
    edge = 0 if reverse else 7
    for g in order:
        gr, gi = br[g], bi[g]
        if carry is not None:
            cr, ci = carry
            gr, gi = gr + p8r * cr - p8i * ci, gi + p8r * ci + p8i * cr
        out_r[g], out_i[g] = gr, gi
        carry = (gr[edge:edge + 1], gi[edge:edge + 1])
    return jnp.concatenate(out_r, axis=0), jnp.concatenate(out_i, axis=0)


def _s5_prev_rows(x, first, t):
    ng = t // 8
    ns = x.shape[1]
    x3 = x.reshape(ng, 8, ns)
    last = x3[:, 7:8, :]
    before = jnp.concatenate([first.reshape(1, 1, ns), last[:ng - 1]], axis=0)
    row8 = lax.broadcasted_iota(jnp.int32, (1, 8, 1), 1)
    return jnp.where(row8 == 0, before, pltpu.roll(x3, 1, 1)).reshape(t, ns)


_GELU_C = math.sqrt(2.0 / math.pi)


def _gelu(y):
    th = jnp.tanh(_GELU_C * (y + 0.044715 * y * y * y))
    return 0.5 * y * (1.0 + th), th


def _sigmoid(x):
    return 1.0 / (1.0 + jnp.exp(-x))


def _s5_fwd(u, wb, a, wc, dskip, wglu, bglu, gnorm, name):
    r = u.shape[0]
    t = S5_TILE
    nt = r // t
    ns = wb.shape[2]
    w = S5_WIDTH

    def body(u_ref, wb_ref, a_ref, wc_ref, d_ref, wg_ref, bg_ref, gn_ref, y_ref, on_ref, xs_ref, carry_ref):
        i = pl.program_id(0)
        ar, ai = a_ref[0], a_ref[1]

        @pl.when(i == 0)
        def _():
            carry_ref[...] = jnp.zeros_like(carry_ref)

        u_ = u_ref[...]
        ub = u_.astype(BF16)
        xs_ref[0] = carry_ref[:, 0, :]
        chunks = list(range(S5_CHUNKS))
        sl_s = [slice(c * (ns // S5_CHUNKS), (c + 1) * (ns // S5_CHUNKS)) for c in chunks]
        sl_u = [slice(c * (w // S5_CHUNKS), (c + 1) * (w // S5_CHUNKS)) for c in chunks]
        bu, xs, ys = {}, {}, {}

        def st_inputs(c):
            bu[c] = (_dot(ub[:, sl_u[c]], wb_ref[0, sl_u[c], sl_s[c]]), _dot(ub[:, sl_u[c]], wb_ref[1, sl_u[c], sl_s[c]]))

        def st_scan(c):
            xr, xi = _s5_scan(*bu[c], ar[:, sl_s[c]], ai[:, sl_s[c]], t, carry=(carry_ref[0, :, sl_s[c]], carry_ref[1, :, sl_s[c]]))
            carry_ref[0, :, sl_s[c]] = xr[t - 1:t, :]
            carry_ref[1, :, sl_s[c]] = xi[t - 1:t, :]
            xs[c] = (xr.astype(BF16), xi.astype(BF16))

        def st_outputs(c):
            ys[c] = _dot(xs[c][0], wc_ref[0, sl_s[c], sl_u[c]]) - _dot(xs[c][1], wc_ref[1, sl_s[c], sl_u[c]])

        _emit_chains(chunks, [st_inputs, st_scan, st_outputs], False)
        y = jnp.concatenate([ys[c] for c in chunks], axis=1) + d_ref[...] * u_
        h, _ = _gelu(y)
        gate = _sigmoid(_dot(h.astype(BF16), wg_ref[...]) + bg_ref[...])
        y_ref[...] = y
        on_ref[...] = _rms(h * gate, gn_ref[...]).astype(BF16)

    full = lambda shape: pl.BlockSpec(shape, lambda i: (0,) * len(shape))
    return _pallas(
        body, name=name, grid=(nt,),
        in_specs=[_row_spec(w, t), full((2, w, ns)), full((2, 1, ns)), full((2, ns, w)), full((1, w)),
                  full((w, w)), full((1, w)), full((1, w))],
        out_specs=[_row_spec(w, t), pl.BlockSpec((t, w), lambda i: (i, 1)), pl.BlockSpec((1, 2, ns), lambda i: (i, 0, 0))],
        out_shape=[jax.ShapeDtypeStruct((r, w), F32), jax.ShapeDtypeStruct((r, 2 * w), BF16),
                   jax.ShapeDtypeStruct((nt, 2, ns), F32)],
        scratch_shapes=[pltpu.VMEM((2, 1, ns), F32)],
        compiler_params=_cparams(("arbitrary",)),
    )(u, wb, a, wc, dskip, wglu, bglu, gnorm)


def _s5_bwd(u, y, don, xstart, wb, a, wc, dskip, wglu, bglu, gnorm, name, ride=((), ()), don_block=0):
    r = u.shape[0]
    t = S5_TILE
    nt = r // t
    ns = wb.shape[2]
    w = S5_WIDTH
    nt_dims = ((1,), (1,))
    tn_dims = ((0,), (0,))

    def body(u_ref, y_ref, don_ref, xs_ref, wb_hbm, a_ref, wc_hbm, d_ref, wg_ref, bg_ref, gn_ref,
             du_ref, da_ref, dd_ref, dbg_ref, dgn_ref, dwb_hbm, dwc_hbm, dwg_hbm,
             wb_ref, wc_ref, lam_ref, acc_wb, acc_wc, acc_wg, sem):
        i = pl.program_id(0)
        ar, ai = a_ref[0], a_ref[1]

        @pl.when(i == 0)
        def _():
            c1 = pltpu.make_async_copy(wb_hbm, wb_ref, sem.at[0])
            c2 = pltpu.make_async_copy(wc_hbm, wc_ref, sem.at[1])
            c1.start()
            c2.start()
            lam_ref[...] = jnp.zeros_like(lam_ref)
            acc_wb[...] = jnp.zeros_like(acc_wb)
            acc_wc[...] = jnp.zeros_like(acc_wc)
            acc_wg[...] = jnp.zeros_like(acc_wg)
            da_ref[...] = jnp.zeros_like(da_ref)
            dd_ref[...] = jnp.zeros_like(dd_ref)
            dbg_ref[...] = jnp.zeros_like(dbg_ref)
            dgn_ref[...] = jnp.zeros_like(dgn_ref)
            c1.wait()
            c2.wait()

        u_ = u_ref[...]
        y_ = y_ref[...]
        ub = u_.astype(BF16)
        h, th = _gelu(y_)
        hb = h.astype(BF16)
        wg = wg_ref[...]
        gate = _sigmoid(_dot(hb, wg) + bg_ref[...])
        d_out, dgn = _rms_bwd(h * gate, gn_ref[...], don_ref[...])
        dgn_ref[...] += dgn
        dhw = d_out * h * gate * (1.0 - gate)
        dhwb = dhw.astype(BF16)
        dh = d_out * gate + _dot(dhwb, wg, nt_dims)
        acc_wg[...] += _dot(hb, dhwb, tn_dims)
        dbg_ref[...] += jnp.sum(dhw, axis=0, keepdims=True)
        dgelu = 0.5 * (1.0 + th) + 0.5 * y_ * (1.0 - th * th) * _GELU_C * (1.0 + 3.0 * 0.044715 * y_ * y_)
        dy = dh * dgelu
        dd_ref[...] += jnp.sum(dy * u_, axis=0, keepdims=True)
        dyb = dy.astype(BF16)
        chunks = list(range(S5_CHUNKS))
        sl_s = [slice(c * (ns // S5_CHUNKS), (c + 1) * (ns // S5_CHUNKS)) for c in chunks]
        sl_u = [slice(c * (w // S5_CHUNKS), (c + 1) * (w // S5_CHUNKS)) for c in chunks]
        bu, gx, x_, lam, dus = {}, {}, {}, {}, {}

        def st_inputs(c):
            su, ss = sl_u[c], sl_s[c]
            bu[c] = (_dot(ub[:, su], wb_ref[0, su, ss]), _dot(ub[:, su], wb_ref[1, su, ss]))
            gx[c] = (_dot(dyb[:, su], wc_ref[0, ss, su], nt_dims), -_dot(dyb[:, su], wc_ref[1, ss, su], nt_dims))

        def st_states(c):
            su, ss = sl_u[c], sl_s[c]
            first = (xs_ref[0, 0:1, ss], xs_ref[0, 1:2, ss])
            xr, xi = _s5_scan(*bu[c], ar[:, ss], ai[:, ss], t, carry=first)
            acc_wc[0, ss, su] += _dot(xr.astype(BF16), dyb[:, su], tn_dims)
            acc_wc[1, ss, su] -= _dot(xi.astype(BF16), dyb[:, su], tn_dims)
            x_[c] = (_s5_prev_rows(xr, first[0], t), _s5_prev_rows(xi, first[1], t))

        def st_adjoint(c):
            su, ss = sl_u[c], sl_s[c]
            lr, li = _s5_scan(*gx[c], ar[:, ss], -ai[:, ss], t, reverse=True, carry=(lam_ref[0, :, ss], lam_ref[1, :, ss]))
            lam_ref[0, :, ss] = lr[0:1, :]
            lam_ref[1, :, ss] = li[0:1, :]
            lrb, lib = lr.astype(BF16), li.astype(BF16)
            acc_wb[0, su, ss] += _dot(ub[:, su], lrb, tn_dims)
            acc_wb[1, su, ss] += _dot(ub[:, su], lib, tn_dims)
            dus[c] = _dot(lrb, wb_ref[0, su, ss], nt_dims) + _dot(lib, wb_ref[1, su, ss], nt_dims)
            lam[c] = (lr, li)

        def st_decay(c):
            ss = sl_s[c]
            (lr, li), (xpr, xpi) = lam[c], x_[c]
            da_ref[0, :, ss] += jnp.sum(lr * xpr + li * xpi, axis=0, keepdims=True)
            da_ref[1, :, ss] += jnp.sum(li * xpr - lr * xpi, axis=0, keepdims=True)

        _emit_chains(chunks, [st_inputs, st_states, st_adjoint, st_decay], False)
        du_ref[...] = d_ref[...] * dy + jnp.concatenate([dus[c] for c in chunks], axis=1)

        @pl.when(i == nt - 1)
        def _():
            cps = [pltpu.make_async_copy(acc_wb, dwb_hbm, sem.at[0]), pltpu.make_async_copy(acc_wc, dwc_hbm, sem.at[1]),
                   pltpu.make_async_copy(acc_wg, dwg_hbm, sem.at[2])]
            for c in cps:
                c.start()
            for c in cps:
                c.wait()

    n_ride = len(ride[0])
    n_in, n_out, n_scratch = 11, 8, 7

    def body_with_ride(*refs):
        ins, rest = refs[:n_in], refs[n_in:]
        ride_in, rest = rest[:n_ride], rest[n_ride:]
        outs, rest = rest[:n_out], rest[n_out:]
        ride_out, rest = rest[:n_ride], rest[n_ride:]
        scratch, ride_sems = rest[:n_scratch], rest[n_scratch:]
        if n_ride:
            @pl.when(pl.program_id(0) == 0)
            def _():
                for cp in _exchange_copies(ride_in, ride_out, ride[1], *ride_sems):
                    cp.start()
        body(*ins, *outs, *scratch)
        if n_ride:
            @pl.when(pl.program_id(0) == nt - 1)
            def _():
                for cp in _exchange_copies(ride_in, ride_out, ride[1], *ride_sems):
                    cp.wait()

    rev = lambda i: (nt - 1 - i, 0)
    full = lambda shape: pl.BlockSpec(shape, lambda i: (0,) * len(shape))
    hbm = pl.BlockSpec(memory_space=pl.ANY)
    outs = _pallas(
        body_with_ride, name=name, grid=(nt,),
        in_specs=[pl.BlockSpec((t, w), rev), pl.BlockSpec((t, w), rev), pl.BlockSpec((t, w), lambda i: (nt - 1 - i, don_block)),
                  pl.BlockSpec((1, 2, ns), lambda i: (nt - 1 - i, 0, 0)), hbm, full((2, 1, ns)), hbm, full((1, w)),
                  full((w, w)), full((1, w)), full((1, w))] + [hbm] * n_ride,
        out_specs=[pl.BlockSpec((t, w), rev), full((2, 1, ns)), full((1, w)), full((1, w)), full((1, w)), hbm, hbm, hbm]
        + [hbm] * n_ride,
        out_shape=[jax.ShapeDtypeStruct((r, w), F32), jax.ShapeDtypeStruct((2, 1, ns), F32)]
        + [jax.ShapeDtypeStruct((1, w), F32)] * 3
        + [jax.ShapeDtypeStruct((2, w, ns), F32), jax.ShapeDtypeStruct((2, ns, w), F32), jax.ShapeDtypeStruct((w, w), F32)]
        + _exchange_shapes(*ride),
        scratch_shapes=[pltpu.VMEM((2, w, ns), BF16), pltpu.VMEM((2, ns, w), BF16), pltpu.VMEM((2, 1, ns), F32),
                        pltpu.VMEM((2, w, ns), F32), pltpu.VMEM((2, ns, w), F32), pltpu.VMEM((w, w), F32),
                        pltpu.SemaphoreType.DMA((3,))] + (_exchange_sems(n_ride) if n_ride else []),
        compiler_params=_cparams(("arbitrary",)),
    )(u, y, don, xstart, wb, a, wc, dskip, wglu, bglu, gnorm, *ride[0])
    return tuple(outs[:n_out]) + (list(outs[n_out:]),)


def _s5_expand(lam_re, lam_im, log_dt, b_re, b_im, c_re, c_im):
    g, n, p = S5_GROUPS, S5_STATE, S5_GROUP
    ns = g * n
    rows = lambda x: x.reshape(1, ns)
    logdt = jnp.repeat(log_dt.reshape(g), n).reshape(1, ns)
    btr = b_re.reshape(ns, p).T
    bti = b_im.reshape(ns, p).T
    ctr = c_re.transpose(0, 2, 1).reshape(ns, p)
    cti = c_im.transpose(0, 2, 1).reshape(ns, p)
    mask = (jnp.arange(g * p)[:, None] // p) == (jnp.arange(ns)[None, :] // n)
    return rows(lam_re), rows(lam_im), logdt, btr, bti, ctr, cti, mask


def _s5_block_diag_b(bb, mask):
    return jnp.where(mask, jnp.tile(bb, (S5_GROUPS, 1)), 0.0)


def _s5_block_diag_c(ct, mask):
    return jnp.where(mask.T, jnp.tile(ct, (1, S5_GROUPS)), 0.0)


def _s5_diag_of_b(dwb, mask):
    return jnp.where(mask, dwb, 0.0).reshape(S5_GROUPS, S5_GROUP, -1).sum(0)


def _s5_diag_of_c(dwc, mask):
    ns = dwc.shape[0]
    return jnp.where(mask.T, dwc, 0.0).reshape(ns, S5_GROUPS, S5_GROUP).sum(1)


DN_PRE_TILE = 256
_DN_QKV = 3 * DN_WIDTH


def _halo_specs(width, tile, nt, prev):
    per = tile // 8
    if prev:
        return pl.BlockSpec((8, width), lambda i: (jnp.maximum(i * per - 1, 0), 0))
    return pl.BlockSpec((8, width), lambda i: (jnp.minimum((i + 1) * per, nt * per - 1), 0))


def _shift_down(x, halo, s, t):
    xx = jnp.concatenate([halo, x], axis=0)
    return pltpu.roll(xx, s, 0)[8:]


def _shift_up(x, halo, s, t):
    xx = jnp.concatenate([x, halo], axis=0)
    return pltpu.roll(xx, t + 8 - s, 0)[:t]


def _silu(x):
    s = _sigmoid(x)
    return x * s, s


def _dn_gates(ab, alog, dtb, live):
    lane = lax.broadcasted_iota(jnp.int32, (1, 128), 1)
    g = -jnp.exp(alog) * _softplus(ab + dtb)
    beta = _sigmoid(ab)
    return jnp.where(live & (lane < DN_HEADS), g, jnp.where(live & (lane < 2 * DN_HEADS), beta, 0.0))


def _dn_pre_fwd(proj, ab, conv_w, alog, dtb, pad, name):
    r = proj.shape[0]
    t = DN_PRE_TILE
    nt = r // t
    scale = DN_HEAD_DIM ** -0.5

    def body(x_ref, halo_ref, ab_ref, w_ref, al_ref, dt_ref, q_ref, k_ref, v_ref, gb_ref):
        i = pl.program_id(0)
        act, _ = _silu(_dn_conv(x_ref[...], jnp.where(i > 0, halo_ref[...], 0.0), w_ref[...], t))
        for hd in range(DN_HEADS):
            sl = slice(hd * 128, (hd + 1) * 128)
            for base, o_ref, sc in ((0, q_ref, scale), (DN_WIDTH, k_ref, 1.0)):
                xh = act[:, base + hd * 128: base + (hd + 1) * 128]
                o_ref[:, sl] = (xh * (lax.rsqrt(jnp.sum(xh * xh, axis=-1, keepdims=True) + EPS) * sc)).astype(BF16)
        v_ref[...] = act[:, 2 * DN_WIDTH:].astype(BF16)
        rows = i * t + lax.broadcasted_iota(jnp.int32, (t, 1), 0)
        gb_ref[...] = _dn_gates(ab_ref[...], al_ref[...], dt_ref[...], rows >= pad)

    return _pallas(
        body, name=name, grid=(nt,),
        in_specs=[pl.BlockSpec((t, _DN_QKV), lambda i: (i, 0)), _halo_specs(_DN_QKV, t, nt, True), _row_spec(128, t),
                  pl.BlockSpec((DN_CONV, _DN_QKV), lambda i: (0, 0)), _vec_spec(128), _vec_spec(128)],
        out_specs=[_row_spec(DN_WIDTH, t), _row_spec(DN_WIDTH, t), _row_spec(DN_WIDTH, t), _row_spec(128, t)],
        out_shape=[jax.ShapeDtypeStruct((r, DN_WIDTH), BF16)] * 3 + [jax.ShapeDtypeStruct((r, 128), F32)],
        compiler_params=_cparams(("parallel",)),
    )(proj, proj, ab, conv_w, alog, dtb)


def _dn_conv(x, halo, w, t):
    co = w[DN_CONV - 1:DN_CONV] * x
    for tap in range(DN_CONV - 1):
        co = co + w[tap:tap + 1] * _shift_down(x, halo, DN_CONV - 1 - tap, t)
    return co


def _dn_pre_bwd(proj, conv_w, dq, dk, dv, dgb, ab, alog, dtb, pad, name):
    r = proj.shape[0]
    t = DN_PRE_TILE
    nt = r // t
    scale = DN_HEAD_DIM ** -0.5

    def body(x_ref, halo_ref, w_ref, dq_ref, dk_ref, dv_ref, dgb_ref, ab_ref, al_ref, dt_ref, dco_ref, dab_ref, dal_ref,
             ddt_ref):
        i = pl.program_id(0)

        @pl.when(i == 0)
        def _():
            dal_ref[...] = jnp.zeros_like(dal_ref)
            ddt_ref[...] = jnp.zeros_like(ddt_ref)

        co_ = _dn_conv(x_ref[...], jnp.where(i > 0, halo_ref[...], 0.0), w_ref[...], t)
        act, sg = _silu(co_)
        dsilu = sg * (1.0 + co_ * (1.0 - sg))
        for hd in range(DN_HEADS):
            sl = slice(hd * 128, (hd + 1) * 128)
            for base, d_ref, sc in ((0, dq_ref, scale), (DN_WIDTH, dk_ref, 1.0)):
                cs = slice(base + hd * 128, base + (hd + 1) * 128)
                xh = act[:, cs]
                rn = lax.rsqrt(jnp.sum(xh * xh, axis=-1, keepdims=True) + EPS)
                xhat = xh * rn
                dy = d_ref[:, sl]
                dx = (sc * rn) * (dy - xhat * jnp.sum(dy * xhat, axis=-1, keepdims=True))
                dco_ref[:, cs] = dx * dsilu[:, cs]
        dco_ref[:, 2 * DN_WIDTH:] = dv_ref[...] * dsilu[:, 2 * DN_WIDTH:]
        rows = i * t + lax.broadcasted_iota(jnp.int32, (t, 1), 0)
        live = rows >= pad
        lane = lax.broadcasted_iota(jnp.int32, (1, 128), 1)
        ab_ = ab_ref[...]
        dgb_ = dgb_ref[...]
        is_g = live & (lane < DN_HEADS)
        is_b = live & (lane >= DN_HEADS) & (lane < 2 * DN_HEADS)
        arg = ab_ + dt_ref[...]
        ea = jnp.exp(al_ref[...])
        da = jnp.where(is_g, -dgb_ * ea * _sigmoid(arg), 0.0)
        beta = _sigmoid(ab_)
        dab_ref[...] = (da + jnp.where(is_b, dgb_ * beta * (1.0 - beta), 0.0)).astype(BF16)
        ddt_ref[...] += jnp.sum(da, axis=0, keepdims=True)
        dal_ref[...] += jnp.sum(jnp.where(is_g, -dgb_ * ea * _softplus(arg), 0.0), axis=0, keepdims=True)

    return _pallas(
        body, name=name, grid=(nt,),
        in_specs=[pl.BlockSpec((t, _DN_QKV), lambda i: (i, 0)), _halo_specs(_DN_QKV, t, nt, True),
                  pl.BlockSpec((DN_CONV, _DN_QKV), lambda i: (0, 0)),
                  _row_spec(DN_WIDTH, t), _row_spec(DN_WIDTH, t), _row_spec(DN_WIDTH, t),
                  _row_spec(128, t), _row_spec(128, t), _vec_spec(128), _vec_spec(128)],
        out_specs=[_row_spec(_DN_QKV, t), _row_spec(128, t), _vec_spec(128), _vec_spec(128)],
        out_shape=[jax.ShapeDtypeStruct((r, _DN_QKV), F32), jax.ShapeDtypeStruct((r, 128), BF16),
                   jax.ShapeDtypeStruct((1, 128), F32), jax.ShapeDtypeStruct((1, 128), F32)],
        compiler_params=_cparams(("arbitrary",)),
    )(proj, proj, conv_w, dq, dk, dv, dgb, ab, alog, dtb)


def _dn_conv_bwd(dco, proj, conv_w, dproj, name):
    r = dco.shape[0]
    t = DN_PRE_TILE
    nt = r // t

    def body(d_ref, dh_ref, x_ref, xh_ref, w_ref, dproj_in, dx_ref, dw_ref):
        i = pl.program_id(0)

        @pl.when(i == 0)
        def _():
            dw_ref[...] = jnp.zeros_like(dw_ref)

        d = d_ref[...]
        dhalo = jnp.where(i < nt - 1, dh_ref[...], 0.0)
        x = x_ref[...]
        xhalo = jnp.where(i > 0, xh_ref[...], 0.0)
        w = w_ref[...]
        dx = w[3:4] * d
        dws = [None] * DN_CONV
        dws[3] = jnp.sum(d * x, axis=0, keepdims=True)
        for tap in range(DN_CONV - 1):
            s = DN_CONV - 1 - tap
            dx = dx + w[tap:tap + 1] * _shift_up(d, dhalo, s, t)
            dws[tap] = jnp.sum(d * _shift_down(x, xhalo, s, t), axis=0, keepdims=True)
        dx_ref[...] = dx.astype(BF16)
        dw_ref[...] += jnp.concatenate(dws + [jnp.zeros((8 - DN_CONV, _DN_QKV), F32)], axis=0)

    return _pallas(
        body, name=name, grid=(nt,),
        in_specs=[_row_spec(_DN_QKV, t), _halo_specs(_DN_QKV, t, nt, False),
                  pl.BlockSpec((t, _DN_QKV), lambda i: (i, 0)), _halo_specs(_DN_QKV, t, nt, True),
                  pl.BlockSpec((DN_CONV, _DN_QKV), lambda i: (0, 0)), pl.BlockSpec(memory_space=pl.ANY)],
        out_specs=[_row_spec(_DN_QKV, t), pl.BlockSpec((8, _DN_QKV), lambda i: (0, 0))],
        out_shape=[jax.ShapeDtypeStruct(dproj.shape, BF16), jax.ShapeDtypeStruct((8, _DN_QKV), F32)],
        input_output_aliases={5: 0},
        compiler_params=_cparams(("arbitrary",)),
    )(dco, dco, proj, proj, conv_w, dproj)


def _split3(x):
    hi = x.astype(BF16)
    return hi, (x - hi.astype(F32)).astype(BF16)


def _dot3s(a, b, dims=((1,), (0,))):
    return _dot(a[0], b[0], dims) + (_dot(a[0], b[1], dims) + _dot(a[1], b[0], dims))


def _dot2s(a, b, dims=((1,), (0,))):
    return _dot(a[0], b[0], dims) + _dot(a[1], b[0], dims)


def _dot3(a, b, dims=((1,), (0,))):
    return _dot3s(_split3(a), _split3(b), dims)


def _dn_inverse_many(n_mats):
    c = n_mats[0].shape[0]
    row = lax.broadcasted_iota(jnp.int32, (c, c), 0)
    col = lax.broadcasted_iota(jnp.int32, (c, c), 1)
    eye = (row == col).astype(F32)
    same = row // DN_SUB == col // DN_SUB
    nds = [jnp.where(same, n, 0.0) for n in n_mats]
    nos = [n - nd for n, nd in zip(n_mats, nds)]

    def geometric(bs, order):
        xs = [eye + b for b in bs]
        sp = [_split3(b) for b in bs]
        k = 2
        while k < order:
            sp = [_split3(_dot2s(s_, s_)) for s_ in sp]
            xs = [x + _dot2s(_split3(x), s_) for x, s_ in zip(xs, sp)]
            k *= 2
        return xs

    tds = [_split3(td) for td in geometric([-nd for nd in nds], DN_SUB)]
    ms = [_dot3s(td, _split3(no)) for td, no in zip(tds, nos)]
    xs = geometric([-m for m in ms], c // DN_SUB)
    return [_dot3s(_split3(x), td) for x, td in zip(xs, tds)]


def _dn_chunk_shared(gb_ref, gbt_ref):
    c = DN_CHUNK
    row = lax.broadcasted_iota(jnp.int32, (c, c), 0)
    col = lax.broadcasted_iota(jnp.int32, (c, c), 1)
    gbv = gb_ref[...]
    gam_all = _split_dot((row >= col).astype(BF16), gbv)
    hi, lo = _split3(gbt_ref[...])
    tri_t = (row <= col).astype(BF16)
    return dict(row=row, col=col, gbv=gbv, gam_all=gam_all, gam_rows=_dot(hi, tri_t) + _dot(lo, tri_t),
                lane=lax.broadcasted_iota(jnp.int32, (1, 128), 1))


def _dn_chunk_common(q, k, v, sh, h):
    c = DN_CHUNK
    row, col, lane = sh["row"], sh["col"], sh["lane"]
    q, k, v = q.astype(F32), k.astype(F32), v.astype(F32)
    gam = jnp.sum(jnp.where(lane == h, sh["gam_all"], 0.0), axis=1, keepdims=True)
    beta = jnp.sum(jnp.where(lane == h + DN_HEADS, sh["gbv"], 0.0), axis=1, keepdims=True)
    gam_row = sh["gam_rows"][h:h + 1]
    dec = jnp.where(row >= col, jnp.exp(jnp.minimum(gam - gam_row, 0.0)), 0.0)
    kb, qb = k.astype(BF16), q.astype(BF16)
    nt_dims = ((1,), (1,))
    kk = _dot(kb, kb, nt_dims)
    qk = _dot(qb, kb, nt_dims)
    eg = jnp.exp(gam)
    gam_l = gam[c - 1:c, :]
    return dict(q=q, k=k, v=v, qb=qb, kb=kb, gam=gam, beta=beta, dec=dec, kk=kk, qk=qk, eg=eg, gam_l=gam_l,
                row=row, col=col, lane=lane, att=qk * dec, qg=q * eg, kt=k * jnp.exp(gam_l - gam),
                rhs=jnp.concatenate([v * beta, k * (beta * eg)], axis=1))


def _dn_fwd(q, k, v, gb, gbt, name):
    r = q.shape[0]
    c = DN_CHUNK
    nc = r // c
    dh = DN_HEAD_DIM
    tn_dims = ((0,), (0,))

    def body(q_ref, k_ref, v_ref, gb_ref, gbt_ref, o_ref, ss_ref, ts_ref, s_ref):
        @pl.when(pl.program_id(0) == 0)
        def _():
            s_ref[...] = jnp.zeros_like(s_ref)

        heads = list(range(DN_HEADS))
        sl = [slice(h * dh, (h + 1) * dh) for h in heads]
        sh = _dn_chunk_shared(gb_ref, gbt_ref)
        zs = [_dn_chunk_common(q_ref[:, sl[h]], k_ref[:, sl[h]], v_ref[:, sl[h]], sh, h) for h in heads]
        t_invs = _dn_inverse_many([jnp.where(sh["row"] > sh["col"], z["beta"] * z["kk"] * z["dec"], 0.0) for z in zs])
        sols = [_dot3(t_inv, z["rhs"]) for t_inv, z in zip(t_invs, zs)]
        ss = [s_ref[h] for h in heads]
        sbs = [s.astype(BF16) for s in ss]
        vnbs = [(sol[:, :dh] - _dot(sol[:, dh:].astype(BF16), sb)).astype(BF16) for sol, sb in zip(sols, sbs)]
        for h in heads:
            o_ref[:, sl[h]] = _dot(zs[h]["qg"].astype(BF16), sbs[h]) + _dot(zs[h]["att"].astype(BF16), vnbs[h])
        for h in heads:
            ss_ref[0, h] = ss[h]
            ts_ref[0, h] = t_invs[h]
            s_ref[h] = ss[h] * jnp.exp(zs[h]["gam_l"]) + _dot(zs[h]["kt"].astype(BF16), vnbs[h], tn_dims)

    blk = pl.BlockSpec((c, DN_WIDTH), lambda ci: (ci, 0))
    sav = pl.BlockSpec((1, DN_HEADS, dh, dh), lambda ci: (ci, 0, 0, 0))
    return _pallas(
        body, name=name, grid=(nc,),
        in_specs=[blk, blk, blk, pl.BlockSpec((c, 128), lambda ci: (ci, 0)), pl.BlockSpec((16, c), lambda ci: (0, ci))],
        out_specs=[blk, sav, sav],
        out_shape=[jax.ShapeDtypeStruct((r, DN_WIDTH), F32), jax.ShapeDtypeStruct((nc, DN_HEADS, dh, dh), F32),
                   jax.ShapeDtypeStruct((nc, DN_HEADS, dh, dh), F32)],
        scratch_shapes=[pltpu.VMEM((DN_HEADS, dh, dh), F32)],
        compiler_params=_cparams(("arbitrary",)),
    )(q, k, v, gb, gbt)


def _dn_bwd(q, k, v, gb, gbt, ssave, tsave, do, name):
    r = q.shape[0]
    c = DN_CHUNK
    nc = r // c
    dh = DN_HEAD_DIM
    nt_dims = ((1,), (1,))
    tn_dims = ((0,), (0,))

    def body(q_ref, k_ref, v_ref, gb_ref, gbt_ref, ss_ref, ts_ref, do_ref, dq_ref, dk_ref, dv_ref, dgb_ref, ds_ref):
        @pl.when(pl.program_id(0) == 0)
        def _():
            ds_ref[...] = jnp.zeros_like(ds_ref)

        heads = list(range(DN_HEADS))
        sl = [slice(h * dh, (h + 1) * dh) for h in heads]
        sh = _dn_chunk_shared(gb_ref, gbt_ref)
        row, col, lane = sh["row"], sh["col"], sh["lane"]
        rs = lambda x: jnp.sum(x, axis=1, keepdims=True)
        tot = lambda x: jnp.sum(rs(x), axis=0, keepdims=True)
        st = [dict() for _ in heads]
        dgb_parts = []

        def s_common(h):
            st[h].update(_dn_chunk_common(q_ref[:, sl[h]], k_ref[:, sl[h]], v_ref[:, sl[h]], sh, h))
            st[h]["t"] = _split3(ts_ref[0, h])

        def s_sol(h):
            st[h]["sol"] = _dot3s(st[h]["t"], _split3(st[h]["rhs"]))

        def s_state(h):
            z = st[h]
            sol = z["sol"]
            kcd = sol[:, dh:]
            s = ss_ref[0, h]
            sb = s.astype(BF16)
            vnb = (sol[:, :dh] - _dot(kcd.astype(BF16), sb)).astype(BF16)
            ds_next = ds_ref[h]
            dsb = ds_next.astype(BF16)
            dob = do_ref[:, sl[h]].astype(BF16)
            z["dqg"] = _dot(dob, sb, nt_dims)
            ds = _dot(z["qg"].astype(BF16), dob, tn_dims)
            z["d_att"] = jnp.where(row >= col, _dot(dob, vnb, nt_dims), 0.0)
            dvn = _dot(z["att"].astype(BF16), dob, tn_dims) + _dot(z["kt"].astype(BF16), dsb)
            z["dkt"] = _dot(vnb, dsb, nt_dims)
            eg_l = jnp.exp(z["gam_l"])
            ds = ds + ds_next * eg_l
            z["dgam_l"] = tot(ds_next * s) * eg_l
            dvnb = dvn.astype(BF16)
            dkcd = -_dot(dvnb, sb, nt_dims)
            ds_ref[h] = ds - _dot(kcd.astype(BF16), dvnb, tn_dims)
            z["dsol"] = jnp.concatenate([dvn, dkcd], axis=1)

        def s_drhs(h):
            st[h]["drhs"] = _dot3s(st[h]["t"], _split3(st[h]["dsol"]), tn_dims)

        def s_dn(h):
            z = st[h]
            z["dn"] = jnp.where(row > col, -_dot3(z["drhs"], z["sol"], nt_dims), 0.0)

        def s_rest(h):
            z = st[h]
            k_, v_, kb, qb = z["k"], z["v"], z["kb"], z["qb"]
            beta, eg, dec, kk, qk, gam, gam_l = z["beta"], z["eg"], z["dec"], z["kk"], z["qk"], z["gam"], z["gam_l"]
            dn, d_att, dqg, dkt = z["dn"], z["d_att"], z["dqg"], z["dkt"]
            drv, drk = z["drhs"][:, :dh], z["drhs"][:, dh:]
            s_rkk = rs(drk * k_)
            dv_ref[:, sl[h]] = drv * beta
            dbeta = rs(drv * v_) + s_rkk * eg + rs(dn * kk * dec)
            dk = drk * (beta * eg)
            dgam = s_rkk * beta * eg
            dkk = (dn * beta * dec).astype(BF16)
            dd = dn * beta * kk + d_att * qk
            dqk = (d_att * dec).astype(BF16)
            dq_ref[:, sl[h]] = _dot(dqk, kb) + dqg * eg
            dk = dk + _dot(dqk, qb, tn_dims) + _dot(dkk, kb) + _dot(dkk, kb, tn_dims)
            w = dd * dec
            wh, wl = _split3(w)
            ones = jnp.ones((c, 128), BF16)
            col_sum = (_dot(wh, ones, tn_dims) + _dot(wl, ones, tn_dims))[:, 0:1]
            dgam = dgam + rs(w) - col_sum + rs(dqg * z["qg"]) - rs(dkt * z["kt"])
            dk_ref[:, sl[h]] = dk + dkt * jnp.exp(gam_l - gam)
            dgam_l = z["dgam_l"] + tot(dkt * z["kt"])
            rowc = lax.broadcasted_iota(jnp.int32, (c, 1), 0)
            dgam = dgam + jnp.where(rowc == c - 1, dgam_l, 0.0)
            dg = _split_dot((row <= col).astype(BF16), jnp.broadcast_to(dgam, (c, 128)))[:, 0:1]
            dgb_parts.append(jnp.where(lane == h, dg, 0.0) + jnp.where(lane == h + DN_HEADS, dbeta, 0.0))

        _emit_chains(heads, [s_common, s_sol, s_state, s_drhs, s_dn, s_rest], False)
        dgb = dgb_parts[0]
        for part in dgb_parts[1:]:
            dgb = dgb + part
        dgb_ref[...] = dgb

    blk = pl.BlockSpec((c, DN_WIDTH), lambda ci: (nc - 1 - ci, 0))
    sav = pl.BlockSpec((1, DN_HEADS, dh, dh), lambda ci: (nc - 1 - ci, 0, 0, 0))
    gspec = pl.BlockSpec((c, 128), lambda ci: (nc - 1 - ci, 0))
    return _pallas(
        body, name=name, grid=(nc,),
        in_specs=[blk, blk, blk, gspec, pl.BlockSpec((16, c), lambda ci: (0, nc - 1 - ci)), sav, sav, blk],
        out_specs=[blk, blk, blk, gspec],
        out_shape=[jax.ShapeDtypeStruct((r, DN_WIDTH), F32)] * 3 + [jax.ShapeDtypeStruct((r, 128), F32)],
        scratch_shapes=[pltpu.VMEM((DN_HEADS, dh, dh), F32)],
        compiler_params=_cparams(("arbitrary",)),
    )(q, k, v, gb, gbt, ssave, tsave, do)


def _dn_post_fwd(o, proj, g, name):
    r = o.shape[0]

    def body(o_ref, z_ref, g_ref, y_ref):
        g_ = g_ref[...]
        for hd in range(DN_HEADS):
            sl = slice(hd * 128, (hd + 1) * 128)
            sz, _ = _silu(z_ref[:, sl])
            y_ref[:, sl] = (_rms(o_ref[:, sl], g_) * sz).astype(BF16)

    return _pallas(body, name=name, grid=(r // ROW_TILE,),
                   in_specs=[_row_spec(DN_WIDTH), pl.BlockSpec((ROW_TILE, DN_WIDTH), lambda i: (i, 3)), _vec_spec(128)],
                   out_specs=_row_spec(DN_WIDTH), out_shape=jax.ShapeDtypeStruct((r, DN_WIDTH), BF16),
                   compiler_params=_cparams(("parallel",)))(o, proj, g)


def _dn_post_bwd(o, proj, g, dy, name):
    r = o.shape[0]

    def body(o_ref, z_ref, g_ref, dy_ref, do_ref, dz_ref, dg_ref):
        @pl.when(pl.program_id(0) == 0)
        def _():
            dg_ref[...] = jnp.zeros_like(dg_ref)

        g_ = g_ref[...]
        for hd in range(DN_HEADS):
            sl = slice(hd * 128, (hd + 1) * 128)
            z_ = z_ref[:, sl]
            sz, sg = _silu(z_)
            dy_ = dy_ref[:, sl]
            o_ = o_ref[:, sl]
            dz_ref[:, sl] = (dy_ * _rms(o_, g_) * (sg * (1.0 + z_ * (1.0 - sg)))).astype(BF16)
            dx, dg = _rms_bwd(o_, g_, dy_ * sz)
            do_ref[:, sl] = dx
            dg_ref[...] += dg

    return _pallas(body, name=name, grid=(r // ROW_TILE,),
                   in_specs=[_row_spec(DN_WIDTH), pl.BlockSpec((ROW_TILE, DN_WIDTH), lambda i: (i, 3)), _vec_spec(128),
                             _row_spec(DN_WIDTH)],
                   out_specs=[_row_spec(DN_WIDTH), pl.BlockSpec((ROW_TILE, DN_WIDTH), lambda i: (i, 3)), _vec_spec(128)],
                   out_shape=[jax.ShapeDtypeStruct((r, DN_WIDTH), F32), jax.ShapeDtypeStruct((r, 4 * DN_WIDTH), BF16),
                              jax.ShapeDtypeStruct((1, 128), F32)],
                   compiler_params=_cparams(("arbitrary",)))(o, proj, g, dy)


def _exchange(arrays, scatter, name):
    n = len(arrays)

    def body(*refs):
        copies = _exchange_copies(refs[:n], refs[n:2 * n], scatter, *refs[2 * n:])
        for cp in copies:
            cp.start()
        for cp in copies:
            cp.wait()

    hbm = pl.BlockSpec(memory_space=pl.ANY)
    return _pallas(
        body, name=name, in_specs=[hbm] * n, out_specs=[hbm] * n, out_shape=_exchange_shapes(arrays, scatter),
        scratch_shapes=_exchange_sems(n),
    )(*arrays)


def _exchange_shapes(arrays, scatter):
    return [jax.ShapeDtypeStruct((N_DEV,) + (a.shape[1:] if sc else a.shape), a.dtype) for a, sc in zip(arrays, scatter)]


def _exchange_sems(n):
    return [pltpu.SemaphoreType.DMA((n * N_DEV,)), pltpu.SemaphoreType.DMA((n * N_DEV,)), pltpu.SemaphoreType.DMA((n,))]


def _exchange_copies(in_refs, out_refs, scatter, send_sems, recv_sems, local_sems):
    mx, my, mc = lax.axis_index("x"), lax.axis_index("y"), lax.axis_index("c")
    me = 4 * mx + 2 * my + mc
    copies = []
    for a in range(len(in_refs)):
        src_own = in_refs[a].at[me] if scatter[a] else in_refs[a]
        copies.append(pltpu.make_async_copy(src_own, out_refs[a].at[me], local_sems.at[a]))
        for kbits in range(1, N_DEV):
            px = lax.rem(mx + ((kbits >> 2) & 1), 2)
            py = lax.rem(my + ((kbits >> 1) & 1), 2)
            pc = lax.rem(mc + (kbits & 1), 2)
            src = in_refs[a].at[4 * px + 2 * py + pc] if scatter[a] else in_refs[a]
            copies.append(pltpu.make_async_remote_copy(
                src_ref=src, dst_ref=out_refs[a].at[me],
                send_sem=send_sems.at[a * N_DEV + kbits], recv_sem=recv_sems.at[a * N_DEV + kbits],
                device_id=(px, py, pc), device_id_type=pl.DeviceIdType.MESH))
    return copies


def _adamw(gstack, w, m, v, name):
    a, b = w.shape
    ta = a
    for t in (1024, 512, 256, 128, 64, 32, 16, 8):
        if a % t == 0 and N_DEV * t * b * 4 <= 4 * 1024 * 1024:
            ta = t
            break
    c1 = 1.0 / (1.0 - ADAM_B1 ** ADAM_STEP)
    c2 = 1.0 / (1.0 - ADAM_B2 ** ADAM_STEP)

    def body(g_ref, w_ref, m_ref, v_ref, og_ref, od_ref, om_ref, ov_ref):
        g = g_ref[0].astype(F32)
        for s in range(1, N_DEV):
            g = g + g_ref[s].astype(F32)
        m_new = ADAM_B1 * m_ref[...] + (1.0 - ADAM_B1) * g
        v_new = ADAM_B2 * v_ref[...] + (1.0 - ADAM_B2) * (g * g)
        og_ref[...] = g
        om_ref[...] = m_new
        ov_ref[...] = v_new
        od_ref[...] = -ADAM_LR * ((m_new * c1) / (jnp.sqrt(v_new * c2) + ADAM_EPS) + ADAM_WD * w_ref[...])

    spec = pl.BlockSpec((ta, b), lambda i: (i, 0))
    return _pallas(
        body, name=name, grid=(a // ta,),
        in_specs=[pl.BlockSpec((N_DEV, ta, b), lambda i: (0, i, 0)), spec, spec, spec],
        out_specs=[spec] * 4, out_shape=[jax.ShapeDtypeStruct((a, b), F32)] * 4,
        compiler_params=_cparams(("parallel",)),
    )(gstack, w, m, v)


_WEIGHTS = ['meta_tokens', 'pre_mix_norm', 'post_mix_norm', 'pre_mlp_norm', 'post_mlp_norm', 'mlp_w1', 'mlp_w2',
            'w_in_even', 'w_out_even', 'sb_out_norm', 's5_lambda_re', 's5_lambda_im', 's5_log_dt', 's5_b_re', 's5_b_im',
            's5_c_re', 's5_c_im', 's5_d', 's5_w_glu', 's5_b_glu', 's5_out_norm', 'w_in_odd', 'dn_conv_w', 'dn_a_log',
            'dn_dt_bias', 'dn_out_norm', 'w_out_odd']
_SHARDED = ['meta_tokens', 'mlp_w1', 'mlp_w2', 'w_in_even', 'w_out_even', 's5_w_glu', 'w_in_odd', 'dn_conv_w', 'w_out_odd']
_SMALL = [n for n in _WEIGHTS if n not in _SHARDED]
_GATHER_FIRST = ['meta_tokens', 'w_in_even', 's5_w_glu', 'w_out_even']
_GATHER_LATE = [n for n in _SHARDED if n not in _GATHER_FIRST]
_REDUCE_EARLY = ['mlp_w1', 'mlp_w2', 'w_in_odd', 'dn_conv_w', 'w_out_odd', 'w_out_even']


def _view2d(name, a):
    return a.reshape(-1, a.shape[-1])


def _unshard(name, g):
    if name == 'mlp_w1':
        return g.reshape(N_DEV, 2, D_MODEL, -1).transpose(1, 2, 0, 3).reshape(2, D_MODEL, D_FF)
    if name == 'mlp_w2':
        return g.reshape(N_DEV, 2, -1, D_MODEL).transpose(1, 0, 2, 3).reshape(2, D_FF, D_MODEL)
    if name in ('w_in_even', 'w_in_odd', 'dn_conv_w', 'meta_tokens'):
        return g.transpose(1, 0, 2).reshape(g.shape[1], -1)
    return g.reshape(-1, g.shape[-1])


def _to_blocks(name, full):
    if name == 'mlp_w1':
        return full.reshape(2, D_MODEL, N_DEV, -1).transpose(2, 0, 1, 3).reshape(N_DEV, 2 * D_MODEL, -1)
    if name == 'mlp_w2':
        return full.reshape(2, N_DEV, -1, D_MODEL).transpose(1, 0, 2, 3).reshape(N_DEV, -1, D_MODEL)
    if name in ('w_in_even', 'w_in_odd', 'dn_conv_w', 'meta_tokens'):
        return full.reshape(full.shape[0], N_DEV, -1).transpose(1, 0, 2)
    return full.reshape(N_DEV, -1, full.shape[-1])


def _pack(parts):
    rows = []
    for p in parts:
        flat = p.reshape(-1)
        rows.append(jnp.pad(flat, (0, (-flat.shape[0]) % 128)).reshape(-1, 128))
    return jnp.concatenate(rows, axis=0)


def _unpack(packed, like):
    out, at = [], 0
    for p in like:
        n = math.prod(p.shape)
        nrow = -(-n // 128)
        out.append(packed[at:at + nrow].reshape(-1)[:n].reshape(p.shape))
        at += nrow
    return out


def _lane_vec(x, width=128):
    flat = x.reshape(-1)
    return jnp.pad(flat, (0, width - flat.shape[0])).reshape(1, width)


def kernel(x, meta_tokens, pre_mix_norm, post_mix_norm, pre_mlp_norm, post_mlp_norm, mlp_w1, mlp_w2, w_in_even, w_out_even, sb_out_norm, s5_lambda_re, s5_lambda_im, s5_log_dt, s5_b_re, s5_b_im, s5_c_re, s5_c_im, s5_d, s5_w_glu, s5_b_glu, s5_out_norm, w_in_odd, dn_conv_w, dn_a_log, dn_dt_bias, dn_out_norm, w_out_odd, loss_target, m_meta_tokens, m_pre_mix_norm, m_post_mix_norm, m_pre_mlp_norm, m_post_mlp_norm, m_mlp_w1, m_mlp_w2, m_w_in_even, m_w_out_even, m_sb_out_norm, m_s5_lambda_re, m_s5_lambda_im, m_s5_log_dt, m_s5_b_re, m_s5_b_im, m_s5_c_re, m_s5_c_im, m_s5_d, m_s5_w_glu, m_s5_b_glu, m_s5_out_norm, m_w_in_odd, m_dn_conv_w, m_dn_a_log, m_dn_dt_bias, m_dn_out_norm, m_w_out_odd, v_meta_tokens, v_pre_mix_norm, v_post_mix_norm, v_pre_mlp_norm, v_post_mlp_norm, v_mlp_w1, v_mlp_w2, v_w_in_even, v_w_out_even, v_sb_out_norm, v_s5_lambda_re, v_s5_lambda_im, v_s5_log_dt, v_s5_b_re, v_s5_b_im, v_s5_c_re, v_s5_c_im, v_s5_d, v_s5_w_glu, v_s5_b_glu, v_s5_out_norm, v_w_in_odd, v_dn_conv_w, v_dn_a_log, v_dn_dt_bias, v_dn_out_norm, v_w_out_odd):
    given = dict(locals())
    w = {n: given[n] for n in _WEIGHTS}
    mom_m = {n: given["m_" + n] for n in _WEIGHTS}
    mom_v = {n: given["v_" + n] for n in _WEIGHTS}

    seq = x.shape[1]
    assert x.shape[0] == 1 and seq % ROW_TILE == 0
    r = seq + ROW_TILE
    pad = ROW_TILE - N_META

    wire = {n: (F32 if n in ('dn_conv_w', 'meta_tokens') else BF16) for n in _SHARDED}
    shard_wire = lambda n: _view2d(n, w[n]).astype(wire[n])
    gathered = _exchange([shard_wire(n) for n in _GATHER_FIRST], [False] * len(_GATHER_FIRST), "gather_first")
    full = {n: _unshard(n, g_) for n, g_ in zip(_GATHER_FIRST, gathered)}
    w_ie, w_oe, w_glu = full['w_in_even'], full['w_out_even'], full['s5_w_glu']
    row = lambda v_: v_.reshape(1, -1)

    hs0 = jnp.concatenate([jnp.zeros((pad, D_MODEL), F32), full['meta_tokens'], x[0]], axis=0)
    hn0 = _norm_pre(hs0, row(pre_mix_norm[0]), "pre_mix_0")
    qkv = _mm_fwd(hn0, w_ie[:, :3 * SB_WIDTH], "in_even_qkv", out_dtypes=(BF16,))
    u = _mm_fwd(hn0, w_ie[:, 3 * SB_WIDTH:], "in_even_u")
    q, k, v = qkv[:, :SB_WIDTH], qkv[:, SB_WIDTH:2 * SB_WIDTH], qkv[:, 2 * SB_WIDTH:]
    nb = r // ATT_BLK
    blocks_t = lambda t_: t_.reshape(nb, ATT_BLK, 4, 128).transpose(2, 0, 3, 1)
    o_sb, ssave, gathered = _sb_fwd(q, k, blocks_t(v), pad, "sb_fwd",
                                    ride=([shard_wire(n) for n in _GATHER_LATE], [False] * len(_GATHER_LATE)))
    full.update({n: _unshard(n, g_) for n, g_ in zip(_GATHER_LATE, gathered)})
    w1, w2, w_oo, conv_w = full['mlp_w1'], full['mlp_w2'], full['w_out_odd'], full['dn_conv_w']
    w_io = full['w_in_odd'][:, :4 * DN_WIDTH]
    w_ab = jnp.pad(full['w_in_odd'][:, 4 * DN_WIDTH:], ((0, 0), (0, 128 - 2 * DN_HEADS)))

    lam_re, lam_im, logdt, btr, bti, ctr, cti, s5_mask = _s5_expand(
        s5_lambda_re[0], s5_lambda_im[0], s5_log_dt[0], s5_b_re[0], s5_b_im[0], s5_c_re[0], s5_c_im[0])
    a_re, a_im, bbr, bbi = _s5_prep(lam_re, lam_im, logdt, btr, bti, "s5_prep")
    s5_wb = jnp.stack([_s5_block_diag_b(bbr, s5_mask), _s5_block_diag_b(bbi, s5_mask)]).astype(BF16)
    s5_wc = jnp.stack([_s5_block_diag_c(ctr, s5_mask), _s5_block_diag_c(cti, s5_mask)]).astype(BF16)
    s5_a = jnp.stack([a_re, a_im])
    s5_args = (s5_wb, s5_a, s5_wc, row(s5_d[0]), w_glu, row(s5_b_glu[0]), row(s5_out_norm[0]))
    y_s5, merged, xstart = _s5_fwd(u, *s5_args, "s5_fwd")
    merged = _norm_pre(o_sb, row(sb_out_norm[0]), "sb_out_norm", into=merged)

    mix0, hs1, hn1 = _mm_norm_fwd(merged, w_oe, hs0, row(post_mix_norm[0]), g_pre=row(pre_mlp_norm[0]), name="out_even")
    relu2 = lambda acc: (jnp.square(jnp.maximum(acc, 0.0)), jnp.maximum(acc, 0.0))
    r0, ra0 = _mm_fwd(hn1, w1[0], "mlp_up_0", out_dtypes=(BF16, BF16), epilogue=relu2)
    m0, hs2, hn2 = _mm_norm_fwd(r0, w2[0], hs1, row(post_mlp_norm[0]), g_pre=row(pre_mix_norm[1]), name="mlp_down_0")

    proj = _mm_fwd(hn2, w_io, "in_odd")
    ab = _mm_fwd(hn2, w_ab, "in_odd_gates")
    alog, dtb = _lane_vec(dn_a_log[0]), _lane_vec(dn_dt_bias[0])
    qd, kd, vd, gb = _dn_pre_fwd(proj, ab, conv_w, alog, dtb, pad, "dn_pre")
    gbt = gb[:, :2 * DN_HEADS].T
    o_dn, s_dn, t_dn = _dn_fwd(qd, kd, vd, gb, gbt, "dn_fwd")
    on_dn = _dn_post_fwd(o_dn, proj, row(dn_out_norm[0]), "dn_post")
    mix1, hs3, hn3 = _mm_norm_fwd(on_dn, w_oo, hs2, row(post_mix_norm[1]), g_pre=row(pre_mlp_norm[1]), name="out_odd")
    r1, ra1 = _mm_fwd(hn3, w1[1], "mlp_up_1", out_dtypes=(BF16, BF16), epilogue=relu2)
    dhs, dm1, dg_post_mlp1, loss_part = _mm_norm_fwd(r1, w2[1], hs3, row(post_mlp_norm[1]),
                                                     loss=(loss_target[0], pad + N_META), name="mlp_down_1_loss")
    loss = lax.psum(loss_part, ("x", "y", "c"))

    g = {}
    drelu2 = lambda acc, ra: (acc * (2.0 * ra.astype(F32)),)

    def mlp_bwd(layer, hn, rr, ra, dm):
        dw2 = _mm_wgrad(rr, dm, f"mlp_down_{layer}_wgrad")
        da = _mm_dgrad(dm, w2[layer], f"mlp_down_{layer}_dgrad", out_dtypes=(BF16,), extras=(ra,), epilogue=drelu2)
        dw1 = _mm_wgrad(hn, da, f"mlp_up_{layer}_wgrad")
        return dw1, dw2, da

    dw1_1, dw2_1, da1 = mlp_bwd(1, hn3, r1, ra1, dm1)
    dhs, dmix1, dg_pre_mlp1, dg_post_mix1 = _dgrad_norm_bwd(
        da1, w1[1], dhs, hs3, row(pre_mlp_norm[1]), post=(mix1, row(post_mix_norm[1])), pad=pad, name="post_mix_1_bwd")

    g['w_out_odd'] = _mm_wgrad(on_dn, dmix1, "out_odd_wgrad")
    d_on_dn = _mm_dgrad(dmix1, w_oo, "out_odd_dgrad")
    do_dn, dproj, dg_dn = _dn_post_bwd(o_dn, proj, row(dn_out_norm[0]), d_on_dn, "dn_post_bwd")
    dqd, dkd, dvd, dgb = _dn_bwd(qd, kd, vd, gb, gbt, s_dn, t_dn, do_dn, "dn_bwd")
    dco, dab, d_alog, d_dtb = _dn_pre_bwd(proj, conv_w, dqd, dkd, dvd, dgb, ab, alog, dtb, pad, "dn_pre_bwd")
    dproj, d_conv = _dn_conv_bwd(dco, proj, conv_w, dproj, "dn_conv_bwd")
    g['w_in_odd'] = jnp.concatenate([_mm_wgrad(hn2, dproj, "in_odd_wgrad"),
                                     _mm_wgrad(hn2, dab, "in_odd_gates_wgrad")[:, :2 * DN_HEADS]], axis=1)
    dhn2_gates = _mm_dgrad(dab, w_ab, "in_odd_gates_dgrad")
    g['dn_conv_w'] = d_conv[:DN_CONV]
    g['dn_a_log'], g['dn_dt_bias'], g['dn_out_norm'] = d_alog[0, :DN_HEADS], d_dtb[0, :DN_HEADS], dg_dn[0]

    dhs, dm0, dg_pre_mix1, dg_post_mlp0 = _dgrad_norm_bwd(
        dproj, w_io, dhs, hs2, row(pre_mix_norm[1]), post=(m0, row(post_mlp_norm[0])), add=dhn2_gates, pad=pad,
        name="post_mlp_0_bwd")
    dw1_0, dw2_0, da0 = mlp_bwd(0, hn1, r0, ra0, dm0)
    dhs, dmix0, dg_pre_mlp0, dg_post_mix0 = _dgrad_norm_bwd(
        da0, w1[0], dhs, hs1, row(pre_mlp_norm[0]), post=(mix0, row(post_mix_norm[0])), pad=pad, name="post_mix_0_bwd")

    g['w_out_even'] = _mm_wgrad(merged, dmix0, "out_even_wgrad")
    dmerged = _mm_dgrad(dmix0, w_oe, "out_even_dgrad")
    _, do_sb, _, dg_sb = _norm_bwd(dmerged, post=(o_sb, row(sb_out_norm[0])), pad=pad, dm_dtype=F32,
                                   dhs_cols=(SB_WIDTH, 0), name="sb_out_norm_bwd")
    dq, dk, dv = _sb_bwd(q, k, v, blocks_t(k), ssave, do_sb, pad, "sb_bwd")
    g['mlp_w1'] = jnp.stack([dw1_0, dw1_1])
    g['mlp_w2'] = jnp.stack([dw2_0, dw2_1])
    grad_wire = lambda n: _to_blocks(n, g[n].reshape(full[n].shape)).astype(wire[n])
    du, d_a, d_d, d_bglu, dg_s5, d_wb, d_wc, g['s5_w_glu'], reduced = _s5_bwd(
        u, y_s5, dmerged, xstart, *s5_args, "s5_bwd", don_block=1,
        ride=([grad_wire(n) for n in _REDUCE_EARLY], [True] * len(_REDUCE_EARLY)))
    stacks = dict(zip(_REDUCE_EARLY, reduced))
    g_lr, g_li, g_dt, g_btr, g_bti = _s5_prep_bwd(
        lam_re, lam_im, logdt, btr, bti, d_a[0], d_a[1],
        _s5_diag_of_b(d_wb[0], s5_mask), _s5_diag_of_b(d_wb[1], s5_mask), "s5_prep_bwd")
    gg, nn, pp = S5_GROUPS, S5_STATE, S5_GROUP
    g['s5_lambda_re'], g['s5_lambda_im'] = g_lr.reshape(gg, nn), g_li.reshape(gg, nn)
    g['s5_log_dt'] = g_dt.reshape(gg, nn)[:, 0]
    g['s5_b_re'], g['s5_b_im'] = g_btr.T.reshape(gg, nn, pp), g_bti.T.reshape(gg, nn, pp)
    g['s5_c_re'] = _s5_diag_of_c(d_wc[0], s5_mask).reshape(gg, nn, pp).transpose(0, 2, 1)
    g['s5_c_im'] = _s5_diag_of_c(d_wc[1], s5_mask).reshape(gg, nn, pp).transpose(0, 2, 1)
    g['s5_d'], g['s5_b_glu'], g['s5_out_norm'], g['sb_out_norm'] = d_d[0], d_bglu[0], dg_s5[0], dg_sb[0]
    dqkvu = jnp.concatenate([dq, dk, dv, du], axis=1).astype(BF16)
    g['w_in_even'] = _mm_wgrad(hn0, dqkvu, "in_even_wgrad")
    dhs, _, dg_pre_mix0, _ = _dgrad_norm_bwd(dqkvu, w_ie, dhs, hs0, row(pre_mix_norm[0]), pad=pad, name="pre_mix_0_bwd")

    g['meta_tokens'] = dhs[pad:pad + N_META]
    g['pre_mix_norm'] = jnp.concatenate([dg_pre_mix0, dg_pre_mix1], axis=0)
    g['post_mix_norm'] = jnp.concatenate([dg_post_mix0, dg_post_mix1], axis=0)
    g['pre_mlp_norm'] = jnp.concatenate([dg_pre_mlp0, dg_pre_mlp1], axis=0)
    g['post_mlp_norm'] = jnp.concatenate([dg_post_mlp0, dg_post_mlp1], axis=0)
    grad_x = dhs[pad + N_META:][None]

    small_like = [w[n] for n in _SMALL]
    last = [n for n in _SHARDED if n not in _REDUCE_EARLY]
    partial = [grad_wire(n) for n in last] + [_pack([g[n].reshape(w[n].shape) for n in _SMALL])]
    reduced = _exchange(partial, [True] * len(last) + [False], "reduce_last")
    stacks.update(zip(last, reduced[:-1]))
    grads, deltas, new_m, new_v = {}, {}, {}, {}
    for n in _SHARDED:
        outs = _adamw(stacks[n], _view2d(n, w[n]), _view2d(n, mom_m[n]), _view2d(n, mom_v[n]), f"adamw_{n}")
        grads[n], deltas[n], new_m[n], new_v[n] = (o.reshape(w[n].shape) for o in outs)
    outs = _adamw(reduced[-1], _pack(small_like), _pack([mom_m[n] for n in _SMALL]), _pack([mom_v[n] for n in _SMALL]),
                  "adamw_small")
    for dst, o in zip((grads, deltas, new_m, new_v), outs):
        for n, part in zip(_SMALL, _unpack(o, small_like)):
            dst[n] = part
    return (loss, grad_x, *[grads[n] for n in _WEIGHTS], *[deltas[n] for n in _WEIGHTS],
            *[new_m[n] for n in _WEIGHTS], *[new_v[n] for n in _WEIGHTS])
```

```python
import math

import jax
import jax.numpy as jnp
from jax import lax
from jax.experimental import pallas as pl
from jax.experimental.pallas import tpu as pltpu

F32 = jnp.float32
BF16 = jnp.bfloat16

D_MODEL = 1024
N_META = 16
SB_HEAD_DIM = 64
SB_WIDTH = 512
S5_WIDTH = 512
S5_GROUP = 16
S5_GROUPS = 32
S5_STATE = 64
S5_NS = S5_GROUPS * S5_STATE
DN_HEAD_DIM = 128
DN_HEADS = 8
DN_WIDTH = 1024
DN_CONV = 4
D_FF = 4096
EPS = 1e-6
N_DEV = 8

ADAM_LR = 0.001
ADAM_B1 = 0.9
ADAM_B2 = 0.999
ADAM_EPS = 1e-08
ADAM_WD = 0.01
ADAM_STEP = 10

ROW_TILE = 512
FUSED_FWD_TILE = 512
ATT_BLK = 256
SB_BLOCKS_PER_TRIP = 3
SB_LOG_ZERO = -106.0
SB_FWD_SKEW = False
SB_BWD_SKEW = True
DN_CHUNK = 128
DN_SUB = 16
S5_TILE = 256
S5_CHUNKS = 4
VMEM_LIMIT = 56 * 1024 * 1024

_HIGH = lax.Precision.HIGHEST


def _pallas(body, **kw):
    return pl.pallas_call(body, **kw)


def _cparams(sem):
    return pltpu.CompilerParams(dimension_semantics=sem, vmem_limit_bytes=VMEM_LIMIT)


def _dot(a, b, dims=((1,), (0,))):
    return lax.dot_general(a, b, (dims, ((), ())), preferred_element_type=F32)


def _dot_hi(a, b):
    return lax.dot_general(a, b, (((1,), (0,)), ((), ())), preferred_element_type=F32, precision=_HIGH)


def _split_dot(m_bf16, x):
    hi = x.astype(BF16)
    lo = (x - hi.astype(F32)).astype(BF16)
    return _dot(m_bf16, hi) + _dot(m_bf16, lo)


def _matmul(a, b, *, ta=False, tb=False, tm, tn, tk, name, out_dtypes=(F32,), extras=(), epilogue=None):
    m, k = (a.shape[1], a.shape[0]) if ta else a.shape
    n = b.shape[0] if tb else b.shape[1]
    assert (b.shape[1] if tb else b.shape[0]) == k
    assert m % tm == 0 and n % tn == 0 and k % tk == 0, (name, m, n, k, tm, tn, tk)
    nk = k // tk
    n_ex = len(extras)
    n_out = len(out_dtypes)
    dims = ((0 if ta else 1,), (1 if tb else 0,))

    def finish(acc, ex_refs, o_refs):
        outs = (acc,) if epilogue is None else epilogue(acc, *[r[...] for r in ex_refs])
        for o_ref, o in zip(o_refs, outs):
            o_ref[...] = o.astype(o_ref.dtype)

    def body(*refs):
        a_ref, b_ref = refs[0], refs[1]
        ex_refs = refs[2:2 + n_ex]
        o_refs = refs[2 + n_ex:2 + n_ex + n_out]
        prod = _dot(a_ref[...].astype(BF16), b_ref[...].astype(BF16), dims)
        if nk == 1:
            finish(prod, ex_refs, o_refs)
            return
        acc_ref = refs[-1]
        kk = pl.program_id(2)

        @pl.when(kk == 0)
        def _():
            acc_ref[...] = prod

        @pl.when(kk > 0)
        def _():
            acc_ref[...] += prod

        @pl.when(kk == nk - 1)
        def _():
            finish(acc_ref[...], ex_refs, o_refs)

    a_spec = pl.BlockSpec((tk, tm), lambda j, i, kk: (kk, i)) if ta else pl.BlockSpec((tm, tk), lambda j, i, kk: (i, kk))
    b_spec = pl.BlockSpec((tn, tk), lambda j, i, kk: (j, kk)) if tb else pl.BlockSpec((tk, tn), lambda j, i, kk: (kk, j))
    o_spec = pl.BlockSpec((tm, tn), lambda j, i, kk: (i, j))
    outs = _pallas(
        body, name=name,
        grid=(n // tn, m // tm, nk),
        in_specs=[a_spec, b_spec] + [o_spec] * n_ex,
        out_specs=[o_spec] * n_out,
        out_shape=[jax.ShapeDtypeStruct((m, n), dt) for dt in out_dtypes],
        scratch_shapes=[] if nk == 1 else [pltpu.VMEM((tm, tn), F32)],
        compiler_params=_cparams(("parallel", "parallel", "arbitrary")),
    )(a, b, *extras)
    return outs[0] if n_out == 1 else outs


def _tile(n, cap):
    best = 128
    for t in range(128, min(n, cap) + 1, 128):
        if n % t == 0:
            best = t
    assert n % best == 0, n
    return best


MM_K_CAP = 4096
WGRAD_ROWS = 1536


MM_LHS_TILE_BYTES = 6 * 1024 * 1024


def _row_tile(x, depth):
    tall = 3 * ROW_TILE
    fits = tall * depth * x.dtype.itemsize <= MM_LHS_TILE_BYTES
    return tall if (x.shape[0] % tall == 0 and fits) else ROW_TILE


def _mm_fwd(x, w, name, **kw):
    k, n = w.shape
    tk = _tile(k, MM_K_CAP)
    return _matmul(x, w, tm=_row_tile(x, tk), tn=_tile(n, 1024), tk=tk, name=name, **kw)


def _mm_dgrad(dy, w, name, **kw):
    k, n = w.shape
    tk = _tile(n, MM_K_CAP)
    return _matmul(dy, w, tb=True, tm=_row_tile(dy, tk), tn=_tile(k, 1024), tk=tk, name=name, **kw)


def _mm_wgrad(x, dy, name):
    k, n = x.shape[1], dy.shape[1]
    rows = x.shape[0]
    return _matmul(x, dy, ta=True, tm=_tile(k, 512), tn=_tile(n, 1024),
                   tk=WGRAD_ROWS if rows % WGRAD_ROWS == 0 else ROW_TILE, name=name)


def _rms(x, g):
    r = lax.rsqrt(jnp.mean(x * x, axis=-1, keepdims=True) + EPS)
    return x * r * g


def _rms_bwd(x, g, dy):
    r = lax.rsqrt(jnp.mean(x * x, axis=-1, keepdims=True) + EPS)
    xh = x * r
    dxh = dy * g
    dx = r * (dxh - xh * jnp.mean(dxh * xh, axis=-1, keepdims=True))
    dg = jnp.sum(dy * xh, axis=0, keepdims=True)
    return dx, dg


def _row_spec(width, tile=ROW_TILE):
    return pl.BlockSpec((tile, width), lambda i: (i, 0))


def _vec_spec(width):
    return pl.BlockSpec((1, width), lambda i: (0, 0))


def _norm_pre(hs, g, name, into=None):
    r, d = hs.shape

    def body(x_ref, g_ref, *rest):
        rest[-1][...] = _rms(x_ref[...], g_ref[...]).astype(BF16)

    if into is None:
        return _pallas(body, name=name, grid=(r // ROW_TILE,), in_specs=[_row_spec(d), _vec_spec(d)],
                       out_specs=_row_spec(d), out_shape=jax.ShapeDtypeStruct((r, d), BF16),
                       compiler_params=_cparams(("parallel",)))(hs, g)
    return _pallas(body, name=name, grid=(r // ROW_TILE,),
                   in_specs=[_row_spec(d), _vec_spec(d), pl.BlockSpec(memory_space=pl.ANY)],
                   out_specs=_row_spec(d), out_shape=jax.ShapeDtypeStruct(into.shape, BF16), input_output_aliases={2: 0},
                   compiler_params=_cparams(("parallel",)))(hs, g, into)


def _mm_norm_fwd(a, w, hs, g_post, *, g_pre=None, loss=None, name):
    k, d = w.shape
    r = a.shape[0]
    assert k <= MM_K_CAP and d == hs.shape[1]
    t = FUSED_FWD_TILE
    nt = r // t

    def body(*refs):
        a_ref, w_ref, hs_ref, gp_ref = refs[:4]
        i = pl.program_id(0)
        m = _dot(a_ref[...].astype(BF16), w_ref[...].astype(BF16))
        gp = gp_ref[...]
        new = hs_ref[...] + _rms(m, gp)
        if loss is None:
            gn_ref, m_ref, o_ref, hn_ref = refs[4:]
            m_ref[...] = m
            o_ref[...] = new
            hn_ref[...] = _rms(new, gn_ref[...]).astype(BF16)
        else:
            t_ref, dhs_ref, dm_ref, dgp_ref, loss_ref = refs[4:]
            live = (i * t + lax.broadcasted_iota(jnp.int32, (t, 1), 0)) >= loss[1]
            diff = jnp.where(live, new - t_ref[...], 0.0)
            dhs = diff * (1.0 / d)
            dhs_ref[...] = dhs
            loss_ref[...] = jnp.full((8, 128), 0.5 / d * jnp.sum(diff * diff), F32)
            dm, dg = _rms_bwd(m, gp, dhs)
            dm_ref[...] = dm.astype(BF16)

            @pl.when(i == 0)
            def _():
                dgp_ref[...] = jnp.zeros_like(dgp_ref)
            dgp_ref[...] += dg

    common_in = [_row_spec(k, t), pl.BlockSpec((k, d), lambda i: (0, 0)), _row_spec(d, t), _vec_spec(d)]
    if loss is None:
        return _pallas(
            body, name=name, grid=(nt,), in_specs=common_in + [_vec_spec(d)],
            out_specs=[_row_spec(d, t)] * 3,
            out_shape=[jax.ShapeDtypeStruct((r, d), F32), jax.ShapeDtypeStruct((r, d), F32), jax.ShapeDtypeStruct((r, d), BF16)],
            compiler_params=_cparams(("parallel",)))(a, w, hs, g_post, g_pre)
    target, first_row = loss
    assert first_row % t == 0
    dhs, dm, dgp, parts = _pallas(
        body, name=name, grid=(nt,),
        in_specs=common_in + [pl.BlockSpec((t, d), lambda i: (jnp.maximum(i - first_row // t, 0), 0))],
        out_specs=[_row_spec(d, t), _row_spec(d, t), _vec_spec(d), pl.BlockSpec((8, 128), lambda i: (i, 0))],
        out_shape=[jax.ShapeDtypeStruct((r, d), F32), jax.ShapeDtypeStruct((r, d), BF16), jax.ShapeDtypeStruct((1, d), F32),
                   jax.ShapeDtypeStruct((nt * 8, 128), F32)],
        compiler_params=_cparams(("arbitrary",)))(a, w, hs, g_post, target)
    return dhs, dm, dgp, jnp.sum(parts[::8, 0])


def _norm_bwd(dhs, *, pre=None, post=None, pad=0, dm_dtype=BF16, dhs_cols=None, name):
    r = dhs.shape[0]
    d = dhs.shape[1] if dhs_cols is None else dhs_cols[0]
    has_pre, has_post = pre is not None, post is not None

    def body(*refs):
        it = iter(refs)
        dhs_ref = next(it)
        if has_pre:
            hs_ref, gn_ref, dhn_ref = next(it), next(it), next(it)
        if has_post:
            m_ref, gp_ref = next(it), next(it)
        if has_pre:
            o_dhs, o_dgn = next(it), next(it)
        if has_post:
            o_dm, o_dgp = next(it), next(it)
        i = pl.program_id(0)
        live = (i * ROW_TILE + lax.broadcasted_iota(jnp.int32, (ROW_TILE, 1), 0)) >= pad
        cur = jnp.where(live, dhs_ref[...], 0.0)
        if has_pre:
            dx, dg = _rms_bwd(hs_ref[...], gn_ref[...], jnp.where(live, dhn_ref[...].astype(F32), 0.0))
            cur = cur + dx
            o_dhs[...] = cur

            @pl.when(i == 0)
            def _():
                o_dgn[...] = jnp.zeros_like(o_dgn)
            o_dgn[...] += dg
        if has_post:
            dm, dg = _rms_bwd(m_ref[...], gp_ref[...], cur)
            o_dm[...] = dm.astype(o_dm.dtype)

            @pl.when(i == 0)
            def _():
                o_dgp[...] = jnp.zeros_like(o_dgp)
            o_dgp[...] += dg

    dhs_spec = _row_spec(d) if dhs_cols is None else pl.BlockSpec((ROW_TILE, d), lambda i: (i, dhs_cols[1]))
    ins, in_specs, out_specs, out_shape = [dhs], [dhs_spec], [], []
    if has_pre:
        ins += list(pre)
        in_specs += [_row_spec(d), _vec_spec(d), _row_spec(d)]
        out_specs += [_row_spec(d), _vec_spec(d)]
        out_shape += [jax.ShapeDtypeStruct((r, d), F32), jax.ShapeDtypeStruct((1, d), F32)]
    if has_post:
        ins += list(post)
        in_specs += [_row_spec(d), _vec_spec(d)]
        out_specs += [_row_spec(d), _vec_spec(d)]
        out_shape += [jax.ShapeDtypeStruct((r, d), dm_dtype), jax.ShapeDtypeStruct((1, d), F32)]
    outs = list(_pallas(body, name=name, grid=(r // ROW_TILE,), in_specs=in_specs, out_specs=out_specs,
                        out_shape=out_shape, compiler_params=_cparams(("arbitrary",)))(*ins))
    dhs_new, dgn = (outs.pop(0), outs.pop(0)) if has_pre else (dhs, None)
    dm, dgp = (outs.pop(0), outs.pop(0)) if has_post else (None, None)
    return dhs_new, dm, dgn, dgp


def _dgrad_norm_bwd(dy, w, dhs, hs, g_pre, *, post=None, add=None, pad=0, name):
    d, n = w.shape
    r = dy.shape[0]
    assert n <= MM_K_CAP and d == dhs.shape[1]
    t = ROW_TILE // 2
    has_post, has_add = post is not None, add is not None
    dims = ((1,), (1,))

    def body(*refs):
        it = iter(refs)
        dy_ref, w_ref = next(it), next(it)
        add_ref = next(it) if has_add else None
        dhs_ref, hs_ref, gn_ref = next(it), next(it), next(it)
        if has_post:
            m_ref, gp_ref = next(it), next(it)
        o_dhs, o_dgn = next(it), next(it)
        if has_post:
            o_dm, o_dgp = next(it), next(it)
        i = pl.program_id(0)
        dhn = _dot(dy_ref[...].astype(BF16), w_ref[...].astype(BF16), dims)
        if has_add:
            dhn = dhn + add_ref[...]
        live = (i * t + lax.broadcasted_iota(jnp.int32, (t, 1), 0)) >= pad
        dx, dg = _rms_bwd(hs_ref[...], gn_ref[...], jnp.where(live, dhn, 0.0))
        cur = jnp.where(live, dhs_ref[...], 0.0) + dx
        o_dhs[...] = cur

        @pl.when(i == 0)
        def _():
            o_dgn[...] = jnp.zeros_like(o_dgn)
        o_dgn[...] += dg
        if has_post:
            dm, dg = _rms_bwd(m_ref[...], gp_ref[...], cur)
            o_dm[...] = dm.astype(BF16)

            @pl.when(i == 0)
            def _():
                o_dgp[...] = jnp.zeros_like(o_dgp)
            o_dgp[...] += dg

    ins = [dy, w] + ([add] if has_add else []) + [dhs, hs, g_pre] + (list(post) if has_post else [])
    in_specs = ([_row_spec(n, t), pl.BlockSpec((d, n), lambda i: (0, 0))] + ([_row_spec(d, t)] if has_add else [])
                + [_row_spec(d, t), _row_spec(d, t), _vec_spec(d)] + ([_row_spec(d, t), _vec_spec(d)] if has_post else []))
    out_specs = [_row_spec(d, t), _vec_spec(d)] + ([_row_spec(d, t), _vec_spec(d)] if has_post else [])
    out_shape = [jax.ShapeDtypeStruct((r, d), F32), jax.ShapeDtypeStruct((1, d), F32)]
    if has_post:
        out_shape += [jax.ShapeDtypeStruct((r, d), BF16), jax.ShapeDtypeStruct((1, d), F32)]
    outs = list(_pallas(body, name=name, grid=(r // t,), in_specs=in_specs, out_specs=out_specs,
                        out_shape=out_shape, compiler_params=_cparams(("arbitrary",)))(*ins))
    return (outs[0], outs[2], outs[1], outs[3]) if has_post else (outs[0], None, outs[1], None)


def _softplus(z):
    return jnp.maximum(z, 0.0) + jnp.log(1.0 + jnp.exp(-jnp.abs(z)))


def _sb_consts(t):
    row = lax.broadcasted_iota(jnp.int32, (t, t), 0)
    col = lax.broadcasted_iota(jnp.int32, (t, t), 1)
    m_up = (col >= row).astype(BF16)
    m_low = (col <= row).astype(BF16)
    return m_up, m_low


def _emit_chains(chains, stages, skew):
    if skew:
        for step in range(len(chains) + len(stages) - 1):
            for si, stage in enumerate(stages):
                if 0 <= step - si < len(chains):
                    stage(chains[step - si])
    else:
        for stage in stages:
            for c in chains:
                stage(c)


def _sb_fwd(q, k, vt3, pad, name, ride=((), ())):
    r = q.shape[0]
    t = ATT_BLK
    nb = r // t
    nbp = -(-(nb + 1) // 8) * 8
    jmin = pad // t
    scale = SB_HEAD_DIM ** -0.5
    n_ride = len(ride[0])

    def body(q_ref, k_ref, vt_ref, *rest):
        ride_in, (o_ref, ss_ref), ride_out = rest[:n_ride], rest[n_ride:n_ride + 2], rest[n_ride + 2:2 * n_ride + 2]
        acc_ref, kn_ref = rest[2 * n_ride + 2:2 * n_ride + 4]
        ride_sems = rest[2 * n_ride + 4:]
        i = pl.program_id(1)
        if n_ride:
            @pl.when((pl.program_id(0) == 0) & (i == 0))
            def _():
                for cp in _exchange_copies(ride_in, ride_out, ride[1], *ride_sems):
                    cp.start()

        @pl.when(i == 0)
        def _():
            def blk(b, m):
                kb = k_ref[pl.ds(pl.multiple_of(b * t, t), t), :].astype(F32)
                return jnp.maximum(m, jnp.max(jnp.sum(kb * kb, axis=1, keepdims=True), axis=0, keepdims=True))
            kn_ref[...] = jnp.broadcast_to(lax.fori_loop(0, nb, blk, jnp.zeros((1, 1), F32)), (8, 128))

        qf = q_ref[...].astype(F32)
        z_bound = scale * jnp.sqrt(jnp.max(jnp.sum(qf * qf, axis=1, keepdims=True)) * jnp.max(kn_ref[...]))

        def need(carry):
            return jnp.maximum(jnp.max(carry[0]), jnp.max(carry[1])) + z_bound >= SB_LOG_ZERO

        qt = qf.T
        sub = lax.broadcasted_iota(jnp.int32, (128, 1), 0)
        m_up, _ = _sb_consts(t)
        kpos0 = lax.broadcasted_iota(jnp.int32, (t, 1), 0)
        qpos = i * t + lax.broadcasted_iota(jnp.int32, (1, t), 1)
        qths = [jnp.where((sub >= 64 * h) & (sub < 64 * (h + 1)), qt * scale, 0.0).astype(BF16) for h in range(2)]
        acc_ref[...] = jnp.zeros_like(acc_ref)

        def sweep(js, carry, masked):
            kbs = [k_ref[pl.ds(pl.multiple_of(j * t, t), t), :] for j in js]
            vts = [vt_ref[0, j] for j in js]
            accs = [acc_ref[0], acc_ref[1]]
            s = list(carry)
            chains = [(n, h) for n in range(len(js)) for h in range(2)]
            masked = [masked] * len(js) if isinstance(masked, bool) else masked
            valid = [(js[n] * t + kpos0 < qpos) & (js[n] * t + kpos0 >= pad) if masked[n] else None for n in range(len(js))]
            zt, inc, saves = {}, {}, []

            def st_scores(c):
                zt[c] = _dot(kbs[c[0]], qths[c[1]])

            def st_cumsum(c):
                lk = -_softplus(zt[c])
                if masked[c[0]]:
                    lk = jnp.where(valid[c[0]], lk, 0.0)
                inc[c] = _split_dot(m_up, lk)

            def st_weights(c):
                n, h = c
                saves.append((h, js[n], s[h]))
                w = jnp.exp(zt[c] + inc[c] + s[h])
                if masked[n]:
                    w = jnp.where(valid[n], w, 0.0)
                accs[h] = accs[h] + _dot(vts[n], w.astype(BF16))
                s[h] = s[h] + inc[c][0:1, :]

            _emit_chains(chains, [st_scores, st_cumsum, st_weights], SB_FWD_SKEW)
            for h, j, val in saves:
                ss_ref[h, 0, pl.ds(j, 1), :] = val
            acc_ref[0] = accs[0]
            acc_ref[1] = accs[1]
            return tuple(s)

        zero = jnp.zeros((1, t), F32)
        bpi = SB_BLOCKS_PER_TRIP
        j, carry = lax.cond(
            i - 1 > jmin,
            lambda: (i - 2, sweep([i, i - 1], (zero, zero), [True, False])),
            lambda: (i - 1, sweep([i], (zero, zero), True)))
        def further(j, carry):
            j, carry = lax.while_loop(
                lambda st: (st[0] - bpi >= jmin) & need(st[1]),
                lambda st: (st[0] - bpi, sweep([st[0] - b for b in range(bpi)], st[1], False)), (j, carry))
            j, carry = lax.while_loop(
                lambda st: (st[0] > jmin) & need(st[1]),
                lambda st: (st[0] - 1, sweep([st[0]], st[1], False)), (j, carry))
            return lax.while_loop(
                lambda st: (st[0] == jmin) & (i > jmin) & need(st[1]),
                lambda st: (st[0] - 1, sweep([st[0]], st[1], True)), (j, carry))[0]

        j = lax.cond((j >= jmin) & need(carry), lambda: further(j, carry), lambda: j)
        first = jnp.full((1, t), j + 1, jnp.int32).astype(F32)
        ss_ref[0, 0, nbp - 1:nbp, :] = first
        ss_ref[1, 0, nbp - 1:nbp, :] = first
        acc = jnp.where(sub < 64, acc_ref[0], acc_ref[1])
        o_ref[...] = acc.T
        if n_ride:
            @pl.when((pl.program_id(0) == 3) & (i == nb - 1))
            def _():
                for cp in _exchange_copies(ride_in, ride_out, ride[1], *ride_sems):
                    cp.wait()

    hbm = pl.BlockSpec(memory_space=pl.ANY)
    outs = _pallas(
        body, name=name, grid=(4, nb),
        in_specs=[pl.BlockSpec((t, 128), lambda hp, i: (i, hp)),
                  pl.BlockSpec((r, 128), lambda hp, i: (0, hp)),
                  pl.BlockSpec((1, nb, 128, t), lambda hp, i: (hp, 0, 0, 0))] + [hbm] * n_ride,
        out_specs=[pl.BlockSpec((t, 128), lambda hp, i: (i, hp)),
                   pl.BlockSpec((2, 1, nbp, t), lambda hp, i: (hp, i, 0, 0))] + [hbm] * n_ride,
        out_shape=[jax.ShapeDtypeStruct((r, SB_WIDTH), F32),
                   jax.ShapeDtypeStruct((8, nb, nbp, t), F32)] + _exchange_shapes(*ride),
        scratch_shapes=[pltpu.VMEM((2, 128, t), F32), pltpu.VMEM((8, 128), F32)] + (_exchange_sems(n_ride) if n_ride else []),
        compiler_params=_cparams(("arbitrary", "arbitrary")),
    )(q, k, vt3, *ride[0])
    return outs[0], outs[1], list(outs[2:])


def _sb_bwd(q, k, v, kt3, ssave, do, pad, name):
    r = q.shape[0]
    t = ATT_BLK
    nb = r // t
    nbp = ssave.shape[2]
    jmin = pad // t
    scale = SB_HEAD_DIM ** -0.5

    def body(q_ref, do_ref, k_ref, v_ref, kt_ref, ss_ref, dq_ref, dk_hbm, dv_hbm, dk_acc, dv_acc, dq_acc, sem):
        hp = pl.program_id(0)
        i = pl.program_id(1)

        @pl.when(i == 0)
        def _():
            dk_acc[...] = jnp.zeros_like(dk_acc)
            dv_acc[...] = jnp.zeros_like(dv_acc)

        qf = q_ref[...].astype(F32)
        dof = do_ref[...]
        qt = qf.T
        dot_ = dof.T
        sub = lax.broadcasted_iota(jnp.int32, (128, 1), 0)
        lane = lax.broadcasted_iota(jnp.int32, (1, 128), 1)
        m_up, m_low = _sb_consts(t)
        kpos0 = lax.broadcasted_iota(jnp.int32, (t, 1), 0)
        qpos = i * t + lax.broadcasted_iota(jnp.int32, (1, t), 1)
        first = jnp.clip(jnp.max(ss_ref[0, 0, nbp - 1:nbp, :]).astype(jnp.int32), jmin, i)
        mid0 = jnp.maximum(first, jmin + 1)
        pair = i - mid0 >= 1
        n_mid = jnp.maximum(i - mid0 - 1, 0)
        n_edge = jnp.where((i > jmin) & (first == jmin), 1, 0)
        in_t = [(sub >= 64 * h) & (sub < 64 * (h + 1)) for h in range(2)]
        in_l = [(lane >= 64 * h) & (lane < 64 * (h + 1)) for h in range(2)]
        qths = [jnp.where(in_t[h], qt * scale, 0.0).astype(BF16) for h in range(2)]
        doths = [jnp.where(in_t[h], dot_, 0.0).astype(BF16) for h in range(2)]
        qhs = [jnp.where(in_l[h], qf * scale, 0.0).astype(BF16) for h in range(2)]
        dohs = [jnp.where(in_l[h], dof, 0.0).astype(BF16) for h in range(2)]
        dq_acc[...] = jnp.zeros_like(dq_acc)

        def sweep(js, carry, masked):
            rows = [pl.ds(pl.multiple_of(j * t, t), t) for j in js]
            kbs = [k_ref[rw, :] for rw in rows]
            vbs = [v_ref[rw, :] for rw in rows]
            kts = [kt_ref[0, j] for j in js]
            sss = [[ss_ref[h, 0, pl.ds(j, 1), :] for h in range(2)] for j in js]
            dv_old = [dv_acc[rw, :] for rw in rows]
            dk_old = [dk_acc[rw, :] for rw in rows]
            dqs = [dq_acc[0], dq_acc[1]]
            ec = list(carry)
            chains = [(n, h) for n in range(len(js)) for h in range(2)]
            masked = [masked] * len(js) if isinstance(masked, bool) else masked
            valid = [(js[n] * t + kpos0 < qpos) & (js[n] * t + kpos0 >= pad) if masked[n] else None for n in range(len(js))]
            zt, dvt, sp, inc, e, big_e = {}, {}, {}, {}, {}, {}

            def st_scores(c):
                zt[c] = _dot(kbs[c[0]], qths[c[1]])
                dvt[c] = _dot(vbs[c[0]], doths[c[1]])

            def st_cumsum(c):
                sp[c] = _softplus(zt[c])
                lk = -sp[c]
                if masked[c[0]]:
                    lk = jnp.where(valid[c[0]], lk, 0.0)
                inc[c] = _split_dot(m_up, lk)

            def st_weights(c):
                n, h = c
                w = jnp.exp(zt[c] + inc[c] + sss[n][h])
                if masked[n]:
                    w = jnp.where(valid[n], w, 0.0)
                dv_old[n] = dv_old[n] + _dot(w.astype(BF16), dohs[h])
                e[c] = w * dvt[c]
                pinc = _split_dot(m_low, e[c])
                big_e[c] = pinc - e[c] + ec[h]
                ec[h] = ec[h] + pinc[t - 1:t, :]

            def st_dscores(c):
                n, h = c
                dz = e[c] - jnp.exp(zt[c] - sp[c]) * (e[c] + big_e[c])
                if masked[n]:
                    dz = jnp.where(valid[n], dz, 0.0)
                dzb = dz.astype(BF16)
                dqs[h] = dqs[h] + _dot(kts[n], dzb)
                dk_old[n] = dk_old[n] + _dot(dzb, qhs[h])

            _emit_chains(chains, [st_scores, st_cumsum, st_weights, st_dscores], SB_BWD_SKEW)
            for n, rw in enumerate(rows):
                dv_acc[rw, :] = dv_old[n]
                dk_acc[rw, :] = dk_old[n]
            dq_acc[0] = dqs[0]
            dq_acc[1] = dqs[1]
            return tuple(ec)

        zero = jnp.zeros((1, t), F32)
        bpi = SB_BLOCKS_PER_TRIP
        carry = lax.fori_loop(0, n_edge, lambda it, c: sweep([jmin + it * 0], c, True), (zero, zero))
        carry = lax.fori_loop(0, n_mid // bpi, lambda it, c: sweep([mid0 + bpi * it + b for b in range(bpi)], c, False), carry)
        n_rem = n_mid % bpi
        carry = lax.fori_loop(0, n_rem, lambda it, c: sweep([i - 1 - n_rem + it], c, False), carry)
        lax.cond(pair, lambda: sweep([i - 1, i], carry, [False, True]), lambda: sweep([i], carry, True))
        dq_ref[...] = (jnp.where(sub < 64, dq_acc[0], dq_acc[1]) * scale).T

        @pl.when(i == nb - 1)
        def _():
            lanes = pl.ds(pl.multiple_of(hp * 128, 128), 128)
            c1 = pltpu.make_async_copy(dk_acc, dk_hbm.at[:, lanes], sem.at[0])
            c2 = pltpu.make_async_copy(dv_acc, dv_hbm.at[:, lanes], sem.at[1])
            c1.start()
            c2.start()
            c1.wait()
            c2.wait()

    return _pallas(
        body, name=name, grid=(4, nb),
        in_specs=[pl.BlockSpec((t, 128), lambda hp, i: (i, hp)),
                  pl.BlockSpec((t, 128), lambda hp, i: (i, hp)),
                  pl.BlockSpec((r, 128), lambda hp, i: (0, hp)),
                  pl.BlockSpec((r, 128), lambda hp, i: (0, hp)),
                  pl.BlockSpec((1, nb, 128, t), lambda hp, i: (hp, 0, 0, 0)),
                  pl.BlockSpec((2, 1, nbp, t), lambda hp, i: (hp, i, 0, 0))],
        out_specs=[pl.BlockSpec((t, 128), lambda hp, i: (i, hp)),
                   pl.BlockSpec(memory_space=pl.ANY), pl.BlockSpec(memory_space=pl.ANY)],
        out_shape=[jax.ShapeDtypeStruct((r, SB_WIDTH), F32),
                   jax.ShapeDtypeStruct((r, SB_WIDTH), F32), jax.ShapeDtypeStruct((r, SB_WIDTH), F32)],
        scratch_shapes=[pltpu.VMEM((r, 128), F32), pltpu.VMEM((r, 128), F32), pltpu.VMEM((2, 128, t), F32),
                        pltpu.SemaphoreType.DMA((2,))],
        compiler_params=_cparams(("arbitrary", "arbitrary")),
    )(q, do, k, v, kt3, ssave)


def _s5_disc(lam_re, lam_im, logdt, btr, bti):
    lr = jnp.minimum(lam_re, -1e-4)
    li = lam_im
    dt = jnp.exp(logdt)
    mag = jnp.exp(lr * dt)
    ang = li * dt
    a_re, a_im = mag * jnp.cos(ang), mag * jnp.sin(ang)
    den = lr * lr + li * li
    nr, ni = a_re - 1.0, a_im
    c_re = (nr * lr + ni * li) / den
    c_im = (ni * lr - nr * li) / den
    return a_re, a_im, c_re * btr - c_im * bti, c_re * bti + c_im * btr


def _s5_prep(lam_re, lam_im, logdt, btr, bti, name):
    ns = lam_re.shape[1]

    def body(lr_ref, li_ref, dt_ref, br_ref, bi_ref, ar_ref, ai_ref, bbr_ref, bbi_ref):
        ar, ai, bbr, bbi = _s5_disc(lr_ref[...], li_ref[...], dt_ref[...], br_ref[...], bi_ref[...])
        ar_ref[...] = ar
        ai_ref[...] = ai
        bbr_ref[...] = bbr
        bbi_ref[...] = bbi

    return _pallas(body, name=name,
                   out_shape=[jax.ShapeDtypeStruct((1, ns), F32)] * 2 + [jax.ShapeDtypeStruct((S5_GROUP, ns), F32)] * 2,
                   )(lam_re, lam_im, logdt, btr, bti)


def _s5_prep_bwd(lam_re, lam_im, logdt, btr, bti, dar, dai, dbbr, dbbi, name):
    ns = lam_re.shape[1]

    def body(lr_ref, li_ref, dt_ref, br_ref, bi_ref, dar_ref, dai_ref, dbr_ref, dbi_ref, o_lr, o_li, o_dt, o_br, o_bi):
        _, vjp = jax.vjp(_s5_disc, lr_ref[...], li_ref[...], dt_ref[...], br_ref[...], bi_ref[...])
        g = vjp((dar_ref[...], dai_ref[...], dbr_ref[...], dbi_ref[...]))
        o_lr[...] = g[0]
        o_li[...] = g[1]
        row = lax.broadcasted_iota(jnp.int32, (ns, ns), 0) // S5_STATE
        col = lax.broadcasted_iota(jnp.int32, (ns, ns), 1) // S5_STATE
        same = (row == col).astype(F32)
        o_dt[...] = _dot_hi(jnp.broadcast_to(g[2], (8, ns)), same)[0:1]
        o_br[...] = g[3]
        o_bi[...] = g[4]

    return _pallas(body, name=name,
                   out_shape=[jax.ShapeDtypeStruct((1, ns), F32)] * 3 + [jax.ShapeDtypeStruct((S5_GROUP, ns), F32)] * 2,
                   compiler_params=pltpu.CompilerParams(vmem_limit_bytes=VMEM_LIMIT),
                   )(lam_re, lam_im, logdt, btr, bti, dar, dai, dbbr, dbbi)


def _s5_scan(br, bi, ar, ai, t, reverse=False, carry=None):
    ng = t // 8
    ns = br.shape[1]
    br, bi = br.reshape(ng, 8, ns), bi.reshape(ng, 8, ns)
    row8 = lax.broadcasted_iota(jnp.int32, (1, 8, 1), 1)
    pr, pi_ = ar, ai
    for k in (1, 2, 4):
        if reverse:
            sr, si, ok = pltpu.roll(br, 8 - k, 1), pltpu.roll(bi, 8 - k, 1), row8 < 8 - k
        else:
            sr, si, ok = pltpu.roll(br, k, 1), pltpu.roll(bi, k, 1), row8 >= k
        sr = jnp.where(ok, sr, 0.0)
        si = jnp.where(ok, si, 0.0)
        br, bi = br + pr * sr - pi_ * si, bi + pr * si + pi_ * sr
        pr, pi_ = pr * pr - pi_ * pi_, 2.0 * pr * pi_
    pw_r, pw_i = [ar], [ai]
    for _ in range(7):
        pw_r.append(pw_r[-1] * ar - pw_i[-1] * ai)
        pw_i.append(pw_r[-2] * ai + pw_i[-1] * ar)
    if reverse:
        pw_r.reverse()
        pw_i.reverse()
    p8r, p8i = jnp.concatenate(pw_r, axis=0), jnp.concatenate(pw_i, axis=0)
    out_r, out_i = [None] * ng, [None] * ng
    order = range(ng - 1, -1, -1) if reverse else range(ng)
    edge = 0 if reverse else 7
    for g in order:
        gr, gi = br[g], bi[g]
        if carry is not None:
            cr, ci = carry
            gr, gi = gr + p8r * cr - p8i * ci, gi + p8r * ci + p8i * cr
        out_r[g], out_i[g] = gr, gi
        carry = (gr[edge:edge + 1], gi[edge:edge + 1])
    return jnp.concatenate(out_r, axis=0), jnp.concatenate(out_i, axis=0)


def _s5_prev_rows(x, first, t):
    ng = t // 8
    ns = x.shape[1]
    x3 = x.reshape(ng, 8, ns)
    last = x3[:, 7:8, :]
    before = jnp.concatenate([first.reshape(1, 1, ns), last[:ng - 1]], axis=0)
    row8 = lax.broadcasted_iota(jnp.int32, (1, 8, 1), 1)
    return jnp.where(row8 == 0, before, pltpu.roll(x3, 1, 1)).reshape(t, ns)


_GELU_C = math.sqrt(2.0 / math.pi)


def _gelu(y):
    th = jnp.tanh(_GELU_C * (y + 0.044715 * y * y * y))
    return 0.5 * y * (1.0 + th), th


def _sigmoid(x):
    return 1.0 / (1.0 + jnp.exp(-x))


def _s5_fwd(u, wb, a, wc, dskip, wglu, bglu, gnorm, name):
    r = u.shape[0]
    t = S5_TILE
    nt = r // t
    ns = wb.shape[2]
    w = S5_WIDTH

    def body(u_ref, wb_ref, a_ref, wc_ref, d_ref, wg_ref, bg_ref, gn_ref, y_ref, on_ref, xs_ref, carry_ref):
        i = pl.program_id(0)
        ar, ai = a_ref[0], a_ref[1]

        @pl.when(i == 0)
        def _():
            carry_ref[...] = jnp.zeros_like(carry_ref)

        u_ = u_ref[...]
        ub = u_.astype(BF16)
        xs_ref[0] = carry_ref[:, 0, :]
        chunks = list(range(S5_CHUNKS))
        sl_s = [slice(c * (ns // S5_CHUNKS), (c + 1) * (ns // S5_CHUNKS)) for c in chunks]
        sl_u = [slice(c * (w // S5_CHUNKS), (c + 1) * (w // S5_CHUNKS)) for c in chunks]
        bu, xs, ys = {}, {}, {}

        def st_inputs(c):
            bu[c] = (_dot(ub[:, sl_u[c]], wb_ref[0, sl_u[c], sl_s[c]]), _dot(ub[:, sl_u[c]], wb_ref[1, sl_u[c], sl_s[c]]))

        def st_scan(c):
            xr, xi = _s5_scan(*bu[c], ar[:, sl_s[c]], ai[:, sl_s[c]], t, carry=(carry_ref[0, :, sl_s[c]], carry_ref[1, :, sl_s[c]]))
            carry_ref[0, :, sl_s[c]] = xr[t - 1:t, :]
            carry_ref[1, :, sl_s[c]] = xi[t - 1:t, :]
            xs[c] = (xr.astype(BF16), xi.astype(BF16))

        def st_outputs(c):
            ys[c] = _dot(xs[c][0], wc_ref[0, sl_s[c], sl_u[c]]) - _dot(xs[c][1], wc_ref[1, sl_s[c], sl_u[c]])

        _emit_chains(chunks, [st_inputs, st_scan, st_outputs], False)
        y = jnp.concatenate([ys[c] for c in chunks], axis=1) + d_ref[...] * u_
        h, _ = _gelu(y)
        gate = _sigmoid(_dot(h.astype(BF16), wg_ref[...]) + bg_ref[...])
        y_ref[...] = y
        on_ref[...] = _rms(h * gate, gn_ref[...]).astype(BF16)

    full = lambda shape: pl.BlockSpec(shape, lambda i: (0,) * len(shape))
    return _pallas(
        body, name=name, grid=(nt,),
        in_specs=[_row_spec(w, t), full((2, w, ns)), full((2, 1, ns)), full((2, ns, w)), full((1, w)),
                  full((w, w)), full((1, w)), full((1, w))],
        out_specs=[_row_spec(w, t), pl.BlockSpec((t, w), lambda i: (i, 1)), pl.BlockSpec((1, 2, ns), lambda i: (i, 0, 0))],
        out_shape=[jax.ShapeDtypeStruct((r, w), F32), jax.ShapeDtypeStruct((r, 2 * w), BF16),
                   jax.ShapeDtypeStruct((nt, 2, ns), F32)],
        scratch_shapes=[pltpu.VMEM((2, 1, ns), F32)],
        compiler_params=_cparams(("arbitrary",)),
    )(u, wb, a, wc, dskip, wglu, bglu, gnorm)


def _s5_bwd(u, y, don, xstart, wb, a, wc, dskip, wglu, bglu, gnorm, name, ride=((), ()), don_block=0):
    r = u.shape[0]
    t = S5_TILE
    nt = r // t
    ns = wb.shape[2]
    w = S5_WIDTH
    nt_dims = ((1,), (1,))
    tn_dims = ((0,), (0,))

    def body(u_ref, y_ref, don_ref, xs_ref, wb_hbm, a_ref, wc_hbm, d_ref, wg_ref, bg_ref, gn_ref,
             du_ref, da_ref, dd_ref, dbg_ref, dgn_ref, dwb_hbm, dwc_hbm, dwg_hbm,
             wb_ref, wc_ref, lam_ref, acc_wb, acc_wc, acc_wg, sem):
        i = pl.program_id(0)
        ar, ai = a_ref[0], a_ref[1]

        @pl.when(i == 0)
        def _():
            c1 = pltpu.make_async_copy(wb_hbm, wb_ref, sem.at[0])
            c2 = pltpu.make_async_copy(wc_hbm, wc_ref, sem.at[1])
            c1.start()
            c2.start()
            lam_ref[...] = jnp.zeros_like(lam_ref)
            acc_wb[...] = jnp.zeros_like(acc_wb)
            acc_wc[...] = jnp.zeros_like(acc_wc)
            acc_wg[...] = jnp.zeros_like(acc_wg)
            da_ref[...] = jnp.zeros_like(da_ref)
            dd_ref[...] = jnp.zeros_like(dd_ref)
            dbg_ref[...] = jnp.zeros_like(dbg_ref)
            dgn_ref[...] = jnp.zeros_like(dgn_ref)
            c1.wait()
            c2.wait()

        u_ = u_ref[...]
        y_ = y_ref[...]
        ub = u_.astype(BF16)
        h, th = _gelu(y_)
        hb = h.astype(BF16)
        wg = wg_ref[...]
        gate = _sigmoid(_dot(hb, wg) + bg_ref[...])
        d_out, dgn = _rms_bwd(h * gate, gn_ref[...], don_ref[...])
        dgn_ref[...] += dgn
        dhw = d_out * h * gate * (1.0 - gate)
        dhwb = dhw.astype(BF16)
        dh = d_out * gate + _dot(dhwb, wg, nt_dims)
        acc_wg[...] += _dot(hb, dhwb, tn_dims)
        dbg_ref[...] += jnp.sum(dhw, axis=0, keepdims=True)
        dgelu = 0.5 * (1.0 + th) + 0.5 * y_ * (1.0 - th * th) * _GELU_C * (1.0 + 3.0 * 0.044715 * y_ * y_)
        dy = dh * dgelu
        dd_ref[...] += jnp.sum(dy * u_, axis=0, keepdims=True)
        dyb = dy.astype(BF16)
        chunks = list(range(S5_CHUNKS))
        sl_s = [slice(c * (ns // S5_CHUNKS), (c + 1) * (ns // S5_CHUNKS)) for c in chunks]
        sl_u = [slice(c * (w // S5_CHUNKS), (c + 1) * (w // S5_CHUNKS)) for c in chunks]
        bu, gx, x_, lam, dus = {}, {}, {}, {}, {}

        def st_inputs(c):
            su, ss = sl_u[c], sl_s[c]
            bu[c] = (_dot(ub[:, su], wb_ref[0, su, ss]), _dot(ub[:, su], wb_ref[1, su, ss]))
            gx[c] = (_dot(dyb[:, su], wc_ref[0, ss, su], nt_dims), -_dot(dyb[:, su], wc_ref[1, ss, su], nt_dims))

        def st_states(c):
            su, ss = sl_u[c], sl_s[c]
            first = (xs_ref[0, 0:1, ss], xs_ref[0, 1:2, ss])
            xr, xi = _s5_scan(*bu[c], ar[:, ss], ai[:, ss], t, carry=first)
            acc_wc[0, ss, su] += _dot(xr.astype(BF16), dyb[:, su], tn_dims)
            acc_wc[1, ss, su] -= _dot(xi.astype(BF16), dyb[:, su], tn_dims)
            x_[c] = (_s5_prev_rows(xr, first[0], t), _s5_prev_rows(xi, first[1], t))

        def st_adjoint(c):
            su, ss = sl_u[c], sl_s[c]
            lr, li = _s5_scan(*gx[c], ar[:, ss], -ai[:, ss], t, reverse=True, carry=(lam_ref[0, :, ss], lam_ref[1, :, ss]))
            lam_ref[0, :, ss] = lr[0:1, :]
            lam_ref[1, :, ss] = li[0:1, :]
            lrb, lib = lr.astype(BF16), li.astype(BF16)
            acc_wb[0, su, ss] += _dot(ub[:, su], lrb, tn_dims)
            acc_wb[1, su, ss] += _dot(ub[:, su], lib, tn_dims)
            dus[c] = _dot(lrb, wb_ref[0, su, ss], nt_dims) + _dot(lib, wb_ref[1, su, ss], nt_dims)
            lam[c] = (lr, li)

        def st_decay(c):
            ss = sl_s[c]
            (lr, li), (xpr, xpi) = lam[c], x_[c]
            da_ref[0, :, ss] += jnp.sum(lr * xpr + li * xpi, axis=0, keepdims=True)
            da_ref[1, :, ss] += jnp.sum(li * xpr - lr * xpi, axis=0, keepdims=True)

        _emit_chains(chunks, [st_inputs, st_states, st_adjoint, st_decay], False)
        du_ref[...] = d_ref[...] * dy + jnp.concatenate([dus[c] for c in chunks], axis=1)

        @pl.when(i == nt - 1)
        def _():
            cps = [pltpu.make_async_copy(acc_wb, dwb_hbm, sem.at[0]), pltpu.make_async_copy(acc_wc, dwc_hbm, sem.at[1]),
                   pltpu.make_async_copy(acc_wg, dwg_hbm, sem.at[2])]
            for c in cps:
                c.start()
            for c in cps:
                c.wait()

    n_ride = len(ride[0])
    n_in, n_out, n_scratch = 11, 8, 7

    def body_with_ride(*refs):
        ins, rest = refs[:n_in], refs[n_in:]
        ride_in, rest = rest[:n_ride], rest[n_ride:]
        outs, rest = rest[:n_out], rest[n_out:]
        ride_out, rest = rest[:n_ride], rest[n_ride:]
        scratch, ride_sems = rest[:n_scratch], rest[n_scratch:]
        if n_ride:
            @pl.when(pl.program_id(0) == 0)
            def _():
                for cp in _exchange_copies(ride_in, ride_out, ride[1], *ride_sems):
                    cp.start()
        body(*ins, *outs, *scratch)
        if n_ride:
            @pl.when(pl.program_id(0) == nt - 1)
            def _():
                for cp in _exchange_copies(ride_in, ride_out, ride[1], *ride_sems):
                    cp.wait()

    rev = lambda i: (nt - 1 - i, 0)
    full = lambda shape: pl.BlockSpec(shape, lambda i: (0,) * len(shape))
    hbm = pl.BlockSpec(memory_space=pl.ANY)
    outs = _pallas(
        body_with_ride, name=name, grid=(nt,),
        in_specs=[pl.BlockSpec((t, w), rev), pl.BlockSpec((t, w), rev), pl.BlockSpec((t, w), lambda i: (nt - 1 - i, don_block)),
                  pl.BlockSpec((1, 2, ns), lambda i: (nt - 1 - i, 0, 0)), hbm, full((2, 1, ns)), hbm, full((1, w)),
                  full((w, w)), full((1, w)), full((1, w))] + [hbm] * n_ride,
        out_specs=[pl.BlockSpec((t, w), rev), full((2, 1, ns)), full((1, w)), full((1, w)), full((1, w)), hbm, hbm, hbm]
        + [hbm] * n_ride,
        out_shape=[jax.ShapeDtypeStruct((r, w), F32), jax.ShapeDtypeStruct((2, 1, ns), F32)]
        + [jax.ShapeDtypeStruct((1, w), F32)] * 3
        + [jax.ShapeDtypeStruct((2, w, ns), F32), jax.ShapeDtypeStruct((2, ns, w), F32), jax.ShapeDtypeStruct((w, w), F32)]
        + _exchange_shapes(*ride),
        scratch_shapes=[pltpu.VMEM((2, w, ns), BF16), pltpu.VMEM((2, ns, w), BF16), pltpu.VMEM((2, 1, ns), F32),
                        pltpu.VMEM((2, w, ns), F32), pltpu.VMEM((2, ns, w), F32), pltpu.VMEM((w, w), F32),
                        pltpu.SemaphoreType.DMA((3,))] + (_exchange_sems(n_ride) if n_ride else []),
        compiler_params=_cparams(("arbitrary",)),
    )(u, y, don, xstart, wb, a, wc, dskip, wglu, bglu, gnorm, *ride[0])
    return tuple(outs[:n_out]) + (list(outs[n_out:]),)


def _s5_expand(lam_re, lam_im, log_dt, b_re, b_im, c_re, c_im):
    g, n, p = S5_GROUPS, S5_STATE, S5_GROUP
    ns = g * n
    rows = lambda x: x.reshape(1, ns)
    logdt = jnp.repeat(log_dt.reshape(g), n).reshape(1, ns)
    btr = b_re.reshape(ns, p).T
    bti = b_im.reshape(ns, p).T
    ctr = c_re.transpose(0, 2, 1).reshape(ns, p)
    cti = c_im.transpose(0, 2, 1).reshape(ns, p)
    mask = (jnp.arange(g * p)[:, None] // p) == (jnp.arange(ns)[None, :] // n)
    return rows(lam_re), rows(lam_im), logdt, btr, bti, ctr, cti, mask


def _s5_block_diag_b(bb, mask):
    return jnp.where(mask, jnp.tile(bb, (S5_GROUPS, 1)), 0.0)


def _s5_block_diag_c(ct, mask):
    return jnp.where(mask.T, jnp.tile(ct, (1, S5_GROUPS)), 0.0)


def _s5_diag_of_b(dwb, mask):
    return jnp.where(mask, dwb, 0.0).reshape(S5_GROUPS, S5_GROUP, -1).sum(0)


def _s5_diag_of_c(dwc, mask):
    ns = dwc.shape[0]
    return jnp.where(mask.T, dwc, 0.0).reshape(ns, S5_GROUPS, S5_GROUP).sum(1)


DN_PRE_TILE = 256
_DN_QKV = 3 * DN_WIDTH


def _halo_specs(width, tile, nt, prev):
    per = tile // 8
    if prev:
        return pl.BlockSpec((8, width), lambda i: (jnp.maximum(i * per - 1, 0), 0))
    return pl.BlockSpec((8, width), lambda i: (jnp.minimum((i + 1) * per, nt * per - 1), 0))


def _shift_down(x, halo, s, t):
    xx = jnp.concatenate([halo, x], axis=0)
    return pltpu.roll(xx, s, 0)[8:]


def _shift_up(x, halo, s, t):
    xx = jnp.concatenate([x, halo], axis=0)
    return pltpu.roll(xx, t + 8 - s, 0)[:t]


def _silu(x):
    s = _sigmoid(x)
    return x * s, s


def _dn_gates(ab, alog, dtb, live):
    lane = lax.broadcasted_iota(jnp.int32, (1, 128), 1)
    g = -jnp.exp(alog) * _softplus(ab + dtb)
    beta = _sigmoid(ab)
    return jnp.where(live & (lane < DN_HEADS), g, jnp.where(live & (lane < 2 * DN_HEADS), beta, 0.0))


def _dn_pre_fwd(proj, ab, conv_w, alog, dtb, pad, name):
    r = proj.shape[0]
    t = DN_PRE_TILE
    nt = r // t
    scale = DN_HEAD_DIM ** -0.5

    def body(x_ref, halo_ref, ab_ref, w_ref, al_ref, dt_ref, q_ref, k_ref, v_ref, gb_ref):
        i = pl.program_id(0)
        act, _ = _silu(_dn_conv(x_ref[...], jnp.where(i > 0, halo_ref[...], 0.0), w_ref[...], t))
        for hd in range(DN_HEADS):
            sl = slice(hd * 128, (hd + 1) * 128)
            for base, o_ref, sc in ((0, q_ref, scale), (DN_WIDTH, k_ref, 1.0)):
                xh = act[:, base + hd * 128: base + (hd + 1) * 128]
                o_ref[:, sl] = (xh * (lax.rsqrt(jnp.sum(xh * xh, axis=-1, keepdims=True) + EPS) * sc)).astype(BF16)
        v_ref[...] = act[:, 2 * DN_WIDTH:].astype(BF16)
        rows = i * t + lax.broadcasted_iota(jnp.int32, (t, 1), 0)
        gb_ref[...] = _dn_gates(ab_ref[...], al_ref[...], dt_ref[...], rows >= pad)

    return _pallas(
        body, name=name, grid=(nt,),
        in_specs=[pl.BlockSpec((t, _DN_QKV), lambda i: (i, 0)), _halo_specs(_DN_QKV, t, nt, True), _row_spec(128, t),
                  pl.BlockSpec((DN_CONV, _DN_QKV), lambda i: (0, 0)), _vec_spec(128), _vec_spec(128)],
        out_specs=[_row_spec(DN_WIDTH, t), _row_spec(DN_WIDTH, t), _row_spec(DN_WIDTH, t), _row_spec(128, t)],
        out_shape=[jax.ShapeDtypeStruct((r, DN_WIDTH), BF16)] * 3 + [jax.ShapeDtypeStruct((r, 128), F32)],
        compiler_params=_cparams(("parallel",)),
    )(proj, proj, ab, conv_w, alog, dtb)


def _dn_conv(x, halo, w, t):
    co = w[DN_CONV - 1:DN_CONV] * x
    for tap in range(DN_CONV - 1):
        co = co + w[tap:tap + 1] * _shift_down(x, halo, DN_CONV - 1 - tap, t)
    return co


def _dn_pre_bwd(proj, conv_w, dq, dk, dv, dgb, ab, alog, dtb, pad, name):
    r = proj.shape[0]
    t = DN_PRE_TILE
    nt = r // t
    scale = DN_HEAD_DIM ** -0.5

    def body(x_ref, halo_ref, w_ref, dq_ref, dk_ref, dv_ref, dgb_ref, ab_ref, al_ref, dt_ref, dco_ref, dab_ref, dal_ref,
             ddt_ref):
        i = pl.program_id(0)

        @pl.when(i == 0)
        def _():
            dal_ref[...] = jnp.zeros_like(dal_ref)
            ddt_ref[...] = jnp.zeros_like(ddt_ref)

        co_ = _dn_conv(x_ref[...], jnp.where(i > 0, halo_ref[...], 0.0), w_ref[...], t)
        act, sg = _silu(co_)
        dsilu = sg * (1.0 + co_ * (1.0 - sg))
        for hd in range(DN_HEADS):
            sl = slice(hd * 128, (hd + 1) * 128)
            for base, d_ref, sc in ((0, dq_ref, scale), (DN_WIDTH, dk_ref, 1.0)):
                cs = slice(base + hd * 128, base + (hd + 1) * 128)
                xh = act[:, cs]
                rn = lax.rsqrt(jnp.sum(xh * xh, axis=-1, keepdims=True) + EPS)
                xhat = xh * rn
                dy = d_ref[:, sl]
                dx = (sc * rn) * (dy - xhat * jnp.sum(dy * xhat, axis=-1, keepdims=True))
                dco_ref[:, cs] = dx * dsilu[:, cs]
        dco_ref[:, 2 * DN_WIDTH:] = dv_ref[...] * dsilu[:, 2 * DN_WIDTH:]
        rows = i * t + lax.broadcasted_iota(jnp.int32, (t, 1), 0)
        live = rows >= pad
        lane = lax.broadcasted_iota(jnp.int32, (1, 128), 1)
        ab_ = ab_ref[...]
        dgb_ = dgb_ref[...]
        is_g = live & (lane < DN_HEADS)
        is_b = live & (lane >= DN_HEADS) & (lane < 2 * DN_HEADS)
        arg = ab_ + dt_ref[...]
        ea = jnp.exp(al_ref[...])
        da = jnp.where(is_g, -dgb_ * ea * _sigmoid(arg), 0.0)
        beta = _sigmoid(ab_)
        dab_ref[...] = (da + jnp.where(is_b, dgb_ * beta * (1.0 - beta), 0.0)).astype(BF16)
        ddt_ref[...] += jnp.sum(da, axis=0, keepdims=True)
        dal_ref[...] += jnp.sum(jnp.where(is_g, -dgb_ * ea * _softplus(arg), 0.0), axis=0, keepdims=True)

    return _pallas(
        body, name=name, grid=(nt,),
        in_specs=[pl.BlockSpec((t, _DN_QKV), lambda i: (i, 0)), _halo_specs(_DN_QKV, t, nt, True),
                  pl.BlockSpec((DN_CONV, _DN_QKV), lambda i: (0, 0)),
                  _row_spec(DN_WIDTH, t), _row_spec(DN_WIDTH, t), _row_spec(DN_WIDTH, t),
                  _row_spec(128, t), _row_spec(128, t), _vec_spec(128), _vec_spec(128)],
        out_specs=[_row_spec(_DN_QKV, t), _row_spec(128, t), _vec_spec(128), _vec_spec(128)],
        out_shape=[jax.ShapeDtypeStruct((r, _DN_QKV), F32), jax.ShapeDtypeStruct((r, 128), BF16),
                   jax.ShapeDtypeStruct((1, 128), F32), jax.ShapeDtypeStruct((1, 128), F32)],
        compiler_params=_cparams(("arbitrary",)),
    )(proj, proj, conv_w, dq, dk, dv, dgb, ab, alog, dtb)


def _dn_conv_bwd(dco, proj, conv_w, dproj, name):
    r = dco.shape[0]
    t = DN_PRE_TILE
    nt = r // t

    def body(d_ref, dh_ref, x_ref, xh_ref, w_ref, dproj_in, dx_ref, dw_ref):
        i = pl.program_id(0)

        @pl.when(i == 0)
        def _():
            dw_ref[...] = jnp.zeros_like(dw_ref)

        d = d_ref[...]
        dhalo = jnp.where(i < nt - 1, dh_ref[...], 0.0)
        x = x_ref[...]
        xhalo = jnp.where(i > 0, xh_ref[...], 0.0)
        w = w_ref[...]
        dx = w[3:4] * d
        dws = [None] * DN_CONV
        dws[3] = jnp.sum(d * x, axis=0, keepdims=True)
        for tap in range(DN_CONV - 1):
            s = DN_CONV - 1 - tap
            dx = dx + w[tap:tap + 1] * _shift_up(d, dhalo, s, t)
            dws[tap] = jnp.sum(d * _shift_down(x, xhalo, s, t), axis=0, keepdims=True)
        dx_ref[...] = dx.astype(BF16)
        dw_ref[...] += jnp.concatenate(dws + [jnp.zeros((8 - DN_CONV, _DN_QKV), F32)], axis=0)

    return _pallas(
        body, name=name, grid=(nt,),
        in_specs=[_row_spec(_DN_QKV, t), _halo_specs(_DN_QKV, t, nt, False),
                  pl.BlockSpec((t, _DN_QKV), lambda i: (i, 0)), _halo_specs(_DN_QKV, t, nt, True),
                  pl.BlockSpec((DN_CONV, _DN_QKV), lambda i: (0, 0)), pl.BlockSpec(memory_space=pl.ANY)],
        out_specs=[_row_spec(_DN_QKV, t), pl.BlockSpec((8, _DN_QKV), lambda i: (0, 0))],
        out_shape=[jax.ShapeDtypeStruct(dproj.shape, BF16), jax.ShapeDtypeStruct((8, _DN_QKV), F32)],
        input_output_aliases={5: 0},
        compiler_params=_cparams(("arbitrary",)),
    )(dco, dco, proj, proj, conv_w, dproj)


def _split3(x):
    hi = x.astype(BF16)
    return hi, (x - hi.astype(F32)).astype(BF16)


def _dot3s(a, b, dims=((1,), (0,))):
    return _dot(a[0], b[0], dims) + (_dot(a[0], b[1], dims) + _dot(a[1], b[0], dims))


def _dot2s(a, b, dims=((1,), (0,))):
    return _dot(a[0], b[0], dims) + _dot(a[1], b[0], dims)


def _dot3(a, b, dims=((1,), (0,))):
    return _dot3s(_split3(a), _split3(b), dims)


def _dn_inverse_many(n_mats):
    c = n_mats[0].shape[0]
    row = lax.broadcasted_iota(jnp.int32, (c, c), 0)
    col = lax.broadcasted_iota(jnp.int32, (c, c), 1)
    eye = (row == col).astype(F32)
    same = row // DN_SUB == col // DN_SUB
    nds = [jnp.where(same, n, 0.0) for n in n_mats]
    nos = [n - nd for n, nd in zip(n_mats, nds)]

    def geometric(bs, order):
        xs = [eye + b for b in bs]
        sp = [_split3(b) for b in bs]
        k = 2
        while k < order:
            sp = [_split3(_dot2s(s_, s_)) for s_ in sp]
            xs = [x + _dot2s(_split3(x), s_) for x, s_ in zip(xs, sp)]
            k *= 2
        return xs

    tds = [_split3(td) for td in geometric([-nd for nd in nds], DN_SUB)]
    ms = [_dot3s(td, _split3(no)) for td, no in zip(tds, nos)]
    xs = geometric([-m for m in ms], c // DN_SUB)
    return [_dot3s(_split3(x), td) for x, td in zip(xs, tds)]


def _dn_chunk_shared(gb_ref, gbt_ref):
    c = DN_CHUNK
    row = lax.broadcasted_iota(jnp.int32, (c, c), 0)
    col = lax.broadcasted_iota(jnp.int32, (c, c), 1)
    gbv = gb_ref[...]
    gam_all = _split_dot((row >= col).astype(BF16), gbv)
    hi, lo = _split3(gbt_ref[...])
    tri_t = (row <= col).astype(BF16)
    return dict(row=row, col=col, gbv=gbv, gam_all=gam_all, gam_rows=_dot(hi, tri_t) + _dot(lo, tri_t),
                lane=lax.broadcasted_iota(jnp.int32, (1, 128), 1))


def _dn_chunk_common(q, k, v, sh, h):
    c = DN_CHUNK
    row, col, lane = sh["row"], sh["col"], sh["lane"]
    q, k, v = q.astype(F32), k.astype(F32), v.astype(F32)
    gam = jnp.sum(jnp.where(lane == h, sh["gam_all"], 0.0), axis=1, keepdims=True)
    beta = jnp.sum(jnp.where(lane == h + DN_HEADS, sh["gbv"], 0.0), axis=1, keepdims=True)
    gam_row = sh["gam_rows"][h:h + 1]
    dec = jnp.where(row >= col, jnp.exp(jnp.minimum(gam - gam_row, 0.0)), 0.0)
    kb, qb = k.astype(BF16), q.astype(BF16)
    nt_dims = ((1,), (1,))
    kk = _dot(kb, kb, nt_dims)
    qk = _dot(qb, kb, nt_dims)
    eg = jnp.exp(gam)
    gam_l = gam[c - 1:c, :]
    return dict(q=q, k=k, v=v, qb=qb, kb=kb, gam=gam, beta=beta, dec=dec, kk=kk, qk=qk, eg=eg, gam_l=gam_l,
                row=row, col=col, lane=lane, att=qk * dec, qg=q * eg, kt=k * jnp.exp(gam_l - gam),
                rhs=jnp.concatenate([v * beta, k * (beta * eg)], axis=1))


def _dn_fwd(q, k, v, gb, gbt, name):
    r = q.shape[0]
    c = DN_CHUNK
    nc = r // c
    dh = DN_HEAD_DIM
    tn_dims = ((0,), (0,))

    def body(q_ref, k_ref, v_ref, gb_ref, gbt_ref, o_ref, ss_ref, ts_ref, s_ref):
        @pl.when(pl.program_id(0) == 0)
        def _():
            s_ref[...] = jnp.zeros_like(s_ref)

        heads = list(range(DN_HEADS))
        sl = [slice(h * dh, (h + 1) * dh) for h in heads]
        sh = _dn_chunk_shared(gb_ref, gbt_ref)
        zs = [_dn_chunk_common(q_ref[:, sl[h]], k_ref[:, sl[h]], v_ref[:, sl[h]], sh, h) for h in heads]
        t_invs = _dn_inverse_many([jnp.where(sh["row"] > sh["col"], z["beta"] * z["kk"] * z["dec"], 0.0) for z in zs])
        sols = [_dot3(t_inv, z["rhs"]) for t_inv, z in zip(t_invs, zs)]
        ss = [s_ref[h] for h in heads]
        sbs = [s.astype(BF16) for s in ss]
        vnbs = [(sol[:, :dh] - _dot(sol[:, dh:].astype(BF16), sb)).astype(BF16) for sol, sb in zip(sols, sbs)]
        for h in heads:
            o_ref[:, sl[h]] = _dot(zs[h]["qg"].astype(BF16), sbs[h]) + _dot(zs[h]["att"].astype(BF16), vnbs[h])
        for h in heads:
            ss_ref[0, h] = ss[h]
            ts_ref[0, h] = t_invs[h]
            s_ref[h] = ss[h] * jnp.exp(zs[h]["gam_l"]) + _dot(zs[h]["kt"].astype(BF16), vnbs[h], tn_dims)

    blk = pl.BlockSpec((c, DN_WIDTH), lambda ci: (ci, 0))
    sav = pl.BlockSpec((1, DN_HEADS, dh, dh), lambda ci: (ci, 0, 0, 0))
    return _pallas(
        body, name=name, grid=(nc,),
        in_specs=[blk, blk, blk, pl.BlockSpec((c, 128), lambda ci: (ci, 0)), pl.BlockSpec((16, c), lambda ci: (0, ci))],
        out_specs=[blk, sav, sav],
        out_shape=[jax.ShapeDtypeStruct((r, DN_WIDTH), F32), jax.ShapeDtypeStruct((nc, DN_HEADS, dh, dh), F32),
                   jax.ShapeDtypeStruct((nc, DN_HEADS, dh, dh), F32)],
        scratch_shapes=[pltpu.VMEM((DN_HEADS, dh, dh), F32)],
        compiler_params=_cparams(("arbitrary",)),
    )(q, k, v, gb, gbt)


def _dn_bwd(q, k, v, gb, gbt, ssave, tsave, do, name):
    r = q.shape[0]
    c = DN_CHUNK
    nc = r // c
    dh = DN_HEAD_DIM
    nt_dims = ((1,), (1,))
    tn_dims = ((0,), (0,))

    def body(q_ref, k_ref, v_ref, gb_ref, gbt_ref, ss_ref, ts_ref, do_ref, dq_ref, dk_ref, dv_ref, dgb_ref, ds_ref):
        @pl.when(pl.program_id(0) == 0)
        def _():
            ds_ref[...] = jnp.zeros_like(ds_ref)

        heads = list(range(DN_HEADS))
        sl = [slice(h * dh, (h + 1) * dh) for h in heads]
        sh = _dn_chunk_shared(gb_ref, gbt_ref)
        row, col, lane = sh["row"], sh["col"], sh["lane"]
        rs = lambda x: jnp.sum(x, axis=1, keepdims=True)
        tot = lambda x: jnp.sum(rs(x), axis=0, keepdims=True)
        st = [dict() for _ in heads]
        dgb_parts = []

        def s_common(h):
            st[h].update(_dn_chunk_common(q_ref[:, sl[h]], k_ref[:, sl[h]], v_ref[:, sl[h]], sh, h))
            st[h]["t"] = _split3(ts_ref[0, h])

        def s_sol(h):
            st[h]["sol"] = _dot3s(st[h]["t"], _split3(st[h]["rhs"]))

        def s_state(h):
            z = st[h]
            sol = z["sol"]
            kcd = sol[:, dh:]
            s = ss_ref[0, h]
            sb = s.astype(BF16)
            vnb = (sol[:, :dh] - _dot(kcd.astype(BF16), sb)).astype(BF16)
            ds_next = ds_ref[h]
            dsb = ds_next.astype(BF16)
            dob = do_ref[:, sl[h]].astype(BF16)
            z["dqg"] = _dot(dob, sb, nt_dims)
            ds = _dot(z["qg"].astype(BF16), dob, tn_dims)
            z["d_att"] = jnp.where(row >= col, _dot(dob, vnb, nt_dims), 0.0)
            dvn = _dot(z["att"].astype(BF16), dob, tn_dims) + _dot(z["kt"].astype(BF16), dsb)
            z["dkt"] = _dot(vnb, dsb, nt_dims)
            eg_l = jnp.exp(z["gam_l"])
            ds = ds + ds_next * eg_l
            z["dgam_l"] = tot(ds_next * s) * eg_l
            dvnb = dvn.astype(BF16)
            dkcd = -_dot(dvnb, sb, nt_dims)
            ds_ref[h] = ds - _dot(kcd.astype(BF16), dvnb, tn_dims)
            z["dsol"] = jnp.concatenate([dvn, dkcd], axis=1)

        def s_drhs(h):
            st[h]["drhs"] = _dot3s(st[h]["t"], _split3(st[h]["dsol"]), tn_dims)

        def s_dn(h):
            z = st[h]
            z["dn"] = jnp.where(row > col, -_dot3(z["drhs"], z["sol"], nt_dims), 0.0)

        def s_rest(h):
            z = st[h]
            k_, v_, kb, qb = z["k"], z["v"], z["kb"], z["qb"]
            beta, eg, dec, kk, qk, gam, gam_l = z["beta"], z["eg"], z["dec"], z["kk"], z["qk"], z["gam"], z["gam_l"]
            dn, d_att, dqg, dkt = z["dn"], z["d_att"], z["dqg"], z["dkt"]
            drv, drk = z["drhs"][:, :dh], z["drhs"][:, dh:]
            s_rkk = rs(drk * k_)
            dv_ref[:, sl[h]] = drv * beta
            dbeta = rs(drv * v_) + s_rkk * eg + rs(dn * kk * dec)
            dk = drk * (beta * eg)
            dgam = s_rkk * beta * eg
            dkk = (dn * beta * dec).astype(BF16)
            dd = dn * beta * kk + d_att * qk
            dqk = (d_att * dec).astype(BF16)
            dq_ref[:, sl[h]] = _dot(dqk, kb) + dqg * eg
            dk = dk + _dot(dqk, qb, tn_dims) + _dot(dkk, kb) + _dot(dkk, kb, tn_dims)
            w = dd * dec
            wh, wl = _split3(w)
            ones = jnp.ones((c, 128), BF16)
            col_sum = (_dot(wh, ones, tn_dims) + _dot(wl, ones, tn_dims))[:, 0:1]
            dgam = dgam + rs(w) - col_sum + rs(dqg * z["qg"]) - rs(dkt * z["kt"])
            dk_ref[:, sl[h]] = dk + dkt * jnp.exp(gam_l - gam)
            dgam_l = z["dgam_l"] + tot(dkt * z["kt"])
            rowc = lax.broadcasted_iota(jnp.int32, (c, 1), 0)
            dgam = dgam + jnp.where(rowc == c - 1, dgam_l, 0.0)
            dg = _split_dot((row <= col).astype(BF16), jnp.broadcast_to(dgam, (c, 128)))[:, 0:1]
            dgb_parts.append(jnp.where(lane == h, dg, 0.0) + jnp.where(lane == h + DN_HEADS, dbeta, 0.0))

        _emit_chains(heads, [s_common, s_sol, s_state, s_drhs, s_dn, s_rest], False)
        dgb = dgb_parts[0]
        for part in dgb_parts[1:]:
            dgb = dgb + part
        dgb_ref[...] = dgb

    blk = pl.BlockSpec((c, DN_WIDTH), lambda ci: (nc - 1 - ci, 0))
    sav = pl.BlockSpec((1, DN_HEADS, dh, dh), lambda ci: (nc - 1 - ci, 0, 0, 0))
    gspec = pl.BlockSpec((c, 128), lambda ci: (nc - 1 - ci, 0))
    return _pallas(
        body, name=name, grid=(nc,),
        in_specs=[blk, blk, blk, gspec, pl.BlockSpec((16, c), lambda ci: (0, nc - 1 - ci)), sav, sav, blk],
        out_specs=[blk, blk, blk, gspec],
        out_shape=[jax.ShapeDtypeStruct((r, DN_WIDTH), F32)] * 3 + [jax.ShapeDtypeStruct((r, 128), F32)],
        scratch_shapes=[pltpu.VMEM((DN_HEADS, dh, dh), F32)],
        compiler_params=_cparams(("arbitrary",)),
    )(q, k, v, gb, gbt, ssave, tsave, do)


def _dn_post_fwd(o, proj, g, name):
    r = o.shape[0]

    def body(o_ref, z_ref, g_ref, y_ref):
        g_ = g_ref[...]
        for hd in range(DN_HEADS):
            sl = slice(hd * 128, (hd + 1) * 128)
            sz, _ = _silu(z_ref[:, sl])
            y_ref[:, sl] = (_rms(o_ref[:, sl], g_) * sz).astype(BF16)

    return _pallas(body, name=name, grid=(r // ROW_TILE,),
                   in_specs=[_row_spec(DN_WIDTH), pl.BlockSpec((ROW_TILE, DN_WIDTH), lambda i: (i, 3)), _vec_spec(128)],
                   out_specs=_row_spec(DN_WIDTH), out_shape=jax.ShapeDtypeStruct((r, DN_WIDTH), BF16),
                   compiler_params=_cparams(("parallel",)))(o, proj, g)


def _dn_post_bwd(o, proj, g, dy, name):
    r = o.shape[0]

    def body(o_ref, z_ref, g_ref, dy_ref, do_ref, dz_ref, dg_ref):
        @pl.when(pl.program_id(0) == 0)
        def _():
            dg_ref[...] = jnp.zeros_like(dg_ref)

        g_ = g_ref[...]
        for hd in range(DN_HEADS):
            sl = slice(hd * 128, (hd + 1) * 128)
            z_ = z_ref[:, sl]
            sz, sg = _silu(z_)
            dy_ = dy_ref[:, sl]
            o_ = o_ref[:, sl]
            dz_ref[:, sl] = (dy_ * _rms(o_, g_) * (sg * (1.0 + z_ * (1.0 - sg)))).astype(BF16)
            dx, dg = _rms_bwd(o_, g_, dy_ * sz)
            do_ref[:, sl] = dx
            dg_ref[...] += dg

    return _pallas(body, name=name, grid=(r // ROW_TILE,),
                   in_specs=[_row_spec(DN_WIDTH), pl.BlockSpec((ROW_TILE, DN_WIDTH), lambda i: (i, 3)), _vec_spec(128),
                             _row_spec(DN_WIDTH)],
                   out_specs=[_row_spec(DN_WIDTH), pl.BlockSpec((ROW_TILE, DN_WIDTH), lambda i: (i, 3)), _vec_spec(128)],
                   out_shape=[jax.ShapeDtypeStruct((r, DN_WIDTH), F32), jax.ShapeDtypeStruct((r, 4 * DN_WIDTH), BF16),
                              jax.ShapeDtypeStruct((1, 128), F32)],
                   compiler_params=_cparams(("arbitrary",)))(o, proj, g, dy)


def _exchange(arrays, scatter, name):
    n = len(arrays)

    def body(*refs):
        copies = _exchange_copies(refs[:n], refs[n:2 * n], scatter, *refs[2 * n:])
        for cp in copies:
            cp.start()
        for cp in copies:
            cp.wait()

    hbm = pl.BlockSpec(memory_space=pl.ANY)
    return _pallas(
        body, name=name, in_specs=[hbm] * n, out_specs=[hbm] * n, out_shape=_exchange_shapes(arrays, scatter),
        scratch_shapes=_exchange_sems(n),
    )(*arrays)


def _exchange_shapes(arrays, scatter):
    return [jax.ShapeDtypeStruct((N_DEV,) + (a.shape[1:] if sc else a.shape), a.dtype) for a, sc in zip(arrays, scatter)]


def _exchange_sems(n):
    return [pltpu.SemaphoreType.DMA((n * N_DEV,)), pltpu.SemaphoreType.DMA((n * N_DEV,)), pltpu.SemaphoreType.DMA((n,))]


def _exchange_copies(in_refs, out_refs, scatter, send_sems, recv_sems, local_sems):
    mx, my, mc = lax.axis_index("x"), lax.axis_index("y"), lax.axis_index("c")
    me = 4 * mx + 2 * my + mc
    copies = []
    for a in range(len(in_refs)):
        src_own = in_refs[a].at[me] if scatter[a] else in_refs[a]
        copies.append(pltpu.make_async_copy(src_own, out_refs[a].at[me], local_sems.at[a]))
        for kbits in range(1, N_DEV):
            px = lax.rem(mx + ((kbits >> 2) & 1), 2)
            py = lax.rem(my + ((kbits >> 1) & 1), 2)
            pc = lax.rem(mc + (kbits & 1), 2)
            src = in_refs[a].at[4 * px + 2 * py + pc] if scatter[a] else in_refs[a]
            copies.append(pltpu.make_async_remote_copy(
                src_ref=src, dst_ref=out_refs[a].at[me],
                send_sem=send_sems.at[a * N_DEV + kbits], recv_sem=recv_sems.at[a * N_DEV + kbits],
                device_id=(px, py, pc), device_id_type=pl.DeviceIdType.MESH))
    return copies


def _adamw(gstack, w, m, v, name):
    a, b = w.shape
    ta = a
    for t in (1024, 512, 256, 128, 64, 32, 16, 8):
        if a % t == 0 and N_DEV * t * b * 4 <= 4 * 1024 * 1024:
            ta = t
            break
    c1 = 1.0 / (1.0 - ADAM_B1 ** ADAM_STEP)
    c2 = 1.0 / (1.0 - ADAM_B2 ** ADAM_STEP)

    def body(g_ref, w_ref, m_ref, v_ref, og_ref, od_ref, om_ref, ov_ref):
        g = g_ref[0].astype(F32)
        for s in range(1, N_DEV):
            g = g + g_ref[s].astype(F32)
        m_new = ADAM_B1 * m_ref[...] + (1.0 - ADAM_B1) * g
        v_new = ADAM_B2 * v_ref[...] + (1.0 - ADAM_B2) * (g * g)
        og_ref[...] = g
        om_ref[...] = m_new
        ov_ref[...] = v_new
        od_ref[...] = -ADAM_LR * ((m_new * c1) / (jnp.sqrt(v_new * c2) + ADAM_EPS) + ADAM_WD * w_ref[...])

    spec = pl.BlockSpec((ta, b), lambda i: (i, 0))
    return _pallas(
        body, name=name, grid=(a // ta,),
        in_specs=[pl.BlockSpec((N_DEV, ta, b), lambda i: (0, i, 0)), spec, spec, spec],
        out_specs=[spec] * 4, out_shape=[jax.ShapeDtypeStruct((a, b), F32)] * 4,
        compiler_params=_cparams(("parallel",)),
    )(gstack, w, m, v)


_WEIGHTS = ['meta_tokens', 'pre_mix_norm', 'post_mix_norm', 'pre_mlp_norm', 'post_mlp_norm', 'mlp_w1', 'mlp_w2',
            'w_in_even', 'w_out_even', 'sb_out_norm', 's5_lambda_re', 's5_lambda_im', 's5_log_dt', 's5_b_re', 's5_b_im',
            's5_c_re', 's5_c_im', 's5_d', 's5_w_glu', 's5_b_glu', 's5_out_norm', 'w_in_odd', 'dn_conv_w', 'dn_a_log',
            'dn_dt_bias', 'dn_out_norm', 'w_out_odd']
_SHARDED = ['meta_tokens', 'mlp_w1', 'mlp_w2', 'w_in_even', 'w_out_even', 's5_w_glu', 'w_in_odd', 'dn_conv_w', 'w_out_odd']
_SMALL = [n for n in _WEIGHTS if n not in _SHARDED]
_GATHER_FIRST = ['meta_tokens', 'w_in_even', 's5_w_glu', 'w_out_even']
_GATHER_LATE = [n for n in _SHARDED if n not in _GATHER_FIRST]
_REDUCE_EARLY = ['mlp_w1', 'mlp_w2', 'w_in_odd', 'dn_conv_w', 'w_out_odd', 'w_out_even']


def _view2d(name, a):
    return a.reshape(-1, a.shape[-1])


def _unshard(name, g):
    if name == 'mlp_w1':
        return g.reshape(N_DEV, 2, D_MODEL, -1).transpose(1, 2, 0, 3).reshape(2, D_MODEL, D_FF)
    if name == 'mlp_w2':
        return g.reshape(N_DEV, 2, -1, D_MODEL).transpose(1, 0, 2, 3).reshape(2, D_FF, D_MODEL)
    if name in ('w_in_even', 'w_in_odd', 'dn_conv_w', 'meta_tokens'):
        return g.transpose(1, 0, 2).reshape(g.shape[1], -1)
    return g.reshape(-1, g.shape[-1])


def _to_blocks(name, full):
    if name == 'mlp_w1':
        return full.reshape(2, D_MODEL, N_DEV, -1).transpose(2, 0, 1, 3).reshape(N_DEV, 2 * D_MODEL, -1)
    if name == 'mlp_w2':
        return full.reshape(2, N_DEV, -1, D_MODEL).transpose(1, 0, 2, 3).reshape(N_DEV, -1, D_MODEL)
    if name in ('w_in_even', 'w_in_odd', 'dn_conv_w', 'meta_tokens'):
        return full.reshape(full.shape[0], N_DEV, -1).transpose(1, 0, 2)
    return full.reshape(N_DEV, -1, full.shape[-1])


def _pack(parts):
    rows = []
    for p in parts:
        flat = p.reshape(-1)
        rows.append(jnp.pad(flat, (0, (-flat.shape[0]) % 128)).reshape(-1, 128))
    return jnp.concatenate(rows, axis=0)


def _unpack(packed, like):
    out, at = [], 0
    for p in like:
        n = math.prod(p.shape)
        nrow = -(-n // 128)
        out.append(packed[at:at + nrow].reshape(-1)[:n].reshape(p.shape))
        at += nrow
    return out


def _lane_vec(x, width=128):
    flat = x.reshape(-1)
    return jnp.pad(flat, (0, width - flat.shape[0])).reshape(1, width)


def kernel(x, meta_tokens, pre_mix_norm, post_mix_norm, pre_mlp_norm, post_mlp_norm, mlp_w1, mlp_w2, w_in_even, w_out_even, sb_out_norm, s5_lambda_re, s5_lambda_im, s5_log_dt, s5_b_re, s5_b_im, s5_c_re, s5_c_im, s5_d, s5_w_glu, s5_b_glu, s5_out_norm, w_in_odd, dn_conv_w, dn_a_log, dn_dt_bias, dn_out_norm, w_out_odd, loss_target, m_meta_tokens, m_pre_mix_norm, m_post_mix_norm, m_pre_mlp_norm, m_post_mlp_norm, m_mlp_w1, m_mlp_w2, m_w_in_even, m_w_out_even, m_sb_out_norm, m_s5_lambda_re, m_s5_lambda_im, m_s5_log_dt, m_s5_b_re, m_s5_b_im, m_s5_c_re, m_s5_c_im, m_s5_d, m_s5_w_glu, m_s5_b_glu, m_s5_out_norm, m_w_in_odd, m_dn_conv_w, m_dn_a_log, m_dn_dt_bias, m_dn_out_norm, m_w_out_odd, v_meta_tokens, v_pre_mix_norm, v_post_mix_norm, v_pre_mlp_norm, v_post_mlp_norm, v_mlp_w1, v_mlp_w2, v_w_in_even, v_w_out_even, v_sb_out_norm, v_s5_lambda_re, v_s5_lambda_im, v_s5_log_dt, v_s5_b_re, v_s5_b_im, v_s5_c_re, v_s5_c_im, v_s5_d, v_s5_w_glu, v_s5_b_glu, v_s5_out_norm, v_w_in_odd, v_dn_conv_w, v_dn_a_log, v_dn_dt_bias, v_dn_out_norm, v_w_out_odd):
    given = dict(locals())
    w = {n: given[n] for n in _WEIGHTS}
    mom_m = {n: given["m_" + n] for n in _WEIGHTS}
    mom_v = {n: given["v_" + n] for n in _WEIGHTS}

    seq = x.shape[1]
    assert x.shape[0] == 1 and seq % ROW_TILE == 0
    r = seq + ROW_TILE
    pad = ROW_TILE - N_META

    wire = {n: (F32 if n in ('dn_conv_w', 'meta_tokens') else BF16) for n in _SHARDED}
    shard_wire = lambda n: _view2d(n, w[n]).astype(wire[n])
    gathered = _exchange([shard_wire(n) for n in _GATHER_FIRST], [False] * len(_GATHER_FIRST), "gather_first")
    full = {n: _unshard(n, g_) for n, g_ in zip(_GATHER_FIRST, gathered)}
    w_ie, w_oe, w_glu = full['w_in_even'], full['w_out_even'], full['s5_w_glu']
    row = lambda v_: v_.reshape(1, -1)

    hs0 = jnp.concatenate([jnp.zeros((pad, D_MODEL), F32), full['meta_tokens'], x[0]], axis=0)
    hn0 = _norm_pre(hs0, row(pre_mix_norm[0]), "pre_mix_0")
    qkv = _mm_fwd(hn0, w_ie[:, :3 * SB_WIDTH], "in_even_qkv", out_dtypes=(BF16,))
    u = _mm_fwd(hn0, w_ie[:, 3 * SB_WIDTH:], "in_even_u")
    q, k, v = qkv[:, :SB_WIDTH], qkv[:, SB_WIDTH:2 * SB_WIDTH], qkv[:, 2 * SB_WIDTH:]
    nb = r // ATT_BLK
    blocks_t = lambda t_: t_.reshape(nb, ATT_BLK, 4, 128).transpose(2, 0, 3, 1)
    o_sb, ssave, gathered = _sb_fwd(q, k, blocks_t(v), pad, "sb_fwd",
                                    ride=([shard_wire(n) for n in _GATHER_LATE], [False] * len(_GATHER_LATE)))
    full.update({n: _unshard(n, g_) for n, g_ in zip(_GATHER_LATE, gathered)})
    w1, w2, w_oo, conv_w = full['mlp_w1'], full['mlp_w2'], full['w_out_odd'], full['dn_conv_w']
    w_io = full['w_in_odd'][:, :4 * DN_WIDTH]
    w_ab = jnp.pad(full['w_in_odd'][:, 4 * DN_WIDTH:], ((0, 0), (0, 128 - 2 * DN_HEADS)))

    lam_re, lam_im, logdt, btr, bti, ctr, cti, s5_mask = _s5_expand(
        s5_lambda_re[0], s5_lambda_im[0], s5_log_dt[0], s5_b_re[0], s5_b_im[0], s5_c_re[0], s5_c_im[0])
    a_re, a_im, bbr, bbi = _s5_prep(lam_re, lam_im, logdt, btr, bti, "s5_prep")
    s5_wb = jnp.stack([_s5_block_diag_b(bbr, s5_mask), _s5_block_diag_b(bbi, s5_mask)]).astype(BF16)
    s5_wc = jnp.stack([_s5_block_diag_c(ctr, s5_mask), _s5_block_diag_c(cti, s5_mask)]).astype(BF16)
    s5_a = jnp.stack([a_re, a_im])
    s5_args = (s5_wb, s5_a, s5_wc, row(s5_d[0]), w_glu, row(s5_b_glu[0]), row(s5_out_norm[0]))
    y_s5, merged, xstart = _s5_fwd(u, *s5_args, "s5_fwd")
    merged = _norm_pre(o_sb, row(sb_out_norm[0]), "sb_out_norm", into=merged)

    mix0, hs1, hn1 = _mm_norm_fwd(merged, w_oe, hs0, row(post_mix_norm[0]), g_pre=row(pre_mlp_norm[0]), name="out_even")
    relu2 = lambda acc: (jnp.square(jnp.maximum(acc, 0.0)), jnp.maximum(acc, 0.0))
    r0, ra0 = _mm_fwd(hn1, w1[0], "mlp_up_0", out_dtypes=(BF16, BF16), epilogue=relu2)
    m0, hs2, hn2 = _mm_norm_fwd(r0, w2[0], hs1, row(post_mlp_norm[0]), g_pre=row(pre_mix_norm[1]), name="mlp_down_0")

    proj = _mm_fwd(hn2, w_io, "in_odd")
    ab = _mm_fwd(hn2, w_ab, "in_odd_gates")
    alog, dtb = _lane_vec(dn_a_log[0]), _lane_vec(dn_dt_bias[0])
    qd, kd, vd, gb = _dn_pre_fwd(proj, ab, conv_w, alog, dtb, pad, "dn_pre")
    gbt = gb[:, :2 * DN_HEADS].T
    o_dn, s_dn, t_dn = _dn_fwd(qd, kd, vd, gb, gbt, "dn_fwd")
    on_dn = _dn_post_fwd(o_dn, proj, row(dn_out_norm[0]), "dn_post")
    mix1, hs3, hn3 = _mm_norm_fwd(on_dn, w_oo, hs2, row(post_mix_norm[1]), g_pre=row(pre_mlp_norm[1]), name="out_odd")
    r1, ra1 = _mm_fwd(hn3, w1[1], "mlp_up_1", out_dtypes=(BF16, BF16), epilogue=relu2)
    dhs, dm1, dg_post_mlp1, loss_part = _mm_norm_fwd(r1, w2[1], hs3, row(post_mlp_norm[1]),
                                                     loss=(loss_target[0], pad + N_META), name="mlp_down_1_loss")
    loss = lax.psum(loss_part, ("x", "y", "c"))

    g = {}
    drelu2 = lambda acc, ra: (acc * (2.0 * ra.astype(F32)),)

    def mlp_bwd(layer, hn, rr, ra, dm):
        dw2 = _mm_wgrad(rr, dm, f"mlp_down_{layer}_wgrad")
        da = _mm_dgrad(dm, w2[layer], f"mlp_down_{layer}_dgrad", out_dtypes=(BF16,), extras=(ra,), epilogue=drelu2)
        dw1 = _mm_wgrad(hn, da, f"mlp_up_{layer}_wgrad")
        return dw1, dw2, da

    dw1_1, dw2_1, da1 = mlp_bwd(1, hn3, r1, ra1, dm1)
    dhs, dmix1, dg_pre_mlp1, dg_post_mix1 = _dgrad_norm_bwd(
        da1, w1[1], dhs, hs3, row(pre_mlp_norm[1]), post=(mix1, row(post_mix_norm[1])), pad=pad, name="post_mix_1_bwd")

    g['w_out_odd'] = _mm_wgrad(on_dn, dmix1, "out_odd_wgrad")
    d_on_dn = _mm_dgrad(dmix1, w_oo, "out_odd_dgrad")
    do_dn, dproj, dg_dn = _dn_post_bwd(o_dn, proj, row(dn_out_norm[0]), d_on_dn, "dn_post_bwd")
    dqd, dkd, dvd, dgb = _dn_bwd(qd, kd, vd, gb, gbt, s_dn, t_dn, do_dn, "dn_bwd")
    dco, dab, d_alog, d_dtb = _dn_pre_bwd(proj, conv_w, dqd, dkd, dvd, dgb, ab, alog, dtb, pad, "dn_pre_bwd")
    dproj, d_conv = _dn_conv_bwd(dco, proj, conv_w, dproj, "dn_conv_bwd")
    g['w_in_odd'] = jnp.concatenate([_mm_wgrad(hn2, dproj, "in_odd_wgrad"),
                                     _mm_wgrad(hn2, dab, "in_odd_gates_wgrad")[:, :2 * DN_HEADS]], axis=1)
    dhn2_gates = _mm_dgrad(dab, w_ab, "in_odd_gates_dgrad")
    g['dn_conv_w'] = d_conv[:DN_CONV]
    g['dn_a_log'], g['dn_dt_bias'], g['dn_out_norm'] = d_alog[0, :DN_HEADS], d_dtb[0, :DN_HEADS], dg_dn[0]

    dhs, dm0, dg_pre_mix1, dg_post_mlp0 = _dgrad_norm_bwd(
        dproj, w_io, dhs, hs2, row(pre_mix_norm[1]), post=(m0, row(post_mlp_norm[0])), add=dhn2_gates, pad=pad,
        name="post_mlp_0_bwd")
    dw1_0, dw2_0, da0 = mlp_bwd(0, hn1, r0, ra0, dm0)
    dhs, dmix0, dg_pre_mlp0, dg_post_mix0 = _dgrad_norm_bwd(
        da0, w1[0], dhs, hs1, row(pre_mlp_norm[0]), post=(mix0, row(post_mix_norm[0])), pad=pad, name="post_mix_0_bwd")

    g['w_out_even'] = _mm_wgrad(merged, dmix0, "out_even_wgrad")
    dmerged = _mm_dgrad(dmix0, w_oe, "out_even_dgrad")
    _, do_sb, _, dg_sb = _norm_bwd(dmerged, post=(o_sb, row(sb_out_norm[0])), pad=pad, dm_dtype=F32,
                                   dhs_cols=(SB_WIDTH, 0), name="sb_out_norm_bwd")
    dq, dk, dv = _sb_bwd(q, k, v, blocks_t(k), ssave, do_sb, pad, "sb_bwd")
    g['mlp_w1'] = jnp.stack([dw1_0, dw1_1])
    g['mlp_w2'] = jnp.stack([dw2_0, dw2_1])
    grad_wire = lambda n: _to_blocks(n, g[n].reshape(full[n].shape)).astype(wire[n])
    du, d_a, d_d, d_bglu, dg_s5, d_wb, d_wc, g['s5_w_glu'], reduced = _s5_bwd(
        u, y_s5, dmerged, xstart, *s5_args, "s5_bwd", don_block=1,
        ride=([grad_wire(n) for n in _REDUCE_EARLY], [True] * len(_REDUCE_EARLY)))
    stacks = dict(zip(_REDUCE_EARLY, reduced))
    g_lr, g_li, g_dt, g_btr, g_bti = _s5_prep_bwd(
        lam_re, lam_im, logdt, btr, bti, d_a[0], d_a[1],
        _s5_diag_of_b(d_wb[0], s5_mask), _s5_diag_of_b(d_wb[1], s5_mask), "s5_prep_bwd")
    gg, nn, pp = S5_GROUPS, S5_STATE, S5_GROUP
    g['s5_lambda_re'], g['s5_lambda_im'] = g_lr.reshape(gg, nn), g_li.reshape(gg, nn)
    g['s5_log_dt'] = g_dt.reshape(gg, nn)[:, 0]
    g['s5_b_re'], g['s5_b_im'] = g_btr.T.reshape(gg, nn, pp), g_bti.T.reshape(gg, nn, pp)
    g['s5_c_re'] = _s5_diag_of_c(d_wc[0], s5_mask).reshape(gg, nn, pp).transpose(0, 2, 1)
    g['s5_c_im'] = _s5_diag_of_c(d_wc[1], s5_mask).reshape(gg, nn, pp).transpose(0, 2, 1)
    g['s5_d'], g['s5_b_glu'], g['s5_out_norm'], g['sb_out_norm'] = d_d[0], d_bglu[0], dg_s5[0], dg_sb[0]
    dqkvu = jnp.concatenate([dq, dk, dv, du], axis=1).astype(BF16)
    g['w_in_even'] = _mm_wgrad(hn0, dqkvu, "in_even_wgrad")
    dhs, _, dg_pre_mix0, _ = _dgrad_norm_bwd(dqkvu, w_ie, dhs, hs0, row(pre_mix_norm[0]), pad=pad, name="pre_mix_0_bwd")

    g['meta_tokens'] = dhs[pad:pad + N_META]
    g['pre_mix_norm'] = jnp.concatenate([dg_pre_mix0, dg_pre_mix1], axis=0)
    g['post_mix_norm'] = jnp.concatenate([dg_post_mix0, dg_post_mix1], axis=0)
    g['pre_mlp_norm'] = jnp.concatenate([dg_pre_mlp0, dg_pre_mlp1], axis=0)
    g['post_mlp_norm'] = jnp.concatenate([dg_post_mlp0, dg_post_mlp1], axis=0)
    grad_x = dhs[pad + N_META:][None]

    small_like = [w[n] for n in _SMALL]
    last = [n for n in _SHARDED if n not in _REDUCE_EARLY]
    partial = [grad_wire(n) for n in last] + [_pack([g[n].reshape(w[n].shape) for n in _SMALL])]
    reduced = _exchange(partial, [True] * len(last) + [False], "reduce_last")
    stacks.update(zip(last, reduced[:-1]))
    grads, deltas, new_m, new_v = {}, {}, {}, {}
    for n in _SHARDED:
        outs = _adamw(stacks[n], _view2d(n, w[n]), _view2d(n, mom_m[n]), _view2d(n, mom_v[n]), f"adamw_{n}")
        grads[n], deltas[n], new_m[n], new_v[n] = (o.reshape(w[n].shape) for o in outs)
    outs = _adamw(reduced[-1], _pack(small_like), _pack([mom_m[n] for n in _SMALL]), _pack([mom_v[n] for n in _SMALL]),
                  "adamw_small")
    for dst, o in zip((grads, deltas, new_m, new_v), outs):
        for n, part in zip(_SMALL, _unpack(o, small_like)):
            dst[n] = part
    return (loss, grad_x, *[grads[n] for n in _WEIGHTS], *[deltas[n] for n in _WEIGHTS],
            *[new_m[n] for n in _WEIGHTS], *[new_v[n] for n in _WEIGHTS])
```

```python
import math

import jax
import jax.numpy as jnp
from jax import lax
from jax.experimental import pallas as pl
from jax.experimental.pallas import tpu as pltpu

F32 = jnp.float32
BF16 = jnp.bfloat16

D_MODEL = 1024
N_META = 16
SB_HEAD_DIM = 64
SB_WIDTH = 512
S5_WIDTH = 512
S5_GROUP = 16
S5_GROUPS = 32
S5_STATE = 64
S5_NS = S5_GROUPS * S5_STATE
DN_HEAD_DIM = 128
DN_HEADS = 8
DN_WIDTH = 1024
DN_CONV = 4
D_FF = 4096
EPS = 1e-6
N_DEV = 8

ADAM_LR = 0.001
ADAM_B1 = 0.9
ADAM_B2 = 0.999
ADAM_EPS = 1e-08
ADAM_WD = 0.01
ADAM_STEP = 10

ROW_TILE = 512
FUSED_FWD_TILE = 512
ATT_BLK = 256
SB_BLOCKS_PER_TRIP = 3
SB_LOG_ZERO = -106.0
SB_FWD_SKEW = False
SB_BWD_SKEW = True
DN_CHUNK = 128
DN_SUB = 16
S5_TILE = 256
S5_CHUNKS = 4
VMEM_LIMIT = 56 * 1024 * 1024

_HIGH = lax.Precision.HIGHEST


def _pallas(body, **kw):
    return pl.pallas_call(body, **kw)


def _cparams(sem):
    return pltpu.CompilerParams(dimension_semantics=sem, vmem_limit_bytes=VMEM_LIMIT)


def _dot(a, b, dims=((1,), (0,))):
    return lax.dot_general(a, b, (dims, ((), ())), preferred_element_type=F32)


def _dot_hi(a, b):
    return lax.dot_general(a, b, (((1,), (0,)), ((), ())), preferred_element_type=F32, precision=_HIGH)


def _split_dot(m_bf16, x):
    hi = x.astype(BF16)
    lo = (x - hi.astype(F32)).astype(BF16)
    return _dot(m_bf16, hi) + _dot(m_bf16, lo)


def _matmul(a, b, *, ta=False, tb=False, tm, tn, tk, name, out_dtypes=(F32,), extras=(), epilogue=None):
    m, k = (a.shape[1], a.shape[0]) if ta else a.shape
    n = b.shape[0] if tb else b.shape[1]
    assert (b.shape[1] if tb else b.shape[0]) == k
    assert m % tm == 0 and n % tn == 0 and k % tk == 0, (name, m, n, k, tm, tn, tk)
    nk = k // tk
    n_ex = len(extras)
    n_out = len(out_dtypes)
    dims = ((0 if ta else 1,), (1 if tb else 0,))

    def finish(acc, ex_refs, o_refs):
        outs = (acc,) if epilogue is None else epilogue(acc, *[r[...] for r in ex_refs])
        for o_ref, o in zip(o_refs, outs):
            o_ref[...] = o.astype(o_ref.dtype)

    def body(*refs):
        a_ref, b_ref = refs[0], refs[1]
        ex_refs = refs[2:2 + n_ex]
        o_refs = refs[2 + n_ex:2 + n_ex + n_out]
        prod = _dot(a_ref[...].astype(BF16), b_ref[...].astype(BF16), dims)
        if nk == 1:
            finish(prod, ex_refs, o_refs)
            return
        acc_ref = refs[-1]
        kk = pl.program_id(2)

        @pl.when(kk == 0)
        def _():
            acc_ref[...] = prod

        @pl.when(kk > 0)
        def _():
            acc_ref[...] += prod

        @pl.when(kk == nk - 1)
        def _():
            finish(acc_ref[...], ex_refs, o_refs)

    a_spec = pl.BlockSpec((tk, tm), lambda j, i, kk: (kk, i)) if ta else pl.BlockSpec((tm, tk), lambda j, i, kk: (i, kk))
    b_spec = pl.BlockSpec((tn, tk), lambda j, i, kk: (j, kk)) if tb else pl.BlockSpec((tk, tn), lambda j, i, kk: (kk, j))
    o_spec = pl.BlockSpec((tm, tn), lambda j, i, kk: (i, j))
    outs = _pallas(
        body, name=name,
        grid=(n // tn, m // tm, nk),
        in_specs=[a_spec, b_spec] + [o_spec] * n_ex,
        out_specs=[o_spec] * n_out,
        out_shape=[jax.ShapeDtypeStruct((m, n), dt) for dt in out_dtypes],
        scratch_shapes=[] if nk == 1 else [pltpu.VMEM((tm, tn), F32)],
        compiler_params=_cparams(("parallel", "parallel", "arbitrary")),
    )(a, b, *extras)
    return outs[0] if n_out == 1 else outs


def _tile(n, cap):
    best = 128
    for t in range(128, min(n, cap) + 1, 128):
        if n % t == 0:
            best = t
    assert n % best == 0, n
    return best


MM_K_CAP = 4096
WGRAD_ROWS = 1536


MM_LHS_TILE_BYTES = 6 * 1024 * 1024


def _row_tile(x, depth):
    tall = 3 * ROW_TILE
    fits = tall * depth * x.dtype.itemsize <= MM_LHS_TILE_BYTES
    return tall if (x.shape[0] % tall == 0 and fits) else ROW_TILE


def _mm_fwd(x, w, name, **kw):
    k, n = w.shape
    tk = _tile(k, MM_K_CAP)
    return _matmul(x, w, tm=_row_tile(x, tk), tn=_tile(n, 1024), tk=tk, name=name, **kw)


def _mm_dgrad(dy, w, name, **kw):
    k, n = w.shape
    tk = _tile(n, MM_K_CAP)
    return _matmul(dy, w, tb=True, tm=_row_tile(dy, tk), tn=_tile(k, 1024), tk=tk, name=name, **kw)


def _mm_wgrad(x, dy, name):
    k, n = x.shape[1], dy.shape[1]
    rows = x.shape[0]
    return _matmul(x, dy, ta=True, tm=_tile(k, 512), tn=_tile(n, 1024),
                   tk=WGRAD_ROWS if rows % WGRAD_ROWS == 0 else ROW_TILE, name=name)


def _rms(x, g):
    r = lax.rsqrt(jnp.mean(x * x, axis=-1, keepdims=True) + EPS)
    return x * r * g


def _rms_bwd(x, g, dy):
    r = lax.rsqrt(jnp.mean(x * x, axis=-1, keepdims=True) + EPS)
    xh = x * r
    dxh = dy * g
    dx = r * (dxh - xh * jnp.mean(dxh * xh, axis=-1, keepdims=True))
    dg = jnp.sum(dy * xh, axis=0, keepdims=True)
    return dx, dg


def _row_spec(width, tile=ROW_TILE):
    return pl.BlockSpec((tile, width), lambda i: (i, 0))


def _vec_spec(width):
    return pl.BlockSpec((1, width), lambda i: (0, 0))


def _norm_pre(hs, g, name, into=None):
    r, d = hs.shape

    def body(x_ref, g_ref, *rest):
        rest[-1][...] = _rms(x_ref[...], g_ref[...]).astype(BF16)

    if into is None:
        return _pallas(body, name=name, grid=(r // ROW_TILE,), in_specs=[_row_spec(d), _vec_spec(d)],
                       out_specs=_row_spec(d), out_shape=jax.ShapeDtypeStruct((r, d), BF16),
                       compiler_params=_cparams(("parallel",)))(hs, g)
    return _pallas(body, name=name, grid=(r // ROW_TILE,),
                   in_specs=[_row_spec(d), _vec_spec(d), pl.BlockSpec(memory_space=pl.ANY)],
                   out_specs=_row_spec(d), out_shape=jax.ShapeDtypeStruct(into.shape, BF16), input_output_aliases={2: 0},
                   compiler_params=_cparams(("parallel",)))(hs, g, into)


def _mm_norm_fwd(a, w, hs, g_post, *, g_pre=None, loss=None, name):
    k, d = w.shape
    r = a.shape[0]
    assert k <= MM_K_CAP and d == hs.shape[1]
    t = FUSED_FWD_TILE
    nt = r // t

    def body(*refs):
        a_ref, w_ref, hs_ref, gp_ref = refs[:4]
        i = pl.program_id(0)
        m = _dot(a_ref[...].astype(BF16), w_ref[...].astype(BF16))
        gp = gp_ref[...]
        new = hs_ref[...] + _rms(m, gp)
        if loss is None:
            gn_ref, m_ref, o_ref, hn_ref = refs[4:]
            m_ref[...] = m
            o_ref[...] = new
            hn_ref[...] = _rms(new, gn_ref[...]).astype(BF16)
        else:
            t_ref, dhs_ref, dm_ref, dgp_ref, loss_ref = refs[4:]
            live = (i * t + lax.broadcasted_iota(jnp.int32, (t, 1), 0)) >= loss[1]
            diff = jnp.where(live, new - t_ref[...], 0.0)
            dhs = diff * (1.0 / d)
            dhs_ref[...] = dhs
            loss_ref[...] = jnp.full((8, 128), 0.5 / d * jnp.sum(diff * diff), F32)
            dm, dg = _rms_bwd(m, gp, dhs)
            dm_ref[...] = dm.astype(BF16)

            @pl.when(i == 0)
            def _():
                dgp_ref[...] = jnp.zeros_like(dgp_ref)
            dgp_ref[...] += dg

    common_in = [_row_spec(k, t), pl.BlockSpec((k, d), lambda i: (0, 0)), _row_spec(d, t), _vec_spec(d)]
    if loss is None:
        return _pallas(
            body, name=name, grid=(nt,), in_specs=common_in + [_vec_spec(d)],
            out_specs=[_row_spec(d, t)] * 3,
            out_shape=[jax.ShapeDtypeStruct((r, d), F32), jax.ShapeDtypeStruct((r, d), F32), jax.ShapeDtypeStruct((r, d), BF16)],
            compiler_params=_cparams(("parallel",)))(a, w, hs, g_post, g_pre)
    target, first_row = loss
    assert first_row % t == 0
    dhs, dm, dgp, parts = _pallas(
        body, name=name, grid=(nt,),
        in_specs=common_in + [pl.BlockSpec((t, d), lambda i: (jnp.maximum(i - first_row // t, 0), 0))],
        out_specs=[_row_spec(d, t), _row_spec(d, t), _vec_spec(d), pl.BlockSpec((8, 128), lambda i: (i, 0))],
        out_shape=[jax.ShapeDtypeStruct((r, d), F32), jax.ShapeDtypeStruct((r, d), BF16), jax.ShapeDtypeStruct((1, d), F32),
                   jax.ShapeDtypeStruct((nt * 8, 128), F32)],
        compiler_params=_cparams(("arbitrary",)))(a, w, hs, g_post, target)
    return dhs, dm, dgp, jnp.sum(parts[::8, 0])


def _norm_bwd(dhs, *, pre=None, post=None, pad=0, dm_dtype=BF16, dhs_cols=None, name):
    r = dhs.shape[0]
    d = dhs.shape[1] if dhs_cols is None else dhs_cols[0]
    has_pre, has_post = pre is not None, post is not None

    def body(*refs):
        it = iter(refs)
        dhs_ref = next(it)
        if has_pre:
            hs_ref, gn_ref, dhn_ref = next(it), next(it), next(it)
        if has_post:
            m_ref, gp_ref = next(it), next(it)
        if has_pre:
            o_dhs, o_dgn = next(it), next(it)
        if has_post:
            o_dm, o_dgp = next(it), next(it)
        i = pl.program_id(0)
        live = (i * ROW_TILE + lax.broadcasted_iota(jnp.int32, (ROW_TILE, 1), 0)) >= pad
        cur = jnp.where(live, dhs_ref[...], 0.0)
        if has_pre:
            dx, dg = _rms_bwd(hs_ref[...], gn_ref[...], jnp.where(live, dhn_ref[...].astype(F32), 0.0))
            cur = cur + dx
            o_dhs[...] = cur

            @pl.when(i == 0)
            def _():
                o_dgn[...] = jnp.zeros_like(o_dgn)
            o_dgn[...] += dg
        if has_post:
            dm, dg = _rms_bwd(m_ref[...], gp_ref[...], cur)
            o_dm[...] = dm.astype(o_dm.dtype)

            @pl.when(i == 0)
            def _():
                o_dgp[...] = jnp.zeros_like(o_dgp)
            o_dgp[...] += dg

    dhs_spec = _row_spec(d) if dhs_cols is None else pl.BlockSpec((ROW_TILE, d), lambda i: (i, dhs_cols[1]))
    ins, in_specs, out_specs, out_shape = [dhs], [dhs_spec], [], []
    if has_pre:
        ins += list(pre)
        in_specs += [_row_spec(d), _vec_spec(d), _row_spec(d)]
        out_specs += [_row_spec(d), _vec_spec(d)]
        out_shape += [jax.ShapeDtypeStruct((r, d), F32), jax.ShapeDtypeStruct((1, d), F32)]
    if has_post:
        ins += list(post)
        in_specs += [_row_spec(d), _vec_spec(d)]
        out_specs += [_row_spec(d), _vec_spec(d)]
        out_shape += [jax.ShapeDtypeStruct((r, d), dm_dtype), jax.ShapeDtypeStruct((1, d), F32)]
    outs = list(_pallas(body, name=name, grid=(r // ROW_TILE,), in_specs=in_specs, out_specs=out_specs,
                        out_shape=out_shape, compiler_params=_cparams(("arbitrary",)))(*ins))
    dhs_new, dgn = (outs.pop(0), outs.pop(0)) if has_pre else (dhs, None)
    dm, dgp = (outs.pop(0), outs.pop(0)) if has_post else (None, None)
    return dhs_new, dm, dgn, dgp


def _dgrad_norm_bwd(dy, w, dhs, hs, g_pre, *, post=None, add=None, pad=0, name):
    d, n = w.shape
    r = dy.shape[0]
    assert n <= MM_K_CAP and d == dhs.shape[1]
    t = ROW_TILE
    has_post, has_add = post is not None, add is not None
    dims = ((1,), (1,))

    def body(*refs):
        it = iter(refs)
        dy_ref, w_ref = next(it), next(it)
        add_ref = next(it) if has_add else None
        dhs_ref, hs_ref, gn_ref = next(it), next(it), next(it)
        if has_post:
            m_ref, gp_ref = next(it), next(it)
        o_dhs, o_dgn = next(it), next(it)
        if has_post:
            o_dm, o_dgp = next(it), next(it)
        i = pl.program_id(0)
        dhn = _dot(dy_ref[...].astype(BF16), w_ref[...].astype(BF16), dims)
        if has_add:
            dhn = dhn + add_ref[...]
        live = (i * t + lax.broadcasted_iota(jnp.int32, (t, 1), 0)) >= pad
        dx, dg = _rms_bwd(hs_ref[...], gn_ref[...], jnp.where(live, dhn, 0.0))
        cur = jnp.where(live, dhs_ref[...], 0.0) + dx
        o_dhs[...] = cur

        @pl.when(i == 0)
        def _():
            o_dgn[...] = jnp.zeros_like(o_dgn)
        o_dgn[...] += dg
        if has_post:
            dm, dg = _rms_bwd(m_ref[...], gp_ref[...], cur)
            o_dm[...] = dm.astype(BF16)

            @pl.when(i == 0)
            def _():
                o_dgp[...] = jnp.zeros_like(o_dgp)
            o_dgp[...] += dg

    ins = [dy, w] + ([add] if has_add else []) + [dhs, hs, g_pre] + (list(post) if has_post else [])
    in_specs = ([_row_spec(n, t), pl.BlockSpec((d, n), lambda i: (0, 0))] + ([_row_spec(d, t)] if has_add else [])
                + [_row_spec(d, t), _row_spec(d, t), _vec_spec(d)] + ([_row_spec(d, t), _vec_spec(d)] if has_post else []))
    out_specs = [_row_spec(d, t), _vec_spec(d)] + ([_row_spec(d, t), _vec_spec(d)] if has_post else [])
    out_shape = [jax.ShapeDtypeStruct((r, d), F32), jax.ShapeDtypeStruct((1, d), F32)]
    if has_post:
        out_shape += [jax.ShapeDtypeStruct((r, d), BF16), jax.ShapeDtypeStruct((1, d), F32)]
    outs = list(_pallas(body, name=name, grid=(r // t,), in_specs=in_specs, out_specs=out_specs,
                        out_shape=out_shape, compiler_params=_cparams(("arbitrary",)))(*ins))
    return (outs[0], outs[2], outs[1], outs[3]) if has_post else (outs[0], None, outs[1], None)


def _softplus(z):
    return jnp.maximum(z, 0.0) + jnp.log(1.0 + jnp.exp(-jnp.abs(z)))


def _sb_consts(t):
    row = lax.broadcasted_iota(jnp.int32, (t, t), 0)
    col = lax.broadcasted_iota(jnp.int32, (t, t), 1)
    m_up = (col >= row).astype(BF16)
    m_low = (col <= row).astype(BF16)
    return m_up, m_low


def _emit_chains(chains, stages, skew):
    if skew:
        for step in range(len(chains) + len(stages) - 1):
            for si, stage in enumerate(stages):
                if 0 <= step - si < len(chains):
                    stage(chains[step - si])
    else:
        for stage in stages:
            for c in chains:
                stage(c)


def _sb_fwd(q, k, vt3, pad, name, ride=((), ())):
    r = q.shape[0]
    t = ATT_BLK
    nb = r // t
    nbp = -(-(nb + 1) // 8) * 8
    jmin = pad // t
    scale = SB_HEAD_DIM ** -0.5
    n_ride = len(ride[0])

    def body(q_ref, k_ref, vt_ref, *rest):
        ride_in, (o_ref, ss_ref), ride_out = rest[:n_ride], rest[n_ride:n_ride + 2], rest[n_ride + 2:2 * n_ride + 2]
        acc_ref, kn_ref = rest[2 * n_ride + 2:2 * n_ride + 4]
        ride_sems = rest[2 * n_ride + 4:]
        i = pl.program_id(1)
        if n_ride:
            @pl.when((pl.program_id(0) == 0) & (i == 0))
            def _():
                for cp in _exchange_copies(ride_in, ride_out, ride[1], *ride_sems):
                    cp.start()

        @pl.when(i == 0)
        def _():
            def blk(b, m):
                kb = k_ref[pl.ds(pl.multiple_of(b * t, t), t), :].astype(F32)
                return jnp.maximum(m, jnp.max(jnp.sum(kb * kb, axis=1, keepdims=True), axis=0, keepdims=True))
            kn_ref[...] = jnp.broadcast_to(lax.fori_loop(0, nb, blk, jnp.zeros((1, 1), F32)), (8, 128))

        qf = q_ref[...].astype(F32)
        z_bound = scale * jnp.sqrt(jnp.max(jnp.sum(qf * qf, axis=1, keepdims=True)) * jnp.max(kn_ref[...]))

        def need(carry):
            return jnp.maximum(jnp.max(carry[0]), jnp.max(carry[1])) + z_bound >= SB_LOG_ZERO

        qt = qf.T
        sub = lax.broadcasted_iota(jnp.int32, (128, 1), 0)
        m_up, _ = _sb_consts(t)
        kpos0 = lax.broadcasted_iota(jnp.int32, (t, 1), 0)
        qpos = i * t + lax.broadcasted_iota(jnp.int32, (1, t), 1)
        qths = [jnp.where((sub >= 64 * h) & (sub < 64 * (h + 1)), qt * scale, 0.0).astype(BF16) for h in range(2)]
        acc_ref[...] = jnp.zeros_like(acc_ref)

        def sweep(js, carry, masked):
            kbs = [k_ref[pl.ds(pl.multiple_of(j * t, t), t), :] for j in js]
            vts = [vt_ref[0, j] for j in js]
            accs = [acc_ref[0], acc_ref[1]]
            s = list(carry)
            chains = [(n, h) for n in range(len(js)) for h in range(2)]
            masked = [masked] * len(js) if isinstance(masked, bool) else masked
            valid = [(js[n] * t + kpos0 < qpos) & (js[n] * t + kpos0 >= pad) if masked[n] else None for n in range(len(js))]
            zt, inc, saves = {}, {}, []

            def st_scores(c):
                zt[c] = _dot(kbs[c[0]], qths[c[1]])

            def st_cumsum(c):
                lk = -_softplus(zt[c])
                if masked[c[0]]:
                    lk = jnp.where(valid[c[0]], lk, 0.0)
                inc[c] = _split_dot(m_up, lk)

            def st_weights(c):
                n, h = c
                saves.append((h, js[n], s[h]))
                w = jnp.exp(zt[c] + inc[c] + s[h])
                if masked[n]:
                    w = jnp.where(valid[n], w, 0.0)
                accs[h] = accs[h] + _dot(vts[n], w.astype(BF16))
                s[h] = s[h] + inc[c][0:1, :]

            _emit_chains(chains, [st_scores, st_cumsum, st_weights], SB_FWD_SKEW)
            for h, j, val in saves:
                ss_ref[h, 0, pl.ds(j, 1), :] = val
            acc_ref[0] = accs[0]
            acc_ref[1] = accs[1]
            return tuple(s)

        zero = jnp.zeros((1, t), F32)
        bpi = SB_BLOCKS_PER_TRIP
        j, carry = lax.cond(
            i - 1 > jmin,
            lambda: (i - 2, sweep([i, i - 1], (zero, zero), [True, False])),
            lambda: (i - 1, sweep([i], (zero, zero), True)))
        def further(j, carry):
            j, carry = lax.while_loop(
                lambda st: (st[0] - bpi >= jmin) & need(st[1]),
                lambda st: (st[0] - bpi, sweep([st[0] - b for b in range(bpi)], st[1], False)), (j, carry))
            j, carry = lax.while_loop(
                lambda st: (st[0] > jmin) & need(st[1]),
                lambda st: (st[0] - 1, sweep([st[0]], st[1], False)), (j, carry))
            return lax.while_loop(
                lambda st: (st[0] == jmin) & (i > jmin) & need(st[1]),
                lambda st: (st[0] - 1, sweep([st[0]], st[1], True)), (j, carry))[0]

        j = lax.cond((j >= jmin) & need(carry), lambda: further(j, carry), lambda: j)
        first = jnp.full((1, t), j + 1, jnp.int32).astype(F32)
        ss_ref[0, 0, nbp - 1:nbp, :] = first
        ss_ref[1, 0, nbp - 1:nbp, :] = first
        acc = jnp.where(sub < 64, acc_ref[0], acc_ref[1])
        o_ref[...] = acc.T
        if n_ride:
            @pl.when((pl.program_id(0) == 3) & (i == nb - 1))
            def _():
                for cp in _exchange_copies(ride_in, ride_out, ride[1], *ride_sems):
                    cp.wait()

    hbm = pl.BlockSpec(memory_space=pl.ANY)
    outs = _pallas(
        body, name=name, grid=(4, nb),
        in_specs=[pl.BlockSpec((t, 128), lambda hp, i: (i, hp)),
                  pl.BlockSpec((r, 128), lambda hp, i: (0, hp)),
                  pl.BlockSpec((1, nb, 128, t), lambda hp, i: (hp, 0, 0, 0))] + [hbm] * n_ride,
        out_specs=[pl.BlockSpec((t, 128), lambda hp, i: (i, hp)),
                   pl.BlockSpec((2, 1, nbp, t), lambda hp, i: (hp, i, 0, 0))] + [hbm] * n_ride,
        out_shape=[jax.ShapeDtypeStruct((r, SB_WIDTH), F32),
                   jax.ShapeDtypeStruct((8, nb, nbp, t), F32)] + _exchange_shapes(*ride),
        scratch_shapes=[pltpu.VMEM((2, 128, t), F32), pltpu.VMEM((8, 128), F32)] + (_exchange_sems(n_ride) if n_ride else []),
        compiler_params=_cparams(("arbitrary", "arbitrary")),
    )(q, k, vt3, *ride[0])
    return outs[0], outs[1], list(outs[2:])


def _sb_bwd(q, k, v, kt3, ssave, do, pad, name):
    r = q.shape[0]
    t = ATT_BLK
    nb = r // t
    nbp = ssave.shape[2]
    jmin = pad // t
    scale = SB_HEAD_DIM ** -0.5

    def body(q_ref, do_ref, k_ref, v_ref, kt_ref, ss_ref, dq_ref, dk_hbm, dv_hbm, dk_acc, dv_acc, dq_acc, sem):
        hp = pl.program_id(0)
        i = pl.program_id(1)

        @pl.when(i == 0)
        def _():
            dk_acc[...] = jnp.zeros_like(dk_acc)
            dv_acc[...] = jnp.zeros_like(dv_acc)

        qf = q_ref[...].astype(F32)
        dof = do_ref[...]
        qt = qf.T
        dot_ = dof.T
        sub = lax.broadcasted_iota(jnp.int32, (128, 1), 0)
        lane = lax.broadcasted_iota(jnp.int32, (1, 128), 1)
        m_up, m_low = _sb_consts(t)
        kpos0 = lax.broadcasted_iota(jnp.int32, (t, 1), 0)
        qpos = i * t + lax.broadcasted_iota(jnp.int32, (1, t), 1)
        first = jnp.clip(jnp.max(ss_ref[0, 0, nbp - 1:nbp, :]).astype(jnp.int32), jmin, i)
        mid0 = jnp.maximum(first, jmin + 1)
        pair = i - mid0 >= 1
        n_mid = jnp.maximum(i - mid0 - 1, 0)
        n_edge = jnp.where((i > jmin) & (first == jmin), 1, 0)
        in_t = [(sub >= 64 * h) & (sub < 64 * (h + 1)) for h in range(2)]
        in_l = [(lane >= 64 * h) & (lane < 64 * (h + 1)) for h in range(2)]
        qths = [jnp.where(in_t[h], qt * scale, 0.0).astype(BF16) for h in range(2)]
        doths = [jnp.where(in_t[h], dot_, 0.0).astype(BF16) for h in range(2)]
        qhs = [jnp.where(in_l[h], qf * scale, 0.0).astype(BF16) for h in range(2)]
        dohs = [jnp.where(in_l[h], dof, 0.0).astype(BF16) for h in range(2)]
        dq_acc[...] = jnp.zeros_like(dq_acc)

        def sweep(js, carry, masked):
            rows = [pl.ds(pl.multiple_of(j * t, t), t) for j in js]
            kbs = [k_ref[rw, :] for rw in rows]
            vbs = [v_ref[rw, :] for rw in rows]
            kts = [kt_ref[0, j] for j in js]
            sss = [[ss_ref[h, 0, pl.ds(j, 1), :] for h in range(2)] for j in js]
            dv_old = [dv_acc[rw, :] for rw in rows]
            dk_old = [dk_acc[rw, :] for rw in rows]
            dqs = [dq_acc[0], dq_acc[1]]
            ec = list(carry)
            chains = [(n, h) for n in range(len(js)) for h in range(2)]
            masked = [masked] * len(js) if isinstance(masked, bool) else masked
            valid = [(js[n] * t + kpos0 < qpos) & (js[n] * t + kpos0 >= pad) if masked[n] else None for n in range(len(js))]
            zt, dvt, sp, inc, e, big_e = {}, {}, {}, {}, {}, {}

            def st_scores(c):
                zt[c] = _dot(kbs[c[0]], qths[c[1]])
                dvt[c] = _dot(vbs[c[0]], doths[c[1]])

            def st_cumsum(c):
                sp[c] = _softplus(zt[c])
                lk = -sp[c]
                if masked[c[0]]:
                    lk = jnp.where(valid[c[0]], lk, 0.0)
                inc[c] = _split_dot(m_up, lk)

            def st_weights(c):
                n, h = c
                w = jnp.exp(zt[c] + inc[c] + sss[n][h])
                if masked[n]:
                    w = jnp.where(valid[n], w, 0.0)
                dv_old[n] = dv_old[n] + _dot(w.astype(BF16), dohs[h])
                e[c] = w * dvt[c]
                pinc = _split_dot(m_low, e[c])
                big_e[c] = pinc - e[c] + ec[h]
                ec[h] = ec[h] + pinc[t - 1:t, :]

            def st_dscores(c):
                n, h = c
                dz = e[c] - jnp.exp(zt[c] - sp[c]) * (e[c] + big_e[c])
                if masked[n]:
                    dz = jnp.where(valid[n], dz, 0.0)
                dzb = dz.astype(BF16)
                dqs[h] = dqs[h] + _dot(kts[n], dzb)
                dk_old[n] = dk_old[n] + _dot(dzb, qhs[h])

            _emit_chains(chains, [st_scores, st_cumsum, st_weights, st_dscores], SB_BWD_SKEW)
            for n, rw in enumerate(rows):
                dv_acc[rw, :] = dv_old[n]
                dk_acc[rw, :] = dk_old[n]
            dq_acc[0] = dqs[0]
            dq_acc[1] = dqs[1]
            return tuple(ec)

        zero = jnp.zeros((1, t), F32)
        bpi = SB_BLOCKS_PER_TRIP
        carry = lax.fori_loop(0, n_edge, lambda it, c: sweep([jmin + it * 0], c, True), (zero, zero))
        carry = lax.fori_loop(0, n_mid // bpi, lambda it, c: sweep([mid0 + bpi * it + b for b in range(bpi)], c, False), carry)
        n_rem = n_mid % bpi
        carry = lax.fori_loop(0, n_rem, lambda it, c: sweep([i - 1 - n_rem + it], c, False), carry)
        lax.cond(pair, lambda: sweep([i - 1, i], carry, [False, True]), lambda: sweep([i], carry, True))
        dq_ref[...] = (jnp.where(sub < 64, dq_acc[0], dq_acc[1]) * scale).T

        @pl.when(i == nb - 1)
        def _():
            lanes = pl.ds(pl.multiple_of(hp * 128, 128), 128)
            c1 = pltpu.make_async_copy(dk_acc, dk_hbm.at[:, lanes], sem.at[0])
            c2 = pltpu.make_async_copy(dv_acc, dv_hbm.at[:, lanes], sem.at[1])
            c1.start()
            c2.start()
            c1.wait()
            c2.wait()

    return _pallas(
        body, name=name, grid=(4, nb),
        in_specs=[pl.BlockSpec((t, 128), lambda hp, i: (i, hp)),
                  pl.BlockSpec((t, 128), lambda hp, i: (i, hp)),
                  pl.BlockSpec((r, 128), lambda hp, i: (0, hp)),
                  pl.BlockSpec((r, 128), lambda hp, i: (0, hp)),
                  pl.BlockSpec((1, nb, 128, t), lambda hp, i: (hp, 0, 0, 0)),
                  pl.BlockSpec((2, 1, nbp, t), lambda hp, i: (hp, i, 0, 0))],
        out_specs=[pl.BlockSpec((t, 128), lambda hp, i: (i, hp)),
                   pl.BlockSpec(memory_space=pl.ANY), pl.BlockSpec(memory_space=pl.ANY)],
        out_shape=[jax.ShapeDtypeStruct((r, SB_WIDTH), F32),
                   jax.ShapeDtypeStruct((r, SB_WIDTH), F32), jax.ShapeDtypeStruct((r, SB_WIDTH), F32)],
        scratch_shapes=[pltpu.VMEM((r, 128), F32), pltpu.VMEM((r, 128), F32), pltpu.VMEM((2, 128, t), F32),
                        pltpu.SemaphoreType.DMA((2,))],
        compiler_params=_cparams(("arbitrary", "arbitrary")),
    )(q, do, k, v, kt3, ssave)


def _s5_disc(lam_re, lam_im, logdt, btr, bti):
    lr = jnp.minimum(lam_re, -1e-4)
    li = lam_im
    dt = jnp.exp(logdt)
    mag = jnp.exp(lr * dt)
    ang = li * dt
    a_re, a_im = mag * jnp.cos(ang), mag * jnp.sin(ang)
    den = lr * lr + li * li
    nr, ni = a_re - 1.0, a_im
    c_re = (nr * lr + ni * li) / den
    c_im = (ni * lr - nr * li) / den
    return a_re, a_im, c_re * btr - c_im * bti, c_re * bti + c_im * btr


def _s5_prep(lam_re, lam_im, logdt, btr, bti, name):
    ns = lam_re.shape[1]

    def body(lr_ref, li_ref, dt_ref, br_ref, bi_ref, ar_ref, ai_ref, bbr_ref, bbi_ref):
        ar, ai, bbr, bbi = _s5_disc(lr_ref[...], li_ref[...], dt_ref[...], br_ref[...], bi_ref[...])
        ar_ref[...] = ar
        ai_ref[...] = ai
        bbr_ref[...] = bbr
        bbi_ref[...] = bbi

    return _pallas(body, name=name,
                   out_shape=[jax.ShapeDtypeStruct((1, ns), F32)] * 2 + [jax.ShapeDtypeStruct((S5_GROUP, ns), F32)] * 2,
                   )(lam_re, lam_im, logdt, btr, bti)


def _s5_prep_bwd(lam_re, lam_im, logdt, btr, bti, dar, dai, dbbr, dbbi, name):
    ns = lam_re.shape[1]

    def body(lr_ref, li_ref, dt_ref, br_ref, bi_ref, dar_ref, dai_ref, dbr_ref, dbi_ref, o_lr, o_li, o_dt, o_br, o_bi):
        _, vjp = jax.vjp(_s5_disc, lr_ref[...], li_ref[...], dt_ref[...], br_ref[...], bi_ref[...])
        g = vjp((dar_ref[...], dai_ref[...], dbr_ref[...], dbi_ref[...]))
        o_lr[...] = g[0]
        o_li[...] = g[1]
        row = lax.broadcasted_iota(jnp.int32, (ns, ns), 0) // S5_STATE
        col = lax.broadcasted_iota(jnp.int32, (ns, ns), 1) // S5_STATE
        same = (row == col).astype(F32)
        o_dt[...] = _dot_hi(jnp.broadcast_to(g[2], (8, ns)), same)[0:1]
        o_br[...] = g[3]
        o_bi[...] = g[4]

    return _pallas(body, name=name,
                   out_shape=[jax.ShapeDtypeStruct((1, ns), F32)] * 3 + [jax.ShapeDtypeStruct((S5_GROUP, ns), F32)] * 2,
                   compiler_params=pltpu.CompilerParams(vmem_limit_bytes=VMEM_LIMIT),
                   )(lam_re, lam_im, logdt, btr, bti, dar, dai, dbbr, dbbi)


def _s5_scan(br, bi, ar, ai, t, reverse=False, carry=None):
    ng = t // 8
    ns = br.shape[1]
    br, bi = br.reshape(ng, 8, ns), bi.reshape(ng, 8, ns)
    row8 = lax.broadcasted_iota(jnp.int32, (1, 8, 1), 1)
    pr, pi_ = ar, ai
    for k in (1, 2, 4):
        if reverse:
            sr, si, ok = pltpu.roll(br, 8 - k, 1), pltpu.roll(bi, 8 - k, 1), row8 < 8 - k
        else:
            sr, si, ok = pltpu.roll(br, k, 1), pltpu.roll(bi, k, 1), row8 >= k
        sr = jnp.where(ok, sr, 0.0)
        si = jnp.where(ok, si, 0.0)
        br, bi = br + pr * sr - pi_ * si, bi + pr * si + pi_ * sr
        pr, pi_ = pr * pr - pi_ * pi_, 2.0 * pr * pi_
    pw_r, pw_i = [ar], [ai]
    for _ in range(7):
        pw_r.append(pw_r[-1] * ar - pw_i[-1] * ai)
        pw_i.append(pw_r[-2] * ai + pw_i[-1] * ar)
    if reverse:
        pw_r.reverse()
        pw_i.reverse()
    p8r, p8i = jnp.concatenate(pw_r, axis=0), jnp.concatenate(pw_i, axis=0)
    out_r, out_i = [None] * ng, [None] * ng
    order = range(ng - 1, -1, -1) if reverse else range(ng)
    edge = 0 if reverse else 7
    for g in order:
        gr, gi = br[g], bi[g]
        if carry is not None:
            cr, ci = carry
            gr, gi = gr + p8r * cr - p8i * ci, gi + p8r * ci + p8i * cr
        out_r[g], out_i[g] = gr, gi
        carry = (gr[edge:edge + 1], gi[edge:edge + 1])
    return jnp.concatenate(out_r, axis=0), jnp.concatenate(out_i, axis=0)


def _s5_prev_rows(x, first, t):
    ng = t // 8
    ns = x.shape[1]
    x3 = x.reshape(ng, 8, ns)
    last = x3[:, 7:8, :]
    before = jnp.concatenate([first.reshape(1, 1, ns), last[:ng - 1]], axis=0)
    row8 = lax.broadcasted_iota(jnp.int32, (1, 8, 1), 1)
    return jnp.where(row8 == 0, before, pltpu.roll(x3, 1, 1)).reshape(t, ns)


_GELU_C = math.sqrt(2.0 / math.pi)


def _gelu(y):
    th = jnp.tanh(_GELU_C * (y + 0.044715 * y * y * y))
    return 0.5 * y * (1.0 + th), th


def _sigmoid(x):
    return 1.0 / (1.0 + jnp.exp(-x))


def _s5_fwd(u, wb, a, wc, dskip, wglu, bglu, gnorm, name):
    r = u.shape[0]
    t = S5_TILE
    nt = r // t
    ns = wb.shape[2]
    w = S5_WIDTH

    def body(u_ref, wb_ref, a_ref, wc_ref, d_ref, wg_ref, bg_ref, gn_ref, y_ref, on_ref, xs_ref, carry_ref):
        i = pl.program_id(0)
        ar, ai = a_ref[0], a_ref[1]

        @pl.when(i == 0)
        def _():
            carry_ref[...] = jnp.zeros_like(carry_ref)

        u_ = u_ref[...]
        ub = u_.astype(BF16)
        xs_ref[0] = carry_ref[:, 0, :]
        chunks = list(range(S5_CHUNKS))
        sl_s = [slice(c * (ns // S5_CHUNKS), (c + 1) * (ns // S5_CHUNKS)) for c in chunks]
        sl_u = [slice(c * (w // S5_CHUNKS), (c + 1) * (w // S5_CHUNKS)) for c in chunks]
        bu, xs, ys = {}, {}, {}

        def st_inputs(c):
            bu[c] = (_dot(ub[:, sl_u[c]], wb_ref[0, sl_u[c], sl_s[c]]), _dot(ub[:, sl_u[c]], wb_ref[1, sl_u[c], sl_s[c]]))

        def st_scan(c):
            xr, xi = _s5_scan(*bu[c], ar[:, sl_s[c]], ai[:, sl_s[c]], t, carry=(carry_ref[0, :, sl_s[c]], carry_ref[1, :, sl_s[c]]))
            carry_ref[0, :, sl_s[c]] = xr[t - 1:t, :]
            carry_ref[1, :, sl_s[c]] = xi[t - 1:t, :]
            xs[c] = (xr.astype(BF16), xi.astype(BF16))

        def st_outputs(c):
            ys[c] = _dot(xs[c][0], wc_ref[0, sl_s[c], sl_u[c]]) - _dot(xs[c][1], wc_ref[1, sl_s[c], sl_u[c]])

        _emit_chains(chunks, [st_inputs, st_scan, st_outputs], False)
        y = jnp.concatenate([ys[c] for c in chunks], axis=1) + d_ref[...] * u_
        h, _ = _gelu(y)
        gate = _sigmoid(_dot(h.astype(BF16), wg_ref[...]) + bg_ref[...])
        y_ref[...] = y
        on_ref[...] = _rms(h * gate, gn_ref[...]).astype(BF16)

    full = lambda shape: pl.BlockSpec(shape, lambda i: (0,) * len(shape))
    return _pallas(
        body, name=name, grid=(nt,),
        in_specs=[_row_spec(w, t), full((2, w, ns)), full((2, 1, ns)), full((2, ns, w)), full((1, w)),
                  full((w, w)), full((1, w)), full((1, w))],
        out_specs=[_row_spec(w, t), pl.BlockSpec((t, w), lambda i: (i, 1)), pl.BlockSpec((1, 2, ns), lambda i: (i, 0, 0))],
        out_shape=[jax.ShapeDtypeStruct((r, w), F32), jax.ShapeDtypeStruct((r, 2 * w), BF16),
                   jax.ShapeDtypeStruct((nt, 2, ns), F32)],
        scratch_shapes=[pltpu.VMEM((2, 1, ns), F32)],
        compiler_params=_cparams(("arbitrary",)),
    )(u, wb, a, wc, dskip, wglu, bglu, gnorm)


def _s5_bwd(u, y, don, xstart, wb, a, wc, dskip, wglu, bglu, gnorm, name, ride=((), ()), don_block=0):
    r = u.shape[0]
    t = S5_TILE
    nt = r // t
    ns = wb.shape[2]
    w = S5_WIDTH
    nt_dims = ((1,), (1,))
    tn_dims = ((0,), (0,))

    def body(u_ref, y_ref, don_ref, xs_ref, wb_hbm, a_ref, wc_hbm, d_ref, wg_ref, bg_ref, gn_ref,
             du_ref, da_ref, dd_ref, dbg_ref, dgn_ref, dwb_hbm, dwc_hbm, dwg_hbm,
             wb_ref, wc_ref, lam_ref, acc_wb, acc_wc, acc_wg, sem):
        i = pl.program_id(0)
        ar, ai = a_ref[0], a_ref[1]

        @pl.when(i == 0)
        def _():
            c1 = pltpu.make_async_copy(wb_hbm, wb_ref, sem.at[0])
            c2 = pltpu.make_async_copy(wc_hbm, wc_ref, sem.at[1])
            c1.start()
            c2.start()
            lam_ref[...] = jnp.zeros_like(lam_ref)
            acc_wb[...] = jnp.zeros_like(acc_wb)
            acc_wc[...] = jnp.zeros_like(acc_wc)
            acc_wg[...] = jnp.zeros_like(acc_wg)
            da_ref[...] = jnp.zeros_like(da_ref)
            dd_ref[...] = jnp.zeros_like(dd_ref)
            dbg_ref[...] = jnp.zeros_like(dbg_ref)
            dgn_ref[...] = jnp.zeros_like(dgn_ref)
            c1.wait()
            c2.wait()

        u_ = u_ref[...]
        y_ = y_ref[...]
        ub = u_.astype(BF16)
        h, th = _gelu(y_)
        hb = h.astype(BF16)
        wg = wg_ref[...]
        gate = _sigmoid(_dot(hb, wg) + bg_ref[...])
        d_out, dgn = _rms_bwd(h * gate, gn_ref[...], don_ref[...])
        dgn_ref[...] += dgn
        dhw = d_out * h * gate * (1.0 - gate)
        dhwb = dhw.astype(BF16)
        dh = d_out * gate + _dot(dhwb, wg, nt_dims)
        acc_wg[...] += _dot(hb, dhwb, tn_dims)
        dbg_ref[...] += jnp.sum(dhw, axis=0, keepdims=True)
        dgelu = 0.5 * (1.0 + th) + 0.5 * y_ * (1.0 - th * th) * _GELU_C * (1.0 + 3.0 * 0.044715 * y_ * y_)
        dy = dh * dgelu
        dd_ref[...] += jnp.sum(dy * u_, axis=0, keepdims=True)
        dyb = dy.astype(BF16)
        chunks = list(range(S5_CHUNKS))
        sl_s = [slice(c * (ns // S5_CHUNKS), (c + 1) * (ns // S5_CHUNKS)) for c in chunks]
        sl_u = [slice(c * (w // S5_CHUNKS), (c + 1) * (w // S5_CHUNKS)) for c in chunks]
        bu, gx, x_, lam, dus = {}, {}, {}, {}, {}

        def st_inputs(c):
            su, ss = sl_u[c], sl_s[c]
            bu[c] = (_dot(ub[:, su], wb_ref[0, su, ss]), _dot(ub[:, su], wb_ref[1, su, ss]))
            gx[c] = (_dot(dyb[:, su], wc_ref[0, ss, su], nt_dims), -_dot(dyb[:, su], wc_ref[1, ss, su], nt_dims))

        def st_states(c):
            su, ss = sl_u[c], sl_s[c]
            first = (xs_ref[0, 0:1, ss], xs_ref[0, 1:2, ss])
            xr, xi = _s5_scan(*bu[c], ar[:, ss], ai[:, ss], t, carry=first)
            acc_wc[0, ss, su] += _dot(xr.astype(BF16), dyb[:, su], tn_dims)
            acc_wc[1, ss, su] -= _dot(xi.astype(BF16), dyb[:, su], tn_dims)
            x_[c] = (_s5_prev_rows(xr, first[0], t), _s5_prev_rows(xi, first[1], t))

        def st_adjoint(c):
            su, ss = sl_u[c], sl_s[c]
            lr, li = _s5_scan(*gx[c], ar[:, ss], -ai[:, ss], t, reverse=True, carry=(lam_ref[0, :, ss], lam_ref[1, :, ss]))
            lam_ref[0, :, ss] = lr[0:1, :]
            lam_ref[1, :, ss] = li[0:1, :]
            lrb, lib = lr.astype(BF16), li.astype(BF16)
            acc_wb[0, su, ss] += _dot(ub[:, su], lrb, tn_dims)
            acc_wb[1, su, ss] += _dot(ub[:, su], lib, tn_dims)
            dus[c] = _dot(lrb, wb_ref[0, su, ss], nt_dims) + _dot(lib, wb_ref[1, su, ss], nt_dims)
            lam[c] = (lr, li)

        def st_decay(c):
            ss = sl_s[c]
            (lr, li), (xpr, xpi) = lam[c], x_[c]
            da_ref[0, :, ss] += jnp.sum(lr * xpr + li * xpi, axis=0, keepdims=True)
            da_ref[1, :, ss] += jnp.sum(li * xpr - lr * xpi, axis=0, keepdims=True)

        _emit_chains(chunks, [st_inputs, st_states, st_adjoint, st_decay], False)
        du_ref[...] = d_ref[...] * dy + jnp.concatenate([dus[c] for c in chunks], axis=1)

        @pl.when(i == nt - 1)
        def _():
            cps = [pltpu.make_async_copy(acc_wb, dwb_hbm, sem.at[0]), pltpu.make_async_copy(acc_wc, dwc_hbm, sem.at[1]),
                   pltpu.make_async_copy(acc_wg, dwg_hbm, sem.at[2])]
            for c in cps:
                c.start()
            for c in cps:
                c.wait()

    n_ride = len(ride[0])
    n_in, n_out, n_scratch = 11, 8, 7

    def body_with_ride(*refs):
        ins, rest = refs[:n_in], refs[n_in:]
        ride_in, rest = rest[:n_ride], rest[n_ride:]
        outs, rest = rest[:n_out], rest[n_out:]
        ride_out, rest = rest[:n_ride], rest[n_ride:]
        scratch, ride_sems = rest[:n_scratch], rest[n_scratch:]
        if n_ride:
            @pl.when(pl.program_id(0) == 0)
            def _():
                for cp in _exchange_copies(ride_in, ride_out, ride[1], *ride_sems):
                    cp.start()
        body(*ins, *outs, *scratch)
        if n_ride:
            @pl.when(pl.program_id(0) == nt - 1)
            def _():
                for cp in _exchange_copies(ride_in, ride_out, ride[1], *ride_sems):
                    cp.wait()

    rev = lambda i: (nt - 1 - i, 0)
    full = lambda shape: pl.BlockSpec(shape, lambda i: (0,) * len(shape))
    hbm = pl.BlockSpec(memory_space=pl.ANY)
    outs = _pallas(
        body_with_ride, name=name, grid=(nt,),
        in_specs=[pl.BlockSpec((t, w), rev), pl.BlockSpec((t, w), rev), pl.BlockSpec((t, w), lambda i: (nt - 1 - i, don_block)),
                  pl.BlockSpec((1, 2, ns), lambda i: (nt - 1 - i, 0, 0)), hbm, full((2, 1, ns)), hbm, full((1, w)),
                  full((w, w)), full((1, w)), full((1, w))] + [hbm] * n_ride,
        out_specs=[pl.BlockSpec((t, w), rev), full((2, 1, ns)), full((1, w)), full((1, w)), full((1, w)), hbm, hbm, hbm]
        + [hbm] * n_ride,
        out_shape=[jax.ShapeDtypeStruct((r, w), F32), jax.ShapeDtypeStruct((2, 1, ns), F32)]
        + [jax.ShapeDtypeStruct((1, w), F32)] * 3
        + [jax.ShapeDtypeStruct((2, w, ns), F32), jax.ShapeDtypeStruct((2, ns, w), F32), jax.ShapeDtypeStruct((w, w), F32)]
        + _exchange_shapes(*ride),
        scratch_shapes=[pltpu.VMEM((2, w, ns), BF16), pltpu.VMEM((2, ns, w), BF16), pltpu.VMEM((2, 1, ns), F32),
                        pltpu.VMEM((2, w, ns), F32), pltpu.VMEM((2, ns, w), F32), pltpu.VMEM((w, w), F32),
                        pltpu.SemaphoreType.DMA((3,))] + (_exchange_sems(n_ride) if n_ride else []),
        compiler_params=_cparams(("arbitrary",)),
    )(u, y, don, xstart, wb, a, wc, dskip, wglu, bglu, gnorm, *ride[0])
    return tuple(outs[:n_out]) + (list(outs[n_out:]),)


def _s5_expand(lam_re, lam_im, log_dt, b_re, b_im, c_re, c_im):
    g, n, p = S5_GROUPS, S5_STATE, S5_GROUP
    ns = g * n
    rows = lambda x: x.reshape(1, ns)
    logdt = jnp.repeat(log_dt.reshape(g), n).reshape(1, ns)
    btr = b_re.reshape(ns, p).T
    bti = b_im.reshape(ns, p).T
    ctr = c_re.transpose(0, 2, 1).reshape(ns, p)
    cti = c_im.transpose(0, 2, 1).reshape(ns, p)
    mask = (jnp.arange(g * p)[:, None] // p) == (jnp.arange(ns)[None, :] // n)
    return rows(lam_re), rows(lam_im), logdt, btr, bti, ctr, cti, mask


def _s5_block_diag_b(bb, mask):
    return jnp.where(mask, jnp.tile(bb, (S5_GROUPS, 1)), 0.0)


def _s5_block_diag_c(ct, mask):
    return jnp.where(mask.T, jnp.tile(ct, (1, S5_GROUPS)), 0.0)


def _s5_diag_of_b(dwb, mask):
    return jnp.where(mask, dwb, 0.0).reshape(S5_GROUPS, S5_GROUP, -1).sum(0)


def _s5_diag_of_c(dwc, mask):
    ns = dwc.shape[0]
    return jnp.where(mask.T, dwc, 0.0).reshape(ns, S5_GROUPS, S5_GROUP).sum(1)


DN_PRE_TILE = 256
_DN_QKV = 3 * DN_WIDTH


def _halo_specs(width, tile, nt, prev):
    per = tile // 8
    if prev:
        return pl.BlockSpec((8, width), lambda i: (jnp.maximum(i * per - 1, 0), 0))
    return pl.BlockSpec((8, width), lambda i: (jnp.minimum((i + 1) * per, nt * per - 1), 0))


def _shift_down(x, halo, s, t):
    xx = jnp.concatenate([halo, x], axis=0)
    return pltpu.roll(xx, s, 0)[8:]


def _shift_up(x, halo, s, t):
    xx = jnp.concatenate([x, halo], axis=0)
    return pltpu.roll(xx, t + 8 - s, 0)[:t]


def _silu(x):
    s = _sigmoid(x)
    return x * s, s


def _dn_gates(ab, alog, dtb, live):
    lane = lax.broadcasted_iota(jnp.int32, (1, 128), 1)
    g = -jnp.exp(alog) * _softplus(ab + dtb)
    beta = _sigmoid(ab)
    return jnp.where(live & (lane < DN_HEADS), g, jnp.where(live & (lane < 2 * DN_HEADS), beta, 0.0))


def _dn_pre_fwd(proj, ab, conv_w, alog, dtb, pad, name):
    r = proj.shape[0]
    t = DN_PRE_TILE
    nt = r // t
    scale = DN_HEAD_DIM ** -0.5

    def body(x_ref, halo_ref, ab_ref, w_ref, al_ref, dt_ref, q_ref, k_ref, v_ref, gb_ref):
        i = pl.program_id(0)
        act, _ = _silu(_dn_conv(x_ref[...], jnp.where(i > 0, halo_ref[...], 0.0), w_ref[...], t))
        for hd in range(DN_HEADS):
            sl = slice(hd * 128, (hd + 1) * 128)
            for base, o_ref, sc in ((0, q_ref, scale), (DN_WIDTH, k_ref, 1.0)):
                xh = act[:, base + hd * 128: base + (hd + 1) * 128]
                o_ref[:, sl] = (xh * (lax.rsqrt(jnp.sum(xh * xh, axis=-1, keepdims=True) + EPS) * sc)).astype(BF16)
        v_ref[...] = act[:, 2 * DN_WIDTH:].astype(BF16)
        rows = i * t + lax.broadcasted_iota(jnp.int32, (t, 1), 0)
        gb_ref[...] = _dn_gates(ab_ref[...], al_ref[...], dt_ref[...], rows >= pad)

    return _pallas(
        body, name=name, grid=(nt,),
        in_specs=[pl.BlockSpec((t, _DN_QKV), lambda i: (i, 0)), _halo_specs(_DN_QKV, t, nt, True), _row_spec(128, t),
                  pl.BlockSpec((DN_CONV, _DN_QKV), lambda i: (0, 0)), _vec_spec(128), _vec_spec(128)],
        out_specs=[_row_spec(DN_WIDTH, t), _row_spec(DN_WIDTH, t), _row_spec(DN_WIDTH, t), _row_spec(128, t)],
        out_shape=[jax.ShapeDtypeStruct((r, DN_WIDTH), BF16)] * 3 + [jax.ShapeDtypeStruct((r, 128), F32)],
        compiler_params=_cparams(("parallel",)),
    )(proj, proj, ab, conv_w, alog, dtb)


def _dn_conv(x, halo, w, t):
    co = w[DN_CONV - 1:DN_CONV] * x
    for tap in range(DN_CONV - 1):
        co = co + w[tap:tap + 1] * _shift_down(x, halo, DN_CONV - 1 - tap, t)
    return co


def _dn_pre_bwd(proj, conv_w, dq, dk, dv, dgb, ab, alog, dtb, pad, name):
    r = proj.shape[0]
    t = DN_PRE_TILE
    nt = r // t
    scale = DN_HEAD_DIM ** -0.5

    def body(x_ref, halo_ref, w_ref, dq_ref, dk_ref, dv_ref, dgb_ref, ab_ref, al_ref, dt_ref, dco_ref, dab_ref, dal_ref,
             ddt_ref):
        i = pl.program_id(0)

        @pl.when(i == 0)
        def _():
            dal_ref[...] = jnp.zeros_like(dal_ref)
            ddt_ref[...] = jnp.zeros_like(ddt_ref)

        co_ = _dn_conv(x_ref[...], jnp.where(i > 0, halo_ref[...], 0.0), w_ref[...], t)
        act, sg = _silu(co_)
        dsilu = sg * (1.0 + co_ * (1.0 - sg))
        for hd in range(DN_HEADS):
            sl = slice(hd * 128, (hd + 1) * 128)
            for base, d_ref, sc in ((0, dq_ref, scale), (DN_WIDTH, dk_ref, 1.0)):
                cs = slice(base + hd * 128, base + (hd + 1) * 128)
                xh = act[:, cs]
                rn = lax.rsqrt(jnp.sum(xh * xh, axis=-1, keepdims=True) + EPS)
                xhat = xh * rn
                dy = d_ref[:, sl]
                dx = (sc * rn) * (dy - xhat * jnp.sum(dy * xhat, axis=-1, keepdims=True))
                dco_ref[:, cs] = dx * dsilu[:, cs]
        dco_ref[:, 2 * DN_WIDTH:] = dv_ref[...] * dsilu[:, 2 * DN_WIDTH:]
        rows = i * t + lax.broadcasted_iota(jnp.int32, (t, 1), 0)
        live = rows >= pad
        lane = lax.broadcasted_iota(jnp.int32, (1, 128), 1)
        ab_ = ab_ref[...]
        dgb_ = dgb_ref[...]
        is_g = live & (lane < DN_HEADS)
        is_b = live & (lane >= DN_HEADS) & (lane < 2 * DN_HEADS)
        arg = ab_ + dt_ref[...]
        ea = jnp.exp(al_ref[...])
        da = jnp.where(is_g, -dgb_ * ea * _sigmoid(arg), 0.0)
        beta = _sigmoid(ab_)
        dab_ref[...] = (da + jnp.where(is_b, dgb_ * beta * (1.0 - beta), 0.0)).astype(BF16)
        ddt_ref[...] += jnp.sum(da, axis=0, keepdims=True)
        dal_ref[...] += jnp.sum(jnp.where(is_g, -dgb_ * ea * _softplus(arg), 0.0), axis=0, keepdims=True)

    return _pallas(
        body, name=name, grid=(nt,),
        in_specs=[pl.BlockSpec((t, _DN_QKV), lambda i: (i, 0)), _halo_specs(_DN_QKV, t, nt, True),
                  pl.BlockSpec((DN_CONV, _DN_QKV), lambda i: (0, 0)),
                  _row_spec(DN_WIDTH, t), _row_spec(DN_WIDTH, t), _row_spec(DN_WIDTH, t),
                  _row_spec(128, t), _row_spec(128, t), _vec_spec(128), _vec_spec(128)],
        out_specs=[_row_spec(_DN_QKV, t), _row_spec(128, t), _vec_spec(128), _vec_spec(128)],
        out_shape=[jax.ShapeDtypeStruct((r, _DN_QKV), F32), jax.ShapeDtypeStruct((r, 128), BF16),
                   jax.ShapeDtypeStruct((1, 128), F32), jax.ShapeDtypeStruct((1, 128), F32)],
        compiler_params=_cparams(("arbitrary",)),
    )(proj, proj, conv_w, dq, dk, dv, dgb, ab, alog, dtb)


def _dn_conv_bwd(dco, proj, conv_w, dproj, name):
    r = dco.shape[0]
    t = DN_PRE_TILE
    nt = r // t

    def body(d_ref, dh_ref, x_ref, xh_ref, w_ref, dproj_in, dx_ref, dw_ref):
        i = pl.program_id(0)

        @pl.when(i == 0)
        def _():
            dw_ref[...] = jnp.zeros_like(dw_ref)

        d = d_ref[...]
        dhalo = jnp.where(i < nt - 1, dh_ref[...], 0.0)
        x = x_ref[...]
        xhalo = jnp.where(i > 0, xh_ref[...], 0.0)
        w = w_ref[...]
        dx = w[3:4] * d
        dws = [None] * DN_CONV
        dws[3] = jnp.sum(d * x, axis=0, keepdims=True)
        for tap in range(DN_CONV - 1):
            s = DN_CONV - 1 - tap
            dx = dx + w[tap:tap + 1] * _shift_up(d, dhalo, s, t)
            dws[tap] = jnp.sum(d * _shift_down(x, xhalo, s, t), axis=0, keepdims=True)
        dx_ref[...] = dx.astype(BF16)
        dw_ref[...] += jnp.concatenate(dws + [jnp.zeros((8 - DN_CONV, _DN_QKV), F32)], axis=0)

    return _pallas(
        body, name=name, grid=(nt,),
        in_specs=[_row_spec(_DN_QKV, t), _halo_specs(_DN_QKV, t, nt, False),
                  pl.BlockSpec((t, _DN_QKV), lambda i: (i, 0)), _halo_specs(_DN_QKV, t, nt, True),
                  pl.BlockSpec((DN_CONV, _DN_QKV), lambda i: (0, 0)), pl.BlockSpec(memory_space=pl.ANY)],
        out_specs=[_row_spec(_DN_QKV, t), pl.BlockSpec((8, _DN_QKV), lambda i: (0, 0))],
        out_shape=[jax.ShapeDtypeStruct(dproj.shape, BF16), jax.ShapeDtypeStruct((8, _DN_QKV), F32)],
        input_output_aliases={5: 0},
        compiler_params=_cparams(("arbitrary",)),
    )(dco, dco, proj, proj, conv_w, dproj)


def _split3(x):
    hi = x.astype(BF16)
    return hi, (x - hi.astype(F32)).astype(BF16)


def _dot3s(a, b, dims=((1,), (0,))):
    return _dot(a[0], b[0], dims) + (_dot(a[0], b[1], dims) + _dot(a[1], b[0], dims))


def _dot2s(a, b, dims=((1,), (0,))):
    return _dot(a[0], b[0], dims) + _dot(a[1], b[0], dims)


def _dot3(a, b, dims=((1,), (0,))):
    return _dot3s(_split3(a), _split3(b), dims)


def _dn_inverse_many(n_mats):
    c = n_mats[0].shape[0]
    row = lax.broadcasted_iota(jnp.int32, (c, c), 0)
    col = lax.broadcasted_iota(jnp.int32, (c, c), 1)
    eye = (row == col).astype(F32)
    same = row // DN_SUB == col // DN_SUB
    nds = [jnp.where(same, n, 0.0) for n in n_mats]
    nos = [n - nd for n, nd in zip(n_mats, nds)]

    def geometric(bs, order):
        xs = [eye + b for b in bs]
        sp = [_split3(b) for b in bs]
        k = 2
        while k < order:
            sp = [_split3(_dot2s(s_, s_)) for s_ in sp]
            xs = [x + _dot2s(_split3(x), s_) for x, s_ in zip(xs, sp)]
            k *= 2
        return xs

    tds = [_split3(td) for td in geometric([-nd for nd in nds], DN_SUB)]
    ms = [_dot3s(td, _split3(no)) for td, no in zip(tds, nos)]
    xs = geometric([-m for m in ms], c // DN_SUB)
    return [_dot3s(_split3(x), td) for x, td in zip(xs, tds)]


def _dn_chunk_shared(gb_ref, gbt_ref):
    c = DN_CHUNK
    row = lax.broadcasted_iota(jnp.int32, (c, c), 0)
    col = lax.broadcasted_iota(jnp.int32, (c, c), 1)
    gbv = gb_ref[...]
    gam_all = _split_dot((row >= col).astype(BF16), gbv)
    hi, lo = _split3(gbt_ref[...])
    tri_t = (row <= col).astype(BF16)
    return dict(row=row, col=col, gbv=gbv, gam_all=gam_all, gam_rows=_dot(hi, tri_t) + _dot(lo, tri_t),
                lane=lax.broadcasted_iota(jnp.int32, (1, 128), 1))


def _dn_chunk_common(q, k, v, sh, h):
    c = DN_CHUNK
    row, col, lane = sh["row"], sh["col"], sh["lane"]
    q, k, v = q.astype(F32), k.astype(F32), v.astype(F32)
    gam = jnp.sum(jnp.where(lane == h, sh["gam_all"], 0.0), axis=1, keepdims=True)
    beta = jnp.sum(jnp.where(lane == h + DN_HEADS, sh["gbv"], 0.0), axis=1, keepdims=True)
    gam_row = sh["gam_rows"][h:h + 1]
    dec = jnp.where(row >= col, jnp.exp(jnp.minimum(gam - gam_row, 0.0)), 0.0)
    kb, qb = k.astype(BF16), q.astype(BF16)
    nt_dims = ((1,), (1,))
    kk = _dot(kb, kb, nt_dims)
    qk = _dot(qb, kb, nt_dims)
    eg = jnp.exp(gam)
    gam_l = gam[c - 1:c, :]
    return dict(q=q, k=k, v=v, qb=qb, kb=kb, gam=gam, beta=beta, dec=dec, kk=kk, qk=qk, eg=eg, gam_l=gam_l,
                row=row, col=col, lane=lane, att=qk * dec, qg=q * eg, kt=k * jnp.exp(gam_l - gam),
                rhs=jnp.concatenate([v * beta, k * (beta * eg)], axis=1))


def _dn_fwd(q, k, v, gb, gbt, name):
    r = q.shape[0]
    c = DN_CHUNK
    nc = r // c
    dh = DN_HEAD_DIM
    tn_dims = ((0,), (0,))

    def body(q_ref, k_ref, v_ref, gb_ref, gbt_ref, o_ref, ss_ref, ts_ref, s_ref):
        @pl.when(pl.program_id(0) == 0)
        def _():
            s_ref[...] = jnp.zeros_like(s_ref)

        heads = list(range(DN_HEADS))
        sl = [slice(h * dh, (h + 1) * dh) for h in heads]
        sh = _dn_chunk_shared(gb_ref, gbt_ref)
        zs = [_dn_chunk_common(q_ref[:, sl[h]], k_ref[:, sl[h]], v_ref[:, sl[h]], sh, h) for h in heads]
        t_invs = _dn_inverse_many([jnp.where(sh["row"] > sh["col"], z["beta"] * z["kk"] * z["dec"], 0.0) for z in zs])
        sols = [_dot3(t_inv, z["rhs"]) for t_inv, z in zip(t_invs, zs)]
        ss = [s_ref[h] for h in heads]
        sbs = [s.astype(BF16) for s in ss]
        vnbs = [(sol[:, :dh] - _dot(sol[:, dh:].astype(BF16), sb)).astype(BF16) for sol, sb in zip(sols, sbs)]
        for h in heads:
            o_ref[:, sl[h]] = _dot(zs[h]["qg"].astype(BF16), sbs[h]) + _dot(zs[h]["att"].astype(BF16), vnbs[h])
        for h in heads:
            ss_ref[0, h] = ss[h]
            ts_ref[0, h] = t_invs[h]
            s_ref[h] = ss[h] * jnp.exp(zs[h]["gam_l"]) + _dot(zs[h]["kt"].astype(BF16), vnbs[h], tn_dims)

    blk = pl.BlockSpec((c, DN_WIDTH), lambda ci: (ci, 0))
    sav = pl.BlockSpec((1, DN_HEADS, dh, dh), lambda ci: (ci, 0, 0, 0))
    return _pallas(
        body, name=name, grid=(nc,),
        in_specs=[blk, blk, blk, pl.BlockSpec((c, 128), lambda ci: (ci, 0)), pl.BlockSpec((16, c), lambda ci: (0, ci))],
        out_specs=[blk, sav, sav],
        out_shape=[jax.ShapeDtypeStruct((r, DN_WIDTH), F32), jax.ShapeDtypeStruct((nc, DN_HEADS, dh, dh), F32),
                   jax.ShapeDtypeStruct((nc, DN_HEADS, dh, dh), F32)],
        scratch_shapes=[pltpu.VMEM((DN_HEADS, dh, dh), F32)],
        compiler_params=_cparams(("arbitrary",)),
    )(q, k, v, gb, gbt)


def _dn_bwd(q, k, v, gb, gbt, ssave, tsave, do, name):
    r = q.shape[0]
    c = DN_CHUNK
    nc = r // c
    dh = DN_HEAD_DIM
    nt_dims = ((1,), (1,))
    tn_dims = ((0,), (0,))

    def body(q_ref, k_ref, v_ref, gb_ref, gbt_ref, ss_ref, ts_ref, do_ref, dq_ref, dk_ref, dv_ref, dgb_ref, ds_ref):
        @pl.when(pl.program_id(0) == 0)
        def _():
            ds_ref[...] = jnp.zeros_like(ds_ref)

        heads = list(range(DN_HEADS))
        sl = [slice(h * dh, (h + 1) * dh) for h in heads]
        sh = _dn_chunk_shared(gb_ref, gbt_ref)
        row, col, lane = sh["row"], sh["col"], sh["lane"]
        rs = lambda x: jnp.sum(x, axis=1, keepdims=True)
        tot = lambda x: jnp.sum(rs(x), axis=0, keepdims=True)
        st = [dict() for _ in heads]
        dgb_parts = []

        def s_common(h):
            st[h].update(_dn_chunk_common(q_ref[:, sl[h]], k_ref[:, sl[h]], v_ref[:, sl[h]], sh, h))
            st[h]["t"] = _split3(ts_ref[0, h])

        def s_sol(h):
            st[h]["sol"] = _dot3s(st[h]["t"], _split3(st[h]["rhs"]))

        def s_state(h):
            z = st[h]
            sol = z["sol"]
            kcd = sol[:, dh:]
            s = ss_ref[0, h]
            sb = s.astype(BF16)
            vnb = (sol[:, :dh] - _dot(kcd.astype(BF16), sb)).astype(BF16)
            ds_next = ds_ref[h]
            dsb = ds_next.astype(BF16)
            dob = do_ref[:, sl[h]].astype(BF16)
            z["dqg"] = _dot(dob, sb, nt_dims)
            ds = _dot(z["qg"].astype(BF16), dob, tn_dims)
            z["d_att"] = jnp.where(row >= col, _dot(dob, vnb, nt_dims), 0.0)
            dvn = _dot(z["att"].astype(BF16), dob, tn_dims) + _dot(z["kt"].astype(BF16), dsb)
            z["dkt"] = _dot(vnb, dsb, nt_dims)
            eg_l = jnp.exp(z["gam_l"])
            ds = ds + ds_next * eg_l
            z["dgam_l"] = tot(ds_next * s) * eg_l
            dvnb = dvn.astype(BF16)
            dkcd = -_dot(dvnb, sb, nt_dims)
            ds_ref[h] = ds - _dot(kcd.astype(BF16), dvnb, tn_dims)
            z["dsol"] = jnp.concatenate([dvn, dkcd], axis=1)

        def s_drhs(h):
            st[h]["drhs"] = _dot3s(st[h]["t"], _split3(st[h]["dsol"]), tn_dims)

        def s_dn(h):
            z = st[h]
            z["dn"] = jnp.where(row > col, -_dot3(z["drhs"], z["sol"], nt_dims), 0.0)

        def s_rest(h):
            z = st[h]
            k_, v_, kb, qb = z["k"], z["v"], z["kb"], z["qb"]
            beta, eg, dec, kk, qk, gam, gam_l = z["beta"], z["eg"], z["dec"], z["kk"], z["qk"], z["gam"], z["gam_l"]
            dn, d_att, dqg, dkt = z["dn"], z["d_att"], z["dqg"], z["dkt"]
            drv, drk = z["drhs"][:, :dh], z["drhs"][:, dh:]
            s_rkk = rs(drk * k_)
            dv_ref[:, sl[h]] = drv * beta
            dbeta = rs(drv * v_) + s_rkk * eg + rs(dn * kk * dec)
            dk = drk * (beta * eg)
            dgam = s_rkk * beta * eg
            dkk = (dn * beta * dec).astype(BF16)
            dd = dn * beta * kk + d_att * qk
            dqk = (d_att * dec).astype(BF16)
            dq_ref[:, sl[h]] = _dot(dqk, kb) + dqg * eg
            dk = dk + _dot(dqk, qb, tn_dims) + _dot(dkk, kb) + _dot(dkk, kb, tn_dims)
            w = dd * dec
            wh, wl = _split3(w)
            ones = jnp.ones((c, 128), BF16)
            col_sum = (_dot(wh, ones, tn_dims) + _dot(wl, ones, tn_dims))[:, 0:1]
            dgam = dgam + rs(w) - col_sum + rs(dqg * z["qg"]) - rs(dkt * z["kt"])
            dk_ref[:, sl[h]] = dk + dkt * jnp.exp(gam_l - gam)
            dgam_l = z["dgam_l"] + tot(dkt * z["kt"])
            rowc = lax.broadcasted_iota(jnp.int32, (c, 1), 0)
            dgam = dgam + jnp.where(rowc == c - 1, dgam_l, 0.0)
            dg = _split_dot((row <= col).astype(BF16), jnp.broadcast_to(dgam, (c, 128)))[:, 0:1]
            dgb_parts.append(jnp.where(lane == h, dg, 0.0) + jnp.where(lane == h + DN_HEADS, dbeta, 0.0))

        _emit_chains(heads, [s_common, s_sol, s_state, s_drhs, s_dn, s_rest], False)
        dgb = dgb_parts[0]
        for part in dgb_parts[1:]:
            dgb = dgb + part
        dgb_ref[...] = dgb

    blk = pl.BlockSpec((c, DN_WIDTH), lambda ci: (nc - 1 - ci, 0))
    sav = pl.BlockSpec((1, DN_HEADS, dh, dh), lambda ci: (nc - 1 - ci, 0, 0, 0))
    gspec = pl.BlockSpec((c, 128), lambda ci: (nc - 1 - ci, 0))
    return _pallas(
        body, name=name, grid=(nc,),
        in_specs=[blk, blk, blk, gspec, pl.BlockSpec((16, c), lambda ci: (0, nc - 1 - ci)), sav, sav, blk],
        out_specs=[blk, blk, blk, gspec],
        out_shape=[jax.ShapeDtypeStruct((r, DN_WIDTH), F32)] * 3 + [jax.ShapeDtypeStruct((r, 128), F32)],
        scratch_shapes=[pltpu.VMEM((DN_HEADS, dh, dh), F32)],
        compiler_params=_cparams(("arbitrary",)),
    )(q, k, v, gb, gbt, ssave, tsave, do)


def _dn_post_fwd(o, proj, g, name):
    r = o.shape[0]

    def body(o_ref, z_ref, g_ref, y_ref):
        g_ = g_ref[...]
        for hd in range(DN_HEADS):
            sl = slice(hd * 128, (hd + 1) * 128)
            sz, _ = _silu(z_ref[:, sl])
            y_ref[:, sl] = (_rms(o_ref[:, sl], g_) * sz).astype(BF16)

    return _pallas(body, name=name, grid=(r // ROW_TILE,),
                   in_specs=[_row_spec(DN_WIDTH), pl.BlockSpec((ROW_TILE, DN_WIDTH), lambda i: (i, 3)), _vec_spec(128)],
                   out_specs=_row_spec(DN_WIDTH), out_shape=jax.ShapeDtypeStruct((r, DN_WIDTH), BF16),
                   compiler_params=_cparams(("parallel",)))(o, proj, g)


def _dn_post_bwd(o, proj, g, dy, name):
    r = o.shape[0]

    def body(o_ref, z_ref, g_ref, dy_ref, do_ref, dz_ref, dg_ref):
        @pl.when(pl.program_id(0) == 0)
        def _():
            dg_ref[...] = jnp.zeros_like(dg_ref)

        g_ = g_ref[...]
        for hd in range(DN_HEADS):
            sl = slice(hd * 128, (hd + 1) * 128)
            z_ = z_ref[:, sl]
            sz, sg = _silu(z_)
            dy_ = dy_ref[:, sl]
            o_ = o_ref[:, sl]
            dz_ref[:, sl] = (dy_ * _rms(o_, g_) * (sg * (1.0 + z_ * (1.0 - sg)))).astype(BF16)
            dx, dg = _rms_bwd(o_, g_, dy_ * sz)
            do_ref[:, sl] = dx
            dg_ref[...] += dg

    return _pallas(body, name=name, grid=(r // ROW_TILE,),
                   in_specs=[_row_spec(DN_WIDTH), pl.BlockSpec((ROW_TILE, DN_WIDTH), lambda i: (i, 3)), _vec_spec(128),
                             _row_spec(DN_WIDTH)],
                   out_specs=[_row_spec(DN_WIDTH), pl.BlockSpec((ROW_TILE, DN_WIDTH), lambda i: (i, 3)), _vec_spec(128)],
                   out_shape=[jax.ShapeDtypeStruct((r, DN_WIDTH), F32), jax.ShapeDtypeStruct((r, 4 * DN_WIDTH), BF16),
                              jax.ShapeDtypeStruct((1, 128), F32)],
                   compiler_params=_cparams(("arbitrary",)))(o, proj, g, dy)


def _exchange(arrays, scatter, name):
    n = len(arrays)

    def body(*refs):
        copies = _exchange_copies(refs[:n], refs[n:2 * n], scatter, *refs[2 * n:])
        for cp in copies:
            cp.start()
        for cp in copies:
            cp.wait()

    hbm = pl.BlockSpec(memory_space=pl.ANY)
    return _pallas(
        body, name=name, in_specs=[hbm] * n, out_specs=[hbm] * n, out_shape=_exchange_shapes(arrays, scatter),
        scratch_shapes=_exchange_sems(n),
    )(*arrays)


def _exchange_shapes(arrays, scatter):
    return [jax.ShapeDtypeStruct((N_DEV,) + (a.shape[1:] if sc else a.shape), a.dtype) for a, sc in zip(arrays, scatter)]


def _exchange_sems(n):
    return [pltpu.SemaphoreType.DMA((n * N_DEV,)), pltpu.SemaphoreType.DMA((n * N_DEV,)), pltpu.SemaphoreType.DMA((n,))]


def _exchange_copies(in_refs, out_refs, scatter, send_sems, recv_sems, local_sems):
    mx, my, mc = lax.axis_index("x"), lax.axis_index("y"), lax.axis_index("c")
    me = 4 * mx + 2 * my + mc
    copies = []
    for a in range(len(in_refs)):
        src_own = in_refs[a].at[me] if scatter[a] else in_refs[a]
        copies.append(pltpu.make_async_copy(src_own, out_refs[a].at[me], local_sems.at[a]))
        for kbits in range(1, N_DEV):
            px = lax.rem(mx + ((kbits >> 2) & 1), 2)
            py = lax.rem(my + ((kbits >> 1) & 1), 2)
            pc = lax.rem(mc + (kbits & 1), 2)
            src = in_refs[a].at[4 * px + 2 * py + pc] if scatter[a] else in_refs[a]
            copies.append(pltpu.make_async_remote_copy(
                src_ref=src, dst_ref=out_refs[a].at[me],
                send_sem=send_sems.at[a * N_DEV + kbits], recv_sem=recv_sems.at[a * N_DEV + kbits],
                device_id=(px, py, pc), device_id_type=pl.DeviceIdType.MESH))
    return copies


def _adamw(gstack, w, m, v, name):
    a, b = w.shape
    ta = a
    for t in (1024, 512, 256, 128, 64, 32, 16, 8):
        if a % t == 0 and N_DEV * t * b * 4 <= 4 * 1024 * 1024:
            ta = t
            break
    c1 = 1.0 / (1.0 - ADAM_B1 ** ADAM_STEP)
    c2 = 1.0 / (1.0 - ADAM_B2 ** ADAM_STEP)

    def body(g_ref, w_ref, m_ref, v_ref, og_ref, od_ref, om_ref, ov_ref):
        g = g_ref[0].astype(F32)
        for s in range(1, N_DEV):
            g = g + g_ref[s].astype(F32)
        m_new = ADAM_B1 * m_ref[...] + (1.0 - ADAM_B1) * g
        v_new = ADAM_B2 * v_ref[...] + (1.0 - ADAM_B2) * (g * g)
        og_ref[...] = g
        om_ref[...] = m_new
        ov_ref[...] = v_new
        od_ref[...] = -ADAM_LR * ((m_new * c1) / (jnp.sqrt(v_new * c2) + ADAM_EPS) + ADAM_WD * w_ref[...])

    spec = pl.BlockSpec((ta, b), lambda i: (i, 0))
    return _pallas(
        body, name=name, grid=(a // ta,),
        in_specs=[pl.BlockSpec((N_DEV, ta, b), lambda i: (0, i, 0)), spec, spec, spec],
        out_specs=[spec] * 4, out_shape=[jax.ShapeDtypeStruct((a, b), F32)] * 4,
        compiler_params=_cparams(("parallel",)),
    )(gstack, w, m, v)


_WEIGHTS = ['meta_tokens', 'pre_mix_norm', 'post_mix_norm', 'pre_mlp_norm', 'post_mlp_norm', 'mlp_w1', 'mlp_w2',
            'w_in_even', 'w_out_even', 'sb_out_norm', 's5_lambda_re', 's5_lambda_im', 's5_log_dt', 's5_b_re', 's5_b_im',
            's5_c_re', 's5_c_im', 's5_d', 's5_w_glu', 's5_b_glu', 's5_out_norm', 'w_in_odd', 'dn_conv_w', 'dn_a_log',
            'dn_dt_bias', 'dn_out_norm', 'w_out_odd']
_SHARDED = ['meta_tokens', 'mlp_w1', 'mlp_w2', 'w_in_even', 'w_out_even', 's5_w_glu', 'w_in_odd', 'dn_conv_w', 'w_out_odd']
_SMALL = [n for n in _WEIGHTS if n not in _SHARDED]
_GATHER_FIRST = ['meta_tokens', 'w_in_even', 's5_w_glu', 'w_out_even']
_GATHER_LATE = [n for n in _SHARDED if n not in _GATHER_FIRST]
_REDUCE_EARLY = ['mlp_w1', 'mlp_w2', 'w_in_odd', 'dn_conv_w', 'w_out_odd', 'w_out_even']


def _view2d(name, a):
    return a.reshape(-1, a.shape[-1])


def _unshard(name, g):
    if name == 'mlp_w1':
        return g.reshape(N_DEV, 2, D_MODEL, -1).transpose(1, 2, 0, 3).reshape(2, D_MODEL, D_FF)
    if name == 'mlp_w2':
        return g.reshape(N_DEV, 2, -1, D_MODEL).transpose(1, 0, 2, 3).reshape(2, D_FF, D_MODEL)
    if name in ('w_in_even', 'w_in_odd', 'dn_conv_w', 'meta_tokens'):
        return g.transpose(1, 0, 2).reshape(g.shape[1], -1)
    return g.reshape(-1, g.shape[-1])


def _to_blocks(name, full):
    if name == 'mlp_w1':
        return full.reshape(2, D_MODEL, N_DEV, -1).transpose(2, 0, 1, 3).reshape(N_DEV, 2 * D_MODEL, -1)
    if name == 'mlp_w2':
        return full.reshape(2, N_DEV, -1, D_MODEL).transpose(1, 0, 2, 3).reshape(N_DEV, -1, D_MODEL)
    if name in ('w_in_even', 'w_in_odd', 'dn_conv_w', 'meta_tokens'):
        return full.reshape(full.shape[0], N_DEV, -1).transpose(1, 0, 2)
    return full.reshape(N_DEV, -1, full.shape[-1])


def _pack(parts):
    rows = []
    for p in parts:
        flat = p.reshape(-1)
        rows.append(jnp.pad(flat, (0, (-flat.shape[0]) % 128)).reshape(-1, 128))
    return jnp.concatenate(rows, axis=0)


def _unpack(packed, like):
    out, at = [], 0
    for p in like:
        n = math.prod(p.shape)
        nrow = -(-n // 128)
        out.append(packed[at:at + nrow].reshape(-1)[:n].reshape(p.shape))
        at += nrow
    return out


def _lane_vec(x, width=128):
    flat = x.reshape(-1)
    return jnp.pad(flat, (0, width - flat.shape[0])).reshape(1, width)


def kernel(x, meta_tokens, pre_mix_norm, post_mix_norm, pre_mlp_norm, post_mlp_norm, mlp_w1, mlp_w2, w_in_even, w_out_even, sb_out_norm, s5_lambda_re, s5_lambda_im, s5_log_dt, s5_b_re, s5_b_im, s5_c_re, s5_c_im, s5_d, s5_w_glu, s5_b_glu, s5_out_norm, w_in_odd, dn_conv_w, dn_a_log, dn_dt_bias, dn_out_norm, w_out_odd, loss_target, m_meta_tokens, m_pre_mix_norm, m_post_mix_norm, m_pre_mlp_norm, m_post_mlp_norm, m_mlp_w1, m_mlp_w2, m_w_in_even, m_w_out_even, m_sb_out_norm, m_s5_lambda_re, m_s5_lambda_im, m_s5_log_dt, m_s5_b_re, m_s5_b_im, m_s5_c_re, m_s5_c_im, m_s5_d, m_s5_w_glu, m_s5_b_glu, m_s5_out_norm, m_w_in_odd, m_dn_conv_w, m_dn_a_log, m_dn_dt_bias, m_dn_out_norm, m_w_out_odd, v_meta_tokens, v_pre_mix_norm, v_post_mix_norm, v_pre_mlp_norm, v_post_mlp_norm, v_mlp_w1, v_mlp_w2, v_w_in_even, v_w_out_even, v_sb_out_norm, v_s5_lambda_re, v_s5_lambda_im, v_s5_log_dt, v_s5_b_re, v_s5_b_im, v_s5_c_re, v_s5_c_im, v_s5_d, v_s5_w_glu, v_s5_b_glu, v_s5_out_norm, v_w_in_odd, v_dn_conv_w, v_dn_a_log, v_dn_dt_bias, v_dn_out_norm, v_w_out_odd):
    given = dict(locals())
    w = {n: given[n] for n in _WEIGHTS}
    mom_m = {n: given["m_" + n] for n in _WEIGHTS}
    mom_v = {n: given["v_" + n] for n in _WEIGHTS}

    seq = x.shape[1]
    assert x.shape[0] == 1 and seq % ROW_TILE == 0
    r = seq + ROW_TILE
    pad = ROW_TILE - N_META

    wire = {n: (F32 if n in ('dn_conv_w', 'meta_tokens') else BF16) for n in _SHARDED}
    shard_wire = lambda n: _view2d(n, w[n]).astype(wire[n])
    gathered = _exchange([shard_wire(n) for n in _GATHER_FIRST], [False] * len(_GATHER_FIRST), "gather_first")
    full = {n: _unshard(n, g_) for n, g_ in zip(_GATHER_FIRST, gathered)}
    w_ie, w_oe, w_glu = full['w_in_even'], full['w_out_even'], full['s5_w_glu']
    row = lambda v_: v_.reshape(1, -1)

    hs0 = jnp.concatenate([jnp.zeros((pad, D_MODEL), F32), full['meta_tokens'], x[0]], axis=0)
    hn0 = _norm_pre(hs0, row(pre_mix_norm[0]), "pre_mix_0")
    qkv = _mm_fwd(hn0, w_ie[:, :3 * SB_WIDTH], "in_even_qkv", out_dtypes=(BF16,))
    u = _mm_fwd(hn0, w_ie[:, 3 * SB_WIDTH:], "in_even_u")
    q, k, v = qkv[:, :SB_WIDTH], qkv[:, SB_WIDTH:2 * SB_WIDTH], qkv[:, 2 * SB_WIDTH:]
    nb = r // ATT_BLK
    blocks_t = lambda t_: t_.reshape(nb, ATT_BLK, 4, 128).transpose(2, 0, 3, 1)
    o_sb, ssave, gathered = _sb_fwd(q, k, blocks_t(v), pad, "sb_fwd",
                                    ride=([shard_wire(n) for n in _GATHER_LATE], [False] * len(_GATHER_LATE)))
    full.update({n: _unshard(n, g_) for n, g_ in zip(_GATHER_LATE, gathered)})
    w1, w2, w_oo, conv_w = full['mlp_w1'], full['mlp_w2'], full['w_out_odd'], full['dn_conv_w']
    w_io = full['w_in_odd'][:, :4 * DN_WIDTH]
    w_ab = jnp.pad(full['w_in_odd'][:, 4 * DN_WIDTH:], ((0, 0), (0, 128 - 2 * DN_HEADS)))

    lam_re, lam_im, logdt, btr, bti, ctr, cti, s5_mask = _s5_expand(
        s5_lambda_re[0], s5_lambda_im[0], s5_log_dt[0], s5_b_re[0], s5_b_im[0], s5_c_re[0], s5_c_im[0])
    a_re, a_im, bbr, bbi = _s5_prep(lam_re, lam_im, logdt, btr, bti, "s5_prep")
    s5_wb = jnp.stack([_s5_block_diag_b(bbr, s5_mask), _s5_block_diag_b(bbi, s5_mask)]).astype(BF16)
    s5_wc = jnp.stack([_s5_block_diag_c(ctr, s5_mask), _s5_block_diag_c(cti, s5_mask)]).astype(BF16)
    s5_a = jnp.stack([a_re, a_im])
    s5_args = (s5_wb, s5_a, s5_wc, row(s5_d[0]), w_glu, row(s5_b_glu[0]), row(s5_out_norm[0]))
    y_s5, merged, xstart = _s5_fwd(u, *s5_args, "s5_fwd")
    merged = _norm_pre(o_sb, row(sb_out_norm[0]), "sb_out_norm", into=merged)

    mix0, hs1, hn1 = _mm_norm_fwd(merged, w_oe, hs0, row(post_mix_norm[0]), g_pre=row(pre_mlp_norm[0]), name="out_even")
    relu2 = lambda acc: (jnp.square(jnp.maximum(acc, 0.0)), jnp.maximum(acc, 0.0))
    r0, ra0 = _mm_fwd(hn1, w1[0], "mlp_up_0", out_dtypes=(BF16, BF16), epilogue=relu2)
    m0, hs2, hn2 = _mm_norm_fwd(r0, w2[0], hs1, row(post_mlp_norm[0]), g_pre=row(pre_mix_norm[1]), name="mlp_down_0")

    proj = _mm_fwd(hn2, w_io, "in_odd")
    ab = _mm_fwd(hn2, w_ab, "in_odd_gates")
    alog, dtb = _lane_vec(dn_a_log[0]), _lane_vec(dn_dt_bias[0])
    qd, kd, vd, gb = _dn_pre_fwd(proj, ab, conv_w, alog, dtb, pad, "dn_pre")
    gbt = gb[:, :2 * DN_HEADS].T
    o_dn, s_dn, t_dn = _dn_fwd(qd, kd, vd, gb, gbt, "dn_fwd")
    on_dn = _dn_post_fwd(o_dn, proj, row(dn_out_norm[0]), "dn_post")
    mix1, hs3, hn3 = _mm_norm_fwd(on_dn, w_oo, hs2, row(post_mix_norm[1]), g_pre=row(pre_mlp_norm[1]), name="out_odd")
    r1, ra1 = _mm_fwd(hn3, w1[1], "mlp_up_1", out_dtypes=(BF16, BF16), epilogue=relu2)
    dhs, dm1, dg_post_mlp1, loss_part = _mm_norm_fwd(r1, w2[1], hs3, row(post_mlp_norm[1]),
                                                     loss=(loss_target[0], pad + N_META), name="mlp_down_1_loss")
    loss = lax.psum(loss_part, ("x", "y", "c"))

    g = {}
    drelu2 = lambda acc, ra: (acc * (2.0 * ra.astype(F32)),)

    def mlp_bwd(layer, hn, rr, ra, dm):
        dw2 = _mm_wgrad(rr, dm, f"mlp_down_{layer}_wgrad")
        da = _mm_dgrad(dm, w2[layer], f"mlp_down_{layer}_dgrad", out_dtypes=(BF16,), extras=(ra,), epilogue=drelu2)
        dw1 = _mm_wgrad(hn, da, f"mlp_up_{layer}_wgrad")
        return dw1, dw2, da

    dw1_1, dw2_1, da1 = mlp_bwd(1, hn3, r1, ra1, dm1)
    dhs, dmix1, dg_pre_mlp1, dg_post_mix1 = _dgrad_norm_bwd(
        da1, w1[1], dhs, hs3, row(pre_mlp_norm[1]), post=(mix1, row(post_mix_norm[1])), pad=pad, name="post_mix_1_bwd")

    g['w_out_odd'] = _mm_wgrad(on_dn, dmix1, "out_odd_wgrad")
    d_on_dn = _mm_dgrad(dmix1, w_oo, "out_odd_dgrad")
    do_dn, dproj, dg_dn = _dn_post_bwd(o_dn, proj, row(dn_out_norm[0]), d_on_dn, "dn_post_bwd")
    dqd, dkd, dvd, dgb = _dn_bwd(qd, kd, vd, gb, gbt, s_dn, t_dn, do_dn, "dn_bwd")
    dco, dab, d_alog, d_dtb = _dn_pre_bwd(proj, conv_w, dqd, dkd, dvd, dgb, ab, alog, dtb, pad, "dn_pre_bwd")
    dproj, d_conv = _dn_conv_bwd(dco, proj, conv_w, dproj, "dn_conv_bwd")
    g['w_in_odd'] = jnp.concatenate([_mm_wgrad(hn2, dproj, "in_odd_wgrad"),
                                     _mm_wgrad(hn2, dab, "in_odd_gates_wgrad")[:, :2 * DN_HEADS]], axis=1)
    dhn2_gates = _mm_dgrad(dab, w_ab, "in_odd_gates_dgrad")
    g['dn_conv_w'] = d_conv[:DN_CONV]
    g['dn_a_log'], g['dn_dt_bias'], g['dn_out_norm'] = d_alog[0, :DN_HEADS], d_dtb[0, :DN_HEADS], dg_dn[0]

    dhs, dm0, dg_pre_mix1, dg_post_mlp0 = _dgrad_norm_bwd(
        dproj, w_io, dhs, hs2, row(pre_mix_norm[1]), post=(m0, row(post_mlp_norm[0])), add=dhn2_gates, pad=pad,
        name="post_mlp_0_bwd")
    dw1_0, dw2_0, da0 = mlp_bwd(0, hn1, r0, ra0, dm0)
    dhs, dmix0, dg_pre_mlp0, dg_post_mix0 = _dgrad_norm_bwd(
        da0, w1[0], dhs, hs1, row(pre_mlp_norm[0]), post=(mix0, row(post_mix_norm[0])), pad=pad, name="post_mix_0_bwd")

    g['w_out_even'] = _mm_wgrad(merged, dmix0, "out_even_wgrad")
    dmerged = _mm_dgrad(dmix0, w_oe, "out_even_dgrad")
    _, do_sb, _, dg_sb = _norm_bwd(dmerged, post=(o_sb, row(sb_out_norm[0])), pad=pad, dm_dtype=F32,
                                   dhs_cols=(SB_WIDTH, 0), name="sb_out_norm_bwd")
    dq, dk, dv = _sb_bwd(q, k, v, blocks_t(k), ssave, do_sb, pad, "sb_bwd")
    g['mlp_w1'] = jnp.stack([dw1_0, dw1_1])
    g['mlp_w2'] = jnp.stack([dw2_0, dw2_1])
    grad_wire = lambda n: _to_blocks(n, g[n].reshape(full[n].shape)).astype(wire[n])
    du, d_a, d_d, d_bglu, dg_s5, d_wb, d_wc, g['s5_w_glu'], reduced = _s5_bwd(
        u, y_s5, dmerged, xstart, *s5_args, "s5_bwd", don_block=1,
        ride=([grad_wire(n) for n in _REDUCE_EARLY], [True] * len(_REDUCE_EARLY)))
    stacks = dict(zip(_REDUCE_EARLY, reduced))
    g_lr, g_li, g_dt, g_btr, g_bti = _s5_prep_bwd(
        lam_re, lam_im, logdt, btr, bti, d_a[0], d_a[1],
        _s5_diag_of_b(d_wb[0], s5_mask), _s5_diag_of_b(d_wb[1], s5_mask), "s5_prep_bwd")
    gg, nn, pp = S5_GROUPS, S5_STATE, S5_GROUP
    g['s5_lambda_re'], g['s5_lambda_im'] = g_lr.reshape(gg, nn), g_li.reshape(gg, nn)
    g['s5_log_dt'] = g_dt.reshape(gg, nn)[:, 0]
    g['s5_b_re'], g['s5_b_im'] = g_btr.T.reshape(gg, nn, pp), g_bti.T.reshape(gg, nn, pp)
    g['s5_c_re'] = _s5_diag_of_c(d_wc[0], s5_mask).reshape(gg, nn, pp).transpose(0, 2, 1)
    g['s5_c_im'] = _s5_diag_of_c(d_wc[1], s5_mask).reshape(gg, nn, pp).transpose(0, 2, 1)
    g['s5_d'], g['s5_b_glu'], g['s5_out_norm'], g['sb_out_norm'] = d_d[0], d_bglu[0], dg_s5[0], dg_sb[0]
    dqkvu = jnp.concatenate([dq, dk, dv, du], axis=1).astype(BF16)
    g['w_in_even'] = _mm_wgrad(hn0, dqkvu, "in_even_wgrad")
    dhs, _, dg_pre_mix0, _ = _dgrad_norm_bwd(dqkvu, w_ie, dhs, hs0, row(pre_mix_norm[0]), pad=pad, name="pre_mix_0_bwd")

    g['meta_tokens'] = dhs[pad:pad + N_META]
    g['pre_mix_norm'] = jnp.concatenate([dg_pre_mix0, dg_pre_mix1], axis=0)
    g['post_mix_norm'] = jnp.concatenate([dg_post_mix0, dg_post_mix1], axis=0)
    g['pre_mlp_norm'] = jnp.concatenate([dg_pre_mlp0, dg_pre_mlp1], axis=0)
    g['post_mlp_norm'] = jnp.concatenate([dg_post_mlp0, dg_post_mlp1], axis=0)
    grad_x = dhs[pad + N_META:][None]

    small_like = [w[n] for n in _SMALL]
    last = [n for n in _SHARDED if n not in _REDUCE_EARLY]
    partial = [grad_wire(n) for n in last] + [_pack([g[n].reshape(w[n].shape) for n in _SMALL])]
    reduced = _exchange(partial, [True] * len(last) + [False], "reduce_last")
    stacks.update(zip(last, reduced[:-1]))
    grads, deltas, new_m, new_v = {}, {}, {}, {}
    for n in _SHARDED:
        outs = _adamw(stacks[n], _view2d(n, w[n]), _view2d(n, mom_m[n]), _view2d(n, mom_v[n]), f"adamw_{n}")
        grads[n], deltas[n], new_m[n], new_v[n] = (o.reshape(w[n].shape) for o in outs)
    outs = _adamw(reduced[-1], _pack(small_like), _pack([mom_m[n] for n in _SMALL]), _pack([mom_v[n] for n in _SMALL]),
                  "adamw_small")
    for dst, o in zip((grads, deltas, new_m, new_v), outs):
        for n, part in zip(_SMALL, _unpack(o, small_like)):
            dst[n] = part
    return (loss, grad_x, *[grads[n] for n in _WEIGHTS], *[deltas[n] for n in _WEIGHTS],
            *[new_m[n] for n in _WEIGHTS], *[new_v[n] for n in _WEIGHTS])
```

```python
import math

import jax
import jax.numpy as jnp
from jax import lax
from jax.experimental import pallas as pl
from jax.experimental.pallas import tpu as pltpu

F32 = jnp.float32
BF16 = jnp.bfloat16

D_MODEL = 1024
N_META = 16
SB_HEAD_DIM = 64
SB_WIDTH = 512
S5_WIDTH = 512
S5_GROUP = 16
S5_GROUPS = 32
S5_STATE = 64
S5_NS = S5_GROUPS * S5_STATE
DN_HEAD_DIM = 128
DN_HEADS = 8
DN_WIDTH = 1024
DN_CONV = 4
D_FF = 4096
EPS = 1e-6
N_DEV = 8

ADAM_LR = 0.001
ADAM_B1 = 0.9
ADAM_B2 = 0.999
ADAM_EPS = 1e-08
ADAM_WD = 0.01
ADAM_STEP = 10

ROW_TILE = 512
FUSED_FWD_TILE = 512
ATT_BLK = 256
SB_BLOCKS_PER_TRIP = 3
SB_LOG_ZERO = -106.0
SB_FWD_SKEW = False
SB_BWD_SKEW = True
DN_CHUNK = 128
DN_SUB = 16
S5_TILE = 256
S5_CHUNKS = 4
VMEM_LIMIT = 56 * 1024 * 1024

_HIGH = lax.Precision.HIGHEST


def _pallas(body, **kw):
    return pl.pallas_call(body, **kw)


def _cparams(sem):
    return pltpu.CompilerParams(dimension_semantics=sem, vmem_limit_bytes=VMEM_LIMIT)


def _dot(a, b, dims=((1,), (0,))):
    return lax.dot_general(a, b, (dims, ((), ())), preferred_element_type=F32)


def _dot_hi(a, b):
    return lax.dot_general(a, b, (((1,), (0,)), ((), ())), preferred_element_type=F32, precision=_HIGH)


def _split_dot(m_bf16, x):
    hi = x.astype(BF16)
    lo = (x - hi.astype(F32)).astype(BF16)
    return _dot(m_bf16, hi) + _dot(m_bf16, lo)


def _matmul(a, b, *, ta=False, tb=False, tm, tn, tk, name, out_dtypes=(F32,), extras=(), epilogue=None):
    m, k = (a.shape[1], a.shape[0]) if ta else a.shape
    n = b.shape[0] if tb else b.shape[1]
    assert (b.shape[1] if tb else b.shape[0]) == k
    assert m % tm == 0 and n % tn == 0 and k % tk == 0, (name, m, n, k, tm, tn, tk)
    nk = k // tk
    n_ex = len(extras)
    n_out = len(out_dtypes)
    dims = ((0 if ta else 1,), (1 if tb else 0,))

    def finish(acc, ex_refs, o_refs):
        outs = (acc,) if epilogue is None else epilogue(acc, *[r[...] for r in ex_refs])
        for o_ref, o in zip(o_refs, outs):
            o_ref[...] = o.astype(o_ref.dtype)

    def body(*refs):
        a_ref, b_ref = refs[0], refs[1]
        ex_refs = refs[2:2 + n_ex]
        o_refs = refs[2 + n_ex:2 + n_ex + n_out]
        prod = _dot(a_ref[...].astype(BF16), b_ref[...].astype(BF16), dims)
        if nk == 1:
            finish(prod, ex_refs, o_refs)
            return
        acc_ref = refs[-1]
        kk = pl.program_id(2)

        @pl.when(kk == 0)
        def _():
            acc_ref[...] = prod

        @pl.when(kk > 0)
        def _():
            acc_ref[...] += prod

        @pl.when(kk == nk - 1)
        def _():
            finish(acc_ref[...], ex_refs, o_refs)

    a_spec = pl.BlockSpec((tk, tm), lambda j, i, kk: (kk, i)) if ta else pl.BlockSpec((tm, tk), lambda j, i, kk: (i, kk))
    b_spec = pl.BlockSpec((tn, tk), lambda j, i, kk: (j, kk)) if tb else pl.BlockSpec((tk, tn), lambda j, i, kk: (kk, j))
    o_spec = pl.BlockSpec((tm, tn), lambda j, i, kk: (i, j))
    outs = _pallas(
        body, name=name,
        grid=(n // tn, m // tm, nk),
        in_specs=[a_spec, b_spec] + [o_spec] * n_ex,
        out_specs=[o_spec] * n_out,
        out_shape=[jax.ShapeDtypeStruct((m, n), dt) for dt in out_dtypes],
        scratch_shapes=[] if nk == 1 else [pltpu.VMEM((tm, tn), F32)],
        compiler_params=_cparams(("parallel", "parallel", "arbitrary")),
    )(a, b, *extras)
    return outs[0] if n_out == 1 else outs


def _tile(n, cap):
    best = 128
    for t in range(128, min(n, cap) + 1, 128):
        if n % t == 0:
            best = t
    assert n % best == 0, n
    return best


MM_K_CAP = 4096
WGRAD_ROWS = 1536


MM_LHS_TILE_BYTES = 6 * 1024 * 1024


def _row_tile(x, depth):
    tall = 3 * ROW_TILE
    fits = tall * depth * x.dtype.itemsize <= MM_LHS_TILE_BYTES
    return tall if (x.shape[0] % tall == 0 and fits) else ROW_TILE


def _mm_fwd(x, w, name, **kw):
    k, n = w.shape
    tk = _tile(k, MM_K_CAP)
    return _matmul(x, w, tm=_row_tile(x, tk), tn=_tile(n, 1024), tk=tk, name=name, **kw)


def _mm_dgrad(dy, w, name, **kw):
    k, n = w.shape
    tk = _tile(n, MM_K_CAP)
    return _matmul(dy, w, tb=True, tm=_row_tile(dy, tk), tn=_tile(k, 1024), tk=tk, name=name, **kw)


def _mm_wgrad(x, dy, name):
    k, n = x.shape[1], dy.shape[1]
    rows = x.shape[0]
    return _matmul(x, dy, ta=True, tm=_tile(k, 1024), tn=_tile(n, 1024),
                   tk=WGRAD_ROWS if rows % WGRAD_ROWS == 0 else ROW_TILE, name=name)


def _rms(x, g):
    r = lax.rsqrt(jnp.mean(x * x, axis=-1, keepdims=True) + EPS)
    return x * r * g


def _rms_bwd(x, g, dy):
    r = lax.rsqrt(jnp.mean(x * x, axis=-1, keepdims=True) + EPS)
    xh = x * r
    dxh = dy * g
    dx = r * (dxh - xh * jnp.mean(dxh * xh, axis=-1, keepdims=True))
    dg = jnp.sum(dy * xh, axis=0, keepdims=True)
    return dx, dg


def _row_spec(width, tile=ROW_TILE):
    return pl.BlockSpec((tile, width), lambda i: (i, 0))


def _vec_spec(width):
    return pl.BlockSpec((1, width), lambda i: (0, 0))


def _norm_pre(hs, g, name, into=None):
    r, d = hs.shape

    def body(x_ref, g_ref, *rest):
        rest[-1][...] = _rms(x_ref[...], g_ref[...]).astype(BF16)

    if into is None:
        return _pallas(body, name=name, grid=(r // ROW_TILE,), in_specs=[_row_spec(d), _vec_spec(d)],
                       out_specs=_row_spec(d), out_shape=jax.ShapeDtypeStruct((r, d), BF16),
                       compiler_params=_cparams(("parallel",)))(hs, g)
    return _pallas(body, name=name, grid=(r // ROW_TILE,),
                   in_specs=[_row_spec(d), _vec_spec(d), pl.BlockSpec(memory_space=pl.ANY)],
                   out_specs=_row_spec(d), out_shape=jax.ShapeDtypeStruct(into.shape, BF16), input_output_aliases={2: 0},
                   compiler_params=_cparams(("parallel",)))(hs, g, into)


def _mm_norm_fwd(a, w, hs, g_post, *, g_pre=None, loss=None, name):
    k, d = w.shape
    r = a.shape[0]
    assert k <= MM_K_CAP and d == hs.shape[1]
    t = FUSED_FWD_TILE
    nt = r // t

    def body(*refs):
        a_ref, w_ref, hs_ref, gp_ref = refs[:4]
        i = pl.program_id(0)
        m = _dot(a_ref[...].astype(BF16), w_ref[...].astype(BF16))
        gp = gp_ref[...]
        new = hs_ref[...] + _rms(m, gp)
        if loss is None:
            gn_ref, m_ref, o_ref, hn_ref = refs[4:]
            m_ref[...] = m
            o_ref[...] = new
            hn_ref[...] = _rms(new, gn_ref[...]).astype(BF16)
        else:
            t_ref, dhs_ref, dm_ref, dgp_ref, loss_ref = refs[4:]
            live = (i * t + lax.broadcasted_iota(jnp.int32, (t, 1), 0)) >= loss[1]
            diff = jnp.where(live, new - t_ref[...], 0.0)
            dhs = diff * (1.0 / d)
            dhs_ref[...] = dhs
            loss_ref[...] = jnp.full((8, 128), 0.5 / d * jnp.sum(diff * diff), F32)
            dm, dg = _rms_bwd(m, gp, dhs)
            dm_ref[...] = dm.astype(BF16)

            @pl.when(i == 0)
            def _():
                dgp_ref[...] = jnp.zeros_like(dgp_ref)
            dgp_ref[...] += dg

    common_in = [_row_spec(k, t), pl.BlockSpec((k, d), lambda i: (0, 0)), _row_spec(d, t), _vec_spec(d)]
    if loss is None:
        return _pallas(
            body, name=name, grid=(nt,), in_specs=common_in + [_vec_spec(d)],
            out_specs=[_row_spec(d, t)] * 3,
            out_shape=[jax.ShapeDtypeStruct((r, d), F32), jax.ShapeDtypeStruct((r, d), F32), jax.ShapeDtypeStruct((r, d), BF16)],
            compiler_params=_cparams(("parallel",)))(a, w, hs, g_post, g_pre)
    target, first_row = loss
    assert first_row % t == 0
    dhs, dm, dgp, parts = _pallas(
        body, name=name, grid=(nt,),
        in_specs=common_in + [pl.BlockSpec((t, d), lambda i: (jnp.maximum(i - first_row // t, 0), 0))],
        out_specs=[_row_spec(d, t), _row_spec(d, t), _vec_spec(d), pl.BlockSpec((8, 128), lambda i: (i, 0))],
        out_shape=[jax.ShapeDtypeStruct((r, d), F32), jax.ShapeDtypeStruct((r, d), BF16), jax.ShapeDtypeStruct((1, d), F32),
                   jax.ShapeDtypeStruct((nt * 8, 128), F32)],
        compiler_params=_cparams(("arbitrary",)))(a, w, hs, g_post, target)
    return dhs, dm, dgp, jnp.sum(parts[::8, 0])


def _norm_bwd(dhs, *, pre=None, post=None, pad=0, dm_dtype=BF16, dhs_cols=None, name):
    r = dhs.shape[0]
    d = dhs.shape[1] if dhs_cols is None else dhs_cols[0]
    has_pre, has_post = pre is not None, post is not None

    def body(*refs):
        it = iter(refs)
        dhs_ref = next(it)
        if has_pre:
            hs_ref, gn_ref, dhn_ref = next(it), next(it), next(it)
        if has_post:
            m_ref, gp_ref = next(it), next(it)
        if has_pre:
            o_dhs, o_dgn = next(it), next(it)
        if has_post:
            o_dm, o_dgp = next(it), next(it)
        i = pl.program_id(0)
        live = (i * ROW_TILE + lax.broadcasted_iota(jnp.int32, (ROW_TILE, 1), 0)) >= pad
        cur = jnp.where(live, dhs_ref[...], 0.0)
        if has_pre:
            dx, dg = _rms_bwd(hs_ref[...], gn_ref[...], jnp.where(live, dhn_ref[...].astype(F32), 0.0))
            cur = cur + dx
            o_dhs[...] = cur

            @pl.when(i == 0)
            def _():
                o_dgn[...] = jnp.zeros_like(o_dgn)
            o_dgn[...] += dg
        if has_post:
            dm, dg = _rms_bwd(m_ref[...], gp_ref[...], cur)
            o_dm[...] = dm.astype(o_dm.dtype)

            @pl.when(i == 0)
            def _():
                o_dgp[...] = jnp.zeros_like(o_dgp)
            o_dgp[...] += dg

    dhs_spec = _row_spec(d) if dhs_cols is None else pl.BlockSpec((ROW_TILE, d), lambda i: (i, dhs_cols[1]))
    ins, in_specs, out_specs, out_shape = [dhs], [dhs_spec], [], []
    if has_pre:
        ins += list(pre)
        in_specs += [_row_spec(d), _vec_spec(d), _row_spec(d)]
        out_specs += [_row_spec(d), _vec_spec(d)]
        out_shape += [jax.ShapeDtypeStruct((r, d), F32), jax.ShapeDtypeStruct((1, d), F32)]
    if has_post:
        ins += list(post)
        in_specs += [_row_spec(d), _vec_spec(d)]
        out_specs += [_row_spec(d), _vec_spec(d)]
        out_shape += [jax.ShapeDtypeStruct((r, d), dm_dtype), jax.ShapeDtypeStruct((1, d), F32)]
    outs = list(_pallas(body, name=name, grid=(r // ROW_TILE,), in_specs=in_specs, out_specs=out_specs,
                        out_shape=out_shape, compiler_params=_cparams(("arbitrary",)))(*ins))
    dhs_new, dgn = (outs.pop(0), outs.pop(0)) if has_pre else (dhs, None)
    dm, dgp = (outs.pop(0), outs.pop(0)) if has_post else (None, None)
    return dhs_new, dm, dgn, dgp


def _dgrad_norm_bwd(dy, w, dhs, hs, g_pre, *, post=None, add=None, pad=0, name):
    d, n = w.shape
    r = dy.shape[0]
    assert n <= MM_K_CAP and d == dhs.shape[1]
    t = ROW_TILE
    has_post, has_add = post is not None, add is not None
    dims = ((1,), (1,))

    def body(*refs):
        it = iter(refs)
        dy_ref, w_ref = next(it), next(it)
        add_ref = next(it) if has_add else None
        dhs_ref, hs_ref, gn_ref = next(it), next(it), next(it)
        if has_post:
            m_ref, gp_ref = next(it), next(it)
        o_dhs, o_dgn = next(it), next(it)
        if has_post:
            o_dm, o_dgp = next(it), next(it)
        i = pl.program_id(0)
        dhn = _dot(dy_ref[...].astype(BF16), w_ref[...].astype(BF16), dims)
        if has_add:
            dhn = dhn + add_ref[...]
        live = (i * t + lax.broadcasted_iota(jnp.int32, (t, 1), 0)) >= pad
        dx, dg = _rms_bwd(hs_ref[...], gn_ref[...], jnp.where(live, dhn, 0.0))
        cur = jnp.where(live, dhs_ref[...], 0.0) + dx
        o_dhs[...] = cur

        @pl.when(i == 0)
        def _():
            o_dgn[...] = jnp.zeros_like(o_dgn)
        o_dgn[...] += dg
        if has_post:
            dm, dg = _rms_bwd(m_ref[...], gp_ref[...], cur)
            o_dm[...] = dm.astype(BF16)

            @pl.when(i == 0)
            def _():
                o_dgp[...] = jnp.zeros_like(o_dgp)
            o_dgp[...] += dg

    ins = [dy, w] + ([add] if has_add else []) + [dhs, hs, g_pre] + (list(post) if has_post else [])
    in_specs = ([_row_spec(n, t), pl.BlockSpec((d, n), lambda i: (0, 0))] + ([_row_spec(d, t)] if has_add else [])
                + [_row_spec(d, t), _row_spec(d, t), _vec_spec(d)] + ([_row_spec(d, t), _vec_spec(d)] if has_post else []))
    out_specs = [_row_spec(d, t), _vec_spec(d)] + ([_row_spec(d, t), _vec_spec(d)] if has_post else [])
    out_shape = [jax.ShapeDtypeStruct((r, d), F32), jax.ShapeDtypeStruct((1, d), F32)]
    if has_post:
        out_shape += [jax.ShapeDtypeStruct((r, d), BF16), jax.ShapeDtypeStruct((1, d), F32)]
    outs = list(_pallas(body, name=name, grid=(r // t,), in_specs=in_specs, out_specs=out_specs,
                        out_shape=out_shape, compiler_params=_cparams(("arbitrary",)))(*ins))
    return (outs[0], outs[2], outs[1], outs[3]) if has_post else (outs[0], None, outs[1], None)


def _softplus(z):
    return jnp.maximum(z, 0.0) + jnp.log(1.0 + jnp.exp(-jnp.abs(z)))


def _sb_consts(t):
    row = lax.broadcasted_iota(jnp.int32, (t, t), 0)
    col = lax.broadcasted_iota(jnp.int32, (t, t), 1)
    m_up = (col >= row).astype(BF16)
    m_low = (col <= row).astype(BF16)
    return m_up, m_low


def _emit_chains(chains, stages, skew):
    if skew:
        for step in range(len(chains) + len(stages) - 1):
            for si, stage in enumerate(stages):
                if 0 <= step - si < len(chains):
                    stage(chains[step - si])
    else:
        for stage in stages:
            for c in chains:
                stage(c)


def _sb_fwd(q, k, vt3, pad, name, ride=((), ())):
    r = q.shape[0]
    t = ATT_BLK
    nb = r // t
    nbp = -(-(nb + 1) // 8) * 8
    jmin = pad // t
    scale = SB_HEAD_DIM ** -0.5
    n_ride = len(ride[0])

    def body(q_ref, k_ref, vt_ref, *rest):
        ride_in, (o_ref, ss_ref), ride_out = rest[:n_ride], rest[n_ride:n_ride + 2], rest[n_ride + 2:2 * n_ride + 2]
        acc_ref, kn_ref = rest[2 * n_ride + 2:2 * n_ride + 4]
        ride_sems = rest[2 * n_ride + 4:]
        i = pl.program_id(1)
        if n_ride:
            @pl.when((pl.program_id(0) == 0) & (i == 0))
            def _():
                for cp in _exchange_copies(ride_in, ride_out, ride[1], *ride_sems):
                    cp.start()

        @pl.when(i == 0)
        def _():
            def blk(b, m):
                kb = k_ref[pl.ds(pl.multiple_of(b * t, t), t), :].astype(F32)
                return jnp.maximum(m, jnp.max(jnp.sum(kb * kb, axis=1, keepdims=True), axis=0, keepdims=True))
            kn_ref[...] = jnp.broadcast_to(lax.fori_loop(0, nb, blk, jnp.zeros((1, 1), F32)), (8, 128))

        qf = q_ref[...].astype(F32)
        z_bound = scale * jnp.sqrt(jnp.max(jnp.sum(qf * qf, axis=1, keepdims=True)) * jnp.max(kn_ref[...]))

        def need(carry):
            return jnp.maximum(jnp.max(carry[0]), jnp.max(carry[1])) + z_bound >= SB_LOG_ZERO

        qt = qf.T
        sub = lax.broadcasted_iota(jnp.int32, (128, 1), 0)
        m_up, _ = _sb_consts(t)
        kpos0 = lax.broadcasted_iota(jnp.int32, (t, 1), 0)
        qpos = i * t + lax.broadcasted_iota(jnp.int32, (1, t), 1)
        qths = [jnp.where((sub >= 64 * h) & (sub < 64 * (h + 1)), qt * scale, 0.0).astype(BF16) for h in range(2)]
        acc_ref[...] = jnp.zeros_like(acc_ref)

        def sweep(js, carry, masked):
            kbs = [k_ref[pl.ds(pl.multiple_of(j * t, t), t), :] for j in js]
            vts = [vt_ref[0, j] for j in js]
            accs = [acc_ref[0], acc_ref[1]]
            s = list(carry)
            chains = [(n, h) for n in range(len(js)) for h in range(2)]
            masked = [masked] * len(js) if isinstance(masked, bool) else masked
            valid = [(js[n] * t + kpos0 < qpos) & (js[n] * t + kpos0 >= pad) if masked[n] else None for n in range(len(js))]
            zt, inc, saves = {}, {}, []

            def st_scores(c):
                zt[c] = _dot(kbs[c[0]], qths[c[1]])

            def st_cumsum(c):
                lk = -_softplus(zt[c])
                if masked[c[0]]:
                    lk = jnp.where(valid[c[0]], lk, 0.0)
                inc[c] = _split_dot(m_up, lk)

            def st_weights(c):
                n, h = c
                saves.append((h, js[n], s[h]))
                w = jnp.exp(zt[c] + inc[c] + s[h])
                if masked[n]:
                    w = jnp.where(valid[n], w, 0.0)
                accs[h] = accs[h] + _dot(vts[n], w.astype(BF16))
                s[h] = s[h] + inc[c][0:1, :]

            _emit_chains(chains, [st_scores, st_cumsum, st_weights], SB_FWD_SKEW)
            for h, j, val in saves:
                ss_ref[h, 0, pl.ds(j, 1), :] = val
            acc_ref[0] = accs[0]
            acc_ref[1] = accs[1]
            return tuple(s)

        zero = jnp.zeros((1, t), F32)
        bpi = SB_BLOCKS_PER_TRIP
        j, carry = lax.cond(
            i - 1 > jmin,
            lambda: (i - 2, sweep([i, i - 1], (zero, zero), [True, False])),
            lambda: (i - 1, sweep([i], (zero, zero), True)))
        def further(j, carry):
            j, carry = lax.while_loop(
                lambda st: (st[0] - bpi >= jmin) & need(st[1]),
                lambda st: (st[0] - bpi, sweep([st[0] - b for b in range(bpi)], st[1], False)), (j, carry))
            j, carry = lax.while_loop(
                lambda st: (st[0] > jmin) & need(st[1]),
                lambda st: (st[0] - 1, sweep([st[0]], st[1], False)), (j, carry))
            return lax.while_loop(
                lambda st: (st[0] == jmin) & (i > jmin) & need(st[1]),
                lambda st: (st[0] - 1, sweep([st[0]], st[1], True)), (j, carry))[0]

        j = lax.cond((j >= jmin) & need(carry), lambda: further(j, carry), lambda: j)
        first = jnp.full((1, t), j + 1, jnp.int32).astype(F32)
        ss_ref[0, 0, nbp - 1:nbp, :] = first
        ss_ref[1, 0, nbp - 1:nbp, :] = first
        acc = jnp.where(sub < 64, acc_ref[0], acc_ref[1])
        o_ref[...] = acc.T
        if n_ride:
            @pl.when((pl.program_id(0) == 3) & (i == nb - 1))
            def _():
                for cp in _exchange_copies(ride_in, ride_out, ride[1], *ride_sems):
                    cp.wait()

    hbm = pl.BlockSpec(memory_space=pl.ANY)
    outs = _pallas(
        body, name=name, grid=(4, nb),
        in_specs=[pl.BlockSpec((t, 128), lambda hp, i: (i, hp)),
                  pl.BlockSpec((r, 128), lambda hp, i: (0, hp)),
                  pl.BlockSpec((1, nb, 128, t), lambda hp, i: (hp, 0, 0, 0))] + [hbm] * n_ride,
        out_specs=[pl.BlockSpec((t, 128), lambda hp, i: (i, hp)),
                   pl.BlockSpec((2, 1, nbp, t), lambda hp, i: (hp, i, 0, 0))] + [hbm] * n_ride,
        out_shape=[jax.ShapeDtypeStruct((r, SB_WIDTH), F32),
                   jax.ShapeDtypeStruct((8, nb, nbp, t), F32)] + _exchange_shapes(*ride),
        scratch_shapes=[pltpu.VMEM((2, 128, t), F32), pltpu.VMEM((8, 128), F32)] + (_exchange_sems(n_ride) if n_ride else []),
        compiler_params=_cparams(("arbitrary", "arbitrary")),
    )(q, k, vt3, *ride[0])
    return outs[0], outs[1], list(outs[2:])


def _sb_bwd(q, k, v, kt3, ssave, do, pad, name):
    r = q.shape[0]
    t = ATT_BLK
    nb = r // t
    nbp = ssave.shape[2]
    jmin = pad // t
    scale = SB_HEAD_DIM ** -0.5

    def body(q_ref, do_ref, k_ref, v_ref, kt_ref, ss_ref, dq_ref, dk_hbm, dv_hbm, dk_acc, dv_acc, dq_acc, sem):
        hp = pl.program_id(0)
        i = pl.program_id(1)

        @pl.when(i == 0)
        def _():
            dk_acc[...] = jnp.zeros_like(dk_acc)
            dv_acc[...] = jnp.zeros_like(dv_acc)

        qf = q_ref[...].astype(F32)
        dof = do_ref[...]
        qt = qf.T
        dot_ = dof.T
        sub = lax.broadcasted_iota(jnp.int32, (128, 1), 0)
        lane = lax.broadcasted_iota(jnp.int32, (1, 128), 1)
        m_up, m_low = _sb_consts(t)
        kpos0 = lax.broadcasted_iota(jnp.int32, (t, 1), 0)
        qpos = i * t + lax.broadcasted_iota(jnp.int32, (1, t), 1)
        first = jnp.clip(jnp.max(ss_ref[0, 0, nbp - 1:nbp, :]).astype(jnp.int32), jmin, i)
        mid0 = jnp.maximum(first, jmin + 1)
        pair = i - mid0 >= 1
        n_mid = jnp.maximum(i - mid0 - 1, 0)
        n_edge = jnp.where((i > jmin) & (first == jmin), 1, 0)
        in_t = [(sub >= 64 * h) & (sub < 64 * (h + 1)) for h in range(2)]
        in_l = [(lane >= 64 * h) & (lane < 64 * (h + 1)) for h in range(2)]
        qths = [jnp.where(in_t[h], qt * scale, 0.0).astype(BF16) for h in range(2)]
        doths = [jnp.where(in_t[h], dot_, 0.0).astype(BF16) for h in range(2)]
        qhs = [jnp.where(in_l[h], qf * scale, 0.0).astype(BF16) for h in range(2)]
        dohs = [jnp.where(in_l[h], dof, 0.0).astype(BF16) for h in range(2)]
        dq_acc[...] = jnp.zeros_like(dq_acc)

        def sweep(js, carry, masked):
            rows = [pl.ds(pl.multiple_of(j * t, t), t) for j in js]
            kbs = [k_ref[rw, :] for rw in rows]
            vbs = [v_ref[rw, :] for rw in rows]
            kts = [kt_ref[0, j] for j in js]
            sss = [[ss_ref[h, 0, pl.ds(j, 1), :] for h in range(2)] for j in js]
            dv_old = [dv_acc[rw, :] for rw in rows]
            dk_old = [dk_acc[rw, :] for rw in rows]
            dqs = [dq_acc[0], dq_acc[1]]
            ec = list(carry)
            chains = [(n, h) for n in range(len(js)) for h in range(2)]
            masked = [masked] * len(js) if isinstance(masked, bool) else masked
            valid = [(js[n] * t + kpos0 < qpos) & (js[n] * t + kpos0 >= pad) if masked[n] else None for n in range(len(js))]
            zt, dvt, sp, inc, e, big_e = {}, {}, {}, {}, {}, {}

            def st_scores(c):
                zt[c] = _dot(kbs[c[0]], qths[c[1]])
                dvt[c] = _dot(vbs[c[0]], doths[c[1]])

            def st_cumsum(c):
                sp[c] = _softplus(zt[c])
                lk = -sp[c]
                if masked[c[0]]:
                    lk = jnp.where(valid[c[0]], lk, 0.0)
                inc[c] = _split_dot(m_up, lk)

            def st_weights(c):
                n, h = c
                w = jnp.exp(zt[c] + inc[c] + sss[n][h])
                if masked[n]:
                    w = jnp.where(valid[n], w, 0.0)
                dv_old[n] = dv_old[n] + _dot(w.astype(BF16), dohs[h])
                e[c] = w * dvt[c]
                pinc = _split_dot(m_low, e[c])
                big_e[c] = pinc - e[c] + ec[h]
                ec[h] = ec[h] + pinc[t - 1:t, :]

            def st_dscores(c):
                n, h = c
                dz = e[c] - jnp.exp(zt[c] - sp[c]) * (e[c] + big_e[c])
                if masked[n]:
                    dz = jnp.where(valid[n], dz, 0.0)
                dzb = dz.astype(BF16)
                dqs[h] = dqs[h] + _dot(kts[n], dzb)
                dk_old[n] = dk_old[n] + _dot(dzb, qhs[h])

            _emit_chains(chains, [st_scores, st_cumsum, st_weights, st_dscores], SB_BWD_SKEW)
            for n, rw in enumerate(rows):
                dv_acc[rw, :] = dv_old[n]
                dk_acc[rw, :] = dk_old[n]
            dq_acc[0] = dqs[0]
            dq_acc[1] = dqs[1]
            return tuple(ec)

        zero = jnp.zeros((1, t), F32)
        bpi = SB_BLOCKS_PER_TRIP
        carry = lax.fori_loop(0, n_edge, lambda it, c: sweep([jmin + it * 0], c, True), (zero, zero))
        carry = lax.fori_loop(0, n_mid // bpi, lambda it, c: sweep([mid0 + bpi * it + b for b in range(bpi)], c, False), carry)
        n_rem = n_mid % bpi
        carry = lax.fori_loop(0, n_rem, lambda it, c: sweep([i - 1 - n_rem + it], c, False), carry)
        lax.cond(pair, lambda: sweep([i - 1, i], carry, [False, True]), lambda: sweep([i], carry, True))
        dq_ref[...] = (jnp.where(sub < 64, dq_acc[0], dq_acc[1]) * scale).T

        @pl.when(i == nb - 1)
        def _():
            lanes = pl.ds(pl.multiple_of(hp * 128, 128), 128)
            c1 = pltpu.make_async_copy(dk_acc, dk_hbm.at[:, lanes], sem.at[0])
            c2 = pltpu.make_async_copy(dv_acc, dv_hbm.at[:, lanes], sem.at[1])
            c1.start()
            c2.start()
            c1.wait()
            c2.wait()

    return _pallas(
        body, name=name, grid=(4, nb),
        in_specs=[pl.BlockSpec((t, 128), lambda hp, i: (i, hp)),
                  pl.BlockSpec((t, 128), lambda hp, i: (i, hp)),
                  pl.BlockSpec((r, 128), lambda hp, i: (0, hp)),
                  pl.BlockSpec((r, 128), lambda hp, i: (0, hp)),
                  pl.BlockSpec((1, nb, 128, t), lambda hp, i: (hp, 0, 0, 0)),
                  pl.BlockSpec((2, 1, nbp, t), lambda hp, i: (hp, i, 0, 0))],
        out_specs=[pl.BlockSpec((t, 128), lambda hp, i: (i, hp)),
                   pl.BlockSpec(memory_space=pl.ANY), pl.BlockSpec(memory_space=pl.ANY)],
        out_shape=[jax.ShapeDtypeStruct((r, SB_WIDTH), F32),
                   jax.ShapeDtypeStruct((r, SB_WIDTH), F32), jax.ShapeDtypeStruct((r, SB_WIDTH), F32)],
        scratch_shapes=[pltpu.VMEM((r, 128), F32), pltpu.VMEM((r, 128), F32), pltpu.VMEM((2, 128, t), F32),
                        pltpu.SemaphoreType.DMA((2,))],
        compiler_params=_cparams(("arbitrary", "arbitrary")),
    )(q, do, k, v, kt3, ssave)


def _s5_disc(lam_re, lam_im, logdt, btr, bti):
    lr = jnp.minimum(lam_re, -1e-4)
    li = lam_im
    dt = jnp.exp(logdt)
    mag = jnp.exp(lr * dt)
    ang = li * dt
    a_re, a_im = mag * jnp.cos(ang), mag * jnp.sin(ang)
    den = lr * lr + li * li
    nr, ni = a_re - 1.0, a_im
    c_re = (nr * lr + ni * li) / den
    c_im = (ni * lr - nr * li) / den
    return a_re, a_im, c_re * btr - c_im * bti, c_re * bti + c_im * btr


def _s5_prep(lam_re, lam_im, logdt, btr, bti, name):
    ns = lam_re.shape[1]

    def body(lr_ref, li_ref, dt_ref, br_ref, bi_ref, ar_ref, ai_ref, bbr_ref, bbi_ref):
        ar, ai, bbr, bbi = _s5_disc(lr_ref[...], li_ref[...], dt_ref[...], br_ref[...], bi_ref[...])
        ar_ref[...] = ar
        ai_ref[...] = ai
        bbr_ref[...] = bbr
        bbi_ref[...] = bbi

    return _pallas(body, name=name,
                   out_shape=[jax.ShapeDtypeStruct((1, ns), F32)] * 2 + [jax.ShapeDtypeStruct((S5_GROUP, ns), F32)] * 2,
                   )(lam_re, lam_im, logdt, btr, bti)


def _s5_prep_bwd(lam_re, lam_im, logdt, btr, bti, dar, dai, dbbr, dbbi, name):
    ns = lam_re.shape[1]

    def body(lr_ref, li_ref, dt_ref, br_ref, bi_ref, dar_ref, dai_ref, dbr_ref, dbi_ref, o_lr, o_li, o_dt, o_br, o_bi):
        _, vjp = jax.vjp(_s5_disc, lr_ref[...], li_ref[...], dt_ref[...], br_ref[...], bi_ref[...])
        g = vjp((dar_ref[...], dai_ref[...], dbr_ref[...], dbi_ref[...]))
        o_lr[...] = g[0]
        o_li[...] = g[1]
        row = lax.broadcasted_iota(jnp.int32, (ns, ns), 0) // S5_STATE
        col = lax.broadcasted_iota(jnp.int32, (ns, ns), 1) // S5_STATE
        same = (row == col).astype(F32)
        o_dt[...] = _dot_hi(jnp.broadcast_to(g[2], (8, ns)), same)[0:1]
        o_br[...] = g[3]
        o_bi[...] = g[4]

    return _pallas(body, name=name,
                   out_shape=[jax.ShapeDtypeStruct((1, ns), F32)] * 3 + [jax.ShapeDtypeStruct((S5_GROUP, ns), F32)] * 2,
                   compiler_params=pltpu.CompilerParams(vmem_limit_bytes=VMEM_LIMIT),
                   )(lam_re, lam_im, logdt, btr, bti, dar, dai, dbbr, dbbi)


def _s5_scan(br, bi, ar, ai, t, reverse=False, carry=None):
    ng = t // 8
    ns = br.shape[1]
    br, bi = br.reshape(ng, 8, ns), bi.reshape(ng, 8, ns)
    row8 = lax.broadcasted_iota(jnp.int32, (1, 8, 1), 1)
    pr, pi_ = ar, ai
    for k in (1, 2, 4):
        if reverse:
            sr, si, ok = pltpu.roll(br, 8 - k, 1), pltpu.roll(bi, 8 - k, 1), row8 < 8 - k
        else:
            sr, si, ok = pltpu.roll(br, k, 1), pltpu.roll(bi, k, 1), row8 >= k
        sr = jnp.where(ok, sr, 0.0)
        si = jnp.where(ok, si, 0.0)
        br, bi = br + pr * sr - pi_ * si, bi + pr * si + pi_ * sr
        pr, pi_ = pr * pr - pi_ * pi_, 2.0 * pr * pi_
    pw_r, pw_i = [ar], [ai]
    for _ in range(7):
        pw_r.append(pw_r[-1] * ar - pw_i[-1] * ai)
        pw_i.append(pw_r[-2] * ai + pw_i[-1] * ar)
    if reverse:
        pw_r.reverse()
        pw_i.reverse()
    p8r, p8i = jnp.concatenate(pw_r, axis=0), jnp.concatenate(pw_i, axis=0)
    out_r, out_i = [None] * ng, [None] * ng
    order = range(ng - 1, -1, -1) if reverse else range(ng)
    edge = 0 if reverse else 7
    for g in order:
        gr, gi = br[g], bi[g]
        if carry is not None:
            cr, ci = carry
            gr, gi = gr + p8r * cr - p8i * ci, gi + p8r * ci + p8i * cr
        out_r[g], out_i[g] = gr, gi
        carry = (gr[edge:edge + 1], gi[edge:edge + 1])
    return jnp.concatenate(out_r, axis=0), jnp.concatenate(out_i, axis=0)


def _s5_prev_rows(x, first, t):
    ng = t // 8
    ns = x.shape[1]
    x3 = x.reshape(ng, 8, ns)
    last = x3[:, 7:8, :]
    before = jnp.concatenate([first.reshape(1, 1, ns), last[:ng - 1]], axis=0)
    row8 = lax.broadcasted_iota(jnp.int32, (1, 8, 1), 1)
    return jnp.where(row8 == 0, before, pltpu.roll(x3, 1, 1)).reshape(t, ns)


_GELU_C = math.sqrt(2.0 / math.pi)


def _gelu(y):
    th = jnp.tanh(_GELU_C * (y + 0.044715 * y * y * y))
    return 0.5 * y * (1.0 + th), th


def _sigmoid(x):
    return 1.0 / (1.0 + jnp.exp(-x))


def _s5_fwd(u, wb, a, wc, dskip, wglu, bglu, gnorm, name):
    r = u.shape[0]
    t = S5_TILE
    nt = r // t
    ns = wb.shape[2]
    w = S5_WIDTH

    def body(u_ref, wb_ref, a_ref, wc_ref, d_ref, wg_ref, bg_ref, gn_ref, y_ref, on_ref, xs_ref, carry_ref):
        i = pl.program_id(0)
        ar, ai = a_ref[0], a_ref[1]

        @pl.when(i == 0)
        def _():
            carry_ref[...] = jnp.zeros_like(carry_ref)

        u_ = u_ref[...]
        ub = u_.astype(BF16)
        xs_ref[0] = carry_ref[:, 0, :]
        chunks = list(range(S5_CHUNKS))
        sl_s = [slice(c * (ns // S5_CHUNKS), (c + 1) * (ns // S5_CHUNKS)) for c in chunks]
        sl_u = [slice(c * (w // S5_CHUNKS), (c + 1) * (w // S5_CHUNKS)) for c in chunks]
        bu, xs, ys = {}, {}, {}

        def st_inputs(c):
            bu[c] = (_dot(ub[:, sl_u[c]], wb_ref[0, sl_u[c], sl_s[c]]), _dot(ub[:, sl_u[c]], wb_ref[1, sl_u[c], sl_s[c]]))

        def st_scan(c):
            xr, xi = _s5_scan(*bu[c], ar[:, sl_s[c]], ai[:, sl_s[c]], t, carry=(carry_ref[0, :, sl_s[c]], carry_ref[1, :, sl_s[c]]))
            carry_ref[0, :, sl_s[c]] = xr[t - 1:t, :]
            carry_ref[1, :, sl_s[c]] = xi[t - 1:t, :]
            xs[c] = (xr.astype(BF16), xi.astype(BF16))

        def st_outputs(c):
            ys[c] = _dot(xs[c][0], wc_ref[0, sl_s[c], sl_u[c]]) - _dot(xs[c][1], wc_ref[1, sl_s[c], sl_u[c]])

        _emit_chains(chunks, [st_inputs, st_scan, st_outputs], False)
        y = jnp.concatenate([ys[c] for c in chunks], axis=1) + d_ref[...] * u_
        h, _ = _gelu(y)
        gate = _sigmoid(_dot(h.astype(BF16), wg_ref[...]) + bg_ref[...])
        y_ref[...] = y
        on_ref[...] = _rms(h * gate, gn_ref[...]).astype(BF16)

    full = lambda shape: pl.BlockSpec(shape, lambda i: (0,) * len(shape))
    return _pallas(
        body, name=name, grid=(nt,),
        in_specs=[_row_spec(w, t), full((2, w, ns)), full((2, 1, ns)), full((2, ns, w)), full((1, w)),
                  full((w, w)), full((1, w)), full((1, w))],
        out_specs=[_row_spec(w, t), pl.BlockSpec((t, w), lambda i: (i, 1)), pl.BlockSpec((1, 2, ns), lambda i: (i, 0, 0))],
        out_shape=[jax.ShapeDtypeStruct((r, w), F32), jax.ShapeDtypeStruct((r, 2 * w), BF16),
                   jax.ShapeDtypeStruct((nt, 2, ns), F32)],
        scratch_shapes=[pltpu.VMEM((2, 1, ns), F32)],
        compiler_params=_cparams(("arbitrary",)),
    )(u, wb, a, wc, dskip, wglu, bglu, gnorm)


def _s5_bwd(u, y, don, xstart, wb, a, wc, dskip, wglu, bglu, gnorm, name, ride=((), ()), don_block=0):
    r = u.shape[0]
    t = S5_TILE
    nt = r // t
    ns = wb.shape[2]
    w = S5_WIDTH
    nt_dims = ((1,), (1,))
    tn_dims = ((0,), (0,))

    def body(u_ref, y_ref, don_ref, xs_ref, wb_hbm, a_ref, wc_hbm, d_ref, wg_ref, bg_ref, gn_ref,
             du_ref, da_ref, dd_ref, dbg_ref, dgn_ref, dwb_hbm, dwc_hbm, dwg_hbm,
             wb_ref, wc_ref, lam_ref, acc_wb, acc_wc, acc_wg, sem):
        i = pl.program_id(0)
        ar, ai = a_ref[0], a_ref[1]

        @pl.when(i == 0)
        def _():
            c1 = pltpu.make_async_copy(wb_hbm, wb_ref, sem.at[0])
            c2 = pltpu.make_async_copy(wc_hbm, wc_ref, sem.at[1])
            c1.start()
            c2.start()
            lam_ref[...] = jnp.zeros_like(lam_ref)
            acc_wb[...] = jnp.zeros_like(acc_wb)
            acc_wc[...] = jnp.zeros_like(acc_wc)
            acc_wg[...] = jnp.zeros_like(acc_wg)
            da_ref[...] = jnp.zeros_like(da_ref)
            dd_ref[...] = jnp.zeros_like(dd_ref)
            dbg_ref[...] = jnp.zeros_like(dbg_ref)
            dgn_ref[...] = jnp.zeros_like(dgn_ref)
            c1.wait()
            c2.wait()

        u_ = u_ref[...]
        y_ = y_ref[...]
        ub = u_.astype(BF16)
        h, th = _gelu(y_)
        hb = h.astype(BF16)
        wg = wg_ref[...]
        gate = _sigmoid(_dot(hb, wg) + bg_ref[...])
        d_out, dgn = _rms_bwd(h * gate, gn_ref[...], don_ref[...])
        dgn_ref[...] += dgn
        dhw = d_out * h * gate * (1.0 - gate)
        dhwb = dhw.astype(BF16)
        dh = d_out * gate + _dot(dhwb, wg, nt_dims)
        acc_wg[...] += _dot(hb, dhwb, tn_dims)
        dbg_ref[...] += jnp.sum(dhw, axis=0, keepdims=True)
        dgelu = 0.5 * (1.0 + th) + 0.5 * y_ * (1.0 - th * th) * _GELU_C * (1.0 + 3.0 * 0.044715 * y_ * y_)
        dy = dh * dgelu
        dd_ref[...] += jnp.sum(dy * u_, axis=0, keepdims=True)
        dyb = dy.astype(BF16)
        chunks = list(range(S5_CHUNKS))
        sl_s = [slice(c * (ns // S5_CHUNKS), (c + 1) * (ns // S5_CHUNKS)) for c in chunks]
        sl_u = [slice(c * (w // S5_CHUNKS), (c + 1) * (w // S5_CHUNKS)) for c in chunks]
        bu, gx, x_, lam, dus = {}, {}, {}, {}, {}

        def st_inputs(c):
            su, ss = sl_u[c], sl_s[c]
            bu[c] = (_dot(ub[:, su], wb_ref[0, su, ss]), _dot(ub[:, su], wb_ref[1, su, ss]))
            gx[c] = (_dot(dyb[:, su], wc_ref[0, ss, su], nt_dims), -_dot(dyb[:, su], wc_ref[1, ss, su], nt_dims))

        def st_states(c):
            su, ss = sl_u[c], sl_s[c]
            first = (xs_ref[0, 0:1, ss], xs_ref[0, 1:2, ss])
            xr, xi = _s5_scan(*bu[c], ar[:, ss], ai[:, ss], t, carry=first)
            acc_wc[0, ss, su] += _dot(xr.astype(BF16), dyb[:, su], tn_dims)
            acc_wc[1, ss, su] -= _dot(xi.astype(BF16), dyb[:, su], tn_dims)
            x_[c] = (_s5_prev_rows(xr, first[0], t), _s5_prev_rows(xi, first[1], t))

        def st_adjoint(c):
            su, ss = sl_u[c], sl_s[c]
            lr, li = _s5_scan(*gx[c], ar[:, ss], -ai[:, ss], t, reverse=True, carry=(lam_ref[0, :, ss], lam_ref[1, :, ss]))
            lam_ref[0, :, ss] = lr[0:1, :]
            lam_ref[1, :, ss] = li[0:1, :]
            lrb, lib = lr.astype(BF16), li.astype(BF16)
            acc_wb[0, su, ss] += _dot(ub[:, su], lrb, tn_dims)
            acc_wb[1, su, ss] += _dot(ub[:, su], lib, tn_dims)
            dus[c] = _dot(lrb, wb_ref[0, su, ss], nt_dims) + _dot(lib, wb_ref[1, su, ss], nt_dims)
            lam[c] = (lr, li)

        def st_decay(c):
            ss = sl_s[c]
            (lr, li), (xpr, xpi) = lam[c], x_[c]
            da_ref[0, :, ss] += jnp.sum(lr * xpr + li * xpi, axis=0, keepdims=True)
            da_ref[1, :, ss] += jnp.sum(li * xpr - lr * xpi, axis=0, keepdims=True)

        _emit_chains(chunks, [st_inputs, st_states, st_adjoint, st_decay], False)
        du_ref[...] = d_ref[...] * dy + jnp.concatenate([dus[c] for c in chunks], axis=1)

        @pl.when(i == nt - 1)
        def _():
            cps = [pltpu.make_async_copy(acc_wb, dwb_hbm, sem.at[0]), pltpu.make_async_copy(acc_wc, dwc_hbm, sem.at[1]),
                   pltpu.make_async_copy(acc_wg, dwg_hbm, sem.at[2])]
            for c in cps:
                c.start()
            for c in cps:
                c.wait()

    n_ride = len(ride[0])
    n_in, n_out, n_scratch = 11, 8, 7

    def body_with_ride(*refs):
        ins, rest = refs[:n_in], refs[n_in:]
        ride_in, rest = rest[:n_ride], rest[n_ride:]
        outs, rest = rest[:n_out], rest[n_out:]
        ride_out, rest = rest[:n_ride], rest[n_ride:]
        scratch, ride_sems = rest[:n_scratch], rest[n_scratch:]
        if n_ride:
            @pl.when(pl.program_id(0) == 0)
            def _():
                for cp in _exchange_copies(ride_in, ride_out, ride[1], *ride_sems):
                    cp.start()
        body(*ins, *outs, *scratch)
        if n_ride:
            @pl.when(pl.program_id(0) == nt - 1)
            def _():
                for cp in _exchange_copies(ride_in, ride_out, ride[1], *ride_sems):
                    cp.wait()

    rev = lambda i: (nt - 1 - i, 0)
    full = lambda shape: pl.BlockSpec(shape, lambda i: (0,) * len(shape))
    hbm = pl.BlockSpec(memory_space=pl.ANY)
    outs = _pallas(
        body_with_ride, name=name, grid=(nt,),
        in_specs=[pl.BlockSpec((t, w), rev), pl.BlockSpec((t, w), rev), pl.BlockSpec((t, w), lambda i: (nt - 1 - i, don_block)),
                  pl.BlockSpec((1, 2, ns), lambda i: (nt - 1 - i, 0, 0)), hbm, full((2, 1, ns)), hbm, full((1, w)),
                  full((w, w)), full((1, w)), full((1, w))] + [hbm] * n_ride,
        out_specs=[pl.BlockSpec((t, w), rev), full((2, 1, ns)), full((1, w)), full((1, w)), full((1, w)), hbm, hbm, hbm]
        + [hbm] * n_ride,
        out_shape=[jax.ShapeDtypeStruct((r, w), F32), jax.ShapeDtypeStruct((2, 1, ns), F32)]
        + [jax.ShapeDtypeStruct((1, w), F32)] * 3
        + [jax.ShapeDtypeStruct((2, w, ns), F32), jax.ShapeDtypeStruct((2, ns, w), F32), jax.ShapeDtypeStruct((w, w), F32)]
        + _exchange_shapes(*ride),
        scratch_shapes=[pltpu.VMEM((2, w, ns), BF16), pltpu.VMEM((2, ns, w), BF16), pltpu.VMEM((2, 1, ns), F32),
                        pltpu.VMEM((2, w, ns), F32), pltpu.VMEM((2, ns, w), F32), pltpu.VMEM((w, w), F32),
                        pltpu.SemaphoreType.DMA((3,))] + (_exchange_sems(n_ride) if n_ride else []),
        compiler_params=_cparams(("arbitrary",)),
    )(u, y, don, xstart, wb, a, wc, dskip, wglu, bglu, gnorm, *ride[0])
    return tuple(outs[:n_out]) + (list(outs[n_out:]),)


def _s5_expand(lam_re, lam_im, log_dt, b_re, b_im, c_re, c_im):
    g, n, p = S5_GROUPS, S5_STATE, S5_GROUP
    ns = g * n
    rows = lambda x: x.reshape(1, ns)
    logdt = jnp.repeat(log_dt.reshape(g), n).reshape(1, ns)
    btr = b_re.reshape(ns, p).T
    bti = b_im.reshape(ns, p).T
    ctr = c_re.transpose(0, 2, 1).reshape(ns, p)
    cti = c_im.transpose(0, 2, 1).reshape(ns, p)
    mask = (jnp.arange(g * p)[:, None] // p) == (jnp.arange(ns)[None, :] // n)
    return rows(lam_re), rows(lam_im), logdt, btr, bti, ctr, cti, mask


def _s5_block_diag_b(bb, mask):
    return jnp.where(mask, jnp.tile(bb, (S5_GROUPS, 1)), 0.0)


def _s5_block_diag_c(ct, mask):
    return jnp.where(mask.T, jnp.tile(ct, (1, S5_GROUPS)), 0.0)


def _s5_diag_of_b(dwb, mask):
    return jnp.where(mask, dwb, 0.0).reshape(S5_GROUPS, S5_GROUP, -1).sum(0)


def _s5_diag_of_c(dwc, mask):
    ns = dwc.shape[0]
    return jnp.where(mask.T, dwc, 0.0).reshape(ns, S5_GROUPS, S5_GROUP).sum(1)


DN_PRE_TILE = 256
_DN_QKV = 3 * DN_WIDTH


def _halo_specs(width, tile, nt, prev):
    per = tile // 8
    if prev:
        return pl.BlockSpec((8, width), lambda i: (jnp.maximum(i * per - 1, 0), 0))
    return pl.BlockSpec((8, width), lambda i: (jnp.minimum((i + 1) * per, nt * per - 1), 0))


def _shift_down(x, halo, s, t):
    xx = jnp.concatenate([halo, x], axis=0)
    return pltpu.roll(xx, s, 0)[8:]


def _shift_up(x, halo, s, t):
    xx = jnp.concatenate([x, halo], axis=0)
    return pltpu.roll(xx, t + 8 - s, 0)[:t]


def _silu(x):
    s = _sigmoid(x)
    return x * s, s


def _dn_gates(ab, alog, dtb, live):
    lane = lax.broadcasted_iota(jnp.int32, (1, 128), 1)
    g = -jnp.exp(alog) * _softplus(ab + dtb)
    beta = _sigmoid(ab)
    return jnp.where(live & (lane < DN_HEADS), g, jnp.where(live & (lane < 2 * DN_HEADS), beta, 0.0))


def _dn_pre_fwd(proj, ab, conv_w, alog, dtb, pad, name):
    r = proj.shape[0]
    t = DN_PRE_TILE
    nt = r // t
    scale = DN_HEAD_DIM ** -0.5

    def body(x_ref, halo_ref, ab_ref, w_ref, al_ref, dt_ref, q_ref, k_ref, v_ref, gb_ref):
        i = pl.program_id(0)
        act, _ = _silu(_dn_conv(x_ref[...], jnp.where(i > 0, halo_ref[...], 0.0), w_ref[...], t))
        for hd in range(DN_HEADS):
            sl = slice(hd * 128, (hd + 1) * 128)
            for base, o_ref, sc in ((0, q_ref, scale), (DN_WIDTH, k_ref, 1.0)):
                xh = act[:, base + hd * 128: base + (hd + 1) * 128]
                o_ref[:, sl] = (xh * (lax.rsqrt(jnp.sum(xh * xh, axis=-1, keepdims=True) + EPS) * sc)).astype(BF16)
        v_ref[...] = act[:, 2 * DN_WIDTH:].astype(BF16)
        rows = i * t + lax.broadcasted_iota(jnp.int32, (t, 1), 0)
        gb_ref[...] = _dn_gates(ab_ref[...], al_ref[...], dt_ref[...], rows >= pad)

    return _pallas(
        body, name=name, grid=(nt,),
        in_specs=[pl.BlockSpec((t, _DN_QKV), lambda i: (i, 0)), _halo_specs(_DN_QKV, t, nt, True), _row_spec(128, t),
                  pl.BlockSpec((DN_CONV, _DN_QKV), lambda i: (0, 0)), _vec_spec(128), _vec_spec(128)],
        out_specs=[_row_spec(DN_WIDTH, t), _row_spec(DN_WIDTH, t), _row_spec(DN_WIDTH, t), _row_spec(128, t)],
        out_shape=[jax.ShapeDtypeStruct((r, DN_WIDTH), BF16)] * 3 + [jax.ShapeDtypeStruct((r, 128), F32)],
        compiler_params=_cparams(("parallel",)),
    )(proj, proj, ab, conv_w, alog, dtb)


def _dn_conv(x, halo, w, t):
    co = w[DN_CONV - 1:DN_CONV] * x
    for tap in range(DN_CONV - 1):
        co = co + w[tap:tap + 1] * _shift_down(x, halo, DN_CONV - 1 - tap, t)
    return co


def _dn_pre_bwd(proj, conv_w, dq, dk, dv, dgb, ab, alog, dtb, pad, name):
    r = proj.shape[0]
    t = DN_PRE_TILE
    nt = r // t
    scale = DN_HEAD_DIM ** -0.5

    def body(x_ref, halo_ref, w_ref, dq_ref, dk_ref, dv_ref, dgb_ref, ab_ref, al_ref, dt_ref, dco_ref, dab_ref, dal_ref,
             ddt_ref):
        i = pl.program_id(0)

        @pl.when(i == 0)
        def _():
            dal_ref[...] = jnp.zeros_like(dal_ref)
            ddt_ref[...] = jnp.zeros_like(ddt_ref)

        co_ = _dn_conv(x_ref[...], jnp.where(i > 0, halo_ref[...], 0.0), w_ref[...], t)
        act, sg = _silu(co_)
        dsilu = sg * (1.0 + co_ * (1.0 - sg))
        for hd in range(DN_HEADS):
            sl = slice(hd * 128, (hd + 1) * 128)
            for base, d_ref, sc in ((0, dq_ref, scale), (DN_WIDTH, dk_ref, 1.0)):
                cs = slice(base + hd * 128, base + (hd + 1) * 128)
                xh = act[:, cs]
                rn = lax.rsqrt(jnp.sum(xh * xh, axis=-1, keepdims=True) + EPS)
                xhat = xh * rn
                dy = d_ref[:, sl]
                dx = (sc * rn) * (dy - xhat * jnp.sum(dy * xhat, axis=-1, keepdims=True))
                dco_ref[:, cs] = dx * dsilu[:, cs]
        dco_ref[:, 2 * DN_WIDTH:] = dv_ref[...] * dsilu[:, 2 * DN_WIDTH:]
        rows = i * t + lax.broadcasted_iota(jnp.int32, (t, 1), 0)
        live = rows >= pad
        lane = lax.broadcasted_iota(jnp.int32, (1, 128), 1)
        ab_ = ab_ref[...]
        dgb_ = dgb_ref[...]
        is_g = live & (lane < DN_HEADS)
        is_b = live & (lane >= DN_HEADS) & (lane < 2 * DN_HEADS)
        arg = ab_ + dt_ref[...]
        ea = jnp.exp(al_ref[...])
        da = jnp.where(is_g, -dgb_ * ea * _sigmoid(arg), 0.0)
        beta = _sigmoid(ab_)
        dab_ref[...] = (da + jnp.where(is_b, dgb_ * beta * (1.0 - beta), 0.0)).astype(BF16)
        ddt_ref[...] += jnp.sum(da, axis=0, keepdims=True)
        dal_ref[...] += jnp.sum(jnp.where(is_g, -dgb_ * ea * _softplus(arg), 0.0), axis=0, keepdims=True)

    return _pallas(
        body, name=name, grid=(nt,),
        in_specs=[pl.BlockSpec((t, _DN_QKV), lambda i: (i, 0)), _halo_specs(_DN_QKV, t, nt, True),
                  pl.BlockSpec((DN_CONV, _DN_QKV), lambda i: (0, 0)),
                  _row_spec(DN_WIDTH, t), _row_spec(DN_WIDTH, t), _row_spec(DN_WIDTH, t),
                  _row_spec(128, t), _row_spec(128, t), _vec_spec(128), _vec_spec(128)],
        out_specs=[_row_spec(_DN_QKV, t), _row_spec(128, t), _vec_spec(128), _vec_spec(128)],
        out_shape=[jax.ShapeDtypeStruct((r, _DN_QKV), F32), jax.ShapeDtypeStruct((r, 128), BF16),
                   jax.ShapeDtypeStruct((1, 128), F32), jax.ShapeDtypeStruct((1, 128), F32)],
        compiler_params=_cparams(("arbitrary",)),
    )(proj, proj, conv_w, dq, dk, dv, dgb, ab, alog, dtb)


def _dn_conv_bwd(dco, proj, conv_w, dproj, name):
    r = dco.shape[0]
    t = DN_PRE_TILE
    nt = r // t

    def body(d_ref, dh_ref, x_ref, xh_ref, w_ref, dproj_in, dx_ref, dw_ref):
        i = pl.program_id(0)

        @pl.when(i == 0)
        def _():
            dw_ref[...] = jnp.zeros_like(dw_ref)

        d = d_ref[...]
        dhalo = jnp.where(i < nt - 1, dh_ref[...], 0.0)
        x = x_ref[...]
        xhalo = jnp.where(i > 0, xh_ref[...], 0.0)
        w = w_ref[...]
        dx = w[3:4] * d
        dws = [None] * DN_CONV
        dws[3] = jnp.sum(d * x, axis=0, keepdims=True)
        for tap in range(DN_CONV - 1):
            s = DN_CONV - 1 - tap
            dx = dx + w[tap:tap + 1] * _shift_up(d, dhalo, s, t)
            dws[tap] = jnp.sum(d * _shift_down(x, xhalo, s, t), axis=0, keepdims=True)
        dx_ref[...] = dx.astype(BF16)
        dw_ref[...] += jnp.concatenate(dws + [jnp.zeros((8 - DN_CONV, _DN_QKV), F32)], axis=0)

    return _pallas(
        body, name=name, grid=(nt,),
        in_specs=[_row_spec(_DN_QKV, t), _halo_specs(_DN_QKV, t, nt, False),
                  pl.BlockSpec((t, _DN_QKV), lambda i: (i, 0)), _halo_specs(_DN_QKV, t, nt, True),
                  pl.BlockSpec((DN_CONV, _DN_QKV), lambda i: (0, 0)), pl.BlockSpec(memory_space=pl.ANY)],
        out_specs=[_row_spec(_DN_QKV, t), pl.BlockSpec((8, _DN_QKV), lambda i: (0, 0))],
        out_shape=[jax.ShapeDtypeStruct(dproj.shape, BF16), jax.ShapeDtypeStruct((8, _DN_QKV), F32)],
        input_output_aliases={5: 0},
        compiler_params=_cparams(("arbitrary",)),
    )(dco, dco, proj, proj, conv_w, dproj)


def _split3(x):
    hi = x.astype(BF16)
    return hi, (x - hi.astype(F32)).astype(BF16)


def _dot3s(a, b, dims=((1,), (0,))):
    return _dot(a[0], b[0], dims) + (_dot(a[0], b[1], dims) + _dot(a[1], b[0], dims))


def _dot2s(a, b, dims=((1,), (0,))):
    return _dot(a[0], b[0], dims) + _dot(a[1], b[0], dims)


def _dot3(a, b, dims=((1,), (0,))):
    return _dot3s(_split3(a), _split3(b), dims)


def _dn_inverse_many(n_mats):
    c = n_mats[0].shape[0]
    row = lax.broadcasted_iota(jnp.int32, (c, c), 0)
    col = lax.broadcasted_iota(jnp.int32, (c, c), 1)
    eye = (row == col).astype(F32)
    same = row // DN_SUB == col // DN_SUB
    nds = [jnp.where(same, n, 0.0) for n in n_mats]
    nos = [n - nd for n, nd in zip(n_mats, nds)]

    def geometric(bs, order):
        xs = [eye + b for b in bs]
        sp = [_split3(b) for b in bs]
        k = 2
        while k < order:
            sp = [_split3(_dot2s(s_, s_)) for s_ in sp]
            xs = [x + _dot2s(_split3(x), s_) for x, s_ in zip(xs, sp)]
            k *= 2
        return xs

    tds = [_split3(td) for td in geometric([-nd for nd in nds], DN_SUB)]
    ms = [_dot3s(td, _split3(no)) for td, no in zip(tds, nos)]
    xs = geometric([-m for m in ms], c // DN_SUB)
    return [_dot3s(_split3(x), td) for x, td in zip(xs, tds)]


def _dn_chunk_shared(gb_ref, gbt_ref):
    c = DN_CHUNK
    row = lax.broadcasted_iota(jnp.int32, (c, c), 0)
    col = lax.broadcasted_iota(jnp.int32, (c, c), 1)
    gbv = gb_ref[...]
    gam_all = _split_dot((row >= col).astype(BF16), gbv)
    hi, lo = _split3(gbt_ref[...])
    tri_t = (row <= col).astype(BF16)
    return dict(row=row, col=col, gbv=gbv, gam_all=gam_all, gam_rows=_dot(hi, tri_t) + _dot(lo, tri_t),
                lane=lax.broadcasted_iota(jnp.int32, (1, 128), 1))


def _dn_chunk_common(q, k, v, sh, h):
    c = DN_CHUNK
    row, col, lane = sh["row"], sh["col"], sh["lane"]
    q, k, v = q.astype(F32), k.astype(F32), v.astype(F32)
    gam = jnp.sum(jnp.where(lane == h, sh["gam_all"], 0.0), axis=1, keepdims=True)
    beta = jnp.sum(jnp.where(lane == h + DN_HEADS, sh["gbv"], 0.0), axis=1, keepdims=True)
    gam_row = sh["gam_rows"][h:h + 1]
    dec = jnp.where(row >= col, jnp.exp(jnp.minimum(gam - gam_row, 0.0)), 0.0)
    kb, qb = k.astype(BF16), q.astype(BF16)
    nt_dims = ((1,), (1,))
    kk = _dot(kb, kb, nt_dims)
    qk = _dot(qb, kb, nt_dims)
    eg = jnp.exp(gam)
    gam_l = gam[c - 1:c, :]
    return dict(q=q, k=k, v=v, qb=qb, kb=kb, gam=gam, beta=beta, dec=dec, kk=kk, qk=qk, eg=eg, gam_l=gam_l,
                row=row, col=col, lane=lane, att=qk * dec, qg=q * eg, kt=k * jnp.exp(gam_l - gam),
                rhs=jnp.concatenate([v * beta, k * (beta * eg)], axis=1))


def _dn_fwd(q, k, v, gb, gbt, name):
    r = q.shape[0]
    c = DN_CHUNK
    nc = r // c
    dh = DN_HEAD_DIM
    tn_dims = ((0,), (0,))

    def body(q_ref, k_ref, v_ref, gb_ref, gbt_ref, o_ref, ss_ref, ts_ref, s_ref):
        @pl.when(pl.program_id(0) == 0)
        def _():
            s_ref[...] = jnp.zeros_like(s_ref)

        heads = list(range(DN_HEADS))
        sl = [slice(h * dh, (h + 1) * dh) for h in heads]
        sh = _dn_chunk_shared(gb_ref, gbt_ref)
        zs = [_dn_chunk_common(q_ref[:, sl[h]], k_ref[:, sl[h]], v_ref[:, sl[h]], sh, h) for h in heads]
        t_invs = _dn_inverse_many([jnp.where(sh["row"] > sh["col"], z["beta"] * z["kk"] * z["dec"], 0.0) for z in zs])
        sols = [_dot3(t_inv, z["rhs"]) for t_inv, z in zip(t_invs, zs)]
        ss = [s_ref[h] for h in heads]
        sbs = [s.astype(BF16) for s in ss]
        vnbs = [(sol[:, :dh] - _dot(sol[:, dh:].astype(BF16), sb)).astype(BF16) for sol, sb in zip(sols, sbs)]
        for h in heads:
            o_ref[:, sl[h]] = _dot(zs[h]["qg"].astype(BF16), sbs[h]) + _dot(zs[h]["att"].astype(BF16), vnbs[h])
        for h in heads:
            ss_ref[0, h] = ss[h]
            ts_ref[0, h] = t_invs[h]
            s_ref[h] = ss[h] * jnp.exp(zs[h]["gam_l"]) + _dot(zs[h]["kt"].astype(BF16), vnbs[h], tn_dims)

    blk = pl.BlockSpec((c, DN_WIDTH), lambda ci: (ci, 0))
    sav = pl.BlockSpec((1, DN_HEADS, dh, dh), lambda ci: (ci, 0, 0, 0))
    return _pallas(
        body, name=name, grid=(nc,),
        in_specs=[blk, blk, blk, pl.BlockSpec((c, 128), lambda ci: (ci, 0)), pl.BlockSpec((16, c), lambda ci: (0, ci))],
        out_specs=[blk, sav, sav],
        out_shape=[jax.ShapeDtypeStruct((r, DN_WIDTH), F32), jax.ShapeDtypeStruct((nc, DN_HEADS, dh, dh), F32),
                   jax.ShapeDtypeStruct((nc, DN_HEADS, dh, dh), F32)],
        scratch_shapes=[pltpu.VMEM((DN_HEADS, dh, dh), F32)],
        compiler_params=_cparams(("arbitrary",)),
    )(q, k, v, gb, gbt)


def _dn_bwd(q, k, v, gb, gbt, ssave, tsave, do, name):
    r = q.shape[0]
    c = DN_CHUNK
    nc = r // c
    dh = DN_HEAD_DIM
    nt_dims = ((1,), (1,))
    tn_dims = ((0,), (0,))

    def body(q_ref, k_ref, v_ref, gb_ref, gbt_ref, ss_ref, ts_ref, do_ref, dq_ref, dk_ref, dv_ref, dgb_ref, ds_ref):
        @pl.when(pl.program_id(0) == 0)
        def _():
            ds_ref[...] = jnp.zeros_like(ds_ref)

        heads = list(range(DN_HEADS))
        sl = [slice(h * dh, (h + 1) * dh) for h in heads]
        sh = _dn_chunk_shared(gb_ref, gbt_ref)
        row, col, lane = sh["row"], sh["col"], sh["lane"]
        rs = lambda x: jnp.sum(x, axis=1, keepdims=True)
        tot = lambda x: jnp.sum(rs(x), axis=0, keepdims=True)
        st = [dict() for _ in heads]
        dgb_parts = []

        def s_common(h):
            st[h].update(_dn_chunk_common(q_ref[:, sl[h]], k_ref[:, sl[h]], v_ref[:, sl[h]], sh, h))
            st[h]["t"] = _split3(ts_ref[0, h])

        def s_sol(h):
            st[h]["sol"] = _dot3s(st[h]["t"], _split3(st[h]["rhs"]))

        def s_state(h):
            z = st[h]
            sol = z["sol"]
            kcd = sol[:, dh:]
            s = ss_ref[0, h]
            sb = s.astype(BF16)
            vnb = (sol[:, :dh] - _dot(kcd.astype(BF16), sb)).astype(BF16)
            ds_next = ds_ref[h]
            dsb = ds_next.astype(BF16)
            dob = do_ref[:, sl[h]].astype(BF16)
            z["dqg"] = _dot(dob, sb, nt_dims)
            ds = _dot(z["qg"].astype(BF16), dob, tn_dims)
            z["d_att"] = jnp.where(row >= col, _dot(dob, vnb, nt_dims), 0.0)
            dvn = _dot(z["att"].astype(BF16), dob, tn_dims) + _dot(z["kt"].astype(BF16), dsb)
            z["dkt"] = _dot(vnb, dsb, nt_dims)
            eg_l = jnp.exp(z["gam_l"])
            ds = ds + ds_next * eg_l
            z["dgam_l"] = tot(ds_next * s) * eg_l
            dvnb = dvn.astype(BF16)
            dkcd = -_dot(dvnb, sb, nt_dims)
            ds_ref[h] = ds - _dot(kcd.astype(BF16), dvnb, tn_dims)
            z["dsol"] = jnp.concatenate([dvn, dkcd], axis=1)

        def s_drhs(h):
            st[h]["drhs"] = _dot3s(st[h]["t"], _split3(st[h]["dsol"]), tn_dims)

        def s_dn(h):
            z = st[h]
            z["dn"] = jnp.where(row > col, -_dot3(z["drhs"], z["sol"], nt_dims), 0.0)

        def s_rest(h):
            z = st[h]
            k_, v_, kb, qb = z["k"], z["v"], z["kb"], z["qb"]
            beta, eg, dec, kk, qk, gam, gam_l = z["beta"], z["eg"], z["dec"], z["kk"], z["qk"], z["gam"], z["gam_l"]
            dn, d_att, dqg, dkt = z["dn"], z["d_att"], z["dqg"], z["dkt"]
            drv, drk = z["drhs"][:, :dh], z["drhs"][:, dh:]
            s_rkk = rs(drk * k_)
            dv_ref[:, sl[h]] = drv * beta
            dbeta = rs(drv * v_) + s_rkk * eg + rs(dn * kk * dec)
            dk = drk * (beta * eg)
            dgam = s_rkk * beta * eg
            dkk = (dn * beta * dec).astype(BF16)
            dd = dn * beta * kk + d_att * qk
            dqk = (d_att * dec).astype(BF16)
            dq_ref[:, sl[h]] = _dot(dqk, kb) + dqg * eg
            dk = dk + _dot(dqk, qb, tn_dims) + _dot(dkk, kb) + _dot(dkk, kb, tn_dims)
            w = dd * dec
            wh, wl = _split3(w)
            ones = jnp.ones((c, 128), BF16)
            col_sum = (_dot(wh, ones, tn_dims) + _dot(wl, ones, tn_dims))[:, 0:1]
            dgam = dgam + rs(w) - col_sum + rs(dqg * z["qg"]) - rs(dkt * z["kt"])
            dk_ref[:, sl[h]] = dk + dkt * jnp.exp(gam_l - gam)
            dgam_l = z["dgam_l"] + tot(dkt * z["kt"])
            rowc = lax.broadcasted_iota(jnp.int32, (c, 1), 0)
            dgam = dgam + jnp.where(rowc == c - 1, dgam_l, 0.0)
            dg = _split_dot((row <= col).astype(BF16), jnp.broadcast_to(dgam, (c, 128)))[:, 0:1]
            dgb_parts.append(jnp.where(lane == h, dg, 0.0) + jnp.where(lane == h + DN_HEADS, dbeta, 0.0))

        _emit_chains(heads, [s_common, s_sol, s_state, s_drhs, s_dn, s_rest], False)
        dgb = dgb_parts[0]
        for part in dgb_parts[1:]:
            dgb = dgb + part
        dgb_ref[...] = dgb

    blk = pl.BlockSpec((c, DN_WIDTH), lambda ci: (nc - 1 - ci, 0))
    sav = pl.BlockSpec((1, DN_HEADS, dh, dh), lambda ci: (nc - 1 - ci, 0, 0, 0))
    gspec = pl.BlockSpec((c, 128), lambda ci: (nc - 1 - ci, 0))
    return _pallas(
        body, name=name, grid=(nc,),
        in_specs=[blk, blk, blk, gspec, pl.BlockSpec((16, c), lambda ci: (0, nc - 1 - ci)), sav, sav, blk],
        out_specs=[blk, blk, blk, gspec],
        out_shape=[jax.ShapeDtypeStruct((r, DN_WIDTH), F32)] * 3 + [jax.ShapeDtypeStruct((r, 128), F32)],
        scratch_shapes=[pltpu.VMEM((DN_HEADS, dh, dh), F32)],
        compiler_params=_cparams(("arbitrary",)),
    )(q, k, v, gb, gbt, ssave, tsave, do)


def _dn_post_fwd(o, proj, g, name):
    r = o.shape[0]

    def body(o_ref, z_ref, g_ref, y_ref):
        g_ = g_ref[...]
        for hd in range(DN_HEADS):
            sl = slice(hd * 128, (hd + 1) * 128)
            sz, _ = _silu(z_ref[:, sl])
            y_ref[:, sl] = (_rms(o_ref[:, sl], g_) * sz).astype(BF16)

    return _pallas(body, name=name, grid=(r // ROW_TILE,),
                   in_specs=[_row_spec(DN_WIDTH), pl.BlockSpec((ROW_TILE, DN_WIDTH), lambda i: (i, 3)), _vec_spec(128)],
                   out_specs=_row_spec(DN_WIDTH), out_shape=jax.ShapeDtypeStruct((r, DN_WIDTH), BF16),
                   compiler_params=_cparams(("parallel",)))(o, proj, g)


def _dn_post_bwd(o, proj, g, dy, name):
    r = o.shape[0]

    def body(o_ref, z_ref, g_ref, dy_ref, do_ref, dz_ref, dg_ref):
        @pl.when(pl.program_id(0) == 0)
        def _():
            dg_ref[...] = jnp.zeros_like(dg_ref)

        g_ = g_ref[...]
        for hd in range(DN_HEADS):
            sl = slice(hd * 128, (hd + 1) * 128)
            z_ = z_ref[:, sl]
            sz, sg = _silu(z_)
            dy_ = dy_ref[:, sl]
            o_ = o_ref[:, sl]
            dz_ref[:, sl] = (dy_ * _rms(o_, g_) * (sg * (1.0 + z_ * (1.0 - sg)))).astype(BF16)
            dx, dg = _rms_bwd(o_, g_, dy_ * sz)
            do_ref[:, sl] = dx
            dg_ref[...] += dg

    return _pallas(body, name=name, grid=(r // ROW_TILE,),
                   in_specs=[_row_spec(DN_WIDTH), pl.BlockSpec((ROW_TILE, DN_WIDTH), lambda i: (i, 3)), _vec_spec(128),
                             _row_spec(DN_WIDTH)],
                   out_specs=[_row_spec(DN_WIDTH), pl.BlockSpec((ROW_TILE, DN_WIDTH), lambda i: (i, 3)), _vec_spec(128)],
                   out_shape=[jax.ShapeDtypeStruct((r, DN_WIDTH), F32), jax.ShapeDtypeStruct((r, 4 * DN_WIDTH), BF16),
                              jax.ShapeDtypeStruct((1, 128), F32)],
                   compiler_params=_cparams(("arbitrary",)))(o, proj, g, dy)


def _exchange(arrays, scatter, name):
    n = len(arrays)

    def body(*refs):
        copies = _exchange_copies(refs[:n], refs[n:2 * n], scatter, *refs[2 * n:])
        for cp in copies:
            cp.start()
        for cp in copies:
            cp.wait()

    hbm = pl.BlockSpec(memory_space=pl.ANY)
    return _pallas(
        body, name=name, in_specs=[hbm] * n, out_specs=[hbm] * n, out_shape=_exchange_shapes(arrays, scatter),
        scratch_shapes=_exchange_sems(n),
    )(*arrays)


def _exchange_shapes(arrays, scatter):
    return [jax.ShapeDtypeStruct((N_DEV,) + (a.shape[1:] if sc else a.shape), a.dtype) for a, sc in zip(arrays, scatter)]


def _exchange_sems(n):
    return [pltpu.SemaphoreType.DMA((n * N_DEV,)), pltpu.SemaphoreType.DMA((n * N_DEV,)), pltpu.SemaphoreType.DMA((n,))]


def _exchange_copies(in_refs, out_refs, scatter, send_sems, recv_sems, local_sems):
    mx, my, mc = lax.axis_index("x"), lax.axis_index("y"), lax.axis_index("c")
    me = 4 * mx + 2 * my + mc
    copies = []
    for a in range(len(in_refs)):
        src_own = in_refs[a].at[me] if scatter[a] else in_refs[a]
        copies.append(pltpu.make_async_copy(src_own, out_refs[a].at[me], local_sems.at[a]))
        for kbits in range(1, N_DEV):
            px = lax.rem(mx + ((kbits >> 2) & 1), 2)
            py = lax.rem(my + ((kbits >> 1) & 1), 2)
            pc = lax.rem(mc + (kbits & 1), 2)
            src = in_refs[a].at[4 * px + 2 * py + pc] if scatter[a] else in_refs[a]
            copies.append(pltpu.make_async_remote_copy(
                src_ref=src, dst_ref=out_refs[a].at[me],
                send_sem=send_sems.at[a * N_DEV + kbits], recv_sem=recv_sems.at[a * N_DEV + kbits],
                device_id=(px, py, pc), device_id_type=pl.DeviceIdType.MESH))
    return copies


def _adamw(gstack, w, m, v, name):
    a, b = w.shape
    ta = a
    for t in (1024, 512, 256, 128, 64, 32, 16, 8):
        if a % t == 0 and N_DEV * t * b * 4 <= 4 * 1024 * 1024:
            ta = t
            break
    c1 = 1.0 / (1.0 - ADAM_B1 ** ADAM_STEP)
    c2 = 1.0 / (1.0 - ADAM_B2 ** ADAM_STEP)

    def body(g_ref, w_ref, m_ref, v_ref, og_ref, od_ref, om_ref, ov_ref):
        g = g_ref[0].astype(F32)
        for s in range(1, N_DEV):
            g = g + g_ref[s].astype(F32)
        m_new = ADAM_B1 * m_ref[...] + (1.0 - ADAM_B1) * g
        v_new = ADAM_B2 * v_ref[...] + (1.0 - ADAM_B2) * (g * g)
        og_ref[...] = g
        om_ref[...] = m_new
        ov_ref[...] = v_new
        od_ref[...] = -ADAM_LR * ((m_new * c1) / (jnp.sqrt(v_new * c2) + ADAM_EPS) + ADAM_WD * w_ref[...])

    spec = pl.BlockSpec((ta, b), lambda i: (i, 0))
    return _pallas(
        body, name=name, grid=(a // ta,),
        in_specs=[pl.BlockSpec((N_DEV, ta, b), lambda i: (0, i, 0)), spec, spec, spec],
        out_specs=[spec] * 4, out_shape=[jax.ShapeDtypeStruct((a, b), F32)] * 4,
        compiler_params=_cparams(("parallel",)),
    )(gstack, w, m, v)


_WEIGHTS = ['meta_tokens', 'pre_mix_norm', 'post_mix_norm', 'pre_mlp_norm', 'post_mlp_norm', 'mlp_w1', 'mlp_w2',
            'w_in_even', 'w_out_even', 'sb_out_norm', 's5_lambda_re', 's5_lambda_im', 's5_log_dt', 's5_b_re', 's5_b_im',
            's5_c_re', 's5_c_im', 's5_d', 's5_w_glu', 's5_b_glu', 's5_out_norm', 'w_in_odd', 'dn_conv_w', 'dn_a_log',
            'dn_dt_bias', 'dn_out_norm', 'w_out_odd']
_SHARDED = ['meta_tokens', 'mlp_w1', 'mlp_w2', 'w_in_even', 'w_out_even', 's5_w_glu', 'w_in_odd', 'dn_conv_w', 'w_out_odd']
_SMALL = [n for n in _WEIGHTS if n not in _SHARDED]
_GATHER_FIRST = ['meta_tokens', 'w_in_even', 's5_w_glu', 'w_out_even']
_GATHER_LATE = [n for n in _SHARDED if n not in _GATHER_FIRST]
_REDUCE_EARLY = ['mlp_w1', 'mlp_w2', 'w_in_odd', 'dn_conv_w', 'w_out_odd', 'w_out_even']


def _view2d(name, a):
    return a.reshape(-1, a.shape[-1])


def _unshard(name, g):
    if name == 'mlp_w1':
        return g.reshape(N_DEV, 2, D_MODEL, -1).transpose(1, 2, 0, 3).reshape(2, D_MODEL, D_FF)
    if name == 'mlp_w2':
        return g.reshape(N_DEV, 2, -1, D_MODEL).transpose(1, 0, 2, 3).reshape(2, D_FF, D_MODEL)
    if name in ('w_in_even', 'w_in_odd', 'dn_conv_w', 'meta_tokens'):
        return g.transpose(1, 0, 2).reshape(g.shape[1], -1)
    return g.reshape(-1, g.shape[-1])


def _to_blocks(name, full):
    if name == 'mlp_w1':
        return full.reshape(2, D_MODEL, N_DEV, -1).transpose(2, 0, 1, 3).reshape(N_DEV, 2 * D_MODEL, -1)
    if name == 'mlp_w2':
        return full.reshape(2, N_DEV, -1, D_MODEL).transpose(1, 0, 2, 3).reshape(N_DEV, -1, D_MODEL)
    if name in ('w_in_even', 'w_in_odd', 'dn_conv_w', 'meta_tokens'):
        return full.reshape(full.shape[0], N_DEV, -1).transpose(1, 0, 2)
    return full.reshape(N_DEV, -1, full.shape[-1])


def _pack(parts):
    rows = []
    for p in parts:
        flat = p.reshape(-1)
        rows.append(jnp.pad(flat, (0, (-flat.shape[0]) % 128)).reshape(-1, 128))
    return jnp.concatenate(rows, axis=0)


def _unpack(packed, like):
    out, at = [], 0
    for p in like:
        n = math.prod(p.shape)
        nrow = -(-n // 128)
        out.append(packed[at:at + nrow].reshape(-1)[:n].reshape(p.shape))
        at += nrow
    return out


def _lane_vec(x, width=128):
    flat = x.reshape(-1)
    return jnp.pad(flat, (0, width - flat.shape[0])).reshape(1, width)


def kernel(x, meta_tokens, pre_mix_norm, post_mix_norm, pre_mlp_norm, post_mlp_norm, mlp_w1, mlp_w2, w_in_even, w_out_even, sb_out_norm, s5_lambda_re, s5_lambda_im, s5_log_dt, s5_b_re, s5_b_im, s5_c_re, s5_c_im, s5_d, s5_w_glu, s5_b_glu, s5_out_norm, w_in_odd, dn_conv_w, dn_a_log, dn_dt_bias, dn_out_norm, w_out_odd, loss_target, m_meta_tokens, m_pre_mix_norm, m_post_mix_norm, m_pre_mlp_norm, m_post_mlp_norm, m_mlp_w1, m_mlp_w2, m_w_in_even, m_w_out_even, m_sb_out_norm, m_s5_lambda_re, m_s5_lambda_im, m_s5_log_dt, m_s5_b_re, m_s5_b_im, m_s5_c_re, m_s5_c_im, m_s5_d, m_s5_w_glu, m_s5_b_glu, m_s5_out_norm, m_w_in_odd, m_dn_conv_w, m_dn_a_log, m_dn_dt_bias, m_dn_out_norm, m_w_out_odd, v_meta_tokens, v_pre_mix_norm, v_post_mix_norm, v_pre_mlp_norm, v_post_mlp_norm, v_mlp_w1, v_mlp_w2, v_w_in_even, v_w_out_even, v_sb_out_norm, v_s5_lambda_re, v_s5_lambda_im, v_s5_log_dt, v_s5_b_re, v_s5_b_im, v_s5_c_re, v_s5_c_im, v_s5_d, v_s5_w_glu, v_s5_b_glu, v_s5_out_norm, v_w_in_odd, v_dn_conv_w, v_dn_a_log, v_dn_dt_bias, v_dn_out_norm, v_w_out_odd):
    given = dict(locals())
    w = {n: given[n] for n in _WEIGHTS}
    mom_m = {n: given["m_" + n] for n in _WEIGHTS}
    mom_v = {n: given["v_" + n] for n in _WEIGHTS}

    seq = x.shape[1]
    assert x.shape[0] == 1 and seq % ROW_TILE == 0
    r = seq + ROW_TILE
    pad = ROW_TILE - N_META

    wire = {n: (F32 if n in ('dn_conv_w', 'meta_tokens') else BF16) for n in _SHARDED}
    shard_wire = lambda n: _view2d(n, w[n]).astype(wire[n])
    gathered = _exchange([shard_wire(n) for n in _GATHER_FIRST], [False] * len(_GATHER_FIRST), "gather_first")
    full = {n: _unshard(n, g_) for n, g_ in zip(_GATHER_FIRST, gathered)}
    w_ie, w_oe, w_glu = full['w_in_even'], full['w_out_even'], full['s5_w_glu']
    row = lambda v_: v_.reshape(1, -1)

    hs0 = jnp.concatenate([jnp.zeros((pad, D_MODEL), F32), full['meta_tokens'], x[0]], axis=0)
    hn0 = _norm_pre(hs0, row(pre_mix_norm[0]), "pre_mix_0")
    qkv = _mm_fwd(hn0, w_ie[:, :3 * SB_WIDTH], "in_even_qkv", out_dtypes=(BF16,))
    u = _mm_fwd(hn0, w_ie[:, 3 * SB_WIDTH:], "in_even_u")
    q, k, v = qkv[:, :SB_WIDTH], qkv[:, SB_WIDTH:2 * SB_WIDTH], qkv[:, 2 * SB_WIDTH:]
    nb = r // ATT_BLK
    blocks_t = lambda t_: t_.reshape(nb, ATT_BLK, 4, 128).transpose(2, 0, 3, 1)
    o_sb, ssave, gathered = _sb_fwd(q, k, blocks_t(v), pad, "sb_fwd",
                                    ride=([shard_wire(n) for n in _GATHER_LATE], [False] * len(_GATHER_LATE)))
    full.update({n: _unshard(n, g_) for n, g_ in zip(_GATHER_LATE, gathered)})
    w1, w2, w_oo, conv_w = full['mlp_w1'], full['mlp_w2'], full['w_out_odd'], full['dn_conv_w']
    w_io = full['w_in_odd'][:, :4 * DN_WIDTH]
    w_ab = jnp.pad(full['w_in_odd'][:, 4 * DN_WIDTH:], ((0, 0), (0, 128 - 2 * DN_HEADS)))

    lam_re, lam_im, logdt, btr, bti, ctr, cti, s5_mask = _s5_expand(
        s5_lambda_re[0], s5_lambda_im[0], s5_log_dt[0], s5_b_re[0], s5_b_im[0], s5_c_re[0], s5_c_im[0])
    a_re, a_im, bbr, bbi = _s5_prep(lam_re, lam_im, logdt, btr, bti, "s5_prep")
    s5_wb = jnp.stack([_s5_block_diag_b(bbr, s5_mask), _s5_block_diag_b(bbi, s5_mask)]).astype(BF16)
    s5_wc = jnp.stack([_s5_block_diag_c(ctr, s5_mask), _s5_block_diag_c(cti, s5_mask)]).astype(BF16)
    s5_a = jnp.stack([a_re, a_im])
    s5_args = (s5_wb, s5_a, s5_wc, row(s5_d[0]), w_glu, row(s5_b_glu[0]), row(s5_out_norm[0]))
    y_s5, merged, xstart = _s5_fwd(u, *s5_args, "s5_fwd")
    merged = _norm_pre(o_sb, row(sb_out_norm[0]), "sb_out_norm", into=merged)

    mix0, hs1, hn1 = _mm_norm_fwd(merged, w_oe, hs0, row(post_mix_norm[0]), g_pre=row(pre_mlp_norm[0]), name="out_even")
    relu2 = lambda acc: (jnp.square(jnp.maximum(acc, 0.0)), jnp.maximum(acc, 0.0))
    r0, ra0 = _mm_fwd(hn1, w1[0], "mlp_up_0", out_dtypes=(BF16, BF16), epilogue=relu2)
    m0, hs2, hn2 = _mm_norm_fwd(r0, w2[0], hs1, row(post_mlp_norm[0]), g_pre=row(pre_mix_norm[1]), name="mlp_down_0")

    proj = _mm_fwd(hn2, w_io, "in_odd")
    ab = _mm_fwd(hn2, w_ab, "in_odd_gates")
    alog, dtb = _lane_vec(dn_a_log[0]), _lane_vec(dn_dt_bias[0])
    qd, kd, vd, gb = _dn_pre_fwd(proj, ab, conv_w, alog, dtb, pad, "dn_pre")
    gbt = gb[:, :2 * DN_HEADS].T
    o_dn, s_dn, t_dn = _dn_fwd(qd, kd, vd, gb, gbt, "dn_fwd")
    on_dn = _dn_post_fwd(o_dn, proj, row(dn_out_norm[0]), "dn_post")
    mix1, hs3, hn3 = _mm_norm_fwd(on_dn, w_oo, hs2, row(post_mix_norm[1]), g_pre=row(pre_mlp_norm[1]), name="out_odd")
    r1, ra1 = _mm_fwd(hn3, w1[1], "mlp_up_1", out_dtypes=(BF16, BF16), epilogue=relu2)
    dhs, dm1, dg_post_mlp1, loss_part = _mm_norm_fwd(r1, w2[1], hs3, row(post_mlp_norm[1]),
                                                     loss=(loss_target[0], pad + N_META), name="mlp_down_1_loss")
    loss = lax.psum(loss_part, ("x", "y", "c"))

    g = {}
    drelu2 = lambda acc, ra: (acc * (2.0 * ra.astype(F32)),)

    def mlp_bwd(layer, hn, rr, ra, dm):
        dw2 = _mm_wgrad(rr, dm, f"mlp_down_{layer}_wgrad")
        da = _mm_dgrad(dm, w2[layer], f"mlp_down_{layer}_dgrad", out_dtypes=(BF16,), extras=(ra,), epilogue=drelu2)
        dw1 = _mm_wgrad(hn, da, f"mlp_up_{layer}_wgrad")
        return dw1, dw2, da

    dw1_1, dw2_1, da1 = mlp_bwd(1, hn3, r1, ra1, dm1)
    dhs, dmix1, dg_pre_mlp1, dg_post_mix1 = _dgrad_norm_bwd(
        da1, w1[1], dhs, hs3, row(pre_mlp_norm[1]), post=(mix1, row(post_mix_norm[1])), pad=pad, name="post_mix_1_bwd")

    g['w_out_odd'] = _mm_wgrad(on_dn, dmix1, "out_odd_wgrad")
    d_on_dn = _mm_dgrad(dmix1, w_oo, "out_odd_dgrad")
    do_dn, dproj, dg_dn = _dn_post_bwd(o_dn, proj, row(dn_out_norm[0]), d_on_dn, "dn_post_bwd")
    dqd, dkd, dvd, dgb = _dn_bwd(qd, kd, vd, gb, gbt, s_dn, t_dn, do_dn, "dn_bwd")
    dco, dab, d_alog, d_dtb = _dn_pre_bwd(proj, conv_w, dqd, dkd, dvd, dgb, ab, alog, dtb, pad, "dn_pre_bwd")
    dproj, d_conv = _dn_conv_bwd(dco, proj, conv_w, dproj, "dn_conv_bwd")
    g['w_in_odd'] = jnp.concatenate([_mm_wgrad(hn2, dproj, "in_odd_wgrad"),
                                     _mm_wgrad(hn2, dab, "in_odd_gates_wgrad")[:, :2 * DN_HEADS]], axis=1)
    dhn2_gates = _mm_dgrad(dab, w_ab, "in_odd_gates_dgrad")
    g['dn_conv_w'] = d_conv[:DN_CONV]
    g['dn_a_log'], g['dn_dt_bias'], g['dn_out_norm'] = d_alog[0, :DN_HEADS], d_dtb[0, :DN_HEADS], dg_dn[0]

    dhs, dm0, dg_pre_mix1, dg_post_mlp0 = _dgrad_norm_bwd(
        dproj, w_io, dhs, hs2, row(pre_mix_norm[1]), post=(m0, row(post_mlp_norm[0])), add=dhn2_gates, pad=pad,
        name="post_mlp_0_bwd")
    dw1_0, dw2_0, da0 = mlp_bwd(0, hn1, r0, ra0, dm0)
    dhs, dmix0, dg_pre_mlp0, dg_post_mix0 = _dgrad_norm_bwd(
        da0, w1[0], dhs, hs1, row(pre_mlp_norm[0]), post=(mix0, row(post_mix_norm[0])), pad=pad, name="post_mix_0_bwd")

    g['w_out_even'] = _mm_wgrad(merged, dmix0, "out_even_wgrad")
    dmerged = _mm_dgrad(dmix0, w_oe, "out_even_dgrad")
    _, do_sb, _, dg_sb = _norm_bwd(dmerged, post=(o_sb, row(sb_out_norm[0])), pad=pad, dm_dtype=F32,
                                   dhs_cols=(SB_WIDTH, 0), name="sb_out_norm_bwd")
    dq, dk, dv = _sb_bwd(q, k, v, blocks_t(k), ssave, do_sb, pad, "sb_bwd")
    g['mlp_w1'] = jnp.stack([dw1_0, dw1_1])
    g['mlp_w2'] = jnp.stack([dw2_0, dw2_1])
    grad_wire = lambda n: _to_blocks(n, g[n].reshape(full[n].shape)).astype(wire[n])
    du, d_a, d_d, d_bglu, dg_s5, d_wb, d_wc, g['s5_w_glu'], reduced = _s5_bwd(
        u, y_s5, dmerged, xstart, *s5_args, "s5_bwd", don_block=1,
        ride=([grad_wire(n) for n in _REDUCE_EARLY], [True] * len(_REDUCE_EARLY)))
    stacks = dict(zip(_REDUCE_EARLY, reduced))
    g_lr, g_li, g_dt, g_btr, g_bti = _s5_prep_bwd(
        lam_re, lam_im, logdt, btr, bti, d_a[0], d_a[1],
        _s5_diag_of_b(d_wb[0], s5_mask), _s5_diag_of_b(d_wb[1], s5_mask), "s5_prep_bwd")
    gg, nn, pp = S5_GROUPS, S5_STATE, S5_GROUP
    g['s5_lambda_re'], g['s5_lambda_im'] = g_lr.reshape(gg, nn), g_li.reshape(gg, nn)
    g['s5_log_dt'] = g_dt.reshape(gg, nn)[:, 0]
    g['s5_b_re'], g['s5_b_im'] = g_btr.T.reshape(gg, nn, pp), g_bti.T.reshape(gg, nn, pp)
    g['s5_c_re'] = _s5_diag_of_c(d_wc[0], s5_mask).reshape(gg, nn, pp).transpose(0, 2, 1)
    g['s5_c_im'] = _s5_diag_of_c(d_wc[1], s5_mask).reshape(gg, nn, pp).transpose(0, 2, 1)
    g['s5_d'], g['s5_b_glu'], g['s5_out_norm'], g['sb_out_norm'] = d_d[0], d_bglu[0], dg_s5[0], dg_sb[0]
    dqkvu = jnp.concatenate([dq, dk, dv, du], axis=1).astype(BF16)
    g['w_in_even'] = _mm_wgrad(hn0, dqkvu, "in_even_wgrad")
    dhs, _, dg_pre_mix0, _ = _dgrad_norm_bwd(dqkvu, w_ie, dhs, hs0, row(pre_mix_norm[0]), pad=pad, name="pre_mix_0_bwd")

    g['meta_tokens'] = dhs[pad:pad + N_META]
    g['pre_mix_norm'] = jnp.concatenate([dg_pre_mix0, dg_pre_mix1], axis=0)
    g['post_mix_norm'] = jnp.concatenate([dg_post_mix0, dg_post_mix1], axis=0)
    g['pre_mlp_norm'] = jnp.concatenate([dg_pre_mlp0, dg_pre_mlp1], axis=0)
    g['post_mlp_norm'] = jnp.concatenate([dg_post_mlp0, dg_post_mlp1], axis=0)
    grad_x = dhs[pad + N_META:][None]

    small_like = [w[n] for n in _SMALL]
    last = [n for n in _SHARDED if n not in _REDUCE_EARLY]
    partial = [grad_wire(n) for n in last] + [_pack([g[n].reshape(w[n].shape) for n in _SMALL])]
    reduced = _exchange(partial, [True] * len(last) + [False], "reduce_last")
    stacks.update(zip(last, reduced[:-1]))
    grads, deltas, new_m, new_v = {}, {}, {}, {}
    for n in _SHARDED:
        outs = _adamw(stacks[n], _view2d(n, w[n]), _view2d(n, mom_m[n]), _view2d(n, mom_v[n]), f"adamw_{n}")
        grads[n], deltas[n], new_m[n], new_v[n] = (o.reshape(w[n].shape) for o in outs)
    outs = _adamw(reduced[-1], _pack(small_like), _pack([mom_m[n] for n in _SMALL]), _pack([mom_v[n] for n in _SMALL]),
                  "adamw_small")
    for dst, o in zip((grads, deltas, new_m, new_v), outs):
        for n, part in zip(_SMALL, _unpack(o, small_like)):
            dst[n] = part
    return (loss, grad_x, *[grads[n] for n in _WEIGHTS], *[deltas[n] for n in _WEIGHTS],
            *[new_m[n] for n in _WEIGHTS], *[new_v[n] for n in _WEIGHTS])
```

```python
import math

import jax
import jax.numpy as jnp
from jax import lax
from jax.experimental import pallas as pl
from jax.experimental.pallas import tpu as pltpu

F32 = jnp.float32
BF16 = jnp.bfloat16

D_MODEL = 1024
N_META = 16
SB_HEAD_DIM = 64
SB_WIDTH = 512
S5_WIDTH = 512
S5_GROUP = 16
S5_GROUPS = 32
S5_STATE = 64
S5_NS = S5_GROUPS * S5_STATE
DN_HEAD_DIM = 128
DN_HEADS = 8
DN_WIDTH = 1024
DN_CONV = 4
D_FF = 4096
EPS = 1e-6
N_DEV = 8

ADAM_LR = 0.001
ADAM_B1 = 0.9
ADAM_B2 = 0.999
ADAM_EPS = 1e-08
ADAM_WD = 0.01
ADAM_STEP = 10

ROW_TILE = 512
JUNCTION_SLICES = 2
FUSED_FWD_TILE = 512
ATT_BLK = 256
SB_BLOCKS_PER_TRIP = 3
SB_LOG_ZERO = -106.0
SB_FWD_SKEW = False
SB_BWD_SKEW = True
DN_CHUNK = 128
DN_SUB = 16
S5_TILE = 256
S5_CHUNKS = 4
VMEM_LIMIT = 56 * 1024 * 1024

_HIGH = lax.Precision.HIGHEST


def _pallas(body, **kw):
    return pl.pallas_call(body, **kw)


def _cparams(sem):
    return pltpu.CompilerParams(dimension_semantics=sem, vmem_limit_bytes=VMEM_LIMIT)


def _dot(a, b, dims=((1,), (0,))):
    return lax.dot_general(a, b, (dims, ((), ())), preferred_element_type=F32)


def _dot_hi(a, b):
    return lax.dot_general(a, b, (((1,), (0,)), ((), ())), preferred_element_type=F32, precision=_HIGH)


def _split_dot(m_bf16, x):
    hi = x.astype(BF16)
    lo = (x - hi.astype(F32)).astype(BF16)
    return _dot(m_bf16, hi) + _dot(m_bf16, lo)


def _matmul(a, b, *, ta=False, tb=False, tm, tn, tk, name, out_dtypes=(F32,), extras=(), epilogue=None):
    m, k = (a.shape[1], a.shape[0]) if ta else a.shape
    n = b.shape[0] if tb else b.shape[1]
    assert (b.shape[1] if tb else b.shape[0]) == k
    assert m % tm == 0 and n % tn == 0 and k % tk == 0, (name, m, n, k, tm, tn, tk)
    nk = k // tk
    n_ex = len(extras)
    n_out = len(out_dtypes)
    dims = ((0 if ta else 1,), (1 if tb else 0,))

    def finish(acc, ex_refs, o_refs):
        outs = (acc,) if epilogue is None else epilogue(acc, *[r[...] for r in ex_refs])
        for o_ref, o in zip(o_refs, outs):
            o_ref[...] = o.astype(o_ref.dtype)

    def body(*refs):
        a_ref, b_ref = refs[0], refs[1]
        ex_refs = refs[2:2 + n_ex]
        o_refs = refs[2 + n_ex:2 + n_ex + n_out]
        prod = _dot(a_ref[...].astype(BF16), b_ref[...].astype(BF16), dims)
        if nk == 1:
            finish(prod, ex_refs, o_refs)
            return
        acc_ref = refs[-1]
        kk = pl.program_id(2)

        @pl.when(kk == 0)
        def _():
            acc_ref[...] = prod

        @pl.when(kk > 0)
        def _():
            acc_ref[...] += prod

        @pl.when(kk == nk - 1)
        def _():
            finish(acc_ref[...], ex_refs, o_refs)

    a_spec = pl.BlockSpec((tk, tm), lambda j, i, kk: (kk, i)) if ta else pl.BlockSpec((tm, tk), lambda j, i, kk: (i, kk))
    b_spec = pl.BlockSpec((tn, tk), lambda j, i, kk: (j, kk)) if tb else pl.BlockSpec((tk, tn), lambda j, i, kk: (kk, j))
    o_spec = pl.BlockSpec((tm, tn), lambda j, i, kk: (i, j))
    outs = _pallas(
        body, name=name,
        grid=(n // tn, m // tm, nk),
        in_specs=[a_spec, b_spec] + [o_spec] * n_ex,
        out_specs=[o_spec] * n_out,
        out_shape=[jax.ShapeDtypeStruct((m, n), dt) for dt in out_dtypes],
        scratch_shapes=[] if nk == 1 else [pltpu.VMEM((tm, tn), F32)],
        compiler_params=_cparams(("parallel", "parallel", "arbitrary")),
    )(a, b, *extras)
    return outs[0] if n_out == 1 else outs


def _tile(n, cap):
    best = 128
    for t in range(128, min(n, cap) + 1, 128):
        if n % t == 0:
            best = t
    assert n % best == 0, n
    return best


MM_K_CAP = 4096
WGRAD_ROWS = 1536


MM_LHS_TILE_BYTES = 6 * 1024 * 1024


def _row_tile(x, depth):
    tall = 3 * ROW_TILE
    fits = tall * depth * x.dtype.itemsize <= MM_LHS_TILE_BYTES
    return tall if (x.shape[0] % tall == 0 and fits) else ROW_TILE


def _mm_fwd(x, w, name, **kw):
    k, n = w.shape
    tk = _tile(k, MM_K_CAP)
    return _matmul(x, w, tm=_row_tile(x, tk), tn=_tile(n, 1024), tk=tk, name=name, **kw)


def _mm_dgrad(dy, w, name, **kw):
    k, n = w.shape
    tk = _tile(n, MM_K_CAP)
    return _matmul(dy, w, tb=True, tm=_row_tile(dy, tk), tn=_tile(k, 1024), tk=tk, name=name, **kw)


def _mm_wgrad(x, dy, name):
    k, n = x.shape[1], dy.shape[1]
    rows = x.shape[0]
    return _matmul(x, dy, ta=True, tm=_tile(k, 1024), tn=_tile(n, 1024),
                   tk=WGRAD_ROWS if rows % WGRAD_ROWS == 0 else ROW_TILE, name=name)


def _rms(x, g):
    r = lax.rsqrt(jnp.mean(x * x, axis=-1, keepdims=True) + EPS)
    return x * r * g


def _rms_bwd(x, g, dy):
    r = lax.rsqrt(jnp.mean(x * x, axis=-1, keepdims=True) + EPS)
    xh = x * r
    dxh = dy * g
    dx = r * (dxh - xh * jnp.mean(dxh * xh, axis=-1, keepdims=True))
    dg = jnp.sum(dy * xh, axis=0, keepdims=True)
    return dx, dg


def _row_spec(width, tile=ROW_TILE):
    return pl.BlockSpec((tile, width), lambda i: (i, 0))


def _vec_spec(width):
    return pl.BlockSpec((1, width), lambda i: (0, 0))


def _norm_pre(hs, g, name, into=None):
    r, d = hs.shape

    def body(x_ref, g_ref, *rest):
        rest[-1][...] = _rms(x_ref[...], g_ref[...]).astype(BF16)

    if into is None:
        return _pallas(body, name=name, grid=(r // ROW_TILE,), in_specs=[_row_spec(d), _vec_spec(d)],
                       out_specs=_row_spec(d), out_shape=jax.ShapeDtypeStruct((r, d), BF16),
                       compiler_params=_cparams(("parallel",)))(hs, g)
    return _pallas(body, name=name, grid=(r // ROW_TILE,),
                   in_specs=[_row_spec(d), _vec_spec(d), pl.BlockSpec(memory_space=pl.ANY)],
                   out_specs=_row_spec(d), out_shape=jax.ShapeDtypeStruct(into.shape, BF16), input_output_aliases={2: 0},
                   compiler_params=_cparams(("parallel",)))(hs, g, into)


def _mm_norm_fwd(a, w, hs, g_post, *, g_pre=None, loss=None, name):
    k, d = w.shape
    r = a.shape[0]
    assert k <= MM_K_CAP and d == hs.shape[1]
    t = FUSED_FWD_TILE
    nt = r // t

    def body(*refs):
        a_ref, w_ref, hs_ref, gp_ref = refs[:4]
        i = pl.program_id(0)
        m = _dot(a_ref[...].astype(BF16), w_ref[...].astype(BF16))
        gp = gp_ref[...]
        new = hs_ref[...] + _rms(m, gp)
        if loss is None:
            gn_ref, m_ref, o_ref, hn_ref = refs[4:]
            m_ref[...] = m
            o_ref[...] = new
            hn_ref[...] = _rms(new, gn_ref[...]).astype(BF16)
        else:
            t_ref, dhs_ref, dm_ref, dgp_ref, loss_ref = refs[4:]
            live = (i * t + lax.broadcasted_iota(jnp.int32, (t, 1), 0)) >= loss[1]
            diff = jnp.where(live, new - t_ref[...], 0.0)
            dhs = diff * (1.0 / d)
            dhs_ref[...] = dhs
            loss_ref[...] = jnp.full((8, 128), 0.5 / d * jnp.sum(diff * diff), F32)
            dm, dg = _rms_bwd(m, gp, dhs)
            dm_ref[...] = dm.astype(BF16)

            @pl.when(i == 0)
            def _():
                dgp_ref[...] = jnp.zeros_like(dgp_ref)
            dgp_ref[...] += dg

    common_in = [_row_spec(k, t), pl.BlockSpec((k, d), lambda i: (0, 0)), _row_spec(d, t), _vec_spec(d)]
    if loss is None:
        return _pallas(
            body, name=name, grid=(nt,), in_specs=common_in + [_vec_spec(d)],
            out_specs=[_row_spec(d, t)] * 3,
            out_shape=[jax.ShapeDtypeStruct((r, d), F32), jax.ShapeDtypeStruct((r, d), F32), jax.ShapeDtypeStruct((r, d), BF16)],
            compiler_params=_cparams(("parallel",)))(a, w, hs, g_post, g_pre)
    target, first_row = loss
    assert first_row % t == 0
    dhs, dm, dgp, parts = _pallas(
        body, name=name, grid=(nt,),
        in_specs=common_in + [pl.BlockSpec((t, d), lambda i: (jnp.maximum(i - first_row // t, 0), 0))],
        out_specs=[_row_spec(d, t), _row_spec(d, t), _vec_spec(d), pl.BlockSpec((8, 128), lambda i: (i, 0))],
        out_shape=[jax.ShapeDtypeStruct((r, d), F32), jax.ShapeDtypeStruct((r, d), BF16), jax.ShapeDtypeStruct((1, d), F32),
                   jax.ShapeDtypeStruct((nt * 8, 128), F32)],
        compiler_params=_cparams(("arbitrary",)))(a, w, hs, g_post, target)
    return dhs, dm, dgp, jnp.sum(parts[::8, 0])


def _norm_bwd(dhs, *, pre=None, post=None, pad=0, dm_dtype=BF16, dhs_cols=None, name):
    r = dhs.shape[0]
    d = dhs.shape[1] if dhs_cols is None else dhs_cols[0]
    has_pre, has_post = pre is not None, post is not None

    def body(*refs):
        it = iter(refs)
        dhs_ref = next(it)
        if has_pre:
            hs_ref, gn_ref, dhn_ref = next(it), next(it), next(it)
        if has_post:
            m_ref, gp_ref = next(it), next(it)
        if has_pre:
            o_dhs, o_dgn = next(it), next(it)
        if has_post:
            o_dm, o_dgp = next(it), next(it)
        i = pl.program_id(0)
        live = (i * ROW_TILE + lax.broadcasted_iota(jnp.int32, (ROW_TILE, 1), 0)) >= pad
        cur = jnp.where(live, dhs_ref[...], 0.0)
        if has_pre:
            dx, dg = _rms_bwd(hs_ref[...], gn_ref[...], jnp.where(live, dhn_ref[...].astype(F32), 0.0))
            cur = cur + dx
            o_dhs[...] = cur

            @pl.when(i == 0)
            def _():
                o_dgn[...] = jnp.zeros_like(o_dgn)
            o_dgn[...] += dg
        if has_post:
            dm, dg = _rms_bwd(m_ref[...], gp_ref[...], cur)
            o_dm[...] = dm.astype(o_dm.dtype)

            @pl.when(i == 0)
            def _():
                o_dgp[...] = jnp.zeros_like(o_dgp)
            o_dgp[...] += dg

    dhs_spec = _row_spec(d) if dhs_cols is None else pl.BlockSpec((ROW_TILE, d), lambda i: (i, dhs_cols[1]))
    ins, in_specs, out_specs, out_shape = [dhs], [dhs_spec], [], []
    if has_pre:
        ins += list(pre)
        in_specs += [_row_spec(d), _vec_spec(d), _row_spec(d)]
        out_specs += [_row_spec(d), _vec_spec(d)]
        out_shape += [jax.ShapeDtypeStruct((r, d), F32), jax.ShapeDtypeStruct((1, d), F32)]
    if has_post:
        ins += list(post)
        in_specs += [_row_spec(d), _vec_spec(d)]
        out_specs += [_row_spec(d), _vec_spec(d)]
        out_shape += [jax.ShapeDtypeStruct((r, d), dm_dtype), jax.ShapeDtypeStruct((1, d), F32)]
    outs = list(_pallas(body, name=name, grid=(r // ROW_TILE,), in_specs=in_specs, out_specs=out_specs,
                        out_shape=out_shape, compiler_params=_cparams(("arbitrary",)))(*ins))
    dhs_new, dgn = (outs.pop(0), outs.pop(0)) if has_pre else (dhs, None)
    dm, dgp = (outs.pop(0), outs.pop(0)) if has_post else (None, None)
    return dhs_new, dm, dgn, dgp


def _dgrad_norm_bwd(dy, w, dhs, hs, g_pre, *, post=None, add=None, pad=0, name):
    d, n = w.shape
    r = dy.shape[0]
    assert n <= MM_K_CAP and d == dhs.shape[1]
    t = ROW_TILE
    has_post, has_add = post is not None, add is not None
    dims = ((1,), (1,))

    def body(*refs):
        it = iter(refs)
        dy_ref, w_ref = next(it), next(it)
        add_ref = next(it) if has_add else None
        dhs_ref, hs_ref, gn_ref = next(it), next(it), next(it)
        if has_post:
            m_ref, gp_ref = next(it), next(it)
        o_dhs, o_dgn = next(it), next(it)
        if has_post:
            o_dm, o_dgp = next(it), next(it)
        i = pl.program_id(0)

        @pl.when(i == 0)
        def _():
            o_dgn[...] = jnp.zeros_like(o_dgn)
            if has_post:
                o_dgp[...] = jnp.zeros_like(o_dgp)

        wb = w_ref[...].astype(BF16)
        parts = [pl.ds(p * (t // JUNCTION_SLICES), t // JUNCTION_SLICES) for p in range(JUNCTION_SLICES)]
        dhns = [_dot(dy_ref[rows, :].astype(BF16), wb, dims) for rows in parts]
        for p, rows in enumerate(parts):
            dhn = dhns[p] + add_ref[rows, :] if has_add else dhns[p]
            live = (i * t + p * (t // JUNCTION_SLICES) + lax.broadcasted_iota(jnp.int32, (t // JUNCTION_SLICES, 1), 0)) >= pad
            dx, dg = _rms_bwd(hs_ref[rows, :], gn_ref[...], jnp.where(live, dhn, 0.0))
            cur = jnp.where(live, dhs_ref[rows, :], 0.0) + dx
            o_dhs[rows, :] = cur
            o_dgn[...] += dg
            if has_post:
                dm, dg = _rms_bwd(m_ref[rows, :], gp_ref[...], cur)
                o_dm[rows, :] = dm.astype(BF16)
                o_dgp[...] += dg

    ins = [dy, w] + ([add] if has_add else []) + [dhs, hs, g_pre] + (list(post) if has_post else [])
    in_specs = ([_row_spec(n, t), pl.BlockSpec((d, n), lambda i: (0, 0))] + ([_row_spec(d, t)] if has_add else [])
                + [_row_spec(d, t), _row_spec(d, t), _vec_spec(d)] + ([_row_spec(d, t), _vec_spec(d)] if has_post else []))
    out_specs = [_row_spec(d, t), _vec_spec(d)] + ([_row_spec(d, t), _vec_spec(d)] if has_post else [])
    out_shape = [jax.ShapeDtypeStruct((r, d), F32), jax.ShapeDtypeStruct((1, d), F32)]
    if has_post:
        out_shape += [jax.ShapeDtypeStruct((r, d), BF16), jax.ShapeDtypeStruct((1, d), F32)]
    outs = list(_pallas(body, name=name, grid=(r // t,), in_specs=in_specs, out_specs=out_specs,
                        out_shape=out_shape, compiler_params=_cparams(("arbitrary",)))(*ins))
    return (outs[0], outs[2], outs[1], outs[3]) if has_post else (outs[0], None, outs[1], None)


def _softplus(z):
    return jnp.maximum(z, 0.0) + jnp.log(1.0 + jnp.exp(-jnp.abs(z)))


def _sb_consts(t):
    row = lax.broadcasted_iota(jnp.int32, (t, t), 0)
    col = lax.broadcasted_iota(jnp.int32, (t, t), 1)
    m_up = (col >= row).astype(BF16)
    m_low = (col <= row).astype(BF16)
    return m_up, m_low


def _emit_chains(chains, stages, skew):
    if skew:
        for step in range(len(chains) + len(stages) - 1):
            for si, stage in enumerate(stages):
                if 0 <= step - si < len(chains):
                    stage(chains[step - si])
    else:
        for stage in stages:
            for c in chains:
                stage(c)


def _sb_fwd(q, k, vt3, pad, name, ride=((), ())):
    r = q.shape[0]
    t = ATT_BLK
    nb = r // t
    nbp = -(-(nb + 1) // 8) * 8
    jmin = pad // t
    scale = SB_HEAD_DIM ** -0.5
    n_ride = len(ride[0])

    def body(q_ref, k_ref, vt_ref, *rest):
        ride_in, (o_ref, ss_ref), ride_out = rest[:n_ride], rest[n_ride:n_ride + 2], rest[n_ride + 2:2 * n_ride + 2]
        acc_ref, kn_ref = rest[2 * n_ride + 2:2 * n_ride + 4]
        ride_sems = rest[2 * n_ride + 4:]
        i = pl.program_id(1)
        if n_ride:
            @pl.when((pl.program_id(0) == 0) & (i == 0))
            def _():
                for cp in _exchange_copies(ride_in, ride_out, ride[1], *ride_sems):
                    cp.start()

        @pl.when(i == 0)
        def _():
            def blk(b, m):
                kb = k_ref[pl.ds(pl.multiple_of(b * t, t), t), :].astype(F32)
                return jnp.maximum(m, jnp.max(jnp.sum(kb * kb, axis=1, keepdims=True), axis=0, keepdims=True))
            kn_ref[...] = jnp.broadcast_to(lax.fori_loop(0, nb, blk, jnp.zeros((1, 1), F32)), (8, 128))

        qf = q_ref[...].astype(F32)
        z_bound = scale * jnp.sqrt(jnp.max(jnp.sum(qf * qf, axis=1, keepdims=True)) * jnp.max(kn_ref[...]))

        def need(carry):
            return jnp.maximum(jnp.max(carry[0]), jnp.max(carry[1])) + z_bound >= SB_LOG_ZERO

        qt = qf.T
        sub = lax.broadcasted_iota(jnp.int32, (128, 1), 0)
        m_up, _ = _sb_consts(t)
        kpos0 = lax.broadcasted_iota(jnp.int32, (t, 1), 0)
        qpos = i * t + lax.broadcasted_iota(jnp.int32, (1, t), 1)
        qths = [jnp.where((sub >= 64 * h) & (sub < 64 * (h + 1)), qt * scale, 0.0).astype(BF16) for h in range(2)]
        acc_ref[...] = jnp.zeros_like(acc_ref)

        def sweep(js, carry, masked):
            kbs = [k_ref[pl.ds(pl.multiple_of(j * t, t), t), :] for j in js]
            vts = [vt_ref[0, j] for j in js]
            accs = [acc_ref[0], acc_ref[1]]
            s = list(carry)
            chains = [(n, h) for n in range(len(js)) for h in range(2)]
            masked = [masked] * len(js) if isinstance(masked, bool) else masked
            valid = [(js[n] * t + kpos0 < qpos) & (js[n] * t + kpos0 >= pad) if masked[n] else None for n in range(len(js))]
            zt, inc, saves = {}, {}, []

            def st_scores(c):
                zt[c] = _dot(kbs[c[0]], qths[c[1]])

            def st_cumsum(c):
                lk = -_softplus(zt[c])
                if masked[c[0]]:
                    lk = jnp.where(valid[c[0]], lk, 0.0)
                inc[c] = _split_dot(m_up, lk)

            def st_weights(c):
                n, h = c
                saves.append((h, js[n], s[h]))
                w = jnp.exp(zt[c] + inc[c] + s[h])
                if masked[n]:
                    w = jnp.where(valid[n], w, 0.0)
                accs[h] = accs[h] + _dot(vts[n], w.astype(BF16))
                s[h] = s[h] + inc[c][0:1, :]

            _emit_chains(chains, [st_scores, st_cumsum, st_weights], SB_FWD_SKEW)
            for h, j, val in saves:
                ss_ref[h, 0, pl.ds(j, 1), :] = val
            acc_ref[0] = accs[0]
            acc_ref[1] = accs[1]
            return tuple(s)

        zero = jnp.zeros((1, t), F32)
        bpi = SB_BLOCKS_PER_TRIP
        j, carry = lax.cond(
            i - 1 > jmin,
            lambda: (i - 2, sweep([i, i - 1], (zero, zero), [True, False])),
            lambda: (i - 1, sweep([i], (zero, zero), True)))
        def further(j, carry):
            j, carry = lax.while_loop(
                lambda st: (st[0] - bpi >= jmin) & need(st[1]),
                lambda st: (st[0] - bpi, sweep([st[0] - b for b in range(bpi)], st[1], False)), (j, carry))
            j, carry = lax.while_loop(
                lambda st: (st[0] > jmin) & need(st[1]),
                lambda st: (st[0] - 1, sweep([st[0]], st[1], False)), (j, carry))
            return lax.while_loop(
                lambda st: (st[0] == jmin) & (i > jmin) & need(st[1]),
                lambda st: (st[0] - 1, sweep([st[0]], st[1], True)), (j, carry))[0]

        j = lax.cond((j >= jmin) & need(carry), lambda: further(j, carry), lambda: j)
        first = jnp.full((1, t), j + 1, jnp.int32).astype(F32)
        ss_ref[0, 0, nbp - 1:nbp, :] = first
        ss_ref[1, 0, nbp - 1:nbp, :] = first
        acc = jnp.where(sub < 64, acc_ref[0], acc_ref[1])
        o_ref[...] = acc.T
        if n_ride:
            @pl.when((pl.program_id(0) == 3) & (i == nb - 1))
            def _():
                for cp in _exchange_copies(ride_in, ride_out, ride[1], *ride_sems):
                    cp.wait()

    hbm = pl.BlockSpec(memory_space=pl.ANY)
    outs = _pallas(
        body, name=name, grid=(4, nb),
        in_specs=[pl.BlockSpec((t, 128), lambda hp, i: (i, hp)),
                  pl.BlockSpec((r, 128), lambda hp, i: (0, hp)),
                  pl.BlockSpec((1, nb, 128, t), lambda hp, i: (hp, 0, 0, 0))] + [hbm] * n_ride,
        out_specs=[pl.BlockSpec((t, 128), lambda hp, i: (i, hp)),
                   pl.BlockSpec((2, 1, nbp, t), lambda hp, i: (hp, i, 0, 0))] + [hbm] * n_ride,
        out_shape=[jax.ShapeDtypeStruct((r, SB_WIDTH), F32),
                   jax.ShapeDtypeStruct((8, nb, nbp, t), F32)] + _exchange_shapes(*ride),
        scratch_shapes=[pltpu.VMEM((2, 128, t), F32), pltpu.VMEM((8, 128), F32)] + (_exchange_sems(n_ride) if n_ride else []),
        compiler_params=_cparams(("arbitrary", "arbitrary")),
    )(q, k, vt3, *ride[0])
    return outs[0], outs[1], list(outs[2:])


def _sb_bwd(q, k, v, kt3, ssave, do, pad, name):
    r = q.shape[0]
    t = ATT_BLK
    nb = r // t
    nbp = ssave.shape[2]
    jmin = pad // t
    scale = SB_HEAD_DIM ** -0.5

    def body(q_ref, do_ref, k_ref, v_ref, kt_ref, ss_ref, dq_ref, dk_hbm, dv_hbm, dk_acc, dv_acc, dq_acc, sem):
        hp = pl.program_id(0)
        i = pl.program_id(1)

        @pl.when(i == 0)
        def _():
            dk_acc[...] = jnp.zeros_like(dk_acc)
            dv_acc[...] = jnp.zeros_like(dv_acc)

        qf = q_ref[...].astype(F32)
        dof = do_ref[...]
        qt = qf.T
        dot_ = dof.T
        sub = lax.broadcasted_iota(jnp.int32, (128, 1), 0)
        lane = lax.broadcasted_iota(jnp.int32, (1, 128), 1)
        m_up, m_low = _sb_consts(t)
        kpos0 = lax.broadcasted_iota(jnp.int32, (t, 1), 0)
        qpos = i * t + lax.broadcasted_iota(jnp.int32, (1, t), 1)
        first = jnp.clip(jnp.max(ss_ref[0, 0, nbp - 1:nbp, :]).astype(jnp.int32), jmin, i)
        mid0 = jnp.maximum(first, jmin + 1)
        pair = i - mid0 >= 1
        n_mid = jnp.maximum(i - mid0 - 1, 0)
        n_edge = jnp.where((i > jmin) & (first == jmin), 1, 0)
        in_t = [(sub >= 64 * h) & (sub < 64 * (h + 1)) for h in range(2)]
        in_l = [(lane >= 64 * h) & (lane < 64 * (h + 1)) for h in range(2)]
        qths = [jnp.where(in_t[h], qt * scale, 0.0).astype(BF16) for h in range(2)]
        doths = [jnp.where(in_t[h], dot_, 0.0).astype(BF16) for h in range(2)]
        qhs = [jnp.where(in_l[h], qf * scale, 0.0).astype(BF16) for h in range(2)]
        dohs = [jnp.where(in_l[h], dof, 0.0).astype(BF16) for h in range(2)]
        dq_acc[...] = jnp.zeros_like(dq_acc)

        def sweep(js, carry, masked):
            rows = [pl.ds(pl.multiple_of(j * t, t), t) for j in js]
            kbs = [k_ref[rw, :] for rw in rows]
            vbs = [v_ref[rw, :] for rw in rows]
            kts = [kt_ref[0, j] for j in js]
            sss = [[ss_ref[h, 0, pl.ds(j, 1), :] for h in range(2)] for j in js]
            dv_old = [dv_acc[rw, :] for rw in rows]
            dk_old = [dk_acc[rw, :] for rw in rows]
            dqs = [dq_acc[0], dq_acc[1]]
            ec = list(carry)
            chains = [(n, h) for n in range(len(js)) for h in range(2)]
            masked = [masked] * len(js) if isinstance(masked, bool) else masked
            valid = [(js[n] * t + kpos0 < qpos) & (js[n] * t + kpos0 >= pad) if masked[n] else None for n in range(len(js))]
            zt, dvt, sp, inc, e, big_e = {}, {}, {}, {}, {}, {}

            def st_scores(c):
                zt[c] = _dot(kbs[c[0]], qths[c[1]])
                dvt[c] = _dot(vbs[c[0]], doths[c[1]])

            def st_cumsum(c):
                sp[c] = _softplus(zt[c])
                lk = -sp[c]
                if masked[c[0]]:
                    lk = jnp.where(valid[c[0]], lk, 0.0)
                inc[c] = _split_dot(m_up, lk)

            def st_weights(c):
                n, h = c
                w = jnp.exp(zt[c] + inc[c] + sss[n][h])
                if masked[n]:
                    w = jnp.where(valid[n], w, 0.0)
                dv_old[n] = dv_old[n] + _dot(w.astype(BF16), dohs[h])
                e[c] = w * dvt[c]
                pinc = _split_dot(m_low, e[c])
                big_e[c] = pinc - e[c] + ec[h]
                ec[h] = ec[h] + pinc[t - 1:t, :]

            def st_dscores(c):
                n, h = c
                dz = e[c] - jnp.exp(zt[c] - sp[c]) * (e[c] + big_e[c])
                if masked[n]:
                    dz = jnp.where(valid[n], dz, 0.0)
                dzb = dz.astype(BF16)
                dqs[h] = dqs[h] + _dot(kts[n], dzb)
                dk_old[n] = dk_old[n] + _dot(dzb, qhs[h])

            _emit_chains(chains, [st_scores, st_cumsum, st_weights, st_dscores], SB_BWD_SKEW)
            for n, rw in enumerate(rows):
                dv_acc[rw, :] = dv_old[n]
                dk_acc[rw, :] = dk_old[n]
            dq_acc[0] = dqs[0]
            dq_acc[1] = dqs[1]
            return tuple(ec)

        zero = jnp.zeros((1, t), F32)
        bpi = SB_BLOCKS_PER_TRIP
        carry = lax.fori_loop(0, n_edge, lambda it, c: sweep([jmin + it * 0], c, True), (zero, zero))
        carry = lax.fori_loop(0, n_mid // bpi, lambda it, c: sweep([mid0 + bpi * it + b for b in range(bpi)], c, False), carry)
        n_rem = n_mid % bpi
        carry = lax.fori_loop(0, n_rem, lambda it, c: sweep([i - 1 - n_rem + it], c, False), carry)
        lax.cond(pair, lambda: sweep([i - 1, i], carry, [False, True]), lambda: sweep([i], carry, True))
        dq_ref[...] = (jnp.where(sub < 64, dq_acc[0], dq_acc[1]) * scale).T

        @pl.when(i == nb - 1)
        def _():
            lanes = pl.ds(pl.multiple_of(hp * 128, 128), 128)
            c1 = pltpu.make_async_copy(dk_acc, dk_hbm.at[:, lanes], sem.at[0])
            c2 = pltpu.make_async_copy(dv_acc, dv_hbm.at[:, lanes], sem.at[1])
            c1.start()
            c2.start()
            c1.wait()
            c2.wait()

    return _pallas(
        body, name=name, grid=(4, nb),
        in_specs=[pl.BlockSpec((t, 128), lambda hp, i: (i, hp)),
                  pl.BlockSpec((t, 128), lambda hp, i: (i, hp)),
                  pl.BlockSpec((r, 128), lambda hp, i: (0, hp)),
                  pl.BlockSpec((r, 128), lambda hp, i: (0, hp)),
                  pl.BlockSpec((1, nb, 128, t), lambda hp, i: (hp, 0, 0, 0)),
                  pl.BlockSpec((2, 1, nbp, t), lambda hp, i: (hp, i, 0, 0))],
        out_specs=[pl.BlockSpec((t, 128), lambda hp, i: (i, hp)),
                   pl.BlockSpec(memory_space=pl.ANY), pl.BlockSpec(memory_space=pl.ANY)],
        out_shape=[jax.ShapeDtypeStruct((r, SB_WIDTH), F32),
                   jax.ShapeDtypeStruct((r, SB_WIDTH), F32), jax.ShapeDtypeStruct((r, SB_WIDTH), F32)],
        scratch_shapes=[pltpu.VMEM((r, 128), F32), pltpu.VMEM((r, 128), F32), pltpu.VMEM((2, 128, t), F32),
                        pltpu.SemaphoreType.DMA((2,))],
        compiler_params=_cparams(("arbitrary", "arbitrary")),
    )(q, do, k, v, kt3, ssave)


def _s5_disc(lam_re, lam_im, logdt, btr, bti):
    lr = jnp.minimum(lam_re, -1e-4)
    li = lam_im
    dt = jnp.exp(logdt)
    mag = jnp.exp(lr * dt)
    ang = li * dt
    a_re, a_im = mag * jnp.cos(ang), mag * jnp.sin(ang)
    den = lr * lr + li * li
    nr, ni = a_re - 1.0, a_im
    c_re = (nr * lr + ni * li) / den
    c_im = (ni * lr - nr * li) / den
    return a_re, a_im, c_re * btr - c_im * bti, c_re * bti + c_im * btr


def _s5_prep(lam_re, lam_im, logdt, btr, bti, name):
    ns = lam_re.shape[1]

    def body(lr_ref, li_ref, dt_ref, br_ref, bi_ref, ar_ref, ai_ref, bbr_ref, bbi_ref):
        ar, ai, bbr, bbi = _s5_disc(lr_ref[...], li_ref[...], dt_ref[...], br_ref[...], bi_ref[...])
        ar_ref[...] = ar
        ai_ref[...] = ai
        bbr_ref[...] = bbr
        bbi_ref[...] = bbi

    return _pallas(body, name=name,
                   out_shape=[jax.ShapeDtypeStruct((1, ns), F32)] * 2 + [jax.ShapeDtypeStruct((S5_GROUP, ns), F32)] * 2,
                   )(lam_re, lam_im, logdt, btr, bti)


def _s5_prep_bwd(lam_re, lam_im, logdt, btr, bti, dar, dai, dbbr, dbbi, name):
    ns = lam_re.shape[1]

    def body(lr_ref, li_ref, dt_ref, br_ref, bi_ref, dar_ref, dai_ref, dbr_ref, dbi_ref, o_lr, o_li, o_dt, o_br, o_bi):
        _, vjp = jax.vjp(_s5_disc, lr_ref[...], li_ref[...], dt_ref[...], br_ref[...], bi_ref[...])
        g = vjp((dar_ref[...], dai_ref[...], dbr_ref[...], dbi_ref[...]))
        o_lr[...] = g[0]
        o_li[...] = g[1]
        row = lax.broadcasted_iota(jnp.int32, (ns, ns), 0) // S5_STATE
        col = lax.broadcasted_iota(jnp.int32, (ns, ns), 1) // S5_STATE
        same = (row == col).astype(F32)
        o_dt[...] = _dot_hi(jnp.broadcast_to(g[2], (8, ns)), same)[0:1]
        o_br[...] = g[3]
        o_bi[...] = g[4]

    return _pallas(body, name=name,
                   out_shape=[jax.ShapeDtypeStruct((1, ns), F32)] * 3 + [jax.ShapeDtypeStruct((S5_GROUP, ns), F32)] * 2,
                   compiler_params=pltpu.CompilerParams(vmem_limit_bytes=VMEM_LIMIT),
                   )(lam_re, lam_im, logdt, btr, bti, dar, dai, dbbr, dbbi)


def _s5_scan(br, bi, ar, ai, t, reverse=False, carry=None):
    ng = t // 8
    ns = br.shape[1]
    br, bi = br.reshape(ng, 8, ns), bi.reshape(ng, 8, ns)
    row8 = lax.broadcasted_iota(jnp.int32, (1, 8, 1), 1)
    pr, pi_ = ar, ai
    for k in (1, 2, 4):
        if reverse:
            sr, si, ok = pltpu.roll(br, 8 - k, 1), pltpu.roll(bi, 8 - k, 1), row8 < 8 - k
        else:
            sr, si, ok = pltpu.roll(br, k, 1), pltpu.roll(bi, k, 1), row8 >= k
        sr = jnp.where(ok, sr, 0.0)
        si = jnp.where(ok, si, 0.0)
        br, bi = br + pr * sr - pi_ * si, bi + pr * si + pi_ * sr
        pr, pi_ = pr * pr - pi_ * pi_, 2.0 * pr * pi_
    pw_r, pw_i = [ar], [ai]
    for _ in range(7):
        pw_r.append(pw_r[-1] * ar - pw_i[-1] * ai)
        pw_i.append(pw_r[-2] * ai + pw_i[-1] * ar)
    if reverse:
        pw_r.reverse()
        pw_i.reverse()
    p8r, p8i = jnp.concatenate(pw_r, axis=0), jnp.concatenate(pw_i, axis=0)
    out_r, out_i = [None] * ng, [None] * ng
    order = range(ng - 1, -1, -1) if reverse else range(ng)
    edge = 0 if reverse else 7
    for g in order:
        gr, gi = br[g], bi[g]
        if carry is not None:
            cr, ci = carry
            gr, gi = gr + p8r * cr - p8i * ci, gi + p8r * ci + p8i * cr
        out_r[g], out_i[g] = gr, gi
        carry = (gr[edge:edge + 1], gi[edge:edge + 1])
    return jnp.concatenate(out_r, axis=0), jnp.concatenate(out_i, axis=0)


def _s5_prev_rows(x, first, t):
    ng = t // 8
    ns = x.shape[1]
    x3 = x.reshape(ng, 8, ns)
    last = x3[:, 7:8, :]
    before = jnp.concatenate([first.reshape(1, 1, ns), last[:ng - 1]], axis=0)
    row8 = lax.broadcasted_iota(jnp.int32, (1, 8, 1), 1)
    return jnp.where(row8 == 0, before, pltpu.roll(x3, 1, 1)).reshape(t, ns)


_GELU_C = math.sqrt(2.0 / math.pi)


def _gelu(y):
    th = jnp.tanh(_GELU_C * (y + 0.044715 * y * y * y))
    return 0.5 * y * (1.0 + th), th


def _sigmoid(x):
    return 1.0 / (1.0 + jnp.exp(-x))


def _s5_fwd(u, wb, a, wc, dskip, wglu, bglu, gnorm, name):
    r = u.shape[0]
    t = S5_TILE
    nt = r // t
    ns = wb.shape[2]
    w = S5_WIDTH

    def body(u_ref, wb_ref, a_ref, wc_ref, d_ref, wg_ref, bg_ref, gn_ref, y_ref, on_ref, xs_ref, carry_ref):
        i = pl.program_id(0)
        ar, ai = a_ref[0], a_ref[1]

        @pl.when(i == 0)
        def _():
            carry_ref[...] = jnp.zeros_like(carry_ref)

        u_ = u_ref[...]
        ub = u_.astype(BF16)
        xs_ref[0] = carry_ref[:, 0, :]
        chunks = list(range(S5_CHUNKS))
        sl_s = [slice(c * (ns // S5_CHUNKS), (c + 1) * (ns // S5_CHUNKS)) for c in chunks]
        sl_u = [slice(c * (w // S5_CHUNKS), (c + 1) * (w // S5_CHUNKS)) for c in chunks]
        bu, xs, ys = {}, {}, {}

        def st_inputs(c):
            bu[c] = (_dot(ub[:, sl_u[c]], wb_ref[0, sl_u[c], sl_s[c]]), _dot(ub[:, sl_u[c]], wb_ref[1, sl_u[c], sl_s[c]]))

        def st_scan(c):
            xr, xi = _s5_scan(*bu[c], ar[:, sl_s[c]], ai[:, sl_s[c]], t, carry=(carry_ref[0, :, sl_s[c]], carry_ref[1, :, sl_s[c]]))
            carry_ref[0, :, sl_s[c]] = xr[t - 1:t, :]
            carry_ref[1, :, sl_s[c]] = xi[t - 1:t, :]
            xs[c] = (xr.astype(BF16), xi.astype(BF16))

        def st_outputs(c):
            ys[c] = _dot(xs[c][0], wc_ref[0, sl_s[c], sl_u[c]]) - _dot(xs[c][1], wc_ref[1, sl_s[c], sl_u[c]])

        _emit_chains(chunks, [st_inputs, st_scan, st_outputs], False)
        y = jnp.concatenate([ys[c] for c in chunks], axis=1) + d_ref[...] * u_
        h, _ = _gelu(y)
        gate = _sigmoid(_dot(h.astype(BF16), wg_ref[...]) + bg_ref[...])
        y_ref[...] = y
        on_ref[...] = _rms(h * gate, gn_ref[...]).astype(BF16)

    full = lambda shape: pl.BlockSpec(shape, lambda i: (0,) * len(shape))
    return _pallas(
        body, name=name, grid=(nt,),
        in_specs=[_row_spec(w, t), full((2, w, ns)), full((2, 1, ns)), full((2, ns, w)), full((1, w)),
                  full((w, w)), full((1, w)), full((1, w))],
        out_specs=[_row_spec(w, t), pl.BlockSpec((t, w), lambda i: (i, 1)), pl.BlockSpec((1, 2, ns), lambda i: (i, 0, 0))],
        out_shape=[jax.ShapeDtypeStruct((r, w), F32), jax.ShapeDtypeStruct((r, 2 * w), BF16),
                   jax.ShapeDtypeStruct((nt, 2, ns), F32)],
        scratch_shapes=[pltpu.VMEM((2, 1, ns), F32)],
        compiler_params=_cparams(("arbitrary",)),
    )(u, wb, a, wc, dskip, wglu, bglu, gnorm)


def _s5_bwd(u, y, don, xstart, wb, a, wc, dskip, wglu, bglu, gnorm, name, ride=((), ()), don_block=0):
    r = u.shape[0]
    t = S5_TILE
    nt = r // t
    ns = wb.shape[2]
    w = S5_WIDTH
    nt_dims = ((1,), (1,))
    tn_dims = ((0,), (0,))

    def body(u_ref, y_ref, don_ref, xs_ref, wb_hbm, a_ref, wc_hbm, d_ref, wg_ref, bg_ref, gn_ref,
             du_ref, da_ref, dd_ref, dbg_ref, dgn_ref, dwb_hbm, dwc_hbm, dwg_hbm,
             wb_ref, wc_ref, lam_ref, acc_wb, acc_wc, acc_wg, sem):
        i = pl.program_id(0)
        ar, ai = a_ref[0], a_ref[1]

        @pl.when(i == 0)
        def _():
            c1 = pltpu.make_async_copy(wb_hbm, wb_ref, sem.at[0])
            c2 = pltpu.make_async_copy(wc_hbm, wc_ref, sem.at[1])
            c1.start()
            c2.start()
            lam_ref[...] = jnp.zeros_like(lam_ref)
            acc_wb[...] = jnp.zeros_like(acc_wb)
            acc_wc[...] = jnp.zeros_like(acc_wc)
            acc_wg[...] = jnp.zeros_like(acc_wg)
            da_ref[...] = jnp.zeros_like(da_ref)
            dd_ref[...] = jnp.zeros_like(dd_ref)
            dbg_ref[...] = jnp.zeros_like(dbg_ref)
            dgn_ref[...] = jnp.zeros_like(dgn_ref)
            c1.wait()
            c2.wait()

        u_ = u_ref[...]
        y_ = y_ref[...]
        ub = u_.astype(BF16)
        h, th = _gelu(y_)
        hb = h.astype(BF16)
        wg = wg_ref[...]
        gate = _sigmoid(_dot(hb, wg) + bg_ref[...])
        d_out, dgn = _rms_bwd(h * gate, gn_ref[...], don_ref[...])
        dgn_ref[...] += dgn
        dhw = d_out * h * gate * (1.0 - gate)
        dhwb = dhw.astype(BF16)
        dh = d_out * gate + _dot(dhwb, wg, nt_dims)
        acc_wg[...] += _dot(hb, dhwb, tn_dims)
        dbg_ref[...] += jnp.sum(dhw, axis=0, keepdims=True)
        dgelu = 0.5 * (1.0 + th) + 0.5 * y_ * (1.0 - th * th) * _GELU_C * (1.0 + 3.0 * 0.044715 * y_ * y_)
        dy = dh * dgelu
        dd_ref[...] += jnp.sum(dy * u_, axis=0, keepdims=True)
        dyb = dy.astype(BF16)
        chunks = list(range(S5_CHUNKS))
        sl_s = [slice(c * (ns // S5_CHUNKS), (c + 1) * (ns // S5_CHUNKS)) for c in chunks]
        sl_u = [slice(c * (w // S5_CHUNKS), (c + 1) * (w // S5_CHUNKS)) for c in chunks]
        bu, gx, x_, lam, dus = {}, {}, {}, {}, {}

        def st_inputs(c):
            su, ss = sl_u[c], sl_s[c]
            bu[c] = (_dot(ub[:, su], wb_ref[0, su, ss]), _dot(ub[:, su], wb_ref[1, su, ss]))
            gx[c] = (_dot(dyb[:, su], wc_ref[0, ss, su], nt_dims), -_dot(dyb[:, su], wc_ref[1, ss, su], nt_dims))

        def st_states(c):
            su, ss = sl_u[c], sl_s[c]
            first = (xs_ref[0, 0:1, ss], xs_ref[0, 1:2, ss])
            xr, xi = _s5_scan(*bu[c], ar[:, ss], ai[:, ss], t, carry=first)
            acc_wc[0, ss, su] += _dot(xr.astype(BF16), dyb[:, su], tn_dims)
            acc_wc[1, ss, su] -= _dot(xi.astype(BF16), dyb[:, su], tn_dims)
            x_[c] = (_s5_prev_rows(xr, first[0], t), _s5_prev_rows(xi, first[1], t))

        def st_adjoint(c):
            su, ss = sl_u[c], sl_s[c]
            lr, li = _s5_scan(*gx[c], ar[:, ss], -ai[:, ss], t, reverse=True, carry=(lam_ref[0, :, ss], lam_ref[1, :, ss]))
            lam_ref[0, :, ss] = lr[0:1, :]
            lam_ref[1, :, ss] = li[0:1, :]
            lrb, lib = lr.astype(BF16), li.astype(BF16)
            acc_wb[0, su, ss] += _dot(ub[:, su], lrb, tn_dims)
            acc_wb[1, su, ss] += _dot(ub[:, su], lib, tn_dims)
            dus[c] = _dot(lrb, wb_ref[0, su, ss], nt_dims) + _dot(lib, wb_ref[1, su, ss], nt_dims)
            lam[c] = (lr, li)

        def st_decay(c):
            ss = sl_s[c]
            (lr, li), (xpr, xpi) = lam[c], x_[c]
            da_ref[0, :, ss] += jnp.sum(lr * xpr + li * xpi, axis=0, keepdims=True)
            da_ref[1, :, ss] += jnp.sum(li * xpr - lr * xpi, axis=0, keepdims=True)

        _emit_chains(chunks, [st_inputs, st_states, st_adjoint, st_decay], False)
        du_ref[...] = d_ref[...] * dy + jnp.concatenate([dus[c] for c in chunks], axis=1)

        @pl.when(i == nt - 1)
        def _():
            cps = [pltpu.make_async_copy(acc_wb, dwb_hbm, sem.at[0]), pltpu.make_async_copy(acc_wc, dwc_hbm, sem.at[1]),
                   pltpu.make_async_copy(acc_wg, dwg_hbm, sem.at[2])]
            for c in cps:
                c.start()
            for c in cps:
                c.wait()

    n_ride = len(ride[0])
    n_in, n_out, n_scratch = 11, 8, 7

    def body_with_ride(*refs):
        ins, rest = refs[:n_in], refs[n_in:]
        ride_in, rest = rest[:n_ride], rest[n_ride:]
        outs, rest = rest[:n_out], rest[n_out:]
        ride_out, rest = rest[:n_ride], rest[n_ride:]
        scratch, ride_sems = rest[:n_scratch], rest[n_scratch:]
        if n_ride:
            @pl.when(pl.program_id(0) == 0)
            def _():
                for cp in _exchange_copies(ride_in, ride_out, ride[1], *ride_sems):
                    cp.start()
        body(*ins, *outs, *scratch)
        if n_ride:
            @pl.when(pl.program_id(0) == nt - 1)
            def _():
                for cp in _exchange_copies(ride_in, ride_out, ride[1], *ride_sems):
                    cp.wait()

    rev = lambda i: (nt - 1 - i, 0)
    full = lambda shape: pl.BlockSpec(shape, lambda i: (0,) * len(shape))
    hbm = pl.BlockSpec(memory_space=pl.ANY)
    outs = _pallas(
        body_with_ride, name=name, grid=(nt,),
        in_specs=[pl.BlockSpec((t, w), rev), pl.BlockSpec((t, w), rev), pl.BlockSpec((t, w), lambda i: (nt - 1 - i, don_block)),
                  pl.BlockSpec((1, 2, ns), lambda i: (nt - 1 - i, 0, 0)), hbm, full((2, 1, ns)), hbm, full((1, w)),
                  full((w, w)), full((1, w)), full((1, w))] + [hbm] * n_ride,
        out_specs=[pl.BlockSpec((t, w), rev), full((2, 1, ns)), full((1, w)), full((1, w)), full((1, w)), hbm, hbm, hbm]
        + [hbm] * n_ride,
        out_shape=[jax.ShapeDtypeStruct((r, w), F32), jax.ShapeDtypeStruct((2, 1, ns), F32)]
        + [jax.ShapeDtypeStruct((1, w), F32)] * 3
        + [jax.ShapeDtypeStruct((2, w, ns), F32), jax.ShapeDtypeStruct((2, ns, w), F32), jax.ShapeDtypeStruct((w, w), F32)]
        + _exchange_shapes(*ride),
        scratch_shapes=[pltpu.VMEM((2, w, ns), BF16), pltpu.VMEM((2, ns, w), BF16), pltpu.VMEM((2, 1, ns), F32),
                        pltpu.VMEM((2, w, ns), F32), pltpu.VMEM((2, ns, w), F32), pltpu.VMEM((w, w), F32),
                        pltpu.SemaphoreType.DMA((3,))] + (_exchange_sems(n_ride) if n_ride else []),
        compiler_params=_cparams(("arbitrary",)),
    )(u, y, don, xstart, wb, a, wc, dskip, wglu, bglu, gnorm, *ride[0])
    return tuple(outs[:n_out]) + (list(outs[n_out:]),)


def _s5_expand(lam_re, lam_im, log_dt, b_re, b_im, c_re, c_im):
    g, n, p = S5_GROUPS, S5_STATE, S5_GROUP
    ns = g * n
    rows = lambda x: x.reshape(1, ns)
    logdt = jnp.repeat(log_dt.reshape(g), n).reshape(1, ns)
    btr = b_re.reshape(ns, p).T
    bti = b_im.reshape(ns, p).T
    ctr = c_re.transpose(0, 2, 1).reshape(ns, p)
    cti = c_im.transpose(0, 2, 1).reshape(ns, p)
    mask = (jnp.arange(g * p)[:, None] // p) == (jnp.arange(ns)[None, :] // n)
    return rows(lam_re), rows(lam_im), logdt, btr, bti, ctr, cti, mask


def _s5_block_diag_b(bb, mask):
    return jnp.where(mask, jnp.tile(bb, (S5_GROUPS, 1)), 0.0)


def _s5_block_diag_c(ct, mask):
    return jnp.where(mask.T, jnp.tile(ct, (1, S5_GROUPS)), 0.0)


def _s5_diag_of_b(dwb, mask):
    return jnp.where(mask, dwb, 0.0).reshape(S5_GROUPS, S5_GROUP, -1).sum(0)


def _s5_diag_of_c(dwc, mask):
    ns = dwc.shape[0]
    return jnp.where(mask.T, dwc, 0.0).reshape(ns, S5_GROUPS, S5_GROUP).sum(1)


DN_PRE_TILE = 256
_DN_QKV = 3 * DN_WIDTH


def _halo_specs(width, tile, nt, prev):
    per = tile // 8
    if prev:
        return pl.BlockSpec((8, width), lambda i: (jnp.maximum(i * per - 1, 0), 0))
    return pl.BlockSpec((8, width), lambda i: (jnp.minimum((i + 1) * per, nt * per - 1), 0))


def _shift_down(x, halo, s, t):
    xx = jnp.concatenate([halo, x], axis=0)
    return pltpu.roll(xx, s, 0)[8:]


def _shift_up(x, halo, s, t):
    xx = jnp.concatenate([x, halo], axis=0)
    return pltpu.roll(xx, t + 8 - s, 0)[:t]


def _silu(x):
    s = _sigmoid(x)
    return x * s, s


def _dn_gates(ab, alog, dtb, live):
    lane = lax.broadcasted_iota(jnp.int32, (1, 128), 1)
    g = -jnp.exp(alog) * _softplus(ab + dtb)
    beta = _sigmoid(ab)
    return jnp.where(live & (lane < DN_HEADS), g, jnp.where(live & (lane < 2 * DN_HEADS), beta, 0.0))


def _dn_pre_fwd(proj, ab, conv_w, alog, dtb, pad, name):
    r = proj.shape[0]
    t = DN_PRE_TILE
    nt = r // t
    scale = DN_HEAD_DIM ** -0.5

    def body(x_ref, halo_ref, ab_ref, w_ref, al_ref, dt_ref, q_ref, k_ref, v_ref, gb_ref):
        i = pl.program_id(0)
        act, _ = _silu(_dn_conv(x_ref[...], jnp.where(i > 0, halo_ref[...], 0.0), w_ref[...], t))
        for hd in range(DN_HEADS):
            sl = slice(hd * 128, (hd + 1) * 128)
            for base, o_ref, sc in ((0, q_ref, scale), (DN_WIDTH, k_ref, 1.0)):
                xh = act[:, base + hd * 128: base + (hd + 1) * 128]
                o_ref[:, sl] = (xh * (lax.rsqrt(jnp.sum(xh * xh, axis=-1, keepdims=True) + EPS) * sc)).astype(BF16)
        v_ref[...] = act[:, 2 * DN_WIDTH:].astype(BF16)
        rows = i * t + lax.broadcasted_iota(jnp.int32, (t, 1), 0)
        gb_ref[...] = _dn_gates(ab_ref[...], al_ref[...], dt_ref[...], rows >= pad)

    return _pallas(
        body, name=name, grid=(nt,),
        in_specs=[pl.BlockSpec((t, _DN_QKV), lambda i: (i, 0)), _halo_specs(_DN_QKV, t, nt, True), _row_spec(128, t),
                  pl.BlockSpec((DN_CONV, _DN_QKV), lambda i: (0, 0)), _vec_spec(128), _vec_spec(128)],
        out_specs=[_row_spec(DN_WIDTH, t), _row_spec(DN_WIDTH, t), _row_spec(DN_WIDTH, t), _row_spec(128, t)],
        out_shape=[jax.ShapeDtypeStruct((r, DN_WIDTH), BF16)] * 3 + [jax.ShapeDtypeStruct((r, 128), F32)],
        compiler_params=_cparams(("parallel",)),
    )(proj, proj, ab, conv_w, alog, dtb)


def _dn_conv(x, halo, w, t):
    co = w[DN_CONV - 1:DN_CONV] * x
    for tap in range(DN_CONV - 1):
        co = co + w[tap:tap + 1] * _shift_down(x, halo, DN_CONV - 1 - tap, t)
    return co


def _dn_pre_bwd(proj, conv_w, dq, dk, dv, dgb, ab, alog, dtb, pad, name):
    r = proj.shape[0]
    t = DN_PRE_TILE
    nt = r // t
    scale = DN_HEAD_DIM ** -0.5

    def body(x_ref, halo_ref, w_ref, dq_ref, dk_ref, dv_ref, dgb_ref, ab_ref, al_ref, dt_ref, dco_ref, dab_ref, dal_ref,
             ddt_ref):
        i = pl.program_id(0)

        @pl.when(i == 0)
        def _():
            dal_ref[...] = jnp.zeros_like(dal_ref)
            ddt_ref[...] = jnp.zeros_like(ddt_ref)

        co_ = _dn_conv(x_ref[...], jnp.where(i > 0, halo_ref[...], 0.0), w_ref[...], t)
        act, sg = _silu(co_)
        dsilu = sg * (1.0 + co_ * (1.0 - sg))
        for hd in range(DN_HEADS):
            sl = slice(hd * 128, (hd + 1) * 128)
            for base, d_ref, sc in ((0, dq_ref, scale), (DN_WIDTH, dk_ref, 1.0)):
                cs = slice(base + hd * 128, base + (hd + 1) * 128)
                xh = act[:, cs]
                rn = lax.rsqrt(jnp.sum(xh * xh, axis=-1, keepdims=True) + EPS)
                xhat = xh * rn
                dy = d_ref[:, sl]
                dx = (sc * rn) * (dy - xhat * jnp.sum(dy * xhat, axis=-1, keepdims=True))
                dco_ref[:, cs] = dx * dsilu[:, cs]
        dco_ref[:, 2 * DN_WIDTH:] = dv_ref[...] * dsilu[:, 2 * DN_WIDTH:]
        rows = i * t + lax.broadcasted_iota(jnp.int32, (t, 1), 0)
        live = rows >= pad
        lane = lax.broadcasted_iota(jnp.int32, (1, 128), 1)
        ab_ = ab_ref[...]
        dgb_ = dgb_ref[...]
        is_g = live & (lane < DN_HEADS)
        is_b = live & (lane >= DN_HEADS) & (lane < 2 * DN_HEADS)
        arg = ab_ + dt_ref[...]
        ea = jnp.exp(al_ref[...])
        da = jnp.where(is_g, -dgb_ * ea * _sigmoid(arg), 0.0)
        beta = _sigmoid(ab_)
        dab_ref[...] = (da + jnp.where(is_b, dgb_ * beta * (1.0 - beta), 0.0)).astype(BF16)
        ddt_ref[...] += jnp.sum(da, axis=0, keepdims=True)
        dal_ref[...] += jnp.sum(jnp.where(is_g, -dgb_ * ea * _softplus(arg), 0.0), axis=0, keepdims=True)

    return _pallas(
        body, name=name, grid=(nt,),
        in_specs=[pl.BlockSpec((t, _DN_QKV), lambda i: (i, 0)), _halo_specs(_DN_QKV, t, nt, True),
                  pl.BlockSpec((DN_CONV, _DN_QKV), lambda i: (0, 0)),
                  _row_spec(DN_WIDTH, t), _row_spec(DN_WIDTH, t), _row_spec(DN_WIDTH, t),
                  _row_spec(128, t), _row_spec(128, t), _vec_spec(128), _vec_spec(128)],
        out_specs=[_row_spec(_DN_QKV, t), _row_spec(128, t), _vec_spec(128), _vec_spec(128)],
        out_shape=[jax.ShapeDtypeStruct((r, _DN_QKV), F32), jax.ShapeDtypeStruct((r, 128), BF16),
                   jax.ShapeDtypeStruct((1, 128), F32), jax.ShapeDtypeStruct((1, 128), F32)],
        compiler_params=_cparams(("arbitrary",)),
    )(proj, proj, conv_w, dq, dk, dv, dgb, ab, alog, dtb)


def _dn_conv_bwd(dco, proj, conv_w, dproj, name):
    r = dco.shape[0]
    t = DN_PRE_TILE
    nt = r // t

    def body(d_ref, dh_ref, x_ref, xh_ref, w_ref, dproj_in, dx_ref, dw_ref):
        i = pl.program_id(0)

        @pl.when(i == 0)
        def _():
            dw_ref[...] = jnp.zeros_like(dw_ref)

        d = d_ref[...]
        dhalo = jnp.where(i < nt - 1, dh_ref[...], 0.0)
        x = x_ref[...]
        xhalo = jnp.where(i > 0, xh_ref[...], 0.0)
        w = w_ref[...]
        dx = w[3:4] * d
        dws = [None] * DN_CONV
        dws[3] = jnp.sum(d * x, axis=0, keepdims=True)
        for tap in range(DN_CONV - 1):
            s = DN_CONV - 1 - tap
            dx = dx + w[tap:tap + 1] * _shift_up(d, dhalo, s, t)
            dws[tap] = jnp.sum(d * _shift_down(x, xhalo, s, t), axis=0, keepdims=True)
        dx_ref[...] = dx.astype(BF16)
        dw_ref[...] += jnp.concatenate(dws + [jnp.zeros((8 - DN_CONV, _DN_QKV), F32)], axis=0)

    return _pallas(
        body, name=name, grid=(nt,),
        in_specs=[_row_spec(_DN_QKV, t), _halo_specs(_DN_QKV, t, nt, False),
                  pl.BlockSpec((t, _DN_QKV), lambda i: (i, 0)), _halo_specs(_DN_QKV, t, nt, True),
                  pl.BlockSpec((DN_CONV, _DN_QKV), lambda i: (0, 0)), pl.BlockSpec(memory_space=pl.ANY)],
        out_specs=[_row_spec(_DN_QKV, t), pl.BlockSpec((8, _DN_QKV), lambda i: (0, 0))],
        out_shape=[jax.ShapeDtypeStruct(dproj.shape, BF16), jax.ShapeDtypeStruct((8, _DN_QKV), F32)],
        input_output_aliases={5: 0},
        compiler_params=_cparams(("arbitrary",)),
    )(dco, dco, proj, proj, conv_w, dproj)


def _split3(x):
    hi = x.astype(BF16)
    return hi, (x - hi.astype(F32)).astype(BF16)


def _dot3s(a, b, dims=((1,), (0,))):
    return _dot(a[0], b[0], dims) + (_dot(a[0], b[1], dims) + _dot(a[1], b[0], dims))


def _dot2s(a, b, dims=((1,), (0,))):
    return _dot(a[0], b[0], dims) + _dot(a[1], b[0], dims)


def _dot3(a, b, dims=((1,), (0,))):
    return _dot3s(_split3(a), _split3(b), dims)


def _dn_inverse_many(n_mats):
    c = n_mats[0].shape[0]
    row = lax.broadcasted_iota(jnp.int32, (c, c), 0)
    col = lax.broadcasted_iota(jnp.int32, (c, c), 1)
    eye = (row == col).astype(F32)
    same = row // DN_SUB == col // DN_SUB
    nds = [jnp.where(same, n, 0.0) for n in n_mats]
    nos = [n - nd for n, nd in zip(n_mats, nds)]

    def geometric(bs, order):
        xs = [eye + b for b in bs]
        sp = [_split3(b) for b in bs]
        k = 2
        while k < order:
            sp = [_split3(_dot2s(s_, s_)) for s_ in sp]
            xs = [x + _dot2s(_split3(x), s_) for x, s_ in zip(xs, sp)]
            k *= 2
        return xs

    tds = [_split3(td) for td in geometric([-nd for nd in nds], DN_SUB)]
    ms = [_dot3s(td, _split3(no)) for td, no in zip(tds, nos)]
    xs = geometric([-m for m in ms], c // DN_SUB)
    return [_dot3s(_split3(x), td) for x, td in zip(xs, tds)]


def _dn_chunk_shared(gb_ref, gbt_ref):
    c = DN_CHUNK
    row = lax.broadcasted_iota(jnp.int32, (c, c), 0)
    col = lax.broadcasted_iota(jnp.int32, (c, c), 1)
    gbv = gb_ref[...]
    gam_all = _split_dot((row >= col).astype(BF16), gbv)
    hi, lo = _split3(gbt_ref[...])
    tri_t = (row <= col).astype(BF16)
    return dict(row=row, col=col, gbv=gbv, gam_all=gam_all, gam_rows=_dot(hi, tri_t) + _dot(lo, tri_t),
                lane=lax.broadcasted_iota(jnp.int32, (1, 128), 1))


def _dn_chunk_common(q, k, v, sh, h):
    c = DN_CHUNK
    row, col, lane = sh["row"], sh["col"], sh["lane"]
    q, k, v = q.astype(F32), k.astype(F32), v.astype(F32)
    gam = jnp.sum(jnp.where(lane == h, sh["gam_all"], 0.0), axis=1, keepdims=True)
    beta = jnp.sum(jnp.where(lane == h + DN_HEADS, sh["gbv"], 0.0), axis=1, keepdims=True)
    gam_row = sh["gam_rows"][h:h + 1]
    dec = jnp.where(row >= col, jnp.exp(jnp.minimum(gam - gam_row, 0.0)), 0.0)
    kb, qb = k.astype(BF16), q.astype(BF16)
    nt_dims = ((1,), (1,))
    kk = _dot(kb, kb, nt_dims)
    qk = _dot(qb, kb, nt_dims)
    eg = jnp.exp(gam)
    gam_l = gam[c - 1:c, :]
    return dict(q=q, k=k, v=v, qb=qb, kb=kb, gam=gam, beta=beta, dec=dec, kk=kk, qk=qk, eg=eg, gam_l=gam_l,
                row=row, col=col, lane=lane, att=qk * dec, qg=q * eg, kt=k * jnp.exp(gam_l - gam),
                rhs=jnp.concatenate([v * beta, k * (beta * eg)], axis=1))


def _dn_fwd(q, k, v, gb, gbt, name):
    r = q.shape[0]
    c = DN_CHUNK
    nc = r // c
    dh = DN_HEAD_DIM
    tn_dims = ((0,), (0,))

    def body(q_ref, k_ref, v_ref, gb_ref, gbt_ref, o_ref, ss_ref, ts_ref, s_ref):
        @pl.when(pl.program_id(0) == 0)
        def _():
            s_ref[...] = jnp.zeros_like(s_ref)

        heads = list(range(DN_HEADS))
        sl = [slice(h * dh, (h + 1) * dh) for h in heads]
        sh = _dn_chunk_shared(gb_ref, gbt_ref)
        zs = [_dn_chunk_common(q_ref[:, sl[h]], k_ref[:, sl[h]], v_ref[:, sl[h]], sh, h) for h in heads]
        t_invs = _dn_inverse_many([jnp.where(sh["row"] > sh["col"], z["beta"] * z["kk"] * z["dec"], 0.0) for z in zs])
        sols = [_dot3(t_inv, z["rhs"]) for t_inv, z in zip(t_invs, zs)]
        ss = [s_ref[h] for h in heads]
        sbs = [s.astype(BF16) for s in ss]
        vnbs = [(sol[:, :dh] - _dot(sol[:, dh:].astype(BF16), sb)).astype(BF16) for sol, sb in zip(sols, sbs)]
        for h in heads:
            o_ref[:, sl[h]] = _dot(zs[h]["qg"].astype(BF16), sbs[h]) + _dot(zs[h]["att"].astype(BF16), vnbs[h])
        for h in heads:
            ss_ref[0, h] = ss[h]
            ts_ref[0, h] = t_invs[h]
            s_ref[h] = ss[h] * jnp.exp(zs[h]["gam_l"]) + _dot(zs[h]["kt"].astype(BF16), vnbs[h], tn_dims)

    blk = pl.BlockSpec((c, DN_WIDTH), lambda ci: (ci, 0))
    sav = pl.BlockSpec((1, DN_HEADS, dh, dh), lambda ci: (ci, 0, 0, 0))
    return _pallas(
        body, name=name, grid=(nc,),
        in_specs=[blk, blk, blk, pl.BlockSpec((c, 128), lambda ci: (ci, 0)), pl.BlockSpec((16, c), lambda ci: (0, ci))],
        out_specs=[blk, sav, sav],
        out_shape=[jax.ShapeDtypeStruct((r, DN_WIDTH), F32), jax.ShapeDtypeStruct((nc, DN_HEADS, dh, dh), F32),
                   jax.ShapeDtypeStruct((nc, DN_HEADS, dh, dh), F32)],
        scratch_shapes=[pltpu.VMEM((DN_HEADS, dh, dh), F32)],
        compiler_params=_cparams(("arbitrary",)),
    )(q, k, v, gb, gbt)


def _dn_bwd(q, k, v, gb, gbt, ssave, tsave, do, name):
    r = q.shape[0]
    c = DN_CHUNK
    nc = r // c
    dh = DN_HEAD_DIM
    nt_dims = ((1,), (1,))
    tn_dims = ((0,), (0,))

    def body(q_ref, k_ref, v_ref, gb_ref, gbt_ref, ss_ref, ts_ref, do_ref, dq_ref, dk_ref, dv_ref, dgb_ref, ds_ref):
        @pl.when(pl.program_id(0) == 0)
        def _():
            ds_ref[...] = jnp.zeros_like(ds_ref)

        heads = list(range(DN_HEADS))
        sl = [slice(h * dh, (h + 1) * dh) for h in heads]
        sh = _dn_chunk_shared(gb_ref, gbt_ref)
        row, col, lane = sh["row"], sh["col"], sh["lane"]
        rs = lambda x: jnp.sum(x, axis=1, keepdims=True)
        tot = lambda x: jnp.sum(rs(x), axis=0, keepdims=True)
        st = [dict() for _ in heads]
        dgb_parts = []

        def s_common(h):
            st[h].update(_dn_chunk_common(q_ref[:, sl[h]], k_ref[:, sl[h]], v_ref[:, sl[h]], sh, h))
            st[h]["t"] = _split3(ts_ref[0, h])

        def s_sol(h):
            st[h]["sol"] = _dot3s(st[h]["t"], _split3(st[h]["rhs"]))

        def s_state(h):
            z = st[h]
            sol = z["sol"]
            kcd = sol[:, dh:]
            s = ss_ref[0, h]
            sb = s.astype(BF16)
            vnb = (sol[:, :dh] - _dot(kcd.astype(BF16), sb)).astype(BF16)
            ds_next = ds_ref[h]
            dsb = ds_next.astype(BF16)
            dob = do_ref[:, sl[h]].astype(BF16)
            z["dqg"] = _dot(dob, sb, nt_dims)
            ds = _dot(z["qg"].astype(BF16), dob, tn_dims)
            z["d_att"] = jnp.where(row >= col, _dot(dob, vnb, nt_dims), 0.0)
            dvn = _dot(z["att"].astype(BF16), dob, tn_dims) + _dot(z["kt"].astype(BF16), dsb)
            z["dkt"] = _dot(vnb, dsb, nt_dims)
            eg_l = jnp.exp(z["gam_l"])
            ds = ds + ds_next * eg_l
            z["dgam_l"] = tot(ds_next * s) * eg_l
            dvnb = dvn.astype(BF16)
            dkcd = -_dot(dvnb, sb, nt_dims)
            ds_ref[h] = ds - _dot(kcd.astype(BF16), dvnb, tn_dims)
            z["dsol"] = jnp.concatenate([dvn, dkcd], axis=1)

        def s_drhs(h):
            st[h]["drhs"] = _dot3s(st[h]["t"], _split3(st[h]["dsol"]), tn_dims)

        def s_dn(h):
            z = st[h]
            z["dn"] = jnp.where(row > col, -_dot3(z["drhs"], z["sol"], nt_dims), 0.0)

        def s_rest(h):
            z = st[h]
            k_, v_, kb, qb = z["k"], z["v"], z["kb"], z["qb"]
            beta, eg, dec, kk, qk, gam, gam_l = z["beta"], z["eg"], z["dec"], z["kk"], z["qk"], z["gam"], z["gam_l"]
            dn, d_att, dqg, dkt = z["dn"], z["d_att"], z["dqg"], z["dkt"]
            drv, drk = z["drhs"][:, :dh], z["drhs"][:, dh:]
            s_rkk = rs(drk * k_)
            dv_ref[:, sl[h]] = drv * beta
            dbeta = rs(drv * v_) + s_rkk * eg + rs(dn * kk * dec)
            dk = drk * (beta * eg)
            dgam = s_rkk * beta * eg
            dkk = (dn * beta * dec).astype(BF16)
            dd = dn * beta * kk + d_att * qk
            dqk = (d_att * dec).astype(BF16)
            dq_ref[:, sl[h]] = _dot(dqk, kb) + dqg * eg
            dk = dk + _dot(dqk, qb, tn_dims) + _dot(dkk, kb) + _dot(dkk, kb, tn_dims)
            w = dd * dec
            wh, wl = _split3(w)
            ones = jnp.ones((c, 128), BF16)
            col_sum = (_dot(wh, ones, tn_dims) + _dot(wl, ones, tn_dims))[:, 0:1]
            dgam = dgam + rs(w) - col_sum + rs(dqg * z["qg"]) - rs(dkt * z["kt"])
            dk_ref[:, sl[h]] = dk + dkt * jnp.exp(gam_l - gam)
            dgam_l = z["dgam_l"] + tot(dkt * z["kt"])
            rowc = lax.broadcasted_iota(jnp.int32, (c, 1), 0)
            dgam = dgam + jnp.where(rowc == c - 1, dgam_l, 0.0)
            dg = _split_dot((row <= col).astype(BF16), jnp.broadcast_to(dgam, (c, 128)))[:, 0:1]
            dgb_parts.append(jnp.where(lane == h, dg, 0.0) + jnp.where(lane == h + DN_HEADS, dbeta, 0.0))

        _emit_chains(heads, [s_common, s_sol, s_state, s_drhs, s_dn, s_rest], False)
        dgb = dgb_parts[0]
        for part in dgb_parts[1:]:
            dgb = dgb + part
        dgb_ref[...] = dgb

    blk = pl.BlockSpec((c, DN_WIDTH), lambda ci: (nc - 1 - ci, 0))
    sav = pl.BlockSpec((1, DN_HEADS, dh, dh), lambda ci: (nc - 1 - ci, 0, 0, 0))
    gspec = pl.BlockSpec((c, 128), lambda ci: (nc - 1 - ci, 0))
    return _pallas(
        body, name=name, grid=(nc,),
        in_specs=[blk, blk, blk, gspec, pl.BlockSpec((16, c), lambda ci: (0, nc - 1 - ci)), sav, sav, blk],
        out_specs=[blk, blk, blk, gspec],
        out_shape=[jax.ShapeDtypeStruct((r, DN_WIDTH), F32)] * 3 + [jax.ShapeDtypeStruct((r, 128), F32)],
        scratch_shapes=[pltpu.VMEM((DN_HEADS, dh, dh), F32)],
        compiler_params=_cparams(("arbitrary",)),
    )(q, k, v, gb, gbt, ssave, tsave, do)


def _dn_post_fwd(o, proj, g, name):
    r = o.shape[0]

    def body(o_ref, z_ref, g_ref, y_ref):
        g_ = g_ref[...]
        for hd in range(DN_HEADS):
            sl = slice(hd * 128, (hd + 1) * 128)
            sz, _ = _silu(z_ref[:, sl])
            y_ref[:, sl] = (_rms(o_ref[:, sl], g_) * sz).astype(BF16)

    return _pallas(body, name=name, grid=(r // ROW_TILE,),
                   in_specs=[_row_spec(DN_WIDTH), pl.BlockSpec((ROW_TILE, DN_WIDTH), lambda i: (i, 3)), _vec_spec(128)],
                   out_specs=_row_spec(DN_WIDTH), out_shape=jax.ShapeDtypeStruct((r, DN_WIDTH), BF16),
                   compiler_params=_cparams(("parallel",)))(o, proj, g)


def _dn_post_bwd(o, proj, g, dy, name):
    r = o.shape[0]

    def body(o_ref, z_ref, g_ref, dy_ref, do_ref, dz_ref, dg_ref):
        @pl.when(pl.program_id(0) == 0)
        def _():
            dg_ref[...] = jnp.zeros_like(dg_ref)

        g_ = g_ref[...]
        for hd in range(DN_HEADS):
            sl = slice(hd * 128, (hd + 1) * 128)
            z_ = z_ref[:, sl]
            sz, sg = _silu(z_)
            dy_ = dy_ref[:, sl]
            o_ = o_ref[:, sl]
            dz_ref[:, sl] = (dy_ * _rms(o_, g_) * (sg * (1.0 + z_ * (1.0 - sg)))).astype(BF16)
            dx, dg = _rms_bwd(o_, g_, dy_ * sz)
            do_ref[:, sl] = dx
            dg_ref[...] += dg

    return _pallas(body, name=name, grid=(r // ROW_TILE,),
                   in_specs=[_row_spec(DN_WIDTH), pl.BlockSpec((ROW_TILE, DN_WIDTH), lambda i: (i, 3)), _vec_spec(128),
                             _row_spec(DN_WIDTH)],
                   out_specs=[_row_spec(DN_WIDTH), pl.BlockSpec((ROW_TILE, DN_WIDTH), lambda i: (i, 3)), _vec_spec(128)],
                   out_shape=[jax.ShapeDtypeStruct((r, DN_WIDTH), F32), jax.ShapeDtypeStruct((r, 4 * DN_WIDTH), BF16),
                              jax.ShapeDtypeStruct((1, 128), F32)],
                   compiler_params=_cparams(("arbitrary",)))(o, proj, g, dy)


def _exchange(arrays, scatter, name):
    n = len(arrays)

    def body(*refs):
        copies = _exchange_copies(refs[:n], refs[n:2 * n], scatter, *refs[2 * n:])
        for cp in copies:
            cp.start()
        for cp in copies:
            cp.wait()

    hbm = pl.BlockSpec(memory_space=pl.ANY)
    return _pallas(
        body, name=name, in_specs=[hbm] * n, out_specs=[hbm] * n, out_shape=_exchange_shapes(arrays, scatter),
        scratch_shapes=_exchange_sems(n),
    )(*arrays)


def _exchange_shapes(arrays, scatter):
    return [jax.ShapeDtypeStruct((N_DEV,) + (a.shape[1:] if sc else a.shape), a.dtype) for a, sc in zip(arrays, scatter)]


def _exchange_sems(n):
    return [pltpu.SemaphoreType.DMA((n * N_DEV,)), pltpu.SemaphoreType.DMA((n * N_DEV,)), pltpu.SemaphoreType.DMA((n,))]


def _exchange_copies(in_refs, out_refs, scatter, send_sems, recv_sems, local_sems):
    mx, my, mc = lax.axis_index("x"), lax.axis_index("y"), lax.axis_index("c")
    me = 4 * mx + 2 * my + mc
    copies = []
    for a in range(len(in_refs)):
        src_own = in_refs[a].at[me] if scatter[a] else in_refs[a]
        copies.append(pltpu.make_async_copy(src_own, out_refs[a].at[me], local_sems.at[a]))
        for kbits in range(1, N_DEV):
            px = lax.rem(mx + ((kbits >> 2) & 1), 2)
            py = lax.rem(my + ((kbits >> 1) & 1), 2)
            pc = lax.rem(mc + (kbits & 1), 2)
            src = in_refs[a].at[4 * px + 2 * py + pc] if scatter[a] else in_refs[a]
            copies.append(pltpu.make_async_remote_copy(
                src_ref=src, dst_ref=out_refs[a].at[me],
                send_sem=send_sems.at[a * N_DEV + kbits], recv_sem=recv_sems.at[a * N_DEV + kbits],
                device_id=(px, py, pc), device_id_type=pl.DeviceIdType.MESH))
    return copies


def _adamw(gstack, w, m, v, name):
    a, b = w.shape
    ta = a
    for t in (1024, 512, 256, 128, 64, 32, 16, 8):
        if a % t == 0 and N_DEV * t * b * 4 <= 4 * 1024 * 1024:
            ta = t
            break
    c1 = 1.0 / (1.0 - ADAM_B1 ** ADAM_STEP)
    c2 = 1.0 / (1.0 - ADAM_B2 ** ADAM_STEP)

    def body(g_ref, w_ref, m_ref, v_ref, og_ref, od_ref, om_ref, ov_ref):
        g = g_ref[0].astype(F32)
        for s in range(1, N_DEV):
            g = g + g_ref[s].astype(F32)
        m_new = ADAM_B1 * m_ref[...] + (1.0 - ADAM_B1) * g
        v_new = ADAM_B2 * v_ref[...] + (1.0 - ADAM_B2) * (g * g)
        og_ref[...] = g
        om_ref[...] = m_new
        ov_ref[...] = v_new
        od_ref[...] = -ADAM_LR * ((m_new * c1) / (jnp.sqrt(v_new * c2) + ADAM_EPS) + ADAM_WD * w_ref[...])

    spec = pl.BlockSpec((ta, b), lambda i: (i, 0))
    return _pallas(
        body, name=name, grid=(a // ta,),
        in_specs=[pl.BlockSpec((N_DEV, ta, b), lambda i: (0, i, 0)), spec, spec, spec],
        out_specs=[spec] * 4, out_shape=[jax.ShapeDtypeStruct((a, b), F32)] * 4,
        compiler_params=_cparams(("parallel",)),
    )(gstack, w, m, v)


_WEIGHTS = ['meta_tokens', 'pre_mix_norm', 'post_mix_norm', 'pre_mlp_norm', 'post_mlp_norm', 'mlp_w1', 'mlp_w2',
            'w_in_even', 'w_out_even', 'sb_out_norm', 's5_lambda_re', 's5_lambda_im', 's5_log_dt', 's5_b_re', 's5_b_im',
            's5_c_re', 's5_c_im', 's5_d', 's5_w_glu', 's5_b_glu', 's5_out_norm', 'w_in_odd', 'dn_conv_w', 'dn_a_log',
            'dn_dt_bias', 'dn_out_norm', 'w_out_odd']
_SHARDED = ['meta_tokens', 'mlp_w1', 'mlp_w2', 'w_in_even', 'w_out_even', 's5_w_glu', 'w_in_odd', 'dn_conv_w', 'w_out_odd']
_SMALL = [n for n in _WEIGHTS if n not in _SHARDED]
_GATHER_FIRST = ['meta_tokens', 'w_in_even', 's5_w_glu', 'w_out_even']
_GATHER_LATE = [n for n in _SHARDED if n not in _GATHER_FIRST]
_REDUCE_EARLY = ['mlp_w1', 'mlp_w2', 'w_in_odd', 'dn_conv_w', 'w_out_odd', 'w_out_even']


def _view2d(name, a):
    return a.reshape(-1, a.shape[-1])


def _unshard(name, g):
    if name == 'mlp_w1':
        return g.reshape(N_DEV, 2, D_MODEL, -1).transpose(1, 2, 0, 3).reshape(2, D_MODEL, D_FF)
    if name == 'mlp_w2':
        return g.reshape(N_DEV, 2, -1, D_MODEL).transpose(1, 0, 2, 3).reshape(2, D_FF, D_MODEL)
    if name in ('w_in_even', 'w_in_odd', 'dn_conv_w', 'meta_tokens'):
        return g.transpose(1, 0, 2).reshape(g.shape[1], -1)
    return g.reshape(-1, g.shape[-1])


def _to_blocks(name, full):
    if name == 'mlp_w1':
        return full.reshape(2, D_MODEL, N_DEV, -1).transpose(2, 0, 1, 3).reshape(N_DEV, 2 * D_MODEL, -1)
    if name == 'mlp_w2':
        return full.reshape(2, N_DEV, -1, D_MODEL).transpose(1, 0, 2, 3).reshape(N_DEV, -1, D_MODEL)
    if name in ('w_in_even', 'w_in_odd', 'dn_conv_w', 'meta_tokens'):
        return full.reshape(full.shape[0], N_DEV, -1).transpose(1, 0, 2)
    return full.reshape(N_DEV, -1, full.shape[-1])


def _pack(parts):
    rows = []
    for p in parts:
        flat = p.reshape(-1)
        rows.append(jnp.pad(flat, (0, (-flat.shape[0]) % 128)).reshape(-1, 128))
    return jnp.concatenate(rows, axis=0)


def _unpack(packed, like):
    out, at = [], 0
    for p in like:
        n = math.prod(p.shape)
        nrow = -(-n // 128)
        out.append(packed[at:at + nrow].reshape(-1)[:n].reshape(p.shape))
        at += nrow
    return out


def _lane_vec(x, width=128):
    flat = x.reshape(-1)
    return jnp.pad(flat, (0, width - flat.shape[0])).reshape(1, width)


def kernel(x, meta_tokens, pre_mix_norm, post_mix_norm, pre_mlp_norm, post_mlp_norm, mlp_w1, mlp_w2, w_in_even, w_out_even, sb_out_norm, s5_lambda_re, s5_lambda_im, s5_log_dt, s5_b_re, s5_b_im, s5_c_re, s5_c_im, s5_d, s5_w_glu, s5_b_glu, s5_out_norm, w_in_odd, dn_conv_w, dn_a_log, dn_dt_bias, dn_out_norm, w_out_odd, loss_target, m_meta_tokens, m_pre_mix_norm, m_post_mix_norm, m_pre_mlp_norm, m_post_mlp_norm, m_mlp_w1, m_mlp_w2, m_w_in_even, m_w_out_even, m_sb_out_norm, m_s5_lambda_re, m_s5_lambda_im, m_s5_log_dt, m_s5_b_re, m_s5_b_im, m_s5_c_re, m_s5_c_im, m_s5_d, m_s5_w_glu, m_s5_b_glu, m_s5_out_norm, m_w_in_odd, m_dn_conv_w, m_dn_a_log, m_dn_dt_bias, m_dn_out_norm, m_w_out_odd, v_meta_tokens, v_pre_mix_norm, v_post_mix_norm, v_pre_mlp_norm, v_post_mlp_norm, v_mlp_w1, v_mlp_w2, v_w_in_even, v_w_out_even, v_sb_out_norm, v_s5_lambda_re, v_s5_lambda_im, v_s5_log_dt, v_s5_b_re, v_s5_b_im, v_s5_c_re, v_s5_c_im, v_s5_d, v_s5_w_glu, v_s5_b_glu, v_s5_out_norm, v_w_in_odd, v_dn_conv_w, v_dn_a_log, v_dn_dt_bias, v_dn_out_norm, v_w_out_odd):
    given = dict(locals())
    w = {n: given[n] for n in _WEIGHTS}
    mom_m = {n: given["m_" + n] for n in _WEIGHTS}
    mom_v = {n: given["v_" + n] for n in _WEIGHTS}

    seq = x.shape[1]
    assert x.shape[0] == 1 and seq % ROW_TILE == 0
    r = seq + ROW_TILE
    pad = ROW_TILE - N_META

    wire = {n: (F32 if n in ('dn_conv_w', 'meta_tokens') else BF16) for n in _SHARDED}
    shard_wire = lambda n: _view2d(n, w[n]).astype(wire[n])
    gathered = _exchange([shard_wire(n) for n in _GATHER_FIRST], [False] * len(_GATHER_FIRST), "gather_first")
    full = {n: _unshard(n, g_) for n, g_ in zip(_GATHER_FIRST, gathered)}
    w_ie, w_oe, w_glu = full['w_in_even'], full['w_out_even'], full['s5_w_glu']
    row = lambda v_: v_.reshape(1, -1)

    hs0 = jnp.concatenate([jnp.zeros((pad, D_MODEL), F32), full['meta_tokens'], x[0]], axis=0)
    hn0 = _norm_pre(hs0, row(pre_mix_norm[0]), "pre_mix_0")
    qkv = _mm_fwd(hn0, w_ie[:, :3 * SB_WIDTH], "in_even_qkv", out_dtypes=(BF16,))
    u = _mm_fwd(hn0, w_ie[:, 3 * SB_WIDTH:], "in_even_u")
    q, k, v = qkv[:, :SB_WIDTH], qkv[:, SB_WIDTH:2 * SB_WIDTH], qkv[:, 2 * SB_WIDTH:]
    nb = r // ATT_BLK
    blocks_t = lambda t_: t_.reshape(nb, ATT_BLK, 4, 128).transpose(2, 0, 3, 1)
    o_sb, ssave, gathered = _sb_fwd(q, k, blocks_t(v), pad, "sb_fwd",
                                    ride=([shard_wire(n) for n in _GATHER_LATE], [False] * len(_GATHER_LATE)))
    full.update({n: _unshard(n, g_) for n, g_ in zip(_GATHER_LATE, gathered)})
    w1, w2, w_oo, conv_w = full['mlp_w1'], full['mlp_w2'], full['w_out_odd'], full['dn_conv_w']
    w_io = full['w_in_odd'][:, :4 * DN_WIDTH]
    w_ab = jnp.pad(full['w_in_odd'][:, 4 * DN_WIDTH:], ((0, 0), (0, 128 - 2 * DN_HEADS)))

    lam_re, lam_im, logdt, btr, bti, ctr, cti, s5_mask = _s5_expand(
        s5_lambda_re[0], s5_lambda_im[0], s5_log_dt[0], s5_b_re[0], s5_b_im[0], s5_c_re[0], s5_c_im[0])
    a_re, a_im, bbr, bbi = _s5_prep(lam_re, lam_im, logdt, btr, bti, "s5_prep")
    s5_wb = jnp.stack([_s5_block_diag_b(bbr, s5_mask), _s5_block_diag_b(bbi, s5_mask)]).astype(BF16)
    s5_wc = jnp.stack([_s5_block_diag_c(ctr, s5_mask), _s5_block_diag_c(cti, s5_mask)]).astype(BF16)
    s5_a = jnp.stack([a_re, a_im])
    s5_args = (s5_wb, s5_a, s5_wc, row(s5_d[0]), w_glu, row(s5_b_glu[0]), row(s5_out_norm[0]))
    y_s5, merged, xstart = _s5_fwd(u, *s5_args, "s5_fwd")
    merged = _norm_pre(o_sb, row(sb_out_norm[0]), "sb_out_norm", into=merged)

    mix0, hs1, hn1 = _mm_norm_fwd(merged, w_oe, hs0, row(post_mix_norm[0]), g_pre=row(pre_mlp_norm[0]), name="out_even")
    relu2 = lambda acc: (jnp.square(jnp.maximum(acc, 0.0)), jnp.maximum(acc, 0.0))
    r0, ra0 = _mm_fwd(hn1, w1[0], "mlp_up_0", out_dtypes=(BF16, BF16), epilogue=relu2)
    m0, hs2, hn2 = _mm_norm_fwd(r0, w2[0], hs1, row(post_mlp_norm[0]), g_pre=row(pre_mix_norm[1]), name="mlp_down_0")

    proj = _mm_fwd(hn2, w_io, "in_odd")
    ab = _mm_fwd(hn2, w_ab, "in_odd_gates")
    alog, dtb = _lane_vec(dn_a_log[0]), _lane_vec(dn_dt_bias[0])
    qd, kd, vd, gb = _dn_pre_fwd(proj, ab, conv_w, alog, dtb, pad, "dn_pre")
    gbt = gb[:, :2 * DN_HEADS].T
    o_dn, s_dn, t_dn = _dn_fwd(qd, kd, vd, gb, gbt, "dn_fwd")
    on_dn = _dn_post_fwd(o_dn, proj, row(dn_out_norm[0]), "dn_post")
    mix1, hs3, hn3 = _mm_norm_fwd(on_dn, w_oo, hs2, row(post_mix_norm[1]), g_pre=row(pre_mlp_norm[1]), name="out_odd")
    r1, ra1 = _mm_fwd(hn3, w1[1], "mlp_up_1", out_dtypes=(BF16, BF16), epilogue=relu2)
    dhs, dm1, dg_post_mlp1, loss_part = _mm_norm_fwd(r1, w2[1], hs3, row(post_mlp_norm[1]),
                                                     loss=(loss_target[0], pad + N_META), name="mlp_down_1_loss")
    loss = lax.psum(loss_part, ("x", "y", "c"))

    g = {}
    drelu2 = lambda acc, ra: (acc * (2.0 * ra.astype(F32)),)

    def mlp_bwd(layer, hn, rr, ra, dm):
        dw2 = _mm_wgrad(rr, dm, f"mlp_down_{layer}_wgrad")
        da = _mm_dgrad(dm, w2[layer], f"mlp_down_{layer}_dgrad", out_dtypes=(BF16,), extras=(ra,), epilogue=drelu2)
        dw1 = _mm_wgrad(hn, da, f"mlp_up_{layer}_wgrad")
        return dw1, dw2, da

    dw1_1, dw2_1, da1 = mlp_bwd(1, hn3, r1, ra1, dm1)
    dhs, dmix1, dg_pre_mlp1, dg_post_mix1 = _dgrad_norm_bwd(
        da1, w1[1], dhs, hs3, row(pre_mlp_norm[1]), post=(mix1, row(post_mix_norm[1])), pad=pad, name="post_mix_1_bwd")

    g['w_out_odd'] = _mm_wgrad(on_dn, dmix1, "out_odd_wgrad")
    d_on_dn = _mm_dgrad(dmix1, w_oo, "out_odd_dgrad")
    do_dn, dproj, dg_dn = _dn_post_bwd(o_dn, proj, row(dn_out_norm[0]), d_on_dn, "dn_post_bwd")
    dqd, dkd, dvd, dgb = _dn_bwd(qd, kd, vd, gb, gbt, s_dn, t_dn, do_dn, "dn_bwd")
    dco, dab, d_alog, d_dtb = _dn_pre_bwd(proj, conv_w, dqd, dkd, dvd, dgb, ab, alog, dtb, pad, "dn_pre_bwd")
    dproj, d_conv = _dn_conv_bwd(dco, proj, conv_w, dproj, "dn_conv_bwd")
    g['w_in_odd'] = jnp.concatenate([_mm_wgrad(hn2, dproj, "in_odd_wgrad"),
                                     _mm_wgrad(hn2, dab, "in_odd_gates_wgrad")[:, :2 * DN_HEADS]], axis=1)
    dhn2_gates = _mm_dgrad(dab, w_ab, "in_odd_gates_dgrad")
    g['dn_conv_w'] = d_conv[:DN_CONV]
    g['dn_a_log'], g['dn_dt_bias'], g['dn_out_norm'] = d_alog[0, :DN_HEADS], d_dtb[0, :DN_HEADS], dg_dn[0]

    dhs, dm0, dg_pre_mix1, dg_post_mlp0 = _dgrad_norm_bwd(
        dproj, w_io, dhs, hs2, row(pre_mix_norm[1]), post=(m0, row(post_mlp_norm[0])), add=dhn2_gates, pad=pad,
        name="post_mlp_0_bwd")
    dw1_0, dw2_0, da0 = mlp_bwd(0, hn1, r0, ra0, dm0)
    dhs, dmix0, dg_pre_mlp0, dg_post_mix0 = _dgrad_norm_bwd(
        da0, w1[0], dhs, hs1, row(pre_mlp_norm[0]), post=(mix0, row(post_mix_norm[0])), pad=pad, name="post_mix_0_bwd")

    g['w_out_even'] = _mm_wgrad(merged, dmix0, "out_even_wgrad")
    dmerged = _mm_dgrad(dmix0, w_oe, "out_even_dgrad")
    _, do_sb, _, dg_sb = _norm_bwd(dmerged, post=(o_sb, row(sb_out_norm[0])), pad=pad, dm_dtype=F32,
                                   dhs_cols=(SB_WIDTH, 0), name="sb_out_norm_bwd")
    dq, dk, dv = _sb_bwd(q, k, v, blocks_t(k), ssave, do_sb, pad, "sb_bwd")
    g['mlp_w1'] = jnp.stack([dw1_0, dw1_1])
    g['mlp_w2'] = jnp.stack([dw2_0, dw2_1])
    grad_wire = lambda n: _to_blocks(n, g[n].reshape(full[n].shape)).astype(wire[n])
    du, d_a, d_d, d_bglu, dg_s5, d_wb, d_wc, g['s5_w_glu'], reduced = _s5_bwd(
        u, y_s5, dmerged, xstart, *s5_args, "s5_bwd", don_block=1,
        ride=([grad_wire(n) for n in _REDUCE_EARLY], [True] * len(_REDUCE_EARLY)))
    stacks = dict(zip(_REDUCE_EARLY, reduced))
    g_lr, g_li, g_dt, g_btr, g_bti = _s5_prep_bwd(
        lam_re, lam_im, logdt, btr, bti, d_a[0], d_a[1],
        _s5_diag_of_b(d_wb[0], s5_mask), _s5_diag_of_b(d_wb[1], s5_mask), "s5_prep_bwd")
    gg, nn, pp = S5_GROUPS, S5_STATE, S5_GROUP
    g['s5_lambda_re'], g['s5_lambda_im'] = g_lr.reshape(gg, nn), g_li.reshape(gg, nn)
    g['s5_log_dt'] = g_dt.reshape(gg, nn)[:, 0]
    g['s5_b_re'], g['s5_b_im'] = g_btr.T.reshape(gg, nn, pp), g_bti.T.reshape(gg, nn, pp)
    g['s5_c_re'] = _s5_diag_of_c(d_wc[0], s5_mask).reshape(gg, nn, pp).transpose(0, 2, 1)
    g['s5_c_im'] = _s5_diag_of_c(d_wc[1], s5_mask).reshape(gg, nn, pp).transpose(0, 2, 1)
    g['s5_d'], g['s5_b_glu'], g['s5_out_norm'], g['sb_out_norm'] = d_d[0], d_bglu[0], dg_s5[0], dg_sb[0]
    dqkvu = jnp.concatenate([dq, dk, dv, du], axis=1).astype(BF16)
    g['w_in_even'] = _mm_wgrad(hn0, dqkvu, "in_even_wgrad")
    dhs, _, dg_pre_mix0, _ = _dgrad_norm_bwd(dqkvu, w_ie, dhs, hs0, row(pre_mix_norm[0]), pad=pad, name="pre_mix_0_bwd")

    g['meta_tokens'] = dhs[pad:pad + N_META]
    g['pre_mix_norm'] = jnp.concatenate([dg_pre_mix0, dg_pre_mix1], axis=0)
    g['post_mix_norm'] = jnp.concatenate([dg_post_mix0, dg_post_mix1], axis=0)
    g['pre_mlp_norm'] = jnp.concatenate([dg_pre_mlp0, dg_pre_mlp1], axis=0)
    g['post_mlp_norm'] = jnp.concatenate([dg_post_mlp0, dg_post_mlp1], axis=0)
    grad_x = dhs[pad + N_META:][None]

    small_like = [w[n] for n in _SMALL]
    last = [n for n in _SHARDED if n not in _REDUCE_EARLY]
    partial = [grad_wire(n) for n in last] + [_pack([g[n].reshape(w[n].shape) for n in _SMALL])]
    reduced = _exchange(partial, [True] * len(last) + [False], "reduce_last")
    stacks.update(zip(last, reduced[:-1]))
    grads, deltas, new_m, new_v = {}, {}, {}, {}
    for n in _SHARDED:
        outs = _adamw(stacks[n], _view2d(n, w[n]), _view2d(n, mom_m[n]), _view2d(n, mom_v[n]), f"adamw_{n}")
        grads[n], deltas[n], new_m[n], new_v[n] = (o.reshape(w[n].shape) for o in outs)
    outs = _adamw(reduced[-1], _pack(small_like), _pack([mom_m[n] for n in _SMALL]), _pack([mom_v[n] for n in _SMALL]),
                  "adamw_small")
    for dst, o in zip((grads, deltas, new_m, new_v), outs):
        for n, part in zip(_SMALL, _unpack(o, small_like)):
            dst[n] = part
    return (loss, grad_x, *[grads[n] for n in _WEIGHTS], *[deltas[n] for n in _WEIGHTS],
            *[new_m[n] for n in _WEIGHTS], *[new_v[n] for n in _WEIGHTS])
```
